```python
import jax, jax.numpy as jnp
from jax import lax
import numpy as np

D_MODEL = 1024
BATCH = 8
SEQ = 8192
DEPTH = 1

PLE_DIM = 256
ATT_HEADS = 8
ATT_KV_HEADS = 2
HEAD_DIM = 64
WINDOW = 128
BLOCK = 128
D_ATT = ATT_HEADS * HEAD_DIM
D_KV = ATT_KV_HEADS * HEAD_DIM
D_RNN = D_MODEL - D_ATT
RNN_BLOCKS = 8
RNN_BLOCK_DIM = D_RNN // RNN_BLOCKS
RNN_CONV = 4
LRU_C = 8.0
D_MIX = D_ATT + D_RNN
D_IN = D_ATT + 2 * D_KV + 2 * D_RNN
D_FF = 3 * D_MODEL
FFN_CONV = 3
LN_EPS = 1e-5
ALPHA = float((2 * DEPTH) ** 0.25)
BETA = float((8 * DEPTH) ** -0.25)

kernel_name = "hymba_swa_sink_rglru_convglu_deepnorm"


def layer_norm(x, g, b):
    xf = x.astype(jnp.float32)
    mu = jnp.mean(xf, axis=-1, keepdims=True)
    var = jnp.mean(jnp.square(xf - mu), axis=-1, keepdims=True)
    y = (xf - mu) * lax.rsqrt(var + LN_EPS)
    return (y * g.astype(jnp.float32) + b.astype(jnp.float32)).astype(x.dtype)


def causal_dwconv(x, w, b):
    width = w.shape[0]
    y = lax.conv_general_dilated(
        x, w[:, None, :].astype(x.dtype), window_strides=(1,),
        padding=[(width - 1, 0)], dimension_numbers=('NWC', 'WIO', 'NWC'),
        feature_group_count=x.shape[-1])
    return y + b.astype(x.dtype)


def sliding_window_sink_attention(q, k, v, sinks):
    B, S = q.shape[0], q.shape[1]
    nb = S // BLOCK
    grp = ATT_HEADS // ATT_KV_HEADS
    qb = q.reshape(B, nb, BLOCK, ATT_KV_HEADS, grp, HEAD_DIM).astype(jnp.float32)

    def band(t):
        tb = t.reshape(B, nb, BLOCK, ATT_KV_HEADS, HEAD_DIM).astype(jnp.float32)
        prev = jnp.pad(tb, ((0, 0), (1, 0), (0, 0), (0, 0), (0, 0)))[:, :-1]
        return jnp.concatenate([prev, tb], axis=2)

    kb, vb = band(k), band(v)
    scores = jnp.einsum('bnqkgd,bnskd->bnkgqs', qb, kb) * (HEAD_DIM ** -0.5)
    qi = jnp.arange(BLOCK)[:, None]
    sj = jnp.arange(2 * BLOCK)[None, :]
    rel = qi + BLOCK - sj
    in_win = (rel >= 0) & (rel < WINDOW)
    key_ok = (jnp.arange(nb)[:, None] * BLOCK - BLOCK + sj) >= 0
    mask = in_win[None] & key_ok[:, None, :]
    scores = jnp.where(mask[None, :, None, None], scores, -jnp.inf)
    sink = sinks.astype(jnp.float32).reshape(ATT_KV_HEADS, grp)[None, None, :, :, None, None]
    sink = jnp.broadcast_to(sink, scores.shape[:-1] + (1,))
    probs = jax.nn.softmax(jnp.concatenate([scores, sink], axis=-1), axis=-1)[..., :-1]
    out = jnp.einsum('bnkgqs,bnskd->bnqkgd', probs, vb)
    return out.reshape(B, S, D_ATT).astype(q.dtype)


def rg_lru(x, w_a, b_a, w_x, b_x, lam):
    B, S, _ = x.shape
    xf = x.astype(jnp.float32)
    xb = xf.reshape(B, S, RNN_BLOCKS, RNN_BLOCK_DIM)
    r = jax.nn.sigmoid(jnp.einsum('bshi,hij->bshj', xb, w_a.astype(jnp.float32)).reshape(B, S, D_RNN)
                       + b_a.astype(jnp.float32))
    i = jax.nn.sigmoid(jnp.einsum('bshi,hij->bshj', xb, w_x.astype(jnp.float32)).reshape(B, S, D_RNN)
                       + b_x.astype(jnp.float32))
    log_a = -LRU_C * r * jax.nn.softplus(-lam.astype(jnp.float32))
    a = jnp.exp(log_a)
    b = jnp.sqrt(-jnp.expm1(2.0 * log_a)) * (i * xf)

    def combine(c1, c2):
        a1, b1 = c1
        a2, b2 = c2
        return a1 * a2, a2 * b1 + b2

    _, h = lax.associative_scan(combine, (a, b), axis=1)
    return h.astype(x.dtype)


def conv_glu_ffn(h, w_up, conv_w, conv_b, w_down):
    up = h @ w_up
    gate, val = jnp.split(up, 2, axis=-1)
    gate = causal_dwconv(gate, conv_w, conv_b)
    return (jax.nn.gelu(gate, approximate=True) * val) @ w_down


def _fwd_setup_inputs(seed: int = 0) -> dict:
    key = jax.random.key(seed)
    ks = jax.random.split(key, 24)
    f32 = jnp.float32
    nrm = lambda k, shape, s: jax.random.normal(k, shape, f32) * s
    L = DEPTH
    u = jax.random.uniform(ks[12], (L, D_RNN), f32, 0.9, 0.999)
    s = u ** (1.0 / LRU_C)
    lru_lambda = jnp.log(s) - jnp.log1p(-s)
    return {
        "x": nrm(ks[0], (BATCH, SEQ, D_MODEL), 1.0),
        "p": nrm(ks[1], (DEPTH, BATCH, SEQ, PLE_DIM), 1.0),
        "w_in": nrm(ks[2], (L, D_MODEL, D_IN), D_MODEL ** -0.5),
        "attn_sinks": nrm(ks[3], (L, ATT_HEADS), 0.5),
        "rnn_conv_w": nrm(ks[4], (L, RNN_CONV, D_RNN), RNN_CONV ** -0.5),
        "rnn_conv_b": nrm(ks[5], (L, D_RNN), 0.01),
        "gate_a_w": nrm(ks[6], (L, RNN_BLOCKS, RNN_BLOCK_DIM, RNN_BLOCK_DIM), RNN_BLOCK_DIM ** -0.5),
        "gate_a_b": nrm(ks[7], (L, D_RNN), 0.01),
        "gate_x_w": nrm(ks[8], (L, RNN_BLOCKS, RNN_BLOCK_DIM, RNN_BLOCK_DIM), RNN_BLOCK_DIM ** -0.5),
        "gate_x_b": nrm(ks[9], (L, D_RNN), 0.01),
        "lru_lambda": lru_lambda,
        "w_out": nrm(ks[10], (L, D_MIX, D_MODEL), BETA * D_MIX ** -0.5),
        "ln1_g": 1.0 + nrm(ks[11], (L, D_MODEL), 0.01),
        "ln1_b": nrm(ks[13], (L, D_MODEL), 0.01),
        "w_ffn_up": nrm(ks[14], (L, D_MODEL, 2 * D_FF), D_MODEL ** -0.5),
        "ffn_conv_w": nrm(ks[15], (L, FFN_CONV, D_FF), FFN_CONV ** -0.5),
        "ffn_conv_b": nrm(ks[16], (L, D_FF), 0.01),
        "w_ffn_down": nrm(ks[17], (L, D_FF, D_MODEL), BETA * D_FF ** -0.5),
        "ple_gate_w": nrm(ks[18], (L, D_MODEL, D_MODEL), D_MODEL ** -0.5),
        "ple_gate_b": nrm(ks[19], (L, D_MODEL), 0.01),
        "ple_proj": nrm(ks[20], (L, PLE_DIM, D_MODEL), BETA * PLE_DIM ** -0.5),
        "ln2_g": 1.0 + nrm(ks[21], (L, D_MODEL), 0.01),
        "ln2_b": nrm(ks[22], (L, D_MODEL), 0.01),
    }


def _fwd_reference(x, p, w_in, attn_sinks, rnn_conv_w, rnn_conv_b, gate_a_w, gate_a_b,
              gate_x_w, gate_x_b, lru_lambda, w_out, ln1_g, ln1_b, w_ffn_up,
              ffn_conv_w, ffn_conv_b, w_ffn_down, ple_gate_w, ple_gate_b, ple_proj,
              ln2_g, ln2_b):
    B, S, _ = x.shape
    splits = [D_ATT, D_ATT + D_KV, D_ATT + 2 * D_KV, D_ATT + 2 * D_KV + D_RNN]
    h = x
    for l in range(DEPTH):
        u = h @ w_in[l]
        q, k, v, xr, gr = jnp.split(u, splits, axis=-1)
        att = sliding_window_sink_attention(
            q.reshape(B, S, ATT_HEADS, HEAD_DIM),
            k.reshape(B, S, ATT_KV_HEADS, HEAD_DIM),
            v.reshape(B, S, ATT_KV_HEADS, HEAD_DIM),
            attn_sinks[l])
        xr = causal_dwconv(xr, rnn_conv_w[l], rnn_conv_b[l])
        rec = rg_lru(xr, gate_a_w[l], gate_a_b[l], gate_x_w[l], gate_x_b[l], lru_lambda[l])
        rec = rec * jax.nn.gelu(gr, approximate=True)
        mix = jnp.concatenate([att, rec], axis=-1) @ w_out[l]
        h = layer_norm(ALPHA * h + mix, ln1_g[l], ln1_b[l])
        ffn = conv_glu_ffn(h, w_ffn_up[l], ffn_conv_w[l], ffn_conv_b[l], w_ffn_down[l])
        ple = jax.nn.sigmoid(h @ ple_gate_w[l] + ple_gate_b[l]) * (p[l] @ ple_proj[l])
        h = layer_norm(ALPHA * h + ffn + ple, ln2_g[l], ln2_b[l])
    return h


import jax as _jax
import jax.numpy as _jnp

TWIN_FORMAT = 'train_step'
FWD_PARAMS = ['x', 'p', 'w_in', 'attn_sinks', 'rnn_conv_w', 'rnn_conv_b', 'gate_a_w', 'gate_a_b', 'gate_x_w', 'gate_x_b', 'lru_lambda', 'w_out', 'ln1_g', 'ln1_b', 'w_ffn_up', 'ffn_conv_w', 'ffn_conv_b', 'w_ffn_down', 'ple_gate_w', 'ple_gate_b', 'ple_proj', 'ln2_g', 'ln2_b']
TWIN_WEIGHTS = ['w_in', 'attn_sinks', 'rnn_conv_w', 'rnn_conv_b', 'gate_a_w', 'gate_a_b', 'gate_x_w', 'gate_x_b', 'lru_lambda', 'w_out', 'ln1_g', 'ln1_b', 'w_ffn_up', 'ffn_conv_w', 'ffn_conv_b', 'w_ffn_down', 'ple_gate_w', 'ple_gate_b', 'ple_proj', 'ln2_g', 'ln2_b']
TWIN_DIFF_INPUT = 'x'
TWIN_INPUTS = ['x', 'p', 'w_in', 'attn_sinks', 'rnn_conv_w', 'rnn_conv_b', 'gate_a_w', 'gate_a_b', 'gate_x_w', 'gate_x_b', 'lru_lambda', 'w_out', 'ln1_g', 'ln1_b', 'w_ffn_up', 'ffn_conv_w', 'ffn_conv_b', 'w_ffn_down', 'ple_gate_w', 'ple_gate_b', 'ple_proj', 'ln2_g', 'ln2_b', 'loss_target', 'm_w_in', 'm_attn_sinks', 'm_rnn_conv_w', 'm_rnn_conv_b', 'm_gate_a_w', 'm_gate_a_b', 'm_gate_x_w', 'm_gate_x_b', 'm_lru_lambda', 'm_w_out', 'm_ln1_g', 'm_ln1_b', 'm_w_ffn_up', 'm_ffn_conv_w', 'm_ffn_conv_b', 'm_w_ffn_down', 'm_ple_gate_w', 'm_ple_gate_b', 'm_ple_proj', 'm_ln2_g', 'm_ln2_b', 'v_w_in', 'v_attn_sinks', 'v_rnn_conv_w', 'v_rnn_conv_b', 'v_gate_a_w', 'v_gate_a_b', 'v_gate_x_w', 'v_gate_x_b', 'v_lru_lambda', 'v_w_out', 'v_ln1_g', 'v_ln1_b', 'v_w_ffn_up', 'v_ffn_conv_w', 'v_ffn_conv_b', 'v_w_ffn_down', 'v_ple_gate_w', 'v_ple_gate_b', 'v_ple_proj', 'v_ln2_g', 'v_ln2_b']
TWIN_OUTPUTS = ['loss', 'grad_x', 'grad_w_in', 'grad_attn_sinks', 'grad_rnn_conv_w', 'grad_rnn_conv_b', 'grad_gate_a_w', 'grad_gate_a_b', 'grad_gate_x_w', 'grad_gate_x_b', 'grad_lru_lambda', 'grad_w_out', 'grad_ln1_g', 'grad_ln1_b', 'grad_w_ffn_up', 'grad_ffn_conv_w', 'grad_ffn_conv_b', 'grad_w_ffn_down', 'grad_ple_gate_w', 'grad_ple_gate_b', 'grad_ple_proj', 'grad_ln2_g', 'grad_ln2_b', 'delta_w_in', 'delta_attn_sinks', 'delta_rnn_conv_w', 'delta_rnn_conv_b', 'delta_gate_a_w', 'delta_gate_a_b', 'delta_gate_x_w', 'delta_gate_x_b', 'delta_lru_lambda', 'delta_w_out', 'delta_ln1_g', 'delta_ln1_b', 'delta_w_ffn_up', 'delta_ffn_conv_w', 'delta_ffn_conv_b', 'delta_w_ffn_down', 'delta_ple_gate_w', 'delta_ple_gate_b', 'delta_ple_proj', 'delta_ln2_g', 'delta_ln2_b', 'new_m_w_in', 'new_m_attn_sinks', 'new_m_rnn_conv_w', 'new_m_rnn_conv_b', 'new_m_gate_a_w', 'new_m_gate_a_b', 'new_m_gate_x_w', 'new_m_gate_x_b', 'new_m_lru_lambda', 'new_m_w_out', 'new_m_ln1_g', 'new_m_ln1_b', 'new_m_w_ffn_up', 'new_m_ffn_conv_w', 'new_m_ffn_conv_b', 'new_m_w_ffn_down', 'new_m_ple_gate_w', 'new_m_ple_gate_b', 'new_m_ple_proj', 'new_m_ln2_g', 'new_m_ln2_b', 'new_v_w_in', 'new_v_attn_sinks', 'new_v_rnn_conv_w', 'new_v_rnn_conv_b', 'new_v_gate_a_w', 'new_v_gate_a_b', 'new_v_gate_x_w', 'new_v_gate_x_b', 'new_v_lru_lambda', 'new_v_w_out', 'new_v_ln1_g', 'new_v_ln1_b', 'new_v_w_ffn_up', 'new_v_ffn_conv_w', 'new_v_ffn_conv_b', 'new_v_w_ffn_down', 'new_v_ple_gate_w', 'new_v_ple_gate_b', 'new_v_ple_proj', 'new_v_ln2_g', 'new_v_ln2_b']
TWIN_LEAF_KINDS = {'loss': 'loss', 'grad_x': 'grad_x', 'grad_w_in': 'grad_w', 'grad_attn_sinks': 'grad_w', 'grad_rnn_conv_w': 'grad_w', 'grad_rnn_conv_b': 'grad_w', 'grad_gate_a_w': 'grad_w', 'grad_gate_a_b': 'grad_w', 'grad_gate_x_w': 'grad_w', 'grad_gate_x_b': 'grad_w', 'grad_lru_lambda': 'grad_w', 'grad_w_out': 'grad_w', 'grad_ln1_g': 'grad_w', 'grad_ln1_b': 'grad_w', 'grad_w_ffn_up': 'grad_w', 'grad_ffn_conv_w': 'grad_w', 'grad_ffn_conv_b': 'grad_w', 'grad_w_ffn_down': 'grad_w', 'grad_ple_gate_w': 'grad_w', 'grad_ple_gate_b': 'grad_w', 'grad_ple_proj': 'grad_w', 'grad_ln2_g': 'grad_w', 'grad_ln2_b': 'grad_w', 'delta_w_in': 'delta_w', 'delta_attn_sinks': 'delta_w', 'delta_rnn_conv_w': 'delta_w', 'delta_rnn_conv_b': 'delta_w', 'delta_gate_a_w': 'delta_w', 'delta_gate_a_b': 'delta_w', 'delta_gate_x_w': 'delta_w', 'delta_gate_x_b': 'delta_w', 'delta_lru_lambda': 'delta_w', 'delta_w_out': 'delta_w', 'delta_ln1_g': 'delta_w', 'delta_ln1_b': 'delta_w', 'delta_w_ffn_up': 'delta_w', 'delta_ffn_conv_w': 'delta_w', 'delta_ffn_conv_b': 'delta_w', 'delta_w_ffn_down': 'delta_w', 'delta_ple_gate_w': 'delta_w', 'delta_ple_gate_b': 'delta_w', 'delta_ple_proj': 'delta_w', 'delta_ln2_g': 'delta_w', 'delta_ln2_b': 'delta_w', 'new_m_w_in': 'new_m', 'new_m_attn_sinks': 'new_m', 'new_m_rnn_conv_w': 'new_m', 'new_m_rnn_conv_b': 'new_m', 'new_m_gate_a_w': 'new_m', 'new_m_gate_a_b': 'new_m', 'new_m_gate_x_w': 'new_m', 'new_m_gate_x_b': 'new_m', 'new_m_lru_lambda': 'new_m', 'new_m_w_out': 'new_m', 'new_m_ln1_g': 'new_m', 'new_m_ln1_b': 'new_m', 'new_m_w_ffn_up': 'new_m', 'new_m_ffn_conv_w': 'new_m', 'new_m_ffn_conv_b': 'new_m', 'new_m_w_ffn_down': 'new_m', 'new_m_ple_gate_w': 'new_m', 'new_m_ple_gate_b': 'new_m', 'new_m_ple_proj': 'new_m', 'new_m_ln2_g': 'new_m', 'new_m_ln2_b': 'new_m', 'new_v_w_in': 'new_v', 'new_v_attn_sinks': 'new_v', 'new_v_rnn_conv_w': 'new_v', 'new_v_rnn_conv_b': 'new_v', 'new_v_gate_a_w': 'new_v', 'new_v_gate_a_b': 'new_v', 'new_v_gate_x_w': 'new_v', 'new_v_gate_x_b': 'new_v', 'new_v_lru_lambda': 'new_v', 'new_v_w_out': 'new_v', 'new_v_ln1_g': 'new_v', 'new_v_ln1_b': 'new_v', 'new_v_w_ffn_up': 'new_v', 'new_v_ffn_conv_w': 'new_v', 'new_v_ffn_conv_b': 'new_v', 'new_v_w_ffn_down': 'new_v', 'new_v_ple_gate_w': 'new_v', 'new_v_ple_gate_b': 'new_v', 'new_v_ple_proj': 'new_v', 'new_v_ln2_g': 'new_v', 'new_v_ln2_b': 'new_v'}


def _forward(args):
    return _fwd_reference(*[args[k] for k in FWD_PARAMS])


def _output_shape():
    out = _jax.eval_shape(lambda: _forward(_fwd_setup_inputs(0)))
    return out.shape, out.dtype

N_MICROBATCH = 1
ADAM_LR = 0.001
ADAM_B1 = 0.9
ADAM_B2 = 0.999
ADAM_EPS = 1e-08
ADAM_WD = 0.01
ADAM_STEP = 10
PER_EXAMPLE_BATCH_AXIS = {'x': 0, 'p': 1, 'loss_target': 0}
SHARED_INPUTS = []
_WEIGHT_DTYPES = {'w_in': _jnp.float32, 'attn_sinks': _jnp.float32, 'rnn_conv_w': _jnp.float32, 'rnn_conv_b': _jnp.float32, 'gate_a_w': _jnp.float32, 'gate_a_b': _jnp.float32, 'gate_x_w': _jnp.float32, 'gate_x_b': _jnp.float32, 'lru_lambda': _jnp.float32, 'w_out': _jnp.float32, 'ln1_g': _jnp.float32, 'ln1_b': _jnp.float32, 'w_ffn_up': _jnp.float32, 'ffn_conv_w': _jnp.float32, 'ffn_conv_b': _jnp.float32, 'w_ffn_down': _jnp.float32, 'ple_gate_w': _jnp.float32, 'ple_gate_b': _jnp.float32, 'ple_proj': _jnp.float32, 'ln2_g': _jnp.float32, 'ln2_b': _jnp.float32}
MOMENT_SCALE = {'w_in': 4.889571e-02, 'attn_sinks': 2.824890e-02, 'rnn_conv_w': 7.009661e-02, 'rnn_conv_b': 8.380698e-01, 'gate_a_w': 3.330268e-02, 'gate_a_b': 2.665183e-02, 'gate_x_w': 5.950741e-02, 'gate_x_b': 3.103484e-02, 'lru_lambda': 4.556266e-02, 'w_out': 8.545966e-02, 'ln1_g': 6.291321e-01, 'ln1_b': 3.737527e-01, 'w_ffn_up': 4.519065e-02, 'ffn_conv_w': 4.509358e-02, 'ffn_conv_b': 4.356926e-02, 'w_ffn_down': 1.293954e-01, 'ple_gate_w': 2.426313e-02, 'ple_gate_b': 2.850425e-02, 'ple_proj': 1.051293e-01, 'ln2_g': 6.395879e+01, 'ln2_b': 1.899320e+00}


def _to_microbatches(a, axis):
    t = _jnp.moveaxis(a, axis, 0)
    t = t.reshape((N_MICROBATCH, t.shape[0] // N_MICROBATCH) + t.shape[1:])
    return _jnp.moveaxis(t, 1, axis + 1)


def setup_inputs(seed: int = 0) -> dict:
    inp = _fwd_setup_inputs(seed)
    key = _jax.random.fold_in(_jax.random.key(seed), 7919)
    shape, _ = _output_shape()
    out = dict(inp)
    out["loss_target"] = _jax.random.normal(_jax.random.fold_in(key, 0), shape, _jnp.float32)
    for i, name in enumerate(TWIN_WEIGHTS):
        w = inp[name].astype(_jnp.float32)
        if MOMENT_SCALE is None:
            s = _jnp.sqrt(_jnp.mean(_jnp.square(w)) + 1e-30)
        else:
            s = MOMENT_SCALE[name]
        km, kv = _jax.random.split(_jax.random.fold_in(key, i + 1))
        out[name] = w
        out["m_" + name] = s * _jax.random.normal(km, w.shape, _jnp.float32)
        out["v_" + name] = (s * s) * _jax.random.uniform(kv, w.shape, _jnp.float32, 0.5, 1.5)
    if N_MICROBATCH > 1:
        for name, axis in PER_EXAMPLE_BATCH_AXIS.items():
            out[name] = _to_microbatches(out[name], axis)
    return {'x': out['x'], 'p': out['p'], 'w_in': out['w_in'], 'attn_sinks': out['attn_sinks'], 'rnn_conv_w': out['rnn_conv_w'], 'rnn_conv_b': out['rnn_conv_b'], 'gate_a_w': out['gate_a_w'], 'gate_a_b': out['gate_a_b'], 'gate_x_w': out['gate_x_w'], 'gate_x_b': out['gate_x_b'], 'lru_lambda': out['lru_lambda'], 'w_out': out['w_out'], 'ln1_g': out['ln1_g'], 'ln1_b': out['ln1_b'], 'w_ffn_up': out['w_ffn_up'], 'ffn_conv_w': out['ffn_conv_w'], 'ffn_conv_b': out['ffn_conv_b'], 'w_ffn_down': out['w_ffn_down'], 'ple_gate_w': out['ple_gate_w'], 'ple_gate_b': out['ple_gate_b'], 'ple_proj': out['ple_proj'], 'ln2_g': out['ln2_g'], 'ln2_b': out['ln2_b'], 'loss_target': out['loss_target'], 'm_w_in': out['m_w_in'], 'm_attn_sinks': out['m_attn_sinks'], 'm_rnn_conv_w': out['m_rnn_conv_w'], 'm_rnn_conv_b': out['m_rnn_conv_b'], 'm_gate_a_w': out['m_gate_a_w'], 'm_gate_a_b': out['m_gate_a_b'], 'm_gate_x_w': out['m_gate_x_w'], 'm_gate_x_b': out['m_gate_x_b'], 'm_lru_lambda': out['m_lru_lambda'], 'm_w_out': out['m_w_out'], 'm_ln1_g': out['m_ln1_g'], 'm_ln1_b': out['m_ln1_b'], 'm_w_ffn_up': out['m_w_ffn_up'], 'm_ffn_conv_w': out['m_ffn_conv_w'], 'm_ffn_conv_b': out['m_ffn_conv_b'], 'm_w_ffn_down': out['m_w_ffn_down'], 'm_ple_gate_w': out['m_ple_gate_w'], 'm_ple_gate_b': out['m_ple_gate_b'], 'm_ple_proj': out['m_ple_proj'], 'm_ln2_g': out['m_ln2_g'], 'm_ln2_b': out['m_ln2_b'], 'v_w_in': out['v_w_in'], 'v_attn_sinks': out['v_attn_sinks'], 'v_rnn_conv_w': out['v_rnn_conv_w'], 'v_rnn_conv_b': out['v_rnn_conv_b'], 'v_gate_a_w': out['v_gate_a_w'], 'v_gate_a_b': out['v_gate_a_b'], 'v_gate_x_w': out['v_gate_x_w'], 'v_gate_x_b': out['v_gate_x_b'], 'v_lru_lambda': out['v_lru_lambda'], 'v_w_out': out['v_w_out'], 'v_ln1_g': out['v_ln1_g'], 'v_ln1_b': out['v_ln1_b'], 'v_w_ffn_up': out['v_w_ffn_up'], 'v_ffn_conv_w': out['v_ffn_conv_w'], 'v_ffn_conv_b': out['v_ffn_conv_b'], 'v_w_ffn_down': out['v_w_ffn_down'], 'v_ple_gate_w': out['v_ple_gate_w'], 'v_ple_gate_b': out['v_ple_gate_b'], 'v_ple_proj': out['v_ple_proj'], 'v_ln2_g': out['v_ln2_g'], 'v_ln2_b': out['v_ln2_b']}


def _loss(weights, diff, rest, loss_target):
    with _jax.named_scope("forward"):
        args = {**rest, TWIN_DIFF_INPUT: diff, **{k: w.astype(_WEIGHT_DTYPES[k]) for k, w in weights.items()}}
        y = _forward(args)
    with _jax.named_scope("loss_head"):
        err = _jnp.square(y.astype(_jnp.float32) - loss_target)
        return 0.5 * _jnp.sum(_jnp.mean(err, axis=-1)) if err.ndim else 0.5 * err


def _adamw(w, g, m, v):
    m = ADAM_B1 * m + (1.0 - ADAM_B1) * g
    v = ADAM_B2 * v + (1.0 - ADAM_B2) * _jnp.square(g)
    m_hat = m / (1.0 - ADAM_B1 ** ADAM_STEP)
    v_hat = v / (1.0 - ADAM_B2 ** ADAM_STEP)
    delta = -ADAM_LR * (m_hat / (_jnp.sqrt(v_hat) + ADAM_EPS) + ADAM_WD * w)
    return delta, m, v


def reference(x, p, w_in, attn_sinks, rnn_conv_w, rnn_conv_b, gate_a_w, gate_a_b, gate_x_w, gate_x_b, lru_lambda, w_out, ln1_g, ln1_b, w_ffn_up, ffn_conv_w, ffn_conv_b, w_ffn_down, ple_gate_w, ple_gate_b, ple_proj, ln2_g, ln2_b, loss_target, m_w_in, m_attn_sinks, m_rnn_conv_w, m_rnn_conv_b, m_gate_a_w, m_gate_a_b, m_gate_x_w, m_gate_x_b, m_lru_lambda, m_w_out, m_ln1_g, m_ln1_b, m_w_ffn_up, m_ffn_conv_w, m_ffn_conv_b, m_w_ffn_down, m_ple_gate_w, m_ple_gate_b, m_ple_proj, m_ln2_g, m_ln2_b, v_w_in, v_attn_sinks, v_rnn_conv_w, v_rnn_conv_b, v_gate_a_w, v_gate_a_b, v_gate_x_w, v_gate_x_b, v_lru_lambda, v_w_out, v_ln1_g, v_ln1_b, v_w_ffn_up, v_ffn_conv_w, v_ffn_conv_b, v_w_ffn_down, v_ple_gate_w, v_ple_gate_b, v_ple_proj, v_ln2_g, v_ln2_b):
    given = dict(x=x, p=p, w_in=w_in, attn_sinks=attn_sinks, rnn_conv_w=rnn_conv_w, rnn_conv_b=rnn_conv_b, gate_a_w=gate_a_w, gate_a_b=gate_a_b, gate_x_w=gate_x_w, gate_x_b=gate_x_b, lru_lambda=lru_lambda, w_out=w_out, ln1_g=ln1_g, ln1_b=ln1_b, w_ffn_up=w_ffn_up, ffn_conv_w=ffn_conv_w, ffn_conv_b=ffn_conv_b, w_ffn_down=w_ffn_down, ple_gate_w=ple_gate_w, ple_gate_b=ple_gate_b, ple_proj=ple_proj, ln2_g=ln2_g, ln2_b=ln2_b, loss_target=loss_target, m_w_in=m_w_in, m_attn_sinks=m_attn_sinks, m_rnn_conv_w=m_rnn_conv_w, m_rnn_conv_b=m_rnn_conv_b, m_gate_a_w=m_gate_a_w, m_gate_a_b=m_gate_a_b, m_gate_x_w=m_gate_x_w, m_gate_x_b=m_gate_x_b, m_lru_lambda=m_lru_lambda, m_w_out=m_w_out, m_ln1_g=m_ln1_g, m_ln1_b=m_ln1_b, m_w_ffn_up=m_w_ffn_up, m_ffn_conv_w=m_ffn_conv_w, m_ffn_conv_b=m_ffn_conv_b, m_w_ffn_down=m_w_ffn_down, m_ple_gate_w=m_ple_gate_w, m_ple_gate_b=m_ple_gate_b, m_ple_proj=m_ple_proj, m_ln2_g=m_ln2_g, m_ln2_b=m_ln2_b, v_w_in=v_w_in, v_attn_sinks=v_attn_sinks, v_rnn_conv_w=v_rnn_conv_w, v_rnn_conv_b=v_rnn_conv_b, v_gate_a_w=v_gate_a_w, v_gate_a_b=v_gate_a_b, v_gate_x_w=v_gate_x_w, v_gate_x_b=v_gate_x_b, v_lru_lambda=v_lru_lambda, v_w_out=v_w_out, v_ln1_g=v_ln1_g, v_ln1_b=v_ln1_b, v_w_ffn_up=v_w_ffn_up, v_ffn_conv_w=v_ffn_conv_w, v_ffn_conv_b=v_ffn_conv_b, v_w_ffn_down=v_w_ffn_down, v_ple_gate_w=v_ple_gate_w, v_ple_gate_b=v_ple_gate_b, v_ple_proj=v_ple_proj, v_ln2_g=v_ln2_g, v_ln2_b=v_ln2_b)
    weights = {n: given[n] for n in TWIN_WEIGHTS}
    shared = {n: given[n] for n in SHARED_INPUTS}
    per_example = {n: given[n] for n in ['x', 'p']}
    grad_fn = _jax.value_and_grad(_loss, argnums=(0, 1))

    def one_microbatch(ex, loss_target):
        ex = dict(ex)
        diff = ex.pop(TWIN_DIFF_INPUT)
        return grad_fn(weights, diff, {**shared, **ex}, loss_target)

    if N_MICROBATCH == 1:
        loss, (grad_w, grad_x) = one_microbatch(per_example, given["loss_target"])
    else:
        def body(carry, xs):
            loss_sum, grad_sum = carry
            l_k, (gw_k, gx_k) = one_microbatch(xs[0], xs[1])
            with _jax.named_scope("update"):
                return (loss_sum + l_k, _jax.tree.map(_jnp.add, grad_sum, gw_k)), gx_k

        init = (_jnp.zeros((), _jnp.float32), _jax.tree.map(_jnp.zeros_like, weights))
        (loss, grad_w), grad_x = _jax.lax.scan(body, init, (per_example, given["loss_target"]))
    with _jax.named_scope("update"):
        delta_w, new_m, new_v = {}, {}, {}
        for n in TWIN_WEIGHTS:
            delta_w[n], new_m[n], new_v[n] = _adamw(weights[n], grad_w[n], given["m_" + n], given["v_" + n])
    return (loss, grad_x, *[grad_w[n] for n in TWIN_WEIGHTS], *[delta_w[n] for n in TWIN_WEIGHTS],
            *[new_m[n] for n in TWIN_WEIGHTS], *[new_v[n] for n in TWIN_WEIGHTS])
```

```python
import functools

import jax
import jax.numpy as jnp
from jax import lax
from jax.experimental import pallas as pl
from jax.experimental.pallas import tpu as pltpu

F32 = jnp.float32
BF16 = jnp.bfloat16
MXU_DTYPE = jnp.bfloat16

D = 1024
D_ATT = 512
D_KV = 128
D_RNN = 512
D_IN = 1792
D_FF = 3072
FF_CHUNK = 512
PLE = 256
HEADS = 8
HEAD_DIM = 64
BLK = 128
RNN_BLOCKS = 8
LN_EPS = 1e-5
LRU_C = 8.0
ALPHA = float(2.0 ** 0.25)
SCALE = HEAD_DIM ** -0.5
NEG = -1e30

ADAM_LR = 0.001
ADAM_B1 = 0.9
ADAM_B2 = 0.999
ADAM_EPS = 1e-08
ADAM_WD = 0.01
ADAM_STEP = 10

VMEM_LIMIT_BYTES = 56 * 1024 * 1024
MESH = pl.DeviceIdType.MESH

PACK_ROWS = (448, 256, 1536, 768, 256, 64)
PACK_OFF = tuple(sum(PACK_ROWS[:i]) for i in range(len(PACK_ROWS) + 1))
PACK_TOTAL = PACK_OFF[-1]
HALF = PACK_TOTAL // 2


def _params(**kw):
    return pltpu.CompilerParams(vmem_limit_bytes=VMEM_LIMIT_BYTES, **kw)


def _mm(a, b):
    return jnp.dot(a.astype(MXU_DTYPE), b.astype(MXU_DTYPE), preferred_element_type=F32)


def _mm_nt(a, b):
    return lax.dot_general(a.astype(MXU_DTYPE), b.astype(MXU_DTYPE), (((1,), (1,)), ((), ())),
                           preferred_element_type=F32)


def _mm_tn(a, b):
    return lax.dot_general(a.astype(MXU_DTYPE), b.astype(MXU_DTYPE), (((0,), (0,)), ((), ())),
                           preferred_element_type=F32)


def _sigmoid(x):
    return 1.0 / (1.0 + jnp.exp(-x))


def _gelu(x):
    c = 0.7978845608028654
    k = 0.044715
    t = jnp.tanh(c * (x + k * x * x * x))
    g = 0.5 * x * (1.0 + t)
    dg = 0.5 * (1.0 + t) + 0.5 * x * (1.0 - t * t) * c * (1.0 + 3.0 * k * x * x)
    return g, dg


def _expm1(x):
    poly = x * (1.0 + x * (0.5 + x * (1.0 / 6.0 + x * (1.0 / 24.0 + x * (1.0 / 120.0)))))
    return jnp.where(jnp.abs(x) < 0.03, poly, jnp.exp(x) - 1.0)


def _softplus(x):
    return jnp.maximum(x, 0.0) + jnp.log(1.0 + jnp.exp(-jnp.abs(x)))


def _ln(z, g, b):
    mu = jnp.mean(z, axis=-1, keepdims=True)
    zc = z - mu
    var = jnp.mean(zc * zc, axis=-1, keepdims=True)
    rstd = lax.rsqrt(var + LN_EPS)
    xhat = zc * rstd
    return xhat * g + b, xhat, rstd


def _ln_bwd(dy, xhat, rstd, g):
    dxh = dy * g
    m1 = jnp.mean(dxh, axis=-1, keepdims=True)
    m2 = jnp.mean(dxh * xhat, axis=-1, keepdims=True)
    return rstd * (dxh - m1 - xhat * m2)


def _colsum(x):
    return jnp.sum(x, axis=0, keepdims=True)


def _full(shape):
    nd = len(shape)
    return pl.BlockSpec(shape, lambda *_: (0,) * nd)


def _rows(tm, cols, fn=None):
    if fn is None:
        return pl.BlockSpec((tm, cols), lambda i: (i, 0))
    return pl.BlockSpec((tm, cols), lambda i: (fn(i), 0))


def _in_proj(x, w_in_t):
    T = x.shape[0]
    tm = 512

    def body(x_ref, w_ref, q_ref, kv_ref, xr_ref, gr_ref, xb_ref):
        xb = x_ref[...].astype(MXU_DTYPE)
        xb_ref[...] = xb.astype(BF16)
        q_ref[...] = _mm_nt(xb, w_ref[0:512, :]).astype(BF16)
        kv_ref[...] = _mm_nt(xb, w_ref[512:768, :]).astype(BF16)
        xr_ref[...] = _mm_nt(xb, w_ref[768:1280, :])
        gr_ref[...] = _mm_nt(xb, w_ref[1280:1792, :])

    return pl.pallas_call(
        body, name="in_proj", grid=(T // tm,),
        in_specs=[_rows(tm, D), _full((D_IN, D))],
        out_specs=[_rows(tm, 512), _rows(tm, 256), _rows(tm, 512), _rows(tm, 512), _rows(tm, D)],
        out_shape=[jax.ShapeDtypeStruct((T, 512), BF16), jax.ShapeDtypeStruct((T, 256), BF16),
                   jax.ShapeDtypeStruct((T, 512), F32), jax.ShapeDtypeStruct((T, 512), F32),
                   jax.ShapeDtypeStruct((T, D), BF16)],
        compiler_params=_params(),
    )(x, w_in_t)


def _attn_masks(i):
    row = lax.broadcasted_iota(jnp.int32, (BLK, BLK), 0)
    col = lax.broadcasted_iota(jnp.int32, (BLK, BLK), 1)
    mask_c = col <= row
    mask_p = jnp.logical_and(col > row, i > 0)
    return mask_c, mask_p


def _attn_probs(qh, kch, kph, sink, mask_c, mask_p):
    sc = jnp.where(mask_c, _mm_nt(qh, kch) * SCALE, NEG)
    sp = jnp.where(mask_p, _mm_nt(qh, kph) * SCALE, NEG)
    m = jnp.maximum(jnp.maximum(jnp.max(sc, axis=1, keepdims=True), jnp.max(sp, axis=1, keepdims=True)), sink)
    pc = jnp.exp(sc - m)
    pp = jnp.exp(sp - m)
    ps = jnp.exp(sink - m)
    den = jnp.sum(pc, axis=1, keepdims=True) + jnp.sum(pp, axis=1, keepdims=True) + ps
    return pc, pp, ps, den


def _attn_fwd(q, kv, sinks):
    T = q.shape[0]
    nb = T // BLK

    def body(q_ref, kv_ref, s_ref, o_ref):
        i = pl.program_id(0)
        cur = pl.multiple_of(i * BLK, BLK)
        prev = pl.multiple_of(jnp.maximum(i - 1, 0) * BLK, BLK)
        kvc = kv_ref[pl.ds(cur, BLK), :]
        kvp = kv_ref[pl.ds(prev, BLK), :]
        mask_c, mask_p = _attn_masks(i)
        for h in range(HEADS):
            g = h // 4
            qh = q_ref[:, h * 64:(h + 1) * 64]
            kch, kph = kvc[:, g * 64:(g + 1) * 64], kvp[:, g * 64:(g + 1) * 64]
            vch, vph = kvc[:, 128 + g * 64:192 + g * 64], kvp[:, 128 + g * 64:192 + g * 64]
            pc, pp, _, den = _attn_probs(qh, kch, kph, s_ref[0, h], mask_c, mask_p)
            o = (_mm(pc, vch) + _mm(pp, vph)) / den
            o_ref[:, h * 64:(h + 1) * 64] = o.astype(BF16)

    return pl.pallas_call(
        body, name="attn_fwd", grid=(nb,),
        in_specs=[_rows(BLK, 512), _full((T, 256)), pl.BlockSpec(memory_space=pltpu.SMEM)],
        out_specs=_rows(BLK, 512),
        out_shape=jax.ShapeDtypeStruct((T, 512), BF16),
        compiler_params=_params(),
    )(q, kv, sinks)


def _attn_bwd(q, kv, do, sinks):
    T = q.shape[0]
    nb = T // BLK

    def body(q_ref, kv_ref, do_ref, s_ref, dq_ref, dkv_ref, ds_ref):
        i = pl.program_id(0)
        cur = pl.multiple_of(i * BLK, BLK)
        prev = pl.multiple_of(jnp.maximum(i - 1, 0) * BLK, BLK)
        kvc = kv_ref[pl.ds(cur, BLK), :]
        kvp = kv_ref[pl.ds(prev, BLK), :]
        mask_c, mask_p = _attn_masks(i)

        @pl.when(i == 0)
        def _():
            ds_ref[...] = jnp.zeros_like(ds_ref)

        for g in range(2):
            kch, kph = kvc[:, g * 64:(g + 1) * 64], kvp[:, g * 64:(g + 1) * 64]
            vch, vph = kvc[:, 128 + g * 64:192 + g * 64], kvp[:, 128 + g * 64:192 + g * 64]
            dkc = jnp.zeros((BLK, 64), F32)
            dkp = jnp.zeros((BLK, 64), F32)
            dvc = jnp.zeros((BLK, 64), F32)
            dvp = jnp.zeros((BLK, 64), F32)
            for h in range(4 * g, 4 * g + 4):
                qh = q_ref[:, h * 64:(h + 1) * 64]
                doh = do_ref[:, h * 64:(h + 1) * 64]
                pc, pp, ps, den = _attn_probs(qh, kch, kph, s_ref[0, h], mask_c, mask_p)
                inv = 1.0 / den
                pc, pp, ps = pc * inv, pp * inv, ps * inv
                dpc = _mm_nt(doh, vch)
                dpp = _mm_nt(doh, vph)
                delta = jnp.sum(pc * dpc, axis=1, keepdims=True) + jnp.sum(pp * dpp, axis=1, keepdims=True)
                dsc = pc * (dpc - delta)
                dsp = pp * (dpp - delta)
                dsink = -jnp.sum(ps * delta, axis=0, keepdims=True)
                ds_ref[h:h + 1, :] += jnp.broadcast_to(dsink, (1, 128))
                dq_ref[:, h * 64:(h + 1) * 64] = ((_mm(dsc, kch) + _mm(dsp, kph)) * SCALE).astype(BF16)
                dkc += _mm_tn(dsc, qh) * SCALE
                dkp += _mm_tn(dsp, qh) * SCALE
                dvc += _mm_tn(pc, doh)
                dvp += _mm_tn(pp, doh)
            dkv_ref[pl.ds(cur, BLK), g * 64:(g + 1) * 64] = dkc
            dkv_ref[pl.ds(cur, BLK), 128 + g * 64:192 + g * 64] = dvc
            dkv_ref[pl.ds(prev, BLK), g * 64:(g + 1) * 64] += dkp
            dkv_ref[pl.ds(prev, BLK), 128 + g * 64:192 + g * 64] += dvp

    return pl.pallas_call(
        body, name="attn_bwd", grid=(nb,),
        in_specs=[_rows(BLK, 512), _full((T, 256)), _rows(BLK, 512), pl.BlockSpec(memory_space=pltpu.SMEM)],
        out_specs=[_rows(BLK, 512), _full((T, 256)), _full((8, 128))],
        out_shape=[jax.ShapeDtypeStruct((T, 512), BF16), jax.ShapeDtypeStruct((T, 256), F32),
                   jax.ShapeDtypeStruct((8, 128), F32)],
        compiler_params=_params(),
    )(q, kv, do, sinks)


def _rows8(tm, cols):
    return lax.broadcasted_iota(jnp.int32, (tm, cols), 0) & 7


def _lru_gates(xc, wa, ba, wx, bx, lam):
    r = _sigmoid(_mm(xc, wa) + ba)
    ii = _sigmoid(_mm(xc, wx) + bx)
    sp = _softplus(-lam)
    la = -LRU_C * r * sp
    a = jnp.exp(la)
    m = jnp.sqrt(-_expm1(2.0 * la))
    return r, ii, sp, a, m


def _rnn_fwd(xr, gr, cw, cb, wa, ba, wx, bx, lam):
    T = xr.shape[0]
    tm = 256
    C = D_RNN

    def body(xr_ref, gr_ref, cw_ref, cb_ref, wa_ref, ba_ref, wx_ref, bx_ref, lam_ref,
             xc_ref, h_ref, rec_ref, ext, a_s, b_s, carry):
        i = pl.program_id(0)

        @pl.when(i == 0)
        def _():
            ext[0:8, :] = jnp.zeros((8, C), F32)
            carry[...] = jnp.zeros((8, C), F32)

        ext[8:8 + tm, :] = xr_ref[...]
        xc = cb_ref[...] + cw_ref[3:4, :] * ext[8:8 + tm, :]
        for k in range(3):
            xc = xc + cw_ref[k:k + 1, :] * ext[5 + k:5 + k + tm, :]
        ext[0:8, :] = ext[tm:tm + 8, :]
        xc_ref[...] = xc
        _, ii, _, a, m = _lru_gates(xc, wa_ref[...], ba_ref[...], wx_ref[...], bx_ref[...], lam_ref[...])
        b = m * ii * xc
        r8 = _rows8(tm, C)
        for d in (1, 2, 4):
            ok = r8 >= d
            a_sh = jnp.where(ok, pltpu.roll(a, d, 0), 1.0)
            b_sh = jnp.where(ok, pltpu.roll(b, d, 0), 0.0)
            b = a * b_sh + b
            a = a * a_sh
        a_s[...] = a
        b_s[...] = b

        def step(g, hin):
            s = pl.multiple_of(g * 8, 8)
            hg = a_s[pl.ds(s, 8), :] * hin + b_s[pl.ds(s, 8), :]
            h_ref[pl.ds(s, 8), :] = hg
            return jnp.broadcast_to(hg[7:8, :], (8, C))

        carry[...] = lax.fori_loop(0, tm // 8, step, carry[...])
        ge, _ = _gelu(gr_ref[...])
        rec_ref[...] = (h_ref[...] * ge).astype(BF16)

    vec = _full((1, C))
    return pl.pallas_call(
        body, name="rnn_fwd", grid=(T // tm,),
        in_specs=[_rows(tm, C), _rows(tm, C), _full((4, C)), vec, _full((C, C)), vec, _full((C, C)), vec, vec],
        out_specs=[_rows(tm, C), _rows(tm, C), _rows(tm, C)],
        out_shape=[jax.ShapeDtypeStruct((T, C), F32), jax.ShapeDtypeStruct((T, C), F32),
                   jax.ShapeDtypeStruct((T, C), BF16)],
        scratch_shapes=[pltpu.VMEM((tm + 8, C), F32), pltpu.VMEM((tm, C), F32), pltpu.VMEM((tm, C), F32),
                        pltpu.VMEM((8, C), F32)],
        compiler_params=_params(),
    )(xr, gr, cw, cb, wa, ba, wx, bx, lam)


def _rnn_bwd(drec, gr, h, xc, xr, cw, wa, ba, wx, bx, lam):
    T = xr.shape[0]
    tm = 256
    C = D_RNN
    nt = T // tm
    t8 = tm // 8

    def body(drec_ref, gr_ref, h_ref, hp_ref, xc_ref, xr_ref, xrp_ref, cw_ref, wa_ref, ba_ref, wx_ref, bx_ref,
             lam_ref, dxr_ref, dgr_ref, dwa_ref, dwx_ref, dvec_ref, c_s, g_s, gout, ext, xext, anext, gcarry):
        i = pl.program_id(0)
        j = nt - 1 - i

        @pl.when(i == 0)
        def _():
            dwa_ref[...] = jnp.zeros_like(dwa_ref)
            dwx_ref[...] = jnp.zeros_like(dwx_ref)
            dvec_ref[...] = jnp.zeros_like(dvec_ref)
            anext[...] = jnp.zeros((8, C), F32)
            gcarry[...] = jnp.zeros((8, C), F32)
            ext[tm:tm + 8, :] = jnp.zeros((8, C), F32)

        xc = xc_ref[...]
        lam = lam_ref[...]
        r, ii, sp, a, m = _lru_gates(xc, wa_ref[...], ba_ref[...], wx_ref[...], bx_ref[...], lam)
        ge, dge = _gelu(gr_ref[...])
        drec = drec_ref[...]
        hh = h_ref[...]
        dgr_ref[...] = (drec * hh * dge).astype(BF16)
        dh = drec * ge
        rowi = lax.broadcasted_iota(jnp.int32, (tm, C), 0)
        c = jnp.where(rowi == tm - 1, jnp.broadcast_to(anext[0:1, :], (tm, C)), pltpu.roll(a, tm - 1, 0))
        anext[...] = a[0:8, :]
        r8 = rowi & 7
        gg = dh
        for d in (1, 2, 4):
            ok = r8 < 8 - d
            c_sh = jnp.where(ok, pltpu.roll(c, tm - d, 0), 1.0)
            g_sh = jnp.where(ok, pltpu.roll(gg, tm - d, 0), 0.0)
            gg = c * g_sh + gg
            c = c * c_sh
        c_s[...] = c
        g_s[...] = gg

        def step(k, gin):
            s = pl.multiple_of((t8 - 1 - k) * 8, 8)
            og = c_s[pl.ds(s, 8), :] * gin + g_s[pl.ds(s, 8), :]
            gout[pl.ds(s, 8), :] = og
            return jnp.broadcast_to(og[0:1, :], (8, C))

        gcarry[...] = lax.fori_loop(0, t8, step, gcarry[...])
        G = gout[...]
        hprev_row = jnp.where(j > 0, hp_ref[7:8, :], 0.0)
        hprev = jnp.where(rowi == 0, jnp.broadcast_to(hprev_row, (tm, C)), pltpu.roll(hh, 1, 0))
        da = G * hprev
        dm = G * ii * xc
        di = G * m * xc
        dxc = G * m * ii
        dla = da * a - dm * a * a / m
        dr = dla * (-LRU_C * sp)
        dsp = _colsum(dla * (-LRU_C * r))
        dlam = dsp * (-_sigmoid(-lam))
        dpr = dr * r * (1.0 - r)
        dpi = di * ii * (1.0 - ii)
        dxc = dxc + _mm_nt(dpr, wa_ref[...]) + _mm_nt(dpi, wx_ref[...])
        dwa_ref[...] += _mm_tn(xc, dpr)
        dwx_ref[...] += _mm_tn(xc, dpi)
        dvec_ref[0:1, :] += _colsum(dpr)
        dvec_ref[1:2, :] += _colsum(dpi)
        dvec_ref[2:3, :] += dlam
        dvec_ref[3:4, :] += _colsum(dxc)
        ext[0:tm, :] = dxc
        dxr = cw_ref[3:4, :] * dxc
        for k in range(3):
            dxr = dxr + cw_ref[k:k + 1, :] * ext[3 - k:3 - k + tm, :]
        ext[tm:tm + 8, :] = dxc[0:8, :]
        dxr_ref[...] = dxr.astype(BF16)
        xext[0:8, :] = jnp.where(j > 0, xrp_ref[...], 0.0)
        xext[8:8 + tm, :] = xr_ref[...]
        for k in range(4):
            dvec_ref[4 + k:5 + k, :] += _colsum(dxc * xext[5 + k:5 + k + tm, :])

    rev = lambda i: nt - 1 - i
    prev8 = lambda i: jnp.maximum((nt - 1 - i) * t8 - 1, 0)
    vec = _full((1, C))
    return pl.pallas_call(
        body, name="rnn_bwd", grid=(nt,),
        in_specs=[_rows(tm, C, rev), _rows(tm, C, rev), _rows(tm, C, rev), _rows(8, C, prev8), _rows(tm, C, rev),
                  _rows(tm, C, rev), _rows(8, C, prev8), _full((4, C)), _full((C, C)), vec, _full((C, C)), vec, vec],
        out_specs=[_rows(tm, C, rev), _rows(tm, C, rev), _full((C, C)), _full((C, C)), _full((8, C))],
        out_shape=[jax.ShapeDtypeStruct((T, C), BF16), jax.ShapeDtypeStruct((T, C), BF16),
                   jax.ShapeDtypeStruct((C, C), F32), jax.ShapeDtypeStruct((C, C), F32),
                   jax.ShapeDtypeStruct((8, C), F32)],
        scratch_shapes=[pltpu.VMEM((tm, C), F32), pltpu.VMEM((tm, C), F32), pltpu.VMEM((tm, C), F32),
                        pltpu.VMEM((tm + 8, C), F32), pltpu.VMEM((tm + 8, C), F32), pltpu.VMEM((8, C), F32),
                        pltpu.VMEM((8, C), F32)],
        compiler_params=_params(),
    )(drec, gr, h, h, xc, xr, xr, cw, wa, ba, wx, bx, lam)


def _out_proj(att, rec, x, w_out):
    T = x.shape[0]
    tm = 512

    def body(att_ref, rec_ref, x_ref, w_ref, z_ref):
        mix = _mm(att_ref[...], w_ref[0:512, :]) + _mm(rec_ref[...], w_ref[512:1024, :])
        z_ref[...] = ALPHA * x_ref[...] + mix

    return pl.pallas_call(
        body, name="out_proj", grid=(T // tm,),
        in_specs=[_rows(tm, 512), _rows(tm, 512), _rows(tm, D), _full((D, D))],
        out_specs=_rows(tm, D),
        out_shape=jax.ShapeDtypeStruct((T, D), F32),
        compiler_params=_params(),
    )(att, rec, x, w_out)


def _ffn_block(z1, p, tgt, w_up_t, w_down, w_g, w_p_t, g1, b1, g2, b2, bg, fcw, fcb):
    T = z1.shape[0]
    tm = 128
    E = tm + 16
    NC = D_FF // FF_CHUNK
    nt = T // tm
    t8 = tm // 8

    def body(z_ref, zp_ref, zn_ref, p_ref, pp_ref, pn_ref, t_ref, tp_ref, tn_ref,
             wup_hbm, wdn_hbm, wg_hbm, wp_hbm, g1_ref, b1_ref, g2_ref, b2_ref, bg_ref, fcw_ref, fcb_ref,
             dz1_ref, h1b_ref, dz2b_ref, dpre_ref, dpp_ref, act_ref, dup_ref, vec_ref, dfc_ref,
             wup, wdn, wg, wp, ze, pe, te, gate_s, val_s, gc_s):
        i = pl.program_id(0)

        @pl.when(i == 0)
        def _():
            pltpu.sync_copy(wup_hbm, wup)
            pltpu.sync_copy(wdn_hbm, wdn)
            pltpu.sync_copy(wg_hbm, wg)
            pltpu.sync_copy(wp_hbm, wp)
            vec_ref[...] = jnp.zeros_like(vec_ref)
            dfc_ref[...] = jnp.zeros_like(dfc_ref)

        for dst, (a, b, c) in ((ze, (zp_ref, z_ref, zn_ref)), (pe, (pp_ref, p_ref, pn_ref)), (te, (tp_ref, t_ref, tn_ref))):
            dst[0:8, :] = a[...]
            dst[8:8 + tm, :] = b[...]
            dst[8 + tm:E, :] = c[...]

        gidx = i * tm - 8 + lax.broadcasted_iota(jnp.int32, (E, 1), 0)
        valid = jnp.logical_and(gidx >= 0, gidx < T)
        g1v, b1v, g2v, b2v = g1_ref[...], b1_ref[...], g2_ref[...], b2_ref[...]
        h1, xh1, rstd1 = _ln(ze[...], g1v, b1v)
        h1b = h1.astype(MXU_DTYPE)
        m = slice(8, 8 + tm)

        def fwd(c, ffn):
            r0 = pl.multiple_of(c * FF_CHUNK, FF_CHUNK)
            gate = jnp.where(valid, _mm_nt(h1b, wup[pl.ds(r0, FF_CHUNK), :]), 0.0)
            val = _mm_nt(h1b, wup[pl.ds(D_FF + r0, FF_CHUNK), :])
            w = fcw_ref[c]
            gc = fcb_ref[c] + w[0:1, :] * pltpu.roll(gate, 2, 0) + w[1:2, :] * pltpu.roll(gate, 1, 0) + w[2:3, :] * gate
            ge, _ = _gelu(gc)
            act = ge * val
            gate_s[c] = gate
            val_s[c] = val
            gc_s[c] = gc
            act_ref[c] = act[m].astype(BF16)
            return ffn + _mm(act, wdn[pl.ds(r0, FF_CHUNK), :])

        ffn = lax.fori_loop(0, NC, fwd, jnp.zeros((E, D), F32))
        sg = _sigmoid(_mm(h1b, wg[...]) + bg_ref[...])
        pp = _mm_nt(pe[...], wp[...])
        z2 = ALPHA * h1 + ffn + sg * pp
        y, xh2, rstd2 = _ln(z2, g2v, b2v)
        diff = y - te[...]
        dy = jnp.where(valid, diff * (1.0 / D), 0.0)
        dz2 = _ln_bwd(dy, xh2, rstd2, g2v)
        dz2b = dz2.astype(MXU_DTYPE)

        def bwd(c, dh1):
            r0 = pl.multiple_of(c * FF_CHUNK, FF_CHUNK)
            dact = _mm_nt(dz2b, wdn[pl.ds(r0, FF_CHUNK), :])
            gate = gate_s[c]
            ge, dge = _gelu(gc_s[c])
            dval = dact * ge
            dgc = dact * val_s[c] * dge
            w = fcw_ref[c]
            dgate = w[2:3, :] * dgc + w[1:2, :] * pltpu.roll(dgc, E - 1, 0) + w[0:1, :] * pltpu.roll(dgc, E - 2, 0)
            dgm, dvm, dgcm = dgate[m], dval[m], dgc[m]
            dup_ref[c] = dgm.astype(BF16)
            dup_ref[NC + c] = dvm.astype(BF16)
            dfc_ref[c, 0:1, :] += _colsum(dgcm * pltpu.roll(gate, 2, 0)[m])
            dfc_ref[c, 1:2, :] += _colsum(dgcm * pltpu.roll(gate, 1, 0)[m])
            dfc_ref[c, 2:3, :] += _colsum(dgcm * gate[m])
            dfc_ref[c, 3:4, :] += _colsum(dgcm)
            return dh1 + _mm(dgm, wup[pl.ds(r0, FF_CHUNK), :]) + _mm(dvm, wup[pl.ds(D_FF + r0, FF_CHUNK), :])

        dz2m = dz2[m]
        sgm, ppm = sg[m], pp[m]
        dpre = dz2m * ppm * sgm * (1.0 - sgm)
        dppv = dz2m * sgm
        dh1 = lax.fori_loop(0, NC, bwd, ALPHA * dz2m + _mm_nt(dpre, wg[...]))
        xh1m = xh1[m]
        dz1_ref[...] = _ln_bwd(dh1, xh1m, rstd1[m], g1v)
        h1b_ref[...] = h1b[m].astype(BF16)
        dz2b_ref[...] = dz2m.astype(BF16)
        dpre_ref[...] = dpre.astype(BF16)
        dpp_ref[...] = dppv.astype(BF16)
        dym = dy[m]
        dfm = diff[m]
        loss = 0.5 * jnp.sum(jnp.sum(dfm * dfm, axis=1, keepdims=True), axis=0, keepdims=True) * (1.0 / D)
        vec_ref[0:1, :] += jnp.broadcast_to(loss, (1, D))
        vec_ref[1:2, :] += _colsum(dym * xh2[m])
        vec_ref[2:3, :] += _colsum(dym)
        vec_ref[3:4, :] += _colsum(dpre)
        vec_ref[4:5, :] += _colsum(dh1 * xh1m)
        vec_ref[5:6, :] += _colsum(dh1)

    prev8 = lambda i: jnp.maximum(i * t8 - 1, 0)
    next8 = lambda i: jnp.minimum((i + 1) * t8, T // 8 - 1)
    anyspec = pl.BlockSpec(memory_space=pl.ANY)
    vec = _full((1, D))

    def halo(cols):
        return [_rows(tm, cols), _rows(8, cols, prev8), _rows(8, cols, next8)]

    def chunked(n):
        return pl.BlockSpec((n, tm, FF_CHUNK), lambda i: (0, i, 0))

    return pl.pallas_call(
        body, name="ffn_block", grid=(nt,),
        in_specs=halo(D) + halo(PLE) + halo(D) + [anyspec] * 4 + [vec] * 5 + [_full((NC, 3, FF_CHUNK)), _full((NC, 1, FF_CHUNK))],
        out_specs=[_rows(tm, D)] * 5 + [chunked(NC), chunked(2 * NC), _full((8, D)), _full((NC, 8, FF_CHUNK))],
        out_shape=[jax.ShapeDtypeStruct((T, D), F32)] + [jax.ShapeDtypeStruct((T, D), BF16)] * 4
                  + [jax.ShapeDtypeStruct((NC, T, FF_CHUNK), BF16), jax.ShapeDtypeStruct((2 * NC, T, FF_CHUNK), BF16),
                     jax.ShapeDtypeStruct((8, D), F32), jax.ShapeDtypeStruct((NC, 8, FF_CHUNK), F32)],
        scratch_shapes=[pltpu.VMEM((2 * D_FF, D), MXU_DTYPE), pltpu.VMEM((D_FF, D), MXU_DTYPE),
                        pltpu.VMEM((D, D), MXU_DTYPE), pltpu.VMEM((D, PLE), MXU_DTYPE),
                        pltpu.VMEM((E, D), F32), pltpu.VMEM((E, PLE), F32), pltpu.VMEM((E, D), F32),
                        pltpu.VMEM((NC, E, FF_CHUNK), F32), pltpu.VMEM((NC, E, FF_CHUNK), F32),
                        pltpu.VMEM((NC, E, FF_CHUNK), F32)],
        compiler_params=_params(),
    )(z1, z1, z1, p, p, p, tgt, tgt, tgt, w_up_t, w_down, w_g, w_p_t, g1, b1, g2, b2, bg,
      fcw.reshape(3, NC, FF_CHUNK).transpose(1, 0, 2), fcb.reshape(NC, 1, FF_CHUNK))


def _out_proj_bwd(dz1, w_out):
    T = dz1.shape[0]
    tm = 512

    def body(dz_ref, w_ref, datt_ref, drec_ref, dzb_ref):
        dzb = dz_ref[...].astype(MXU_DTYPE)
        dzb_ref[...] = dzb.astype(BF16)
        datt_ref[...] = _mm_nt(dzb, w_ref[0:512, :]).astype(BF16)
        drec_ref[...] = _mm_nt(dzb, w_ref[512:1024, :])

    return pl.pallas_call(
        body, name="out_proj_bwd", grid=(T // tm,),
        in_specs=[_rows(tm, D), _full((D, D))],
        out_specs=[_rows(tm, 512), _rows(tm, 512), _rows(tm, D)],
        out_shape=[jax.ShapeDtypeStruct((T, 512), BF16), jax.ShapeDtypeStruct((T, 512), F32),
                   jax.ShapeDtypeStruct((T, D), BF16)],
        compiler_params=_params(),
    )(dz1, w_out)


def _in_proj_bwd(dq, dkv, dxr, dgr, dz1, w_in_t):
    T = dz1.shape[0]
    tm = 512

    def body(dq_ref, dkv_ref, dxr_ref, dgr_ref, dz_ref, w_ref, dx_ref, du_ref):
        dkv = dkv_ref[...].astype(BF16)
        dx_ref[...] = (ALPHA * dz_ref[...] + _mm(dq_ref[...], w_ref[0:512, :]) + _mm(dkv, w_ref[512:768, :])
                       + _mm(dxr_ref[...], w_ref[768:1280, :]) + _mm(dgr_ref[...], w_ref[1280:1792, :]))
        du_ref[:, 0:512] = dq_ref[...]
        du_ref[:, 512:768] = dkv
        du_ref[:, 768:1280] = dxr_ref[...]
        du_ref[:, 1280:1792] = dgr_ref[...]

    return pl.pallas_call(
        body, name="in_proj_bwd", grid=(T // tm,),
        in_specs=[_rows(tm, 512), _rows(tm, 256), _rows(tm, 512), _rows(tm, 512), _rows(tm, D), _full((D_IN, D))],
        out_specs=[_rows(tm, D), _rows(tm, D_IN)],
        out_shape=[jax.ShapeDtypeStruct((T, D), F32), jax.ShapeDtypeStruct((T, D_IN), BF16)],
        compiler_params=_params(),
    )(dq, dkv, dxr, dgr, dz1, w_in_t)


def _weight_grad(a, b, bm, name):
    if a.ndim == 3:
        assert a.shape[2] == bm
        T, M = a.shape[1], a.shape[0] * bm
        a_spec = pl.BlockSpec((None, 512, bm), lambda m, k: (m, k, 0))
    else:
        T, M = a.shape
        a_spec = pl.BlockSpec((512, bm), lambda m, k: (k, m))
    N = b.shape[1]
    bt = 512
    nk = T // bt

    def body(a_ref, b_ref, o_ref):
        k = pl.program_id(1)

        @pl.when(k == 0)
        def _():
            o_ref[...] = jnp.zeros_like(o_ref)

        o_ref[...] += _mm_tn(a_ref[...], b_ref[...])

    return pl.pallas_call(
        body, name=name, grid=(M // bm, nk),
        in_specs=[a_spec, pl.BlockSpec((bt, N), lambda m, k: (k, 0))],
        out_specs=pl.BlockSpec((bm, N), lambda m, k: (m, 0)),
        out_shape=jax.ShapeDtypeStruct((M, N), F32),
        compiler_params=_params(),
    )(a, b)


def _adamw(w, g, m, v, name):
    R, C = w.shape
    tr = R // 8 if R % 64 == 0 else R
    c1 = 1.0 / (1.0 - ADAM_B1 ** ADAM_STEP)
    c2 = 1.0 / (1.0 - ADAM_B2 ** ADAM_STEP)

    def body(w_ref, g_ref, m_ref, v_ref, d_ref, nm_ref, nv_ref):
        g = g_ref[...]
        nm = ADAM_B1 * m_ref[...] + (1.0 - ADAM_B1) * g
        nv = ADAM_B2 * v_ref[...] + (1.0 - ADAM_B2) * g * g
        nm_ref[...] = nm
        nv_ref[...] = nv
        d_ref[...] = -ADAM_LR * ((nm * c1) / (jnp.sqrt(nv * c2) + ADAM_EPS) + ADAM_WD * w_ref[...])

    spec = pl.BlockSpec((tr, C), lambda i: (i, 0))
    return pl.pallas_call(
        body, name=name, grid=(R // tr,),
        in_specs=[spec] * 4, out_specs=[spec] * 3,
        out_shape=[jax.ShapeDtypeStruct((R, C), F32)] * 3,
        compiler_params=_params(),
    )(w, g, m, v)


def _add2(a, b, name):
    R, C = a.shape
    tr = 416

    def body(a_ref, b_ref, o_ref):
        o_ref[...] = a_ref[...] + b_ref[...]

    spec = pl.BlockSpec((tr, C), lambda i: (i, 0))
    return pl.pallas_call(body, name=name, grid=(R // tr,), in_specs=[spec] * 2, out_specs=spec,
                          out_shape=jax.ShapeDtypeStruct((R, C), F32), compiler_params=_params())(a, b)


def _add4(a, name):
    _, R, C = a.shape
    tr = 416

    def body(a_ref, o_ref):
        o_ref[...] = ((a_ref[0] + a_ref[1]) + a_ref[2]) + a_ref[3]

    return pl.pallas_call(body, name=name, grid=(R // tr,),
                          in_specs=[pl.BlockSpec((4, tr, C), lambda i: (0, i, 0))],
                          out_specs=pl.BlockSpec((tr, C), lambda i: (i, 0)),
                          out_shape=jax.ShapeDtypeStruct((R, C), F32), compiler_params=_params())(a)


def _pos():
    return lax.axis_index("x"), lax.axis_index("y"), lax.axis_index("c")


def _other_chips(x, y):
    return [(1 - x, y), (x, 1 - y), (1 - x, 1 - y)]


def _gather_weights(wpack, cpack):
    def body(w_ref, c_ref, gw_ref, gc_ref, send_sems, recv_sems, local_sems):
        x, y, c = _pos()
        me = 2 * x + y
        loc = [pltpu.make_async_copy(w_ref, gw_ref.at[me], local_sems.at[0]),
               pltpu.make_async_copy(c_ref, gc_ref.at[me], local_sems.at[1])]
        for cp in loc:
            cp.start()
        sends = []
        for k, (px, py) in enumerate(_other_chips(x, y)):
            for n, (src, dst) in enumerate(((w_ref, gw_ref), (c_ref, gc_ref))):
                cp = pltpu.make_async_remote_copy(src_ref=src, dst_ref=dst.at[me], send_sem=send_sems.at[2 * k + n],
                                                  recv_sem=recv_sems.at[2 * k + n], device_id=(px, py, c),
                                                  device_id_type=MESH)
                cp.start()
                sends.append(cp)
        for k, (px, py) in enumerate(_other_chips(x, y)):
            for n, (src, dst) in enumerate(((w_ref, gw_ref), (c_ref, gc_ref))):
                pltpu.make_async_remote_copy(src_ref=src, dst_ref=dst.at[2 * px + py], send_sem=send_sems.at[2 * k + n],
                                             recv_sem=recv_sems.at[2 * k + n], device_id=(px, py, c),
                                             device_id_type=MESH).wait_recv()
        for cp in sends:
            cp.wait_send()
        for cp in loc:
            cp.wait()

    anyspec = pl.BlockSpec(memory_space=pl.ANY)
    return pl.pallas_call(
        body, name="gather_weights",
        in_specs=[anyspec, anyspec], out_specs=[anyspec, anyspec],
        out_shape=[jax.ShapeDtypeStruct((4,) + wpack.shape, wpack.dtype), jax.ShapeDtypeStruct((4,) + cpack.shape, cpack.dtype)],
        scratch_shapes=[pltpu.SemaphoreType.DMA((6,)), pltpu.SemaphoreType.DMA((6,)), pltpu.SemaphoreType.DMA((2,))],
        compiler_params=_params(has_side_effects=True),
    )(wpack, cpack)


def _allreduce_small(s):
    R = s.shape[0]

    def body(s_ref, o_ref, buf, send_sems, recv_sems):
        x, y, c = _pos()
        me = 4 * x + 2 * y + c
        buf[me] = s_ref[...]
        sends = []
        for k in range(1, 8):
            peer = (x ^ (k >> 2), y ^ ((k >> 1) & 1), c ^ (k & 1))
            cp = pltpu.make_async_remote_copy(src_ref=s_ref, dst_ref=buf.at[me], send_sem=send_sems.at[k - 1],
                                              recv_sem=recv_sems.at[k - 1], device_id=peer, device_id_type=MESH)
            cp.start()
            sends.append(cp)
        for k in range(1, 8):
            px, py, pc = x ^ (k >> 2), y ^ ((k >> 1) & 1), c ^ (k & 1)
            pltpu.make_async_remote_copy(src_ref=s_ref, dst_ref=buf.at[4 * px + 2 * py + pc], send_sem=send_sems.at[k - 1],
                                         recv_sem=recv_sems.at[k - 1], device_id=(px, py, pc),
                                         device_id_type=MESH).wait_recv()
        for cp in sends:
            cp.wait_send()
        acc = buf[0]
        for d in range(1, 8):
            acc = acc + buf[d]
        o_ref[...] = acc

    vm = pl.BlockSpec(memory_space=pltpu.VMEM)
    return pl.pallas_call(
        body, name="allreduce_small", in_specs=[vm], out_specs=vm,
        out_shape=jax.ShapeDtypeStruct((R, 128), F32),
        scratch_shapes=[pltpu.VMEM((8, R, 128), F32), pltpu.SemaphoreType.DMA((7,)), pltpu.SemaphoreType.DMA((7,))],
        compiler_params=_params(has_side_effects=True),
    )(s)


def _swap_halves(g):
    def body(g_ref, o_ref, send_sem, recv_sem):
        x, y, c = _pos()
        start = pl.multiple_of((1 - c) * HALF, 8)
        cp = pltpu.make_async_remote_copy(src_ref=g_ref.at[:, pl.ds(start, HALF), :], dst_ref=o_ref, send_sem=send_sem,
                                          recv_sem=recv_sem, device_id=(x, y, 1 - c), device_id_type=MESH)
        cp.start()
        cp.wait()

    anyspec = pl.BlockSpec(memory_space=pl.ANY)
    return pl.pallas_call(
        body, name="swap_halves", in_specs=[anyspec], out_specs=anyspec,
        out_shape=jax.ShapeDtypeStruct((4, HALF, 1024), F32),
        scratch_shapes=[pltpu.SemaphoreType.DMA, pltpu.SemaphoreType.DMA],
        compiler_params=_params(has_side_effects=True),
    )(g)


def _scatter_chips(s):
    def body(s_ref, o_ref, send_sems, recv_sems, local_sem):
        x, y, c = _pos()
        me = 2 * x + y
        loc = pltpu.make_async_copy(s_ref.at[me], o_ref.at[me], local_sem)
        loc.start()
        sends = []
        for k, (px, py) in enumerate(_other_chips(x, y)):
            cp = pltpu.make_async_remote_copy(src_ref=s_ref.at[2 * px + py], dst_ref=o_ref.at[me], send_sem=send_sems.at[k],
                                              recv_sem=recv_sems.at[k], device_id=(px, py, c), device_id_type=MESH)
            cp.start()
            sends.append(cp)
        for k, (px, py) in enumerate(_other_chips(x, y)):
            pltpu.make_async_remote_copy(src_ref=s_ref.at[me], dst_ref=o_ref.at[2 * px + py], send_sem=send_sems.at[k],
                                         recv_sem=recv_sems.at[k], device_id=(px, py, c), device_id_type=MESH).wait_recv()
        for cp in sends:
            cp.wait_send()
        loc.wait()

    anyspec = pl.BlockSpec(memory_space=pl.ANY)
    return pl.pallas_call(
        body, name="scatter_chips", in_specs=[anyspec], out_specs=anyspec,
        out_shape=jax.ShapeDtypeStruct((4, HALF, 1024), F32),
        scratch_shapes=[pltpu.SemaphoreType.DMA((3,)), pltpu.SemaphoreType.DMA((3,)), pltpu.SemaphoreType.DMA],
        compiler_params=_params(has_side_effects=True),
    )(s)


def _share_halves(r):
    def body(r_ref, o_ref, send_sem, recv_sem, local_sem):
        x, y, c = _pos()
        loc = pltpu.make_async_copy(r_ref, o_ref.at[c], local_sem)
        loc.start()
        cp = pltpu.make_async_remote_copy(src_ref=r_ref, dst_ref=o_ref.at[c], send_sem=send_sem, recv_sem=recv_sem,
                                          device_id=(x, y, 1 - c), device_id_type=MESH)
        cp.start()
        pltpu.make_async_remote_copy(src_ref=r_ref, dst_ref=o_ref.at[1 - c], send_sem=send_sem, recv_sem=recv_sem,
                                     device_id=(x, y, 1 - c), device_id_type=MESH).wait_recv()
        cp.wait_send()
        loc.wait()

    anyspec = pl.BlockSpec(memory_space=pl.ANY)
    return pl.pallas_call(
        body, name="share_halves", in_specs=[anyspec], out_specs=anyspec,
        out_shape=jax.ShapeDtypeStruct((2, HALF, 1024), F32),
        scratch_shapes=[pltpu.SemaphoreType.DMA, pltpu.SemaphoreType.DMA, pltpu.SemaphoreType.DMA],
        compiler_params=_params(has_side_effects=True),
    )(r)


def _add_half(g, r, c):
    tr = 416
    nb = HALF // tr

    def body(c_ref, g_ref, r_ref, o_ref):
        o_ref[...] = g_ref[...] + r_ref[...]

    grid_spec = pltpu.PrefetchScalarGridSpec(
        num_scalar_prefetch=1, grid=(4, nb),
        in_specs=[pl.BlockSpec((1, tr, 1024), lambda j, i, c_ref: (j, c_ref[0] * nb + i, 0)),
                  pl.BlockSpec((1, tr, 1024), lambda j, i, c_ref: (j, i, 0))],
        out_specs=pl.BlockSpec((1, tr, 1024), lambda j, i, c_ref: (j, i, 0)))
    return pl.pallas_call(body, name="add_half", grid_spec=grid_spec,
                          out_shape=jax.ShapeDtypeStruct((4, HALF, 1024), F32), compiler_params=_params())(c, g, r)


def _block_diag(w):
    eye = jnp.eye(RNN_BLOCKS, dtype=w.dtype)
    return (eye[:, None, :, None] * w[:, :, None, :]).reshape(D_RNN, D_RNN)


def _diag_blocks(wd):
    d = wd.reshape(RNN_BLOCKS, 64, RNN_BLOCKS, 64)
    return jnp.stack([d[h, :, h, :] for h in range(RNN_BLOCKS)])


def _layer_grads(x, p, tgt, gw, small):
    row = lambda v: v.reshape(1, -1)
    wa = _block_diag(small["gate_a_w"]).astype(MXU_DTYPE)
    wx = _block_diag(small["gate_x_w"]).astype(MXU_DTYPE)
    sinks = small["attn_sinks"].reshape(1, HEADS)

    q, kv, xr, gr, xb = _in_proj(x, gw["w_in_t"])
    att = _attn_fwd(q, kv, sinks)
    xc, h, rec = _rnn_fwd(xr, gr, small["rnn_conv_w"], row(small["rnn_conv_b"]), wa, row(small["gate_a_b"]),
                          wx, row(small["gate_x_b"]), row(small["lru_lambda"]))
    z1 = _out_proj(att, rec, x, gw["w_out"])
    dz1, h1b, dz2b, dpre, dpp, act, dup, vec, dfc = _ffn_block(
        z1, p, tgt, gw["w_up_t"], gw["w_down"], gw["w_g"], gw["w_p_t"], row(small["ln1_g"]), row(small["ln1_b"]),
        row(small["ln2_g"]), row(small["ln2_b"]), row(small["ple_gate_b"]), small["ffn_conv_w"], row(small["ffn_conv_b"]))
    datt, drec, dz1b = _out_proj_bwd(dz1, gw["w_out"])
    dxr, dgr, dwa, dwx, dvec = _rnn_bwd(drec, gr, h, xc, xr, small["rnn_conv_w"], wa, row(small["gate_a_b"]),
                                        wx, row(small["gate_x_b"]), row(small["lru_lambda"]))
    dq, dkv, dsinks = _attn_bwd(q, kv, datt, sinks)
    grad_x, du = _in_proj_bwd(dq, dkv, dxr, dgr, dz1, gw["w_in_t"])

    mix = jnp.concatenate([att, rec], axis=1)
    pb = p.astype(BF16)
    big = {
        "w_in_t": _weight_grad(du, xb, 256, "dw_in"),
        "w_out": _weight_grad(mix, dz1b, 512, "dw_out"),
        "w_up_t": _weight_grad(dup, h1b, 512, "dw_up"),
        "w_down": _weight_grad(act, dz2b, 512, "dw_down"),
        "w_g": _weight_grad(h1b, dpre, 512, "dw_gate"),
        "w_p_t": _weight_grad(dpp, pb, 512, "dw_proj"),
    }
    sg = {
        "attn_sinks": dsinks[:, 0],
        "rnn_conv_w": dvec[4:8],
        "rnn_conv_b": dvec[3],
        "gate_a_w": _diag_blocks(dwa),
        "gate_a_b": dvec[0],
        "gate_x_w": _diag_blocks(dwx),
        "gate_x_b": dvec[1],
        "lru_lambda": dvec[2],
        "ln1_g": vec[4],
        "ln1_b": vec[5],
        "ffn_conv_w": dfc[:, 0:3].transpose(1, 0, 2).reshape(3, D_FF),
        "ffn_conv_b": dfc[:, 3].reshape(D_FF),
        "ple_gate_b": vec[3],
        "ln2_g": vec[1],
        "ln2_b": vec[2],
    }
    return grad_x, big, sg, vec[0, 0:1]


BIG = ("w_in", "w_out", "w_ffn_up", "w_ffn_down", "ple_gate_w", "ple_proj")
BIG_KEYS = ("w_in_t", "w_out", "w_up_t", "w_down", "w_g", "w_p_t")
BIG_T = (True, False, True, False, False, True)
SMALL = ("attn_sinks", "rnn_conv_w", "rnn_conv_b", "gate_a_w", "gate_a_b", "gate_x_w", "gate_x_b", "lru_lambda",
         "ln1_g", "ln1_b", "ffn_conv_w", "ffn_conv_b", "ple_gate_b", "ln2_g", "ln2_b")
SHARDED_SMALL = ("rnn_conv_w", "ffn_conv_w")
WEIGHTS = ("w_in", "attn_sinks", "rnn_conv_w", "rnn_conv_b", "gate_a_w", "gate_a_b", "gate_x_w", "gate_x_b",
           "lru_lambda", "w_out", "ln1_g", "ln1_b", "w_ffn_up", "ffn_conv_w", "ffn_conv_b", "w_ffn_down",
           "ple_gate_w", "ple_gate_b", "ple_proj", "ln2_g", "ln2_b")


def _pack_big(d):
    parts = []
    for name, t in zip(BIG, BIG_T):
        a = d[name]
        a = a.T if t else a
        parts.append(a.reshape(-1, 1024))
    return jnp.concatenate(parts, axis=0)


def _unpack_big(a):
    out = {}
    shapes = {"w_in": (448, 1024), "w_out": (256, 1024), "w_ffn_up": (1536, 1024), "w_ffn_down": (768, 1024),
              "ple_gate_w": (256, 1024), "ple_proj": (256, 256)}
    for i, (name, t) in enumerate(zip(BIG, BIG_T)):
        s = a[PACK_OFF[i]:PACK_OFF[i + 1]].reshape(shapes[name])
        out[name] = (s.T if t else s)[None]
    return out


def _pack_vecs(items):
    parts, offs, n = [], [], 0
    for a in items:
        f = a.reshape(-1).astype(F32)
        pad = (-f.shape[0]) % 128
        parts.append(jnp.pad(f, (0, pad)))
        offs.append(n)
        n += (f.shape[0] + pad) // 128
    padr = (-n) % 8
    if padr:
        parts.append(jnp.zeros((padr * 128,), F32))
    return jnp.concatenate(parts).reshape(-1, 128), offs


def _unpack_vecs(a, offs, shapes):
    flat = a.reshape(-1)
    out = []
    for o, s in zip(offs, shapes):
        n = 1
        for d in s:
            n *= d
        out.append(flat[o * 128:o * 128 + n].reshape(s))
    return out


def kernel(x, p, w_in, attn_sinks, rnn_conv_w, rnn_conv_b, gate_a_w, gate_a_b, gate_x_w, gate_x_b, lru_lambda, w_out, ln1_g, ln1_b, w_ffn_up, ffn_conv_w, ffn_conv_b, w_ffn_down, ple_gate_w, ple_gate_b, ple_proj, ln2_g, ln2_b, loss_target, m_w_in, m_attn_sinks, m_rnn_conv_w, m_rnn_conv_b, m_gate_a_w, m_gate_a_b, m_gate_x_w, m_gate_x_b, m_lru_lambda, m_w_out, m_ln1_g, m_ln1_b, m_w_ffn_up, m_ffn_conv_w, m_ffn_conv_b, m_w_ffn_down, m_ple_gate_w, m_ple_gate_b, m_ple_proj, m_ln2_g, m_ln2_b, v_w_in, v_attn_sinks, v_rnn_conv_w, v_rnn_conv_b, v_gate_a_w, v_gate_a_b, v_gate_x_w, v_gate_x_b, v_lru_lambda, v_w_out, v_ln1_g, v_ln1_b, v_w_ffn_up, v_ffn_conv_w, v_ffn_conv_b, v_w_ffn_down, v_ple_gate_w, v_ple_gate_b, v_ple_proj, v_ln2_g, v_ln2_b):
    w = dict(w_in=w_in, attn_sinks=attn_sinks, rnn_conv_w=rnn_conv_w, rnn_conv_b=rnn_conv_b, gate_a_w=gate_a_w,
             gate_a_b=gate_a_b, gate_x_w=gate_x_w, gate_x_b=gate_x_b, lru_lambda=lru_lambda, w_out=w_out, ln1_g=ln1_g,
             ln1_b=ln1_b, w_ffn_up=w_ffn_up, ffn_conv_w=ffn_conv_w, ffn_conv_b=ffn_conv_b, w_ffn_down=w_ffn_down,
             ple_gate_w=ple_gate_w, ple_gate_b=ple_gate_b, ple_proj=ple_proj, ln2_g=ln2_g, ln2_b=ln2_b)
    m = dict(w_in=m_w_in, attn_sinks=m_attn_sinks, rnn_conv_w=m_rnn_conv_w, rnn_conv_b=m_rnn_conv_b, gate_a_w=m_gate_a_w,
             gate_a_b=m_gate_a_b, gate_x_w=m_gate_x_w, gate_x_b=m_gate_x_b, lru_lambda=m_lru_lambda, w_out=m_w_out,
             ln1_g=m_ln1_g, ln1_b=m_ln1_b, w_ffn_up=m_w_ffn_up, ffn_conv_w=m_ffn_conv_w, ffn_conv_b=m_ffn_conv_b,
             w_ffn_down=m_w_ffn_down, ple_gate_w=m_ple_gate_w, ple_gate_b=m_ple_gate_b, ple_proj=m_ple_proj,
             ln2_g=m_ln2_g, ln2_b=m_ln2_b)
    v = dict(w_in=v_w_in, attn_sinks=v_attn_sinks, rnn_conv_w=v_rnn_conv_w, rnn_conv_b=v_rnn_conv_b, gate_a_w=v_gate_a_w,
             gate_a_b=v_gate_a_b, gate_x_w=v_gate_x_w, gate_x_b=v_gate_x_b, lru_lambda=v_lru_lambda, w_out=v_w_out,
             ln1_g=v_ln1_g, ln1_b=v_ln1_b, w_ffn_up=v_w_ffn_up, ffn_conv_w=v_ffn_conv_w, ffn_conv_b=v_ffn_conv_b,
             w_ffn_down=v_w_ffn_down, ple_gate_w=v_ple_gate_w, ple_gate_b=v_ple_gate_b, ple_proj=v_ple_proj,
             ln2_g=v_ln2_g, ln2_b=v_ln2_b)
    w, m, v = ({k: a[0] for k, a in d.items()} for d in (w, m, v))
    chip = 2 * lax.axis_index("x") + lax.axis_index("y")
    core = lax.axis_index("c")

    wpack = _pack_big(w)
    cpack, _ = _pack_vecs([w["rnn_conv_w"], w["ffn_conv_w"]])
    gwp, gcp = _gather_weights(wpack.astype(MXU_DTYPE), cpack)
    gw = {}
    for i, key in enumerate(BIG_KEYS):
        a = gwp[:, PACK_OFF[i]:PACK_OFF[i + 1]]
        gw[key] = a.reshape(4 * 256, 256) if key == "w_p_t" else a.reshape(-1, 1024)
    small = {k: w[k] for k in SMALL}
    small["rnn_conv_w"] = gcp[:, 0:4].reshape(4, 4, 128).transpose(1, 0, 2).reshape(4, 512)
    small["ffn_conv_w"] = gcp[:, 4:22].reshape(4, 3, 768).transpose(1, 0, 2).reshape(3, 3072)

    grad_x, big, sg, loss = _layer_grads(x[0], p[0, 0], loss_target[0], gw, small)

    spack, offs = _pack_vecs([sg[k] for k in SMALL] + [loss])
    ssum = _allreduce_small(spack)
    shapes = [sg[k].shape for k in SMALL] + [(1,)]
    red = dict(zip(SMALL + ("loss",), _unpack_vecs(ssum, offs, shapes)))
    red["rnn_conv_w"] = lax.dynamic_slice_in_dim(red["rnn_conv_w"], chip * 128, 128, axis=1)
    red["ffn_conv_w"] = lax.dynamic_slice_in_dim(red["ffn_conv_w"], chip * 768, 768, axis=1)

    parts = []
    for i, key in enumerate(BIG_KEYS):
        parts.append(big[key].reshape(4, PACK_ROWS[i], 1024))
    gpack = jnp.concatenate(parts, axis=1)
    sib = _swap_halves(gpack)
    chip_sum = _add_half(gpack, sib, core.reshape(1).astype(jnp.int32))
    from_chips = _scatter_chips(chip_sum)
    half = _add4(from_chips, "add_chips")
    gbig = _share_halves(half).reshape(PACK_TOTAL, 1024)

    dbig, mbig, vbig = _adamw(wpack, gbig, _pack_big(m), _pack_big(v), "adamw_big")
    wsm, offs2 = _pack_vecs([w[k] for k in SMALL])
    gsm, _ = _pack_vecs([red[k] for k in SMALL])
    msm, _ = _pack_vecs([m[k] for k in SMALL])
    vsm, _ = _pack_vecs([v[k] for k in SMALL])
    dsm, nmsm, nvsm = _adamw(wsm, gsm, msm, vsm, "adamw_small")
    shapes2 = [w[k].shape for k in SMALL]

    def named(bigp, smallp):
        d = _unpack_big(bigp)
        d.update({k: a[None] for k, a in zip(SMALL, _unpack_vecs(smallp, offs2, shapes2))})
        return [d[k] for k in WEIGHTS]

    grads = named(gbig, gsm)
    return (red["loss"].reshape(()), grad_x[None], *grads, *named(dbig, dsm), *named(mbig, nmsm), *named(vbig, nvsm))
```

```python
import functools

import jax
import jax.numpy as jnp
from jax import lax
from jax.experimental import pallas as pl
from jax.experimental.pallas import tpu as pltpu

F32 = jnp.float32
BF16 = jnp.bfloat16
MXU_DTYPE = jnp.bfloat16

D = 1024
D_ATT = 512
D_KV = 128
D_RNN = 512
D_IN = 1792
D_FF = 3072
FF_CHUNK = 512
PLE = 256
HEADS = 8
HEAD_DIM = 64
BLK = 128
RNN_BLOCKS = 8
LN_EPS = 1e-5
LRU_C = 8.0
ALPHA = float(2.0 ** 0.25)
SCALE = HEAD_DIM ** -0.5
NEG = -1e30

ADAM_LR = 0.001
ADAM_B1 = 0.9
ADAM_B2 = 0.999
ADAM_EPS = 1e-08
ADAM_WD = 0.01
ADAM_STEP = 10

VMEM_LIMIT_BYTES = 56 * 1024 * 1024
MESH = pl.DeviceIdType.MESH

PACK_ROWS = (448, 256, 1536, 768, 256, 64)
PACK_OFF = tuple(sum(PACK_ROWS[:i]) for i in range(len(PACK_ROWS) + 1))
PACK_TOTAL = PACK_OFF[-1]
HALF = PACK_TOTAL // 2


def _params(**kw):
    return pltpu.CompilerParams(vmem_limit_bytes=VMEM_LIMIT_BYTES, **kw)


def _mm(a, b):
    return jnp.dot(a.astype(MXU_DTYPE), b.astype(MXU_DTYPE), preferred_element_type=F32)


def _mm_nt(a, b):
    return lax.dot_general(a.astype(MXU_DTYPE), b.astype(MXU_DTYPE), (((1,), (1,)), ((), ())),
                           preferred_element_type=F32)


def _mm_tn(a, b):
    return lax.dot_general(a.astype(MXU_DTYPE), b.astype(MXU_DTYPE), (((0,), (0,)), ((), ())),
                           preferred_element_type=F32)


def _sigmoid(x):
    return 1.0 / (1.0 + jnp.exp(-x))


def _gelu(x):
    c = 0.7978845608028654
    k = 0.044715
    t = jnp.tanh(c * (x + k * x * x * x))
    g = 0.5 * x * (1.0 + t)
    dg = 0.5 * (1.0 + t) + 0.5 * x * (1.0 - t * t) * c * (1.0 + 3.0 * k * x * x)
    return g, dg


def _expm1(x):
    poly = x * (1.0 + x * (0.5 + x * (1.0 / 6.0 + x * (1.0 / 24.0 + x * (1.0 / 120.0)))))
    return jnp.where(jnp.abs(x) < 0.03, poly, jnp.exp(x) - 1.0)


def _softplus(x):
    return jnp.maximum(x, 0.0) + jnp.log(1.0 + jnp.exp(-jnp.abs(x)))


def _ln(z, g, b):
    mu = jnp.mean(z, axis=-1, keepdims=True)
    zc = z - mu
    var = jnp.mean(zc * zc, axis=-1, keepdims=True)
    rstd = lax.rsqrt(var + LN_EPS)
    xhat = zc * rstd
    return xhat * g + b, xhat, rstd


def _ln_bwd(dy, xhat, rstd, g):
    dxh = dy * g
    m1 = jnp.mean(dxh, axis=-1, keepdims=True)
    m2 = jnp.mean(dxh * xhat, axis=-1, keepdims=True)
    return rstd * (dxh - m1 - xhat * m2)


def _colsum(x):
    return jnp.sum(x, axis=0, keepdims=True)


def _full(shape):
    nd = len(shape)
    return pl.BlockSpec(shape, lambda *_: (0,) * nd)


def _rows(tm, cols, fn=None):
    if fn is None:
        return pl.BlockSpec((tm, cols), lambda i: (i, 0))
    return pl.BlockSpec((tm, cols), lambda i: (fn(i), 0))


def _in_proj(x, w_in_t):
    T = x.shape[0]
    tm = 512

    def body(x_ref, w_ref, q_ref, kv_ref, xr_ref, gr_ref, xb_ref):
        xb = x_ref[...].astype(MXU_DTYPE)
        xb_ref[...] = xb.astype(BF16)
        q_ref[...] = _mm_nt(xb, w_ref[0:512, :]).astype(BF16)
        kv_ref[...] = _mm_nt(xb, w_ref[512:768, :]).astype(BF16)
        xr_ref[...] = _mm_nt(xb, w_ref[768:1280, :])
        gr_ref[...] = _mm_nt(xb, w_ref[1280:1792, :])

    return pl.pallas_call(
        body, name="in_proj", grid=(T // tm,),
        in_specs=[_rows(tm, D), _full((D_IN, D))],
        out_specs=[_rows(tm, 512), _rows(tm, 256), _rows(tm, 512), _rows(tm, 512), _rows(tm, D)],
        out_shape=[jax.ShapeDtypeStruct((T, 512), BF16), jax.ShapeDtypeStruct((T, 256), BF16),
                   jax.ShapeDtypeStruct((T, 512), F32), jax.ShapeDtypeStruct((T, 512), F32),
                   jax.ShapeDtypeStruct((T, D), BF16)],
        compiler_params=_params(),
    )(x, w_in_t)


def _attn_masks(i):
    row = lax.broadcasted_iota(jnp.int32, (BLK, BLK), 0)
    col = lax.broadcasted_iota(jnp.int32, (BLK, BLK), 1)
    mask_c = col <= row
    mask_p = jnp.logical_and(col > row, i > 0)
    return mask_c, mask_p


def _attn_probs(qh, kch, kph, sink, mask_c, mask_p):
    sc = jnp.where(mask_c, _mm_nt(qh, kch) * SCALE, NEG)
    sp = jnp.where(mask_p, _mm_nt(qh, kph) * SCALE, NEG)
    m = jnp.maximum(jnp.maximum(jnp.max(sc, axis=1, keepdims=True), jnp.max(sp, axis=1, keepdims=True)), sink)
    pc = jnp.exp(sc - m)
    pp = jnp.exp(sp - m)
    ps = jnp.exp(sink - m)
    den = jnp.sum(pc, axis=1, keepdims=True) + jnp.sum(pp, axis=1, keepdims=True) + ps
    return pc, pp, ps, den


def _attn_fwd(q, kv, sinks):
    T = q.shape[0]
    nb = T // BLK

    def body(q_ref, kv_ref, s_ref, o_ref):
        i = pl.program_id(0)
        cur = pl.multiple_of(i * BLK, BLK)
        prev = pl.multiple_of(jnp.maximum(i - 1, 0) * BLK, BLK)
        kvc = kv_ref[pl.ds(cur, BLK), :]
        kvp = kv_ref[pl.ds(prev, BLK), :]
        mask_c, mask_p = _attn_masks(i)
        for h in range(HEADS):
            g = h // 4
            qh = q_ref[:, h * 64:(h + 1) * 64]
            kch, kph = kvc[:, g * 64:(g + 1) * 64], kvp[:, g * 64:(g + 1) * 64]
            vch, vph = kvc[:, 128 + g * 64:192 + g * 64], kvp[:, 128 + g * 64:192 + g * 64]
            pc, pp, _, den = _attn_probs(qh, kch, kph, s_ref[0, h], mask_c, mask_p)
            o = (_mm(pc, vch) + _mm(pp, vph)) / den
            o_ref[:, h * 64:(h + 1) * 64] = o.astype(BF16)

    return pl.pallas_call(
        body, name="attn_fwd", grid=(nb,),
        in_specs=[_rows(BLK, 512), _full((T, 256)), pl.BlockSpec(memory_space=pltpu.SMEM)],
        out_specs=_rows(BLK, 512),
        out_shape=jax.ShapeDtypeStruct((T, 512), BF16),
        compiler_params=_params(),
    )(q, kv, sinks)


def _attn_bwd(q, kv, do, sinks):
    T = q.shape[0]
    nb = T // BLK

    def body(q_ref, kv_ref, do_ref, s_ref, dq_ref, dkv_ref, ds_ref):
        i = pl.program_id(0)
        cur = pl.multiple_of(i * BLK, BLK)
        prev = pl.multiple_of(jnp.maximum(i - 1, 0) * BLK, BLK)
        kvc = kv_ref[pl.ds(cur, BLK), :]
        kvp = kv_ref[pl.ds(prev, BLK), :]
        mask_c, mask_p = _attn_masks(i)

        @pl.when(i == 0)
        def _():
            ds_ref[...] = jnp.zeros_like(ds_ref)

        for g in range(2):
            kch, kph = kvc[:, g * 64:(g + 1) * 64], kvp[:, g * 64:(g + 1) * 64]
            vch, vph = kvc[:, 128 + g * 64:192 + g * 64], kvp[:, 128 + g * 64:192 + g * 64]
            dkc = jnp.zeros((BLK, 64), F32)
            dkp = jnp.zeros((BLK, 64), F32)
            dvc = jnp.zeros((BLK, 64), F32)
            dvp = jnp.zeros((BLK, 64), F32)
            for h in range(4 * g, 4 * g + 4):
                qh = q_ref[:, h * 64:(h + 1) * 64]
                doh = do_ref[:, h * 64:(h + 1) * 64]
                pc, pp, ps, den = _attn_probs(qh, kch, kph, s_ref[0, h], mask_c, mask_p)
                inv = 1.0 / den
                pc, pp, ps = pc * inv, pp * inv, ps * inv
                dpc = _mm_nt(doh, vch)
                dpp = _mm_nt(doh, vph)
                delta = jnp.sum(pc * dpc, axis=1, keepdims=True) + jnp.sum(pp * dpp, axis=1, keepdims=True)
                dsc = pc * (dpc - delta)
                dsp = pp * (dpp - delta)
                dsink = -jnp.sum(ps * delta, axis=0, keepdims=True)
                ds_ref[h:h + 1, :] += jnp.broadcast_to(dsink, (1, 128))
                dq_ref[:, h * 64:(h + 1) * 64] = ((_mm(dsc, kch) + _mm(dsp, kph)) * SCALE).astype(BF16)
                dkc += _mm_tn(dsc, qh) * SCALE
                dkp += _mm_tn(dsp, qh) * SCALE
                dvc += _mm_tn(pc, doh)
                dvp += _mm_tn(pp, doh)
            dkv_ref[pl.ds(cur, BLK), g * 64:(g + 1) * 64] = dkc
            dkv_ref[pl.ds(cur, BLK), 128 + g * 64:192 + g * 64] = dvc
            dkv_ref[pl.ds(prev, BLK), g * 64:(g + 1) * 64] += dkp
            dkv_ref[pl.ds(prev, BLK), 128 + g * 64:192 + g * 64] += dvp

    return pl.pallas_call(
        body, name="attn_bwd", grid=(nb,),
        in_specs=[_rows(BLK, 512), _full((T, 256)), _rows(BLK, 512), pl.BlockSpec(memory_space=pltpu.SMEM)],
        out_specs=[_rows(BLK, 512), _full((T, 256)), _full((8, 128))],
        out_shape=[jax.ShapeDtypeStruct((T, 512), BF16), jax.ShapeDtypeStruct((T, 256), F32),
                   jax.ShapeDtypeStruct((8, 128), F32)],
        compiler_params=_params(),
    )(q, kv, do, sinks)


def _rows8(tm, cols):
    return lax.broadcasted_iota(jnp.int32, (tm, cols), 0) & 7


def _lru_gates(xc, wa, ba, wx, bx, lam):
    r = _sigmoid(_mm(xc, wa) + ba)
    ii = _sigmoid(_mm(xc, wx) + bx)
    sp = _softplus(-lam)
    la = -LRU_C * r * sp
    a = jnp.exp(la)
    m = jnp.sqrt(-_expm1(2.0 * la))
    return r, ii, sp, a, m


def _rnn_fwd(xr, gr, cw, cb, wa, ba, wx, bx, lam):
    T = xr.shape[0]
    tm = 256
    C = D_RNN

    def body(xr_ref, gr_ref, cw_ref, cb_ref, wa_ref, ba_ref, wx_ref, bx_ref, lam_ref,
             xc_ref, h_ref, rec_ref, ext, a_s, b_s, carry):
        i = pl.program_id(0)

        @pl.when(i == 0)
        def _():
            ext[0:8, :] = jnp.zeros((8, C), F32)
            carry[...] = jnp.zeros((8, C), F32)

        ext[8:8 + tm, :] = xr_ref[...]
        xc = cb_ref[...] + cw_ref[3:4, :] * ext[8:8 + tm, :]
        for k in range(3):
            xc = xc + cw_ref[k:k + 1, :] * ext[5 + k:5 + k + tm, :]
        ext[0:8, :] = ext[tm:tm + 8, :]
        xc_ref[...] = xc
        _, ii, _, a, m = _lru_gates(xc, wa_ref[...], ba_ref[...], wx_ref[...], bx_ref[...], lam_ref[...])
        b = m * ii * xc
        r8 = _rows8(tm, C)
        for d in (1, 2, 4):
            ok = r8 >= d
            a_sh = jnp.where(ok, pltpu.roll(a, d, 0), 1.0)
            b_sh = jnp.where(ok, pltpu.roll(b, d, 0), 0.0)
            b = a * b_sh + b
            a = a * a_sh
        a_s[...] = a
        b_s[...] = b

        def step(g, hin):
            s = pl.multiple_of(g * 8, 8)
            hg = a_s[pl.ds(s, 8), :] * hin + b_s[pl.ds(s, 8), :]
            h_ref[pl.ds(s, 8), :] = hg
            return jnp.broadcast_to(hg[7:8, :], (8, C))

        carry[...] = lax.fori_loop(0, tm // 8, step, carry[...])
        ge, _ = _gelu(gr_ref[...])
        rec_ref[...] = (h_ref[...] * ge).astype(BF16)

    vec = _full((1, C))
    return pl.pallas_call(
        body, name="rnn_fwd", grid=(T // tm,),
        in_specs=[_rows(tm, C), _rows(tm, C), _full((4, C)), vec, _full((C, C)), vec, _full((C, C)), vec, vec],
        out_specs=[_rows(tm, C), _rows(tm, C), _rows(tm, C)],
        out_shape=[jax.ShapeDtypeStruct((T, C), F32), jax.ShapeDtypeStruct((T, C), F32),
                   jax.ShapeDtypeStruct((T, C), BF16)],
        scratch_shapes=[pltpu.VMEM((tm + 8, C), F32), pltpu.VMEM((tm, C), F32), pltpu.VMEM((tm, C), F32),
                        pltpu.VMEM((8, C), F32)],
        compiler_params=_params(),
    )(xr, gr, cw, cb, wa, ba, wx, bx, lam)


def _rnn_bwd(drec, gr, h, xc, xr, cw, wa, ba, wx, bx, lam):
    T = xr.shape[0]
    tm = 256
    C = D_RNN
    nt = T // tm
    t8 = tm // 8

    def body(drec_ref, gr_ref, h_ref, hp_ref, xc_ref, xr_ref, xrp_ref, cw_ref, wa_ref, ba_ref, wx_ref, bx_ref,
             lam_ref, dxr_ref, dgr_ref, dwa_ref, dwx_ref, dvec_ref, c_s, g_s, gout, ext, xext, anext, gcarry):
        i = pl.program_id(0)
        j = nt - 1 - i

        @pl.when(i == 0)
        def _():
            dwa_ref[...] = jnp.zeros_like(dwa_ref)
            dwx_ref[...] = jnp.zeros_like(dwx_ref)
            dvec_ref[...] = jnp.zeros_like(dvec_ref)
            anext[...] = jnp.zeros((8, C), F32)
            gcarry[...] = jnp.zeros((8, C), F32)
            ext[tm:tm + 8, :] = jnp.zeros((8, C), F32)

        xc = xc_ref[...]
        lam = lam_ref[...]
        r, ii, sp, a, m = _lru_gates(xc, wa_ref[...], ba_ref[...], wx_ref[...], bx_ref[...], lam)
        ge, dge = _gelu(gr_ref[...])
        drec = drec_ref[...]
        hh = h_ref[...]
        dgr_ref[...] = (drec * hh * dge).astype(BF16)
        dh = drec * ge
        rowi = lax.broadcasted_iota(jnp.int32, (tm, C), 0)
        c = jnp.where(rowi == tm - 1, jnp.broadcast_to(anext[0:1, :], (tm, C)), pltpu.roll(a, tm - 1, 0))
        anext[...] = a[0:8, :]
        r8 = rowi & 7
        gg = dh
        for d in (1, 2, 4):
            ok = r8 < 8 - d
            c_sh = jnp.where(ok, pltpu.roll(c, tm - d, 0), 1.0)
            g_sh = jnp.where(ok, pltpu.roll(gg, tm - d, 0), 0.0)
            gg = c * g_sh + gg
            c = c * c_sh
        c_s[...] = c
        g_s[...] = gg

        def step(k, gin):
            s = pl.multiple_of((t8 - 1 - k) * 8, 8)
            og = c_s[pl.ds(s, 8), :] * gin + g_s[pl.ds(s, 8), :]
            gout[pl.ds(s, 8), :] = og
            return jnp.broadcast_to(og[0:1, :], (8, C))

        gcarry[...] = lax.fori_loop(0, t8, step, gcarry[...])
        G = gout[...]
        hprev_row = jnp.where(j > 0, hp_ref[7:8, :], 0.0)
        hprev = jnp.where(rowi == 0, jnp.broadcast_to(hprev_row, (tm, C)), pltpu.roll(hh, 1, 0))
        da = G * hprev
        dm = G * ii * xc
        di = G * m * xc
        dxc = G * m * ii
        dla = da * a - dm * a * a / m
        dr = dla * (-LRU_C * sp)
        dsp = _colsum(dla * (-LRU_C * r))
        dlam = dsp * (-_sigmoid(-lam))
        dpr = dr * r * (1.0 - r)
        dpi = di * ii * (1.0 - ii)
        dxc = dxc + _mm_nt(dpr, wa_ref[...]) + _mm_nt(dpi, wx_ref[...])
        dwa_ref[...] += _mm_tn(xc, dpr)
        dwx_ref[...] += _mm_tn(xc, dpi)
        dvec_ref[0:1, :] += _colsum(dpr)
        dvec_ref[1:2, :] += _colsum(dpi)
        dvec_ref[2:3, :] += dlam
        dvec_ref[3:4, :] += _colsum(dxc)
        ext[0:tm, :] = dxc
        dxr = cw_ref[3:4, :] * dxc
        for k in range(3):
            dxr = dxr + cw_ref[k:k + 1, :] * ext[3 - k:3 - k + tm, :]
        ext[tm:tm + 8, :] = dxc[0:8, :]
        dxr_ref[...] = dxr.astype(BF16)
        xext[0:8, :] = jnp.where(j > 0, xrp_ref[...], 0.0)
        xext[8:8 + tm, :] = xr_ref[...]
        for k in range(4):
            dvec_ref[4 + k:5 + k, :] += _colsum(dxc * xext[5 + k:5 + k + tm, :])

    rev = lambda i: nt - 1 - i
    prev8 = lambda i: jnp.maximum((nt - 1 - i) * t8 - 1, 0)
    vec = _full((1, C))
    return pl.pallas_call(
        body, name="rnn_bwd", grid=(nt,),
        in_specs=[_rows(tm, C, rev), _rows(tm, C, rev), _rows(tm, C, rev), _rows(8, C, prev8), _rows(tm, C, rev),
                  _rows(tm, C, rev), _rows(8, C, prev8), _full((4, C)), _full((C, C)), vec, _full((C, C)), vec, vec],
        out_specs=[_rows(tm, C, rev), _rows(tm, C, rev), _full((C, C)), _full((C, C)), _full((8, C))],
        out_shape=[jax.ShapeDtypeStruct((T, C), BF16), jax.ShapeDtypeStruct((T, C), BF16),
                   jax.ShapeDtypeStruct((C, C), F32), jax.ShapeDtypeStruct((C, C), F32),
                   jax.ShapeDtypeStruct((8, C), F32)],
        scratch_shapes=[pltpu.VMEM((tm, C), F32), pltpu.VMEM((tm, C), F32), pltpu.VMEM((tm, C), F32),
                        pltpu.VMEM((tm + 8, C), F32), pltpu.VMEM((tm + 8, C), F32), pltpu.VMEM((8, C), F32),
                        pltpu.VMEM((8, C), F32)],
        compiler_params=_params(),
    )(drec, gr, h, h, xc, xr, xr, cw, wa, ba, wx, bx, lam)


def _out_proj(att, rec, x, w_out, g1, b1):
    T = x.shape[0]
    tm = 512

    def body(att_ref, rec_ref, x_ref, w_ref, g1_ref, b1_ref, z_ref, h_ref):
        mix = _mm(att_ref[...], w_ref[0:512, :]) + _mm(rec_ref[...], w_ref[512:1024, :])
        z1 = ALPHA * x_ref[...] + mix
        z_ref[...] = z1
        h1, _, _ = _ln(z1, g1_ref[...], b1_ref[...])
        h_ref[...] = h1.astype(MXU_DTYPE).astype(BF16)

    return pl.pallas_call(
        body, name="out_proj", grid=(T // tm,),
        in_specs=[_rows(tm, 512), _rows(tm, 512), _rows(tm, D), _full((D, D)), _full((1, D)), _full((1, D))],
        out_specs=[_rows(tm, D), _rows(tm, D)],
        out_shape=[jax.ShapeDtypeStruct((T, D), F32), jax.ShapeDtypeStruct((T, D), BF16)],
        compiler_params=_params(),
    )(att, rec, x, w_out, g1, b1)


NC = D_FF // FF_CHUNK


def _ffn_up(h1b, w_up_t, fcw, fcb):
    T = h1b.shape[0]
    tm = 512
    CW = FF_CHUNK

    def body(h_ref, wg_ref, wv_ref, fcw_ref, fcb_ref, gate_ref, val_ref, act_ref, ext):
        i = pl.program_id(1)

        @pl.when(i == 0)
        def _():
            ext[0:8, :] = jnp.zeros((8, CW), F32)

        hb = h_ref[...]
        gate = _mm_nt(hb, wg_ref[...])
        val = _mm_nt(hb, wv_ref[...])
        gate_ref[...] = gate
        val_ref[...] = val
        ext[8:8 + tm, :] = gate
        gc = (fcb_ref[...] + fcw_ref[0:1, :] * ext[6:6 + tm, :] + fcw_ref[1:2, :] * ext[7:7 + tm, :]
              + fcw_ref[2:3, :] * gate)
        ext[0:8, :] = ext[tm:tm + 8, :]
        ge, _ = _gelu(gc)
        act_ref[...] = (ge * val).astype(BF16)

    chunk = pl.BlockSpec((None, tm, CW), lambda c, i: (c, i, 0))
    return pl.pallas_call(
        body, name="ffn_up", grid=(NC, T // tm),
        in_specs=[pl.BlockSpec((tm, D), lambda c, i: (i, 0)), pl.BlockSpec((CW, D), lambda c, i: (c, 0)),
                  pl.BlockSpec((CW, D), lambda c, i: (NC + c, 0)), pl.BlockSpec((None, 3, CW), lambda c, i: (c, 0, 0)),
                  pl.BlockSpec((None, 1, CW), lambda c, i: (c, 0, 0))],
        out_specs=[chunk, chunk, chunk],
        out_shape=[jax.ShapeDtypeStruct((NC, T, CW), F32), jax.ShapeDtypeStruct((NC, T, CW), F32),
                   jax.ShapeDtypeStruct((NC, T, CW), BF16)],
        scratch_shapes=[pltpu.VMEM((tm + 8, CW), F32)],
        compiler_params=_params(),
    )(h1b, w_up_t, w_up_t, fcw, fcb)


def _ffn_down(act, z1, p, tgt, w_down, w_g, w_p_t, g1, b1, g2, b2, bg):
    T = z1.shape[0]
    tm = 256

    def body(act_ref, z_ref, p_ref, t_ref, wdn_hbm, wg_hbm, wp_hbm, g1_ref, b1_ref, g2_ref, b2_ref, bg_ref,
             dz2_ref, dz2b_ref, dpre_ref, dpp_ref, vec_ref, wdn, wg, wp):
        @pl.when(pl.program_id(0) == 0)
        def _():
            pltpu.sync_copy(wdn_hbm, wdn)
            pltpu.sync_copy(wg_hbm, wg)
            pltpu.sync_copy(wp_hbm, wp)
            vec_ref[...] = jnp.zeros_like(vec_ref)

        g2v = g2_ref[...]
        h1, _, _ = _ln(z_ref[...], g1_ref[...], b1_ref[...])
        h1b = h1.astype(MXU_DTYPE)
        ffn = _mm(act_ref[0], wdn[0:FF_CHUNK, :])
        for c in range(1, NC):
            ffn = ffn + _mm(act_ref[c], wdn[c * FF_CHUNK:(c + 1) * FF_CHUNK, :])
        sg = _sigmoid(_mm(h1b, wg[...]) + bg_ref[...])
        pp = _mm_nt(p_ref[...], wp[...])
        z2 = ALPHA * h1 + ffn + sg * pp
        y, xh2, rstd2 = _ln(z2, g2v, b2_ref[...])
        diff = y - t_ref[...]
        dy = diff * (1.0 / D)
        dz2 = _ln_bwd(dy, xh2, rstd2, g2v)
        dpre = dz2 * pp * sg * (1.0 - sg)
        dz2_ref[...] = dz2
        dz2b_ref[...] = dz2.astype(BF16)
        dpre_ref[...] = dpre.astype(BF16)
        dpp_ref[...] = (dz2 * sg).astype(BF16)
        loss = 0.5 * jnp.sum(jnp.sum(diff * diff, axis=1, keepdims=True), axis=0, keepdims=True) * (1.0 / D)
        vec_ref[0:1, :] += jnp.broadcast_to(loss, (1, D))
        vec_ref[1:2, :] += _colsum(dy * xh2)
        vec_ref[2:3, :] += _colsum(dy)
        vec_ref[3:4, :] += _colsum(dpre)

    anyspec = pl.BlockSpec(memory_space=pl.ANY)
    vec = _full((1, D))
    return pl.pallas_call(
        body, name="ffn_down", grid=(T // tm,),
        in_specs=[pl.BlockSpec((NC, tm, FF_CHUNK), lambda i: (0, i, 0)), _rows(tm, D), _rows(tm, PLE), _rows(tm, D),
                  anyspec, anyspec, anyspec] + [vec] * 5,
        out_specs=[_rows(tm, D)] * 4 + [_full((8, D))],
        out_shape=[jax.ShapeDtypeStruct((T, D), F32)] + [jax.ShapeDtypeStruct((T, D), BF16)] * 3
                  + [jax.ShapeDtypeStruct((8, D), F32)],
        scratch_shapes=[pltpu.VMEM((D_FF, D), MXU_DTYPE), pltpu.VMEM((D, D), MXU_DTYPE), pltpu.VMEM((D, PLE), MXU_DTYPE)],
        compiler_params=_params(),
    )(act, z1, p, tgt, w_down, w_g, w_p_t, g1, b1, g2, b2, bg)


def _ffn_bwd(dz2b, gate, val, w_down, fcw, fcb):
    T = dz2b.shape[0]
    tm = 512
    CW = FF_CHUNK
    nt = T // tm
    t8 = tm // 8

    def body(dz_ref, wdn_ref, gate_ref, gp_ref, val_ref, fcw_ref, fcb_ref, dup_ref, dfc_ref, gext, dext):
        i = pl.program_id(1)
        j = nt - 1 - i

        @pl.when(i == 0)
        def _():
            dext[tm:tm + 8, :] = jnp.zeros((8, CW), F32)
            dfc_ref[...] = jnp.zeros_like(dfc_ref)

        gate = gate_ref[...]
        gext[0:8, :] = jnp.where(j > 0, gp_ref[...], 0.0)
        gext[8:8 + tm, :] = gate
        gate1 = gext[7:7 + tm, :]
        gate2 = gext[6:6 + tm, :]
        gc = fcb_ref[...] + fcw_ref[0:1, :] * gate2 + fcw_ref[1:2, :] * gate1 + fcw_ref[2:3, :] * gate
        ge, dge = _gelu(gc)
        dact = _mm_nt(dz_ref[...], wdn_ref[...])
        dgc = dact * val_ref[...] * dge
        dext[0:tm, :] = dgc
        dgate = fcw_ref[2:3, :] * dgc + fcw_ref[1:2, :] * dext[1:1 + tm, :] + fcw_ref[0:1, :] * dext[2:2 + tm, :]
        dext[tm:tm + 8, :] = dgc[0:8, :]
        dup_ref[0] = dgate.astype(BF16)
        dup_ref[1] = (dact * ge).astype(BF16)
        dfc_ref[0:1, :] += _colsum(dgc * gate2)
        dfc_ref[1:2, :] += _colsum(dgc * gate1)
        dfc_ref[2:3, :] += _colsum(dgc * gate)
        dfc_ref[3:4, :] += _colsum(dgc)

    rev = lambda c, i: (c, nt - 1 - i, 0)
    return pl.pallas_call(
        body, name="ffn_bwd", grid=(NC, nt),
        in_specs=[pl.BlockSpec((tm, D), lambda c, i: (nt - 1 - i, 0)), pl.BlockSpec((CW, D), lambda c, i: (c, 0)),
                  pl.BlockSpec((None, tm, CW), rev),
                  pl.BlockSpec((None, 8, CW), lambda c, i: (c, jnp.maximum((nt - 1 - i) * t8 - 1, 0), 0)),
                  pl.BlockSpec((None, tm, CW), rev), pl.BlockSpec((None, 3, CW), lambda c, i: (c, 0, 0)),
                  pl.BlockSpec((None, 1, CW), lambda c, i: (c, 0, 0))],
        out_specs=[pl.BlockSpec((None, 2, tm, CW), lambda c, i: (c, 0, nt - 1 - i, 0)),
                   pl.BlockSpec((None, 8, CW), lambda c, i: (c, 0, 0))],
        out_shape=[jax.ShapeDtypeStruct((NC, 2, T, CW), BF16), jax.ShapeDtypeStruct((NC, 8, CW), F32)],
        scratch_shapes=[pltpu.VMEM((tm + 8, CW), F32), pltpu.VMEM((tm + 8, CW), F32)],
        compiler_params=_params(),
    )(dz2b, w_down, gate, gate, val, fcw, fcb)


def _ffn_dh1(dup, dz2, dpre, z1, w_up_t, w_g, g1, b1):
    T = z1.shape[0]
    tm = 256

    def body(dup_ref, dz2_ref, dpre_ref, z_ref, wup_hbm, wg_hbm, g1_ref, b1_ref, dz1_ref, vec_ref, wup, wg):
        @pl.when(pl.program_id(0) == 0)
        def _():
            pltpu.sync_copy(wup_hbm, wup)
            pltpu.sync_copy(wg_hbm, wg)
            vec_ref[...] = jnp.zeros_like(vec_ref)

        g1v = g1_ref[...]
        _, xh1, rstd1 = _ln(z_ref[...], g1v, b1_ref[...])
        dh1 = ALPHA * dz2_ref[...] + _mm_nt(dpre_ref[...], wg[...])
        for c in range(NC):
            for s in range(2):
                r0 = s * D_FF + c * FF_CHUNK
                dh1 = dh1 + _mm(dup_ref[c, s], wup[r0:r0 + FF_CHUNK, :])
        dz1_ref[...] = _ln_bwd(dh1, xh1, rstd1, g1v)
        vec_ref[0:1, :] += _colsum(dh1 * xh1)
        vec_ref[1:2, :] += _colsum(dh1)

    anyspec = pl.BlockSpec(memory_space=pl.ANY)
    vec = _full((1, D))
    return pl.pallas_call(
        body, name="ffn_dh1", grid=(T // tm,),
        in_specs=[pl.BlockSpec((NC, 2, tm, FF_CHUNK), lambda i: (0, 0, i, 0)), _rows(tm, D), _rows(tm, D), _rows(tm, D),
                  anyspec, anyspec, vec, vec],
        out_specs=[_rows(tm, D), _full((8, D))],
        out_shape=[jax.ShapeDtypeStruct((T, D), F32), jax.ShapeDtypeStruct((8, D), F32)],
        scratch_shapes=[pltpu.VMEM((2 * D_FF, D), MXU_DTYPE), pltpu.VMEM((D, D), MXU_DTYPE)],
        compiler_params=_params(),
    )(dup, dz2, dpre, z1, w_up_t, w_g, g1, b1)


def _out_proj_bwd(dz1, w_out):
    T = dz1.shape[0]
    tm = 512

    def body(dz_ref, w_ref, datt_ref, drec_ref, dzb_ref):
        dzb = dz_ref[...].astype(MXU_DTYPE)
        dzb_ref[...] = dzb.astype(BF16)
        datt_ref[...] = _mm_nt(dzb, w_ref[0:512, :]).astype(BF16)
        drec_ref[...] = _mm_nt(dzb, w_ref[512:1024, :])

    return pl.pallas_call(
        body, name="out_proj_bwd", grid=(T // tm,),
        in_specs=[_rows(tm, D), _full((D, D))],
        out_specs=[_rows(tm, 512), _rows(tm, 512), _rows(tm, D)],
        out_shape=[jax.ShapeDtypeStruct((T, 512), BF16), jax.ShapeDtypeStruct((T, 512), F32),
                   jax.ShapeDtypeStruct((T, D), BF16)],
        compiler_params=_params(),
    )(dz1, w_out)


def _in_proj_bwd(dq, dkv, dxr, dgr, dz1, w_in_t):
    T = dz1.shape[0]
    tm = 512

    def body(dq_ref, dkv_ref, dxr_ref, dgr_ref, dz_ref, w_ref, dx_ref, du_ref):
        dkv = dkv_ref[...].astype(BF16)
        dx_ref[...] = (ALPHA * dz_ref[...] + _mm(dq_ref[...], w_ref[0:512, :]) + _mm(dkv, w_ref[512:768, :])
                       + _mm(dxr_ref[...], w_ref[768:1280, :]) + _mm(dgr_ref[...], w_ref[1280:1792, :]))
        du_ref[:, 0:512] = dq_ref[...]
        du_ref[:, 512:768] = dkv
        du_ref[:, 768:1280] = dxr_ref[...]
        du_ref[:, 1280:1792] = dgr_ref[...]

    return pl.pallas_call(
        body, name="in_proj_bwd", grid=(T // tm,),
        in_specs=[_rows(tm, 512), _rows(tm, 256), _rows(tm, 512), _rows(tm, 512), _rows(tm, D), _full((D_IN, D))],
        out_specs=[_rows(tm, D), _rows(tm, D_IN)],
        out_shape=[jax.ShapeDtypeStruct((T, D), F32), jax.ShapeDtypeStruct((T, D_IN), BF16)],
        compiler_params=_params(),
    )(dq, dkv, dxr, dgr, dz1, w_in_t)


def _weight_grad(a, b, bm, name, out_block=lambda m: m):
    if a.ndim == 3:
        assert a.shape[2] == bm
        T, M = a.shape[1], a.shape[0] * bm
        a_spec = pl.BlockSpec((None, 512, bm), lambda m, k: (m, k, 0))
    else:
        T, M = a.shape
        a_spec = pl.BlockSpec((512, bm), lambda m, k: (k, m))
    N = b.shape[1]
    bt = 512
    nk = T // bt

    def body(a_ref, b_ref, o_ref):
        k = pl.program_id(1)

        @pl.when(k == 0)
        def _():
            o_ref[...] = jnp.zeros_like(o_ref)

        o_ref[...] += _mm_tn(a_ref[...], b_ref[...])

    return pl.pallas_call(
        body, name=name, grid=(M // bm, nk),
        in_specs=[a_spec, pl.BlockSpec((bt, N), lambda m, k: (k, 0))],
        out_specs=pl.BlockSpec((bm, N), lambda m, k: (out_block(m), 0)),
        out_shape=jax.ShapeDtypeStruct((M, N), F32),
        compiler_params=_params(),
    )(a, b)


def _adamw(w, g, m, v, name):
    R, C = w.shape
    tr = R // 8 if R % 64 == 0 else R
    c1 = 1.0 / (1.0 - ADAM_B1 ** ADAM_STEP)
    c2 = 1.0 / (1.0 - ADAM_B2 ** ADAM_STEP)

    def body(w_ref, g_ref, m_ref, v_ref, d_ref, nm_ref, nv_ref):
        g = g_ref[...]
        nm = ADAM_B1 * m_ref[...] + (1.0 - ADAM_B1) * g
        nv = ADAM_B2 * v_ref[...] + (1.0 - ADAM_B2) * g * g
        nm_ref[...] = nm
        nv_ref[...] = nv
        d_ref[...] = -ADAM_LR * ((nm * c1) / (jnp.sqrt(nv * c2) + ADAM_EPS) + ADAM_WD * w_ref[...])

    spec = pl.BlockSpec((tr, C), lambda i: (i, 0))
    return pl.pallas_call(
        body, name=name, grid=(R // tr,),
        in_specs=[spec] * 4, out_specs=[spec] * 3,
        out_shape=[jax.ShapeDtypeStruct((R, C), F32)] * 3,
        compiler_params=_params(),
    )(w, g, m, v)


def _adamw_halves(w, mine, sib, m, v, c):
    tr = 416
    nb = HALF // tr
    c1 = 1.0 / (1.0 - ADAM_B1 ** ADAM_STEP)
    c2 = 1.0 / (1.0 - ADAM_B2 ** ADAM_STEP)

    def body(c_ref, w_ref, a_ref, b_ref, m_ref, v_ref, g_ref, d_ref, nm_ref, nv_ref):
        own = (pl.program_id(0) // nb) == c_ref[0]
        g = jnp.where(own, a_ref[...], b_ref[...])
        nm = ADAM_B1 * m_ref[...] + (1.0 - ADAM_B1) * g
        nv = ADAM_B2 * v_ref[...] + (1.0 - ADAM_B2) * g * g
        g_ref[...] = g
        nm_ref[...] = nm
        nv_ref[...] = nv
        d_ref[...] = -ADAM_LR * ((nm * c1) / (jnp.sqrt(nv * c2) + ADAM_EPS) + ADAM_WD * w_ref[...])

    full = pl.BlockSpec((tr, 1024), lambda i, c_ref: (i, 0))
    half = pl.BlockSpec((tr, 1024), lambda i, c_ref: (i % nb, 0))
    grid_spec = pltpu.PrefetchScalarGridSpec(num_scalar_prefetch=1, grid=(2 * nb,),
                                             in_specs=[full, half, half, full, full], out_specs=[full] * 4)
    return pl.pallas_call(body, name="adamw_big", grid_spec=grid_spec,
                          out_shape=[jax.ShapeDtypeStruct((PACK_TOTAL, 1024), F32)] * 4,
                          compiler_params=_params())(c, w, mine, sib, m, v)


def _add4(a, name):
    _, R, C = a.shape
    tr = 416

    def body(a_ref, o_ref):
        o_ref[...] = ((a_ref[0].astype(F32) + a_ref[1].astype(F32)) + a_ref[2].astype(F32)) + a_ref[3].astype(F32)

    return pl.pallas_call(body, name=name, grid=(R // tr,),
                          in_specs=[pl.BlockSpec((4, tr, C), lambda i: (0, i, 0))],
                          out_specs=pl.BlockSpec((tr, C), lambda i: (i, 0)),
                          out_shape=jax.ShapeDtypeStruct((R, C), F32), compiler_params=_params())(a)


def _pos():
    return lax.axis_index("x"), lax.axis_index("y"), lax.axis_index("c")


def _other_chips(x, y):
    return [(1 - x, y), (x, 1 - y), (1 - x, 1 - y)]


def _gather_weights(wpack, cpack):
    def body(w_ref, c_ref, gw_ref, gc_ref, send_sems, recv_sems, local_sems):
        x, y, c = _pos()
        me = 2 * x + y
        chips = _other_chips(x, y)
        mine = pl.ds(pl.multiple_of(c * HALF, 16), HALF)
        theirs = pl.ds(pl.multiple_of((1 - c) * HALF, 16), HALF)
        loc = [pltpu.make_async_copy(w_ref, gw_ref.at[me], local_sems.at[0]),
               pltpu.make_async_copy(c_ref, gc_ref.at[me], local_sems.at[1])]
        for cp in loc:
            cp.start()

        def copy(k, src, dst, to):
            return pltpu.make_async_remote_copy(src_ref=src, dst_ref=dst, send_sem=send_sems.at[k], recv_sem=recv_sems.at[k],
                                                device_id=to, device_id_type=MESH)

        sends = []
        for k, (px, py) in enumerate(chips):
            sends.append(copy(k, w_ref.at[mine], gw_ref.at[me, mine], (px, py, c)))
            sends.append(copy(3 + k, c_ref, gc_ref.at[me], (px, py, c)))
        for cp in sends:
            cp.start()
        for k, (px, py) in enumerate(chips):
            j = 2 * px + py
            copy(k, w_ref.at[mine], gw_ref.at[j, mine], (px, py, c)).wait_recv()
            fwd = copy(6 + k, gw_ref.at[j, mine], gw_ref.at[j, mine], (x, y, 1 - c))
            fwd.start()
            sends.append(fwd)
        for k, (px, py) in enumerate(chips):
            j = 2 * px + py
            copy(3 + k, c_ref, gc_ref.at[j], (px, py, c)).wait_recv()
            copy(6 + k, gw_ref.at[j, theirs], gw_ref.at[j, theirs], (x, y, 1 - c)).wait_recv()
        for cp in sends:
            cp.wait_send()
        for cp in loc:
            cp.wait()

    anyspec = pl.BlockSpec(memory_space=pl.ANY)
    return pl.pallas_call(
        body, name="gather_weights",
        in_specs=[anyspec, anyspec], out_specs=[anyspec, anyspec],
        out_shape=[jax.ShapeDtypeStruct((4,) + wpack.shape, wpack.dtype), jax.ShapeDtypeStruct((4,) + cpack.shape, cpack.dtype)],
        scratch_shapes=[pltpu.SemaphoreType.DMA((9,)), pltpu.SemaphoreType.DMA((9,)), pltpu.SemaphoreType.DMA((2,))],
        compiler_params=_params(has_side_effects=True),
    )(wpack, cpack)


def _allreduce_small(s):
    R = s.shape[0]

    def body(s_ref, o_ref, buf, send_sems, recv_sems):
        x, y, c = _pos()
        me = 4 * x + 2 * y + c
        buf[me] = s_ref[...]
        sends = []
        for k in range(1, 8):
            peer = (x ^ (k >> 2), y ^ ((k >> 1) & 1), c ^ (k & 1))
            cp = pltpu.make_async_remote_copy(src_ref=s_ref, dst_ref=buf.at[me], send_sem=send_sems.at[k - 1],
                                              recv_sem=recv_sems.at[k - 1], device_id=peer, device_id_type=MESH)
            cp.start()
            sends.append(cp)
        for k in range(1, 8):
            px, py, pc = x ^ (k >> 2), y ^ ((k >> 1) & 1), c ^ (k & 1)
            pltpu.make_async_remote_copy(src_ref=s_ref, dst_ref=buf.at[4 * px + 2 * py + pc], send_sem=send_sems.at[k - 1],
                                         recv_sem=recv_sems.at[k - 1], device_id=(px, py, pc),
                                         device_id_type=MESH).wait_recv()
        for cp in sends:
            cp.wait_send()
        acc = buf[0]
        for d in range(1, 8):
            acc = acc + buf[d]
        o_ref[...] = acc

    vm = pl.BlockSpec(memory_space=pltpu.VMEM)
    return pl.pallas_call(
        body, name="allreduce_small", in_specs=[vm], out_specs=vm,
        out_shape=jax.ShapeDtypeStruct((R, 128), F32),
        scratch_shapes=[pltpu.VMEM((8, R, 128), F32), pltpu.SemaphoreType.DMA((7,)), pltpu.SemaphoreType.DMA((7,))],
        compiler_params=_params(has_side_effects=True),
    )(s)


def _swap_halves(g):
    def body(g_ref, o_ref, send_sem, recv_sem):
        x, y, c = _pos()
        start = pl.multiple_of((1 - c) * HALF, 8)
        cp = pltpu.make_async_remote_copy(src_ref=g_ref.at[:, pl.ds(start, HALF), :], dst_ref=o_ref, send_sem=send_sem,
                                          recv_sem=recv_sem, device_id=(x, y, 1 - c), device_id_type=MESH)
        cp.start()
        cp.wait()

    anyspec = pl.BlockSpec(memory_space=pl.ANY)
    return pl.pallas_call(
        body, name="swap_halves", in_specs=[anyspec], out_specs=anyspec,
        out_shape=jax.ShapeDtypeStruct((4, HALF, 1024), F32),
        scratch_shapes=[pltpu.SemaphoreType.DMA, pltpu.SemaphoreType.DMA],
        compiler_params=_params(has_side_effects=True),
    )(g)


def _scatter_chips(s):
    def body(s_ref, o_ref, send_sems, recv_sems, local_sem):
        x, y, c = _pos()
        me = 2 * x + y
        loc = pltpu.make_async_copy(s_ref.at[me], o_ref.at[me], local_sem)
        loc.start()
        sends = []
        for k, (px, py) in enumerate(_other_chips(x, y)):
            cp = pltpu.make_async_remote_copy(src_ref=s_ref.at[2 * px + py], dst_ref=o_ref.at[me], send_sem=send_sems.at[k],
                                              recv_sem=recv_sems.at[k], device_id=(px, py, c), device_id_type=MESH)
            cp.start()
            sends.append(cp)
        for k, (px, py) in enumerate(_other_chips(x, y)):
            pltpu.make_async_remote_copy(src_ref=s_ref.at[me], dst_ref=o_ref.at[2 * px + py], send_sem=send_sems.at[k],
                                         recv_sem=recv_sems.at[k], device_id=(px, py, c), device_id_type=MESH).wait_recv()
        for cp in sends:
            cp.wait_send()
        loc.wait()

    anyspec = pl.BlockSpec(memory_space=pl.ANY)
    return pl.pallas_call(
        body, name="scatter_chips", in_specs=[anyspec], out_specs=anyspec,
        out_shape=jax.ShapeDtypeStruct((4, HALF, 1024), s.dtype),
        scratch_shapes=[pltpu.SemaphoreType.DMA((3,)), pltpu.SemaphoreType.DMA((3,)), pltpu.SemaphoreType.DMA],
        compiler_params=_params(has_side_effects=True),
    )(s)


def _send_half(r):
    def body(r_ref, o_ref, send_sem, recv_sem):
        x, y, c = _pos()
        cp = pltpu.make_async_remote_copy(src_ref=r_ref, dst_ref=o_ref, send_sem=send_sem, recv_sem=recv_sem,
                                          device_id=(x, y, 1 - c), device_id_type=MESH)
        cp.start()
        cp.wait()

    anyspec = pl.BlockSpec(memory_space=pl.ANY)
    return pl.pallas_call(
        body, name="send_half", in_specs=[anyspec], out_specs=anyspec,
        out_shape=jax.ShapeDtypeStruct((HALF, 1024), F32),
        scratch_shapes=[pltpu.SemaphoreType.DMA, pltpu.SemaphoreType.DMA],
        compiler_params=_params(has_side_effects=True),
    )(r)


def _add_half(g, r, c):
    tr = 416
    nb = HALF // tr

    def body(c_ref, g_ref, r_ref, o_ref):
        o_ref[...] = (g_ref[...] + r_ref[...]).astype(BF16)

    grid_spec = pltpu.PrefetchScalarGridSpec(
        num_scalar_prefetch=1, grid=(4, nb),
        in_specs=[pl.BlockSpec((1, tr, 1024), lambda j, i, c_ref: (j, c_ref[0] * nb + i, 0)),
                  pl.BlockSpec((1, tr, 1024), lambda j, i, c_ref: (j, i, 0))],
        out_specs=pl.BlockSpec((1, tr, 1024), lambda j, i, c_ref: (j, i, 0)))
    return pl.pallas_call(body, name="add_half", grid_spec=grid_spec,
                          out_shape=jax.ShapeDtypeStruct((4, HALF, 1024), BF16), compiler_params=_params())(c, g, r)


def _block_diag(w):
    eye = jnp.eye(RNN_BLOCKS, dtype=w.dtype)
    return (eye[:, None, :, None] * w[:, :, None, :]).reshape(D_RNN, D_RNN)


def _diag_blocks(wd):
    d = wd.reshape(RNN_BLOCKS, 64, RNN_BLOCKS, 64)
    return jnp.stack([d[h, :, h, :] for h in range(RNN_BLOCKS)])


def _layer_grads(x, p, tgt, gw, small):
    row = lambda v: v.reshape(1, -1)
    wa = _block_diag(small["gate_a_w"]).astype(MXU_DTYPE)
    wx = _block_diag(small["gate_x_w"]).astype(MXU_DTYPE)
    sinks = small["attn_sinks"].reshape(1, HEADS)

    q, kv, xr, gr, xb = _in_proj(x, gw["w_in_t"])
    att = _attn_fwd(q, kv, sinks)
    xc, h, rec = _rnn_fwd(xr, gr, small["rnn_conv_w"], row(small["rnn_conv_b"]), wa, row(small["gate_a_b"]),
                          wx, row(small["gate_x_b"]), row(small["lru_lambda"]))
    g1, b1 = row(small["ln1_g"]), row(small["ln1_b"])
    fcw = small["ffn_conv_w"].reshape(3, NC, FF_CHUNK).transpose(1, 0, 2)
    fcb = small["ffn_conv_b"].reshape(NC, 1, FF_CHUNK)
    z1, h1b = _out_proj(att, rec, x, gw["w_out"], g1, b1)
    gate, val, act = _ffn_up(h1b, gw["w_up_t"], fcw, fcb)
    dz2, dz2b, dpre, dpp, vec2 = _ffn_down(act, z1, p, tgt, gw["w_down"], gw["w_g"], gw["w_p_t"], g1, b1,
                                           row(small["ln2_g"]), row(small["ln2_b"]), row(small["ple_gate_b"]))
    dup, dfc = _ffn_bwd(dz2b, gate, val, gw["w_down"], fcw, fcb)
    dz1, vec1 = _ffn_dh1(dup, dz2, dpre, z1, gw["w_up_t"], gw["w_g"], g1, b1)
    datt, drec, dz1b = _out_proj_bwd(dz1, gw["w_out"])
    dxr, dgr, dwa, dwx, dvec = _rnn_bwd(drec, gr, h, xc, xr, small["rnn_conv_w"], wa, row(small["gate_a_b"]),
                                        wx, row(small["gate_x_b"]), row(small["lru_lambda"]))
    dq, dkv, dsinks = _attn_bwd(q, kv, datt, sinks)
    grad_x, du = _in_proj_bwd(dq, dkv, dxr, dgr, dz1, gw["w_in_t"])

    mix = jnp.concatenate([att, rec], axis=1)
    pb = p.astype(BF16)
    big = {
        "w_in_t": _weight_grad(du, xb, 256, "dw_in"),
        "w_out": _weight_grad(mix, dz1b, 512, "dw_out"),
        "w_up_t": _weight_grad(dup.reshape(2 * NC, -1, FF_CHUNK), h1b, FF_CHUNK, "dw_up",
                               out_block=lambda m: (m % 2) * NC + m // 2),
        "w_down": _weight_grad(act, dz2b, 512, "dw_down"),
        "w_g": _weight_grad(h1b, dpre, 512, "dw_gate"),
        "w_p_t": _weight_grad(dpp, pb, 512, "dw_proj"),
    }
    sg = {
        "attn_sinks": dsinks[:, 0],
        "rnn_conv_w": dvec[4:8],
        "rnn_conv_b": dvec[3],
        "gate_a_w": _diag_blocks(dwa),
        "gate_a_b": dvec[0],
        "gate_x_w": _diag_blocks(dwx),
        "gate_x_b": dvec[1],
        "lru_lambda": dvec[2],
        "ln1_g": vec1[0],
        "ln1_b": vec1[1],
        "ffn_conv_w": dfc[:, 0:3].transpose(1, 0, 2).reshape(3, D_FF),
        "ffn_conv_b": dfc[:, 3].reshape(D_FF),
        "ple_gate_b": vec2[3],
        "ln2_g": vec2[1],
        "ln2_b": vec2[2],
    }
    return grad_x, big, sg, vec2[0, 0:1]


BIG = ("w_in", "w_out", "w_ffn_up", "w_ffn_down", "ple_gate_w", "ple_proj")
BIG_KEYS = ("w_in_t", "w_out", "w_up_t", "w_down", "w_g", "w_p_t")
BIG_T = (True, False, True, False, False, True)
SMALL = ("attn_sinks", "rnn_conv_w", "rnn_conv_b", "gate_a_w", "gate_a_b", "gate_x_w", "gate_x_b", "lru_lambda",
         "ln1_g", "ln1_b", "ffn_conv_w", "ffn_conv_b", "ple_gate_b", "ln2_g", "ln2_b")
SHARDED_SMALL = ("rnn_conv_w", "ffn_conv_w")
WEIGHTS = ("w_in", "attn_sinks", "rnn_conv_w", "rnn_conv_b", "gate_a_w", "gate_a_b", "gate_x_w", "gate_x_b",
           "lru_lambda", "w_out", "ln1_g", "ln1_b", "w_ffn_up", "ffn_conv_w", "ffn_conv_b", "w_ffn_down",
           "ple_gate_w", "ple_gate_b", "ple_proj", "ln2_g", "ln2_b")


def _pack_big(d):
    parts = []
    for name, t in zip(BIG, BIG_T):
        a = d[name]
        a = a.T if t else a
        parts.append(a.reshape(-1, 1024))
    return jnp.concatenate(parts, axis=0)


def _unpack_big(a):
    out = {}
    shapes = {"w_in": (448, 1024), "w_out": (256, 1024), "w_ffn_up": (1536, 1024), "w_ffn_down": (768, 1024),
              "ple_gate_w": (256, 1024), "ple_proj": (256, 256)}
    for i, (name, t) in enumerate(zip(BIG, BIG_T)):
        s = a[PACK_OFF[i]:PACK_OFF[i + 1]].reshape(shapes[name])
        out[name] = (s.T if t else s)[None]
    return out


def _pack_vecs(items):
    parts, offs, n = [], [], 0
    for a in items:
        f = a.reshape(-1).astype(F32)
        pad = (-f.shape[0]) % 128
        parts.append(jnp.pad(f, (0, pad)))
        offs.append(n)
        n += (f.shape[0] + pad) // 128
    padr = (-n) % 8
    if padr:
        parts.append(jnp.zeros((padr * 128,), F32))
    return jnp.concatenate(parts).reshape(-1, 128), offs


def _unpack_vecs(a, offs, shapes):
    flat = a.reshape(-1)
    out = []
    for o, s in zip(offs, shapes):
        n = 1
        for d in s:
            n *= d
        out.append(flat[o * 128:o * 128 + n].reshape(s))
    return out


def kernel(x, p, w_in, attn_sinks, rnn_conv_w, rnn_conv_b, gate_a_w, gate_a_b, gate_x_w, gate_x_b, lru_lambda, w_out, ln1_g, ln1_b, w_ffn_up, ffn_conv_w, ffn_conv_b, w_ffn_down, ple_gate_w, ple_gate_b, ple_proj, ln2_g, ln2_b, loss_target, m_w_in, m_attn_sinks, m_rnn_conv_w, m_rnn_conv_b, m_gate_a_w, m_gate_a_b, m_gate_x_w, m_gate_x_b, m_lru_lambda, m_w_out, m_ln1_g, m_ln1_b, m_w_ffn_up, m_ffn_conv_w, m_ffn_conv_b, m_w_ffn_down, m_ple_gate_w, m_ple_gate_b, m_ple_proj, m_ln2_g, m_ln2_b, v_w_in, v_attn_sinks, v_rnn_conv_w, v_rnn_conv_b, v_gate_a_w, v_gate_a_b, v_gate_x_w, v_gate_x_b, v_lru_lambda, v_w_out, v_ln1_g, v_ln1_b, v_w_ffn_up, v_ffn_conv_w, v_ffn_conv_b, v_w_ffn_down, v_ple_gate_w, v_ple_gate_b, v_ple_proj, v_ln2_g, v_ln2_b):
    w = dict(w_in=w_in, attn_sinks=attn_sinks, rnn_conv_w=rnn_conv_w, rnn_conv_b=rnn_conv_b, gate_a_w=gate_a_w,
             gate_a_b=gate_a_b, gate_x_w=gate_x_w, gate_x_b=gate_x_b, lru_lambda=lru_lambda, w_out=w_out, ln1_g=ln1_g,
             ln1_b=ln1_b, w_ffn_up=w_ffn_up, ffn_conv_w=ffn_conv_w, ffn_conv_b=ffn_conv_b, w_ffn_down=w_ffn_down,
             ple_gate_w=ple_gate_w, ple_gate_b=ple_gate_b, ple_proj=ple_proj, ln2_g=ln2_g, ln2_b=ln2_b)
    m = dict(w_in=m_w_in, attn_sinks=m_attn_sinks, rnn_conv_w=m_rnn_conv_w, rnn_conv_b=m_rnn_conv_b, gate_a_w=m_gate_a_w,
             gate_a_b=m_gate_a_b, gate_x_w=m_gate_x_w, gate_x_b=m_gate_x_b, lru_lambda=m_lru_lambda, w_out=m_w_out,
             ln1_g=m_ln1_g, ln1_b=m_ln1_b, w_ffn_up=m_w_ffn_up, ffn_conv_w=m_ffn_conv_w, ffn_conv_b=m_ffn_conv_b,
             w_ffn_down=m_w_ffn_down, ple_gate_w=m_ple_gate_w, ple_gate_b=m_ple_gate_b, ple_proj=m_ple_proj,
             ln2_g=m_ln2_g, ln2_b=m_ln2_b)
    v = dict(w_in=v_w_in, attn_sinks=v_attn_sinks, rnn_conv_w=v_rnn_conv_w, rnn_conv_b=v_rnn_conv_b, gate_a_w=v_gate_a_w,
             gate_a_b=v_gate_a_b, gate_x_w=v_gate_x_w, gate_x_b=v_gate_x_b, lru_lambda=v_lru_lambda, w_out=v_w_out,
             ln1_g=v_ln1_g, ln1_b=v_ln1_b, w_ffn_up=v_w_ffn_up, ffn_conv_w=v_ffn_conv_w, ffn_conv_b=v_ffn_conv_b,
             w_ffn_down=v_w_ffn_down, ple_gate_w=v_ple_gate_w, ple_gate_b=v_ple_gate_b, ple_proj=v_ple_proj,
             ln2_g=v_ln2_g, ln2_b=v_ln2_b)
    w, m, v = ({k: a[0] for k, a in d.items()} for d in (w, m, v))
    chip = 2 * lax.axis_index("x") + lax.axis_index("y")
    core = lax.axis_index("c")

    wpack = _pack_big(w)
    cpack, _ = _pack_vecs([w["rnn_conv_w"], w["ffn_conv_w"]])
    gwp, gcp = _gather_weights(wpack.astype(MXU_DTYPE), cpack)
    gw = {}
    for i, key in enumerate(BIG_KEYS):
        a = gwp[:, PACK_OFF[i]:PACK_OFF[i + 1]]
        gw[key] = a.reshape(4 * 256, 256) if key == "w_p_t" else a.reshape(-1, 1024)
    small = {k: w[k] for k in SMALL}
    small["rnn_conv_w"] = gcp[:, 0:4].reshape(4, 4, 128).transpose(1, 0, 2).reshape(4, 512)
    small["ffn_conv_w"] = gcp[:, 4:22].reshape(4, 3, 768).transpose(1, 0, 2).reshape(3, 3072)

    grad_x, big, sg, loss = _layer_grads(x[0], p[0, 0], loss_target[0], gw, small)

    spack, offs = _pack_vecs([sg[k] for k in SMALL] + [loss])
    ssum = _allreduce_small(spack)
    shapes = [sg[k].shape for k in SMALL] + [(1,)]
    red = dict(zip(SMALL + ("loss",), _unpack_vecs(ssum, offs, shapes)))
    red["rnn_conv_w"] = lax.dynamic_slice_in_dim(red["rnn_conv_w"], chip * 128, 128, axis=1)
    red["ffn_conv_w"] = lax.dynamic_slice_in_dim(red["ffn_conv_w"], chip * 768, 768, axis=1)

    parts = []
    for i, key in enumerate(BIG_KEYS):
        parts.append(big[key].reshape(4, PACK_ROWS[i], 1024))
    gpack = jnp.concatenate(parts, axis=1)
    core1 = core.reshape(1).astype(jnp.int32)
    sib = _swap_halves(gpack)
    chip_sum = _add_half(gpack, sib, core1)
    from_chips = _scatter_chips(chip_sum)
    half = _add4(from_chips, "add_chips")
    other_half = _send_half(half)

    gbig, dbig, mbig, vbig = _adamw_halves(wpack, half, other_half, _pack_big(m), _pack_big(v), core1)
    wsm, offs2 = _pack_vecs([w[k] for k in SMALL])
    gsm, _ = _pack_vecs([red[k] for k in SMALL])
    msm, _ = _pack_vecs([m[k] for k in SMALL])
    vsm, _ = _pack_vecs([v[k] for k in SMALL])
    dsm, nmsm, nvsm = _adamw(wsm, gsm, msm, vsm, "adamw_small")
    shapes2 = [w[k].shape for k in SMALL]

    def named(bigp, smallp):
        d = _unpack_big(bigp)
        d.update({k: a[None] for k, a in zip(SMALL, _unpack_vecs(smallp, offs2, shapes2))})
        return [d[k] for k in WEIGHTS]

    grads = named(gbig, gsm)
    return (red["loss"].reshape(()), grad_x[None], *grads, *named(dbig, dsm), *named(mbig, nmsm), *named(vbig, nvsm))
```

```python
import functools

import jax
import jax.numpy as jnp
from jax import lax
from jax.experimental import pallas as pl
from jax.experimental.pallas import tpu as pltpu

F32 = jnp.float32
BF16 = jnp.bfloat16
MXU_DTYPE = jnp.bfloat16

D = 1024
D_ATT = 512
D_KV = 128
D_RNN = 512
D_IN = 1792
D_FF = 3072
FF_CHUNK = 512
PLE = 256
HEADS = 8
HEAD_DIM = 64
BLK = 128
RNN_BLOCKS = 8
LN_EPS = 1e-5
LRU_C = 8.0
ALPHA = float(2.0 ** 0.25)
SCALE = HEAD_DIM ** -0.5
NEG = -1e30

ADAM_LR = 0.001
ADAM_B1 = 0.9
ADAM_B2 = 0.999
ADAM_EPS = 1e-08
ADAM_WD = 0.01
ADAM_STEP = 10

VMEM_LIMIT_BYTES = 56 * 1024 * 1024
MESH = pl.DeviceIdType.MESH

PACK_ROWS = (448, 256, 1536, 768, 256, 64)
PACK_OFF = tuple(sum(PACK_ROWS[:i]) for i in range(len(PACK_ROWS) + 1))
PACK_TOTAL = PACK_OFF[-1]
HALF = PACK_TOTAL // 2


def _params(**kw):
    return pltpu.CompilerParams(vmem_limit_bytes=VMEM_LIMIT_BYTES, **kw)


def _mm(a, b):
    return jnp.dot(a.astype(MXU_DTYPE), b.astype(MXU_DTYPE), preferred_element_type=F32)


def _mm_nt(a, b):
    return lax.dot_general(a.astype(MXU_DTYPE), b.astype(MXU_DTYPE), (((1,), (1,)), ((), ())),
                           preferred_element_type=F32)


def _mm_tn(a, b):
    return lax.dot_general(a.astype(MXU_DTYPE), b.astype(MXU_DTYPE), (((0,), (0,)), ((), ())),
                           preferred_element_type=F32)


def _sigmoid(x):
    return 1.0 / (1.0 + jnp.exp(-x))


def _gelu(x):
    c = 0.7978845608028654
    k = 0.044715
    t = jnp.tanh(c * (x + k * x * x * x))
    g = 0.5 * x * (1.0 + t)
    dg = 0.5 * (1.0 + t) + 0.5 * x * (1.0 - t * t) * c * (1.0 + 3.0 * k * x * x)
    return g, dg


def _expm1(x):
    poly = x * (1.0 + x * (0.5 + x * (1.0 / 6.0 + x * (1.0 / 24.0 + x * (1.0 / 120.0)))))
    return jnp.where(jnp.abs(x) < 0.03, poly, jnp.exp(x) - 1.0)


def _softplus(x):
    return jnp.maximum(x, 0.0) + jnp.log(1.0 + jnp.exp(-jnp.abs(x)))


def _ln(z, g, b):
    mu = jnp.mean(z, axis=-1, keepdims=True)
    zc = z - mu
    var = jnp.mean(zc * zc, axis=-1, keepdims=True)
    rstd = lax.rsqrt(var + LN_EPS)
    xhat = zc * rstd
    return xhat * g + b, xhat, rstd


def _ln_bwd(dy, xhat, rstd, g):
    dxh = dy * g
    m1 = jnp.mean(dxh, axis=-1, keepdims=True)
    m2 = jnp.mean(dxh * xhat, axis=-1, keepdims=True)
    return rstd * (dxh - m1 - xhat * m2)


def _colsum(x):
    return jnp.sum(x, axis=0, keepdims=True)


def _full(shape):
    nd = len(shape)
    return pl.BlockSpec(shape, lambda *_: (0,) * nd)


def _rows(tm, cols, fn=None):
    if fn is None:
        return pl.BlockSpec((tm, cols), lambda i: (i, 0))
    return pl.BlockSpec((tm, cols), lambda i: (fn(i), 0))


def _heads(tm):
    return pl.BlockSpec((HEADS, tm, HEAD_DIM), lambda i: (0, i, 0))


def _in_proj(x, w_in_t):
    T = x.shape[0]
    tm = 512

    def body(x_ref, w_ref, q_ref, kv_ref, xr_ref, gr_ref, xb_ref):
        xb = x_ref[...].astype(MXU_DTYPE)
        xb_ref[...] = xb.astype(BF16)
        q = _mm_nt(xb, w_ref[0:512, :])
        for h in range(HEADS):
            q_ref[h] = q[:, h * 64:(h + 1) * 64].astype(BF16)
        kv_ref[...] = _mm_nt(xb, w_ref[512:768, :]).astype(BF16)
        xr_ref[...] = _mm_nt(xb, w_ref[768:1280, :])
        gr_ref[...] = _mm_nt(xb, w_ref[1280:1792, :])

    return pl.pallas_call(
        body, name="in_proj", grid=(T // tm,),
        in_specs=[_rows(tm, D), _full((D_IN, D))],
        out_specs=[_heads(tm), _rows(tm, 256), _rows(tm, 512), _rows(tm, 512), _rows(tm, D)],
        out_shape=[jax.ShapeDtypeStruct((HEADS, T, 64), BF16), jax.ShapeDtypeStruct((T, 256), BF16),
                   jax.ShapeDtypeStruct((T, 512), F32), jax.ShapeDtypeStruct((T, 512), F32),
                   jax.ShapeDtypeStruct((T, D), BF16)],
        compiler_params=_params(),
    )(x, w_in_t)


def _attn_band(kv_ref, i):
    cur = pl.multiple_of(i * BLK, BLK)
    prev = pl.multiple_of(jnp.maximum(i - 1, 0) * BLK, BLK)
    band = jnp.concatenate([kv_ref[pl.ds(prev, BLK), :], kv_ref[pl.ds(cur, BLK), :]], axis=0)
    key = lax.broadcasted_iota(jnp.int32, (2 * BLK, 4 * BLK), 0)
    qry = lax.broadcasted_iota(jnp.int32, (2 * BLK, 4 * BLK), 1) & (BLK - 1)
    in_prev = jnp.logical_and(jnp.logical_and(key < BLK, key > qry), i > 0)
    mask = jnp.logical_or(in_prev, jnp.logical_and(key >= BLK, key - BLK <= qry))
    return band, mask, cur, prev


def _attn_scores(band, mask, qs, s_ref, g):
    st = jnp.where(mask, _mm_nt(band[:, g * 64:(g + 1) * 64], qs) * SCALE, NEG)
    lane = lax.broadcasted_iota(jnp.int32, (1, 4 * BLK), 1)
    sv = jnp.where(lane < BLK, s_ref[0, 4 * g],
                   jnp.where(lane < 2 * BLK, s_ref[0, 4 * g + 1], jnp.where(lane < 3 * BLK, s_ref[0, 4 * g + 2], s_ref[0, 4 * g + 3])))
    m = jnp.maximum(jnp.max(st, axis=0, keepdims=True), sv)
    p = jnp.exp(st - m)
    ps = jnp.exp(sv - m)
    return p, ps, jnp.sum(p, axis=0, keepdims=True) + ps


def _attn_fwd(q, kv, sinks):
    T = kv.shape[0]

    def body(q_ref, kv_ref, s_ref, o_ref):
        i = pl.program_id(0)
        band, mask, _, _ = _attn_band(kv_ref, i)
        for g in range(2):
            qs = q_ref[4 * g:4 * g + 4].reshape(4 * BLK, HEAD_DIM)
            p, _, den = _attn_scores(band, mask, qs, s_ref, g)
            ot = _mm_tn(band[:, 128:256], p) / den
            for hh in range(4):
                o = ot[:, hh * BLK:(hh + 1) * BLK].T
                o_ref[:, (4 * g + hh) * 64:(4 * g + hh + 1) * 64] = o[:, g * 64:(g + 1) * 64].astype(BF16)

    return pl.pallas_call(
        body, name="attn_fwd", grid=(T // BLK,),
        in_specs=[_heads(BLK), _full((T, 256)), pl.BlockSpec(memory_space=pltpu.SMEM)],
        out_specs=_rows(BLK, 512),
        out_shape=jax.ShapeDtypeStruct((T, 512), BF16),
        compiler_params=_params(),
    )(q, kv, sinks)


def _attn_bwd(q, kv, do, sinks):
    T = kv.shape[0]

    def body(q_ref, kv_ref, do_ref, s_ref, dq_ref, dkv_ref, ds_ref):
        i = pl.program_id(0)
        band, mask, cur, prev = _attn_band(kv_ref, i)

        @pl.when(i == 0)
        def _():
            ds_ref[...] = jnp.zeros_like(ds_ref)

        for g in range(2):
            qs = q_ref[4 * g:4 * g + 4].reshape(4 * BLK, HEAD_DIM)
            dos = do_ref[4 * g:4 * g + 4].reshape(4 * BLK, HEAD_DIM)
            p, ps, den = _attn_scores(band, mask, qs, s_ref, g)
            inv = 1.0 / den
            p = p * inv
            dpt = _mm_nt(band[:, 128 + g * 64:192 + g * 64], dos)
            delta = jnp.sum(p * dpt, axis=0, keepdims=True)
            dst = p * (dpt - delta)
            dsv = -(ps * inv) * delta
            for hh in range(4):
                dsink = jnp.sum(dsv[:, hh * BLK:(hh + 1) * BLK], axis=1, keepdims=True)
                ds_ref[4 * g + hh:4 * g + hh + 1, :] += jnp.broadcast_to(dsink, (1, 128))
            dqt = _mm_tn(band[:, 0:128], dst) * SCALE
            for hh in range(4):
                dqh = dqt[:, hh * BLK:(hh + 1) * BLK].T
                dq_ref[:, (4 * g + hh) * 64:(4 * g + hh + 1) * 64] = dqh[:, g * 64:(g + 1) * 64].astype(BF16)
            dk = _mm(dst, qs) * SCALE
            dv = _mm(p, dos)
            dkv_ref[pl.ds(cur, BLK), g * 64:(g + 1) * 64] = dk[BLK:2 * BLK]
            dkv_ref[pl.ds(cur, BLK), 128 + g * 64:192 + g * 64] = dv[BLK:2 * BLK]
            dkv_ref[pl.ds(prev, BLK), g * 64:(g + 1) * 64] += dk[0:BLK]
            dkv_ref[pl.ds(prev, BLK), 128 + g * 64:192 + g * 64] += dv[0:BLK]

    return pl.pallas_call(
        body, name="attn_bwd", grid=(T // BLK,),
        in_specs=[_heads(BLK), _full((T, 256)), _heads(BLK), pl.BlockSpec(memory_space=pltpu.SMEM)],
        out_specs=[_rows(BLK, 512), _full((T, 256)), _full((8, 128))],
        out_shape=[jax.ShapeDtypeStruct((T, 512), BF16), jax.ShapeDtypeStruct((T, 256), F32),
                   jax.ShapeDtypeStruct((8, 128), F32)],
        compiler_params=_params(),
    )(q, kv, do, sinks)


def _rows8(tm, cols):
    return lax.broadcasted_iota(jnp.int32, (tm, cols), 0) & 7


def _lru_gates(xc, wa, ba, wx, bx, lam):
    r = _sigmoid(_mm(xc, wa) + ba)
    ii = _sigmoid(_mm(xc, wx) + bx)
    sp = _softplus(-lam)
    la = -LRU_C * r * sp
    a = jnp.exp(la)
    m = jnp.sqrt(-_expm1(2.0 * la))
    return r, ii, sp, a, m


def _rnn_fwd(xr, gr, cw, cb, wa, ba, wx, bx, lam):
    T = xr.shape[0]
    tm = 256
    C = D_RNN

    def body(xr_ref, gr_ref, cw_ref, cb_ref, wa_ref, ba_ref, wx_ref, bx_ref, lam_ref,
             xc_ref, h_ref, rec_ref, ext, a_s, b_s, carry):
        i = pl.program_id(0)

        @pl.when(i == 0)
        def _():
            ext[0:8, :] = jnp.zeros((8, C), F32)
            carry[...] = jnp.zeros((8, C), F32)

        ext[8:8 + tm, :] = xr_ref[...]
        xc = cb_ref[...] + cw_ref[3:4, :] * ext[8:8 + tm, :]
        for k in range(3):
            xc = xc + cw_ref[k:k + 1, :] * ext[5 + k:5 + k + tm, :]
        ext[0:8, :] = ext[tm:tm + 8, :]
        xc_ref[...] = xc
        _, ii, _, a, m = _lru_gates(xc, wa_ref[...], ba_ref[...], wx_ref[...], bx_ref[...], lam_ref[...])
        b = m * ii * xc
        r8 = _rows8(tm, C)
        for d in (1, 2, 4):
            ok = r8 >= d
            a_sh = jnp.where(ok, pltpu.roll(a, d, 0), 1.0)
            b_sh = jnp.where(ok, pltpu.roll(b, d, 0), 0.0)
            b = a * b_sh + b
            a = a * a_sh
        a_s[...] = a
        b_s[...] = b

        def step(g, hin):
            s = pl.multiple_of(g * 8, 8)
            hg = a_s[pl.ds(s, 8), :] * hin + b_s[pl.ds(s, 8), :]
            h_ref[pl.ds(s, 8), :] = hg
            return jnp.broadcast_to(hg[7:8, :], (8, C))

        carry[...] = lax.fori_loop(0, tm // 8, step, carry[...])
        ge, _ = _gelu(gr_ref[...])
        rec_ref[...] = (h_ref[...] * ge).astype(BF16)

    vec = _full((1, C))
    return pl.pallas_call(
        body, name="rnn_fwd", grid=(T // tm,),
        in_specs=[_rows(tm, C), _rows(tm, C), _full((4, C)), vec, _full((C, C)), vec, _full((C, C)), vec, vec],
        out_specs=[_rows(tm, C), _rows(tm, C), _rows(tm, C)],
        out_shape=[jax.ShapeDtypeStruct((T, C), F32), jax.ShapeDtypeStruct((T, C), F32),
                   jax.ShapeDtypeStruct((T, C), BF16)],
        scratch_shapes=[pltpu.VMEM((tm + 8, C), F32), pltpu.VMEM((tm, C), F32), pltpu.VMEM((tm, C), F32),
                        pltpu.VMEM((8, C), F32)],
        compiler_params=_params(),
    )(xr, gr, cw, cb, wa, ba, wx, bx, lam)


def _rnn_bwd(drec, gr, h, xc, xr, cw, wa, ba, wx, bx, lam):
    T = xr.shape[0]
    tm = 256
    C = D_RNN
    nt = T // tm
    t8 = tm // 8

    def body(drec_ref, gr_ref, h_ref, hp_ref, xc_ref, xr_ref, xrp_ref, cw_ref, wa_ref, ba_ref, wx_ref, bx_ref,
             lam_ref, dxr_ref, dgr_ref, dwa_ref, dwx_ref, dvec_ref, c_s, g_s, gout, ext, xext, anext, gcarry):
        i = pl.program_id(0)
        j = nt - 1 - i

        @pl.when(i == 0)
        def _():
            dwa_ref[...] = jnp.zeros_like(dwa_ref)
            dwx_ref[...] = jnp.zeros_like(dwx_ref)
            dvec_ref[...] = jnp.zeros_like(dvec_ref)
            anext[...] = jnp.zeros((8, C), F32)
            gcarry[...] = jnp.zeros((8, C), F32)
            ext[tm:tm + 8, :] = jnp.zeros((8, C), F32)

        xc = xc_ref[...]
        lam = lam_ref[...]
        r, ii, sp, a, m = _lru_gates(xc, wa_ref[...], ba_ref[...], wx_ref[...], bx_ref[...], lam)
        ge, dge = _gelu(gr_ref[...])
        drec = drec_ref[...]
        hh = h_ref[...]
        dgr_ref[...] = (drec * hh * dge).astype(BF16)
        dh = drec * ge
        rowi = lax.broadcasted_iota(jnp.int32, (tm, C), 0)
        c = jnp.where(rowi == tm - 1, jnp.broadcast_to(anext[0:1, :], (tm, C)), pltpu.roll(a, tm - 1, 0))
        anext[...] = a[0:8, :]
        r8 = rowi & 7
        gg = dh
        for d in (1, 2, 4):
            ok = r8 < 8 - d
            c_sh = jnp.where(ok, pltpu.roll(c, tm - d, 0), 1.0)
            g_sh = jnp.where(ok, pltpu.roll(gg, tm - d, 0), 0.0)
            gg = c * g_sh + gg
            c = c * c_sh
        c_s[...] = c
        g_s[...] = gg

        def step(k, gin):
            s = pl.multiple_of((t8 - 1 - k) * 8, 8)
            og = c_s[pl.ds(s, 8), :] * gin + g_s[pl.ds(s, 8), :]
            gout[pl.ds(s, 8), :] = og
            return jnp.broadcast_to(og[0:1, :], (8, C))

        gcarry[...] = lax.fori_loop(0, t8, step, gcarry[...])
        G = gout[...]
        hprev_row = jnp.where(j > 0, hp_ref[7:8, :], 0.0)
        hprev = jnp.where(rowi == 0, jnp.broadcast_to(hprev_row, (tm, C)), pltpu.roll(hh, 1, 0))
        da = G * hprev
        dm = G * ii * xc
        di = G * m * xc
        dxc = G * m * ii
        dla = da * a - dm * a * a / m
        dr = dla * (-LRU_C * sp)
        dsp = _colsum(dla * (-LRU_C * r))
        dlam = dsp * (-_sigmoid(-lam))
        dpr = dr * r * (1.0 - r)
        dpi = di * ii * (1.0 - ii)
        dxc = dxc + _mm_nt(dpr, wa_ref[...]) + _mm_nt(dpi, wx_ref[...])
        dwa_ref[...] += _mm_tn(xc, dpr)
        dwx_ref[...] += _mm_tn(xc, dpi)
        dvec_ref[0:1, :] += _colsum(dpr)
        dvec_ref[1:2, :] += _colsum(dpi)
        dvec_ref[2:3, :] += dlam
        dvec_ref[3:4, :] += _colsum(dxc)
        ext[0:tm, :] = dxc
        dxr = cw_ref[3:4, :] * dxc
        for k in range(3):
            dxr = dxr + cw_ref[k:k + 1, :] * ext[3 - k:3 - k + tm, :]
        ext[tm:tm + 8, :] = dxc[0:8, :]
        dxr_ref[...] = dxr.astype(BF16)
        xext[0:8, :] = jnp.where(j > 0, xrp_ref[...], 0.0)
        xext[8:8 + tm, :] = xr_ref[...]
        for k in range(4):
            dvec_ref[4 + k:5 + k, :] += _colsum(dxc * xext[5 + k:5 + k + tm, :])

    rev = lambda i: nt - 1 - i
    prev8 = lambda i: jnp.maximum((nt - 1 - i) * t8 - 1, 0)
    vec = _full((1, C))
    return pl.pallas_call(
        body, name="rnn_bwd", grid=(nt,),
        in_specs=[_rows(tm, C, rev), _rows(tm, C, rev), _rows(tm, C, rev), _rows(8, C, prev8), _rows(tm, C, rev),
                  _rows(tm, C, rev), _rows(8, C, prev8), _full((4, C)), _full((C, C)), vec, _full((C, C)), vec, vec],
        out_specs=[_rows(tm, C, rev), _rows(tm, C, rev), _full((C, C)), _full((C, C)), _full((8, C))],
        out_shape=[jax.ShapeDtypeStruct((T, C), BF16), jax.ShapeDtypeStruct((T, C), BF16),
                   jax.ShapeDtypeStruct((C, C), F32), jax.ShapeDtypeStruct((C, C), F32),
                   jax.ShapeDtypeStruct((8, C), F32)],
        scratch_shapes=[pltpu.VMEM((tm, C), F32), pltpu.VMEM((tm, C), F32), pltpu.VMEM((tm, C), F32),
                        pltpu.VMEM((tm + 8, C), F32), pltpu.VMEM((tm + 8, C), F32), pltpu.VMEM((8, C), F32),
                        pltpu.VMEM((8, C), F32)],
        compiler_params=_params(),
    )(drec, gr, h, h, xc, xr, xr, cw, wa, ba, wx, bx, lam)


def _out_proj(att, rec, x, w_out, g1, b1):
    T = x.shape[0]
    tm = 512

    def body(att_ref, rec_ref, x_ref, w_ref, g1_ref, b1_ref, z_ref, h_ref):
        mix = _mm(att_ref[...], w_ref[0:512, :]) + _mm(rec_ref[...], w_ref[512:1024, :])
        z1 = ALPHA * x_ref[...] + mix
        z_ref[...] = z1
        h1, _, _ = _ln(z1, g1_ref[...], b1_ref[...])
        h_ref[...] = h1.astype(MXU_DTYPE).astype(BF16)

    return pl.pallas_call(
        body, name="out_proj", grid=(T // tm,),
        in_specs=[_rows(tm, 512), _rows(tm, 512), _rows(tm, D), _full((D, D)), _full((1, D)), _full((1, D))],
        out_specs=[_rows(tm, D), _rows(tm, D)],
        out_shape=[jax.ShapeDtypeStruct((T, D), F32), jax.ShapeDtypeStruct((T, D), BF16)],
        compiler_params=_params(),
    )(att, rec, x, w_out, g1, b1)


NC = D_FF // FF_CHUNK


def _ffn_up(h1b, w_up_t, fcw, fcb):
    T = h1b.shape[0]
    tm = 512
    CW = FF_CHUNK

    def body(h_ref, wg_ref, wv_ref, fcw_ref, fcb_ref, gate_ref, val_ref, act_ref, ext):
        i = pl.program_id(1)

        @pl.when(i == 0)
        def _():
            ext[0:8, :] = jnp.zeros((8, CW), F32)

        hb = h_ref[...]
        gate = _mm_nt(hb, wg_ref[...])
        val = _mm_nt(hb, wv_ref[...])
        gate_ref[...] = gate
        val_ref[...] = val
        ext[8:8 + tm, :] = gate
        gc = (fcb_ref[...] + fcw_ref[0:1, :] * ext[6:6 + tm, :] + fcw_ref[1:2, :] * ext[7:7 + tm, :]
              + fcw_ref[2:3, :] * gate)
        ext[0:8, :] = ext[tm:tm + 8, :]
        ge, _ = _gelu(gc)
        act_ref[...] = (ge * val).astype(BF16)

    chunk = pl.BlockSpec((None, tm, CW), lambda c, i: (c, i, 0))
    return pl.pallas_call(
        body, name="ffn_up", grid=(NC, T // tm),
        in_specs=[pl.BlockSpec((tm, D), lambda c, i: (i, 0)), pl.BlockSpec((CW, D), lambda c, i: (c, 0)),
                  pl.BlockSpec((CW, D), lambda c, i: (NC + c, 0)), pl.BlockSpec((None, 3, CW), lambda c, i: (c, 0, 0)),
                  pl.BlockSpec((None, 1, CW), lambda c, i: (c, 0, 0))],
        out_specs=[chunk, chunk, chunk],
        out_shape=[jax.ShapeDtypeStruct((NC, T, CW), F32), jax.ShapeDtypeStruct((NC, T, CW), F32),
                   jax.ShapeDtypeStruct((NC, T, CW), BF16)],
        scratch_shapes=[pltpu.VMEM((tm + 8, CW), F32)],
        compiler_params=_params(),
    )(h1b, w_up_t, w_up_t, fcw, fcb)


def _ffn_down(act, z1, p, tgt, w_down, w_g, w_p_t, g1, b1, g2, b2, bg):
    T = z1.shape[0]
    tm = 256

    def body(act_ref, z_ref, p_ref, t_ref, wdn_hbm, wg_hbm, wp_hbm, g1_ref, b1_ref, g2_ref, b2_ref, bg_ref,
             dz2_ref, dz2b_ref, dpre_ref, dpp_ref, vec_ref, wdn, wg, wp):
        @pl.when(pl.program_id(0) == 0)
        def _():
            pltpu.sync_copy(wdn_hbm, wdn)
            pltpu.sync_copy(wg_hbm, wg)
            pltpu.sync_copy(wp_hbm, wp)
            vec_ref[...] = jnp.zeros_like(vec_ref)

        g2v = g2_ref[...]
        h1, _, _ = _ln(z_ref[...], g1_ref[...], b1_ref[...])
        h1b = h1.astype(MXU_DTYPE)
        ffn = _mm(act_ref[0], wdn[0:FF_CHUNK, :])
        for c in range(1, NC):
            ffn = ffn + _mm(act_ref[c], wdn[c * FF_CHUNK:(c + 1) * FF_CHUNK, :])
        sg = _sigmoid(_mm(h1b, wg[...]) + bg_ref[...])
        pp = _mm_nt(p_ref[...], wp[...])
        z2 = ALPHA * h1 + ffn + sg * pp
        y, xh2, rstd2 = _ln(z2, g2v, b2_ref[...])
        diff = y - t_ref[...]
        dy = diff * (1.0 / D)
        dz2 = _ln_bwd(dy, xh2, rstd2, g2v)
        dpre = dz2 * pp * sg * (1.0 - sg)
        dz2_ref[...] = dz2
        dz2b_ref[...] = dz2.astype(BF16)
        dpre_ref[...] = dpre.astype(BF16)
        dpp_ref[...] = (dz2 * sg).astype(BF16)
        loss = 0.5 * jnp.sum(jnp.sum(diff * diff, axis=1, keepdims=True), axis=0, keepdims=True) * (1.0 / D)
        vec_ref[0:1, :] += jnp.broadcast_to(loss, (1, D))
        vec_ref[1:2, :] += _colsum(dy * xh2)
        vec_ref[2:3, :] += _colsum(dy)
        vec_ref[3:4, :] += _colsum(dpre)

    anyspec = pl.BlockSpec(memory_space=pl.ANY)
    vec = _full((1, D))
    return pl.pallas_call(
        body, name="ffn_down", grid=(T // tm,),
        in_specs=[pl.BlockSpec((NC, tm, FF_CHUNK), lambda i: (0, i, 0)), _rows(tm, D), _rows(tm, PLE), _rows(tm, D),
                  anyspec, anyspec, anyspec] + [vec] * 5,
        out_specs=[_rows(tm, D)] * 4 + [_full((8, D))],
        out_shape=[jax.ShapeDtypeStruct((T, D), F32)] + [jax.ShapeDtypeStruct((T, D), BF16)] * 3
                  + [jax.ShapeDtypeStruct((8, D), F32)],
        scratch_shapes=[pltpu.VMEM((D_FF, D), MXU_DTYPE), pltpu.VMEM((D, D), MXU_DTYPE), pltpu.VMEM((D, PLE), MXU_DTYPE)],
        compiler_params=_params(),
    )(act, z1, p, tgt, w_down, w_g, w_p_t, g1, b1, g2, b2, bg)


def _ffn_bwd(dz2b, gate, val, w_down, fcw, fcb):
    T = dz2b.shape[0]
    tm = 512
    CW = FF_CHUNK
    nt = T // tm
    t8 = tm // 8

    def body(dz_ref, wdn_ref, gate_ref, gp_ref, val_ref, fcw_ref, fcb_ref, dup_ref, dfc_ref, gext, dext):
        i = pl.program_id(1)
        j = nt - 1 - i

        @pl.when(i == 0)
        def _():
            dext[tm:tm + 8, :] = jnp.zeros((8, CW), F32)
            dfc_ref[...] = jnp.zeros_like(dfc_ref)

        gate = gate_ref[...]
        gext[0:8, :] = jnp.where(j > 0, gp_ref[...], 0.0)
        gext[8:8 + tm, :] = gate
        gate1 = gext[7:7 + tm, :]
        gate2 = gext[6:6 + tm, :]
        gc = fcb_ref[...] + fcw_ref[0:1, :] * gate2 + fcw_ref[1:2, :] * gate1 + fcw_ref[2:3, :] * gate
        ge, dge = _gelu(gc)
        dact = _mm_nt(dz_ref[...], wdn_ref[...])
        dgc = dact * val_ref[...] * dge
        dext[0:tm, :] = dgc
        dgate = fcw_ref[2:3, :] * dgc + fcw_ref[1:2, :] * dext[1:1 + tm, :] + fcw_ref[0:1, :] * dext[2:2 + tm, :]
        dext[tm:tm + 8, :] = dgc[0:8, :]
        dup_ref[0] = dgate.astype(BF16)
        dup_ref[1] = (dact * ge).astype(BF16)
        dfc_ref[0:1, :] += _colsum(dgc * gate2)
        dfc_ref[1:2, :] += _colsum(dgc * gate1)
        dfc_ref[2:3, :] += _colsum(dgc * gate)
        dfc_ref[3:4, :] += _colsum(dgc)

    rev = lambda c, i: (c, nt - 1 - i, 0)
    return pl.pallas_call(
        body, name="ffn_bwd", grid=(NC, nt),
        in_specs=[pl.BlockSpec((tm, D), lambda c, i: (nt - 1 - i, 0)), pl.BlockSpec((CW, D), lambda c, i: (c, 0)),
                  pl.BlockSpec((None, tm, CW), rev),
                  pl.BlockSpec((None, 8, CW), lambda c, i: (c, jnp.maximum((nt - 1 - i) * t8 - 1, 0), 0)),
                  pl.BlockSpec((None, tm, CW), rev), pl.BlockSpec((None, 3, CW), lambda c, i: (c, 0, 0)),
                  pl.BlockSpec((None, 1, CW), lambda c, i: (c, 0, 0))],
        out_specs=[pl.BlockSpec((None, 2, tm, CW), lambda c, i: (c, 0, nt - 1 - i, 0)),
                   pl.BlockSpec((None, 8, CW), lambda c, i: (c, 0, 0))],
        out_shape=[jax.ShapeDtypeStruct((NC, 2, T, CW), BF16), jax.ShapeDtypeStruct((NC, 8, CW), F32)],
        scratch_shapes=[pltpu.VMEM((tm + 8, CW), F32), pltpu.VMEM((tm + 8, CW), F32)],
        compiler_params=_params(),
    )(dz2b, w_down, gate, gate, val, fcw, fcb)


def _ffn_dh1(dup, dz2, dpre, z1, w_up_t, w_g, g1, b1):
    T = z1.shape[0]
    tm = 256

    def body(dup_ref, dz2_ref, dpre_ref, z_ref, wup_hbm, wg_hbm, g1_ref, b1_ref, dz1_ref, vec_ref, wup, wg):
        @pl.when(pl.program_id(0) == 0)
        def _():
            pltpu.sync_copy(wup_hbm, wup)
            pltpu.sync_copy(wg_hbm, wg)
            vec_ref[...] = jnp.zeros_like(vec_ref)

        g1v = g1_ref[...]
        _, xh1, rstd1 = _ln(z_ref[...], g1v, b1_ref[...])
        dh1 = ALPHA * dz2_ref[...] + _mm_nt(dpre_ref[...], wg[...])
        for c in range(NC):
            for s in range(2):
                r0 = s * D_FF + c * FF_CHUNK
                dh1 = dh1 + _mm(dup_ref[c, s], wup[r0:r0 + FF_CHUNK, :])
        dz1_ref[...] = _ln_bwd(dh1, xh1, rstd1, g1v)
        vec_ref[0:1, :] += _colsum(dh1 * xh1)
        vec_ref[1:2, :] += _colsum(dh1)

    anyspec = pl.BlockSpec(memory_space=pl.ANY)
    vec = _full((1, D))
    return pl.pallas_call(
        body, name="ffn_dh1", grid=(T // tm,),
        in_specs=[pl.BlockSpec((NC, 2, tm, FF_CHUNK), lambda i: (0, 0, i, 0)), _rows(tm, D), _rows(tm, D), _rows(tm, D),
                  anyspec, anyspec, vec, vec],
        out_specs=[_rows(tm, D), _full((8, D))],
        out_shape=[jax.ShapeDtypeStruct((T, D), F32), jax.ShapeDtypeStruct((8, D), F32)],
        scratch_shapes=[pltpu.VMEM((2 * D_FF, D), MXU_DTYPE), pltpu.VMEM((D, D), MXU_DTYPE)],
        compiler_params=_params(),
    )(dup, dz2, dpre, z1, w_up_t, w_g, g1, b1)


def _out_proj_bwd(dz1, w_out):
    T = dz1.shape[0]
    tm = 512

    def body(dz_ref, w_ref, datt_ref, drec_ref, dzb_ref):
        dzb = dz_ref[...].astype(MXU_DTYPE)
        dzb_ref[...] = dzb.astype(BF16)
        datt = _mm_nt(dzb, w_ref[0:512, :])
        for h in range(HEADS):
            datt_ref[h] = datt[:, h * 64:(h + 1) * 64].astype(BF16)
        drec_ref[...] = _mm_nt(dzb, w_ref[512:1024, :])

    return pl.pallas_call(
        body, name="out_proj_bwd", grid=(T // tm,),
        in_specs=[_rows(tm, D), _full((D, D))],
        out_specs=[_heads(tm), _rows(tm, 512), _rows(tm, D)],
        out_shape=[jax.ShapeDtypeStruct((HEADS, T, 64), BF16), jax.ShapeDtypeStruct((T, 512), F32),
                   jax.ShapeDtypeStruct((T, D), BF16)],
        compiler_params=_params(),
    )(dz1, w_out)


def _in_proj_bwd(dq, dkv, dxr, dgr, dz1, w_in_t):
    T = dz1.shape[0]
    tm = 512

    def body(dq_ref, dkv_ref, dxr_ref, dgr_ref, dz_ref, w_ref, dx_ref, du_ref):
        dkv = dkv_ref[...].astype(BF16)
        dx_ref[...] = (ALPHA * dz_ref[...] + _mm(dq_ref[...], w_ref[0:512, :]) + _mm(dkv, w_ref[512:768, :])
                       + _mm(dxr_ref[...], w_ref[768:1280, :]) + _mm(dgr_ref[...], w_ref[1280:1792, :]))
        du_ref[:, 0:512] = dq_ref[...]
        du_ref[:, 512:768] = dkv
        du_ref[:, 768:1280] = dxr_ref[...]
        du_ref[:, 1280:1792] = dgr_ref[...]

    return pl.pallas_call(
        body, name="in_proj_bwd", grid=(T // tm,),
        in_specs=[_rows(tm, 512), _rows(tm, 256), _rows(tm, 512), _rows(tm, 512), _rows(tm, D), _full((D_IN, D))],
        out_specs=[_rows(tm, D), _rows(tm, D_IN)],
        out_shape=[jax.ShapeDtypeStruct((T, D), F32), jax.ShapeDtypeStruct((T, D_IN), BF16)],
        compiler_params=_params(),
    )(dq, dkv, dxr, dgr, dz1, w_in_t)


def _weight_grad(a, b, bm, name, out_block=lambda m: m):
    bt = min(2048, b.shape[0])
    if a.ndim == 3:
        assert a.shape[2] == bm
        T, M = a.shape[1], a.shape[0] * bm
        a_spec = pl.BlockSpec((None, bt, bm), lambda m, k: (m, k, 0))
    else:
        T, M = a.shape
        a_spec = pl.BlockSpec((bt, bm), lambda m, k: (k, m))
    N = b.shape[1]
    nk = T // bt

    def body(a_ref, b_ref, o_ref):
        k = pl.program_id(1)

        @pl.when(k == 0)
        def _():
            o_ref[...] = jnp.zeros_like(o_ref)

        o_ref[...] += _mm_tn(a_ref[...], b_ref[...])

    return pl.pallas_call(
        body, name=name, grid=(M // bm, nk),
        in_specs=[a_spec, pl.BlockSpec((bt, N), lambda m, k: (k, 0))],
        out_specs=pl.BlockSpec((bm, N), lambda m, k: (out_block(m), 0)),
        out_shape=jax.ShapeDtypeStruct((M, N), F32),
        compiler_params=_params(),
    )(a, b)


def _adamw(w, g, m, v, name):
    R, C = w.shape
    tr = R // 8 if R % 64 == 0 else R
    c1 = 1.0 / (1.0 - ADAM_B1 ** ADAM_STEP)
    c2 = 1.0 / (1.0 - ADAM_B2 ** ADAM_STEP)

    def body(w_ref, g_ref, m_ref, v_ref, d_ref, nm_ref, nv_ref):
        g = g_ref[...]
        nm = ADAM_B1 * m_ref[...] + (1.0 - ADAM_B1) * g
        nv = ADAM_B2 * v_ref[...] + (1.0 - ADAM_B2) * g * g
        nm_ref[...] = nm
        nv_ref[...] = nv
        d_ref[...] = -ADAM_LR * ((nm * c1) / (jnp.sqrt(nv * c2) + ADAM_EPS) + ADAM_WD * w_ref[...])

    spec = pl.BlockSpec((tr, C), lambda i: (i, 0))
    return pl.pallas_call(
        body, name=name, grid=(R // tr,),
        in_specs=[spec] * 4, out_specs=[spec] * 3,
        out_shape=[jax.ShapeDtypeStruct((R, C), F32)] * 3,
        compiler_params=_params(),
    )(w, g, m, v)


def _adamw_halves(w, mine, sib, m, v, c):
    tr = 416
    nb = HALF // tr
    c1 = 1.0 / (1.0 - ADAM_B1 ** ADAM_STEP)
    c2 = 1.0 / (1.0 - ADAM_B2 ** ADAM_STEP)

    def body(c_ref, w_ref, a_ref, b_ref, m_ref, v_ref, g_ref, d_ref, nm_ref, nv_ref):
        own = (pl.program_id(0) // nb) == c_ref[0]
        g = jnp.where(own, a_ref[...], b_ref[...])
        nm = ADAM_B1 * m_ref[...] + (1.0 - ADAM_B1) * g
        nv = ADAM_B2 * v_ref[...] + (1.0 - ADAM_B2) * g * g
        g_ref[...] = g
        nm_ref[...] = nm
        nv_ref[...] = nv
        d_ref[...] = -ADAM_LR * ((nm * c1) / (jnp.sqrt(nv * c2) + ADAM_EPS) + ADAM_WD * w_ref[...])

    full = pl.BlockSpec((tr, 1024), lambda i, c_ref: (i, 0))
    half = pl.BlockSpec((tr, 1024), lambda i, c_ref: (i % nb, 0))
    grid_spec = pltpu.PrefetchScalarGridSpec(num_scalar_prefetch=1, grid=(2 * nb,),
                                             in_specs=[full, half, half, full, full], out_specs=[full] * 4)
    return pl.pallas_call(body, name="adamw_big", grid_spec=grid_spec,
                          out_shape=[jax.ShapeDtypeStruct((PACK_TOTAL, 1024), F32)] * 4,
                          compiler_params=_params())(c, w, mine, sib, m, v)


def _add4(a, name):
    _, R, C = a.shape
    tr = 416

    def body(a_ref, o_ref):
        o_ref[...] = ((a_ref[0].astype(F32) + a_ref[1].astype(F32)) + a_ref[2].astype(F32)) + a_ref[3].astype(F32)

    return pl.pallas_call(body, name=name, grid=(R // tr,),
                          in_specs=[pl.BlockSpec((4, tr, C), lambda i: (0, i, 0))],
                          out_specs=pl.BlockSpec((tr, C), lambda i: (i, 0)),
                          out_shape=jax.ShapeDtypeStruct((R, C), F32), compiler_params=_params())(a)


def _pos():
    return lax.axis_index("x"), lax.axis_index("y"), lax.axis_index("c")


def _other_chips(x, y):
    return [(1 - x, y), (x, 1 - y), (1 - x, 1 - y)]


def _gather_weights(wpack, cpack):
    def body(w_ref, c_ref, gw_ref, gc_ref, send_sems, recv_sems, local_sems):
        x, y, c = _pos()
        me = 2 * x + y
        chips = _other_chips(x, y)
        mine = pl.ds(pl.multiple_of(c * HALF, 16), HALF)
        theirs = pl.ds(pl.multiple_of((1 - c) * HALF, 16), HALF)
        loc = [pltpu.make_async_copy(w_ref, gw_ref.at[me], local_sems.at[0]),
               pltpu.make_async_copy(c_ref, gc_ref.at[me], local_sems.at[1])]
        for cp in loc:
            cp.start()

        def copy(k, src, dst, to):
            return pltpu.make_async_remote_copy(src_ref=src, dst_ref=dst, send_sem=send_sems.at[k], recv_sem=recv_sems.at[k],
                                                device_id=to, device_id_type=MESH)

        sends = []
        for k, (px, py) in enumerate(chips):
            sends.append(copy(k, w_ref.at[mine], gw_ref.at[me, mine], (px, py, c)))
            sends.append(copy(3 + k, c_ref, gc_ref.at[me], (px, py, c)))
        for cp in sends:
            cp.start()
        for k, (px, py) in enumerate(chips):
            j = 2 * px + py
            copy(k, w_ref.at[mine], gw_ref.at[j, mine], (px, py, c)).wait_recv()
            fwd = copy(6 + k, gw_ref.at[j, mine], gw_ref.at[j, mine], (x, y, 1 - c))
            fwd.start()
            sends.append(fwd)
        for k, (px, py) in enumerate(chips):
            j = 2 * px + py
            copy(3 + k, c_ref, gc_ref.at[j], (px, py, c)).wait_recv()
            copy(6 + k, gw_ref.at[j, theirs], gw_ref.at[j, theirs], (x, y, 1 - c)).wait_recv()
        for cp in sends:
            cp.wait_send()
        for cp in loc:
            cp.wait()

    anyspec = pl.BlockSpec(memory_space=pl.ANY)
    return pl.pallas_call(
        body, name="gather_weights",
        in_specs=[anyspec, anyspec], out_specs=[anyspec, anyspec],
        out_shape=[jax.ShapeDtypeStruct((4,) + wpack.shape, wpack.dtype), jax.ShapeDtypeStruct((4,) + cpack.shape, cpack.dtype)],
        scratch_shapes=[pltpu.SemaphoreType.DMA((9,)), pltpu.SemaphoreType.DMA((9,)), pltpu.SemaphoreType.DMA((2,))],
        compiler_params=_params(has_side_effects=True),
    )(wpack, cpack)


def _allreduce_small(s):
    R = s.shape[0]

    def body(s_ref, o_ref, buf, send_sems, recv_sems):
        x, y, c = _pos()
        me = 4 * x + 2 * y + c
        buf[me] = s_ref[...]
        sends = []
        for k in range(1, 8):
            peer = (x ^ (k >> 2), y ^ ((k >> 1) & 1), c ^ (k & 1))
            cp = pltpu.make_async_remote_copy(src_ref=s_ref, dst_ref=buf.at[me], send_sem=send_sems.at[k - 1],
                                              recv_sem=recv_sems.at[k - 1], device_id=peer, device_id_type=MESH)
            cp.start()
            sends.append(cp)
        for k in range(1, 8):
            px, py, pc = x ^ (k >> 2), y ^ ((k >> 1) & 1), c ^ (k & 1)
            pltpu.make_async_remote_copy(src_ref=s_ref, dst_ref=buf.at[4 * px + 2 * py + pc], send_sem=send_sems.at[k - 1],
                                         recv_sem=recv_sems.at[k - 1], device_id=(px, py, pc),
                                         device_id_type=MESH).wait_recv()
        for cp in sends:
            cp.wait_send()
        acc = buf[0]
        for d in range(1, 8):
            acc = acc + buf[d]
        o_ref[...] = acc

    vm = pl.BlockSpec(memory_space=pltpu.VMEM)
    return pl.pallas_call(
        body, name="allreduce_small", in_specs=[vm], out_specs=vm,
        out_shape=jax.ShapeDtypeStruct((R, 128), F32),
        scratch_shapes=[pltpu.VMEM((8, R, 128), F32), pltpu.SemaphoreType.DMA((7,)), pltpu.SemaphoreType.DMA((7,))],
        compiler_params=_params(has_side_effects=True),
    )(s)


def _swap_halves(g):
    def body(g_ref, o_ref, send_sem, recv_sem):
        x, y, c = _pos()
        start = pl.multiple_of((1 - c) * HALF, 8)
        cp = pltpu.make_async_remote_copy(src_ref=g_ref.at[:, pl.ds(start, HALF), :], dst_ref=o_ref, send_sem=send_sem,
                                          recv_sem=recv_sem, device_id=(x, y, 1 - c), device_id_type=MESH)
        cp.start()
        cp.wait()

    anyspec = pl.BlockSpec(memory_space=pl.ANY)
    return pl.pallas_call(
        body, name="swap_halves", in_specs=[anyspec], out_specs=anyspec,
        out_shape=jax.ShapeDtypeStruct((4, HALF, 1024), F32),
        scratch_shapes=[pltpu.SemaphoreType.DMA, pltpu.SemaphoreType.DMA],
        compiler_params=_params(has_side_effects=True),
    )(g)


def _scatter_chips(s):
    def body(s_ref, o_ref, send_sems, recv_sems, local_sem):
        x, y, c = _pos()
        me = 2 * x + y
        loc = pltpu.make_async_copy(s_ref.at[me], o_ref.at[me], local_sem)
        loc.start()
        sends = []
        for k, (px, py) in enumerate(_other_chips(x, y)):
            cp = pltpu.make_async_remote_copy(src_ref=s_ref.at[2 * px + py], dst_ref=o_ref.at[me], send_sem=send_sems.at[k],
                                              recv_sem=recv_sems.at[k], device_id=(px, py, c), device_id_type=MESH)
            cp.start()
            sends.append(cp)
        for k, (px, py) in enumerate(_other_chips(x, y)):
            pltpu.make_async_remote_copy(src_ref=s_ref.at[me], dst_ref=o_ref.at[2 * px + py], send_sem=send_sems.at[k],
                                         recv_sem=recv_sems.at[k], device_id=(px, py, c), device_id_type=MESH).wait_recv()
        for cp in sends:
            cp.wait_send()
        loc.wait()

    anyspec = pl.BlockSpec(memory_space=pl.ANY)
    return pl.pallas_call(
        body, name="scatter_chips", in_specs=[anyspec], out_specs=anyspec,
        out_shape=jax.ShapeDtypeStruct((4, HALF, 1024), s.dtype),
        scratch_shapes=[pltpu.SemaphoreType.DMA((3,)), pltpu.SemaphoreType.DMA((3,)), pltpu.SemaphoreType.DMA],
        compiler_params=_params(has_side_effects=True),
    )(s)


def _send_half(r):
    def body(r_ref, o_ref, send_sem, recv_sem):
        x, y, c = _pos()
        cp = pltpu.make_async_remote_copy(src_ref=r_ref, dst_ref=o_ref, send_sem=send_sem, recv_sem=recv_sem,
                                          device_id=(x, y, 1 - c), device_id_type=MESH)
        cp.start()
        cp.wait()

    anyspec = pl.BlockSpec(memory_space=pl.ANY)
    return pl.pallas_call(
        body, name="send_half", in_specs=[anyspec], out_specs=anyspec,
        out_shape=jax.ShapeDtypeStruct((HALF, 1024), F32),
        scratch_shapes=[pltpu.SemaphoreType.DMA, pltpu.SemaphoreType.DMA],
        compiler_params=_params(has_side_effects=True),
    )(r)


def _add_half(g, r, c):
    tr = 416
    nb = HALF // tr

    def body(c_ref, g_ref, r_ref, o_ref):
        o_ref[...] = (g_ref[...] + r_ref[...]).astype(BF16)

    grid_spec = pltpu.PrefetchScalarGridSpec(
        num_scalar_prefetch=1, grid=(4, nb),
        in_specs=[pl.BlockSpec((1, tr, 1024), lambda j, i, c_ref: (j, c_ref[0] * nb + i, 0)),
                  pl.BlockSpec((1, tr, 1024), lambda j, i, c_ref: (j, i, 0))],
        out_specs=pl.BlockSpec((1, tr, 1024), lambda j, i, c_ref: (j, i, 0)))
    return pl.pallas_call(body, name="add_half", grid_spec=grid_spec,
                          out_shape=jax.ShapeDtypeStruct((4, HALF, 1024), BF16), compiler_params=_params())(c, g, r)


def _block_diag(w):
    eye = jnp.eye(RNN_BLOCKS, dtype=w.dtype)
    return (eye[:, None, :, None] * w[:, :, None, :]).reshape(D_RNN, D_RNN)


def _diag_blocks(wd):
    d = wd.reshape(RNN_BLOCKS, 64, RNN_BLOCKS, 64)
    return jnp.stack([d[h, :, h, :] for h in range(RNN_BLOCKS)])


def _layer_grads(x, p, tgt, gw, small):
    row = lambda v: v.reshape(1, -1)
    wa = _block_diag(small["gate_a_w"]).astype(MXU_DTYPE)
    wx = _block_diag(small["gate_x_w"]).astype(MXU_DTYPE)
    sinks = small["attn_sinks"].reshape(1, HEADS)

    q, kv, xr, gr, xb = _in_proj(x, gw["w_in_t"])
    att = _attn_fwd(q, kv, sinks)
    xc, h, rec = _rnn_fwd(xr, gr, small["rnn_conv_w"], row(small["rnn_conv_b"]), wa, row(small["gate_a_b"]),
                          wx, row(small["gate_x_b"]), row(small["lru_lambda"]))
    g1, b1 = row(small["ln1_g"]), row(small["ln1_b"])
    fcw = small["ffn_conv_w"].reshape(3, NC, FF_CHUNK).transpose(1, 0, 2)
    fcb = small["ffn_conv_b"].reshape(NC, 1, FF_CHUNK)
    z1, h1b = _out_proj(att, rec, x, gw["w_out"], g1, b1)
    gate, val, act = _ffn_up(h1b, gw["w_up_t"], fcw, fcb)
    dz2, dz2b, dpre, dpp, vec2 = _ffn_down(act, z1, p, tgt, gw["w_down"], gw["w_g"], gw["w_p_t"], g1, b1,
                                           row(small["ln2_g"]), row(small["ln2_b"]), row(small["ple_gate_b"]))
    dup, dfc = _ffn_bwd(dz2b, gate, val, gw["w_down"], fcw, fcb)
    dz1, vec1 = _ffn_dh1(dup, dz2, dpre, z1, gw["w_up_t"], gw["w_g"], g1, b1)
    datt, drec, dz1b = _out_proj_bwd(dz1, gw["w_out"])
    dxr, dgr, dwa, dwx, dvec = _rnn_bwd(drec, gr, h, xc, xr, small["rnn_conv_w"], wa, row(small["gate_a_b"]),
                                        wx, row(small["gate_x_b"]), row(small["lru_lambda"]))
    dq, dkv, dsinks = _attn_bwd(q, kv, datt, sinks)
    grad_x, du = _in_proj_bwd(dq, dkv, dxr, dgr, dz1, gw["w_in_t"])

    mix = jnp.concatenate([att, rec], axis=1)
    pb = p.astype(BF16)
    big = {
        "w_in_t": _weight_grad(du, xb, 256, "dw_in"),
        "w_out": _weight_grad(mix, dz1b, 512, "dw_out"),
        "w_up_t": _weight_grad(dup.reshape(2 * NC, -1, FF_CHUNK), h1b, FF_CHUNK, "dw_up",
                               out_block=lambda m: (m % 2) * NC + m // 2),
        "w_down": _weight_grad(act, dz2b, 512, "dw_down"),
        "w_g": _weight_grad(h1b, dpre, 512, "dw_gate"),
        "w_p_t": _weight_grad(dpp, pb, 512, "dw_proj"),
    }
    sg = {
        "attn_sinks": dsinks[:, 0],
        "rnn_conv_w": dvec[4:8],
        "rnn_conv_b": dvec[3],
        "gate_a_w": _diag_blocks(dwa),
        "gate_a_b": dvec[0],
        "gate_x_w": _diag_blocks(dwx),
        "gate_x_b": dvec[1],
        "lru_lambda": dvec[2],
        "ln1_g": vec1[0],
        "ln1_b": vec1[1],
        "ffn_conv_w": dfc[:, 0:3].transpose(1, 0, 2).reshape(3, D_FF),
        "ffn_conv_b": dfc[:, 3].reshape(D_FF),
        "ple_gate_b": vec2[3],
        "ln2_g": vec2[1],
        "ln2_b": vec2[2],
    }
    return grad_x, big, sg, vec2[0, 0:1]


BIG = ("w_in", "w_out", "w_ffn_up", "w_ffn_down", "ple_gate_w", "ple_proj")
BIG_KEYS = ("w_in_t", "w_out", "w_up_t", "w_down", "w_g", "w_p_t")
BIG_T = (True, False, True, False, False, True)
SMALL = ("attn_sinks", "rnn_conv_w", "rnn_conv_b", "gate_a_w", "gate_a_b", "gate_x_w", "gate_x_b", "lru_lambda",
         "ln1_g", "ln1_b", "ffn_conv_w", "ffn_conv_b", "ple_gate_b", "ln2_g", "ln2_b")
SHARDED_SMALL = ("rnn_conv_w", "ffn_conv_w")
WEIGHTS = ("w_in", "attn_sinks", "rnn_conv_w", "rnn_conv_b", "gate_a_w", "gate_a_b", "gate_x_w", "gate_x_b",
           "lru_lambda", "w_out", "ln1_g", "ln1_b", "w_ffn_up", "ffn_conv_w", "ffn_conv_b", "w_ffn_down",
           "ple_gate_w", "ple_gate_b", "ple_proj", "ln2_g", "ln2_b")


def _pack_big(d):
    parts = []
    for name, t in zip(BIG, BIG_T):
        a = d[name]
        a = a.T if t else a
        parts.append(a.reshape(-1, 1024))
    return jnp.concatenate(parts, axis=0)


def _unpack_big(a):
    out = {}
    shapes = {"w_in": (448, 1024), "w_out": (256, 1024), "w_ffn_up": (1536, 1024), "w_ffn_down": (768, 1024),
              "ple_gate_w": (256, 1024), "ple_proj": (256, 256)}
    for i, (name, t) in enumerate(zip(BIG, BIG_T)):
        s = a[PACK_OFF[i]:PACK_OFF[i + 1]].reshape(shapes[name])
        out[name] = (s.T if t else s)[None]
    return out


def _pack_vecs(items):
    parts, offs, n = [], [], 0
    for a in items:
        f = a.reshape(-1).astype(F32)
        pad = (-f.shape[0]) % 128
        parts.append(jnp.pad(f, (0, pad)))
        offs.append(n)
        n += (f.shape[0] + pad) // 128
    padr = (-n) % 8
    if padr:
        parts.append(jnp.zeros((padr * 128,), F32))
    return jnp.concatenate(parts).reshape(-1, 128), offs


def _unpack_vecs(a, offs, shapes):
    flat = a.reshape(-1)
    out = []
    for o, s in zip(offs, shapes):
        n = 1
        for d in s:
            n *= d
        out.append(flat[o * 128:o * 128 + n].reshape(s))
    return out


def kernel(x, p, w_in, attn_sinks, rnn_conv_w, rnn_conv_b, gate_a_w, gate_a_b, gate_x_w, gate_x_b, lru_lambda, w_out, ln1_g, ln1_b, w_ffn_up, ffn_conv_w, ffn_conv_b, w_ffn_down, ple_gate_w, ple_gate_b, ple_proj, ln2_g, ln2_b, loss_target, m_w_in, m_attn_sinks, m_rnn_conv_w, m_rnn_conv_b, m_gate_a_w, m_gate_a_b, m_gate_x_w, m_gate_x_b, m_lru_lambda, m_w_out, m_ln1_g, m_ln1_b, m_w_ffn_up, m_ffn_conv_w, m_ffn_conv_b, m_w_ffn_down, m_ple_gate_w, m_ple_gate_b, m_ple_proj, m_ln2_g, m_ln2_b, v_w_in, v_attn_sinks, v_rnn_conv_w, v_rnn_conv_b, v_gate_a_w, v_gate_a_b, v_gate_x_w, v_gate_x_b, v_lru_lambda, v_w_out, v_ln1_g, v_ln1_b, v_w_ffn_up, v_ffn_conv_w, v_ffn_conv_b, v_w_ffn_down, v_ple_gate_w, v_ple_gate_b, v_ple_proj, v_ln2_g, v_ln2_b):
    w = dict(w_in=w_in, attn_sinks=attn_sinks, rnn_conv_w=rnn_conv_w, rnn_conv_b=rnn_conv_b, gate_a_w=gate_a_w,
             gate_a_b=gate_a_b, gate_x_w=gate_x_w, gate_x_b=gate_x_b, lru_lambda=lru_lambda, w_out=w_out, ln1_g=ln1_g,
             ln1_b=ln1_b, w_ffn_up=w_ffn_up, ffn_conv_w=ffn_conv_w, ffn_conv_b=ffn_conv_b, w_ffn_down=w_ffn_down,
             ple_gate_w=ple_gate_w, ple_gate_b=ple_gate_b, ple_proj=ple_proj, ln2_g=ln2_g, ln2_b=ln2_b)
    m = dict(w_in=m_w_in, attn_sinks=m_attn_sinks, rnn_conv_w=m_rnn_conv_w, rnn_conv_b=m_rnn_conv_b, gate_a_w=m_gate_a_w,
             gate_a_b=m_gate_a_b, gate_x_w=m_gate_x_w, gate_x_b=m_gate_x_b, lru_lambda=m_lru_lambda, w_out=m_w_out,
             ln1_g=m_ln1_g, ln1_b=m_ln1_b, w_ffn_up=m_w_ffn_up, ffn_conv_w=m_ffn_conv_w, ffn_conv_b=m_ffn_conv_b,
             w_ffn_down=m_w_ffn_down, ple_gate_w=m_ple_gate_w, ple_gate_b=m_ple_gate_b, ple_proj=m_ple_proj,
             ln2_g=m_ln2_g, ln2_b=m_ln2_b)
    v = dict(w_in=v_w_in, attn_sinks=v_attn_sinks, rnn_conv_w=v_rnn_conv_w, rnn_conv_b=v_rnn_conv_b, gate_a_w=v_gate_a_w,
             gate_a_b=v_gate_a_b, gate_x_w=v_gate_x_w, gate_x_b=v_gate_x_b, lru_lambda=v_lru_lambda, w_out=v_w_out,
             ln1_g=v_ln1_g, ln1_b=v_ln1_b, w_ffn_up=v_w_ffn_up, ffn_conv_w=v_ffn_conv_w, ffn_conv_b=v_ffn_conv_b,
             w_ffn_down=v_w_ffn_down, ple_gate_w=v_ple_gate_w, ple_gate_b=v_ple_gate_b, ple_proj=v_ple_proj,
             ln2_g=v_ln2_g, ln2_b=v_ln2_b)
    w, m, v = ({k: a[0] for k, a in d.items()} for d in (w, m, v))
    chip = 2 * lax.axis_index("x") + lax.axis_index("y")
    core = lax.axis_index("c")

    wpack = _pack_big(w)
    cpack, _ = _pack_vecs([w["rnn_conv_w"], w["ffn_conv_w"]])
    gwp, gcp = _gather_weights(wpack.astype(MXU_DTYPE), cpack)
    gw = {}
    for i, key in enumerate(BIG_KEYS):
        a = gwp[:, PACK_OFF[i]:PACK_OFF[i + 1]]
        gw[key] = a.reshape(4 * 256, 256) if key == "w_p_t" else a.reshape(-1, 1024)
    small = {k: w[k] for k in SMALL}
    small["rnn_conv_w"] = gcp[:, 0:4].reshape(4, 4, 128).transpose(1, 0, 2).reshape(4, 512)
    small["ffn_conv_w"] = gcp[:, 4:22].reshape(4, 3, 768).transpose(1, 0, 2).reshape(3, 3072)

    grad_x, big, sg, loss = _layer_grads(x[0], p[0, 0], loss_target[0], gw, small)

    spack, offs = _pack_vecs([sg[k] for k in SMALL] + [loss])
    ssum = _allreduce_small(spack)
    shapes = [sg[k].shape for k in SMALL] + [(1,)]
    red = dict(zip(SMALL + ("loss",), _unpack_vecs(ssum, offs, shapes)))
    red["rnn_conv_w"] = lax.dynamic_slice_in_dim(red["rnn_conv_w"], chip * 128, 128, axis=1)
    red["ffn_conv_w"] = lax.dynamic_slice_in_dim(red["ffn_conv_w"], chip * 768, 768, axis=1)

    parts = []
    for i, key in enumerate(BIG_KEYS):
        parts.append(big[key].reshape(4, PACK_ROWS[i], 1024))
    gpack = jnp.concatenate(parts, axis=1)
    core1 = core.reshape(1).astype(jnp.int32)
    sib = _swap_halves(gpack)
    chip_sum = _add_half(gpack, sib, core1)
    from_chips = _scatter_chips(chip_sum)
    half = _add4(from_chips, "add_chips")
    other_half = _send_half(half)

    gbig, dbig, mbig, vbig = _adamw_halves(wpack, half, other_half, _pack_big(m), _pack_big(v), core1)
    wsm, offs2 = _pack_vecs([w[k] for k in SMALL])
    gsm, _ = _pack_vecs([red[k] for k in SMALL])
    msm, _ = _pack_vecs([m[k] for k in SMALL])
    vsm, _ = _pack_vecs([v[k] for k in SMALL])
    dsm, nmsm, nvsm = _adamw(wsm, gsm, msm, vsm, "adamw_small")
    shapes2 = [w[k].shape for k in SMALL]

    def named(bigp, smallp):
        d = _unpack_big(bigp)
        d.update({k: a[None] for k, a in zip(SMALL, _unpack_vecs(smallp, offs2, shapes2))})
        return [d[k] for k in WEIGHTS]

    grads = named(gbig, gsm)
    return (red["loss"].reshape(()), grad_x[None], *grads, *named(dbig, dsm), *named(mbig, nmsm), *named(vbig, nvsm))
```

```python
import functools

import jax
import jax.numpy as jnp
from jax import lax
from jax.experimental import pallas as pl
from jax.experimental.pallas import tpu as pltpu

F32 = jnp.float32
BF16 = jnp.bfloat16
MXU_DTYPE = jnp.bfloat16

D = 1024
D_ATT = 512
D_KV = 128
D_RNN = 512
D_IN = 1792
D_FF = 3072
FF_CHUNK = 512
PLE = 256
HEADS = 8
HEAD_DIM = 64
BLK = 128
RNN_BLOCKS = 8
LN_EPS = 1e-5
LRU_C = 8.0
ALPHA = float(2.0 ** 0.25)
SCALE = HEAD_DIM ** -0.5
NEG = -1e30

ADAM_LR = 0.001
ADAM_B1 = 0.9
ADAM_B2 = 0.999
ADAM_EPS = 1e-08
ADAM_WD = 0.01
ADAM_STEP = 10

VMEM_LIMIT_BYTES = 56 * 1024 * 1024
MESH = pl.DeviceIdType.MESH

PACK_ROWS = (448, 256, 1536, 768, 256, 64)
PACK_OFF = tuple(sum(PACK_ROWS[:i]) for i in range(len(PACK_ROWS) + 1))
PACK_TOTAL = PACK_OFF[-1]
HALF = PACK_TOTAL // 2


def _params(**kw):
    return pltpu.CompilerParams(vmem_limit_bytes=VMEM_LIMIT_BYTES, **kw)


def _mm(a, b):
    return jnp.dot(a.astype(MXU_DTYPE), b.astype(MXU_DTYPE), preferred_element_type=F32)


def _mm_nt(a, b):
    return lax.dot_general(a.astype(MXU_DTYPE), b.astype(MXU_DTYPE), (((1,), (1,)), ((), ())),
                           preferred_element_type=F32)


def _mm_tn(a, b):
    return lax.dot_general(a.astype(MXU_DTYPE), b.astype(MXU_DTYPE), (((0,), (0,)), ((), ())),
                           preferred_element_type=F32)


def _sigmoid(x):
    return 1.0 / (1.0 + jnp.exp(-x))


def _gelu(x):
    c = 0.7978845608028654
    k = 0.044715
    t = jnp.tanh(c * (x + k * x * x * x))
    g = 0.5 * x * (1.0 + t)
    dg = 0.5 * (1.0 + t) + 0.5 * x * (1.0 - t * t) * c * (1.0 + 3.0 * k * x * x)
    return g, dg


def _expm1(x):
    poly = x * (1.0 + x * (0.5 + x * (1.0 / 6.0 + x * (1.0 / 24.0 + x * (1.0 / 120.0)))))
    return jnp.where(jnp.abs(x) < 0.03, poly, jnp.exp(x) - 1.0)


def _softplus(x):
    return jnp.maximum(x, 0.0) + jnp.log(1.0 + jnp.exp(-jnp.abs(x)))


def _ln(z, g, b):
    mu = jnp.mean(z, axis=-1, keepdims=True)
    zc = z - mu
    var = jnp.mean(zc * zc, axis=-1, keepdims=True)
    rstd = lax.rsqrt(var + LN_EPS)
    xhat = zc * rstd
    return xhat * g + b, xhat, rstd


def _ln_bwd(dy, xhat, rstd, g):
    dxh = dy * g
    m1 = jnp.mean(dxh, axis=-1, keepdims=True)
    m2 = jnp.mean(dxh * xhat, axis=-1, keepdims=True)
    return rstd * (dxh - m1 - xhat * m2)


def _colsum(x):
    return jnp.sum(x, axis=0, keepdims=True)


def _full(shape):
    nd = len(shape)
    return pl.BlockSpec(shape, lambda *_: (0,) * nd)


def _rows(tm, cols, fn=None):
    if fn is None:
        return pl.BlockSpec((tm, cols), lambda i: (i, 0))
    return pl.BlockSpec((tm, cols), lambda i: (fn(i), 0))


def _heads(tm):
    return pl.BlockSpec((HEADS, tm, HEAD_DIM), lambda i: (0, i, 0))


def _in_proj(x, w_in_t):
    T = x.shape[0]
    tm = 512

    def body(x_ref, w_ref, q_ref, kv_ref, xr_ref, gr_ref, xb_ref):
        xb = x_ref[...].astype(MXU_DTYPE)
        xb_ref[...] = xb.astype(BF16)
        q = _mm_nt(xb, w_ref[0:512, :])
        for h in range(HEADS):
            q_ref[h] = q[:, h * 64:(h + 1) * 64].astype(BF16)
        kv_ref[...] = _mm_nt(xb, w_ref[512:768, :]).astype(BF16)
        xr_ref[...] = _mm_nt(xb, w_ref[768:1280, :])
        gr_ref[...] = _mm_nt(xb, w_ref[1280:1792, :])

    return pl.pallas_call(
        body, name="in_proj", grid=(T // tm,),
        in_specs=[_rows(tm, D), _full((D_IN, D))],
        out_specs=[_heads(tm), _rows(tm, 256), _rows(tm, 512), _rows(tm, 512), _rows(tm, D)],
        out_shape=[jax.ShapeDtypeStruct((HEADS, T, 64), BF16), jax.ShapeDtypeStruct((T, 256), BF16),
                   jax.ShapeDtypeStruct((T, 512), F32), jax.ShapeDtypeStruct((T, 512), F32),
                   jax.ShapeDtypeStruct((T, D), BF16)],
        compiler_params=_params(),
    )(x, w_in_t)


def _attn_band(kv_ref, i):
    cur = pl.multiple_of(i * BLK, BLK)
    prev = pl.multiple_of(jnp.maximum(i - 1, 0) * BLK, BLK)
    band = jnp.concatenate([kv_ref[pl.ds(prev, BLK), :], kv_ref[pl.ds(cur, BLK), :]], axis=0)
    key = lax.broadcasted_iota(jnp.int32, (2 * BLK, 4 * BLK), 0)
    qry = lax.broadcasted_iota(jnp.int32, (2 * BLK, 4 * BLK), 1) & (BLK - 1)
    in_prev = jnp.logical_and(jnp.logical_and(key < BLK, key > qry), i > 0)
    mask = jnp.logical_or(in_prev, jnp.logical_and(key >= BLK, key - BLK <= qry))
    return band, mask, cur, prev


def _attn_scores(band, mask, qs, s_ref, g):
    st = jnp.where(mask, _mm_nt(band[:, g * 64:(g + 1) * 64], qs) * SCALE, NEG)
    lane = lax.broadcasted_iota(jnp.int32, (1, 4 * BLK), 1)
    sv = jnp.where(lane < BLK, s_ref[0, 4 * g],
                   jnp.where(lane < 2 * BLK, s_ref[0, 4 * g + 1], jnp.where(lane < 3 * BLK, s_ref[0, 4 * g + 2], s_ref[0, 4 * g + 3])))
    m = jnp.maximum(jnp.max(st, axis=0, keepdims=True), sv)
    p = jnp.exp(st - m)
    ps = jnp.exp(sv - m)
    return p, ps, jnp.sum(p, axis=0, keepdims=True) + ps


def _pos():
    return lax.axis_index("x"), lax.axis_index("y"), lax.axis_index("c")


def _other_chips(x, y):
    return [(1 - x, y), (x, 1 - y), (1 - x, 1 - y)]


def _gather_steps(w_ref, gw_ref, send_sems, recv_sems, local_sem):
    x, y, c = _pos()
    me = 2 * x + y
    chips = _other_chips(x, y)
    half = w_ref.shape[0] // 2
    mine = pl.ds(pl.multiple_of(c * half, 16), half)
    theirs = pl.ds(pl.multiple_of((1 - c) * half, 16), half)
    loc = pltpu.make_async_copy(w_ref, gw_ref.at[me], local_sem)

    def copy(k, src, dst, to):
        return pltpu.make_async_remote_copy(src_ref=src, dst_ref=dst, send_sem=send_sems.at[k], recv_sem=recv_sems.at[k],
                                            device_id=to, device_id_type=MESH)

    def out(k):
        px, py = chips[k]
        return copy(k, w_ref.at[mine], gw_ref.at[me, mine], (px, py, c))

    def fwd(k, rows):
        px, py = chips[k]
        return copy(3 + k, gw_ref.at[2 * px + py, rows], gw_ref.at[2 * px + py, rows], (x, y, 1 - c))

    def start():
        loc.start()
        for k in range(3):
            out(k).start()

    def forward():
        for k in range(3):
            px, py = chips[k]
            copy(k, w_ref.at[mine], gw_ref.at[2 * px + py, mine], (px, py, c)).wait_recv()
            fwd(k, mine).start()

    def finish():
        for k in range(3):
            fwd(k, theirs).wait_recv()
        for k in range(3):
            out(k).wait_send()
            fwd(k, mine).wait_send()
        loc.wait()

    return start, forward, finish


GATHER_SCRATCH = [pltpu.SemaphoreType.DMA((6,)), pltpu.SemaphoreType.DMA((6,)), pltpu.SemaphoreType.DMA]


def _host_gather(body, n_in, n_out, nsteps):
    def wrapped(*refs):
        ins, w_ref = refs[:n_in], refs[n_in]
        outs, gw_ref = refs[n_in + 1:n_in + 1 + n_out], refs[n_in + 1 + n_out]
        scratch, sems = refs[n_in + 2 + n_out:-3], refs[-3:]
        start, forward, finish = _gather_steps(w_ref, gw_ref, *sems)
        i = pl.program_id(0)
        pl.when(i == 0)(start)
        body(*ins, *outs, *scratch)
        pl.when(i == nsteps // 2)(forward)
        pl.when(i == nsteps - 1)(finish)

    return wrapped


def _attn_fwd(q, kv, sinks, wsrc=None):
    T = kv.shape[0]

    def body(q_ref, kv_ref, s_ref, o_ref):
        i = pl.program_id(0)
        band, mask, _, _ = _attn_band(kv_ref, i)
        for g in range(2):
            qs = q_ref[4 * g:4 * g + 4].reshape(4 * BLK, HEAD_DIM)
            p, _, den = _attn_scores(band, mask, qs, s_ref, g)
            ot = _mm_tn(band[:, 128:256], p) / den
            for hh in range(4):
                o = ot[:, hh * BLK:(hh + 1) * BLK].T
                o_ref[:, (4 * g + hh) * 64:(4 * g + hh + 1) * 64] = o[:, g * 64:(g + 1) * 64].astype(BF16)

    in_specs = [_heads(BLK), _full((T, 256)), pl.BlockSpec(memory_space=pltpu.SMEM)]
    out_specs = [_rows(BLK, 512)]
    out_shape = [jax.ShapeDtypeStruct((T, 512), BF16)]
    if wsrc is None:
        return pl.pallas_call(body, name="attn_fwd", grid=(T // BLK,), in_specs=in_specs, out_specs=out_specs,
                              out_shape=out_shape, compiler_params=_params())(q, kv, sinks)[0], None
    anyspec = pl.BlockSpec(memory_space=pl.ANY)
    return pl.pallas_call(
        _host_gather(body, 3, 1, T // BLK), name="attn_fwd_gather", grid=(T // BLK,),
        in_specs=in_specs + [anyspec], out_specs=out_specs + [anyspec],
        out_shape=out_shape + [jax.ShapeDtypeStruct((4,) + wsrc.shape, wsrc.dtype)],
        scratch_shapes=GATHER_SCRATCH, compiler_params=_params(has_side_effects=True),
    )(q, kv, sinks, wsrc)


def _attn_bwd(q, kv, do, sinks):
    T = kv.shape[0]

    def body(q_ref, kv_ref, do_ref, s_ref, dq_ref, dkv_ref, ds_ref):
        i = pl.program_id(0)
        band, mask, cur, prev = _attn_band(kv_ref, i)

        @pl.when(i == 0)
        def _():
            ds_ref[...] = jnp.zeros_like(ds_ref)

        for g in range(2):
            qs = q_ref[4 * g:4 * g + 4].reshape(4 * BLK, HEAD_DIM)
            dos = do_ref[4 * g:4 * g + 4].reshape(4 * BLK, HEAD_DIM)
            p, ps, den = _attn_scores(band, mask, qs, s_ref, g)
            inv = 1.0 / den
            p = p * inv
            dpt = _mm_nt(band[:, 128 + g * 64:192 + g * 64], dos)
            delta = jnp.sum(p * dpt, axis=0, keepdims=True)
            dst = p * (dpt - delta)
            dsv = -(ps * inv) * delta
            for hh in range(4):
                dsink = jnp.sum(dsv[:, hh * BLK:(hh + 1) * BLK], axis=1, keepdims=True)
                ds_ref[4 * g + hh:4 * g + hh + 1, :] += jnp.broadcast_to(dsink, (1, 128))
            dqt = _mm_tn(band[:, 0:128], dst) * SCALE
            for hh in range(4):
                dqh = dqt[:, hh * BLK:(hh + 1) * BLK].T
                dq_ref[:, (4 * g + hh) * 64:(4 * g + hh + 1) * 64] = dqh[:, g * 64:(g + 1) * 64].astype(BF16)
            dk = _mm(dst, qs) * SCALE
            dv = _mm(p, dos)
            dkv_ref[pl.ds(cur, BLK), g * 64:(g + 1) * 64] = dk[BLK:2 * BLK]
            dkv_ref[pl.ds(cur, BLK), 128 + g * 64:192 + g * 64] = dv[BLK:2 * BLK]
            dkv_ref[pl.ds(prev, BLK), g * 64:(g + 1) * 64] += dk[0:BLK]
            dkv_ref[pl.ds(prev, BLK), 128 + g * 64:192 + g * 64] += dv[0:BLK]

    return pl.pallas_call(
        body, name="attn_bwd", grid=(T // BLK,),
        in_specs=[_heads(BLK), _full((T, 256)), _heads(BLK), pl.BlockSpec(memory_space=pltpu.SMEM)],
        out_specs=[_rows(BLK, 512), _full((T, 256)), _full((8, 128))],
        out_shape=[jax.ShapeDtypeStruct((T, 512), BF16), jax.ShapeDtypeStruct((T, 256), F32),
                   jax.ShapeDtypeStruct((8, 128), F32)],
        compiler_params=_params(),
    )(q, kv, do, sinks)


def _rows8(tm, cols):
    return lax.broadcasted_iota(jnp.int32, (tm, cols), 0) & 7


def _lru_gates(xc, wa, ba, wx, bx, lam):
    r = _sigmoid(_mm(xc, wa) + ba)
    ii = _sigmoid(_mm(xc, wx) + bx)
    sp = _softplus(-lam)
    la = -LRU_C * r * sp
    a = jnp.exp(la)
    m = jnp.sqrt(-_expm1(2.0 * la))
    return r, ii, sp, a, m


def _rnn_fwd(xr, gr, cw, cb, wa, ba, wx, bx, lam, wsrc=None):
    T = xr.shape[0]
    tm = 256
    C = D_RNN

    def body(xr_ref, gr_ref, cw_ref, cb_ref, wa_ref, ba_ref, wx_ref, bx_ref, lam_ref,
             xc_ref, h_ref, rec_ref, ext, a_s, b_s, carry):
        i = pl.program_id(0)

        @pl.when(i == 0)
        def _():
            ext[0:8, :] = jnp.zeros((8, C), F32)
            carry[...] = jnp.zeros((8, C), F32)

        ext[8:8 + tm, :] = xr_ref[...]
        xc = cb_ref[...] + cw_ref[3:4, :] * ext[8:8 + tm, :]
        for k in range(3):
            xc = xc + cw_ref[k:k + 1, :] * ext[5 + k:5 + k + tm, :]
        ext[0:8, :] = ext[tm:tm + 8, :]
        xc_ref[...] = xc
        _, ii, _, a, m = _lru_gates(xc, wa_ref[...], ba_ref[...], wx_ref[...], bx_ref[...], lam_ref[...])
        b = m * ii * xc
        r8 = _rows8(tm, C)
        for d in (1, 2, 4):
            ok = r8 >= d
            a_sh = jnp.where(ok, pltpu.roll(a, d, 0), 1.0)
            b_sh = jnp.where(ok, pltpu.roll(b, d, 0), 0.0)
            b = a * b_sh + b
            a = a * a_sh
        a_s[...] = a
        b_s[...] = b

        def step(g, hin):
            s = pl.multiple_of(g * 8, 8)
            hg = a_s[pl.ds(s, 8), :] * hin + b_s[pl.ds(s, 8), :]
            h_ref[pl.ds(s, 8), :] = hg
            return jnp.broadcast_to(hg[7:8, :], (8, C))

        carry[...] = lax.fori_loop(0, tm // 8, step, carry[...])
        ge, _ = _gelu(gr_ref[...])
        rec_ref[...] = (h_ref[...] * ge).astype(BF16)

    vec = _full((1, C))
    in_specs = [_rows(tm, C), _rows(tm, C), _full((4, C)), vec, _full((C, C)), vec, _full((C, C)), vec, vec]
    out_specs = [_rows(tm, C), _rows(tm, C), _rows(tm, C)]
    out_shape = [jax.ShapeDtypeStruct((T, C), F32), jax.ShapeDtypeStruct((T, C), F32), jax.ShapeDtypeStruct((T, C), BF16)]
    scratch = [pltpu.VMEM((tm + 8, C), F32), pltpu.VMEM((tm, C), F32), pltpu.VMEM((tm, C), F32), pltpu.VMEM((8, C), F32)]
    args = (xr, gr, cw, cb, wa, ba, wx, bx, lam)
    if wsrc is None:
        return (*pl.pallas_call(body, name="rnn_fwd", grid=(T // tm,), in_specs=in_specs, out_specs=out_specs,
                                out_shape=out_shape, scratch_shapes=scratch, compiler_params=_params())(*args), None)
    anyspec = pl.BlockSpec(memory_space=pl.ANY)
    return pl.pallas_call(
        _host_gather(body, 9, 3, T // tm), name="rnn_fwd_gather", grid=(T // tm,),
        in_specs=in_specs + [anyspec], out_specs=out_specs + [anyspec],
        out_shape=out_shape + [jax.ShapeDtypeStruct((4,) + wsrc.shape, wsrc.dtype)],
        scratch_shapes=scratch + GATHER_SCRATCH, compiler_params=_params(has_side_effects=True),
    )(*args, wsrc)


def _rnn_bwd(drec, gr, h, xc, xr, cw, wa, ba, wx, bx, lam):
    T = xr.shape[0]
    tm = 256
    C = D_RNN
    nt = T // tm
    t8 = tm // 8

    def body(drec_ref, gr_ref, h_ref, hp_ref, xc_ref, xr_ref, xrp_ref, cw_ref, wa_ref, ba_ref, wx_ref, bx_ref,
             lam_ref, dxr_ref, dgr_ref, dwa_ref, dwx_ref, dvec_ref, c_s, g_s, gout, ext, xext, anext, gcarry):
        i = pl.program_id(0)
        j = nt - 1 - i

        @pl.when(i == 0)
        def _():
            dwa_ref[...] = jnp.zeros_like(dwa_ref)
            dwx_ref[...] = jnp.zeros_like(dwx_ref)
            dvec_ref[...] = jnp.zeros_like(dvec_ref)
            anext[...] = jnp.zeros((8, C), F32)
            gcarry[...] = jnp.zeros((8, C), F32)
            ext[tm:tm + 8, :] = jnp.zeros((8, C), F32)

        xc = xc_ref[...]
        lam = lam_ref[...]
        r, ii, sp, a, m = _lru_gates(xc, wa_ref[...], ba_ref[...], wx_ref[...], bx_ref[...], lam)
        ge, dge = _gelu(gr_ref[...])
        drec = drec_ref[...]
        hh = h_ref[...]
        dgr_ref[...] = (drec * hh * dge).astype(BF16)
        dh = drec * ge
        rowi = lax.broadcasted_iota(jnp.int32, (tm, C), 0)
        c = jnp.where(rowi == tm - 1, jnp.broadcast_to(anext[0:1, :], (tm, C)), pltpu.roll(a, tm - 1, 0))
        anext[...] = a[0:8, :]
        r8 = rowi & 7
        gg = dh
        for d in (1, 2, 4):
            ok = r8 < 8 - d
            c_sh = jnp.where(ok, pltpu.roll(c, tm - d, 0), 1.0)
            g_sh = jnp.where(ok, pltpu.roll(gg, tm - d, 0), 0.0)
            gg = c * g_sh + gg
            c = c * c_sh
        c_s[...] = c
        g_s[...] = gg

        def step(k, gin):
            s = pl.multiple_of((t8 - 1 - k) * 8, 8)
            og = c_s[pl.ds(s, 8), :] * gin + g_s[pl.ds(s, 8), :]
            gout[pl.ds(s, 8), :] = og
            return jnp.broadcast_to(og[0:1, :], (8, C))

        gcarry[...] = lax.fori_loop(0, t8, step, gcarry[...])
        G = gout[...]
        hprev_row = jnp.where(j > 0, hp_ref[7:8, :], 0.0)
        hprev = jnp.where(rowi == 0, jnp.broadcast_to(hprev_row, (tm, C)), pltpu.roll(hh, 1, 0))
        da = G * hprev
        dm = G * ii * xc
        di = G * m * xc
        dxc = G * m * ii
        dla = da * a - dm * a * a / m
        dr = dla * (-LRU_C * sp)
        dsp = _colsum(dla * (-LRU_C * r))
        dlam = dsp * (-_sigmoid(-lam))
        dpr = dr * r * (1.0 - r)
        dpi = di * ii * (1.0 - ii)
        dxc = dxc + _mm_nt(dpr, wa_ref[...]) + _mm_nt(dpi, wx_ref[...])
        dwa_ref[...] += _mm_tn(xc, dpr)
        dwx_ref[...] += _mm_tn(xc, dpi)
        dvec_ref[0:1, :] += _colsum(dpr)
        dvec_ref[1:2, :] += _colsum(dpi)
        dvec_ref[2:3, :] += dlam
        dvec_ref[3:4, :] += _colsum(dxc)
        ext[0:tm, :] = dxc
        dxr = cw_ref[3:4, :] * dxc
        for k in range(3):
            dxr = dxr + cw_ref[k:k + 1, :] * ext[3 - k:3 - k + tm, :]
        ext[tm:tm + 8, :] = dxc[0:8, :]
        dxr_ref[...] = dxr.astype(BF16)
        xext[0:8, :] = jnp.where(j > 0, xrp_ref[...], 0.0)
        xext[8:8 + tm, :] = xr_ref[...]
        for k in range(4):
            dvec_ref[4 + k:5 + k, :] += _colsum(dxc * xext[5 + k:5 + k + tm, :])

    rev = lambda i: nt - 1 - i
    prev8 = lambda i: jnp.maximum((nt - 1 - i) * t8 - 1, 0)
    vec = _full((1, C))
    return pl.pallas_call(
        body, name="rnn_bwd", grid=(nt,),
        in_specs=[_rows(tm, C, rev), _rows(tm, C, rev), _rows(tm, C, rev), _rows(8, C, prev8), _rows(tm, C, rev),
                  _rows(tm, C, rev), _rows(8, C, prev8), _full((4, C)), _full((C, C)), vec, _full((C, C)), vec, vec],
        out_specs=[_rows(tm, C, rev), _rows(tm, C, rev), _full((C, C)), _full((C, C)), _full((8, C))],
        out_shape=[jax.ShapeDtypeStruct((T, C), BF16), jax.ShapeDtypeStruct((T, C), BF16),
                   jax.ShapeDtypeStruct((C, C), F32), jax.ShapeDtypeStruct((C, C), F32),
                   jax.ShapeDtypeStruct((8, C), F32)],
        scratch_shapes=[pltpu.VMEM((tm, C), F32), pltpu.VMEM((tm, C), F32), pltpu.VMEM((tm, C), F32),
                        pltpu.VMEM((tm + 8, C), F32), pltpu.VMEM((tm + 8, C), F32), pltpu.VMEM((8, C), F32),
                        pltpu.VMEM((8, C), F32)],
        compiler_params=_params(),
    )(drec, gr, h, h, xc, xr, xr, cw, wa, ba, wx, bx, lam)


def _out_proj(att, rec, x, w_out, g1, b1):
    T = x.shape[0]
    tm = 512

    def body(att_ref, rec_ref, x_ref, w_ref, g1_ref, b1_ref, z_ref, h_ref):
        mix = _mm(att_ref[...], w_ref[0:512, :]) + _mm(rec_ref[...], w_ref[512:1024, :])
        z1 = ALPHA * x_ref[...] + mix
        z_ref[...] = z1
        h1, _, _ = _ln(z1, g1_ref[...], b1_ref[...])
        h_ref[...] = h1.astype(MXU_DTYPE).astype(BF16)

    return pl.pallas_call(
        body, name="out_proj", grid=(T // tm,),
        in_specs=[_rows(tm, 512), _rows(tm, 512), _rows(tm, D), _full((D, D)), _full((1, D)), _full((1, D))],
        out_specs=[_rows(tm, D), _rows(tm, D)],
        out_shape=[jax.ShapeDtypeStruct((T, D), F32), jax.ShapeDtypeStruct((T, D), BF16)],
        compiler_params=_params(),
    )(att, rec, x, w_out, g1, b1)


NC = D_FF // FF_CHUNK


def _ffn_up(h1b, w_up_t, fcw, fcb):
    T = h1b.shape[0]
    tm = 512
    CW = FF_CHUNK

    def body(h_ref, wg_ref, wv_ref, fcw_ref, fcb_ref, gate_ref, val_ref, act_ref, ext):
        i = pl.program_id(1)

        @pl.when(i == 0)
        def _():
            ext[0:8, :] = jnp.zeros((8, CW), F32)

        hb = h_ref[...]
        gate = _mm_nt(hb, wg_ref[...])
        val = _mm_nt(hb, wv_ref[...])
        gate_ref[...] = gate
        val_ref[...] = val
        ext[8:8 + tm, :] = gate
        gc = (fcb_ref[...] + fcw_ref[0:1, :] * ext[6:6 + tm, :] + fcw_ref[1:2, :] * ext[7:7 + tm, :]
              + fcw_ref[2:3, :] * gate)
        ext[0:8, :] = ext[tm:tm + 8, :]
        ge, _ = _gelu(gc)
        act_ref[...] = (ge * val).astype(BF16)

    chunk = pl.BlockSpec((None, tm, CW), lambda c, i: (c, i, 0))
    return pl.pallas_call(
        body, name="ffn_up", grid=(NC, T // tm),
        in_specs=[pl.BlockSpec((tm, D), lambda c, i: (i, 0)), pl.BlockSpec((CW, D), lambda c, i: (c, 0)),
                  pl.BlockSpec((CW, D), lambda c, i: (NC + c, 0)), pl.BlockSpec((None, 3, CW), lambda c, i: (c, 0, 0)),
                  pl.BlockSpec((None, 1, CW), lambda c, i: (c, 0, 0))],
        out_specs=[chunk, chunk, chunk],
        out_shape=[jax.ShapeDtypeStruct((NC, T, CW), F32), jax.ShapeDtypeStruct((NC, T, CW), F32),
                   jax.ShapeDtypeStruct((NC, T, CW), BF16)],
        scratch_shapes=[pltpu.VMEM((tm + 8, CW), F32)],
        compiler_params=_params(),
    )(h1b, w_up_t, w_up_t, fcw, fcb)


def _ffn_down(act, z1, p, tgt, w_down, w_g, w_p_t, g1, b1, g2, b2, bg):
    T = z1.shape[0]
    tm = 256

    def body(act_ref, z_ref, p_ref, t_ref, wdn_hbm, wg_hbm, wp_hbm, g1_ref, b1_ref, g2_ref, b2_ref, bg_ref,
             dz2_ref, dz2b_ref, dpre_ref, dpp_ref, vec_ref, wdn, wg, wp):
        @pl.when(pl.program_id(0) == 0)
        def _():
            pltpu.sync_copy(wdn_hbm, wdn)
            pltpu.sync_copy(wg_hbm, wg)
            pltpu.sync_copy(wp_hbm, wp)
            vec_ref[...] = jnp.zeros_like(vec_ref)

        g2v = g2_ref[...]
        h1, _, _ = _ln(z_ref[...], g1_ref[...], b1_ref[...])
        h1b = h1.astype(MXU_DTYPE)
        ffn = _mm(act_ref[0], wdn[0:FF_CHUNK, :])
        for c in range(1, NC):
            ffn = ffn + _mm(act_ref[c], wdn[c * FF_CHUNK:(c + 1) * FF_CHUNK, :])
        sg = _sigmoid(_mm(h1b, wg[...]) + bg_ref[...])
        pp = _mm_nt(p_ref[...], wp[...])
        z2 = ALPHA * h1 + ffn + sg * pp
        y, xh2, rstd2 = _ln(z2, g2v, b2_ref[...])
        diff = y - t_ref[...]
        dy = diff * (1.0 / D)
        dz2 = _ln_bwd(dy, xh2, rstd2, g2v)
        dpre = dz2 * pp * sg * (1.0 - sg)
        dz2_ref[...] = dz2
        dz2b_ref[...] = dz2.astype(BF16)
        dpre_ref[...] = dpre.astype(BF16)
        dpp_ref[...] = (dz2 * sg).astype(BF16)
        loss = 0.5 * jnp.sum(jnp.sum(diff * diff, axis=1, keepdims=True), axis=0, keepdims=True) * (1.0 / D)
        vec_ref[0:1, :] += jnp.broadcast_to(loss, (1, D))
        vec_ref[1:2, :] += _colsum(dy * xh2)
        vec_ref[2:3, :] += _colsum(dy)
        vec_ref[3:4, :] += _colsum(dpre)

    anyspec = pl.BlockSpec(memory_space=pl.ANY)
    vec = _full((1, D))
    return pl.pallas_call(
        body, name="ffn_down", grid=(T // tm,),
        in_specs=[pl.BlockSpec((NC, tm, FF_CHUNK), lambda i: (0, i, 0)), _rows(tm, D), _rows(tm, PLE), _rows(tm, D),
                  anyspec, anyspec, anyspec] + [vec] * 5,
        out_specs=[_rows(tm, D)] * 4 + [_full((8, D))],
        out_shape=[jax.ShapeDtypeStruct((T, D), F32)] + [jax.ShapeDtypeStruct((T, D), BF16)] * 3
                  + [jax.ShapeDtypeStruct((8, D), F32)],
        scratch_shapes=[pltpu.VMEM((D_FF, D), MXU_DTYPE), pltpu.VMEM((D, D), MXU_DTYPE), pltpu.VMEM((D, PLE), MXU_DTYPE)],
        compiler_params=_params(),
    )(act, z1, p, tgt, w_down, w_g, w_p_t, g1, b1, g2, b2, bg)


def _ffn_bwd(dz2b, gate, val, w_down, fcw, fcb):
    T = dz2b.shape[0]
    tm = 512
    CW = FF_CHUNK
    nt = T // tm
    t8 = tm // 8

    def body(dz_ref, wdn_ref, gate_ref, gp_ref, val_ref, fcw_ref, fcb_ref, dup_ref, dfc_ref, gext, dext):
        i = pl.program_id(1)
        j = nt - 1 - i

        @pl.when(i == 0)
        def _():
            dext[tm:tm + 8, :] = jnp.zeros((8, CW), F32)
            dfc_ref[...] = jnp.zeros_like(dfc_ref)

        gate = gate_ref[...]
        gext[0:8, :] = jnp.where(j > 0, gp_ref[...], 0.0)
        gext[8:8 + tm, :] = gate
        gate1 = gext[7:7 + tm, :]
        gate2 = gext[6:6 + tm, :]
        gc = fcb_ref[...] + fcw_ref[0:1, :] * gate2 + fcw_ref[1:2, :] * gate1 + fcw_ref[2:3, :] * gate
        ge, dge = _gelu(gc)
        dact = _mm_nt(dz_ref[...], wdn_ref[...])
        dgc = dact * val_ref[...] * dge
        dext[0:tm, :] = dgc
        dgate = fcw_ref[2:3, :] * dgc + fcw_ref[1:2, :] * dext[1:1 + tm, :] + fcw_ref[0:1, :] * dext[2:2 + tm, :]
        dext[tm:tm + 8, :] = dgc[0:8, :]
        dup_ref[0] = dgate.astype(BF16)
        dup_ref[1] = (dact * ge).astype(BF16)
        dfc_ref[0:1, :] += _colsum(dgc * gate2)
        dfc_ref[1:2, :] += _colsum(dgc * gate1)
        dfc_ref[2:3, :] += _colsum(dgc * gate)
        dfc_ref[3:4, :] += _colsum(dgc)

    rev = lambda c, i: (c, nt - 1 - i, 0)
    return pl.pallas_call(
        body, name="ffn_bwd", grid=(NC, nt),
        in_specs=[pl.BlockSpec((tm, D), lambda c, i: (nt - 1 - i, 0)), pl.BlockSpec((CW, D), lambda c, i: (c, 0)),
                  pl.BlockSpec((None, tm, CW), rev),
                  pl.BlockSpec((None, 8, CW), lambda c, i: (c, jnp.maximum((nt - 1 - i) * t8 - 1, 0), 0)),
                  pl.BlockSpec((None, tm, CW), rev), pl.BlockSpec((None, 3, CW), lambda c, i: (c, 0, 0)),
                  pl.BlockSpec((None, 1, CW), lambda c, i: (c, 0, 0))],
        out_specs=[pl.BlockSpec((None, 2, tm, CW), lambda c, i: (c, 0, nt - 1 - i, 0)),
                   pl.BlockSpec((None, 8, CW), lambda c, i: (c, 0, 0))],
        out_shape=[jax.ShapeDtypeStruct((NC, 2, T, CW), BF16), jax.ShapeDtypeStruct((NC, 8, CW), F32)],
        scratch_shapes=[pltpu.VMEM((tm + 8, CW), F32), pltpu.VMEM((tm + 8, CW), F32)],
        compiler_params=_params(),
    )(dz2b, w_down, gate, gate, val, fcw, fcb)


def _ffn_dh1(dup, dz2, dpre, z1, w_up_t, w_g, g1, b1):
    T = z1.shape[0]
    tm = 256

    def body(dup_ref, dz2_ref, dpre_ref, z_ref, wup_hbm, wg_hbm, g1_ref, b1_ref, dz1_ref, vec_ref, wup, wg):
        @pl.when(pl.program_id(0) == 0)
        def _():
            pltpu.sync_copy(wup_hbm, wup)
            pltpu.sync_copy(wg_hbm, wg)
            vec_ref[...] = jnp.zeros_like(vec_ref)

        g1v = g1_ref[...]
        _, xh1, rstd1 = _ln(z_ref[...], g1v, b1_ref[...])
        dh1 = ALPHA * dz2_ref[...] + _mm_nt(dpre_ref[...], wg[...])
        for c in range(NC):
            for s in range(2):
                r0 = s * D_FF + c * FF_CHUNK
                dh1 = dh1 + _mm(dup_ref[c, s], wup[r0:r0 + FF_CHUNK, :])
        dz1_ref[...] = _ln_bwd(dh1, xh1, rstd1, g1v)
        vec_ref[0:1, :] += _colsum(dh1 * xh1)
        vec_ref[1:2, :] += _colsum(dh1)

    anyspec = pl.BlockSpec(memory_space=pl.ANY)
    vec = _full((1, D))
    return pl.pallas_call(
        body, name="ffn_dh1", grid=(T // tm,),
        in_specs=[pl.BlockSpec((NC, 2, tm, FF_CHUNK), lambda i: (0, 0, i, 0)), _rows(tm, D), _rows(tm, D), _rows(tm, D),
                  anyspec, anyspec, vec, vec],
        out_specs=[_rows(tm, D), _full((8, D))],
        out_shape=[jax.ShapeDtypeStruct((T, D), F32), jax.ShapeDtypeStruct((8, D), F32)],
        scratch_shapes=[pltpu.VMEM((2 * D_FF, D), MXU_DTYPE), pltpu.VMEM((D, D), MXU_DTYPE)],
        compiler_params=_params(),
    )(dup, dz2, dpre, z1, w_up_t, w_g, g1, b1)


def _out_proj_bwd(dz1, w_out):
    T = dz1.shape[0]
    tm = 512

    def body(dz_ref, w_ref, datt_ref, drec_ref, dzb_ref):
        dzb = dz_ref[...].astype(MXU_DTYPE)
        dzb_ref[...] = dzb.astype(BF16)
        datt = _mm_nt(dzb, w_ref[0:512, :])
        for h in range(HEADS):
            datt_ref[h] = datt[:, h * 64:(h + 1) * 64].astype(BF16)
        drec_ref[...] = _mm_nt(dzb, w_ref[512:1024, :])

    return pl.pallas_call(
        body, name="out_proj_bwd", grid=(T // tm,),
        in_specs=[_rows(tm, D), _full((D, D))],
        out_specs=[_heads(tm), _rows(tm, 512), _rows(tm, D)],
        out_shape=[jax.ShapeDtypeStruct((HEADS, T, 64), BF16), jax.ShapeDtypeStruct((T, 512), F32),
                   jax.ShapeDtypeStruct((T, D), BF16)],
        compiler_params=_params(),
    )(dz1, w_out)


def _in_proj_bwd(dq, dkv, dxr, dgr, dz1, w_in_t):
    T = dz1.shape[0]
    tm = 512

    def body(dq_ref, dkv_ref, dxr_ref, dgr_ref, dz_ref, w_ref, dx_ref, du_ref):
        dkv = dkv_ref[...].astype(BF16)
        dx_ref[...] = (ALPHA * dz_ref[...] + _mm(dq_ref[...], w_ref[0:512, :]) + _mm(dkv, w_ref[512:768, :])
                       + _mm(dxr_ref[...], w_ref[768:1280, :]) + _mm(dgr_ref[...], w_ref[1280:1792, :]))
        du_ref[:, 0:512] = dq_ref[...]
        du_ref[:, 512:768] = dkv
        du_ref[:, 768:1280] = dxr_ref[...]
        du_ref[:, 1280:1792] = dgr_ref[...]

    return pl.pallas_call(
        body, name="in_proj_bwd", grid=(T // tm,),
        in_specs=[_rows(tm, 512), _rows(tm, 256), _rows(tm, 512), _rows(tm, 512), _rows(tm, D), _full((D_IN, D))],
        out_specs=[_rows(tm, D), _rows(tm, D_IN)],
        out_shape=[jax.ShapeDtypeStruct((T, D), F32), jax.ShapeDtypeStruct((T, D_IN), BF16)],
        compiler_params=_params(),
    )(dq, dkv, dxr, dgr, dz1, w_in_t)


def _weight_grad(a, b, bm, name, out_block=lambda m: m):
    bt = min(2048, b.shape[0])
    if a.ndim == 3:
        assert a.shape[2] == bm
        T, M = a.shape[1], a.shape[0] * bm
        a_spec = pl.BlockSpec((None, bt, bm), lambda m, k: (m, k, 0))
    else:
        T, M = a.shape
        a_spec = pl.BlockSpec((bt, bm), lambda m, k: (k, m))
    N = b.shape[1]
    nk = T // bt

    def body(a_ref, b_ref, o_ref):
        k = pl.program_id(1)

        @pl.when(k == 0)
        def _():
            o_ref[...] = jnp.zeros_like(o_ref)

        o_ref[...] += _mm_tn(a_ref[...], b_ref[...])

    return pl.pallas_call(
        body, name=name, grid=(M // bm, nk),
        in_specs=[a_spec, pl.BlockSpec((bt, N), lambda m, k: (k, 0))],
        out_specs=pl.BlockSpec((bm, N), lambda m, k: (out_block(m), 0)),
        out_shape=jax.ShapeDtypeStruct((M, N), F32),
        compiler_params=_params(),
    )(a, b)


def _adamw(w, g, m, v, name):
    R, C = w.shape
    tr = R // 8 if R % 64 == 0 else R
    c1 = 1.0 / (1.0 - ADAM_B1 ** ADAM_STEP)
    c2 = 1.0 / (1.0 - ADAM_B2 ** ADAM_STEP)

    def body(w_ref, g_ref, m_ref, v_ref, d_ref, nm_ref, nv_ref):
        g = g_ref[...]
        nm = ADAM_B1 * m_ref[...] + (1.0 - ADAM_B1) * g
        nv = ADAM_B2 * v_ref[...] + (1.0 - ADAM_B2) * g * g
        nm_ref[...] = nm
        nv_ref[...] = nv
        d_ref[...] = -ADAM_LR * ((nm * c1) / (jnp.sqrt(nv * c2) + ADAM_EPS) + ADAM_WD * w_ref[...])

    spec = pl.BlockSpec((tr, C), lambda i: (i, 0))
    return pl.pallas_call(
        body, name=name, grid=(R // tr,),
        in_specs=[spec] * 4, out_specs=[spec] * 3,
        out_shape=[jax.ShapeDtypeStruct((R, C), F32)] * 3,
        compiler_params=_params(),
    )(w, g, m, v)


def _adamw_halves(w, mine, sib, m, v, c):
    tr = 416
    nb = HALF // tr
    c1 = 1.0 / (1.0 - ADAM_B1 ** ADAM_STEP)
    c2 = 1.0 / (1.0 - ADAM_B2 ** ADAM_STEP)

    def body(c_ref, w_ref, a_ref, b_ref, m_ref, v_ref, g_ref, d_ref, nm_ref, nv_ref):
        own = (pl.program_id(0) // nb) == c_ref[0]
        g = jnp.where(own, a_ref[...], b_ref[...])
        nm = ADAM_B1 * m_ref[...] + (1.0 - ADAM_B1) * g
        nv = ADAM_B2 * v_ref[...] + (1.0 - ADAM_B2) * g * g
        g_ref[...] = g
        nm_ref[...] = nm
        nv_ref[...] = nv
        d_ref[...] = -ADAM_LR * ((nm * c1) / (jnp.sqrt(nv * c2) + ADAM_EPS) + ADAM_WD * w_ref[...])

    full = pl.BlockSpec((tr, 1024), lambda i, c_ref: (i, 0))
    half = pl.BlockSpec((tr, 1024), lambda i, c_ref: (i % nb, 0))
    grid_spec = pltpu.PrefetchScalarGridSpec(num_scalar_prefetch=1, grid=(2 * nb,),
                                             in_specs=[full, half, half, full, full], out_specs=[full] * 4)
    return pl.pallas_call(body, name="adamw_big", grid_spec=grid_spec,
                          out_shape=[jax.ShapeDtypeStruct((PACK_TOTAL, 1024), F32)] * 4,
                          compiler_params=_params())(c, w, mine, sib, m, v)


def _add4(a, name):
    _, R, C = a.shape
    tr = 416

    def body(a_ref, o_ref):
        o_ref[...] = ((a_ref[0].astype(F32) + a_ref[1].astype(F32)) + a_ref[2].astype(F32)) + a_ref[3].astype(F32)

    return pl.pallas_call(body, name=name, grid=(R // tr,),
                          in_specs=[pl.BlockSpec((4, tr, C), lambda i: (0, i, 0))],
                          out_specs=pl.BlockSpec((tr, C), lambda i: (i, 0)),
                          out_shape=jax.ShapeDtypeStruct((R, C), F32), compiler_params=_params())(a)


def _gather_first(wsrc, cpack):
    def body(w_ref, c_ref, gw_ref, gc_ref, send_sems, recv_sems, local_sem, csend, crecv, clocal):
        x, y, c = _pos()
        me = 2 * x + y
        chips = _other_chips(x, y)
        start, forward, finish = _gather_steps(w_ref, gw_ref, send_sems, recv_sems, local_sem)
        start()
        loc = pltpu.make_async_copy(c_ref, gc_ref.at[me], clocal)
        loc.start()

        def conv_copy(k, slot):
            px, py = chips[k]
            return pltpu.make_async_remote_copy(src_ref=c_ref, dst_ref=gc_ref.at[slot], send_sem=csend.at[k],
                                                recv_sem=crecv.at[k], device_id=(px, py, c), device_id_type=MESH)

        for k in range(3):
            conv_copy(k, me).start()
        forward()
        finish()
        for k, (px, py) in enumerate(chips):
            conv_copy(k, 2 * px + py).wait_recv()
        for k in range(3):
            conv_copy(k, me).wait_send()
        loc.wait()

    anyspec = pl.BlockSpec(memory_space=pl.ANY)
    return pl.pallas_call(
        body, name="gather_first",
        in_specs=[anyspec, anyspec], out_specs=[anyspec, anyspec],
        out_shape=[jax.ShapeDtypeStruct((4,) + wsrc.shape, wsrc.dtype), jax.ShapeDtypeStruct((4,) + cpack.shape, cpack.dtype)],
        scratch_shapes=GATHER_SCRATCH + [pltpu.SemaphoreType.DMA((3,)), pltpu.SemaphoreType.DMA((3,)), pltpu.SemaphoreType.DMA],
        compiler_params=_params(has_side_effects=True),
    )(wsrc, cpack)


def _allreduce_small(s):
    R = s.shape[0]

    def body(s_ref, o_ref, buf, send_sems, recv_sems):
        x, y, c = _pos()
        me = 4 * x + 2 * y + c
        buf[me] = s_ref[...]
        sends = []
        for k in range(1, 8):
            peer = (x ^ (k >> 2), y ^ ((k >> 1) & 1), c ^ (k & 1))
            cp = pltpu.make_async_remote_copy(src_ref=s_ref, dst_ref=buf.at[me], send_sem=send_sems.at[k - 1],
                                              recv_sem=recv_sems.at[k - 1], device_id=peer, device_id_type=MESH)
            cp.start()
            sends.append(cp)
        for k in range(1, 8):
            px, py, pc = x ^ (k >> 2), y ^ ((k >> 1) & 1), c ^ (k & 1)
            pltpu.make_async_remote_copy(src_ref=s_ref, dst_ref=buf.at[4 * px + 2 * py + pc], send_sem=send_sems.at[k - 1],
                                         recv_sem=recv_sems.at[k - 1], device_id=(px, py, pc),
                                         device_id_type=MESH).wait_recv()
        for cp in sends:
            cp.wait_send()
        acc = buf[0]
        for d in range(1, 8):
            acc = acc + buf[d]
        o_ref[...] = acc

    vm = pl.BlockSpec(memory_space=pltpu.VMEM)
    return pl.pallas_call(
        body, name="allreduce_small", in_specs=[vm], out_specs=vm,
        out_shape=jax.ShapeDtypeStruct((R, 128), F32),
        scratch_shapes=[pltpu.VMEM((8, R, 128), F32), pltpu.SemaphoreType.DMA((7,)), pltpu.SemaphoreType.DMA((7,))],
        compiler_params=_params(has_side_effects=True),
    )(s)


def _swap_halves(g):
    def body(g_ref, o_ref, send_sem, recv_sem):
        x, y, c = _pos()
        start = pl.multiple_of((1 - c) * HALF, 8)
        cp = pltpu.make_async_remote_copy(src_ref=g_ref.at[:, pl.ds(start, HALF), :], dst_ref=o_ref, send_sem=send_sem,
                                          recv_sem=recv_sem, device_id=(x, y, 1 - c), device_id_type=MESH)
        cp.start()
        cp.wait()

    anyspec = pl.BlockSpec(memory_space=pl.ANY)
    return pl.pallas_call(
        body, name="swap_halves", in_specs=[anyspec], out_specs=anyspec,
        out_shape=jax.ShapeDtypeStruct((4, HALF, 1024), F32),
        scratch_shapes=[pltpu.SemaphoreType.DMA, pltpu.SemaphoreType.DMA],
        compiler_params=_params(has_side_effects=True),
    )(g)


def _scatter_chips(s):
    def body(s_ref, o_ref, send_sems, recv_sems, local_sem):
        x, y, c = _pos()
        me = 2 * x + y
        loc = pltpu.make_async_copy(s_ref.at[me], o_ref.at[me], local_sem)
        loc.start()
        sends = []
        for k, (px, py) in enumerate(_other_chips(x, y)):
            cp = pltpu.make_async_remote_copy(src_ref=s_ref.at[2 * px + py], dst_ref=o_ref.at[me], send_sem=send_sems.at[k],
                                              recv_sem=recv_sems.at[k], device_id=(px, py, c), device_id_type=MESH)
            cp.start()
            sends.append(cp)
        for k, (px, py) in enumerate(_other_chips(x, y)):
            pltpu.make_async_remote_copy(src_ref=s_ref.at[me], dst_ref=o_ref.at[2 * px + py], send_sem=send_sems.at[k],
                                         recv_sem=recv_sems.at[k], device_id=(px, py, c), device_id_type=MESH).wait_recv()
        for cp in sends:
            cp.wait_send()
        loc.wait()

    anyspec = pl.BlockSpec(memory_space=pl.ANY)
    return pl.pallas_call(
        body, name="scatter_chips", in_specs=[anyspec], out_specs=anyspec,
        out_shape=jax.ShapeDtypeStruct((4, HALF, 1024), s.dtype),
        scratch_shapes=[pltpu.SemaphoreType.DMA((3,)), pltpu.SemaphoreType.DMA((3,)), pltpu.SemaphoreType.DMA],
        compiler_params=_params(has_side_effects=True),
    )(s)


def _send_half(r):
    def body(r_ref, o_ref, send_sem, recv_sem):
        x, y, c = _pos()
        cp = pltpu.make_async_remote_copy(src_ref=r_ref, dst_ref=o_ref, send_sem=send_sem, recv_sem=recv_sem,
                                          device_id=(x, y, 1 - c), device_id_type=MESH)
        cp.start()
        cp.wait()

    anyspec = pl.BlockSpec(memory_space=pl.ANY)
    return pl.pallas_call(
        body, name="send_half", in_specs=[anyspec], out_specs=anyspec,
        out_shape=jax.ShapeDtypeStruct((HALF, 1024), F32),
        scratch_shapes=[pltpu.SemaphoreType.DMA, pltpu.SemaphoreType.DMA],
        compiler_params=_params(has_side_effects=True),
    )(r)


def _add_half(g, r, c):
    tr = 416
    nb = HALF // tr

    def body(c_ref, g_ref, r_ref, o_ref):
        o_ref[...] = (g_ref[...] + r_ref[...]).astype(BF16)

    grid_spec = pltpu.PrefetchScalarGridSpec(
        num_scalar_prefetch=1, grid=(4, nb),
        in_specs=[pl.BlockSpec((1, tr, 1024), lambda j, i, c_ref: (j, c_ref[0] * nb + i, 0)),
                  pl.BlockSpec((1, tr, 1024), lambda j, i, c_ref: (j, i, 0))],
        out_specs=pl.BlockSpec((1, tr, 1024), lambda j, i, c_ref: (j, i, 0)))
    return pl.pallas_call(body, name="add_half", grid_spec=grid_spec,
                          out_shape=jax.ShapeDtypeStruct((4, HALF, 1024), BF16), compiler_params=_params())(c, g, r)


def _block_diag(w):
    eye = jnp.eye(RNN_BLOCKS, dtype=w.dtype)
    return (eye[:, None, :, None] * w[:, :, None, :]).reshape(D_RNN, D_RNN)


def _diag_blocks(wd):
    d = wd.reshape(RNN_BLOCKS, 64, RNN_BLOCKS, 64)
    return jnp.stack([d[h, :, h, :] for h in range(RNN_BLOCKS)])


def _split_pack(a, first, last):
    out, base = {}, PACK_OFF[first]
    for i in range(first, last):
        s = a[:, PACK_OFF[i] - base:PACK_OFF[i + 1] - base]
        out[BIG_KEYS[i]] = s.reshape(4 * 256, 256) if BIG_KEYS[i] == "w_p_t" else s.reshape(-1, 1024)
    return out


def _layer_grads(x, p, tgt, gw, small, shard=None):
    row = lambda v: v.reshape(1, -1)
    wa = _block_diag(small["gate_a_w"]).astype(MXU_DTYPE)
    wx = _block_diag(small["gate_x_w"]).astype(MXU_DTYPE)
    sinks = small["attn_sinks"].reshape(1, HEADS)

    q, kv, xr, gr, xb = _in_proj(x, gw["w_in_t"])
    att, ga = _attn_fwd(q, kv, sinks, None if shard is None else shard[PACK_OFF[1]:PACK_OFF[3]])
    xc, h, rec, gb = _rnn_fwd(xr, gr, small["rnn_conv_w"], row(small["rnn_conv_b"]), wa, row(small["gate_a_b"]),
                              wx, row(small["gate_x_b"]), row(small["lru_lambda"]),
                              None if shard is None else shard[PACK_OFF[3]:PACK_OFF[6]])
    if shard is not None:
        gw = {**gw, **_split_pack(ga, 1, 3), **_split_pack(gb, 3, 6)}
    g1, b1 = row(small["ln1_g"]), row(small["ln1_b"])
    fcw = small["ffn_conv_w"].reshape(3, NC, FF_CHUNK).transpose(1, 0, 2)
    fcb = small["ffn_conv_b"].reshape(NC, 1, FF_CHUNK)
    z1, h1b = _out_proj(att, rec, x, gw["w_out"], g1, b1)
    gate, val, act = _ffn_up(h1b, gw["w_up_t"], fcw, fcb)
    dz2, dz2b, dpre, dpp, vec2 = _ffn_down(act, z1, p, tgt, gw["w_down"], gw["w_g"], gw["w_p_t"], g1, b1,
                                           row(small["ln2_g"]), row(small["ln2_b"]), row(small["ple_gate_b"]))
    dup, dfc = _ffn_bwd(dz2b, gate, val, gw["w_down"], fcw, fcb)
    dz1, vec1 = _ffn_dh1(dup, dz2, dpre, z1, gw["w_up_t"], gw["w_g"], g1, b1)
    datt, drec, dz1b = _out_proj_bwd(dz1, gw["w_out"])
    dxr, dgr, dwa, dwx, dvec = _rnn_bwd(drec, gr, h, xc, xr, small["rnn_conv_w"], wa, row(small["gate_a_b"]),
                                        wx, row(small["gate_x_b"]), row(small["lru_lambda"]))
    dq, dkv, dsinks = _attn_bwd(q, kv, datt, sinks)
    grad_x, du = _in_proj_bwd(dq, dkv, dxr, dgr, dz1, gw["w_in_t"])

    mix = jnp.concatenate([att, rec], axis=1)
    pb = p.astype(BF16)
    big = {
        "w_in_t": _weight_grad(du, xb, 256, "dw_in"),
        "w_out": _weight_grad(mix, dz1b, 512, "dw_out"),
        "w_up_t": _weight_grad(dup.reshape(2 * NC, -1, FF_CHUNK), h1b, FF_CHUNK, "dw_up",
                               out_block=lambda m: (m % 2) * NC + m // 2),
        "w_down": _weight_grad(act, dz2b, 512, "dw_down"),
        "w_g": _weight_grad(h1b, dpre, 512, "dw_gate"),
        "w_p_t": _weight_grad(dpp, pb, 512, "dw_proj"),
    }
    sg = {
        "attn_sinks": dsinks[:, 0],
        "rnn_conv_w": dvec[4:8],
        "rnn_conv_b": dvec[3],
        "gate_a_w": _diag_blocks(dwa),
        "gate_a_b": dvec[0],
        "gate_x_w": _diag_blocks(dwx),
        "gate_x_b": dvec[1],
        "lru_lambda": dvec[2],
        "ln1_g": vec1[0],
        "ln1_b": vec1[1],
        "ffn_conv_w": dfc[:, 0:3].transpose(1, 0, 2).reshape(3, D_FF),
        "ffn_conv_b": dfc[:, 3].reshape(D_FF),
        "ple_gate_b": vec2[3],
        "ln2_g": vec2[1],
        "ln2_b": vec2[2],
    }
    return grad_x, big, sg, vec2[0, 0:1]


BIG = ("w_in", "w_out", "w_ffn_up", "w_ffn_down", "ple_gate_w", "ple_proj")
BIG_KEYS = ("w_in_t", "w_out", "w_up_t", "w_down", "w_g", "w_p_t")
BIG_T = (True, False, True, False, False, True)
SMALL = ("attn_sinks", "rnn_conv_w", "rnn_conv_b", "gate_a_w", "gate_a_b", "gate_x_w", "gate_x_b", "lru_lambda",
         "ln1_g", "ln1_b", "ffn_conv_w", "ffn_conv_b", "ple_gate_b", "ln2_g", "ln2_b")
SHARDED_SMALL = ("rnn_conv_w", "ffn_conv_w")
WEIGHTS = ("w_in", "attn_sinks", "rnn_conv_w", "rnn_conv_b", "gate_a_w", "gate_a_b", "gate_x_w", "gate_x_b",
           "lru_lambda", "w_out", "ln1_g", "ln1_b", "w_ffn_up", "ffn_conv_w", "ffn_conv_b", "w_ffn_down",
           "ple_gate_w", "ple_gate_b", "ple_proj", "ln2_g", "ln2_b")


def _pack_big(d):
    parts = []
    for name, t in zip(BIG, BIG_T):
        a = d[name]
        a = a.T if t else a
        parts.append(a.reshape(-1, 1024))
    return jnp.concatenate(parts, axis=0)


def _unpack_big(a):
    out = {}
    shapes = {"w_in": (448, 1024), "w_out": (256, 1024), "w_ffn_up": (1536, 1024), "w_ffn_down": (768, 1024),
              "ple_gate_w": (256, 1024), "ple_proj": (256, 256)}
    for i, (name, t) in enumerate(zip(BIG, BIG_T)):
        s = a[PACK_OFF[i]:PACK_OFF[i + 1]].reshape(shapes[name])
        out[name] = (s.T if t else s)[None]
    return out


def _pack_vecs(items):
    parts, offs, n = [], [], 0
    for a in items:
        f = a.reshape(-1).astype(F32)
        pad = (-f.shape[0]) % 128
        parts.append(jnp.pad(f, (0, pad)))
        offs.append(n)
        n += (f.shape[0] + pad) // 128
    padr = (-n) % 8
    if padr:
        parts.append(jnp.zeros((padr * 128,), F32))
    return jnp.concatenate(parts).reshape(-1, 128), offs


def _unpack_vecs(a, offs, shapes):
    flat = a.reshape(-1)
    out = []
    for o, s in zip(offs, shapes):
        n = 1
        for d in s:
            n *= d
        out.append(flat[o * 128:o * 128 + n].reshape(s))
    return out


def kernel(x, p, w_in, attn_sinks, rnn_conv_w, rnn_conv_b, gate_a_w, gate_a_b, gate_x_w, gate_x_b, lru_lambda, w_out, ln1_g, ln1_b, w_ffn_up, ffn_conv_w, ffn_conv_b, w_ffn_down, ple_gate_w, ple_gate_b, ple_proj, ln2_g, ln2_b, loss_target, m_w_in, m_attn_sinks, m_rnn_conv_w, m_rnn_conv_b, m_gate_a_w, m_gate_a_b, m_gate_x_w, m_gate_x_b, m_lru_lambda, m_w_out, m_ln1_g, m_ln1_b, m_w_ffn_up, m_ffn_conv_w, m_ffn_conv_b, m_w_ffn_down, m_ple_gate_w, m_ple_gate_b, m_ple_proj, m_ln2_g, m_ln2_b, v_w_in, v_attn_sinks, v_rnn_conv_w, v_rnn_conv_b, v_gate_a_w, v_gate_a_b, v_gate_x_w, v_gate_x_b, v_lru_lambda, v_w_out, v_ln1_g, v_ln1_b, v_w_ffn_up, v_ffn_conv_w, v_ffn_conv_b, v_w_ffn_down, v_ple_gate_w, v_ple_gate_b, v_ple_proj, v_ln2_g, v_ln2_b):
    w = dict(w_in=w_in, attn_sinks=attn_sinks, rnn_conv_w=rnn_conv_w, rnn_conv_b=rnn_conv_b, gate_a_w=gate_a_w,
             gate_a_b=gate_a_b, gate_x_w=gate_x_w, gate_x_b=gate_x_b, lru_lambda=lru_lambda, w_out=w_out, ln1_g=ln1_g,
             ln1_b=ln1_b, w_ffn_up=w_ffn_up, ffn_conv_w=ffn_conv_w, ffn_conv_b=ffn_conv_b, w_ffn_down=w_ffn_down,
             ple_gate_w=ple_gate_w, ple_gate_b=ple_gate_b, ple_proj=ple_proj, ln2_g=ln2_g, ln2_b=ln2_b)
    m = dict(w_in=m_w_in, attn_sinks=m_attn_sinks, rnn_conv_w=m_rnn_conv_w, rnn_conv_b=m_rnn_conv_b, gate_a_w=m_gate_a_w,
             gate_a_b=m_gate_a_b, gate_x_w=m_gate_x_w, gate_x_b=m_gate_x_b, lru_lambda=m_lru_lambda, w_out=m_w_out,
             ln1_g=m_ln1_g, ln1_b=m_ln1_b, w_ffn_up=m_w_ffn_up, ffn_conv_w=m_ffn_conv_w, ffn_conv_b=m_ffn_conv_b,
             w_ffn_down=m_w_ffn_down, ple_gate_w=m_ple_gate_w, ple_gate_b=m_ple_gate_b, ple_proj=m_ple_proj,
             ln2_g=m_ln2_g, ln2_b=m_ln2_b)
    v = dict(w_in=v_w_in, attn_sinks=v_attn_sinks, rnn_conv_w=v_rnn_conv_w, rnn_conv_b=v_rnn_conv_b, gate_a_w=v_gate_a_w,
             gate_a_b=v_gate_a_b, gate_x_w=v_gate_x_w, gate_x_b=v_gate_x_b, lru_lambda=v_lru_lambda, w_out=v_w_out,
             ln1_g=v_ln1_g, ln1_b=v_ln1_b, w_ffn_up=v_w_ffn_up, ffn_conv_w=v_ffn_conv_w, ffn_conv_b=v_ffn_conv_b,
             w_ffn_down=v_w_ffn_down, ple_gate_w=v_ple_gate_w, ple_gate_b=v_ple_gate_b, ple_proj=v_ple_proj,
             ln2_g=v_ln2_g, ln2_b=v_ln2_b)
    w, m, v = ({k: a[0] for k, a in d.items()} for d in (w, m, v))
    chip = 2 * lax.axis_index("x") + lax.axis_index("y")
    core = lax.axis_index("c")

    wpack = _pack_big(w)
    cpack, _ = _pack_vecs([w["rnn_conv_w"], w["ffn_conv_w"]])
    shard = wpack.astype(MXU_DTYPE)
    g_in, gcp = _gather_first(shard[PACK_OFF[0]:PACK_OFF[1]], cpack)
    gw = _split_pack(g_in, 0, 1)
    small = {k: w[k] for k in SMALL}
    small["rnn_conv_w"] = gcp[:, 0:4].reshape(4, 4, 128).transpose(1, 0, 2).reshape(4, 512)
    small["ffn_conv_w"] = gcp[:, 4:22].reshape(4, 3, 768).transpose(1, 0, 2).reshape(3, 3072)

    grad_x, big, sg, loss = _layer_grads(x[0], p[0, 0], loss_target[0], gw, small, shard)

    spack, offs = _pack_vecs([sg[k] for k in SMALL] + [loss])
    ssum = _allreduce_small(spack)
    shapes = [sg[k].shape for k in SMALL] + [(1,)]
    red = dict(zip(SMALL + ("loss",), _unpack_vecs(ssum, offs, shapes)))
    red["rnn_conv_w"] = lax.dynamic_slice_in_dim(red["rnn_conv_w"], chip * 128, 128, axis=1)
    red["ffn_conv_w"] = lax.dynamic_slice_in_dim(red["ffn_conv_w"], chip * 768, 768, axis=1)

    parts = []
    for i, key in enumerate(BIG_KEYS):
        parts.append(big[key].reshape(4, PACK_ROWS[i], 1024))
    gpack = jnp.concatenate(parts, axis=1)
    core1 = core.reshape(1).astype(jnp.int32)
    sib = _swap_halves(gpack)
    chip_sum = _add_half(gpack, sib, core1)
    from_chips = _scatter_chips(chip_sum)
    half = _add4(from_chips, "add_chips")
    other_half = _send_half(half)

    gbig, dbig, mbig, vbig = _adamw_halves(wpack, half, other_half, _pack_big(m), _pack_big(v), core1)
    wsm, offs2 = _pack_vecs([w[k] for k in SMALL])
    gsm, _ = _pack_vecs([red[k] for k in SMALL])
    msm, _ = _pack_vecs([m[k] for k in SMALL])
    vsm, _ = _pack_vecs([v[k] for k in SMALL])
    dsm, nmsm, nvsm = _adamw(wsm, gsm, msm, vsm, "adamw_small")
    shapes2 = [w[k].shape for k in SMALL]

    def named(bigp, smallp):
        d = _unpack_big(bigp)
        d.update({k: a[None] for k, a in zip(SMALL, _unpack_vecs(smallp, offs2, shapes2))})
        return [d[k] for k in WEIGHTS]

    grads = named(gbig, gsm)
    return (red["loss"].reshape(()), grad_x[None], *grads, *named(dbig, dsm), *named(mbig, nmsm), *named(vbig, nvsm))
```

```python
import functools

import jax
import jax.numpy as jnp
from jax import lax
from jax.experimental import pallas as pl
from jax.experimental.pallas import tpu as pltpu

F32 = jnp.float32
BF16 = jnp.bfloat16
MXU_DTYPE = jnp.bfloat16

D = 1024
D_ATT = 512
D_KV = 128
D_RNN = 512
D_IN = 1792
D_FF = 3072
FF_CHUNK = 512
PLE = 256
HEADS = 8
HEAD_DIM = 64
BLK = 128
RNN_BLOCKS = 8
LN_EPS = 1e-5
LRU_C = 8.0
ALPHA = float(2.0 ** 0.25)
SCALE = HEAD_DIM ** -0.5
NEG = -1e30

ADAM_LR = 0.001
ADAM_B1 = 0.9
ADAM_B2 = 0.999
ADAM_EPS = 1e-08
ADAM_WD = 0.01
ADAM_STEP = 10

VMEM_LIMIT_BYTES = 56 * 1024 * 1024
MESH = pl.DeviceIdType.MESH

PACK_ROWS = (448, 256, 1536, 768, 256, 64)
PACK_OFF = tuple(sum(PACK_ROWS[:i]) for i in range(len(PACK_ROWS) + 1))
PACK_TOTAL = PACK_OFF[-1]
HALF = PACK_TOTAL // 2


def _params(**kw):
    return pltpu.CompilerParams(vmem_limit_bytes=VMEM_LIMIT_BYTES, **kw)


def _mm(a, b):
    return jnp.dot(a.astype(MXU_DTYPE), b.astype(MXU_DTYPE), preferred_element_type=F32)


def _mm_nt(a, b):
    return lax.dot_general(a.astype(MXU_DTYPE), b.astype(MXU_DTYPE), (((1,), (1,)), ((), ())),
                           preferred_element_type=F32)


def _mm_tn(a, b):
    return lax.dot_general(a.astype(MXU_DTYPE), b.astype(MXU_DTYPE), (((0,), (0,)), ((), ())),
                           preferred_element_type=F32)


def _sigmoid(x):
    return 1.0 / (1.0 + jnp.exp(-x))


def _gelu(x):
    c = 0.7978845608028654
    k = 0.044715
    t = jnp.tanh(c * (x + k * x * x * x))
    g = 0.5 * x * (1.0 + t)
    dg = 0.5 * (1.0 + t) + 0.5 * x * (1.0 - t * t) * c * (1.0 + 3.0 * k * x * x)
    return g, dg


def _expm1(x):
    poly = x * (1.0 + x * (0.5 + x * (1.0 / 6.0 + x * (1.0 / 24.0 + x * (1.0 / 120.0)))))
    return jnp.where(jnp.abs(x) < 0.03, poly, jnp.exp(x) - 1.0)


def _softplus(x):
    return jnp.maximum(x, 0.0) + jnp.log(1.0 + jnp.exp(-jnp.abs(x)))


def _ln(z, g, b):
    mu = jnp.mean(z, axis=-1, keepdims=True)
    zc = z - mu
    var = jnp.mean(zc * zc, axis=-1, keepdims=True)
    rstd = lax.rsqrt(var + LN_EPS)
    xhat = zc * rstd
    return xhat * g + b, xhat, rstd


def _ln_bwd(dy, xhat, rstd, g):
    dxh = dy * g
    m1 = jnp.mean(dxh, axis=-1, keepdims=True)
    m2 = jnp.mean(dxh * xhat, axis=-1, keepdims=True)
    return rstd * (dxh - m1 - xhat * m2)


def _colsum(x):
    return jnp.sum(x, axis=0, keepdims=True)


def _full(shape):
    nd = len(shape)
    return pl.BlockSpec(shape, lambda *_: (0,) * nd)


def _rows(tm, cols, fn=None):
    if fn is None:
        return pl.BlockSpec((tm, cols), lambda i: (i, 0))
    return pl.BlockSpec((tm, cols), lambda i: (fn(i), 0))


def _heads(tm):
    return pl.BlockSpec((HEADS, tm, HEAD_DIM), lambda i: (0, i, 0))


def _in_proj(x, w_in_t):
    T = x.shape[0]
    tm = 512

    def body(x_ref, w_ref, q_ref, kv_ref, xr_ref, gr_ref, xb_ref):
        xb = x_ref[...].astype(MXU_DTYPE)
        xb_ref[...] = xb.astype(BF16)
        q = _mm_nt(xb, w_ref[0:512, :])
        for h in range(HEADS):
            q_ref[h] = q[:, h * 64:(h + 1) * 64].astype(BF16)
        kv_ref[...] = _mm_nt(xb, w_ref[512:768, :]).astype(BF16)
        xr_ref[...] = _mm_nt(xb, w_ref[768:1280, :])
        gr_ref[...] = _mm_nt(xb, w_ref[1280:1792, :])

    return pl.pallas_call(
        body, name="in_proj", grid=(T // tm,),
        in_specs=[_rows(tm, D), _full((D_IN, D))],
        out_specs=[_heads(tm), _rows(tm, 256), _rows(tm, 512), _rows(tm, 512), _rows(tm, D)],
        out_shape=[jax.ShapeDtypeStruct((HEADS, T, 64), BF16), jax.ShapeDtypeStruct((T, 256), BF16),
                   jax.ShapeDtypeStruct((T, 512), F32), jax.ShapeDtypeStruct((T, 512), F32),
                   jax.ShapeDtypeStruct((T, D), BF16)],
        compiler_params=_params(),
    )(x, w_in_t)


def _attn_band(kv_ref, i):
    cur = pl.multiple_of(i * BLK, BLK)
    prev = pl.multiple_of(jnp.maximum(i - 1, 0) * BLK, BLK)
    band = jnp.concatenate([kv_ref[pl.ds(prev, BLK), :], kv_ref[pl.ds(cur, BLK), :]], axis=0)
    key = lax.broadcasted_iota(jnp.int32, (2 * BLK, 4 * BLK), 0)
    qry = lax.broadcasted_iota(jnp.int32, (2 * BLK, 4 * BLK), 1) & (BLK - 1)
    in_prev = jnp.logical_and(jnp.logical_and(key < BLK, key > qry), i > 0)
    mask = jnp.logical_or(in_prev, jnp.logical_and(key >= BLK, key - BLK <= qry))
    return band, mask, cur, prev


def _attn_scores(band, mask, qs, s_ref, g):
    st = jnp.where(mask, _mm_nt(band[:, g * 64:(g + 1) * 64], qs) * SCALE, NEG)
    lane = lax.broadcasted_iota(jnp.int32, (1, 4 * BLK), 1)
    sv = jnp.where(lane < BLK, s_ref[0, 4 * g],
                   jnp.where(lane < 2 * BLK, s_ref[0, 4 * g + 1], jnp.where(lane < 3 * BLK, s_ref[0, 4 * g + 2], s_ref[0, 4 * g + 3])))
    m = jnp.maximum(jnp.max(st, axis=0, keepdims=True), sv)
    p = jnp.exp(st - m)
    ps = jnp.exp(sv - m)
    return p, ps, jnp.sum(p, axis=0, keepdims=True) + ps


def _pos():
    return lax.axis_index("x"), lax.axis_index("y"), lax.axis_index("c")


def _other_chips(x, y):
    return [(1 - x, y), (x, 1 - y), (1 - x, 1 - y)]


def _gather_steps(w_ref, gw_ref, send_sems, recv_sems, local_sem):
    x, y, c = _pos()
    me = 2 * x + y
    chips = _other_chips(x, y)
    half = w_ref.shape[0] // 2
    mine = pl.ds(pl.multiple_of(c * half, 16), half)
    theirs = pl.ds(pl.multiple_of((1 - c) * half, 16), half)
    loc = pltpu.make_async_copy(w_ref, gw_ref.at[me], local_sem)

    def copy(k, src, dst, to):
        return pltpu.make_async_remote_copy(src_ref=src, dst_ref=dst, send_sem=send_sems.at[k], recv_sem=recv_sems.at[k],
                                            device_id=to, device_id_type=MESH)

    def out(k):
        px, py = chips[k]
        return copy(k, w_ref.at[mine], gw_ref.at[me, mine], (px, py, c))

    def fwd(k, rows):
        px, py = chips[k]
        return copy(3 + k, gw_ref.at[2 * px + py, rows], gw_ref.at[2 * px + py, rows], (x, y, 1 - c))

    def start():
        loc.start()
        for k in range(3):
            out(k).start()

    def forward():
        for k in range(3):
            px, py = chips[k]
            copy(k, w_ref.at[mine], gw_ref.at[2 * px + py, mine], (px, py, c)).wait_recv()
            fwd(k, mine).start()

    def finish():
        for k in range(3):
            fwd(k, theirs).wait_recv()
        for k in range(3):
            out(k).wait_send()
            fwd(k, mine).wait_send()
        loc.wait()

    return start, forward, finish


GATHER_SCRATCH = [pltpu.SemaphoreType.DMA((6,)), pltpu.SemaphoreType.DMA((6,)), pltpu.SemaphoreType.DMA]


class _Exchange:
    def __init__(self, args, out_shape, scratch, make):
        self.args, self.out_shape, self.scratch, self.make = list(args), list(out_shape), list(scratch), make


def _gather_exchange(wsrc):
    return _Exchange([wsrc], [jax.ShapeDtypeStruct((4,) + wsrc.shape, wsrc.dtype)], GATHER_SCRATCH,
                     lambda ins, outs, sems: _gather_steps(ins[0], outs[0], *sems))


def _launch(body, name, grid, in_specs, out_specs, out_shape, scratch, args, exchange=None):
    if exchange is None:
        return pl.pallas_call(body, name=name, grid=grid, in_specs=in_specs, out_specs=out_specs, out_shape=out_shape,
                              scratch_shapes=scratch, compiler_params=_params())(*args)
    n_in, n_out, ei, eo, ns = len(in_specs), len(out_specs), len(exchange.args), len(exchange.out_shape), len(exchange.scratch)
    nsteps = 1
    for g in grid:
        nsteps *= g

    def wrapped(*refs):
        ins, xin = refs[:n_in], refs[n_in:n_in + ei]
        outs, xout = refs[n_in + ei:n_in + ei + n_out], refs[n_in + ei + n_out:n_in + ei + n_out + eo]
        rest = refs[n_in + ei + n_out + eo:]
        own, sems = rest[:len(rest) - ns], rest[len(rest) - ns:]
        start, forward, finish = exchange.make(xin, xout, sems)
        i = pl.program_id(0)
        for d in range(1, len(grid)):
            i = i * grid[d] + pl.program_id(d)
        pl.when(i == 0)(start)
        body(*ins, *outs, *own)
        pl.when(i == max(nsteps - 3, 0))(forward)
        pl.when(i == nsteps - 1)(finish)

    anyspec = pl.BlockSpec(memory_space=pl.ANY)
    return pl.pallas_call(
        wrapped, name=name + "_x", grid=grid, in_specs=list(in_specs) + [anyspec] * ei, out_specs=list(out_specs) + [anyspec] * eo,
        out_shape=list(out_shape) + exchange.out_shape, scratch_shapes=list(scratch) + exchange.scratch,
        compiler_params=_params(has_side_effects=True))(*args, *exchange.args)


def _attn_fwd(q, kv, sinks, exchange=None):
    T = kv.shape[0]

    def body(q_ref, kv_ref, s_ref, o_ref):
        i = pl.program_id(0)
        band, mask, _, _ = _attn_band(kv_ref, i)
        for g in range(2):
            qs = q_ref[4 * g:4 * g + 4].reshape(4 * BLK, HEAD_DIM)
            p, _, den = _attn_scores(band, mask, qs, s_ref, g)
            ot = _mm_tn(band[:, 128:256], p) / den
            for hh in range(4):
                o = ot[:, hh * BLK:(hh + 1) * BLK].T
                o_ref[:, (4 * g + hh) * 64:(4 * g + hh + 1) * 64] = o[:, g * 64:(g + 1) * 64].astype(BF16)

    return _launch(body, "attn_fwd", (T // BLK,), [_heads(BLK), _full((T, 256)), pl.BlockSpec(memory_space=pltpu.SMEM)],
                   [_rows(BLK, 512)], [jax.ShapeDtypeStruct((T, 512), BF16)], [], (q, kv, sinks), exchange)


def _attn_bwd(q, kv, do, sinks, exchange=None):
    T = kv.shape[0]

    def body(q_ref, kv_ref, do_ref, s_ref, dq_ref, dkv_ref, ds_ref):
        i = pl.program_id(0)
        band, mask, cur, prev = _attn_band(kv_ref, i)

        @pl.when(i == 0)
        def _():
            ds_ref[...] = jnp.zeros_like(ds_ref)

        for g in range(2):
            qs = q_ref[4 * g:4 * g + 4].reshape(4 * BLK, HEAD_DIM)
            dos = do_ref[4 * g:4 * g + 4].reshape(4 * BLK, HEAD_DIM)
            p, ps, den = _attn_scores(band, mask, qs, s_ref, g)
            inv = 1.0 / den
            p = p * inv
            dpt = _mm_nt(band[:, 128 + g * 64:192 + g * 64], dos)
            delta = jnp.sum(p * dpt, axis=0, keepdims=True)
            dst = p * (dpt - delta)
            dsv = -(ps * inv) * delta
            for hh in range(4):
                dsink = jnp.sum(dsv[:, hh * BLK:(hh + 1) * BLK], axis=1, keepdims=True)
                ds_ref[4 * g + hh:4 * g + hh + 1, :] += jnp.broadcast_to(dsink, (1, 128))
            dqt = _mm_tn(band[:, 0:128], dst) * SCALE
            for hh in range(4):
                dqh = dqt[:, hh * BLK:(hh + 1) * BLK].T
                dq_ref[:, (4 * g + hh) * 64:(4 * g + hh + 1) * 64] = dqh[:, g * 64:(g + 1) * 64].astype(BF16)
            dk = _mm(dst, qs) * SCALE
            dv = _mm(p, dos)
            dkv_ref[pl.ds(cur, BLK), g * 64:(g + 1) * 64] = dk[BLK:2 * BLK]
            dkv_ref[pl.ds(cur, BLK), 128 + g * 64:192 + g * 64] = dv[BLK:2 * BLK]
            dkv_ref[pl.ds(prev, BLK), g * 64:(g + 1) * 64] += dk[0:BLK]
            dkv_ref[pl.ds(prev, BLK), 128 + g * 64:192 + g * 64] += dv[0:BLK]

    return _launch(body, "attn_bwd", (T // BLK,),
                   [_heads(BLK), _full((T, 256)), _heads(BLK), pl.BlockSpec(memory_space=pltpu.SMEM)],
                   [_rows(BLK, 512), _full((T, 256)), _full((8, 128))],
                   [jax.ShapeDtypeStruct((T, 512), BF16), jax.ShapeDtypeStruct((T, 256), F32),
                    jax.ShapeDtypeStruct((8, 128), F32)], [], (q, kv, do, sinks), exchange)


def _rows8(tm, cols):
    return lax.broadcasted_iota(jnp.int32, (tm, cols), 0) & 7


def _lru_gates(xc, wa, ba, wx, bx, lam):
    r = _sigmoid(_mm(xc, wa) + ba)
    ii = _sigmoid(_mm(xc, wx) + bx)
    sp = _softplus(-lam)
    la = -LRU_C * r * sp
    a = jnp.exp(la)
    m = jnp.sqrt(-_expm1(2.0 * la))
    return r, ii, sp, a, m


def _rnn_fwd(xr, gr, cw, cb, wa, ba, wx, bx, lam, exchange=None):
    T = xr.shape[0]
    tm = 256
    C = D_RNN

    def body(xr_ref, gr_ref, cw_ref, cb_ref, wa_ref, ba_ref, wx_ref, bx_ref, lam_ref,
             xc_ref, h_ref, rec_ref, ext, a_s, b_s, carry):
        i = pl.program_id(0)

        @pl.when(i == 0)
        def _():
            ext[0:8, :] = jnp.zeros((8, C), F32)
            carry[...] = jnp.zeros((8, C), F32)

        ext[8:8 + tm, :] = xr_ref[...]
        xc = cb_ref[...] + cw_ref[3:4, :] * ext[8:8 + tm, :]
        for k in range(3):
            xc = xc + cw_ref[k:k + 1, :] * ext[5 + k:5 + k + tm, :]
        ext[0:8, :] = ext[tm:tm + 8, :]
        xc_ref[...] = xc
        _, ii, _, a, m = _lru_gates(xc, wa_ref[...], ba_ref[...], wx_ref[...], bx_ref[...], lam_ref[...])
        b = m * ii * xc
        r8 = _rows8(tm, C)
        for d in (1, 2, 4):
            ok = r8 >= d
            a_sh = jnp.where(ok, pltpu.roll(a, d, 0), 1.0)
            b_sh = jnp.where(ok, pltpu.roll(b, d, 0), 0.0)
            b = a * b_sh + b
            a = a * a_sh
        a_s[...] = a
        b_s[...] = b

        def step(g, hin):
            s = pl.multiple_of(g * 8, 8)
            hg = a_s[pl.ds(s, 8), :] * hin + b_s[pl.ds(s, 8), :]
            h_ref[pl.ds(s, 8), :] = hg
            return jnp.broadcast_to(hg[7:8, :], (8, C))

        carry[...] = lax.fori_loop(0, tm // 8, step, carry[...])
        ge, _ = _gelu(gr_ref[...])
        rec_ref[...] = (h_ref[...] * ge).astype(BF16)

    vec = _full((1, C))
    in_specs = [_rows(tm, C), _rows(tm, C), _full((4, C)), vec, _full((C, C)), vec, _full((C, C)), vec, vec]
    out_specs = [_rows(tm, C), _rows(tm, C), _rows(tm, C)]
    out_shape = [jax.ShapeDtypeStruct((T, C), F32), jax.ShapeDtypeStruct((T, C), F32), jax.ShapeDtypeStruct((T, C), BF16)]
    scratch = [pltpu.VMEM((tm + 8, C), F32), pltpu.VMEM((tm, C), F32), pltpu.VMEM((tm, C), F32), pltpu.VMEM((8, C), F32)]
    return _launch(body, "rnn_fwd", (T // tm,), in_specs, out_specs, out_shape, scratch,
                   (xr, gr, cw, cb, wa, ba, wx, bx, lam), exchange)


def _rnn_bwd(drec, gr, h, xc, xr, cw, wa, ba, wx, bx, lam, exchange=None):
    T = xr.shape[0]
    tm = 256
    C = D_RNN
    nt = T // tm
    t8 = tm // 8

    def body(drec_ref, gr_ref, h_ref, hp_ref, xc_ref, xr_ref, xrp_ref, cw_ref, wa_ref, ba_ref, wx_ref, bx_ref,
             lam_ref, dxr_ref, dgr_ref, dwa_ref, dwx_ref, dvec_ref, c_s, g_s, gout, ext, xext, anext, gcarry):
        i = pl.program_id(0)
        j = nt - 1 - i

        @pl.when(i == 0)
        def _():
            dwa_ref[...] = jnp.zeros_like(dwa_ref)
            dwx_ref[...] = jnp.zeros_like(dwx_ref)
            dvec_ref[...] = jnp.zeros_like(dvec_ref)
            anext[...] = jnp.zeros((8, C), F32)
            gcarry[...] = jnp.zeros((8, C), F32)
            ext[tm:tm + 8, :] = jnp.zeros((8, C), F32)

        xc = xc_ref[...]
        lam = lam_ref[...]
        r, ii, sp, a, m = _lru_gates(xc, wa_ref[...], ba_ref[...], wx_ref[...], bx_ref[...], lam)
        ge, dge = _gelu(gr_ref[...])
        drec = drec_ref[...]
        hh = h_ref[...]
        dgr_ref[...] = (drec * hh * dge).astype(BF16)
        dh = drec * ge
        rowi = lax.broadcasted_iota(jnp.int32, (tm, C), 0)
        c = jnp.where(rowi == tm - 1, jnp.broadcast_to(anext[0:1, :], (tm, C)), pltpu.roll(a, tm - 1, 0))
        anext[...] = a[0:8, :]
        r8 = rowi & 7
        gg = dh
        for d in (1, 2, 4):
            ok = r8 < 8 - d
            c_sh = jnp.where(ok, pltpu.roll(c, tm - d, 0), 1.0)
            g_sh = jnp.where(ok, pltpu.roll(gg, tm - d, 0), 0.0)
            gg = c * g_sh + gg
            c = c * c_sh
        c_s[...] = c
        g_s[...] = gg

        def step(k, gin):
            s = pl.multiple_of((t8 - 1 - k) * 8, 8)
            og = c_s[pl.ds(s, 8), :] * gin + g_s[pl.ds(s, 8), :]
            gout[pl.ds(s, 8), :] = og
            return jnp.broadcast_to(og[0:1, :], (8, C))

        gcarry[...] = lax.fori_loop(0, t8, step, gcarry[...])
        G = gout[...]
        hprev_row = jnp.where(j > 0, hp_ref[7:8, :], 0.0)
        hprev = jnp.where(rowi == 0, jnp.broadcast_to(hprev_row, (tm, C)), pltpu.roll(hh, 1, 0))
        da = G * hprev
        dm = G * ii * xc
        di = G * m * xc
        dxc = G * m * ii
        dla = da * a - dm * a * a / m
        dr = dla * (-LRU_C * sp)
        dsp = _colsum(dla * (-LRU_C * r))
        dlam = dsp * (-_sigmoid(-lam))
        dpr = dr * r * (1.0 - r)
        dpi = di * ii * (1.0 - ii)
        dxc = dxc + _mm_nt(dpr, wa_ref[...]) + _mm_nt(dpi, wx_ref[...])
        dwa_ref[...] += _mm_tn(xc, dpr)
        dwx_ref[...] += _mm_tn(xc, dpi)
        dvec_ref[0:1, :] += _colsum(dpr)
        dvec_ref[1:2, :] += _colsum(dpi)
        dvec_ref[2:3, :] += dlam
        dvec_ref[3:4, :] += _colsum(dxc)
        ext[0:tm, :] = dxc
        dxr = cw_ref[3:4, :] * dxc
        for k in range(3):
            dxr = dxr + cw_ref[k:k + 1, :] * ext[3 - k:3 - k + tm, :]
        ext[tm:tm + 8, :] = dxc[0:8, :]
        dxr_ref[...] = dxr.astype(BF16)
        xext[0:8, :] = jnp.where(j > 0, xrp_ref[...], 0.0)
        xext[8:8 + tm, :] = xr_ref[...]
        for k in range(4):
            dvec_ref[4 + k:5 + k, :] += _colsum(dxc * xext[5 + k:5 + k + tm, :])

    rev = lambda i: nt - 1 - i
    prev8 = lambda i: jnp.maximum((nt - 1 - i) * t8 - 1, 0)
    vec = _full((1, C))
    return _launch(
        body, "rnn_bwd", (nt,),
        [_rows(tm, C, rev), _rows(tm, C, rev), _rows(tm, C, rev), _rows(8, C, prev8), _rows(tm, C, rev),
         _rows(tm, C, rev), _rows(8, C, prev8), _full((4, C)), _full((C, C)), vec, _full((C, C)), vec, vec],
        [_rows(tm, C, rev), _rows(tm, C, rev), _full((C, C)), _full((C, C)), _full((8, C))],
        [jax.ShapeDtypeStruct((T, C), BF16), jax.ShapeDtypeStruct((T, C), BF16),
         jax.ShapeDtypeStruct((C, C), F32), jax.ShapeDtypeStruct((C, C), F32), jax.ShapeDtypeStruct((8, C), F32)],
        [pltpu.VMEM((tm, C), F32), pltpu.VMEM((tm, C), F32), pltpu.VMEM((tm, C), F32),
         pltpu.VMEM((tm + 8, C), F32), pltpu.VMEM((tm + 8, C), F32), pltpu.VMEM((8, C), F32), pltpu.VMEM((8, C), F32)],
        (drec, gr, h, h, xc, xr, xr, cw, wa, ba, wx, bx, lam), exchange)


def _out_proj(att, rec, x, w_out, g1, b1):
    T = x.shape[0]
    tm = 512

    def body(att_ref, rec_ref, x_ref, w_ref, g1_ref, b1_ref, z_ref, h_ref):
        mix = _mm(att_ref[...], w_ref[0:512, :]) + _mm(rec_ref[...], w_ref[512:1024, :])
        z1 = ALPHA * x_ref[...] + mix
        z_ref[...] = z1
        h1, _, _ = _ln(z1, g1_ref[...], b1_ref[...])
        h_ref[...] = h1.astype(MXU_DTYPE).astype(BF16)

    return pl.pallas_call(
        body, name="out_proj", grid=(T // tm,),
        in_specs=[_rows(tm, 512), _rows(tm, 512), _rows(tm, D), _full((D, D)), _full((1, D)), _full((1, D))],
        out_specs=[_rows(tm, D), _rows(tm, D)],
        out_shape=[jax.ShapeDtypeStruct((T, D), F32), jax.ShapeDtypeStruct((T, D), BF16)],
        compiler_params=_params(),
    )(att, rec, x, w_out, g1, b1)


NC = D_FF // FF_CHUNK


def _ffn_up(h1b, w_up_t, fcw, fcb):
    T = h1b.shape[0]
    tm = 512
    CW = FF_CHUNK

    def body(h_ref, wg_ref, wv_ref, fcw_ref, fcb_ref, gate_ref, val_ref, act_ref, ext):
        i = pl.program_id(1)

        @pl.when(i == 0)
        def _():
            ext[0:8, :] = jnp.zeros((8, CW), F32)

        hb = h_ref[...]
        gate = _mm_nt(hb, wg_ref[...])
        val = _mm_nt(hb, wv_ref[...])
        gate_ref[...] = gate
        val_ref[...] = val
        ext[8:8 + tm, :] = gate
        gc = (fcb_ref[...] + fcw_ref[0:1, :] * ext[6:6 + tm, :] + fcw_ref[1:2, :] * ext[7:7 + tm, :]
              + fcw_ref[2:3, :] * gate)
        ext[0:8, :] = ext[tm:tm + 8, :]
        ge, _ = _gelu(gc)
        act_ref[...] = (ge * val).astype(BF16)

    chunk = pl.BlockSpec((None, tm, CW), lambda c, i: (c, i, 0))
    return pl.pallas_call(
        body, name="ffn_up", grid=(NC, T // tm),
        in_specs=[pl.BlockSpec((tm, D), lambda c, i: (i, 0)), pl.BlockSpec((CW, D), lambda c, i: (c, 0)),
                  pl.BlockSpec((CW, D), lambda c, i: (NC + c, 0)), pl.BlockSpec((None, 3, CW), lambda c, i: (c, 0, 0)),
                  pl.BlockSpec((None, 1, CW), lambda c, i: (c, 0, 0))],
        out_specs=[chunk, chunk, chunk],
        out_shape=[jax.ShapeDtypeStruct((NC, T, CW), F32), jax.ShapeDtypeStruct((NC, T, CW), F32),
                   jax.ShapeDtypeStruct((NC, T, CW), BF16)],
        scratch_shapes=[pltpu.VMEM((tm + 8, CW), F32)],
        compiler_params=_params(),
    )(h1b, w_up_t, w_up_t, fcw, fcb)


def _ffn_down(act, z1, p, tgt, w_down, w_g, w_p_t, g1, b1, g2, b2, bg):
    T = z1.shape[0]
    tm = 256

    def body(act_ref, z_ref, p_ref, t_ref, wdn_hbm, wg_hbm, wp_hbm, g1_ref, b1_ref, g2_ref, b2_ref, bg_ref,
             dz2_ref, dz2b_ref, dpre_ref, dpp_ref, vec_ref, wdn, wg, wp):
        @pl.when(pl.program_id(0) == 0)
        def _():
            pltpu.sync_copy(wdn_hbm, wdn)
            pltpu.sync_copy(wg_hbm, wg)
            pltpu.sync_copy(wp_hbm, wp)
            vec_ref[...] = jnp.zeros_like(vec_ref)

        g2v = g2_ref[...]
        h1, _, _ = _ln(z_ref[...], g1_ref[...], b1_ref[...])
        h1b = h1.astype(MXU_DTYPE)
        ffn = _mm(act_ref[0], wdn[0:FF_CHUNK, :])
        for c in range(1, NC):
            ffn = ffn + _mm(act_ref[c], wdn[c * FF_CHUNK:(c + 1) * FF_CHUNK, :])
        sg = _sigmoid(_mm(h1b, wg[...]) + bg_ref[...])
        pp = _mm_nt(p_ref[...], wp[...])
        z2 = ALPHA * h1 + ffn + sg * pp
        y, xh2, rstd2 = _ln(z2, g2v, b2_ref[...])
        diff = y - t_ref[...]
        dy = diff * (1.0 / D)
        dz2 = _ln_bwd(dy, xh2, rstd2, g2v)
        dpre = dz2 * pp * sg * (1.0 - sg)
        dz2_ref[...] = dz2
        dz2b_ref[...] = dz2.astype(BF16)
        dpre_ref[...] = dpre.astype(BF16)
        dpp_ref[...] = (dz2 * sg).astype(BF16)
        loss = 0.5 * jnp.sum(jnp.sum(diff * diff, axis=1, keepdims=True), axis=0, keepdims=True) * (1.0 / D)
        vec_ref[0:1, :] += jnp.broadcast_to(loss, (1, D))
        vec_ref[1:2, :] += _colsum(dy * xh2)
        vec_ref[2:3, :] += _colsum(dy)
        vec_ref[3:4, :] += _colsum(dpre)

    anyspec = pl.BlockSpec(memory_space=pl.ANY)
    vec = _full((1, D))
    return pl.pallas_call(
        body, name="ffn_down", grid=(T // tm,),
        in_specs=[pl.BlockSpec((NC, tm, FF_CHUNK), lambda i: (0, i, 0)), _rows(tm, D), _rows(tm, PLE), _rows(tm, D),
                  anyspec, anyspec, anyspec] + [vec] * 5,
        out_specs=[_rows(tm, D)] * 4 + [_full((8, D))],
        out_shape=[jax.ShapeDtypeStruct((T, D), F32)] + [jax.ShapeDtypeStruct((T, D), BF16)] * 3
                  + [jax.ShapeDtypeStruct((8, D), F32)],
        scratch_shapes=[pltpu.VMEM((D_FF, D), MXU_DTYPE), pltpu.VMEM((D, D), MXU_DTYPE), pltpu.VMEM((D, PLE), MXU_DTYPE)],
        compiler_params=_params(),
    )(act, z1, p, tgt, w_down, w_g, w_p_t, g1, b1, g2, b2, bg)


def _ffn_bwd(dz2b, gate, val, w_down, fcw, fcb):
    T = dz2b.shape[0]
    tm = 512
    CW = FF_CHUNK
    nt = T // tm
    t8 = tm // 8

    def body(dz_ref, wdn_ref, gate_ref, gp_ref, val_ref, fcw_ref, fcb_ref, dup_ref, dfc_ref, gext, dext):
        i = pl.program_id(1)
        j = nt - 1 - i

        @pl.when(i == 0)
        def _():
            dext[tm:tm + 8, :] = jnp.zeros((8, CW), F32)
            dfc_ref[...] = jnp.zeros_like(dfc_ref)

        gate = gate_ref[...]
        gext[0:8, :] = jnp.where(j > 0, gp_ref[...], 0.0)
        gext[8:8 + tm, :] = gate
        gate1 = gext[7:7 + tm, :]
        gate2 = gext[6:6 + tm, :]
        gc = fcb_ref[...] + fcw_ref[0:1, :] * gate2 + fcw_ref[1:2, :] * gate1 + fcw_ref[2:3, :] * gate
        ge, dge = _gelu(gc)
        dact = _mm_nt(dz_ref[...], wdn_ref[...])
        dgc = dact * val_ref[...] * dge
        dext[0:tm, :] = dgc
        dgate = fcw_ref[2:3, :] * dgc + fcw_ref[1:2, :] * dext[1:1 + tm, :] + fcw_ref[0:1, :] * dext[2:2 + tm, :]
        dext[tm:tm + 8, :] = dgc[0:8, :]
        dup_ref[0] = dgate.astype(BF16)
        dup_ref[1] = (dact * ge).astype(BF16)
        dfc_ref[0:1, :] += _colsum(dgc * gate2)
        dfc_ref[1:2, :] += _colsum(dgc * gate1)
        dfc_ref[2:3, :] += _colsum(dgc * gate)
        dfc_ref[3:4, :] += _colsum(dgc)

    rev = lambda c, i: (c, nt - 1 - i, 0)
    return pl.pallas_call(
        body, name="ffn_bwd", grid=(NC, nt),
        in_specs=[pl.BlockSpec((tm, D), lambda c, i: (nt - 1 - i, 0)), pl.BlockSpec((CW, D), lambda c, i: (c, 0)),
                  pl.BlockSpec((None, tm, CW), rev),
                  pl.BlockSpec((None, 8, CW), lambda c, i: (c, jnp.maximum((nt - 1 - i) * t8 - 1, 0), 0)),
                  pl.BlockSpec((None, tm, CW), rev), pl.BlockSpec((None, 3, CW), lambda c, i: (c, 0, 0)),
                  pl.BlockSpec((None, 1, CW), lambda c, i: (c, 0, 0))],
        out_specs=[pl.BlockSpec((None, 2, tm, CW), lambda c, i: (c, 0, nt - 1 - i, 0)),
                   pl.BlockSpec((None, 8, CW), lambda c, i: (c, 0, 0))],
        out_shape=[jax.ShapeDtypeStruct((NC, 2, T, CW), BF16), jax.ShapeDtypeStruct((NC, 8, CW), F32)],
        scratch_shapes=[pltpu.VMEM((tm + 8, CW), F32), pltpu.VMEM((tm + 8, CW), F32)],
        compiler_params=_params(),
    )(dz2b, w_down, gate, gate, val, fcw, fcb)


def _ffn_dh1(dup, dz2, dpre, z1, w_up_t, w_g, g1, b1):
    T = z1.shape[0]
    tm = 256

    def body(dup_ref, dz2_ref, dpre_ref, z_ref, wup_hbm, wg_hbm, g1_ref, b1_ref, dz1_ref, vec_ref, wup, wg):
        @pl.when(pl.program_id(0) == 0)
        def _():
            pltpu.sync_copy(wup_hbm, wup)
            pltpu.sync_copy(wg_hbm, wg)
            vec_ref[...] = jnp.zeros_like(vec_ref)

        g1v = g1_ref[...]
        _, xh1, rstd1 = _ln(z_ref[...], g1v, b1_ref[...])
        dh1 = ALPHA * dz2_ref[...] + _mm_nt(dpre_ref[...], wg[...])
        for c in range(NC):
            for s in range(2):
                r0 = s * D_FF + c * FF_CHUNK
                dh1 = dh1 + _mm(dup_ref[c, s], wup[r0:r0 + FF_CHUNK, :])
        dz1_ref[...] = _ln_bwd(dh1, xh1, rstd1, g1v)
        vec_ref[0:1, :] += _colsum(dh1 * xh1)
        vec_ref[1:2, :] += _colsum(dh1)

    anyspec = pl.BlockSpec(memory_space=pl.ANY)
    vec = _full((1, D))
    return pl.pallas_call(
        body, name="ffn_dh1", grid=(T // tm,),
        in_specs=[pl.BlockSpec((NC, 2, tm, FF_CHUNK), lambda i: (0, 0, i, 0)), _rows(tm, D), _rows(tm, D), _rows(tm, D),
                  anyspec, anyspec, vec, vec],
        out_specs=[_rows(tm, D), _full((8, D))],
        out_shape=[jax.ShapeDtypeStruct((T, D), F32), jax.ShapeDtypeStruct((8, D), F32)],
        scratch_shapes=[pltpu.VMEM((2 * D_FF, D), MXU_DTYPE), pltpu.VMEM((D, D), MXU_DTYPE)],
        compiler_params=_params(),
    )(dup, dz2, dpre, z1, w_up_t, w_g, g1, b1)


def _out_proj_bwd(dz1, w_out, exchange=None):
    T = dz1.shape[0]
    tm = 512

    def body(dz_ref, w_ref, datt_ref, drec_ref, dzb_ref):
        dzb = dz_ref[...].astype(MXU_DTYPE)
        dzb_ref[...] = dzb.astype(BF16)
        datt = _mm_nt(dzb, w_ref[0:512, :])
        for h in range(HEADS):
            datt_ref[h] = datt[:, h * 64:(h + 1) * 64].astype(BF16)
        drec_ref[...] = _mm_nt(dzb, w_ref[512:1024, :])

    return _launch(body, "out_proj_bwd", (T // tm,), [_rows(tm, D), _full((D, D))],
                   [_heads(tm), _rows(tm, 512), _rows(tm, D)],
                   [jax.ShapeDtypeStruct((HEADS, T, 64), BF16), jax.ShapeDtypeStruct((T, 512), F32),
                    jax.ShapeDtypeStruct((T, D), BF16)], [], (dz1, w_out), exchange)


def _in_proj_bwd(dq, dkv, dxr, dgr, dz1, w_in_t):
    T = dz1.shape[0]
    tm = 512

    def body(dq_ref, dkv_ref, dxr_ref, dgr_ref, dz_ref, w_ref, dx_ref, du_ref):
        dkv = dkv_ref[...].astype(BF16)
        dx_ref[...] = (ALPHA * dz_ref[...] + _mm(dq_ref[...], w_ref[0:512, :]) + _mm(dkv, w_ref[512:768, :])
                       + _mm(dxr_ref[...], w_ref[768:1280, :]) + _mm(dgr_ref[...], w_ref[1280:1792, :]))
        du_ref[:, 0:512] = dq_ref[...]
        du_ref[:, 512:768] = dkv
        du_ref[:, 768:1280] = dxr_ref[...]
        du_ref[:, 1280:1792] = dgr_ref[...]

    return pl.pallas_call(
        body, name="in_proj_bwd", grid=(T // tm,),
        in_specs=[_rows(tm, 512), _rows(tm, 256), _rows(tm, 512), _rows(tm, 512), _rows(tm, D), _full((D_IN, D))],
        out_specs=[_rows(tm, D), _rows(tm, D_IN)],
        out_shape=[jax.ShapeDtypeStruct((T, D), F32), jax.ShapeDtypeStruct((T, D_IN), BF16)],
        compiler_params=_params(),
    )(dq, dkv, dxr, dgr, dz1, w_in_t)


def _weight_grad(a, b, bm, name, out_block=lambda m: m):
    bt = min(2048, b.shape[0])
    if a.ndim == 3:
        assert a.shape[2] == bm
        T, M = a.shape[1], a.shape[0] * bm
        a_spec = pl.BlockSpec((None, bt, bm), lambda m, k: (m, k, 0))
    else:
        T, M = a.shape
        a_spec = pl.BlockSpec((bt, bm), lambda m, k: (k, m))
    N = b.shape[1]
    nk = T // bt

    def body(a_ref, b_ref, o_ref):
        k = pl.program_id(1)

        @pl.when(k == 0)
        def _():
            o_ref[...] = jnp.zeros_like(o_ref)

        o_ref[...] += _mm_tn(a_ref[...], b_ref[...])

    return pl.pallas_call(
        body, name=name, grid=(M // bm, nk),
        in_specs=[a_spec, pl.BlockSpec((bt, N), lambda m, k: (k, 0))],
        out_specs=pl.BlockSpec((bm, N), lambda m, k: (out_block(m), 0)),
        out_shape=jax.ShapeDtypeStruct((M, N), F32),
        compiler_params=_params(),
    )(a, b)


def _adamw(w, g, m, v, name):
    R, C = w.shape
    tr = R // 8 if R % 64 == 0 else R
    c1 = 1.0 / (1.0 - ADAM_B1 ** ADAM_STEP)
    c2 = 1.0 / (1.0 - ADAM_B2 ** ADAM_STEP)

    def body(w_ref, g_ref, m_ref, v_ref, d_ref, nm_ref, nv_ref):
        g = g_ref[...]
        nm = ADAM_B1 * m_ref[...] + (1.0 - ADAM_B1) * g
        nv = ADAM_B2 * v_ref[...] + (1.0 - ADAM_B2) * g * g
        nm_ref[...] = nm
        nv_ref[...] = nv
        d_ref[...] = -ADAM_LR * ((nm * c1) / (jnp.sqrt(nv * c2) + ADAM_EPS) + ADAM_WD * w_ref[...])

    spec = pl.BlockSpec((tr, C), lambda i: (i, 0))
    return pl.pallas_call(
        body, name=name, grid=(R // tr,),
        in_specs=[spec] * 4, out_specs=[spec] * 3,
        out_shape=[jax.ShapeDtypeStruct((R, C), F32)] * 3,
        compiler_params=_params(),
    )(w, g, m, v)


def _adamw_halves(w, mine, sib, m, v, c, name):
    R = w.shape[0]
    nb = 4
    tr = R // (2 * nb)
    assert tr % 8 == 0
    c1 = 1.0 / (1.0 - ADAM_B1 ** ADAM_STEP)
    c2 = 1.0 / (1.0 - ADAM_B2 ** ADAM_STEP)

    def body(c_ref, w_ref, a_ref, b_ref, m_ref, v_ref, g_ref, d_ref, nm_ref, nv_ref):
        own = (pl.program_id(0) // nb) == c_ref[0]
        g = jnp.where(own, a_ref[...], b_ref[...])
        nm = ADAM_B1 * m_ref[...] + (1.0 - ADAM_B1) * g
        nv = ADAM_B2 * v_ref[...] + (1.0 - ADAM_B2) * g * g
        g_ref[...] = g
        nm_ref[...] = nm
        nv_ref[...] = nv
        d_ref[...] = -ADAM_LR * ((nm * c1) / (jnp.sqrt(nv * c2) + ADAM_EPS) + ADAM_WD * w_ref[...])

    full = pl.BlockSpec((tr, 1024), lambda i, c_ref: (i, 0))
    half = pl.BlockSpec((tr, 1024), lambda i, c_ref: (i % nb, 0))
    grid_spec = pltpu.PrefetchScalarGridSpec(num_scalar_prefetch=1, grid=(2 * nb,),
                                             in_specs=[full, half, half, full, full], out_specs=[full] * 4)
    return pl.pallas_call(body, name=name, grid_spec=grid_spec, out_shape=[jax.ShapeDtypeStruct((R, 1024), F32)] * 4,
                          compiler_params=_params())(c, w, mine, sib, m, v)


def _add4(a, name):
    _, R, C = a.shape
    tr = R // 2
    assert tr % 16 == 0

    def body(a_ref, o_ref):
        o_ref[...] = ((a_ref[0].astype(F32) + a_ref[1].astype(F32)) + a_ref[2].astype(F32)) + a_ref[3].astype(F32)

    return pl.pallas_call(body, name=name, grid=(R // tr,),
                          in_specs=[pl.BlockSpec((4, tr, C), lambda i: (0, i, 0))],
                          out_specs=pl.BlockSpec((tr, C), lambda i: (i, 0)),
                          out_shape=jax.ShapeDtypeStruct((R, C), F32), compiler_params=_params())(a)


def _gather_first(wsrc, cpack):
    def body(w_ref, c_ref, gw_ref, gc_ref, send_sems, recv_sems, local_sem, csend, crecv, clocal):
        x, y, c = _pos()
        me = 2 * x + y
        chips = _other_chips(x, y)
        start, forward, finish = _gather_steps(w_ref, gw_ref, send_sems, recv_sems, local_sem)
        start()
        loc = pltpu.make_async_copy(c_ref, gc_ref.at[me], clocal)
        loc.start()

        def conv_copy(k, slot):
            px, py = chips[k]
            return pltpu.make_async_remote_copy(src_ref=c_ref, dst_ref=gc_ref.at[slot], send_sem=csend.at[k],
                                                recv_sem=crecv.at[k], device_id=(px, py, c), device_id_type=MESH)

        for k in range(3):
            conv_copy(k, me).start()
        forward()
        finish()
        for k, (px, py) in enumerate(chips):
            conv_copy(k, 2 * px + py).wait_recv()
        for k in range(3):
            conv_copy(k, me).wait_send()
        loc.wait()

    anyspec = pl.BlockSpec(memory_space=pl.ANY)
    return pl.pallas_call(
        body, name="gather_first",
        in_specs=[anyspec, anyspec], out_specs=[anyspec, anyspec],
        out_shape=[jax.ShapeDtypeStruct((4,) + wsrc.shape, wsrc.dtype), jax.ShapeDtypeStruct((4,) + cpack.shape, cpack.dtype)],
        scratch_shapes=GATHER_SCRATCH + [pltpu.SemaphoreType.DMA((3,)), pltpu.SemaphoreType.DMA((3,)), pltpu.SemaphoreType.DMA],
        compiler_params=_params(has_side_effects=True),
    )(wsrc, cpack)


def _allreduce_small(s):
    R = s.shape[0]

    def body(s_ref, o_ref, buf, send_sems, recv_sems):
        x, y, c = _pos()
        me = 4 * x + 2 * y + c
        buf[me] = s_ref[...]
        sends = []
        for k in range(1, 8):
            peer = (x ^ (k >> 2), y ^ ((k >> 1) & 1), c ^ (k & 1))
            cp = pltpu.make_async_remote_copy(src_ref=s_ref, dst_ref=buf.at[me], send_sem=send_sems.at[k - 1],
                                              recv_sem=recv_sems.at[k - 1], device_id=peer, device_id_type=MESH)
            cp.start()
            sends.append(cp)
        for k in range(1, 8):
            px, py, pc = x ^ (k >> 2), y ^ ((k >> 1) & 1), c ^ (k & 1)
            pltpu.make_async_remote_copy(src_ref=s_ref, dst_ref=buf.at[4 * px + 2 * py + pc], send_sem=send_sems.at[k - 1],
                                         recv_sem=recv_sems.at[k - 1], device_id=(px, py, pc),
                                         device_id_type=MESH).wait_recv()
        for cp in sends:
            cp.wait_send()
        acc = buf[0]
        for d in range(1, 8):
            acc = acc + buf[d]
        o_ref[...] = acc

    vm = pl.BlockSpec(memory_space=pltpu.VMEM)
    return pl.pallas_call(
        body, name="allreduce_small", in_specs=[vm], out_specs=vm,
        out_shape=jax.ShapeDtypeStruct((R, 128), F32),
        scratch_shapes=[pltpu.VMEM((8, R, 128), F32), pltpu.SemaphoreType.DMA((7,)), pltpu.SemaphoreType.DMA((7,))],
        compiler_params=_params(has_side_effects=True),
    )(s)


def _swap_exchange(g):
    half = g.shape[1] // 2

    def make(ins, outs, sems):
        x, y, c = _pos()
        rows = pl.ds(pl.multiple_of((1 - c) * half, 8), half)
        cp = pltpu.make_async_remote_copy(src_ref=ins[0].at[:, rows, :], dst_ref=outs[0], send_sem=sems[0], recv_sem=sems[1],
                                          device_id=(x, y, 1 - c), device_id_type=MESH)
        return cp.start, lambda: None, cp.wait

    return _Exchange([g], [jax.ShapeDtypeStruct((4, half, 1024), g.dtype)],
                     [pltpu.SemaphoreType.DMA, pltpu.SemaphoreType.DMA], make)


def _scatter_exchange(s):
    def make(ins, outs, sems):
        s_ref, o_ref = ins[0], outs[0]
        send_sems, recv_sems, local_sem = sems
        x, y, c = _pos()
        me = 2 * x + y
        chips = _other_chips(x, y)
        loc = pltpu.make_async_copy(s_ref.at[me], o_ref.at[me], local_sem)

        def copy(k, src_slot, dst_slot):
            px, py = chips[k]
            return pltpu.make_async_remote_copy(src_ref=s_ref.at[src_slot], dst_ref=o_ref.at[dst_slot], send_sem=send_sems.at[k],
                                                recv_sem=recv_sems.at[k], device_id=(px, py, c), device_id_type=MESH)

        def start():
            loc.start()
            for k, (px, py) in enumerate(chips):
                copy(k, 2 * px + py, me).start()

        def finish():
            for k, (px, py) in enumerate(chips):
                copy(k, me, 2 * px + py).wait_recv()
            for k, (px, py) in enumerate(chips):
                copy(k, 2 * px + py, me).wait_send()
            loc.wait()

        return start, lambda: None, finish

    return _Exchange([s], [jax.ShapeDtypeStruct(s.shape, s.dtype)],
                     [pltpu.SemaphoreType.DMA((3,)), pltpu.SemaphoreType.DMA((3,)), pltpu.SemaphoreType.DMA], make)


def _send_exchange(r):
    def make(ins, outs, sems):
        x, y, c = _pos()
        cp = pltpu.make_async_remote_copy(src_ref=ins[0], dst_ref=outs[0], send_sem=sems[0], recv_sem=sems[1],
                                          device_id=(x, y, 1 - c), device_id_type=MESH)
        return cp.start, lambda: None, cp.wait

    return _Exchange([r], [jax.ShapeDtypeStruct(r.shape, r.dtype)], [pltpu.SemaphoreType.DMA, pltpu.SemaphoreType.DMA], make)


def _run_exchange(ex, name):
    ei, eo = len(ex.args), len(ex.out_shape)

    def body(*refs):
        start, forward, finish = ex.make(refs[:ei], refs[ei:ei + eo], refs[ei + eo:])
        start()
        forward()
        finish()

    anyspec = pl.BlockSpec(memory_space=pl.ANY)
    return pl.pallas_call(body, name=name, in_specs=[anyspec] * ei, out_specs=[anyspec] * eo, out_shape=ex.out_shape,
                          scratch_shapes=ex.scratch, compiler_params=_params(has_side_effects=True))(*ex.args)


def _add_half(g, r, c, name):
    H = r.shape[1]
    tr = H // 2
    assert tr % 16 == 0

    def body(c_ref, g_ref, r_ref, o_ref):
        o_ref[...] = (g_ref[...] + r_ref[...]).astype(BF16)

    grid_spec = pltpu.PrefetchScalarGridSpec(
        num_scalar_prefetch=1, grid=(4, 2),
        in_specs=[pl.BlockSpec((1, tr, 1024), lambda j, i, c_ref: (j, c_ref[0] * 2 + i, 0)),
                  pl.BlockSpec((1, tr, 1024), lambda j, i, c_ref: (j, i, 0))],
        out_specs=pl.BlockSpec((1, tr, 1024), lambda j, i, c_ref: (j, i, 0)))
    return pl.pallas_call(body, name=name, grid_spec=grid_spec,
                          out_shape=jax.ShapeDtypeStruct((4, H, 1024), BF16), compiler_params=_params())(c, g, r)


def _block_diag(w):
    eye = jnp.eye(RNN_BLOCKS, dtype=w.dtype)
    return (eye[:, None, :, None] * w[:, :, None, :]).reshape(D_RNN, D_RNN)


def _diag_blocks(wd):
    d = wd.reshape(RNN_BLOCKS, 64, RNN_BLOCKS, 64)
    return jnp.stack([d[h, :, h, :] for h in range(RNN_BLOCKS)])


def _split_pack(a, first, last):
    out, base = {}, PACK_OFF[first]
    for i in range(first, last):
        s = a[:, PACK_OFF[i] - base:PACK_OFF[i + 1] - base]
        out[BIG_KEYS[i]] = s.reshape(4 * 256, 256) if BIG_KEYS[i] == "w_p_t" else s.reshape(-1, 1024)
    return out


def _pack_grads(big, first, last):
    return jnp.concatenate([big[BIG_KEYS[i]].reshape(4, PACK_ROWS[i], 1024) for i in range(first, last)], axis=1)


def _layer_grads(x, p, tgt, gw, small, shard=None, core=None):
    row = lambda v: v.reshape(1, -1)
    wa = _block_diag(small["gate_a_w"]).astype(MXU_DTYPE)
    wx = _block_diag(small["gate_x_w"]).astype(MXU_DTYPE)
    sinks = small["attn_sinks"].reshape(1, HEADS)

    dist = shard is not None
    q, kv, xr, gr, xb = _in_proj(x, gw["w_in_t"])
    att, *ga = _attn_fwd(q, kv, sinks, _gather_exchange(shard[PACK_OFF[1]:PACK_OFF[3]]) if dist else None)
    xc, h, rec, *gb = _rnn_fwd(xr, gr, small["rnn_conv_w"], row(small["rnn_conv_b"]), wa, row(small["gate_a_b"]),
                               wx, row(small["gate_x_b"]), row(small["lru_lambda"]),
                               _gather_exchange(shard[PACK_OFF[3]:PACK_OFF[6]]) if dist else None)
    if dist:
        gw = {**gw, **_split_pack(ga[0], 1, 3), **_split_pack(gb[0], 3, 6)}
    g1, b1 = row(small["ln1_g"]), row(small["ln1_b"])
    fcw = small["ffn_conv_w"].reshape(3, NC, FF_CHUNK).transpose(1, 0, 2)
    fcb = small["ffn_conv_b"].reshape(NC, 1, FF_CHUNK)
    z1, h1b = _out_proj(att, rec, x, gw["w_out"], g1, b1)
    gate, val, act = _ffn_up(h1b, gw["w_up_t"], fcw, fcb)
    dz2, dz2b, dpre, dpp, vec2 = _ffn_down(act, z1, p, tgt, gw["w_down"], gw["w_g"], gw["w_p_t"], g1, b1,
                                           row(small["ln2_g"]), row(small["ln2_b"]), row(small["ple_gate_b"]))
    dup, dfc = _ffn_bwd(dz2b, gate, val, gw["w_down"], fcw, fcb)
    dz1, vec1 = _ffn_dh1(dup, dz2, dpre, z1, gw["w_up_t"], gw["w_g"], g1, b1)
    big = {
        "w_up_t": _weight_grad(dup.reshape(2 * NC, -1, FF_CHUNK), h1b, FF_CHUNK, "dw_up",
                               out_block=lambda m: (m % 2) * NC + m // 2),
        "w_down": _weight_grad(act, dz2b, 512, "dw_down"),
        "w_g": _weight_grad(h1b, dpre, 512, "dw_gate"),
        "w_p_t": _weight_grad(dpp, p.astype(BF16), 512, "dw_proj"),
    }
    reduced = None
    if dist:
        g_ffn = _pack_grads(big, 2, 6)
        ex = _swap_exchange(g_ffn)
    datt, drec, dz1b, *got = _out_proj_bwd(dz1, gw["w_out"], ex if dist else None)
    if dist:
        ex = _scatter_exchange(_add_half(g_ffn, got[0], core, "add_half_ffn"))
    dxr, dgr, dwa, dwx, dvec, *got = _rnn_bwd(drec, gr, h, xc, xr, small["rnn_conv_w"], wa, row(small["gate_a_b"]),
                                              wx, row(small["gate_x_b"]), row(small["lru_lambda"]), ex if dist else None)
    if dist:
        mine = _add4(got[0], "add_chips_ffn")
        ex = _send_exchange(mine)
    dq, dkv, dsinks, *got = _attn_bwd(q, kv, datt, sinks, ex if dist else None)
    if dist:
        reduced = (mine, got[0])
        big = {}
    grad_x, du = _in_proj_bwd(dq, dkv, dxr, dgr, dz1, gw["w_in_t"])
    big["w_in_t"] = _weight_grad(du, xb, 256, "dw_in")
    big["w_out"] = _weight_grad(jnp.concatenate([att, rec], axis=1), dz1b, 512, "dw_out")
    sg = {
        "attn_sinks": dsinks[:, 0],
        "rnn_conv_w": dvec[4:8],
        "rnn_conv_b": dvec[3],
        "gate_a_w": _diag_blocks(dwa),
        "gate_a_b": dvec[0],
        "gate_x_w": _diag_blocks(dwx),
        "gate_x_b": dvec[1],
        "lru_lambda": dvec[2],
        "ln1_g": vec1[0],
        "ln1_b": vec1[1],
        "ffn_conv_w": dfc[:, 0:3].transpose(1, 0, 2).reshape(3, D_FF),
        "ffn_conv_b": dfc[:, 3].reshape(D_FF),
        "ple_gate_b": vec2[3],
        "ln2_g": vec2[1],
        "ln2_b": vec2[2],
    }
    return grad_x, big, sg, vec2[0, 0:1], reduced


BIG = ("w_in", "w_out", "w_ffn_up", "w_ffn_down", "ple_gate_w", "ple_proj")
BIG_KEYS = ("w_in_t", "w_out", "w_up_t", "w_down", "w_g", "w_p_t")
BIG_T = (True, False, True, False, False, True)
SMALL = ("attn_sinks", "rnn_conv_w", "rnn_conv_b", "gate_a_w", "gate_a_b", "gate_x_w", "gate_x_b", "lru_lambda",
         "ln1_g", "ln1_b", "ffn_conv_w", "ffn_conv_b", "ple_gate_b", "ln2_g", "ln2_b")
SHARDED_SMALL = ("rnn_conv_w", "ffn_conv_w")
WEIGHTS = ("w_in", "attn_sinks", "rnn_conv_w", "rnn_conv_b", "gate_a_w", "gate_a_b", "gate_x_w", "gate_x_b",
           "lru_lambda", "w_out", "ln1_g", "ln1_b", "w_ffn_up", "ffn_conv_w", "ffn_conv_b", "w_ffn_down",
           "ple_gate_w", "ple_gate_b", "ple_proj", "ln2_g", "ln2_b")


def _pack_big(d, first=0, last=6):
    parts = []
    for name, t in zip(BIG[first:last], BIG_T[first:last]):
        a = d[name]
        a = a.T if t else a
        parts.append(a.reshape(-1, 1024))
    return jnp.concatenate(parts, axis=0)


def _unpack_big(a, first, last):
    out, base = {}, PACK_OFF[first]
    shapes = {"w_in": (448, 1024), "w_out": (256, 1024), "w_ffn_up": (1536, 1024), "w_ffn_down": (768, 1024),
              "ple_gate_w": (256, 1024), "ple_proj": (256, 256)}
    for i in range(first, last):
        s = a[PACK_OFF[i] - base:PACK_OFF[i + 1] - base].reshape(shapes[BIG[i]])
        out[BIG[i]] = (s.T if BIG_T[i] else s)[None]
    return out


def _pack_vecs(items):
    parts, offs, n = [], [], 0
    for a in items:
        f = a.reshape(-1).astype(F32)
        pad = (-f.shape[0]) % 128
        parts.append(jnp.pad(f, (0, pad)))
        offs.append(n)
        n += (f.shape[0] + pad) // 128
    padr = (-n) % 8
    if padr:
        parts.append(jnp.zeros((padr * 128,), F32))
    return jnp.concatenate(parts).reshape(-1, 128), offs


def _unpack_vecs(a, offs, shapes):
    flat = a.reshape(-1)
    out = []
    for o, s in zip(offs, shapes):
        n = 1
        for d in s:
            n *= d
        out.append(flat[o * 128:o * 128 + n].reshape(s))
    return out


def kernel(x, p, w_in, attn_sinks, rnn_conv_w, rnn_conv_b, gate_a_w, gate_a_b, gate_x_w, gate_x_b, lru_lambda, w_out, ln1_g, ln1_b, w_ffn_up, ffn_conv_w, ffn_conv_b, w_ffn_down, ple_gate_w, ple_gate_b, ple_proj, ln2_g, ln2_b, loss_target, m_w_in, m_attn_sinks, m_rnn_conv_w, m_rnn_conv_b, m_gate_a_w, m_gate_a_b, m_gate_x_w, m_gate_x_b, m_lru_lambda, m_w_out, m_ln1_g, m_ln1_b, m_w_ffn_up, m_ffn_conv_w, m_ffn_conv_b, m_w_ffn_down, m_ple_gate_w, m_ple_gate_b, m_ple_proj, m_ln2_g, m_ln2_b, v_w_in, v_attn_sinks, v_rnn_conv_w, v_rnn_conv_b, v_gate_a_w, v_gate_a_b, v_gate_x_w, v_gate_x_b, v_lru_lambda, v_w_out, v_ln1_g, v_ln1_b, v_w_ffn_up, v_ffn_conv_w, v_ffn_conv_b, v_w_ffn_down, v_ple_gate_w, v_ple_gate_b, v_ple_proj, v_ln2_g, v_ln2_b):
    w = dict(w_in=w_in, attn_sinks=attn_sinks, rnn_conv_w=rnn_conv_w, rnn_conv_b=rnn_conv_b, gate_a_w=gate_a_w,
             gate_a_b=gate_a_b, gate_x_w=gate_x_w, gate_x_b=gate_x_b, lru_lambda=lru_lambda, w_out=w_out, ln1_g=ln1_g,
             ln1_b=ln1_b, w_ffn_up=w_ffn_up, ffn_conv_w=ffn_conv_w, ffn_conv_b=ffn_conv_b, w_ffn_down=w_ffn_down,
             ple_gate_w=ple_gate_w, ple_gate_b=ple_gate_b, ple_proj=ple_proj, ln2_g=ln2_g, ln2_b=ln2_b)
    m = dict(w_in=m_w_in, attn_sinks=m_attn_sinks, rnn_conv_w=m_rnn_conv_w, rnn_conv_b=m_rnn_conv_b, gate_a_w=m_gate_a_w,
             gate_a_b=m_gate_a_b, gate_x_w=m_gate_x_w, gate_x_b=m_gate_x_b, lru_lambda=m_lru_lambda, w_out=m_w_out,
             ln1_g=m_ln1_g, ln1_b=m_ln1_b, w_ffn_up=m_w_ffn_up, ffn_conv_w=m_ffn_conv_w, ffn_conv_b=m_ffn_conv_b,
             w_ffn_down=m_w_ffn_down, ple_gate_w=m_ple_gate_w, ple_gate_b=m_ple_gate_b, ple_proj=m_ple_proj,
             ln2_g=m_ln2_g, ln2_b=m_ln2_b)
    v = dict(w_in=v_w_in, attn_sinks=v_attn_sinks, rnn_conv_w=v_rnn_conv_w, rnn_conv_b=v_rnn_conv_b, gate_a_w=v_gate_a_w,
             gate_a_b=v_gate_a_b, gate_x_w=v_gate_x_w, gate_x_b=v_gate_x_b, lru_lambda=v_lru_lambda, w_out=v_w_out,
             ln1_g=v_ln1_g, ln1_b=v_ln1_b, w_ffn_up=v_w_ffn_up, ffn_conv_w=v_ffn_conv_w, ffn_conv_b=v_ffn_conv_b,
             w_ffn_down=v_w_ffn_down, ple_gate_w=v_ple_gate_w, ple_gate_b=v_ple_gate_b, ple_proj=v_ple_proj,
             ln2_g=v_ln2_g, ln2_b=v_ln2_b)
    w, m, v = ({k: a[0] for k, a in d.items()} for d in (w, m, v))
    chip = 2 * lax.axis_index("x") + lax.axis_index("y")
    core = lax.axis_index("c")

    wpack = _pack_big(w)
    cpack, _ = _pack_vecs([w["rnn_conv_w"], w["ffn_conv_w"]])
    shard = wpack.astype(MXU_DTYPE)
    g_in, gcp = _gather_first(shard[PACK_OFF[0]:PACK_OFF[1]], cpack)
    gw = _split_pack(g_in, 0, 1)
    small = {k: w[k] for k in SMALL}
    small["rnn_conv_w"] = gcp[:, 0:4].reshape(4, 4, 128).transpose(1, 0, 2).reshape(4, 512)
    small["ffn_conv_w"] = gcp[:, 4:22].reshape(4, 3, 768).transpose(1, 0, 2).reshape(3, 3072)

    core1 = core.reshape(1).astype(jnp.int32)
    grad_x, big, sg, loss, ffn_halves = _layer_grads(x[0], p[0, 0], loss_target[0], gw, small, shard, core1)

    spack, offs = _pack_vecs([sg[k] for k in SMALL] + [loss])
    ssum = _allreduce_small(spack)
    shapes = [sg[k].shape for k in SMALL] + [(1,)]
    red = dict(zip(SMALL + ("loss",), _unpack_vecs(ssum, offs, shapes)))
    red["rnn_conv_w"] = lax.dynamic_slice_in_dim(red["rnn_conv_w"], chip * 128, 128, axis=1)
    red["ffn_conv_w"] = lax.dynamic_slice_in_dim(red["ffn_conv_w"], chip * 768, 768, axis=1)

    g_mix = _pack_grads(big, 0, 2)
    sib, = _run_exchange(_swap_exchange(g_mix), "swap_mix")
    from_chips, = _run_exchange(_scatter_exchange(_add_half(g_mix, sib, core1, "add_half_mix")), "scatter_mix")
    mix_mine = _add4(from_chips, "add_chips_mix")
    mix_other, = _run_exchange(_send_exchange(mix_mine), "send_mix")

    mix_out = _adamw_halves(wpack[:PACK_OFF[2]], mix_mine, mix_other, _pack_big(m, 0, 2), _pack_big(v, 0, 2), core1,
                            "adamw_mix")
    ffn_out = _adamw_halves(wpack[PACK_OFF[2]:], *ffn_halves, _pack_big(m, 2, 6), _pack_big(v, 2, 6), core1, "adamw_ffn")
    wsm, offs2 = _pack_vecs([w[k] for k in SMALL])
    gsm, _ = _pack_vecs([red[k] for k in SMALL])
    msm, _ = _pack_vecs([m[k] for k in SMALL])
    vsm, _ = _pack_vecs([v[k] for k in SMALL])
    dsm, nmsm, nvsm = _adamw(wsm, gsm, msm, vsm, "adamw_small")
    shapes2 = [w[k].shape for k in SMALL]

    def named(n, smallp):
        d = {**_unpack_big(mix_out[n], 0, 2), **_unpack_big(ffn_out[n], 2, 6)}
        d.update({k: a[None] for k, a in zip(SMALL, _unpack_vecs(smallp, offs2, shapes2))})
        return [d[k] for k in WEIGHTS]

    return (red["loss"].reshape(()), grad_x[None], *named(0, gsm), *named(1, dsm), *named(2, nmsm), *named(3, nvsm))
```

```python
import functools

import jax
import jax.numpy as jnp
from jax import lax
from jax.experimental import pallas as pl
from jax.experimental.pallas import tpu as pltpu

F32 = jnp.float32
BF16 = jnp.bfloat16
MXU_DTYPE = jnp.bfloat16

D = 1024
D_ATT = 512
D_KV = 128
D_RNN = 512
D_IN = 1792
D_FF = 3072
FF_CHUNK = 512
PLE = 256
HEADS = 8
HEAD_DIM = 64
BLK = 128
RNN_BLOCKS = 8
LN_EPS = 1e-5
LRU_C = 8.0
ALPHA = float(2.0 ** 0.25)
SCALE = HEAD_DIM ** -0.5
NEG = -1e30

ADAM_LR = 0.001
ADAM_B1 = 0.9
ADAM_B2 = 0.999
ADAM_EPS = 1e-08
ADAM_WD = 0.01
ADAM_STEP = 10

VMEM_LIMIT_BYTES = 56 * 1024 * 1024
MESH = pl.DeviceIdType.MESH

PACK_ROWS = (448, 256, 1536, 768, 256, 64)
PACK_OFF = tuple(sum(PACK_ROWS[:i]) for i in range(len(PACK_ROWS) + 1))
PACK_TOTAL = PACK_OFF[-1]
HALF = PACK_TOTAL // 2


def _params(**kw):
    return pltpu.CompilerParams(vmem_limit_bytes=VMEM_LIMIT_BYTES, **kw)


def _mm(a, b):
    return jnp.dot(a.astype(MXU_DTYPE), b.astype(MXU_DTYPE), preferred_element_type=F32)


def _mm_nt(a, b):
    return lax.dot_general(a.astype(MXU_DTYPE), b.astype(MXU_DTYPE), (((1,), (1,)), ((), ())),
                           preferred_element_type=F32)


def _mm_tn(a, b):
    return lax.dot_general(a.astype(MXU_DTYPE), b.astype(MXU_DTYPE), (((0,), (0,)), ((), ())),
                           preferred_element_type=F32)


def _sigmoid(x):
    return 1.0 / (1.0 + jnp.exp(-x))


def _gelu(x):
    c = 0.7978845608028654
    k = 0.044715
    t = jnp.tanh(c * (x + k * x * x * x))
    g = 0.5 * x * (1.0 + t)
    dg = 0.5 * (1.0 + t) + 0.5 * x * (1.0 - t * t) * c * (1.0 + 3.0 * k * x * x)
    return g, dg


def _expm1(x):
    poly = x * (1.0 + x * (0.5 + x * (1.0 / 6.0 + x * (1.0 / 24.0 + x * (1.0 / 120.0)))))
    return jnp.where(jnp.abs(x) < 0.03, poly, jnp.exp(x) - 1.0)


def _softplus(x):
    return jnp.maximum(x, 0.0) + jnp.log(1.0 + jnp.exp(-jnp.abs(x)))


def _ln(z, g, b):
    mu = jnp.mean(z, axis=-1, keepdims=True)
    zc = z - mu
    var = jnp.mean(zc * zc, axis=-1, keepdims=True)
    rstd = lax.rsqrt(var + LN_EPS)
    xhat = zc * rstd
    return xhat * g + b, xhat, rstd


def _ln_bwd(dy, xhat, rstd, g):
    dxh = dy * g
    m1 = jnp.mean(dxh, axis=-1, keepdims=True)
    m2 = jnp.mean(dxh * xhat, axis=-1, keepdims=True)
    return rstd * (dxh - m1 - xhat * m2)


def _colsum(x):
    return jnp.sum(x, axis=0, keepdims=True)


def _full(shape):
    nd = len(shape)
    return pl.BlockSpec(shape, lambda *_: (0,) * nd)


def _rows(tm, cols, fn=None):
    if fn is None:
        return pl.BlockSpec((tm, cols), lambda i: (i, 0))
    return pl.BlockSpec((tm, cols), lambda i: (fn(i), 0))


def _heads(tm):
    return pl.BlockSpec((HEADS, tm, HEAD_DIM), lambda i: (0, i, 0))


def _in_proj(x, w_in_t):
    T = x.shape[0]
    tm = 512

    def body(x_ref, w_ref, q_ref, kv_ref, xr_ref, gr_ref, xb_ref):
        xb = x_ref[...].astype(MXU_DTYPE)
        xb_ref[...] = xb.astype(BF16)
        q = _mm_nt(xb, w_ref[0:512, :])
        for h in range(HEADS):
            q_ref[h] = q[:, h * 64:(h + 1) * 64].astype(BF16)
        kv_ref[...] = _mm_nt(xb, w_ref[512:768, :]).astype(BF16)
        xr_ref[...] = _mm_nt(xb, w_ref[768:1280, :])
        gr_ref[...] = _mm_nt(xb, w_ref[1280:1792, :])

    return pl.pallas_call(
        body, name="in_proj", grid=(T // tm,),
        in_specs=[_rows(tm, D), _full((D_IN, D))],
        out_specs=[_heads(tm), _rows(tm, 256), _rows(tm, 512), _rows(tm, 512), _rows(tm, D)],
        out_shape=[jax.ShapeDtypeStruct((HEADS, T, 64), BF16), jax.ShapeDtypeStruct((T, 256), BF16),
                   jax.ShapeDtypeStruct((T, 512), F32), jax.ShapeDtypeStruct((T, 512), F32),
                   jax.ShapeDtypeStruct((T, D), BF16)],
        compiler_params=_params(),
    )(x, w_in_t)


def _attn_band(kv_ref, i):
    cur = pl.multiple_of(i * BLK, BLK)
    prev = pl.multiple_of(jnp.maximum(i - 1, 0) * BLK, BLK)
    band = jnp.concatenate([kv_ref[pl.ds(prev, BLK), :], kv_ref[pl.ds(cur, BLK), :]], axis=0)
    key = lax.broadcasted_iota(jnp.int32, (2 * BLK, 4 * BLK), 0)
    qry = lax.broadcasted_iota(jnp.int32, (2 * BLK, 4 * BLK), 1) & (BLK - 1)
    in_prev = jnp.logical_and(jnp.logical_and(key < BLK, key > qry), i > 0)
    mask = jnp.logical_or(in_prev, jnp.logical_and(key >= BLK, key - BLK <= qry))
    return band, mask, cur, prev


def _attn_scores(band, mask, qs, s_ref, g):
    st = jnp.where(mask, _mm_nt(band[:, g * 64:(g + 1) * 64], qs) * SCALE, NEG)
    lane = lax.broadcasted_iota(jnp.int32, (1, 4 * BLK), 1)
    sv = jnp.where(lane < BLK, s_ref[0, 4 * g],
                   jnp.where(lane < 2 * BLK, s_ref[0, 4 * g + 1], jnp.where(lane < 3 * BLK, s_ref[0, 4 * g + 2], s_ref[0, 4 * g + 3])))
    m = jnp.maximum(jnp.max(st, axis=0, keepdims=True), sv)
    p = jnp.exp(st - m)
    ps = jnp.exp(sv - m)
    return p, ps, jnp.sum(p, axis=0, keepdims=True) + ps


def _pos():
    return lax.axis_index("x"), lax.axis_index("y"), lax.axis_index("c")


def _other_chips(x, y):
    return [(1 - x, y), (x, 1 - y), (1 - x, 1 - y)]


def _gather_steps(w_ref, gw_ref, send_sems, recv_sems, local_sem):
    x, y, c = _pos()
    me = 2 * x + y
    chips = _other_chips(x, y)
    half = w_ref.shape[0] // 2
    mine = pl.ds(pl.multiple_of(c * half, 16), half)
    theirs = pl.ds(pl.multiple_of((1 - c) * half, 16), half)
    loc = pltpu.make_async_copy(w_ref, gw_ref.at[me], local_sem)

    def copy(k, src, dst, to):
        return pltpu.make_async_remote_copy(src_ref=src, dst_ref=dst, send_sem=send_sems.at[k], recv_sem=recv_sems.at[k],
                                            device_id=to, device_id_type=MESH)

    def out(k):
        px, py = chips[k]
        return copy(k, w_ref.at[mine], gw_ref.at[me, mine], (px, py, c))

    def fwd(k, rows):
        px, py = chips[k]
        return copy(3 + k, gw_ref.at[2 * px + py, rows], gw_ref.at[2 * px + py, rows], (x, y, 1 - c))

    def start():
        loc.start()
        for k in range(3):
            out(k).start()

    def forward():
        for k in range(3):
            px, py = chips[k]
            copy(k, w_ref.at[mine], gw_ref.at[2 * px + py, mine], (px, py, c)).wait_recv()
            fwd(k, mine).start()

    def finish():
        for k in range(3):
            fwd(k, theirs).wait_recv()
        for k in range(3):
            out(k).wait_send()
            fwd(k, mine).wait_send()
        loc.wait()

    return start, forward, finish


GATHER_SCRATCH = [pltpu.SemaphoreType.DMA((6,)), pltpu.SemaphoreType.DMA((6,)), pltpu.SemaphoreType.DMA]


class _Exchange:
    def __init__(self, args, out_shape, scratch, make):
        self.args, self.out_shape, self.scratch, self.make = list(args), list(out_shape), list(scratch), make


def _gather_exchange(wsrc):
    return _Exchange([wsrc], [jax.ShapeDtypeStruct((4,) + wsrc.shape, wsrc.dtype)], GATHER_SCRATCH,
                     lambda ins, outs, sems: _gather_steps(ins[0], outs[0], *sems))


def _launch(body, name, grid, in_specs, out_specs, out_shape, scratch, args, exchange=None):
    if exchange is None:
        return pl.pallas_call(body, name=name, grid=grid, in_specs=in_specs, out_specs=out_specs, out_shape=out_shape,
                              scratch_shapes=scratch, compiler_params=_params())(*args)
    n_in, n_out, ei, eo, ns = len(in_specs), len(out_specs), len(exchange.args), len(exchange.out_shape), len(exchange.scratch)
    nsteps = 1
    for g in grid:
        nsteps *= g

    def wrapped(*refs):
        ins, xin = refs[:n_in], refs[n_in:n_in + ei]
        outs, xout = refs[n_in + ei:n_in + ei + n_out], refs[n_in + ei + n_out:n_in + ei + n_out + eo]
        rest = refs[n_in + ei + n_out + eo:]
        own, sems = rest[:len(rest) - ns], rest[len(rest) - ns:]
        start, forward, finish = exchange.make(xin, xout, sems)
        i = pl.program_id(0)
        for d in range(1, len(grid)):
            i = i * grid[d] + pl.program_id(d)
        pl.when(i == 0)(start)
        body(*ins, *outs, *own)
        pl.when(i == max(nsteps - 3, 0))(forward)
        pl.when(i == nsteps - 1)(finish)

    anyspec = pl.BlockSpec(memory_space=pl.ANY)
    return pl.pallas_call(
        wrapped, name=name + "_x", grid=grid, in_specs=list(in_specs) + [anyspec] * ei, out_specs=list(out_specs) + [anyspec] * eo,
        out_shape=list(out_shape) + exchange.out_shape, scratch_shapes=list(scratch) + exchange.scratch,
        compiler_params=_params(has_side_effects=True))(*args, *exchange.args)


def _attn_fwd(q, kv, sinks, exchange=None):
    T = kv.shape[0]

    def body(q_ref, kv_ref, s_ref, o_ref):
        i = pl.program_id(0)
        band, mask, _, _ = _attn_band(kv_ref, i)
        for g in range(2):
            qs = q_ref[4 * g:4 * g + 4].reshape(4 * BLK, HEAD_DIM)
            p, _, den = _attn_scores(band, mask, qs, s_ref, g)
            ot = _mm_tn(band[:, 128:256], p) / den
            for hh in range(4):
                o = ot[:, hh * BLK:(hh + 1) * BLK].T
                o_ref[:, (4 * g + hh) * 64:(4 * g + hh + 1) * 64] = o[:, g * 64:(g + 1) * 64].astype(BF16)

    return _launch(body, "attn_fwd", (T // BLK,), [_heads(BLK), _full((T, 256)), pl.BlockSpec(memory_space=pltpu.SMEM)],
                   [_rows(BLK, 512)], [jax.ShapeDtypeStruct((T, 512), BF16)], [], (q, kv, sinks), exchange)


def _attn_bwd(q, kv, do, sinks, exchange=None):
    T = kv.shape[0]

    def body(q_ref, kv_ref, do_ref, s_ref, dq_ref, dkv_ref, ds_ref):
        i = pl.program_id(0)
        band, mask, cur, prev = _attn_band(kv_ref, i)

        @pl.when(i == 0)
        def _():
            ds_ref[...] = jnp.zeros_like(ds_ref)

        for g in range(2):
            qs = q_ref[4 * g:4 * g + 4].reshape(4 * BLK, HEAD_DIM)
            dos = do_ref[4 * g:4 * g + 4].reshape(4 * BLK, HEAD_DIM)
            p, ps, den = _attn_scores(band, mask, qs, s_ref, g)
            inv = 1.0 / den
            p = p * inv
            dpt = _mm_nt(band[:, 128 + g * 64:192 + g * 64], dos)
            delta = jnp.sum(p * dpt, axis=0, keepdims=True)
            dst = p * (dpt - delta)
            dsv = -(ps * inv) * delta
            for hh in range(4):
                dsink = jnp.sum(dsv[:, hh * BLK:(hh + 1) * BLK], axis=1, keepdims=True)
                ds_ref[4 * g + hh:4 * g + hh + 1, :] += jnp.broadcast_to(dsink, (1, 128))
            dqt = _mm_tn(band[:, 0:128], dst) * SCALE
            for hh in range(4):
                dqh = dqt[:, hh * BLK:(hh + 1) * BLK].T
                dq_ref[:, (4 * g + hh) * 64:(4 * g + hh + 1) * 64] = dqh[:, g * 64:(g + 1) * 64].astype(BF16)
            dk = _mm(dst, qs) * SCALE
            dv = _mm(p, dos)
            dkv_ref[pl.ds(cur, BLK), g * 64:(g + 1) * 64] = dk[BLK:2 * BLK]
            dkv_ref[pl.ds(cur, BLK), 128 + g * 64:192 + g * 64] = dv[BLK:2 * BLK]
            dkv_ref[pl.ds(prev, BLK), g * 64:(g + 1) * 64] += dk[0:BLK]
            dkv_ref[pl.ds(prev, BLK), 128 + g * 64:192 + g * 64] += dv[0:BLK]

    return _launch(body, "attn_bwd", (T // BLK,),
                   [_heads(BLK), _full((T, 256)), _heads(BLK), pl.BlockSpec(memory_space=pltpu.SMEM)],
                   [_rows(BLK, 512), _full((T, 256)), _full((8, 128))],
                   [jax.ShapeDtypeStruct((T, 512), BF16), jax.ShapeDtypeStruct((T, 256), F32),
                    jax.ShapeDtypeStruct((8, 128), F32)], [], (q, kv, do, sinks), exchange)


def _rows8(tm, cols):
    return lax.broadcasted_iota(jnp.int32, (tm, cols), 0) & 7


def _lru_gates(xc, wa, ba, wx, bx, lam):
    r = _sigmoid(_mm(xc, wa) + ba)
    ii = _sigmoid(_mm(xc, wx) + bx)
    sp = _softplus(-lam)
    la = -LRU_C * r * sp
    a = jnp.exp(la)
    m = jnp.sqrt(-_expm1(2.0 * la))
    return r, ii, sp, a, m


def _rnn_fwd(xr, gr, cw, cb, wa, ba, wx, bx, lam, exchange=None):
    T = xr.shape[0]
    tm = 256
    C = D_RNN

    def body(xr_ref, gr_ref, cw_ref, cb_ref, wa_ref, ba_ref, wx_ref, bx_ref, lam_ref,
             xc_ref, h_ref, rec_ref, ext, a_s, b_s, carry):
        i = pl.program_id(0)

        @pl.when(i == 0)
        def _():
            ext[0:8, :] = jnp.zeros((8, C), F32)
            carry[...] = jnp.zeros((8, C), F32)

        ext[8:8 + tm, :] = xr_ref[...]
        xc = cb_ref[...] + cw_ref[3:4, :] * ext[8:8 + tm, :]
        for k in range(3):
            xc = xc + cw_ref[k:k + 1, :] * ext[5 + k:5 + k + tm, :]
        ext[0:8, :] = ext[tm:tm + 8, :]
        xc_ref[...] = xc
        _, ii, _, a, m = _lru_gates(xc, wa_ref[...], ba_ref[...], wx_ref[...], bx_ref[...], lam_ref[...])
        b = m * ii * xc
        r8 = _rows8(tm, C)
        for d in (1, 2, 4):
            ok = r8 >= d
            a_sh = jnp.where(ok, pltpu.roll(a, d, 0), 1.0)
            b_sh = jnp.where(ok, pltpu.roll(b, d, 0), 0.0)
            b = a * b_sh + b
            a = a * a_sh
        a_s[...] = a
        b_s[...] = b

        def step(g, hin):
            s = pl.multiple_of(g * 8, 8)
            hg = a_s[pl.ds(s, 8), :] * hin + b_s[pl.ds(s, 8), :]
            h_ref[pl.ds(s, 8), :] = hg
            return jnp.broadcast_to(hg[7:8, :], (8, C))

        carry[...] = lax.fori_loop(0, tm // 8, step, carry[...])
        ge, _ = _gelu(gr_ref[...])
        rec_ref[...] = (h_ref[...] * ge).astype(BF16)

    vec = _full((1, C))
    in_specs = [_rows(tm, C), _rows(tm, C), _full((4, C)), vec, _full((C, C)), vec, _full((C, C)), vec, vec]
    out_specs = [_rows(tm, C), _rows(tm, C), _rows(tm, C)]
    out_shape = [jax.ShapeDtypeStruct((T, C), F32), jax.ShapeDtypeStruct((T, C), F32), jax.ShapeDtypeStruct((T, C), BF16)]
    scratch = [pltpu.VMEM((tm + 8, C), F32), pltpu.VMEM((tm, C), F32), pltpu.VMEM((tm, C), F32), pltpu.VMEM((8, C), F32)]
    return _launch(body, "rnn_fwd", (T // tm,), in_specs, out_specs, out_shape, scratch,
                   (xr, gr, cw, cb, wa, ba, wx, bx, lam), exchange)


def _rnn_bwd(drec, gr, h, xc, xr, cw, wa, ba, wx, bx, lam, exchange=None):
    T = xr.shape[0]
    tm = 256
    C = D_RNN
    nt = T // tm
    t8 = tm // 8

    def body(drec_ref, gr_ref, h_ref, hp_ref, xc_ref, xr_ref, xrp_ref, cw_ref, wa_ref, ba_ref, wx_ref, bx_ref,
             lam_ref, dxr_ref, dgr_ref, dwa_ref, dwx_ref, dvec_ref, c_s, g_s, gout, ext, xext, anext, gcarry):
        i = pl.program_id(0)
        j = nt - 1 - i

        @pl.when(i == 0)
        def _():
            dwa_ref[...] = jnp.zeros_like(dwa_ref)
            dwx_ref[...] = jnp.zeros_like(dwx_ref)
            dvec_ref[...] = jnp.zeros_like(dvec_ref)
            anext[...] = jnp.zeros((8, C), F32)
            gcarry[...] = jnp.zeros((8, C), F32)
            ext[tm:tm + 8, :] = jnp.zeros((8, C), F32)

        xc = xc_ref[...]
        lam = lam_ref[...]
        r, ii, sp, a, m = _lru_gates(xc, wa_ref[...], ba_ref[...], wx_ref[...], bx_ref[...], lam)
        ge, dge = _gelu(gr_ref[...])
        drec = drec_ref[...]
        hh = h_ref[...]
        dgr_ref[...] = (drec * hh * dge).astype(BF16)
        dh = drec * ge
        rowi = lax.broadcasted_iota(jnp.int32, (tm, C), 0)
        c = jnp.where(rowi == tm - 1, jnp.broadcast_to(anext[0:1, :], (tm, C)), pltpu.roll(a, tm - 1, 0))
        anext[...] = a[0:8, :]
        r8 = rowi & 7
        gg = dh
        for d in (1, 2, 4):
            ok = r8 < 8 - d
            c_sh = jnp.where(ok, pltpu.roll(c, tm - d, 0), 1.0)
            g_sh = jnp.where(ok, pltpu.roll(gg, tm - d, 0), 0.0)
            gg = c * g_sh + gg
            c = c * c_sh
        c_s[...] = c
        g_s[...] = gg

        def step(k, gin):
            s = pl.multiple_of((t8 - 1 - k) * 8, 8)
            og = c_s[pl.ds(s, 8), :] * gin + g_s[pl.ds(s, 8), :]
            gout[pl.ds(s, 8), :] = og
            return jnp.broadcast_to(og[0:1, :], (8, C))

        gcarry[...] = lax.fori_loop(0, t8, step, gcarry[...])
        G = gout[...]
        hprev_row = jnp.where(j > 0, hp_ref[7:8, :], 0.0)
        hprev = jnp.where(rowi == 0, jnp.broadcast_to(hprev_row, (tm, C)), pltpu.roll(hh, 1, 0))
        da = G * hprev
        dm = G * ii * xc
        di = G * m * xc
        dxc = G * m * ii
        dla = da * a - dm * a * a / m
        dr = dla * (-LRU_C * sp)
        dsp = _colsum(dla * (-LRU_C * r))
        dlam = dsp * (-_sigmoid(-lam))
        dpr = dr * r * (1.0 - r)
        dpi = di * ii * (1.0 - ii)
        dxc = dxc + _mm_nt(dpr, wa_ref[...]) + _mm_nt(dpi, wx_ref[...])
        dwa_ref[...] += _mm_tn(xc, dpr)
        dwx_ref[...] += _mm_tn(xc, dpi)
        dvec_ref[0:1, :] += _colsum(dpr)
        dvec_ref[1:2, :] += _colsum(dpi)
        dvec_ref[2:3, :] += dlam
        dvec_ref[3:4, :] += _colsum(dxc)
        ext[0:tm, :] = dxc
        dxr = cw_ref[3:4, :] * dxc
        for k in range(3):
            dxr = dxr + cw_ref[k:k + 1, :] * ext[3 - k:3 - k + tm, :]
        ext[tm:tm + 8, :] = dxc[0:8, :]
        dxr_ref[...] = dxr.astype(BF16)
        xext[0:8, :] = jnp.where(j > 0, xrp_ref[...], 0.0)
        xext[8:8 + tm, :] = xr_ref[...]
        for k in range(4):
            dvec_ref[4 + k:5 + k, :] += _colsum(dxc * xext[5 + k:5 + k + tm, :])

    rev = lambda i: nt - 1 - i
    prev8 = lambda i: jnp.maximum((nt - 1 - i) * t8 - 1, 0)
    vec = _full((1, C))
    return _launch(
        body, "rnn_bwd", (nt,),
        [_rows(tm, C, rev), _rows(tm, C, rev), _rows(tm, C, rev), _rows(8, C, prev8), _rows(tm, C, rev),
         _rows(tm, C, rev), _rows(8, C, prev8), _full((4, C)), _full((C, C)), vec, _full((C, C)), vec, vec],
        [_rows(tm, C, rev), _rows(tm, C, rev), _full((C, C)), _full((C, C)), _full((8, C))],
        [jax.ShapeDtypeStruct((T, C), BF16), jax.ShapeDtypeStruct((T, C), BF16),
         jax.ShapeDtypeStruct((C, C), F32), jax.ShapeDtypeStruct((C, C), F32), jax.ShapeDtypeStruct((8, C), F32)],
        [pltpu.VMEM((tm, C), F32), pltpu.VMEM((tm, C), F32), pltpu.VMEM((tm, C), F32),
         pltpu.VMEM((tm + 8, C), F32), pltpu.VMEM((tm + 8, C), F32), pltpu.VMEM((8, C), F32), pltpu.VMEM((8, C), F32)],
        (drec, gr, h, h, xc, xr, xr, cw, wa, ba, wx, bx, lam), exchange)


def _out_proj(att, rec, x, w_out, g1, b1):
    T = x.shape[0]
    tm = 512

    def body(att_ref, rec_ref, x_ref, w_ref, g1_ref, b1_ref, z_ref, h_ref):
        mix = _mm(att_ref[...], w_ref[0:512, :]) + _mm(rec_ref[...], w_ref[512:1024, :])
        z1 = ALPHA * x_ref[...] + mix
        z_ref[...] = z1
        h1, _, _ = _ln(z1, g1_ref[...], b1_ref[...])
        h_ref[...] = h1.astype(MXU_DTYPE).astype(BF16)

    return pl.pallas_call(
        body, name="out_proj", grid=(T // tm,),
        in_specs=[_rows(tm, 512), _rows(tm, 512), _rows(tm, D), _full((D, D)), _full((1, D)), _full((1, D))],
        out_specs=[_rows(tm, D), _rows(tm, D)],
        out_shape=[jax.ShapeDtypeStruct((T, D), F32), jax.ShapeDtypeStruct((T, D), BF16)],
        compiler_params=_params(),
    )(att, rec, x, w_out, g1, b1)


NC = D_FF // FF_CHUNK


def _ffn_up(h1b, w_up_t, fcw, fcb):
    T = h1b.shape[0]
    tm = 512
    CW = FF_CHUNK

    def body(h_ref, wg_ref, wv_ref, fcw_ref, fcb_ref, gate_ref, val_ref, act_ref, ext):
        i = pl.program_id(1)

        @pl.when(i == 0)
        def _():
            ext[0:8, :] = jnp.zeros((8, CW), F32)

        hb = h_ref[...]
        gate = _mm_nt(hb, wg_ref[...])
        val = _mm_nt(hb, wv_ref[...])
        gate_ref[...] = gate
        val_ref[...] = val
        ext[8:8 + tm, :] = gate
        gc = (fcb_ref[...] + fcw_ref[0:1, :] * ext[6:6 + tm, :] + fcw_ref[1:2, :] * ext[7:7 + tm, :]
              + fcw_ref[2:3, :] * gate)
        ext[0:8, :] = ext[tm:tm + 8, :]
        ge, _ = _gelu(gc)
        act_ref[...] = (ge * val).astype(BF16)

    chunk = pl.BlockSpec((None, tm, CW), lambda c, i: (c, i, 0))
    return pl.pallas_call(
        body, name="ffn_up", grid=(NC, T // tm),
        in_specs=[pl.BlockSpec((tm, D), lambda c, i: (i, 0)), pl.BlockSpec((CW, D), lambda c, i: (c, 0)),
                  pl.BlockSpec((CW, D), lambda c, i: (NC + c, 0)), pl.BlockSpec((None, 3, CW), lambda c, i: (c, 0, 0)),
                  pl.BlockSpec((None, 1, CW), lambda c, i: (c, 0, 0))],
        out_specs=[chunk, chunk, chunk],
        out_shape=[jax.ShapeDtypeStruct((NC, T, CW), F32), jax.ShapeDtypeStruct((NC, T, CW), F32),
                   jax.ShapeDtypeStruct((NC, T, CW), BF16)],
        scratch_shapes=[pltpu.VMEM((tm + 8, CW), F32)],
        compiler_params=_params(),
    )(h1b, w_up_t, w_up_t, fcw, fcb)


def _ffn_down(act, z1, p, tgt, w_down, w_g, w_p_t, g1, b1, g2, b2, bg):
    T = z1.shape[0]
    tm = 256

    def body(act_ref, z_ref, p_ref, t_ref, wdn_hbm, wg_hbm, wp_hbm, g1_ref, b1_ref, g2_ref, b2_ref, bg_ref,
             dz2_ref, dz2b_ref, dpre_ref, dpp_ref, vec_ref, wdn, wg, wp):
        @pl.when(pl.program_id(0) == 0)
        def _():
            pltpu.sync_copy(wdn_hbm, wdn)
            pltpu.sync_copy(wg_hbm, wg)
            pltpu.sync_copy(wp_hbm, wp)
            vec_ref[...] = jnp.zeros_like(vec_ref)

        g2v = g2_ref[...]
        h1, _, _ = _ln(z_ref[...], g1_ref[...], b1_ref[...])
        h1b = h1.astype(MXU_DTYPE)
        ffn = _mm(act_ref[0], wdn[0:FF_CHUNK, :])
        for c in range(1, NC):
            ffn = ffn + _mm(act_ref[c], wdn[c * FF_CHUNK:(c + 1) * FF_CHUNK, :])
        sg = _sigmoid(_mm(h1b, wg[...]) + bg_ref[...])
        pp = _mm_nt(p_ref[...], wp[...])
        z2 = ALPHA * h1 + ffn + sg * pp
        y, xh2, rstd2 = _ln(z2, g2v, b2_ref[...])
        diff = y - t_ref[...]
        dy = diff * (1.0 / D)
        dz2 = _ln_bwd(dy, xh2, rstd2, g2v)
        dpre = dz2 * pp * sg * (1.0 - sg)
        dz2_ref[...] = dz2
        dz2b_ref[...] = dz2.astype(BF16)
        dpre_ref[...] = dpre.astype(BF16)
        dpp_ref[...] = (dz2 * sg).astype(BF16)
        loss = 0.5 * jnp.sum(jnp.sum(diff * diff, axis=1, keepdims=True), axis=0, keepdims=True) * (1.0 / D)
        vec_ref[0:1, :] += jnp.broadcast_to(loss, (1, D))
        vec_ref[1:2, :] += _colsum(dy * xh2)
        vec_ref[2:3, :] += _colsum(dy)
        vec_ref[3:4, :] += _colsum(dpre)

    anyspec = pl.BlockSpec(memory_space=pl.ANY)
    vec = _full((1, D))
    return pl.pallas_call(
        body, name="ffn_down", grid=(T // tm,),
        in_specs=[pl.BlockSpec((NC, tm, FF_CHUNK), lambda i: (0, i, 0)), _rows(tm, D), _rows(tm, PLE), _rows(tm, D),
                  anyspec, anyspec, anyspec] + [vec] * 5,
        out_specs=[_rows(tm, D)] * 4 + [_full((8, D))],
        out_shape=[jax.ShapeDtypeStruct((T, D), F32)] + [jax.ShapeDtypeStruct((T, D), BF16)] * 3
                  + [jax.ShapeDtypeStruct((8, D), F32)],
        scratch_shapes=[pltpu.VMEM((D_FF, D), MXU_DTYPE), pltpu.VMEM((D, D), MXU_DTYPE), pltpu.VMEM((D, PLE), MXU_DTYPE)],
        compiler_params=_params(),
    )(act, z1, p, tgt, w_down, w_g, w_p_t, g1, b1, g2, b2, bg)


def _ffn_bwd(dz2b, gate, val, w_down, fcw, fcb):
    T = dz2b.shape[0]
    tm = 512
    CW = FF_CHUNK
    nt = T // tm
    t8 = tm // 8

    def body(dz_ref, wdn_ref, gate_ref, gp_ref, val_ref, fcw_ref, fcb_ref, dup_ref, dfc_ref, gext, dext):
        i = pl.program_id(1)
        j = nt - 1 - i

        @pl.when(i == 0)
        def _():
            dext[tm:tm + 8, :] = jnp.zeros((8, CW), F32)
            dfc_ref[...] = jnp.zeros_like(dfc_ref)

        gate = gate_ref[...]
        gext[0:8, :] = jnp.where(j > 0, gp_ref[...], 0.0)
        gext[8:8 + tm, :] = gate
        gate1 = gext[7:7 + tm, :]
        gate2 = gext[6:6 + tm, :]
        gc = fcb_ref[...] + fcw_ref[0:1, :] * gate2 + fcw_ref[1:2, :] * gate1 + fcw_ref[2:3, :] * gate
        ge, dge = _gelu(gc)
        dact = _mm_nt(dz_ref[...], wdn_ref[...])
        dgc = dact * val_ref[...] * dge
        dext[0:tm, :] = dgc
        dgate = fcw_ref[2:3, :] * dgc + fcw_ref[1:2, :] * dext[1:1 + tm, :] + fcw_ref[0:1, :] * dext[2:2 + tm, :]
        dext[tm:tm + 8, :] = dgc[0:8, :]
        dup_ref[0] = dgate.astype(BF16)
        dup_ref[1] = (dact * ge).astype(BF16)
        dfc_ref[0:1, :] += _colsum(dgc * gate2)
        dfc_ref[1:2, :] += _colsum(dgc * gate1)
        dfc_ref[2:3, :] += _colsum(dgc * gate)
        dfc_ref[3:4, :] += _colsum(dgc)

    rev = lambda c, i: (c, nt - 1 - i, 0)
    return pl.pallas_call(
        body, name="ffn_bwd", grid=(NC, nt),
        in_specs=[pl.BlockSpec((tm, D), lambda c, i: (nt - 1 - i, 0)), pl.BlockSpec((CW, D), lambda c, i: (c, 0)),
                  pl.BlockSpec((None, tm, CW), rev),
                  pl.BlockSpec((None, 8, CW), lambda c, i: (c, jnp.maximum((nt - 1 - i) * t8 - 1, 0), 0)),
                  pl.BlockSpec((None, tm, CW), rev), pl.BlockSpec((None, 3, CW), lambda c, i: (c, 0, 0)),
                  pl.BlockSpec((None, 1, CW), lambda c, i: (c, 0, 0))],
        out_specs=[pl.BlockSpec((None, 2, tm, CW), lambda c, i: (c, 0, nt - 1 - i, 0)),
                   pl.BlockSpec((None, 8, CW), lambda c, i: (c, 0, 0))],
        out_shape=[jax.ShapeDtypeStruct((NC, 2, T, CW), BF16), jax.ShapeDtypeStruct((NC, 8, CW), F32)],
        scratch_shapes=[pltpu.VMEM((tm + 8, CW), F32), pltpu.VMEM((tm + 8, CW), F32)],
        compiler_params=_params(),
    )(dz2b, w_down, gate, gate, val, fcw, fcb)


def _ffn_dh1(dup, dz2, dpre, z1, w_up_t, w_g, g1, b1):
    T = z1.shape[0]
    tm = 256

    def body(dup_ref, dz2_ref, dpre_ref, z_ref, wup_hbm, wg_hbm, g1_ref, b1_ref, dz1_ref, vec_ref, wup, wg):
        @pl.when(pl.program_id(0) == 0)
        def _():
            pltpu.sync_copy(wup_hbm, wup)
            pltpu.sync_copy(wg_hbm, wg)
            vec_ref[...] = jnp.zeros_like(vec_ref)

        g1v = g1_ref[...]
        _, xh1, rstd1 = _ln(z_ref[...], g1v, b1_ref[...])
        dh1 = ALPHA * dz2_ref[...] + _mm_nt(dpre_ref[...], wg[...])
        for c in range(NC):
            for s in range(2):
                r0 = s * D_FF + c * FF_CHUNK
                dh1 = dh1 + _mm(dup_ref[c, s], wup[r0:r0 + FF_CHUNK, :])
        dz1_ref[...] = _ln_bwd(dh1, xh1, rstd1, g1v)
        vec_ref[0:1, :] += _colsum(dh1 * xh1)
        vec_ref[1:2, :] += _colsum(dh1)

    anyspec = pl.BlockSpec(memory_space=pl.ANY)
    vec = _full((1, D))
    return pl.pallas_call(
        body, name="ffn_dh1", grid=(T // tm,),
        in_specs=[pl.BlockSpec((NC, 2, tm, FF_CHUNK), lambda i: (0, 0, i, 0)), _rows(tm, D), _rows(tm, D), _rows(tm, D),
                  anyspec, anyspec, vec, vec],
        out_specs=[_rows(tm, D), _full((8, D))],
        out_shape=[jax.ShapeDtypeStruct((T, D), F32), jax.ShapeDtypeStruct((8, D), F32)],
        scratch_shapes=[pltpu.VMEM((2 * D_FF, D), MXU_DTYPE), pltpu.VMEM((D, D), MXU_DTYPE)],
        compiler_params=_params(),
    )(dup, dz2, dpre, z1, w_up_t, w_g, g1, b1)


def _out_proj_bwd(dz1, w_out, exchange=None):
    T = dz1.shape[0]
    tm = 512

    def body(dz_ref, w_ref, datt_ref, drec_ref, dzb_ref):
        dzb = dz_ref[...].astype(MXU_DTYPE)
        dzb_ref[...] = dzb.astype(BF16)
        datt = _mm_nt(dzb, w_ref[0:512, :])
        for h in range(HEADS):
            datt_ref[h] = datt[:, h * 64:(h + 1) * 64].astype(BF16)
        drec_ref[...] = _mm_nt(dzb, w_ref[512:1024, :])

    return _launch(body, "out_proj_bwd", (T // tm,), [_rows(tm, D), _full((D, D))],
                   [_heads(tm), _rows(tm, 512), _rows(tm, D)],
                   [jax.ShapeDtypeStruct((HEADS, T, 64), BF16), jax.ShapeDtypeStruct((T, 512), F32),
                    jax.ShapeDtypeStruct((T, D), BF16)], [], (dz1, w_out), exchange)


def _in_proj_bwd(dq, dkv, dxr, dgr, dz1, w_in_t):
    T = dz1.shape[0]
    tm = 512

    def body(dq_ref, dkv_ref, dxr_ref, dgr_ref, dz_ref, w_ref, dx_ref, du_ref):
        dkv = dkv_ref[...].astype(BF16)
        dx_ref[...] = (ALPHA * dz_ref[...] + _mm(dq_ref[...], w_ref[0:512, :]) + _mm(dkv, w_ref[512:768, :])
                       + _mm(dxr_ref[...], w_ref[768:1280, :]) + _mm(dgr_ref[...], w_ref[1280:1792, :]))
        du_ref[:, 0:512] = dq_ref[...]
        du_ref[:, 512:768] = dkv
        du_ref[:, 768:1280] = dxr_ref[...]
        du_ref[:, 1280:1792] = dgr_ref[...]

    return pl.pallas_call(
        body, name="in_proj_bwd", grid=(T // tm,),
        in_specs=[_rows(tm, 512), _rows(tm, 256), _rows(tm, 512), _rows(tm, 512), _rows(tm, D), _full((D_IN, D))],
        out_specs=[_rows(tm, D), _rows(tm, D_IN)],
        out_shape=[jax.ShapeDtypeStruct((T, D), F32), jax.ShapeDtypeStruct((T, D_IN), BF16)],
        compiler_params=_params(),
    )(dq, dkv, dxr, dgr, dz1, w_in_t)


def _accumulate_tn(a_ref, b_ref, o_ref):
    @pl.when(pl.program_id(1) == 0)
    def _():
        o_ref[...] = jnp.zeros_like(o_ref)

    o_ref[...] += _mm_tn(a_ref[...], b_ref[...])


def _weight_grad_cols(a, b, name, n_blocks, b_spec, out_shape, out_spec):
    T, M = a.shape
    bt = min(2048, T)
    return pl.pallas_call(
        functools.partial(_accumulate_tn), name=name, grid=(n_blocks, T // bt),
        in_specs=[pl.BlockSpec((bt, M), lambda m, k: (k, 0)), b_spec(bt)], out_specs=out_spec,
        out_shape=jax.ShapeDtypeStruct(out_shape, F32), compiler_params=_params())(a, b)


def _weight_grad(a, b, bm, name):
    bt = min(2048, b.shape[0])
    if a.ndim == 3:
        assert a.shape[2] == bm
        T, M = a.shape[1], a.shape[0] * bm
        a_spec = pl.BlockSpec((None, bt, bm), lambda m, k: (m, k, 0))
    else:
        T, M = a.shape
        a_spec = pl.BlockSpec((bt, bm), lambda m, k: (k, m))
    N = b.shape[1]
    nk = T // bt

    return pl.pallas_call(
        functools.partial(_accumulate_tn), name=name, grid=(M // bm, nk),
        in_specs=[a_spec, pl.BlockSpec((bt, N), lambda m, k: (k, 0))],
        out_specs=pl.BlockSpec((bm, N), lambda m, k: (m, 0)),
        out_shape=jax.ShapeDtypeStruct((M, N), F32),
        compiler_params=_params(),
    )(a, b)


def _adamw(w, g, m, v, name):
    R, C = w.shape
    tr = R // 8 if R % 64 == 0 else R
    c1 = 1.0 / (1.0 - ADAM_B1 ** ADAM_STEP)
    c2 = 1.0 / (1.0 - ADAM_B2 ** ADAM_STEP)

    def body(w_ref, g_ref, m_ref, v_ref, d_ref, nm_ref, nv_ref):
        g = g_ref[...]
        nm = ADAM_B1 * m_ref[...] + (1.0 - ADAM_B1) * g
        nv = ADAM_B2 * v_ref[...] + (1.0 - ADAM_B2) * g * g
        nm_ref[...] = nm
        nv_ref[...] = nv
        d_ref[...] = -ADAM_LR * ((nm * c1) / (jnp.sqrt(nv * c2) + ADAM_EPS) + ADAM_WD * w_ref[...])

    spec = pl.BlockSpec((tr, C), lambda i: (i, 0))
    return pl.pallas_call(
        body, name=name, grid=(R // tr,),
        in_specs=[spec] * 4, out_specs=[spec] * 3,
        out_shape=[jax.ShapeDtypeStruct((R, C), F32)] * 3,
        compiler_params=_params(),
    )(w, g, m, v)


def _adamw_halves(ws, mines, sibs, ms, vs, c, name):
    n, nb = len(ws), 4
    c1 = 1.0 / (1.0 - ADAM_B1 ** ADAM_STEP)
    c2 = 1.0 / (1.0 - ADAM_B2 ** ADAM_STEP)

    def body(c_ref, *refs):
        own = (pl.program_id(0) // nb) == c_ref[0]
        for i in range(n):
            w_ref, a_ref, b_ref, m_ref, v_ref = refs[5 * i:5 * i + 5]
            g_ref, d_ref, nm_ref, nv_ref = refs[5 * n + 4 * i:5 * n + 4 * i + 4]
            g = jnp.where(own, a_ref[...], b_ref[...])
            nm = ADAM_B1 * m_ref[...] + (1.0 - ADAM_B1) * g
            nv = ADAM_B2 * v_ref[...] + (1.0 - ADAM_B2) * g * g
            g_ref[...] = g
            nm_ref[...] = nm
            nv_ref[...] = nv
            d_ref[...] = -ADAM_LR * ((nm * c1) / (jnp.sqrt(nv * c2) + ADAM_EPS) + ADAM_WD * w_ref[...])

    in_specs, out_specs, out_shape, args = [], [], [], []
    for w, a, b, m, v in zip(ws, mines, sibs, ms, vs):
        R, C = w.shape
        tr = R // (2 * nb)
        assert tr % 8 == 0 and a.shape == (R // 2, C)
        full = pl.BlockSpec((tr, C), lambda i, c_ref: (i, 0))
        half = pl.BlockSpec((tr, C), lambda i, c_ref: (i % nb, 0))
        in_specs += [full, half, half, full, full]
        out_specs += [full] * 4
        out_shape += [jax.ShapeDtypeStruct((R, C), F32)] * 4
        args += [w, a, b, m, v]
    grid_spec = pltpu.PrefetchScalarGridSpec(num_scalar_prefetch=1, grid=(2 * nb,), in_specs=in_specs, out_specs=out_specs)
    out = pl.pallas_call(body, name=name, grid_spec=grid_spec, out_shape=out_shape, compiler_params=_params())(c, *args)
    return [tuple(out[4 * i:4 * i + 4]) for i in range(n)]


def _add4(fs, name):
    n = len(fs)

    def body(*refs):
        for a_ref, o_ref in zip(refs[:n], refs[n:]):
            o_ref[...] = ((a_ref[0].astype(F32) + a_ref[1].astype(F32)) + a_ref[2].astype(F32)) + a_ref[3].astype(F32)

    for f in fs:
        assert (f.shape[1] // 2) % 16 == 0
    return pl.pallas_call(
        body, name=name, grid=(2,),
        in_specs=[pl.BlockSpec((4, f.shape[1] // 2, f.shape[2]), lambda i: (0, i, 0)) for f in fs],
        out_specs=[pl.BlockSpec((f.shape[1] // 2, f.shape[2]), lambda i: (i, 0)) for f in fs],
        out_shape=[jax.ShapeDtypeStruct(f.shape[1:], F32) for f in fs], compiler_params=_params())(*fs)


def _gather_first(wsrc, cpack):
    def body(w_ref, c_ref, gw_ref, gc_ref, send_sems, recv_sems, local_sem, csend, crecv, clocal):
        x, y, c = _pos()
        me = 2 * x + y
        chips = _other_chips(x, y)
        start, forward, finish = _gather_steps(w_ref, gw_ref, send_sems, recv_sems, local_sem)
        start()
        loc = pltpu.make_async_copy(c_ref, gc_ref.at[me], clocal)
        loc.start()

        def conv_copy(k, slot):
            px, py = chips[k]
            return pltpu.make_async_remote_copy(src_ref=c_ref, dst_ref=gc_ref.at[slot], send_sem=csend.at[k],
                                                recv_sem=crecv.at[k], device_id=(px, py, c), device_id_type=MESH)

        for k in range(3):
            conv_copy(k, me).start()
        forward()
        finish()
        for k, (px, py) in enumerate(chips):
            conv_copy(k, 2 * px + py).wait_recv()
        for k in range(3):
            conv_copy(k, me).wait_send()
        loc.wait()

    anyspec = pl.BlockSpec(memory_space=pl.ANY)
    return pl.pallas_call(
        body, name="gather_first",
        in_specs=[anyspec, anyspec], out_specs=[anyspec, anyspec],
        out_shape=[jax.ShapeDtypeStruct((4,) + wsrc.shape, wsrc.dtype), jax.ShapeDtypeStruct((4,) + cpack.shape, cpack.dtype)],
        scratch_shapes=GATHER_SCRATCH + [pltpu.SemaphoreType.DMA((3,)), pltpu.SemaphoreType.DMA((3,)), pltpu.SemaphoreType.DMA],
        compiler_params=_params(has_side_effects=True),
    )(wsrc, cpack)


def _allreduce_small(s):
    R = s.shape[0]

    def body(s_ref, o_ref, buf, send_sems, recv_sems):
        x, y, c = _pos()
        me = 4 * x + 2 * y + c
        buf[me] = s_ref[...]
        sends = []
        for k in range(1, 8):
            peer = (x ^ (k >> 2), y ^ ((k >> 1) & 1), c ^ (k & 1))
            cp = pltpu.make_async_remote_copy(src_ref=s_ref, dst_ref=buf.at[me], send_sem=send_sems.at[k - 1],
                                              recv_sem=recv_sems.at[k - 1], device_id=peer, device_id_type=MESH)
            cp.start()
            sends.append(cp)
        for k in range(1, 8):
            px, py, pc = x ^ (k >> 2), y ^ ((k >> 1) & 1), c ^ (k & 1)
            pltpu.make_async_remote_copy(src_ref=s_ref, dst_ref=buf.at[4 * px + 2 * py + pc], send_sem=send_sems.at[k - 1],
                                         recv_sem=recv_sems.at[k - 1], device_id=(px, py, pc),
                                         device_id_type=MESH).wait_recv()
        for cp in sends:
            cp.wait_send()
        acc = buf[0]
        for d in range(1, 8):
            acc = acc + buf[d]
        o_ref[...] = acc

    vm = pl.BlockSpec(memory_space=pltpu.VMEM)
    return pl.pallas_call(
        body, name="allreduce_small", in_specs=[vm], out_specs=vm,
        out_shape=jax.ShapeDtypeStruct((R, 128), F32),
        scratch_shapes=[pltpu.VMEM((8, R, 128), F32), pltpu.SemaphoreType.DMA((7,)), pltpu.SemaphoreType.DMA((7,))],
        compiler_params=_params(has_side_effects=True),
    )(s)


def _swap_exchange(gs):
    n = len(gs)

    def make(ins, outs, sems):
        x, y, c = _pos()
        cps = []
        for i in range(n):
            half = gs[i].shape[1] // 2
            rows = pl.ds(pl.multiple_of((1 - c) * half, 8), half)
            cps.append(pltpu.make_async_remote_copy(src_ref=ins[i].at[:, rows, :], dst_ref=outs[i], send_sem=sems[0].at[i],
                                                    recv_sem=sems[1].at[i], device_id=(x, y, 1 - c), device_id_type=MESH))

        def start():
            for cp in cps:
                cp.start()

        def finish():
            for cp in cps:
                cp.wait()

        return start, lambda: None, finish

    return _Exchange(gs, [jax.ShapeDtypeStruct((4, g.shape[1] // 2, g.shape[2]), g.dtype) for g in gs],
                     [pltpu.SemaphoreType.DMA((n,)), pltpu.SemaphoreType.DMA((n,))], make)


def _scatter_exchange(ss):
    n = len(ss)

    def make(ins, outs, sems):
        send_sems, recv_sems, local_sems = sems
        x, y, c = _pos()
        me = 2 * x + y
        chips = _other_chips(x, y)
        locs = [pltpu.make_async_copy(ins[i].at[me], outs[i].at[me], local_sems.at[i]) for i in range(n)]

        def copy(i, k, src_slot, dst_slot):
            px, py = chips[k]
            return pltpu.make_async_remote_copy(src_ref=ins[i].at[src_slot], dst_ref=outs[i].at[dst_slot],
                                                send_sem=send_sems.at[3 * i + k], recv_sem=recv_sems.at[3 * i + k],
                                                device_id=(px, py, c), device_id_type=MESH)

        def start():
            for i in range(n):
                locs[i].start()
                for k, (px, py) in enumerate(chips):
                    copy(i, k, 2 * px + py, me).start()

        def finish():
            for i in range(n):
                for k, (px, py) in enumerate(chips):
                    copy(i, k, me, 2 * px + py).wait_recv()
            for i in range(n):
                for k, (px, py) in enumerate(chips):
                    copy(i, k, 2 * px + py, me).wait_send()
                locs[i].wait()

        return start, lambda: None, finish

    return _Exchange(ss, [jax.ShapeDtypeStruct(s.shape, s.dtype) for s in ss],
                     [pltpu.SemaphoreType.DMA((3 * n,)), pltpu.SemaphoreType.DMA((3 * n,)), pltpu.SemaphoreType.DMA((n,))], make)


def _send_exchange(rs):
    n = len(rs)

    def make(ins, outs, sems):
        x, y, c = _pos()
        cps = [pltpu.make_async_remote_copy(src_ref=ins[i], dst_ref=outs[i], send_sem=sems[0].at[i], recv_sem=sems[1].at[i],
                                            device_id=(x, y, 1 - c), device_id_type=MESH) for i in range(n)]

        def start():
            for cp in cps:
                cp.start()

        def finish():
            for cp in cps:
                cp.wait()

        return start, lambda: None, finish

    return _Exchange(rs, [jax.ShapeDtypeStruct(r.shape, r.dtype) for r in rs],
                     [pltpu.SemaphoreType.DMA((n,)), pltpu.SemaphoreType.DMA((n,))], make)


def _run_exchange(ex, name):
    ei, eo = len(ex.args), len(ex.out_shape)

    def body(*refs):
        start, forward, finish = ex.make(refs[:ei], refs[ei:ei + eo], refs[ei + eo:])
        start()
        forward()
        finish()

    anyspec = pl.BlockSpec(memory_space=pl.ANY)
    return pl.pallas_call(body, name=name, in_specs=[anyspec] * ei, out_specs=[anyspec] * eo, out_shape=ex.out_shape,
                          scratch_shapes=ex.scratch, compiler_params=_params(has_side_effects=True))(*ex.args)


def _add_half(gs, rs, c, name):
    n = len(gs)

    def body(c_ref, *refs):
        for g_ref, r_ref, o_ref in zip(refs[:n], refs[n:2 * n], refs[2 * n:]):
            o_ref[...] = (g_ref[...] + r_ref[...]).astype(BF16)

    g_specs, r_specs, out_shape = [], [], []
    for g, r in zip(gs, rs):
        _, H, C = r.shape
        tr = H // 2
        assert tr % 16 == 0 and g.shape == (4, 2 * H, C)
        g_specs.append(pl.BlockSpec((1, tr, C), lambda j, i, c_ref: (j, c_ref[0] * 2 + i, 0)))
        r_specs.append(pl.BlockSpec((1, tr, C), lambda j, i, c_ref: (j, i, 0)))
        out_shape.append(jax.ShapeDtypeStruct((4, H, C), BF16))
    grid_spec = pltpu.PrefetchScalarGridSpec(num_scalar_prefetch=1, grid=(4, 2), in_specs=g_specs + r_specs, out_specs=r_specs)
    return pl.pallas_call(body, name=name, grid_spec=grid_spec, out_shape=out_shape, compiler_params=_params())(c, *gs, *rs)


def _block_diag(w):
    eye = jnp.eye(RNN_BLOCKS, dtype=w.dtype)
    return (eye[:, None, :, None] * w[:, :, None, :]).reshape(D_RNN, D_RNN)


def _diag_blocks(wd):
    d = wd.reshape(RNN_BLOCKS, 64, RNN_BLOCKS, 64)
    return jnp.stack([d[h, :, h, :] for h in range(RNN_BLOCKS)])


def _split_pack(a, first, last):
    out, base = {}, PACK_OFF[first]
    for i in range(first, last):
        s = a[:, PACK_OFF[i] - base:PACK_OFF[i + 1] - base]
        out[BIG_KEYS[i]] = s.reshape(4 * 256, 256) if BIG_KEYS[i] == "w_p_t" else s.reshape(-1, 1024)
    return out


def _layer_grads(x, p, tgt, gw, small, shard=None, core=None):
    row = lambda v: v.reshape(1, -1)
    wa = _block_diag(small["gate_a_w"]).astype(MXU_DTYPE)
    wx = _block_diag(small["gate_x_w"]).astype(MXU_DTYPE)
    sinks = small["attn_sinks"].reshape(1, HEADS)

    dist = shard is not None
    q, kv, xr, gr, xb = _in_proj(x, gw["w_in_t"])
    att, *ga = _attn_fwd(q, kv, sinks, _gather_exchange(shard[PACK_OFF[1]:PACK_OFF[3]]) if dist else None)
    xc, h, rec, *gb = _rnn_fwd(xr, gr, small["rnn_conv_w"], row(small["rnn_conv_b"]), wa, row(small["gate_a_b"]),
                               wx, row(small["gate_x_b"]), row(small["lru_lambda"]),
                               _gather_exchange(shard[PACK_OFF[3]:PACK_OFF[6]]) if dist else None)
    if dist:
        gw = {**gw, **_split_pack(ga[0], 1, 3), **_split_pack(gb[0], 3, 6)}
    g1, b1 = row(small["ln1_g"]), row(small["ln1_b"])
    fcw = small["ffn_conv_w"].reshape(3, NC, FF_CHUNK).transpose(1, 0, 2)
    fcb = small["ffn_conv_b"].reshape(NC, 1, FF_CHUNK)
    z1, h1b = _out_proj(att, rec, x, gw["w_out"], g1, b1)
    gate, val, act = _ffn_up(h1b, gw["w_up_t"], fcw, fcb)
    dz2, dz2b, dpre, dpp, vec2 = _ffn_down(act, z1, p, tgt, gw["w_down"], gw["w_g"], gw["w_p_t"], g1, b1,
                                           row(small["ln2_g"]), row(small["ln2_b"]), row(small["ple_gate_b"]))
    dup, dfc = _ffn_bwd(dz2b, gate, val, gw["w_down"], fcw, fcb)
    dz1, vec1 = _ffn_dh1(dup, dz2, dpre, z1, gw["w_up_t"], gw["w_g"], g1, b1)
    per_chip = 2 * D_FF // 4 // FF_CHUNK
    big = {
        "w_ffn_up": _weight_grad_cols(
            h1b, dup.reshape(2 * NC, -1, FF_CHUNK), "dw_up", 2 * NC,
            lambda bt: pl.BlockSpec((None, bt, FF_CHUNK), lambda m, k: (m, k, 0)), (4, D, 2 * D_FF // 4),
            pl.BlockSpec((None, D, FF_CHUNK), lambda m, k: (2 * (m % 2) + (m // 2) // per_chip, 0, (m // 2) % per_chip))),
        "w_ffn_down": _weight_grad(act, dz2b, 512, "dw_down").reshape(4, D_FF // 4, D),
        "ple_gate_w": _weight_grad(h1b, dpre, 512, "dw_gate").reshape(4, D // 4, D),
        "ple_proj": _weight_grad_cols(
            p.astype(BF16), dpp, "dw_proj", 4, lambda bt: pl.BlockSpec((bt, D // 4), lambda j, k: (k, j)),
            (4, PLE, D // 4), pl.BlockSpec((None, PLE, D // 4), lambda j, k: (j, 0, 0))),
    }
    reduced = None
    if dist:
        g_ffn = [big[k] for k in FFN_WEIGHTS]
        ex = _swap_exchange(g_ffn)
    datt, drec, dz1b, *got = _out_proj_bwd(dz1, gw["w_out"], ex if dist else None)
    if dist:
        ex = _scatter_exchange(_add_half(g_ffn, got, core, "add_half_ffn"))
    dxr, dgr, dwa, dwx, dvec, *got = _rnn_bwd(drec, gr, h, xc, xr, small["rnn_conv_w"], wa, row(small["gate_a_b"]),
                                              wx, row(small["gate_x_b"]), row(small["lru_lambda"]), ex if dist else None)
    if dist:
        mine = _add4(got, "add_chips_ffn")
        ex = _send_exchange(mine)
    dq, dkv, dsinks, *got = _attn_bwd(q, kv, datt, sinks, ex if dist else None)
    if dist:
        reduced = (mine, got)
        big = {}
    grad_x, du = _in_proj_bwd(dq, dkv, dxr, dgr, dz1, gw["w_in_t"])
    big["w_in"] = _weight_grad(xb, du, 512, "dw_in").reshape(D, 4, D_IN // 4).transpose(1, 0, 2)
    big["w_out"] = _weight_grad(jnp.concatenate([att, rec], axis=1), dz1b, 512, "dw_out").reshape(4, D // 4, D)
    sg = {
        "attn_sinks": dsinks[:, 0],
        "rnn_conv_w": dvec[4:8],
        "rnn_conv_b": dvec[3],
        "gate_a_w": _diag_blocks(dwa),
        "gate_a_b": dvec[0],
        "gate_x_w": _diag_blocks(dwx),
        "gate_x_b": dvec[1],
        "lru_lambda": dvec[2],
        "ln1_g": vec1[0],
        "ln1_b": vec1[1],
        "ffn_conv_w": dfc[:, 0:3].transpose(1, 0, 2).reshape(3, D_FF),
        "ffn_conv_b": dfc[:, 3].reshape(D_FF),
        "ple_gate_b": vec2[3],
        "ln2_g": vec2[1],
        "ln2_b": vec2[2],
    }
    return grad_x, big, sg, vec2[0, 0:1], reduced


BIG = ("w_in", "w_out", "w_ffn_up", "w_ffn_down", "ple_gate_w", "ple_proj")
BIG_KEYS = ("w_in_t", "w_out", "w_up_t", "w_down", "w_g", "w_p_t")
BIG_T = (True, False, True, False, False, True)
FFN_WEIGHTS = ("w_ffn_up", "w_ffn_down", "ple_gate_w", "ple_proj")
MIX_WEIGHTS = ("w_in", "w_out")
SMALL = ("attn_sinks", "rnn_conv_w", "rnn_conv_b", "gate_a_w", "gate_a_b", "gate_x_w", "gate_x_b", "lru_lambda",
         "ln1_g", "ln1_b", "ffn_conv_w", "ffn_conv_b", "ple_gate_b", "ln2_g", "ln2_b")
SHARDED_SMALL = ("rnn_conv_w", "ffn_conv_w")
WEIGHTS = ("w_in", "attn_sinks", "rnn_conv_w", "rnn_conv_b", "gate_a_w", "gate_a_b", "gate_x_w", "gate_x_b",
           "lru_lambda", "w_out", "ln1_g", "ln1_b", "w_ffn_up", "ffn_conv_w", "ffn_conv_b", "w_ffn_down",
           "ple_gate_w", "ple_gate_b", "ple_proj", "ln2_g", "ln2_b")


def _pack_big(d, first=0, last=6):
    parts = []
    for name, t in zip(BIG[first:last], BIG_T[first:last]):
        a = d[name]
        a = a.T if t else a
        parts.append(a.reshape(-1, 1024))
    return jnp.concatenate(parts, axis=0)


def _pack_vecs(items):
    parts, offs, n = [], [], 0
    for a in items:
        f = a.reshape(-1).astype(F32)
        pad = (-f.shape[0]) % 128
        parts.append(jnp.pad(f, (0, pad)))
        offs.append(n)
        n += (f.shape[0] + pad) // 128
    padr = (-n) % 8
    if padr:
        parts.append(jnp.zeros((padr * 128,), F32))
    return jnp.concatenate(parts).reshape(-1, 128), offs


def _unpack_vecs(a, offs, shapes):
    flat = a.reshape(-1)
    out = []
    for o, s in zip(offs, shapes):
        n = 1
        for d in s:
            n *= d
        out.append(flat[o * 128:o * 128 + n].reshape(s))
    return out


def kernel(x, p, w_in, attn_sinks, rnn_conv_w, rnn_conv_b, gate_a_w, gate_a_b, gate_x_w, gate_x_b, lru_lambda, w_out, ln1_g, ln1_b, w_ffn_up, ffn_conv_w, ffn_conv_b, w_ffn_down, ple_gate_w, ple_gate_b, ple_proj, ln2_g, ln2_b, loss_target, m_w_in, m_attn_sinks, m_rnn_conv_w, m_rnn_conv_b, m_gate_a_w, m_gate_a_b, m_gate_x_w, m_gate_x_b, m_lru_lambda, m_w_out, m_ln1_g, m_ln1_b, m_w_ffn_up, m_ffn_conv_w, m_ffn_conv_b, m_w_ffn_down, m_ple_gate_w, m_ple_gate_b, m_ple_proj, m_ln2_g, m_ln2_b, v_w_in, v_attn_sinks, v_rnn_conv_w, v_rnn_conv_b, v_gate_a_w, v_gate_a_b, v_gate_x_w, v_gate_x_b, v_lru_lambda, v_w_out, v_ln1_g, v_ln1_b, v_w_ffn_up, v_ffn_conv_w, v_ffn_conv_b, v_w_ffn_down, v_ple_gate_w, v_ple_gate_b, v_ple_proj, v_ln2_g, v_ln2_b):
    w = dict(w_in=w_in, attn_sinks=attn_sinks, rnn_conv_w=rnn_conv_w, rnn_conv_b=rnn_conv_b, gate_a_w=gate_a_w,
             gate_a_b=gate_a_b, gate_x_w=gate_x_w, gate_x_b=gate_x_b, lru_lambda=lru_lambda, w_out=w_out, ln1_g=ln1_g,
             ln1_b=ln1_b, w_ffn_up=w_ffn_up, ffn_conv_w=ffn_conv_w, ffn_conv_b=ffn_conv_b, w_ffn_down=w_ffn_down,
             ple_gate_w=ple_gate_w, ple_gate_b=ple_gate_b, ple_proj=ple_proj, ln2_g=ln2_g, ln2_b=ln2_b)
    m = dict(w_in=m_w_in, attn_sinks=m_attn_sinks, rnn_conv_w=m_rnn_conv_w, rnn_conv_b=m_rnn_conv_b, gate_a_w=m_gate_a_w,
             gate_a_b=m_gate_a_b, gate_x_w=m_gate_x_w, gate_x_b=m_gate_x_b, lru_lambda=m_lru_lambda, w_out=m_w_out,
             ln1_g=m_ln1_g, ln1_b=m_ln1_b, w_ffn_up=m_w_ffn_up, ffn_conv_w=m_ffn_conv_w, ffn_conv_b=m_ffn_conv_b,
             w_ffn_down=m_w_ffn_down, ple_gate_w=m_ple_gate_w, ple_gate_b=m_ple_gate_b, ple_proj=m_ple_proj,
             ln2_g=m_ln2_g, ln2_b=m_ln2_b)
    v = dict(w_in=v_w_in, attn_sinks=v_attn_sinks, rnn_conv_w=v_rnn_conv_w, rnn_conv_b=v_rnn_conv_b, gate_a_w=v_gate_a_w,
             gate_a_b=v_gate_a_b, gate_x_w=v_gate_x_w, gate_x_b=v_gate_x_b, lru_lambda=v_lru_lambda, w_out=v_w_out,
             ln1_g=v_ln1_g, ln1_b=v_ln1_b, w_ffn_up=v_w_ffn_up, ffn_conv_w=v_ffn_conv_w, ffn_conv_b=v_ffn_conv_b,
             w_ffn_down=v_w_ffn_down, ple_gate_w=v_ple_gate_w, ple_gate_b=v_ple_gate_b, ple_proj=v_ple_proj,
             ln2_g=v_ln2_g, ln2_b=v_ln2_b)
    w, m, v = ({k: a[0] for k, a in d.items()} for d in (w, m, v))
    chip = 2 * lax.axis_index("x") + lax.axis_index("y")
    core = lax.axis_index("c")

    wpack = _pack_big(w)
    cpack, _ = _pack_vecs([w["rnn_conv_w"], w["ffn_conv_w"]])
    shard = wpack.astype(MXU_DTYPE)
    g_in, gcp = _gather_first(shard[PACK_OFF[0]:PACK_OFF[1]], cpack)
    gw = _split_pack(g_in, 0, 1)
    small = {k: w[k] for k in SMALL}
    small["rnn_conv_w"] = gcp[:, 0:4].reshape(4, 4, 128).transpose(1, 0, 2).reshape(4, 512)
    small["ffn_conv_w"] = gcp[:, 4:22].reshape(4, 3, 768).transpose(1, 0, 2).reshape(3, 3072)

    core1 = core.reshape(1).astype(jnp.int32)
    grad_x, big, sg, loss, ffn_halves = _layer_grads(x[0], p[0, 0], loss_target[0], gw, small, shard, core1)

    spack, offs = _pack_vecs([sg[k] for k in SMALL] + [loss])
    ssum = _allreduce_small(spack)
    shapes = [sg[k].shape for k in SMALL] + [(1,)]
    red = dict(zip(SMALL + ("loss",), _unpack_vecs(ssum, offs, shapes)))
    red["rnn_conv_w"] = lax.dynamic_slice_in_dim(red["rnn_conv_w"], chip * 128, 128, axis=1)
    red["ffn_conv_w"] = lax.dynamic_slice_in_dim(red["ffn_conv_w"], chip * 768, 768, axis=1)

    g_mix = [big[k] for k in MIX_WEIGHTS]
    sib = _run_exchange(_swap_exchange(g_mix), "swap_mix")
    from_chips = _run_exchange(_scatter_exchange(_add_half(g_mix, sib, core1, "add_half_mix")), "scatter_mix")
    mix_mine = _add4(from_chips, "add_chips_mix")
    mix_other = _run_exchange(_send_exchange(mix_mine), "send_mix")

    def adamw(names, mine, other, name):
        return dict(zip(names, _adamw_halves([w[k] for k in names], mine, other, [m[k] for k in names],
                                             [v[k] for k in names], core1, name)))

    big_out = {**adamw(MIX_WEIGHTS, mix_mine, mix_other, "adamw_mix"), **adamw(FFN_WEIGHTS, *ffn_halves, "adamw_ffn")}
    wsm, offs2 = _pack_vecs([w[k] for k in SMALL])
    gsm, _ = _pack_vecs([red[k] for k in SMALL])
    msm, _ = _pack_vecs([m[k] for k in SMALL])
    vsm, _ = _pack_vecs([v[k] for k in SMALL])
    dsm, nmsm, nvsm = _adamw(wsm, gsm, msm, vsm, "adamw_small")
    shapes2 = [w[k].shape for k in SMALL]

    def named(n, smallp):
        d = {k: out[n][None] for k, out in big_out.items()}
        d.update({k: a[None] for k, a in zip(SMALL, _unpack_vecs(smallp, offs2, shapes2))})
        return [d[k] for k in WEIGHTS]

    return (red["loss"].reshape(()), grad_x[None], *named(0, gsm), *named(1, dsm), *named(2, nmsm), *named(3, nvsm))
```

```python
import functools

import jax
import jax.numpy as jnp
from jax import lax
from jax.experimental import pallas as pl
from jax.experimental.pallas import tpu as pltpu

F32 = jnp.float32
BF16 = jnp.bfloat16
MXU_DTYPE = jnp.bfloat16

D = 1024
D_ATT = 512
D_KV = 128
D_RNN = 512
D_IN = 1792
D_FF = 3072
FF_CHUNK = 512
PLE = 256
HEADS = 8
HEAD_DIM = 64
BLK = 128
RNN_BLOCKS = 8
LN_EPS = 1e-5
LRU_C = 8.0
ALPHA = float(2.0 ** 0.25)
SCALE = HEAD_DIM ** -0.5
NEG = -1e30

ADAM_LR = 0.001
ADAM_B1 = 0.9
ADAM_B2 = 0.999
ADAM_EPS = 1e-08
ADAM_WD = 0.01
ADAM_STEP = 10

VMEM_LIMIT_BYTES = 56 * 1024 * 1024
MESH = pl.DeviceIdType.MESH

PACK_ROWS = (448, 1536, 256, 768, 256, 64)
PACK_OFF = tuple(sum(PACK_ROWS[:i]) for i in range(len(PACK_ROWS) + 1))
PACK_TOTAL = PACK_OFF[-1]


def _params(**kw):
    return pltpu.CompilerParams(vmem_limit_bytes=VMEM_LIMIT_BYTES, **kw)


def _mm(a, b):
    return jnp.dot(a.astype(MXU_DTYPE), b.astype(MXU_DTYPE), preferred_element_type=F32)


def _mm_nt(a, b):
    return lax.dot_general(a.astype(MXU_DTYPE), b.astype(MXU_DTYPE), (((1,), (1,)), ((), ())),
                           preferred_element_type=F32)


def _mm_tn(a, b):
    return lax.dot_general(a.astype(MXU_DTYPE), b.astype(MXU_DTYPE), (((0,), (0,)), ((), ())),
                           preferred_element_type=F32)


def _sigmoid(x):
    return 1.0 / (1.0 + jnp.exp(-x))


def _gelu(x):
    c = 0.7978845608028654
    k = 0.044715
    t = jnp.tanh(c * (x + k * x * x * x))
    g = 0.5 * x * (1.0 + t)
    dg = 0.5 * (1.0 + t) + 0.5 * x * (1.0 - t * t) * c * (1.0 + 3.0 * k * x * x)
    return g, dg


def _expm1(x):
    poly = x * (1.0 + x * (0.5 + x * (1.0 / 6.0 + x * (1.0 / 24.0 + x * (1.0 / 120.0)))))
    return jnp.where(jnp.abs(x) < 0.03, poly, jnp.exp(x) - 1.0)


def _softplus(x):
    return jnp.maximum(x, 0.0) + jnp.log(1.0 + jnp.exp(-jnp.abs(x)))


def _ln(z, g, b):
    mu = jnp.mean(z, axis=-1, keepdims=True)
    zc = z - mu
    var = jnp.mean(zc * zc, axis=-1, keepdims=True)
    rstd = lax.rsqrt(var + LN_EPS)
    xhat = zc * rstd
    return xhat * g + b, xhat, rstd


def _ln_bwd(dy, xhat, rstd, g):
    dxh = dy * g
    m1 = jnp.mean(dxh, axis=-1, keepdims=True)
    m2 = jnp.mean(dxh * xhat, axis=-1, keepdims=True)
    return rstd * (dxh - m1 - xhat * m2)


def _colsum(x):
    return jnp.sum(x, axis=0, keepdims=True)


def _full(shape):
    nd = len(shape)
    return pl.BlockSpec(shape, lambda *_: (0,) * nd)


def _rows(tm, cols, fn=None):
    if fn is None:
        return pl.BlockSpec((tm, cols), lambda i: (i, 0))
    return pl.BlockSpec((tm, cols), lambda i: (fn(i), 0))


def _heads(tm):
    return pl.BlockSpec((HEADS, tm, HEAD_DIM), lambda i: (0, i, 0))


def _in_proj(x, w_in_t):
    T = x.shape[0]
    tm = 512

    def body(x_ref, w_ref, q_ref, kv_ref, xr_ref, gr_ref, xb_ref):
        xb = x_ref[...].astype(MXU_DTYPE)
        xb_ref[...] = xb.astype(BF16)
        q = _mm_nt(xb, w_ref[0:512, :])
        for h in range(HEADS):
            q_ref[h] = q[:, h * 64:(h + 1) * 64].astype(BF16)
        kv_ref[...] = _mm_nt(xb, w_ref[512:768, :]).astype(BF16)
        xr_ref[...] = _mm_nt(xb, w_ref[768:1280, :])
        gr_ref[...] = _mm_nt(xb, w_ref[1280:1792, :])

    return pl.pallas_call(
        body, name="in_proj", grid=(T // tm,),
        in_specs=[_rows(tm, D), _full((D_IN, D))],
        out_specs=[_heads(tm), _rows(tm, 256), _rows(tm, 512), _rows(tm, 512), _rows(tm, D)],
        out_shape=[jax.ShapeDtypeStruct((HEADS, T, 64), BF16), jax.ShapeDtypeStruct((T, 256), BF16),
                   jax.ShapeDtypeStruct((T, 512), F32), jax.ShapeDtypeStruct((T, 512), F32),
                   jax.ShapeDtypeStruct((T, D), BF16)],
        compiler_params=_params(),
    )(x, w_in_t)


def _attn_band(kv_ref, i):
    cur = pl.multiple_of(i * BLK, BLK)
    prev = pl.multiple_of(jnp.maximum(i - 1, 0) * BLK, BLK)
    band = jnp.concatenate([kv_ref[pl.ds(prev, BLK), :], kv_ref[pl.ds(cur, BLK), :]], axis=0)
    key = lax.broadcasted_iota(jnp.int32, (2 * BLK, 4 * BLK), 0)
    qry = lax.broadcasted_iota(jnp.int32, (2 * BLK, 4 * BLK), 1) & (BLK - 1)
    in_prev = jnp.logical_and(jnp.logical_and(key < BLK, key > qry), i > 0)
    mask = jnp.logical_or(in_prev, jnp.logical_and(key >= BLK, key - BLK <= qry))
    return band, mask, cur, prev


def _attn_scores(band, mask, qs, s_ref, g):
    st = jnp.where(mask, _mm_nt(band[:, g * 64:(g + 1) * 64], qs) * SCALE, NEG)
    lane = lax.broadcasted_iota(jnp.int32, (1, 4 * BLK), 1)
    sv = jnp.where(lane < BLK, s_ref[0, 4 * g],
                   jnp.where(lane < 2 * BLK, s_ref[0, 4 * g + 1], jnp.where(lane < 3 * BLK, s_ref[0, 4 * g + 2], s_ref[0, 4 * g + 3])))
    m = jnp.maximum(jnp.max(st, axis=0, keepdims=True), sv)
    p = jnp.exp(st - m)
    ps = jnp.exp(sv - m)
    return p, ps, jnp.sum(p, axis=0, keepdims=True) + ps


def _pos():
    return lax.axis_index("x"), lax.axis_index("y"), lax.axis_index("c")


def _other_chips(x, y):
    return [(1 - x, y), (x, 1 - y), (1 - x, 1 - y)]


def _gather_steps(w_ref, gw_ref, send_sems, recv_sems, local_sem):
    x, y, c = _pos()
    me = 2 * x + y
    chips = _other_chips(x, y)
    half = w_ref.shape[0] // 2
    mine = pl.ds(pl.multiple_of(c * half, 16), half)
    theirs = pl.ds(pl.multiple_of((1 - c) * half, 16), half)
    loc = pltpu.make_async_copy(w_ref, gw_ref.at[me], local_sem)

    def copy(k, src, dst, to):
        return pltpu.make_async_remote_copy(src_ref=src, dst_ref=dst, send_sem=send_sems.at[k], recv_sem=recv_sems.at[k],
                                            device_id=to, device_id_type=MESH)

    def out(k):
        px, py = chips[k]
        return copy(k, w_ref.at[mine], gw_ref.at[me, mine], (px, py, c))

    def fwd(k, rows):
        px, py = chips[k]
        return copy(3 + k, gw_ref.at[2 * px + py, rows], gw_ref.at[2 * px + py, rows], (x, y, 1 - c))

    def start():
        loc.start()
        for k in range(3):
            out(k).start()

    def forward():
        for k in range(3):
            px, py = chips[k]
            copy(k, w_ref.at[mine], gw_ref.at[2 * px + py, mine], (px, py, c)).wait_recv()
            fwd(k, mine).start()

    def finish():
        for k in range(3):
            fwd(k, theirs).wait_recv()
        for k in range(3):
            out(k).wait_send()
            fwd(k, mine).wait_send()
        loc.wait()

    return start, forward, finish


GATHER_SCRATCH = [pltpu.SemaphoreType.DMA((6,)), pltpu.SemaphoreType.DMA((6,)), pltpu.SemaphoreType.DMA]


class _Exchange:
    def __init__(self, args, out_shape, scratch, make):
        self.args, self.out_shape, self.scratch, self.make = list(args), list(out_shape), list(scratch), make


def _gather_exchange(wsrc):
    return _Exchange([wsrc], [jax.ShapeDtypeStruct((4,) + wsrc.shape, wsrc.dtype)], GATHER_SCRATCH,
                     lambda ins, outs, sems: _gather_steps(ins[0], outs[0], *sems))


def _launch(body, name, grid, in_specs, out_specs, out_shape, scratch, args, exchange=None, prefetch=0):
    def call(fn, fn_name, ins, outs, shapes, scr, operands, effects):
        spec = pltpu.PrefetchScalarGridSpec(num_scalar_prefetch=prefetch, grid=grid, in_specs=ins, out_specs=outs,
                                            scratch_shapes=scr)
        return pl.pallas_call(fn, name=fn_name, grid_spec=spec, out_shape=shapes,
                              compiler_params=_params(has_side_effects=effects))(*operands)

    if exchange is None:
        return call(body, name, list(in_specs), list(out_specs), list(out_shape), list(scratch), args, False)
    n_in, n_out, ei, eo, ns = len(in_specs), len(out_specs), len(exchange.args), len(exchange.out_shape), len(exchange.scratch)
    nsteps = 1
    for g in grid:
        nsteps *= g

    def wrapped(*refs):
        scalars, refs = refs[:prefetch], refs[prefetch:]
        ins, xin = refs[:n_in], refs[n_in:n_in + ei]
        outs, xout = refs[n_in + ei:n_in + ei + n_out], refs[n_in + ei + n_out:n_in + ei + n_out + eo]
        rest = refs[n_in + ei + n_out + eo:]
        own, sems = rest[:len(rest) - ns], rest[len(rest) - ns:]
        start, forward, finish = exchange.make(xin, xout, sems)
        i = pl.program_id(0)
        for d in range(1, len(grid)):
            i = i * grid[d] + pl.program_id(d)
        pl.when(i == 0)(start)
        body(*scalars, *ins, *outs, *own)
        pl.when(i == max(nsteps - 3, 0))(forward)
        pl.when(i == nsteps - 1)(finish)

    anyspec = pl.BlockSpec(memory_space=pl.ANY)
    return call(wrapped, name + "_x", list(in_specs) + [anyspec] * ei, list(out_specs) + [anyspec] * eo,
                list(out_shape) + exchange.out_shape, list(scratch) + exchange.scratch, (*args, *exchange.args), True)


def _attn_fwd(q, kv, sinks, exchange=None):
    T = kv.shape[0]

    def body(q_ref, kv_ref, s_ref, o_ref):
        i = pl.program_id(0)
        band, mask, _, _ = _attn_band(kv_ref, i)
        for g in range(2):
            qs = q_ref[4 * g:4 * g + 4].reshape(4 * BLK, HEAD_DIM)
            p, _, den = _attn_scores(band, mask, qs, s_ref, g)
            ot = _mm_tn(band[:, 128:256], p) / den
            for hh in range(4):
                o = ot[:, hh * BLK:(hh + 1) * BLK].T
                o_ref[:, (4 * g + hh) * 64:(4 * g + hh + 1) * 64] = o[:, g * 64:(g + 1) * 64].astype(BF16)

    return _launch(body, "attn_fwd", (T // BLK,), [_heads(BLK), _full((T, 256)), pl.BlockSpec(memory_space=pltpu.SMEM)],
                   [_rows(BLK, 512)], [jax.ShapeDtypeStruct((T, 512), BF16)], [], (q, kv, sinks), exchange)


def _attn_bwd(q, kv, do, sinks, exchange=None):
    T = kv.shape[0]

    def body(q_ref, kv_ref, do_ref, s_ref, dq_ref, dkv_ref, ds_ref):
        i = pl.program_id(0)
        band, mask, cur, prev = _attn_band(kv_ref, i)

        @pl.when(i == 0)
        def _():
            ds_ref[...] = jnp.zeros_like(ds_ref)

        for g in range(2):
            qs = q_ref[4 * g:4 * g + 4].reshape(4 * BLK, HEAD_DIM)
            dos = do_ref[4 * g:4 * g + 4].reshape(4 * BLK, HEAD_DIM)
            p, ps, den = _attn_scores(band, mask, qs, s_ref, g)
            inv = 1.0 / den
            p = p * inv
            dpt = _mm_nt(band[:, 128 + g * 64:192 + g * 64], dos)
            delta = jnp.sum(p * dpt, axis=0, keepdims=True)
            dst = p * (dpt - delta)
            dsv = -(ps * inv) * delta
            for hh in range(4):
                dsink = jnp.sum(dsv[:, hh * BLK:(hh + 1) * BLK], axis=1, keepdims=True)
                ds_ref[4 * g + hh:4 * g + hh + 1, :] += jnp.broadcast_to(dsink, (1, 128))
            dqt = _mm_tn(band[:, 0:128], dst) * SCALE
            for hh in range(4):
                dqh = dqt[:, hh * BLK:(hh + 1) * BLK].T
                dq_ref[:, (4 * g + hh) * 64:(4 * g + hh + 1) * 64] = dqh[:, g * 64:(g + 1) * 64].astype(BF16)
            dk = _mm(dst, qs) * SCALE
            dv = _mm(p, dos)
            dkv_ref[pl.ds(cur, BLK), g * 64:(g + 1) * 64] = dk[BLK:2 * BLK]
            dkv_ref[pl.ds(cur, BLK), 128 + g * 64:192 + g * 64] = dv[BLK:2 * BLK]
            dkv_ref[pl.ds(prev, BLK), g * 64:(g + 1) * 64] += dk[0:BLK]
            dkv_ref[pl.ds(prev, BLK), 128 + g * 64:192 + g * 64] += dv[0:BLK]

    return _launch(body, "attn_bwd", (T // BLK,),
                   [_heads(BLK), _full((T, 256)), _heads(BLK), pl.BlockSpec(memory_space=pltpu.SMEM)],
                   [_rows(BLK, 512), _full((T, 256)), _full((8, 128))],
                   [jax.ShapeDtypeStruct((T, 512), BF16), jax.ShapeDtypeStruct((T, 256), F32),
                    jax.ShapeDtypeStruct((8, 128), F32)], [], (q, kv, do, sinks), exchange)


def _rows8(tm, cols):
    return lax.broadcasted_iota(jnp.int32, (tm, cols), 0) & 7


def _lru_gates(xc, wa, ba, wx, bx, lam):
    r = _sigmoid(_mm(xc, wa) + ba)
    ii = _sigmoid(_mm(xc, wx) + bx)
    sp = _softplus(-lam)
    la = -LRU_C * r * sp
    a = jnp.exp(la)
    m = jnp.sqrt(-_expm1(2.0 * la))
    return r, ii, sp, a, m


def _rnn_fwd(xr, gr, cw, cb, wa, ba, wx, bx, lam, exchange=None):
    T = xr.shape[0]
    tm = 256
    C = D_RNN

    def body(xr_ref, gr_ref, cw_ref, cb_ref, wa_ref, ba_ref, wx_ref, bx_ref, lam_ref,
             xc_ref, h_ref, rec_ref, ext, a_s, b_s, carry):
        i = pl.program_id(0)

        @pl.when(i == 0)
        def _():
            ext[0:8, :] = jnp.zeros((8, C), F32)
            carry[...] = jnp.zeros((8, C), F32)

        ext[8:8 + tm, :] = xr_ref[...]
        xc = cb_ref[...] + cw_ref[3:4, :] * ext[8:8 + tm, :]
        for k in range(3):
            xc = xc + cw_ref[k:k + 1, :] * ext[5 + k:5 + k + tm, :]
        ext[0:8, :] = ext[tm:tm + 8, :]
        xc_ref[...] = xc
        _, ii, _, a, m = _lru_gates(xc, wa_ref[...], ba_ref[...], wx_ref[...], bx_ref[...], lam_ref[...])
        b = m * ii * xc
        r8 = _rows8(tm, C)
        for d in (1, 2, 4):
            ok = r8 >= d
            a_sh = jnp.where(ok, pltpu.roll(a, d, 0), 1.0)
            b_sh = jnp.where(ok, pltpu.roll(b, d, 0), 0.0)
            b = a * b_sh + b
            a = a * a_sh
        a_s[...] = a
        b_s[...] = b

        def step(g, hin):
            s = pl.multiple_of(g * 8, 8)
            hg = a_s[pl.ds(s, 8), :] * hin + b_s[pl.ds(s, 8), :]
            h_ref[pl.ds(s, 8), :] = hg
            return jnp.broadcast_to(hg[7:8, :], (8, C))

        carry[...] = lax.fori_loop(0, tm // 8, step, carry[...])
        ge, _ = _gelu(gr_ref[...])
        rec_ref[...] = (h_ref[...] * ge).astype(BF16)

    vec = _full((1, C))
    in_specs = [_rows(tm, C), _rows(tm, C), _full((4, C)), vec, _full((C, C)), vec, _full((C, C)), vec, vec]
    out_specs = [_rows(tm, C), _rows(tm, C), _rows(tm, C)]
    out_shape = [jax.ShapeDtypeStruct((T, C), F32), jax.ShapeDtypeStruct((T, C), F32), jax.ShapeDtypeStruct((T, C), BF16)]
    scratch = [pltpu.VMEM((tm + 8, C), F32), pltpu.VMEM((tm, C), F32), pltpu.VMEM((tm, C), F32), pltpu.VMEM((8, C), F32)]
    return _launch(body, "rnn_fwd", (T // tm,), in_specs, out_specs, out_shape, scratch,
                   (xr, gr, cw, cb, wa, ba, wx, bx, lam), exchange)


def _rnn_bwd(drec, gr, h, xc, xr, cw, wa, ba, wx, bx, lam, exchange=None):
    T = xr.shape[0]
    tm = 256
    C = D_RNN
    nt = T // tm
    t8 = tm // 8

    def body(drec_ref, gr_ref, h_ref, hp_ref, xc_ref, xr_ref, xrp_ref, cw_ref, wa_ref, ba_ref, wx_ref, bx_ref,
             lam_ref, dxr_ref, dgr_ref, dwa_ref, dwx_ref, dvec_ref, c_s, g_s, gout, ext, xext, anext, gcarry):
        i = pl.program_id(0)
        j = nt - 1 - i

        @pl.when(i == 0)
        def _():
            dwa_ref[...] = jnp.zeros_like(dwa_ref)
            dwx_ref[...] = jnp.zeros_like(dwx_ref)
            dvec_ref[...] = jnp.zeros_like(dvec_ref)
            anext[...] = jnp.zeros((8, C), F32)
            gcarry[...] = jnp.zeros((8, C), F32)
            ext[tm:tm + 8, :] = jnp.zeros((8, C), F32)

        xc = xc_ref[...]
        lam = lam_ref[...]
        r, ii, sp, a, m = _lru_gates(xc, wa_ref[...], ba_ref[...], wx_ref[...], bx_ref[...], lam)
        ge, dge = _gelu(gr_ref[...])
        drec = drec_ref[...]
        hh = h_ref[...]
        dgr_ref[...] = (drec * hh * dge).astype(BF16)
        dh = drec * ge
        rowi = lax.broadcasted_iota(jnp.int32, (tm, C), 0)
        c = jnp.where(rowi == tm - 1, jnp.broadcast_to(anext[0:1, :], (tm, C)), pltpu.roll(a, tm - 1, 0))
        anext[...] = a[0:8, :]
        r8 = rowi & 7
        gg = dh
        for d in (1, 2, 4):
            ok = r8 < 8 - d
            c_sh = jnp.where(ok, pltpu.roll(c, tm - d, 0), 1.0)
            g_sh = jnp.where(ok, pltpu.roll(gg, tm - d, 0), 0.0)
            gg = c * g_sh + gg
            c = c * c_sh
        c_s[...] = c
        g_s[...] = gg

        def step(k, gin):
            s = pl.multiple_of((t8 - 1 - k) * 8, 8)
            og = c_s[pl.ds(s, 8), :] * gin + g_s[pl.ds(s, 8), :]
            gout[pl.ds(s, 8), :] = og
            return jnp.broadcast_to(og[0:1, :], (8, C))

        gcarry[...] = lax.fori_loop(0, t8, step, gcarry[...])
        G = gout[...]
        hprev_row = jnp.where(j > 0, hp_ref[7:8, :], 0.0)
        hprev = jnp.where(rowi == 0, jnp.broadcast_to(hprev_row, (tm, C)), pltpu.roll(hh, 1, 0))
        da = G * hprev
        dm = G * ii * xc
        di = G * m * xc
        dxc = G * m * ii
        dla = da * a - dm * a * a / m
        dr = dla * (-LRU_C * sp)
        dsp = _colsum(dla * (-LRU_C * r))
        dlam = dsp * (-_sigmoid(-lam))
        dpr = dr * r * (1.0 - r)
        dpi = di * ii * (1.0 - ii)
        dxc = dxc + _mm_nt(dpr, wa_ref[...]) + _mm_nt(dpi, wx_ref[...])
        dwa_ref[...] += _mm_tn(xc, dpr)
        dwx_ref[...] += _mm_tn(xc, dpi)
        dvec_ref[0:1, :] += _colsum(dpr)
        dvec_ref[1:2, :] += _colsum(dpi)
        dvec_ref[2:3, :] += dlam
        dvec_ref[3:4, :] += _colsum(dxc)
        ext[0:tm, :] = dxc
        dxr = cw_ref[3:4, :] * dxc
        for k in range(3):
            dxr = dxr + cw_ref[k:k + 1, :] * ext[3 - k:3 - k + tm, :]
        ext[tm:tm + 8, :] = dxc[0:8, :]
        dxr_ref[...] = dxr.astype(BF16)
        xext[0:8, :] = jnp.where(j > 0, xrp_ref[...], 0.0)
        xext[8:8 + tm, :] = xr_ref[...]
        for k in range(4):
            dvec_ref[4 + k:5 + k, :] += _colsum(dxc * xext[5 + k:5 + k + tm, :])

    rev = lambda i: nt - 1 - i
    prev8 = lambda i: jnp.maximum((nt - 1 - i) * t8 - 1, 0)
    vec = _full((1, C))
    return _launch(
        body, "rnn_bwd", (nt,),
        [_rows(tm, C, rev), _rows(tm, C, rev), _rows(tm, C, rev), _rows(8, C, prev8), _rows(tm, C, rev),
         _rows(tm, C, rev), _rows(8, C, prev8), _full((4, C)), _full((C, C)), vec, _full((C, C)), vec, vec],
        [_rows(tm, C, rev), _rows(tm, C, rev), _full((C, C)), _full((C, C)), _full((8, C))],
        [jax.ShapeDtypeStruct((T, C), BF16), jax.ShapeDtypeStruct((T, C), BF16),
         jax.ShapeDtypeStruct((C, C), F32), jax.ShapeDtypeStruct((C, C), F32), jax.ShapeDtypeStruct((8, C), F32)],
        [pltpu.VMEM((tm, C), F32), pltpu.VMEM((tm, C), F32), pltpu.VMEM((tm, C), F32),
         pltpu.VMEM((tm + 8, C), F32), pltpu.VMEM((tm + 8, C), F32), pltpu.VMEM((8, C), F32), pltpu.VMEM((8, C), F32)],
        (drec, gr, h, h, xc, xr, xr, cw, wa, ba, wx, bx, lam), exchange)


def _out_proj(att, rec, x, w_out, g1, b1):
    T = x.shape[0]
    tm = 512

    def body(att_ref, rec_ref, x_ref, w_ref, g1_ref, b1_ref, z_ref, h_ref):
        mix = _mm(att_ref[...], w_ref[0:512, :]) + _mm(rec_ref[...], w_ref[512:1024, :])
        z1 = ALPHA * x_ref[...] + mix
        z_ref[...] = z1
        h1, _, _ = _ln(z1, g1_ref[...], b1_ref[...])
        h_ref[...] = h1.astype(MXU_DTYPE).astype(BF16)

    return pl.pallas_call(
        body, name="out_proj", grid=(T // tm,),
        in_specs=[_rows(tm, 512), _rows(tm, 512), _rows(tm, D), _full((D, D)), _full((1, D)), _full((1, D))],
        out_specs=[_rows(tm, D), _rows(tm, D)],
        out_shape=[jax.ShapeDtypeStruct((T, D), F32), jax.ShapeDtypeStruct((T, D), BF16)],
        compiler_params=_params(),
    )(att, rec, x, w_out, g1, b1)


NC = D_FF // FF_CHUNK


def _ffn_up(h1b, w_up_t, fcw, fcb):
    T = h1b.shape[0]
    tm = 512
    CW = FF_CHUNK

    def body(h_ref, wg_ref, wv_ref, fcw_ref, fcb_ref, gate_ref, ge_ref, vd_ref, act_ref, ext):
        i = pl.program_id(1)

        @pl.when(i == 0)
        def _():
            ext[0:8, :] = jnp.zeros((8, CW), F32)

        hb = h_ref[...]
        gate = _mm_nt(hb, wg_ref[...])
        val = _mm_nt(hb, wv_ref[...])
        gate_ref[...] = gate
        ext[8:8 + tm, :] = gate
        gc = (fcb_ref[...] + fcw_ref[0:1, :] * ext[6:6 + tm, :] + fcw_ref[1:2, :] * ext[7:7 + tm, :]
              + fcw_ref[2:3, :] * gate)
        ext[0:8, :] = ext[tm:tm + 8, :]
        ge, dge = _gelu(gc)
        ge_ref[...] = ge
        vd_ref[...] = val * dge
        act_ref[...] = (ge * val).astype(BF16)

    chunk = pl.BlockSpec((None, tm, CW), lambda c, i: (c, i, 0))
    return pl.pallas_call(
        body, name="ffn_up", grid=(NC, T // tm),
        in_specs=[pl.BlockSpec((tm, D), lambda c, i: (i, 0)), pl.BlockSpec((CW, D), lambda c, i: (c, 0)),
                  pl.BlockSpec((CW, D), lambda c, i: (NC + c, 0)), pl.BlockSpec((None, 3, CW), lambda c, i: (c, 0, 0)),
                  pl.BlockSpec((None, 1, CW), lambda c, i: (c, 0, 0))],
        out_specs=[chunk] * 4,
        out_shape=[jax.ShapeDtypeStruct((NC, T, CW), F32)] * 3 + [jax.ShapeDtypeStruct((NC, T, CW), BF16)],
        scratch_shapes=[pltpu.VMEM((tm + 8, CW), F32)],
        compiler_params=_params(),
    )(h1b, w_up_t, w_up_t, fcw, fcb)


def _ffn_down(act, z1, p, tgt, w_down, w_g, w_p_t, g1, b1, g2, b2, bg):
    T = z1.shape[0]
    tm = 256

    def body(act_ref, z_ref, p_ref, t_ref, wdn_hbm, wg_hbm, wp_hbm, g1_ref, b1_ref, g2_ref, b2_ref, bg_ref,
             dz2_ref, dz2b_ref, dpre_ref, dpp_ref, vec_ref, wdn, wg, wp):
        @pl.when(pl.program_id(0) == 0)
        def _():
            pltpu.sync_copy(wdn_hbm, wdn)
            pltpu.sync_copy(wg_hbm, wg)
            pltpu.sync_copy(wp_hbm, wp)
            vec_ref[...] = jnp.zeros_like(vec_ref)

        g2v = g2_ref[...]
        h1, _, _ = _ln(z_ref[...], g1_ref[...], b1_ref[...])
        h1b = h1.astype(MXU_DTYPE)
        ffn = _mm(act_ref[0], wdn[0:FF_CHUNK, :])
        for c in range(1, NC):
            ffn = ffn + _mm(act_ref[c], wdn[c * FF_CHUNK:(c + 1) * FF_CHUNK, :])
        sg = _sigmoid(_mm(h1b, wg[...]) + bg_ref[...])
        pp = _mm_nt(p_ref[...], wp[...])
        z2 = ALPHA * h1 + ffn + sg * pp
        y, xh2, rstd2 = _ln(z2, g2v, b2_ref[...])
        diff = y - t_ref[...]
        dy = diff * (1.0 / D)
        dz2 = _ln_bwd(dy, xh2, rstd2, g2v)
        dpre = dz2 * pp * sg * (1.0 - sg)
        dz2_ref[...] = dz2
        dz2b_ref[...] = dz2.astype(BF16)
        dpre_ref[...] = dpre.astype(BF16)
        dpp_ref[...] = (dz2 * sg).astype(BF16)
        loss = 0.5 * jnp.sum(jnp.sum(diff * diff, axis=1, keepdims=True), axis=0, keepdims=True) * (1.0 / D)
        vec_ref[0:1, :] += jnp.broadcast_to(loss, (1, D))
        vec_ref[1:2, :] += _colsum(dy * xh2)
        vec_ref[2:3, :] += _colsum(dy)
        vec_ref[3:4, :] += _colsum(dpre)

    anyspec = pl.BlockSpec(memory_space=pl.ANY)
    vec = _full((1, D))
    return pl.pallas_call(
        body, name="ffn_down", grid=(T // tm,),
        in_specs=[pl.BlockSpec((NC, tm, FF_CHUNK), lambda i: (0, i, 0)), _rows(tm, D), _rows(tm, PLE), _rows(tm, D),
                  anyspec, anyspec, anyspec] + [vec] * 5,
        out_specs=[_rows(tm, D)] * 4 + [_full((8, D))],
        out_shape=[jax.ShapeDtypeStruct((T, D), F32)] + [jax.ShapeDtypeStruct((T, D), BF16)] * 3
                  + [jax.ShapeDtypeStruct((8, D), F32)],
        scratch_shapes=[pltpu.VMEM((D_FF, D), MXU_DTYPE), pltpu.VMEM((D, D), MXU_DTYPE), pltpu.VMEM((D, PLE), MXU_DTYPE)],
        compiler_params=_params(),
    )(act, z1, p, tgt, w_down, w_g, w_p_t, g1, b1, g2, b2, bg)


def _ffn_bwd(dz2b, gate, ge, vd, w_down, fcw):
    T = dz2b.shape[0]
    tm = 512
    CW = FF_CHUNK
    nt = T // tm

    def body(dz_ref, wdn_ref, gate_ref, ge_ref, vd_ref, fcw_ref, dup_ref, dfc_ref, dext):
        i = pl.program_id(1)

        @pl.when(i == 0)
        def _():
            dext[tm:tm + 8, :] = jnp.zeros((8, CW), F32)
            dfc_ref[...] = jnp.zeros_like(dfc_ref)

        gate = gate_ref[...]
        dact = _mm_nt(dz_ref[...], wdn_ref[...])
        dgc = dact * vd_ref[...]
        dext[0:tm, :] = dgc
        dgc1 = dext[1:1 + tm, :]
        dgc2 = dext[2:2 + tm, :]
        dext[tm:tm + 8, :] = dgc[0:8, :]
        dup_ref[0] = (fcw_ref[2:3, :] * dgc + fcw_ref[1:2, :] * dgc1 + fcw_ref[0:1, :] * dgc2).astype(BF16)
        dup_ref[1] = (dact * ge_ref[...]).astype(BF16)
        dfc_ref[0:1, :] += _colsum(dgc2 * gate)
        dfc_ref[1:2, :] += _colsum(dgc1 * gate)
        dfc_ref[2:3, :] += _colsum(dgc * gate)
        dfc_ref[3:4, :] += _colsum(dgc)

    rev = lambda c, i: (c, nt - 1 - i, 0)
    chunk = pl.BlockSpec((None, tm, CW), rev)
    return pl.pallas_call(
        body, name="ffn_bwd", grid=(NC, nt),
        in_specs=[pl.BlockSpec((tm, D), lambda c, i: (nt - 1 - i, 0)), pl.BlockSpec((CW, D), lambda c, i: (c, 0)),
                  chunk, chunk, chunk, pl.BlockSpec((None, 3, CW), lambda c, i: (c, 0, 0))],
        out_specs=[pl.BlockSpec((None, 2, tm, CW), lambda c, i: (c, 0, nt - 1 - i, 0)),
                   pl.BlockSpec((None, 8, CW), lambda c, i: (c, 0, 0))],
        out_shape=[jax.ShapeDtypeStruct((NC, 2, T, CW), BF16), jax.ShapeDtypeStruct((NC, 8, CW), F32)],
        scratch_shapes=[pltpu.VMEM((tm + 8, CW), F32)],
        compiler_params=_params(),
    )(dz2b, w_down, gate, ge, vd, fcw)


def _ffn_dh1(dup, dz2, dpre, z1, w_up_t, w_g, g1, b1):
    T = z1.shape[0]
    tm = 256

    def body(dup_ref, dz2_ref, dpre_ref, z_ref, wup_hbm, wg_hbm, g1_ref, b1_ref, dz1_ref, vec_ref, wup, wg):
        @pl.when(pl.program_id(0) == 0)
        def _():
            pltpu.sync_copy(wup_hbm, wup)
            pltpu.sync_copy(wg_hbm, wg)
            vec_ref[...] = jnp.zeros_like(vec_ref)

        g1v = g1_ref[...]
        _, xh1, rstd1 = _ln(z_ref[...], g1v, b1_ref[...])
        dh1 = ALPHA * dz2_ref[...] + _mm_nt(dpre_ref[...], wg[...])
        for c in range(NC):
            for s in range(2):
                r0 = s * D_FF + c * FF_CHUNK
                dh1 = dh1 + _mm(dup_ref[c, s], wup[r0:r0 + FF_CHUNK, :])
        dz1_ref[...] = _ln_bwd(dh1, xh1, rstd1, g1v)
        vec_ref[0:1, :] += _colsum(dh1 * xh1)
        vec_ref[1:2, :] += _colsum(dh1)

    anyspec = pl.BlockSpec(memory_space=pl.ANY)
    vec = _full((1, D))
    return pl.pallas_call(
        body, name="ffn_dh1", grid=(T // tm,),
        in_specs=[pl.BlockSpec((NC, 2, tm, FF_CHUNK), lambda i: (0, 0, i, 0)), _rows(tm, D), _rows(tm, D), _rows(tm, D),
                  anyspec, anyspec, vec, vec],
        out_specs=[_rows(tm, D), _full((8, D))],
        out_shape=[jax.ShapeDtypeStruct((T, D), F32), jax.ShapeDtypeStruct((8, D), F32)],
        scratch_shapes=[pltpu.VMEM((2 * D_FF, D), MXU_DTYPE), pltpu.VMEM((D, D), MXU_DTYPE)],
        compiler_params=_params(),
    )(dup, dz2, dpre, z1, w_up_t, w_g, g1, b1)


def _out_proj_bwd(dz1, w_out, exchange=None):
    T = dz1.shape[0]
    tm = 512

    def body(dz_ref, w_ref, datt_ref, drec_ref, dzb_ref):
        dzb = dz_ref[...].astype(MXU_DTYPE)
        dzb_ref[...] = dzb.astype(BF16)
        datt = _mm_nt(dzb, w_ref[0:512, :])
        for h in range(HEADS):
            datt_ref[h] = datt[:, h * 64:(h + 1) * 64].astype(BF16)
        drec_ref[...] = _mm_nt(dzb, w_ref[512:1024, :])

    return _launch(body, "out_proj_bwd", (T // tm,), [_rows(tm, D), _full((D, D))],
                   [_heads(tm), _rows(tm, 512), _rows(tm, D)],
                   [jax.ShapeDtypeStruct((HEADS, T, 64), BF16), jax.ShapeDtypeStruct((T, 512), F32),
                    jax.ShapeDtypeStruct((T, D), BF16)], [], (dz1, w_out), exchange)


def _in_proj_bwd(dq, dkv, dxr, dgr, dz1, w_in_t, exchange=None):
    T = dz1.shape[0]
    tm = 512

    def body(dq_ref, dkv_ref, dxr_ref, dgr_ref, dz_ref, w_ref, dx_ref, du_ref):
        dkv = dkv_ref[...].astype(BF16)
        dx_ref[...] = (ALPHA * dz_ref[...] + _mm(dq_ref[...], w_ref[0:512, :]) + _mm(dkv, w_ref[512:768, :])
                       + _mm(dxr_ref[...], w_ref[768:1280, :]) + _mm(dgr_ref[...], w_ref[1280:1792, :]))
        du_ref[:, 0:512] = dq_ref[...]
        du_ref[:, 512:768] = dkv
        du_ref[:, 768:1280] = dxr_ref[...]
        du_ref[:, 1280:1792] = dgr_ref[...]

    return _launch(body, "in_proj_bwd", (T // tm,),
                   [_rows(tm, 512), _rows(tm, 256), _rows(tm, 512), _rows(tm, 512), _rows(tm, D), _full((D_IN, D))],
                   [_rows(tm, D), _rows(tm, D_IN)],
                   [jax.ShapeDtypeStruct((T, D), F32), jax.ShapeDtypeStruct((T, D_IN), BF16)], [],
                   (dq, dkv, dxr, dgr, dz1, w_in_t), exchange)


def _accumulate_tn(a_ref, b_ref, o_ref):
    @pl.when(pl.program_id(1) == 0)
    def _():
        o_ref[...] = jnp.zeros_like(o_ref)

    o_ref[...] += _mm_tn(a_ref[...], b_ref[...])


def _weight_grad_cols(a, b, name, n_blocks, b_spec, out_shape, out_spec):
    T, M = a.shape
    bt = min(2048, T)
    return pl.pallas_call(
        functools.partial(_accumulate_tn), name=name, grid=(n_blocks, T // bt),
        in_specs=[pl.BlockSpec((bt, M), lambda m, k: (k, 0)), b_spec(bt)], out_specs=out_spec,
        out_shape=jax.ShapeDtypeStruct(out_shape, F32), compiler_params=_params())(a, b)


def _weight_grad(a, b, bm, name):
    bt = min(2048, b.shape[0])
    if a.ndim == 3:
        assert a.shape[2] == bm
        T, M = a.shape[1], a.shape[0] * bm
        a_spec = pl.BlockSpec((None, bt, bm), lambda m, k: (m, k, 0))
    else:
        T, M = a.shape
        a_spec = pl.BlockSpec((bt, bm), lambda m, k: (k, m))
    N = b.shape[1]
    nk = T // bt

    return pl.pallas_call(
        functools.partial(_accumulate_tn), name=name, grid=(M // bm, nk),
        in_specs=[a_spec, pl.BlockSpec((bt, N), lambda m, k: (k, 0))],
        out_specs=pl.BlockSpec((bm, N), lambda m, k: (m, 0)),
        out_shape=jax.ShapeDtypeStruct((M, N), F32),
        compiler_params=_params(),
    )(a, b)


def _adamw(w, g, m, v, name):
    R, C = w.shape
    tr = R // 8 if R % 64 == 0 else R
    c1 = 1.0 / (1.0 - ADAM_B1 ** ADAM_STEP)
    c2 = 1.0 / (1.0 - ADAM_B2 ** ADAM_STEP)

    def body(w_ref, g_ref, m_ref, v_ref, d_ref, nm_ref, nv_ref):
        g = g_ref[...]
        nm = ADAM_B1 * m_ref[...] + (1.0 - ADAM_B1) * g
        nv = ADAM_B2 * v_ref[...] + (1.0 - ADAM_B2) * g * g
        nm_ref[...] = nm
        nv_ref[...] = nv
        d_ref[...] = -ADAM_LR * ((nm * c1) / (jnp.sqrt(nv * c2) + ADAM_EPS) + ADAM_WD * w_ref[...])

    spec = pl.BlockSpec((tr, C), lambda i: (i, 0))
    return pl.pallas_call(
        body, name=name, grid=(R // tr,),
        in_specs=[spec] * 4, out_specs=[spec] * 3,
        out_shape=[jax.ShapeDtypeStruct((R, C), F32)] * 3,
        compiler_params=_params(),
    )(w, g, m, v)


def _adamw_halves(ws, mines, sibs, ms, vs, c, name, exchange=None):
    n, nb = len(ws), 4
    c1 = 1.0 / (1.0 - ADAM_B1 ** ADAM_STEP)
    c2 = 1.0 / (1.0 - ADAM_B2 ** ADAM_STEP)

    def body(c_ref, *refs):
        own = (pl.program_id(0) // nb) == c_ref[0]
        for i in range(n):
            w_ref, a_ref, b_ref, m_ref, v_ref = refs[5 * i:5 * i + 5]
            g_ref, d_ref, nm_ref, nv_ref = refs[5 * n + 4 * i:5 * n + 4 * i + 4]
            g = jnp.where(own, a_ref[...], b_ref[...])
            nm = ADAM_B1 * m_ref[...] + (1.0 - ADAM_B1) * g
            nv = ADAM_B2 * v_ref[...] + (1.0 - ADAM_B2) * g * g
            g_ref[...] = g
            nm_ref[...] = nm
            nv_ref[...] = nv
            d_ref[...] = -ADAM_LR * ((nm * c1) / (jnp.sqrt(nv * c2) + ADAM_EPS) + ADAM_WD * w_ref[...])

    in_specs, out_specs, out_shape, args = [], [], [], []
    for w, a, b, m, v in zip(ws, mines, sibs, ms, vs):
        R, C = w.shape
        tr = R // (2 * nb)
        assert tr % 8 == 0 and a.shape == (R // 2, C)
        full = pl.BlockSpec((tr, C), lambda i, c_ref: (i, 0))
        half = pl.BlockSpec((tr, C), lambda i, c_ref: (i % nb, 0))
        in_specs += [full, half, half, full, full]
        out_specs += [full] * 4
        out_shape += [jax.ShapeDtypeStruct((R, C), F32)] * 4
        args += [w, a, b, m, v]
    out = _launch(body, name, (2 * nb,), in_specs, out_specs, out_shape, [], (c, *args), exchange, prefetch=1)
    return [tuple(out[4 * i:4 * i + 4]) for i in range(n)], list(out[4 * n:])


def _add4(fs, name):
    n = len(fs)

    def body(*refs):
        for a_ref, o_ref in zip(refs[:n], refs[n:]):
            o_ref[...] = ((a_ref[0].astype(F32) + a_ref[1].astype(F32)) + a_ref[2].astype(F32)) + a_ref[3].astype(F32)

    for f in fs:
        assert (f.shape[1] // 2) % 16 == 0
    return pl.pallas_call(
        body, name=name, grid=(2,),
        in_specs=[pl.BlockSpec((4, f.shape[1] // 2, f.shape[2]), lambda i: (0, i, 0)) for f in fs],
        out_specs=[pl.BlockSpec((f.shape[1] // 2, f.shape[2]), lambda i: (i, 0)) for f in fs],
        out_shape=[jax.ShapeDtypeStruct(f.shape[1:], F32) for f in fs], compiler_params=_params())(*fs)


def _gather_first(wsrc, cpack):
    def body(w_ref, c_ref, gw_ref, gc_ref, send_sems, recv_sems, local_sem, csend, crecv, clocal):
        x, y, c = _pos()
        me = 2 * x + y
        chips = _other_chips(x, y)
        start, forward, finish = _gather_steps(w_ref, gw_ref, send_sems, recv_sems, local_sem)
        start()
        loc = pltpu.make_async_copy(c_ref, gc_ref.at[me], clocal)
        loc.start()

        def conv_copy(k, slot):
            px, py = chips[k]
            return pltpu.make_async_remote_copy(src_ref=c_ref, dst_ref=gc_ref.at[slot], send_sem=csend.at[k],
                                                recv_sem=crecv.at[k], device_id=(px, py, c), device_id_type=MESH)

        for k in range(3):
            conv_copy(k, me).start()
        forward()
        finish()
        for k, (px, py) in enumerate(chips):
            conv_copy(k, 2 * px + py).wait_recv()
        for k in range(3):
            conv_copy(k, me).wait_send()
        loc.wait()

    anyspec = pl.BlockSpec(memory_space=pl.ANY)
    return pl.pallas_call(
        body, name="gather_first",
        in_specs=[anyspec, anyspec], out_specs=[anyspec, anyspec],
        out_shape=[jax.ShapeDtypeStruct((4,) + wsrc.shape, wsrc.dtype), jax.ShapeDtypeStruct((4,) + cpack.shape, cpack.dtype)],
        scratch_shapes=GATHER_SCRATCH + [pltpu.SemaphoreType.DMA((3,)), pltpu.SemaphoreType.DMA((3,)), pltpu.SemaphoreType.DMA],
        compiler_params=_params(has_side_effects=True),
    )(wsrc, cpack)


def _all_devices_exchange(s):
    def make(ins, outs, sems):
        s_ref, o_ref = ins[0], outs[0]
        send_sems, recv_sems, local_sem = sems
        x, y, c = _pos()
        me = 4 * x + 2 * y + c
        loc = pltpu.make_async_copy(s_ref, o_ref.at[me], local_sem)

        def copy(k, slot):
            peer = (x ^ (k >> 2), y ^ ((k >> 1) & 1), c ^ (k & 1))
            return pltpu.make_async_remote_copy(src_ref=s_ref, dst_ref=o_ref.at[slot], send_sem=send_sems.at[k - 1],
                                                recv_sem=recv_sems.at[k - 1], device_id=peer, device_id_type=MESH)

        def start():
            loc.start()
            for k in range(1, 8):
                copy(k, me).start()

        def finish():
            for k in range(1, 8):
                copy(k, 4 * (x ^ (k >> 2)) + 2 * (y ^ ((k >> 1) & 1)) + (c ^ (k & 1))).wait_recv()
            for k in range(1, 8):
                copy(k, me).wait_send()
            loc.wait()

        return start, lambda: None, finish

    return _Exchange([s], [jax.ShapeDtypeStruct((8,) + s.shape, s.dtype)],
                     [pltpu.SemaphoreType.DMA((7,)), pltpu.SemaphoreType.DMA((7,)), pltpu.SemaphoreType.DMA], make)


def _sum_devices(a):
    def body(a_ref, o_ref):
        acc = a_ref[0]
        for d in range(1, 8):
            acc = acc + a_ref[d]
        o_ref[...] = acc

    vm = pl.BlockSpec(memory_space=pltpu.VMEM)
    return pl.pallas_call(body, name="sum_devices", in_specs=[vm], out_specs=vm,
                          out_shape=jax.ShapeDtypeStruct(a.shape[1:], F32), compiler_params=_params())(a)


def _swap_exchange(gs):
    n = len(gs)

    def make(ins, outs, sems):
        x, y, c = _pos()
        cps = []
        for i in range(n):
            half = gs[i].shape[1] // 2
            rows = pl.ds(pl.multiple_of((1 - c) * half, 8), half)
            cps.append(pltpu.make_async_remote_copy(src_ref=ins[i].at[:, rows, :], dst_ref=outs[i], send_sem=sems[0].at[i],
                                                    recv_sem=sems[1].at[i], device_id=(x, y, 1 - c), device_id_type=MESH))

        def start():
            for cp in cps:
                cp.start()

        def finish():
            for cp in cps:
                cp.wait()

        return start, lambda: None, finish

    return _Exchange(gs, [jax.ShapeDtypeStruct((4, g.shape[1] // 2, g.shape[2]), g.dtype) for g in gs],
                     [pltpu.SemaphoreType.DMA((n,)), pltpu.SemaphoreType.DMA((n,))], make)


def _scatter_exchange(ss):
    n = len(ss)

    def make(ins, outs, sems):
        send_sems, recv_sems, local_sems = sems
        x, y, c = _pos()
        me = 2 * x + y
        chips = _other_chips(x, y)
        locs = [pltpu.make_async_copy(ins[i].at[me], outs[i].at[me], local_sems.at[i]) for i in range(n)]

        def copy(i, k, src_slot, dst_slot):
            px, py = chips[k]
            return pltpu.make_async_remote_copy(src_ref=ins[i].at[src_slot], dst_ref=outs[i].at[dst_slot],
                                                send_sem=send_sems.at[3 * i + k], recv_sem=recv_sems.at[3 * i + k],
                                                device_id=(px, py, c), device_id_type=MESH)

        def start():
            for i in range(n):
                locs[i].start()
                for k, (px, py) in enumerate(chips):
                    copy(i, k, 2 * px + py, me).start()

        def finish():
            for i in range(n):
                for k, (px, py) in enumerate(chips):
                    copy(i, k, me, 2 * px + py).wait_recv()
            for i in range(n):
                for k, (px, py) in enumerate(chips):
                    copy(i, k, 2 * px + py, me).wait_send()
                locs[i].wait()

        return start, lambda: None, finish

    return _Exchange(ss, [jax.ShapeDtypeStruct(s.shape, s.dtype) for s in ss],
                     [pltpu.SemaphoreType.DMA((3 * n,)), pltpu.SemaphoreType.DMA((3 * n,)), pltpu.SemaphoreType.DMA((n,))], make)


def _send_exchange(rs):
    n = len(rs)

    def make(ins, outs, sems):
        x, y, c = _pos()
        cps = [pltpu.make_async_remote_copy(src_ref=ins[i], dst_ref=outs[i], send_sem=sems[0].at[i], recv_sem=sems[1].at[i],
                                            device_id=(x, y, 1 - c), device_id_type=MESH) for i in range(n)]

        def start():
            for cp in cps:
                cp.start()

        def finish():
            for cp in cps:
                cp.wait()

        return start, lambda: None, finish

    return _Exchange(rs, [jax.ShapeDtypeStruct(r.shape, r.dtype) for r in rs],
                     [pltpu.SemaphoreType.DMA((n,)), pltpu.SemaphoreType.DMA((n,))], make)


def _run_exchange(ex, name):
    ei, eo = len(ex.args), len(ex.out_shape)

    def body(*refs):
        start, forward, finish = ex.make(refs[:ei], refs[ei:ei + eo], refs[ei + eo:])
        start()
        forward()
        finish()

    anyspec = pl.BlockSpec(memory_space=pl.ANY)
    return pl.pallas_call(body, name=name, in_specs=[anyspec] * ei, out_specs=[anyspec] * eo, out_shape=ex.out_shape,
                          scratch_shapes=ex.scratch, compiler_params=_params(has_side_effects=True))(*ex.args)


def _add_half(gs, rs, c, name):
    n = len(gs)

    def body(c_ref, *refs):
        for g_ref, r_ref, o_ref in zip(refs[:n], refs[n:2 * n], refs[2 * n:]):
            o_ref[...] = (g_ref[...] + r_ref[...]).astype(BF16)

    g_specs, r_specs, out_shape = [], [], []
    for g, r in zip(gs, rs):
        _, H, C = r.shape
        tr = H // 2
        assert tr % 16 == 0 and g.shape == (4, 2 * H, C)
        g_specs.append(pl.BlockSpec((1, tr, C), lambda j, i, c_ref: (j, c_ref[0] * 2 + i, 0)))
        r_specs.append(pl.BlockSpec((1, tr, C), lambda j, i, c_ref: (j, i, 0)))
        out_shape.append(jax.ShapeDtypeStruct((4, H, C), BF16))
    grid_spec = pltpu.PrefetchScalarGridSpec(num_scalar_prefetch=1, grid=(4, 2), in_specs=g_specs + r_specs, out_specs=r_specs)
    return pl.pallas_call(body, name=name, grid_spec=grid_spec, out_shape=out_shape, compiler_params=_params())(c, *gs, *rs)


def _block_diag(w):
    eye = jnp.eye(RNN_BLOCKS, dtype=w.dtype)
    return (eye[:, None, :, None] * w[:, :, None, :]).reshape(D_RNN, D_RNN)


def _diag_blocks(wd):
    d = wd.reshape(RNN_BLOCKS, 64, RNN_BLOCKS, 64)
    return jnp.stack([d[h, :, h, :] for h in range(RNN_BLOCKS)])


def _split_pack(a, first, last):
    out, base = {}, PACK_OFF[first]
    for i in range(first, last):
        s = a[:, PACK_OFF[i] - base:PACK_OFF[i + 1] - base]
        out[BIG_KEYS[i]] = s.reshape(4 * 256, 256) if BIG_KEYS[i] == "w_p_t" else s.reshape(-1, 1024)
    return out


def _layer_grads(x, p, tgt, gw, small, shard=None, core=None):
    row = lambda v: v.reshape(1, -1)
    wa = _block_diag(small["gate_a_w"]).astype(MXU_DTYPE)
    wx = _block_diag(small["gate_x_w"]).astype(MXU_DTYPE)
    sinks = small["attn_sinks"].reshape(1, HEADS)

    dist = shard is not None
    q, kv, xr, gr, xb = _in_proj(x, gw["w_in_t"])
    att, *ga = _attn_fwd(q, kv, sinks, _gather_exchange(shard[PACK_OFF[1]:PACK_OFF[2]]) if dist else None)
    xc, h, rec, *gb = _rnn_fwd(xr, gr, small["rnn_conv_w"], row(small["rnn_conv_b"]), wa, row(small["gate_a_b"]),
                               wx, row(small["gate_x_b"]), row(small["lru_lambda"]),
                               _gather_exchange(shard[PACK_OFF[2]:PACK_OFF[6]]) if dist else None)
    if dist:
        gw = {**gw, **_split_pack(ga[0], 1, 2), **_split_pack(gb[0], 2, 6)}
    g1, b1 = row(small["ln1_g"]), row(small["ln1_b"])
    fcw = small["ffn_conv_w"].reshape(3, NC, FF_CHUNK).transpose(1, 0, 2)
    fcb = small["ffn_conv_b"].reshape(NC, 1, FF_CHUNK)
    z1, h1b = _out_proj(att, rec, x, gw["w_out"], g1, b1)
    gate, ge, vd, act = _ffn_up(h1b, gw["w_up_t"], fcw, fcb)
    dz2, dz2b, dpre, dpp, vec2 = _ffn_down(act, z1, p, tgt, gw["w_down"], gw["w_g"], gw["w_p_t"], g1, b1,
                                           row(small["ln2_g"]), row(small["ln2_b"]), row(small["ple_gate_b"]))
    dup, dfc = _ffn_bwd(dz2b, gate, ge, vd, gw["w_down"], fcw)
    dz1, vec1 = _ffn_dh1(dup, dz2, dpre, z1, gw["w_up_t"], gw["w_g"], g1, b1)
    per_chip = 2 * D_FF // 4 // FF_CHUNK
    big = {
        "w_ffn_up": _weight_grad_cols(
            h1b, dup.reshape(2 * NC, -1, FF_CHUNK), "dw_up", 2 * NC,
            lambda bt: pl.BlockSpec((None, bt, FF_CHUNK), lambda m, k: (m, k, 0)), (4, D, 2 * D_FF // 4),
            pl.BlockSpec((None, D, FF_CHUNK), lambda m, k: (2 * (m % 2) + (m // 2) // per_chip, 0, (m // 2) % per_chip))),
        "w_ffn_down": _weight_grad(act, dz2b, 512, "dw_down").reshape(4, D_FF // 4, D),
        "ple_gate_w": _weight_grad(h1b, dpre, 512, "dw_gate").reshape(4, D // 4, D),
        "ple_proj": _weight_grad_cols(
            p.astype(BF16), dpp, "dw_proj", 4, lambda bt: pl.BlockSpec((bt, D // 4), lambda j, k: (k, j)),
            (4, PLE, D // 4), pl.BlockSpec((None, PLE, D // 4), lambda j, k: (j, 0, 0))),
    }
    reduced = None
    if dist:
        g_ffn = [big[k] for k in FFN_WEIGHTS]
        ex = _swap_exchange(g_ffn)
    datt, drec, dz1b, *got = _out_proj_bwd(dz1, gw["w_out"], ex if dist else None)
    if dist:
        ex = _scatter_exchange(_add_half(g_ffn, got, core, "add_half_ffn"))
    dxr, dgr, dwa, dwx, dvec, *got = _rnn_bwd(drec, gr, h, xc, xr, small["rnn_conv_w"], wa, row(small["gate_a_b"]),
                                              wx, row(small["gate_x_b"]), row(small["lru_lambda"]), ex if dist else None)
    if dist:
        mine = _add4(got, "add_chips_ffn")
        ex = _send_exchange(mine)
    dq, dkv, dsinks, *got = _attn_bwd(q, kv, datt, sinks, ex if dist else None)
    if dist:
        reduced = (mine, got)
        big = {}
    sg = {
        "attn_sinks": dsinks[:, 0],
        "rnn_conv_w": dvec[4:8],
        "rnn_conv_b": dvec[3],
        "gate_a_w": _diag_blocks(dwa),
        "gate_a_b": dvec[0],
        "gate_x_w": _diag_blocks(dwx),
        "gate_x_b": dvec[1],
        "lru_lambda": dvec[2],
        "ln1_g": vec1[0],
        "ln1_b": vec1[1],
        "ffn_conv_w": dfc[:, 0:3].transpose(1, 0, 2).reshape(3, D_FF),
        "ffn_conv_b": dfc[:, 3].reshape(D_FF),
        "ple_gate_b": vec2[3],
        "ln2_g": vec2[1],
        "ln2_b": vec2[2],
    }
    loss = vec2[0, 0:1]
    ex = _all_devices_exchange(_pack_vecs([sg[k] for k in SMALL] + [loss])[0]) if dist else None
    grad_x, du, *small_all = _in_proj_bwd(dq, dkv, dxr, dgr, dz1, gw["w_in_t"], ex)
    big["w_in"] = _weight_grad(xb, du, 512, "dw_in").reshape(D, 4, D_IN // 4).transpose(1, 0, 2)
    big["w_out"] = _weight_grad(jnp.concatenate([att, rec], axis=1), dz1b, 512, "dw_out").reshape(4, D // 4, D)
    return grad_x, big, sg, loss, reduced, small_all


BIG = ("w_in", "w_ffn_up", "w_out", "w_ffn_down", "ple_gate_w", "ple_proj")
BIG_KEYS = ("w_in_t", "w_up_t", "w_out", "w_down", "w_g", "w_p_t")
BIG_T = (True, True, False, False, False, True)
FFN_WEIGHTS = ("w_ffn_up", "w_ffn_down", "ple_gate_w", "ple_proj")
MIX_WEIGHTS = ("w_in", "w_out")
SMALL = ("attn_sinks", "rnn_conv_w", "rnn_conv_b", "gate_a_w", "gate_a_b", "gate_x_w", "gate_x_b", "lru_lambda",
         "ln1_g", "ln1_b", "ffn_conv_w", "ffn_conv_b", "ple_gate_b", "ln2_g", "ln2_b")
SHARDED_SMALL = ("rnn_conv_w", "ffn_conv_w")
WEIGHTS = ("w_in", "attn_sinks", "rnn_conv_w", "rnn_conv_b", "gate_a_w", "gate_a_b", "gate_x_w", "gate_x_b",
           "lru_lambda", "w_out", "ln1_g", "ln1_b", "w_ffn_up", "ffn_conv_w", "ffn_conv_b", "w_ffn_down",
           "ple_gate_w", "ple_gate_b", "ple_proj", "ln2_g", "ln2_b")


def _pack_big(d, first=0, last=6):
    parts = []
    for name, t in zip(BIG[first:last], BIG_T[first:last]):
        a = d[name]
        a = a.T if t else a
        parts.append(a.reshape(-1, 1024))
    return jnp.concatenate(parts, axis=0)


def _pack_vecs(items):
    parts, offs, n = [], [], 0
    for a in items:
        f = a.reshape(-1).astype(F32)
        pad = (-f.shape[0]) % 128
        parts.append(jnp.pad(f, (0, pad)))
        offs.append(n)
        n += (f.shape[0] + pad) // 128
    padr = (-n) % 8
    if padr:
        parts.append(jnp.zeros((padr * 128,), F32))
    return jnp.concatenate(parts).reshape(-1, 128), offs


def _unpack_vecs(a, offs, shapes):
    flat = a.reshape(-1)
    out = []
    for o, s in zip(offs, shapes):
        n = 1
        for d in s:
            n *= d
        out.append(flat[o * 128:o * 128 + n].reshape(s))
    return out


def kernel(x, p, w_in, attn_sinks, rnn_conv_w, rnn_conv_b, gate_a_w, gate_a_b, gate_x_w, gate_x_b, lru_lambda, w_out, ln1_g, ln1_b, w_ffn_up, ffn_conv_w, ffn_conv_b, w_ffn_down, ple_gate_w, ple_gate_b, ple_proj, ln2_g, ln2_b, loss_target, m_w_in, m_attn_sinks, m_rnn_conv_w, m_rnn_conv_b, m_gate_a_w, m_gate_a_b, m_gate_x_w, m_gate_x_b, m_lru_lambda, m_w_out, m_ln1_g, m_ln1_b, m_w_ffn_up, m_ffn_conv_w, m_ffn_conv_b, m_w_ffn_down, m_ple_gate_w, m_ple_gate_b, m_ple_proj, m_ln2_g, m_ln2_b, v_w_in, v_attn_sinks, v_rnn_conv_w, v_rnn_conv_b, v_gate_a_w, v_gate_a_b, v_gate_x_w, v_gate_x_b, v_lru_lambda, v_w_out, v_ln1_g, v_ln1_b, v_w_ffn_up, v_ffn_conv_w, v_ffn_conv_b, v_w_ffn_down, v_ple_gate_w, v_ple_gate_b, v_ple_proj, v_ln2_g, v_ln2_b):
    w = dict(w_in=w_in, attn_sinks=attn_sinks, rnn_conv_w=rnn_conv_w, rnn_conv_b=rnn_conv_b, gate_a_w=gate_a_w,
             gate_a_b=gate_a_b, gate_x_w=gate_x_w, gate_x_b=gate_x_b, lru_lambda=lru_lambda, w_out=w_out, ln1_g=ln1_g,
             ln1_b=ln1_b, w_ffn_up=w_ffn_up, ffn_conv_w=ffn_conv_w, ffn_conv_b=ffn_conv_b, w_ffn_down=w_ffn_down,
             ple_gate_w=ple_gate_w, ple_gate_b=ple_gate_b, ple_proj=ple_proj, ln2_g=ln2_g, ln2_b=ln2_b)
    m = dict(w_in=m_w_in, attn_sinks=m_attn_sinks, rnn_conv_w=m_rnn_conv_w, rnn_conv_b=m_rnn_conv_b, gate_a_w=m_gate_a_w,
             gate_a_b=m_gate_a_b, gate_x_w=m_gate_x_w, gate_x_b=m_gate_x_b, lru_lambda=m_lru_lambda, w_out=m_w_out,
             ln1_g=m_ln1_g, ln1_b=m_ln1_b, w_ffn_up=m_w_ffn_up, ffn_conv_w=m_ffn_conv_w, ffn_conv_b=m_ffn_conv_b,
             w_ffn_down=m_w_ffn_down, ple_gate_w=m_ple_gate_w, ple_gate_b=m_ple_gate_b, ple_proj=m_ple_proj,
             ln2_g=m_ln2_g, ln2_b=m_ln2_b)
    v = dict(w_in=v_w_in, attn_sinks=v_attn_sinks, rnn_conv_w=v_rnn_conv_w, rnn_conv_b=v_rnn_conv_b, gate_a_w=v_gate_a_w,
             gate_a_b=v_gate_a_b, gate_x_w=v_gate_x_w, gate_x_b=v_gate_x_b, lru_lambda=v_lru_lambda, w_out=v_w_out,
             ln1_g=v_ln1_g, ln1_b=v_ln1_b, w_ffn_up=v_w_ffn_up, ffn_conv_w=v_ffn_conv_w, ffn_conv_b=v_ffn_conv_b,
             w_ffn_down=v_w_ffn_down, ple_gate_w=v_ple_gate_w, ple_gate_b=v_ple_gate_b, ple_proj=v_ple_proj,
             ln2_g=v_ln2_g, ln2_b=v_ln2_b)
    w, m, v = ({k: a[0] for k, a in d.items()} for d in (w, m, v))
    chip = 2 * lax.axis_index("x") + lax.axis_index("y")
    core = lax.axis_index("c")

    wpack = _pack_big(w)
    cpack, _ = _pack_vecs([w["rnn_conv_w"], w["ffn_conv_w"]])
    shard = wpack.astype(MXU_DTYPE)
    g_in, gcp = _gather_first(shard[PACK_OFF[0]:PACK_OFF[1]], cpack)
    gw = _split_pack(g_in, 0, 1)
    small = {k: w[k] for k in SMALL}
    small["rnn_conv_w"] = gcp[:, 0:4].reshape(4, 4, 128).transpose(1, 0, 2).reshape(4, 512)
    small["ffn_conv_w"] = gcp[:, 4:22].reshape(4, 3, 768).transpose(1, 0, 2).reshape(3, 3072)

    core1 = core.reshape(1).astype(jnp.int32)
    grad_x, big, sg, loss, ffn_halves, small_all = _layer_grads(x[0], p[0, 0], loss_target[0], gw, small, shard, core1)

    shapes = [sg[k].shape for k in SMALL] + [(1,)]
    _, offs = _pack_vecs([jnp.zeros(s, F32) for s in shapes])
    red = dict(zip(SMALL + ("loss",), _unpack_vecs(_sum_devices(small_all[0]), offs, shapes)))
    red["rnn_conv_w"] = lax.dynamic_slice_in_dim(red["rnn_conv_w"], chip * 128, 128, axis=1)
    red["ffn_conv_w"] = lax.dynamic_slice_in_dim(red["ffn_conv_w"], chip * 768, 768, axis=1)

    def adamw(names, mine, other, name, exchange=None):
        out, got = _adamw_halves([w[k] for k in names], mine, other, [m[k] for k in names], [v[k] for k in names],
                                 core1, name, exchange)
        return dict(zip(names, out)), got

    g_mix = [big[k] for k in MIX_WEIGHTS]
    sib = _run_exchange(_swap_exchange(g_mix), "swap_mix")
    ffn_out, from_chips = adamw(FFN_WEIGHTS, *ffn_halves, "adamw_ffn",
                                _scatter_exchange(_add_half(g_mix, sib, core1, "add_half_mix")))
    mix_mine = _add4(from_chips, "add_chips_mix")
    mix_other = _run_exchange(_send_exchange(mix_mine), "send_mix")
    big_out = {**adamw(MIX_WEIGHTS, mix_mine, mix_other, "adamw_mix")[0], **ffn_out}
    wsm, offs2 = _pack_vecs([w[k] for k in SMALL])
    gsm, _ = _pack_vecs([red[k] for k in SMALL])
    msm, _ = _pack_vecs([m[k] for k in SMALL])
    vsm, _ = _pack_vecs([v[k] for k in SMALL])
    dsm, nmsm, nvsm = _adamw(wsm, gsm, msm, vsm, "adamw_small")
    shapes2 = [w[k].shape for k in SMALL]

    def named(n, smallp):
        d = {k: out[n][None] for k, out in big_out.items()}
        d.update({k: a[None] for k, a in zip(SMALL, _unpack_vecs(smallp, offs2, shapes2))})
        return [d[k] for k in WEIGHTS]

    return (red["loss"].reshape(()), grad_x[None], *named(0, gsm), *named(1, dsm), *named(2, nmsm), *named(3, nvsm))
```

```python
import functools

import jax
import jax.numpy as jnp
from jax import lax
from jax.experimental import pallas as pl
from jax.experimental.pallas import tpu as pltpu

F32 = jnp.float32
BF16 = jnp.bfloat16
MXU_DTYPE = jnp.bfloat16

D = 1024
D_ATT = 512
D_KV = 128
D_RNN = 512
D_IN = 1792
D_FF = 3072
FF_CHUNK = 512
PLE = 256
HEADS = 8
HEAD_DIM = 64
BLK = 128
RNN_BLOCKS = 8
LN_EPS = 1e-5
LRU_C = 8.0
ALPHA = float(2.0 ** 0.25)
SCALE = HEAD_DIM ** -0.5
NEG = -1e30

ADAM_LR = 0.001
ADAM_B1 = 0.9
ADAM_B2 = 0.999
ADAM_EPS = 1e-08
ADAM_WD = 0.01
ADAM_STEP = 10

VMEM_LIMIT_BYTES = 56 * 1024 * 1024
MESH = pl.DeviceIdType.MESH

PACK_ROWS = (448, 1536, 256, 768, 256, 64)
PACK_OFF = tuple(sum(PACK_ROWS[:i]) for i in range(len(PACK_ROWS) + 1))
PACK_TOTAL = PACK_OFF[-1]


def _params(**kw):
    return pltpu.CompilerParams(vmem_limit_bytes=VMEM_LIMIT_BYTES, **kw)


def _mm(a, b):
    return jnp.dot(a.astype(MXU_DTYPE), b.astype(MXU_DTYPE), preferred_element_type=F32)


def _mm_nt(a, b):
    return lax.dot_general(a.astype(MXU_DTYPE), b.astype(MXU_DTYPE), (((1,), (1,)), ((), ())),
                           preferred_element_type=F32)


def _mm_tn(a, b):
    return lax.dot_general(a.astype(MXU_DTYPE), b.astype(MXU_DTYPE), (((0,), (0,)), ((), ())),
                           preferred_element_type=F32)


def _sigmoid(x):
    return 1.0 / (1.0 + jnp.exp(-x))


def _gelu(x):
    c = 0.7978845608028654
    k = 0.044715
    x2 = x * x
    t = jnp.tanh(x * (c + (c * k) * x2))
    h = 0.5 * (1.0 + t)
    return x * h, h * (1.0 + (x * (1.0 - t)) * (c + (3.0 * c * k) * x2))


def _shift_rows(x, s, edge8):
    R = x.shape[0]
    row8 = lax.broadcasted_iota(jnp.int32, (8, x.shape[1]), 0)
    if s > 0:
        rolled = pltpu.roll(x, s, 0)
        first = jnp.where(row8 < s, pltpu.roll(edge8, s, 0), rolled[0:8])
        return jnp.concatenate([first, rolled[8:]], axis=0)
    k = -s
    rolled = pltpu.roll(x, R - k, 0)
    last = jnp.where(row8 >= 8 - k, pltpu.roll(edge8, 8 - k, 0), rolled[R - 8:])
    return jnp.concatenate([rolled[:R - 8], last], axis=0)


def _expm1(x):
    poly = x * (1.0 + x * (0.5 + x * (1.0 / 6.0 + x * (1.0 / 24.0 + x * (1.0 / 120.0)))))
    return jnp.where(jnp.abs(x) < 0.03, poly, jnp.exp(x) - 1.0)


def _softplus(x):
    return jnp.maximum(x, 0.0) + jnp.log(1.0 + jnp.exp(-jnp.abs(x)))


def _ln(z, g, b):
    mu = jnp.mean(z, axis=-1, keepdims=True)
    zc = z - mu
    var = jnp.mean(zc * zc, axis=-1, keepdims=True)
    rstd = lax.rsqrt(var + LN_EPS)
    xhat = zc * rstd
    return xhat * g + b, xhat, rstd


def _ln_bwd(dy, xhat, rstd, g):
    dxh = dy * g
    m1 = jnp.mean(dxh, axis=-1, keepdims=True)
    m2 = jnp.mean(dxh * xhat, axis=-1, keepdims=True)
    return rstd * (dxh - m1 - xhat * m2)


def _colsum(x):
    return jnp.sum(x, axis=0, keepdims=True)


def _full(shape):
    nd = len(shape)
    return pl.BlockSpec(shape, lambda *_: (0,) * nd)


def _rows(tm, cols, fn=None):
    if fn is None:
        return pl.BlockSpec((tm, cols), lambda i: (i, 0))
    return pl.BlockSpec((tm, cols), lambda i: (fn(i), 0))


def _heads(tm):
    return pl.BlockSpec((HEADS, tm, HEAD_DIM), lambda i: (0, i, 0))


def _in_proj(x, w_in_t):
    T = x.shape[0]
    tm = 512

    def body(x_ref, w_ref, q_ref, kv_ref, xr_ref, gr_ref, xb_ref):
        xb = x_ref[...].astype(MXU_DTYPE)
        xb_ref[...] = xb.astype(BF16)
        q = _mm_nt(xb, w_ref[0:512, :])
        for h in range(HEADS):
            q_ref[h] = q[:, h * 64:(h + 1) * 64].astype(BF16)
        kv_ref[...] = _mm_nt(xb, w_ref[512:768, :]).astype(BF16)
        xr_ref[...] = _mm_nt(xb, w_ref[768:1280, :])
        gr_ref[...] = _mm_nt(xb, w_ref[1280:1792, :])

    return pl.pallas_call(
        body, name="in_proj", grid=(T // tm,),
        in_specs=[_rows(tm, D), _full((D_IN, D))],
        out_specs=[_heads(tm), _rows(tm, 256), _rows(tm, 512), _rows(tm, 512), _rows(tm, D)],
        out_shape=[jax.ShapeDtypeStruct((HEADS, T, 64), BF16), jax.ShapeDtypeStruct((T, 256), BF16),
                   jax.ShapeDtypeStruct((T, 512), F32), jax.ShapeDtypeStruct((T, 512), F32),
                   jax.ShapeDtypeStruct((T, D), BF16)],
        compiler_params=_params(),
    )(x, w_in_t)


def _attn_band(kv_ref, i):
    cur = pl.multiple_of(i * BLK, BLK)
    prev = pl.multiple_of(jnp.maximum(i - 1, 0) * BLK, BLK)
    band = jnp.concatenate([kv_ref[pl.ds(prev, BLK), :], kv_ref[pl.ds(cur, BLK), :]], axis=0)
    key = lax.broadcasted_iota(jnp.int32, (2 * BLK, 4 * BLK), 0)
    qry = lax.broadcasted_iota(jnp.int32, (2 * BLK, 4 * BLK), 1) & (BLK - 1)
    in_prev = jnp.logical_and(jnp.logical_and(key < BLK, key > qry), i > 0)
    mask = jnp.logical_or(in_prev, jnp.logical_and(key >= BLK, key - BLK <= qry))
    return band, mask, cur, prev


def _attn_scores(band, mask, qs, s_ref, g):
    st = jnp.where(mask, _mm_nt(band[:, g * 64:(g + 1) * 64], qs) * SCALE, NEG)
    lane = lax.broadcasted_iota(jnp.int32, (1, 4 * BLK), 1)
    sv = jnp.where(lane < BLK, s_ref[0, 4 * g],
                   jnp.where(lane < 2 * BLK, s_ref[0, 4 * g + 1], jnp.where(lane < 3 * BLK, s_ref[0, 4 * g + 2], s_ref[0, 4 * g + 3])))
    m = jnp.maximum(jnp.max(st, axis=0, keepdims=True), sv)
    p = jnp.exp(st - m)
    ps = jnp.exp(sv - m)
    return p, ps, jnp.sum(p, axis=0, keepdims=True) + ps


def _pos():
    return lax.axis_index("x"), lax.axis_index("y"), lax.axis_index("c")


def _other_chips(x, y):
    return [(1 - x, y), (x, 1 - y), (1 - x, 1 - y)]


def _gather_steps(w_ref, gw_ref, send_sems, recv_sems, local_sem):
    x, y, c = _pos()
    me = 2 * x + y
    chips = _other_chips(x, y)
    half = w_ref.shape[0] // 2
    mine = pl.ds(pl.multiple_of(c * half, 16), half)
    theirs = pl.ds(pl.multiple_of((1 - c) * half, 16), half)
    loc = pltpu.make_async_copy(w_ref, gw_ref.at[me], local_sem)

    def copy(k, src, dst, to):
        return pltpu.make_async_remote_copy(src_ref=src, dst_ref=dst, send_sem=send_sems.at[k], recv_sem=recv_sems.at[k],
                                            device_id=to, device_id_type=MESH)

    def out(k):
        px, py = chips[k]
        return copy(k, w_ref.at[mine], gw_ref.at[me, mine], (px, py, c))

    def fwd(k, rows):
        px, py = chips[k]
        return copy(3 + k, gw_ref.at[2 * px + py, rows], gw_ref.at[2 * px + py, rows], (x, y, 1 - c))

    def start():
        loc.start()
        for k in range(3):
            out(k).start()

    def forward():
        for k in range(3):
            px, py = chips[k]
            copy(k, w_ref.at[mine], gw_ref.at[2 * px + py, mine], (px, py, c)).wait_recv()
            fwd(k, mine).start()

    def finish():
        for k in range(3):
            fwd(k, theirs).wait_recv()
        for k in range(3):
            out(k).wait_send()
            fwd(k, mine).wait_send()
        loc.wait()

    return start, forward, finish


GATHER_SCRATCH = [pltpu.SemaphoreType.DMA((6,)), pltpu.SemaphoreType.DMA((6,)), pltpu.SemaphoreType.DMA]


class _Exchange:
    def __init__(self, args, out_shape, scratch, make):
        self.args, self.out_shape, self.scratch, self.make = list(args), list(out_shape), list(scratch), make


def _gather_exchange(wsrc):
    return _Exchange([wsrc], [jax.ShapeDtypeStruct((4,) + wsrc.shape, wsrc.dtype)], GATHER_SCRATCH,
                     lambda ins, outs, sems: _gather_steps(ins[0], outs[0], *sems))


def _launch(body, name, grid, in_specs, out_specs, out_shape, scratch, args, exchange=None, prefetch=0):
    def call(fn, fn_name, ins, outs, shapes, scr, operands, effects):
        spec = pltpu.PrefetchScalarGridSpec(num_scalar_prefetch=prefetch, grid=grid, in_specs=ins, out_specs=outs,
                                            scratch_shapes=scr)
        return pl.pallas_call(fn, name=fn_name, grid_spec=spec, out_shape=shapes,
                              compiler_params=_params(has_side_effects=effects))(*operands)

    if exchange is None:
        return call(body, name, list(in_specs), list(out_specs), list(out_shape), list(scratch), args, False)
    n_in, n_out, ei, eo, ns = len(in_specs), len(out_specs), len(exchange.args), len(exchange.out_shape), len(exchange.scratch)
    nsteps = 1
    for g in grid:
        nsteps *= g

    def wrapped(*refs):
        scalars, refs = refs[:prefetch], refs[prefetch:]
        ins, xin = refs[:n_in], refs[n_in:n_in + ei]
        outs, xout = refs[n_in + ei:n_in + ei + n_out], refs[n_in + ei + n_out:n_in + ei + n_out + eo]
        rest = refs[n_in + ei + n_out + eo:]
        own, sems = rest[:len(rest) - ns], rest[len(rest) - ns:]
        start, forward, finish = exchange.make(xin, xout, sems)
        i = pl.program_id(0)
        for d in range(1, len(grid)):
            i = i * grid[d] + pl.program_id(d)
        pl.when(i == 0)(start)
        body(*scalars, *ins, *outs, *own)
        pl.when(i == max(nsteps - 3, 0))(forward)
        pl.when(i == nsteps - 1)(finish)

    anyspec = pl.BlockSpec(memory_space=pl.ANY)
    return call(wrapped, name + "_x", list(in_specs) + [anyspec] * ei, list(out_specs) + [anyspec] * eo,
                list(out_shape) + exchange.out_shape, list(scratch) + exchange.scratch, (*args, *exchange.args), True)


def _attn_fwd(q, kv, sinks, exchange=None):
    T = kv.shape[0]

    def body(q_ref, kv_ref, s_ref, o_ref):
        i = pl.program_id(0)
        band, mask, _, _ = _attn_band(kv_ref, i)
        for g in range(2):
            qs = q_ref[4 * g:4 * g + 4].reshape(4 * BLK, HEAD_DIM)
            p, _, den = _attn_scores(band, mask, qs, s_ref, g)
            ot = _mm_tn(band[:, 128:256], p) / den
            for hh in range(4):
                o = ot[:, hh * BLK:(hh + 1) * BLK].T
                o_ref[:, (4 * g + hh) * 64:(4 * g + hh + 1) * 64] = o[:, g * 64:(g + 1) * 64].astype(BF16)

    return _launch(body, "attn_fwd", (T // BLK,), [_heads(BLK), _full((T, 256)), pl.BlockSpec(memory_space=pltpu.SMEM)],
                   [_rows(BLK, 512)], [jax.ShapeDtypeStruct((T, 512), BF16)], [], (q, kv, sinks), exchange)


def _attn_bwd(q, kv, do, sinks, exchange=None):
    T = kv.shape[0]

    def body(q_ref, kv_ref, do_ref, s_ref, dq_ref, dkv_ref, ds_ref):
        i = pl.program_id(0)
        band, mask, cur, prev = _attn_band(kv_ref, i)

        @pl.when(i == 0)
        def _():
            ds_ref[...] = jnp.zeros_like(ds_ref)

        for g in range(2):
            qs = q_ref[4 * g:4 * g + 4].reshape(4 * BLK, HEAD_DIM)
            dos = do_ref[4 * g:4 * g + 4].reshape(4 * BLK, HEAD_DIM)
            p, ps, den = _attn_scores(band, mask, qs, s_ref, g)
            inv = 1.0 / den
            p = p * inv
            dpt = _mm_nt(band[:, 128 + g * 64:192 + g * 64], dos)
            delta = jnp.sum(p * dpt, axis=0, keepdims=True)
            dst = p * (dpt - delta)
            dsv = -(ps * inv) * delta
            for hh in range(4):
                dsink = jnp.sum(dsv[:, hh * BLK:(hh + 1) * BLK], axis=1, keepdims=True)
                ds_ref[4 * g + hh:4 * g + hh + 1, :] += jnp.broadcast_to(dsink, (1, 128))
            dqt = _mm_tn(band[:, 0:128], dst) * SCALE
            for hh in range(4):
                dqh = dqt[:, hh * BLK:(hh + 1) * BLK].T
                dq_ref[:, (4 * g + hh) * 64:(4 * g + hh + 1) * 64] = dqh[:, g * 64:(g + 1) * 64].astype(BF16)
            dk = _mm(dst, qs) * SCALE
            dv = _mm(p, dos)
            dkv_ref[pl.ds(cur, BLK), g * 64:(g + 1) * 64] = dk[BLK:2 * BLK]
            dkv_ref[pl.ds(cur, BLK), 128 + g * 64:192 + g * 64] = dv[BLK:2 * BLK]
            dkv_ref[pl.ds(prev, BLK), g * 64:(g + 1) * 64] += dk[0:BLK]
            dkv_ref[pl.ds(prev, BLK), 128 + g * 64:192 + g * 64] += dv[0:BLK]

    return _launch(body, "attn_bwd", (T // BLK,),
                   [_heads(BLK), _full((T, 256)), _heads(BLK), pl.BlockSpec(memory_space=pltpu.SMEM)],
                   [_rows(BLK, 512), _full((T, 256)), _full((8, 128))],
                   [jax.ShapeDtypeStruct((T, 512), BF16), jax.ShapeDtypeStruct((T, 256), F32),
                    jax.ShapeDtypeStruct((8, 128), F32)], [], (q, kv, do, sinks), exchange)


def _rows8(tm, cols):
    return lax.broadcasted_iota(jnp.int32, (tm, cols), 0) & 7


def _lru_gates(xc, wa, ba, wx, bx, lam):
    r = _sigmoid(_mm(xc, wa) + ba)
    ii = _sigmoid(_mm(xc, wx) + bx)
    sp = _softplus(-lam)
    la = -LRU_C * r * sp
    a = jnp.exp(la)
    m = jnp.sqrt(-_expm1(2.0 * la))
    return r, ii, sp, a, m


def _rnn_fwd(xr, gr, cw, cb, wa, ba, wx, bx, lam, exchange=None):
    T = xr.shape[0]
    tm = 256
    C = D_RNN

    def body(xr_ref, gr_ref, cw_ref, cb_ref, wa_ref, ba_ref, wx_ref, bx_ref, lam_ref,
             xc_ref, h_ref, rec_ref, ext, a_s, b_s, carry):
        i = pl.program_id(0)

        @pl.when(i == 0)
        def _():
            ext[...] = jnp.zeros((8, C), F32)
            carry[...] = jnp.zeros((8, C), F32)

        xr = xr_ref[...]
        edge = ext[...]
        xc = cb_ref[...] + cw_ref[3:4, :] * xr
        for k in range(3):
            xc = xc + cw_ref[k:k + 1, :] * _shift_rows(xr, 3 - k, edge)
        ext[...] = xr[tm - 8:tm, :]
        xc_ref[...] = xc
        _, ii, _, a, m = _lru_gates(xc, wa_ref[...], ba_ref[...], wx_ref[...], bx_ref[...], lam_ref[...])
        b = m * ii * xc
        r8 = _rows8(tm, C)
        for d in (1, 2, 4):
            ok = r8 >= d
            a_sh = jnp.where(ok, pltpu.roll(a, d, 0), 1.0)
            b_sh = jnp.where(ok, pltpu.roll(b, d, 0), 0.0)
            b = a * b_sh + b
            a = a * a_sh
        a_s[...] = a
        b_s[...] = b

        def step(g, hin):
            s = pl.multiple_of(g * 8, 8)
            hg = a_s[pl.ds(s, 8), :] * hin + b_s[pl.ds(s, 8), :]
            h_ref[pl.ds(s, 8), :] = hg
            return jnp.broadcast_to(hg[7:8, :], (8, C))

        carry[...] = lax.fori_loop(0, tm // 8, step, carry[...])
        ge, _ = _gelu(gr_ref[...])
        rec_ref[...] = (h_ref[...] * ge).astype(BF16)

    vec = _full((1, C))
    in_specs = [_rows(tm, C), _rows(tm, C), _full((4, C)), vec, _full((C, C)), vec, _full((C, C)), vec, vec]
    out_specs = [_rows(tm, C), _rows(tm, C), _rows(tm, C)]
    out_shape = [jax.ShapeDtypeStruct((T, C), F32), jax.ShapeDtypeStruct((T, C), F32), jax.ShapeDtypeStruct((T, C), BF16)]
    scratch = [pltpu.VMEM((8, C), F32), pltpu.VMEM((tm, C), F32), pltpu.VMEM((tm, C), F32), pltpu.VMEM((8, C), F32)]
    return _launch(body, "rnn_fwd", (T // tm,), in_specs, out_specs, out_shape, scratch,
                   (xr, gr, cw, cb, wa, ba, wx, bx, lam), exchange)


def _rnn_bwd(drec, gr, h, xc, xr, cw, wa, ba, wx, bx, lam, exchange=None):
    T = xr.shape[0]
    tm = 256
    C = D_RNN
    nt = T // tm
    t8 = tm // 8

    def body(drec_ref, gr_ref, h_ref, hp_ref, xc_ref, xr_ref, cw_ref, wa_ref, ba_ref, wx_ref, bx_ref,
             lam_ref, dxr_ref, dgr_ref, dwa_ref, dwx_ref, dvec_ref, c_s, g_s, gout, ext, anext, gcarry):
        i = pl.program_id(0)
        j = nt - 1 - i

        @pl.when(i == 0)
        def _():
            dwa_ref[...] = jnp.zeros_like(dwa_ref)
            dwx_ref[...] = jnp.zeros_like(dwx_ref)
            dvec_ref[...] = jnp.zeros_like(dvec_ref)
            anext[...] = jnp.zeros((8, C), F32)
            gcarry[...] = jnp.zeros((8, C), F32)
            ext[...] = jnp.zeros((8, C), F32)

        xc = xc_ref[...]
        lam = lam_ref[...]
        r, ii, sp, a, m = _lru_gates(xc, wa_ref[...], ba_ref[...], wx_ref[...], bx_ref[...], lam)
        ge, dge = _gelu(gr_ref[...])
        drec = drec_ref[...]
        hh = h_ref[...]
        dgr_ref[...] = (drec * hh * dge).astype(BF16)
        dh = drec * ge
        rowi = lax.broadcasted_iota(jnp.int32, (tm, C), 0)
        c = jnp.where(rowi == tm - 1, jnp.broadcast_to(anext[0:1, :], (tm, C)), pltpu.roll(a, tm - 1, 0))
        anext[...] = a[0:8, :]
        r8 = rowi & 7
        gg = dh
        for d in (1, 2, 4):
            ok = r8 < 8 - d
            c_sh = jnp.where(ok, pltpu.roll(c, tm - d, 0), 1.0)
            g_sh = jnp.where(ok, pltpu.roll(gg, tm - d, 0), 0.0)
            gg = c * g_sh + gg
            c = c * c_sh
        c_s[...] = c
        g_s[...] = gg

        def step(k, gin):
            s = pl.multiple_of((t8 - 1 - k) * 8, 8)
            og = c_s[pl.ds(s, 8), :] * gin + g_s[pl.ds(s, 8), :]
            gout[pl.ds(s, 8), :] = og
            return jnp.broadcast_to(og[0:1, :], (8, C))

        gcarry[...] = lax.fori_loop(0, t8, step, gcarry[...])
        G = gout[...]
        hprev_row = jnp.where(j > 0, hp_ref[7:8, :], 0.0)
        hprev = jnp.where(rowi == 0, jnp.broadcast_to(hprev_row, (tm, C)), pltpu.roll(hh, 1, 0))
        da = G * hprev
        dm = G * ii * xc
        di = G * m * xc
        dxc = G * m * ii
        dla = da * a - dm * a * a / m
        dr = dla * (-LRU_C * sp)
        dsp = _colsum(dla * (-LRU_C * r))
        dlam = dsp * (-_sigmoid(-lam))
        dpr = dr * r * (1.0 - r)
        dpi = di * ii * (1.0 - ii)
        dxc = dxc + _mm_nt(dpr, wa_ref[...]) + _mm_nt(dpi, wx_ref[...])
        dwa_ref[...] += _mm_tn(xc, dpr)
        dwx_ref[...] += _mm_tn(xc, dpi)
        dvec_ref[0:1, :] += _colsum(dpr)
        dvec_ref[1:2, :] += _colsum(dpi)
        dvec_ref[2:3, :] += dlam
        dvec_ref[3:4, :] += _colsum(dxc)
        edge = ext[...]
        xr = xr_ref[...]
        dxr = cw_ref[3:4, :] * dxc
        dvec_ref[7:8, :] += _colsum(dxc * xr)
        for k in range(3):
            up = _shift_rows(dxc, k - 3, edge)
            dxr = dxr + cw_ref[k:k + 1, :] * up
            dvec_ref[4 + k:5 + k, :] += _colsum(up * xr)
        ext[...] = dxc[0:8, :]
        dxr_ref[...] = dxr.astype(BF16)

    rev = lambda i: nt - 1 - i
    prev8 = lambda i: jnp.maximum((nt - 1 - i) * t8 - 1, 0)
    vec = _full((1, C))
    return _launch(
        body, "rnn_bwd", (nt,),
        [_rows(tm, C, rev), _rows(tm, C, rev), _rows(tm, C, rev), _rows(8, C, prev8), _rows(tm, C, rev),
         _rows(tm, C, rev), _full((4, C)), _full((C, C)), vec, _full((C, C)), vec, vec],
        [_rows(tm, C, rev), _rows(tm, C, rev), _full((C, C)), _full((C, C)), _full((8, C))],
        [jax.ShapeDtypeStruct((T, C), BF16), jax.ShapeDtypeStruct((T, C), BF16),
         jax.ShapeDtypeStruct((C, C), F32), jax.ShapeDtypeStruct((C, C), F32), jax.ShapeDtypeStruct((8, C), F32)],
        [pltpu.VMEM((tm, C), F32), pltpu.VMEM((tm, C), F32), pltpu.VMEM((tm, C), F32),
         pltpu.VMEM((8, C), F32), pltpu.VMEM((8, C), F32), pltpu.VMEM((8, C), F32)],
        (drec, gr, h, h, xc, xr, cw, wa, ba, wx, bx, lam), exchange)


def _out_proj(att, rec, x, w_out, g1, b1):
    T = x.shape[0]
    tm = 512

    def body(att_ref, rec_ref, x_ref, w_ref, g1_ref, b1_ref, z_ref, h_ref):
        mix = _mm(att_ref[...], w_ref[0:512, :]) + _mm(rec_ref[...], w_ref[512:1024, :])
        z1 = ALPHA * x_ref[...] + mix
        z_ref[...] = z1
        h1, _, _ = _ln(z1, g1_ref[...], b1_ref[...])
        h_ref[...] = h1.astype(MXU_DTYPE).astype(BF16)

    return pl.pallas_call(
        body, name="out_proj", grid=(T // tm,),
        in_specs=[_rows(tm, 512), _rows(tm, 512), _rows(tm, D), _full((D, D)), _full((1, D)), _full((1, D))],
        out_specs=[_rows(tm, D), _rows(tm, D)],
        out_shape=[jax.ShapeDtypeStruct((T, D), F32), jax.ShapeDtypeStruct((T, D), BF16)],
        compiler_params=_params(),
    )(att, rec, x, w_out, g1, b1)


NC = D_FF // FF_CHUNK


def _ffn_up(h1b, w_up_t, fcw, fcb):
    T = h1b.shape[0]
    tm = 512
    CW = FF_CHUNK

    def body(h_ref, wg_ref, wv_ref, fcw_ref, fcb_ref, gate_ref, ge_ref, vd_ref, act_ref, before):
        i = pl.program_id(1)

        @pl.when(i == 0)
        def _():
            before[...] = jnp.zeros((8, CW), F32)

        hb = h_ref[...]
        gate = _mm_nt(hb, wg_ref[...])
        val = _mm_nt(hb, wv_ref[...])
        gate_ref[...] = gate
        edge = before[...]
        gc = (fcb_ref[...] + fcw_ref[0:1, :] * _shift_rows(gate, 2, edge) + fcw_ref[1:2, :] * _shift_rows(gate, 1, edge)
              + fcw_ref[2:3, :] * gate)
        before[...] = gate[tm - 8:tm, :]
        ge, dge = _gelu(gc)
        ge_ref[...] = ge
        vd_ref[...] = val * dge
        act_ref[...] = (ge * val).astype(BF16)

    chunk = pl.BlockSpec((None, tm, CW), lambda c, i: (c, i, 0))
    return pl.pallas_call(
        body, name="ffn_up", grid=(NC, T // tm),
        in_specs=[pl.BlockSpec((tm, D), lambda c, i: (i, 0)), pl.BlockSpec((CW, D), lambda c, i: (c, 0)),
                  pl.BlockSpec((CW, D), lambda c, i: (NC + c, 0)), pl.BlockSpec((None, 3, CW), lambda c, i: (c, 0, 0)),
                  pl.BlockSpec((None, 1, CW), lambda c, i: (c, 0, 0))],
        out_specs=[chunk] * 4,
        out_shape=[jax.ShapeDtypeStruct((NC, T, CW), F32)] * 3 + [jax.ShapeDtypeStruct((NC, T, CW), BF16)],
        scratch_shapes=[pltpu.VMEM((8, CW), F32)],
        compiler_params=_params(),
    )(h1b, w_up_t, w_up_t, fcw, fcb)


def _ffn_down(act, z1, p, tgt, w_down, w_g, w_p_t, g1, b1, g2, b2, bg):
    T = z1.shape[0]
    tm = 256

    def body(act_ref, z_ref, p_ref, t_ref, wdn_hbm, wg_hbm, wp_hbm, g1_ref, b1_ref, g2_ref, b2_ref, bg_ref,
             dz2_ref, dz2b_ref, dpre_ref, dpp_ref, vec_ref, wdn, wg, wp):
        @pl.when(pl.program_id(0) == 0)
        def _():
            pltpu.sync_copy(wdn_hbm, wdn)
            pltpu.sync_copy(wg_hbm, wg)
            pltpu.sync_copy(wp_hbm, wp)
            vec_ref[...] = jnp.zeros_like(vec_ref)

        g2v = g2_ref[...]
        h1, _, _ = _ln(z_ref[...], g1_ref[...], b1_ref[...])
        h1b = h1.astype(MXU_DTYPE)
        ffn = _mm(act_ref[0], wdn[0:FF_CHUNK, :])
        for c in range(1, NC):
            ffn = ffn + _mm(act_ref[c], wdn[c * FF_CHUNK:(c + 1) * FF_CHUNK, :])
        sg = _sigmoid(_mm(h1b, wg[...]) + bg_ref[...])
        pp = _mm_nt(p_ref[...], wp[...])
        z2 = ALPHA * h1 + ffn + sg * pp
        y, xh2, rstd2 = _ln(z2, g2v, b2_ref[...])
        diff = y - t_ref[...]
        dy = diff * (1.0 / D)
        dz2 = _ln_bwd(dy, xh2, rstd2, g2v)
        dpre = dz2 * pp * sg * (1.0 - sg)
        dz2_ref[...] = dz2
        dz2b_ref[...] = dz2.astype(BF16)
        dpre_ref[...] = dpre.astype(BF16)
        dpp_ref[...] = (dz2 * sg).astype(BF16)
        loss = 0.5 * jnp.sum(jnp.sum(diff * diff, axis=1, keepdims=True), axis=0, keepdims=True) * (1.0 / D)
        vec_ref[0:1, :] += jnp.broadcast_to(loss, (1, D))
        vec_ref[1:2, :] += _colsum(dy * xh2)
        vec_ref[2:3, :] += _colsum(dy)
        vec_ref[3:4, :] += _colsum(dpre)

    anyspec = pl.BlockSpec(memory_space=pl.ANY)
    vec = _full((1, D))
    return pl.pallas_call(
        body, name="ffn_down", grid=(T // tm,),
        in_specs=[pl.BlockSpec((NC, tm, FF_CHUNK), lambda i: (0, i, 0)), _rows(tm, D), _rows(tm, PLE), _rows(tm, D),
                  anyspec, anyspec, anyspec] + [vec] * 5,
        out_specs=[_rows(tm, D)] * 4 + [_full((8, D))],
        out_shape=[jax.ShapeDtypeStruct((T, D), F32)] + [jax.ShapeDtypeStruct((T, D), BF16)] * 3
                  + [jax.ShapeDtypeStruct((8, D), F32)],
        scratch_shapes=[pltpu.VMEM((D_FF, D), MXU_DTYPE), pltpu.VMEM((D, D), MXU_DTYPE), pltpu.VMEM((D, PLE), MXU_DTYPE)],
        compiler_params=_params(),
    )(act, z1, p, tgt, w_down, w_g, w_p_t, g1, b1, g2, b2, bg)


def _ffn_bwd(dz2b, gate, ge, vd, w_down, fcw):
    T = dz2b.shape[0]
    tm = 512
    CW = FF_CHUNK
    nt = T // tm

    def body(dz_ref, wdn_ref, gate_ref, ge_ref, vd_ref, fcw_ref, dup_ref, dfc_ref, after):
        i = pl.program_id(1)

        @pl.when(i == 0)
        def _():
            after[...] = jnp.zeros((8, CW), F32)
            dfc_ref[...] = jnp.zeros_like(dfc_ref)

        gate = gate_ref[...]
        dact = _mm_nt(dz_ref[...], wdn_ref[...])
        dgc = dact * vd_ref[...]
        edge = after[...]
        dgc1 = _shift_rows(dgc, -1, edge)
        dgc2 = _shift_rows(dgc, -2, edge)
        after[...] = dgc[0:8, :]
        dup_ref[0] = (fcw_ref[2:3, :] * dgc + fcw_ref[1:2, :] * dgc1 + fcw_ref[0:1, :] * dgc2).astype(BF16)
        dup_ref[1] = (dact * ge_ref[...]).astype(BF16)
        dfc_ref[0:1, :] += _colsum(dgc2 * gate)
        dfc_ref[1:2, :] += _colsum(dgc1 * gate)
        dfc_ref[2:3, :] += _colsum(dgc * gate)
        dfc_ref[3:4, :] += _colsum(dgc)

    rev = lambda c, i: (c, nt - 1 - i, 0)
    chunk = pl.BlockSpec((None, tm, CW), rev)
    return pl.pallas_call(
        body, name="ffn_bwd", grid=(NC, nt),
        in_specs=[pl.BlockSpec((tm, D), lambda c, i: (nt - 1 - i, 0)), pl.BlockSpec((CW, D), lambda c, i: (c, 0)),
                  chunk, chunk, chunk, pl.BlockSpec((None, 3, CW), lambda c, i: (c, 0, 0))],
        out_specs=[pl.BlockSpec((None, 2, tm, CW), lambda c, i: (c, 0, nt - 1 - i, 0)),
                   pl.BlockSpec((None, 8, CW), lambda c, i: (c, 0, 0))],
        out_shape=[jax.ShapeDtypeStruct((NC, 2, T, CW), BF16), jax.ShapeDtypeStruct((NC, 8, CW), F32)],
        scratch_shapes=[pltpu.VMEM((8, CW), F32)],
        compiler_params=_params(),
    )(dz2b, w_down, gate, ge, vd, fcw)


def _ffn_dh1(dup, dz2, dpre, z1, w_up_t, w_g, g1, b1):
    T = z1.shape[0]
    tm = 256

    def body(dup_ref, dz2_ref, dpre_ref, z_ref, wup_hbm, wg_hbm, g1_ref, b1_ref, dz1_ref, vec_ref, wup, wg):
        @pl.when(pl.program_id(0) == 0)
        def _():
            pltpu.sync_copy(wup_hbm, wup)
            pltpu.sync_copy(wg_hbm, wg)
            vec_ref[...] = jnp.zeros_like(vec_ref)

        g1v = g1_ref[...]
        _, xh1, rstd1 = _ln(z_ref[...], g1v, b1_ref[...])
        dh1 = ALPHA * dz2_ref[...] + _mm_nt(dpre_ref[...], wg[...])
        for c in range(NC):
            for s in range(2):
                r0 = s * D_FF + c * FF_CHUNK
                dh1 = dh1 + _mm(dup_ref[c, s], wup[r0:r0 + FF_CHUNK, :])
        dz1_ref[...] = _ln_bwd(dh1, xh1, rstd1, g1v)
        vec_ref[0:1, :] += _colsum(dh1 * xh1)
        vec_ref[1:2, :] += _colsum(dh1)

    anyspec = pl.BlockSpec(memory_space=pl.ANY)
    vec = _full((1, D))
    return pl.pallas_call(
        body, name="ffn_dh1", grid=(T // tm,),
        in_specs=[pl.BlockSpec((NC, 2, tm, FF_CHUNK), lambda i: (0, 0, i, 0)), _rows(tm, D), _rows(tm, D), _rows(tm, D),
                  anyspec, anyspec, vec, vec],
        out_specs=[_rows(tm, D), _full((8, D))],
        out_shape=[jax.ShapeDtypeStruct((T, D), F32), jax.ShapeDtypeStruct((8, D), F32)],
        scratch_shapes=[pltpu.VMEM((2 * D_FF, D), MXU_DTYPE), pltpu.VMEM((D, D), MXU_DTYPE)],
        compiler_params=_params(),
    )(dup, dz2, dpre, z1, w_up_t, w_g, g1, b1)


def _out_proj_bwd(dz1, w_out, exchange=None):
    T = dz1.shape[0]
    tm = 512

    def body(dz_ref, w_ref, datt_ref, drec_ref, dzb_ref):
        dzb = dz_ref[...].astype(MXU_DTYPE)
        dzb_ref[...] = dzb.astype(BF16)
        datt = _mm_nt(dzb, w_ref[0:512, :])
        for h in range(HEADS):
            datt_ref[h] = datt[:, h * 64:(h + 1) * 64].astype(BF16)
        drec_ref[...] = _mm_nt(dzb, w_ref[512:1024, :])

    return _launch(body, "out_proj_bwd", (T // tm,), [_rows(tm, D), _full((D, D))],
                   [_heads(tm), _rows(tm, 512), _rows(tm, D)],
                   [jax.ShapeDtypeStruct((HEADS, T, 64), BF16), jax.ShapeDtypeStruct((T, 512), F32),
                    jax.ShapeDtypeStruct((T, D), BF16)], [], (dz1, w_out), exchange)


def _in_proj_bwd(dq, dkv, dxr, dgr, dz1, w_in_t, exchange=None):
    T = dz1.shape[0]
    tm = 512

    def body(dq_ref, dkv_ref, dxr_ref, dgr_ref, dz_ref, w_ref, dx_ref, du_ref):
        dkv = dkv_ref[...].astype(BF16)
        dx_ref[...] = (ALPHA * dz_ref[...] + _mm(dq_ref[...], w_ref[0:512, :]) + _mm(dkv, w_ref[512:768, :])
                       + _mm(dxr_ref[...], w_ref[768:1280, :]) + _mm(dgr_ref[...], w_ref[1280:1792, :]))
        du_ref[:, 0:512] = dq_ref[...]
        du_ref[:, 512:768] = dkv
        du_ref[:, 768:1280] = dxr_ref[...]
        du_ref[:, 1280:1792] = dgr_ref[...]

    return _launch(body, "in_proj_bwd", (T // tm,),
                   [_rows(tm, 512), _rows(tm, 256), _rows(tm, 512), _rows(tm, 512), _rows(tm, D), _full((D_IN, D))],
                   [_rows(tm, D), _rows(tm, D_IN)],
                   [jax.ShapeDtypeStruct((T, D), F32), jax.ShapeDtypeStruct((T, D_IN), BF16)], [],
                   (dq, dkv, dxr, dgr, dz1, w_in_t), exchange)


def _accumulate_tn(a_ref, b_ref, o_ref):
    @pl.when(pl.program_id(1) == 0)
    def _():
        o_ref[...] = jnp.zeros_like(o_ref)

    o_ref[...] += _mm_tn(a_ref[...], b_ref[...])


def _weight_grad_cols(a, b, name, n_blocks, b_spec, out_shape, out_spec):
    T, M = a.shape
    bt = min(2048, T)
    return pl.pallas_call(
        functools.partial(_accumulate_tn), name=name, grid=(n_blocks, T // bt),
        in_specs=[pl.BlockSpec((bt, M), lambda m, k: (k, 0)), b_spec(bt)], out_specs=out_spec,
        out_shape=jax.ShapeDtypeStruct(out_shape, F32), compiler_params=_params())(a, b)


def _weight_grad(a, b, bm, name):
    bt = min(2048, b.shape[0])
    if a.ndim == 3:
        assert a.shape[2] == bm
        T, M = a.shape[1], a.shape[0] * bm
        a_spec = pl.BlockSpec((None, bt, bm), lambda m, k: (m, k, 0))
    else:
        T, M = a.shape
        a_spec = pl.BlockSpec((bt, bm), lambda m, k: (k, m))
    N = b.shape[1]
    nk = T // bt

    return pl.pallas_call(
        functools.partial(_accumulate_tn), name=name, grid=(M // bm, nk),
        in_specs=[a_spec, pl.BlockSpec((bt, N), lambda m, k: (k, 0))],
        out_specs=pl.BlockSpec((bm, N), lambda m, k: (m, 0)),
        out_shape=jax.ShapeDtypeStruct((M, N), F32),
        compiler_params=_params(),
    )(a, b)


def _adamw(w, g, m, v, name):
    R, C = w.shape
    tr = R // 8 if R % 64 == 0 else R
    c1 = 1.0 / (1.0 - ADAM_B1 ** ADAM_STEP)
    c2 = 1.0 / (1.0 - ADAM_B2 ** ADAM_STEP)

    def body(w_ref, g_ref, m_ref, v_ref, d_ref, nm_ref, nv_ref):
        g = g_ref[...]
        nm = ADAM_B1 * m_ref[...] + (1.0 - ADAM_B1) * g
        nv = ADAM_B2 * v_ref[...] + (1.0 - ADAM_B2) * g * g
        nm_ref[...] = nm
        nv_ref[...] = nv
        d_ref[...] = -ADAM_LR * ((nm * c1) / (jnp.sqrt(nv * c2) + ADAM_EPS) + ADAM_WD * w_ref[...])

    spec = pl.BlockSpec((tr, C), lambda i: (i, 0))
    return pl.pallas_call(
        body, name=name, grid=(R // tr,),
        in_specs=[spec] * 4, out_specs=[spec] * 3,
        out_shape=[jax.ShapeDtypeStruct((R, C), F32)] * 3,
        compiler_params=_params(),
    )(w, g, m, v)


def _adamw_halves(ws, mines, sibs, ms, vs, c, name, exchange=None):
    n, nb = len(ws), 4
    c1 = 1.0 / (1.0 - ADAM_B1 ** ADAM_STEP)
    c2 = 1.0 / (1.0 - ADAM_B2 ** ADAM_STEP)

    def body(c_ref, *refs):
        own = (pl.program_id(0) // nb) == c_ref[0]
        for i in range(n):
            w_ref, a_ref, b_ref, m_ref, v_ref = refs[5 * i:5 * i + 5]
            g_ref, d_ref, nm_ref, nv_ref = refs[5 * n + 4 * i:5 * n + 4 * i + 4]
            g = jnp.where(own, a_ref[...], b_ref[...])
            nm = ADAM_B1 * m_ref[...] + (1.0 - ADAM_B1) * g
            nv = ADAM_B2 * v_ref[...] + (1.0 - ADAM_B2) * g * g
            g_ref[...] = g
            nm_ref[...] = nm
            nv_ref[...] = nv
            d_ref[...] = -ADAM_LR * ((nm * c1) / (jnp.sqrt(nv * c2) + ADAM_EPS) + ADAM_WD * w_ref[...])

    in_specs, out_specs, out_shape, args = [], [], [], []
    for w, a, b, m, v in zip(ws, mines, sibs, ms, vs):
        R, C = w.shape
        tr = R // (2 * nb)
        assert tr % 8 == 0 and a.shape == (R // 2, C)
        full = pl.BlockSpec((tr, C), lambda i, c_ref: (i, 0))
        half = pl.BlockSpec((tr, C), lambda i, c_ref: (i % nb, 0))
        in_specs += [full, half, half, full, full]
        out_specs += [full] * 4
        out_shape += [jax.ShapeDtypeStruct((R, C), F32)] * 4
        args += [w, a, b, m, v]
    out = _launch(body, name, (2 * nb,), in_specs, out_specs, out_shape, [], (c, *args), exchange, prefetch=1)
    return [tuple(out[4 * i:4 * i + 4]) for i in range(n)], list(out[4 * n:])


def _add4(fs, name):
    n = len(fs)

    def body(*refs):
        for a_ref, o_ref in zip(refs[:n], refs[n:]):
            o_ref[...] = ((a_ref[0].astype(F32) + a_ref[1].astype(F32)) + a_ref[2].astype(F32)) + a_ref[3].astype(F32)

    for f in fs:
        assert (f.shape[1] // 2) % 16 == 0
    return pl.pallas_call(
        body, name=name, grid=(2,),
        in_specs=[pl.BlockSpec((4, f.shape[1] // 2, f.shape[2]), lambda i: (0, i, 0)) for f in fs],
        out_specs=[pl.BlockSpec((f.shape[1] // 2, f.shape[2]), lambda i: (i, 0)) for f in fs],
        out_shape=[jax.ShapeDtypeStruct(f.shape[1:], F32) for f in fs], compiler_params=_params())(*fs)


def _gather_first(wsrc, cpack):
    def body(w_ref, c_ref, gw_ref, gc_ref, send_sems, recv_sems, local_sem, csend, crecv, clocal):
        x, y, c = _pos()
        me = 2 * x + y
        chips = _other_chips(x, y)
        start, forward, finish = _gather_steps(w_ref, gw_ref, send_sems, recv_sems, local_sem)
        start()
        loc = pltpu.make_async_copy(c_ref, gc_ref.at[me], clocal)
        loc.start()

        def conv_copy(k, slot):
            px, py = chips[k]
            return pltpu.make_async_remote_copy(src_ref=c_ref, dst_ref=gc_ref.at[slot], send_sem=csend.at[k],
                                                recv_sem=crecv.at[k], device_id=(px, py, c), device_id_type=MESH)

        for k in range(3):
            conv_copy(k, me).start()
        forward()
        finish()
        for k, (px, py) in enumerate(chips):
            conv_copy(k, 2 * px + py).wait_recv()
        for k in range(3):
            conv_copy(k, me).wait_send()
        loc.wait()

    anyspec = pl.BlockSpec(memory_space=pl.ANY)
    return pl.pallas_call(
        body, name="gather_first",
        in_specs=[anyspec, anyspec], out_specs=[anyspec, anyspec],
        out_shape=[jax.ShapeDtypeStruct((4,) + wsrc.shape, wsrc.dtype), jax.ShapeDtypeStruct((4,) + cpack.shape, cpack.dtype)],
        scratch_shapes=GATHER_SCRATCH + [pltpu.SemaphoreType.DMA((3,)), pltpu.SemaphoreType.DMA((3,)), pltpu.SemaphoreType.DMA],
        compiler_params=_params(has_side_effects=True),
    )(wsrc, cpack)


def _all_devices_exchange(s):
    def make(ins, outs, sems):
        s_ref, o_ref = ins[0], outs[0]
        send_sems, recv_sems, local_sem = sems
        x, y, c = _pos()
        me = 4 * x + 2 * y + c
        loc = pltpu.make_async_copy(s_ref, o_ref.at[me], local_sem)

        def copy(k, slot):
            peer = (x ^ (k >> 2), y ^ ((k >> 1) & 1), c ^ (k & 1))
            return pltpu.make_async_remote_copy(src_ref=s_ref, dst_ref=o_ref.at[slot], send_sem=send_sems.at[k - 1],
                                                recv_sem=recv_sems.at[k - 1], device_id=peer, device_id_type=MESH)

        def start():
            loc.start()
            for k in range(1, 8):
                copy(k, me).start()

        def finish():
            for k in range(1, 8):
                copy(k, 4 * (x ^ (k >> 2)) + 2 * (y ^ ((k >> 1) & 1)) + (c ^ (k & 1))).wait_recv()
            for k in range(1, 8):
                copy(k, me).wait_send()
            loc.wait()

        return start, lambda: None, finish

    return _Exchange([s], [jax.ShapeDtypeStruct((8,) + s.shape, s.dtype)],
                     [pltpu.SemaphoreType.DMA((7,)), pltpu.SemaphoreType.DMA((7,)), pltpu.SemaphoreType.DMA], make)


def _sum_devices(a):
    def body(a_ref, o_ref):
        acc = a_ref[0]
        for d in range(1, 8):
            acc = acc + a_ref[d]
        o_ref[...] = acc

    vm = pl.BlockSpec(memory_space=pltpu.VMEM)
    return pl.pallas_call(body, name="sum_devices", in_specs=[vm], out_specs=vm,
                          out_shape=jax.ShapeDtypeStruct(a.shape[1:], F32), compiler_params=_params())(a)


def _swap_exchange(gs):
    n = len(gs)

    def make(ins, outs, sems):
        x, y, c = _pos()
        cps = []
        for i in range(n):
            half = gs[i].shape[1] // 2
            rows = pl.ds(pl.multiple_of((1 - c) * half, 8), half)
            cps.append(pltpu.make_async_remote_copy(src_ref=ins[i].at[:, rows, :], dst_ref=outs[i], send_sem=sems[0].at[i],
                                                    recv_sem=sems[1].at[i], device_id=(x, y, 1 - c), device_id_type=MESH))

        def start():
            for cp in cps:
                cp.start()

        def finish():
            for cp in cps:
                cp.wait()

        return start, lambda: None, finish

    return _Exchange(gs, [jax.ShapeDtypeStruct((4, g.shape[1] // 2, g.shape[2]), g.dtype) for g in gs],
                     [pltpu.SemaphoreType.DMA((n,)), pltpu.SemaphoreType.DMA((n,))], make)


def _scatter_exchange(ss):
    n = len(ss)

    def make(ins, outs, sems):
        send_sems, recv_sems, local_sems = sems
        x, y, c = _pos()
        me = 2 * x + y
        chips = _other_chips(x, y)
        locs = [pltpu.make_async_copy(ins[i].at[me], outs[i].at[me], local_sems.at[i]) for i in range(n)]

        def copy(i, k, src_slot, dst_slot):
            px, py = chips[k]
            return pltpu.make_async_remote_copy(src_ref=ins[i].at[src_slot], dst_ref=outs[i].at[dst_slot],
                                                send_sem=send_sems.at[3 * i + k], recv_sem=recv_sems.at[3 * i + k],
                                                device_id=(px, py, c), device_id_type=MESH)

        def start():
            for i in range(n):
                locs[i].start()
                for k, (px, py) in enumerate(chips):
                    copy(i, k, 2 * px + py, me).start()

        def finish():
            for i in range(n):
                for k, (px, py) in enumerate(chips):
                    copy(i, k, me, 2 * px + py).wait_recv()
            for i in range(n):
                for k, (px, py) in enumerate(chips):
                    copy(i, k, 2 * px + py, me).wait_send()
                locs[i].wait()

        return start, lambda: None, finish

    return _Exchange(ss, [jax.ShapeDtypeStruct(s.shape, s.dtype) for s in ss],
                     [pltpu.SemaphoreType.DMA((3 * n,)), pltpu.SemaphoreType.DMA((3 * n,)), pltpu.SemaphoreType.DMA((n,))], make)


def _send_exchange(rs):
    n = len(rs)

    def make(ins, outs, sems):
        x, y, c = _pos()
        cps = [pltpu.make_async_remote_copy(src_ref=ins[i], dst_ref=outs[i], send_sem=sems[0].at[i], recv_sem=sems[1].at[i],
                                            device_id=(x, y, 1 - c), device_id_type=MESH) for i in range(n)]

        def start():
            for cp in cps:
                cp.start()

        def finish():
            for cp in cps:
                cp.wait()

        return start, lambda: None, finish

    return _Exchange(rs, [jax.ShapeDtypeStruct(r.shape, r.dtype) for r in rs],
                     [pltpu.SemaphoreType.DMA((n,)), pltpu.SemaphoreType.DMA((n,))], make)


def _run_exchange(ex, name):
    ei, eo = len(ex.args), len(ex.out_shape)

    def body(*refs):
        start, forward, finish = ex.make(refs[:ei], refs[ei:ei + eo], refs[ei + eo:])
        start()
        forward()
        finish()

    anyspec = pl.BlockSpec(memory_space=pl.ANY)
    return pl.pallas_call(body, name=name, in_specs=[anyspec] * ei, out_specs=[anyspec] * eo, out_shape=ex.out_shape,
                          scratch_shapes=ex.scratch, compiler_params=_params(has_side_effects=True))(*ex.args)


def _add_half(gs, rs, c, name):
    n = len(gs)

    def body(c_ref, *refs):
        for g_ref, r_ref, o_ref in zip(refs[:n], refs[n:2 * n], refs[2 * n:]):
            o_ref[...] = (g_ref[...] + r_ref[...]).astype(BF16)

    g_specs, r_specs, out_shape = [], [], []
    for g, r in zip(gs, rs):
        _, H, C = r.shape
        tr = H // 2
        assert tr % 16 == 0 and g.shape == (4, 2 * H, C)
        g_specs.append(pl.BlockSpec((1, tr, C), lambda j, i, c_ref: (j, c_ref[0] * 2 + i, 0)))
        r_specs.append(pl.BlockSpec((1, tr, C), lambda j, i, c_ref: (j, i, 0)))
        out_shape.append(jax.ShapeDtypeStruct((4, H, C), BF16))
    grid_spec = pltpu.PrefetchScalarGridSpec(num_scalar_prefetch=1, grid=(4, 2), in_specs=g_specs + r_specs, out_specs=r_specs)
    return pl.pallas_call(body, name=name, grid_spec=grid_spec, out_shape=out_shape, compiler_params=_params())(c, *gs, *rs)


def _block_diag(w):
    eye = jnp.eye(RNN_BLOCKS, dtype=w.dtype)
    return (eye[:, None, :, None] * w[:, :, None, :]).reshape(D_RNN, D_RNN)


def _diag_blocks(wd):
    d = wd.reshape(RNN_BLOCKS, 64, RNN_BLOCKS, 64)
    return jnp.stack([d[h, :, h, :] for h in range(RNN_BLOCKS)])


def _split_pack(a, first, last):
    out, base = {}, PACK_OFF[first]
    for i in range(first, last):
        s = a[:, PACK_OFF[i] - base:PACK_OFF[i + 1] - base]
        out[BIG_KEYS[i]] = s.reshape(4 * 256, 256) if BIG_KEYS[i] == "w_p_t" else s.reshape(-1, 1024)
    return out


def _layer_grads(x, p, tgt, gw, small, shard=None, core=None):
    row = lambda v: v.reshape(1, -1)
    wa = _block_diag(small["gate_a_w"]).astype(MXU_DTYPE)
    wx = _block_diag(small["gate_x_w"]).astype(MXU_DTYPE)
    sinks = small["attn_sinks"].reshape(1, HEADS)

    dist = shard is not None
    q, kv, xr, gr, xb = _in_proj(x, gw["w_in_t"])
    att, *ga = _attn_fwd(q, kv, sinks, _gather_exchange(shard[PACK_OFF[1]:PACK_OFF[2]]) if dist else None)
    xc, h, rec, *gb = _rnn_fwd(xr, gr, small["rnn_conv_w"], row(small["rnn_conv_b"]), wa, row(small["gate_a_b"]),
                               wx, row(small["gate_x_b"]), row(small["lru_lambda"]),
                               _gather_exchange(shard[PACK_OFF[2]:PACK_OFF[6]]) if dist else None)
    if dist:
        gw = {**gw, **_split_pack(ga[0], 1, 2), **_split_pack(gb[0], 2, 6)}
    g1, b1 = row(small["ln1_g"]), row(small["ln1_b"])
    fcw = small["ffn_conv_w"].reshape(3, NC, FF_CHUNK).transpose(1, 0, 2)
    fcb = small["ffn_conv_b"].reshape(NC, 1, FF_CHUNK)
    z1, h1b = _out_proj(att, rec, x, gw["w_out"], g1, b1)
    gate, ge, vd, act = _ffn_up(h1b, gw["w_up_t"], fcw, fcb)
    dz2, dz2b, dpre, dpp, vec2 = _ffn_down(act, z1, p, tgt, gw["w_down"], gw["w_g"], gw["w_p_t"], g1, b1,
                                           row(small["ln2_g"]), row(small["ln2_b"]), row(small["ple_gate_b"]))
    dup, dfc = _ffn_bwd(dz2b, gate, ge, vd, gw["w_down"], fcw)
    dz1, vec1 = _ffn_dh1(dup, dz2, dpre, z1, gw["w_up_t"], gw["w_g"], g1, b1)
    per_chip = 2 * D_FF // 4 // FF_CHUNK
    big = {
        "w_ffn_up": _weight_grad_cols(
            h1b, dup.reshape(2 * NC, -1, FF_CHUNK), "dw_up", 2 * NC,
            lambda bt: pl.BlockSpec((None, bt, FF_CHUNK), lambda m, k: (m, k, 0)), (4, D, 2 * D_FF // 4),
            pl.BlockSpec((None, D, FF_CHUNK), lambda m, k: (2 * (m % 2) + (m // 2) // per_chip, 0, (m // 2) % per_chip))),
        "w_ffn_down": _weight_grad(act, dz2b, 512, "dw_down").reshape(4, D_FF // 4, D),
        "ple_gate_w": _weight_grad(h1b, dpre, 512, "dw_gate").reshape(4, D // 4, D),
        "ple_proj": _weight_grad_cols(
            p.astype(BF16), dpp, "dw_proj", 4, lambda bt: pl.BlockSpec((bt, D // 4), lambda j, k: (k, j)),
            (4, PLE, D // 4), pl.BlockSpec((None, PLE, D // 4), lambda j, k: (j, 0, 0))),
    }
    reduced = None
    if dist:
        g_ffn = [big[k] for k in FFN_WEIGHTS]
        ex = _swap_exchange(g_ffn)
    datt, drec, dz1b, *got = _out_proj_bwd(dz1, gw["w_out"], ex if dist else None)
    if dist:
        ex = _scatter_exchange(_add_half(g_ffn, got, core, "add_half_ffn"))
    dxr, dgr, dwa, dwx, dvec, *got = _rnn_bwd(drec, gr, h, xc, xr, small["rnn_conv_w"], wa, row(small["gate_a_b"]),
                                              wx, row(small["gate_x_b"]), row(small["lru_lambda"]), ex if dist else None)
    if dist:
        mine = _add4(got, "add_chips_ffn")
        ex = _send_exchange(mine)
    dq, dkv, dsinks, *got = _attn_bwd(q, kv, datt, sinks, ex if dist else None)
    if dist:
        reduced = (mine, got)
        big = {}
    sg = {
        "attn_sinks": dsinks[:, 0],
        "rnn_conv_w": dvec[4:8],
        "rnn_conv_b": dvec[3],
        "gate_a_w": _diag_blocks(dwa),
        "gate_a_b": dvec[0],
        "gate_x_w": _diag_blocks(dwx),
        "gate_x_b": dvec[1],
        "lru_lambda": dvec[2],
        "ln1_g": vec1[0],
        "ln1_b": vec1[1],
        "ffn_conv_w": dfc[:, 0:3].transpose(1, 0, 2).reshape(3, D_FF),
        "ffn_conv_b": dfc[:, 3].reshape(D_FF),
        "ple_gate_b": vec2[3],
        "ln2_g": vec2[1],
        "ln2_b": vec2[2],
    }
    loss = vec2[0, 0:1]
    ex = _all_devices_exchange(_pack_vecs([sg[k] for k in SMALL] + [loss])[0]) if dist else None
    grad_x, du, *small_all = _in_proj_bwd(dq, dkv, dxr, dgr, dz1, gw["w_in_t"], ex)
    big["w_in"] = _weight_grad(xb, du, 512, "dw_in").reshape(D, 4, D_IN // 4).transpose(1, 0, 2)
    big["w_out"] = _weight_grad(jnp.concatenate([att, rec], axis=1), dz1b, 512, "dw_out").reshape(4, D // 4, D)
    return grad_x, big, sg, loss, reduced, small_all


BIG = ("w_in", "w_ffn_up", "w_out", "w_ffn_down", "ple_gate_w", "ple_proj")
BIG_KEYS = ("w_in_t", "w_up_t", "w_out", "w_down", "w_g", "w_p_t")
BIG_T = (True, True, False, False, False, True)
FFN_WEIGHTS = ("w_ffn_up", "w_ffn_down", "ple_gate_w", "ple_proj")
MIX_WEIGHTS = ("w_in", "w_out")
SMALL = ("attn_sinks", "rnn_conv_w", "rnn_conv_b", "gate_a_w", "gate_a_b", "gate_x_w", "gate_x_b", "lru_lambda",
         "ln1_g", "ln1_b", "ffn_conv_w", "ffn_conv_b", "ple_gate_b", "ln2_g", "ln2_b")
SHARDED_SMALL = ("rnn_conv_w", "ffn_conv_w")
WEIGHTS = ("w_in", "attn_sinks", "rnn_conv_w", "rnn_conv_b", "gate_a_w", "gate_a_b", "gate_x_w", "gate_x_b",
           "lru_lambda", "w_out", "ln1_g", "ln1_b", "w_ffn_up", "ffn_conv_w", "ffn_conv_b", "w_ffn_down",
           "ple_gate_w", "ple_gate_b", "ple_proj", "ln2_g", "ln2_b")


def _pack_big(d, first=0, last=6):
    parts = []
    for name, t in zip(BIG[first:last], BIG_T[first:last]):
        a = d[name]
        a = a.T if t else a
        parts.append(a.reshape(-1, 1024))
    return jnp.concatenate(parts, axis=0)


def _pack_vecs(items):
    parts, offs, n = [], [], 0
    for a in items:
        f = a.reshape(-1).astype(F32)
        pad = (-f.shape[0]) % 128
        parts.append(jnp.pad(f, (0, pad)))
        offs.append(n)
        n += (f.shape[0] + pad) // 128
    padr = (-n) % 8
    if padr:
        parts.append(jnp.zeros((padr * 128,), F32))
    return jnp.concatenate(parts).reshape(-1, 128), offs


def _unpack_vecs(a, offs, shapes):
    flat = a.reshape(-1)
    out = []
    for o, s in zip(offs, shapes):
        n = 1
        for d in s:
            n *= d
        out.append(flat[o * 128:o * 128 + n].reshape(s))
    return out


def kernel(x, p, w_in, attn_sinks, rnn_conv_w, rnn_conv_b, gate_a_w, gate_a_b, gate_x_w, gate_x_b, lru_lambda, w_out, ln1_g, ln1_b, w_ffn_up, ffn_conv_w, ffn_conv_b, w_ffn_down, ple_gate_w, ple_gate_b, ple_proj, ln2_g, ln2_b, loss_target, m_w_in, m_attn_sinks, m_rnn_conv_w, m_rnn_conv_b, m_gate_a_w, m_gate_a_b, m_gate_x_w, m_gate_x_b, m_lru_lambda, m_w_out, m_ln1_g, m_ln1_b, m_w_ffn_up, m_ffn_conv_w, m_ffn_conv_b, m_w_ffn_down, m_ple_gate_w, m_ple_gate_b, m_ple_proj, m_ln2_g, m_ln2_b, v_w_in, v_attn_sinks, v_rnn_conv_w, v_rnn_conv_b, v_gate_a_w, v_gate_a_b, v_gate_x_w, v_gate_x_b, v_lru_lambda, v_w_out, v_ln1_g, v_ln1_b, v_w_ffn_up, v_ffn_conv_w, v_ffn_conv_b, v_w_ffn_down, v_ple_gate_w, v_ple_gate_b, v_ple_proj, v_ln2_g, v_ln2_b):
    w = dict(w_in=w_in, attn_sinks=attn_sinks, rnn_conv_w=rnn_conv_w, rnn_conv_b=rnn_conv_b, gate_a_w=gate_a_w,
             gate_a_b=gate_a_b, gate_x_w=gate_x_w, gate_x_b=gate_x_b, lru_lambda=lru_lambda, w_out=w_out, ln1_g=ln1_g,
             ln1_b=ln1_b, w_ffn_up=w_ffn_up, ffn_conv_w=ffn_conv_w, ffn_conv_b=ffn_conv_b, w_ffn_down=w_ffn_down,
             ple_gate_w=ple_gate_w, ple_gate_b=ple_gate_b, ple_proj=ple_proj, ln2_g=ln2_g, ln2_b=ln2_b)
    m = dict(w_in=m_w_in, attn_sinks=m_attn_sinks, rnn_conv_w=m_rnn_conv_w, rnn_conv_b=m_rnn_conv_b, gate_a_w=m_gate_a_w,
             gate_a_b=m_gate_a_b, gate_x_w=m_gate_x_w, gate_x_b=m_gate_x_b, lru_lambda=m_lru_lambda, w_out=m_w_out,
             ln1_g=m_ln1_g, ln1_b=m_ln1_b, w_ffn_up=m_w_ffn_up, ffn_conv_w=m_ffn_conv_w, ffn_conv_b=m_ffn_conv_b,
             w_ffn_down=m_w_ffn_down, ple_gate_w=m_ple_gate_w, ple_gate_b=m_ple_gate_b, ple_proj=m_ple_proj,
             ln2_g=m_ln2_g, ln2_b=m_ln2_b)
    v = dict(w_in=v_w_in, attn_sinks=v_attn_sinks, rnn_conv_w=v_rnn_conv_w, rnn_conv_b=v_rnn_conv_b, gate_a_w=v_gate_a_w,
             gate_a_b=v_gate_a_b, gate_x_w=v_gate_x_w, gate_x_b=v_gate_x_b, lru_lambda=v_lru_lambda, w_out=v_w_out,
             ln1_g=v_ln1_g, ln1_b=v_ln1_b, w_ffn_up=v_w_ffn_up, ffn_conv_w=v_ffn_conv_w, ffn_conv_b=v_ffn_conv_b,
             w_ffn_down=v_w_ffn_down, ple_gate_w=v_ple_gate_w, ple_gate_b=v_ple_gate_b, ple_proj=v_ple_proj,
             ln2_g=v_ln2_g, ln2_b=v_ln2_b)
    w, m, v = ({k: a[0] for k, a in d.items()} for d in (w, m, v))
    chip = 2 * lax.axis_index("x") + lax.axis_index("y")
    core = lax.axis_index("c")

    wpack = _pack_big(w)
    cpack, _ = _pack_vecs([w["rnn_conv_w"], w["ffn_conv_w"]])
    shard = wpack.astype(MXU_DTYPE)
    g_in, gcp = _gather_first(shard[PACK_OFF[0]:PACK_OFF[1]], cpack)
    gw = _split_pack(g_in, 0, 1)
    small = {k: w[k] for k in SMALL}
    small["rnn_conv_w"] = gcp[:, 0:4].reshape(4, 4, 128).transpose(1, 0, 2).reshape(4, 512)
    small["ffn_conv_w"] = gcp[:, 4:22].reshape(4, 3, 768).transpose(1, 0, 2).reshape(3, 3072)

    core1 = core.reshape(1).astype(jnp.int32)
    grad_x, big, sg, loss, ffn_halves, small_all = _layer_grads(x[0], p[0, 0], loss_target[0], gw, small, shard, core1)

    shapes = [sg[k].shape for k in SMALL] + [(1,)]
    _, offs = _pack_vecs([jnp.zeros(s, F32) for s in shapes])
    red = dict(zip(SMALL + ("loss",), _unpack_vecs(_sum_devices(small_all[0]), offs, shapes)))
    red["rnn_conv_w"] = lax.dynamic_slice_in_dim(red["rnn_conv_w"], chip * 128, 128, axis=1)
    red["ffn_conv_w"] = lax.dynamic_slice_in_dim(red["ffn_conv_w"], chip * 768, 768, axis=1)

    def adamw(names, mine, other, name, exchange=None):
        out, got = _adamw_halves([w[k] for k in names], mine, other, [m[k] for k in names], [v[k] for k in names],
                                 core1, name, exchange)
        return dict(zip(names, out)), got

    g_mix = [big[k] for k in MIX_WEIGHTS]
    sib = _run_exchange(_swap_exchange(g_mix), "swap_mix")
    ffn_out, from_chips = adamw(FFN_WEIGHTS, *ffn_halves, "adamw_ffn",
                                _scatter_exchange(_add_half(g_mix, sib, core1, "add_half_mix")))
    mix_mine = _add4(from_chips, "add_chips_mix")
    mix_other = _run_exchange(_send_exchange(mix_mine), "send_mix")
    big_out = {**adamw(MIX_WEIGHTS, mix_mine, mix_other, "adamw_mix")[0], **ffn_out}
    wsm, offs2 = _pack_vecs([w[k] for k in SMALL])
    gsm, _ = _pack_vecs([red[k] for k in SMALL])
    msm, _ = _pack_vecs([m[k] for k in SMALL])
    vsm, _ = _pack_vecs([v[k] for k in SMALL])
    dsm, nmsm, nvsm = _adamw(wsm, gsm, msm, vsm, "adamw_small")
    shapes2 = [w[k].shape for k in SMALL]

    def named(n, smallp):
        d = {k: out[n][None] for k, out in big_out.items()}
        d.update({k: a[None] for k, a in zip(SMALL, _unpack_vecs(smallp, offs2, shapes2))})
        return [d[k] for k in WEIGHTS]

    return (red["loss"].reshape(()), grad_x[None], *named(0, gsm), *named(1, dsm), *named(2, nmsm), *named(3, nvsm))
```

```python
import functools

import jax
import jax.numpy as jnp
from jax import lax
from jax.experimental import pallas as pl
from jax.experimental.pallas import tpu as pltpu

F32 = jnp.float32
BF16 = jnp.bfloat16
MXU_DTYPE = jnp.bfloat16

D = 1024
D_ATT = 512
D_KV = 128
D_RNN = 512
D_IN = 1792
D_FF = 3072
FF_CHUNK = 512
PLE = 256
HEADS = 8
HEAD_DIM = 64
BLK = 128
RNN_BLOCKS = 8
LN_EPS = 1e-5
LRU_C = 8.0
ALPHA = float(2.0 ** 0.25)
SCALE = HEAD_DIM ** -0.5
NEG = -1e30

ADAM_LR = 0.001
ADAM_B1 = 0.9
ADAM_B2 = 0.999
ADAM_EPS = 1e-08
ADAM_WD = 0.01
ADAM_STEP = 10

VMEM_LIMIT_BYTES = 56 * 1024 * 1024
MESH = pl.DeviceIdType.MESH

PACK_ROWS = (448, 1536, 256, 768, 256, 64)
PACK_OFF = tuple(sum(PACK_ROWS[:i]) for i in range(len(PACK_ROWS) + 1))
PACK_TOTAL = PACK_OFF[-1]


def _params(**kw):
    return pltpu.CompilerParams(vmem_limit_bytes=VMEM_LIMIT_BYTES, **kw)


def _mm(a, b):
    return jnp.dot(a.astype(MXU_DTYPE), b.astype(MXU_DTYPE), preferred_element_type=F32)


def _mm_nt(a, b):
    return lax.dot_general(a.astype(MXU_DTYPE), b.astype(MXU_DTYPE), (((1,), (1,)), ((), ())),
                           preferred_element_type=F32)


def _mm_tn(a, b):
    return lax.dot_general(a.astype(MXU_DTYPE), b.astype(MXU_DTYPE), (((0,), (0,)), ((), ())),
                           preferred_element_type=F32)


def _sigmoid(x):
    return 1.0 / (1.0 + jnp.exp(-x))


def _gelu(x):
    c = 0.7978845608028654
    k = 0.044715
    x2 = x * x
    t = jnp.tanh(x * (c + (c * k) * x2))
    h = 0.5 * (1.0 + t)
    return x * h, h * (1.0 + (x * (1.0 - t)) * (c + (3.0 * c * k) * x2))


def _shift_rows(x, s, edge8):
    R = x.shape[0]
    row8 = lax.broadcasted_iota(jnp.int32, (8, x.shape[1]), 0)
    if s > 0:
        rolled = pltpu.roll(x, s, 0)
        first = jnp.where(row8 < s, pltpu.roll(edge8, s, 0), rolled[0:8])
        return jnp.concatenate([first, rolled[8:]], axis=0)
    k = -s
    rolled = pltpu.roll(x, R - k, 0)
    last = jnp.where(row8 >= 8 - k, pltpu.roll(edge8, 8 - k, 0), rolled[R - 8:])
    return jnp.concatenate([rolled[:R - 8], last], axis=0)


def _softplus(x):
    return jnp.maximum(x, 0.0) + jnp.log(1.0 + jnp.exp(-jnp.abs(x)))


def _ln(z, g, b):
    mu = jnp.mean(z, axis=-1, keepdims=True)
    zc = z - mu
    var = jnp.mean(zc * zc, axis=-1, keepdims=True)
    rstd = lax.rsqrt(var + LN_EPS)
    xhat = zc * rstd
    return xhat * g + b, xhat, rstd


def _ln_bwd(dy, xhat, rstd, g):
    dxh = dy * g
    m1 = jnp.mean(dxh, axis=-1, keepdims=True)
    m2 = jnp.mean(dxh * xhat, axis=-1, keepdims=True)
    return rstd * (dxh - m1 - xhat * m2)


def _colsum(x):
    return jnp.sum(x, axis=0, keepdims=True)


def _full(shape):
    nd = len(shape)
    return pl.BlockSpec(shape, lambda *_: (0,) * nd)


def _rows(tm, cols, fn=None):
    if fn is None:
        return pl.BlockSpec((tm, cols), lambda i: (i, 0))
    return pl.BlockSpec((tm, cols), lambda i: (fn(i), 0))


def _heads(tm):
    return pl.BlockSpec((HEADS, tm, HEAD_DIM), lambda i: (0, i, 0))


def _in_proj(x, w_in_t):
    T = x.shape[0]
    tm = 512

    def body(x_ref, w_ref, q_ref, kv_ref, xr_ref, gr_ref, xb_ref):
        xb = x_ref[...].astype(MXU_DTYPE)
        xb_ref[...] = xb.astype(BF16)
        q = _mm_nt(xb, w_ref[0:512, :])
        for h in range(HEADS):
            q_ref[h] = q[:, h * 64:(h + 1) * 64].astype(BF16)
        kv_ref[...] = _mm_nt(xb, w_ref[512:768, :]).astype(BF16)
        xr_ref[...] = _mm_nt(xb, w_ref[768:1280, :])
        gr_ref[...] = _mm_nt(xb, w_ref[1280:1792, :])

    return pl.pallas_call(
        body, name="in_proj", grid=(T // tm,),
        in_specs=[_rows(tm, D), _full((D_IN, D))],
        out_specs=[_heads(tm), _rows(tm, 256), _rows(tm, 512), _rows(tm, 512), _rows(tm, D)],
        out_shape=[jax.ShapeDtypeStruct((HEADS, T, 64), BF16), jax.ShapeDtypeStruct((T, 256), BF16),
                   jax.ShapeDtypeStruct((T, 512), F32), jax.ShapeDtypeStruct((T, 512), F32),
                   jax.ShapeDtypeStruct((T, D), BF16)],
        compiler_params=_params(),
    )(x, w_in_t)


def _attn_band(kv_ref, i):
    cur = pl.multiple_of(i * BLK, BLK)
    prev = pl.multiple_of(jnp.maximum(i - 1, 0) * BLK, BLK)
    band = jnp.concatenate([kv_ref[pl.ds(prev, BLK), :], kv_ref[pl.ds(cur, BLK), :]], axis=0)
    key = lax.broadcasted_iota(jnp.int32, (2 * BLK, 4 * BLK), 0)
    qry = lax.broadcasted_iota(jnp.int32, (2 * BLK, 4 * BLK), 1) & (BLK - 1)
    in_prev = jnp.logical_and(jnp.logical_and(key < BLK, key > qry), i > 0)
    mask = jnp.logical_or(in_prev, jnp.logical_and(key >= BLK, key - BLK <= qry))
    return band, mask, cur, prev


def _attn_scores(band, mask, qs, s_ref, g):
    st = jnp.where(mask, _mm_nt(band[:, g * 64:(g + 1) * 64], qs) * SCALE, NEG)
    lane = lax.broadcasted_iota(jnp.int32, (1, 4 * BLK), 1)
    sv = jnp.where(lane < BLK, s_ref[0, 4 * g],
                   jnp.where(lane < 2 * BLK, s_ref[0, 4 * g + 1], jnp.where(lane < 3 * BLK, s_ref[0, 4 * g + 2], s_ref[0, 4 * g + 3])))
    m = jnp.maximum(jnp.max(st, axis=0, keepdims=True), sv)
    p = jnp.exp(st - m)
    ps = jnp.exp(sv - m)
    return p, ps, jnp.sum(p, axis=0, keepdims=True) + ps


def _pos():
    return lax.axis_index("x"), lax.axis_index("y"), lax.axis_index("c")


def _other_chips(x, y):
    return [(1 - x, y), (x, 1 - y), (1 - x, 1 - y)]


def _gather_steps(w_ref, gw_ref, send_sems, recv_sems, local_sem):
    x, y, c = _pos()
    me = 2 * x + y
    chips = _other_chips(x, y)
    half = w_ref.shape[0] // 2
    mine = pl.ds(pl.multiple_of(c * half, 16), half)
    theirs = pl.ds(pl.multiple_of((1 - c) * half, 16), half)
    loc = pltpu.make_async_copy(w_ref, gw_ref.at[me], local_sem)

    def copy(k, src, dst, to):
        return pltpu.make_async_remote_copy(src_ref=src, dst_ref=dst, send_sem=send_sems.at[k], recv_sem=recv_sems.at[k],
                                            device_id=to, device_id_type=MESH)

    def out(k):
        px, py = chips[k]
        return copy(k, w_ref.at[mine], gw_ref.at[me, mine], (px, py, c))

    def fwd(k, rows):
        px, py = chips[k]
        return copy(3 + k, gw_ref.at[2 * px + py, rows], gw_ref.at[2 * px + py, rows], (x, y, 1 - c))

    def start():
        loc.start()
        for k in range(3):
            out(k).start()

    def forward():
        for k in range(3):
            px, py = chips[k]
            copy(k, w_ref.at[mine], gw_ref.at[2 * px + py, mine], (px, py, c)).wait_recv()
            fwd(k, mine).start()

    def finish():
        for k in range(3):
            fwd(k, theirs).wait_recv()
        for k in range(3):
            out(k).wait_send()
            fwd(k, mine).wait_send()
        loc.wait()

    return start, forward, finish


GATHER_SCRATCH = [pltpu.SemaphoreType.DMA((6,)), pltpu.SemaphoreType.DMA((6,)), pltpu.SemaphoreType.DMA]


class _Exchange:
    def __init__(self, args, out_shape, scratch, make):
        self.args, self.out_shape, self.scratch, self.make = list(args), list(out_shape), list(scratch), make


def _gather_exchange(wsrc):
    return _Exchange([wsrc], [jax.ShapeDtypeStruct((4,) + wsrc.shape, wsrc.dtype)], GATHER_SCRATCH,
                     lambda ins, outs, sems: _gather_steps(ins[0], outs[0], *sems))


def _launch(body, name, grid, in_specs, out_specs, out_shape, scratch, args, exchange=None, prefetch=0):
    def call(fn, fn_name, ins, outs, shapes, scr, operands, effects):
        spec = pltpu.PrefetchScalarGridSpec(num_scalar_prefetch=prefetch, grid=grid, in_specs=ins, out_specs=outs,
                                            scratch_shapes=scr)
        return pl.pallas_call(fn, name=fn_name, grid_spec=spec, out_shape=shapes,
                              compiler_params=_params(has_side_effects=effects))(*operands)

    if exchange is None:
        return call(body, name, list(in_specs), list(out_specs), list(out_shape), list(scratch), args, False)
    n_in, n_out, ei, eo, ns = len(in_specs), len(out_specs), len(exchange.args), len(exchange.out_shape), len(exchange.scratch)
    nsteps = 1
    for g in grid:
        nsteps *= g

    def wrapped(*refs):
        scalars, refs = refs[:prefetch], refs[prefetch:]
        ins, xin = refs[:n_in], refs[n_in:n_in + ei]
        outs, xout = refs[n_in + ei:n_in + ei + n_out], refs[n_in + ei + n_out:n_in + ei + n_out + eo]
        rest = refs[n_in + ei + n_out + eo:]
        own, sems = rest[:len(rest) - ns], rest[len(rest) - ns:]
        start, forward, finish = exchange.make(xin, xout, sems)
        i = pl.program_id(0)
        for d in range(1, len(grid)):
            i = i * grid[d] + pl.program_id(d)
        pl.when(i == 0)(start)
        body(*scalars, *ins, *outs, *own)
        pl.when(i == max(nsteps - 3, 0))(forward)
        pl.when(i == nsteps - 1)(finish)

    anyspec = pl.BlockSpec(memory_space=pl.ANY)
    return call(wrapped, name + "_x", list(in_specs) + [anyspec] * ei, list(out_specs) + [anyspec] * eo,
                list(out_shape) + exchange.out_shape, list(scratch) + exchange.scratch, (*args, *exchange.args), True)


def _attn_fwd(q, kv, sinks, exchange=None):
    T = kv.shape[0]

    def body(q_ref, kv_ref, s_ref, o_ref):
        i = pl.program_id(0)
        band, mask, _, _ = _attn_band(kv_ref, i)
        for g in range(2):
            qs = q_ref[4 * g:4 * g + 4].reshape(4 * BLK, HEAD_DIM)
            p, _, den = _attn_scores(band, mask, qs, s_ref, g)
            ot = _mm_tn(band[:, 128:256], p) / den
            for hh in range(4):
                o = ot[:, hh * BLK:(hh + 1) * BLK].T
                o_ref[:, (4 * g + hh) * 64:(4 * g + hh + 1) * 64] = o[:, g * 64:(g + 1) * 64].astype(BF16)

    return _launch(body, "attn_fwd", (T // BLK,), [_heads(BLK), _full((T, 256)), pl.BlockSpec(memory_space=pltpu.SMEM)],
                   [_rows(BLK, 512)], [jax.ShapeDtypeStruct((T, 512), BF16)], [], (q, kv, sinks), exchange)


def _attn_bwd(q, kv, do, sinks, exchange=None):
    T = kv.shape[0]

    def body(q_ref, kv_ref, do_ref, s_ref, dq_ref, dkv_ref, ds_ref):
        i = pl.program_id(0)
        band, mask, cur, prev = _attn_band(kv_ref, i)

        @pl.when(i == 0)
        def _():
            ds_ref[...] = jnp.zeros_like(ds_ref)

        for g in range(2):
            qs = q_ref[4 * g:4 * g + 4].reshape(4 * BLK, HEAD_DIM)
            dos = do_ref[4 * g:4 * g + 4].reshape(4 * BLK, HEAD_DIM)
            p, ps, den = _attn_scores(band, mask, qs, s_ref, g)
            inv = 1.0 / den
            p = p * inv
            dpt = _mm_nt(band[:, 128 + g * 64:192 + g * 64], dos)
            delta = jnp.sum(p * dpt, axis=0, keepdims=True)
            dst = p * (dpt - delta)
            dsv = -(ps * inv) * delta
            for hh in range(4):
                dsink = jnp.sum(dsv[:, hh * BLK:(hh + 1) * BLK], axis=1, keepdims=True)
                ds_ref[4 * g + hh:4 * g + hh + 1, :] += jnp.broadcast_to(dsink, (1, 128))
            dqt = _mm_tn(band[:, 0:128], dst) * SCALE
            for hh in range(4):
                dqh = dqt[:, hh * BLK:(hh + 1) * BLK].T
                dq_ref[:, (4 * g + hh) * 64:(4 * g + hh + 1) * 64] = dqh[:, g * 64:(g + 1) * 64].astype(BF16)
            dk = _mm(dst, qs) * SCALE
            dv = _mm(p, dos)
            dkv_ref[pl.ds(cur, BLK), g * 64:(g + 1) * 64] = dk[BLK:2 * BLK]
            dkv_ref[pl.ds(cur, BLK), 128 + g * 64:192 + g * 64] = dv[BLK:2 * BLK]
            dkv_ref[pl.ds(prev, BLK), g * 64:(g + 1) * 64] += dk[0:BLK]
            dkv_ref[pl.ds(prev, BLK), 128 + g * 64:192 + g * 64] += dv[0:BLK]

    return _launch(body, "attn_bwd", (T // BLK,),
                   [_heads(BLK), _full((T, 256)), _heads(BLK), pl.BlockSpec(memory_space=pltpu.SMEM)],
                   [_rows(BLK, 512), _full((T, 256)), _full((8, 128))],
                   [jax.ShapeDtypeStruct((T, 512), BF16), jax.ShapeDtypeStruct((T, 256), F32),
                    jax.ShapeDtypeStruct((8, 128), F32)], [], (q, kv, do, sinks), exchange)


def _rows8(tm, cols):
    return lax.broadcasted_iota(jnp.int32, (tm, cols), 0) & 7


def _lru_gates(xc, wa, ba, wx, bx, lam):
    r = _sigmoid(_mm(xc, wa) + ba)
    ii = _sigmoid(_mm(xc, wx) + bx)
    sp = _softplus(-lam)
    la = -LRU_C * r * sp
    a = jnp.exp(la)
    m = jnp.sqrt(-jnp.tanh(la) * (a * a + 1.0))
    return r, ii, sp, a, m


def _rnn_fwd(xr, gr, cw, cb, wa, ba, wx, bx, lam, exchange=None):
    T = xr.shape[0]
    tm = 256
    C = D_RNN

    def body(xr_ref, gr_ref, cw_ref, cb_ref, wa_ref, ba_ref, wx_ref, bx_ref, lam_ref,
             xc_ref, h_ref, rec_ref, ext, a_s, b_s, carry):
        i = pl.program_id(0)

        @pl.when(i == 0)
        def _():
            ext[...] = jnp.zeros((8, C), F32)
            carry[...] = jnp.zeros((8, C), F32)

        xr = xr_ref[...]
        edge = ext[...]
        xc = cb_ref[...] + cw_ref[3:4, :] * xr
        for k in range(3):
            xc = xc + cw_ref[k:k + 1, :] * _shift_rows(xr, 3 - k, edge)
        ext[...] = xr[tm - 8:tm, :]
        xc_ref[...] = xc
        _, ii, _, a, m = _lru_gates(xc, wa_ref[...], ba_ref[...], wx_ref[...], bx_ref[...], lam_ref[...])
        b = m * ii * xc
        r8 = _rows8(tm, C)
        for d in (1, 2, 4):
            ok = r8 >= d
            a_sh = jnp.where(ok, pltpu.roll(a, d, 0), 1.0)
            b_sh = jnp.where(ok, pltpu.roll(b, d, 0), 0.0)
            b = a * b_sh + b
            a = a * a_sh
        a_s[...] = a
        b_s[...] = b

        def step(g, hin):
            s = pl.multiple_of(g * 8, 8)
            hg = a_s[pl.ds(s, 8), :] * hin + b_s[pl.ds(s, 8), :]
            h_ref[pl.ds(s, 8), :] = hg
            return jnp.broadcast_to(hg[7:8, :], (8, C))

        carry[...] = lax.fori_loop(0, tm // 8, step, carry[...])
        ge, _ = _gelu(gr_ref[...])
        rec_ref[...] = (h_ref[...] * ge).astype(BF16)

    vec = _full((1, C))
    in_specs = [_rows(tm, C), _rows(tm, C), _full((4, C)), vec, _full((C, C)), vec, _full((C, C)), vec, vec]
    out_specs = [_rows(tm, C), _rows(tm, C), _rows(tm, C)]
    out_shape = [jax.ShapeDtypeStruct((T, C), F32), jax.ShapeDtypeStruct((T, C), F32), jax.ShapeDtypeStruct((T, C), BF16)]
    scratch = [pltpu.VMEM((8, C), F32), pltpu.VMEM((tm, C), F32), pltpu.VMEM((tm, C), F32), pltpu.VMEM((8, C), F32)]
    return _launch(body, "rnn_fwd", (T // tm,), in_specs, out_specs, out_shape, scratch,
                   (xr, gr, cw, cb, wa, ba, wx, bx, lam), exchange)


def _rnn_bwd(drec, gr, h, xc, xr, cw, wa, ba, wx, bx, lam, exchange=None):
    T = xr.shape[0]
    tm = 256
    C = D_RNN
    nt = T // tm
    t8 = tm // 8

    def body(drec_ref, gr_ref, h_ref, hp_ref, xc_ref, xr_ref, cw_ref, wa_ref, ba_ref, wx_ref, bx_ref,
             lam_ref, dxr_ref, dgr_ref, dwa_ref, dwx_ref, dvec_ref, c_s, g_s, gout, ext, anext, gcarry):
        i = pl.program_id(0)
        j = nt - 1 - i

        @pl.when(i == 0)
        def _():
            dwa_ref[...] = jnp.zeros_like(dwa_ref)
            dwx_ref[...] = jnp.zeros_like(dwx_ref)
            dvec_ref[...] = jnp.zeros_like(dvec_ref)
            anext[...] = jnp.zeros((8, C), F32)
            gcarry[...] = jnp.zeros((8, C), F32)
            ext[...] = jnp.zeros((8, C), F32)

        xc = xc_ref[...]
        lam = lam_ref[...]
        r, ii, sp, a, m = _lru_gates(xc, wa_ref[...], ba_ref[...], wx_ref[...], bx_ref[...], lam)
        ge, dge = _gelu(gr_ref[...])
        drec = drec_ref[...]
        hh = h_ref[...]
        dgr_ref[...] = (drec * hh * dge).astype(BF16)
        dh = drec * ge
        rowi = lax.broadcasted_iota(jnp.int32, (tm, C), 0)
        c = jnp.where(rowi == tm - 1, jnp.broadcast_to(anext[0:1, :], (tm, C)), pltpu.roll(a, tm - 1, 0))
        anext[...] = a[0:8, :]
        r8 = rowi & 7
        gg = dh
        for d in (1, 2, 4):
            ok = r8 < 8 - d
            c_sh = jnp.where(ok, pltpu.roll(c, tm - d, 0), 1.0)
            g_sh = jnp.where(ok, pltpu.roll(gg, tm - d, 0), 0.0)
            gg = c * g_sh + gg
            c = c * c_sh
        c_s[...] = c
        g_s[...] = gg

        def step(k, gin):
            s = pl.multiple_of((t8 - 1 - k) * 8, 8)
            og = c_s[pl.ds(s, 8), :] * gin + g_s[pl.ds(s, 8), :]
            gout[pl.ds(s, 8), :] = og
            return jnp.broadcast_to(og[0:1, :], (8, C))

        gcarry[...] = lax.fori_loop(0, t8, step, gcarry[...])
        G = gout[...]
        hprev_row = jnp.where(j > 0, hp_ref[7:8, :], 0.0)
        hprev = jnp.where(rowi == 0, jnp.broadcast_to(hprev_row, (tm, C)), pltpu.roll(hh, 1, 0))
        da = G * hprev
        dm = G * ii * xc
        di = G * m * xc
        dxc = G * m * ii
        dla = da * a - dm * a * a / m
        dr = dla * (-LRU_C * sp)
        dsp = _colsum(dla * (-LRU_C * r))
        dlam = dsp * (-_sigmoid(-lam))
        dpr = dr * r * (1.0 - r)
        dpi = di * ii * (1.0 - ii)
        dxc = dxc + _mm_nt(dpr, wa_ref[...]) + _mm_nt(dpi, wx_ref[...])
        dwa_ref[...] += _mm_tn(xc, dpr)
        dwx_ref[...] += _mm_tn(xc, dpi)
        dvec_ref[0:1, :] += _colsum(dpr)
        dvec_ref[1:2, :] += _colsum(dpi)
        dvec_ref[2:3, :] += dlam
        dvec_ref[3:4, :] += _colsum(dxc)
        edge = ext[...]
        xr = xr_ref[...]
        dxr = cw_ref[3:4, :] * dxc
        dvec_ref[7:8, :] += _colsum(dxc * xr)
        for k in range(3):
            up = _shift_rows(dxc, k - 3, edge)
            dxr = dxr + cw_ref[k:k + 1, :] * up
            dvec_ref[4 + k:5 + k, :] += _colsum(up * xr)
        ext[...] = dxc[0:8, :]
        dxr_ref[...] = dxr.astype(BF16)

    rev = lambda i: nt - 1 - i
    prev8 = lambda i: jnp.maximum((nt - 1 - i) * t8 - 1, 0)
    vec = _full((1, C))
    return _launch(
        body, "rnn_bwd", (nt,),
        [_rows(tm, C, rev), _rows(tm, C, rev), _rows(tm, C, rev), _rows(8, C, prev8), _rows(tm, C, rev),
         _rows(tm, C, rev), _full((4, C)), _full((C, C)), vec, _full((C, C)), vec, vec],
        [_rows(tm, C, rev), _rows(tm, C, rev), _full((C, C)), _full((C, C)), _full((8, C))],
        [jax.ShapeDtypeStruct((T, C), BF16), jax.ShapeDtypeStruct((T, C), BF16),
         jax.ShapeDtypeStruct((C, C), F32), jax.ShapeDtypeStruct((C, C), F32), jax.ShapeDtypeStruct((8, C), F32)],
        [pltpu.VMEM((tm, C), F32), pltpu.VMEM((tm, C), F32), pltpu.VMEM((tm, C), F32),
         pltpu.VMEM((8, C), F32), pltpu.VMEM((8, C), F32), pltpu.VMEM((8, C), F32)],
        (drec, gr, h, h, xc, xr, cw, wa, ba, wx, bx, lam), exchange)


def _out_proj(att, rec, x, w_out, g1, b1):
    T = x.shape[0]
    tm = 512

    def body(att_ref, rec_ref, x_ref, w_ref, g1_ref, b1_ref, z_ref, h_ref):
        mix = _mm(att_ref[...], w_ref[0:512, :]) + _mm(rec_ref[...], w_ref[512:1024, :])
        z1 = ALPHA * x_ref[...] + mix
        z_ref[...] = z1
        h1, _, _ = _ln(z1, g1_ref[...], b1_ref[...])
        h_ref[...] = h1.astype(MXU_DTYPE).astype(BF16)

    return pl.pallas_call(
        body, name="out_proj", grid=(T // tm,),
        in_specs=[_rows(tm, 512), _rows(tm, 512), _rows(tm, D), _full((D, D)), _full((1, D)), _full((1, D))],
        out_specs=[_rows(tm, D), _rows(tm, D)],
        out_shape=[jax.ShapeDtypeStruct((T, D), F32), jax.ShapeDtypeStruct((T, D), BF16)],
        compiler_params=_params(),
    )(att, rec, x, w_out, g1, b1)


NC = D_FF // FF_CHUNK


def _ffn_up(h1b, w_up_t, fcw, fcb, exchange=None):
    T = h1b.shape[0]
    tm = 512
    CW = FF_CHUNK

    def body(h_ref, wg_ref, wv_ref, fcw_ref, fcb_ref, gate_ref, ge_ref, vd_ref, act_ref, before):
        i = pl.program_id(1)

        @pl.when(i == 0)
        def _():
            before[...] = jnp.zeros((8, CW), F32)

        hb = h_ref[...]
        gate = _mm_nt(hb, wg_ref[...])
        val = _mm_nt(hb, wv_ref[...])
        gate_ref[...] = gate.astype(BF16)
        edge = before[...]
        gc = (fcb_ref[...] + fcw_ref[0:1, :] * _shift_rows(gate, 2, edge) + fcw_ref[1:2, :] * _shift_rows(gate, 1, edge)
              + fcw_ref[2:3, :] * gate)
        before[...] = gate[tm - 8:tm, :]
        ge, dge = _gelu(gc)
        ge_ref[...] = ge.astype(BF16)
        vd_ref[...] = (val * dge).astype(BF16)
        act_ref[...] = (ge * val).astype(BF16)

    chunk = pl.BlockSpec((None, tm, CW), lambda c, i: (c, i, 0))
    return _launch(
        body, "ffn_up", (NC, T // tm),
        [pl.BlockSpec((tm, D), lambda c, i: (i, 0)), pl.BlockSpec((CW, D), lambda c, i: (c, 0)),
         pl.BlockSpec((CW, D), lambda c, i: (NC + c, 0)), pl.BlockSpec((None, 3, CW), lambda c, i: (c, 0, 0)),
         pl.BlockSpec((None, 1, CW), lambda c, i: (c, 0, 0))],
        [chunk] * 4, [jax.ShapeDtypeStruct((NC, T, CW), BF16)] * 4, [pltpu.VMEM((8, CW), F32)],
        (h1b, w_up_t, w_up_t, fcw, fcb), exchange)


def _ffn_down(act, z1, p, tgt, w_down, w_g, w_p_t, g1, b1, g2, b2, bg):
    T = z1.shape[0]
    tm = 256

    def body(act_ref, z_ref, p_ref, t_ref, wdn_hbm, wg_hbm, wp_hbm, g1_ref, b1_ref, g2_ref, b2_ref, bg_ref,
             dz2_ref, dz2b_ref, dpre_ref, dpp_ref, vec_ref, wdn, wg, wp):
        @pl.when(pl.program_id(0) == 0)
        def _():
            pltpu.sync_copy(wdn_hbm, wdn)
            pltpu.sync_copy(wg_hbm, wg)
            pltpu.sync_copy(wp_hbm, wp)
            vec_ref[...] = jnp.zeros_like(vec_ref)

        g2v = g2_ref[...]
        h1, _, _ = _ln(z_ref[...], g1_ref[...], b1_ref[...])
        h1b = h1.astype(MXU_DTYPE)
        ffn = _mm(act_ref[0], wdn[0:FF_CHUNK, :])
        for c in range(1, NC):
            ffn = ffn + _mm(act_ref[c], wdn[c * FF_CHUNK:(c + 1) * FF_CHUNK, :])
        sg = _sigmoid(_mm(h1b, wg[...]) + bg_ref[...])
        pp = _mm_nt(p_ref[...], wp[...])
        z2 = ALPHA * h1 + ffn + sg * pp
        y, xh2, rstd2 = _ln(z2, g2v, b2_ref[...])
        diff = y - t_ref[...]
        dy = diff * (1.0 / D)
        dz2 = _ln_bwd(dy, xh2, rstd2, g2v)
        dpre = dz2 * pp * sg * (1.0 - sg)
        dz2_ref[...] = dz2
        dz2b_ref[...] = dz2.astype(BF16)
        dpre_ref[...] = dpre.astype(BF16)
        dpp_ref[...] = (dz2 * sg).astype(BF16)
        loss = 0.5 * jnp.sum(jnp.sum(diff * diff, axis=1, keepdims=True), axis=0, keepdims=True) * (1.0 / D)
        vec_ref[0:1, :] += jnp.broadcast_to(loss, (1, D))
        vec_ref[1:2, :] += _colsum(dy * xh2)
        vec_ref[2:3, :] += _colsum(dy)
        vec_ref[3:4, :] += _colsum(dpre)

    anyspec = pl.BlockSpec(memory_space=pl.ANY)
    vec = _full((1, D))
    return pl.pallas_call(
        body, name="ffn_down", grid=(T // tm,),
        in_specs=[pl.BlockSpec((NC, tm, FF_CHUNK), lambda i: (0, i, 0)), _rows(tm, D), _rows(tm, PLE), _rows(tm, D),
                  anyspec, anyspec, anyspec] + [vec] * 5,
        out_specs=[_rows(tm, D)] * 4 + [_full((8, D))],
        out_shape=[jax.ShapeDtypeStruct((T, D), F32)] + [jax.ShapeDtypeStruct((T, D), BF16)] * 3
                  + [jax.ShapeDtypeStruct((8, D), F32)],
        scratch_shapes=[pltpu.VMEM((D_FF, D), MXU_DTYPE), pltpu.VMEM((D, D), MXU_DTYPE), pltpu.VMEM((D, PLE), MXU_DTYPE)],
        compiler_params=_params(),
    )(act, z1, p, tgt, w_down, w_g, w_p_t, g1, b1, g2, b2, bg)


def _ffn_bwd(dz2b, gate, ge, vd, w_down, fcw):
    T = dz2b.shape[0]
    tm = 512
    CW = FF_CHUNK
    nt = T // tm

    def body(dz_ref, wdn_ref, gate_ref, ge_ref, vd_ref, fcw_ref, dup_ref, dfc_ref, after):
        i = pl.program_id(1)

        @pl.when(i == 0)
        def _():
            after[...] = jnp.zeros((8, CW), F32)
            dfc_ref[...] = jnp.zeros_like(dfc_ref)

        gate = gate_ref[...].astype(F32)
        dact = _mm_nt(dz_ref[...], wdn_ref[...])
        dgc = dact * vd_ref[...].astype(F32)
        edge = after[...]
        dgc1 = _shift_rows(dgc, -1, edge)
        dgc2 = _shift_rows(dgc, -2, edge)
        after[...] = dgc[0:8, :]
        dup_ref[0] = (fcw_ref[2:3, :] * dgc + fcw_ref[1:2, :] * dgc1 + fcw_ref[0:1, :] * dgc2).astype(BF16)
        dup_ref[1] = (dact * ge_ref[...].astype(F32)).astype(BF16)
        dfc_ref[0:1, :] += _colsum(dgc2 * gate)
        dfc_ref[1:2, :] += _colsum(dgc1 * gate)
        dfc_ref[2:3, :] += _colsum(dgc * gate)
        dfc_ref[3:4, :] += _colsum(dgc)

    rev = lambda c, i: (c, nt - 1 - i, 0)
    chunk = pl.BlockSpec((None, tm, CW), rev)
    return pl.pallas_call(
        body, name="ffn_bwd", grid=(NC, nt),
        in_specs=[pl.BlockSpec((tm, D), lambda c, i: (nt - 1 - i, 0)), pl.BlockSpec((CW, D), lambda c, i: (c, 0)),
                  chunk, chunk, chunk, pl.BlockSpec((None, 3, CW), lambda c, i: (c, 0, 0))],
        out_specs=[pl.BlockSpec((None, 2, tm, CW), lambda c, i: (c, 0, nt - 1 - i, 0)),
                   pl.BlockSpec((None, 8, CW), lambda c, i: (c, 0, 0))],
        out_shape=[jax.ShapeDtypeStruct((NC, 2, T, CW), BF16), jax.ShapeDtypeStruct((NC, 8, CW), F32)],
        scratch_shapes=[pltpu.VMEM((8, CW), F32)],
        compiler_params=_params(),
    )(dz2b, w_down, gate, ge, vd, fcw)


def _ffn_dh1(dup, dz2, dpre, z1, w_up_t, w_g, g1, b1):
    T = z1.shape[0]
    tm = 256

    def body(dup_ref, dz2_ref, dpre_ref, z_ref, wup_hbm, wg_hbm, g1_ref, b1_ref, dz1_ref, vec_ref, wup, wg):
        @pl.when(pl.program_id(0) == 0)
        def _():
            pltpu.sync_copy(wup_hbm, wup)
            pltpu.sync_copy(wg_hbm, wg)
            vec_ref[...] = jnp.zeros_like(vec_ref)

        g1v = g1_ref[...]
        _, xh1, rstd1 = _ln(z_ref[...], g1v, b1_ref[...])
        dh1 = ALPHA * dz2_ref[...] + _mm_nt(dpre_ref[...], wg[...])
        for c in range(NC):
            for s in range(2):
                r0 = s * D_FF + c * FF_CHUNK
                dh1 = dh1 + _mm(dup_ref[c, s], wup[r0:r0 + FF_CHUNK, :])
        dz1_ref[...] = _ln_bwd(dh1, xh1, rstd1, g1v)
        vec_ref[0:1, :] += _colsum(dh1 * xh1)
        vec_ref[1:2, :] += _colsum(dh1)

    anyspec = pl.BlockSpec(memory_space=pl.ANY)
    vec = _full((1, D))
    return pl.pallas_call(
        body, name="ffn_dh1", grid=(T // tm,),
        in_specs=[pl.BlockSpec((NC, 2, tm, FF_CHUNK), lambda i: (0, 0, i, 0)), _rows(tm, D), _rows(tm, D), _rows(tm, D),
                  anyspec, anyspec, vec, vec],
        out_specs=[_rows(tm, D), _full((8, D))],
        out_shape=[jax.ShapeDtypeStruct((T, D), F32), jax.ShapeDtypeStruct((8, D), F32)],
        scratch_shapes=[pltpu.VMEM((2 * D_FF, D), MXU_DTYPE), pltpu.VMEM((D, D), MXU_DTYPE)],
        compiler_params=_params(),
    )(dup, dz2, dpre, z1, w_up_t, w_g, g1, b1)


def _out_proj_bwd(dz1, w_out, exchange=None):
    T = dz1.shape[0]
    tm = 512

    def body(dz_ref, w_ref, datt_ref, drec_ref, dzb_ref):
        dzb = dz_ref[...].astype(MXU_DTYPE)
        dzb_ref[...] = dzb.astype(BF16)
        datt = _mm_nt(dzb, w_ref[0:512, :])
        for h in range(HEADS):
            datt_ref[h] = datt[:, h * 64:(h + 1) * 64].astype(BF16)
        drec_ref[...] = _mm_nt(dzb, w_ref[512:1024, :])

    return _launch(body, "out_proj_bwd", (T // tm,), [_rows(tm, D), _full((D, D))],
                   [_heads(tm), _rows(tm, 512), _rows(tm, D)],
                   [jax.ShapeDtypeStruct((HEADS, T, 64), BF16), jax.ShapeDtypeStruct((T, 512), F32),
                    jax.ShapeDtypeStruct((T, D), BF16)], [], (dz1, w_out), exchange)


def _in_proj_bwd(dq, dkv, dxr, dgr, dz1, w_in_t, exchange=None):
    T = dz1.shape[0]
    tm = 512
    W = D_IN // 4

    def body(dq_ref, dkv_ref, dxr_ref, dgr_ref, dz_ref, w_ref, dx_ref, du_ref):
        dkv = dkv_ref[...]
        dx_ref[...] = (ALPHA * dz_ref[...] + _mm(dq_ref[...], w_ref[0:512, :]) + _mm(dkv, w_ref[512:768, :])
                       + _mm(dxr_ref[...], w_ref[768:1280, :]) + _mm(dgr_ref[...], w_ref[1280:1792, :]))
        dq, dxr, dgr = dq_ref[...].astype(F32), dxr_ref[...].astype(F32), dgr_ref[...].astype(F32)
        du_ref[0] = dq[:, 0:W].astype(BF16)
        du_ref[1, :, 0:64] = dq[:, W:512].astype(BF16)
        du_ref[1, :, 64:320] = dkv.astype(BF16)
        du_ref[1, :, 320:W] = dxr[:, 0:128].astype(BF16)
        du_ref[2, :, 0:384] = dxr[:, 128:512].astype(BF16)
        du_ref[2, :, 384:W] = dgr[:, 0:64].astype(BF16)
        du_ref[3] = dgr[:, 64:512].astype(BF16)

    return _launch(body, "in_proj_bwd", (T // tm,),
                   [_rows(tm, 512), _rows(tm, 256), _rows(tm, 512), _rows(tm, 512), _rows(tm, D), _full((D_IN, D))],
                   [_rows(tm, D), pl.BlockSpec((4, tm, W), lambda i: (0, i, 0))],
                   [jax.ShapeDtypeStruct((T, D), F32), jax.ShapeDtypeStruct((4, T, W), BF16)], [],
                   (dq, dkv, dxr, dgr, dz1, w_in_t), exchange)


def _accumulate_tn(a_ref, b_ref, o_ref):
    @pl.when(pl.program_id(1) == 0)
    def _():
        o_ref[...] = jnp.zeros_like(o_ref)

    o_ref[...] += _mm_tn(a_ref[...], b_ref[...])


def _weight_grad_cols(a, b, name, n_blocks, b_spec, out_shape, out_spec):
    T, M = a.shape
    bt = min(2048, T)
    return pl.pallas_call(
        functools.partial(_accumulate_tn), name=name, grid=(n_blocks, T // bt),
        in_specs=[pl.BlockSpec((bt, M), lambda m, k: (k, 0)), b_spec(bt)], out_specs=out_spec,
        out_shape=jax.ShapeDtypeStruct(out_shape, F32), compiler_params=_params())(a, b)


def _weight_grad(a, b, bm, name):
    bt = min(2048, b.shape[0])
    if a.ndim == 3:
        assert a.shape[2] == bm
        T, M = a.shape[1], a.shape[0] * bm
        a_spec = pl.BlockSpec((None, bt, bm), lambda m, k: (m, k, 0))
    else:
        T, M = a.shape
        a_spec = pl.BlockSpec((bt, bm), lambda m, k: (k, m))
    N = b.shape[1]
    nk = T // bt

    return pl.pallas_call(
        functools.partial(_accumulate_tn), name=name, grid=(M // bm, nk),
        in_specs=[a_spec, pl.BlockSpec((bt, N), lambda m, k: (k, 0))],
        out_specs=pl.BlockSpec((bm, N), lambda m, k: (m, 0)),
        out_shape=jax.ShapeDtypeStruct((M, N), F32),
        compiler_params=_params(),
    )(a, b)


def _adamw(w, g, m, v, name):
    R, C = w.shape
    tr = R // 8 if R % 64 == 0 else R
    c1 = 1.0 / (1.0 - ADAM_B1 ** ADAM_STEP)
    c2 = 1.0 / (1.0 - ADAM_B2 ** ADAM_STEP)

    def body(w_ref, g_ref, m_ref, v_ref, d_ref, nm_ref, nv_ref):
        g = g_ref[...]
        nm = ADAM_B1 * m_ref[...] + (1.0 - ADAM_B1) * g
        nv = ADAM_B2 * v_ref[...] + (1.0 - ADAM_B2) * g * g
        nm_ref[...] = nm
        nv_ref[...] = nv
        d_ref[...] = -ADAM_LR * ((nm * c1) / (jnp.sqrt(nv * c2) + ADAM_EPS) + ADAM_WD * w_ref[...])

    spec = pl.BlockSpec((tr, C), lambda i: (i, 0))
    return pl.pallas_call(
        body, name=name, grid=(R // tr,),
        in_specs=[spec] * 4, out_specs=[spec] * 3,
        out_shape=[jax.ShapeDtypeStruct((R, C), F32)] * 3,
        compiler_params=_params(),
    )(w, g, m, v)


def _adamw_halves(ws, mines, sibs, ms, vs, c, name, exchange=None):
    n, nb = len(ws), 4
    c1 = 1.0 / (1.0 - ADAM_B1 ** ADAM_STEP)
    c2 = 1.0 / (1.0 - ADAM_B2 ** ADAM_STEP)

    def body(c_ref, *refs):
        own = (pl.program_id(0) // nb) == c_ref[0]
        for i in range(n):
            w_ref, a_ref, b_ref, m_ref, v_ref = refs[5 * i:5 * i + 5]
            g_ref, d_ref, nm_ref, nv_ref = refs[5 * n + 4 * i:5 * n + 4 * i + 4]
            g = jnp.where(own, a_ref[...], b_ref[...])
            nm = ADAM_B1 * m_ref[...] + (1.0 - ADAM_B1) * g
            nv = ADAM_B2 * v_ref[...] + (1.0 - ADAM_B2) * g * g
            g_ref[...] = g
            nm_ref[...] = nm
            nv_ref[...] = nv
            d_ref[...] = -ADAM_LR * ((nm * c1) / (jnp.sqrt(nv * c2) + ADAM_EPS) + ADAM_WD * w_ref[...])

    in_specs, out_specs, out_shape, args = [], [], [], []
    for w, a, b, m, v in zip(ws, mines, sibs, ms, vs):
        R, C = w.shape
        tr = R // (2 * nb)
        assert tr % 8 == 0 and a.shape == (R // 2, C)
        full = pl.BlockSpec((tr, C), lambda i, c_ref: (i, 0))
        half = pl.BlockSpec((tr, C), lambda i, c_ref: (i % nb, 0))
        in_specs += [full, half, half, full, full]
        out_specs += [full] * 4
        out_shape += [jax.ShapeDtypeStruct((R, C), F32)] * 4
        args += [w, a, b, m, v]
    out = _launch(body, name, (2 * nb,), in_specs, out_specs, out_shape, [], (c, *args), exchange, prefetch=1)
    return [tuple(out[4 * i:4 * i + 4]) for i in range(n)], list(out[4 * n:])


def _add4(fs, name):
    n = len(fs)

    def body(*refs):
        for a_ref, o_ref in zip(refs[:n], refs[n:]):
            o_ref[...] = ((a_ref[0].astype(F32) + a_ref[1].astype(F32)) + a_ref[2].astype(F32)) + a_ref[3].astype(F32)

    for f in fs:
        assert (f.shape[1] // 2) % 16 == 0
    return pl.pallas_call(
        body, name=name, grid=(2,),
        in_specs=[pl.BlockSpec((4, f.shape[1] // 2, f.shape[2]), lambda i: (0, i, 0)) for f in fs],
        out_specs=[pl.BlockSpec((f.shape[1] // 2, f.shape[2]), lambda i: (i, 0)) for f in fs],
        out_shape=[jax.ShapeDtypeStruct(f.shape[1:], F32) for f in fs], compiler_params=_params())(*fs)


def _gather_first(wsrc, cpack):
    def body(w_ref, c_ref, gw_ref, gc_ref, send_sems, recv_sems, local_sem, csend, crecv, clocal):
        x, y, c = _pos()
        me = 2 * x + y
        chips = _other_chips(x, y)
        start, forward, finish = _gather_steps(w_ref, gw_ref, send_sems, recv_sems, local_sem)
        start()
        loc = pltpu.make_async_copy(c_ref, gc_ref.at[me], clocal)
        loc.start()

        def conv_copy(k, slot):
            px, py = chips[k]
            return pltpu.make_async_remote_copy(src_ref=c_ref, dst_ref=gc_ref.at[slot], send_sem=csend.at[k],
                                                recv_sem=crecv.at[k], device_id=(px, py, c), device_id_type=MESH)

        for k in range(3):
            conv_copy(k, me).start()
        forward()
        finish()
        for k, (px, py) in enumerate(chips):
            conv_copy(k, 2 * px + py).wait_recv()
        for k in range(3):
            conv_copy(k, me).wait_send()
        loc.wait()

    anyspec = pl.BlockSpec(memory_space=pl.ANY)
    return pl.pallas_call(
        body, name="gather_first",
        in_specs=[anyspec, anyspec], out_specs=[anyspec, anyspec],
        out_shape=[jax.ShapeDtypeStruct((4,) + wsrc.shape, wsrc.dtype), jax.ShapeDtypeStruct((4,) + cpack.shape, cpack.dtype)],
        scratch_shapes=GATHER_SCRATCH + [pltpu.SemaphoreType.DMA((3,)), pltpu.SemaphoreType.DMA((3,)), pltpu.SemaphoreType.DMA],
        compiler_params=_params(has_side_effects=True),
    )(wsrc, cpack)


def _all_devices_exchange(s):
    def make(ins, outs, sems):
        s_ref, o_ref = ins[0], outs[0]
        send_sems, recv_sems, local_sem = sems
        x, y, c = _pos()
        me = 4 * x + 2 * y + c
        loc = pltpu.make_async_copy(s_ref, o_ref.at[me], local_sem)

        def copy(k, slot):
            peer = (x ^ (k >> 2), y ^ ((k >> 1) & 1), c ^ (k & 1))
            return pltpu.make_async_remote_copy(src_ref=s_ref, dst_ref=o_ref.at[slot], send_sem=send_sems.at[k - 1],
                                                recv_sem=recv_sems.at[k - 1], device_id=peer, device_id_type=MESH)

        def start():
            loc.start()
            for k in range(1, 8):
                copy(k, me).start()

        def finish():
            for k in range(1, 8):
                copy(k, 4 * (x ^ (k >> 2)) + 2 * (y ^ ((k >> 1) & 1)) + (c ^ (k & 1))).wait_recv()
            for k in range(1, 8):
                copy(k, me).wait_send()
            loc.wait()

        return start, lambda: None, finish

    return _Exchange([s], [jax.ShapeDtypeStruct((8,) + s.shape, s.dtype)],
                     [pltpu.SemaphoreType.DMA((7,)), pltpu.SemaphoreType.DMA((7,)), pltpu.SemaphoreType.DMA], make)


def _sum_devices(a):
    def body(a_ref, o_ref):
        acc = a_ref[0]
        for d in range(1, 8):
            acc = acc + a_ref[d]
        o_ref[...] = acc

    vm = pl.BlockSpec(memory_space=pltpu.VMEM)
    return pl.pallas_call(body, name="sum_devices", in_specs=[vm], out_specs=vm,
                          out_shape=jax.ShapeDtypeStruct(a.shape[1:], F32), compiler_params=_params())(a)


def _swap_exchange(gs):
    n = len(gs)

    def make(ins, outs, sems):
        x, y, c = _pos()
        cps = []
        for i in range(n):
            half = gs[i].shape[1] // 2
            rows = pl.ds(pl.multiple_of((1 - c) * half, 8), half)
            cps.append(pltpu.make_async_remote_copy(src_ref=ins[i].at[:, rows, :], dst_ref=outs[i], send_sem=sems[0].at[i],
                                                    recv_sem=sems[1].at[i], device_id=(x, y, 1 - c), device_id_type=MESH))

        def start():
            for cp in cps:
                cp.start()

        def finish():
            for cp in cps:
                cp.wait()

        return start, lambda: None, finish

    return _Exchange(gs, [jax.ShapeDtypeStruct((4, g.shape[1] // 2, g.shape[2]), g.dtype) for g in gs],
                     [pltpu.SemaphoreType.DMA((n,)), pltpu.SemaphoreType.DMA((n,))], make)


def _scatter_exchange(ss):
    n = len(ss)

    def make(ins, outs, sems):
        send_sems, recv_sems, local_sems = sems
        x, y, c = _pos()
        me = 2 * x + y
        chips = _other_chips(x, y)
        locs = [pltpu.make_async_copy(ins[i].at[me], outs[i].at[me], local_sems.at[i]) for i in range(n)]

        def copy(i, k, src_slot, dst_slot):
            px, py = chips[k]
            return pltpu.make_async_remote_copy(src_ref=ins[i].at[src_slot], dst_ref=outs[i].at[dst_slot],
                                                send_sem=send_sems.at[3 * i + k], recv_sem=recv_sems.at[3 * i + k],
                                                device_id=(px, py, c), device_id_type=MESH)

        def start():
            for i in range(n):
                locs[i].start()
                for k, (px, py) in enumerate(chips):
                    copy(i, k, 2 * px + py, me).start()

        def finish():
            for i in range(n):
                for k, (px, py) in enumerate(chips):
                    copy(i, k, me, 2 * px + py).wait_recv()
            for i in range(n):
                for k, (px, py) in enumerate(chips):
                    copy(i, k, 2 * px + py, me).wait_send()
                locs[i].wait()

        return start, lambda: None, finish

    return _Exchange(ss, [jax.ShapeDtypeStruct(s.shape, s.dtype) for s in ss],
                     [pltpu.SemaphoreType.DMA((3 * n,)), pltpu.SemaphoreType.DMA((3 * n,)), pltpu.SemaphoreType.DMA((n,))], make)


def _send_exchange(rs):
    n = len(rs)

    def make(ins, outs, sems):
        x, y, c = _pos()
        cps = [pltpu.make_async_remote_copy(src_ref=ins[i], dst_ref=outs[i], send_sem=sems[0].at[i], recv_sem=sems[1].at[i],
                                            device_id=(x, y, 1 - c), device_id_type=MESH) for i in range(n)]

        def start():
            for cp in cps:
                cp.start()

        def finish():
            for cp in cps:
                cp.wait()

        return start, lambda: None, finish

    return _Exchange(rs, [jax.ShapeDtypeStruct(r.shape, r.dtype) for r in rs],
                     [pltpu.SemaphoreType.DMA((n,)), pltpu.SemaphoreType.DMA((n,))], make)


def _run_exchange(ex, name):
    ei, eo = len(ex.args), len(ex.out_shape)

    def body(*refs):
        start, forward, finish = ex.make(refs[:ei], refs[ei:ei + eo], refs[ei + eo:])
        start()
        forward()
        finish()

    anyspec = pl.BlockSpec(memory_space=pl.ANY)
    return pl.pallas_call(body, name=name, in_specs=[anyspec] * ei, out_specs=[anyspec] * eo, out_shape=ex.out_shape,
                          scratch_shapes=ex.scratch, compiler_params=_params(has_side_effects=True))(*ex.args)


def _add_half(gs, rs, c, name):
    n = len(gs)

    def body(c_ref, *refs):
        for g_ref, r_ref, o_ref in zip(refs[:n], refs[n:2 * n], refs[2 * n:]):
            o_ref[...] = (g_ref[...] + r_ref[...]).astype(BF16)

    g_specs, r_specs, out_shape = [], [], []
    for g, r in zip(gs, rs):
        _, H, C = r.shape
        tr = H // 2
        assert tr % 16 == 0 and g.shape == (4, 2 * H, C)
        g_specs.append(pl.BlockSpec((1, tr, C), lambda j, i, c_ref: (j, c_ref[0] * 2 + i, 0)))
        r_specs.append(pl.BlockSpec((1, tr, C), lambda j, i, c_ref: (j, i, 0)))
        out_shape.append(jax.ShapeDtypeStruct((4, H, C), BF16))
    grid_spec = pltpu.PrefetchScalarGridSpec(num_scalar_prefetch=1, grid=(4, 2), in_specs=g_specs + r_specs, out_specs=r_specs)
    return pl.pallas_call(body, name=name, grid_spec=grid_spec, out_shape=out_shape, compiler_params=_params())(c, *gs, *rs)


def _block_diag(w):
    eye = jnp.eye(RNN_BLOCKS, dtype=w.dtype)
    return (eye[:, None, :, None] * w[:, :, None, :]).reshape(D_RNN, D_RNN)


def _diag_blocks(wd):
    d = wd.reshape(RNN_BLOCKS, 64, RNN_BLOCKS, 64)
    return jnp.stack([d[h, :, h, :] for h in range(RNN_BLOCKS)])


def _split_pack(a, first, last):
    out, base = {}, PACK_OFF[first]
    for i in range(first, last):
        s = a[:, PACK_OFF[i] - base:PACK_OFF[i + 1] - base]
        out[BIG_KEYS[i]] = s.reshape(4 * 256, 256) if BIG_KEYS[i] == "w_p_t" else s.reshape(-1, 1024)
    return out


def _layer_grads(x, p, tgt, gw, small, shard=None, core=None):
    row = lambda v: v.reshape(1, -1)
    wa = _block_diag(small["gate_a_w"]).astype(MXU_DTYPE)
    wx = _block_diag(small["gate_x_w"]).astype(MXU_DTYPE)
    sinks = small["attn_sinks"].reshape(1, HEADS)

    dist = shard is not None
    q, kv, xr, gr, xb = _in_proj(x, gw["w_in_t"])
    cut = PACK_OFF[1] + PACK_ROWS[1] // 2
    att, *ga = _attn_fwd(q, kv, sinks, _gather_exchange(shard[PACK_OFF[1]:cut]) if dist else None)
    xc, h, rec, *gb = _rnn_fwd(xr, gr, small["rnn_conv_w"], row(small["rnn_conv_b"]), wa, row(small["gate_a_b"]),
                               wx, row(small["gate_x_b"]), row(small["lru_lambda"]),
                               _gather_exchange(shard[cut:PACK_OFF[3]]) if dist else None)
    if dist:
        gw = {**gw, **_split_pack(jnp.concatenate([ga[0], gb[0]], axis=1), 1, 3)}
    g1, b1 = row(small["ln1_g"]), row(small["ln1_b"])
    fcw = small["ffn_conv_w"].reshape(3, NC, FF_CHUNK).transpose(1, 0, 2)
    fcb = small["ffn_conv_b"].reshape(NC, 1, FF_CHUNK)
    z1, h1b = _out_proj(att, rec, x, gw["w_out"], g1, b1)
    gate, ge, vd, act, *gc = _ffn_up(h1b, gw["w_up_t"], fcw, fcb,
                                     _gather_exchange(shard[PACK_OFF[3]:PACK_OFF[6]]) if dist else None)
    if dist:
        gw = {**gw, **_split_pack(gc[0], 3, 6)}
    dz2, dz2b, dpre, dpp, vec2 = _ffn_down(act, z1, p, tgt, gw["w_down"], gw["w_g"], gw["w_p_t"], g1, b1,
                                           row(small["ln2_g"]), row(small["ln2_b"]), row(small["ple_gate_b"]))
    dup, dfc = _ffn_bwd(dz2b, gate, ge, vd, gw["w_down"], fcw)
    dz1, vec1 = _ffn_dh1(dup, dz2, dpre, z1, gw["w_up_t"], gw["w_g"], g1, b1)
    per_chip = 2 * D_FF // 4 // FF_CHUNK
    big = {
        "w_ffn_up": _weight_grad_cols(
            h1b, dup.reshape(2 * NC, -1, FF_CHUNK), "dw_up", 2 * NC,
            lambda bt: pl.BlockSpec((None, bt, FF_CHUNK), lambda m, k: (m, k, 0)), (4, D, 2 * D_FF // 4),
            pl.BlockSpec((None, D, FF_CHUNK), lambda m, k: (2 * (m % 2) + (m // 2) // per_chip, 0, (m // 2) % per_chip))),
        "w_ffn_down": _weight_grad(act, dz2b, 512, "dw_down").reshape(4, D_FF // 4, D),
        "ple_gate_w": _weight_grad(h1b, dpre, 512, "dw_gate").reshape(4, D // 4, D),
        "ple_proj": _weight_grad_cols(
            p.astype(BF16), dpp, "dw_proj", 4, lambda bt: pl.BlockSpec((bt, D // 4), lambda j, k: (k, j)),
            (4, PLE, D // 4), pl.BlockSpec((None, PLE, D // 4), lambda j, k: (j, 0, 0))),
    }
    reduced = None
    if dist:
        g_ffn = [big[k] for k in FFN_WEIGHTS]
        ex = _swap_exchange(g_ffn)
    datt, drec, dz1b, *got = _out_proj_bwd(dz1, gw["w_out"], ex if dist else None)
    if dist:
        ex = _scatter_exchange(_add_half(g_ffn, got, core, "add_half_ffn"))
    dxr, dgr, dwa, dwx, dvec, *got = _rnn_bwd(drec, gr, h, xc, xr, small["rnn_conv_w"], wa, row(small["gate_a_b"]),
                                              wx, row(small["gate_x_b"]), row(small["lru_lambda"]), ex if dist else None)
    if dist:
        mine = _add4(got, "add_chips_ffn")
        ex = _send_exchange(mine)
    dq, dkv, dsinks, *got = _attn_bwd(q, kv, datt, sinks, ex if dist else None)
    if dist:
        reduced = (mine, got)
        big = {}
    sg = {
        "attn_sinks": dsinks[:, 0],
        "rnn_conv_w": dvec[4:8],
        "rnn_conv_b": dvec[3],
        "gate_a_w": _diag_blocks(dwa),
        "gate_a_b": dvec[0],
        "gate_x_w": _diag_blocks(dwx),
        "gate_x_b": dvec[1],
        "lru_lambda": dvec[2],
        "ln1_g": vec1[0],
        "ln1_b": vec1[1],
        "ffn_conv_w": dfc[:, 0:3].transpose(1, 0, 2).reshape(3, D_FF),
        "ffn_conv_b": dfc[:, 3].reshape(D_FF),
        "ple_gate_b": vec2[3],
        "ln2_g": vec2[1],
        "ln2_b": vec2[2],
    }
    loss = vec2[0, 0:1]
    ex = _all_devices_exchange(_pack_vecs([sg[k] for k in SMALL] + [loss])[0]) if dist else None
    grad_x, du, *small_all = _in_proj_bwd(dq, dkv, dxr, dgr, dz1, gw["w_in_t"], ex)
    big["w_in"] = _weight_grad_cols(
        xb, du, "dw_in", 4, lambda bt: pl.BlockSpec((None, bt, D_IN // 4), lambda j, k: (j, k, 0)), (4, D, D_IN // 4),
        pl.BlockSpec((None, D, D_IN // 4), lambda j, k: (j, 0, 0)))
    big["w_out"] = _weight_grad(jnp.concatenate([att, rec], axis=1), dz1b, 512, "dw_out").reshape(4, D // 4, D)
    return grad_x, big, sg, loss, reduced, small_all


BIG = ("w_in", "w_ffn_up", "w_out", "w_ffn_down", "ple_gate_w", "ple_proj")
BIG_KEYS = ("w_in_t", "w_up_t", "w_out", "w_down", "w_g", "w_p_t")
BIG_T = (True, True, False, False, False, True)
FFN_WEIGHTS = ("w_ffn_up", "w_ffn_down", "ple_gate_w", "ple_proj")
MIX_WEIGHTS = ("w_in", "w_out")
SMALL = ("attn_sinks", "rnn_conv_w", "rnn_conv_b", "gate_a_w", "gate_a_b", "gate_x_w", "gate_x_b", "lru_lambda",
         "ln1_g", "ln1_b", "ffn_conv_w", "ffn_conv_b", "ple_gate_b", "ln2_g", "ln2_b")
SHARDED_SMALL = ("rnn_conv_w", "ffn_conv_w")
WEIGHTS = ("w_in", "attn_sinks", "rnn_conv_w", "rnn_conv_b", "gate_a_w", "gate_a_b", "gate_x_w", "gate_x_b",
           "lru_lambda", "w_out", "ln1_g", "ln1_b", "w_ffn_up", "ffn_conv_w", "ffn_conv_b", "w_ffn_down",
           "ple_gate_w", "ple_gate_b", "ple_proj", "ln2_g", "ln2_b")


def _pack_big(d, first=0, last=6):
    parts = []
    for name, t in zip(BIG[first:last], BIG_T[first:last]):
        a = d[name]
        a = a.T if t else a
        parts.append(a.reshape(-1, 1024))
    return jnp.concatenate(parts, axis=0)


def _pack_vecs(items):
    parts, offs, n = [], [], 0
    for a in items:
        f = a.reshape(-1).astype(F32)
        pad = (-f.shape[0]) % 128
        parts.append(jnp.pad(f, (0, pad)))
        offs.append(n)
        n += (f.shape[0] + pad) // 128
    padr = (-n) % 8
    if padr:
        parts.append(jnp.zeros((padr * 128,), F32))
    return jnp.concatenate(parts).reshape(-1, 128), offs


def _unpack_vecs(a, offs, shapes):
    flat = a.reshape(-1)
    out = []
    for o, s in zip(offs, shapes):
        n = 1
        for d in s:
            n *= d
        out.append(flat[o * 128:o * 128 + n].reshape(s))
    return out


def kernel(x, p, w_in, attn_sinks, rnn_conv_w, rnn_conv_b, gate_a_w, gate_a_b, gate_x_w, gate_x_b, lru_lambda, w_out, ln1_g, ln1_b, w_ffn_up, ffn_conv_w, ffn_conv_b, w_ffn_down, ple_gate_w, ple_gate_b, ple_proj, ln2_g, ln2_b, loss_target, m_w_in, m_attn_sinks, m_rnn_conv_w, m_rnn_conv_b, m_gate_a_w, m_gate_a_b, m_gate_x_w, m_gate_x_b, m_lru_lambda, m_w_out, m_ln1_g, m_ln1_b, m_w_ffn_up, m_ffn_conv_w, m_ffn_conv_b, m_w_ffn_down, m_ple_gate_w, m_ple_gate_b, m_ple_proj, m_ln2_g, m_ln2_b, v_w_in, v_attn_sinks, v_rnn_conv_w, v_rnn_conv_b, v_gate_a_w, v_gate_a_b, v_gate_x_w, v_gate_x_b, v_lru_lambda, v_w_out, v_ln1_g, v_ln1_b, v_w_ffn_up, v_ffn_conv_w, v_ffn_conv_b, v_w_ffn_down, v_ple_gate_w, v_ple_gate_b, v_ple_proj, v_ln2_g, v_ln2_b):
    w = dict(w_in=w_in, attn_sinks=attn_sinks, rnn_conv_w=rnn_conv_w, rnn_conv_b=rnn_conv_b, gate_a_w=gate_a_w,
             gate_a_b=gate_a_b, gate_x_w=gate_x_w, gate_x_b=gate_x_b, lru_lambda=lru_lambda, w_out=w_out, ln1_g=ln1_g,
             ln1_b=ln1_b, w_ffn_up=w_ffn_up, ffn_conv_w=ffn_conv_w, ffn_conv_b=ffn_conv_b, w_ffn_down=w_ffn_down,
             ple_gate_w=ple_gate_w, ple_gate_b=ple_gate_b, ple_proj=ple_proj, ln2_g=ln2_g, ln2_b=ln2_b)
    m = dict(w_in=m_w_in, attn_sinks=m_attn_sinks, rnn_conv_w=m_rnn_conv_w, rnn_conv_b=m_rnn_conv_b, gate_a_w=m_gate_a_w,
             gate_a_b=m_gate_a_b, gate_x_w=m_gate_x_w, gate_x_b=m_gate_x_b, lru_lambda=m_lru_lambda, w_out=m_w_out,
             ln1_g=m_ln1_g, ln1_b=m_ln1_b, w_ffn_up=m_w_ffn_up, ffn_conv_w=m_ffn_conv_w, ffn_conv_b=m_ffn_conv_b,
             w_ffn_down=m_w_ffn_down, ple_gate_w=m_ple_gate_w, ple_gate_b=m_ple_gate_b, ple_proj=m_ple_proj,
             ln2_g=m_ln2_g, ln2_b=m_ln2_b)
    v = dict(w_in=v_w_in, attn_sinks=v_attn_sinks, rnn_conv_w=v_rnn_conv_w, rnn_conv_b=v_rnn_conv_b, gate_a_w=v_gate_a_w,
             gate_a_b=v_gate_a_b, gate_x_w=v_gate_x_w, gate_x_b=v_gate_x_b, lru_lambda=v_lru_lambda, w_out=v_w_out,
             ln1_g=v_ln1_g, ln1_b=v_ln1_b, w_ffn_up=v_w_ffn_up, ffn_conv_w=v_ffn_conv_w, ffn_conv_b=v_ffn_conv_b,
             w_ffn_down=v_w_ffn_down, ple_gate_w=v_ple_gate_w, ple_gate_b=v_ple_gate_b, ple_proj=v_ple_proj,
             ln2_g=v_ln2_g, ln2_b=v_ln2_b)
    w, m, v = ({k: a[0] for k, a in d.items()} for d in (w, m, v))
    chip = 2 * lax.axis_index("x") + lax.axis_index("y")
    core = lax.axis_index("c")

    wpack = _pack_big(w)
    cpack, _ = _pack_vecs([w["rnn_conv_w"], w["ffn_conv_w"]])
    shard = wpack.astype(MXU_DTYPE)
    g_in, gcp = _gather_first(shard[PACK_OFF[0]:PACK_OFF[1]], cpack)
    gw = _split_pack(g_in, 0, 1)
    small = {k: w[k] for k in SMALL}
    small["rnn_conv_w"] = gcp[:, 0:4].reshape(4, 4, 128).transpose(1, 0, 2).reshape(4, 512)
    small["ffn_conv_w"] = gcp[:, 4:22].reshape(4, 3, 768).transpose(1, 0, 2).reshape(3, 3072)

    core1 = core.reshape(1).astype(jnp.int32)
    grad_x, big, sg, loss, ffn_halves, small_all = _layer_grads(x[0], p[0, 0], loss_target[0], gw, small, shard, core1)

    shapes = [sg[k].shape for k in SMALL] + [(1,)]
    _, offs = _pack_vecs([jnp.zeros(s, F32) for s in shapes])
    red = dict(zip(SMALL + ("loss",), _unpack_vecs(_sum_devices(small_all[0]), offs, shapes)))
    red["rnn_conv_w"] = lax.dynamic_slice_in_dim(red["rnn_conv_w"], chip * 128, 128, axis=1)
    red["ffn_conv_w"] = lax.dynamic_slice_in_dim(red["ffn_conv_w"], chip * 768, 768, axis=1)

    def adamw(names, mine, other, name, exchange=None):
        out, got = _adamw_halves([w[k] for k in names], mine, other, [m[k] for k in names], [v[k] for k in names],
                                 core1, name, exchange)
        return dict(zip(names, out)), got

    g_mix = [big[k] for k in MIX_WEIGHTS]
    sib = _run_exchange(_swap_exchange(g_mix), "swap_mix")
    ffn_out, from_chips = adamw(FFN_WEIGHTS, *ffn_halves, "adamw_ffn",
                                _scatter_exchange(_add_half(g_mix, sib, core1, "add_half_mix")))
    mix_mine = _add4(from_chips, "add_chips_mix")
    mix_other = _run_exchange(_send_exchange(mix_mine), "send_mix")
    big_out = {**adamw(MIX_WEIGHTS, mix_mine, mix_other, "adamw_mix")[0], **ffn_out}
    wsm, offs2 = _pack_vecs([w[k] for k in SMALL])
    gsm, _ = _pack_vecs([red[k] for k in SMALL])
    msm, _ = _pack_vecs([m[k] for k in SMALL])
    vsm, _ = _pack_vecs([v[k] for k in SMALL])
    dsm, nmsm, nvsm = _adamw(wsm, gsm, msm, vsm, "adamw_small")
    shapes2 = [w[k].shape for k in SMALL]

    def named(n, smallp):
        d = {k: out[n][None] for k, out in big_out.items()}
        d.update({k: a[None] for k, a in zip(SMALL, _unpack_vecs(smallp, offs2, shapes2))})
        return [d[k] for k in WEIGHTS]

    return (red["loss"].reshape(()), grad_x[None], *named(0, gsm), *named(1, dsm), *named(2, nmsm), *named(3, nvsm))
```

```python
import functools

import jax
import jax.numpy as jnp
from jax import lax
from jax.experimental import pallas as pl
from jax.experimental.pallas import tpu as pltpu

F32 = jnp.float32
BF16 = jnp.bfloat16
MXU_DTYPE = jnp.bfloat16

D = 1024
D_ATT = 512
D_KV = 128
D_RNN = 512
D_IN = 1792
D_FF = 3072
FF_CHUNK = 512
PLE = 256
HEADS = 8
HEAD_DIM = 64
BLK = 128
RNN_BLOCKS = 8
LN_EPS = 1e-5
LRU_C = 8.0
ALPHA = float(2.0 ** 0.25)
SCALE = HEAD_DIM ** -0.5
NEG = -1e30

ADAM_LR = 0.001
ADAM_B1 = 0.9
ADAM_B2 = 0.999
ADAM_EPS = 1e-08
ADAM_WD = 0.01
ADAM_STEP = 10

VMEM_LIMIT_BYTES = 56 * 1024 * 1024
MESH = pl.DeviceIdType.MESH

PACK_ROWS = (448, 1536, 256, 768, 256, 64)
PACK_OFF = tuple(sum(PACK_ROWS[:i]) for i in range(len(PACK_ROWS) + 1))
PACK_TOTAL = PACK_OFF[-1]


def _params(**kw):
    return pltpu.CompilerParams(vmem_limit_bytes=VMEM_LIMIT_BYTES, **kw)


def _mm(a, b):
    return jnp.dot(a.astype(MXU_DTYPE), b.astype(MXU_DTYPE), preferred_element_type=F32)


def _mm_nt(a, b):
    return lax.dot_general(a.astype(MXU_DTYPE), b.astype(MXU_DTYPE), (((1,), (1,)), ((), ())),
                           preferred_element_type=F32)


def _mm_tn(a, b):
    return lax.dot_general(a.astype(MXU_DTYPE), b.astype(MXU_DTYPE), (((0,), (0,)), ((), ())),
                           preferred_element_type=F32)


def _sigmoid(x):
    return 1.0 / (1.0 + jnp.exp(-x))


def _gelu(x):
    c = 0.7978845608028654
    k = 0.044715
    x2 = x * x
    t = jnp.tanh(x * (c + (c * k) * x2))
    h = 0.5 * (1.0 + t)
    return x * h, h * (1.0 + (x * (1.0 - t)) * (c + (3.0 * c * k) * x2))


def _shift_rows(x, s, edge8):
    R = x.shape[0]
    row8 = lax.broadcasted_iota(jnp.int32, (8, x.shape[1]), 0)
    if s > 0:
        rolled = pltpu.roll(x, s, 0)
        first = jnp.where(row8 < s, pltpu.roll(edge8, s, 0), rolled[0:8])
        return jnp.concatenate([first, rolled[8:]], axis=0)
    k = -s
    rolled = pltpu.roll(x, R - k, 0)
    last = jnp.where(row8 >= 8 - k, pltpu.roll(edge8, 8 - k, 0), rolled[R - 8:])
    return jnp.concatenate([rolled[:R - 8], last], axis=0)


def _softplus(x):
    return jnp.maximum(x, 0.0) + jnp.log(1.0 + jnp.exp(-jnp.abs(x)))


def _ln(z, g, b):
    mu = jnp.mean(z, axis=-1, keepdims=True)
    zc = z - mu
    var = jnp.mean(zc * zc, axis=-1, keepdims=True)
    rstd = lax.rsqrt(var + LN_EPS)
    xhat = zc * rstd
    return xhat * g + b, xhat, rstd


def _ln_bwd(dy, xhat, rstd, g):
    dxh = dy * g
    m1 = jnp.mean(dxh, axis=-1, keepdims=True)
    m2 = jnp.mean(dxh * xhat, axis=-1, keepdims=True)
    return rstd * (dxh - m1 - xhat * m2)


def _colsum(x):
    return jnp.sum(x, axis=0, keepdims=True)


def _full(shape):
    nd = len(shape)
    return pl.BlockSpec(shape, lambda *_: (0,) * nd)


def _rows(tm, cols, fn=None):
    if fn is None:
        return pl.BlockSpec((tm, cols), lambda i: (i, 0))
    return pl.BlockSpec((tm, cols), lambda i: (fn(i), 0))


def _heads(tm):
    return pl.BlockSpec((HEADS, tm, HEAD_DIM), lambda i: (0, i, 0))


def _in_proj(x, w_in_t):
    T = x.shape[0]
    tm = 512

    def body(x_ref, w_ref, q_ref, kv_ref, xr_ref, gr_ref, xb_ref):
        xb = x_ref[...].astype(MXU_DTYPE)
        xb_ref[...] = xb.astype(BF16)
        q = _mm_nt(xb, w_ref[0:512, :])
        for h in range(HEADS):
            q_ref[h] = q[:, h * 64:(h + 1) * 64].astype(BF16)
        kv_ref[...] = _mm_nt(xb, w_ref[512:768, :]).astype(BF16)
        xr_ref[...] = _mm_nt(xb, w_ref[768:1280, :])
        gr_ref[...] = _mm_nt(xb, w_ref[1280:1792, :])

    return pl.pallas_call(
        body, name="in_proj", grid=(T // tm,),
        in_specs=[_rows(tm, D), _full((D_IN, D))],
        out_specs=[_heads(tm), _rows(tm, 256), _rows(tm, 512), _rows(tm, 512), _rows(tm, D)],
        out_shape=[jax.ShapeDtypeStruct((HEADS, T, 64), BF16), jax.ShapeDtypeStruct((T, 256), BF16),
                   jax.ShapeDtypeStruct((T, 512), F32), jax.ShapeDtypeStruct((T, 512), F32),
                   jax.ShapeDtypeStruct((T, D), BF16)],
        compiler_params=_params(),
    )(x, w_in_t)


def _attn_band(kv_ref, i):
    cur = pl.multiple_of(i * BLK, BLK)
    prev = pl.multiple_of(jnp.maximum(i - 1, 0) * BLK, BLK)
    band = jnp.concatenate([kv_ref[pl.ds(prev, BLK), :], kv_ref[pl.ds(cur, BLK), :]], axis=0)
    key = lax.broadcasted_iota(jnp.int32, (2 * BLK, 4 * BLK), 0)
    qry = lax.broadcasted_iota(jnp.int32, (2 * BLK, 4 * BLK), 1) & (BLK - 1)
    in_prev = jnp.logical_and(jnp.logical_and(key < BLK, key > qry), i > 0)
    mask = jnp.logical_or(in_prev, jnp.logical_and(key >= BLK, key - BLK <= qry))
    return band, mask, cur, prev


def _attn_scores(band, mask, qs, s_ref, g):
    st = jnp.where(mask, _mm_nt(band[:, g * 64:(g + 1) * 64], qs) * SCALE, NEG)
    lane = lax.broadcasted_iota(jnp.int32, (1, 4 * BLK), 1)
    sv = jnp.where(lane < BLK, s_ref[0, 4 * g],
                   jnp.where(lane < 2 * BLK, s_ref[0, 4 * g + 1], jnp.where(lane < 3 * BLK, s_ref[0, 4 * g + 2], s_ref[0, 4 * g + 3])))
    m = jnp.maximum(jnp.max(st, axis=0, keepdims=True), sv)
    p = jnp.exp(st - m)
    ps = jnp.exp(sv - m)
    return p, ps, jnp.sum(p, axis=0, keepdims=True) + ps


def _pos():
    return lax.axis_index("x"), lax.axis_index("y"), lax.axis_index("c")


def _other_chips(x, y):
    return [(1 - x, y), (x, 1 - y), (1 - x, 1 - y)]


def _gather_steps(w_ref, gw_ref, send_sems, recv_sems, local_sem):
    x, y, c = _pos()
    me = 2 * x + y
    chips = _other_chips(x, y)
    half = w_ref.shape[0] // 2
    mine = pl.ds(pl.multiple_of(c * half, 16), half)
    theirs = pl.ds(pl.multiple_of((1 - c) * half, 16), half)
    loc = pltpu.make_async_copy(w_ref, gw_ref.at[me], local_sem)

    def copy(k, src, dst, to):
        return pltpu.make_async_remote_copy(src_ref=src, dst_ref=dst, send_sem=send_sems.at[k], recv_sem=recv_sems.at[k],
                                            device_id=to, device_id_type=MESH)

    def out(k):
        px, py = chips[k]
        return copy(k, w_ref.at[mine], gw_ref.at[me, mine], (px, py, c))

    def fwd(k, rows):
        px, py = chips[k]
        return copy(3 + k, gw_ref.at[2 * px + py, rows], gw_ref.at[2 * px + py, rows], (x, y, 1 - c))

    def start():
        loc.start()
        for k in range(3):
            out(k).start()

    def forward():
        for k in range(3):
            px, py = chips[k]
            copy(k, w_ref.at[mine], gw_ref.at[2 * px + py, mine], (px, py, c)).wait_recv()
            fwd(k, mine).start()

    def finish():
        for k in range(3):
            fwd(k, theirs).wait_recv()
        for k in range(3):
            out(k).wait_send()
            fwd(k, mine).wait_send()
        loc.wait()

    return start, forward, finish


GATHER_SCRATCH = [pltpu.SemaphoreType.DMA((6,)), pltpu.SemaphoreType.DMA((6,)), pltpu.SemaphoreType.DMA]


class _Exchange:
    def __init__(self, args, out_shape, scratch, make):
        self.args, self.out_shape, self.scratch, self.make = list(args), list(out_shape), list(scratch), make


def _gather_exchange(wsrc):
    return _Exchange([wsrc], [jax.ShapeDtypeStruct((4,) + wsrc.shape, wsrc.dtype)], GATHER_SCRATCH,
                     lambda ins, outs, sems: _gather_steps(ins[0], outs[0], *sems))


def _launch(body, name, grid, in_specs, out_specs, out_shape, scratch, args, exchange=None, prefetch=0):
    def call(fn, fn_name, ins, outs, shapes, scr, operands, effects):
        spec = pltpu.PrefetchScalarGridSpec(num_scalar_prefetch=prefetch, grid=grid, in_specs=ins, out_specs=outs,
                                            scratch_shapes=scr)
        return pl.pallas_call(fn, name=fn_name, grid_spec=spec, out_shape=shapes,
                              compiler_params=_params(has_side_effects=effects))(*operands)

    if exchange is None:
        return call(body, name, list(in_specs), list(out_specs), list(out_shape), list(scratch), args, False)
    n_in, n_out, ei, eo, ns = len(in_specs), len(out_specs), len(exchange.args), len(exchange.out_shape), len(exchange.scratch)
    nsteps = 1
    for g in grid:
        nsteps *= g

    def wrapped(*refs):
        scalars, refs = refs[:prefetch], refs[prefetch:]
        ins, xin = refs[:n_in], refs[n_in:n_in + ei]
        outs, xout = refs[n_in + ei:n_in + ei + n_out], refs[n_in + ei + n_out:n_in + ei + n_out + eo]
        rest = refs[n_in + ei + n_out + eo:]
        own, sems = rest[:len(rest) - ns], rest[len(rest) - ns:]
        start, forward, finish = exchange.make(xin, xout, sems)
        i = pl.program_id(0)
        for d in range(1, len(grid)):
            i = i * grid[d] + pl.program_id(d)
        pl.when(i == 0)(start)
        body(*scalars, *ins, *outs, *own)
        pl.when(i == max(nsteps - 3, 0))(forward)
        pl.when(i == nsteps - 1)(finish)

    anyspec = pl.BlockSpec(memory_space=pl.ANY)
    return call(wrapped, name + "_x", list(in_specs) + [anyspec] * ei, list(out_specs) + [anyspec] * eo,
                list(out_shape) + exchange.out_shape, list(scratch) + exchange.scratch, (*args, *exchange.args), True)


def _attn_fwd(q, kv, sinks, exchange=None):
    T = kv.shape[0]

    def body(q_ref, kv_ref, s_ref, o_ref):
        i = pl.program_id(0)
        band, mask, _, _ = _attn_band(kv_ref, i)
        for g in range(2):
            qs = q_ref[4 * g:4 * g + 4].reshape(4 * BLK, HEAD_DIM)
            p, _, den = _attn_scores(band, mask, qs, s_ref, g)
            ot = _mm_tn(band[:, 128:256], p) / den
            for hh in range(4):
                o = ot[:, hh * BLK:(hh + 1) * BLK].T
                o_ref[:, (4 * g + hh) * 64:(4 * g + hh + 1) * 64] = o[:, g * 64:(g + 1) * 64].astype(BF16)

    return _launch(body, "attn_fwd", (T // BLK,), [_heads(BLK), _full((T, 256)), pl.BlockSpec(memory_space=pltpu.SMEM)],
                   [_rows(BLK, 512)], [jax.ShapeDtypeStruct((T, 512), BF16)], [], (q, kv, sinks), exchange)


def _attn_bwd(q, kv, do, sinks, exchange=None):
    T = kv.shape[0]

    def body(q_ref, kv_ref, do_ref, s_ref, dq_ref, dkv_ref, ds_ref):
        i = pl.program_id(0)
        band, mask, cur, prev = _attn_band(kv_ref, i)

        @pl.when(i == 0)
        def _():
            ds_ref[...] = jnp.zeros_like(ds_ref)

        for g in range(2):
            qs = q_ref[4 * g:4 * g + 4].reshape(4 * BLK, HEAD_DIM)
            dos = do_ref[4 * g:4 * g + 4].reshape(4 * BLK, HEAD_DIM)
            p, ps, den = _attn_scores(band, mask, qs, s_ref, g)
            inv = 1.0 / den
            p = p * inv
            dpt = _mm_nt(band[:, 128 + g * 64:192 + g * 64], dos)
            delta = jnp.sum(p * dpt, axis=0, keepdims=True)
            dst = p * (dpt - delta)
            dsv = -(ps * inv) * delta
            for hh in range(4):
                dsink = jnp.sum(dsv[:, hh * BLK:(hh + 1) * BLK], axis=1, keepdims=True)
                ds_ref[4 * g + hh:4 * g + hh + 1, :] += jnp.broadcast_to(dsink, (1, 128))
            dqt = _mm_tn(band[:, 0:128], dst) * SCALE
            for hh in range(4):
                dqh = dqt[:, hh * BLK:(hh + 1) * BLK].T
                dq_ref[:, (4 * g + hh) * 64:(4 * g + hh + 1) * 64] = dqh[:, g * 64:(g + 1) * 64].astype(BF16)
            dk = _mm(dst, qs) * SCALE
            dv = _mm(p, dos)
            dkv_ref[pl.ds(cur, BLK), g * 64:(g + 1) * 64] = dk[BLK:2 * BLK]
            dkv_ref[pl.ds(cur, BLK), 128 + g * 64:192 + g * 64] = dv[BLK:2 * BLK]
            dkv_ref[pl.ds(prev, BLK), g * 64:(g + 1) * 64] += dk[0:BLK]
            dkv_ref[pl.ds(prev, BLK), 128 + g * 64:192 + g * 64] += dv[0:BLK]

    return _launch(body, "attn_bwd", (T // BLK,),
                   [_heads(BLK), _full((T, 256)), _heads(BLK), pl.BlockSpec(memory_space=pltpu.SMEM)],
                   [_rows(BLK, 512), _full((T, 256)), _full((8, 128))],
                   [jax.ShapeDtypeStruct((T, 512), BF16), jax.ShapeDtypeStruct((T, 256), F32),
                    jax.ShapeDtypeStruct((8, 128), F32)], [], (q, kv, do, sinks), exchange)


def _rows8(tm, cols):
    return lax.broadcasted_iota(jnp.int32, (tm, cols), 0) & 7


def _lru_gates(xc, wa, ba, wx, bx, lam):
    r = _sigmoid(_mm(xc, wa) + ba)
    ii = _sigmoid(_mm(xc, wx) + bx)
    sp = _softplus(-lam)
    la = -LRU_C * r * sp
    a = jnp.exp(la)
    m = jnp.sqrt(-jnp.tanh(la) * (a * a + 1.0))
    return r, ii, sp, a, m


def _rnn_fwd(xr, gr, cw, cb, wa, ba, wx, bx, lam, exchange=None):
    T = xr.shape[0]
    tm = 256
    C = D_RNN

    def body(xr_ref, gr_ref, cw_ref, cb_ref, wa_ref, ba_ref, wx_ref, bx_ref, lam_ref,
             xc_ref, h_ref, rec_ref, ext, a_s, b_s, carry):
        i = pl.program_id(0)

        @pl.when(i == 0)
        def _():
            ext[...] = jnp.zeros((8, C), F32)
            carry[...] = jnp.zeros((8, C), F32)

        xr = xr_ref[...]
        edge = ext[...]
        xc = cb_ref[...] + cw_ref[3:4, :] * xr
        for k in range(3):
            xc = xc + cw_ref[k:k + 1, :] * _shift_rows(xr, 3 - k, edge)
        ext[...] = xr[tm - 8:tm, :]
        xc_ref[...] = xc
        _, ii, _, a, m = _lru_gates(xc, wa_ref[...], ba_ref[...], wx_ref[...], bx_ref[...], lam_ref[...])
        b = m * ii * xc
        r8 = _rows8(tm, C)
        for d in (1, 2, 4):
            ok = r8 >= d
            a_sh = jnp.where(ok, pltpu.roll(a, d, 0), 1.0)
            b_sh = jnp.where(ok, pltpu.roll(b, d, 0), 0.0)
            b = a * b_sh + b
            a = a * a_sh
        a_s[...] = a
        b_s[...] = b

        def step(g, hin):
            s = pl.multiple_of(g * 8, 8)
            hg = a_s[pl.ds(s, 8), :] * hin + b_s[pl.ds(s, 8), :]
            h_ref[pl.ds(s, 8), :] = hg
            return jnp.broadcast_to(hg[7:8, :], (8, C))

        carry[...] = lax.fori_loop(0, tm // 8, step, carry[...])
        ge, _ = _gelu(gr_ref[...])
        rec_ref[...] = (h_ref[...] * ge).astype(BF16)

    vec = _full((1, C))
    in_specs = [_rows(tm, C), _rows(tm, C), _full((4, C)), vec, _full((C, C)), vec, _full((C, C)), vec, vec]
    out_specs = [_rows(tm, C), _rows(tm, C), _rows(tm, C)]
    out_shape = [jax.ShapeDtypeStruct((T, C), F32), jax.ShapeDtypeStruct((T, C), F32), jax.ShapeDtypeStruct((T, C), BF16)]
    scratch = [pltpu.VMEM((8, C), F32), pltpu.VMEM((tm, C), F32), pltpu.VMEM((tm, C), F32), pltpu.VMEM((8, C), F32)]
    return _launch(body, "rnn_fwd", (T // tm,), in_specs, out_specs, out_shape, scratch,
                   (xr, gr, cw, cb, wa, ba, wx, bx, lam), exchange)


def _rnn_bwd(drec, gr, h, xc, xr, cw, wa, ba, wx, bx, lam, exchange=None):
    T = xr.shape[0]
    tm = 256
    C = D_RNN
    nt = T // tm
    t8 = tm // 8

    def body(drec_ref, gr_ref, h_ref, hp_ref, xc_ref, xr_ref, cw_ref, wa_ref, ba_ref, wx_ref, bx_ref,
             lam_ref, dxr_ref, dgr_ref, dwa_ref, dwx_ref, dvec_ref, c_s, g_s, gout, ext, anext, gcarry):
        i = pl.program_id(0)
        j = nt - 1 - i

        @pl.when(i == 0)
        def _():
            dwa_ref[...] = jnp.zeros_like(dwa_ref)
            dwx_ref[...] = jnp.zeros_like(dwx_ref)
            dvec_ref[...] = jnp.zeros_like(dvec_ref)
            anext[...] = jnp.zeros((8, C), F32)
            gcarry[...] = jnp.zeros((8, C), F32)
            ext[...] = jnp.zeros((8, C), F32)

        xc = xc_ref[...]
        lam = lam_ref[...]
        r, ii, sp, a, m = _lru_gates(xc, wa_ref[...], ba_ref[...], wx_ref[...], bx_ref[...], lam)
        ge, dge = _gelu(gr_ref[...])
        drec = drec_ref[...]
        hh = h_ref[...]
        dgr_ref[...] = (drec * hh * dge).astype(BF16)
        dh = drec * ge
        rowi = lax.broadcasted_iota(jnp.int32, (tm, C), 0)
        c = jnp.where(rowi == tm - 1, jnp.broadcast_to(anext[0:1, :], (tm, C)), pltpu.roll(a, tm - 1, 0))
        anext[...] = a[0:8, :]
        r8 = rowi & 7
        gg = dh
        for d in (1, 2, 4):
            ok = r8 < 8 - d
            c_sh = jnp.where(ok, pltpu.roll(c, tm - d, 0), 1.0)
            g_sh = jnp.where(ok, pltpu.roll(gg, tm - d, 0), 0.0)
            gg = c * g_sh + gg
            c = c * c_sh
        c_s[...] = c
        g_s[...] = gg

        def step(k, gin):
            s = pl.multiple_of((t8 - 1 - k) * 8, 8)
            og = c_s[pl.ds(s, 8), :] * gin + g_s[pl.ds(s, 8), :]
            gout[pl.ds(s, 8), :] = og
            return jnp.broadcast_to(og[0:1, :], (8, C))

        gcarry[...] = lax.fori_loop(0, t8, step, gcarry[...])
        G = gout[...]
        hprev_row = jnp.where(j > 0, hp_ref[7:8, :], 0.0)
        hprev = jnp.where(rowi == 0, jnp.broadcast_to(hprev_row, (tm, C)), pltpu.roll(hh, 1, 0))
        da = G * hprev
        dm = G * ii * xc
        di = G * m * xc
        dxc = G * m * ii
        dla = da * a - dm * a * a / m
        dr = dla * (-LRU_C * sp)
        dsp = _colsum(dla * (-LRU_C * r))
        dlam = dsp * (-_sigmoid(-lam))
        dpr = dr * r * (1.0 - r)
        dpi = di * ii * (1.0 - ii)
        dxc = dxc + _mm_nt(dpr, wa_ref[...]) + _mm_nt(dpi, wx_ref[...])
        dwa_ref[...] += _mm_tn(xc, dpr)
        dwx_ref[...] += _mm_tn(xc, dpi)
        dvec_ref[0:1, :] += _colsum(dpr)
        dvec_ref[1:2, :] += _colsum(dpi)
        dvec_ref[2:3, :] += dlam
        dvec_ref[3:4, :] += _colsum(dxc)
        edge = ext[...]
        xr = xr_ref[...]
        dxr = cw_ref[3:4, :] * dxc
        dvec_ref[7:8, :] += _colsum(dxc * xr)
        for k in range(3):
            up = _shift_rows(dxc, k - 3, edge)
            dxr = dxr + cw_ref[k:k + 1, :] * up
            dvec_ref[4 + k:5 + k, :] += _colsum(up * xr)
        ext[...] = dxc[0:8, :]
        dxr_ref[...] = dxr.astype(BF16)

    rev = lambda i: nt - 1 - i
    prev8 = lambda i: jnp.maximum((nt - 1 - i) * t8 - 1, 0)
    vec = _full((1, C))
    return _launch(
        body, "rnn_bwd", (nt,),
        [_rows(tm, C, rev), _rows(tm, C, rev), _rows(tm, C, rev), _rows(8, C, prev8), _rows(tm, C, rev),
         _rows(tm, C, rev), _full((4, C)), _full((C, C)), vec, _full((C, C)), vec, vec],
        [_rows(tm, C, rev), _rows(tm, C, rev), _full((C, C)), _full((C, C)), _full((8, C))],
        [jax.ShapeDtypeStruct((T, C), BF16), jax.ShapeDtypeStruct((T, C), BF16),
         jax.ShapeDtypeStruct((C, C), F32), jax.ShapeDtypeStruct((C, C), F32), jax.ShapeDtypeStruct((8, C), F32)],
        [pltpu.VMEM((tm, C), F32), pltpu.VMEM((tm, C), F32), pltpu.VMEM((tm, C), F32),
         pltpu.VMEM((8, C), F32), pltpu.VMEM((8, C), F32), pltpu.VMEM((8, C), F32)],
        (drec, gr, h, h, xc, xr, cw, wa, ba, wx, bx, lam), exchange)


def _out_proj(att, rec, x, w_out, g1, b1):
    T = x.shape[0]
    tm = 512

    def body(att_ref, rec_ref, x_ref, w_ref, g1_ref, b1_ref, z_ref, h_ref):
        mix = _mm(att_ref[...], w_ref[0:512, :]) + _mm(rec_ref[...], w_ref[512:1024, :])
        z1 = ALPHA * x_ref[...] + mix
        z_ref[...] = z1
        h1, _, _ = _ln(z1, g1_ref[...], b1_ref[...])
        h_ref[...] = h1.astype(MXU_DTYPE).astype(BF16)

    return pl.pallas_call(
        body, name="out_proj", grid=(T // tm,),
        in_specs=[_rows(tm, 512), _rows(tm, 512), _rows(tm, D), _full((D, D)), _full((1, D)), _full((1, D))],
        out_specs=[_rows(tm, D), _rows(tm, D)],
        out_shape=[jax.ShapeDtypeStruct((T, D), F32), jax.ShapeDtypeStruct((T, D), BF16)],
        compiler_params=_params(),
    )(att, rec, x, w_out, g1, b1)


NC = D_FF // FF_CHUNK


def _ffn_up(h1b, w_up_t, fcw, fcb, exchange=None):
    T = h1b.shape[0]
    tm = 512
    CW = FF_CHUNK

    def body(h_ref, wg_ref, wv_ref, fcw_ref, fcb_ref, gate_ref, ge_ref, vd_ref, act_ref, before):
        i = pl.program_id(1)

        @pl.when(i == 0)
        def _():
            before[...] = jnp.zeros((8, CW), F32)

        hb = h_ref[...]
        gate = _mm_nt(hb, wg_ref[...])
        val = _mm_nt(hb, wv_ref[...])
        gate_ref[...] = gate.astype(BF16)
        edge = before[...]
        gc = (fcb_ref[...] + fcw_ref[0:1, :] * _shift_rows(gate, 2, edge) + fcw_ref[1:2, :] * _shift_rows(gate, 1, edge)
              + fcw_ref[2:3, :] * gate)
        before[...] = gate[tm - 8:tm, :]
        ge, dge = _gelu(gc)
        ge_ref[...] = ge.astype(BF16)
        vd_ref[...] = (val * dge).astype(BF16)
        act_ref[...] = (ge * val).astype(BF16)

    chunk = pl.BlockSpec((None, tm, CW), lambda c, i: (c, i, 0))
    return _launch(
        body, "ffn_up", (NC, T // tm),
        [pl.BlockSpec((tm, D), lambda c, i: (i, 0)), pl.BlockSpec((CW, D), lambda c, i: (c, 0)),
         pl.BlockSpec((CW, D), lambda c, i: (NC + c, 0)), pl.BlockSpec((None, 3, CW), lambda c, i: (c, 0, 0)),
         pl.BlockSpec((None, 1, CW), lambda c, i: (c, 0, 0))],
        [chunk] * 4, [jax.ShapeDtypeStruct((NC, T, CW), BF16)] * 4, [pltpu.VMEM((8, CW), F32)],
        (h1b, w_up_t, w_up_t, fcw, fcb), exchange)


def _ffn_down(act, z1, p, tgt, w_down, w_g, w_p_t, g1, b1, g2, b2, bg):
    T = z1.shape[0]
    tm = 256

    def body(act_ref, z_ref, p_ref, t_ref, wdn_hbm, wg_hbm, wp_hbm, g1_ref, b1_ref, g2_ref, b2_ref, bg_ref,
             dz2_ref, dz2b_ref, dpre_ref, dpp_ref, vec_ref, wdn, wg, wp):
        @pl.when(pl.program_id(0) == 0)
        def _():
            pltpu.sync_copy(wdn_hbm, wdn)
            pltpu.sync_copy(wg_hbm, wg)
            pltpu.sync_copy(wp_hbm, wp)
            vec_ref[...] = jnp.zeros_like(vec_ref)

        g2v = g2_ref[...]
        h1, _, _ = _ln(z_ref[...], g1_ref[...], b1_ref[...])
        h1b = h1.astype(MXU_DTYPE)
        ffn = _mm(act_ref[0], wdn[0:FF_CHUNK, :])
        for c in range(1, NC):
            ffn = ffn + _mm(act_ref[c], wdn[c * FF_CHUNK:(c + 1) * FF_CHUNK, :])
        sg = _sigmoid(_mm(h1b, wg[...]) + bg_ref[...])
        pp = _mm_nt(p_ref[...], wp[...])
        z2 = ALPHA * h1 + ffn + sg * pp
        y, xh2, rstd2 = _ln(z2, g2v, b2_ref[...])
        diff = y - t_ref[...]
        dy = diff * (1.0 / D)
        dz2 = _ln_bwd(dy, xh2, rstd2, g2v)
        dpre = dz2 * pp * sg * (1.0 - sg)
        dz2_ref[...] = dz2
        dz2b_ref[...] = dz2.astype(BF16)
        dpre_ref[...] = dpre.astype(BF16)
        dpp_ref[...] = (dz2 * sg).astype(BF16)
        loss = 0.5 * jnp.sum(jnp.sum(diff * diff, axis=1, keepdims=True), axis=0, keepdims=True) * (1.0 / D)
        vec_ref[0:1, :] += jnp.broadcast_to(loss, (1, D))
        vec_ref[1:2, :] += _colsum(dy * xh2)
        vec_ref[2:3, :] += _colsum(dy)
        vec_ref[3:4, :] += _colsum(dpre)

    anyspec = pl.BlockSpec(memory_space=pl.ANY)
    vec = _full((1, D))
    return pl.pallas_call(
        body, name="ffn_down", grid=(T // tm,),
        in_specs=[pl.BlockSpec((NC, tm, FF_CHUNK), lambda i: (0, i, 0)), _rows(tm, D), _rows(tm, PLE), _rows(tm, D),
                  anyspec, anyspec, anyspec] + [vec] * 5,
        out_specs=[_rows(tm, D)] * 4 + [_full((8, D))],
        out_shape=[jax.ShapeDtypeStruct((T, D), F32)] + [jax.ShapeDtypeStruct((T, D), BF16)] * 3
                  + [jax.ShapeDtypeStruct((8, D), F32)],
        scratch_shapes=[pltpu.VMEM((D_FF, D), MXU_DTYPE), pltpu.VMEM((D, D), MXU_DTYPE), pltpu.VMEM((D, PLE), MXU_DTYPE)],
        compiler_params=_params(),
    )(act, z1, p, tgt, w_down, w_g, w_p_t, g1, b1, g2, b2, bg)


def _ffn_bwd(dz2b, gate, ge, vd, w_down, fcw):
    T = dz2b.shape[0]
    tm = 512
    CW = FF_CHUNK
    nt = T // tm

    def body(dz_ref, wdn_ref, gate_ref, ge_ref, vd_ref, fcw_ref, dup_ref, dfc_ref, after):
        i = pl.program_id(1)

        @pl.when(i == 0)
        def _():
            after[...] = jnp.zeros((8, CW), F32)
            dfc_ref[...] = jnp.zeros_like(dfc_ref)

        gate = gate_ref[...].astype(F32)
        dact = _mm_nt(dz_ref[...], wdn_ref[...])
        dgc = dact * vd_ref[...].astype(F32)
        edge = after[...]
        dgc1 = _shift_rows(dgc, -1, edge)
        dgc2 = _shift_rows(dgc, -2, edge)
        after[...] = dgc[0:8, :]
        dup_ref[0] = (fcw_ref[2:3, :] * dgc + fcw_ref[1:2, :] * dgc1 + fcw_ref[0:1, :] * dgc2).astype(BF16)
        dup_ref[1] = (dact * ge_ref[...].astype(F32)).astype(BF16)
        dfc_ref[0:1, :] += _colsum(dgc2 * gate)
        dfc_ref[1:2, :] += _colsum(dgc1 * gate)
        dfc_ref[2:3, :] += _colsum(dgc * gate)
        dfc_ref[3:4, :] += _colsum(dgc)

    rev = lambda c, i: (c, nt - 1 - i, 0)
    chunk = pl.BlockSpec((None, tm, CW), rev)
    return pl.pallas_call(
        body, name="ffn_bwd", grid=(NC, nt),
        in_specs=[pl.BlockSpec((tm, D), lambda c, i: (nt - 1 - i, 0)), pl.BlockSpec((CW, D), lambda c, i: (c, 0)),
                  chunk, chunk, chunk, pl.BlockSpec((None, 3, CW), lambda c, i: (c, 0, 0))],
        out_specs=[pl.BlockSpec((None, 2, tm, CW), lambda c, i: (c, 0, nt - 1 - i, 0)),
                   pl.BlockSpec((None, 8, CW), lambda c, i: (c, 0, 0))],
        out_shape=[jax.ShapeDtypeStruct((NC, 2, T, CW), BF16), jax.ShapeDtypeStruct((NC, 8, CW), F32)],
        scratch_shapes=[pltpu.VMEM((8, CW), F32)],
        compiler_params=_params(),
    )(dz2b, w_down, gate, ge, vd, fcw)


def _ffn_dh1(dup, dz2, dpre, z1, w_up_t, w_g, g1, b1):
    T = z1.shape[0]
    tm = 256

    def body(dup_ref, dz2_ref, dpre_ref, z_ref, wup_hbm, wg_hbm, g1_ref, b1_ref, dz1_ref, vec_ref, wup, wg):
        @pl.when(pl.program_id(0) == 0)
        def _():
            pltpu.sync_copy(wup_hbm, wup)
            pltpu.sync_copy(wg_hbm, wg)
            vec_ref[...] = jnp.zeros_like(vec_ref)

        g1v = g1_ref[...]
        _, xh1, rstd1 = _ln(z_ref[...], g1v, b1_ref[...])
        dh1 = ALPHA * dz2_ref[...] + _mm_nt(dpre_ref[...], wg[...])
        for c in range(NC):
            for s in range(2):
                r0 = s * D_FF + c * FF_CHUNK
                dh1 = dh1 + _mm(dup_ref[c, s], wup[r0:r0 + FF_CHUNK, :])
        dz1_ref[...] = _ln_bwd(dh1, xh1, rstd1, g1v)
        vec_ref[0:1, :] += _colsum(dh1 * xh1)
        vec_ref[1:2, :] += _colsum(dh1)

    anyspec = pl.BlockSpec(memory_space=pl.ANY)
    vec = _full((1, D))
    return pl.pallas_call(
        body, name="ffn_dh1", grid=(T // tm,),
        in_specs=[pl.BlockSpec((NC, 2, tm, FF_CHUNK), lambda i: (0, 0, i, 0)), _rows(tm, D), _rows(tm, D), _rows(tm, D),
                  anyspec, anyspec, vec, vec],
        out_specs=[_rows(tm, D), _full((8, D))],
        out_shape=[jax.ShapeDtypeStruct((T, D), F32), jax.ShapeDtypeStruct((8, D), F32)],
        scratch_shapes=[pltpu.VMEM((2 * D_FF, D), MXU_DTYPE), pltpu.VMEM((D, D), MXU_DTYPE)],
        compiler_params=_params(),
    )(dup, dz2, dpre, z1, w_up_t, w_g, g1, b1)


def _out_proj_bwd(dz1, w_out, exchange=None):
    T = dz1.shape[0]
    tm = 512

    def body(dz_ref, w_ref, datt_ref, drec_ref):
        dzb = dz_ref[...].astype(MXU_DTYPE)
        datt = _mm_nt(dzb, w_ref[0:512, :])
        for h in range(HEADS):
            datt_ref[h] = datt[:, h * 64:(h + 1) * 64].astype(BF16)
        drec_ref[...] = _mm_nt(dzb, w_ref[512:1024, :])

    return _launch(body, "out_proj_bwd", (T // tm,), [_rows(tm, D), _full((D, D))], [_heads(tm), _rows(tm, 512)],
                   [jax.ShapeDtypeStruct((HEADS, T, 64), BF16), jax.ShapeDtypeStruct((T, 512), F32)], [],
                   (dz1, w_out), exchange)


def _in_proj_bwd(dq, dkv, dxr, dgr, dz1, w_in_t, exchange=None):
    T = dz1.shape[0]
    tm = 512
    W = D_IN // 4

    def body(dq_ref, dkv_ref, dxr_ref, dgr_ref, dz_ref, w_ref, dx_ref, du_ref):
        dkv = dkv_ref[...]
        dx_ref[...] = (ALPHA * dz_ref[...] + _mm(dq_ref[...], w_ref[0:512, :]) + _mm(dkv, w_ref[512:768, :])
                       + _mm(dxr_ref[...], w_ref[768:1280, :]) + _mm(dgr_ref[...], w_ref[1280:1792, :]))
        dq, dxr, dgr = dq_ref[...].astype(F32), dxr_ref[...].astype(F32), dgr_ref[...].astype(F32)
        du_ref[0] = dq[:, 0:W].astype(BF16)
        du_ref[1, :, 0:64] = dq[:, W:512].astype(BF16)
        du_ref[1, :, 64:320] = dkv.astype(BF16)
        du_ref[1, :, 320:W] = dxr[:, 0:128].astype(BF16)
        du_ref[2, :, 0:384] = dxr[:, 128:512].astype(BF16)
        du_ref[2, :, 384:W] = dgr[:, 0:64].astype(BF16)
        du_ref[3] = dgr[:, 64:512].astype(BF16)

    return _launch(body, "in_proj_bwd", (T // tm,),
                   [_rows(tm, 512), _rows(tm, 256), _rows(tm, 512), _rows(tm, 512), _rows(tm, D), _full((D_IN, D))],
                   [_rows(tm, D), pl.BlockSpec((4, tm, W), lambda i: (0, i, 0))],
                   [jax.ShapeDtypeStruct((T, D), F32), jax.ShapeDtypeStruct((4, T, W), BF16)], [],
                   (dq, dkv, dxr, dgr, dz1, w_in_t), exchange)


def _accumulate_tn(a_ref, b_ref, o_ref):
    @pl.when(pl.program_id(1) == 0)
    def _():
        o_ref[...] = jnp.zeros_like(o_ref)

    o_ref[...] += _mm_tn(a_ref[...], b_ref[...])


def _weight_grad_cols(a, b, name, n_blocks, b_spec, out_shape, out_spec, exchange=None):
    T, M = a.shape
    bt = min(2048, T)
    return _launch(functools.partial(_accumulate_tn), name, (n_blocks, T // bt),
                   [pl.BlockSpec((bt, M), lambda m, k: (k, 0)), b_spec(bt)], [out_spec],
                   [jax.ShapeDtypeStruct(out_shape, F32)], [], (a, b), exchange)


def _weight_grad(a, b, bm, name):
    bt = min(2048, b.shape[0])
    if a.ndim == 3:
        assert a.shape[2] == bm
        T, M = a.shape[1], a.shape[0] * bm
        a_spec = pl.BlockSpec((None, bt, bm), lambda m, k: (m, k, 0))
    else:
        T, M = a.shape
        a_spec = pl.BlockSpec((bt, bm), lambda m, k: (k, m))
    N = b.shape[1]
    nk = T // bt

    return pl.pallas_call(
        functools.partial(_accumulate_tn), name=name, grid=(M // bm, nk),
        in_specs=[a_spec, pl.BlockSpec((bt, N), lambda m, k: (k, 0))],
        out_specs=pl.BlockSpec((bm, N), lambda m, k: (m, 0)),
        out_shape=jax.ShapeDtypeStruct((M, N), F32),
        compiler_params=_params(),
    )(a, b)


def _adamw(w, g, m, v, name):
    R, C = w.shape
    tr = R // 8 if R % 64 == 0 else R
    c1 = 1.0 / (1.0 - ADAM_B1 ** ADAM_STEP)
    c2 = 1.0 / (1.0 - ADAM_B2 ** ADAM_STEP)

    def body(w_ref, g_ref, m_ref, v_ref, d_ref, nm_ref, nv_ref):
        g = g_ref[...]
        nm = ADAM_B1 * m_ref[...] + (1.0 - ADAM_B1) * g
        nv = ADAM_B2 * v_ref[...] + (1.0 - ADAM_B2) * g * g
        nm_ref[...] = nm
        nv_ref[...] = nv
        d_ref[...] = -ADAM_LR * ((nm * c1) / (jnp.sqrt(nv * c2) + ADAM_EPS) + ADAM_WD * w_ref[...])

    spec = pl.BlockSpec((tr, C), lambda i: (i, 0))
    return pl.pallas_call(
        body, name=name, grid=(R // tr,),
        in_specs=[spec] * 4, out_specs=[spec] * 3,
        out_shape=[jax.ShapeDtypeStruct((R, C), F32)] * 3,
        compiler_params=_params(),
    )(w, g, m, v)


def _adamw_halves(ws, mines, sibs, ms, vs, c, name, exchange=None):
    n, nb = len(ws), 4
    c1 = 1.0 / (1.0 - ADAM_B1 ** ADAM_STEP)
    c2 = 1.0 / (1.0 - ADAM_B2 ** ADAM_STEP)

    def body(c_ref, *refs):
        own = (pl.program_id(0) // nb) == c_ref[0]
        for i in range(n):
            w_ref, a_ref, b_ref, m_ref, v_ref = refs[5 * i:5 * i + 5]
            g_ref, d_ref, nm_ref, nv_ref = refs[5 * n + 4 * i:5 * n + 4 * i + 4]
            g = jnp.where(own, a_ref[...], b_ref[...])
            nm = ADAM_B1 * m_ref[...] + (1.0 - ADAM_B1) * g
            nv = ADAM_B2 * v_ref[...] + (1.0 - ADAM_B2) * g * g
            g_ref[...] = g
            nm_ref[...] = nm
            nv_ref[...] = nv
            d_ref[...] = -ADAM_LR * ((nm * c1) / (jnp.sqrt(nv * c2) + ADAM_EPS) + ADAM_WD * w_ref[...])

    in_specs, out_specs, out_shape, args = [], [], [], []
    for w, a, b, m, v in zip(ws, mines, sibs, ms, vs):
        R, C = w.shape
        tr = R // (2 * nb)
        assert tr % 8 == 0 and a.shape == (R // 2, C)
        full = pl.BlockSpec((tr, C), lambda i, c_ref: (i, 0))
        half = pl.BlockSpec((tr, C), lambda i, c_ref: (i % nb, 0))
        in_specs += [full, half, half, full, full]
        out_specs += [full] * 4
        out_shape += [jax.ShapeDtypeStruct((R, C), F32)] * 4
        args += [w, a, b, m, v]
    out = _launch(body, name, (2 * nb,), in_specs, out_specs, out_shape, [], (c, *args), exchange, prefetch=1)
    return [tuple(out[4 * i:4 * i + 4]) for i in range(n)], list(out[4 * n:])


def _add4(fs, name):
    n = len(fs)

    def body(*refs):
        for a_ref, o_ref in zip(refs[:n], refs[n:]):
            o_ref[...] = ((a_ref[0].astype(F32) + a_ref[1].astype(F32)) + a_ref[2].astype(F32)) + a_ref[3].astype(F32)

    for f in fs:
        assert (f.shape[1] // 2) % 16 == 0
    return pl.pallas_call(
        body, name=name, grid=(2,),
        in_specs=[pl.BlockSpec((4, f.shape[1] // 2, f.shape[2]), lambda i: (0, i, 0)) for f in fs],
        out_specs=[pl.BlockSpec((f.shape[1] // 2, f.shape[2]), lambda i: (i, 0)) for f in fs],
        out_shape=[jax.ShapeDtypeStruct(f.shape[1:], F32) for f in fs], compiler_params=_params())(*fs)


def _gather_first(wsrc, cpack):
    def body(w_ref, c_ref, gw_ref, gc_ref, send_sems, recv_sems, local_sem, csend, crecv, clocal):
        x, y, c = _pos()
        me = 2 * x + y
        chips = _other_chips(x, y)
        start, forward, finish = _gather_steps(w_ref, gw_ref, send_sems, recv_sems, local_sem)
        start()
        loc = pltpu.make_async_copy(c_ref, gc_ref.at[me], clocal)
        loc.start()

        def conv_copy(k, slot):
            px, py = chips[k]
            return pltpu.make_async_remote_copy(src_ref=c_ref, dst_ref=gc_ref.at[slot], send_sem=csend.at[k],
                                                recv_sem=crecv.at[k], device_id=(px, py, c), device_id_type=MESH)

        for k in range(3):
            conv_copy(k, me).start()
        forward()
        finish()
        for k, (px, py) in enumerate(chips):
            conv_copy(k, 2 * px + py).wait_recv()
        for k in range(3):
            conv_copy(k, me).wait_send()
        loc.wait()

    anyspec = pl.BlockSpec(memory_space=pl.ANY)
    return pl.pallas_call(
        body, name="gather_first",
        in_specs=[anyspec, anyspec], out_specs=[anyspec, anyspec],
        out_shape=[jax.ShapeDtypeStruct((4,) + wsrc.shape, wsrc.dtype), jax.ShapeDtypeStruct((4,) + cpack.shape, cpack.dtype)],
        scratch_shapes=GATHER_SCRATCH + [pltpu.SemaphoreType.DMA((3,)), pltpu.SemaphoreType.DMA((3,)), pltpu.SemaphoreType.DMA],
        compiler_params=_params(has_side_effects=True),
    )(wsrc, cpack)


def _all_devices_exchange(s):
    def make(ins, outs, sems):
        s_ref, o_ref = ins[0], outs[0]
        send_sems, recv_sems, local_sem = sems
        x, y, c = _pos()
        me = 4 * x + 2 * y + c
        loc = pltpu.make_async_copy(s_ref, o_ref.at[me], local_sem)

        def copy(k, slot):
            peer = (x ^ (k >> 2), y ^ ((k >> 1) & 1), c ^ (k & 1))
            return pltpu.make_async_remote_copy(src_ref=s_ref, dst_ref=o_ref.at[slot], send_sem=send_sems.at[k - 1],
                                                recv_sem=recv_sems.at[k - 1], device_id=peer, device_id_type=MESH)

        def start():
            loc.start()
            for k in range(1, 8):
                copy(k, me).start()

        def finish():
            for k in range(1, 8):
                copy(k, 4 * (x ^ (k >> 2)) + 2 * (y ^ ((k >> 1) & 1)) + (c ^ (k & 1))).wait_recv()
            for k in range(1, 8):
                copy(k, me).wait_send()
            loc.wait()

        return start, lambda: None, finish

    return _Exchange([s], [jax.ShapeDtypeStruct((8,) + s.shape, s.dtype)],
                     [pltpu.SemaphoreType.DMA((7,)), pltpu.SemaphoreType.DMA((7,)), pltpu.SemaphoreType.DMA], make)


def _sum_devices(a):
    def body(a_ref, o_ref):
        acc = a_ref[0]
        for d in range(1, 8):
            acc = acc + a_ref[d]
        o_ref[...] = acc

    vm = pl.BlockSpec(memory_space=pltpu.VMEM)
    return pl.pallas_call(body, name="sum_devices", in_specs=[vm], out_specs=vm,
                          out_shape=jax.ShapeDtypeStruct(a.shape[1:], F32), compiler_params=_params())(a)


def _swap_exchange(gs):
    n = len(gs)

    def make(ins, outs, sems):
        x, y, c = _pos()
        cps = []
        for i in range(n):
            half = gs[i].shape[1] // 2
            rows = pl.ds(pl.multiple_of((1 - c) * half, 8), half)
            cps.append(pltpu.make_async_remote_copy(src_ref=ins[i].at[:, rows, :], dst_ref=outs[i], send_sem=sems[0].at[i],
                                                    recv_sem=sems[1].at[i], device_id=(x, y, 1 - c), device_id_type=MESH))

        def start():
            for cp in cps:
                cp.start()

        def finish():
            for cp in cps:
                cp.wait()

        return start, lambda: None, finish

    return _Exchange(gs, [jax.ShapeDtypeStruct((4, g.shape[1] // 2, g.shape[2]), g.dtype) for g in gs],
                     [pltpu.SemaphoreType.DMA((n,)), pltpu.SemaphoreType.DMA((n,))], make)


def _scatter_exchange(ss):
    n = len(ss)

    def make(ins, outs, sems):
        send_sems, recv_sems, local_sems = sems
        x, y, c = _pos()
        me = 2 * x + y
        chips = _other_chips(x, y)
        locs = [pltpu.make_async_copy(ins[i].at[me], outs[i].at[me], local_sems.at[i]) for i in range(n)]

        def copy(i, k, src_slot, dst_slot):
            px, py = chips[k]
            return pltpu.make_async_remote_copy(src_ref=ins[i].at[src_slot], dst_ref=outs[i].at[dst_slot],
                                                send_sem=send_sems.at[3 * i + k], recv_sem=recv_sems.at[3 * i + k],
                                                device_id=(px, py, c), device_id_type=MESH)

        def start():
            for i in range(n):
                locs[i].start()
                for k, (px, py) in enumerate(chips):
                    copy(i, k, 2 * px + py, me).start()

        def finish():
            for i in range(n):
                for k, (px, py) in enumerate(chips):
                    copy(i, k, me, 2 * px + py).wait_recv()
            for i in range(n):
                for k, (px, py) in enumerate(chips):
                    copy(i, k, 2 * px + py, me).wait_send()
                locs[i].wait()

        return start, lambda: None, finish

    return _Exchange(ss, [jax.ShapeDtypeStruct(s.shape, s.dtype) for s in ss],
                     [pltpu.SemaphoreType.DMA((3 * n,)), pltpu.SemaphoreType.DMA((3 * n,)), pltpu.SemaphoreType.DMA((n,))], make)


def _send_exchange(rs):
    n = len(rs)

    def make(ins, outs, sems):
        x, y, c = _pos()
        cps = [pltpu.make_async_remote_copy(src_ref=ins[i], dst_ref=outs[i], send_sem=sems[0].at[i], recv_sem=sems[1].at[i],
                                            device_id=(x, y, 1 - c), device_id_type=MESH) for i in range(n)]

        def start():
            for cp in cps:
                cp.start()

        def finish():
            for cp in cps:
                cp.wait()

        return start, lambda: None, finish

    return _Exchange(rs, [jax.ShapeDtypeStruct(r.shape, r.dtype) for r in rs],
                     [pltpu.SemaphoreType.DMA((n,)), pltpu.SemaphoreType.DMA((n,))], make)


def _run_exchange(ex, name):
    ei, eo = len(ex.args), len(ex.out_shape)

    def body(*refs):
        start, forward, finish = ex.make(refs[:ei], refs[ei:ei + eo], refs[ei + eo:])
        start()
        forward()
        finish()

    anyspec = pl.BlockSpec(memory_space=pl.ANY)
    return pl.pallas_call(body, name=name, in_specs=[anyspec] * ei, out_specs=[anyspec] * eo, out_shape=ex.out_shape,
                          scratch_shapes=ex.scratch, compiler_params=_params(has_side_effects=True))(*ex.args)


def _add_half(gs, rs, c, name):
    n = len(gs)

    def body(c_ref, *refs):
        for g_ref, r_ref, o_ref in zip(refs[:n], refs[n:2 * n], refs[2 * n:]):
            o_ref[...] = (g_ref[...] + r_ref[...]).astype(BF16)

    g_specs, r_specs, out_shape = [], [], []
    for g, r in zip(gs, rs):
        _, H, C = r.shape
        tr = H // 2
        assert tr % 16 == 0 and g.shape == (4, 2 * H, C)
        g_specs.append(pl.BlockSpec((1, tr, C), lambda j, i, c_ref: (j, c_ref[0] * 2 + i, 0)))
        r_specs.append(pl.BlockSpec((1, tr, C), lambda j, i, c_ref: (j, i, 0)))
        out_shape.append(jax.ShapeDtypeStruct((4, H, C), BF16))
    grid_spec = pltpu.PrefetchScalarGridSpec(num_scalar_prefetch=1, grid=(4, 2), in_specs=g_specs + r_specs, out_specs=r_specs)
    return pl.pallas_call(body, name=name, grid_spec=grid_spec, out_shape=out_shape, compiler_params=_params())(c, *gs, *rs)


def _block_diag(w):
    eye = jnp.eye(RNN_BLOCKS, dtype=w.dtype)
    return (eye[:, None, :, None] * w[:, :, None, :]).reshape(D_RNN, D_RNN)


def _diag_blocks(wd):
    d = wd.reshape(RNN_BLOCKS, 64, RNN_BLOCKS, 64)
    return jnp.stack([d[h, :, h, :] for h in range(RNN_BLOCKS)])


def _split_pack(a, first, last):
    out, base = {}, PACK_OFF[first]
    for i in range(first, last):
        s = a[:, PACK_OFF[i] - base:PACK_OFF[i + 1] - base]
        out[BIG_KEYS[i]] = s.reshape(4 * 256, 256) if BIG_KEYS[i] == "w_p_t" else s.reshape(-1, 1024)
    return out


def _layer_grads(x, p, tgt, gw, small, shard=None, core=None):
    row = lambda v: v.reshape(1, -1)
    wa = _block_diag(small["gate_a_w"]).astype(MXU_DTYPE)
    wx = _block_diag(small["gate_x_w"]).astype(MXU_DTYPE)
    sinks = small["attn_sinks"].reshape(1, HEADS)

    dist = shard is not None
    q, kv, xr, gr, xb = _in_proj(x, gw["w_in_t"])
    cut = PACK_OFF[1] + PACK_ROWS[1] // 2
    att, *ga = _attn_fwd(q, kv, sinks, _gather_exchange(shard[PACK_OFF[1]:cut]) if dist else None)
    xc, h, rec, *gb = _rnn_fwd(xr, gr, small["rnn_conv_w"], row(small["rnn_conv_b"]), wa, row(small["gate_a_b"]),
                               wx, row(small["gate_x_b"]), row(small["lru_lambda"]),
                               _gather_exchange(shard[cut:PACK_OFF[3]]) if dist else None)
    if dist:
        gw = {**gw, **_split_pack(jnp.concatenate([ga[0], gb[0]], axis=1), 1, 3)}
    g1, b1 = row(small["ln1_g"]), row(small["ln1_b"])
    fcw = small["ffn_conv_w"].reshape(3, NC, FF_CHUNK).transpose(1, 0, 2)
    fcb = small["ffn_conv_b"].reshape(NC, 1, FF_CHUNK)
    z1, h1b = _out_proj(att, rec, x, gw["w_out"], g1, b1)
    gate, ge, vd, act, *gc = _ffn_up(h1b, gw["w_up_t"], fcw, fcb,
                                     _gather_exchange(shard[PACK_OFF[3]:PACK_OFF[6]]) if dist else None)
    if dist:
        gw = {**gw, **_split_pack(gc[0], 3, 6)}
    dz2, dz2b, dpre, dpp, vec2 = _ffn_down(act, z1, p, tgt, gw["w_down"], gw["w_g"], gw["w_p_t"], g1, b1,
                                           row(small["ln2_g"]), row(small["ln2_b"]), row(small["ple_gate_b"]))
    dup, dfc = _ffn_bwd(dz2b, gate, ge, vd, gw["w_down"], fcw)
    dz1, vec1 = _ffn_dh1(dup, dz2, dpre, z1, gw["w_up_t"], gw["w_g"], g1, b1)
    per_chip = 2 * D_FF // 4 // FF_CHUNK
    big = {
        "w_ffn_up": _weight_grad_cols(
            h1b, dup.reshape(2 * NC, -1, FF_CHUNK), "dw_up", 2 * NC,
            lambda bt: pl.BlockSpec((None, bt, FF_CHUNK), lambda m, k: (m, k, 0)), (4, D, 2 * D_FF // 4),
            pl.BlockSpec((None, D, FF_CHUNK), lambda m, k: (2 * (m % 2) + (m // 2) // per_chip, 0, (m // 2) % per_chip)))[0],
        "w_ffn_down": _weight_grad(act, dz2b, 512, "dw_down").reshape(4, D_FF // 4, D),
        "ple_gate_w": _weight_grad(h1b, dpre, 512, "dw_gate").reshape(4, D // 4, D),
        "ple_proj": _weight_grad_cols(
            p.astype(BF16), dpp, "dw_proj", 4, lambda bt: pl.BlockSpec((bt, D // 4), lambda j, k: (k, j)),
            (4, PLE, D // 4), pl.BlockSpec((None, PLE, D // 4), lambda j, k: (j, 0, 0)))[0],
        "w_out": _weight_grad(jnp.concatenate([att, rec], axis=1), dz1, 512, "dw_out").reshape(4, D // 4, D),
    }
    reduced = None
    if dist:
        g_ffn = [big[k] for k in EARLY_WEIGHTS]
        ex = _swap_exchange(g_ffn)
    datt, drec, *got = _out_proj_bwd(dz1, gw["w_out"], ex if dist else None)
    if dist:
        ex = _scatter_exchange(_add_half(g_ffn, got, core, "add_half_ffn"))
    dxr, dgr, dwa, dwx, dvec, *got = _rnn_bwd(drec, gr, h, xc, xr, small["rnn_conv_w"], wa, row(small["gate_a_b"]),
                                              wx, row(small["gate_x_b"]), row(small["lru_lambda"]), ex if dist else None)
    if dist:
        mine = _add4(got, "add_chips_ffn")
        ex = _send_exchange(mine)
    dq, dkv, dsinks, *got = _attn_bwd(q, kv, datt, sinks, ex if dist else None)
    if dist:
        reduced = (mine, got)
        big = {}
    sg = {
        "attn_sinks": dsinks[:, 0],
        "rnn_conv_w": dvec[4:8],
        "rnn_conv_b": dvec[3],
        "gate_a_w": _diag_blocks(dwa),
        "gate_a_b": dvec[0],
        "gate_x_w": _diag_blocks(dwx),
        "gate_x_b": dvec[1],
        "lru_lambda": dvec[2],
        "ln1_g": vec1[0],
        "ln1_b": vec1[1],
        "ffn_conv_w": dfc[:, 0:3].transpose(1, 0, 2).reshape(3, D_FF),
        "ffn_conv_b": dfc[:, 3].reshape(D_FF),
        "ple_gate_b": vec2[3],
        "ln2_g": vec2[1],
        "ln2_b": vec2[2],
    }
    loss = vec2[0, 0:1]
    grad_x, du = _in_proj_bwd(dq, dkv, dxr, dgr, dz1, gw["w_in_t"])
    ex = _all_devices_exchange(_pack_vecs([sg[k] for k in SMALL] + [loss])[0]) if dist else None
    big["w_in"], *small_all = _weight_grad_cols(
        xb, du, "dw_in", 4, lambda bt: pl.BlockSpec((None, bt, D_IN // 4), lambda j, k: (j, k, 0)), (4, D, D_IN // 4),
        pl.BlockSpec((None, D, D_IN // 4), lambda j, k: (j, 0, 0)), ex)
    return grad_x, big, sg, loss, reduced, small_all


BIG = ("w_in", "w_ffn_up", "w_out", "w_ffn_down", "ple_gate_w", "ple_proj")
BIG_KEYS = ("w_in_t", "w_up_t", "w_out", "w_down", "w_g", "w_p_t")
BIG_T = (True, True, False, False, False, True)
EARLY_WEIGHTS = ("w_ffn_up", "w_ffn_down", "ple_gate_w", "ple_proj", "w_out")
LATE_WEIGHTS = ("w_in",)
SMALL = ("attn_sinks", "rnn_conv_w", "rnn_conv_b", "gate_a_w", "gate_a_b", "gate_x_w", "gate_x_b", "lru_lambda",
         "ln1_g", "ln1_b", "ffn_conv_w", "ffn_conv_b", "ple_gate_b", "ln2_g", "ln2_b")
SHARDED_SMALL = ("rnn_conv_w", "ffn_conv_w")
WEIGHTS = ("w_in", "attn_sinks", "rnn_conv_w", "rnn_conv_b", "gate_a_w", "gate_a_b", "gate_x_w", "gate_x_b",
           "lru_lambda", "w_out", "ln1_g", "ln1_b", "w_ffn_up", "ffn_conv_w", "ffn_conv_b", "w_ffn_down",
           "ple_gate_w", "ple_gate_b", "ple_proj", "ln2_g", "ln2_b")


def _pack_big(d, first=0, last=6):
    parts = []
    for name, t in zip(BIG[first:last], BIG_T[first:last]):
        a = d[name]
        a = a.T if t else a
        parts.append(a.reshape(-1, 1024))
    return jnp.concatenate(parts, axis=0)


def _pack_vecs(items):
    parts, offs, n = [], [], 0
    for a in items:
        f = a.reshape(-1).astype(F32)
        pad = (-f.shape[0]) % 128
        parts.append(jnp.pad(f, (0, pad)))
        offs.append(n)
        n += (f.shape[0] + pad) // 128
    padr = (-n) % 8
    if padr:
        parts.append(jnp.zeros((padr * 128,), F32))
    return jnp.concatenate(parts).reshape(-1, 128), offs


def _unpack_vecs(a, offs, shapes):
    flat = a.reshape(-1)
    out = []
    for o, s in zip(offs, shapes):
        n = 1
        for d in s:
            n *= d
        out.append(flat[o * 128:o * 128 + n].reshape(s))
    return out


def kernel(x, p, w_in, attn_sinks, rnn_conv_w, rnn_conv_b, gate_a_w, gate_a_b, gate_x_w, gate_x_b, lru_lambda, w_out, ln1_g, ln1_b, w_ffn_up, ffn_conv_w, ffn_conv_b, w_ffn_down, ple_gate_w, ple_gate_b, ple_proj, ln2_g, ln2_b, loss_target, m_w_in, m_attn_sinks, m_rnn_conv_w, m_rnn_conv_b, m_gate_a_w, m_gate_a_b, m_gate_x_w, m_gate_x_b, m_lru_lambda, m_w_out, m_ln1_g, m_ln1_b, m_w_ffn_up, m_ffn_conv_w, m_ffn_conv_b, m_w_ffn_down, m_ple_gate_w, m_ple_gate_b, m_ple_proj, m_ln2_g, m_ln2_b, v_w_in, v_attn_sinks, v_rnn_conv_w, v_rnn_conv_b, v_gate_a_w, v_gate_a_b, v_gate_x_w, v_gate_x_b, v_lru_lambda, v_w_out, v_ln1_g, v_ln1_b, v_w_ffn_up, v_ffn_conv_w, v_ffn_conv_b, v_w_ffn_down, v_ple_gate_w, v_ple_gate_b, v_ple_proj, v_ln2_g, v_ln2_b):
    w = dict(w_in=w_in, attn_sinks=attn_sinks, rnn_conv_w=rnn_conv_w, rnn_conv_b=rnn_conv_b, gate_a_w=gate_a_w,
             gate_a_b=gate_a_b, gate_x_w=gate_x_w, gate_x_b=gate_x_b, lru_lambda=lru_lambda, w_out=w_out, ln1_g=ln1_g,
             ln1_b=ln1_b, w_ffn_up=w_ffn_up, ffn_conv_w=ffn_conv_w, ffn_conv_b=ffn_conv_b, w_ffn_down=w_ffn_down,
             ple_gate_w=ple_gate_w, ple_gate_b=ple_gate_b, ple_proj=ple_proj, ln2_g=ln2_g, ln2_b=ln2_b)
    m = dict(w_in=m_w_in, attn_sinks=m_attn_sinks, rnn_conv_w=m_rnn_conv_w, rnn_conv_b=m_rnn_conv_b, gate_a_w=m_gate_a_w,
             gate_a_b=m_gate_a_b, gate_x_w=m_gate_x_w, gate_x_b=m_gate_x_b, lru_lambda=m_lru_lambda, w_out=m_w_out,
             ln1_g=m_ln1_g, ln1_b=m_ln1_b, w_ffn_up=m_w_ffn_up, ffn_conv_w=m_ffn_conv_w, ffn_conv_b=m_ffn_conv_b,
             w_ffn_down=m_w_ffn_down, ple_gate_w=m_ple_gate_w, ple_gate_b=m_ple_gate_b, ple_proj=m_ple_proj,
             ln2_g=m_ln2_g, ln2_b=m_ln2_b)
    v = dict(w_in=v_w_in, attn_sinks=v_attn_sinks, rnn_conv_w=v_rnn_conv_w, rnn_conv_b=v_rnn_conv_b, gate_a_w=v_gate_a_w,
             gate_a_b=v_gate_a_b, gate_x_w=v_gate_x_w, gate_x_b=v_gate_x_b, lru_lambda=v_lru_lambda, w_out=v_w_out,
             ln1_g=v_ln1_g, ln1_b=v_ln1_b, w_ffn_up=v_w_ffn_up, ffn_conv_w=v_ffn_conv_w, ffn_conv_b=v_ffn_conv_b,
             w_ffn_down=v_w_ffn_down, ple_gate_w=v_ple_gate_w, ple_gate_b=v_ple_gate_b, ple_proj=v_ple_proj,
             ln2_g=v_ln2_g, ln2_b=v_ln2_b)
    w, m, v = ({k: a[0] for k, a in d.items()} for d in (w, m, v))
    chip = 2 * lax.axis_index("x") + lax.axis_index("y")
    core = lax.axis_index("c")

    wpack = _pack_big(w)
    cpack, _ = _pack_vecs([w["rnn_conv_w"], w["ffn_conv_w"]])
    shard = wpack.astype(MXU_DTYPE)
    g_in, gcp = _gather_first(shard[PACK_OFF[0]:PACK_OFF[1]], cpack)
    gw = _split_pack(g_in, 0, 1)
    small = {k: w[k] for k in SMALL}
    small["rnn_conv_w"] = gcp[:, 0:4].reshape(4, 4, 128).transpose(1, 0, 2).reshape(4, 512)
    small["ffn_conv_w"] = gcp[:, 4:22].reshape(4, 3, 768).transpose(1, 0, 2).reshape(3, 3072)

    core1 = core.reshape(1).astype(jnp.int32)
    grad_x, big, sg, loss, ffn_halves, small_all = _layer_grads(x[0], p[0, 0], loss_target[0], gw, small, shard, core1)

    shapes = [sg[k].shape for k in SMALL] + [(1,)]
    _, offs = _pack_vecs([jnp.zeros(s, F32) for s in shapes])
    red = dict(zip(SMALL + ("loss",), _unpack_vecs(_sum_devices(small_all[0]), offs, shapes)))
    red["rnn_conv_w"] = lax.dynamic_slice_in_dim(red["rnn_conv_w"], chip * 128, 128, axis=1)
    red["ffn_conv_w"] = lax.dynamic_slice_in_dim(red["ffn_conv_w"], chip * 768, 768, axis=1)

    g_late = [big[k] for k in LATE_WEIGHTS]
    sib = _run_exchange(_swap_exchange(g_late), "swap_late")
    from_chips = _run_exchange(_scatter_exchange(_add_half(g_late, sib, core1, "add_half_late")), "scatter_late")
    late_mine = _add4(from_chips, "add_chips_late")
    late_other = _run_exchange(_send_exchange(late_mine), "send_late")

    def adamw(names, mine, other, name):
        out, _ = _adamw_halves([w[k] for k in names], mine, other, [m[k] for k in names], [v[k] for k in names],
                               core1, name)
        return dict(zip(names, out))

    big_out = {**adamw(LATE_WEIGHTS, late_mine, late_other, "adamw_late"), **adamw(EARLY_WEIGHTS, *ffn_halves, "adamw_early")}
    wsm, offs2 = _pack_vecs([w[k] for k in SMALL])
    gsm, _ = _pack_vecs([red[k] for k in SMALL])
    msm, _ = _pack_vecs([m[k] for k in SMALL])
    vsm, _ = _pack_vecs([v[k] for k in SMALL])
    dsm, nmsm, nvsm = _adamw(wsm, gsm, msm, vsm, "adamw_small")
    shapes2 = [w[k].shape for k in SMALL]

    def named(n, smallp):
        d = {k: out[n][None] for k, out in big_out.items()}
        d.update({k: a[None] for k, a in zip(SMALL, _unpack_vecs(smallp, offs2, shapes2))})
        return [d[k] for k in WEIGHTS]

    return (red["loss"].reshape(()), grad_x[None], *named(0, gsm), *named(1, dsm), *named(2, nmsm), *named(3, nvsm))
```

```python
import functools

import jax
import jax.numpy as jnp
from jax import lax
from jax.experimental import pallas as pl
from jax.experimental.pallas import tpu as pltpu

F32 = jnp.float32
BF16 = jnp.bfloat16
MXU_DTYPE = jnp.bfloat16

D = 1024
D_ATT = 512
D_KV = 128
D_RNN = 512
D_IN = 1792
D_FF = 3072
FF_CHUNK = 512
PLE = 256
HEADS = 8
HEAD_DIM = 64
BLK = 128
RNN_BLOCKS = 8
LN_EPS = 1e-5
LRU_C = 8.0
ALPHA = float(2.0 ** 0.25)
SCALE = HEAD_DIM ** -0.5
NEG = -1e30

ADAM_LR = 0.001
ADAM_B1 = 0.9
ADAM_B2 = 0.999
ADAM_EPS = 1e-08
ADAM_WD = 0.01
ADAM_STEP = 10

VMEM_LIMIT_BYTES = 56 * 1024 * 1024
MESH = pl.DeviceIdType.MESH

PACK_ROWS = (448, 1536, 256, 768, 256, 64)
PACK_OFF = tuple(sum(PACK_ROWS[:i]) for i in range(len(PACK_ROWS) + 1))
PACK_TOTAL = PACK_OFF[-1]


def _params(**kw):
    return pltpu.CompilerParams(vmem_limit_bytes=VMEM_LIMIT_BYTES, **kw)


def _mm(a, b):
    return jnp.dot(a.astype(MXU_DTYPE), b.astype(MXU_DTYPE), preferred_element_type=F32)


def _mm_nt(a, b):
    return lax.dot_general(a.astype(MXU_DTYPE), b.astype(MXU_DTYPE), (((1,), (1,)), ((), ())),
                           preferred_element_type=F32)


def _mm_tn(a, b):
    return lax.dot_general(a.astype(MXU_DTYPE), b.astype(MXU_DTYPE), (((0,), (0,)), ((), ())),
                           preferred_element_type=F32)


def _sigmoid(x):
    return 1.0 / (1.0 + jnp.exp(-x))


def _gelu(x):
    c = 0.7978845608028654
    k = 0.044715
    x2 = x * x
    t = jnp.tanh(x * (c + (c * k) * x2))
    h = 0.5 * (1.0 + t)
    return x * h, h * (1.0 + (x * (1.0 - t)) * (c + (3.0 * c * k) * x2))


def _shift_rows(x, s, edge8):
    R = x.shape[0]
    row8 = lax.broadcasted_iota(jnp.int32, (8, x.shape[1]), 0)
    if s > 0:
        rolled = pltpu.roll(x, s, 0)
        first = jnp.where(row8 < s, pltpu.roll(edge8, s, 0), rolled[0:8])
        return jnp.concatenate([first, rolled[8:]], axis=0)
    k = -s
    rolled = pltpu.roll(x, R - k, 0)
    last = jnp.where(row8 >= 8 - k, pltpu.roll(edge8, 8 - k, 0), rolled[R - 8:])
    return jnp.concatenate([rolled[:R - 8], last], axis=0)


def _softplus(x):
    return jnp.maximum(x, 0.0) + jnp.log(1.0 + jnp.exp(-jnp.abs(x)))


def _ln(z, g, b):
    mu = jnp.mean(z, axis=-1, keepdims=True)
    zc = z - mu
    var = jnp.mean(zc * zc, axis=-1, keepdims=True)
    rstd = lax.rsqrt(var + LN_EPS)
    xhat = zc * rstd
    return xhat * g + b, xhat, rstd


def _ln_bwd(dy, xhat, rstd, g):
    dxh = dy * g
    m1 = jnp.mean(dxh, axis=-1, keepdims=True)
    m2 = jnp.mean(dxh * xhat, axis=-1, keepdims=True)
    return rstd * (dxh - m1 - xhat * m2)


def _colsum(x):
    return jnp.sum(x, axis=0, keepdims=True)


def _full(shape):
    nd = len(shape)
    return pl.BlockSpec(shape, lambda *_: (0,) * nd)


def _rows(tm, cols, fn=None):
    if fn is None:
        return pl.BlockSpec((tm, cols), lambda i: (i, 0))
    return pl.BlockSpec((tm, cols), lambda i: (fn(i), 0))


def _heads(tm):
    return pl.BlockSpec((HEADS, tm, HEAD_DIM), lambda i: (0, i, 0))


def _in_proj(x, w_in_t):
    T = x.shape[0]
    tm = 512

    def body(x_ref, w_ref, q_ref, kv_ref, xr_ref, gr_ref, xb_ref):
        xb = x_ref[...].astype(MXU_DTYPE)
        xb_ref[...] = xb.astype(BF16)
        q = _mm_nt(xb, w_ref[0:512, :])
        for h in range(HEADS):
            q_ref[h] = q[:, h * 64:(h + 1) * 64].astype(BF16)
        kv_ref[...] = _mm_nt(xb, w_ref[512:768, :]).astype(BF16)
        xr_ref[...] = _mm_nt(xb, w_ref[768:1280, :])
        gr_ref[...] = _mm_nt(xb, w_ref[1280:1792, :])

    return pl.pallas_call(
        body, name="in_proj", grid=(T // tm,),
        in_specs=[_rows(tm, D), _full((D_IN, D))],
        out_specs=[_heads(tm), _rows(tm, 256), _rows(tm, 512), _rows(tm, 512), _rows(tm, D)],
        out_shape=[jax.ShapeDtypeStruct((HEADS, T, 64), BF16), jax.ShapeDtypeStruct((T, 256), BF16),
                   jax.ShapeDtypeStruct((T, 512), F32), jax.ShapeDtypeStruct((T, 512), F32),
                   jax.ShapeDtypeStruct((T, D), BF16)],
        compiler_params=_params(),
    )(x, w_in_t)


def _attn_band(kv_ref, i):
    cur = pl.multiple_of(i * BLK, BLK)
    prev = pl.multiple_of(jnp.maximum(i - 1, 0) * BLK, BLK)
    band = jnp.concatenate([kv_ref[pl.ds(prev, BLK), :], kv_ref[pl.ds(cur, BLK), :]], axis=0)
    key = lax.broadcasted_iota(jnp.int32, (2 * BLK, 4 * BLK), 0)
    qry = lax.broadcasted_iota(jnp.int32, (2 * BLK, 4 * BLK), 1) & (BLK - 1)
    in_prev = jnp.logical_and(jnp.logical_and(key < BLK, key > qry), i > 0)
    mask = jnp.logical_or(in_prev, jnp.logical_and(key >= BLK, key - BLK <= qry))
    return band, mask, cur, prev


def _attn_scores(band, mask, qs, s_ref, g):
    st = jnp.where(mask, _mm_nt(band[:, g * 64:(g + 1) * 64], qs) * SCALE, NEG)
    lane = lax.broadcasted_iota(jnp.int32, (1, 4 * BLK), 1)
    sv = jnp.where(lane < BLK, s_ref[0, 4 * g],
                   jnp.where(lane < 2 * BLK, s_ref[0, 4 * g + 1], jnp.where(lane < 3 * BLK, s_ref[0, 4 * g + 2], s_ref[0, 4 * g + 3])))
    m = jnp.maximum(jnp.max(st, axis=0, keepdims=True), sv)
    p = jnp.exp(st - m)
    ps = jnp.exp(sv - m)
    return p, ps, jnp.sum(p, axis=0, keepdims=True) + ps


def _pos():
    return lax.axis_index("x"), lax.axis_index("y"), lax.axis_index("c")


def _other_chips(x, y):
    return [(1 - x, y), (x, 1 - y), (1 - x, 1 - y)]


def _gather_steps(w_ref, gw_ref, send_sems, recv_sems, local_sem):
    x, y, c = _pos()
    me = 2 * x + y
    chips = _other_chips(x, y)
    half = w_ref.shape[0] // 2
    mine = pl.ds(pl.multiple_of(c * half, 16), half)
    theirs = pl.ds(pl.multiple_of((1 - c) * half, 16), half)
    loc = pltpu.make_async_copy(w_ref, gw_ref.at[me], local_sem)

    def copy(k, src, dst, to):
        return pltpu.make_async_remote_copy(src_ref=src, dst_ref=dst, send_sem=send_sems.at[k], recv_sem=recv_sems.at[k],
                                            device_id=to, device_id_type=MESH)

    def out(k):
        px, py = chips[k]
        return copy(k, w_ref.at[mine], gw_ref.at[me, mine], (px, py, c))

    def fwd(k, rows):
        px, py = chips[k]
        return copy(3 + k, gw_ref.at[2 * px + py, rows], gw_ref.at[2 * px + py, rows], (x, y, 1 - c))

    def start():
        loc.start()
        for k in range(3):
            out(k).start()

    def forward():
        for k in range(3):
            px, py = chips[k]
            copy(k, w_ref.at[mine], gw_ref.at[2 * px + py, mine], (px, py, c)).wait_recv()
            fwd(k, mine).start()

    def finish():
        for k in range(3):
            fwd(k, theirs).wait_recv()
        for k in range(3):
            out(k).wait_send()
            fwd(k, mine).wait_send()
        loc.wait()

    return start, forward, finish


GATHER_SCRATCH = [pltpu.SemaphoreType.DMA((6,)), pltpu.SemaphoreType.DMA((6,)), pltpu.SemaphoreType.DMA]


class _Exchange:
    def __init__(self, args, out_shape, scratch, make):
        self.args, self.out_shape, self.scratch, self.make = list(args), list(out_shape), list(scratch), make


def _gather_exchange(wsrc):
    return _Exchange([wsrc], [jax.ShapeDtypeStruct((4,) + wsrc.shape, wsrc.dtype)], GATHER_SCRATCH,
                     lambda ins, outs, sems: _gather_steps(ins[0], outs[0], *sems))


def _launch(body, name, grid, in_specs, out_specs, out_shape, scratch, args, exchange=None, prefetch=0):
    def call(fn, fn_name, ins, outs, shapes, scr, operands, effects):
        spec = pltpu.PrefetchScalarGridSpec(num_scalar_prefetch=prefetch, grid=grid, in_specs=ins, out_specs=outs,
                                            scratch_shapes=scr)
        return pl.pallas_call(fn, name=fn_name, grid_spec=spec, out_shape=shapes,
                              compiler_params=_params(has_side_effects=effects))(*operands)

    if exchange is None:
        return call(body, name, list(in_specs), list(out_specs), list(out_shape), list(scratch), args, False)
    n_in, n_out, ei, eo, ns = len(in_specs), len(out_specs), len(exchange.args), len(exchange.out_shape), len(exchange.scratch)
    nsteps = 1
    for g in grid:
        nsteps *= g

    def wrapped(*refs):
        scalars, refs = refs[:prefetch], refs[prefetch:]
        ins, xin = refs[:n_in], refs[n_in:n_in + ei]
        outs, xout = refs[n_in + ei:n_in + ei + n_out], refs[n_in + ei + n_out:n_in + ei + n_out + eo]
        rest = refs[n_in + ei + n_out + eo:]
        own, sems = rest[:len(rest) - ns], rest[len(rest) - ns:]
        start, forward, finish = exchange.make(xin, xout, sems)
        i = pl.program_id(0)
        for d in range(1, len(grid)):
            i = i * grid[d] + pl.program_id(d)
        pl.when(i == 0)(start)
        body(*scalars, *ins, *outs, *own)
        pl.when(i == max(nsteps - 3, 0))(forward)
        pl.when(i == nsteps - 1)(finish)

    anyspec = pl.BlockSpec(memory_space=pl.ANY)
    return call(wrapped, name + "_x", list(in_specs) + [anyspec] * ei, list(out_specs) + [anyspec] * eo,
                list(out_shape) + exchange.out_shape, list(scratch) + exchange.scratch, (*args, *exchange.args), True)


def _attn_fwd(q, kv, sinks, exchange=None):
    T = kv.shape[0]

    def body(q_ref, kv_ref, s_ref, o_ref):
        i = pl.program_id(0)
        band, mask, _, _ = _attn_band(kv_ref, i)
        for g in range(2):
            qs = q_ref[4 * g:4 * g + 4].reshape(4 * BLK, HEAD_DIM)
            p, _, den = _attn_scores(band, mask, qs, s_ref, g)
            ot = _mm_tn(band[:, 128:256], p) / den
            for hh in range(4):
                o = ot[:, hh * BLK:(hh + 1) * BLK].T
                o_ref[:, (4 * g + hh) * 64:(4 * g + hh + 1) * 64] = o[:, g * 64:(g + 1) * 64].astype(BF16)

    return _launch(body, "attn_fwd", (T // BLK,), [_heads(BLK), _full((T, 256)), pl.BlockSpec(memory_space=pltpu.SMEM)],
                   [_rows(BLK, 512)], [jax.ShapeDtypeStruct((T, 512), BF16)], [], (q, kv, sinks), exchange)


def _attn_bwd(q, kv, do, sinks, exchange=None):
    T = kv.shape[0]

    def body(q_ref, kv_ref, do_ref, s_ref, dq_ref, dkv_ref, ds_ref):
        i = pl.program_id(0)
        band, mask, cur, prev = _attn_band(kv_ref, i)

        @pl.when(i == 0)
        def _():
            ds_ref[...] = jnp.zeros_like(ds_ref)

        for g in range(2):
            qs = q_ref[4 * g:4 * g + 4].reshape(4 * BLK, HEAD_DIM)
            dos = do_ref[4 * g:4 * g + 4].reshape(4 * BLK, HEAD_DIM)
            p, ps, den = _attn_scores(band, mask, qs, s_ref, g)
            inv = 1.0 / den
            p = p * inv
            dpt = _mm_nt(band[:, 128 + g * 64:192 + g * 64], dos)
            delta = jnp.sum(p * dpt, axis=0, keepdims=True)
            dst = p * (dpt - delta)
            dsv = -(ps * inv) * delta
            for hh in range(4):
                dsink = jnp.sum(dsv[:, hh * BLK:(hh + 1) * BLK], axis=1, keepdims=True)
                ds_ref[4 * g + hh:4 * g + hh + 1, :] += jnp.broadcast_to(dsink, (1, 128))
            dqt = _mm_tn(band[:, 0:128], dst) * SCALE
            for hh in range(4):
                dqh = dqt[:, hh * BLK:(hh + 1) * BLK].T
                dq_ref[:, (4 * g + hh) * 64:(4 * g + hh + 1) * 64] = dqh[:, g * 64:(g + 1) * 64].astype(BF16)
            dk = _mm(dst, qs) * SCALE
            dv = _mm(p, dos)
            dkv_ref[pl.ds(cur, BLK), g * 64:(g + 1) * 64] = dk[BLK:2 * BLK]
            dkv_ref[pl.ds(cur, BLK), 128 + g * 64:192 + g * 64] = dv[BLK:2 * BLK]
            dkv_ref[pl.ds(prev, BLK), g * 64:(g + 1) * 64] += dk[0:BLK]
            dkv_ref[pl.ds(prev, BLK), 128 + g * 64:192 + g * 64] += dv[0:BLK]

    return _launch(body, "attn_bwd", (T // BLK,),
                   [_heads(BLK), _full((T, 256)), _heads(BLK), pl.BlockSpec(memory_space=pltpu.SMEM)],
                   [_rows(BLK, 512), _full((T, 256)), _full((8, 128))],
                   [jax.ShapeDtypeStruct((T, 512), BF16), jax.ShapeDtypeStruct((T, 256), F32),
                    jax.ShapeDtypeStruct((8, 128), F32)], [], (q, kv, do, sinks), exchange)


def _rows8(tm, cols):
    return lax.broadcasted_iota(jnp.int32, (tm, cols), 0) & 7


def _lru_gates(xc, wa, ba, wx, bx, lam):
    r = _sigmoid(_mm(xc, wa) + ba)
    ii = _sigmoid(_mm(xc, wx) + bx)
    sp = _softplus(-lam)
    la = -LRU_C * r * sp
    a = jnp.exp(la)
    m = jnp.sqrt(-jnp.tanh(la) * (a * a + 1.0))
    return r, ii, sp, a, m


def _rnn_fwd(xr, gr, cw, cb, wa, ba, wx, bx, lam, exchange=None):
    T = xr.shape[0]
    tm = 256
    C = D_RNN

    def body(xr_ref, gr_ref, cw_ref, cb_ref, wa_ref, ba_ref, wx_ref, bx_ref, lam_ref,
             xc_ref, h_ref, rec_ref, ext, a_s, b_s, carry):
        i = pl.program_id(0)

        @pl.when(i == 0)
        def _():
            ext[...] = jnp.zeros((8, C), F32)
            carry[...] = jnp.zeros((8, C), F32)

        xr = xr_ref[...]
        edge = ext[...]
        xc = cb_ref[...] + cw_ref[3:4, :] * xr
        for k in range(3):
            xc = xc + cw_ref[k:k + 1, :] * _shift_rows(xr, 3 - k, edge)
        ext[...] = xr[tm - 8:tm, :]
        xc_ref[...] = xc
        _, ii, _, a, m = _lru_gates(xc, wa_ref[...], ba_ref[...], wx_ref[...], bx_ref[...], lam_ref[...])
        b = m * ii * xc
        r8 = _rows8(tm, C)
        for d in (1, 2, 4):
            ok = r8 >= d
            a_sh = jnp.where(ok, pltpu.roll(a, d, 0), 1.0)
            b_sh = jnp.where(ok, pltpu.roll(b, d, 0), 0.0)
            b = a * b_sh + b
            a = a * a_sh
        a_s[...] = a
        b_s[...] = b

        def step(g, hin):
            s = pl.multiple_of(g * 8, 8)
            hg = a_s[pl.ds(s, 8), :] * hin + b_s[pl.ds(s, 8), :]
            h_ref[pl.ds(s, 8), :] = hg
            return jnp.broadcast_to(hg[7:8, :], (8, C))

        carry[...] = lax.fori_loop(0, tm // 8, step, carry[...])
        ge, _ = _gelu(gr_ref[...])
        rec_ref[...] = (h_ref[...] * ge).astype(BF16)

    vec = _full((1, C))
    in_specs = [_rows(tm, C), _rows(tm, C), _full((4, C)), vec, _full((C, C)), vec, _full((C, C)), vec, vec]
    out_specs = [_rows(tm, C), _rows(tm, C), _rows(tm, C)]
    out_shape = [jax.ShapeDtypeStruct((T, C), F32), jax.ShapeDtypeStruct((T, C), F32), jax.ShapeDtypeStruct((T, C), BF16)]
    scratch = [pltpu.VMEM((8, C), F32), pltpu.VMEM((tm, C), F32), pltpu.VMEM((tm, C), F32), pltpu.VMEM((8, C), F32)]
    return _launch(body, "rnn_fwd", (T // tm,), in_specs, out_specs, out_shape, scratch,
                   (xr, gr, cw, cb, wa, ba, wx, bx, lam), exchange)


def _rnn_bwd(drec, gr, h, xc, xr, cw, wa, ba, wx, bx, lam, exchange=None):
    T = xr.shape[0]
    tm = 256
    C = D_RNN
    nt = T // tm
    t8 = tm // 8

    def body(drec_ref, gr_ref, h_ref, hp_ref, xc_ref, xr_ref, cw_ref, wa_ref, ba_ref, wx_ref, bx_ref,
             lam_ref, dxr_ref, dgr_ref, dwa_ref, dwx_ref, dvec_ref, c_s, g_s, gout, ext, anext, gcarry):
        i = pl.program_id(0)
        j = nt - 1 - i

        @pl.when(i == 0)
        def _():
            dwa_ref[...] = jnp.zeros_like(dwa_ref)
            dwx_ref[...] = jnp.zeros_like(dwx_ref)
            dvec_ref[...] = jnp.zeros_like(dvec_ref)
            anext[...] = jnp.zeros((8, C), F32)
            gcarry[...] = jnp.zeros((8, C), F32)
            ext[...] = jnp.zeros((8, C), F32)

        xc = xc_ref[...]
        lam = lam_ref[...]
        r, ii, sp, a, m = _lru_gates(xc, wa_ref[...], ba_ref[...], wx_ref[...], bx_ref[...], lam)
        ge, dge = _gelu(gr_ref[...])
        drec = drec_ref[...]
        hh = h_ref[...]
        dgr_ref[...] = (drec * hh * dge).astype(BF16)
        dh = drec * ge
        rowi = lax.broadcasted_iota(jnp.int32, (tm, C), 0)
        c = jnp.where(rowi == tm - 1, jnp.broadcast_to(anext[0:1, :], (tm, C)), pltpu.roll(a, tm - 1, 0))
        anext[...] = a[0:8, :]
        r8 = rowi & 7
        gg = dh
        for d in (1, 2, 4):
            ok = r8 < 8 - d
            c_sh = jnp.where(ok, pltpu.roll(c, tm - d, 0), 1.0)
            g_sh = jnp.where(ok, pltpu.roll(gg, tm - d, 0), 0.0)
            gg = c * g_sh + gg
            c = c * c_sh
        c_s[...] = c
        g_s[...] = gg

        def step(k, gin):
            s = pl.multiple_of((t8 - 1 - k) * 8, 8)
            og = c_s[pl.ds(s, 8), :] * gin + g_s[pl.ds(s, 8), :]
            gout[pl.ds(s, 8), :] = og
            return jnp.broadcast_to(og[0:1, :], (8, C))

        gcarry[...] = lax.fori_loop(0, t8, step, gcarry[...])
        G = gout[...]
        hprev_row = jnp.where(j > 0, hp_ref[7:8, :], 0.0)
        hprev = jnp.where(rowi == 0, jnp.broadcast_to(hprev_row, (tm, C)), pltpu.roll(hh, 1, 0))
        da = G * hprev
        dm = G * ii * xc
        di = G * m * xc
        dxc = G * m * ii
        dla = da * a - dm * a * a / m
        dr = dla * (-LRU_C * sp)
        dsp = _colsum(dla * (-LRU_C * r))
        dlam = dsp * (-_sigmoid(-lam))
        dpr = dr * r * (1.0 - r)
        dpi = di * ii * (1.0 - ii)
        dxc = dxc + _mm_nt(dpr, wa_ref[...]) + _mm_nt(dpi, wx_ref[...])
        dwa_ref[...] += _mm_tn(xc, dpr)
        dwx_ref[...] += _mm_tn(xc, dpi)
        dvec_ref[0:1, :] += _colsum(dpr)
        dvec_ref[1:2, :] += _colsum(dpi)
        dvec_ref[2:3, :] += dlam
        dvec_ref[3:4, :] += _colsum(dxc)
        edge = ext[...]
        xr = xr_ref[...]
        dxr = cw_ref[3:4, :] * dxc
        dvec_ref[7:8, :] += _colsum(dxc * xr)
        for k in range(3):
            up = _shift_rows(dxc, k - 3, edge)
            dxr = dxr + cw_ref[k:k + 1, :] * up
            dvec_ref[4 + k:5 + k, :] += _colsum(up * xr)
        ext[...] = dxc[0:8, :]
        dxr_ref[...] = dxr.astype(BF16)

    rev = lambda i: nt - 1 - i
    prev8 = lambda i: jnp.maximum((nt - 1 - i) * t8 - 1, 0)
    vec = _full((1, C))
    return _launch(
        body, "rnn_bwd", (nt,),
        [_rows(tm, C, rev), _rows(tm, C, rev), _rows(tm, C, rev), _rows(8, C, prev8), _rows(tm, C, rev),
         _rows(tm, C, rev), _full((4, C)), _full((C, C)), vec, _full((C, C)), vec, vec],
        [_rows(tm, C, rev), _rows(tm, C, rev), _full((C, C)), _full((C, C)), _full((8, C))],
        [jax.ShapeDtypeStruct((T, C), BF16), jax.ShapeDtypeStruct((T, C), BF16),
         jax.ShapeDtypeStruct((C, C), F32), jax.ShapeDtypeStruct((C, C), F32), jax.ShapeDtypeStruct((8, C), F32)],
        [pltpu.VMEM((tm, C), F32), pltpu.VMEM((tm, C), F32), pltpu.VMEM((tm, C), F32),
         pltpu.VMEM((8, C), F32), pltpu.VMEM((8, C), F32), pltpu.VMEM((8, C), F32)],
        (drec, gr, h, h, xc, xr, cw, wa, ba, wx, bx, lam), exchange)


def _out_proj(att, rec, x, w_out, g1, b1):
    T = x.shape[0]
    tm = 512

    def body(att_ref, rec_ref, x_ref, w_ref, g1_ref, b1_ref, z_ref, h_ref):
        mix = _mm(att_ref[...], w_ref[0:512, :]) + _mm(rec_ref[...], w_ref[512:1024, :])
        z1 = ALPHA * x_ref[...] + mix
        z_ref[...] = z1
        h1, _, _ = _ln(z1, g1_ref[...], b1_ref[...])
        h_ref[...] = h1.astype(MXU_DTYPE).astype(BF16)

    return pl.pallas_call(
        body, name="out_proj", grid=(T // tm,),
        in_specs=[_rows(tm, 512), _rows(tm, 512), _rows(tm, D), _full((D, D)), _full((1, D)), _full((1, D))],
        out_specs=[_rows(tm, D), _rows(tm, D)],
        out_shape=[jax.ShapeDtypeStruct((T, D), F32), jax.ShapeDtypeStruct((T, D), BF16)],
        compiler_params=_params(),
    )(att, rec, x, w_out, g1, b1)


NC = D_FF // FF_CHUNK


def _ffn_up(h1b, w_up_t, fcw, fcb, exchange=None):
    T = h1b.shape[0]
    tm = min(1024, T)
    CW = FF_CHUNK

    def body(h_ref, wg_ref, wv_ref, fcw_ref, fcb_ref, gate_ref, ge_ref, vd_ref, act_ref, before):
        i = pl.program_id(1)

        @pl.when(i == 0)
        def _():
            before[...] = jnp.zeros((8, CW), F32)

        hb = h_ref[...]
        gate = _mm_nt(hb, wg_ref[...])
        val = _mm_nt(hb, wv_ref[...])
        gate_ref[...] = gate.astype(BF16)
        edge = before[...]
        gc = (fcb_ref[...] + fcw_ref[0:1, :] * _shift_rows(gate, 2, edge) + fcw_ref[1:2, :] * _shift_rows(gate, 1, edge)
              + fcw_ref[2:3, :] * gate)
        before[...] = gate[tm - 8:tm, :]
        ge, dge = _gelu(gc)
        ge_ref[...] = ge.astype(BF16)
        vd_ref[...] = (val * dge).astype(BF16)
        act_ref[...] = (ge * val).astype(BF16)

    chunk = pl.BlockSpec((None, tm, CW), lambda c, i: (c, i, 0))
    return _launch(
        body, "ffn_up", (NC, T // tm),
        [pl.BlockSpec((tm, D), lambda c, i: (i, 0)), pl.BlockSpec((CW, D), lambda c, i: (c, 0)),
         pl.BlockSpec((CW, D), lambda c, i: (NC + c, 0)), pl.BlockSpec((None, 3, CW), lambda c, i: (c, 0, 0)),
         pl.BlockSpec((None, 1, CW), lambda c, i: (c, 0, 0))],
        [chunk] * 4, [jax.ShapeDtypeStruct((NC, T, CW), BF16)] * 4, [pltpu.VMEM((8, CW), F32)],
        (h1b, w_up_t, w_up_t, fcw, fcb), exchange)


def _ffn_down(act, z1, p, tgt, w_down, w_g, w_p_t, g1, b1, g2, b2, bg):
    T = z1.shape[0]
    tm = 256

    def body(act_ref, z_ref, p_ref, t_ref, wdn_hbm, wg_hbm, wp_hbm, g1_ref, b1_ref, g2_ref, b2_ref, bg_ref,
             dz2_ref, dz2b_ref, dpre_ref, dpp_ref, vec_ref, wdn, wg, wp):
        @pl.when(pl.program_id(0) == 0)
        def _():
            pltpu.sync_copy(wdn_hbm, wdn)
            pltpu.sync_copy(wg_hbm, wg)
            pltpu.sync_copy(wp_hbm, wp)
            vec_ref[...] = jnp.zeros_like(vec_ref)

        g2v = g2_ref[...]
        h1, _, _ = _ln(z_ref[...], g1_ref[...], b1_ref[...])
        h1b = h1.astype(MXU_DTYPE)
        ffn = _mm(act_ref[0], wdn[0:FF_CHUNK, :])
        for c in range(1, NC):
            ffn = ffn + _mm(act_ref[c], wdn[c * FF_CHUNK:(c + 1) * FF_CHUNK, :])
        sg = _sigmoid(_mm(h1b, wg[...]) + bg_ref[...])
        pp = _mm_nt(p_ref[...], wp[...])
        z2 = ALPHA * h1 + ffn + sg * pp
        y, xh2, rstd2 = _ln(z2, g2v, b2_ref[...])
        diff = y - t_ref[...]
        dy = diff * (1.0 / D)
        dz2 = _ln_bwd(dy, xh2, rstd2, g2v)
        dpre = dz2 * pp * sg * (1.0 - sg)
        dz2_ref[...] = dz2
        dz2b_ref[...] = dz2.astype(BF16)
        dpre_ref[...] = dpre.astype(BF16)
        dpp_ref[...] = (dz2 * sg).astype(BF16)
        loss = 0.5 * jnp.sum(jnp.sum(diff * diff, axis=1, keepdims=True), axis=0, keepdims=True) * (1.0 / D)
        vec_ref[0:1, :] += jnp.broadcast_to(loss, (1, D))
        vec_ref[1:2, :] += _colsum(dy * xh2)
        vec_ref[2:3, :] += _colsum(dy)
        vec_ref[3:4, :] += _colsum(dpre)

    anyspec = pl.BlockSpec(memory_space=pl.ANY)
    vec = _full((1, D))
    return pl.pallas_call(
        body, name="ffn_down", grid=(T // tm,),
        in_specs=[pl.BlockSpec((NC, tm, FF_CHUNK), lambda i: (0, i, 0)), _rows(tm, D), _rows(tm, PLE), _rows(tm, D),
                  anyspec, anyspec, anyspec] + [vec] * 5,
        out_specs=[_rows(tm, D)] * 4 + [_full((8, D))],
        out_shape=[jax.ShapeDtypeStruct((T, D), F32)] + [jax.ShapeDtypeStruct((T, D), BF16)] * 3
                  + [jax.ShapeDtypeStruct((8, D), F32)],
        scratch_shapes=[pltpu.VMEM((D_FF, D), MXU_DTYPE), pltpu.VMEM((D, D), MXU_DTYPE), pltpu.VMEM((D, PLE), MXU_DTYPE)],
        compiler_params=_params(),
    )(act, z1, p, tgt, w_down, w_g, w_p_t, g1, b1, g2, b2, bg)


def _ffn_bwd(dz2b, gate, ge, vd, w_down, fcw):
    T = dz2b.shape[0]
    tm = min(1024, T)
    CW = FF_CHUNK
    nt = T // tm

    def body(dz_ref, wdn_ref, gate_ref, ge_ref, vd_ref, fcw_ref, dup_ref, dfc_ref, after):
        i = pl.program_id(1)

        @pl.when(i == 0)
        def _():
            after[...] = jnp.zeros((8, CW), F32)
            dfc_ref[...] = jnp.zeros_like(dfc_ref)

        gate = gate_ref[...].astype(F32)
        dact = _mm_nt(dz_ref[...], wdn_ref[...])
        dgc = dact * vd_ref[...].astype(F32)
        edge = after[...]
        dgc1 = _shift_rows(dgc, -1, edge)
        dgc2 = _shift_rows(dgc, -2, edge)
        after[...] = dgc[0:8, :]
        dup_ref[0] = (fcw_ref[2:3, :] * dgc + fcw_ref[1:2, :] * dgc1 + fcw_ref[0:1, :] * dgc2).astype(BF16)
        dup_ref[1] = (dact * ge_ref[...].astype(F32)).astype(BF16)
        dfc_ref[0:1, :] += _colsum(dgc2 * gate)
        dfc_ref[1:2, :] += _colsum(dgc1 * gate)
        dfc_ref[2:3, :] += _colsum(dgc * gate)
        dfc_ref[3:4, :] += _colsum(dgc)

    rev = lambda c, i: (c, nt - 1 - i, 0)
    chunk = pl.BlockSpec((None, tm, CW), rev)
    return pl.pallas_call(
        body, name="ffn_bwd", grid=(NC, nt),
        in_specs=[pl.BlockSpec((tm, D), lambda c, i: (nt - 1 - i, 0)), pl.BlockSpec((CW, D), lambda c, i: (c, 0)),
                  chunk, chunk, chunk, pl.BlockSpec((None, 3, CW), lambda c, i: (c, 0, 0))],
        out_specs=[pl.BlockSpec((None, 2, tm, CW), lambda c, i: (c, 0, nt - 1 - i, 0)),
                   pl.BlockSpec((None, 8, CW), lambda c, i: (c, 0, 0))],
        out_shape=[jax.ShapeDtypeStruct((NC, 2, T, CW), BF16), jax.ShapeDtypeStruct((NC, 8, CW), F32)],
        scratch_shapes=[pltpu.VMEM((8, CW), F32)],
        compiler_params=_params(),
    )(dz2b, w_down, gate, ge, vd, fcw)


def _ffn_dh1(dup, dz2, dpre, z1, w_up_t, w_g, g1, b1):
    T = z1.shape[0]
    tm = 256

    def body(dup_ref, dz2_ref, dpre_ref, z_ref, wup_hbm, wg_hbm, g1_ref, b1_ref, dz1_ref, vec_ref, wup, wg):
        @pl.when(pl.program_id(0) == 0)
        def _():
            pltpu.sync_copy(wup_hbm, wup)
            pltpu.sync_copy(wg_hbm, wg)
            vec_ref[...] = jnp.zeros_like(vec_ref)

        g1v = g1_ref[...]
        _, xh1, rstd1 = _ln(z_ref[...], g1v, b1_ref[...])
        dh1 = ALPHA * dz2_ref[...] + _mm_nt(dpre_ref[...], wg[...])
        for c in range(NC):
            for s in range(2):
                r0 = s * D_FF + c * FF_CHUNK
                dh1 = dh1 + _mm(dup_ref[c, s], wup[r0:r0 + FF_CHUNK, :])
        dz1_ref[...] = _ln_bwd(dh1, xh1, rstd1, g1v)
        vec_ref[0:1, :] += _colsum(dh1 * xh1)
        vec_ref[1:2, :] += _colsum(dh1)

    anyspec = pl.BlockSpec(memory_space=pl.ANY)
    vec = _full((1, D))
    return pl.pallas_call(
        body, name="ffn_dh1", grid=(T // tm,),
        in_specs=[pl.BlockSpec((NC, 2, tm, FF_CHUNK), lambda i: (0, 0, i, 0)), _rows(tm, D), _rows(tm, D), _rows(tm, D),
                  anyspec, anyspec, vec, vec],
        out_specs=[_rows(tm, D), _full((8, D))],
        out_shape=[jax.ShapeDtypeStruct((T, D), F32), jax.ShapeDtypeStruct((8, D), F32)],
        scratch_shapes=[pltpu.VMEM((2 * D_FF, D), MXU_DTYPE), pltpu.VMEM((D, D), MXU_DTYPE)],
        compiler_params=_params(),
    )(dup, dz2, dpre, z1, w_up_t, w_g, g1, b1)


def _out_proj_bwd(dz1, w_out, exchange=None):
    T = dz1.shape[0]
    tm = 512

    def body(dz_ref, w_ref, datt_ref, drec_ref):
        dzb = dz_ref[...].astype(MXU_DTYPE)
        datt = _mm_nt(dzb, w_ref[0:512, :])
        for h in range(HEADS):
            datt_ref[h] = datt[:, h * 64:(h + 1) * 64].astype(BF16)
        drec_ref[...] = _mm_nt(dzb, w_ref[512:1024, :])

    return _launch(body, "out_proj_bwd", (T // tm,), [_rows(tm, D), _full((D, D))], [_heads(tm), _rows(tm, 512)],
                   [jax.ShapeDtypeStruct((HEADS, T, 64), BF16), jax.ShapeDtypeStruct((T, 512), F32)], [],
                   (dz1, w_out), exchange)


def _in_proj_bwd(dq, dkv, dxr, dgr, dz1, w_in_t, exchange=None):
    T = dz1.shape[0]
    tm = 512
    W = D_IN // 4

    def body(dq_ref, dkv_ref, dxr_ref, dgr_ref, dz_ref, w_ref, dx_ref, du_ref):
        dkv = dkv_ref[...]
        dx_ref[...] = (ALPHA * dz_ref[...] + _mm(dq_ref[...], w_ref[0:512, :]) + _mm(dkv, w_ref[512:768, :])
                       + _mm(dxr_ref[...], w_ref[768:1280, :]) + _mm(dgr_ref[...], w_ref[1280:1792, :]))
        dq, dxr, dgr = dq_ref[...].astype(F32), dxr_ref[...].astype(F32), dgr_ref[...].astype(F32)
        du_ref[0] = dq[:, 0:W].astype(BF16)
        du_ref[1, :, 0:64] = dq[:, W:512].astype(BF16)
        du_ref[1, :, 64:320] = dkv.astype(BF16)
        du_ref[1, :, 320:W] = dxr[:, 0:128].astype(BF16)
        du_ref[2, :, 0:384] = dxr[:, 128:512].astype(BF16)
        du_ref[2, :, 384:W] = dgr[:, 0:64].astype(BF16)
        du_ref[3] = dgr[:, 64:512].astype(BF16)

    return _launch(body, "in_proj_bwd", (T // tm,),
                   [_rows(tm, 512), _rows(tm, 256), _rows(tm, 512), _rows(tm, 512), _rows(tm, D), _full((D_IN, D))],
                   [_rows(tm, D), pl.BlockSpec((4, tm, W), lambda i: (0, i, 0))],
                   [jax.ShapeDtypeStruct((T, D), F32), jax.ShapeDtypeStruct((4, T, W), BF16)], [],
                   (dq, dkv, dxr, dgr, dz1, w_in_t), exchange)


def _accumulate_tn(a_ref, b_ref, o_ref):
    @pl.when(pl.program_id(1) == 0)
    def _():
        o_ref[...] = jnp.zeros_like(o_ref)

    o_ref[...] += _mm_tn(a_ref[...], b_ref[...])


def _weight_grad_cols(a, b, name, n_blocks, b_spec, out_shape, out_spec, exchange=None):
    T, M = a.shape
    bt = min(2048, T)
    return _launch(functools.partial(_accumulate_tn), name, (n_blocks, T // bt),
                   [pl.BlockSpec((bt, M), lambda m, k: (k, 0)), b_spec(bt)], [out_spec],
                   [jax.ShapeDtypeStruct(out_shape, F32)], [], (a, b), exchange)


def _weight_grad(a, b, bm, name):
    bt = min(2048, b.shape[0])
    if a.ndim == 3:
        assert a.shape[2] == bm
        T, M = a.shape[1], a.shape[0] * bm
        a_spec = pl.BlockSpec((None, bt, bm), lambda m, k: (m, k, 0))
    else:
        T, M = a.shape
        a_spec = pl.BlockSpec((bt, bm), lambda m, k: (k, m))
    N = b.shape[1]
    nk = T // bt

    return pl.pallas_call(
        functools.partial(_accumulate_tn), name=name, grid=(M // bm, nk),
        in_specs=[a_spec, pl.BlockSpec((bt, N), lambda m, k: (k, 0))],
        out_specs=pl.BlockSpec((bm, N), lambda m, k: (m, 0)),
        out_shape=jax.ShapeDtypeStruct((M, N), F32),
        compiler_params=_params(),
    )(a, b)


def _adamw(w, g, m, v, name):
    R, C = w.shape
    tr = R // 8 if R % 64 == 0 else R
    c1 = 1.0 / (1.0 - ADAM_B1 ** ADAM_STEP)
    c2 = 1.0 / (1.0 - ADAM_B2 ** ADAM_STEP)

    def body(w_ref, g_ref, m_ref, v_ref, d_ref, nm_ref, nv_ref):
        g = g_ref[...]
        nm = ADAM_B1 * m_ref[...] + (1.0 - ADAM_B1) * g
        nv = ADAM_B2 * v_ref[...] + (1.0 - ADAM_B2) * g * g
        nm_ref[...] = nm
        nv_ref[...] = nv
        d_ref[...] = -ADAM_LR * ((nm * c1) / (jnp.sqrt(nv * c2) + ADAM_EPS) + ADAM_WD * w_ref[...])

    spec = pl.BlockSpec((tr, C), lambda i: (i, 0))
    return pl.pallas_call(
        body, name=name, grid=(R // tr,),
        in_specs=[spec] * 4, out_specs=[spec] * 3,
        out_shape=[jax.ShapeDtypeStruct((R, C), F32)] * 3,
        compiler_params=_params(),
    )(w, g, m, v)


def _adamw_halves(ws, mines, sibs, ms, vs, c, name, exchange=None):
    n, nb = len(ws), 4
    c1 = 1.0 / (1.0 - ADAM_B1 ** ADAM_STEP)
    c2 = 1.0 / (1.0 - ADAM_B2 ** ADAM_STEP)

    def body(c_ref, *refs):
        own = (pl.program_id(0) // nb) == c_ref[0]
        for i in range(n):
            w_ref, a_ref, b_ref, m_ref, v_ref = refs[5 * i:5 * i + 5]
            g_ref, d_ref, nm_ref, nv_ref = refs[5 * n + 4 * i:5 * n + 4 * i + 4]
            g = jnp.where(own, a_ref[...], b_ref[...])
            nm = ADAM_B1 * m_ref[...] + (1.0 - ADAM_B1) * g
            nv = ADAM_B2 * v_ref[...] + (1.0 - ADAM_B2) * g * g
            g_ref[...] = g
            nm_ref[...] = nm
            nv_ref[...] = nv
            d_ref[...] = -ADAM_LR * ((nm * c1) / (jnp.sqrt(nv * c2) + ADAM_EPS) + ADAM_WD * w_ref[...])

    in_specs, out_specs, out_shape, args = [], [], [], []
    for w, a, b, m, v in zip(ws, mines, sibs, ms, vs):
        R, C = w.shape
        tr = R // (2 * nb)
        assert tr % 8 == 0 and a.shape == (R // 2, C)
        full = pl.BlockSpec((tr, C), lambda i, c_ref: (i, 0))
        half = pl.BlockSpec((tr, C), lambda i, c_ref: (i % nb, 0))
        in_specs += [full, half, half, full, full]
        out_specs += [full] * 4
        out_shape += [jax.ShapeDtypeStruct((R, C), F32)] * 4
        args += [w, a, b, m, v]
    out = _launch(body, name, (2 * nb,), in_specs, out_specs, out_shape, [], (c, *args), exchange, prefetch=1)
    return [tuple(out[4 * i:4 * i + 4]) for i in range(n)], list(out[4 * n:])


def _add4(fs, name):
    n = len(fs)

    def body(*refs):
        for a_ref, o_ref in zip(refs[:n], refs[n:]):
            o_ref[...] = ((a_ref[0].astype(F32) + a_ref[1].astype(F32)) + a_ref[2].astype(F32)) + a_ref[3].astype(F32)

    for f in fs:
        assert (f.shape[1] // 2) % 16 == 0
    return pl.pallas_call(
        body, name=name, grid=(2,),
        in_specs=[pl.BlockSpec((4, f.shape[1] // 2, f.shape[2]), lambda i: (0, i, 0)) for f in fs],
        out_specs=[pl.BlockSpec((f.shape[1] // 2, f.shape[2]), lambda i: (i, 0)) for f in fs],
        out_shape=[jax.ShapeDtypeStruct(f.shape[1:], F32) for f in fs], compiler_params=_params())(*fs)


def _gather_first(wsrc, cpack):
    def body(w_ref, c_ref, gw_ref, gc_ref, send_sems, recv_sems, local_sem, csend, crecv, clocal):
        x, y, c = _pos()
        me = 2 * x + y
        chips = _other_chips(x, y)
        start, forward, finish = _gather_steps(w_ref, gw_ref, send_sems, recv_sems, local_sem)
        start()
        loc = pltpu.make_async_copy(c_ref, gc_ref.at[me], clocal)
        loc.start()

        def conv_copy(k, slot):
            px, py = chips[k]
            return pltpu.make_async_remote_copy(src_ref=c_ref, dst_ref=gc_ref.at[slot], send_sem=csend.at[k],
                                                recv_sem=crecv.at[k], device_id=(px, py, c), device_id_type=MESH)

        for k in range(3):
            conv_copy(k, me).start()
        forward()
        finish()
        for k, (px, py) in enumerate(chips):
            conv_copy(k, 2 * px + py).wait_recv()
        for k in range(3):
            conv_copy(k, me).wait_send()
        loc.wait()

    anyspec = pl.BlockSpec(memory_space=pl.ANY)
    return pl.pallas_call(
        body, name="gather_first",
        in_specs=[anyspec, anyspec], out_specs=[anyspec, anyspec],
        out_shape=[jax.ShapeDtypeStruct((4,) + wsrc.shape, wsrc.dtype), jax.ShapeDtypeStruct((4,) + cpack.shape, cpack.dtype)],
        scratch_shapes=GATHER_SCRATCH + [pltpu.SemaphoreType.DMA((3,)), pltpu.SemaphoreType.DMA((3,)), pltpu.SemaphoreType.DMA],
        compiler_params=_params(has_side_effects=True),
    )(wsrc, cpack)


def _all_devices_exchange(s):
    def make(ins, outs, sems):
        s_ref, o_ref = ins[0], outs[0]
        send_sems, recv_sems, local_sem = sems
        x, y, c = _pos()
        me = 4 * x + 2 * y + c
        loc = pltpu.make_async_copy(s_ref, o_ref.at[me], local_sem)

        def copy(k, slot):
            peer = (x ^ (k >> 2), y ^ ((k >> 1) & 1), c ^ (k & 1))
            return pltpu.make_async_remote_copy(src_ref=s_ref, dst_ref=o_ref.at[slot], send_sem=send_sems.at[k - 1],
                                                recv_sem=recv_sems.at[k - 1], device_id=peer, device_id_type=MESH)

        def start():
            loc.start()
            for k in range(1, 8):
                copy(k, me).start()

        def finish():
            for k in range(1, 8):
                copy(k, 4 * (x ^ (k >> 2)) + 2 * (y ^ ((k >> 1) & 1)) + (c ^ (k & 1))).wait_recv()
            for k in range(1, 8):
                copy(k, me).wait_send()
            loc.wait()

        return start, lambda: None, finish

    return _Exchange([s], [jax.ShapeDtypeStruct((8,) + s.shape, s.dtype)],
                     [pltpu.SemaphoreType.DMA((7,)), pltpu.SemaphoreType.DMA((7,)), pltpu.SemaphoreType.DMA], make)


def _sum_devices(a):
    def body(a_ref, o_ref):
        acc = a_ref[0]
        for d in range(1, 8):
            acc = acc + a_ref[d]
        o_ref[...] = acc

    vm = pl.BlockSpec(memory_space=pltpu.VMEM)
    return pl.pallas_call(body, name="sum_devices", in_specs=[vm], out_specs=vm,
                          out_shape=jax.ShapeDtypeStruct(a.shape[1:], F32), compiler_params=_params())(a)


def _swap_exchange(gs):
    n = len(gs)

    def make(ins, outs, sems):
        x, y, c = _pos()
        cps = []
        for i in range(n):
            half = gs[i].shape[1] // 2
            rows = pl.ds(pl.multiple_of((1 - c) * half, 8), half)
            cps.append(pltpu.make_async_remote_copy(src_ref=ins[i].at[:, rows, :], dst_ref=outs[i], send_sem=sems[0].at[i],
                                                    recv_sem=sems[1].at[i], device_id=(x, y, 1 - c), device_id_type=MESH))

        def start():
            for cp in cps:
                cp.start()

        def finish():
            for cp in cps:
                cp.wait()

        return start, lambda: None, finish

    return _Exchange(gs, [jax.ShapeDtypeStruct((4, g.shape[1] // 2, g.shape[2]), g.dtype) for g in gs],
                     [pltpu.SemaphoreType.DMA((n,)), pltpu.SemaphoreType.DMA((n,))], make)


def _scatter_exchange(ss):
    n = len(ss)

    def make(ins, outs, sems):
        send_sems, recv_sems, local_sems = sems
        x, y, c = _pos()
        me = 2 * x + y
        chips = _other_chips(x, y)
        locs = [pltpu.make_async_copy(ins[i].at[me], outs[i].at[me], local_sems.at[i]) for i in range(n)]

        def copy(i, k, src_slot, dst_slot):
            px, py = chips[k]
            return pltpu.make_async_remote_copy(src_ref=ins[i].at[src_slot], dst_ref=outs[i].at[dst_slot],
                                                send_sem=send_sems.at[3 * i + k], recv_sem=recv_sems.at[3 * i + k],
                                                device_id=(px, py, c), device_id_type=MESH)

        def start():
            for i in range(n):
                locs[i].start()
                for k, (px, py) in enumerate(chips):
                    copy(i, k, 2 * px + py, me).start()

        def finish():
            for i in range(n):
                for k, (px, py) in enumerate(chips):
                    copy(i, k, me, 2 * px + py).wait_recv()
            for i in range(n):
                for k, (px, py) in enumerate(chips):
                    copy(i, k, 2 * px + py, me).wait_send()
                locs[i].wait()

        return start, lambda: None, finish

    return _Exchange(ss, [jax.ShapeDtypeStruct(s.shape, s.dtype) for s in ss],
                     [pltpu.SemaphoreType.DMA((3 * n,)), pltpu.SemaphoreType.DMA((3 * n,)), pltpu.SemaphoreType.DMA((n,))], make)


def _send_exchange(rs):
    n = len(rs)

    def make(ins, outs, sems):
        x, y, c = _pos()
        cps = [pltpu.make_async_remote_copy(src_ref=ins[i], dst_ref=outs[i], send_sem=sems[0].at[i], recv_sem=sems[1].at[i],
                                            device_id=(x, y, 1 - c), device_id_type=MESH) for i in range(n)]

        def start():
            for cp in cps:
                cp.start()

        def finish():
            for cp in cps:
                cp.wait()

        return start, lambda: None, finish

    return _Exchange(rs, [jax.ShapeDtypeStruct(r.shape, r.dtype) for r in rs],
                     [pltpu.SemaphoreType.DMA((n,)), pltpu.SemaphoreType.DMA((n,))], make)


def _run_exchange(ex, name):
    ei, eo = len(ex.args), len(ex.out_shape)

    def body(*refs):
        start, forward, finish = ex.make(refs[:ei], refs[ei:ei + eo], refs[ei + eo:])
        start()
        forward()
        finish()

    anyspec = pl.BlockSpec(memory_space=pl.ANY)
    return pl.pallas_call(body, name=name, in_specs=[anyspec] * ei, out_specs=[anyspec] * eo, out_shape=ex.out_shape,
                          scratch_shapes=ex.scratch, compiler_params=_params(has_side_effects=True))(*ex.args)


def _add_half(gs, rs, c, name):
    n = len(gs)

    def body(c_ref, *refs):
        for g_ref, r_ref, o_ref in zip(refs[:n], refs[n:2 * n], refs[2 * n:]):
            o_ref[...] = (g_ref[...] + r_ref[...]).astype(BF16)

    g_specs, r_specs, out_shape = [], [], []
    for g, r in zip(gs, rs):
        _, H, C = r.shape
        tr = H // 2
        assert tr % 16 == 0 and g.shape == (4, 2 * H, C)
        g_specs.append(pl.BlockSpec((1, tr, C), lambda j, i, c_ref: (j, c_ref[0] * 2 + i, 0)))
        r_specs.append(pl.BlockSpec((1, tr, C), lambda j, i, c_ref: (j, i, 0)))
        out_shape.append(jax.ShapeDtypeStruct((4, H, C), BF16))
    grid_spec = pltpu.PrefetchScalarGridSpec(num_scalar_prefetch=1, grid=(4, 2), in_specs=g_specs + r_specs, out_specs=r_specs)
    return pl.pallas_call(body, name=name, grid_spec=grid_spec, out_shape=out_shape, compiler_params=_params())(c, *gs, *rs)


def _block_diag(w):
    eye = jnp.eye(RNN_BLOCKS, dtype=w.dtype)
    return (eye[:, None, :, None] * w[:, :, None, :]).reshape(D_RNN, D_RNN)


def _diag_blocks(wd):
    d = wd.reshape(RNN_BLOCKS, 64, RNN_BLOCKS, 64)
    return jnp.stack([d[h, :, h, :] for h in range(RNN_BLOCKS)])


def _split_pack(a, first, last):
    out, base = {}, PACK_OFF[first]
    for i in range(first, last):
        s = a[:, PACK_OFF[i] - base:PACK_OFF[i + 1] - base]
        out[BIG_KEYS[i]] = s.reshape(4 * 256, 256) if BIG_KEYS[i] == "w_p_t" else s.reshape(-1, 1024)
    return out


def _layer_grads(x, p, tgt, gw, small, shard=None, core=None):
    row = lambda v: v.reshape(1, -1)
    wa = _block_diag(small["gate_a_w"]).astype(MXU_DTYPE)
    wx = _block_diag(small["gate_x_w"]).astype(MXU_DTYPE)
    sinks = small["attn_sinks"].reshape(1, HEADS)

    dist = shard is not None
    q, kv, xr, gr, xb = _in_proj(x, gw["w_in_t"])
    cut = PACK_OFF[1] + PACK_ROWS[1] // 2
    att, *ga = _attn_fwd(q, kv, sinks, _gather_exchange(shard[PACK_OFF[1]:cut]) if dist else None)
    xc, h, rec, *gb = _rnn_fwd(xr, gr, small["rnn_conv_w"], row(small["rnn_conv_b"]), wa, row(small["gate_a_b"]),
                               wx, row(small["gate_x_b"]), row(small["lru_lambda"]),
                               _gather_exchange(shard[cut:PACK_OFF[3]]) if dist else None)
    if dist:
        gw = {**gw, **_split_pack(jnp.concatenate([ga[0], gb[0]], axis=1), 1, 3)}
    g1, b1 = row(small["ln1_g"]), row(small["ln1_b"])
    fcw = small["ffn_conv_w"].reshape(3, NC, FF_CHUNK).transpose(1, 0, 2)
    fcb = small["ffn_conv_b"].reshape(NC, 1, FF_CHUNK)
    z1, h1b = _out_proj(att, rec, x, gw["w_out"], g1, b1)
    gate, ge, vd, act, *gc = _ffn_up(h1b, gw["w_up_t"], fcw, fcb,
                                     _gather_exchange(shard[PACK_OFF[3]:PACK_OFF[6]]) if dist else None)
    if dist:
        gw = {**gw, **_split_pack(gc[0], 3, 6)}
    dz2, dz2b, dpre, dpp, vec2 = _ffn_down(act, z1, p, tgt, gw["w_down"], gw["w_g"], gw["w_p_t"], g1, b1,
                                           row(small["ln2_g"]), row(small["ln2_b"]), row(small["ple_gate_b"]))
    dup, dfc = _ffn_bwd(dz2b, gate, ge, vd, gw["w_down"], fcw)
    dz1, vec1 = _ffn_dh1(dup, dz2, dpre, z1, gw["w_up_t"], gw["w_g"], g1, b1)
    per_chip = 2 * D_FF // 4 // FF_CHUNK
    big = {
        "w_ffn_up": _weight_grad_cols(
            h1b, dup.reshape(2 * NC, -1, FF_CHUNK), "dw_up", 2 * NC,
            lambda bt: pl.BlockSpec((None, bt, FF_CHUNK), lambda m, k: (m, k, 0)), (4, D, 2 * D_FF // 4),
            pl.BlockSpec((None, D, FF_CHUNK), lambda m, k: (2 * (m % 2) + (m // 2) // per_chip, 0, (m // 2) % per_chip)))[0],
        "w_ffn_down": _weight_grad(act, dz2b, 512, "dw_down").reshape(4, D_FF // 4, D),
        "ple_gate_w": _weight_grad(h1b, dpre, 512, "dw_gate").reshape(4, D // 4, D),
        "ple_proj": _weight_grad_cols(
            p.astype(BF16), dpp, "dw_proj", 4, lambda bt: pl.BlockSpec((bt, D // 4), lambda j, k: (k, j)),
            (4, PLE, D // 4), pl.BlockSpec((None, PLE, D // 4), lambda j, k: (j, 0, 0)))[0],
        "w_out": _weight_grad(jnp.concatenate([att, rec], axis=1), dz1, 512, "dw_out").reshape(4, D // 4, D),
    }
    reduced = None
    if dist:
        g_ffn = [big[k] for k in EARLY_WEIGHTS]
        ex = _swap_exchange(g_ffn)
    datt, drec, *got = _out_proj_bwd(dz1, gw["w_out"], ex if dist else None)
    if dist:
        ex = _scatter_exchange(_add_half(g_ffn, got, core, "add_half_ffn"))
    dxr, dgr, dwa, dwx, dvec, *got = _rnn_bwd(drec, gr, h, xc, xr, small["rnn_conv_w"], wa, row(small["gate_a_b"]),
                                              wx, row(small["gate_x_b"]), row(small["lru_lambda"]), ex if dist else None)
    if dist:
        mine = _add4(got, "add_chips_ffn")
        ex = _send_exchange(mine)
    dq, dkv, dsinks, *got = _attn_bwd(q, kv, datt, sinks, ex if dist else None)
    if dist:
        reduced = (mine, got)
        big = {}
    sg = {
        "attn_sinks": dsinks[:, 0],
        "rnn_conv_w": dvec[4:8],
        "rnn_conv_b": dvec[3],
        "gate_a_w": _diag_blocks(dwa),
        "gate_a_b": dvec[0],
        "gate_x_w": _diag_blocks(dwx),
        "gate_x_b": dvec[1],
        "lru_lambda": dvec[2],
        "ln1_g": vec1[0],
        "ln1_b": vec1[1],
        "ffn_conv_w": dfc[:, 0:3].transpose(1, 0, 2).reshape(3, D_FF),
        "ffn_conv_b": dfc[:, 3].reshape(D_FF),
        "ple_gate_b": vec2[3],
        "ln2_g": vec2[1],
        "ln2_b": vec2[2],
    }
    loss = vec2[0, 0:1]
    grad_x, du = _in_proj_bwd(dq, dkv, dxr, dgr, dz1, gw["w_in_t"])
    ex = _all_devices_exchange(_pack_vecs([sg[k] for k in SMALL] + [loss])[0]) if dist else None
    big["w_in"], *small_all = _weight_grad_cols(
        xb, du, "dw_in", 4, lambda bt: pl.BlockSpec((None, bt, D_IN // 4), lambda j, k: (j, k, 0)), (4, D, D_IN // 4),
        pl.BlockSpec((None, D, D_IN // 4), lambda j, k: (j, 0, 0)), ex)
    return grad_x, big, sg, loss, reduced, small_all


BIG = ("w_in", "w_ffn_up", "w_out", "w_ffn_down", "ple_gate_w", "ple_proj")
BIG_KEYS = ("w_in_t", "w_up_t", "w_out", "w_down", "w_g", "w_p_t")
BIG_T = (True, True, False, False, False, True)
EARLY_WEIGHTS = ("w_ffn_up", "w_ffn_down", "ple_gate_w", "ple_proj", "w_out")
LATE_WEIGHTS = ("w_in",)
SMALL = ("attn_sinks", "rnn_conv_w", "rnn_conv_b", "gate_a_w", "gate_a_b", "gate_x_w", "gate_x_b", "lru_lambda",
         "ln1_g", "ln1_b", "ffn_conv_w", "ffn_conv_b", "ple_gate_b", "ln2_g", "ln2_b")
SHARDED_SMALL = ("rnn_conv_w", "ffn_conv_w")
WEIGHTS = ("w_in", "attn_sinks", "rnn_conv_w", "rnn_conv_b", "gate_a_w", "gate_a_b", "gate_x_w", "gate_x_b",
           "lru_lambda", "w_out", "ln1_g", "ln1_b", "w_ffn_up", "ffn_conv_w", "ffn_conv_b", "w_ffn_down",
           "ple_gate_w", "ple_gate_b", "ple_proj", "ln2_g", "ln2_b")


def _pack_big(d, first=0, last=6):
    parts = []
    for name, t in zip(BIG[first:last], BIG_T[first:last]):
        a = d[name]
        a = a.T if t else a
        parts.append(a.reshape(-1, 1024))
    return jnp.concatenate(parts, axis=0)


def _pack_vecs(items):
    parts, offs, n = [], [], 0
    for a in items:
        f = a.reshape(-1).astype(F32)
        pad = (-f.shape[0]) % 128
        parts.append(jnp.pad(f, (0, pad)))
        offs.append(n)
        n += (f.shape[0] + pad) // 128
    padr = (-n) % 8
    if padr:
        parts.append(jnp.zeros((padr * 128,), F32))
    return jnp.concatenate(parts).reshape(-1, 128), offs


def _unpack_vecs(a, offs, shapes):
    flat = a.reshape(-1)
    out = []
    for o, s in zip(offs, shapes):
        n = 1
        for d in s:
            n *= d
        out.append(flat[o * 128:o * 128 + n].reshape(s))
    return out


def kernel(x, p, w_in, attn_sinks, rnn_conv_w, rnn_conv_b, gate_a_w, gate_a_b, gate_x_w, gate_x_b, lru_lambda, w_out, ln1_g, ln1_b, w_ffn_up, ffn_conv_w, ffn_conv_b, w_ffn_down, ple_gate_w, ple_gate_b, ple_proj, ln2_g, ln2_b, loss_target, m_w_in, m_attn_sinks, m_rnn_conv_w, m_rnn_conv_b, m_gate_a_w, m_gate_a_b, m_gate_x_w, m_gate_x_b, m_lru_lambda, m_w_out, m_ln1_g, m_ln1_b, m_w_ffn_up, m_ffn_conv_w, m_ffn_conv_b, m_w_ffn_down, m_ple_gate_w, m_ple_gate_b, m_ple_proj, m_ln2_g, m_ln2_b, v_w_in, v_attn_sinks, v_rnn_conv_w, v_rnn_conv_b, v_gate_a_w, v_gate_a_b, v_gate_x_w, v_gate_x_b, v_lru_lambda, v_w_out, v_ln1_g, v_ln1_b, v_w_ffn_up, v_ffn_conv_w, v_ffn_conv_b, v_w_ffn_down, v_ple_gate_w, v_ple_gate_b, v_ple_proj, v_ln2_g, v_ln2_b):
    w = dict(w_in=w_in, attn_sinks=attn_sinks, rnn_conv_w=rnn_conv_w, rnn_conv_b=rnn_conv_b, gate_a_w=gate_a_w,
             gate_a_b=gate_a_b, gate_x_w=gate_x_w, gate_x_b=gate_x_b, lru_lambda=lru_lambda, w_out=w_out, ln1_g=ln1_g,
             ln1_b=ln1_b, w_ffn_up=w_ffn_up, ffn_conv_w=ffn_conv_w, ffn_conv_b=ffn_conv_b, w_ffn_down=w_ffn_down,
             ple_gate_w=ple_gate_w, ple_gate_b=ple_gate_b, ple_proj=ple_proj, ln2_g=ln2_g, ln2_b=ln2_b)
    m = dict(w_in=m_w_in, attn_sinks=m_attn_sinks, rnn_conv_w=m_rnn_conv_w, rnn_conv_b=m_rnn_conv_b, gate_a_w=m_gate_a_w,
             gate_a_b=m_gate_a_b, gate_x_w=m_gate_x_w, gate_x_b=m_gate_x_b, lru_lambda=m_lru_lambda, w_out=m_w_out,
             ln1_g=m_ln1_g, ln1_b=m_ln1_b, w_ffn_up=m_w_ffn_up, ffn_conv_w=m_ffn_conv_w, ffn_conv_b=m_ffn_conv_b,
             w_ffn_down=m_w_ffn_down, ple_gate_w=m_ple_gate_w, ple_gate_b=m_ple_gate_b, ple_proj=m_ple_proj,
             ln2_g=m_ln2_g, ln2_b=m_ln2_b)
    v = dict(w_in=v_w_in, attn_sinks=v_attn_sinks, rnn_conv_w=v_rnn_conv_w, rnn_conv_b=v_rnn_conv_b, gate_a_w=v_gate_a_w,
             gate_a_b=v_gate_a_b, gate_x_w=v_gate_x_w, gate_x_b=v_gate_x_b, lru_lambda=v_lru_lambda, w_out=v_w_out,
             ln1_g=v_ln1_g, ln1_b=v_ln1_b, w_ffn_up=v_w_ffn_up, ffn_conv_w=v_ffn_conv_w, ffn_conv_b=v_ffn_conv_b,
             w_ffn_down=v_w_ffn_down, ple_gate_w=v_ple_gate_w, ple_gate_b=v_ple_gate_b, ple_proj=v_ple_proj,
             ln2_g=v_ln2_g, ln2_b=v_ln2_b)
    w, m, v = ({k: a[0] for k, a in d.items()} for d in (w, m, v))
    chip = 2 * lax.axis_index("x") + lax.axis_index("y")
    core = lax.axis_index("c")

    wpack = _pack_big(w)
    cpack, _ = _pack_vecs([w["rnn_conv_w"], w["ffn_conv_w"]])
    shard = wpack.astype(MXU_DTYPE)
    g_in, gcp = _gather_first(shard[PACK_OFF[0]:PACK_OFF[1]], cpack)
    gw = _split_pack(g_in, 0, 1)
    small = {k: w[k] for k in SMALL}
    small["rnn_conv_w"] = gcp[:, 0:4].reshape(4, 4, 128).transpose(1, 0, 2).reshape(4, 512)
    small["ffn_conv_w"] = gcp[:, 4:22].reshape(4, 3, 768).transpose(1, 0, 2).reshape(3, 3072)

    core1 = core.reshape(1).astype(jnp.int32)
    grad_x, big, sg, loss, ffn_halves, small_all = _layer_grads(x[0], p[0, 0], loss_target[0], gw, small, shard, core1)

    shapes = [sg[k].shape for k in SMALL] + [(1,)]
    _, offs = _pack_vecs([jnp.zeros(s, F32) for s in shapes])
    red = dict(zip(SMALL + ("loss",), _unpack_vecs(_sum_devices(small_all[0]), offs, shapes)))
    red["rnn_conv_w"] = lax.dynamic_slice_in_dim(red["rnn_conv_w"], chip * 128, 128, axis=1)
    red["ffn_conv_w"] = lax.dynamic_slice_in_dim(red["ffn_conv_w"], chip * 768, 768, axis=1)

    g_late = [big[k] for k in LATE_WEIGHTS]
    sib = _run_exchange(_swap_exchange(g_late), "swap_late")
    from_chips = _run_exchange(_scatter_exchange(_add_half(g_late, sib, core1, "add_half_late")), "scatter_late")
    late_mine = _add4(from_chips, "add_chips_late")
    late_other = _run_exchange(_send_exchange(late_mine), "send_late")

    def adamw(names, mine, other, name):
        out, _ = _adamw_halves([w[k] for k in names], mine, other, [m[k] for k in names], [v[k] for k in names],
                               core1, name)
        return dict(zip(names, out))

    big_out = {**adamw(LATE_WEIGHTS, late_mine, late_other, "adamw_late"), **adamw(EARLY_WEIGHTS, *ffn_halves, "adamw_early")}
    wsm, offs2 = _pack_vecs([w[k] for k in SMALL])
    gsm, _ = _pack_vecs([red[k] for k in SMALL])
    msm, _ = _pack_vecs([m[k] for k in SMALL])
    vsm, _ = _pack_vecs([v[k] for k in SMALL])
    dsm, nmsm, nvsm = _adamw(wsm, gsm, msm, vsm, "adamw_small")
    shapes2 = [w[k].shape for k in SMALL]

    def named(n, smallp):
        d = {k: out[n][None] for k, out in big_out.items()}
        d.update({k: a[None] for k, a in zip(SMALL, _unpack_vecs(smallp, offs2, shapes2))})
        return [d[k] for k in WEIGHTS]

    return (red["loss"].reshape(()), grad_x[None], *named(0, gsm), *named(1, dsm), *named(2, nmsm), *named(3, nvsm))
```

```python
import functools

import jax
import jax.numpy as jnp
from jax import lax
from jax.experimental import pallas as pl
from jax.experimental.pallas import tpu as pltpu

F32 = jnp.float32
BF16 = jnp.bfloat16
MXU_DTYPE = jnp.bfloat16

D = 1024
D_ATT = 512
D_KV = 128
D_RNN = 512
D_IN = 1792
D_FF = 3072
FF_CHUNK = 512
PLE = 256
HEADS = 8
HEAD_DIM = 64
BLK = 128
ATTN_BLOCKS = 2
RNN_BLOCKS = 8
LN_EPS = 1e-5
LRU_C = 8.0
ALPHA = float(2.0 ** 0.25)
SCALE = HEAD_DIM ** -0.5
NEG = -1e30

ADAM_LR = 0.001
ADAM_B1 = 0.9
ADAM_B2 = 0.999
ADAM_EPS = 1e-08
ADAM_WD = 0.01
ADAM_STEP = 10

VMEM_LIMIT_BYTES = 56 * 1024 * 1024
MESH = pl.DeviceIdType.MESH

PACK_ROWS = (448, 1536, 256, 768, 256, 64)
PACK_OFF = tuple(sum(PACK_ROWS[:i]) for i in range(len(PACK_ROWS) + 1))
PACK_TOTAL = PACK_OFF[-1]


def _params(**kw):
    return pltpu.CompilerParams(vmem_limit_bytes=VMEM_LIMIT_BYTES, **kw)


def _mm(a, b):
    return jnp.dot(a.astype(MXU_DTYPE), b.astype(MXU_DTYPE), preferred_element_type=F32)


def _mm_nt(a, b):
    return lax.dot_general(a.astype(MXU_DTYPE), b.astype(MXU_DTYPE), (((1,), (1,)), ((), ())),
                           preferred_element_type=F32)


def _mm_tn(a, b):
    return lax.dot_general(a.astype(MXU_DTYPE), b.astype(MXU_DTYPE), (((0,), (0,)), ((), ())),
                           preferred_element_type=F32)


def _sigmoid(x):
    return 1.0 / (1.0 + jnp.exp(-x))


def _gelu(x):
    c = 0.7978845608028654
    k = 0.044715
    x2 = x * x
    t = jnp.tanh(x * (c + (c * k) * x2))
    h = 0.5 * (1.0 + t)
    return x * h, h * (1.0 + (x * (1.0 - t)) * (c + (3.0 * c * k) * x2))


def _shift_rows(x, s, edge8):
    R = x.shape[0]
    row8 = lax.broadcasted_iota(jnp.int32, (8, x.shape[1]), 0)
    if s > 0:
        rolled = pltpu.roll(x, s, 0)
        first = jnp.where(row8 < s, pltpu.roll(edge8, s, 0), rolled[0:8])
        return jnp.concatenate([first, rolled[8:]], axis=0)
    k = -s
    rolled = pltpu.roll(x, R - k, 0)
    last = jnp.where(row8 >= 8 - k, pltpu.roll(edge8, 8 - k, 0), rolled[R - 8:])
    return jnp.concatenate([rolled[:R - 8], last], axis=0)


def _softplus(x):
    return jnp.maximum(x, 0.0) + jnp.log(1.0 + jnp.exp(-jnp.abs(x)))


def _ln(z, g, b):
    mu = jnp.mean(z, axis=-1, keepdims=True)
    zc = z - mu
    var = jnp.mean(zc * zc, axis=-1, keepdims=True)
    rstd = lax.rsqrt(var + LN_EPS)
    xhat = zc * rstd
    return xhat * g + b, xhat, rstd


def _ln_bwd(dy, xhat, rstd, g):
    dxh = dy * g
    m1 = jnp.mean(dxh, axis=-1, keepdims=True)
    m2 = jnp.mean(dxh * xhat, axis=-1, keepdims=True)
    return rstd * (dxh - m1 - xhat * m2)


def _colsum(x):
    return jnp.sum(x, axis=0, keepdims=True)


def _full(shape):
    nd = len(shape)
    return pl.BlockSpec(shape, lambda *_: (0,) * nd)


def _rows(tm, cols, fn=None):
    if fn is None:
        return pl.BlockSpec((tm, cols), lambda i: (i, 0))
    return pl.BlockSpec((tm, cols), lambda i: (fn(i), 0))


def _heads(tm):
    return pl.BlockSpec((HEADS, tm, HEAD_DIM), lambda i: (0, i, 0))


def _in_proj(x, w_in_t):
    T = x.shape[0]
    tm = 512

    def body(x_ref, w_ref, q_ref, kv_ref, xr_ref, gr_ref, xb_ref):
        xb = x_ref[...].astype(MXU_DTYPE)
        xb_ref[...] = xb.astype(BF16)
        q = _mm_nt(xb, w_ref[0:512, :])
        for h in range(HEADS):
            q_ref[h] = q[:, h * 64:(h + 1) * 64].astype(BF16)
        kv_ref[...] = _mm_nt(xb, w_ref[512:768, :]).astype(BF16)
        xr_ref[...] = _mm_nt(xb, w_ref[768:1280, :])
        gr_ref[...] = _mm_nt(xb, w_ref[1280:1792, :])

    return pl.pallas_call(
        body, name="in_proj", grid=(T // tm,),
        in_specs=[_rows(tm, D), _full((D_IN, D))],
        out_specs=[_heads(tm), _rows(tm, 256), _rows(tm, 512), _rows(tm, 512), _rows(tm, D)],
        out_shape=[jax.ShapeDtypeStruct((HEADS, T, 64), BF16), jax.ShapeDtypeStruct((T, 256), BF16),
                   jax.ShapeDtypeStruct((T, 512), F32), jax.ShapeDtypeStruct((T, 512), F32),
                   jax.ShapeDtypeStruct((T, D), BF16)],
        compiler_params=_params(),
    )(x, w_in_t)


def _attn_band(kv_ref, i):
    cur = pl.multiple_of(i * BLK, BLK)
    prev = pl.multiple_of(jnp.maximum(i - 1, 0) * BLK, BLK)
    band = jnp.concatenate([kv_ref[pl.ds(prev, BLK), :], kv_ref[pl.ds(cur, BLK), :]], axis=0)
    key = lax.broadcasted_iota(jnp.int32, (2 * BLK, 4 * BLK), 0)
    qry = lax.broadcasted_iota(jnp.int32, (2 * BLK, 4 * BLK), 1) & (BLK - 1)
    in_prev = jnp.logical_and(jnp.logical_and(key < BLK, key > qry), i > 0)
    mask = jnp.logical_or(in_prev, jnp.logical_and(key >= BLK, key - BLK <= qry))
    return band, mask, cur, prev


def _attn_scores(band, mask, qs, s_ref, g):
    st = jnp.where(mask, _mm_nt(band[:, g * 64:(g + 1) * 64], qs) * SCALE, NEG)
    lane = lax.broadcasted_iota(jnp.int32, (1, 4 * BLK), 1)
    sv = jnp.where(lane < BLK, s_ref[0, 4 * g],
                   jnp.where(lane < 2 * BLK, s_ref[0, 4 * g + 1], jnp.where(lane < 3 * BLK, s_ref[0, 4 * g + 2], s_ref[0, 4 * g + 3])))
    m = jnp.maximum(jnp.max(st, axis=0, keepdims=True), sv)
    p = jnp.exp(st - m)
    ps = jnp.exp(sv - m)
    return p, ps, jnp.sum(p, axis=0, keepdims=True) + ps


def _pos():
    return lax.axis_index("x"), lax.axis_index("y"), lax.axis_index("c")


def _other_chips(x, y):
    return [(1 - x, y), (x, 1 - y), (1 - x, 1 - y)]


def _gather_steps(w_ref, gw_ref, send_sems, recv_sems, local_sem):
    x, y, c = _pos()
    me = 2 * x + y
    chips = _other_chips(x, y)
    half = w_ref.shape[0] // 2
    mine = pl.ds(pl.multiple_of(c * half, 16), half)
    theirs = pl.ds(pl.multiple_of((1 - c) * half, 16), half)
    loc = pltpu.make_async_copy(w_ref, gw_ref.at[me], local_sem)

    def copy(k, src, dst, to):
        return pltpu.make_async_remote_copy(src_ref=src, dst_ref=dst, send_sem=send_sems.at[k], recv_sem=recv_sems.at[k],
                                            device_id=to, device_id_type=MESH)

    def out(k):
        px, py = chips[k]
        return copy(k, w_ref.at[mine], gw_ref.at[me, mine], (px, py, c))

    def fwd(k, rows):
        px, py = chips[k]
        return copy(3 + k, gw_ref.at[2 * px + py, rows], gw_ref.at[2 * px + py, rows], (x, y, 1 - c))

    def start():
        loc.start()
        for k in range(3):
            out(k).start()

    def forward():
        for k in range(3):
            px, py = chips[k]
            copy(k, w_ref.at[mine], gw_ref.at[2 * px + py, mine], (px, py, c)).wait_recv()
            fwd(k, mine).start()

    def finish():
        for k in range(3):
            fwd(k, theirs).wait_recv()
        for k in range(3):
            out(k).wait_send()
            fwd(k, mine).wait_send()
        loc.wait()

    return start, forward, finish


GATHER_SCRATCH = [pltpu.SemaphoreType.DMA((6,)), pltpu.SemaphoreType.DMA((6,)), pltpu.SemaphoreType.DMA]


class _Exchange:
    def __init__(self, args, out_shape, scratch, make):
        self.args, self.out_shape, self.scratch, self.make = list(args), list(out_shape), list(scratch), make


def _gather_exchange(wsrc):
    return _Exchange([wsrc], [jax.ShapeDtypeStruct((4,) + wsrc.shape, wsrc.dtype)], GATHER_SCRATCH,
                     lambda ins, outs, sems: _gather_steps(ins[0], outs[0], *sems))


def _launch(body, name, grid, in_specs, out_specs, out_shape, scratch, args, exchange=None, prefetch=0):
    def call(fn, fn_name, ins, outs, shapes, scr, operands, effects):
        spec = pltpu.PrefetchScalarGridSpec(num_scalar_prefetch=prefetch, grid=grid, in_specs=ins, out_specs=outs,
                                            scratch_shapes=scr)
        return pl.pallas_call(fn, name=fn_name, grid_spec=spec, out_shape=shapes,
                              compiler_params=_params(has_side_effects=effects))(*operands)

    if exchange is None:
        return call(body, name, list(in_specs), list(out_specs), list(out_shape), list(scratch), args, False)
    n_in, n_out, ei, eo, ns = len(in_specs), len(out_specs), len(exchange.args), len(exchange.out_shape), len(exchange.scratch)
    nsteps = 1
    for g in grid:
        nsteps *= g

    def wrapped(*refs):
        scalars, refs = refs[:prefetch], refs[prefetch:]
        ins, xin = refs[:n_in], refs[n_in:n_in + ei]
        outs, xout = refs[n_in + ei:n_in + ei + n_out], refs[n_in + ei + n_out:n_in + ei + n_out + eo]
        rest = refs[n_in + ei + n_out + eo:]
        own, sems = rest[:len(rest) - ns], rest[len(rest) - ns:]
        start, forward, finish = exchange.make(xin, xout, sems)
        i = pl.program_id(0)
        for d in range(1, len(grid)):
            i = i * grid[d] + pl.program_id(d)
        pl.when(i == 0)(start)
        body(*scalars, *ins, *outs, *own)
        pl.when(i == max(nsteps - 3, 0))(forward)
        pl.when(i == nsteps - 1)(finish)

    anyspec = pl.BlockSpec(memory_space=pl.ANY)
    return call(wrapped, name + "_x", list(in_specs) + [anyspec] * ei, list(out_specs) + [anyspec] * eo,
                list(out_shape) + exchange.out_shape, list(scratch) + exchange.scratch, (*args, *exchange.args), True)


def _attn_fwd(q, kv, sinks, exchange=None):
    T = kv.shape[0]

    def body(q_ref, kv_ref, s_ref, o_ref):
        for b in range(ATTN_BLOCKS):
            rows = slice(b * BLK, (b + 1) * BLK)
            band, mask, _, _ = _attn_band(kv_ref, ATTN_BLOCKS * pl.program_id(0) + b)
            for g in range(2):
                qs = q_ref[4 * g:4 * g + 4, rows, :].reshape(4 * BLK, HEAD_DIM)
                p, _, den = _attn_scores(band, mask, qs, s_ref, g)
                ot = _mm_tn(band[:, 128:256], p) / den
                for hh in range(4):
                    o = ot[:, hh * BLK:(hh + 1) * BLK].T
                    o_ref[rows, (4 * g + hh) * 64:(4 * g + hh + 1) * 64] = o[:, g * 64:(g + 1) * 64].astype(BF16)

    tq = ATTN_BLOCKS * BLK
    return _launch(body, "attn_fwd", (T // tq,), [_heads(tq), _full((T, 256)), pl.BlockSpec(memory_space=pltpu.SMEM)],
                   [_rows(tq, 512)], [jax.ShapeDtypeStruct((T, 512), BF16)], [], (q, kv, sinks), exchange)


def _attn_bwd(q, kv, do, sinks, exchange=None):
    T = kv.shape[0]

    def body(q_ref, kv_ref, do_ref, s_ref, dq_ref, dkv_ref, ds_ref):
        @pl.when(pl.program_id(0) == 0)
        def _():
            ds_ref[...] = jnp.zeros_like(ds_ref)

        for b in range(ATTN_BLOCKS):
            rows = slice(b * BLK, (b + 1) * BLK)
            band, mask, cur, prev = _attn_band(kv_ref, ATTN_BLOCKS * pl.program_id(0) + b)
            for g in range(2):
                qs = q_ref[4 * g:4 * g + 4, rows, :].reshape(4 * BLK, HEAD_DIM)
                dos = do_ref[4 * g:4 * g + 4, rows, :].reshape(4 * BLK, HEAD_DIM)
                p, ps, den = _attn_scores(band, mask, qs, s_ref, g)
                inv = 1.0 / den
                p = p * inv
                dpt = _mm_nt(band[:, 128 + g * 64:192 + g * 64], dos)
                delta = jnp.sum(p * dpt, axis=0, keepdims=True)
                dst = p * (dpt - delta)
                dsv = -(ps * inv) * delta
                for hh in range(4):
                    dsink = jnp.sum(dsv[:, hh * BLK:(hh + 1) * BLK], axis=1, keepdims=True)
                    ds_ref[4 * g + hh:4 * g + hh + 1, :] += jnp.broadcast_to(dsink, (1, 128))
                dqt = _mm_tn(band[:, 0:128], dst) * SCALE
                for hh in range(4):
                    dqh = dqt[:, hh * BLK:(hh + 1) * BLK].T
                    dq_ref[rows, (4 * g + hh) * 64:(4 * g + hh + 1) * 64] = dqh[:, g * 64:(g + 1) * 64].astype(BF16)
                dk = _mm(dst, qs) * SCALE
                dv = _mm(p, dos)
                dkv_ref[pl.ds(cur, BLK), g * 64:(g + 1) * 64] = dk[BLK:2 * BLK]
                dkv_ref[pl.ds(cur, BLK), 128 + g * 64:192 + g * 64] = dv[BLK:2 * BLK]
                dkv_ref[pl.ds(prev, BLK), g * 64:(g + 1) * 64] += dk[0:BLK]
                dkv_ref[pl.ds(prev, BLK), 128 + g * 64:192 + g * 64] += dv[0:BLK]

    tq = ATTN_BLOCKS * BLK
    return _launch(body, "attn_bwd", (T // tq,),
                   [_heads(tq), _full((T, 256)), _heads(tq), pl.BlockSpec(memory_space=pltpu.SMEM)],
                   [_rows(tq, 512), _full((T, 256)), _full((8, 128))],
                   [jax.ShapeDtypeStruct((T, 512), BF16), jax.ShapeDtypeStruct((T, 256), F32),
                    jax.ShapeDtypeStruct((8, 128), F32)], [], (q, kv, do, sinks), exchange)


def _rows8(tm, cols):
    return lax.broadcasted_iota(jnp.int32, (tm, cols), 0) & 7


def _lru_gates(xc, wa, ba, wx, bx, lam):
    r = _sigmoid(_mm(xc, wa) + ba)
    ii = _sigmoid(_mm(xc, wx) + bx)
    sp = _softplus(-lam)
    la = -LRU_C * r * sp
    a = jnp.exp(la)
    m = jnp.sqrt(-jnp.tanh(la) * (a * a + 1.0))
    return r, ii, sp, a, m


def _rnn_fwd(xr, gr, cw, cb, wa, ba, wx, bx, lam, exchange=None):
    T = xr.shape[0]
    tm = 512
    C = D_RNN

    def body(xr_ref, gr_ref, cw_ref, cb_ref, wa_ref, ba_ref, wx_ref, bx_ref, lam_ref,
             xc_ref, h_ref, rec_ref, ext, a_s, b_s, carry):
        i = pl.program_id(0)

        @pl.when(i == 0)
        def _():
            ext[...] = jnp.zeros((8, C), F32)
            carry[...] = jnp.zeros((8, C), F32)

        xr = xr_ref[...]
        edge = ext[...]
        xc = cb_ref[...] + cw_ref[3:4, :] * xr
        for k in range(3):
            xc = xc + cw_ref[k:k + 1, :] * _shift_rows(xr, 3 - k, edge)
        ext[...] = xr[tm - 8:tm, :]
        xc_ref[...] = xc
        _, ii, _, a, m = _lru_gates(xc, wa_ref[...], ba_ref[...], wx_ref[...], bx_ref[...], lam_ref[...])
        b = m * ii * xc
        r8 = _rows8(tm, C)
        for d in (1, 2, 4):
            ok = r8 >= d
            a_sh = jnp.where(ok, pltpu.roll(a, d, 0), 1.0)
            b_sh = jnp.where(ok, pltpu.roll(b, d, 0), 0.0)
            b = a * b_sh + b
            a = a * a_sh
        a_s[...] = a
        b_s[...] = b

        def step(g, hin):
            s = pl.multiple_of(g * 8, 8)
            hg = a_s[pl.ds(s, 8), :] * hin + b_s[pl.ds(s, 8), :]
            h_ref[pl.ds(s, 8), :] = hg
            return jnp.broadcast_to(hg[7:8, :], (8, C))

        carry[...] = lax.fori_loop(0, tm // 8, step, carry[...])
        ge, _ = _gelu(gr_ref[...])
        rec_ref[...] = (h_ref[...] * ge).astype(BF16)

    vec = _full((1, C))
    in_specs = [_rows(tm, C), _rows(tm, C), _full((4, C)), vec, _full((C, C)), vec, _full((C, C)), vec, vec]
    out_specs = [_rows(tm, C), _rows(tm, C), _rows(tm, C)]
    out_shape = [jax.ShapeDtypeStruct((T, C), F32), jax.ShapeDtypeStruct((T, C), F32), jax.ShapeDtypeStruct((T, C), BF16)]
    scratch = [pltpu.VMEM((8, C), F32), pltpu.VMEM((tm, C), F32), pltpu.VMEM((tm, C), F32), pltpu.VMEM((8, C), F32)]
    return _launch(body, "rnn_fwd", (T // tm,), in_specs, out_specs, out_shape, scratch,
                   (xr, gr, cw, cb, wa, ba, wx, bx, lam), exchange)


def _rnn_bwd(drec, gr, h, xc, xr, cw, wa, ba, wx, bx, lam, exchange=None):
    T = xr.shape[0]
    tm = 512
    C = D_RNN
    nt = T // tm
    t8 = tm // 8

    def body(drec_ref, gr_ref, h_ref, hp_ref, xc_ref, xr_ref, cw_ref, wa_ref, ba_ref, wx_ref, bx_ref,
             lam_ref, dxr_ref, dgr_ref, dwa_ref, dwx_ref, dvec_ref, c_s, g_s, gout, ext, anext, gcarry):
        i = pl.program_id(0)
        j = nt - 1 - i

        @pl.when(i == 0)
        def _():
            dwa_ref[...] = jnp.zeros_like(dwa_ref)
            dwx_ref[...] = jnp.zeros_like(dwx_ref)
            dvec_ref[...] = jnp.zeros_like(dvec_ref)
            anext[...] = jnp.zeros((8, C), F32)
            gcarry[...] = jnp.zeros((8, C), F32)
            ext[...] = jnp.zeros((8, C), F32)

        xc = xc_ref[...]
        lam = lam_ref[...]
        r, ii, sp, a, m = _lru_gates(xc, wa_ref[...], ba_ref[...], wx_ref[...], bx_ref[...], lam)
        ge, dge = _gelu(gr_ref[...])
        drec = drec_ref[...]
        hh = h_ref[...]
        dgr_ref[...] = (drec * hh * dge).astype(BF16)
        dh = drec * ge
        rowi = lax.broadcasted_iota(jnp.int32, (tm, C), 0)
        c = jnp.where(rowi == tm - 1, jnp.broadcast_to(anext[0:1, :], (tm, C)), pltpu.roll(a, tm - 1, 0))
        anext[...] = a[0:8, :]
        r8 = rowi & 7
        gg = dh
        for d in (1, 2, 4):
            ok = r8 < 8 - d
            c_sh = jnp.where(ok, pltpu.roll(c, tm - d, 0), 1.0)
            g_sh = jnp.where(ok, pltpu.roll(gg, tm - d, 0), 0.0)
            gg = c * g_sh + gg
            c = c * c_sh
        c_s[...] = c
        g_s[...] = gg

        def step(k, gin):
            s = pl.multiple_of((t8 - 1 - k) * 8, 8)
            og = c_s[pl.ds(s, 8), :] * gin + g_s[pl.ds(s, 8), :]
            gout[pl.ds(s, 8), :] = og
            return jnp.broadcast_to(og[0:1, :], (8, C))

        gcarry[...] = lax.fori_loop(0, t8, step, gcarry[...])
        G = gout[...]
        hprev_row = jnp.where(j > 0, hp_ref[7:8, :], 0.0)
        hprev = jnp.where(rowi == 0, jnp.broadcast_to(hprev_row, (tm, C)), pltpu.roll(hh, 1, 0))
        da = G * hprev
        dm = G * ii * xc
        di = G * m * xc
        dxc = G * m * ii
        dla = da * a - dm * a * a / m
        dr = dla * (-LRU_C * sp)
        dsp = _colsum(dla * (-LRU_C * r))
        dlam = dsp * (-_sigmoid(-lam))
        dpr = dr * r * (1.0 - r)
        dpi = di * ii * (1.0 - ii)
        dxc = dxc + _mm_nt(dpr, wa_ref[...]) + _mm_nt(dpi, wx_ref[...])
        dwa_ref[...] += _mm_tn(xc, dpr)
        dwx_ref[...] += _mm_tn(xc, dpi)
        dvec_ref[0:1, :] += _colsum(dpr)
        dvec_ref[1:2, :] += _colsum(dpi)
        dvec_ref[2:3, :] += dlam
        dvec_ref[3:4, :] += _colsum(dxc)
        edge = ext[...]
        xr = xr_ref[...]
        dxr = cw_ref[3:4, :] * dxc
        dvec_ref[7:8, :] += _colsum(dxc * xr)
        for k in range(3):
            up = _shift_rows(dxc, k - 3, edge)
            dxr = dxr + cw_ref[k:k + 1, :] * up
            dvec_ref[4 + k:5 + k, :] += _colsum(up * xr)
        ext[...] = dxc[0:8, :]
        dxr_ref[...] = dxr.astype(BF16)

    rev = lambda i: nt - 1 - i
    prev8 = lambda i: jnp.maximum((nt - 1 - i) * t8 - 1, 0)
    vec = _full((1, C))
    return _launch(
        body, "rnn_bwd", (nt,),
        [_rows(tm, C, rev), _rows(tm, C, rev), _rows(tm, C, rev), _rows(8, C, prev8), _rows(tm, C, rev),
         _rows(tm, C, rev), _full((4, C)), _full((C, C)), vec, _full((C, C)), vec, vec],
        [_rows(tm, C, rev), _rows(tm, C, rev), _full((C, C)), _full((C, C)), _full((8, C))],
        [jax.ShapeDtypeStruct((T, C), BF16), jax.ShapeDtypeStruct((T, C), BF16),
         jax.ShapeDtypeStruct((C, C), F32), jax.ShapeDtypeStruct((C, C), F32), jax.ShapeDtypeStruct((8, C), F32)],
        [pltpu.VMEM((tm, C), F32), pltpu.VMEM((tm, C), F32), pltpu.VMEM((tm, C), F32),
         pltpu.VMEM((8, C), F32), pltpu.VMEM((8, C), F32), pltpu.VMEM((8, C), F32)],
        (drec, gr, h, h, xc, xr, cw, wa, ba, wx, bx, lam), exchange)


def _out_proj(att, rec, x, w_out, g1, b1):
    T = x.shape[0]
    tm = 512

    def body(att_ref, rec_ref, x_ref, w_ref, g1_ref, b1_ref, z_ref, h_ref):
        mix = _mm(att_ref[...], w_ref[0:512, :]) + _mm(rec_ref[...], w_ref[512:1024, :])
        z1 = ALPHA * x_ref[...] + mix
        z_ref[...] = z1
        h1, _, _ = _ln(z1, g1_ref[...], b1_ref[...])
        h_ref[...] = h1.astype(MXU_DTYPE).astype(BF16)

    return pl.pallas_call(
        body, name="out_proj", grid=(T // tm,),
        in_specs=[_rows(tm, 512), _rows(tm, 512), _rows(tm, D), _full((D, D)), _full((1, D)), _full((1, D))],
        out_specs=[_rows(tm, D), _rows(tm, D)],
        out_shape=[jax.ShapeDtypeStruct((T, D), F32), jax.ShapeDtypeStruct((T, D), BF16)],
        compiler_params=_params(),
    )(att, rec, x, w_out, g1, b1)


NC = D_FF // FF_CHUNK


def _ffn_up(h1b, w_up_t, fcw, fcb, exchange=None):
    T = h1b.shape[0]
    tm = min(1024, T)
    CW = FF_CHUNK

    def body(h_ref, wg_ref, wv_ref, fcw_ref, fcb_ref, gate_ref, ge_ref, vd_ref, act_ref, before):
        i = pl.program_id(1)

        @pl.when(i == 0)
        def _():
            before[...] = jnp.zeros((8, CW), F32)

        hb = h_ref[...]
        gate = _mm_nt(hb, wg_ref[...])
        val = _mm_nt(hb, wv_ref[...])
        gate_ref[...] = gate.astype(BF16)
        edge = before[...]
        gc = (fcb_ref[...] + fcw_ref[0:1, :] * _shift_rows(gate, 2, edge) + fcw_ref[1:2, :] * _shift_rows(gate, 1, edge)
              + fcw_ref[2:3, :] * gate)
        before[...] = gate[tm - 8:tm, :]
        ge, dge = _gelu(gc)
        ge_ref[...] = ge.astype(BF16)
        vd_ref[...] = (val * dge).astype(BF16)
        act_ref[...] = (ge * val).astype(BF16)

    chunk = pl.BlockSpec((None, tm, CW), lambda c, i: (c, i, 0))
    return _launch(
        body, "ffn_up", (NC, T // tm),
        [pl.BlockSpec((tm, D), lambda c, i: (i, 0)), pl.BlockSpec((CW, D), lambda c, i: (c, 0)),
         pl.BlockSpec((CW, D), lambda c, i: (NC + c, 0)), pl.BlockSpec((None, 3, CW), lambda c, i: (c, 0, 0)),
         pl.BlockSpec((None, 1, CW), lambda c, i: (c, 0, 0))],
        [chunk] * 4, [jax.ShapeDtypeStruct((NC, T, CW), BF16)] * 4, [pltpu.VMEM((8, CW), F32)],
        (h1b, w_up_t, w_up_t, fcw, fcb), exchange)


def _ffn_down(act, z1, p, tgt, w_down, w_g, w_p_t, g1, b1, g2, b2, bg):
    T = z1.shape[0]
    tm = 256

    def body(act_ref, z_ref, p_ref, t_ref, wdn_hbm, wg_hbm, wp_hbm, g1_ref, b1_ref, g2_ref, b2_ref, bg_ref,
             dz2_ref, dz2b_ref, dpre_ref, dpp_ref, vec_ref, wdn, wg, wp):
        @pl.when(pl.program_id(0) == 0)
        def _():
            pltpu.sync_copy(wdn_hbm, wdn)
            pltpu.sync_copy(wg_hbm, wg)
            pltpu.sync_copy(wp_hbm, wp)
            vec_ref[...] = jnp.zeros_like(vec_ref)

        g2v = g2_ref[...]
        h1, _, _ = _ln(z_ref[...], g1_ref[...], b1_ref[...])
        h1b = h1.astype(MXU_DTYPE)
        ffn = _mm(act_ref[0], wdn[0:FF_CHUNK, :])
        for c in range(1, NC):
            ffn = ffn + _mm(act_ref[c], wdn[c * FF_CHUNK:(c + 1) * FF_CHUNK, :])
        sg = _sigmoid(_mm(h1b, wg[...]) + bg_ref[...])
        pp = _mm_nt(p_ref[...], wp[...])
        z2 = ALPHA * h1 + ffn + sg * pp
        y, xh2, rstd2 = _ln(z2, g2v, b2_ref[...])
        diff = y - t_ref[...]
        dy = diff * (1.0 / D)
        dz2 = _ln_bwd(dy, xh2, rstd2, g2v)
        dpre = dz2 * pp * sg * (1.0 - sg)
        dz2_ref[...] = dz2
        dz2b_ref[...] = dz2.astype(BF16)
        dpre_ref[...] = dpre.astype(BF16)
        dpp_ref[...] = (dz2 * sg).astype(BF16)
        loss = 0.5 * jnp.sum(jnp.sum(diff * diff, axis=1, keepdims=True), axis=0, keepdims=True) * (1.0 / D)
        vec_ref[0:1, :] += jnp.broadcast_to(loss, (1, D))
        vec_ref[1:2, :] += _colsum(dy * xh2)
        vec_ref[2:3, :] += _colsum(dy)
        vec_ref[3:4, :] += _colsum(dpre)

    anyspec = pl.BlockSpec(memory_space=pl.ANY)
    vec = _full((1, D))
    return pl.pallas_call(
        body, name="ffn_down", grid=(T // tm,),
        in_specs=[pl.BlockSpec((NC, tm, FF_CHUNK), lambda i: (0, i, 0)), _rows(tm, D), _rows(tm, PLE), _rows(tm, D),
                  anyspec, anyspec, anyspec] + [vec] * 5,
        out_specs=[_rows(tm, D)] * 4 + [_full((8, D))],
        out_shape=[jax.ShapeDtypeStruct((T, D), F32)] + [jax.ShapeDtypeStruct((T, D), BF16)] * 3
                  + [jax.ShapeDtypeStruct((8, D), F32)],
        scratch_shapes=[pltpu.VMEM((D_FF, D), MXU_DTYPE), pltpu.VMEM((D, D), MXU_DTYPE), pltpu.VMEM((D, PLE), MXU_DTYPE)],
        compiler_params=_params(),
    )(act, z1, p, tgt, w_down, w_g, w_p_t, g1, b1, g2, b2, bg)


def _ffn_bwd(dz2b, gate, ge, vd, w_down, fcw):
    T = dz2b.shape[0]
    tm = min(1024, T)
    CW = FF_CHUNK
    nt = T // tm

    def body(dz_ref, wdn_ref, gate_ref, ge_ref, vd_ref, fcw_ref, dup_ref, dfc_ref, after):
        i = pl.program_id(1)

        @pl.when(i == 0)
        def _():
            after[...] = jnp.zeros((8, CW), F32)
            dfc_ref[...] = jnp.zeros_like(dfc_ref)

        gate = gate_ref[...].astype(F32)
        dact = _mm_nt(dz_ref[...], wdn_ref[...])
        dgc = dact * vd_ref[...].astype(F32)
        edge = after[...]
        dgc1 = _shift_rows(dgc, -1, edge)
        dgc2 = _shift_rows(dgc, -2, edge)
        after[...] = dgc[0:8, :]
        dup_ref[0] = (fcw_ref[2:3, :] * dgc + fcw_ref[1:2, :] * dgc1 + fcw_ref[0:1, :] * dgc2).astype(BF16)
        dup_ref[1] = (dact * ge_ref[...].astype(F32)).astype(BF16)
        dfc_ref[0:1, :] += _colsum(dgc2 * gate)
        dfc_ref[1:2, :] += _colsum(dgc1 * gate)
        dfc_ref[2:3, :] += _colsum(dgc * gate)
        dfc_ref[3:4, :] += _colsum(dgc)

    rev = lambda c, i: (c, nt - 1 - i, 0)
    chunk = pl.BlockSpec((None, tm, CW), rev)
    return pl.pallas_call(
        body, name="ffn_bwd", grid=(NC, nt),
        in_specs=[pl.BlockSpec((tm, D), lambda c, i: (nt - 1 - i, 0)), pl.BlockSpec((CW, D), lambda c, i: (c, 0)),
                  chunk, chunk, chunk, pl.BlockSpec((None, 3, CW), lambda c, i: (c, 0, 0))],
        out_specs=[pl.BlockSpec((None, 2, tm, CW), lambda c, i: (c, 0, nt - 1 - i, 0)),
                   pl.BlockSpec((None, 8, CW), lambda c, i: (c, 0, 0))],
        out_shape=[jax.ShapeDtypeStruct((NC, 2, T, CW), BF16), jax.ShapeDtypeStruct((NC, 8, CW), F32)],
        scratch_shapes=[pltpu.VMEM((8, CW), F32)],
        compiler_params=_params(),
    )(dz2b, w_down, gate, ge, vd, fcw)


def _ffn_dh1(dup, dz2, dpre, z1, w_up_t, w_g, g1, b1):
    T = z1.shape[0]
    tm = 256

    def body(dup_ref, dz2_ref, dpre_ref, z_ref, wup_hbm, wg_hbm, g1_ref, b1_ref, dz1_ref, vec_ref, wup, wg):
        @pl.when(pl.program_id(0) == 0)
        def _():
            pltpu.sync_copy(wup_hbm, wup)
            pltpu.sync_copy(wg_hbm, wg)
            vec_ref[...] = jnp.zeros_like(vec_ref)

        g1v = g1_ref[...]
        _, xh1, rstd1 = _ln(z_ref[...], g1v, b1_ref[...])
        dh1 = ALPHA * dz2_ref[...] + _mm_nt(dpre_ref[...], wg[...])
        for c in range(NC):
            for s in range(2):
                r0 = s * D_FF + c * FF_CHUNK
                dh1 = dh1 + _mm(dup_ref[c, s], wup[r0:r0 + FF_CHUNK, :])
        dz1_ref[...] = _ln_bwd(dh1, xh1, rstd1, g1v)
        vec_ref[0:1, :] += _colsum(dh1 * xh1)
        vec_ref[1:2, :] += _colsum(dh1)

    anyspec = pl.BlockSpec(memory_space=pl.ANY)
    vec = _full((1, D))
    return pl.pallas_call(
        body, name="ffn_dh1", grid=(T // tm,),
        in_specs=[pl.BlockSpec((NC, 2, tm, FF_CHUNK), lambda i: (0, 0, i, 0)), _rows(tm, D), _rows(tm, D), _rows(tm, D),
                  anyspec, anyspec, vec, vec],
        out_specs=[_rows(tm, D), _full((8, D))],
        out_shape=[jax.ShapeDtypeStruct((T, D), F32), jax.ShapeDtypeStruct((8, D), F32)],
        scratch_shapes=[pltpu.VMEM((2 * D_FF, D), MXU_DTYPE), pltpu.VMEM((D, D), MXU_DTYPE)],
        compiler_params=_params(),
    )(dup, dz2, dpre, z1, w_up_t, w_g, g1, b1)


def _out_proj_bwd(dz1, w_out, exchange=None):
    T = dz1.shape[0]
    tm = 512

    def body(dz_ref, w_ref, datt_ref, drec_ref):
        dzb = dz_ref[...].astype(MXU_DTYPE)
        datt = _mm_nt(dzb, w_ref[0:512, :])
        for h in range(HEADS):
            datt_ref[h] = datt[:, h * 64:(h + 1) * 64].astype(BF16)
        drec_ref[...] = _mm_nt(dzb, w_ref[512:1024, :])

    return _launch(body, "out_proj_bwd", (T // tm,), [_rows(tm, D), _full((D, D))], [_heads(tm), _rows(tm, 512)],
                   [jax.ShapeDtypeStruct((HEADS, T, 64), BF16), jax.ShapeDtypeStruct((T, 512), F32)], [],
                   (dz1, w_out), exchange)


def _in_proj_bwd(dq, dkv, dxr, dgr, dz1, w_in_t, exchange=None):
    T = dz1.shape[0]
    tm = 512
    W = D_IN // 4

    def body(dq_ref, dkv_ref, dxr_ref, dgr_ref, dz_ref, w_ref, dx_ref, du_ref):
        dkv = dkv_ref[...]
        dx_ref[...] = (ALPHA * dz_ref[...] + _mm(dq_ref[...], w_ref[0:512, :]) + _mm(dkv, w_ref[512:768, :])
                       + _mm(dxr_ref[...], w_ref[768:1280, :]) + _mm(dgr_ref[...], w_ref[1280:1792, :]))
        dq, dxr, dgr = dq_ref[...].astype(F32), dxr_ref[...].astype(F32), dgr_ref[...].astype(F32)
        du_ref[0] = dq[:, 0:W].astype(BF16)
        du_ref[1, :, 0:64] = dq[:, W:512].astype(BF16)
        du_ref[1, :, 64:320] = dkv.astype(BF16)
        du_ref[1, :, 320:W] = dxr[:, 0:128].astype(BF16)
        du_ref[2, :, 0:384] = dxr[:, 128:512].astype(BF16)
        du_ref[2, :, 384:W] = dgr[:, 0:64].astype(BF16)
        du_ref[3] = dgr[:, 64:512].astype(BF16)

    return _launch(body, "in_proj_bwd", (T // tm,),
                   [_rows(tm, 512), _rows(tm, 256), _rows(tm, 512), _rows(tm, 512), _rows(tm, D), _full((D_IN, D))],
                   [_rows(tm, D), pl.BlockSpec((4, tm, W), lambda i: (0, i, 0))],
                   [jax.ShapeDtypeStruct((T, D), F32), jax.ShapeDtypeStruct((4, T, W), BF16)], [],
                   (dq, dkv, dxr, dgr, dz1, w_in_t), exchange)


def _accumulate_tn(a_ref, b_ref, o_ref):
    @pl.when(pl.program_id(1) == 0)
    def _():
        o_ref[...] = jnp.zeros_like(o_ref)

    o_ref[...] += _mm_tn(a_ref[...], b_ref[...])


def _weight_grad_cols(a, b, name, n_blocks, b_spec, out_shape, out_spec, exchange=None):
    T, M = a.shape
    bt = min(2048, T)
    return _launch(functools.partial(_accumulate_tn), name, (n_blocks, T // bt),
                   [pl.BlockSpec((bt, M), lambda m, k: (k, 0)), b_spec(bt)], [out_spec],
                   [jax.ShapeDtypeStruct(out_shape, F32)], [], (a, b), exchange)


def _weight_grad(a, b, bm, name):
    bt = min(2048, b.shape[0])
    if a.ndim == 3:
        assert a.shape[2] == bm
        T, M = a.shape[1], a.shape[0] * bm
        a_spec = pl.BlockSpec((None, bt, bm), lambda m, k: (m, k, 0))
    else:
        T, M = a.shape
        a_spec = pl.BlockSpec((bt, bm), lambda m, k: (k, m))
    N = b.shape[1]
    nk = T // bt

    return pl.pallas_call(
        functools.partial(_accumulate_tn), name=name, grid=(M // bm, nk),
        in_specs=[a_spec, pl.BlockSpec((bt, N), lambda m, k: (k, 0))],
        out_specs=pl.BlockSpec((bm, N), lambda m, k: (m, 0)),
        out_shape=jax.ShapeDtypeStruct((M, N), F32),
        compiler_params=_params(),
    )(a, b)


def _adamw(w, g, m, v, name):
    R, C = w.shape
    tr = R // 8 if R % 64 == 0 else R
    c1 = 1.0 / (1.0 - ADAM_B1 ** ADAM_STEP)
    c2 = 1.0 / (1.0 - ADAM_B2 ** ADAM_STEP)

    def body(w_ref, g_ref, m_ref, v_ref, d_ref, nm_ref, nv_ref):
        g = g_ref[...]
        nm = ADAM_B1 * m_ref[...] + (1.0 - ADAM_B1) * g
        nv = ADAM_B2 * v_ref[...] + (1.0 - ADAM_B2) * g * g
        nm_ref[...] = nm
        nv_ref[...] = nv
        d_ref[...] = -ADAM_LR * ((nm * c1) / (jnp.sqrt(nv * c2) + ADAM_EPS) + ADAM_WD * w_ref[...])

    spec = pl.BlockSpec((tr, C), lambda i: (i, 0))
    return pl.pallas_call(
        body, name=name, grid=(R // tr,),
        in_specs=[spec] * 4, out_specs=[spec] * 3,
        out_shape=[jax.ShapeDtypeStruct((R, C), F32)] * 3,
        compiler_params=_params(),
    )(w, g, m, v)


def _adamw_halves(ws, mines, sibs, ms, vs, c, name, exchange=None):
    n, nb = len(ws), 4
    c1 = 1.0 / (1.0 - ADAM_B1 ** ADAM_STEP)
    c2 = 1.0 / (1.0 - ADAM_B2 ** ADAM_STEP)

    def body(c_ref, *refs):
        own = (pl.program_id(0) // nb) == c_ref[0]
        for i in range(n):
            w_ref, a_ref, b_ref, m_ref, v_ref = refs[5 * i:5 * i + 5]
            g_ref, d_ref, nm_ref, nv_ref = refs[5 * n + 4 * i:5 * n + 4 * i + 4]
            g = jnp.where(own, a_ref[...], b_ref[...])
            nm = ADAM_B1 * m_ref[...] + (1.0 - ADAM_B1) * g
            nv = ADAM_B2 * v_ref[...] + (1.0 - ADAM_B2) * g * g
            g_ref[...] = g
            nm_ref[...] = nm
            nv_ref[...] = nv
            d_ref[...] = -ADAM_LR * ((nm * c1) / (jnp.sqrt(nv * c2) + ADAM_EPS) + ADAM_WD * w_ref[...])

    in_specs, out_specs, out_shape, args = [], [], [], []
    for w, a, b, m, v in zip(ws, mines, sibs, ms, vs):
        R, C = w.shape
        tr = R // (2 * nb)
        assert tr % 8 == 0 and a.shape == (R // 2, C)
        full = pl.BlockSpec((tr, C), lambda i, c_ref: (i, 0))
        half = pl.BlockSpec((tr, C), lambda i, c_ref: (i % nb, 0))
        in_specs += [full, half, half, full, full]
        out_specs += [full] * 4
        out_shape += [jax.ShapeDtypeStruct((R, C), F32)] * 4
        args += [w, a, b, m, v]
    out = _launch(body, name, (2 * nb,), in_specs, out_specs, out_shape, [], (c, *args), exchange, prefetch=1)
    return [tuple(out[4 * i:4 * i + 4]) for i in range(n)], list(out[4 * n:])


def _add4(fs, name):
    n = len(fs)

    def body(*refs):
        for a_ref, o_ref in zip(refs[:n], refs[n:]):
            o_ref[...] = ((a_ref[0].astype(F32) + a_ref[1].astype(F32)) + a_ref[2].astype(F32)) + a_ref[3].astype(F32)

    for f in fs:
        assert (f.shape[1] // 2) % 16 == 0
    return pl.pallas_call(
        body, name=name, grid=(2,),
        in_specs=[pl.BlockSpec((4, f.shape[1] // 2, f.shape[2]), lambda i: (0, i, 0)) for f in fs],
        out_specs=[pl.BlockSpec((f.shape[1] // 2, f.shape[2]), lambda i: (i, 0)) for f in fs],
        out_shape=[jax.ShapeDtypeStruct(f.shape[1:], F32) for f in fs], compiler_params=_params())(*fs)


def _gather_first(wsrc, cpack):
    def body(w_ref, c_ref, gw_ref, gc_ref, send_sems, recv_sems, local_sem, csend, crecv, clocal):
        x, y, c = _pos()
        me = 2 * x + y
        chips = _other_chips(x, y)
        start, forward, finish = _gather_steps(w_ref, gw_ref, send_sems, recv_sems, local_sem)
        start()
        loc = pltpu.make_async_copy(c_ref, gc_ref.at[me], clocal)
        loc.start()

        def conv_copy(k, slot):
            px, py = chips[k]
            return pltpu.make_async_remote_copy(src_ref=c_ref, dst_ref=gc_ref.at[slot], send_sem=csend.at[k],
                                                recv_sem=crecv.at[k], device_id=(px, py, c), device_id_type=MESH)

        for k in range(3):
            conv_copy(k, me).start()
        forward()
        finish()
        for k, (px, py) in enumerate(chips):
            conv_copy(k, 2 * px + py).wait_recv()
        for k in range(3):
            conv_copy(k, me).wait_send()
        loc.wait()

    anyspec = pl.BlockSpec(memory_space=pl.ANY)
    return pl.pallas_call(
        body, name="gather_first",
        in_specs=[anyspec, anyspec], out_specs=[anyspec, anyspec],
        out_shape=[jax.ShapeDtypeStruct((4,) + wsrc.shape, wsrc.dtype), jax.ShapeDtypeStruct((4,) + cpack.shape, cpack.dtype)],
        scratch_shapes=GATHER_SCRATCH + [pltpu.SemaphoreType.DMA((3,)), pltpu.SemaphoreType.DMA((3,)), pltpu.SemaphoreType.DMA],
        compiler_params=_params(has_side_effects=True),
    )(wsrc, cpack)


def _all_devices_exchange(s):
    def make(ins, outs, sems):
        s_ref, o_ref = ins[0], outs[0]
        send_sems, recv_sems, local_sem = sems
        x, y, c = _pos()
        me = 4 * x + 2 * y + c
        loc = pltpu.make_async_copy(s_ref, o_ref.at[me], local_sem)

        def copy(k, slot):
            peer = (x ^ (k >> 2), y ^ ((k >> 1) & 1), c ^ (k & 1))
            return pltpu.make_async_remote_copy(src_ref=s_ref, dst_ref=o_ref.at[slot], send_sem=send_sems.at[k - 1],
                                                recv_sem=recv_sems.at[k - 1], device_id=peer, device_id_type=MESH)

        def start():
            loc.start()
            for k in range(1, 8):
                copy(k, me).start()

        def finish():
            for k in range(1, 8):
                copy(k, 4 * (x ^ (k >> 2)) + 2 * (y ^ ((k >> 1) & 1)) + (c ^ (k & 1))).wait_recv()
            for k in range(1, 8):
                copy(k, me).wait_send()
            loc.wait()

        return start, lambda: None, finish

    return _Exchange([s], [jax.ShapeDtypeStruct((8,) + s.shape, s.dtype)],
                     [pltpu.SemaphoreType.DMA((7,)), pltpu.SemaphoreType.DMA((7,)), pltpu.SemaphoreType.DMA], make)


def _sum_devices(a):
    def body(a_ref, o_ref):
        acc = a_ref[0]
        for d in range(1, 8):
            acc = acc + a_ref[d]
        o_ref[...] = acc

    vm = pl.BlockSpec(memory_space=pltpu.VMEM)
    return pl.pallas_call(body, name="sum_devices", in_specs=[vm], out_specs=vm,
                          out_shape=jax.ShapeDtypeStruct(a.shape[1:], F32), compiler_params=_params())(a)


def _swap_exchange(gs):
    n = len(gs)

    def make(ins, outs, sems):
        x, y, c = _pos()
        cps = []
        for i in range(n):
            half = gs[i].shape[1] // 2
            rows = pl.ds(pl.multiple_of((1 - c) * half, 8), half)
            cps.append(pltpu.make_async_remote_copy(src_ref=ins[i].at[:, rows, :], dst_ref=outs[i], send_sem=sems[0].at[i],
                                                    recv_sem=sems[1].at[i], device_id=(x, y, 1 - c), device_id_type=MESH))

        def start():
            for cp in cps:
                cp.start()

        def finish():
            for cp in cps:
                cp.wait()

        return start, lambda: None, finish

    return _Exchange(gs, [jax.ShapeDtypeStruct((4, g.shape[1] // 2, g.shape[2]), g.dtype) for g in gs],
                     [pltpu.SemaphoreType.DMA((n,)), pltpu.SemaphoreType.DMA((n,))], make)


def _scatter_exchange(ss):
    n = len(ss)

    def make(ins, outs, sems):
        send_sems, recv_sems, local_sems = sems
        x, y, c = _pos()
        me = 2 * x + y
        chips = _other_chips(x, y)
        locs = [pltpu.make_async_copy(ins[i].at[me], outs[i].at[me], local_sems.at[i]) for i in range(n)]

        def copy(i, k, src_slot, dst_slot):
            px, py = chips[k]
            return pltpu.make_async_remote_copy(src_ref=ins[i].at[src_slot], dst_ref=outs[i].at[dst_slot],
                                                send_sem=send_sems.at[3 * i + k], recv_sem=recv_sems.at[3 * i + k],
                                                device_id=(px, py, c), device_id_type=MESH)

        def start():
            for i in range(n):
                locs[i].start()
                for k, (px, py) in enumerate(chips):
                    copy(i, k, 2 * px + py, me).start()

        def finish():
            for i in range(n):
                for k, (px, py) in enumerate(chips):
                    copy(i, k, me, 2 * px + py).wait_recv()
            for i in range(n):
                for k, (px, py) in enumerate(chips):
                    copy(i, k, 2 * px + py, me).wait_send()
                locs[i].wait()

        return start, lambda: None, finish

    return _Exchange(ss, [jax.ShapeDtypeStruct(s.shape, s.dtype) for s in ss],
                     [pltpu.SemaphoreType.DMA((3 * n,)), pltpu.SemaphoreType.DMA((3 * n,)), pltpu.SemaphoreType.DMA((n,))], make)


def _send_exchange(rs):
    n = len(rs)

    def make(ins, outs, sems):
        x, y, c = _pos()
        cps = [pltpu.make_async_remote_copy(src_ref=ins[i], dst_ref=outs[i], send_sem=sems[0].at[i], recv_sem=sems[1].at[i],
                                            device_id=(x, y, 1 - c), device_id_type=MESH) for i in range(n)]

        def start():
            for cp in cps:
                cp.start()

        def finish():
            for cp in cps:
                cp.wait()

        return start, lambda: None, finish

    return _Exchange(rs, [jax.ShapeDtypeStruct(r.shape, r.dtype) for r in rs],
                     [pltpu.SemaphoreType.DMA((n,)), pltpu.SemaphoreType.DMA((n,))], make)


def _run_exchange(ex, name):
    ei, eo = len(ex.args), len(ex.out_shape)

    def body(*refs):
        start, forward, finish = ex.make(refs[:ei], refs[ei:ei + eo], refs[ei + eo:])
        start()
        forward()
        finish()

    anyspec = pl.BlockSpec(memory_space=pl.ANY)
    return pl.pallas_call(body, name=name, in_specs=[anyspec] * ei, out_specs=[anyspec] * eo, out_shape=ex.out_shape,
                          scratch_shapes=ex.scratch, compiler_params=_params(has_side_effects=True))(*ex.args)


def _add_half(gs, rs, c, name):
    n = len(gs)

    def body(c_ref, *refs):
        for g_ref, r_ref, o_ref in zip(refs[:n], refs[n:2 * n], refs[2 * n:]):
            o_ref[...] = (g_ref[...] + r_ref[...]).astype(BF16)

    g_specs, r_specs, out_shape = [], [], []
    for g, r in zip(gs, rs):
        _, H, C = r.shape
        tr = H // 2
        assert tr % 16 == 0 and g.shape == (4, 2 * H, C)
        g_specs.append(pl.BlockSpec((1, tr, C), lambda j, i, c_ref: (j, c_ref[0] * 2 + i, 0)))
        r_specs.append(pl.BlockSpec((1, tr, C), lambda j, i, c_ref: (j, i, 0)))
        out_shape.append(jax.ShapeDtypeStruct((4, H, C), BF16))
    grid_spec = pltpu.PrefetchScalarGridSpec(num_scalar_prefetch=1, grid=(4, 2), in_specs=g_specs + r_specs, out_specs=r_specs)
    return pl.pallas_call(body, name=name, grid_spec=grid_spec, out_shape=out_shape, compiler_params=_params())(c, *gs, *rs)


def _block_diag(w):
    eye = jnp.eye(RNN_BLOCKS, dtype=w.dtype)
    return (eye[:, None, :, None] * w[:, :, None, :]).reshape(D_RNN, D_RNN)


def _diag_blocks(wd):
    d = wd.reshape(RNN_BLOCKS, 64, RNN_BLOCKS, 64)
    return jnp.stack([d[h, :, h, :] for h in range(RNN_BLOCKS)])


def _split_pack(a, first, last):
    out, base = {}, PACK_OFF[first]
    for i in range(first, last):
        s = a[:, PACK_OFF[i] - base:PACK_OFF[i + 1] - base]
        out[BIG_KEYS[i]] = s.reshape(4 * 256, 256) if BIG_KEYS[i] == "w_p_t" else s.reshape(-1, 1024)
    return out


def _layer_grads(x, p, tgt, gw, small, shard=None, core=None):
    row = lambda v: v.reshape(1, -1)
    wa = _block_diag(small["gate_a_w"]).astype(MXU_DTYPE)
    wx = _block_diag(small["gate_x_w"]).astype(MXU_DTYPE)
    sinks = small["attn_sinks"].reshape(1, HEADS)

    dist = shard is not None
    q, kv, xr, gr, xb = _in_proj(x, gw["w_in_t"])
    cut = PACK_OFF[1] + PACK_ROWS[1] // 2
    att, *ga = _attn_fwd(q, kv, sinks, _gather_exchange(shard[PACK_OFF[1]:cut]) if dist else None)
    xc, h, rec, *gb = _rnn_fwd(xr, gr, small["rnn_conv_w"], row(small["rnn_conv_b"]), wa, row(small["gate_a_b"]),
                               wx, row(small["gate_x_b"]), row(small["lru_lambda"]),
                               _gather_exchange(shard[cut:PACK_OFF[3]]) if dist else None)
    if dist:
        gw = {**gw, **_split_pack(jnp.concatenate([ga[0], gb[0]], axis=1), 1, 3)}
    g1, b1 = row(small["ln1_g"]), row(small["ln1_b"])
    fcw = small["ffn_conv_w"].reshape(3, NC, FF_CHUNK).transpose(1, 0, 2)
    fcb = small["ffn_conv_b"].reshape(NC, 1, FF_CHUNK)
    z1, h1b = _out_proj(att, rec, x, gw["w_out"], g1, b1)
    gate, ge, vd, act, *gc = _ffn_up(h1b, gw["w_up_t"], fcw, fcb,
                                     _gather_exchange(shard[PACK_OFF[3]:PACK_OFF[6]]) if dist else None)
    if dist:
        gw = {**gw, **_split_pack(gc[0], 3, 6)}
    dz2, dz2b, dpre, dpp, vec2 = _ffn_down(act, z1, p, tgt, gw["w_down"], gw["w_g"], gw["w_p_t"], g1, b1,
                                           row(small["ln2_g"]), row(small["ln2_b"]), row(small["ple_gate_b"]))
    dup, dfc = _ffn_bwd(dz2b, gate, ge, vd, gw["w_down"], fcw)
    dz1, vec1 = _ffn_dh1(dup, dz2, dpre, z1, gw["w_up_t"], gw["w_g"], g1, b1)
    per_chip = 2 * D_FF // 4 // FF_CHUNK
    big = {
        "w_ffn_up": _weight_grad_cols(
            h1b, dup.reshape(2 * NC, -1, FF_CHUNK), "dw_up", 2 * NC,
            lambda bt: pl.BlockSpec((None, bt, FF_CHUNK), lambda m, k: (m, k, 0)), (4, D, 2 * D_FF // 4),
            pl.BlockSpec((None, D, FF_CHUNK), lambda m, k: (2 * (m % 2) + (m // 2) // per_chip, 0, (m // 2) % per_chip)))[0],
        "w_ffn_down": _weight_grad(act, dz2b, 512, "dw_down").reshape(4, D_FF // 4, D),
        "ple_gate_w": _weight_grad(h1b, dpre, 512, "dw_gate").reshape(4, D // 4, D),
        "ple_proj": _weight_grad_cols(
            p.astype(BF16), dpp, "dw_proj", 4, lambda bt: pl.BlockSpec((bt, D // 4), lambda j, k: (k, j)),
            (4, PLE, D // 4), pl.BlockSpec((None, PLE, D // 4), lambda j, k: (j, 0, 0)))[0],
        "w_out": _weight_grad(jnp.concatenate([att, rec], axis=1), dz1, 512, "dw_out").reshape(4, D // 4, D),
    }
    reduced = None
    if dist:
        g_ffn = [big[k] for k in EARLY_WEIGHTS]
        ex = _swap_exchange(g_ffn)
    datt, drec, *got = _out_proj_bwd(dz1, gw["w_out"], ex if dist else None)
    if dist:
        ex = _scatter_exchange(_add_half(g_ffn, got, core, "add_half_ffn"))
    dxr, dgr, dwa, dwx, dvec, *got = _rnn_bwd(drec, gr, h, xc, xr, small["rnn_conv_w"], wa, row(small["gate_a_b"]),
                                              wx, row(small["gate_x_b"]), row(small["lru_lambda"]), ex if dist else None)
    if dist:
        mine = _add4(got, "add_chips_ffn")
        ex = _send_exchange(mine)
    dq, dkv, dsinks, *got = _attn_bwd(q, kv, datt, sinks, ex if dist else None)
    if dist:
        reduced = (mine, got)
        big = {}
    sg = {
        "attn_sinks": dsinks[:, 0],
        "rnn_conv_w": dvec[4:8],
        "rnn_conv_b": dvec[3],
        "gate_a_w": _diag_blocks(dwa),
        "gate_a_b": dvec[0],
        "gate_x_w": _diag_blocks(dwx),
        "gate_x_b": dvec[1],
        "lru_lambda": dvec[2],
        "ln1_g": vec1[0],
        "ln1_b": vec1[1],
        "ffn_conv_w": dfc[:, 0:3].transpose(1, 0, 2).reshape(3, D_FF),
        "ffn_conv_b": dfc[:, 3].reshape(D_FF),
        "ple_gate_b": vec2[3],
        "ln2_g": vec2[1],
        "ln2_b": vec2[2],
    }
    loss = vec2[0, 0:1]
    grad_x, du = _in_proj_bwd(dq, dkv, dxr, dgr, dz1, gw["w_in_t"])
    ex = _all_devices_exchange(_pack_vecs([sg[k] for k in SMALL] + [loss])[0]) if dist else None
    big["w_in"], *small_all = _weight_grad_cols(
        xb, du, "dw_in", 4, lambda bt: pl.BlockSpec((None, bt, D_IN // 4), lambda j, k: (j, k, 0)), (4, D, D_IN // 4),
        pl.BlockSpec((None, D, D_IN // 4), lambda j, k: (j, 0, 0)), ex)
    return grad_x, big, sg, loss, reduced, small_all


BIG = ("w_in", "w_ffn_up", "w_out", "w_ffn_down", "ple_gate_w", "ple_proj")
BIG_KEYS = ("w_in_t", "w_up_t", "w_out", "w_down", "w_g", "w_p_t")
BIG_T = (True, True, False, False, False, True)
EARLY_WEIGHTS = ("w_ffn_up", "w_ffn_down", "ple_gate_w", "ple_proj", "w_out")
LATE_WEIGHTS = ("w_in",)
SMALL = ("attn_sinks", "rnn_conv_w", "rnn_conv_b", "gate_a_w", "gate_a_b", "gate_x_w", "gate_x_b", "lru_lambda",
         "ln1_g", "ln1_b", "ffn_conv_w", "ffn_conv_b", "ple_gate_b", "ln2_g", "ln2_b")
SHARDED_SMALL = ("rnn_conv_w", "ffn_conv_w")
WEIGHTS = ("w_in", "attn_sinks", "rnn_conv_w", "rnn_conv_b", "gate_a_w", "gate_a_b", "gate_x_w", "gate_x_b",
           "lru_lambda", "w_out", "ln1_g", "ln1_b", "w_ffn_up", "ffn_conv_w", "ffn_conv_b", "w_ffn_down",
           "ple_gate_w", "ple_gate_b", "ple_proj", "ln2_g", "ln2_b")


def _pack_big(d, first=0, last=6):
    parts = []
    for name, t in zip(BIG[first:last], BIG_T[first:last]):
        a = d[name]
        a = a.T if t else a
        parts.append(a.reshape(-1, 1024))
    return jnp.concatenate(parts, axis=0)


def _pack_vecs(items):
    parts, offs, n = [], [], 0
    for a in items:
        f = a.reshape(-1).astype(F32)
        pad = (-f.shape[0]) % 128
        parts.append(jnp.pad(f, (0, pad)))
        offs.append(n)
        n += (f.shape[0] + pad) // 128
    padr = (-n) % 8
    if padr:
        parts.append(jnp.zeros((padr * 128,), F32))
    return jnp.concatenate(parts).reshape(-1, 128), offs


def _unpack_vecs(a, offs, shapes):
    flat = a.reshape(-1)
    out = []
    for o, s in zip(offs, shapes):
        n = 1
        for d in s:
            n *= d
        out.append(flat[o * 128:o * 128 + n].reshape(s))
    return out


def kernel(x, p, w_in, attn_sinks, rnn_conv_w, rnn_conv_b, gate_a_w, gate_a_b, gate_x_w, gate_x_b, lru_lambda, w_out, ln1_g, ln1_b, w_ffn_up, ffn_conv_w, ffn_conv_b, w_ffn_down, ple_gate_w, ple_gate_b, ple_proj, ln2_g, ln2_b, loss_target, m_w_in, m_attn_sinks, m_rnn_conv_w, m_rnn_conv_b, m_gate_a_w, m_gate_a_b, m_gate_x_w, m_gate_x_b, m_lru_lambda, m_w_out, m_ln1_g, m_ln1_b, m_w_ffn_up, m_ffn_conv_w, m_ffn_conv_b, m_w_ffn_down, m_ple_gate_w, m_ple_gate_b, m_ple_proj, m_ln2_g, m_ln2_b, v_w_in, v_attn_sinks, v_rnn_conv_w, v_rnn_conv_b, v_gate_a_w, v_gate_a_b, v_gate_x_w, v_gate_x_b, v_lru_lambda, v_w_out, v_ln1_g, v_ln1_b, v_w_ffn_up, v_ffn_conv_w, v_ffn_conv_b, v_w_ffn_down, v_ple_gate_w, v_ple_gate_b, v_ple_proj, v_ln2_g, v_ln2_b):
    w = dict(w_in=w_in, attn_sinks=attn_sinks, rnn_conv_w=rnn_conv_w, rnn_conv_b=rnn_conv_b, gate_a_w=gate_a_w,
             gate_a_b=gate_a_b, gate_x_w=gate_x_w, gate_x_b=gate_x_b, lru_lambda=lru_lambda, w_out=w_out, ln1_g=ln1_g,
             ln1_b=ln1_b, w_ffn_up=w_ffn_up, ffn_conv_w=ffn_conv_w, ffn_conv_b=ffn_conv_b, w_ffn_down=w_ffn_down,
             ple_gate_w=ple_gate_w, ple_gate_b=ple_gate_b, ple_proj=ple_proj, ln2_g=ln2_g, ln2_b=ln2_b)
    m = dict(w_in=m_w_in, attn_sinks=m_attn_sinks, rnn_conv_w=m_rnn_conv_w, rnn_conv_b=m_rnn_conv_b, gate_a_w=m_gate_a_w,
             gate_a_b=m_gate_a_b, gate_x_w=m_gate_x_w, gate_x_b=m_gate_x_b, lru_lambda=m_lru_lambda, w_out=m_w_out,
             ln1_g=m_ln1_g, ln1_b=m_ln1_b, w_ffn_up=m_w_ffn_up, ffn_conv_w=m_ffn_conv_w, ffn_conv_b=m_ffn_conv_b,
             w_ffn_down=m_w_ffn_down, ple_gate_w=m_ple_gate_w, ple_gate_b=m_ple_gate_b, ple_proj=m_ple_proj,
             ln2_g=m_ln2_g, ln2_b=m_ln2_b)
    v = dict(w_in=v_w_in, attn_sinks=v_attn_sinks, rnn_conv_w=v_rnn_conv_w, rnn_conv_b=v_rnn_conv_b, gate_a_w=v_gate_a_w,
             gate_a_b=v_gate_a_b, gate_x_w=v_gate_x_w, gate_x_b=v_gate_x_b, lru_lambda=v_lru_lambda, w_out=v_w_out,
             ln1_g=v_ln1_g, ln1_b=v_ln1_b, w_ffn_up=v_w_ffn_up, ffn_conv_w=v_ffn_conv_w, ffn_conv_b=v_ffn_conv_b,
             w_ffn_down=v_w_ffn_down, ple_gate_w=v_ple_gate_w, ple_gate_b=v_ple_gate_b, ple_proj=v_ple_proj,
             ln2_g=v_ln2_g, ln2_b=v_ln2_b)
    w, m, v = ({k: a[0] for k, a in d.items()} for d in (w, m, v))
    chip = 2 * lax.axis_index("x") + lax.axis_index("y")
    core = lax.axis_index("c")

    wpack = _pack_big(w)
    cpack, _ = _pack_vecs([w["rnn_conv_w"], w["ffn_conv_w"]])
    shard = wpack.astype(MXU_DTYPE)
    g_in, gcp = _gather_first(shard[PACK_OFF[0]:PACK_OFF[1]], cpack)
    gw = _split_pack(g_in, 0, 1)
    small = {k: w[k] for k in SMALL}
    small["rnn_conv_w"] = gcp[:, 0:4].reshape(4, 4, 128).transpose(1, 0, 2).reshape(4, 512)
    small["ffn_conv_w"] = gcp[:, 4:22].reshape(4, 3, 768).transpose(1, 0, 2).reshape(3, 3072)

    core1 = core.reshape(1).astype(jnp.int32)
    grad_x, big, sg, loss, ffn_halves, small_all = _layer_grads(x[0], p[0, 0], loss_target[0], gw, small, shard, core1)

    shapes = [sg[k].shape for k in SMALL] + [(1,)]
    _, offs = _pack_vecs([jnp.zeros(s, F32) for s in shapes])
    red = dict(zip(SMALL + ("loss",), _unpack_vecs(_sum_devices(small_all[0]), offs, shapes)))
    red["rnn_conv_w"] = lax.dynamic_slice_in_dim(red["rnn_conv_w"], chip * 128, 128, axis=1)
    red["ffn_conv_w"] = lax.dynamic_slice_in_dim(red["ffn_conv_w"], chip * 768, 768, axis=1)

    g_late = [big[k] for k in LATE_WEIGHTS]
    sib = _run_exchange(_swap_exchange(g_late), "swap_late")
    from_chips = _run_exchange(_scatter_exchange(_add_half(g_late, sib, core1, "add_half_late")), "scatter_late")
    late_mine = _add4(from_chips, "add_chips_late")
    late_other = _run_exchange(_send_exchange(late_mine), "send_late")

    def adamw(names, mine, other, name):
        out, _ = _adamw_halves([w[k] for k in names], mine, other, [m[k] for k in names], [v[k] for k in names],
                               core1, name)
        return dict(zip(names, out))

    big_out = {**adamw(LATE_WEIGHTS, late_mine, late_other, "adamw_late"), **adamw(EARLY_WEIGHTS, *ffn_halves, "adamw_early")}
    wsm, offs2 = _pack_vecs([w[k] for k in SMALL])
    gsm, _ = _pack_vecs([red[k] for k in SMALL])
    msm, _ = _pack_vecs([m[k] for k in SMALL])
    vsm, _ = _pack_vecs([v[k] for k in SMALL])
    dsm, nmsm, nvsm = _adamw(wsm, gsm, msm, vsm, "adamw_small")
    shapes2 = [w[k].shape for k in SMALL]

    def named(n, smallp):
        d = {k: out[n][None] for k, out in big_out.items()}
        d.update({k: a[None] for k, a in zip(SMALL, _unpack_vecs(smallp, offs2, shapes2))})
        return [d[k] for k in WEIGHTS]

    return (red["loss"].reshape(()), grad_x[None], *named(0, gsm), *named(1, dsm), *named(2, nmsm), *named(3, nvsm))
```

```python
import functools

import jax
import jax.numpy as jnp
from jax import lax
from jax.experimental import pallas as pl
from jax.experimental.pallas import tpu as pltpu

F32 = jnp.float32
BF16 = jnp.bfloat16
MXU_DTYPE = jnp.bfloat16

D = 1024
D_ATT = 512
D_KV = 128
D_RNN = 512
D_IN = 1792
D_FF = 3072
FF_CHUNK = 512
PLE = 256
HEADS = 8
HEAD_DIM = 64
BLK = 128
ATTN_BLOCKS = 2
DW_TOKENS = 4096
RNN_BLOCKS = 8
LN_EPS = 1e-5
LRU_C = 8.0
ALPHA = float(2.0 ** 0.25)
SCALE = HEAD_DIM ** -0.5
NEG = -1e30

ADAM_LR = 0.001
ADAM_B1 = 0.9
ADAM_B2 = 0.999
ADAM_EPS = 1e-08
ADAM_WD = 0.01
ADAM_STEP = 10

VMEM_LIMIT_BYTES = 56 * 1024 * 1024
MESH = pl.DeviceIdType.MESH

PACK_ROWS = (448, 1536, 256, 768, 256, 64)
PACK_OFF = tuple(sum(PACK_ROWS[:i]) for i in range(len(PACK_ROWS) + 1))
PACK_TOTAL = PACK_OFF[-1]


def _params(**kw):
    return pltpu.CompilerParams(vmem_limit_bytes=VMEM_LIMIT_BYTES, **kw)


def _mm(a, b):
    return jnp.dot(a.astype(MXU_DTYPE), b.astype(MXU_DTYPE), preferred_element_type=F32)


def _mm_nt(a, b):
    return lax.dot_general(a.astype(MXU_DTYPE), b.astype(MXU_DTYPE), (((1,), (1,)), ((), ())),
                           preferred_element_type=F32)


def _mm_tn(a, b):
    return lax.dot_general(a.astype(MXU_DTYPE), b.astype(MXU_DTYPE), (((0,), (0,)), ((), ())),
                           preferred_element_type=F32)


def _sigmoid(x):
    return 0.5 + 0.5 * jnp.tanh(0.5 * x)


def _gelu(x):
    c = 0.7978845608028654
    k = 0.044715
    x2 = x * x
    t = jnp.tanh(x * (c + (c * k) * x2))
    h = 0.5 * (1.0 + t)
    return x * h, h * (1.0 + (x * (1.0 - t)) * (c + (3.0 * c * k) * x2))


def _shift_rows(x, s, edge8):
    R = x.shape[0]
    row8 = lax.broadcasted_iota(jnp.int32, (8, x.shape[1]), 0)
    if s > 0:
        rolled = pltpu.roll(x, s, 0)
        first = jnp.where(row8 < s, pltpu.roll(edge8, s, 0), rolled[0:8])
        return jnp.concatenate([first, rolled[8:]], axis=0)
    k = -s
    rolled = pltpu.roll(x, R - k, 0)
    last = jnp.where(row8 >= 8 - k, pltpu.roll(edge8, 8 - k, 0), rolled[R - 8:])
    return jnp.concatenate([rolled[:R - 8], last], axis=0)


def _softplus(x):
    return jnp.maximum(x, 0.0) + jnp.log(1.0 + jnp.exp(-jnp.abs(x)))


def _ln(z, g, b):
    mu = jnp.mean(z, axis=-1, keepdims=True)
    zc = z - mu
    var = jnp.mean(zc * zc, axis=-1, keepdims=True)
    rstd = lax.rsqrt(var + LN_EPS)
    xhat = zc * rstd
    return xhat * g + b, xhat, rstd


def _ln_bwd(dy, xhat, rstd, g):
    dxh = dy * g
    m1 = jnp.mean(dxh, axis=-1, keepdims=True)
    m2 = jnp.mean(dxh * xhat, axis=-1, keepdims=True)
    return rstd * (dxh - m1 - xhat * m2)


def _colsum(x):
    return jnp.sum(x, axis=0, keepdims=True)


def _full(shape):
    nd = len(shape)
    return pl.BlockSpec(shape, lambda *_: (0,) * nd)


def _rows(tm, cols, fn=None):
    if fn is None:
        return pl.BlockSpec((tm, cols), lambda i: (i, 0))
    return pl.BlockSpec((tm, cols), lambda i: (fn(i), 0))


def _heads(tm):
    return pl.BlockSpec((HEADS, tm, HEAD_DIM), lambda i: (0, i, 0))


def _in_proj(x, w_in_t):
    T = x.shape[0]
    tm = 512

    def body(x_ref, w_ref, q_ref, kv_ref, xr_ref, gr_ref, xb_ref):
        xb = x_ref[...].astype(MXU_DTYPE)
        xb_ref[...] = xb.astype(BF16)
        q = _mm_nt(xb, w_ref[0:512, :])
        for h in range(HEADS):
            q_ref[h] = q[:, h * 64:(h + 1) * 64].astype(BF16)
        kv_ref[...] = _mm_nt(xb, w_ref[512:768, :]).astype(BF16)
        xr_ref[...] = _mm_nt(xb, w_ref[768:1280, :])
        gr_ref[...] = _mm_nt(xb, w_ref[1280:1792, :])

    return pl.pallas_call(
        body, name="in_proj", grid=(T // tm,),
        in_specs=[_rows(tm, D), _full((D_IN, D))],
        out_specs=[_heads(tm), _rows(tm, 256), _rows(tm, 512), _rows(tm, 512), _rows(tm, D)],
        out_shape=[jax.ShapeDtypeStruct((HEADS, T, 64), BF16), jax.ShapeDtypeStruct((T, 256), BF16),
                   jax.ShapeDtypeStruct((T, 512), F32), jax.ShapeDtypeStruct((T, 512), F32),
                   jax.ShapeDtypeStruct((T, D), BF16)],
        compiler_params=_params(),
    )(x, w_in_t)


def _attn_band(kv_ref, i):
    cur = pl.multiple_of(i * BLK, BLK)
    prev = pl.multiple_of(jnp.maximum(i - 1, 0) * BLK, BLK)
    band = jnp.concatenate([kv_ref[pl.ds(prev, BLK), :], kv_ref[pl.ds(cur, BLK), :]], axis=0)
    key = lax.broadcasted_iota(jnp.int32, (2 * BLK, 4 * BLK), 0)
    qry = lax.broadcasted_iota(jnp.int32, (2 * BLK, 4 * BLK), 1) & (BLK - 1)
    in_prev = jnp.logical_and(jnp.logical_and(key < BLK, key > qry), i > 0)
    mask = jnp.logical_or(in_prev, jnp.logical_and(key >= BLK, key - BLK <= qry))
    return band, mask, cur, prev


def _attn_scores(band, mask, qs, s_ref, g):
    st = jnp.where(mask, _mm_nt(band[:, g * 64:(g + 1) * 64], qs) * SCALE, NEG)
    lane = lax.broadcasted_iota(jnp.int32, (1, 4 * BLK), 1)
    sv = jnp.where(lane < BLK, s_ref[0, 4 * g],
                   jnp.where(lane < 2 * BLK, s_ref[0, 4 * g + 1], jnp.where(lane < 3 * BLK, s_ref[0, 4 * g + 2], s_ref[0, 4 * g + 3])))
    m = jnp.maximum(jnp.max(st, axis=0, keepdims=True), sv)
    p = jnp.exp(st - m)
    ps = jnp.exp(sv - m)
    return p, ps, jnp.sum(p, axis=0, keepdims=True) + ps


def _pos():
    return lax.axis_index("x"), lax.axis_index("y"), lax.axis_index("c")


def _other_chips(x, y):
    return [(1 - x, y), (x, 1 - y), (1 - x, 1 - y)]


def _gather_steps(w_ref, gw_ref, send_sems, recv_sems, local_sem):
    x, y, c = _pos()
    me = 2 * x + y
    chips = _other_chips(x, y)
    half = w_ref.shape[0] // 2
    mine = pl.ds(pl.multiple_of(c * half, 16), half)
    theirs = pl.ds(pl.multiple_of((1 - c) * half, 16), half)
    loc = pltpu.make_async_copy(w_ref, gw_ref.at[me], local_sem)

    def copy(k, src, dst, to):
        return pltpu.make_async_remote_copy(src_ref=src, dst_ref=dst, send_sem=send_sems.at[k], recv_sem=recv_sems.at[k],
                                            device_id=to, device_id_type=MESH)

    def out(k):
        px, py = chips[k]
        return copy(k, w_ref.at[mine], gw_ref.at[me, mine], (px, py, c))

    def fwd(k, rows):
        px, py = chips[k]
        return copy(3 + k, gw_ref.at[2 * px + py, rows], gw_ref.at[2 * px + py, rows], (x, y, 1 - c))

    def start():
        loc.start()
        for k in range(3):
            out(k).start()

    def forward():
        for k in range(3):
            px, py = chips[k]
            copy(k, w_ref.at[mine], gw_ref.at[2 * px + py, mine], (px, py, c)).wait_recv()
            fwd(k, mine).start()

    def finish():
        for k in range(3):
            fwd(k, theirs).wait_recv()
        for k in range(3):
            out(k).wait_send()
            fwd(k, mine).wait_send()
        loc.wait()

    return start, forward, finish


GATHER_SCRATCH = [pltpu.SemaphoreType.DMA((6,)), pltpu.SemaphoreType.DMA((6,)), pltpu.SemaphoreType.DMA]


class _Exchange:
    def __init__(self, args, out_shape, scratch, make):
        self.args, self.out_shape, self.scratch, self.make = list(args), list(out_shape), list(scratch), make


def _join_exchanges(a, b):
    na, nao, nas = len(a.args), len(a.out_shape), len(a.scratch)

    def make(ins, outs, sems):
        steps_a = a.make(ins[:na], outs[:nao], sems[:nas])
        steps_b = b.make(ins[na:], outs[nao:], sems[nas:])

        def both(f, g):
            def run():
                f()
                g()
            return run

        return tuple(both(f, g) for f, g in zip(steps_a, steps_b))

    return _Exchange(a.args + b.args, a.out_shape + b.out_shape, a.scratch + b.scratch, make)


def _gather_exchange(wsrc):
    return _Exchange([wsrc], [jax.ShapeDtypeStruct((4,) + wsrc.shape, wsrc.dtype)], GATHER_SCRATCH,
                     lambda ins, outs, sems: _gather_steps(ins[0], outs[0], *sems))


def _launch(body, name, grid, in_specs, out_specs, out_shape, scratch, args, exchange=None, prefetch=0):
    def call(fn, fn_name, ins, outs, shapes, scr, operands, effects):
        spec = pltpu.PrefetchScalarGridSpec(num_scalar_prefetch=prefetch, grid=grid, in_specs=ins, out_specs=outs,
                                            scratch_shapes=scr)
        return pl.pallas_call(fn, name=fn_name, grid_spec=spec, out_shape=shapes,
                              compiler_params=_params(has_side_effects=effects))(*operands)

    if exchange is None:
        return call(body, name, list(in_specs), list(out_specs), list(out_shape), list(scratch), args, False)
    n_in, n_out, ei, eo, ns = len(in_specs), len(out_specs), len(exchange.args), len(exchange.out_shape), len(exchange.scratch)
    nsteps = 1
    for g in grid:
        nsteps *= g

    def wrapped(*refs):
        scalars, refs = refs[:prefetch], refs[prefetch:]
        ins, xin = refs[:n_in], refs[n_in:n_in + ei]
        outs, xout = refs[n_in + ei:n_in + ei + n_out], refs[n_in + ei + n_out:n_in + ei + n_out + eo]
        rest = refs[n_in + ei + n_out + eo:]
        own, sems = rest[:len(rest) - ns], rest[len(rest) - ns:]
        start, forward, finish = exchange.make(xin, xout, sems)
        i = pl.program_id(0)
        for d in range(1, len(grid)):
            i = i * grid[d] + pl.program_id(d)
        pl.when(i == 0)(start)
        body(*scalars, *ins, *outs, *own)
        pl.when(i == max(nsteps - 3, 0))(forward)
        pl.when(i == nsteps - 1)(finish)

    anyspec = pl.BlockSpec(memory_space=pl.ANY)
    return call(wrapped, name + "_x", list(in_specs) + [anyspec] * ei, list(out_specs) + [anyspec] * eo,
                list(out_shape) + exchange.out_shape, list(scratch) + exchange.scratch, (*args, *exchange.args), True)


def _attn_fwd(q, kv, sinks, exchange=None):
    T = kv.shape[0]

    def body(q_ref, kv_ref, s_ref, o_ref):
        for b in range(ATTN_BLOCKS):
            rows = slice(b * BLK, (b + 1) * BLK)
            band, mask, _, _ = _attn_band(kv_ref, ATTN_BLOCKS * pl.program_id(0) + b)
            for g in range(2):
                qs = q_ref[4 * g:4 * g + 4, rows, :].reshape(4 * BLK, HEAD_DIM)
                p, _, den = _attn_scores(band, mask, qs, s_ref, g)
                ot = _mm_tn(band[:, 128:256], p) * (1.0 / den)
                for hh in range(4):
                    o = ot[:, hh * BLK:(hh + 1) * BLK].T
                    o_ref[rows, (4 * g + hh) * 64:(4 * g + hh + 1) * 64] = o[:, g * 64:(g + 1) * 64].astype(BF16)

    tq = ATTN_BLOCKS * BLK
    return _launch(body, "attn_fwd", (T // tq,), [_heads(tq), _full((T, 256)), pl.BlockSpec(memory_space=pltpu.SMEM)],
                   [_rows(tq, 512)], [jax.ShapeDtypeStruct((T, 512), BF16)], [], (q, kv, sinks), exchange)


def _attn_bwd(q, kv, do, sinks, exchange=None):
    T = kv.shape[0]

    def body(q_ref, kv_ref, do_ref, s_ref, dq_ref, dkv_ref, ds_ref):
        @pl.when(pl.program_id(0) == 0)
        def _():
            ds_ref[...] = jnp.zeros_like(ds_ref)

        for b in range(ATTN_BLOCKS):
            rows = slice(b * BLK, (b + 1) * BLK)
            band, mask, cur, prev = _attn_band(kv_ref, ATTN_BLOCKS * pl.program_id(0) + b)
            for g in range(2):
                qs = q_ref[4 * g:4 * g + 4, rows, :].reshape(4 * BLK, HEAD_DIM)
                dos = do_ref[4 * g:4 * g + 4, rows, :].reshape(4 * BLK, HEAD_DIM)
                p, ps, den = _attn_scores(band, mask, qs, s_ref, g)
                inv = 1.0 / den
                p = p * inv
                dpt = _mm_nt(band[:, 128 + g * 64:192 + g * 64], dos)
                delta = jnp.sum(p * dpt, axis=0, keepdims=True)
                dst = p * (dpt - delta)
                dsv = -(ps * inv) * delta
                for hh in range(4):
                    dsink = jnp.sum(dsv[:, hh * BLK:(hh + 1) * BLK], axis=1, keepdims=True)
                    ds_ref[4 * g + hh:4 * g + hh + 1, :] += jnp.broadcast_to(dsink, (1, 128))
                dqt = _mm_tn(band[:, 0:128], dst) * SCALE
                for hh in range(4):
                    dqh = dqt[:, hh * BLK:(hh + 1) * BLK].T
                    dq_ref[rows, (4 * g + hh) * 64:(4 * g + hh + 1) * 64] = dqh[:, g * 64:(g + 1) * 64].astype(BF16)
                dk = _mm(dst, qs) * SCALE
                dv = _mm(p, dos)
                dkv_ref[pl.ds(cur, BLK), g * 64:(g + 1) * 64] = dk[BLK:2 * BLK]
                dkv_ref[pl.ds(cur, BLK), 128 + g * 64:192 + g * 64] = dv[BLK:2 * BLK]
                dkv_ref[pl.ds(prev, BLK), g * 64:(g + 1) * 64] += dk[0:BLK]
                dkv_ref[pl.ds(prev, BLK), 128 + g * 64:192 + g * 64] += dv[0:BLK]

    tq = ATTN_BLOCKS * BLK
    return _launch(body, "attn_bwd", (T // tq,),
                   [_heads(tq), _full((T, 256)), _heads(tq), pl.BlockSpec(memory_space=pltpu.SMEM)],
                   [_rows(tq, 512), _full((T, 256)), _full((8, 128))],
                   [jax.ShapeDtypeStruct((T, 512), BF16), jax.ShapeDtypeStruct((T, 256), F32),
                    jax.ShapeDtypeStruct((8, 128), F32)], [], (q, kv, do, sinks), exchange)


def _rows8(tm, cols):
    return lax.broadcasted_iota(jnp.int32, (tm, cols), 0) & 7


def _lru_gates(xc, wa, ba, wx, bx, lam):
    r = _sigmoid(_mm(xc, wa) + ba)
    ii = _sigmoid(_mm(xc, wx) + bx)
    sp = _softplus(-lam)
    la = -LRU_C * r * sp
    a = jnp.exp(la)
    m = jnp.sqrt(-jnp.tanh(la) * (a * a + 1.0))
    return r, ii, sp, a, m


def _rnn_fwd(xr, gr, cw, cb, wa, ba, wx, bx, lam, exchange=None):
    T = xr.shape[0]
    tm = 512
    C = D_RNN

    def body(xr_ref, gr_ref, cw_ref, cb_ref, wa_ref, ba_ref, wx_ref, bx_ref, lam_ref,
             xc_ref, h_ref, rec_ref, ext, a_s, b_s, carry):
        i = pl.program_id(0)

        @pl.when(i == 0)
        def _():
            ext[...] = jnp.zeros((8, C), F32)
            carry[...] = jnp.zeros((8, C), F32)

        xr = xr_ref[...]
        edge = ext[...]
        xc = cb_ref[...] + cw_ref[3:4, :] * xr
        for k in range(3):
            xc = xc + cw_ref[k:k + 1, :] * _shift_rows(xr, 3 - k, edge)
        ext[...] = xr[tm - 8:tm, :]
        xc_ref[...] = xc
        _, ii, _, a, m = _lru_gates(xc, wa_ref[...], ba_ref[...], wx_ref[...], bx_ref[...], lam_ref[...])
        b = m * ii * xc
        r8 = _rows8(tm, C)
        for d in (1, 2, 4):
            ok = r8 >= d
            a_sh = jnp.where(ok, pltpu.roll(a, d, 0), 1.0)
            b_sh = jnp.where(ok, pltpu.roll(b, d, 0), 0.0)
            b = a * b_sh + b
            a = a * a_sh
        a_s[...] = a
        b_s[...] = b

        def step(g, hin):
            s = pl.multiple_of(g * 8, 8)
            hg = a_s[pl.ds(s, 8), :] * hin + b_s[pl.ds(s, 8), :]
            h_ref[pl.ds(s, 8), :] = hg
            return jnp.broadcast_to(hg[7:8, :], (8, C))

        carry[...] = lax.fori_loop(0, tm // 8, step, carry[...])
        ge, _ = _gelu(gr_ref[...])
        rec_ref[...] = (h_ref[...] * ge).astype(BF16)

    vec = _full((1, C))
    in_specs = [_rows(tm, C), _rows(tm, C), _full((4, C)), vec, _full((C, C)), vec, _full((C, C)), vec, vec]
    out_specs = [_rows(tm, C), _rows(tm, C), _rows(tm, C)]
    out_shape = [jax.ShapeDtypeStruct((T, C), F32), jax.ShapeDtypeStruct((T, C), F32), jax.ShapeDtypeStruct((T, C), BF16)]
    scratch = [pltpu.VMEM((8, C), F32), pltpu.VMEM((tm, C), F32), pltpu.VMEM((tm, C), F32), pltpu.VMEM((8, C), F32)]
    return _launch(body, "rnn_fwd", (T // tm,), in_specs, out_specs, out_shape, scratch,
                   (xr, gr, cw, cb, wa, ba, wx, bx, lam), exchange)


def _rnn_bwd(drec, gr, h, xc, xr, cw, wa, ba, wx, bx, lam, exchange=None):
    T = xr.shape[0]
    tm = 512
    C = D_RNN
    nt = T // tm
    t8 = tm // 8

    def body(drec_ref, gr_ref, h_ref, hp_ref, xc_ref, xr_ref, cw_ref, wa_ref, ba_ref, wx_ref, bx_ref,
             lam_ref, dxr_ref, dgr_ref, dwa_ref, dwx_ref, dvec_ref, c_s, g_s, gout, ext, anext, gcarry):
        i = pl.program_id(0)
        j = nt - 1 - i

        @pl.when(i == 0)
        def _():
            dwa_ref[...] = jnp.zeros_like(dwa_ref)
            dwx_ref[...] = jnp.zeros_like(dwx_ref)
            dvec_ref[...] = jnp.zeros_like(dvec_ref)
            anext[...] = jnp.zeros((8, C), F32)
            gcarry[...] = jnp.zeros((8, C), F32)
            ext[...] = jnp.zeros((8, C), F32)

        xc = xc_ref[...]
        lam = lam_ref[...]
        r, ii, sp, a, m = _lru_gates(xc, wa_ref[...], ba_ref[...], wx_ref[...], bx_ref[...], lam)
        ge, dge = _gelu(gr_ref[...])
        drec = drec_ref[...]
        hh = h_ref[...]
        dgr_ref[...] = (drec * hh * dge).astype(BF16)
        dh = drec * ge
        rowi = lax.broadcasted_iota(jnp.int32, (tm, C), 0)
        c = jnp.where(rowi == tm - 1, jnp.broadcast_to(anext[0:1, :], (tm, C)), pltpu.roll(a, tm - 1, 0))
        anext[...] = a[0:8, :]
        r8 = rowi & 7
        gg = dh
        for d in (1, 2, 4):
            ok = r8 < 8 - d
            c_sh = jnp.where(ok, pltpu.roll(c, tm - d, 0), 1.0)
            g_sh = jnp.where(ok, pltpu.roll(gg, tm - d, 0), 0.0)
            gg = c * g_sh + gg
            c = c * c_sh
        c_s[...] = c
        g_s[...] = gg

        def step(k, gin):
            s = pl.multiple_of((t8 - 1 - k) * 8, 8)
            og = c_s[pl.ds(s, 8), :] * gin + g_s[pl.ds(s, 8), :]
            gout[pl.ds(s, 8), :] = og
            return jnp.broadcast_to(og[0:1, :], (8, C))

        gcarry[...] = lax.fori_loop(0, t8, step, gcarry[...])
        G = gout[...]
        hprev_row = jnp.where(j > 0, hp_ref[7:8, :], 0.0)
        hprev = jnp.where(rowi == 0, jnp.broadcast_to(hprev_row, (tm, C)), pltpu.roll(hh, 1, 0))
        da = G * hprev
        dm = G * ii * xc
        di = G * m * xc
        dxc = G * m * ii
        dla = da * a - dm * a * a / m
        dr = dla * (-LRU_C * sp)
        dsp = _colsum(dla * (-LRU_C * r))
        dlam = dsp * (-_sigmoid(-lam))
        dpr = dr * r * (1.0 - r)
        dpi = di * ii * (1.0 - ii)
        dxc = dxc + _mm_nt(dpr, wa_ref[...]) + _mm_nt(dpi, wx_ref[...])
        dwa_ref[...] += _mm_tn(xc, dpr)
        dwx_ref[...] += _mm_tn(xc, dpi)
        dvec_ref[0:1, :] += _colsum(dpr)
        dvec_ref[1:2, :] += _colsum(dpi)
        dvec_ref[2:3, :] += dlam
        dvec_ref[3:4, :] += _colsum(dxc)
        edge = ext[...]
        xr = xr_ref[...]
        dxr = cw_ref[3:4, :] * dxc
        dvec_ref[7:8, :] += _colsum(dxc * xr)
        for k in range(3):
            up = _shift_rows(dxc, k - 3, edge)
            dxr = dxr + cw_ref[k:k + 1, :] * up
            dvec_ref[4 + k:5 + k, :] += _colsum(up * xr)
        ext[...] = dxc[0:8, :]
        dxr_ref[...] = dxr.astype(BF16)

    rev = lambda i: nt - 1 - i
    prev8 = lambda i: jnp.maximum((nt - 1 - i) * t8 - 1, 0)
    vec = _full((1, C))
    return _launch(
        body, "rnn_bwd", (nt,),
        [_rows(tm, C, rev), _rows(tm, C, rev), _rows(tm, C, rev), _rows(8, C, prev8), _rows(tm, C, rev),
         _rows(tm, C, rev), _full((4, C)), _full((C, C)), vec, _full((C, C)), vec, vec],
        [_rows(tm, C, rev), _rows(tm, C, rev), _full((C, C)), _full((C, C)), _full((8, C))],
        [jax.ShapeDtypeStruct((T, C), BF16), jax.ShapeDtypeStruct((T, C), BF16),
         jax.ShapeDtypeStruct((C, C), F32), jax.ShapeDtypeStruct((C, C), F32), jax.ShapeDtypeStruct((8, C), F32)],
        [pltpu.VMEM((tm, C), F32), pltpu.VMEM((tm, C), F32), pltpu.VMEM((tm, C), F32),
         pltpu.VMEM((8, C), F32), pltpu.VMEM((8, C), F32), pltpu.VMEM((8, C), F32)],
        (drec, gr, h, h, xc, xr, cw, wa, ba, wx, bx, lam), exchange)


def _out_proj(att, rec, x, w_out, g1, b1):
    T = x.shape[0]
    tm = 512

    def body(att_ref, rec_ref, x_ref, w_ref, g1_ref, b1_ref, z_ref, h_ref):
        mix = _mm(att_ref[...], w_ref[0:512, :]) + _mm(rec_ref[...], w_ref[512:1024, :])
        z1 = ALPHA * x_ref[...] + mix
        z_ref[...] = z1
        h1, _, _ = _ln(z1, g1_ref[...], b1_ref[...])
        h_ref[...] = h1.astype(MXU_DTYPE).astype(BF16)

    return pl.pallas_call(
        body, name="out_proj", grid=(T // tm,),
        in_specs=[_rows(tm, 512), _rows(tm, 512), _rows(tm, D), _full((D, D)), _full((1, D)), _full((1, D))],
        out_specs=[_rows(tm, D), _rows(tm, D)],
        out_shape=[jax.ShapeDtypeStruct((T, D), F32), jax.ShapeDtypeStruct((T, D), BF16)],
        compiler_params=_params(),
    )(att, rec, x, w_out, g1, b1)


NC = D_FF // FF_CHUNK


def _ffn_up(h1b, w_up_t, fcw, fcb, exchange=None):
    T = h1b.shape[0]
    tm = min(1024, T)
    CW = FF_CHUNK

    def body(h_ref, wg_ref, wv_ref, fcw_ref, fcb_ref, gate_ref, ge_ref, vd_ref, act_ref, before):
        i = pl.program_id(1)

        @pl.when(i == 0)
        def _():
            before[...] = jnp.zeros((8, CW), F32)

        hb = h_ref[...]
        gate = _mm_nt(hb, wg_ref[...])
        val = _mm_nt(hb, wv_ref[...])
        gate_ref[...] = gate.astype(BF16)
        edge = before[...]
        gc = (fcb_ref[...] + fcw_ref[0:1, :] * _shift_rows(gate, 2, edge) + fcw_ref[1:2, :] * _shift_rows(gate, 1, edge)
              + fcw_ref[2:3, :] * gate)
        before[...] = gate[tm - 8:tm, :]
        ge, dge = _gelu(gc)
        ge_ref[...] = ge.astype(BF16)
        vd_ref[...] = (val * dge).astype(BF16)
        act_ref[...] = (ge * val).astype(BF16)

    chunk = pl.BlockSpec((None, tm, CW), lambda c, i: (c, i, 0))
    return _launch(
        body, "ffn_up", (NC, T // tm),
        [pl.BlockSpec((tm, D), lambda c, i: (i, 0)), pl.BlockSpec((CW, D), lambda c, i: (c, 0)),
         pl.BlockSpec((CW, D), lambda c, i: (NC + c, 0)), pl.BlockSpec((None, 3, CW), lambda c, i: (c, 0, 0)),
         pl.BlockSpec((None, 1, CW), lambda c, i: (c, 0, 0))],
        [chunk] * 4, [jax.ShapeDtypeStruct((NC, T, CW), BF16)] * 4, [pltpu.VMEM((8, CW), F32)],
        (h1b, w_up_t, w_up_t, fcw, fcb), exchange)


def _ffn_down(act, z1, p, tgt, w_down, w_g, w_p_t, g1, b1, g2, b2, bg):
    T = z1.shape[0]
    tm = 256

    def body(act_ref, z_ref, p_ref, t_ref, wdn_hbm, wg_hbm, wp_hbm, g1_ref, b1_ref, g2_ref, b2_ref, bg_ref,
             dz2_ref, dz2b_ref, dpre_ref, dpp_ref, vec_ref, wdn, wg, wp):
        @pl.when(pl.program_id(0) == 0)
        def _():
            pltpu.sync_copy(wdn_hbm, wdn)
            pltpu.sync_copy(wg_hbm, wg)
            pltpu.sync_copy(wp_hbm, wp)
            vec_ref[...] = jnp.zeros_like(vec_ref)

        g2v = g2_ref[...]
        h1, _, _ = _ln(z_ref[...], g1_ref[...], b1_ref[...])
        h1b = h1.astype(MXU_DTYPE)
        ffn = _mm(act_ref[0], wdn[0:FF_CHUNK, :])
        for c in range(1, NC):
            ffn = ffn + _mm(act_ref[c], wdn[c * FF_CHUNK:(c + 1) * FF_CHUNK, :])
        sg = _sigmoid(_mm(h1b, wg[...]) + bg_ref[...])
        pp = _mm_nt(p_ref[...], wp[...])
        z2 = ALPHA * h1 + ffn + sg * pp
        y, xh2, rstd2 = _ln(z2, g2v, b2_ref[...])
        diff = y - t_ref[...]
        dy = diff * (1.0 / D)
        dz2 = _ln_bwd(dy, xh2, rstd2, g2v)
        dpre = dz2 * pp * sg * (1.0 - sg)
        dz2_ref[...] = dz2
        dz2b_ref[...] = dz2.astype(BF16)
        dpre_ref[...] = dpre.astype(BF16)
        dpp_ref[...] = (dz2 * sg).astype(BF16)
        loss = 0.5 * jnp.sum(jnp.sum(diff * diff, axis=1, keepdims=True), axis=0, keepdims=True) * (1.0 / D)
        vec_ref[0:1, :] += jnp.broadcast_to(loss, (1, D))
        vec_ref[1:2, :] += _colsum(dy * xh2)
        vec_ref[2:3, :] += _colsum(dy)
        vec_ref[3:4, :] += _colsum(dpre)

    anyspec = pl.BlockSpec(memory_space=pl.ANY)
    vec = _full((1, D))
    return pl.pallas_call(
        body, name="ffn_down", grid=(T // tm,),
        in_specs=[pl.BlockSpec((NC, tm, FF_CHUNK), lambda i: (0, i, 0)), _rows(tm, D), _rows(tm, PLE), _rows(tm, D),
                  anyspec, anyspec, anyspec] + [vec] * 5,
        out_specs=[_rows(tm, D)] * 4 + [_full((8, D))],
        out_shape=[jax.ShapeDtypeStruct((T, D), F32)] + [jax.ShapeDtypeStruct((T, D), BF16)] * 3
                  + [jax.ShapeDtypeStruct((8, D), F32)],
        scratch_shapes=[pltpu.VMEM((D_FF, D), MXU_DTYPE), pltpu.VMEM((D, D), MXU_DTYPE), pltpu.VMEM((D, PLE), MXU_DTYPE)],
        compiler_params=_params(),
    )(act, z1, p, tgt, w_down, w_g, w_p_t, g1, b1, g2, b2, bg)


def _ffn_bwd(dz2b, gate, ge, vd, w_down, fcw):
    T = dz2b.shape[0]
    tm = min(1024, T)
    CW = FF_CHUNK
    nt = T // tm

    def body(dz_ref, wdn_ref, gate_ref, ge_ref, vd_ref, fcw_ref, dup_ref, dfc_ref, after):
        i = pl.program_id(1)

        @pl.when(i == 0)
        def _():
            after[...] = jnp.zeros((8, CW), F32)
            dfc_ref[...] = jnp.zeros_like(dfc_ref)

        gate = gate_ref[...].astype(F32)
        dact = _mm_nt(dz_ref[...], wdn_ref[...])
        dgc = dact * vd_ref[...].astype(F32)
        edge = after[...]
        dgc1 = _shift_rows(dgc, -1, edge)
        dgc2 = _shift_rows(dgc, -2, edge)
        after[...] = dgc[0:8, :]
        dup_ref[0] = (fcw_ref[2:3, :] * dgc + fcw_ref[1:2, :] * dgc1 + fcw_ref[0:1, :] * dgc2).astype(BF16)
        dup_ref[1] = (dact * ge_ref[...].astype(F32)).astype(BF16)
        dfc_ref[0:1, :] += _colsum(dgc2 * gate)
        dfc_ref[1:2, :] += _colsum(dgc1 * gate)
        dfc_ref[2:3, :] += _colsum(dgc * gate)
        dfc_ref[3:4, :] += _colsum(dgc)

    rev = lambda c, i: (c, nt - 1 - i, 0)
    chunk = pl.BlockSpec((None, tm, CW), rev)
    return pl.pallas_call(
        body, name="ffn_bwd", grid=(NC, nt),
        in_specs=[pl.BlockSpec((tm, D), lambda c, i: (nt - 1 - i, 0)), pl.BlockSpec((CW, D), lambda c, i: (c, 0)),
                  chunk, chunk, chunk, pl.BlockSpec((None, 3, CW), lambda c, i: (c, 0, 0))],
        out_specs=[pl.BlockSpec((None, 2, tm, CW), lambda c, i: (c, 0, nt - 1 - i, 0)),
                   pl.BlockSpec((None, 8, CW), lambda c, i: (c, 0, 0))],
        out_shape=[jax.ShapeDtypeStruct((NC, 2, T, CW), BF16), jax.ShapeDtypeStruct((NC, 8, CW), F32)],
        scratch_shapes=[pltpu.VMEM((8, CW), F32)],
        compiler_params=_params(),
    )(dz2b, w_down, gate, ge, vd, fcw)


def _ffn_dh1(dup, dz2, dpre, z1, w_up_t, w_g, g1, b1):
    T = z1.shape[0]
    tm = 256

    def body(dup_ref, dz2_ref, dpre_ref, z_ref, wup_hbm, wg_hbm, g1_ref, b1_ref, dz1_ref, vec_ref, wup, wg):
        @pl.when(pl.program_id(0) == 0)
        def _():
            pltpu.sync_copy(wup_hbm, wup)
            pltpu.sync_copy(wg_hbm, wg)
            vec_ref[...] = jnp.zeros_like(vec_ref)

        g1v = g1_ref[...]
        _, xh1, rstd1 = _ln(z_ref[...], g1v, b1_ref[...])
        dh1 = ALPHA * dz2_ref[...] + _mm_nt(dpre_ref[...], wg[...])
        for c in range(NC):
            for s in range(2):
                r0 = s * D_FF + c * FF_CHUNK
                dh1 = dh1 + _mm(dup_ref[c, s], wup[r0:r0 + FF_CHUNK, :])
        dz1_ref[...] = _ln_bwd(dh1, xh1, rstd1, g1v)
        vec_ref[0:1, :] += _colsum(dh1 * xh1)
        vec_ref[1:2, :] += _colsum(dh1)

    anyspec = pl.BlockSpec(memory_space=pl.ANY)
    vec = _full((1, D))
    return pl.pallas_call(
        body, name="ffn_dh1", grid=(T // tm,),
        in_specs=[pl.BlockSpec((NC, 2, tm, FF_CHUNK), lambda i: (0, 0, i, 0)), _rows(tm, D), _rows(tm, D), _rows(tm, D),
                  anyspec, anyspec, vec, vec],
        out_specs=[_rows(tm, D), _full((8, D))],
        out_shape=[jax.ShapeDtypeStruct((T, D), F32), jax.ShapeDtypeStruct((8, D), F32)],
        scratch_shapes=[pltpu.VMEM((2 * D_FF, D), MXU_DTYPE), pltpu.VMEM((D, D), MXU_DTYPE)],
        compiler_params=_params(),
    )(dup, dz2, dpre, z1, w_up_t, w_g, g1, b1)


def _out_proj_bwd(dz1, w_out, exchange=None):
    T = dz1.shape[0]
    tm = 512

    def body(dz_ref, w_ref, datt_ref, drec_ref):
        dzb = dz_ref[...].astype(MXU_DTYPE)
        datt = _mm_nt(dzb, w_ref[0:512, :])
        for h in range(HEADS):
            datt_ref[h] = datt[:, h * 64:(h + 1) * 64].astype(BF16)
        drec_ref[...] = _mm_nt(dzb, w_ref[512:1024, :])

    return _launch(body, "out_proj_bwd", (T // tm,), [_rows(tm, D), _full((D, D))], [_heads(tm), _rows(tm, 512)],
                   [jax.ShapeDtypeStruct((HEADS, T, 64), BF16), jax.ShapeDtypeStruct((T, 512), F32)], [],
                   (dz1, w_out), exchange)


def _in_proj_bwd(dq, dkv, dxr, dgr, dz1, w_in_t, exchange=None):
    T = dz1.shape[0]
    tm = 512
    W = D_IN // 4

    def body(dq_ref, dkv_ref, dxr_ref, dgr_ref, dz_ref, w_ref, dx_ref, du_ref):
        dkv = dkv_ref[...]
        dx_ref[...] = (ALPHA * dz_ref[...] + _mm(dq_ref[...], w_ref[0:512, :]) + _mm(dkv, w_ref[512:768, :])
                       + _mm(dxr_ref[...], w_ref[768:1280, :]) + _mm(dgr_ref[...], w_ref[1280:1792, :]))
        dq, dxr, dgr = dq_ref[...].astype(F32), dxr_ref[...].astype(F32), dgr_ref[...].astype(F32)
        du_ref[0] = dq[:, 0:W].astype(BF16)
        du_ref[1, :, 0:64] = dq[:, W:512].astype(BF16)
        du_ref[1, :, 64:320] = dkv.astype(BF16)
        du_ref[1, :, 320:W] = dxr[:, 0:128].astype(BF16)
        du_ref[2, :, 0:384] = dxr[:, 128:512].astype(BF16)
        du_ref[2, :, 384:W] = dgr[:, 0:64].astype(BF16)
        du_ref[3] = dgr[:, 64:512].astype(BF16)

    return _launch(body, "in_proj_bwd", (T // tm,),
                   [_rows(tm, 512), _rows(tm, 256), _rows(tm, 512), _rows(tm, 512), _rows(tm, D), _full((D_IN, D))],
                   [_rows(tm, D), pl.BlockSpec((4, tm, W), lambda i: (0, i, 0))],
                   [jax.ShapeDtypeStruct((T, D), F32), jax.ShapeDtypeStruct((4, T, W), BF16)], [],
                   (dq, dkv, dxr, dgr, dz1, w_in_t), exchange)


def _accumulate_tn(a_ref, b_ref, o_ref):
    @pl.when(pl.program_id(1) == 0)
    def _():
        o_ref[...] = jnp.zeros_like(o_ref)

    o_ref[...] += _mm_tn(a_ref[...], b_ref[...])


def _weight_grad_cols(a, b, name, n_blocks, b_spec, out_shape, out_spec, exchange=None):
    T, M = a.shape
    bt = min(DW_TOKENS, T)
    return _launch(functools.partial(_accumulate_tn), name, (n_blocks, T // bt),
                   [pl.BlockSpec((bt, M), lambda m, k: (k, 0)), b_spec(bt)], [out_spec],
                   [jax.ShapeDtypeStruct(out_shape, F32)], [], (a, b), exchange)


def _weight_grad(a, b, bm, name):
    bt = min(DW_TOKENS // 2 if b.dtype == F32 else DW_TOKENS, b.shape[0])
    if a.ndim == 3:
        assert a.shape[2] == bm
        T, M = a.shape[1], a.shape[0] * bm
        a_spec = pl.BlockSpec((None, bt, bm), lambda m, k: (m, k, 0))
    else:
        T, M = a.shape
        a_spec = pl.BlockSpec((bt, bm), lambda m, k: (k, m))
    N = b.shape[1]
    nk = T // bt

    return pl.pallas_call(
        functools.partial(_accumulate_tn), name=name, grid=(M // bm, nk),
        in_specs=[a_spec, pl.BlockSpec((bt, N), lambda m, k: (k, 0))],
        out_specs=pl.BlockSpec((bm, N), lambda m, k: (m, 0)),
        out_shape=jax.ShapeDtypeStruct((M, N), F32),
        compiler_params=_params(),
    )(a, b)


def _adamw(w, g, m, v, name):
    R, C = w.shape
    tr = R // 8 if R % 64 == 0 else R
    c1 = 1.0 / (1.0 - ADAM_B1 ** ADAM_STEP)
    c2 = 1.0 / (1.0 - ADAM_B2 ** ADAM_STEP)

    def body(w_ref, g_ref, m_ref, v_ref, d_ref, nm_ref, nv_ref):
        g = g_ref[...]
        nm = ADAM_B1 * m_ref[...] + (1.0 - ADAM_B1) * g
        nv = ADAM_B2 * v_ref[...] + (1.0 - ADAM_B2) * g * g
        nm_ref[...] = nm
        nv_ref[...] = nv
        d_ref[...] = -ADAM_LR * ((nm * c1) / (jnp.sqrt(nv * c2) + ADAM_EPS) + ADAM_WD * w_ref[...])

    spec = pl.BlockSpec((tr, C), lambda i: (i, 0))
    return pl.pallas_call(
        body, name=name, grid=(R // tr,),
        in_specs=[spec] * 4, out_specs=[spec] * 3,
        out_shape=[jax.ShapeDtypeStruct((R, C), F32)] * 3,
        compiler_params=_params(),
    )(w, g, m, v)


def _adamw_halves(ws, mines, sibs, ms, vs, c, name, exchange=None):
    n, nb = len(ws), 4
    c1 = 1.0 / (1.0 - ADAM_B1 ** ADAM_STEP)
    c2 = 1.0 / (1.0 - ADAM_B2 ** ADAM_STEP)

    def body(c_ref, *refs):
        own = (pl.program_id(0) // nb) == c_ref[0]
        for i in range(n):
            w_ref, a_ref, b_ref, m_ref, v_ref = refs[5 * i:5 * i + 5]
            g_ref, d_ref, nm_ref, nv_ref = refs[5 * n + 4 * i:5 * n + 4 * i + 4]
            g = jnp.where(own, a_ref[...], b_ref[...])
            nm = ADAM_B1 * m_ref[...] + (1.0 - ADAM_B1) * g
            nv = ADAM_B2 * v_ref[...] + (1.0 - ADAM_B2) * g * g
            g_ref[...] = g
            nm_ref[...] = nm
            nv_ref[...] = nv
            d_ref[...] = -ADAM_LR * ((nm * c1) / (jnp.sqrt(nv * c2) + ADAM_EPS) + ADAM_WD * w_ref[...])

    in_specs, out_specs, out_shape, args = [], [], [], []
    for w, a, b, m, v in zip(ws, mines, sibs, ms, vs):
        R, C = w.shape
        tr = R // (2 * nb)
        assert tr % 8 == 0 and a.shape == (R // 2, C)
        full = pl.BlockSpec((tr, C), lambda i, c_ref: (i, 0))
        half = pl.BlockSpec((tr, C), lambda i, c_ref: (i % nb, 0))
        in_specs += [full, half, half, full, full]
        out_specs += [full] * 4
        out_shape += [jax.ShapeDtypeStruct((R, C), F32)] * 4
        args += [w, a, b, m, v]
    out = _launch(body, name, (2 * nb,), in_specs, out_specs, out_shape, [], (c, *args), exchange, prefetch=1)
    return [tuple(out[4 * i:4 * i + 4]) for i in range(n)], list(out[4 * n:])


def _add4(fs, name):
    n = len(fs)

    def body(*refs):
        for a_ref, o_ref in zip(refs[:n], refs[n:]):
            o_ref[...] = ((a_ref[0].astype(F32) + a_ref[1].astype(F32)) + a_ref[2].astype(F32)) + a_ref[3].astype(F32)

    for f in fs:
        assert (f.shape[1] // 2) % 16 == 0
    return pl.pallas_call(
        body, name=name, grid=(2,),
        in_specs=[pl.BlockSpec((4, f.shape[1] // 2, f.shape[2]), lambda i: (0, i, 0)) for f in fs],
        out_specs=[pl.BlockSpec((f.shape[1] // 2, f.shape[2]), lambda i: (i, 0)) for f in fs],
        out_shape=[jax.ShapeDtypeStruct(f.shape[1:], F32) for f in fs], compiler_params=_params())(*fs)


def _gather_first(wsrc, cpack):
    def body(w_ref, c_ref, gw_ref, gc_ref, send_sems, recv_sems, local_sem, csend, crecv, clocal):
        x, y, c = _pos()
        me = 2 * x + y
        chips = _other_chips(x, y)
        start, forward, finish = _gather_steps(w_ref, gw_ref, send_sems, recv_sems, local_sem)
        start()
        loc = pltpu.make_async_copy(c_ref, gc_ref.at[me], clocal)
        loc.start()

        def conv_copy(k, slot):
            px, py = chips[k]
            return pltpu.make_async_remote_copy(src_ref=c_ref, dst_ref=gc_ref.at[slot], send_sem=csend.at[k],
                                                recv_sem=crecv.at[k], device_id=(px, py, c), device_id_type=MESH)

        for k in range(3):
            conv_copy(k, me).start()
        forward()
        finish()
        for k, (px, py) in enumerate(chips):
            conv_copy(k, 2 * px + py).wait_recv()
        for k in range(3):
            conv_copy(k, me).wait_send()
        loc.wait()

    anyspec = pl.BlockSpec(memory_space=pl.ANY)
    return pl.pallas_call(
        body, name="gather_first",
        in_specs=[anyspec, anyspec], out_specs=[anyspec, anyspec],
        out_shape=[jax.ShapeDtypeStruct((4,) + wsrc.shape, wsrc.dtype), jax.ShapeDtypeStruct((4,) + cpack.shape, cpack.dtype)],
        scratch_shapes=GATHER_SCRATCH + [pltpu.SemaphoreType.DMA((3,)), pltpu.SemaphoreType.DMA((3,)), pltpu.SemaphoreType.DMA],
        compiler_params=_params(has_side_effects=True),
    )(wsrc, cpack)


def _all_devices_exchange(s):
    def make(ins, outs, sems):
        s_ref, o_ref = ins[0], outs[0]
        send_sems, recv_sems, local_sem = sems
        x, y, c = _pos()
        me = 4 * x + 2 * y + c
        loc = pltpu.make_async_copy(s_ref, o_ref.at[me], local_sem)

        def copy(k, slot):
            peer = (x ^ (k >> 2), y ^ ((k >> 1) & 1), c ^ (k & 1))
            return pltpu.make_async_remote_copy(src_ref=s_ref, dst_ref=o_ref.at[slot], send_sem=send_sems.at[k - 1],
                                                recv_sem=recv_sems.at[k - 1], device_id=peer, device_id_type=MESH)

        def start():
            loc.start()
            for k in range(1, 8):
                copy(k, me).start()

        def finish():
            for k in range(1, 8):
                copy(k, 4 * (x ^ (k >> 2)) + 2 * (y ^ ((k >> 1) & 1)) + (c ^ (k & 1))).wait_recv()
            for k in range(1, 8):
                copy(k, me).wait_send()
            loc.wait()

        return start, lambda: None, finish

    return _Exchange([s], [jax.ShapeDtypeStruct((8,) + s.shape, s.dtype)],
                     [pltpu.SemaphoreType.DMA((7,)), pltpu.SemaphoreType.DMA((7,)), pltpu.SemaphoreType.DMA], make)


def _sum_devices(a):
    def body(a_ref, o_ref):
        acc = a_ref[0]
        for d in range(1, 8):
            acc = acc + a_ref[d]
        o_ref[...] = acc

    vm = pl.BlockSpec(memory_space=pltpu.VMEM)
    return pl.pallas_call(body, name="sum_devices", in_specs=[vm], out_specs=vm,
                          out_shape=jax.ShapeDtypeStruct(a.shape[1:], F32), compiler_params=_params())(a)


def _swap_exchange(gs):
    n = len(gs)

    def make(ins, outs, sems):
        x, y, c = _pos()
        cps = []
        for i in range(n):
            half = gs[i].shape[1] // 2
            rows = pl.ds(pl.multiple_of((1 - c) * half, 8), half)
            cps.append(pltpu.make_async_remote_copy(src_ref=ins[i].at[:, rows, :], dst_ref=outs[i], send_sem=sems[0].at[i],
                                                    recv_sem=sems[1].at[i], device_id=(x, y, 1 - c), device_id_type=MESH))

        def start():
            for cp in cps:
                cp.start()

        def finish():
            for cp in cps:
                cp.wait()

        return start, lambda: None, finish

    return _Exchange(gs, [jax.ShapeDtypeStruct((4, g.shape[1] // 2, g.shape[2]), g.dtype) for g in gs],
                     [pltpu.SemaphoreType.DMA((n,)), pltpu.SemaphoreType.DMA((n,))], make)


def _scatter_exchange(ss):
    n = len(ss)

    def make(ins, outs, sems):
        send_sems, recv_sems, local_sems = sems
        x, y, c = _pos()
        me = 2 * x + y
        chips = _other_chips(x, y)
        locs = [pltpu.make_async_copy(ins[i].at[me], outs[i].at[me], local_sems.at[i]) for i in range(n)]

        def copy(i, k, src_slot, dst_slot):
            px, py = chips[k]
            return pltpu.make_async_remote_copy(src_ref=ins[i].at[src_slot], dst_ref=outs[i].at[dst_slot],
                                                send_sem=send_sems.at[3 * i + k], recv_sem=recv_sems.at[3 * i + k],
                                                device_id=(px, py, c), device_id_type=MESH)

        def start():
            for i in range(n):
                locs[i].start()
                for k, (px, py) in enumerate(chips):
                    copy(i, k, 2 * px + py, me).start()

        def finish():
            for i in range(n):
                for k, (px, py) in enumerate(chips):
                    copy(i, k, me, 2 * px + py).wait_recv()
            for i in range(n):
                for k, (px, py) in enumerate(chips):
                    copy(i, k, 2 * px + py, me).wait_send()
                locs[i].wait()

        return start, lambda: None, finish

    return _Exchange(ss, [jax.ShapeDtypeStruct(s.shape, s.dtype) for s in ss],
                     [pltpu.SemaphoreType.DMA((3 * n,)), pltpu.SemaphoreType.DMA((3 * n,)), pltpu.SemaphoreType.DMA((n,))], make)


def _send_exchange(rs):
    n = len(rs)

    def make(ins, outs, sems):
        x, y, c = _pos()
        cps = [pltpu.make_async_remote_copy(src_ref=ins[i], dst_ref=outs[i], send_sem=sems[0].at[i], recv_sem=sems[1].at[i],
                                            device_id=(x, y, 1 - c), device_id_type=MESH) for i in range(n)]

        def start():
            for cp in cps:
                cp.start()

        def finish():
            for cp in cps:
                cp.wait()

        return start, lambda: None, finish

    return _Exchange(rs, [jax.ShapeDtypeStruct(r.shape, r.dtype) for r in rs],
                     [pltpu.SemaphoreType.DMA((n,)), pltpu.SemaphoreType.DMA((n,))], make)


def _run_exchange(ex, name):
    ei, eo = len(ex.args), len(ex.out_shape)

    def body(*refs):
        start, forward, finish = ex.make(refs[:ei], refs[ei:ei + eo], refs[ei + eo:])
        start()
        forward()
        finish()

    anyspec = pl.BlockSpec(memory_space=pl.ANY)
    return pl.pallas_call(body, name=name, in_specs=[anyspec] * ei, out_specs=[anyspec] * eo, out_shape=ex.out_shape,
                          scratch_shapes=ex.scratch, compiler_params=_params(has_side_effects=True))(*ex.args)


def _add_half(gs, rs, c, name):
    n = len(gs)

    def body(c_ref, *refs):
        for g_ref, r_ref, o_ref in zip(refs[:n], refs[n:2 * n], refs[2 * n:]):
            o_ref[...] = (g_ref[...] + r_ref[...]).astype(BF16)

    g_specs, r_specs, out_shape = [], [], []
    for g, r in zip(gs, rs):
        _, H, C = r.shape
        tr = H // 2
        assert tr % 16 == 0 and g.shape == (4, 2 * H, C)
        g_specs.append(pl.BlockSpec((1, tr, C), lambda j, i, c_ref: (j, c_ref[0] * 2 + i, 0)))
        r_specs.append(pl.BlockSpec((1, tr, C), lambda j, i, c_ref: (j, i, 0)))
        out_shape.append(jax.ShapeDtypeStruct((4, H, C), BF16))
    grid_spec = pltpu.PrefetchScalarGridSpec(num_scalar_prefetch=1, grid=(4, 2), in_specs=g_specs + r_specs, out_specs=r_specs)
    return pl.pallas_call(body, name=name, grid_spec=grid_spec, out_shape=out_shape, compiler_params=_params())(c, *gs, *rs)


def _block_diag(w):
    eye = jnp.eye(RNN_BLOCKS, dtype=w.dtype)
    return (eye[:, None, :, None] * w[:, :, None, :]).reshape(D_RNN, D_RNN)


def _diag_blocks(wd):
    d = wd.reshape(RNN_BLOCKS, 64, RNN_BLOCKS, 64)
    return jnp.stack([d[h, :, h, :] for h in range(RNN_BLOCKS)])


def _split_pack(a, first, last):
    out, base = {}, PACK_OFF[first]
    for i in range(first, last):
        s = a[:, PACK_OFF[i] - base:PACK_OFF[i + 1] - base]
        out[BIG_KEYS[i]] = s.reshape(4 * 256, 256) if BIG_KEYS[i] == "w_p_t" else s.reshape(-1, 1024)
    return out


def _layer_grads(x, p, tgt, gw, small, shard=None, core=None):
    row = lambda v: v.reshape(1, -1)
    wa = _block_diag(small["gate_a_w"]).astype(MXU_DTYPE)
    wx = _block_diag(small["gate_x_w"]).astype(MXU_DTYPE)
    sinks = small["attn_sinks"].reshape(1, HEADS)

    dist = shard is not None
    q, kv, xr, gr, xb = _in_proj(x, gw["w_in_t"])
    cut = PACK_OFF[1] + PACK_ROWS[1] // 2
    att, *ga = _attn_fwd(q, kv, sinks, _gather_exchange(shard[PACK_OFF[1]:cut]) if dist else None)
    xc, h, rec, *gb = _rnn_fwd(xr, gr, small["rnn_conv_w"], row(small["rnn_conv_b"]), wa, row(small["gate_a_b"]),
                               wx, row(small["gate_x_b"]), row(small["lru_lambda"]),
                               _gather_exchange(shard[cut:PACK_OFF[3]]) if dist else None)
    if dist:
        gw = {**gw, **_split_pack(jnp.concatenate([ga[0], gb[0]], axis=1), 1, 3)}
    g1, b1 = row(small["ln1_g"]), row(small["ln1_b"])
    fcw = small["ffn_conv_w"].reshape(3, NC, FF_CHUNK).transpose(1, 0, 2)
    fcb = small["ffn_conv_b"].reshape(NC, 1, FF_CHUNK)
    z1, h1b = _out_proj(att, rec, x, gw["w_out"], g1, b1)
    gate, ge, vd, act, *gc = _ffn_up(h1b, gw["w_up_t"], fcw, fcb,
                                     _gather_exchange(shard[PACK_OFF[3]:PACK_OFF[6]]) if dist else None)
    if dist:
        gw = {**gw, **_split_pack(gc[0], 3, 6)}
    dz2, dz2b, dpre, dpp, vec2 = _ffn_down(act, z1, p, tgt, gw["w_down"], gw["w_g"], gw["w_p_t"], g1, b1,
                                           row(small["ln2_g"]), row(small["ln2_b"]), row(small["ple_gate_b"]))
    dup, dfc = _ffn_bwd(dz2b, gate, ge, vd, gw["w_down"], fcw)
    dz1, vec1 = _ffn_dh1(dup, dz2, dpre, z1, gw["w_up_t"], gw["w_g"], g1, b1)
    per_chip = 2 * D_FF // 4 // FF_CHUNK
    big = {
        "w_ffn_up": _weight_grad_cols(
            h1b, dup.reshape(2 * NC, -1, FF_CHUNK), "dw_up", 2 * NC,
            lambda bt: pl.BlockSpec((None, bt, FF_CHUNK), lambda m, k: (m, k, 0)), (4, D, 2 * D_FF // 4),
            pl.BlockSpec((None, D, FF_CHUNK), lambda m, k: (2 * (m % 2) + (m // 2) // per_chip, 0, (m // 2) % per_chip)))[0],
        "w_ffn_down": _weight_grad(act, dz2b, 512, "dw_down").reshape(4, D_FF // 4, D),
        "ple_gate_w": _weight_grad(h1b, dpre, 512, "dw_gate").reshape(4, D // 4, D),
        "ple_proj": _weight_grad_cols(
            p.astype(BF16), dpp, "dw_proj", 4, lambda bt: pl.BlockSpec((bt, D // 4), lambda j, k: (k, j)),
            (4, PLE, D // 4), pl.BlockSpec((None, PLE, D // 4), lambda j, k: (j, 0, 0)))[0],
        "w_out": _weight_grad(jnp.concatenate([att, rec], axis=1), dz1, 512, "dw_out").reshape(4, D // 4, D),
    }
    reduced = None
    if dist:
        g_ffn = [big[k] for k in EARLY_WEIGHTS]
        ex = _swap_exchange(g_ffn)
    datt, drec, *got = _out_proj_bwd(dz1, gw["w_out"], ex if dist else None)
    if dist:
        sums = _add_half(g_ffn, got, core, "add_half_ffn")
        ex, ex2 = _scatter_exchange(sums[:1]), _scatter_exchange(sums[1:])
    dxr, dgr, dwa, dwx, dvec, *got = _rnn_bwd(drec, gr, h, xc, xr, small["rnn_conv_w"], wa, row(small["gate_a_b"]),
                                              wx, row(small["gate_x_b"]), row(small["lru_lambda"]), ex if dist else None)
    dq, dkv, dsinks, *got2 = _attn_bwd(q, kv, datt, sinks, ex2 if dist else None)
    if dist:
        mine = _add4(got + got2, "add_chips_ffn")
        big = {}
    sg = {
        "attn_sinks": dsinks[:, 0],
        "rnn_conv_w": dvec[4:8],
        "rnn_conv_b": dvec[3],
        "gate_a_w": _diag_blocks(dwa),
        "gate_a_b": dvec[0],
        "gate_x_w": _diag_blocks(dwx),
        "gate_x_b": dvec[1],
        "lru_lambda": dvec[2],
        "ln1_g": vec1[0],
        "ln1_b": vec1[1],
        "ffn_conv_w": dfc[:, 0:3].transpose(1, 0, 2).reshape(3, D_FF),
        "ffn_conv_b": dfc[:, 3].reshape(D_FF),
        "ple_gate_b": vec2[3],
        "ln2_g": vec2[1],
        "ln2_b": vec2[2],
    }
    loss = vec2[0, 0:1]
    grad_x, du = _in_proj_bwd(dq, dkv, dxr, dgr, dz1, gw["w_in_t"])
    ex = None
    if dist:
        ex = _join_exchanges(_send_exchange(mine), _all_devices_exchange(_pack_vecs([sg[k] for k in SMALL] + [loss])[0]))
    big["w_in"], *got = _weight_grad_cols(
        xb, du, "dw_in", 4, lambda bt: pl.BlockSpec((None, bt, D_IN // 4), lambda j, k: (j, k, 0)), (4, D, D_IN // 4),
        pl.BlockSpec((None, D, D_IN // 4), lambda j, k: (j, 0, 0)), ex)
    if dist:
        reduced = (mine, got[:len(mine)])
    return grad_x, big, sg, loss, reduced, got[-1:]


BIG = ("w_in", "w_ffn_up", "w_out", "w_ffn_down", "ple_gate_w", "ple_proj")
BIG_KEYS = ("w_in_t", "w_up_t", "w_out", "w_down", "w_g", "w_p_t")
BIG_T = (True, True, False, False, False, True)
EARLY_WEIGHTS = ("w_ffn_up", "w_ffn_down", "ple_gate_w", "ple_proj", "w_out")
LATE_WEIGHTS = ("w_in",)
SMALL = ("attn_sinks", "rnn_conv_w", "rnn_conv_b", "gate_a_w", "gate_a_b", "gate_x_w", "gate_x_b", "lru_lambda",
         "ln1_g", "ln1_b", "ffn_conv_w", "ffn_conv_b", "ple_gate_b", "ln2_g", "ln2_b")
SHARDED_SMALL = ("rnn_conv_w", "ffn_conv_w")
WEIGHTS = ("w_in", "attn_sinks", "rnn_conv_w", "rnn_conv_b", "gate_a_w", "gate_a_b", "gate_x_w", "gate_x_b",
           "lru_lambda", "w_out", "ln1_g", "ln1_b", "w_ffn_up", "ffn_conv_w", "ffn_conv_b", "w_ffn_down",
           "ple_gate_w", "ple_gate_b", "ple_proj", "ln2_g", "ln2_b")


def _pack_big(d, first=0, last=6):
    parts = []
    for name, t in zip(BIG[first:last], BIG_T[first:last]):
        a = d[name]
        a = a.T if t else a
        parts.append(a.reshape(-1, 1024))
    return jnp.concatenate(parts, axis=0)


def _pack_vecs(items):
    parts, offs, n = [], [], 0
    for a in items:
        f = a.reshape(-1).astype(F32)
        pad = (-f.shape[0]) % 128
        parts.append(jnp.pad(f, (0, pad)))
        offs.append(n)
        n += (f.shape[0] + pad) // 128
    padr = (-n) % 8
    if padr:
        parts.append(jnp.zeros((padr * 128,), F32))
    return jnp.concatenate(parts).reshape(-1, 128), offs


def _unpack_vecs(a, offs, shapes):
    flat = a.reshape(-1)
    out = []
    for o, s in zip(offs, shapes):
        n = 1
        for d in s:
            n *= d
        out.append(flat[o * 128:o * 128 + n].reshape(s))
    return out


def kernel(x, p, w_in, attn_sinks, rnn_conv_w, rnn_conv_b, gate_a_w, gate_a_b, gate_x_w, gate_x_b, lru_lambda, w_out, ln1_g, ln1_b, w_ffn_up, ffn_conv_w, ffn_conv_b, w_ffn_down, ple_gate_w, ple_gate_b, ple_proj, ln2_g, ln2_b, loss_target, m_w_in, m_attn_sinks, m_rnn_conv_w, m_rnn_conv_b, m_gate_a_w, m_gate_a_b, m_gate_x_w, m_gate_x_b, m_lru_lambda, m_w_out, m_ln1_g, m_ln1_b, m_w_ffn_up, m_ffn_conv_w, m_ffn_conv_b, m_w_ffn_down, m_ple_gate_w, m_ple_gate_b, m_ple_proj, m_ln2_g, m_ln2_b, v_w_in, v_attn_sinks, v_rnn_conv_w, v_rnn_conv_b, v_gate_a_w, v_gate_a_b, v_gate_x_w, v_gate_x_b, v_lru_lambda, v_w_out, v_ln1_g, v_ln1_b, v_w_ffn_up, v_ffn_conv_w, v_ffn_conv_b, v_w_ffn_down, v_ple_gate_w, v_ple_gate_b, v_ple_proj, v_ln2_g, v_ln2_b):
    w = dict(w_in=w_in, attn_sinks=attn_sinks, rnn_conv_w=rnn_conv_w, rnn_conv_b=rnn_conv_b, gate_a_w=gate_a_w,
             gate_a_b=gate_a_b, gate_x_w=gate_x_w, gate_x_b=gate_x_b, lru_lambda=lru_lambda, w_out=w_out, ln1_g=ln1_g,
             ln1_b=ln1_b, w_ffn_up=w_ffn_up, ffn_conv_w=ffn_conv_w, ffn_conv_b=ffn_conv_b, w_ffn_down=w_ffn_down,
             ple_gate_w=ple_gate_w, ple_gate_b=ple_gate_b, ple_proj=ple_proj, ln2_g=ln2_g, ln2_b=ln2_b)
    m = dict(w_in=m_w_in, attn_sinks=m_attn_sinks, rnn_conv_w=m_rnn_conv_w, rnn_conv_b=m_rnn_conv_b, gate_a_w=m_gate_a_w,
             gate_a_b=m_gate_a_b, gate_x_w=m_gate_x_w, gate_x_b=m_gate_x_b, lru_lambda=m_lru_lambda, w_out=m_w_out,
             ln1_g=m_ln1_g, ln1_b=m_ln1_b, w_ffn_up=m_w_ffn_up, ffn_conv_w=m_ffn_conv_w, ffn_conv_b=m_ffn_conv_b,
             w_ffn_down=m_w_ffn_down, ple_gate_w=m_ple_gate_w, ple_gate_b=m_ple_gate_b, ple_proj=m_ple_proj,
             ln2_g=m_ln2_g, ln2_b=m_ln2_b)
    v = dict(w_in=v_w_in, attn_sinks=v_attn_sinks, rnn_conv_w=v_rnn_conv_w, rnn_conv_b=v_rnn_conv_b, gate_a_w=v_gate_a_w,
             gate_a_b=v_gate_a_b, gate_x_w=v_gate_x_w, gate_x_b=v_gate_x_b, lru_lambda=v_lru_lambda, w_out=v_w_out,
             ln1_g=v_ln1_g, ln1_b=v_ln1_b, w_ffn_up=v_w_ffn_up, ffn_conv_w=v_ffn_conv_w, ffn_conv_b=v_ffn_conv_b,
             w_ffn_down=v_w_ffn_down, ple_gate_w=v_ple_gate_w, ple_gate_b=v_ple_gate_b, ple_proj=v_ple_proj,
             ln2_g=v_ln2_g, ln2_b=v_ln2_b)
    w, m, v = ({k: a[0] for k, a in d.items()} for d in (w, m, v))
    chip = 2 * lax.axis_index("x") + lax.axis_index("y")
    core = lax.axis_index("c")

    wpack = _pack_big(w)
    cpack, _ = _pack_vecs([w["rnn_conv_w"], w["ffn_conv_w"]])
    shard = wpack.astype(MXU_DTYPE)
    g_in, gcp = _gather_first(shard[PACK_OFF[0]:PACK_OFF[1]], cpack)
    gw = _split_pack(g_in, 0, 1)
    small = {k: w[k] for k in SMALL}
    small["rnn_conv_w"] = gcp[:, 0:4].reshape(4, 4, 128).transpose(1, 0, 2).reshape(4, 512)
    small["ffn_conv_w"] = gcp[:, 4:22].reshape(4, 3, 768).transpose(1, 0, 2).reshape(3, 3072)

    core1 = core.reshape(1).astype(jnp.int32)
    grad_x, big, sg, loss, ffn_halves, small_all = _layer_grads(x[0], p[0, 0], loss_target[0], gw, small, shard, core1)

    shapes = [sg[k].shape for k in SMALL] + [(1,)]
    _, offs = _pack_vecs([jnp.zeros(s, F32) for s in shapes])
    red = dict(zip(SMALL + ("loss",), _unpack_vecs(_sum_devices(small_all[0]), offs, shapes)))
    red["rnn_conv_w"] = lax.dynamic_slice_in_dim(red["rnn_conv_w"], chip * 128, 128, axis=1)
    red["ffn_conv_w"] = lax.dynamic_slice_in_dim(red["ffn_conv_w"], chip * 768, 768, axis=1)

    g_late = [big[k] for k in LATE_WEIGHTS]
    sib = _run_exchange(_swap_exchange(g_late), "swap_late")
    from_chips = _run_exchange(_scatter_exchange(_add_half(g_late, sib, core1, "add_half_late")), "scatter_late")
    late_mine = _add4(from_chips, "add_chips_late")
    late_other = _run_exchange(_send_exchange(late_mine), "send_late")

    def adamw(names, mine, other, name):
        out, _ = _adamw_halves([w[k] for k in names], mine, other, [m[k] for k in names], [v[k] for k in names],
                               core1, name)
        return dict(zip(names, out))

    big_out = {**adamw(LATE_WEIGHTS, late_mine, late_other, "adamw_late"), **adamw(EARLY_WEIGHTS, *ffn_halves, "adamw_early")}
    wsm, offs2 = _pack_vecs([w[k] for k in SMALL])
    gsm, _ = _pack_vecs([red[k] for k in SMALL])
    msm, _ = _pack_vecs([m[k] for k in SMALL])
    vsm, _ = _pack_vecs([v[k] for k in SMALL])
    dsm, nmsm, nvsm = _adamw(wsm, gsm, msm, vsm, "adamw_small")
    shapes2 = [w[k].shape for k in SMALL]

    def named(n, smallp):
        d = {k: out[n][None] for k, out in big_out.items()}
        d.update({k: a[None] for k, a in zip(SMALL, _unpack_vecs(smallp, offs2, shapes2))})
        return [d[k] for k in WEIGHTS]

    return (red["loss"].reshape(()), grad_x[None], *named(0, gsm), *named(1, dsm), *named(2, nmsm), *named(3, nvsm))
```

```python
import functools

import jax
import jax.numpy as jnp
from jax import lax
from jax.experimental import pallas as pl
from jax.experimental.pallas import tpu as pltpu

F32 = jnp.float32
BF16 = jnp.bfloat16
MXU_DTYPE = jnp.bfloat16

D = 1024
D_ATT = 512
D_KV = 128
D_RNN = 512
D_IN = 1792
D_FF = 3072
FF_CHUNK = 512
PLE = 256
HEADS = 8
HEAD_DIM = 64
BLK = 128
ATTN_BLOCKS = 2
DW_TOKENS = 4096
RNN_BLOCKS = 8
LN_EPS = 1e-5
LRU_C = 8.0
ALPHA = float(2.0 ** 0.25)
SCALE = HEAD_DIM ** -0.5
NEG = -1e30

ADAM_LR = 0.001
ADAM_B1 = 0.9
ADAM_B2 = 0.999
ADAM_EPS = 1e-08
ADAM_WD = 0.01
ADAM_STEP = 10

VMEM_LIMIT_BYTES = 56 * 1024 * 1024
MESH = pl.DeviceIdType.MESH

PACK_ROWS = (448, 1536, 256, 768, 256, 64)
PACK_OFF = tuple(sum(PACK_ROWS[:i]) for i in range(len(PACK_ROWS) + 1))
PACK_TOTAL = PACK_OFF[-1]


def _params(**kw):
    return pltpu.CompilerParams(vmem_limit_bytes=VMEM_LIMIT_BYTES, **kw)


def _mm(a, b):
    return jnp.dot(a.astype(MXU_DTYPE), b.astype(MXU_DTYPE), preferred_element_type=F32)


def _mm_nt(a, b):
    return lax.dot_general(a.astype(MXU_DTYPE), b.astype(MXU_DTYPE), (((1,), (1,)), ((), ())),
                           preferred_element_type=F32)


def _mm_tn(a, b):
    return lax.dot_general(a.astype(MXU_DTYPE), b.astype(MXU_DTYPE), (((0,), (0,)), ((), ())),
                           preferred_element_type=F32)


def _sigmoid(x):
    return 0.5 + 0.5 * jnp.tanh(0.5 * x)


def _gelu(x):
    c = 0.7978845608028654
    k = 0.044715
    x2 = x * x
    t = jnp.tanh(x * (c + (c * k) * x2))
    h = 0.5 * (1.0 + t)
    return x * h, h * (1.0 + (x * (1.0 - t)) * (c + (3.0 * c * k) * x2))


def _shift_rows(x, s, edge8):
    R = x.shape[0]
    row8 = lax.broadcasted_iota(jnp.int32, (8, x.shape[1]), 0)
    if s > 0:
        rolled = pltpu.roll(x, s, 0)
        first = jnp.where(row8 < s, pltpu.roll(edge8, s, 0), rolled[0:8])
        return jnp.concatenate([first, rolled[8:]], axis=0)
    k = -s
    rolled = pltpu.roll(x, R - k, 0)
    last = jnp.where(row8 >= 8 - k, pltpu.roll(edge8, 8 - k, 0), rolled[R - 8:])
    return jnp.concatenate([rolled[:R - 8], last], axis=0)


def _softplus(x):
    return jnp.maximum(x, 0.0) + jnp.log(1.0 + jnp.exp(-jnp.abs(x)))


def _ln(z, g, b):
    mu = jnp.mean(z, axis=-1, keepdims=True)
    zc = z - mu
    var = jnp.mean(zc * zc, axis=-1, keepdims=True)
    rstd = lax.rsqrt(var + LN_EPS)
    xhat = zc * rstd
    return xhat * g + b, xhat, rstd


def _ln_bwd(dy, xhat, rstd, g):
    dxh = dy * g
    m1 = jnp.mean(dxh, axis=-1, keepdims=True)
    m2 = jnp.mean(dxh * xhat, axis=-1, keepdims=True)
    return rstd * (dxh - m1 - xhat * m2)


def _colsum(x):
    return jnp.sum(x, axis=0, keepdims=True)


def _full(shape):
    nd = len(shape)
    return pl.BlockSpec(shape, lambda *_: (0,) * nd)


def _rows(tm, cols, fn=None):
    if fn is None:
        return pl.BlockSpec((tm, cols), lambda i: (i, 0))
    return pl.BlockSpec((tm, cols), lambda i: (fn(i), 0))


def _heads(tm):
    return pl.BlockSpec((HEADS, tm, HEAD_DIM), lambda i: (0, i, 0))


def _in_proj(x, w_in_t):
    T = x.shape[0]
    tm = 512

    def body(x_ref, w_ref, q_ref, kv_ref, xr_ref, gr_ref, xb_ref):
        xb = x_ref[...].astype(MXU_DTYPE)
        xb_ref[...] = xb.astype(BF16)
        q = _mm_nt(xb, w_ref[0:512, :])
        for h in range(HEADS):
            q_ref[h] = q[:, h * 64:(h + 1) * 64].astype(BF16)
        kv_ref[...] = _mm_nt(xb, w_ref[512:768, :]).astype(BF16)
        xr_ref[...] = _mm_nt(xb, w_ref[768:1280, :])
        gr_ref[...] = _mm_nt(xb, w_ref[1280:1792, :])

    return pl.pallas_call(
        body, name="in_proj", grid=(T // tm,),
        in_specs=[_rows(tm, D), _full((D_IN, D))],
        out_specs=[_heads(tm), _rows(tm, 256), _rows(tm, 512), _rows(tm, 512), _rows(tm, D)],
        out_shape=[jax.ShapeDtypeStruct((HEADS, T, 64), BF16), jax.ShapeDtypeStruct((T, 256), BF16),
                   jax.ShapeDtypeStruct((T, 512), F32), jax.ShapeDtypeStruct((T, 512), F32),
                   jax.ShapeDtypeStruct((T, D), BF16)],
        compiler_params=_params(),
    )(x, w_in_t)


def _attn_band(kv_ref, i):
    cur = pl.multiple_of(i * BLK, BLK)
    prev = pl.multiple_of(jnp.maximum(i - 1, 0) * BLK, BLK)
    band = jnp.concatenate([kv_ref[pl.ds(prev, BLK), :], kv_ref[pl.ds(cur, BLK), :]], axis=0)
    key = lax.broadcasted_iota(jnp.int32, (2 * BLK, 4 * BLK), 0)
    qry = lax.broadcasted_iota(jnp.int32, (2 * BLK, 4 * BLK), 1) & (BLK - 1)
    in_prev = jnp.logical_and(jnp.logical_and(key < BLK, key > qry), i > 0)
    mask = jnp.logical_or(in_prev, jnp.logical_and(key >= BLK, key - BLK <= qry))
    return band, mask, cur, prev


def _attn_scores(band, mask, qs, s_ref, g):
    st = jnp.where(mask, _mm_nt(band[:, g * 64:(g + 1) * 64], qs) * SCALE, NEG)
    lane = lax.broadcasted_iota(jnp.int32, (1, 4 * BLK), 1)
    sv = jnp.where(lane < BLK, s_ref[0, 4 * g],
                   jnp.where(lane < 2 * BLK, s_ref[0, 4 * g + 1], jnp.where(lane < 3 * BLK, s_ref[0, 4 * g + 2], s_ref[0, 4 * g + 3])))
    m = jnp.maximum(jnp.max(st, axis=0, keepdims=True), sv)
    p = jnp.exp(st - m)
    ps = jnp.exp(sv - m)
    return p, ps, jnp.sum(p, axis=0, keepdims=True) + ps


def _pos():
    return lax.axis_index("x"), lax.axis_index("y"), lax.axis_index("c")


def _other_chips(x, y):
    return [(1 - x, y), (x, 1 - y), (1 - x, 1 - y)]


def _gather_steps(w_ref, gw_ref, send_sems, recv_sems, local_sem):
    x, y, c = _pos()
    me = 2 * x + y
    chips = _other_chips(x, y)
    half = w_ref.shape[0] // 2
    mine = pl.ds(pl.multiple_of(c * half, 16), half)
    theirs = pl.ds(pl.multiple_of((1 - c) * half, 16), half)
    loc = pltpu.make_async_copy(w_ref, gw_ref.at[me], local_sem)

    def copy(k, src, dst, to):
        return pltpu.make_async_remote_copy(src_ref=src, dst_ref=dst, send_sem=send_sems.at[k], recv_sem=recv_sems.at[k],
                                            device_id=to, device_id_type=MESH)

    def out(k):
        px, py = chips[k]
        return copy(k, w_ref.at[mine], gw_ref.at[me, mine], (px, py, c))

    def fwd(k, rows):
        px, py = chips[k]
        return copy(3 + k, gw_ref.at[2 * px + py, rows], gw_ref.at[2 * px + py, rows], (x, y, 1 - c))

    def start():
        loc.start()
        for k in range(3):
            out(k).start()

    def forward():
        for k in range(3):
            px, py = chips[k]
            copy(k, w_ref.at[mine], gw_ref.at[2 * px + py, mine], (px, py, c)).wait_recv()
            fwd(k, mine).start()

    def finish():
        for k in range(3):
            fwd(k, theirs).wait_recv()
        for k in range(3):
            out(k).wait_send()
            fwd(k, mine).wait_send()
        loc.wait()

    return start, forward, finish


GATHER_SCRATCH = [pltpu.SemaphoreType.DMA((6,)), pltpu.SemaphoreType.DMA((6,)), pltpu.SemaphoreType.DMA]


class _Exchange:
    def __init__(self, args, out_shape, scratch, make):
        self.args, self.out_shape, self.scratch, self.make = list(args), list(out_shape), list(scratch), make


def _join_exchanges(a, b):
    na, nao, nas = len(a.args), len(a.out_shape), len(a.scratch)

    def make(ins, outs, sems):
        steps_a = a.make(ins[:na], outs[:nao], sems[:nas])
        steps_b = b.make(ins[na:], outs[nao:], sems[nas:])

        def both(f, g):
            def run():
                f()
                g()
            return run

        return tuple(both(f, g) for f, g in zip(steps_a, steps_b))

    return _Exchange(a.args + b.args, a.out_shape + b.out_shape, a.scratch + b.scratch, make)


def _gather_exchange(wsrc):
    return _Exchange([wsrc], [jax.ShapeDtypeStruct((4,) + wsrc.shape, wsrc.dtype)], GATHER_SCRATCH,
                     lambda ins, outs, sems: _gather_steps(ins[0], outs[0], *sems))


def _launch(body, name, grid, in_specs, out_specs, out_shape, scratch, args, exchange=None, prefetch=0):
    def call(fn, fn_name, ins, outs, shapes, scr, operands, effects):
        spec = pltpu.PrefetchScalarGridSpec(num_scalar_prefetch=prefetch, grid=grid, in_specs=ins, out_specs=outs,
                                            scratch_shapes=scr)
        return pl.pallas_call(fn, name=fn_name, grid_spec=spec, out_shape=shapes,
                              compiler_params=_params(has_side_effects=effects))(*operands)

    if exchange is None:
        return call(body, name, list(in_specs), list(out_specs), list(out_shape), list(scratch), args, False)
    n_in, n_out, ei, eo, ns = len(in_specs), len(out_specs), len(exchange.args), len(exchange.out_shape), len(exchange.scratch)
    nsteps = 1
    for g in grid:
        nsteps *= g

    def wrapped(*refs):
        scalars, refs = refs[:prefetch], refs[prefetch:]
        ins, xin = refs[:n_in], refs[n_in:n_in + ei]
        outs, xout = refs[n_in + ei:n_in + ei + n_out], refs[n_in + ei + n_out:n_in + ei + n_out + eo]
        rest = refs[n_in + ei + n_out + eo:]
        own, sems = rest[:len(rest) - ns], rest[len(rest) - ns:]
        start, forward, finish = exchange.make(xin, xout, sems)
        i = pl.program_id(0)
        for d in range(1, len(grid)):
            i = i * grid[d] + pl.program_id(d)
        pl.when(i == 0)(start)
        body(*scalars, *ins, *outs, *own)
        pl.when(i == max(nsteps - 3, 0))(forward)
        pl.when(i == nsteps - 1)(finish)

    anyspec = pl.BlockSpec(memory_space=pl.ANY)
    return call(wrapped, name + "_x", list(in_specs) + [anyspec] * ei, list(out_specs) + [anyspec] * eo,
                list(out_shape) + exchange.out_shape, list(scratch) + exchange.scratch, (*args, *exchange.args), True)


def _attn_fwd(q, kv, sinks, exchange=None):
    T = kv.shape[0]

    def body(q_ref, kv_ref, s_ref, o_ref):
        for b in range(ATTN_BLOCKS):
            rows = slice(b * BLK, (b + 1) * BLK)
            band, mask, _, _ = _attn_band(kv_ref, ATTN_BLOCKS * pl.program_id(0) + b)
            for g in range(2):
                qs = q_ref[4 * g:4 * g + 4, rows, :].reshape(4 * BLK, HEAD_DIM)
                p, _, den = _attn_scores(band, mask, qs, s_ref, g)
                ot = _mm_tn(band[:, 128:256], p) * (1.0 / den)
                for hh in range(4):
                    o = ot[:, hh * BLK:(hh + 1) * BLK].T
                    o_ref[rows, (4 * g + hh) * 64:(4 * g + hh + 1) * 64] = o[:, g * 64:(g + 1) * 64].astype(BF16)

    tq = ATTN_BLOCKS * BLK
    return _launch(body, "attn_fwd", (T // tq,), [_heads(tq), _full((T, 256)), pl.BlockSpec(memory_space=pltpu.SMEM)],
                   [_rows(tq, 512)], [jax.ShapeDtypeStruct((T, 512), BF16)], [], (q, kv, sinks), exchange)


def _attn_bwd(q, kv, do, sinks, exchange=None):
    T = kv.shape[0]

    def body(q_ref, kv_ref, do_ref, s_ref, dq_ref, dkv_ref, ds_ref):
        @pl.when(pl.program_id(0) == 0)
        def _():
            ds_ref[...] = jnp.zeros_like(ds_ref)

        for b in range(ATTN_BLOCKS):
            rows = slice(b * BLK, (b + 1) * BLK)
            band, mask, cur, prev = _attn_band(kv_ref, ATTN_BLOCKS * pl.program_id(0) + b)
            for g in range(2):
                qs = q_ref[4 * g:4 * g + 4, rows, :].reshape(4 * BLK, HEAD_DIM)
                dos = do_ref[4 * g:4 * g + 4, rows, :].reshape(4 * BLK, HEAD_DIM)
                p, ps, den = _attn_scores(band, mask, qs, s_ref, g)
                inv = 1.0 / den
                p = p * inv
                dpt = _mm_nt(band[:, 128 + g * 64:192 + g * 64], dos)
                delta = jnp.sum(p * dpt, axis=0, keepdims=True)
                dst = p * (dpt - delta)
                dsv = -(ps * inv) * delta
                for hh in range(4):
                    dsink = jnp.sum(dsv[:, hh * BLK:(hh + 1) * BLK], axis=1, keepdims=True)
                    ds_ref[4 * g + hh:4 * g + hh + 1, :] += jnp.broadcast_to(dsink, (1, 128))
                dqt = _mm_tn(band[:, 0:128], dst) * SCALE
                for hh in range(4):
                    dqh = dqt[:, hh * BLK:(hh + 1) * BLK].T
                    dq_ref[rows, (4 * g + hh) * 64:(4 * g + hh + 1) * 64] = dqh[:, g * 64:(g + 1) * 64].astype(BF16)
                dk = _mm(dst, qs) * SCALE
                dv = _mm(p, dos)
                dkv_ref[pl.ds(cur, BLK), g * 64:(g + 1) * 64] = dk[BLK:2 * BLK]
                dkv_ref[pl.ds(cur, BLK), 128 + g * 64:192 + g * 64] = dv[BLK:2 * BLK]
                dkv_ref[pl.ds(prev, BLK), g * 64:(g + 1) * 64] += dk[0:BLK]
                dkv_ref[pl.ds(prev, BLK), 128 + g * 64:192 + g * 64] += dv[0:BLK]

    tq = ATTN_BLOCKS * BLK
    return _launch(body, "attn_bwd", (T // tq,),
                   [_heads(tq), _full((T, 256)), _heads(tq), pl.BlockSpec(memory_space=pltpu.SMEM)],
                   [_rows(tq, 512), _full((T, 256)), _full((8, 128))],
                   [jax.ShapeDtypeStruct((T, 512), BF16), jax.ShapeDtypeStruct((T, 256), F32),
                    jax.ShapeDtypeStruct((8, 128), F32)], [], (q, kv, do, sinks), exchange)


def _rows8(tm, cols):
    return lax.broadcasted_iota(jnp.int32, (tm, cols), 0) & 7


def _lru_gates(xc, wa, ba, wx, bx, lam):
    r = _sigmoid(_mm(xc, wa) + ba)
    ii = _sigmoid(_mm(xc, wx) + bx)
    sp = _softplus(-lam)
    la = -LRU_C * r * sp
    a = jnp.exp(la)
    m = jnp.sqrt(-jnp.tanh(la) * (a * a + 1.0))
    return r, ii, sp, a, m


def _rnn_fwd(xr, gr, cw, cb, wa, ba, wx, bx, lam, exchange=None):
    T = xr.shape[0]
    tm = 512
    C = D_RNN

    def body(xr_ref, gr_ref, cw_ref, cb_ref, wa_ref, ba_ref, wx_ref, bx_ref, lam_ref,
             xc_ref, h_ref, rec_ref, ext, a_s, b_s, carry):
        i = pl.program_id(0)

        @pl.when(i == 0)
        def _():
            ext[...] = jnp.zeros((8, C), F32)
            carry[...] = jnp.zeros((8, C), F32)

        xr = xr_ref[...]
        edge = ext[...]
        xc = cb_ref[...] + cw_ref[3:4, :] * xr
        for k in range(3):
            xc = xc + cw_ref[k:k + 1, :] * _shift_rows(xr, 3 - k, edge)
        ext[...] = xr[tm - 8:tm, :]
        xc_ref[...] = xc
        _, ii, _, a, m = _lru_gates(xc, wa_ref[...], ba_ref[...], wx_ref[...], bx_ref[...], lam_ref[...])
        b = m * ii * xc
        r8 = _rows8(tm, C)
        for d in (1, 2, 4):
            ok = r8 >= d
            a_sh = jnp.where(ok, pltpu.roll(a, d, 0), 1.0)
            b_sh = jnp.where(ok, pltpu.roll(b, d, 0), 0.0)
            b = a * b_sh + b
            a = a * a_sh
        a_s[...] = a
        b_s[...] = b

        def step(g, hin):
            s = pl.multiple_of(g * 8, 8)
            hg = a_s[pl.ds(s, 8), :] * hin + b_s[pl.ds(s, 8), :]
            h_ref[pl.ds(s, 8), :] = hg
            return jnp.broadcast_to(hg[7:8, :], (8, C))

        carry[...] = lax.fori_loop(0, tm // 8, step, carry[...])
        ge, _ = _gelu(gr_ref[...])
        rec_ref[...] = (h_ref[...] * ge).astype(BF16)

    vec = _full((1, C))
    in_specs = [_rows(tm, C), _rows(tm, C), _full((4, C)), vec, _full((C, C)), vec, _full((C, C)), vec, vec]
    out_specs = [_rows(tm, C), _rows(tm, C), _rows(tm, C)]
    out_shape = [jax.ShapeDtypeStruct((T, C), F32), jax.ShapeDtypeStruct((T, C), F32), jax.ShapeDtypeStruct((T, C), BF16)]
    scratch = [pltpu.VMEM((8, C), F32), pltpu.VMEM((tm, C), F32), pltpu.VMEM((tm, C), F32), pltpu.VMEM((8, C), F32)]
    return _launch(body, "rnn_fwd", (T // tm,), in_specs, out_specs, out_shape, scratch,
                   (xr, gr, cw, cb, wa, ba, wx, bx, lam), exchange)


def _rnn_bwd(drec, gr, h, xc, xr, cw, wa, ba, wx, bx, lam, exchange=None):
    T = xr.shape[0]
    tm = 512
    C = D_RNN
    nt = T // tm
    t8 = tm // 8

    def body(drec_ref, gr_ref, h_ref, hp_ref, xc_ref, xr_ref, cw_ref, wa_ref, ba_ref, wx_ref, bx_ref,
             lam_ref, dxr_ref, dgr_ref, dwa_ref, dwx_ref, dvec_ref, c_s, g_s, gout, ext, anext, gcarry):
        i = pl.program_id(0)
        j = nt - 1 - i

        @pl.when(i == 0)
        def _():
            dwa_ref[...] = jnp.zeros_like(dwa_ref)
            dwx_ref[...] = jnp.zeros_like(dwx_ref)
            dvec_ref[...] = jnp.zeros_like(dvec_ref)
            anext[...] = jnp.zeros((8, C), F32)
            gcarry[...] = jnp.zeros((8, C), F32)
            ext[...] = jnp.zeros((8, C), F32)

        xc = xc_ref[...]
        lam = lam_ref[...]
        r, ii, sp, a, m = _lru_gates(xc, wa_ref[...], ba_ref[...], wx_ref[...], bx_ref[...], lam)
        ge, dge = _gelu(gr_ref[...])
        drec = drec_ref[...]
        hh = h_ref[...]
        dgr_ref[...] = (drec * hh * dge).astype(BF16)
        dh = drec * ge
        rowi = lax.broadcasted_iota(jnp.int32, (tm, C), 0)
        c = jnp.where(rowi == tm - 1, jnp.broadcast_to(anext[0:1, :], (tm, C)), pltpu.roll(a, tm - 1, 0))
        anext[...] = a[0:8, :]
        r8 = rowi & 7
        gg = dh
        for d in (1, 2, 4):
            ok = r8 < 8 - d
            c_sh = jnp.where(ok, pltpu.roll(c, tm - d, 0), 1.0)
            g_sh = jnp.where(ok, pltpu.roll(gg, tm - d, 0), 0.0)
            gg = c * g_sh + gg
            c = c * c_sh
        c_s[...] = c
        g_s[...] = gg

        def step(k, gin):
            s = pl.multiple_of((t8 - 1 - k) * 8, 8)
            og = c_s[pl.ds(s, 8), :] * gin + g_s[pl.ds(s, 8), :]
            gout[pl.ds(s, 8), :] = og
            return jnp.broadcast_to(og[0:1, :], (8, C))

        gcarry[...] = lax.fori_loop(0, t8, step, gcarry[...])
        G = gout[...]
        hprev_row = jnp.where(j > 0, hp_ref[7:8, :], 0.0)
        hprev = jnp.where(rowi == 0, jnp.broadcast_to(hprev_row, (tm, C)), pltpu.roll(hh, 1, 0))
        da = G * hprev
        dm = G * ii * xc
        di = G * m * xc
        dxc = G * m * ii
        dla = da * a - dm * a * a / m
        dr = dla * (-LRU_C * sp)
        dsp = _colsum(dla * (-LRU_C * r))
        dlam = dsp * (-_sigmoid(-lam))
        dpr = dr * r * (1.0 - r)
        dpi = di * ii * (1.0 - ii)
        dxc = dxc + _mm_nt(dpr, wa_ref[...]) + _mm_nt(dpi, wx_ref[...])
        dwa_ref[...] += _mm_tn(xc, dpr)
        dwx_ref[...] += _mm_tn(xc, dpi)
        dvec_ref[0:1, :] += _colsum(dpr)
        dvec_ref[1:2, :] += _colsum(dpi)
        dvec_ref[2:3, :] += dlam
        dvec_ref[3:4, :] += _colsum(dxc)
        edge = ext[...]
        xr = xr_ref[...]
        dxr = cw_ref[3:4, :] * dxc
        dvec_ref[7:8, :] += _colsum(dxc * xr)
        for k in range(3):
            up = _shift_rows(dxc, k - 3, edge)
            dxr = dxr + cw_ref[k:k + 1, :] * up
            dvec_ref[4 + k:5 + k, :] += _colsum(up * xr)
        ext[...] = dxc[0:8, :]
        dxr_ref[...] = dxr.astype(BF16)

    rev = lambda i: nt - 1 - i
    prev8 = lambda i: jnp.maximum((nt - 1 - i) * t8 - 1, 0)
    vec = _full((1, C))
    return _launch(
        body, "rnn_bwd", (nt,),
        [_rows(tm, C, rev), _rows(tm, C, rev), _rows(tm, C, rev), _rows(8, C, prev8), _rows(tm, C, rev),
         _rows(tm, C, rev), _full((4, C)), _full((C, C)), vec, _full((C, C)), vec, vec],
        [_rows(tm, C, rev), _rows(tm, C, rev), _full((C, C)), _full((C, C)), _full((8, C))],
        [jax.ShapeDtypeStruct((T, C), BF16), jax.ShapeDtypeStruct((T, C), BF16),
         jax.ShapeDtypeStruct((C, C), F32), jax.ShapeDtypeStruct((C, C), F32), jax.ShapeDtypeStruct((8, C), F32)],
        [pltpu.VMEM((tm, C), F32), pltpu.VMEM((tm, C), F32), pltpu.VMEM((tm, C), F32),
         pltpu.VMEM((8, C), F32), pltpu.VMEM((8, C), F32), pltpu.VMEM((8, C), F32)],
        (drec, gr, h, h, xc, xr, cw, wa, ba, wx, bx, lam), exchange)


def _out_proj(att, rec, x, w_out, g1, b1):
    T = x.shape[0]
    tm = 512

    def body(att_ref, rec_ref, x_ref, w_ref, g1_ref, b1_ref, z_ref, h_ref):
        mix = _mm(att_ref[...], w_ref[0:512, :]) + _mm(rec_ref[...], w_ref[512:1024, :])
        z1 = ALPHA * x_ref[...] + mix
        z_ref[...] = z1
        h1, _, _ = _ln(z1, g1_ref[...], b1_ref[...])
        h_ref[...] = h1.astype(MXU_DTYPE).astype(BF16)

    return pl.pallas_call(
        body, name="out_proj", grid=(T // tm,),
        in_specs=[_rows(tm, 512), _rows(tm, 512), _rows(tm, D), _full((D, D)), _full((1, D)), _full((1, D))],
        out_specs=[_rows(tm, D), _rows(tm, D)],
        out_shape=[jax.ShapeDtypeStruct((T, D), F32), jax.ShapeDtypeStruct((T, D), BF16)],
        compiler_params=_params(),
    )(att, rec, x, w_out, g1, b1)


NC = D_FF // FF_CHUNK


def _ffn_up(h1b, w_up_t, fcw, fcb, exchange=None):
    T = h1b.shape[0]
    tm = min(1024, T)
    CW = FF_CHUNK

    def body(h_ref, wg_ref, wv_ref, fcw_ref, fcb_ref, gate_ref, ge_ref, vd_ref, act_ref, before):
        i = pl.program_id(1)

        @pl.when(i == 0)
        def _():
            before[...] = jnp.zeros((8, CW), F32)

        hb = h_ref[...]
        gate = _mm_nt(hb, wg_ref[...])
        val = _mm_nt(hb, wv_ref[...])
        gate_ref[...] = gate.astype(BF16)
        edge = before[...]
        gc = (fcb_ref[...] + fcw_ref[0:1, :] * _shift_rows(gate, 2, edge) + fcw_ref[1:2, :] * _shift_rows(gate, 1, edge)
              + fcw_ref[2:3, :] * gate)
        before[...] = gate[tm - 8:tm, :]
        ge, dge = _gelu(gc)
        ge_ref[...] = ge.astype(BF16)
        vd_ref[...] = (val * dge).astype(BF16)
        act_ref[...] = (ge * val).astype(BF16)

    chunk = pl.BlockSpec((None, tm, CW), lambda c, i: (c, i, 0))
    return _launch(
        body, "ffn_up", (NC, T // tm),
        [pl.BlockSpec((tm, D), lambda c, i: (i, 0)), pl.BlockSpec((CW, D), lambda c, i: (c, 0)),
         pl.BlockSpec((CW, D), lambda c, i: (NC + c, 0)), pl.BlockSpec((None, 3, CW), lambda c, i: (c, 0, 0)),
         pl.BlockSpec((None, 1, CW), lambda c, i: (c, 0, 0))],
        [chunk] * 4, [jax.ShapeDtypeStruct((NC, T, CW), BF16)] * 4, [pltpu.VMEM((8, CW), F32)],
        (h1b, w_up_t, w_up_t, fcw, fcb), exchange)


def _ffn_down(act, z1, p, tgt, w_down, w_g, w_p_t, g1, b1, g2, b2, bg):
    T = z1.shape[0]
    tm = 512

    def body(act_ref, z_ref, p_ref, t_ref, wdn_hbm, wg_hbm, wp_hbm, g1_ref, b1_ref, g2_ref, b2_ref, bg_ref,
             dz2_ref, dz2b_ref, dpre_ref, dpp_ref, vec_ref, wdn, wg, wp):
        @pl.when(pl.program_id(0) == 0)
        def _():
            pltpu.sync_copy(wdn_hbm, wdn)
            pltpu.sync_copy(wg_hbm, wg)
            pltpu.sync_copy(wp_hbm, wp)
            vec_ref[...] = jnp.zeros_like(vec_ref)

        g2v = g2_ref[...]
        h1, _, _ = _ln(z_ref[...], g1_ref[...], b1_ref[...])
        h1b = h1.astype(MXU_DTYPE)
        ffn = _mm(act_ref[0], wdn[0:FF_CHUNK, :])
        for c in range(1, NC):
            ffn = ffn + _mm(act_ref[c], wdn[c * FF_CHUNK:(c + 1) * FF_CHUNK, :])
        sg = _sigmoid(_mm(h1b, wg[...]) + bg_ref[...])
        pp = _mm_nt(p_ref[...], wp[...])
        z2 = ALPHA * h1 + ffn + sg * pp
        y, xh2, rstd2 = _ln(z2, g2v, b2_ref[...])
        diff = y - t_ref[...]
        dy = diff * (1.0 / D)
        dz2 = _ln_bwd(dy, xh2, rstd2, g2v)
        dpre = dz2 * pp * sg * (1.0 - sg)
        dz2_ref[...] = dz2
        dz2b_ref[...] = dz2.astype(BF16)
        dpre_ref[...] = dpre.astype(BF16)
        dpp_ref[...] = (dz2 * sg).astype(BF16)
        loss = 0.5 * jnp.sum(jnp.sum(diff * diff, axis=1, keepdims=True), axis=0, keepdims=True) * (1.0 / D)
        vec_ref[0:1, :] += jnp.broadcast_to(loss, (1, D))
        vec_ref[1:2, :] += _colsum(dy * xh2)
        vec_ref[2:3, :] += _colsum(dy)
        vec_ref[3:4, :] += _colsum(dpre)

    anyspec = pl.BlockSpec(memory_space=pl.ANY)
    vec = _full((1, D))
    return pl.pallas_call(
        body, name="ffn_down", grid=(T // tm,),
        in_specs=[pl.BlockSpec((NC, tm, FF_CHUNK), lambda i: (0, i, 0)), _rows(tm, D), _rows(tm, PLE), _rows(tm, D),
                  anyspec, anyspec, anyspec] + [vec] * 5,
        out_specs=[_rows(tm, D)] * 4 + [_full((8, D))],
        out_shape=[jax.ShapeDtypeStruct((T, D), F32)] + [jax.ShapeDtypeStruct((T, D), BF16)] * 3
                  + [jax.ShapeDtypeStruct((8, D), F32)],
        scratch_shapes=[pltpu.VMEM((D_FF, D), MXU_DTYPE), pltpu.VMEM((D, D), MXU_DTYPE), pltpu.VMEM((D, PLE), MXU_DTYPE)],
        compiler_params=_params(),
    )(act, z1, p, tgt, w_down, w_g, w_p_t, g1, b1, g2, b2, bg)


def _ffn_bwd(dz2b, gate, ge, vd, w_down, fcw):
    T = dz2b.shape[0]
    tm = min(1024, T)
    CW = FF_CHUNK
    nt = T // tm

    def body(dz_ref, wdn_ref, gate_ref, ge_ref, vd_ref, fcw_ref, dup_ref, dfc_ref, after):
        i = pl.program_id(1)

        @pl.when(i == 0)
        def _():
            after[...] = jnp.zeros((8, CW), F32)
            dfc_ref[...] = jnp.zeros_like(dfc_ref)

        gate = gate_ref[...].astype(F32)
        dact = _mm_nt(dz_ref[...], wdn_ref[...])
        dgc = dact * vd_ref[...].astype(F32)
        edge = after[...]
        dgc1 = _shift_rows(dgc, -1, edge)
        dgc2 = _shift_rows(dgc, -2, edge)
        after[...] = dgc[0:8, :]
        dup_ref[0] = (fcw_ref[2:3, :] * dgc + fcw_ref[1:2, :] * dgc1 + fcw_ref[0:1, :] * dgc2).astype(BF16)
        dup_ref[1] = (dact * ge_ref[...].astype(F32)).astype(BF16)
        dfc_ref[0:1, :] += _colsum(dgc2 * gate)
        dfc_ref[1:2, :] += _colsum(dgc1 * gate)
        dfc_ref[2:3, :] += _colsum(dgc * gate)
        dfc_ref[3:4, :] += _colsum(dgc)

    rev = lambda c, i: (c, nt - 1 - i, 0)
    chunk = pl.BlockSpec((None, tm, CW), rev)
    return pl.pallas_call(
        body, name="ffn_bwd", grid=(NC, nt),
        in_specs=[pl.BlockSpec((tm, D), lambda c, i: (nt - 1 - i, 0)), pl.BlockSpec((CW, D), lambda c, i: (c, 0)),
                  chunk, chunk, chunk, pl.BlockSpec((None, 3, CW), lambda c, i: (c, 0, 0))],
        out_specs=[pl.BlockSpec((None, 2, tm, CW), lambda c, i: (c, 0, nt - 1 - i, 0)),
                   pl.BlockSpec((None, 8, CW), lambda c, i: (c, 0, 0))],
        out_shape=[jax.ShapeDtypeStruct((NC, 2, T, CW), BF16), jax.ShapeDtypeStruct((NC, 8, CW), F32)],
        scratch_shapes=[pltpu.VMEM((8, CW), F32)],
        compiler_params=_params(),
    )(dz2b, w_down, gate, ge, vd, fcw)


def _ffn_dh1(dup, dz2, dpre, z1, w_up_t, w_g, g1, b1):
    T = z1.shape[0]
    tm = 512

    def body(dup_ref, dz2_ref, dpre_ref, z_ref, wup_hbm, wg_hbm, g1_ref, b1_ref, dz1_ref, vec_ref, wup, wg):
        @pl.when(pl.program_id(0) == 0)
        def _():
            pltpu.sync_copy(wup_hbm, wup)
            pltpu.sync_copy(wg_hbm, wg)
            vec_ref[...] = jnp.zeros_like(vec_ref)

        g1v = g1_ref[...]
        _, xh1, rstd1 = _ln(z_ref[...], g1v, b1_ref[...])
        dh1 = ALPHA * dz2_ref[...] + _mm_nt(dpre_ref[...], wg[...])
        for c in range(NC):
            for s in range(2):
                r0 = s * D_FF + c * FF_CHUNK
                dh1 = dh1 + _mm(dup_ref[c, s], wup[r0:r0 + FF_CHUNK, :])
        dz1_ref[...] = _ln_bwd(dh1, xh1, rstd1, g1v)
        vec_ref[0:1, :] += _colsum(dh1 * xh1)
        vec_ref[1:2, :] += _colsum(dh1)

    anyspec = pl.BlockSpec(memory_space=pl.ANY)
    vec = _full((1, D))
    return pl.pallas_call(
        body, name="ffn_dh1", grid=(T // tm,),
        in_specs=[pl.BlockSpec((NC, 2, tm, FF_CHUNK), lambda i: (0, 0, i, 0)), _rows(tm, D), _rows(tm, D), _rows(tm, D),
                  anyspec, anyspec, vec, vec],
        out_specs=[_rows(tm, D), _full((8, D))],
        out_shape=[jax.ShapeDtypeStruct((T, D), F32), jax.ShapeDtypeStruct((8, D), F32)],
        scratch_shapes=[pltpu.VMEM((2 * D_FF, D), MXU_DTYPE), pltpu.VMEM((D, D), MXU_DTYPE)],
        compiler_params=_params(),
    )(dup, dz2, dpre, z1, w_up_t, w_g, g1, b1)


def _out_proj_bwd(dz1, w_out, exchange=None):
    T = dz1.shape[0]
    tm = 512

    def body(dz_ref, w_ref, datt_ref, drec_ref):
        dzb = dz_ref[...].astype(MXU_DTYPE)
        datt = _mm_nt(dzb, w_ref[0:512, :])
        for h in range(HEADS):
            datt_ref[h] = datt[:, h * 64:(h + 1) * 64].astype(BF16)
        drec_ref[...] = _mm_nt(dzb, w_ref[512:1024, :])

    return _launch(body, "out_proj_bwd", (T // tm,), [_rows(tm, D), _full((D, D))], [_heads(tm), _rows(tm, 512)],
                   [jax.ShapeDtypeStruct((HEADS, T, 64), BF16), jax.ShapeDtypeStruct((T, 512), F32)], [],
                   (dz1, w_out), exchange)


def _in_proj_bwd(dq, dkv, dxr, dgr, dz1, w_in_t, exchange=None):
    T = dz1.shape[0]
    tm = 512
    W = D_IN // 4

    def body(dq_ref, dkv_ref, dxr_ref, dgr_ref, dz_ref, w_ref, dx_ref, du_ref):
        dkv = dkv_ref[...]
        dx_ref[...] = (ALPHA * dz_ref[...] + _mm(dq_ref[...], w_ref[0:512, :]) + _mm(dkv, w_ref[512:768, :])
                       + _mm(dxr_ref[...], w_ref[768:1280, :]) + _mm(dgr_ref[...], w_ref[1280:1792, :]))
        dq, dxr, dgr = dq_ref[...].astype(F32), dxr_ref[...].astype(F32), dgr_ref[...].astype(F32)
        du_ref[0] = dq[:, 0:W].astype(BF16)
        du_ref[1, :, 0:64] = dq[:, W:512].astype(BF16)
        du_ref[1, :, 64:320] = dkv.astype(BF16)
        du_ref[1, :, 320:W] = dxr[:, 0:128].astype(BF16)
        du_ref[2, :, 0:384] = dxr[:, 128:512].astype(BF16)
        du_ref[2, :, 384:W] = dgr[:, 0:64].astype(BF16)
        du_ref[3] = dgr[:, 64:512].astype(BF16)

    return _launch(body, "in_proj_bwd", (T // tm,),
                   [_rows(tm, 512), _rows(tm, 256), _rows(tm, 512), _rows(tm, 512), _rows(tm, D), _full((D_IN, D))],
                   [_rows(tm, D), pl.BlockSpec((4, tm, W), lambda i: (0, i, 0))],
                   [jax.ShapeDtypeStruct((T, D), F32), jax.ShapeDtypeStruct((4, T, W), BF16)], [],
                   (dq, dkv, dxr, dgr, dz1, w_in_t), exchange)


def _accumulate_tn(a_ref, b_ref, o_ref):
    @pl.when(pl.program_id(1) == 0)
    def _():
        o_ref[...] = jnp.zeros_like(o_ref)

    o_ref[...] += _mm_tn(a_ref[...], b_ref[...])


def _weight_grad_cols(a, b, name, n_blocks, b_spec, out_shape, out_spec, exchange=None):
    T, M = a.shape
    bt = min(DW_TOKENS, T)
    return _launch(functools.partial(_accumulate_tn), name, (n_blocks, T // bt),
                   [pl.BlockSpec((bt, M), lambda m, k: (k, 0)), b_spec(bt)], [out_spec],
                   [jax.ShapeDtypeStruct(out_shape, F32)], [], (a, b), exchange)


def _weight_grad(a, b, bm, name):
    bt = min(DW_TOKENS // 2 if b.dtype == F32 else DW_TOKENS, b.shape[0])
    if a.ndim == 3:
        assert a.shape[2] == bm
        T, M = a.shape[1], a.shape[0] * bm
        a_spec = pl.BlockSpec((None, bt, bm), lambda m, k: (m, k, 0))
    else:
        T, M = a.shape
        a_spec = pl.BlockSpec((bt, bm), lambda m, k: (k, m))
    N = b.shape[1]
    nk = T // bt

    return pl.pallas_call(
        functools.partial(_accumulate_tn), name=name, grid=(M // bm, nk),
        in_specs=[a_spec, pl.BlockSpec((bt, N), lambda m, k: (k, 0))],
        out_specs=pl.BlockSpec((bm, N), lambda m, k: (m, 0)),
        out_shape=jax.ShapeDtypeStruct((M, N), F32),
        compiler_params=_params(),
    )(a, b)


def _adamw(w, g, m, v, name):
    R, C = w.shape
    tr = R // 8 if R % 64 == 0 else R
    c1 = 1.0 / (1.0 - ADAM_B1 ** ADAM_STEP)
    c2 = 1.0 / (1.0 - ADAM_B2 ** ADAM_STEP)

    def body(w_ref, g_ref, m_ref, v_ref, d_ref, nm_ref, nv_ref):
        g = g_ref[...]
        nm = ADAM_B1 * m_ref[...] + (1.0 - ADAM_B1) * g
        nv = ADAM_B2 * v_ref[...] + (1.0 - ADAM_B2) * g * g
        nm_ref[...] = nm
        nv_ref[...] = nv
        d_ref[...] = -ADAM_LR * ((nm * c1) / (jnp.sqrt(nv * c2) + ADAM_EPS) + ADAM_WD * w_ref[...])

    spec = pl.BlockSpec((tr, C), lambda i: (i, 0))
    return pl.pallas_call(
        body, name=name, grid=(R // tr,),
        in_specs=[spec] * 4, out_specs=[spec] * 3,
        out_shape=[jax.ShapeDtypeStruct((R, C), F32)] * 3,
        compiler_params=_params(),
    )(w, g, m, v)


def _adamw_halves(ws, mines, sibs, ms, vs, c, name, exchange=None):
    n, nb = len(ws), 4
    c1 = 1.0 / (1.0 - ADAM_B1 ** ADAM_STEP)
    c2 = 1.0 / (1.0 - ADAM_B2 ** ADAM_STEP)

    def body(c_ref, *refs):
        own = (pl.program_id(0) // nb) == c_ref[0]
        for i in range(n):
            w_ref, a_ref, b_ref, m_ref, v_ref = refs[5 * i:5 * i + 5]
            g_ref, d_ref, nm_ref, nv_ref = refs[5 * n + 4 * i:5 * n + 4 * i + 4]
            g = jnp.where(own, a_ref[...], b_ref[...])
            nm = ADAM_B1 * m_ref[...] + (1.0 - ADAM_B1) * g
            nv = ADAM_B2 * v_ref[...] + (1.0 - ADAM_B2) * g * g
            g_ref[...] = g
            nm_ref[...] = nm
            nv_ref[...] = nv
            d_ref[...] = -ADAM_LR * ((nm * c1) / (jnp.sqrt(nv * c2) + ADAM_EPS) + ADAM_WD * w_ref[...])

    in_specs, out_specs, out_shape, args = [], [], [], []
    for w, a, b, m, v in zip(ws, mines, sibs, ms, vs):
        R, C = w.shape
        tr = R // (2 * nb)
        assert tr % 8 == 0 and a.shape == (R // 2, C)
        full = pl.BlockSpec((tr, C), lambda i, c_ref: (i, 0))
        half = pl.BlockSpec((tr, C), lambda i, c_ref: (i % nb, 0))
        in_specs += [full, half, half, full, full]
        out_specs += [full] * 4
        out_shape += [jax.ShapeDtypeStruct((R, C), F32)] * 4
        args += [w, a, b, m, v]
    out = _launch(body, name, (2 * nb,), in_specs, out_specs, out_shape, [], (c, *args), exchange, prefetch=1)
    return [tuple(out[4 * i:4 * i + 4]) for i in range(n)], list(out[4 * n:])


def _add4(fs, name):
    n = len(fs)

    def body(*refs):
        for a_ref, o_ref in zip(refs[:n], refs[n:]):
            o_ref[...] = ((a_ref[0].astype(F32) + a_ref[1].astype(F32)) + a_ref[2].astype(F32)) + a_ref[3].astype(F32)

    for f in fs:
        assert (f.shape[1] // 2) % 16 == 0
    return pl.pallas_call(
        body, name=name, grid=(2,),
        in_specs=[pl.BlockSpec((4, f.shape[1] // 2, f.shape[2]), lambda i: (0, i, 0)) for f in fs],
        out_specs=[pl.BlockSpec((f.shape[1] // 2, f.shape[2]), lambda i: (i, 0)) for f in fs],
        out_shape=[jax.ShapeDtypeStruct(f.shape[1:], F32) for f in fs], compiler_params=_params())(*fs)


def _gather_first(wsrc, cpack):
    def body(w_ref, c_ref, gw_ref, gc_ref, send_sems, recv_sems, local_sem, csend, crecv, clocal):
        x, y, c = _pos()
        me = 2 * x + y
        chips = _other_chips(x, y)
        start, forward, finish = _gather_steps(w_ref, gw_ref, send_sems, recv_sems, local_sem)
        start()
        loc = pltpu.make_async_copy(c_ref, gc_ref.at[me], clocal)
        loc.start()

        def conv_copy(k, slot):
            px, py = chips[k]
            return pltpu.make_async_remote_copy(src_ref=c_ref, dst_ref=gc_ref.at[slot], send_sem=csend.at[k],
                                                recv_sem=crecv.at[k], device_id=(px, py, c), device_id_type=MESH)

        for k in range(3):
            conv_copy(k, me).start()
        forward()
        finish()
        for k, (px, py) in enumerate(chips):
            conv_copy(k, 2 * px + py).wait_recv()
        for k in range(3):
            conv_copy(k, me).wait_send()
        loc.wait()

    anyspec = pl.BlockSpec(memory_space=pl.ANY)
    return pl.pallas_call(
        body, name="gather_first",
        in_specs=[anyspec, anyspec], out_specs=[anyspec, anyspec],
        out_shape=[jax.ShapeDtypeStruct((4,) + wsrc.shape, wsrc.dtype), jax.ShapeDtypeStruct((4,) + cpack.shape, cpack.dtype)],
        scratch_shapes=GATHER_SCRATCH + [pltpu.SemaphoreType.DMA((3,)), pltpu.SemaphoreType.DMA((3,)), pltpu.SemaphoreType.DMA],
        compiler_params=_params(has_side_effects=True),
    )(wsrc, cpack)


def _all_devices_exchange(s):
    def make(ins, outs, sems):
        s_ref, o_ref = ins[0], outs[0]
        send_sems, recv_sems, local_sem = sems
        x, y, c = _pos()
        me = 4 * x + 2 * y + c
        loc = pltpu.make_async_copy(s_ref, o_ref.at[me], local_sem)

        def copy(k, slot):
            peer = (x ^ (k >> 2), y ^ ((k >> 1) & 1), c ^ (k & 1))
            return pltpu.make_async_remote_copy(src_ref=s_ref, dst_ref=o_ref.at[slot], send_sem=send_sems.at[k - 1],
                                                recv_sem=recv_sems.at[k - 1], device_id=peer, device_id_type=MESH)

        def start():
            loc.start()
            for k in range(1, 8):
                copy(k, me).start()

        def finish():
            for k in range(1, 8):
                copy(k, 4 * (x ^ (k >> 2)) + 2 * (y ^ ((k >> 1) & 1)) + (c ^ (k & 1))).wait_recv()
            for k in range(1, 8):
                copy(k, me).wait_send()
            loc.wait()

        return start, lambda: None, finish

    return _Exchange([s], [jax.ShapeDtypeStruct((8,) + s.shape, s.dtype)],
                     [pltpu.SemaphoreType.DMA((7,)), pltpu.SemaphoreType.DMA((7,)), pltpu.SemaphoreType.DMA], make)


def _sum_devices(a):
    def body(a_ref, o_ref):
        acc = a_ref[0]
        for d in range(1, 8):
            acc = acc + a_ref[d]
        o_ref[...] = acc

    vm = pl.BlockSpec(memory_space=pltpu.VMEM)
    return pl.pallas_call(body, name="sum_devices", in_specs=[vm], out_specs=vm,
                          out_shape=jax.ShapeDtypeStruct(a.shape[1:], F32), compiler_params=_params())(a)


def _swap_exchange(gs):
    n = len(gs)

    def make(ins, outs, sems):
        x, y, c = _pos()
        cps = []
        for i in range(n):
            half = gs[i].shape[1] // 2
            rows = pl.ds(pl.multiple_of((1 - c) * half, 8), half)
            cps.append(pltpu.make_async_remote_copy(src_ref=ins[i].at[:, rows, :], dst_ref=outs[i], send_sem=sems[0].at[i],
                                                    recv_sem=sems[1].at[i], device_id=(x, y, 1 - c), device_id_type=MESH))

        def start():
            for cp in cps:
                cp.start()

        def finish():
            for cp in cps:
                cp.wait()

        return start, lambda: None, finish

    return _Exchange(gs, [jax.ShapeDtypeStruct((4, g.shape[1] // 2, g.shape[2]), g.dtype) for g in gs],
                     [pltpu.SemaphoreType.DMA((n,)), pltpu.SemaphoreType.DMA((n,))], make)


def _scatter_exchange(ss):
    n = len(ss)

    def make(ins, outs, sems):
        send_sems, recv_sems, local_sems = sems
        x, y, c = _pos()
        me = 2 * x + y
        chips = _other_chips(x, y)
        locs = [pltpu.make_async_copy(ins[i].at[me], outs[i].at[me], local_sems.at[i]) for i in range(n)]

        def copy(i, k, src_slot, dst_slot):
            px, py = chips[k]
            return pltpu.make_async_remote_copy(src_ref=ins[i].at[src_slot], dst_ref=outs[i].at[dst_slot],
                                                send_sem=send_sems.at[3 * i + k], recv_sem=recv_sems.at[3 * i + k],
                                                device_id=(px, py, c), device_id_type=MESH)

        def start():
            for i in range(n):
                locs[i].start()
                for k, (px, py) in enumerate(chips):
                    copy(i, k, 2 * px + py, me).start()

        def finish():
            for i in range(n):
                for k, (px, py) in enumerate(chips):
                    copy(i, k, me, 2 * px + py).wait_recv()
            for i in range(n):
                for k, (px, py) in enumerate(chips):
                    copy(i, k, 2 * px + py, me).wait_send()
                locs[i].wait()

        return start, lambda: None, finish

    return _Exchange(ss, [jax.ShapeDtypeStruct(s.shape, s.dtype) for s in ss],
                     [pltpu.SemaphoreType.DMA((3 * n,)), pltpu.SemaphoreType.DMA((3 * n,)), pltpu.SemaphoreType.DMA((n,))], make)


def _send_exchange(rs):
    n = len(rs)

    def make(ins, outs, sems):
        x, y, c = _pos()
        cps = [pltpu.make_async_remote_copy(src_ref=ins[i], dst_ref=outs[i], send_sem=sems[0].at[i], recv_sem=sems[1].at[i],
                                            device_id=(x, y, 1 - c), device_id_type=MESH) for i in range(n)]

        def start():
            for cp in cps:
                cp.start()

        def finish():
            for cp in cps:
                cp.wait()

        return start, lambda: None, finish

    return _Exchange(rs, [jax.ShapeDtypeStruct(r.shape, r.dtype) for r in rs],
                     [pltpu.SemaphoreType.DMA((n,)), pltpu.SemaphoreType.DMA((n,))], make)


def _run_exchange(ex, name):
    ei, eo = len(ex.args), len(ex.out_shape)

    def body(*refs):
        start, forward, finish = ex.make(refs[:ei], refs[ei:ei + eo], refs[ei + eo:])
        start()
        forward()
        finish()

    anyspec = pl.BlockSpec(memory_space=pl.ANY)
    return pl.pallas_call(body, name=name, in_specs=[anyspec] * ei, out_specs=[anyspec] * eo, out_shape=ex.out_shape,
                          scratch_shapes=ex.scratch, compiler_params=_params(has_side_effects=True))(*ex.args)


def _add_half(gs, rs, c, name):
    n = len(gs)

    def body(c_ref, *refs):
        for g_ref, r_ref, o_ref in zip(refs[:n], refs[n:2 * n], refs[2 * n:]):
            o_ref[...] = (g_ref[...] + r_ref[...]).astype(BF16)

    g_specs, r_specs, out_shape = [], [], []
    for g, r in zip(gs, rs):
        _, H, C = r.shape
        tr = H // 2
        assert tr % 16 == 0 and g.shape == (4, 2 * H, C)
        g_specs.append(pl.BlockSpec((1, tr, C), lambda j, i, c_ref: (j, c_ref[0] * 2 + i, 0)))
        r_specs.append(pl.BlockSpec((1, tr, C), lambda j, i, c_ref: (j, i, 0)))
        out_shape.append(jax.ShapeDtypeStruct((4, H, C), BF16))
    grid_spec = pltpu.PrefetchScalarGridSpec(num_scalar_prefetch=1, grid=(4, 2), in_specs=g_specs + r_specs, out_specs=r_specs)
    return pl.pallas_call(body, name=name, grid_spec=grid_spec, out_shape=out_shape, compiler_params=_params())(c, *gs, *rs)


def _block_diag(w):
    eye = jnp.eye(RNN_BLOCKS, dtype=w.dtype)
    return (eye[:, None, :, None] * w[:, :, None, :]).reshape(D_RNN, D_RNN)


def _diag_blocks(wd):
    d = wd.reshape(RNN_BLOCKS, 64, RNN_BLOCKS, 64)
    return jnp.stack([d[h, :, h, :] for h in range(RNN_BLOCKS)])


def _split_pack(a, first, last):
    out, base = {}, PACK_OFF[first]
    for i in range(first, last):
        s = a[:, PACK_OFF[i] - base:PACK_OFF[i + 1] - base]
        out[BIG_KEYS[i]] = s.reshape(4 * 256, 256) if BIG_KEYS[i] == "w_p_t" else s.reshape(-1, 1024)
    return out


def _layer_grads(x, p, tgt, gw, small, shard=None, core=None):
    row = lambda v: v.reshape(1, -1)
    wa = _block_diag(small["gate_a_w"]).astype(MXU_DTYPE)
    wx = _block_diag(small["gate_x_w"]).astype(MXU_DTYPE)
    sinks = small["attn_sinks"].reshape(1, HEADS)

    dist = shard is not None
    q, kv, xr, gr, xb = _in_proj(x, gw["w_in_t"])
    cut = PACK_OFF[1] + PACK_ROWS[1] // 2
    att, *ga = _attn_fwd(q, kv, sinks, _gather_exchange(shard[PACK_OFF[1]:cut]) if dist else None)
    xc, h, rec, *gb = _rnn_fwd(xr, gr, small["rnn_conv_w"], row(small["rnn_conv_b"]), wa, row(small["gate_a_b"]),
                               wx, row(small["gate_x_b"]), row(small["lru_lambda"]),
                               _gather_exchange(shard[cut:PACK_OFF[3]]) if dist else None)
    if dist:
        gw = {**gw, **_split_pack(jnp.concatenate([ga[0], gb[0]], axis=1), 1, 3)}
    g1, b1 = row(small["ln1_g"]), row(small["ln1_b"])
    fcw = small["ffn_conv_w"].reshape(3, NC, FF_CHUNK).transpose(1, 0, 2)
    fcb = small["ffn_conv_b"].reshape(NC, 1, FF_CHUNK)
    z1, h1b = _out_proj(att, rec, x, gw["w_out"], g1, b1)
    gate, ge, vd, act, *gc = _ffn_up(h1b, gw["w_up_t"], fcw, fcb,
                                     _gather_exchange(shard[PACK_OFF[3]:PACK_OFF[6]]) if dist else None)
    if dist:
        gw = {**gw, **_split_pack(gc[0], 3, 6)}
    dz2, dz2b, dpre, dpp, vec2 = _ffn_down(act, z1, p, tgt, gw["w_down"], gw["w_g"], gw["w_p_t"], g1, b1,
                                           row(small["ln2_g"]), row(small["ln2_b"]), row(small["ple_gate_b"]))
    dup, dfc = _ffn_bwd(dz2b, gate, ge, vd, gw["w_down"], fcw)
    dz1, vec1 = _ffn_dh1(dup, dz2, dpre, z1, gw["w_up_t"], gw["w_g"], g1, b1)
    per_chip = 2 * D_FF // 4 // FF_CHUNK
    big = {
        "w_ffn_up": _weight_grad_cols(
            h1b, dup.reshape(2 * NC, -1, FF_CHUNK), "dw_up", 2 * NC,
            lambda bt: pl.BlockSpec((None, bt, FF_CHUNK), lambda m, k: (m, k, 0)), (4, D, 2 * D_FF // 4),
            pl.BlockSpec((None, D, FF_CHUNK), lambda m, k: (2 * (m % 2) + (m // 2) // per_chip, 0, (m // 2) % per_chip)))[0],
        "w_ffn_down": _weight_grad(act, dz2b, 512, "dw_down").reshape(4, D_FF // 4, D),
        "ple_gate_w": _weight_grad(h1b, dpre, 512, "dw_gate").reshape(4, D // 4, D),
        "ple_proj": _weight_grad_cols(
            p.astype(BF16), dpp, "dw_proj", 4, lambda bt: pl.BlockSpec((bt, D // 4), lambda j, k: (k, j)),
            (4, PLE, D // 4), pl.BlockSpec((None, PLE, D // 4), lambda j, k: (j, 0, 0)))[0],
        "w_out": _weight_grad(jnp.concatenate([att, rec], axis=1), dz1, 512, "dw_out").reshape(4, D // 4, D),
    }
    reduced = None
    if dist:
        g_ffn = [big[k] for k in EARLY_WEIGHTS]
        ex = _swap_exchange(g_ffn)
    datt, drec, *got = _out_proj_bwd(dz1, gw["w_out"], ex if dist else None)
    if dist:
        sums = _add_half(g_ffn, got, core, "add_half_ffn")
        ex, ex2 = _scatter_exchange(sums[:1]), _scatter_exchange(sums[1:])
    dxr, dgr, dwa, dwx, dvec, *got = _rnn_bwd(drec, gr, h, xc, xr, small["rnn_conv_w"], wa, row(small["gate_a_b"]),
                                              wx, row(small["gate_x_b"]), row(small["lru_lambda"]), ex if dist else None)
    dq, dkv, dsinks, *got2 = _attn_bwd(q, kv, datt, sinks, ex2 if dist else None)
    if dist:
        mine = _add4(got + got2, "add_chips_ffn")
        big = {}
    sg = {
        "attn_sinks": dsinks[:, 0],
        "rnn_conv_w": dvec[4:8],
        "rnn_conv_b": dvec[3],
        "gate_a_w": _diag_blocks(dwa),
        "gate_a_b": dvec[0],
        "gate_x_w": _diag_blocks(dwx),
        "gate_x_b": dvec[1],
        "lru_lambda": dvec[2],
        "ln1_g": vec1[0],
        "ln1_b": vec1[1],
        "ffn_conv_w": dfc[:, 0:3].transpose(1, 0, 2).reshape(3, D_FF),
        "ffn_conv_b": dfc[:, 3].reshape(D_FF),
        "ple_gate_b": vec2[3],
        "ln2_g": vec2[1],
        "ln2_b": vec2[2],
    }
    loss = vec2[0, 0:1]
    grad_x, du = _in_proj_bwd(dq, dkv, dxr, dgr, dz1, gw["w_in_t"])
    ex = None
    if dist:
        ex = _join_exchanges(_send_exchange(mine), _all_devices_exchange(_pack_vecs([sg[k] for k in SMALL] + [loss])[0]))
    big["w_in"], *got = _weight_grad_cols(
        xb, du, "dw_in", 4, lambda bt: pl.BlockSpec((None, bt, D_IN // 4), lambda j, k: (j, k, 0)), (4, D, D_IN // 4),
        pl.BlockSpec((None, D, D_IN // 4), lambda j, k: (j, 0, 0)), ex)
    if dist:
        reduced = (mine, got[:len(mine)])
    return grad_x, big, sg, loss, reduced, got[-1:]


BIG = ("w_in", "w_ffn_up", "w_out", "w_ffn_down", "ple_gate_w", "ple_proj")
BIG_KEYS = ("w_in_t", "w_up_t", "w_out", "w_down", "w_g", "w_p_t")
BIG_T = (True, True, False, False, False, True)
EARLY_WEIGHTS = ("w_ffn_up", "w_ffn_down", "ple_gate_w", "ple_proj", "w_out")
LATE_WEIGHTS = ("w_in",)
SMALL = ("attn_sinks", "rnn_conv_w", "rnn_conv_b", "gate_a_w", "gate_a_b", "gate_x_w", "gate_x_b", "lru_lambda",
         "ln1_g", "ln1_b", "ffn_conv_w", "ffn_conv_b", "ple_gate_b", "ln2_g", "ln2_b")
SHARDED_SMALL = ("rnn_conv_w", "ffn_conv_w")
WEIGHTS = ("w_in", "attn_sinks", "rnn_conv_w", "rnn_conv_b", "gate_a_w", "gate_a_b", "gate_x_w", "gate_x_b",
           "lru_lambda", "w_out", "ln1_g", "ln1_b", "w_ffn_up", "ffn_conv_w", "ffn_conv_b", "w_ffn_down",
           "ple_gate_w", "ple_gate_b", "ple_proj", "ln2_g", "ln2_b")


def _pack_big(d, first=0, last=6):
    parts = []
    for name, t in zip(BIG[first:last], BIG_T[first:last]):
        a = d[name]
        a = a.T if t else a
        parts.append(a.reshape(-1, 1024))
    return jnp.concatenate(parts, axis=0)


def _pack_vecs(items):
    parts, offs, n = [], [], 0
    for a in items:
        f = a.reshape(-1).astype(F32)
        pad = (-f.shape[0]) % 128
        parts.append(jnp.pad(f, (0, pad)))
        offs.append(n)
        n += (f.shape[0] + pad) // 128
    padr = (-n) % 8
    if padr:
        parts.append(jnp.zeros((padr * 128,), F32))
    return jnp.concatenate(parts).reshape(-1, 128), offs


def _unpack_vecs(a, offs, shapes):
    flat = a.reshape(-1)
    out = []
    for o, s in zip(offs, shapes):
        n = 1
        for d in s:
            n *= d
        out.append(flat[o * 128:o * 128 + n].reshape(s))
    return out


def kernel(x, p, w_in, attn_sinks, rnn_conv_w, rnn_conv_b, gate_a_w, gate_a_b, gate_x_w, gate_x_b, lru_lambda, w_out, ln1_g, ln1_b, w_ffn_up, ffn_conv_w, ffn_conv_b, w_ffn_down, ple_gate_w, ple_gate_b, ple_proj, ln2_g, ln2_b, loss_target, m_w_in, m_attn_sinks, m_rnn_conv_w, m_rnn_conv_b, m_gate_a_w, m_gate_a_b, m_gate_x_w, m_gate_x_b, m_lru_lambda, m_w_out, m_ln1_g, m_ln1_b, m_w_ffn_up, m_ffn_conv_w, m_ffn_conv_b, m_w_ffn_down, m_ple_gate_w, m_ple_gate_b, m_ple_proj, m_ln2_g, m_ln2_b, v_w_in, v_attn_sinks, v_rnn_conv_w, v_rnn_conv_b, v_gate_a_w, v_gate_a_b, v_gate_x_w, v_gate_x_b, v_lru_lambda, v_w_out, v_ln1_g, v_ln1_b, v_w_ffn_up, v_ffn_conv_w, v_ffn_conv_b, v_w_ffn_down, v_ple_gate_w, v_ple_gate_b, v_ple_proj, v_ln2_g, v_ln2_b):
    w = dict(w_in=w_in, attn_sinks=attn_sinks, rnn_conv_w=rnn_conv_w, rnn_conv_b=rnn_conv_b, gate_a_w=gate_a_w,
             gate_a_b=gate_a_b, gate_x_w=gate_x_w, gate_x_b=gate_x_b, lru_lambda=lru_lambda, w_out=w_out, ln1_g=ln1_g,
             ln1_b=ln1_b, w_ffn_up=w_ffn_up, ffn_conv_w=ffn_conv_w, ffn_conv_b=ffn_conv_b, w_ffn_down=w_ffn_down,
             ple_gate_w=ple_gate_w, ple_gate_b=ple_gate_b, ple_proj=ple_proj, ln2_g=ln2_g, ln2_b=ln2_b)
    m = dict(w_in=m_w_in, attn_sinks=m_attn_sinks, rnn_conv_w=m_rnn_conv_w, rnn_conv_b=m_rnn_conv_b, gate_a_w=m_gate_a_w,
             gate_a_b=m_gate_a_b, gate_x_w=m_gate_x_w, gate_x_b=m_gate_x_b, lru_lambda=m_lru_lambda, w_out=m_w_out,
             ln1_g=m_ln1_g, ln1_b=m_ln1_b, w_ffn_up=m_w_ffn_up, ffn_conv_w=m_ffn_conv_w, ffn_conv_b=m_ffn_conv_b,
             w_ffn_down=m_w_ffn_down, ple_gate_w=m_ple_gate_w, ple_gate_b=m_ple_gate_b, ple_proj=m_ple_proj,
             ln2_g=m_ln2_g, ln2_b=m_ln2_b)
    v = dict(w_in=v_w_in, attn_sinks=v_attn_sinks, rnn_conv_w=v_rnn_conv_w, rnn_conv_b=v_rnn_conv_b, gate_a_w=v_gate_a_w,
             gate_a_b=v_gate_a_b, gate_x_w=v_gate_x_w, gate_x_b=v_gate_x_b, lru_lambda=v_lru_lambda, w_out=v_w_out,
             ln1_g=v_ln1_g, ln1_b=v_ln1_b, w_ffn_up=v_w_ffn_up, ffn_conv_w=v_ffn_conv_w, ffn_conv_b=v_ffn_conv_b,
             w_ffn_down=v_w_ffn_down, ple_gate_w=v_ple_gate_w, ple_gate_b=v_ple_gate_b, ple_proj=v_ple_proj,
             ln2_g=v_ln2_g, ln2_b=v_ln2_b)
    w, m, v = ({k: a[0] for k, a in d.items()} for d in (w, m, v))
    chip = 2 * lax.axis_index("x") + lax.axis_index("y")
    core = lax.axis_index("c")

    wpack = _pack_big(w)
    cpack, _ = _pack_vecs([w["rnn_conv_w"], w["ffn_conv_w"]])
    shard = wpack.astype(MXU_DTYPE)
    g_in, gcp = _gather_first(shard[PACK_OFF[0]:PACK_OFF[1]], cpack)
    gw = _split_pack(g_in, 0, 1)
    small = {k: w[k] for k in SMALL}
    small["rnn_conv_w"] = gcp[:, 0:4].reshape(4, 4, 128).transpose(1, 0, 2).reshape(4, 512)
    small["ffn_conv_w"] = gcp[:, 4:22].reshape(4, 3, 768).transpose(1, 0, 2).reshape(3, 3072)

    core1 = core.reshape(1).astype(jnp.int32)
    grad_x, big, sg, loss, ffn_halves, small_all = _layer_grads(x[0], p[0, 0], loss_target[0], gw, small, shard, core1)

    shapes = [sg[k].shape for k in SMALL] + [(1,)]
    _, offs = _pack_vecs([jnp.zeros(s, F32) for s in shapes])
    red = dict(zip(SMALL + ("loss",), _unpack_vecs(_sum_devices(small_all[0]), offs, shapes)))
    red["rnn_conv_w"] = lax.dynamic_slice_in_dim(red["rnn_conv_w"], chip * 128, 128, axis=1)
    red["ffn_conv_w"] = lax.dynamic_slice_in_dim(red["ffn_conv_w"], chip * 768, 768, axis=1)

    g_late = [big[k] for k in LATE_WEIGHTS]
    sib = _run_exchange(_swap_exchange(g_late), "swap_late")
    from_chips = _run_exchange(_scatter_exchange(_add_half(g_late, sib, core1, "add_half_late")), "scatter_late")
    late_mine = _add4(from_chips, "add_chips_late")
    late_other = _run_exchange(_send_exchange(late_mine), "send_late")

    def adamw(names, mine, other, name):
        out, _ = _adamw_halves([w[k] for k in names], mine, other, [m[k] for k in names], [v[k] for k in names],
                               core1, name)
        return dict(zip(names, out))

    big_out = {**adamw(LATE_WEIGHTS, late_mine, late_other, "adamw_late"), **adamw(EARLY_WEIGHTS, *ffn_halves, "adamw_early")}
    wsm, offs2 = _pack_vecs([w[k] for k in SMALL])
    gsm, _ = _pack_vecs([red[k] for k in SMALL])
    msm, _ = _pack_vecs([m[k] for k in SMALL])
    vsm, _ = _pack_vecs([v[k] for k in SMALL])
    dsm, nmsm, nvsm = _adamw(wsm, gsm, msm, vsm, "adamw_small")
    shapes2 = [w[k].shape for k in SMALL]

    def named(n, smallp):
        d = {k: out[n][None] for k, out in big_out.items()}
        d.update({k: a[None] for k, a in zip(SMALL, _unpack_vecs(smallp, offs2, shapes2))})
        return [d[k] for k in WEIGHTS]

    return (red["loss"].reshape(()), grad_x[None], *named(0, gsm), *named(1, dsm), *named(2, nmsm), *named(3, nvsm))
```

```python
import functools

import jax
import jax.numpy as jnp
from jax import lax
from jax.experimental import pallas as pl
from jax.experimental.pallas import tpu as pltpu

F32 = jnp.float32
BF16 = jnp.bfloat16
MXU_DTYPE = jnp.bfloat16

D = 1024
D_ATT = 512
D_KV = 128
D_RNN = 512
D_IN = 1792
D_FF = 3072
FF_CHUNK = 512
PLE = 256
HEADS = 8
HEAD_DIM = 64
BLK = 128
ATTN_BLOCKS = 4
DW_TOKENS = 4096
RNN_BLOCKS = 8
LN_EPS = 1e-5
LRU_C = 8.0
ALPHA = float(2.0 ** 0.25)
SCALE = HEAD_DIM ** -0.5
NEG = -1e30

ADAM_LR = 0.001
ADAM_B1 = 0.9
ADAM_B2 = 0.999
ADAM_EPS = 1e-08
ADAM_WD = 0.01
ADAM_STEP = 10

VMEM_LIMIT_BYTES = 56 * 1024 * 1024
MESH = pl.DeviceIdType.MESH

PACK_ROWS = (448, 1536, 256, 768, 256, 64)
PACK_OFF = tuple(sum(PACK_ROWS[:i]) for i in range(len(PACK_ROWS) + 1))
PACK_TOTAL = PACK_OFF[-1]


def _params(**kw):
    return pltpu.CompilerParams(vmem_limit_bytes=VMEM_LIMIT_BYTES, **kw)


def _mm(a, b):
    return jnp.dot(a.astype(MXU_DTYPE), b.astype(MXU_DTYPE), preferred_element_type=F32)


def _mm_nt(a, b):
    return lax.dot_general(a.astype(MXU_DTYPE), b.astype(MXU_DTYPE), (((1,), (1,)), ((), ())),
                           preferred_element_type=F32)


def _mm_tn(a, b):
    return lax.dot_general(a.astype(MXU_DTYPE), b.astype(MXU_DTYPE), (((0,), (0,)), ((), ())),
                           preferred_element_type=F32)


def _sigmoid(x):
    return 0.5 + 0.5 * jnp.tanh(0.5 * x)


def _gelu(x):
    c = 0.7978845608028654
    k = 0.044715
    x2 = x * x
    t = jnp.tanh(x * (c + (c * k) * x2))
    h = 0.5 * (1.0 + t)
    return x * h, h * (1.0 + (x * (1.0 - t)) * (c + (3.0 * c * k) * x2))


def _shift_rows(x, s, edge8):
    R = x.shape[0]
    row8 = lax.broadcasted_iota(jnp.int32, (8, x.shape[1]), 0)
    if s > 0:
        rolled = pltpu.roll(x, s, 0)
        first = jnp.where(row8 < s, pltpu.roll(edge8, s, 0), rolled[0:8])
        return jnp.concatenate([first, rolled[8:]], axis=0)
    k = -s
    rolled = pltpu.roll(x, R - k, 0)
    last = jnp.where(row8 >= 8 - k, pltpu.roll(edge8, 8 - k, 0), rolled[R - 8:])
    return jnp.concatenate([rolled[:R - 8], last], axis=0)


def _softplus(x):
    return jnp.maximum(x, 0.0) + jnp.log(1.0 + jnp.exp(-jnp.abs(x)))


def _ln(z, g, b):
    mu = jnp.mean(z, axis=-1, keepdims=True)
    zc = z - mu
    var = jnp.mean(zc * zc, axis=-1, keepdims=True)
    rstd = lax.rsqrt(var + LN_EPS)
    xhat = zc * rstd
    return xhat * g + b, xhat, rstd


def _ln_bwd(dy, xhat, rstd, g):
    dxh = dy * g
    m1 = jnp.mean(dxh, axis=-1, keepdims=True)
    m2 = jnp.mean(dxh * xhat, axis=-1, keepdims=True)
    return rstd * (dxh - m1 - xhat * m2)


def _colsum(x):
    return jnp.sum(x, axis=0, keepdims=True)


def _full(shape):
    nd = len(shape)
    return pl.BlockSpec(shape, lambda *_: (0,) * nd)


def _rows(tm, cols, fn=None):
    if fn is None:
        return pl.BlockSpec((tm, cols), lambda i: (i, 0))
    return pl.BlockSpec((tm, cols), lambda i: (fn(i), 0))


def _heads(tm):
    return pl.BlockSpec((HEADS, tm, HEAD_DIM), lambda i: (0, i, 0))


def _in_proj(x, w_in_t):
    T = x.shape[0]
    tm = min(1024, T)

    def body(x_ref, w_ref, q_ref, kv_ref, xr_ref, gr_ref, xb_ref):
        xb = x_ref[...].astype(MXU_DTYPE)
        xb_ref[...] = xb.astype(BF16)
        q = _mm_nt(xb, w_ref[0:512, :])
        for h in range(HEADS):
            q_ref[h] = q[:, h * 64:(h + 1) * 64].astype(BF16)
        kv_ref[...] = _mm_nt(xb, w_ref[512:768, :]).astype(BF16)
        xr_ref[...] = _mm_nt(xb, w_ref[768:1280, :])
        gr_ref[...] = _mm_nt(xb, w_ref[1280:1792, :])

    return pl.pallas_call(
        body, name="in_proj", grid=(T // tm,),
        in_specs=[_rows(tm, D), _full((D_IN, D))],
        out_specs=[_heads(tm), _rows(tm, 256), _rows(tm, 512), _rows(tm, 512), _rows(tm, D)],
        out_shape=[jax.ShapeDtypeStruct((HEADS, T, 64), BF16), jax.ShapeDtypeStruct((T, 256), BF16),
                   jax.ShapeDtypeStruct((T, 512), F32), jax.ShapeDtypeStruct((T, 512), F32),
                   jax.ShapeDtypeStruct((T, D), BF16)],
        compiler_params=_params(),
    )(x, w_in_t)


def _attn_band(kv_ref, i):
    cur = pl.multiple_of(i * BLK, BLK)
    prev = pl.multiple_of(jnp.maximum(i - 1, 0) * BLK, BLK)
    band = jnp.concatenate([kv_ref[pl.ds(prev, BLK), :], kv_ref[pl.ds(cur, BLK), :]], axis=0)
    key = lax.broadcasted_iota(jnp.int32, (2 * BLK, 4 * BLK), 0)
    qry = lax.broadcasted_iota(jnp.int32, (2 * BLK, 4 * BLK), 1) & (BLK - 1)
    in_prev = jnp.logical_and(jnp.logical_and(key < BLK, key > qry), i > 0)
    mask = jnp.logical_or(in_prev, jnp.logical_and(key >= BLK, key - BLK <= qry))
    return band, mask, cur, prev


def _attn_scores(band, mask, qs, s_ref, g):
    st = jnp.where(mask, _mm_nt(band[:, g * 64:(g + 1) * 64], qs) * SCALE, NEG)
    lane = lax.broadcasted_iota(jnp.int32, (1, 4 * BLK), 1)
    sv = jnp.where(lane < BLK, s_ref[0, 4 * g],
                   jnp.where(lane < 2 * BLK, s_ref[0, 4 * g + 1], jnp.where(lane < 3 * BLK, s_ref[0, 4 * g + 2], s_ref[0, 4 * g + 3])))
    m = jnp.maximum(jnp.max(st, axis=0, keepdims=True), sv)
    p = jnp.exp(st - m)
    ps = jnp.exp(sv - m)
    return p, ps, jnp.sum(p, axis=0, keepdims=True) + ps


def _pos():
    return lax.axis_index("x"), lax.axis_index("y"), lax.axis_index("c")


def _other_chips(x, y):
    return [(1 - x, y), (x, 1 - y), (1 - x, 1 - y)]


def _gather_steps(w_ref, gw_ref, send_sems, recv_sems, local_sem):
    x, y, c = _pos()
    me = 2 * x + y
    chips = _other_chips(x, y)
    half = w_ref.shape[0] // 2
    mine = pl.ds(pl.multiple_of(c * half, 16), half)
    theirs = pl.ds(pl.multiple_of((1 - c) * half, 16), half)
    loc = pltpu.make_async_copy(w_ref, gw_ref.at[me], local_sem)

    def copy(k, src, dst, to):
        return pltpu.make_async_remote_copy(src_ref=src, dst_ref=dst, send_sem=send_sems.at[k], recv_sem=recv_sems.at[k],
                                            device_id=to, device_id_type=MESH)

    def out(k):
        px, py = chips[k]
        return copy(k, w_ref.at[mine], gw_ref.at[me, mine], (px, py, c))

    def fwd(k, rows):
        px, py = chips[k]
        return copy(3 + k, gw_ref.at[2 * px + py, rows], gw_ref.at[2 * px + py, rows], (x, y, 1 - c))

    def start():
        loc.start()
        for k in range(3):
            out(k).start()

    def forward():
        for k in range(3):
            px, py = chips[k]
            copy(k, w_ref.at[mine], gw_ref.at[2 * px + py, mine], (px, py, c)).wait_recv()
            fwd(k, mine).start()

    def finish():
        for k in range(3):
            fwd(k, theirs).wait_recv()
        for k in range(3):
            out(k).wait_send()
            fwd(k, mine).wait_send()
        loc.wait()

    return start, forward, finish


GATHER_SCRATCH = [pltpu.SemaphoreType.DMA((6,)), pltpu.SemaphoreType.DMA((6,)), pltpu.SemaphoreType.DMA]


class _Exchange:
    def __init__(self, args, out_shape, scratch, make):
        self.args, self.out_shape, self.scratch, self.make = list(args), list(out_shape), list(scratch), make


def _join_exchanges(a, b):
    na, nao, nas = len(a.args), len(a.out_shape), len(a.scratch)

    def make(ins, outs, sems):
        steps_a = a.make(ins[:na], outs[:nao], sems[:nas])
        steps_b = b.make(ins[na:], outs[nao:], sems[nas:])

        def both(f, g):
            def run():
                f()
                g()
            return run

        return tuple(both(f, g) for f, g in zip(steps_a, steps_b))

    return _Exchange(a.args + b.args, a.out_shape + b.out_shape, a.scratch + b.scratch, make)


def _gather_exchange(wsrc):
    return _Exchange([wsrc], [jax.ShapeDtypeStruct((4,) + wsrc.shape, wsrc.dtype)], GATHER_SCRATCH,
                     lambda ins, outs, sems: _gather_steps(ins[0], outs[0], *sems))


def _launch(body, name, grid, in_specs, out_specs, out_shape, scratch, args, exchange=None, prefetch=0):
    def call(fn, fn_name, ins, outs, shapes, scr, operands, effects):
        spec = pltpu.PrefetchScalarGridSpec(num_scalar_prefetch=prefetch, grid=grid, in_specs=ins, out_specs=outs,
                                            scratch_shapes=scr)
        return pl.pallas_call(fn, name=fn_name, grid_spec=spec, out_shape=shapes,
                              compiler_params=_params(has_side_effects=effects))(*operands)

    if exchange is None:
        return call(body, name, list(in_specs), list(out_specs), list(out_shape), list(scratch), args, False)
    n_in, n_out, ei, eo, ns = len(in_specs), len(out_specs), len(exchange.args), len(exchange.out_shape), len(exchange.scratch)
    nsteps = 1
    for g in grid:
        nsteps *= g

    def wrapped(*refs):
        scalars, refs = refs[:prefetch], refs[prefetch:]
        ins, xin = refs[:n_in], refs[n_in:n_in + ei]
        outs, xout = refs[n_in + ei:n_in + ei + n_out], refs[n_in + ei + n_out:n_in + ei + n_out + eo]
        rest = refs[n_in + ei + n_out + eo:]
        own, sems = rest[:len(rest) - ns], rest[len(rest) - ns:]
        start, forward, finish = exchange.make(xin, xout, sems)
        i = pl.program_id(0)
        for d in range(1, len(grid)):
            i = i * grid[d] + pl.program_id(d)
        pl.when(i == 0)(start)
        body(*scalars, *ins, *outs, *own)
        pl.when(i == max(nsteps - 3, 0))(forward)
        pl.when(i == nsteps - 1)(finish)

    anyspec = pl.BlockSpec(memory_space=pl.ANY)
    return call(wrapped, name + "_x", list(in_specs) + [anyspec] * ei, list(out_specs) + [anyspec] * eo,
                list(out_shape) + exchange.out_shape, list(scratch) + exchange.scratch, (*args, *exchange.args), True)


def _attn_fwd(q, kv, sinks, exchange=None):
    T = kv.shape[0]

    def body(q_ref, kv_ref, s_ref, o_ref):
        for b in range(ATTN_BLOCKS):
            rows = slice(b * BLK, (b + 1) * BLK)
            band, mask, _, _ = _attn_band(kv_ref, ATTN_BLOCKS * pl.program_id(0) + b)
            for g in range(2):
                qs = q_ref[4 * g:4 * g + 4, rows, :].reshape(4 * BLK, HEAD_DIM)
                p, _, den = _attn_scores(band, mask, qs, s_ref, g)
                ot = _mm_tn(band[:, 128:256], p) * (1.0 / den)
                for hh in range(4):
                    o = ot[:, hh * BLK:(hh + 1) * BLK].T
                    o_ref[rows, (4 * g + hh) * 64:(4 * g + hh + 1) * 64] = o[:, g * 64:(g + 1) * 64].astype(BF16)

    tq = ATTN_BLOCKS * BLK
    return _launch(body, "attn_fwd", (T // tq,), [_heads(tq), _full((T, 256)), pl.BlockSpec(memory_space=pltpu.SMEM)],
                   [_rows(tq, 512)], [jax.ShapeDtypeStruct((T, 512), BF16)], [], (q, kv, sinks), exchange)


def _attn_bwd(q, kv, do, sinks, exchange=None):
    T = kv.shape[0]

    def body(q_ref, kv_ref, do_ref, s_ref, dq_ref, dkv_ref, ds_ref):
        @pl.when(pl.program_id(0) == 0)
        def _():
            ds_ref[...] = jnp.zeros_like(ds_ref)

        for b in range(ATTN_BLOCKS):
            rows = slice(b * BLK, (b + 1) * BLK)
            band, mask, cur, prev = _attn_band(kv_ref, ATTN_BLOCKS * pl.program_id(0) + b)
            for g in range(2):
                qs = q_ref[4 * g:4 * g + 4, rows, :].reshape(4 * BLK, HEAD_DIM)
                dos = do_ref[4 * g:4 * g + 4, rows, :].reshape(4 * BLK, HEAD_DIM)
                p, ps, den = _attn_scores(band, mask, qs, s_ref, g)
                inv = 1.0 / den
                p = p * inv
                dpt = _mm_nt(band[:, 128 + g * 64:192 + g * 64], dos)
                delta = jnp.sum(p * dpt, axis=0, keepdims=True)
                dst = p * (dpt - delta)
                dsv = -(ps * inv) * delta
                for hh in range(4):
                    dsink = jnp.sum(dsv[:, hh * BLK:(hh + 1) * BLK], axis=1, keepdims=True)
                    ds_ref[4 * g + hh:4 * g + hh + 1, :] += jnp.broadcast_to(dsink, (1, 128))
                dqt = _mm_tn(band[:, 0:128], dst) * SCALE
                for hh in range(4):
                    dqh = dqt[:, hh * BLK:(hh + 1) * BLK].T
                    dq_ref[rows, (4 * g + hh) * 64:(4 * g + hh + 1) * 64] = dqh[:, g * 64:(g + 1) * 64].astype(BF16)
                dk = _mm(dst, qs) * SCALE
                dv = _mm(p, dos)
                dkv_ref[pl.ds(cur, BLK), g * 64:(g + 1) * 64] = dk[BLK:2 * BLK]
                dkv_ref[pl.ds(cur, BLK), 128 + g * 64:192 + g * 64] = dv[BLK:2 * BLK]
                dkv_ref[pl.ds(prev, BLK), g * 64:(g + 1) * 64] += dk[0:BLK]
                dkv_ref[pl.ds(prev, BLK), 128 + g * 64:192 + g * 64] += dv[0:BLK]

    tq = ATTN_BLOCKS * BLK
    return _launch(body, "attn_bwd", (T // tq,),
                   [_heads(tq), _full((T, 256)), _heads(tq), pl.BlockSpec(memory_space=pltpu.SMEM)],
                   [_rows(tq, 512), _full((T, 256)), _full((8, 128))],
                   [jax.ShapeDtypeStruct((T, 512), BF16), jax.ShapeDtypeStruct((T, 256), F32),
                    jax.ShapeDtypeStruct((8, 128), F32)], [], (q, kv, do, sinks), exchange)


def _rows8(tm, cols):
    return lax.broadcasted_iota(jnp.int32, (tm, cols), 0) & 7


def _lru_gates(xc, wa, ba, wx, bx, lam):
    r = _sigmoid(_mm(xc, wa) + ba)
    ii = _sigmoid(_mm(xc, wx) + bx)
    sp = _softplus(-lam)
    la = -LRU_C * r * sp
    a = jnp.exp(la)
    m = jnp.sqrt(-jnp.tanh(la) * (a * a + 1.0))
    return r, ii, sp, a, m


def _rnn_fwd(xr, gr, cw, cb, wa, ba, wx, bx, lam, exchange=None):
    T = xr.shape[0]
    tm = 512
    C = D_RNN

    def body(xr_ref, gr_ref, cw_ref, cb_ref, wa_ref, ba_ref, wx_ref, bx_ref, lam_ref,
             xc_ref, h_ref, rec_ref, ext, a_s, b_s, carry):
        i = pl.program_id(0)

        @pl.when(i == 0)
        def _():
            ext[...] = jnp.zeros((8, C), F32)
            carry[...] = jnp.zeros((8, C), F32)

        xr = xr_ref[...]
        edge = ext[...]
        xc = cb_ref[...] + cw_ref[3:4, :] * xr
        for k in range(3):
            xc = xc + cw_ref[k:k + 1, :] * _shift_rows(xr, 3 - k, edge)
        ext[...] = xr[tm - 8:tm, :]
        xc_ref[...] = xc
        _, ii, _, a, m = _lru_gates(xc, wa_ref[...], ba_ref[...], wx_ref[...], bx_ref[...], lam_ref[...])
        b = m * ii * xc
        r8 = _rows8(tm, C)
        for d in (1, 2, 4):
            ok = r8 >= d
            a_sh = jnp.where(ok, pltpu.roll(a, d, 0), 1.0)
            b_sh = jnp.where(ok, pltpu.roll(b, d, 0), 0.0)
            b = a * b_sh + b
            a = a * a_sh
        a_s[...] = a
        b_s[...] = b

        def step(g, hin):
            s = pl.multiple_of(g * 8, 8)
            hg = a_s[pl.ds(s, 8), :] * hin + b_s[pl.ds(s, 8), :]
            h_ref[pl.ds(s, 8), :] = hg
            return jnp.broadcast_to(hg[7:8, :], (8, C))

        carry[...] = lax.fori_loop(0, tm // 8, step, carry[...], unroll=4)
        ge, _ = _gelu(gr_ref[...])
        rec_ref[...] = (h_ref[...] * ge).astype(BF16)

    vec = _full((1, C))
    in_specs = [_rows(tm, C), _rows(tm, C), _full((4, C)), vec, _full((C, C)), vec, _full((C, C)), vec, vec]
    out_specs = [_rows(tm, C), _rows(tm, C), _rows(tm, C)]
    out_shape = [jax.ShapeDtypeStruct((T, C), F32), jax.ShapeDtypeStruct((T, C), F32), jax.ShapeDtypeStruct((T, C), BF16)]
    scratch = [pltpu.VMEM((8, C), F32), pltpu.VMEM((tm, C), F32), pltpu.VMEM((tm, C), F32), pltpu.VMEM((8, C), F32)]
    return _launch(body, "rnn_fwd", (T // tm,), in_specs, out_specs, out_shape, scratch,
                   (xr, gr, cw, cb, wa, ba, wx, bx, lam), exchange)


def _rnn_bwd(drec, gr, h, xc, xr, cw, wa, ba, wx, bx, lam, exchange=None):
    T = xr.shape[0]
    tm = 512
    C = D_RNN
    nt = T // tm
    t8 = tm // 8

    def body(drec_ref, gr_ref, h_ref, hp_ref, xc_ref, xr_ref, cw_ref, wa_ref, ba_ref, wx_ref, bx_ref,
             lam_ref, dxr_ref, dgr_ref, dwa_ref, dwx_ref, dvec_ref, c_s, g_s, gout, ext, anext, gcarry):
        i = pl.program_id(0)
        j = nt - 1 - i

        @pl.when(i == 0)
        def _():
            dwa_ref[...] = jnp.zeros_like(dwa_ref)
            dwx_ref[...] = jnp.zeros_like(dwx_ref)
            dvec_ref[...] = jnp.zeros_like(dvec_ref)
            anext[...] = jnp.zeros((8, C), F32)
            gcarry[...] = jnp.zeros((8, C), F32)
            ext[...] = jnp.zeros((8, C), F32)

        xc = xc_ref[...]
        lam = lam_ref[...]
        r, ii, sp, a, m = _lru_gates(xc, wa_ref[...], ba_ref[...], wx_ref[...], bx_ref[...], lam)
        ge, dge = _gelu(gr_ref[...])
        drec = drec_ref[...]
        hh = h_ref[...]
        dgr_ref[...] = (drec * hh * dge).astype(BF16)
        dh = drec * ge
        rowi = lax.broadcasted_iota(jnp.int32, (tm, C), 0)
        c = jnp.where(rowi == tm - 1, jnp.broadcast_to(anext[0:1, :], (tm, C)), pltpu.roll(a, tm - 1, 0))
        anext[...] = a[0:8, :]
        r8 = rowi & 7
        gg = dh
        for d in (1, 2, 4):
            ok = r8 < 8 - d
            c_sh = jnp.where(ok, pltpu.roll(c, tm - d, 0), 1.0)
            g_sh = jnp.where(ok, pltpu.roll(gg, tm - d, 0), 0.0)
            gg = c * g_sh + gg
            c = c * c_sh
        c_s[...] = c
        g_s[...] = gg

        def step(k, gin):
            s = pl.multiple_of((t8 - 1 - k) * 8, 8)
            og = c_s[pl.ds(s, 8), :] * gin + g_s[pl.ds(s, 8), :]
            gout[pl.ds(s, 8), :] = og
            return jnp.broadcast_to(og[0:1, :], (8, C))

        gcarry[...] = lax.fori_loop(0, t8, step, gcarry[...], unroll=4)
        G = gout[...]
        hprev_row = jnp.where(j > 0, hp_ref[7:8, :], 0.0)
        hprev = jnp.where(rowi == 0, jnp.broadcast_to(hprev_row, (tm, C)), pltpu.roll(hh, 1, 0))
        da = G * hprev
        dm = G * ii * xc
        di = G * m * xc
        dxc = G * m * ii
        dla = da * a - dm * a * a / m
        dr = dla * (-LRU_C * sp)
        dsp = _colsum(dla * (-LRU_C * r))
        dlam = dsp * (-_sigmoid(-lam))
        dpr = dr * r * (1.0 - r)
        dpi = di * ii * (1.0 - ii)
        dxc = dxc + _mm_nt(dpr, wa_ref[...]) + _mm_nt(dpi, wx_ref[...])
        dwa_ref[...] += _mm_tn(xc, dpr)
        dwx_ref[...] += _mm_tn(xc, dpi)
        dvec_ref[0:1, :] += _colsum(dpr)
        dvec_ref[1:2, :] += _colsum(dpi)
        dvec_ref[2:3, :] += dlam
        dvec_ref[3:4, :] += _colsum(dxc)
        edge = ext[...]
        xr = xr_ref[...]
        dxr = cw_ref[3:4, :] * dxc
        dvec_ref[7:8, :] += _colsum(dxc * xr)
        for k in range(3):
            up = _shift_rows(dxc, k - 3, edge)
            dxr = dxr + cw_ref[k:k + 1, :] * up
            dvec_ref[4 + k:5 + k, :] += _colsum(up * xr)
        ext[...] = dxc[0:8, :]
        dxr_ref[...] = dxr.astype(BF16)

    rev = lambda i: nt - 1 - i
    prev8 = lambda i: jnp.maximum((nt - 1 - i) * t8 - 1, 0)
    vec = _full((1, C))
    return _launch(
        body, "rnn_bwd", (nt,),
        [_rows(tm, C, rev), _rows(tm, C, rev), _rows(tm, C, rev), _rows(8, C, prev8), _rows(tm, C, rev),
         _rows(tm, C, rev), _full((4, C)), _full((C, C)), vec, _full((C, C)), vec, vec],
        [_rows(tm, C, rev), _rows(tm, C, rev), _full((C, C)), _full((C, C)), _full((8, C))],
        [jax.ShapeDtypeStruct((T, C), BF16), jax.ShapeDtypeStruct((T, C), BF16),
         jax.ShapeDtypeStruct((C, C), F32), jax.ShapeDtypeStruct((C, C), F32), jax.ShapeDtypeStruct((8, C), F32)],
        [pltpu.VMEM((tm, C), F32), pltpu.VMEM((tm, C), F32), pltpu.VMEM((tm, C), F32),
         pltpu.VMEM((8, C), F32), pltpu.VMEM((8, C), F32), pltpu.VMEM((8, C), F32)],
        (drec, gr, h, h, xc, xr, cw, wa, ba, wx, bx, lam), exchange)


def _out_proj(att, rec, x, w_out, g1, b1):
    T = x.shape[0]
    tm = min(1024, T)

    def body(att_ref, rec_ref, x_ref, w_ref, g1_ref, b1_ref, z_ref, h_ref):
        mix = _mm(att_ref[...], w_ref[0:512, :]) + _mm(rec_ref[...], w_ref[512:1024, :])
        z1 = ALPHA * x_ref[...] + mix
        z_ref[...] = z1
        h1, _, _ = _ln(z1, g1_ref[...], b1_ref[...])
        h_ref[...] = h1.astype(MXU_DTYPE).astype(BF16)

    return pl.pallas_call(
        body, name="out_proj", grid=(T // tm,),
        in_specs=[_rows(tm, 512), _rows(tm, 512), _rows(tm, D), _full((D, D)), _full((1, D)), _full((1, D))],
        out_specs=[_rows(tm, D), _rows(tm, D)],
        out_shape=[jax.ShapeDtypeStruct((T, D), F32), jax.ShapeDtypeStruct((T, D), BF16)],
        compiler_params=_params(),
    )(att, rec, x, w_out, g1, b1)


NC = D_FF // FF_CHUNK


def _ffn_up(h1b, w_up_t, fcw, fcb, exchange=None):
    T = h1b.shape[0]
    tm = min(1024, T)
    CW = FF_CHUNK

    def body(h_ref, wg_ref, wv_ref, fcw_ref, fcb_ref, gate_ref, ge_ref, vd_ref, act_ref, before):
        i = pl.program_id(1)

        @pl.when(i == 0)
        def _():
            before[...] = jnp.zeros((8, CW), F32)

        hb = h_ref[...]
        gate = _mm_nt(hb, wg_ref[...])
        val = _mm_nt(hb, wv_ref[...])
        gate_ref[...] = gate.astype(BF16)
        edge = before[...]
        gc = (fcb_ref[...] + fcw_ref[0:1, :] * _shift_rows(gate, 2, edge) + fcw_ref[1:2, :] * _shift_rows(gate, 1, edge)
              + fcw_ref[2:3, :] * gate)
        before[...] = gate[tm - 8:tm, :]
        ge, dge = _gelu(gc)
        ge_ref[...] = ge.astype(BF16)
        vd_ref[...] = (val * dge).astype(BF16)
        act_ref[...] = (ge * val).astype(BF16)

    chunk = pl.BlockSpec((None, tm, CW), lambda c, i: (c, i, 0))
    return _launch(
        body, "ffn_up", (NC, T // tm),
        [pl.BlockSpec((tm, D), lambda c, i: (i, 0)), pl.BlockSpec((CW, D), lambda c, i: (c, 0)),
         pl.BlockSpec((CW, D), lambda c, i: (NC + c, 0)), pl.BlockSpec((None, 3, CW), lambda c, i: (c, 0, 0)),
         pl.BlockSpec((None, 1, CW), lambda c, i: (c, 0, 0))],
        [chunk] * 4, [jax.ShapeDtypeStruct((NC, T, CW), BF16)] * 4, [pltpu.VMEM((8, CW), F32)],
        (h1b, w_up_t, w_up_t, fcw, fcb), exchange)


def _ffn_down(act, z1, p, tgt, w_down, w_g, w_p_t, g1, b1, g2, b2, bg):
    T = z1.shape[0]
    tm = 512

    def body(act_ref, z_ref, p_ref, t_ref, wdn_hbm, wg_hbm, wp_hbm, g1_ref, b1_ref, g2_ref, b2_ref, bg_ref,
             dz2_ref, dz2b_ref, dpre_ref, dpp_ref, vec_ref, wdn, wg, wp):
        @pl.when(pl.program_id(0) == 0)
        def _():
            pltpu.sync_copy(wdn_hbm, wdn)
            pltpu.sync_copy(wg_hbm, wg)
            pltpu.sync_copy(wp_hbm, wp)
            vec_ref[...] = jnp.zeros_like(vec_ref)

        g2v = g2_ref[...]
        h1, _, _ = _ln(z_ref[...], g1_ref[...], b1_ref[...])
        h1b = h1.astype(MXU_DTYPE)
        ffn = _mm(act_ref[0], wdn[0:FF_CHUNK, :])
        for c in range(1, NC):
            ffn = ffn + _mm(act_ref[c], wdn[c * FF_CHUNK:(c + 1) * FF_CHUNK, :])
        sg = _sigmoid(_mm(h1b, wg[...]) + bg_ref[...])
        pp = _mm_nt(p_ref[...], wp[...])
        z2 = ALPHA * h1 + ffn + sg * pp
        y, xh2, rstd2 = _ln(z2, g2v, b2_ref[...])
        diff = y - t_ref[...]
        dy = diff * (1.0 / D)
        dz2 = _ln_bwd(dy, xh2, rstd2, g2v)
        dpre = dz2 * pp * sg * (1.0 - sg)
        dz2_ref[...] = dz2
        dz2b_ref[...] = dz2.astype(BF16)
        dpre_ref[...] = dpre.astype(BF16)
        dpp_ref[...] = (dz2 * sg).astype(BF16)
        loss = 0.5 * jnp.sum(jnp.sum(diff * diff, axis=1, keepdims=True), axis=0, keepdims=True) * (1.0 / D)
        vec_ref[0:1, :] += jnp.broadcast_to(loss, (1, D))
        vec_ref[1:2, :] += _colsum(dy * xh2)
        vec_ref[2:3, :] += _colsum(dy)
        vec_ref[3:4, :] += _colsum(dpre)

    anyspec = pl.BlockSpec(memory_space=pl.ANY)
    vec = _full((1, D))
    return pl.pallas_call(
        body, name="ffn_down", grid=(T // tm,),
        in_specs=[pl.BlockSpec((NC, tm, FF_CHUNK), lambda i: (0, i, 0)), _rows(tm, D), _rows(tm, PLE), _rows(tm, D),
                  anyspec, anyspec, anyspec] + [vec] * 5,
        out_specs=[_rows(tm, D)] * 4 + [_full((8, D))],
        out_shape=[jax.ShapeDtypeStruct((T, D), F32)] + [jax.ShapeDtypeStruct((T, D), BF16)] * 3
                  + [jax.ShapeDtypeStruct((8, D), F32)],
        scratch_shapes=[pltpu.VMEM((D_FF, D), MXU_DTYPE), pltpu.VMEM((D, D), MXU_DTYPE), pltpu.VMEM((D, PLE), MXU_DTYPE)],
        compiler_params=_params(),
    )(act, z1, p, tgt, w_down, w_g, w_p_t, g1, b1, g2, b2, bg)


def _ffn_bwd(dz2b, gate, ge, vd, w_down, fcw):
    T = dz2b.shape[0]
    tm = min(1024, T)
    CW = FF_CHUNK
    nt = T // tm

    def body(dz_ref, wdn_ref, gate_ref, ge_ref, vd_ref, fcw_ref, dup_ref, dfc_ref, after):
        i = pl.program_id(1)

        @pl.when(i == 0)
        def _():
            after[...] = jnp.zeros((8, CW), F32)
            dfc_ref[...] = jnp.zeros_like(dfc_ref)

        gate = gate_ref[...].astype(F32)
        dact = _mm_nt(dz_ref[...], wdn_ref[...])
        dgc = dact * vd_ref[...].astype(F32)
        edge = after[...]
        dgc1 = _shift_rows(dgc, -1, edge)
        dgc2 = _shift_rows(dgc, -2, edge)
        after[...] = dgc[0:8, :]
        dup_ref[0] = (fcw_ref[2:3, :] * dgc + fcw_ref[1:2, :] * dgc1 + fcw_ref[0:1, :] * dgc2).astype(BF16)
        dup_ref[1] = (dact * ge_ref[...].astype(F32)).astype(BF16)
        dfc_ref[0:1, :] += _colsum(dgc2 * gate)
        dfc_ref[1:2, :] += _colsum(dgc1 * gate)
        dfc_ref[2:3, :] += _colsum(dgc * gate)
        dfc_ref[3:4, :] += _colsum(dgc)

    rev = lambda c, i: (c, nt - 1 - i, 0)
    chunk = pl.BlockSpec((None, tm, CW), rev)
    return pl.pallas_call(
        body, name="ffn_bwd", grid=(NC, nt),
        in_specs=[pl.BlockSpec((tm, D), lambda c, i: (nt - 1 - i, 0)), pl.BlockSpec((CW, D), lambda c, i: (c, 0)),
                  chunk, chunk, chunk, pl.BlockSpec((None, 3, CW), lambda c, i: (c, 0, 0))],
        out_specs=[pl.BlockSpec((None, 2, tm, CW), lambda c, i: (c, 0, nt - 1 - i, 0)),
                   pl.BlockSpec((None, 8, CW), lambda c, i: (c, 0, 0))],
        out_shape=[jax.ShapeDtypeStruct((NC, 2, T, CW), BF16), jax.ShapeDtypeStruct((NC, 8, CW), F32)],
        scratch_shapes=[pltpu.VMEM((8, CW), F32)],
        compiler_params=_params(),
    )(dz2b, w_down, gate, ge, vd, fcw)


def _ffn_dh1(dup, dz2, dpre, z1, w_up_t, w_g, g1, b1):
    T = z1.shape[0]
    tm = 512

    def body(dup_ref, dz2_ref, dpre_ref, z_ref, wup_hbm, wg_hbm, g1_ref, b1_ref, dz1_ref, vec_ref, wup, wg):
        @pl.when(pl.program_id(0) == 0)
        def _():
            pltpu.sync_copy(wup_hbm, wup)
            pltpu.sync_copy(wg_hbm, wg)
            vec_ref[...] = jnp.zeros_like(vec_ref)

        g1v = g1_ref[...]
        _, xh1, rstd1 = _ln(z_ref[...], g1v, b1_ref[...])
        dh1 = ALPHA * dz2_ref[...] + _mm_nt(dpre_ref[...], wg[...])
        for c in range(NC):
            for s in range(2):
                r0 = s * D_FF + c * FF_CHUNK
                dh1 = dh1 + _mm(dup_ref[c, s], wup[r0:r0 + FF_CHUNK, :])
        dz1_ref[...] = _ln_bwd(dh1, xh1, rstd1, g1v)
        vec_ref[0:1, :] += _colsum(dh1 * xh1)
        vec_ref[1:2, :] += _colsum(dh1)

    anyspec = pl.BlockSpec(memory_space=pl.ANY)
    vec = _full((1, D))
    return pl.pallas_call(
        body, name="ffn_dh1", grid=(T // tm,),
        in_specs=[pl.BlockSpec((NC, 2, tm, FF_CHUNK), lambda i: (0, 0, i, 0)), _rows(tm, D), _rows(tm, D), _rows(tm, D),
                  anyspec, anyspec, vec, vec],
        out_specs=[_rows(tm, D), _full((8, D))],
        out_shape=[jax.ShapeDtypeStruct((T, D), F32), jax.ShapeDtypeStruct((8, D), F32)],
        scratch_shapes=[pltpu.VMEM((2 * D_FF, D), MXU_DTYPE), pltpu.VMEM((D, D), MXU_DTYPE)],
        compiler_params=_params(),
    )(dup, dz2, dpre, z1, w_up_t, w_g, g1, b1)


def _out_proj_bwd(dz1, w_out, exchange=None):
    T = dz1.shape[0]
    tm = min(1024, T)

    def body(dz_ref, w_ref, datt_ref, drec_ref):
        dzb = dz_ref[...].astype(MXU_DTYPE)
        datt = _mm_nt(dzb, w_ref[0:512, :])
        for h in range(HEADS):
            datt_ref[h] = datt[:, h * 64:(h + 1) * 64].astype(BF16)
        drec_ref[...] = _mm_nt(dzb, w_ref[512:1024, :])

    return _launch(body, "out_proj_bwd", (T // tm,), [_rows(tm, D), _full((D, D))], [_heads(tm), _rows(tm, 512)],
                   [jax.ShapeDtypeStruct((HEADS, T, 64), BF16), jax.ShapeDtypeStruct((T, 512), F32)], [],
                   (dz1, w_out), exchange)


def _in_proj_bwd(dq, dkv, dxr, dgr, dz1, w_in_t, exchange=None):
    T = dz1.shape[0]
    tm = 512
    W = D_IN // 4

    def body(dq_ref, dkv_ref, dxr_ref, dgr_ref, dz_ref, w_ref, dx_ref, du_ref):
        dkv = dkv_ref[...]
        dx_ref[...] = (ALPHA * dz_ref[...] + _mm(dq_ref[...], w_ref[0:512, :]) + _mm(dkv, w_ref[512:768, :])
                       + _mm(dxr_ref[...], w_ref[768:1280, :]) + _mm(dgr_ref[...], w_ref[1280:1792, :]))
        dq, dxr, dgr = dq_ref[...].astype(F32), dxr_ref[...].astype(F32), dgr_ref[...].astype(F32)
        du_ref[0] = dq[:, 0:W].astype(BF16)
        du_ref[1, :, 0:64] = dq[:, W:512].astype(BF16)
        du_ref[1, :, 64:320] = dkv.astype(BF16)
        du_ref[1, :, 320:W] = dxr[:, 0:128].astype(BF16)
        du_ref[2, :, 0:384] = dxr[:, 128:512].astype(BF16)
        du_ref[2, :, 384:W] = dgr[:, 0:64].astype(BF16)
        du_ref[3] = dgr[:, 64:512].astype(BF16)

    return _launch(body, "in_proj_bwd", (T // tm,),
                   [_rows(tm, 512), _rows(tm, 256), _rows(tm, 512), _rows(tm, 512), _rows(tm, D), _full((D_IN, D))],
                   [_rows(tm, D), pl.BlockSpec((4, tm, W), lambda i: (0, i, 0))],
                   [jax.ShapeDtypeStruct((T, D), F32), jax.ShapeDtypeStruct((4, T, W), BF16)], [],
                   (dq, dkv, dxr, dgr, dz1, w_in_t), exchange)


def _accumulate_tn(a_ref, b_ref, o_ref):
    @pl.when(pl.program_id(1) == 0)
    def _():
        o_ref[...] = jnp.zeros_like(o_ref)

    o_ref[...] += _mm_tn(a_ref[...], b_ref[...])


def _weight_grad_cols(a, b, name, n_blocks, b_spec, out_shape, out_spec, exchange=None):
    T, M = a.shape
    bt = min(DW_TOKENS, T)
    return _launch(functools.partial(_accumulate_tn), name, (n_blocks, T // bt),
                   [pl.BlockSpec((bt, M), lambda m, k: (k, 0)), b_spec(bt)], [out_spec],
                   [jax.ShapeDtypeStruct(out_shape, F32)], [], (a, b), exchange)


def _weight_grad(a, b, bm, name):
    bt = min(DW_TOKENS // 2 if b.dtype == F32 else DW_TOKENS, b.shape[0])
    if a.ndim == 3:
        assert a.shape[2] == bm
        T, M = a.shape[1], a.shape[0] * bm
        a_spec = pl.BlockSpec((None, bt, bm), lambda m, k: (m, k, 0))
    else:
        T, M = a.shape
        a_spec = pl.BlockSpec((bt, bm), lambda m, k: (k, m))
    N = b.shape[1]
    nk = T // bt

    return pl.pallas_call(
        functools.partial(_accumulate_tn), name=name, grid=(M // bm, nk),
        in_specs=[a_spec, pl.BlockSpec((bt, N), lambda m, k: (k, 0))],
        out_specs=pl.BlockSpec((bm, N), lambda m, k: (m, 0)),
        out_shape=jax.ShapeDtypeStruct((M, N), F32),
        compiler_params=_params(),
    )(a, b)


def _adamw(w, g, m, v, name):
    R, C = w.shape
    tr = R // 8 if R % 64 == 0 else R
    c1 = 1.0 / (1.0 - ADAM_B1 ** ADAM_STEP)
    c2 = 1.0 / (1.0 - ADAM_B2 ** ADAM_STEP)

    def body(w_ref, g_ref, m_ref, v_ref, d_ref, nm_ref, nv_ref):
        g = g_ref[...]
        nm = ADAM_B1 * m_ref[...] + (1.0 - ADAM_B1) * g
        nv = ADAM_B2 * v_ref[...] + (1.0 - ADAM_B2) * g * g
        nm_ref[...] = nm
        nv_ref[...] = nv
        d_ref[...] = -ADAM_LR * ((nm * c1) / (jnp.sqrt(nv * c2) + ADAM_EPS) + ADAM_WD * w_ref[...])

    spec = pl.BlockSpec((tr, C), lambda i: (i, 0))
    return pl.pallas_call(
        body, name=name, grid=(R // tr,),
        in_specs=[spec] * 4, out_specs=[spec] * 3,
        out_shape=[jax.ShapeDtypeStruct((R, C), F32)] * 3,
        compiler_params=_params(),
    )(w, g, m, v)


def _adamw_halves(ws, mines, sibs, ms, vs, c, name, exchange=None):
    n, nb = len(ws), 4
    c1 = 1.0 / (1.0 - ADAM_B1 ** ADAM_STEP)
    c2 = 1.0 / (1.0 - ADAM_B2 ** ADAM_STEP)

    def body(c_ref, *refs):
        own = (pl.program_id(0) // nb) == c_ref[0]
        for i in range(n):
            w_ref, a_ref, b_ref, m_ref, v_ref = refs[5 * i:5 * i + 5]
            g_ref, d_ref, nm_ref, nv_ref = refs[5 * n + 4 * i:5 * n + 4 * i + 4]
            g = jnp.where(own, a_ref[...], b_ref[...])
            nm = ADAM_B1 * m_ref[...] + (1.0 - ADAM_B1) * g
            nv = ADAM_B2 * v_ref[...] + (1.0 - ADAM_B2) * g * g
            g_ref[...] = g
            nm_ref[...] = nm
            nv_ref[...] = nv
            d_ref[...] = -ADAM_LR * ((nm * c1) / (jnp.sqrt(nv * c2) + ADAM_EPS) + ADAM_WD * w_ref[...])

    in_specs, out_specs, out_shape, args = [], [], [], []
    for w, a, b, m, v in zip(ws, mines, sibs, ms, vs):
        R, C = w.shape
        tr = R // (2 * nb)
        assert tr % 8 == 0 and a.shape == (R // 2, C)
        full = pl.BlockSpec((tr, C), lambda i, c_ref: (i, 0))
        half = pl.BlockSpec((tr, C), lambda i, c_ref: (i % nb, 0))
        in_specs += [full, half, half, full, full]
        out_specs += [full] * 4
        out_shape += [jax.ShapeDtypeStruct((R, C), F32)] * 4
        args += [w, a, b, m, v]
    out = _launch(body, name, (2 * nb,), in_specs, out_specs, out_shape, [], (c, *args), exchange, prefetch=1)
    return [tuple(out[4 * i:4 * i + 4]) for i in range(n)], list(out[4 * n:])


def _add4(fs, name):
    n = len(fs)

    def body(*refs):
        for a_ref, o_ref in zip(refs[:n], refs[n:]):
            o_ref[...] = ((a_ref[0].astype(F32) + a_ref[1].astype(F32)) + a_ref[2].astype(F32)) + a_ref[3].astype(F32)

    for f in fs:
        assert (f.shape[1] // 2) % 16 == 0
    return pl.pallas_call(
        body, name=name, grid=(2,),
        in_specs=[pl.BlockSpec((4, f.shape[1] // 2, f.shape[2]), lambda i: (0, i, 0)) for f in fs],
        out_specs=[pl.BlockSpec((f.shape[1] // 2, f.shape[2]), lambda i: (i, 0)) for f in fs],
        out_shape=[jax.ShapeDtypeStruct(f.shape[1:], F32) for f in fs], compiler_params=_params())(*fs)


def _gather_first(wsrc, cpack):
    def body(w_ref, c_ref, gw_ref, gc_ref, send_sems, recv_sems, local_sem, csend, crecv, clocal):
        x, y, c = _pos()
        me = 2 * x + y
        chips = _other_chips(x, y)
        start, forward, finish = _gather_steps(w_ref, gw_ref, send_sems, recv_sems, local_sem)
        start()
        loc = pltpu.make_async_copy(c_ref, gc_ref.at[me], clocal)
        loc.start()

        def conv_copy(k, slot):
            px, py = chips[k]
            return pltpu.make_async_remote_copy(src_ref=c_ref, dst_ref=gc_ref.at[slot], send_sem=csend.at[k],
                                                recv_sem=crecv.at[k], device_id=(px, py, c), device_id_type=MESH)

        for k in range(3):
            conv_copy(k, me).start()
        forward()
        finish()
        for k, (px, py) in enumerate(chips):
            conv_copy(k, 2 * px + py).wait_recv()
        for k in range(3):
            conv_copy(k, me).wait_send()
        loc.wait()

    anyspec = pl.BlockSpec(memory_space=pl.ANY)
    return pl.pallas_call(
        body, name="gather_first",
        in_specs=[anyspec, anyspec], out_specs=[anyspec, anyspec],
        out_shape=[jax.ShapeDtypeStruct((4,) + wsrc.shape, wsrc.dtype), jax.ShapeDtypeStruct((4,) + cpack.shape, cpack.dtype)],
        scratch_shapes=GATHER_SCRATCH + [pltpu.SemaphoreType.DMA((3,)), pltpu.SemaphoreType.DMA((3,)), pltpu.SemaphoreType.DMA],
        compiler_params=_params(has_side_effects=True),
    )(wsrc, cpack)


def _all_devices_exchange(s):
    def make(ins, outs, sems):
        s_ref, o_ref = ins[0], outs[0]
        send_sems, recv_sems, local_sem = sems
        x, y, c = _pos()
        me = 4 * x + 2 * y + c
        loc = pltpu.make_async_copy(s_ref, o_ref.at[me], local_sem)

        def copy(k, slot):
            peer = (x ^ (k >> 2), y ^ ((k >> 1) & 1), c ^ (k & 1))
            return pltpu.make_async_remote_copy(src_ref=s_ref, dst_ref=o_ref.at[slot], send_sem=send_sems.at[k - 1],
                                                recv_sem=recv_sems.at[k - 1], device_id=peer, device_id_type=MESH)

        def start():
            loc.start()
            for k in range(1, 8):
                copy(k, me).start()

        def finish():
            for k in range(1, 8):
                copy(k, 4 * (x ^ (k >> 2)) + 2 * (y ^ ((k >> 1) & 1)) + (c ^ (k & 1))).wait_recv()
            for k in range(1, 8):
                copy(k, me).wait_send()
            loc.wait()

        return start, lambda: None, finish

    return _Exchange([s], [jax.ShapeDtypeStruct((8,) + s.shape, s.dtype)],
                     [pltpu.SemaphoreType.DMA((7,)), pltpu.SemaphoreType.DMA((7,)), pltpu.SemaphoreType.DMA], make)


def _sum_devices(a):
    def body(a_ref, o_ref):
        acc = a_ref[0]
        for d in range(1, 8):
            acc = acc + a_ref[d]
        o_ref[...] = acc

    vm = pl.BlockSpec(memory_space=pltpu.VMEM)
    return pl.pallas_call(body, name="sum_devices", in_specs=[vm], out_specs=vm,
                          out_shape=jax.ShapeDtypeStruct(a.shape[1:], F32), compiler_params=_params())(a)


def _swap_exchange(gs):
    n = len(gs)

    def make(ins, outs, sems):
        x, y, c = _pos()
        cps = []
        for i in range(n):
            half = gs[i].shape[1] // 2
            rows = pl.ds(pl.multiple_of((1 - c) * half, 8), half)
            cps.append(pltpu.make_async_remote_copy(src_ref=ins[i].at[:, rows, :], dst_ref=outs[i], send_sem=sems[0].at[i],
                                                    recv_sem=sems[1].at[i], device_id=(x, y, 1 - c), device_id_type=MESH))

        def start():
            for cp in cps:
                cp.start()

        def finish():
            for cp in cps:
                cp.wait()

        return start, lambda: None, finish

    return _Exchange(gs, [jax.ShapeDtypeStruct((4, g.shape[1] // 2, g.shape[2]), g.dtype) for g in gs],
                     [pltpu.SemaphoreType.DMA((n,)), pltpu.SemaphoreType.DMA((n,))], make)


def _scatter_exchange(ss):
    n = len(ss)

    def make(ins, outs, sems):
        send_sems, recv_sems, local_sems = sems
        x, y, c = _pos()
        me = 2 * x + y
        chips = _other_chips(x, y)
        locs = [pltpu.make_async_copy(ins[i].at[me], outs[i].at[me], local_sems.at[i]) for i in range(n)]

        def copy(i, k, src_slot, dst_slot):
            px, py = chips[k]
            return pltpu.make_async_remote_copy(src_ref=ins[i].at[src_slot], dst_ref=outs[i].at[dst_slot],
                                                send_sem=send_sems.at[3 * i + k], recv_sem=recv_sems.at[3 * i + k],
                                                device_id=(px, py, c), device_id_type=MESH)

        def start():
            for i in range(n):
                locs[i].start()
                for k, (px, py) in enumerate(chips):
                    copy(i, k, 2 * px + py, me).start()

        def finish():
            for i in range(n):
                for k, (px, py) in enumerate(chips):
                    copy(i, k, me, 2 * px + py).wait_recv()
            for i in range(n):
                for k, (px, py) in enumerate(chips):
                    copy(i, k, 2 * px + py, me).wait_send()
                locs[i].wait()

        return start, lambda: None, finish

    return _Exchange(ss, [jax.ShapeDtypeStruct(s.shape, s.dtype) for s in ss],
                     [pltpu.SemaphoreType.DMA((3 * n,)), pltpu.SemaphoreType.DMA((3 * n,)), pltpu.SemaphoreType.DMA((n,))], make)


def _send_exchange(rs):
    n = len(rs)

    def make(ins, outs, sems):
        x, y, c = _pos()
        cps = [pltpu.make_async_remote_copy(src_ref=ins[i], dst_ref=outs[i], send_sem=sems[0].at[i], recv_sem=sems[1].at[i],
                                            device_id=(x, y, 1 - c), device_id_type=MESH) for i in range(n)]

        def start():
            for cp in cps:
                cp.start()

        def finish():
            for cp in cps:
                cp.wait()

        return start, lambda: None, finish

    return _Exchange(rs, [jax.ShapeDtypeStruct(r.shape, r.dtype) for r in rs],
                     [pltpu.SemaphoreType.DMA((n,)), pltpu.SemaphoreType.DMA((n,))], make)


def _run_exchange(ex, name):
    ei, eo = len(ex.args), len(ex.out_shape)

    def body(*refs):
        start, forward, finish = ex.make(refs[:ei], refs[ei:ei + eo], refs[ei + eo:])
        start()
        forward()
        finish()

    anyspec = pl.BlockSpec(memory_space=pl.ANY)
    return pl.pallas_call(body, name=name, in_specs=[anyspec] * ei, out_specs=[anyspec] * eo, out_shape=ex.out_shape,
                          scratch_shapes=ex.scratch, compiler_params=_params(has_side_effects=True))(*ex.args)


def _add_half(gs, rs, c, name):
    n = len(gs)

    def body(c_ref, *refs):
        for g_ref, r_ref, o_ref in zip(refs[:n], refs[n:2 * n], refs[2 * n:]):
            o_ref[...] = (g_ref[...] + r_ref[...]).astype(BF16)

    g_specs, r_specs, out_shape = [], [], []
    for g, r in zip(gs, rs):
        _, H, C = r.shape
        tr = H // 2
        assert tr % 16 == 0 and g.shape == (4, 2 * H, C)
        g_specs.append(pl.BlockSpec((1, tr, C), lambda j, i, c_ref: (j, c_ref[0] * 2 + i, 0)))
        r_specs.append(pl.BlockSpec((1, tr, C), lambda j, i, c_ref: (j, i, 0)))
        out_shape.append(jax.ShapeDtypeStruct((4, H, C), BF16))
    grid_spec = pltpu.PrefetchScalarGridSpec(num_scalar_prefetch=1, grid=(4, 2), in_specs=g_specs + r_specs, out_specs=r_specs)
    return pl.pallas_call(body, name=name, grid_spec=grid_spec, out_shape=out_shape, compiler_params=_params())(c, *gs, *rs)


def _block_diag(w):
    eye = jnp.eye(RNN_BLOCKS, dtype=w.dtype)
    return (eye[:, None, :, None] * w[:, :, None, :]).reshape(D_RNN, D_RNN)


def _diag_blocks(wd):
    d = wd.reshape(RNN_BLOCKS, 64, RNN_BLOCKS, 64)
    return jnp.stack([d[h, :, h, :] for h in range(RNN_BLOCKS)])


def _split_pack(a, first, last):
    out, base = {}, PACK_OFF[first]
    for i in range(first, last):
        s = a[:, PACK_OFF[i] - base:PACK_OFF[i + 1] - base]
        out[BIG_KEYS[i]] = s.reshape(4 * 256, 256) if BIG_KEYS[i] == "w_p_t" else s.reshape(-1, 1024)
    return out


def _layer_grads(x, p, tgt, gw, small, shard=None, core=None):
    row = lambda v: v.reshape(1, -1)
    wa = _block_diag(small["gate_a_w"]).astype(MXU_DTYPE)
    wx = _block_diag(small["gate_x_w"]).astype(MXU_DTYPE)
    sinks = small["attn_sinks"].reshape(1, HEADS)

    dist = shard is not None
    q, kv, xr, gr, xb = _in_proj(x, gw["w_in_t"])
    cut = PACK_OFF[1] + PACK_ROWS[1] // 2
    att, *ga = _attn_fwd(q, kv, sinks, _gather_exchange(shard[PACK_OFF[1]:cut]) if dist else None)
    xc, h, rec, *gb = _rnn_fwd(xr, gr, small["rnn_conv_w"], row(small["rnn_conv_b"]), wa, row(small["gate_a_b"]),
                               wx, row(small["gate_x_b"]), row(small["lru_lambda"]),
                               _gather_exchange(shard[cut:PACK_OFF[3]]) if dist else None)
    if dist:
        gw = {**gw, **_split_pack(jnp.concatenate([ga[0], gb[0]], axis=1), 1, 3)}
    g1, b1 = row(small["ln1_g"]), row(small["ln1_b"])
    fcw = small["ffn_conv_w"].reshape(3, NC, FF_CHUNK).transpose(1, 0, 2)
    fcb = small["ffn_conv_b"].reshape(NC, 1, FF_CHUNK)
    z1, h1b = _out_proj(att, rec, x, gw["w_out"], g1, b1)
    gate, ge, vd, act, *gc = _ffn_up(h1b, gw["w_up_t"], fcw, fcb,
                                     _gather_exchange(shard[PACK_OFF[3]:PACK_OFF[6]]) if dist else None)
    if dist:
        gw = {**gw, **_split_pack(gc[0], 3, 6)}
    dz2, dz2b, dpre, dpp, vec2 = _ffn_down(act, z1, p, tgt, gw["w_down"], gw["w_g"], gw["w_p_t"], g1, b1,
                                           row(small["ln2_g"]), row(small["ln2_b"]), row(small["ple_gate_b"]))
    dup, dfc = _ffn_bwd(dz2b, gate, ge, vd, gw["w_down"], fcw)
    dz1, vec1 = _ffn_dh1(dup, dz2, dpre, z1, gw["w_up_t"], gw["w_g"], g1, b1)
    per_chip = 2 * D_FF // 4 // FF_CHUNK
    big = {
        "w_ffn_up": _weight_grad_cols(
            h1b, dup.reshape(2 * NC, -1, FF_CHUNK), "dw_up", 2 * NC,
            lambda bt: pl.BlockSpec((None, bt, FF_CHUNK), lambda m, k: (m, k, 0)), (4, D, 2 * D_FF // 4),
            pl.BlockSpec((None, D, FF_CHUNK), lambda m, k: (2 * (m % 2) + (m // 2) // per_chip, 0, (m // 2) % per_chip)))[0],
        "w_ffn_down": _weight_grad(act, dz2b, 512, "dw_down").reshape(4, D_FF // 4, D),
        "ple_gate_w": _weight_grad(h1b, dpre, 512, "dw_gate").reshape(4, D // 4, D),
        "ple_proj": _weight_grad_cols(
            p, dpp, "dw_proj", 4, lambda bt: pl.BlockSpec((bt, D // 4), lambda j, k: (k, j)),
            (4, PLE, D // 4), pl.BlockSpec((None, PLE, D // 4), lambda j, k: (j, 0, 0)))[0],
        "w_out": jnp.concatenate([_weight_grad(att, dz1, 512, "dw_out_att"), _weight_grad(rec, dz1, 512, "dw_out_rec")],
                                 axis=0).reshape(4, D // 4, D),
    }
    reduced = None
    if dist:
        g_ffn = [big[k] for k in EARLY_WEIGHTS]
        ex = _swap_exchange(g_ffn)
    datt, drec, *got = _out_proj_bwd(dz1, gw["w_out"], ex if dist else None)
    if dist:
        sums = _add_half(g_ffn, got, core, "add_half_ffn")
        ex, ex2 = _scatter_exchange(sums[:1]), _scatter_exchange(sums[1:])
    dxr, dgr, dwa, dwx, dvec, *got = _rnn_bwd(drec, gr, h, xc, xr, small["rnn_conv_w"], wa, row(small["gate_a_b"]),
                                              wx, row(small["gate_x_b"]), row(small["lru_lambda"]), ex if dist else None)
    dq, dkv, dsinks, *got2 = _attn_bwd(q, kv, datt, sinks, ex2 if dist else None)
    if dist:
        mine = _add4(got + got2, "add_chips_ffn")
        big = {}
    sg = {
        "attn_sinks": dsinks[:, 0],
        "rnn_conv_w": dvec[4:8],
        "rnn_conv_b": dvec[3],
        "gate_a_w": _diag_blocks(dwa),
        "gate_a_b": dvec[0],
        "gate_x_w": _diag_blocks(dwx),
        "gate_x_b": dvec[1],
        "lru_lambda": dvec[2],
        "ln1_g": vec1[0],
        "ln1_b": vec1[1],
        "ffn_conv_w": dfc[:, 0:3].transpose(1, 0, 2).reshape(3, D_FF),
        "ffn_conv_b": dfc[:, 3].reshape(D_FF),
        "ple_gate_b": vec2[3],
        "ln2_g": vec2[1],
        "ln2_b": vec2[2],
    }
    loss = vec2[0, 0:1]
    grad_x, du = _in_proj_bwd(dq, dkv, dxr, dgr, dz1, gw["w_in_t"])
    ex = None
    if dist:
        ex = _join_exchanges(_send_exchange(mine), _all_devices_exchange(_pack_vecs([sg[k] for k in SMALL] + [loss])[0]))
    big["w_in"], *got = _weight_grad_cols(
        xb, du, "dw_in", 4, lambda bt: pl.BlockSpec((None, bt, D_IN // 4), lambda j, k: (j, k, 0)), (4, D, D_IN // 4),
        pl.BlockSpec((None, D, D_IN // 4), lambda j, k: (j, 0, 0)), ex)
    if dist:
        reduced = (mine, got[:len(mine)])
    return grad_x, big, sg, loss, reduced, got[-1:]


BIG = ("w_in", "w_ffn_up", "w_out", "w_ffn_down", "ple_gate_w", "ple_proj")
BIG_KEYS = ("w_in_t", "w_up_t", "w_out", "w_down", "w_g", "w_p_t")
BIG_T = (True, True, False, False, False, True)
EARLY_WEIGHTS = ("w_ffn_up", "w_ffn_down", "ple_gate_w", "ple_proj", "w_out")
LATE_WEIGHTS = ("w_in",)
SMALL = ("attn_sinks", "rnn_conv_w", "rnn_conv_b", "gate_a_w", "gate_a_b", "gate_x_w", "gate_x_b", "lru_lambda",
         "ln1_g", "ln1_b", "ffn_conv_w", "ffn_conv_b", "ple_gate_b", "ln2_g", "ln2_b")
SHARDED_SMALL = ("rnn_conv_w", "ffn_conv_w")
WEIGHTS = ("w_in", "attn_sinks", "rnn_conv_w", "rnn_conv_b", "gate_a_w", "gate_a_b", "gate_x_w", "gate_x_b",
           "lru_lambda", "w_out", "ln1_g", "ln1_b", "w_ffn_up", "ffn_conv_w", "ffn_conv_b", "w_ffn_down",
           "ple_gate_w", "ple_gate_b", "ple_proj", "ln2_g", "ln2_b")


def _pack_big(d, first=0, last=6):
    parts = []
    for name, t in zip(BIG[first:last], BIG_T[first:last]):
        a = d[name]
        a = a.T if t else a
        parts.append(a.reshape(-1, 1024))
    return jnp.concatenate(parts, axis=0)


def _pack_vecs(items):
    parts, offs, n = [], [], 0
    for a in items:
        f = a.reshape(-1).astype(F32)
        pad = (-f.shape[0]) % 128
        parts.append(jnp.pad(f, (0, pad)))
        offs.append(n)
        n += (f.shape[0] + pad) // 128
    padr = (-n) % 8
    if padr:
        parts.append(jnp.zeros((padr * 128,), F32))
    return jnp.concatenate(parts).reshape(-1, 128), offs


def _unpack_vecs(a, offs, shapes):
    flat = a.reshape(-1)
    out = []
    for o, s in zip(offs, shapes):
        n = 1
        for d in s:
            n *= d
        out.append(flat[o * 128:o * 128 + n].reshape(s))
    return out


def kernel(x, p, w_in, attn_sinks, rnn_conv_w, rnn_conv_b, gate_a_w, gate_a_b, gate_x_w, gate_x_b, lru_lambda, w_out, ln1_g, ln1_b, w_ffn_up, ffn_conv_w, ffn_conv_b, w_ffn_down, ple_gate_w, ple_gate_b, ple_proj, ln2_g, ln2_b, loss_target, m_w_in, m_attn_sinks, m_rnn_conv_w, m_rnn_conv_b, m_gate_a_w, m_gate_a_b, m_gate_x_w, m_gate_x_b, m_lru_lambda, m_w_out, m_ln1_g, m_ln1_b, m_w_ffn_up, m_ffn_conv_w, m_ffn_conv_b, m_w_ffn_down, m_ple_gate_w, m_ple_gate_b, m_ple_proj, m_ln2_g, m_ln2_b, v_w_in, v_attn_sinks, v_rnn_conv_w, v_rnn_conv_b, v_gate_a_w, v_gate_a_b, v_gate_x_w, v_gate_x_b, v_lru_lambda, v_w_out, v_ln1_g, v_ln1_b, v_w_ffn_up, v_ffn_conv_w, v_ffn_conv_b, v_w_ffn_down, v_ple_gate_w, v_ple_gate_b, v_ple_proj, v_ln2_g, v_ln2_b):
    w = dict(w_in=w_in, attn_sinks=attn_sinks, rnn_conv_w=rnn_conv_w, rnn_conv_b=rnn_conv_b, gate_a_w=gate_a_w,
             gate_a_b=gate_a_b, gate_x_w=gate_x_w, gate_x_b=gate_x_b, lru_lambda=lru_lambda, w_out=w_out, ln1_g=ln1_g,
             ln1_b=ln1_b, w_ffn_up=w_ffn_up, ffn_conv_w=ffn_conv_w, ffn_conv_b=ffn_conv_b, w_ffn_down=w_ffn_down,
             ple_gate_w=ple_gate_w, ple_gate_b=ple_gate_b, ple_proj=ple_proj, ln2_g=ln2_g, ln2_b=ln2_b)
    m = dict(w_in=m_w_in, attn_sinks=m_attn_sinks, rnn_conv_w=m_rnn_conv_w, rnn_conv_b=m_rnn_conv_b, gate_a_w=m_gate_a_w,
             gate_a_b=m_gate_a_b, gate_x_w=m_gate_x_w, gate_x_b=m_gate_x_b, lru_lambda=m_lru_lambda, w_out=m_w_out,
             ln1_g=m_ln1_g, ln1_b=m_ln1_b, w_ffn_up=m_w_ffn_up, ffn_conv_w=m_ffn_conv_w, ffn_conv_b=m_ffn_conv_b,
             w_ffn_down=m_w_ffn_down, ple_gate_w=m_ple_gate_w, ple_gate_b=m_ple_gate_b, ple_proj=m_ple_proj,
             ln2_g=m_ln2_g, ln2_b=m_ln2_b)
    v = dict(w_in=v_w_in, attn_sinks=v_attn_sinks, rnn_conv_w=v_rnn_conv_w, rnn_conv_b=v_rnn_conv_b, gate_a_w=v_gate_a_w,
             gate_a_b=v_gate_a_b, gate_x_w=v_gate_x_w, gate_x_b=v_gate_x_b, lru_lambda=v_lru_lambda, w_out=v_w_out,
             ln1_g=v_ln1_g, ln1_b=v_ln1_b, w_ffn_up=v_w_ffn_up, ffn_conv_w=v_ffn_conv_w, ffn_conv_b=v_ffn_conv_b,
             w_ffn_down=v_w_ffn_down, ple_gate_w=v_ple_gate_w, ple_gate_b=v_ple_gate_b, ple_proj=v_ple_proj,
             ln2_g=v_ln2_g, ln2_b=v_ln2_b)
    w, m, v = ({k: a[0] for k, a in d.items()} for d in (w, m, v))
    chip = 2 * lax.axis_index("x") + lax.axis_index("y")
    core = lax.axis_index("c")

    wpack = _pack_big(w)
    cpack, _ = _pack_vecs([w["rnn_conv_w"], w["ffn_conv_w"]])
    shard = wpack.astype(MXU_DTYPE)
    g_in, gcp = _gather_first(shard[PACK_OFF[0]:PACK_OFF[1]], cpack)
    gw = _split_pack(g_in, 0, 1)
    small = {k: w[k] for k in SMALL}
    small["rnn_conv_w"] = gcp[:, 0:4].reshape(4, 4, 128).transpose(1, 0, 2).reshape(4, 512)
    small["ffn_conv_w"] = gcp[:, 4:22].reshape(4, 3, 768).transpose(1, 0, 2).reshape(3, 3072)

    core1 = core.reshape(1).astype(jnp.int32)
    grad_x, big, sg, loss, ffn_halves, small_all = _layer_grads(x[0], p[0, 0], loss_target[0], gw, small, shard, core1)

    shapes = [sg[k].shape for k in SMALL] + [(1,)]
    _, offs = _pack_vecs([jnp.zeros(s, F32) for s in shapes])
    red = dict(zip(SMALL + ("loss",), _unpack_vecs(_sum_devices(small_all[0]), offs, shapes)))
    red["rnn_conv_w"] = lax.dynamic_slice_in_dim(red["rnn_conv_w"], chip * 128, 128, axis=1)
    red["ffn_conv_w"] = lax.dynamic_slice_in_dim(red["ffn_conv_w"], chip * 768, 768, axis=1)

    g_late = [big[k] for k in LATE_WEIGHTS]
    sib = _run_exchange(_swap_exchange(g_late), "swap_late")
    from_chips = _run_exchange(_scatter_exchange(_add_half(g_late, sib, core1, "add_half_late")), "scatter_late")
    late_mine = _add4(from_chips, "add_chips_late")
    late_other = _run_exchange(_send_exchange(late_mine), "send_late")

    def adamw(names, mine, other, name):
        out, _ = _adamw_halves([w[k] for k in names], mine, other, [m[k] for k in names], [v[k] for k in names],
                               core1, name)
        return dict(zip(names, out))

    big_out = {**adamw(LATE_WEIGHTS, late_mine, late_other, "adamw_late"), **adamw(EARLY_WEIGHTS, *ffn_halves, "adamw_early")}
    wsm, offs2 = _pack_vecs([w[k] for k in SMALL])
    gsm, _ = _pack_vecs([red[k] for k in SMALL])
    msm, _ = _pack_vecs([m[k] for k in SMALL])
    vsm, _ = _pack_vecs([v[k] for k in SMALL])
    dsm, nmsm, nvsm = _adamw(wsm, gsm, msm, vsm, "adamw_small")
    shapes2 = [w[k].shape for k in SMALL]

    def named(n, smallp):
        d = {k: out[n][None] for k, out in big_out.items()}
        d.update({k: a[None] for k, a in zip(SMALL, _unpack_vecs(smallp, offs2, shapes2))})
        return [d[k] for k in WEIGHTS]

    return (red["loss"].reshape(()), grad_x[None], *named(0, gsm), *named(1, dsm), *named(2, nmsm), *named(3, nvsm))
```

```python
import functools

import jax
import jax.numpy as jnp
from jax import lax
from jax.experimental import pallas as pl
from jax.experimental.pallas import tpu as pltpu

F32 = jnp.float32
BF16 = jnp.bfloat16
MXU_DTYPE = jnp.bfloat16

D = 1024
D_ATT = 512
D_KV = 128
D_RNN = 512
D_IN = 1792
D_FF = 3072
FF_CHUNK = 512
PLE = 256
HEADS = 8
HEAD_DIM = 64
BLK = 128
ATTN_BLOCKS = 4
DW_TOKENS = 4096
RNN_BLOCKS = 8
LN_EPS = 1e-5
LRU_C = 8.0
ALPHA = float(2.0 ** 0.25)
SCALE = HEAD_DIM ** -0.5
NEG = -1e30

ADAM_LR = 0.001
ADAM_B1 = 0.9
ADAM_B2 = 0.999
ADAM_EPS = 1e-08
ADAM_WD = 0.01
ADAM_STEP = 10

VMEM_LIMIT_BYTES = 56 * 1024 * 1024
MESH = pl.DeviceIdType.MESH

PACK_ROWS = (448, 1536, 256, 768, 256, 64)
PACK_OFF = tuple(sum(PACK_ROWS[:i]) for i in range(len(PACK_ROWS) + 1))
PACK_TOTAL = PACK_OFF[-1]


def _params(**kw):
    return pltpu.CompilerParams(vmem_limit_bytes=VMEM_LIMIT_BYTES, **kw)


def _mm(a, b):
    return jnp.dot(a.astype(MXU_DTYPE), b.astype(MXU_DTYPE), preferred_element_type=F32)


def _mm_nt(a, b):
    return lax.dot_general(a.astype(MXU_DTYPE), b.astype(MXU_DTYPE), (((1,), (1,)), ((), ())),
                           preferred_element_type=F32)


def _mm_tn(a, b):
    return lax.dot_general(a.astype(MXU_DTYPE), b.astype(MXU_DTYPE), (((0,), (0,)), ((), ())),
                           preferred_element_type=F32)


def _sigmoid(x):
    return 0.5 + 0.5 * jnp.tanh(0.5 * x)


def _gelu(x):
    c = 0.7978845608028654
    k = 0.044715
    x2 = x * x
    t = jnp.tanh(x * (c + (c * k) * x2))
    h = 0.5 * (1.0 + t)
    return x * h, h * (1.0 + (x * (1.0 - t)) * (c + (3.0 * c * k) * x2))


def _shift_rows(x, s, edge8):
    R = x.shape[0]
    row8 = lax.broadcasted_iota(jnp.int32, (8, x.shape[1]), 0)
    if s > 0:
        rolled = pltpu.roll(x, s, 0)
        first = jnp.where(row8 < s, pltpu.roll(edge8, s, 0), rolled[0:8])
        return jnp.concatenate([first, rolled[8:]], axis=0)
    k = -s
    rolled = pltpu.roll(x, R - k, 0)
    last = jnp.where(row8 >= 8 - k, pltpu.roll(edge8, 8 - k, 0), rolled[R - 8:])
    return jnp.concatenate([rolled[:R - 8], last], axis=0)


def _softplus(x):
    return jnp.maximum(x, 0.0) + jnp.log(1.0 + jnp.exp(-jnp.abs(x)))


def _ln(z, g, b):
    mu = jnp.mean(z, axis=-1, keepdims=True)
    zc = z - mu
    var = jnp.mean(zc * zc, axis=-1, keepdims=True)
    rstd = lax.rsqrt(var + LN_EPS)
    xhat = zc * rstd
    return xhat * g + b, xhat, rstd


def _ln_bwd(dy, xhat, rstd, g):
    dxh = dy * g
    m1 = jnp.mean(dxh, axis=-1, keepdims=True)
    m2 = jnp.mean(dxh * xhat, axis=-1, keepdims=True)
    return rstd * (dxh - m1 - xhat * m2)


def _colsum(x):
    return jnp.sum(x, axis=0, keepdims=True)


def _full(shape):
    nd = len(shape)
    return pl.BlockSpec(shape, lambda *_: (0,) * nd)


def _rows(tm, cols, fn=None):
    if fn is None:
        return pl.BlockSpec((tm, cols), lambda i: (i, 0))
    return pl.BlockSpec((tm, cols), lambda i: (fn(i), 0))


def _heads(tm):
    return pl.BlockSpec((HEADS, tm, HEAD_DIM), lambda i: (0, i, 0))


def _in_proj(x, w_in_t):
    T = x.shape[0]
    tm = min(1024, T)

    def body(x_ref, w_ref, q_ref, kv_ref, xr_ref, gr_ref, xb_ref):
        xb = x_ref[...].astype(MXU_DTYPE)
        xb_ref[...] = xb.astype(BF16)
        q = _mm_nt(xb, w_ref[0:512, :])
        for h in range(HEADS):
            q_ref[h] = q[:, h * 64:(h + 1) * 64].astype(BF16)
        kv_ref[...] = _mm_nt(xb, w_ref[512:768, :]).astype(BF16)
        xr_ref[...] = _mm_nt(xb, w_ref[768:1280, :])
        gr_ref[...] = _mm_nt(xb, w_ref[1280:1792, :])

    return pl.pallas_call(
        body, name="in_proj", grid=(T // tm,),
        in_specs=[_rows(tm, D), _full((D_IN, D))],
        out_specs=[_heads(tm), _rows(tm, 256), _rows(tm, 512), _rows(tm, 512), _rows(tm, D)],
        out_shape=[jax.ShapeDtypeStruct((HEADS, T, 64), BF16), jax.ShapeDtypeStruct((T, 256), BF16),
                   jax.ShapeDtypeStruct((T, 512), F32), jax.ShapeDtypeStruct((T, 512), F32),
                   jax.ShapeDtypeStruct((T, D), BF16)],
        compiler_params=_params(),
    )(x, w_in_t)


def _attn_band(kv_ref, i):
    cur = pl.multiple_of(i * BLK, BLK)
    prev = pl.multiple_of(jnp.maximum(i - 1, 0) * BLK, BLK)
    band = jnp.concatenate([kv_ref[pl.ds(prev, BLK), :], kv_ref[pl.ds(cur, BLK), :]], axis=0)
    key = lax.broadcasted_iota(jnp.int32, (2 * BLK, 4 * BLK), 0)
    qry = lax.broadcasted_iota(jnp.int32, (2 * BLK, 4 * BLK), 1) & (BLK - 1)
    in_prev = jnp.logical_and(jnp.logical_and(key < BLK, key > qry), i > 0)
    mask = jnp.logical_or(in_prev, jnp.logical_and(key >= BLK, key - BLK <= qry))
    return band, mask, cur, prev


def _attn_scores(band, mask, qs, s_ref, g):
    st = jnp.where(mask, _mm_nt(band[:, g * 64:(g + 1) * 64], qs) * SCALE, NEG)
    lane = lax.broadcasted_iota(jnp.int32, (1, 4 * BLK), 1)
    sv = jnp.where(lane < BLK, s_ref[0, 4 * g],
                   jnp.where(lane < 2 * BLK, s_ref[0, 4 * g + 1], jnp.where(lane < 3 * BLK, s_ref[0, 4 * g + 2], s_ref[0, 4 * g + 3])))
    m = jnp.maximum(jnp.max(st, axis=0, keepdims=True), sv)
    p = jnp.exp(st - m)
    ps = jnp.exp(sv - m)
    return p, ps, jnp.sum(p, axis=0, keepdims=True) + ps


def _pos():
    return lax.axis_index("x"), lax.axis_index("y"), lax.axis_index("c")


def _other_chips(x, y):
    return [(1 - x, y), (x, 1 - y), (1 - x, 1 - y)]


def _gather_steps(w_ref, gw_ref, send_sems, recv_sems, local_sem):
    x, y, c = _pos()
    me = 2 * x + y
    chips = _other_chips(x, y)
    half = w_ref.shape[0] // 2
    mine = pl.ds(pl.multiple_of(c * half, 16), half)
    theirs = pl.ds(pl.multiple_of((1 - c) * half, 16), half)
    loc = pltpu.make_async_copy(w_ref, gw_ref.at[me], local_sem)

    def copy(k, src, dst, to):
        return pltpu.make_async_remote_copy(src_ref=src, dst_ref=dst, send_sem=send_sems.at[k], recv_sem=recv_sems.at[k],
                                            device_id=to, device_id_type=MESH)

    def out(k):
        px, py = chips[k]
        return copy(k, w_ref.at[mine], gw_ref.at[me, mine], (px, py, c))

    def fwd(k, rows):
        px, py = chips[k]
        return copy(3 + k, gw_ref.at[2 * px + py, rows], gw_ref.at[2 * px + py, rows], (x, y, 1 - c))

    def start():
        loc.start()
        for k in range(3):
            out(k).start()

    def forward():
        for k in range(3):
            px, py = chips[k]
            copy(k, w_ref.at[mine], gw_ref.at[2 * px + py, mine], (px, py, c)).wait_recv()
            fwd(k, mine).start()

    def finish():
        for k in range(3):
            fwd(k, theirs).wait_recv()
        for k in range(3):
            out(k).wait_send()
            fwd(k, mine).wait_send()
        loc.wait()

    return start, forward, finish


GATHER_SCRATCH = [pltpu.SemaphoreType.DMA((6,)), pltpu.SemaphoreType.DMA((6,)), pltpu.SemaphoreType.DMA]


class _Exchange:
    def __init__(self, args, out_shape, scratch, make):
        self.args, self.out_shape, self.scratch, self.make = list(args), list(out_shape), list(scratch), make


def _join_exchanges(a, b):
    na, nao, nas = len(a.args), len(a.out_shape), len(a.scratch)

    def make(ins, outs, sems):
        steps_a = a.make(ins[:na], outs[:nao], sems[:nas])
        steps_b = b.make(ins[na:], outs[nao:], sems[nas:])

        def both(f, g):
            def run():
                f()
                g()
            return run

        return tuple(both(f, g) for f, g in zip(steps_a, steps_b))

    return _Exchange(a.args + b.args, a.out_shape + b.out_shape, a.scratch + b.scratch, make)


def _gather_exchange(wsrc):
    return _Exchange([wsrc], [jax.ShapeDtypeStruct((4,) + wsrc.shape, wsrc.dtype)], GATHER_SCRATCH,
                     lambda ins, outs, sems: _gather_steps(ins[0], outs[0], *sems))


def _launch(body, name, grid, in_specs, out_specs, out_shape, scratch, args, exchange=None, prefetch=0):
    def call(fn, fn_name, ins, outs, shapes, scr, operands, effects):
        spec = pltpu.PrefetchScalarGridSpec(num_scalar_prefetch=prefetch, grid=grid, in_specs=ins, out_specs=outs,
                                            scratch_shapes=scr)
        return pl.pallas_call(fn, name=fn_name, grid_spec=spec, out_shape=shapes,
                              compiler_params=_params(has_side_effects=effects))(*operands)

    if exchange is None:
        return call(body, name, list(in_specs), list(out_specs), list(out_shape), list(scratch), args, False)
    n_in, n_out, ei, eo, ns = len(in_specs), len(out_specs), len(exchange.args), len(exchange.out_shape), len(exchange.scratch)
    nsteps = 1
    for g in grid:
        nsteps *= g

    def wrapped(*refs):
        scalars, refs = refs[:prefetch], refs[prefetch:]
        ins, xin = refs[:n_in], refs[n_in:n_in + ei]
        outs, xout = refs[n_in + ei:n_in + ei + n_out], refs[n_in + ei + n_out:n_in + ei + n_out + eo]
        rest = refs[n_in + ei + n_out + eo:]
        own, sems = rest[:len(rest) - ns], rest[len(rest) - ns:]
        start, forward, finish = exchange.make(xin, xout, sems)
        i = pl.program_id(0)
        for d in range(1, len(grid)):
            i = i * grid[d] + pl.program_id(d)
        pl.when(i == 0)(start)
        body(*scalars, *ins, *outs, *own)
        pl.when(i == max(nsteps - 3, 0))(forward)
        pl.when(i == nsteps - 1)(finish)

    anyspec = pl.BlockSpec(memory_space=pl.ANY)
    return call(wrapped, name + "_x", list(in_specs) + [anyspec] * ei, list(out_specs) + [anyspec] * eo,
                list(out_shape) + exchange.out_shape, list(scratch) + exchange.scratch, (*args, *exchange.args), True)


def _attn_fwd(q, kv, sinks, exchange=None):
    T = kv.shape[0]

    def body(q_ref, kv_ref, s_ref, o_ref):
        for b in range(ATTN_BLOCKS):
            rows = slice(b * BLK, (b + 1) * BLK)
            band, mask, _, _ = _attn_band(kv_ref, ATTN_BLOCKS * pl.program_id(0) + b)
            for g in range(2):
                qs = q_ref[4 * g:4 * g + 4, rows, :].reshape(4 * BLK, HEAD_DIM)
                p, _, den = _attn_scores(band, mask, qs, s_ref, g)
                ot = _mm_tn(band[:, 128:256], p) * (1.0 / den)
                for hh in range(4):
                    o = ot[:, hh * BLK:(hh + 1) * BLK].T
                    o_ref[rows, (4 * g + hh) * 64:(4 * g + hh + 1) * 64] = o[:, g * 64:(g + 1) * 64].astype(BF16)

    tq = ATTN_BLOCKS * BLK
    return _launch(body, "attn_fwd", (T // tq,), [_heads(tq), _full((T, 256)), pl.BlockSpec(memory_space=pltpu.SMEM)],
                   [_rows(tq, 512)], [jax.ShapeDtypeStruct((T, 512), BF16)], [], (q, kv, sinks), exchange)


def _attn_bwd(q, kv, do, sinks, exchange=None):
    T = kv.shape[0]

    def body(q_ref, kv_ref, do_ref, s_ref, dq_ref, dkv_ref, ds_ref):
        @pl.when(pl.program_id(0) == 0)
        def _():
            ds_ref[...] = jnp.zeros_like(ds_ref)

        for b in range(ATTN_BLOCKS):
            rows = slice(b * BLK, (b + 1) * BLK)
            band, mask, cur, prev = _attn_band(kv_ref, ATTN_BLOCKS * pl.program_id(0) + b)
            for g in range(2):
                qs = q_ref[4 * g:4 * g + 4, rows, :].reshape(4 * BLK, HEAD_DIM)
                dos = do_ref[4 * g:4 * g + 4, rows, :].reshape(4 * BLK, HEAD_DIM)
                p, ps, den = _attn_scores(band, mask, qs, s_ref, g)
                inv = 1.0 / den
                p = p * inv
                dpt = _mm_nt(band[:, 128 + g * 64:192 + g * 64], dos)
                delta = jnp.sum(p * dpt, axis=0, keepdims=True)
                dst = p * (dpt - delta)
                dsv = -(ps * inv) * delta
                for hh in range(4):
                    dsink = jnp.sum(dsv[:, hh * BLK:(hh + 1) * BLK], axis=1, keepdims=True)
                    ds_ref[4 * g + hh:4 * g + hh + 1, :] += jnp.broadcast_to(dsink, (1, 128))
                dqt = _mm_tn(band[:, 0:128], dst) * SCALE
                for hh in range(4):
                    dqh = dqt[:, hh * BLK:(hh + 1) * BLK].T
                    dq_ref[rows, (4 * g + hh) * 64:(4 * g + hh + 1) * 64] = dqh[:, g * 64:(g + 1) * 64].astype(BF16)
                dk = _mm(dst, qs) * SCALE
                dv = _mm(p, dos)
                dkv_ref[pl.ds(cur, BLK), g * 64:(g + 1) * 64] = dk[BLK:2 * BLK]
                dkv_ref[pl.ds(cur, BLK), 128 + g * 64:192 + g * 64] = dv[BLK:2 * BLK]
                dkv_ref[pl.ds(prev, BLK), g * 64:(g + 1) * 64] += dk[0:BLK]
                dkv_ref[pl.ds(prev, BLK), 128 + g * 64:192 + g * 64] += dv[0:BLK]

    tq = ATTN_BLOCKS * BLK
    return _launch(body, "attn_bwd", (T // tq,),
                   [_heads(tq), _full((T, 256)), _heads(tq), pl.BlockSpec(memory_space=pltpu.SMEM)],
                   [_rows(tq, 512), _full((T, 256)), _full((8, 128))],
                   [jax.ShapeDtypeStruct((T, 512), BF16), jax.ShapeDtypeStruct((T, 256), F32),
                    jax.ShapeDtypeStruct((8, 128), F32)], [], (q, kv, do, sinks), exchange)


def _rows8(tm, cols):
    return lax.broadcasted_iota(jnp.int32, (tm, cols), 0) & 7


def _lru_gates(xc, wa, ba, wx, bx, lam):
    r = _sigmoid(_mm(xc, wa) + ba)
    ii = _sigmoid(_mm(xc, wx) + bx)
    sp = _softplus(-lam)
    la = -LRU_C * r * sp
    a = jnp.exp(la)
    m = jnp.sqrt(-jnp.tanh(la) * (a * a + 1.0))
    return r, ii, sp, a, m


def _rnn_fwd(xr, gr, cw, cb, wa, ba, wx, bx, lam, exchange=None):
    T = xr.shape[0]
    tm = 512
    C = D_RNN

    def body(xr_ref, gr_ref, cw_ref, cb_ref, wa_ref, ba_ref, wx_ref, bx_ref, lam_ref,
             xc_ref, h_ref, rec_ref, ext, a_s, b_s, carry):
        i = pl.program_id(0)

        @pl.when(i == 0)
        def _():
            ext[...] = jnp.zeros((8, C), F32)
            carry[...] = jnp.zeros((8, C), F32)

        xr = xr_ref[...]
        edge = ext[...]
        xc = cb_ref[...] + cw_ref[3:4, :] * xr
        for k in range(3):
            xc = xc + cw_ref[k:k + 1, :] * _shift_rows(xr, 3 - k, edge)
        ext[...] = xr[tm - 8:tm, :]
        xc_ref[...] = xc
        _, ii, _, a, m = _lru_gates(xc, wa_ref[...], ba_ref[...], wx_ref[...], bx_ref[...], lam_ref[...])
        b = m * ii * xc
        r8 = _rows8(tm, C)
        for d in (1, 2, 4):
            ok = r8 >= d
            a_sh = jnp.where(ok, pltpu.roll(a, d, 0), 1.0)
            b_sh = jnp.where(ok, pltpu.roll(b, d, 0), 0.0)
            b = a * b_sh + b
            a = a * a_sh
        a_s[...] = a
        b_s[...] = b

        def step(g, hin):
            s = pl.multiple_of(g * 8, 8)
            hg = a_s[pl.ds(s, 8), :] * hin + b_s[pl.ds(s, 8), :]
            h_ref[pl.ds(s, 8), :] = hg
            return jnp.broadcast_to(hg[7:8, :], (8, C))

        carry[...] = lax.fori_loop(0, tm // 8, step, carry[...], unroll=4)
        ge, _ = _gelu(gr_ref[...])
        rec_ref[...] = (h_ref[...] * ge).astype(BF16)

    vec = _full((1, C))
    in_specs = [_rows(tm, C), _rows(tm, C), _full((4, C)), vec, _full((C, C)), vec, _full((C, C)), vec, vec]
    out_specs = [_rows(tm, C), _rows(tm, C), _rows(tm, C)]
    out_shape = [jax.ShapeDtypeStruct((T, C), F32), jax.ShapeDtypeStruct((T, C), F32), jax.ShapeDtypeStruct((T, C), BF16)]
    scratch = [pltpu.VMEM((8, C), F32), pltpu.VMEM((tm, C), F32), pltpu.VMEM((tm, C), F32), pltpu.VMEM((8, C), F32)]
    return _launch(body, "rnn_fwd", (T // tm,), in_specs, out_specs, out_shape, scratch,
                   (xr, gr, cw, cb, wa, ba, wx, bx, lam), exchange)


def _rnn_bwd(drec, gr, h, xc, xr, cw, wa, ba, wx, bx, lam, exchange=None):
    T = xr.shape[0]
    tm = 512
    C = D_RNN
    nt = T // tm
    t8 = tm // 8

    def body(drec_ref, gr_ref, h_ref, hp_ref, xc_ref, xr_ref, cw_ref, wa_ref, ba_ref, wx_ref, bx_ref,
             lam_ref, dxr_ref, dgr_ref, dwa_ref, dwx_ref, dvec_ref, c_s, g_s, gout, ext, anext, gcarry):
        i = pl.program_id(0)
        j = nt - 1 - i

        @pl.when(i == 0)
        def _():
            dwa_ref[...] = jnp.zeros_like(dwa_ref)
            dwx_ref[...] = jnp.zeros_like(dwx_ref)
            dvec_ref[...] = jnp.zeros_like(dvec_ref)
            anext[...] = jnp.zeros((8, C), F32)
            gcarry[...] = jnp.zeros((8, C), F32)
            ext[...] = jnp.zeros((8, C), F32)

        xc = xc_ref[...]
        lam = lam_ref[...]
        r, ii, sp, a, m = _lru_gates(xc, wa_ref[...], ba_ref[...], wx_ref[...], bx_ref[...], lam)
        ge, dge = _gelu(gr_ref[...])
        drec = drec_ref[...]
        hh = h_ref[...]
        dgr_ref[...] = (drec * hh * dge).astype(BF16)
        dh = drec * ge
        rowi = lax.broadcasted_iota(jnp.int32, (tm, C), 0)
        c = jnp.where(rowi == tm - 1, jnp.broadcast_to(anext[0:1, :], (tm, C)), pltpu.roll(a, tm - 1, 0))
        anext[...] = a[0:8, :]
        r8 = rowi & 7
        gg = dh
        for d in (1, 2, 4):
            ok = r8 < 8 - d
            c_sh = jnp.where(ok, pltpu.roll(c, tm - d, 0), 1.0)
            g_sh = jnp.where(ok, pltpu.roll(gg, tm - d, 0), 0.0)
            gg = c * g_sh + gg
            c = c * c_sh
        c_s[...] = c
        g_s[...] = gg

        def step(k, gin):
            s = pl.multiple_of((t8 - 1 - k) * 8, 8)
            og = c_s[pl.ds(s, 8), :] * gin + g_s[pl.ds(s, 8), :]
            gout[pl.ds(s, 8), :] = og
            return jnp.broadcast_to(og[0:1, :], (8, C))

        gcarry[...] = lax.fori_loop(0, t8, step, gcarry[...], unroll=4)
        G = gout[...]
        hprev_row = jnp.where(j > 0, hp_ref[7:8, :], 0.0)
        hprev = jnp.where(rowi == 0, jnp.broadcast_to(hprev_row, (tm, C)), pltpu.roll(hh, 1, 0))
        da = G * hprev
        dm = G * ii * xc
        di = G * m * xc
        dxc = G * m * ii
        dla = da * a - dm * a * a / m
        dr = dla * (-LRU_C * sp)
        dsp = _colsum(dla * (-LRU_C * r))
        dlam = dsp * (-_sigmoid(-lam))
        dpr = dr * r * (1.0 - r)
        dpi = di * ii * (1.0 - ii)
        dxc = dxc + _mm_nt(dpr, wa_ref[...]) + _mm_nt(dpi, wx_ref[...])
        dwa_ref[...] += _mm_tn(xc, dpr)
        dwx_ref[...] += _mm_tn(xc, dpi)
        dvec_ref[0:1, :] += _colsum(dpr)
        dvec_ref[1:2, :] += _colsum(dpi)
        dvec_ref[2:3, :] += dlam
        dvec_ref[3:4, :] += _colsum(dxc)
        edge = ext[...]
        xr = xr_ref[...]
        dxr = cw_ref[3:4, :] * dxc
        dvec_ref[7:8, :] += _colsum(dxc * xr)
        for k in range(3):
            up = _shift_rows(dxc, k - 3, edge)
            dxr = dxr + cw_ref[k:k + 1, :] * up
            dvec_ref[4 + k:5 + k, :] += _colsum(up * xr)
        ext[...] = dxc[0:8, :]
        dxr_ref[...] = dxr.astype(BF16)

    rev = lambda i: nt - 1 - i
    prev8 = lambda i: jnp.maximum((nt - 1 - i) * t8 - 1, 0)
    vec = _full((1, C))
    return _launch(
        body, "rnn_bwd", (nt,),
        [_rows(tm, C, rev), _rows(tm, C, rev), _rows(tm, C, rev), _rows(8, C, prev8), _rows(tm, C, rev),
         _rows(tm, C, rev), _full((4, C)), _full((C, C)), vec, _full((C, C)), vec, vec],
        [_rows(tm, C, rev), _rows(tm, C, rev), _full((C, C)), _full((C, C)), _full((8, C))],
        [jax.ShapeDtypeStruct((T, C), BF16), jax.ShapeDtypeStruct((T, C), BF16),
         jax.ShapeDtypeStruct((C, C), F32), jax.ShapeDtypeStruct((C, C), F32), jax.ShapeDtypeStruct((8, C), F32)],
        [pltpu.VMEM((tm, C), F32), pltpu.VMEM((tm, C), F32), pltpu.VMEM((tm, C), F32),
         pltpu.VMEM((8, C), F32), pltpu.VMEM((8, C), F32), pltpu.VMEM((8, C), F32)],
        (drec, gr, h, h, xc, xr, cw, wa, ba, wx, bx, lam), exchange)


def _out_proj(att, rec, x, w_out, g1, b1):
    T = x.shape[0]
    tm = min(1024, T)

    def body(att_ref, rec_ref, x_ref, w_ref, g1_ref, b1_ref, z_ref, h_ref):
        mix = _mm(att_ref[...], w_ref[0:512, :]) + _mm(rec_ref[...], w_ref[512:1024, :])
        z1 = ALPHA * x_ref[...] + mix
        z_ref[...] = z1
        h1, _, _ = _ln(z1, g1_ref[...], b1_ref[...])
        h_ref[...] = h1.astype(MXU_DTYPE).astype(BF16)

    return pl.pallas_call(
        body, name="out_proj", grid=(T // tm,),
        in_specs=[_rows(tm, 512), _rows(tm, 512), _rows(tm, D), _full((D, D)), _full((1, D)), _full((1, D))],
        out_specs=[_rows(tm, D), _rows(tm, D)],
        out_shape=[jax.ShapeDtypeStruct((T, D), F32), jax.ShapeDtypeStruct((T, D), BF16)],
        compiler_params=_params(),
    )(att, rec, x, w_out, g1, b1)


NC = D_FF // FF_CHUNK


def _ffn_up(h1b, w_up_t, fcw, fcb, exchange=None):
    T = h1b.shape[0]
    tm = min(1024, T)
    CW = FF_CHUNK

    def body(h_ref, wg_ref, wv_ref, fcw_ref, fcb_ref, gate_ref, ge_ref, vd_ref, act_ref, before):
        i = pl.program_id(1)

        @pl.when(i == 0)
        def _():
            before[...] = jnp.zeros((8, CW), F32)

        hb = h_ref[...]
        gate = _mm_nt(hb, wg_ref[...])
        val = _mm_nt(hb, wv_ref[...])
        gate_ref[...] = gate.astype(BF16)
        edge = before[...]
        gc = (fcb_ref[...] + fcw_ref[0:1, :] * _shift_rows(gate, 2, edge) + fcw_ref[1:2, :] * _shift_rows(gate, 1, edge)
              + fcw_ref[2:3, :] * gate)
        before[...] = gate[tm - 8:tm, :]
        ge, dge = _gelu(gc)
        ge_ref[...] = ge.astype(BF16)
        vd_ref[...] = (val * dge).astype(BF16)
        act_ref[...] = (ge * val).astype(BF16)

    chunk = pl.BlockSpec((None, tm, CW), lambda c, i: (c, i, 0))
    return _launch(
        body, "ffn_up", (NC, T // tm),
        [pl.BlockSpec((tm, D), lambda c, i: (i, 0)), pl.BlockSpec((CW, D), lambda c, i: (c, 0)),
         pl.BlockSpec((CW, D), lambda c, i: (NC + c, 0)), pl.BlockSpec((None, 3, CW), lambda c, i: (c, 0, 0)),
         pl.BlockSpec((None, 1, CW), lambda c, i: (c, 0, 0))],
        [chunk] * 4, [jax.ShapeDtypeStruct((NC, T, CW), BF16)] * 4, [pltpu.VMEM((8, CW), F32)],
        (h1b, w_up_t, w_up_t, fcw, fcb), exchange)


def _ffn_down(act, z1, p, tgt, w_down, w_g, w_p_t, g1, b1, g2, b2, bg):
    T = z1.shape[0]
    tm = 512

    def body(act_ref, z_ref, p_ref, t_ref, wdn_hbm, wg_hbm, wp_hbm, g1_ref, b1_ref, g2_ref, b2_ref, bg_ref,
             dz2_ref, dz2b_ref, dpre_ref, dpp_ref, vec_ref, wdn, wg, wp):
        @pl.when(pl.program_id(0) == 0)
        def _():
            pltpu.sync_copy(wdn_hbm, wdn)
            pltpu.sync_copy(wg_hbm, wg)
            pltpu.sync_copy(wp_hbm, wp)
            vec_ref[...] = jnp.zeros_like(vec_ref)

        g2v = g2_ref[...]
        h1, _, _ = _ln(z_ref[...], g1_ref[...], b1_ref[...])
        h1b = h1.astype(MXU_DTYPE)
        ffn = _mm(act_ref[0], wdn[0:FF_CHUNK, :])
        for c in range(1, NC):
            ffn = ffn + _mm(act_ref[c], wdn[c * FF_CHUNK:(c + 1) * FF_CHUNK, :])
        sg = _sigmoid(_mm(h1b, wg[...]) + bg_ref[...])
        pp = _mm_nt(p_ref[...], wp[...])
        z2 = ALPHA * h1 + ffn + sg * pp
        y, xh2, rstd2 = _ln(z2, g2v, b2_ref[...])
        diff = y - t_ref[...]
        dy = diff * (1.0 / D)
        dz2 = _ln_bwd(dy, xh2, rstd2, g2v)
        dpre = dz2 * pp * sg * (1.0 - sg)
        dz2_ref[...] = dz2
        dz2b_ref[...] = dz2.astype(BF16)
        dpre_ref[...] = dpre.astype(BF16)
        dpp_ref[...] = (dz2 * sg).astype(BF16)
        loss = 0.5 * jnp.sum(jnp.sum(diff * diff, axis=1, keepdims=True), axis=0, keepdims=True) * (1.0 / D)
        vec_ref[0:1, :] += jnp.broadcast_to(loss, (1, D))
        vec_ref[1:2, :] += _colsum(dy * xh2)
        vec_ref[2:3, :] += _colsum(dy)
        vec_ref[3:4, :] += _colsum(dpre)

    anyspec = pl.BlockSpec(memory_space=pl.ANY)
    vec = _full((1, D))
    return pl.pallas_call(
        body, name="ffn_down", grid=(T // tm,),
        in_specs=[pl.BlockSpec((NC, tm, FF_CHUNK), lambda i: (0, i, 0)), _rows(tm, D), _rows(tm, PLE), _rows(tm, D),
                  anyspec, anyspec, anyspec] + [vec] * 5,
        out_specs=[_rows(tm, D)] * 4 + [_full((8, D))],
        out_shape=[jax.ShapeDtypeStruct((T, D), F32)] + [jax.ShapeDtypeStruct((T, D), BF16)] * 3
                  + [jax.ShapeDtypeStruct((8, D), F32)],
        scratch_shapes=[pltpu.VMEM((D_FF, D), MXU_DTYPE), pltpu.VMEM((D, D), MXU_DTYPE), pltpu.VMEM((D, PLE), MXU_DTYPE)],
        compiler_params=_params(),
    )(act, z1, p, tgt, w_down, w_g, w_p_t, g1, b1, g2, b2, bg)


def _ffn_bwd(dz2b, gate, ge, vd, w_down, fcw):
    T = dz2b.shape[0]
    tm = min(1024, T)
    CW = FF_CHUNK
    nt = T // tm

    def body(dz_ref, wdn_ref, gate_ref, ge_ref, vd_ref, fcw_ref, dup_ref, dfc_ref, after):
        i = pl.program_id(1)

        @pl.when(i == 0)
        def _():
            after[...] = jnp.zeros((8, CW), F32)
            dfc_ref[...] = jnp.zeros_like(dfc_ref)

        gate = gate_ref[...].astype(F32)
        dact = _mm_nt(dz_ref[...], wdn_ref[...])
        dgc = dact * vd_ref[...].astype(F32)
        edge = after[...]
        dgc1 = _shift_rows(dgc, -1, edge)
        dgc2 = _shift_rows(dgc, -2, edge)
        after[...] = dgc[0:8, :]
        dup_ref[0] = (fcw_ref[2:3, :] * dgc + fcw_ref[1:2, :] * dgc1 + fcw_ref[0:1, :] * dgc2).astype(BF16)
        dup_ref[1] = (dact * ge_ref[...].astype(F32)).astype(BF16)
        dfc_ref[0:1, :] += _colsum(dgc2 * gate)
        dfc_ref[1:2, :] += _colsum(dgc1 * gate)
        dfc_ref[2:3, :] += _colsum(dgc * gate)
        dfc_ref[3:4, :] += _colsum(dgc)

    rev = lambda c, i: (c, nt - 1 - i, 0)
    chunk = pl.BlockSpec((None, tm, CW), rev)
    return pl.pallas_call(
        body, name="ffn_bwd", grid=(NC, nt),
        in_specs=[pl.BlockSpec((tm, D), lambda c, i: (nt - 1 - i, 0)), pl.BlockSpec((CW, D), lambda c, i: (c, 0)),
                  chunk, chunk, chunk, pl.BlockSpec((None, 3, CW), lambda c, i: (c, 0, 0))],
        out_specs=[pl.BlockSpec((None, 2, tm, CW), lambda c, i: (c, 0, nt - 1 - i, 0)),
                   pl.BlockSpec((None, 8, CW), lambda c, i: (c, 0, 0))],
        out_shape=[jax.ShapeDtypeStruct((NC, 2, T, CW), BF16), jax.ShapeDtypeStruct((NC, 8, CW), F32)],
        scratch_shapes=[pltpu.VMEM((8, CW), F32)],
        compiler_params=_params(),
    )(dz2b, w_down, gate, ge, vd, fcw)


def _ffn_dh1(dup, dz2, dpre, z1, w_up_t, w_g, g1, b1):
    T = z1.shape[0]
    tm = 512

    def body(dup_ref, dz2_ref, dpre_ref, z_ref, wup_hbm, wg_hbm, g1_ref, b1_ref, dz1_ref, vec_ref, wup, wg):
        @pl.when(pl.program_id(0) == 0)
        def _():
            pltpu.sync_copy(wup_hbm, wup)
            pltpu.sync_copy(wg_hbm, wg)
            vec_ref[...] = jnp.zeros_like(vec_ref)

        g1v = g1_ref[...]
        _, xh1, rstd1 = _ln(z_ref[...], g1v, b1_ref[...])
        dh1 = ALPHA * dz2_ref[...] + _mm_nt(dpre_ref[...], wg[...])
        for c in range(NC):
            for s in range(2):
                r0 = s * D_FF + c * FF_CHUNK
                dh1 = dh1 + _mm(dup_ref[c, s], wup[r0:r0 + FF_CHUNK, :])
        dz1_ref[...] = _ln_bwd(dh1, xh1, rstd1, g1v)
        vec_ref[0:1, :] += _colsum(dh1 * xh1)
        vec_ref[1:2, :] += _colsum(dh1)

    anyspec = pl.BlockSpec(memory_space=pl.ANY)
    vec = _full((1, D))
    return pl.pallas_call(
        body, name="ffn_dh1", grid=(T // tm,),
        in_specs=[pl.BlockSpec((NC, 2, tm, FF_CHUNK), lambda i: (0, 0, i, 0)), _rows(tm, D), _rows(tm, D), _rows(tm, D),
                  anyspec, anyspec, vec, vec],
        out_specs=[_rows(tm, D), _full((8, D))],
        out_shape=[jax.ShapeDtypeStruct((T, D), F32), jax.ShapeDtypeStruct((8, D), F32)],
        scratch_shapes=[pltpu.VMEM((2 * D_FF, D), MXU_DTYPE), pltpu.VMEM((D, D), MXU_DTYPE)],
        compiler_params=_params(),
    )(dup, dz2, dpre, z1, w_up_t, w_g, g1, b1)


def _out_proj_bwd(dz1, w_out, exchange=None):
    T = dz1.shape[0]
    tm = min(1024, T)

    def body(dz_ref, w_ref, datt_ref, drec_ref):
        dzb = dz_ref[...].astype(MXU_DTYPE)
        datt = _mm_nt(dzb, w_ref[0:512, :])
        for h in range(HEADS):
            datt_ref[h] = datt[:, h * 64:(h + 1) * 64].astype(BF16)
        drec_ref[...] = _mm_nt(dzb, w_ref[512:1024, :])

    return _launch(body, "out_proj_bwd", (T // tm,), [_rows(tm, D), _full((D, D))], [_heads(tm), _rows(tm, 512)],
                   [jax.ShapeDtypeStruct((HEADS, T, 64), BF16), jax.ShapeDtypeStruct((T, 512), F32)], [],
                   (dz1, w_out), exchange)


def _in_proj_bwd(dq, dkv, dxr, dgr, dz1, w_in_t, exchange=None):
    T = dz1.shape[0]
    tm = 512
    W = D_IN // 4

    def body(dq_ref, dkv_ref, dxr_ref, dgr_ref, dz_ref, w_ref, dx_ref, du_ref):
        dkv = dkv_ref[...]
        dx_ref[...] = (ALPHA * dz_ref[...] + _mm(dq_ref[...], w_ref[0:512, :]) + _mm(dkv, w_ref[512:768, :])
                       + _mm(dxr_ref[...], w_ref[768:1280, :]) + _mm(dgr_ref[...], w_ref[1280:1792, :]))
        dq, dxr, dgr = dq_ref[...].astype(F32), dxr_ref[...].astype(F32), dgr_ref[...].astype(F32)
        du_ref[0] = dq[:, 0:W].astype(BF16)
        du_ref[1, :, 0:64] = dq[:, W:512].astype(BF16)
        du_ref[1, :, 64:320] = dkv.astype(BF16)
        du_ref[1, :, 320:W] = dxr[:, 0:128].astype(BF16)
        du_ref[2, :, 0:384] = dxr[:, 128:512].astype(BF16)
        du_ref[2, :, 384:W] = dgr[:, 0:64].astype(BF16)
        du_ref[3] = dgr[:, 64:512].astype(BF16)

    return _launch(body, "in_proj_bwd", (T // tm,),
                   [_rows(tm, 512), _rows(tm, 256), _rows(tm, 512), _rows(tm, 512), _rows(tm, D), _full((D_IN, D))],
                   [_rows(tm, D), pl.BlockSpec((4, tm, W), lambda i: (0, i, 0))],
                   [jax.ShapeDtypeStruct((T, D), F32), jax.ShapeDtypeStruct((4, T, W), BF16)], [],
                   (dq, dkv, dxr, dgr, dz1, w_in_t), exchange)


def _accumulate_tn(a_ref, b_ref, o_ref):
    @pl.when(pl.program_id(1) == 0)
    def _():
        o_ref[...] = jnp.zeros_like(o_ref)

    o_ref[...] += _mm_tn(a_ref[...], b_ref[...])


def _weight_grad_cols(a, b, name, n_blocks, b_spec, out_shape, out_spec, exchange=None):
    T, M = a.shape
    bt = min(DW_TOKENS, T)
    return _launch(functools.partial(_accumulate_tn), name, (n_blocks, T // bt),
                   [pl.BlockSpec((bt, M), lambda m, k: (k, 0)), b_spec(bt)], [out_spec],
                   [jax.ShapeDtypeStruct(out_shape, F32)], [], (a, b), exchange)


def _dw_out(att, rec, dz1):
    T = dz1.shape[0]
    bt = min(DW_TOKENS // 2, T)

    def body(att_ref, rec_ref, dz_ref, o_ref):
        @pl.when(pl.program_id(0) == 0)
        def _():
            o_ref[...] = jnp.zeros_like(o_ref)

        dz = dz_ref[...].astype(MXU_DTYPE)
        o_ref[0:512, :] += _mm_tn(att_ref[...], dz)
        o_ref[512:1024, :] += _mm_tn(rec_ref[...], dz)

    return pl.pallas_call(
        body, name="dw_out", grid=(T // bt,), in_specs=[_rows(bt, 512), _rows(bt, 512), _rows(bt, D)],
        out_specs=_full((D, D)), out_shape=jax.ShapeDtypeStruct((D, D), F32), compiler_params=_params())(att, rec, dz1)


def _weight_grad(a, b, bm, name, exchange=None):
    bt = min(DW_TOKENS // 2 if b.dtype == F32 else DW_TOKENS, b.shape[0])
    if a.ndim == 3:
        assert a.shape[2] == bm
        T, M = a.shape[1], a.shape[0] * bm
        a_spec = pl.BlockSpec((None, bt, bm), lambda m, k: (m, k, 0))
    else:
        T, M = a.shape
        a_spec = pl.BlockSpec((bt, bm), lambda m, k: (k, m))
    N = b.shape[1]
    nk = T // bt

    out = _launch(functools.partial(_accumulate_tn), name, (M // bm, nk),
                  [a_spec, pl.BlockSpec((bt, N), lambda m, k: (k, 0))], [pl.BlockSpec((bm, N), lambda m, k: (m, 0))],
                  [jax.ShapeDtypeStruct((M, N), F32)], [], (a, b), exchange)
    return out[0] if exchange is None else out


def _adamw(w, g, m, v, name):
    R, C = w.shape
    tr = R // 8 if R % 64 == 0 else R
    c1 = 1.0 / (1.0 - ADAM_B1 ** ADAM_STEP)
    c2 = 1.0 / (1.0 - ADAM_B2 ** ADAM_STEP)

    def body(w_ref, g_ref, m_ref, v_ref, d_ref, nm_ref, nv_ref):
        g = g_ref[...]
        nm = ADAM_B1 * m_ref[...] + (1.0 - ADAM_B1) * g
        nv = ADAM_B2 * v_ref[...] + (1.0 - ADAM_B2) * g * g
        nm_ref[...] = nm
        nv_ref[...] = nv
        d_ref[...] = -ADAM_LR * ((nm * c1) / (jnp.sqrt(nv * c2) + ADAM_EPS) + ADAM_WD * w_ref[...])

    spec = pl.BlockSpec((tr, C), lambda i: (i, 0))
    return pl.pallas_call(
        body, name=name, grid=(R // tr,),
        in_specs=[spec] * 4, out_specs=[spec] * 3,
        out_shape=[jax.ShapeDtypeStruct((R, C), F32)] * 3,
        compiler_params=_params(),
    )(w, g, m, v)


def _adamw_halves(ws, mines, sibs, ms, vs, c, name, exchange=None):
    n, nb = len(ws), 4
    c1 = 1.0 / (1.0 - ADAM_B1 ** ADAM_STEP)
    c2 = 1.0 / (1.0 - ADAM_B2 ** ADAM_STEP)

    def body(c_ref, *refs):
        own = (pl.program_id(0) // nb) == c_ref[0]
        for i in range(n):
            w_ref, a_ref, b_ref, m_ref, v_ref = refs[5 * i:5 * i + 5]
            g_ref, d_ref, nm_ref, nv_ref = refs[5 * n + 4 * i:5 * n + 4 * i + 4]
            g = jnp.where(own, a_ref[...], b_ref[...])
            nm = ADAM_B1 * m_ref[...] + (1.0 - ADAM_B1) * g
            nv = ADAM_B2 * v_ref[...] + (1.0 - ADAM_B2) * g * g
            g_ref[...] = g
            nm_ref[...] = nm
            nv_ref[...] = nv
            d_ref[...] = -ADAM_LR * ((nm * c1) / (jnp.sqrt(nv * c2) + ADAM_EPS) + ADAM_WD * w_ref[...])

    in_specs, out_specs, out_shape, args = [], [], [], []
    for w, a, b, m, v in zip(ws, mines, sibs, ms, vs):
        R, C = w.shape
        tr = R // (2 * nb)
        assert tr % 8 == 0 and a.shape == (R // 2, C)
        full = pl.BlockSpec((tr, C), lambda i, c_ref: (i, 0))
        half = pl.BlockSpec((tr, C), lambda i, c_ref: (i % nb, 0))
        in_specs += [full, half, half, full, full]
        out_specs += [full] * 4
        out_shape += [jax.ShapeDtypeStruct((R, C), F32)] * 4
        args += [w, a, b, m, v]
    out = _launch(body, name, (2 * nb,), in_specs, out_specs, out_shape, [], (c, *args), exchange, prefetch=1)
    return [tuple(out[4 * i:4 * i + 4]) for i in range(n)], list(out[4 * n:])


def _add4(fs, name):
    n = len(fs)

    def body(*refs):
        for a_ref, o_ref in zip(refs[:n], refs[n:]):
            o_ref[...] = ((a_ref[0].astype(F32) + a_ref[1].astype(F32)) + a_ref[2].astype(F32)) + a_ref[3].astype(F32)

    for f in fs:
        assert (f.shape[1] // 2) % 16 == 0
    return pl.pallas_call(
        body, name=name, grid=(2,),
        in_specs=[pl.BlockSpec((4, f.shape[1] // 2, f.shape[2]), lambda i: (0, i, 0)) for f in fs],
        out_specs=[pl.BlockSpec((f.shape[1] // 2, f.shape[2]), lambda i: (i, 0)) for f in fs],
        out_shape=[jax.ShapeDtypeStruct(f.shape[1:], F32) for f in fs], compiler_params=_params())(*fs)


def _gather_first(wsrc, cpack):
    def body(w_ref, c_ref, gw_ref, gc_ref, send_sems, recv_sems, local_sem, csend, crecv, clocal):
        x, y, c = _pos()
        me = 2 * x + y
        chips = _other_chips(x, y)
        start, forward, finish = _gather_steps(w_ref, gw_ref, send_sems, recv_sems, local_sem)
        start()
        loc = pltpu.make_async_copy(c_ref, gc_ref.at[me], clocal)
        loc.start()

        def conv_copy(k, slot):
            px, py = chips[k]
            return pltpu.make_async_remote_copy(src_ref=c_ref, dst_ref=gc_ref.at[slot], send_sem=csend.at[k],
                                                recv_sem=crecv.at[k], device_id=(px, py, c), device_id_type=MESH)

        for k in range(3):
            conv_copy(k, me).start()
        forward()
        finish()
        for k, (px, py) in enumerate(chips):
            conv_copy(k, 2 * px + py).wait_recv()
        for k in range(3):
            conv_copy(k, me).wait_send()
        loc.wait()

    anyspec = pl.BlockSpec(memory_space=pl.ANY)
    return pl.pallas_call(
        body, name="gather_first",
        in_specs=[anyspec, anyspec], out_specs=[anyspec, anyspec],
        out_shape=[jax.ShapeDtypeStruct((4,) + wsrc.shape, wsrc.dtype), jax.ShapeDtypeStruct((4,) + cpack.shape, cpack.dtype)],
        scratch_shapes=GATHER_SCRATCH + [pltpu.SemaphoreType.DMA((3,)), pltpu.SemaphoreType.DMA((3,)), pltpu.SemaphoreType.DMA],
        compiler_params=_params(has_side_effects=True),
    )(wsrc, cpack)


def _all_devices_exchange(s):
    def make(ins, outs, sems):
        s_ref, o_ref = ins[0], outs[0]
        send_sems, recv_sems, local_sem = sems
        x, y, c = _pos()
        me = 4 * x + 2 * y + c
        loc = pltpu.make_async_copy(s_ref, o_ref.at[me], local_sem)

        def copy(k, slot):
            peer = (x ^ (k >> 2), y ^ ((k >> 1) & 1), c ^ (k & 1))
            return pltpu.make_async_remote_copy(src_ref=s_ref, dst_ref=o_ref.at[slot], send_sem=send_sems.at[k - 1],
                                                recv_sem=recv_sems.at[k - 1], device_id=peer, device_id_type=MESH)

        def start():
            loc.start()
            for k in range(1, 8):
                copy(k, me).start()

        def finish():
            for k in range(1, 8):
                copy(k, 4 * (x ^ (k >> 2)) + 2 * (y ^ ((k >> 1) & 1)) + (c ^ (k & 1))).wait_recv()
            for k in range(1, 8):
                copy(k, me).wait_send()
            loc.wait()

        return start, lambda: None, finish

    return _Exchange([s], [jax.ShapeDtypeStruct((8,) + s.shape, s.dtype)],
                     [pltpu.SemaphoreType.DMA((7,)), pltpu.SemaphoreType.DMA((7,)), pltpu.SemaphoreType.DMA], make)


def _sum_devices(a):
    def body(a_ref, o_ref):
        acc = a_ref[0]
        for d in range(1, 8):
            acc = acc + a_ref[d]
        o_ref[...] = acc

    vm = pl.BlockSpec(memory_space=pltpu.VMEM)
    return pl.pallas_call(body, name="sum_devices", in_specs=[vm], out_specs=vm,
                          out_shape=jax.ShapeDtypeStruct(a.shape[1:], F32), compiler_params=_params())(a)


def _swap_exchange(gs):
    n = len(gs)

    def make(ins, outs, sems):
        x, y, c = _pos()
        cps = []
        for i in range(n):
            half = gs[i].shape[1] // 2
            rows = pl.ds(pl.multiple_of((1 - c) * half, 8), half)
            cps.append(pltpu.make_async_remote_copy(src_ref=ins[i].at[:, rows, :], dst_ref=outs[i], send_sem=sems[0].at[i],
                                                    recv_sem=sems[1].at[i], device_id=(x, y, 1 - c), device_id_type=MESH))

        def start():
            for cp in cps:
                cp.start()

        def finish():
            for cp in cps:
                cp.wait()

        return start, lambda: None, finish

    return _Exchange(gs, [jax.ShapeDtypeStruct((4, g.shape[1] // 2, g.shape[2]), g.dtype) for g in gs],
                     [pltpu.SemaphoreType.DMA((n,)), pltpu.SemaphoreType.DMA((n,))], make)


def _scatter_exchange(ss):
    n = len(ss)

    def make(ins, outs, sems):
        send_sems, recv_sems, local_sems = sems
        x, y, c = _pos()
        me = 2 * x + y
        chips = _other_chips(x, y)
        locs = [pltpu.make_async_copy(ins[i].at[me], outs[i].at[me], local_sems.at[i]) for i in range(n)]

        def copy(i, k, src_slot, dst_slot):
            px, py = chips[k]
            return pltpu.make_async_remote_copy(src_ref=ins[i].at[src_slot], dst_ref=outs[i].at[dst_slot],
                                                send_sem=send_sems.at[3 * i + k], recv_sem=recv_sems.at[3 * i + k],
                                                device_id=(px, py, c), device_id_type=MESH)

        def start():
            for i in range(n):
                locs[i].start()
                for k, (px, py) in enumerate(chips):
                    copy(i, k, 2 * px + py, me).start()

        def finish():
            for i in range(n):
                for k, (px, py) in enumerate(chips):
                    copy(i, k, me, 2 * px + py).wait_recv()
            for i in range(n):
                for k, (px, py) in enumerate(chips):
                    copy(i, k, 2 * px + py, me).wait_send()
                locs[i].wait()

        return start, lambda: None, finish

    return _Exchange(ss, [jax.ShapeDtypeStruct(s.shape, s.dtype) for s in ss],
                     [pltpu.SemaphoreType.DMA((3 * n,)), pltpu.SemaphoreType.DMA((3 * n,)), pltpu.SemaphoreType.DMA((n,))], make)


def _send_exchange(rs):
    n = len(rs)

    def make(ins, outs, sems):
        x, y, c = _pos()
        cps = [pltpu.make_async_remote_copy(src_ref=ins[i], dst_ref=outs[i], send_sem=sems[0].at[i], recv_sem=sems[1].at[i],
                                            device_id=(x, y, 1 - c), device_id_type=MESH) for i in range(n)]

        def start():
            for cp in cps:
                cp.start()

        def finish():
            for cp in cps:
                cp.wait()

        return start, lambda: None, finish

    return _Exchange(rs, [jax.ShapeDtypeStruct(r.shape, r.dtype) for r in rs],
                     [pltpu.SemaphoreType.DMA((n,)), pltpu.SemaphoreType.DMA((n,))], make)


def _run_exchange(ex, name):
    ei, eo = len(ex.args), len(ex.out_shape)

    def body(*refs):
        start, forward, finish = ex.make(refs[:ei], refs[ei:ei + eo], refs[ei + eo:])
        start()
        forward()
        finish()

    anyspec = pl.BlockSpec(memory_space=pl.ANY)
    return pl.pallas_call(body, name=name, in_specs=[anyspec] * ei, out_specs=[anyspec] * eo, out_shape=ex.out_shape,
                          scratch_shapes=ex.scratch, compiler_params=_params(has_side_effects=True))(*ex.args)


def _add_half(gs, rs, c, name):
    n = len(gs)

    def body(c_ref, *refs):
        for g_ref, r_ref, o_ref in zip(refs[:n], refs[n:2 * n], refs[2 * n:]):
            o_ref[...] = (g_ref[...] + r_ref[...]).astype(BF16)

    g_specs, r_specs, out_shape = [], [], []
    for g, r in zip(gs, rs):
        _, H, C = r.shape
        tr = H // 2
        assert tr % 16 == 0 and g.shape == (4, 2 * H, C)
        g_specs.append(pl.BlockSpec((1, tr, C), lambda j, i, c_ref: (j, c_ref[0] * 2 + i, 0)))
        r_specs.append(pl.BlockSpec((1, tr, C), lambda j, i, c_ref: (j, i, 0)))
        out_shape.append(jax.ShapeDtypeStruct((4, H, C), BF16))
    grid_spec = pltpu.PrefetchScalarGridSpec(num_scalar_prefetch=1, grid=(4, 2), in_specs=g_specs + r_specs, out_specs=r_specs)
    return pl.pallas_call(body, name=name, grid_spec=grid_spec, out_shape=out_shape, compiler_params=_params())(c, *gs, *rs)


def _block_diag(w):
    eye = jnp.eye(RNN_BLOCKS, dtype=w.dtype)
    return (eye[:, None, :, None] * w[:, :, None, :]).reshape(D_RNN, D_RNN)


def _diag_blocks(wd):
    d = wd.reshape(RNN_BLOCKS, 64, RNN_BLOCKS, 64)
    return jnp.stack([d[h, :, h, :] for h in range(RNN_BLOCKS)])


def _split_pack(a, first, last):
    out, base = {}, PACK_OFF[first]
    for i in range(first, last):
        s = a[:, PACK_OFF[i] - base:PACK_OFF[i + 1] - base]
        out[BIG_KEYS[i]] = s.reshape(4 * 256, 256) if BIG_KEYS[i] == "w_p_t" else s.reshape(-1, 1024)
    return out


def _layer_grads(x, p, tgt, gw, small, shard=None, core=None):
    row = lambda v: v.reshape(1, -1)
    wa = _block_diag(small["gate_a_w"]).astype(MXU_DTYPE)
    wx = _block_diag(small["gate_x_w"]).astype(MXU_DTYPE)
    sinks = small["attn_sinks"].reshape(1, HEADS)

    dist = shard is not None
    q, kv, xr, gr, xb = _in_proj(x, gw["w_in_t"])
    cut = PACK_OFF[1] + PACK_ROWS[1] // 2
    att, *ga = _attn_fwd(q, kv, sinks, _gather_exchange(shard[PACK_OFF[1]:cut]) if dist else None)
    xc, h, rec, *gb = _rnn_fwd(xr, gr, small["rnn_conv_w"], row(small["rnn_conv_b"]), wa, row(small["gate_a_b"]),
                               wx, row(small["gate_x_b"]), row(small["lru_lambda"]),
                               _gather_exchange(shard[cut:PACK_OFF[3]]) if dist else None)
    if dist:
        gw = {**gw, **_split_pack(jnp.concatenate([ga[0], gb[0]], axis=1), 1, 3)}
    g1, b1 = row(small["ln1_g"]), row(small["ln1_b"])
    fcw = small["ffn_conv_w"].reshape(3, NC, FF_CHUNK).transpose(1, 0, 2)
    fcb = small["ffn_conv_b"].reshape(NC, 1, FF_CHUNK)
    z1, h1b = _out_proj(att, rec, x, gw["w_out"], g1, b1)
    gate, ge, vd, act, *gc = _ffn_up(h1b, gw["w_up_t"], fcw, fcb,
                                     _gather_exchange(shard[PACK_OFF[3]:PACK_OFF[6]]) if dist else None)
    if dist:
        gw = {**gw, **_split_pack(gc[0], 3, 6)}
    dz2, dz2b, dpre, dpp, vec2 = _ffn_down(act, z1, p, tgt, gw["w_down"], gw["w_g"], gw["w_p_t"], g1, b1,
                                           row(small["ln2_g"]), row(small["ln2_b"]), row(small["ple_gate_b"]))
    dup, dfc = _ffn_bwd(dz2b, gate, ge, vd, gw["w_down"], fcw)
    dz1, vec1 = _ffn_dh1(dup, dz2, dpre, z1, gw["w_up_t"], gw["w_g"], g1, b1)
    per_chip = 2 * D_FF // 4 // FF_CHUNK
    big = {"w_ffn_up": _weight_grad_cols(
        h1b, dup.reshape(2 * NC, -1, FF_CHUNK), "dw_up", 2 * NC,
        lambda bt: pl.BlockSpec((None, bt, FF_CHUNK), lambda m, k: (m, k, 0)), (4, D, 2 * D_FF // 4),
        pl.BlockSpec((None, D, FF_CHUNK), lambda m, k: (2 * (m % 2) + (m // 2) // per_chip, 0, (m // 2) % per_chip)))[0]}
    g_dn, *got_up = _weight_grad(act, dz2b, 512, "dw_down", _swap_exchange([big["w_ffn_up"]])) if dist else (
        _weight_grad(act, dz2b, 512, "dw_down"),)
    big["w_ffn_down"] = g_dn.reshape(4, D_FF // 4, D)
    big["ple_gate_w"] = _weight_grad(h1b, dpre, 512, "dw_gate").reshape(4, D // 4, D)
    big["ple_proj"] = _weight_grad_cols(
        p, dpp, "dw_proj", 4, lambda bt: pl.BlockSpec((bt, D // 4), lambda j, k: (k, j)),
        (4, PLE, D // 4), pl.BlockSpec((None, PLE, D // 4), lambda j, k: (j, 0, 0)))[0]
    big["w_out"] = _dw_out(att, rec, dz1).reshape(4, D // 4, D)
    reduced = None
    if dist:
        g_ffn = [big[k] for k in EARLY_WEIGHTS]
        ex = _swap_exchange(g_ffn[1:])
    datt, drec, *got = _out_proj_bwd(dz1, gw["w_out"], ex if dist else None)
    if dist:
        sums = _add_half(g_ffn, got_up + got, core, "add_half_ffn")
        ex, ex2 = _scatter_exchange(sums[:1]), _scatter_exchange(sums[1:])
    dxr, dgr, dwa, dwx, dvec, *got = _rnn_bwd(drec, gr, h, xc, xr, small["rnn_conv_w"], wa, row(small["gate_a_b"]),
                                              wx, row(small["gate_x_b"]), row(small["lru_lambda"]), ex if dist else None)
    dq, dkv, dsinks, *got2 = _attn_bwd(q, kv, datt, sinks, ex2 if dist else None)
    if dist:
        mine = _add4(got + got2, "add_chips_ffn")
        big = {}
    sg = {
        "attn_sinks": dsinks[:, 0],
        "rnn_conv_w": dvec[4:8],
        "rnn_conv_b": dvec[3],
        "gate_a_w": _diag_blocks(dwa),
        "gate_a_b": dvec[0],
        "gate_x_w": _diag_blocks(dwx),
        "gate_x_b": dvec[1],
        "lru_lambda": dvec[2],
        "ln1_g": vec1[0],
        "ln1_b": vec1[1],
        "ffn_conv_w": dfc[:, 0:3].transpose(1, 0, 2).reshape(3, D_FF),
        "ffn_conv_b": dfc[:, 3].reshape(D_FF),
        "ple_gate_b": vec2[3],
        "ln2_g": vec2[1],
        "ln2_b": vec2[2],
    }
    loss = vec2[0, 0:1]
    grad_x, du = _in_proj_bwd(dq, dkv, dxr, dgr, dz1, gw["w_in_t"])
    ex = None
    if dist:
        ex = _join_exchanges(_send_exchange(mine), _all_devices_exchange(_pack_vecs([sg[k] for k in SMALL] + [loss])[0]))
    big["w_in"], *got = _weight_grad_cols(
        xb, du, "dw_in", 4, lambda bt: pl.BlockSpec((None, bt, D_IN // 4), lambda j, k: (j, k, 0)), (4, D, D_IN // 4),
        pl.BlockSpec((None, D, D_IN // 4), lambda j, k: (j, 0, 0)), ex)
    if dist:
        reduced = (mine, got[:len(mine)])
    return grad_x, big, sg, loss, reduced, got[-1:]


BIG = ("w_in", "w_ffn_up", "w_out", "w_ffn_down", "ple_gate_w", "ple_proj")
BIG_KEYS = ("w_in_t", "w_up_t", "w_out", "w_down", "w_g", "w_p_t")
BIG_T = (True, True, False, False, False, True)
EARLY_WEIGHTS = ("w_ffn_up", "w_ffn_down", "ple_gate_w", "ple_proj", "w_out")
LATE_WEIGHTS = ("w_in",)
SMALL = ("attn_sinks", "rnn_conv_w", "rnn_conv_b", "gate_a_w", "gate_a_b", "gate_x_w", "gate_x_b", "lru_lambda",
         "ln1_g", "ln1_b", "ffn_conv_w", "ffn_conv_b", "ple_gate_b", "ln2_g", "ln2_b")
SHARDED_SMALL = ("rnn_conv_w", "ffn_conv_w")
WEIGHTS = ("w_in", "attn_sinks", "rnn_conv_w", "rnn_conv_b", "gate_a_w", "gate_a_b", "gate_x_w", "gate_x_b",
           "lru_lambda", "w_out", "ln1_g", "ln1_b", "w_ffn_up", "ffn_conv_w", "ffn_conv_b", "w_ffn_down",
           "ple_gate_w", "ple_gate_b", "ple_proj", "ln2_g", "ln2_b")


def _pack_big(d, first=0, last=6):
    parts = []
    for name, t in zip(BIG[first:last], BIG_T[first:last]):
        a = d[name]
        a = a.T if t else a
        parts.append(a.reshape(-1, 1024))
    return jnp.concatenate(parts, axis=0)


def _pack_vecs(items):
    parts, offs, n = [], [], 0
    for a in items:
        f = a.reshape(-1).astype(F32)
        pad = (-f.shape[0]) % 128
        parts.append(jnp.pad(f, (0, pad)))
        offs.append(n)
        n += (f.shape[0] + pad) // 128
    padr = (-n) % 8
    if padr:
        parts.append(jnp.zeros((padr * 128,), F32))
    return jnp.concatenate(parts).reshape(-1, 128), offs


def _unpack_vecs(a, offs, shapes):
    flat = a.reshape(-1)
    out = []
    for o, s in zip(offs, shapes):
        n = 1
        for d in s:
            n *= d
        out.append(flat[o * 128:o * 128 + n].reshape(s))
    return out


def kernel(x, p, w_in, attn_sinks, rnn_conv_w, rnn_conv_b, gate_a_w, gate_a_b, gate_x_w, gate_x_b, lru_lambda, w_out, ln1_g, ln1_b, w_ffn_up, ffn_conv_w, ffn_conv_b, w_ffn_down, ple_gate_w, ple_gate_b, ple_proj, ln2_g, ln2_b, loss_target, m_w_in, m_attn_sinks, m_rnn_conv_w, m_rnn_conv_b, m_gate_a_w, m_gate_a_b, m_gate_x_w, m_gate_x_b, m_lru_lambda, m_w_out, m_ln1_g, m_ln1_b, m_w_ffn_up, m_ffn_conv_w, m_ffn_conv_b, m_w_ffn_down, m_ple_gate_w, m_ple_gate_b, m_ple_proj, m_ln2_g, m_ln2_b, v_w_in, v_attn_sinks, v_rnn_conv_w, v_rnn_conv_b, v_gate_a_w, v_gate_a_b, v_gate_x_w, v_gate_x_b, v_lru_lambda, v_w_out, v_ln1_g, v_ln1_b, v_w_ffn_up, v_ffn_conv_w, v_ffn_conv_b, v_w_ffn_down, v_ple_gate_w, v_ple_gate_b, v_ple_proj, v_ln2_g, v_ln2_b):
    w = dict(w_in=w_in, attn_sinks=attn_sinks, rnn_conv_w=rnn_conv_w, rnn_conv_b=rnn_conv_b, gate_a_w=gate_a_w,
             gate_a_b=gate_a_b, gate_x_w=gate_x_w, gate_x_b=gate_x_b, lru_lambda=lru_lambda, w_out=w_out, ln1_g=ln1_g,
             ln1_b=ln1_b, w_ffn_up=w_ffn_up, ffn_conv_w=ffn_conv_w, ffn_conv_b=ffn_conv_b, w_ffn_down=w_ffn_down,
             ple_gate_w=ple_gate_w, ple_gate_b=ple_gate_b, ple_proj=ple_proj, ln2_g=ln2_g, ln2_b=ln2_b)
    m = dict(w_in=m_w_in, attn_sinks=m_attn_sinks, rnn_conv_w=m_rnn_conv_w, rnn_conv_b=m_rnn_conv_b, gate_a_w=m_gate_a_w,
             gate_a_b=m_gate_a_b, gate_x_w=m_gate_x_w, gate_x_b=m_gate_x_b, lru_lambda=m_lru_lambda, w_out=m_w_out,
             ln1_g=m_ln1_g, ln1_b=m_ln1_b, w_ffn_up=m_w_ffn_up, ffn_conv_w=m_ffn_conv_w, ffn_conv_b=m_ffn_conv_b,
             w_ffn_down=m_w_ffn_down, ple_gate_w=m_ple_gate_w, ple_gate_b=m_ple_gate_b, ple_proj=m_ple_proj,
             ln2_g=m_ln2_g, ln2_b=m_ln2_b)
    v = dict(w_in=v_w_in, attn_sinks=v_attn_sinks, rnn_conv_w=v_rnn_conv_w, rnn_conv_b=v_rnn_conv_b, gate_a_w=v_gate_a_w,
             gate_a_b=v_gate_a_b, gate_x_w=v_gate_x_w, gate_x_b=v_gate_x_b, lru_lambda=v_lru_lambda, w_out=v_w_out,
             ln1_g=v_ln1_g, ln1_b=v_ln1_b, w_ffn_up=v_w_ffn_up, ffn_conv_w=v_ffn_conv_w, ffn_conv_b=v_ffn_conv_b,
             w_ffn_down=v_w_ffn_down, ple_gate_w=v_ple_gate_w, ple_gate_b=v_ple_gate_b, ple_proj=v_ple_proj,
             ln2_g=v_ln2_g, ln2_b=v_ln2_b)
    w, m, v = ({k: a[0] for k, a in d.items()} for d in (w, m, v))
    chip = 2 * lax.axis_index("x") + lax.axis_index("y")
    core = lax.axis_index("c")

    wpack = _pack_big(w)
    cpack, _ = _pack_vecs([w["rnn_conv_w"], w["ffn_conv_w"]])
    shard = wpack.astype(MXU_DTYPE)
    g_in, gcp = _gather_first(shard[PACK_OFF[0]:PACK_OFF[1]], cpack)
    gw = _split_pack(g_in, 0, 1)
    small = {k: w[k] for k in SMALL}
    small["rnn_conv_w"] = gcp[:, 0:4].reshape(4, 4, 128).transpose(1, 0, 2).reshape(4, 512)
    small["ffn_conv_w"] = gcp[:, 4:22].reshape(4, 3, 768).transpose(1, 0, 2).reshape(3, 3072)

    core1 = core.reshape(1).astype(jnp.int32)
    grad_x, big, sg, loss, ffn_halves, small_all = _layer_grads(x[0], p[0, 0], loss_target[0], gw, small, shard, core1)

    shapes = [sg[k].shape for k in SMALL] + [(1,)]
    _, offs = _pack_vecs([jnp.zeros(s, F32) for s in shapes])
    red = dict(zip(SMALL + ("loss",), _unpack_vecs(_sum_devices(small_all[0]), offs, shapes)))
    red["rnn_conv_w"] = lax.dynamic_slice_in_dim(red["rnn_conv_w"], chip * 128, 128, axis=1)
    red["ffn_conv_w"] = lax.dynamic_slice_in_dim(red["ffn_conv_w"], chip * 768, 768, axis=1)

    g_late = [big[k] for k in LATE_WEIGHTS]
    sib = _run_exchange(_swap_exchange(g_late), "swap_late")
    from_chips = _run_exchange(_scatter_exchange(_add_half(g_late, sib, core1, "add_half_late")), "scatter_late")
    late_mine = _add4(from_chips, "add_chips_late")
    late_other = _run_exchange(_send_exchange(late_mine), "send_late")

    def adamw(names, mine, other, name):
        out, _ = _adamw_halves([w[k] for k in names], mine, other, [m[k] for k in names], [v[k] for k in names],
                               core1, name)
        return dict(zip(names, out))

    big_out = {**adamw(LATE_WEIGHTS, late_mine, late_other, "adamw_late"), **adamw(EARLY_WEIGHTS, *ffn_halves, "adamw_early")}
    wsm, offs2 = _pack_vecs([w[k] for k in SMALL])
    gsm, _ = _pack_vecs([red[k] for k in SMALL])
    msm, _ = _pack_vecs([m[k] for k in SMALL])
    vsm, _ = _pack_vecs([v[k] for k in SMALL])
    dsm, nmsm, nvsm = _adamw(wsm, gsm, msm, vsm, "adamw_small")
    shapes2 = [w[k].shape for k in SMALL]

    def named(n, smallp):
        d = {k: out[n][None] for k, out in big_out.items()}
        d.update({k: a[None] for k, a in zip(SMALL, _unpack_vecs(smallp, offs2, shapes2))})
        return [d[k] for k in WEIGHTS]

    return (red["loss"].reshape(()), grad_x[None], *named(0, gsm), *named(1, dsm), *named(2, nmsm), *named(3, nvsm))
```

```python
import functools

import jax
import jax.numpy as jnp
from jax import lax
from jax.experimental import pallas as pl
from jax.experimental.pallas import tpu as pltpu

F32 = jnp.float32
BF16 = jnp.bfloat16
MXU_DTYPE = jnp.bfloat16

D = 1024
D_ATT = 512
D_KV = 128
D_RNN = 512
D_IN = 1792
D_FF = 3072
FF_CHUNK = 512
PLE = 256
HEADS = 8
HEAD_DIM = 64
BLK = 128
ATTN_BLOCKS = 4
DW_TOKENS = 4096
RNN_BLOCKS = 8
LN_EPS = 1e-5
LRU_C = 8.0
ALPHA = float(2.0 ** 0.25)
SCALE = HEAD_DIM ** -0.5
NEG = -1e30

ADAM_LR = 0.001
ADAM_B1 = 0.9
ADAM_B2 = 0.999
ADAM_EPS = 1e-08
ADAM_WD = 0.01
ADAM_STEP = 10

VMEM_LIMIT_BYTES = 56 * 1024 * 1024
MESH = pl.DeviceIdType.MESH

PACK_ROWS = (448, 1536, 256, 768, 256, 64)
PACK_OFF = tuple(sum(PACK_ROWS[:i]) for i in range(len(PACK_ROWS) + 1))
PACK_TOTAL = PACK_OFF[-1]


def _params(**kw):
    return pltpu.CompilerParams(vmem_limit_bytes=VMEM_LIMIT_BYTES, **kw)


def _mm(a, b):
    return jnp.dot(a.astype(MXU_DTYPE), b.astype(MXU_DTYPE), preferred_element_type=F32)


def _mm_nt(a, b):
    return lax.dot_general(a.astype(MXU_DTYPE), b.astype(MXU_DTYPE), (((1,), (1,)), ((), ())),
                           preferred_element_type=F32)


def _mm_tn(a, b):
    return lax.dot_general(a.astype(MXU_DTYPE), b.astype(MXU_DTYPE), (((0,), (0,)), ((), ())),
                           preferred_element_type=F32)


def _sigmoid(x):
    return 0.5 + 0.5 * jnp.tanh(0.5 * x)


def _gelu(x):
    c = 0.7978845608028654
    k = 0.044715
    x2 = x * x
    t = jnp.tanh(x * (c + (c * k) * x2))
    h = 0.5 * (1.0 + t)
    return x * h, h * (1.0 + (x * (1.0 - t)) * (c + (3.0 * c * k) * x2))


def _shift_rows(x, s, edge8):
    R = x.shape[0]
    row8 = lax.broadcasted_iota(jnp.int32, (8, x.shape[1]), 0)
    if s > 0:
        rolled = pltpu.roll(x, s, 0)
        first = jnp.where(row8 < s, pltpu.roll(edge8, s, 0), rolled[0:8])
        return jnp.concatenate([first, rolled[8:]], axis=0)
    k = -s
    rolled = pltpu.roll(x, R - k, 0)
    last = jnp.where(row8 >= 8 - k, pltpu.roll(edge8, 8 - k, 0), rolled[R - 8:])
    return jnp.concatenate([rolled[:R - 8], last], axis=0)


def _softplus(x):
    return jnp.maximum(x, 0.0) + jnp.log(1.0 + jnp.exp(-jnp.abs(x)))


def _ln(z, g, b):
    mu = jnp.mean(z, axis=-1, keepdims=True)
    zc = z - mu
    var = jnp.mean(zc * zc, axis=-1, keepdims=True)
    rstd = lax.rsqrt(var + LN_EPS)
    xhat = zc * rstd
    return xhat * g + b, xhat, rstd


def _ln_bwd(dy, xhat, rstd, g):
    dxh = dy * g
    m1 = jnp.mean(dxh, axis=-1, keepdims=True)
    m2 = jnp.mean(dxh * xhat, axis=-1, keepdims=True)
    return rstd * (dxh - m1 - xhat * m2)


def _colsum(x):
    return jnp.sum(x, axis=0, keepdims=True)


def _full(shape):
    nd = len(shape)
    return pl.BlockSpec(shape, lambda *_: (0,) * nd)


def _rows(tm, cols, fn=None):
    if fn is None:
        return pl.BlockSpec((tm, cols), lambda i: (i, 0))
    return pl.BlockSpec((tm, cols), lambda i: (fn(i), 0))


def _heads(tm):
    return pl.BlockSpec((HEADS, tm, HEAD_DIM), lambda i: (0, i, 0))


def _in_proj(x, w_in_t):
    T = x.shape[0]
    tm = min(1024, T)

    def body(x_ref, w_ref, q_ref, kv_ref, xr_ref, gr_ref, xb_ref):
        xb = x_ref[...].astype(MXU_DTYPE)
        xb_ref[...] = xb.astype(BF16)
        q = _mm_nt(xb, w_ref[0:512, :])
        for h in range(HEADS):
            q_ref[h] = q[:, h * 64:(h + 1) * 64].astype(BF16)
        kv_ref[...] = _mm_nt(xb, w_ref[512:768, :]).astype(BF16)
        xr_ref[...] = _mm_nt(xb, w_ref[768:1280, :])
        gr_ref[...] = _mm_nt(xb, w_ref[1280:1792, :])

    return pl.pallas_call(
        body, name="in_proj", grid=(T // tm,),
        in_specs=[_rows(tm, D), _full((D_IN, D))],
        out_specs=[_heads(tm), _rows(tm, 256), _rows(tm, 512), _rows(tm, 512), _rows(tm, D)],
        out_shape=[jax.ShapeDtypeStruct((HEADS, T, 64), BF16), jax.ShapeDtypeStruct((T, 256), BF16),
                   jax.ShapeDtypeStruct((T, 512), F32), jax.ShapeDtypeStruct((T, 512), F32),
                   jax.ShapeDtypeStruct((T, D), BF16)],
        compiler_params=_params(),
    )(x, w_in_t)


def _attn_band(kv_ref, i):
    cur = pl.multiple_of(i * BLK, BLK)
    prev = pl.multiple_of(jnp.maximum(i - 1, 0) * BLK, BLK)
    band = jnp.concatenate([kv_ref[pl.ds(prev, BLK), :], kv_ref[pl.ds(cur, BLK), :]], axis=0)
    key = lax.broadcasted_iota(jnp.int32, (2 * BLK, 4 * BLK), 0)
    qry = lax.broadcasted_iota(jnp.int32, (2 * BLK, 4 * BLK), 1) & (BLK - 1)
    in_prev = jnp.logical_and(jnp.logical_and(key < BLK, key > qry), i > 0)
    mask = jnp.logical_or(in_prev, jnp.logical_and(key >= BLK, key - BLK <= qry))
    return band, mask, cur, prev


def _attn_scores(band, mask, qs, s_ref, g):
    st = jnp.where(mask, _mm_nt(band[:, g * 64:(g + 1) * 64], qs) * SCALE, NEG)
    lane = lax.broadcasted_iota(jnp.int32, (1, 4 * BLK), 1)
    sv = jnp.where(lane < BLK, s_ref[0, 4 * g],
                   jnp.where(lane < 2 * BLK, s_ref[0, 4 * g + 1], jnp.where(lane < 3 * BLK, s_ref[0, 4 * g + 2], s_ref[0, 4 * g + 3])))
    m = jnp.maximum(jnp.max(st, axis=0, keepdims=True), sv)
    p = jnp.exp(st - m)
    ps = jnp.exp(sv - m)
    return p, ps, jnp.sum(p, axis=0, keepdims=True) + ps


def _pos():
    return lax.axis_index("x"), lax.axis_index("y"), lax.axis_index("c")


def _other_chips(x, y):
    return [(1 - x, y), (x, 1 - y), (1 - x, 1 - y)]


def _gather_steps(w_ref, gw_ref, send_sems, recv_sems, local_sem):
    x, y, c = _pos()
    me = 2 * x + y
    chips = _other_chips(x, y)
    half = w_ref.shape[0] // 2
    mine = pl.ds(pl.multiple_of(c * half, 16), half)
    theirs = pl.ds(pl.multiple_of((1 - c) * half, 16), half)
    loc = pltpu.make_async_copy(w_ref, gw_ref.at[me], local_sem)

    def copy(k, src, dst, to):
        return pltpu.make_async_remote_copy(src_ref=src, dst_ref=dst, send_sem=send_sems.at[k], recv_sem=recv_sems.at[k],
                                            device_id=to, device_id_type=MESH)

    def out(k):
        px, py = chips[k]
        return copy(k, w_ref.at[mine], gw_ref.at[me, mine], (px, py, c))

    def fwd(k, rows):
        px, py = chips[k]
        return copy(3 + k, gw_ref.at[2 * px + py, rows], gw_ref.at[2 * px + py, rows], (x, y, 1 - c))

    def start():
        loc.start()
        for k in range(3):
            out(k).start()

    def forward():
        for k in range(3):
            px, py = chips[k]
            copy(k, w_ref.at[mine], gw_ref.at[2 * px + py, mine], (px, py, c)).wait_recv()
            fwd(k, mine).start()

    def finish():
        for k in range(3):
            fwd(k, theirs).wait_recv()
        for k in range(3):
            out(k).wait_send()
            fwd(k, mine).wait_send()
        loc.wait()

    return start, forward, finish


GATHER_SCRATCH = [pltpu.SemaphoreType.DMA((6,)), pltpu.SemaphoreType.DMA((6,)), pltpu.SemaphoreType.DMA]


class _Exchange:
    def __init__(self, args, out_shape, scratch, make):
        self.args, self.out_shape, self.scratch, self.make = list(args), list(out_shape), list(scratch), make


def _join_exchanges(a, b):
    na, nao, nas = len(a.args), len(a.out_shape), len(a.scratch)

    def make(ins, outs, sems):
        steps_a = a.make(ins[:na], outs[:nao], sems[:nas])
        steps_b = b.make(ins[na:], outs[nao:], sems[nas:])

        def both(f, g):
            def run():
                f()
                g()
            return run

        return tuple(both(f, g) for f, g in zip(steps_a, steps_b))

    return _Exchange(a.args + b.args, a.out_shape + b.out_shape, a.scratch + b.scratch, make)


def _gather_exchange(wsrc):
    return _Exchange([wsrc], [jax.ShapeDtypeStruct((4,) + wsrc.shape, wsrc.dtype)], GATHER_SCRATCH,
                     lambda ins, outs, sems: _gather_steps(ins[0], outs[0], *sems))


def _launch(body, name, grid, in_specs, out_specs, out_shape, scratch, args, exchange=None, prefetch=0):
    def call(fn, fn_name, ins, outs, shapes, scr, operands, effects):
        spec = pltpu.PrefetchScalarGridSpec(num_scalar_prefetch=prefetch, grid=grid, in_specs=ins, out_specs=outs,
                                            scratch_shapes=scr)
        return pl.pallas_call(fn, name=fn_name, grid_spec=spec, out_shape=shapes,
                              compiler_params=_params(has_side_effects=effects))(*operands)

    if exchange is None:
        return call(body, name, list(in_specs), list(out_specs), list(out_shape), list(scratch), args, False)
    n_in, n_out, ei, eo, ns = len(in_specs), len(out_specs), len(exchange.args), len(exchange.out_shape), len(exchange.scratch)
    nsteps = 1
    for g in grid:
        nsteps *= g

    def wrapped(*refs):
        scalars, refs = refs[:prefetch], refs[prefetch:]
        ins, xin = refs[:n_in], refs[n_in:n_in + ei]
        outs, xout = refs[n_in + ei:n_in + ei + n_out], refs[n_in + ei + n_out:n_in + ei + n_out + eo]
        rest = refs[n_in + ei + n_out + eo:]
        own, sems = rest[:len(rest) - ns], rest[len(rest) - ns:]
        start, forward, finish = exchange.make(xin, xout, sems)
        i = pl.program_id(0)
        for d in range(1, len(grid)):
            i = i * grid[d] + pl.program_id(d)
        pl.when(i == 0)(start)
        body(*scalars, *ins, *outs, *own)
        pl.when(i == max(nsteps - 3, 0))(forward)
        pl.when(i == nsteps - 1)(finish)

    anyspec = pl.BlockSpec(memory_space=pl.ANY)
    return call(wrapped, name + "_x", list(in_specs) + [anyspec] * ei, list(out_specs) + [anyspec] * eo,
                list(out_shape) + exchange.out_shape, list(scratch) + exchange.scratch, (*args, *exchange.args), True)


def _attn_fwd(q, kv, sinks, exchange=None):
    T = kv.shape[0]

    def body(q_ref, kv_ref, s_ref, o_ref):
        for b in range(ATTN_BLOCKS):
            rows = slice(b * BLK, (b + 1) * BLK)
            band, mask, _, _ = _attn_band(kv_ref, ATTN_BLOCKS * pl.program_id(0) + b)
            for g in range(2):
                qs = q_ref[4 * g:4 * g + 4, rows, :].reshape(4 * BLK, HEAD_DIM)
                p, _, den = _attn_scores(band, mask, qs, s_ref, g)
                ot = _mm_tn(band[:, 128:256], p) * (1.0 / den)
                for hh in range(4):
                    o = ot[:, hh * BLK:(hh + 1) * BLK].T
                    o_ref[rows, (4 * g + hh) * 64:(4 * g + hh + 1) * 64] = o[:, g * 64:(g + 1) * 64].astype(BF16)

    tq = ATTN_BLOCKS * BLK
    return _launch(body, "attn_fwd", (T // tq,), [_heads(tq), _full((T, 256)), pl.BlockSpec(memory_space=pltpu.SMEM)],
                   [_rows(tq, 512)], [jax.ShapeDtypeStruct((T, 512), BF16)], [], (q, kv, sinks), exchange)


def _attn_bwd(q, kv, do, sinks, exchange=None):
    T = kv.shape[0]

    def body(q_ref, kv_ref, do_ref, s_ref, dq_ref, dkv_ref, ds_ref):
        @pl.when(pl.program_id(0) == 0)
        def _():
            ds_ref[...] = jnp.zeros_like(ds_ref)

        for b in range(ATTN_BLOCKS):
            rows = slice(b * BLK, (b + 1) * BLK)
            band, mask, cur, prev = _attn_band(kv_ref, ATTN_BLOCKS * pl.program_id(0) + b)
            for g in range(2):
                qs = q_ref[4 * g:4 * g + 4, rows, :].reshape(4 * BLK, HEAD_DIM)
                dos = do_ref[4 * g:4 * g + 4, rows, :].reshape(4 * BLK, HEAD_DIM)
                p, ps, den = _attn_scores(band, mask, qs, s_ref, g)
                inv = 1.0 / den
                p = p * inv
                dpt = _mm_nt(band[:, 128 + g * 64:192 + g * 64], dos)
                delta = jnp.sum(p * dpt, axis=0, keepdims=True)
                dst = p * (dpt - delta)
                dsv = -(ps * inv) * delta
                for hh in range(4):
                    dsink = jnp.sum(dsv[:, hh * BLK:(hh + 1) * BLK], axis=1, keepdims=True)
                    ds_ref[4 * g + hh:4 * g + hh + 1, :] += jnp.broadcast_to(dsink, (1, 128))
                dqt = _mm_tn(band[:, 0:128], dst) * SCALE
                for hh in range(4):
                    dqh = dqt[:, hh * BLK:(hh + 1) * BLK].T
                    dq_ref[rows, (4 * g + hh) * 64:(4 * g + hh + 1) * 64] = dqh[:, g * 64:(g + 1) * 64].astype(BF16)
                dk = _mm(dst, qs) * SCALE
                dv = _mm(p, dos)
                dkv_ref[pl.ds(cur, BLK), g * 64:(g + 1) * 64] = dk[BLK:2 * BLK]
                dkv_ref[pl.ds(cur, BLK), 128 + g * 64:192 + g * 64] = dv[BLK:2 * BLK]
                dkv_ref[pl.ds(prev, BLK), g * 64:(g + 1) * 64] += dk[0:BLK]
                dkv_ref[pl.ds(prev, BLK), 128 + g * 64:192 + g * 64] += dv[0:BLK]

    tq = ATTN_BLOCKS * BLK
    return _launch(body, "attn_bwd", (T // tq,),
                   [_heads(tq), _full((T, 256)), _heads(tq), pl.BlockSpec(memory_space=pltpu.SMEM)],
                   [_rows(tq, 512), _full((T, 256)), _full((8, 128))],
                   [jax.ShapeDtypeStruct((T, 512), BF16), jax.ShapeDtypeStruct((T, 256), F32),
                    jax.ShapeDtypeStruct((8, 128), F32)], [], (q, kv, do, sinks), exchange)


def _rows8(tm, cols):
    return lax.broadcasted_iota(jnp.int32, (tm, cols), 0) & 7


def _lru_gates(xc, wa, ba, wx, bx, lam):
    r = _sigmoid(_mm(xc, wa) + ba)
    ii = _sigmoid(_mm(xc, wx) + bx)
    sp = _softplus(-lam)
    la = -LRU_C * r * sp
    a = jnp.exp(la)
    m = jnp.sqrt(-jnp.tanh(la) * (a * a + 1.0))
    return r, ii, sp, a, m


def _rnn_fwd(xr, gr, cw, cb, wa, ba, wx, bx, lam, exchange=None):
    T = xr.shape[0]
    tm = 512
    C = D_RNN

    def body(xr_ref, gr_ref, cw_ref, cb_ref, wa_ref, ba_ref, wx_ref, bx_ref, lam_ref,
             xc_ref, h_ref, rec_ref, ext, a_s, b_s, carry):
        i = pl.program_id(0)

        @pl.when(i == 0)
        def _():
            ext[...] = jnp.zeros((8, C), F32)
            carry[...] = jnp.zeros((8, C), F32)

        xr = xr_ref[...]
        edge = ext[...]
        xc = cb_ref[...] + cw_ref[3:4, :] * xr
        for k in range(3):
            xc = xc + cw_ref[k:k + 1, :] * _shift_rows(xr, 3 - k, edge)
        ext[...] = xr[tm - 8:tm, :]
        xc_ref[...] = xc
        _, ii, _, a, m = _lru_gates(xc, wa_ref[...], ba_ref[...], wx_ref[...], bx_ref[...], lam_ref[...])
        b = m * ii * xc
        r8 = _rows8(tm, C)
        for d in (1, 2, 4):
            ok = r8 >= d
            a_sh = jnp.where(ok, pltpu.roll(a, d, 0), 1.0)
            b_sh = jnp.where(ok, pltpu.roll(b, d, 0), 0.0)
            b = a * b_sh + b
            a = a * a_sh
        a_s[...] = a
        b_s[...] = b

        def step(g, hin):
            s = pl.multiple_of(g * 8, 8)
            hg = a_s[pl.ds(s, 8), :] * hin + b_s[pl.ds(s, 8), :]
            h_ref[pl.ds(s, 8), :] = hg
            return jnp.broadcast_to(hg[7:8, :], (8, C))

        carry[...] = lax.fori_loop(0, tm // 8, step, carry[...], unroll=4)
        ge, _ = _gelu(gr_ref[...])
        rec_ref[...] = (h_ref[...] * ge).astype(BF16)

    vec = _full((1, C))
    in_specs = [_rows(tm, C), _rows(tm, C), _full((4, C)), vec, _full((C, C)), vec, _full((C, C)), vec, vec]
    out_specs = [_rows(tm, C), _rows(tm, C), _rows(tm, C)]
    out_shape = [jax.ShapeDtypeStruct((T, C), F32), jax.ShapeDtypeStruct((T, C), F32), jax.ShapeDtypeStruct((T, C), BF16)]
    scratch = [pltpu.VMEM((8, C), F32), pltpu.VMEM((tm, C), F32), pltpu.VMEM((tm, C), F32), pltpu.VMEM((8, C), F32)]
    return _launch(body, "rnn_fwd", (T // tm,), in_specs, out_specs, out_shape, scratch,
                   (xr, gr, cw, cb, wa, ba, wx, bx, lam), exchange)


def _rnn_bwd(drec, gr, h, xc, xr, cw, wa, ba, wx, bx, lam, exchange=None):
    T = xr.shape[0]
    tm = 512
    C = D_RNN
    nt = T // tm
    t8 = tm // 8

    def body(drec_ref, gr_ref, h_ref, hp_ref, xc_ref, xr_ref, cw_ref, wa_ref, ba_ref, wx_ref, bx_ref,
             lam_ref, dxr_ref, dgr_ref, dwa_ref, dwx_ref, dvec_ref, c_s, g_s, gout, ext, anext, gcarry):
        i = pl.program_id(0)
        j = nt - 1 - i

        @pl.when(i == 0)
        def _():
            dwa_ref[...] = jnp.zeros_like(dwa_ref)
            dwx_ref[...] = jnp.zeros_like(dwx_ref)
            dvec_ref[...] = jnp.zeros_like(dvec_ref)
            anext[...] = jnp.zeros((8, C), F32)
            gcarry[...] = jnp.zeros((8, C), F32)
            ext[...] = jnp.zeros((8, C), F32)

        xc = xc_ref[...]
        lam = lam_ref[...]
        r, ii, sp, a, m = _lru_gates(xc, wa_ref[...], ba_ref[...], wx_ref[...], bx_ref[...], lam)
        ge, dge = _gelu(gr_ref[...])
        drec = drec_ref[...]
        hh = h_ref[...]
        dgr_ref[...] = (drec * hh * dge).astype(BF16)
        dh = drec * ge
        rowi = lax.broadcasted_iota(jnp.int32, (tm, C), 0)
        c = jnp.where(rowi == tm - 1, jnp.broadcast_to(anext[0:1, :], (tm, C)), pltpu.roll(a, tm - 1, 0))
        anext[...] = a[0:8, :]
        r8 = rowi & 7
        gg = dh
        for d in (1, 2, 4):
            ok = r8 < 8 - d
            c_sh = jnp.where(ok, pltpu.roll(c, tm - d, 0), 1.0)
            g_sh = jnp.where(ok, pltpu.roll(gg, tm - d, 0), 0.0)
            gg = c * g_sh + gg
            c = c * c_sh
        c_s[...] = c
        g_s[...] = gg

        def step(k, gin):
            s = pl.multiple_of((t8 - 1 - k) * 8, 8)
            og = c_s[pl.ds(s, 8), :] * gin + g_s[pl.ds(s, 8), :]
            gout[pl.ds(s, 8), :] = og
            return jnp.broadcast_to(og[0:1, :], (8, C))

        gcarry[...] = lax.fori_loop(0, t8, step, gcarry[...], unroll=4)
        G = gout[...]
        hprev_row = jnp.where(j > 0, hp_ref[7:8, :], 0.0)
        hprev = jnp.where(rowi == 0, jnp.broadcast_to(hprev_row, (tm, C)), pltpu.roll(hh, 1, 0))
        da = G * hprev
        dm = G * ii * xc
        di = G * m * xc
        dxc = G * m * ii
        dla = da * a - dm * a * a / m
        dr = dla * (-LRU_C * sp)
        dsp = _colsum(dla * (-LRU_C * r))
        dlam = dsp * (-_sigmoid(-lam))
        dpr = dr * r * (1.0 - r)
        dpi = di * ii * (1.0 - ii)
        dxc = dxc + _mm_nt(dpr, wa_ref[...]) + _mm_nt(dpi, wx_ref[...])
        dwa_ref[...] += _mm_tn(xc, dpr)
        dwx_ref[...] += _mm_tn(xc, dpi)
        dvec_ref[0:1, :] += _colsum(dpr)
        dvec_ref[1:2, :] += _colsum(dpi)
        dvec_ref[2:3, :] += dlam
        dvec_ref[3:4, :] += _colsum(dxc)
        edge = ext[...]
        xr = xr_ref[...]
        dxr = cw_ref[3:4, :] * dxc
        dvec_ref[7:8, :] += _colsum(dxc * xr)
        for k in range(3):
            up = _shift_rows(dxc, k - 3, edge)
            dxr = dxr + cw_ref[k:k + 1, :] * up
            dvec_ref[4 + k:5 + k, :] += _colsum(up * xr)
        ext[...] = dxc[0:8, :]
        dxr_ref[...] = dxr.astype(BF16)

    rev = lambda i: nt - 1 - i
    prev8 = lambda i: jnp.maximum((nt - 1 - i) * t8 - 1, 0)
    vec = _full((1, C))
    return _launch(
        body, "rnn_bwd", (nt,),
        [_rows(tm, C, rev), _rows(tm, C, rev), _rows(tm, C, rev), _rows(8, C, prev8), _rows(tm, C, rev),
         _rows(tm, C, rev), _full((4, C)), _full((C, C)), vec, _full((C, C)), vec, vec],
        [_rows(tm, C, rev), _rows(tm, C, rev), _full((C, C)), _full((C, C)), _full((8, C))],
        [jax.ShapeDtypeStruct((T, C), BF16), jax.ShapeDtypeStruct((T, C), BF16),
         jax.ShapeDtypeStruct((C, C), F32), jax.ShapeDtypeStruct((C, C), F32), jax.ShapeDtypeStruct((8, C), F32)],
        [pltpu.VMEM((tm, C), F32), pltpu.VMEM((tm, C), F32), pltpu.VMEM((tm, C), F32),
         pltpu.VMEM((8, C), F32), pltpu.VMEM((8, C), F32), pltpu.VMEM((8, C), F32)],
        (drec, gr, h, h, xc, xr, cw, wa, ba, wx, bx, lam), exchange)


def _out_proj(att, rec, x, w_out, g1, b1):
    T = x.shape[0]
    tm = min(1024, T)

    def body(att_ref, rec_ref, x_ref, w_ref, g1_ref, b1_ref, z_ref, h_ref):
        mix = _mm(att_ref[...], w_ref[0:512, :]) + _mm(rec_ref[...], w_ref[512:1024, :])
        z1 = ALPHA * x_ref[...] + mix
        z_ref[...] = z1
        h1, _, _ = _ln(z1, g1_ref[...], b1_ref[...])
        h_ref[...] = h1.astype(MXU_DTYPE).astype(BF16)

    return pl.pallas_call(
        body, name="out_proj", grid=(T // tm,),
        in_specs=[_rows(tm, 512), _rows(tm, 512), _rows(tm, D), _full((D, D)), _full((1, D)), _full((1, D))],
        out_specs=[_rows(tm, D), _rows(tm, D)],
        out_shape=[jax.ShapeDtypeStruct((T, D), F32), jax.ShapeDtypeStruct((T, D), BF16)],
        compiler_params=_params(),
    )(att, rec, x, w_out, g1, b1)


NC = D_FF // FF_CHUNK


def _ffn_up(h1b, w_up_t, fcw, fcb, exchange=None):
    T = h1b.shape[0]
    tm = min(1024, T)
    CW = FF_CHUNK

    def body(h_ref, wg_ref, wv_ref, fcw_ref, fcb_ref, gate_ref, ge_ref, vd_ref, act_ref, before):
        i = pl.program_id(1)

        @pl.when(i == 0)
        def _():
            before[...] = jnp.zeros((8, CW), F32)

        hb = h_ref[...]
        gate = _mm_nt(hb, wg_ref[...])
        val = _mm_nt(hb, wv_ref[...])
        gate_ref[...] = gate.astype(BF16)
        edge = before[...]
        gc = (fcb_ref[...] + fcw_ref[0:1, :] * _shift_rows(gate, 2, edge) + fcw_ref[1:2, :] * _shift_rows(gate, 1, edge)
              + fcw_ref[2:3, :] * gate)
        before[...] = gate[tm - 8:tm, :]
        ge, dge = _gelu(gc)
        ge_ref[...] = ge.astype(BF16)
        vd_ref[...] = (val * dge).astype(BF16)
        act_ref[...] = (ge * val).astype(BF16)

    chunk = pl.BlockSpec((None, tm, CW), lambda c, i: (c, i, 0))
    return _launch(
        body, "ffn_up", (NC, T // tm),
        [pl.BlockSpec((tm, D), lambda c, i: (i, 0)), pl.BlockSpec((CW, D), lambda c, i: (c, 0)),
         pl.BlockSpec((CW, D), lambda c, i: (NC + c, 0)), pl.BlockSpec((None, 3, CW), lambda c, i: (c, 0, 0)),
         pl.BlockSpec((None, 1, CW), lambda c, i: (c, 0, 0))],
        [chunk] * 4, [jax.ShapeDtypeStruct((NC, T, CW), BF16)] * 4, [pltpu.VMEM((8, CW), F32)],
        (h1b, w_up_t, w_up_t, fcw, fcb), exchange)


def _ffn_down(act, z1, p, tgt, w_down, w_g, w_p_t, g1, b1, g2, b2, bg):
    T = z1.shape[0]
    tm = 512

    def body(act_ref, z_ref, p_ref, t_ref, wdn_hbm, wg_hbm, wp_hbm, g1_ref, b1_ref, g2_ref, b2_ref, bg_ref,
             dz2_ref, dz2b_ref, dpre_ref, dpp_ref, vec_ref, wdn, wg, wp):
        @pl.when(pl.program_id(0) == 0)
        def _():
            pltpu.sync_copy(wdn_hbm, wdn)
            pltpu.sync_copy(wg_hbm, wg)
            pltpu.sync_copy(wp_hbm, wp)
            vec_ref[...] = jnp.zeros_like(vec_ref)

        g2v = g2_ref[...]
        for r in (slice(0, tm // 2), slice(tm // 2, tm)):
            h1, _, _ = _ln(z_ref[r, :], g1_ref[...], b1_ref[...])
            h1b = h1.astype(MXU_DTYPE)
            ffn = _mm(act_ref[0, r, :], wdn[0:FF_CHUNK, :])
            for c in range(1, NC):
                ffn = ffn + _mm(act_ref[c, r, :], wdn[c * FF_CHUNK:(c + 1) * FF_CHUNK, :])
            sg = _sigmoid(_mm(h1b, wg[...]) + bg_ref[...])
            pp = _mm_nt(p_ref[r, :], wp[...])
            z2 = ALPHA * h1 + ffn + sg * pp
            y, xh2, rstd2 = _ln(z2, g2v, b2_ref[...])
            diff = y - t_ref[r, :]
            dy = diff * (1.0 / D)
            dz2 = _ln_bwd(dy, xh2, rstd2, g2v)
            dpre = dz2 * pp * sg * (1.0 - sg)
            dz2_ref[r, :] = dz2
            dz2b_ref[r, :] = dz2.astype(BF16)
            dpre_ref[r, :] = dpre.astype(BF16)
            dpp_ref[r, :] = (dz2 * sg).astype(BF16)
            loss = 0.5 * jnp.sum(jnp.sum(diff * diff, axis=1, keepdims=True), axis=0, keepdims=True) * (1.0 / D)
            vec_ref[0:1, :] += jnp.broadcast_to(loss, (1, D))
            vec_ref[1:2, :] += _colsum(dy * xh2)
            vec_ref[2:3, :] += _colsum(dy)
            vec_ref[3:4, :] += _colsum(dpre)

    anyspec = pl.BlockSpec(memory_space=pl.ANY)
    vec = _full((1, D))
    return pl.pallas_call(
        body, name="ffn_down", grid=(T // tm,),
        in_specs=[pl.BlockSpec((NC, tm, FF_CHUNK), lambda i: (0, i, 0)), _rows(tm, D), _rows(tm, PLE), _rows(tm, D),
                  anyspec, anyspec, anyspec] + [vec] * 5,
        out_specs=[_rows(tm, D)] * 4 + [_full((8, D))],
        out_shape=[jax.ShapeDtypeStruct((T, D), F32)] + [jax.ShapeDtypeStruct((T, D), BF16)] * 3
                  + [jax.ShapeDtypeStruct((8, D), F32)],
        scratch_shapes=[pltpu.VMEM((D_FF, D), MXU_DTYPE), pltpu.VMEM((D, D), MXU_DTYPE), pltpu.VMEM((D, PLE), MXU_DTYPE)],
        compiler_params=_params(),
    )(act, z1, p, tgt, w_down, w_g, w_p_t, g1, b1, g2, b2, bg)


def _ffn_bwd(dz2b, gate, ge, vd, w_down, fcw):
    T = dz2b.shape[0]
    tm = min(1024, T)
    CW = FF_CHUNK
    nt = T // tm

    def body(dz_ref, wdn_ref, gate_ref, ge_ref, vd_ref, fcw_ref, dup_ref, dfc_ref, after):
        i = pl.program_id(1)

        @pl.when(i == 0)
        def _():
            after[...] = jnp.zeros((8, CW), F32)
            dfc_ref[...] = jnp.zeros_like(dfc_ref)

        gate = gate_ref[...].astype(F32)
        dact = _mm_nt(dz_ref[...], wdn_ref[...])
        dgc = dact * vd_ref[...].astype(F32)
        edge = after[...]
        dgc1 = _shift_rows(dgc, -1, edge)
        dgc2 = _shift_rows(dgc, -2, edge)
        after[...] = dgc[0:8, :]
        dup_ref[0] = (fcw_ref[2:3, :] * dgc + fcw_ref[1:2, :] * dgc1 + fcw_ref[0:1, :] * dgc2).astype(BF16)
        dup_ref[1] = (dact * ge_ref[...].astype(F32)).astype(BF16)
        dfc_ref[0:1, :] += _colsum(dgc2 * gate)
        dfc_ref[1:2, :] += _colsum(dgc1 * gate)
        dfc_ref[2:3, :] += _colsum(dgc * gate)
        dfc_ref[3:4, :] += _colsum(dgc)

    rev = lambda c, i: (c, nt - 1 - i, 0)
    chunk = pl.BlockSpec((None, tm, CW), rev)
    return pl.pallas_call(
        body, name="ffn_bwd", grid=(NC, nt),
        in_specs=[pl.BlockSpec((tm, D), lambda c, i: (nt - 1 - i, 0)), pl.BlockSpec((CW, D), lambda c, i: (c, 0)),
                  chunk, chunk, chunk, pl.BlockSpec((None, 3, CW), lambda c, i: (c, 0, 0))],
        out_specs=[pl.BlockSpec((None, 2, tm, CW), lambda c, i: (c, 0, nt - 1 - i, 0)),
                   pl.BlockSpec((None, 8, CW), lambda c, i: (c, 0, 0))],
        out_shape=[jax.ShapeDtypeStruct((NC, 2, T, CW), BF16), jax.ShapeDtypeStruct((NC, 8, CW), F32)],
        scratch_shapes=[pltpu.VMEM((8, CW), F32)],
        compiler_params=_params(),
    )(dz2b, w_down, gate, ge, vd, fcw)


def _ffn_dh1(dup, dz2, dpre, z1, w_up_t, w_g, g1, b1):
    T = z1.shape[0]
    tm = 512

    def body(dup_ref, dz2_ref, dpre_ref, z_ref, wup_hbm, wg_hbm, g1_ref, b1_ref, dz1_ref, vec_ref, wup, wg):
        @pl.when(pl.program_id(0) == 0)
        def _():
            pltpu.sync_copy(wup_hbm, wup)
            pltpu.sync_copy(wg_hbm, wg)
            vec_ref[...] = jnp.zeros_like(vec_ref)

        g1v = g1_ref[...]
        _, xh1, rstd1 = _ln(z_ref[...], g1v, b1_ref[...])
        dh1 = ALPHA * dz2_ref[...] + _mm_nt(dpre_ref[...], wg[...])
        for c in range(NC):
            for s in range(2):
                r0 = s * D_FF + c * FF_CHUNK
                dh1 = dh1 + _mm(dup_ref[c, s], wup[r0:r0 + FF_CHUNK, :])
        dz1_ref[...] = _ln_bwd(dh1, xh1, rstd1, g1v)
        vec_ref[0:1, :] += _colsum(dh1 * xh1)
        vec_ref[1:2, :] += _colsum(dh1)

    anyspec = pl.BlockSpec(memory_space=pl.ANY)
    vec = _full((1, D))
    return pl.pallas_call(
        body, name="ffn_dh1", grid=(T // tm,),
        in_specs=[pl.BlockSpec((NC, 2, tm, FF_CHUNK), lambda i: (0, 0, i, 0)), _rows(tm, D), _rows(tm, D), _rows(tm, D),
                  anyspec, anyspec, vec, vec],
        out_specs=[_rows(tm, D), _full((8, D))],
        out_shape=[jax.ShapeDtypeStruct((T, D), F32), jax.ShapeDtypeStruct((8, D), F32)],
        scratch_shapes=[pltpu.VMEM((2 * D_FF, D), MXU_DTYPE), pltpu.VMEM((D, D), MXU_DTYPE)],
        compiler_params=_params(),
    )(dup, dz2, dpre, z1, w_up_t, w_g, g1, b1)


def _out_proj_bwd(dz1, w_out, exchange=None):
    T = dz1.shape[0]
    tm = min(1024, T)

    def body(dz_ref, w_ref, datt_ref, drec_ref):
        dzb = dz_ref[...].astype(MXU_DTYPE)
        datt = _mm_nt(dzb, w_ref[0:512, :])
        for h in range(HEADS):
            datt_ref[h] = datt[:, h * 64:(h + 1) * 64].astype(BF16)
        drec_ref[...] = _mm_nt(dzb, w_ref[512:1024, :])

    return _launch(body, "out_proj_bwd", (T // tm,), [_rows(tm, D), _full((D, D))], [_heads(tm), _rows(tm, 512)],
                   [jax.ShapeDtypeStruct((HEADS, T, 64), BF16), jax.ShapeDtypeStruct((T, 512), F32)], [],
                   (dz1, w_out), exchange)


def _in_proj_bwd(dq, dkv, dxr, dgr, dz1, w_in_t, exchange=None):
    T = dz1.shape[0]
    tm = 512
    W = D_IN // 4

    def body(dq_ref, dkv_ref, dxr_ref, dgr_ref, dz_ref, w_ref, dx_ref, du_ref):
        dkv = dkv_ref[...]
        dx_ref[...] = (ALPHA * dz_ref[...] + _mm(dq_ref[...], w_ref[0:512, :]) + _mm(dkv, w_ref[512:768, :])
                       + _mm(dxr_ref[...], w_ref[768:1280, :]) + _mm(dgr_ref[...], w_ref[1280:1792, :]))
        dq, dxr, dgr = dq_ref[...].astype(F32), dxr_ref[...].astype(F32), dgr_ref[...].astype(F32)
        du_ref[0] = dq[:, 0:W].astype(BF16)
        du_ref[1, :, 0:64] = dq[:, W:512].astype(BF16)
        du_ref[1, :, 64:320] = dkv.astype(BF16)
        du_ref[1, :, 320:W] = dxr[:, 0:128].astype(BF16)
        du_ref[2, :, 0:384] = dxr[:, 128:512].astype(BF16)
        du_ref[2, :, 384:W] = dgr[:, 0:64].astype(BF16)
        du_ref[3] = dgr[:, 64:512].astype(BF16)

    return _launch(body, "in_proj_bwd", (T // tm,),
                   [_rows(tm, 512), _rows(tm, 256), _rows(tm, 512), _rows(tm, 512), _rows(tm, D), _full((D_IN, D))],
                   [_rows(tm, D), pl.BlockSpec((4, tm, W), lambda i: (0, i, 0))],
                   [jax.ShapeDtypeStruct((T, D), F32), jax.ShapeDtypeStruct((4, T, W), BF16)], [],
                   (dq, dkv, dxr, dgr, dz1, w_in_t), exchange)


def _accumulate_tn(a_ref, b_ref, o_ref):
    @pl.when(pl.program_id(1) == 0)
    def _():
        o_ref[...] = jnp.zeros_like(o_ref)

    o_ref[...] += _mm_tn(a_ref[...], b_ref[...])


def _weight_grad_cols(a, b, name, n_blocks, b_spec, out_shape, out_spec, exchange=None):
    T, M = a.shape
    bt = min(DW_TOKENS, T)
    return _launch(functools.partial(_accumulate_tn), name, (n_blocks, T // bt),
                   [pl.BlockSpec((bt, M), lambda m, k: (k, 0)), b_spec(bt)], [out_spec],
                   [jax.ShapeDtypeStruct(out_shape, F32)], [], (a, b), exchange)


def _dw_out(att, rec, dz1):
    T = dz1.shape[0]
    bt = min(DW_TOKENS // 2, T)

    def body(att_ref, rec_ref, dz_ref, o_ref):
        @pl.when(pl.program_id(0) == 0)
        def _():
            o_ref[...] = jnp.zeros_like(o_ref)

        dz = dz_ref[...].astype(MXU_DTYPE)
        o_ref[0:512, :] += _mm_tn(att_ref[...], dz)
        o_ref[512:1024, :] += _mm_tn(rec_ref[...], dz)

    return pl.pallas_call(
        body, name="dw_out", grid=(T // bt,), in_specs=[_rows(bt, 512), _rows(bt, 512), _rows(bt, D)],
        out_specs=_full((D, D)), out_shape=jax.ShapeDtypeStruct((D, D), F32), compiler_params=_params())(att, rec, dz1)


def _weight_grad(a, b, bm, name, exchange=None):
    bt = min(DW_TOKENS // 2 if b.dtype == F32 else DW_TOKENS, b.shape[0])
    if a.ndim == 3:
        assert a.shape[2] == bm
        T, M = a.shape[1], a.shape[0] * bm
        a_spec = pl.BlockSpec((None, bt, bm), lambda m, k: (m, k, 0))
    else:
        T, M = a.shape
        a_spec = pl.BlockSpec((bt, bm), lambda m, k: (k, m))
    N = b.shape[1]
    nk = T // bt

    out = _launch(functools.partial(_accumulate_tn), name, (M // bm, nk),
                  [a_spec, pl.BlockSpec((bt, N), lambda m, k: (k, 0))], [pl.BlockSpec((bm, N), lambda m, k: (m, 0))],
                  [jax.ShapeDtypeStruct((M, N), F32)], [], (a, b), exchange)
    return out[0] if exchange is None else out


def _adamw(w, g, m, v, name):
    R, C = w.shape
    tr = R // 8 if R % 64 == 0 else R
    c1 = 1.0 / (1.0 - ADAM_B1 ** ADAM_STEP)
    c2 = 1.0 / (1.0 - ADAM_B2 ** ADAM_STEP)

    def body(w_ref, g_ref, m_ref, v_ref, d_ref, nm_ref, nv_ref):
        g = g_ref[...]
        nm = ADAM_B1 * m_ref[...] + (1.0 - ADAM_B1) * g
        nv = ADAM_B2 * v_ref[...] + (1.0 - ADAM_B2) * g * g
        nm_ref[...] = nm
        nv_ref[...] = nv
        d_ref[...] = -ADAM_LR * ((nm * c1) / (jnp.sqrt(nv * c2) + ADAM_EPS) + ADAM_WD * w_ref[...])

    spec = pl.BlockSpec((tr, C), lambda i: (i, 0))
    return pl.pallas_call(
        body, name=name, grid=(R // tr,),
        in_specs=[spec] * 4, out_specs=[spec] * 3,
        out_shape=[jax.ShapeDtypeStruct((R, C), F32)] * 3,
        compiler_params=_params(),
    )(w, g, m, v)


def _adamw_halves(ws, mines, sibs, ms, vs, c, name, exchange=None):
    n, nb = len(ws), 4
    c1 = 1.0 / (1.0 - ADAM_B1 ** ADAM_STEP)
    c2 = 1.0 / (1.0 - ADAM_B2 ** ADAM_STEP)

    def body(c_ref, *refs):
        own = (pl.program_id(0) // nb) == c_ref[0]
        for i in range(n):
            w_ref, a_ref, b_ref, m_ref, v_ref = refs[5 * i:5 * i + 5]
            g_ref, d_ref, nm_ref, nv_ref = refs[5 * n + 4 * i:5 * n + 4 * i + 4]
            g = jnp.where(own, a_ref[...], b_ref[...])
            nm = ADAM_B1 * m_ref[...] + (1.0 - ADAM_B1) * g
            nv = ADAM_B2 * v_ref[...] + (1.0 - ADAM_B2) * g * g
            g_ref[...] = g
            nm_ref[...] = nm
            nv_ref[...] = nv
            d_ref[...] = -ADAM_LR * ((nm * c1) / (jnp.sqrt(nv * c2) + ADAM_EPS) + ADAM_WD * w_ref[...])

    in_specs, out_specs, out_shape, args = [], [], [], []
    for w, a, b, m, v in zip(ws, mines, sibs, ms, vs):
        R, C = w.shape
        tr = R // (2 * nb)
        assert tr % 8 == 0 and a.shape == (R // 2, C)
        full = pl.BlockSpec((tr, C), lambda i, c_ref: (i, 0))
        half = pl.BlockSpec((tr, C), lambda i, c_ref: (i % nb, 0))
        in_specs += [full, half, half, full, full]
        out_specs += [full] * 4
        out_shape += [jax.ShapeDtypeStruct((R, C), F32)] * 4
        args += [w, a, b, m, v]
    out = _launch(body, name, (2 * nb,), in_specs, out_specs, out_shape, [], (c, *args), exchange, prefetch=1)
    return [tuple(out[4 * i:4 * i + 4]) for i in range(n)], list(out[4 * n:])


def _add4(fs, name):
    n = len(fs)

    def body(*refs):
        for a_ref, o_ref in zip(refs[:n], refs[n:]):
            o_ref[...] = ((a_ref[0].astype(F32) + a_ref[1].astype(F32)) + a_ref[2].astype(F32)) + a_ref[3].astype(F32)

    for f in fs:
        assert (f.shape[1] // 2) % 16 == 0
    return pl.pallas_call(
        body, name=name, grid=(2,),
        in_specs=[pl.BlockSpec((4, f.shape[1] // 2, f.shape[2]), lambda i: (0, i, 0)) for f in fs],
        out_specs=[pl.BlockSpec((f.shape[1] // 2, f.shape[2]), lambda i: (i, 0)) for f in fs],
        out_shape=[jax.ShapeDtypeStruct(f.shape[1:], F32) for f in fs], compiler_params=_params())(*fs)


def _gather_first(wsrc, cpack):
    def body(w_ref, c_ref, gw_ref, gc_ref, send_sems, recv_sems, local_sem, csend, crecv, clocal):
        x, y, c = _pos()
        me = 2 * x + y
        chips = _other_chips(x, y)
        start, forward, finish = _gather_steps(w_ref, gw_ref, send_sems, recv_sems, local_sem)
        start()
        loc = pltpu.make_async_copy(c_ref, gc_ref.at[me], clocal)
        loc.start()

        def conv_copy(k, slot):
            px, py = chips[k]
            return pltpu.make_async_remote_copy(src_ref=c_ref, dst_ref=gc_ref.at[slot], send_sem=csend.at[k],
                                                recv_sem=crecv.at[k], device_id=(px, py, c), device_id_type=MESH)

        for k in range(3):
            conv_copy(k, me).start()
        forward()
        finish()
        for k, (px, py) in enumerate(chips):
            conv_copy(k, 2 * px + py).wait_recv()
        for k in range(3):
            conv_copy(k, me).wait_send()
        loc.wait()

    anyspec = pl.BlockSpec(memory_space=pl.ANY)
    return pl.pallas_call(
        body, name="gather_first",
        in_specs=[anyspec, anyspec], out_specs=[anyspec, anyspec],
        out_shape=[jax.ShapeDtypeStruct((4,) + wsrc.shape, wsrc.dtype), jax.ShapeDtypeStruct((4,) + cpack.shape, cpack.dtype)],
        scratch_shapes=GATHER_SCRATCH + [pltpu.SemaphoreType.DMA((3,)), pltpu.SemaphoreType.DMA((3,)), pltpu.SemaphoreType.DMA],
        compiler_params=_params(has_side_effects=True),
    )(wsrc, cpack)


def _all_devices_exchange(s):
    def make(ins, outs, sems):
        s_ref, o_ref = ins[0], outs[0]
        send_sems, recv_sems, local_sem = sems
        x, y, c = _pos()
        me = 4 * x + 2 * y + c
        loc = pltpu.make_async_copy(s_ref, o_ref.at[me], local_sem)

        def copy(k, slot):
            peer = (x ^ (k >> 2), y ^ ((k >> 1) & 1), c ^ (k & 1))
            return pltpu.make_async_remote_copy(src_ref=s_ref, dst_ref=o_ref.at[slot], send_sem=send_sems.at[k - 1],
                                                recv_sem=recv_sems.at[k - 1], device_id=peer, device_id_type=MESH)

        def start():
            loc.start()
            for k in range(1, 8):
                copy(k, me).start()

        def finish():
            for k in range(1, 8):
                copy(k, 4 * (x ^ (k >> 2)) + 2 * (y ^ ((k >> 1) & 1)) + (c ^ (k & 1))).wait_recv()
            for k in range(1, 8):
                copy(k, me).wait_send()
            loc.wait()

        return start, lambda: None, finish

    return _Exchange([s], [jax.ShapeDtypeStruct((8,) + s.shape, s.dtype)],
                     [pltpu.SemaphoreType.DMA((7,)), pltpu.SemaphoreType.DMA((7,)), pltpu.SemaphoreType.DMA], make)


def _sum_devices(a):
    def body(a_ref, o_ref):
        acc = a_ref[0]
        for d in range(1, 8):
            acc = acc + a_ref[d]
        o_ref[...] = acc

    vm = pl.BlockSpec(memory_space=pltpu.VMEM)
    return pl.pallas_call(body, name="sum_devices", in_specs=[vm], out_specs=vm,
                          out_shape=jax.ShapeDtypeStruct(a.shape[1:], F32), compiler_params=_params())(a)


def _swap_exchange(gs):
    n = len(gs)

    def make(ins, outs, sems):
        x, y, c = _pos()
        cps = []
        for i in range(n):
            half = gs[i].shape[1] // 2
            rows = pl.ds(pl.multiple_of((1 - c) * half, 8), half)
            cps.append(pltpu.make_async_remote_copy(src_ref=ins[i].at[:, rows, :], dst_ref=outs[i], send_sem=sems[0].at[i],
                                                    recv_sem=sems[1].at[i], device_id=(x, y, 1 - c), device_id_type=MESH))

        def start():
            for cp in cps:
                cp.start()

        def finish():
            for cp in cps:
                cp.wait()

        return start, lambda: None, finish

    return _Exchange(gs, [jax.ShapeDtypeStruct((4, g.shape[1] // 2, g.shape[2]), g.dtype) for g in gs],
                     [pltpu.SemaphoreType.DMA((n,)), pltpu.SemaphoreType.DMA((n,))], make)


def _scatter_exchange(ss):
    n = len(ss)

    def make(ins, outs, sems):
        send_sems, recv_sems, local_sems = sems
        x, y, c = _pos()
        me = 2 * x + y
        chips = _other_chips(x, y)
        locs = [pltpu.make_async_copy(ins[i].at[me], outs[i].at[me], local_sems.at[i]) for i in range(n)]

        def copy(i, k, src_slot, dst_slot):
            px, py = chips[k]
            return pltpu.make_async_remote_copy(src_ref=ins[i].at[src_slot], dst_ref=outs[i].at[dst_slot],
                                                send_sem=send_sems.at[3 * i + k], recv_sem=recv_sems.at[3 * i + k],
                                                device_id=(px, py, c), device_id_type=MESH)

        def start():
            for i in range(n):
                locs[i].start()
                for k, (px, py) in enumerate(chips):
                    copy(i, k, 2 * px + py, me).start()

        def finish():
            for i in range(n):
                for k, (px, py) in enumerate(chips):
                    copy(i, k, me, 2 * px + py).wait_recv()
            for i in range(n):
                for k, (px, py) in enumerate(chips):
                    copy(i, k, 2 * px + py, me).wait_send()
                locs[i].wait()

        return start, lambda: None, finish

    return _Exchange(ss, [jax.ShapeDtypeStruct(s.shape, s.dtype) for s in ss],
                     [pltpu.SemaphoreType.DMA((3 * n,)), pltpu.SemaphoreType.DMA((3 * n,)), pltpu.SemaphoreType.DMA((n,))], make)


def _send_exchange(rs):
    n = len(rs)

    def make(ins, outs, sems):
        x, y, c = _pos()
        cps = [pltpu.make_async_remote_copy(src_ref=ins[i], dst_ref=outs[i], send_sem=sems[0].at[i], recv_sem=sems[1].at[i],
                                            device_id=(x, y, 1 - c), device_id_type=MESH) for i in range(n)]

        def start():
            for cp in cps:
                cp.start()

        def finish():
            for cp in cps:
                cp.wait()

        return start, lambda: None, finish

    return _Exchange(rs, [jax.ShapeDtypeStruct(r.shape, r.dtype) for r in rs],
                     [pltpu.SemaphoreType.DMA((n,)), pltpu.SemaphoreType.DMA((n,))], make)


def _reduce_in_vmem(g):
    _, R, C = g.shape
    H = R // 2

    def body(g_ref, mine_ref, other_ref, sib, part, got, swap_sems, send_sems, recv_sems, last_sems):
        x, y, c = _pos()
        me = 2 * x + y
        chips = _other_chips(x, y)
        sibling = (x, y, 1 - c)
        mine = pl.ds(pl.multiple_of(c * H, 8), H)
        theirs = pl.ds(pl.multiple_of((1 - c) * H, 8), H)
        swap = pltpu.make_async_remote_copy(src_ref=g_ref.at[:, theirs, :], dst_ref=sib, send_sem=swap_sems.at[0],
                                            recv_sem=swap_sems.at[1], device_id=sibling, device_id_type=MESH)
        swap.start()
        swap.wait()
        part[...] = (g_ref[:, mine, :] + sib[...]).astype(BF16)

        def copy(k, src_slot, dst_slot):
            px, py = chips[k]
            return pltpu.make_async_remote_copy(src_ref=part.at[src_slot], dst_ref=got.at[dst_slot], send_sem=send_sems.at[k],
                                                recv_sem=recv_sems.at[k], device_id=(px, py, c), device_id_type=MESH)

        for k, (px, py) in enumerate(chips):
            copy(k, 2 * px + py, me).start()
        got[me] = part[me]
        for k, (px, py) in enumerate(chips):
            copy(k, me, 2 * px + py).wait_recv()
        for k, (px, py) in enumerate(chips):
            copy(k, 2 * px + py, me).wait_send()
        mine_ref[...] = ((got[0].astype(F32) + got[1].astype(F32)) + got[2].astype(F32)) + got[3].astype(F32)
        last = pltpu.make_async_remote_copy(src_ref=mine_ref, dst_ref=other_ref, send_sem=last_sems.at[0],
                                            recv_sem=last_sems.at[1], device_id=sibling, device_id_type=MESH)
        last.start()
        last.wait()

    vm = pl.BlockSpec(memory_space=pltpu.VMEM)
    half = jax.ShapeDtypeStruct((H, C), F32)
    return pl.pallas_call(
        body, name="reduce_late", in_specs=[vm], out_specs=[vm, vm], out_shape=[half, half],
        scratch_shapes=[pltpu.VMEM((4, H, C), F32), pltpu.VMEM((4, H, C), BF16), pltpu.VMEM((4, H, C), BF16),
                        pltpu.SemaphoreType.DMA((2,)), pltpu.SemaphoreType.DMA((3,)), pltpu.SemaphoreType.DMA((3,)),
                        pltpu.SemaphoreType.DMA((2,))],
        compiler_params=_params(has_side_effects=True))(g)


def _add_half(gs, rs, c, name):
    n = len(gs)

    def body(c_ref, *refs):
        for g_ref, r_ref, o_ref in zip(refs[:n], refs[n:2 * n], refs[2 * n:]):
            o_ref[...] = (g_ref[...] + r_ref[...]).astype(BF16)

    g_specs, r_specs, out_shape = [], [], []
    for g, r in zip(gs, rs):
        _, H, C = r.shape
        tr = H // 2
        assert tr % 16 == 0 and g.shape == (4, 2 * H, C)
        g_specs.append(pl.BlockSpec((1, tr, C), lambda j, i, c_ref: (j, c_ref[0] * 2 + i, 0)))
        r_specs.append(pl.BlockSpec((1, tr, C), lambda j, i, c_ref: (j, i, 0)))
        out_shape.append(jax.ShapeDtypeStruct((4, H, C), BF16))
    grid_spec = pltpu.PrefetchScalarGridSpec(num_scalar_prefetch=1, grid=(4, 2), in_specs=g_specs + r_specs, out_specs=r_specs)
    return pl.pallas_call(body, name=name, grid_spec=grid_spec, out_shape=out_shape, compiler_params=_params())(c, *gs, *rs)


def _block_diag(w):
    eye = jnp.eye(RNN_BLOCKS, dtype=w.dtype)
    return (eye[:, None, :, None] * w[:, :, None, :]).reshape(D_RNN, D_RNN)


def _diag_blocks(wd):
    d = wd.reshape(RNN_BLOCKS, 64, RNN_BLOCKS, 64)
    return jnp.stack([d[h, :, h, :] for h in range(RNN_BLOCKS)])


def _split_pack(a, first, last):
    out, base = {}, PACK_OFF[first]
    for i in range(first, last):
        s = a[:, PACK_OFF[i] - base:PACK_OFF[i + 1] - base]
        out[BIG_KEYS[i]] = s.reshape(4 * 256, 256) if BIG_KEYS[i] == "w_p_t" else s.reshape(-1, 1024)
    return out


def _layer_grads(x, p, tgt, gw, small, shard=None, core=None):
    row = lambda v: v.reshape(1, -1)
    wa = _block_diag(small["gate_a_w"]).astype(MXU_DTYPE)
    wx = _block_diag(small["gate_x_w"]).astype(MXU_DTYPE)
    sinks = small["attn_sinks"].reshape(1, HEADS)

    dist = shard is not None
    q, kv, xr, gr, xb = _in_proj(x, gw["w_in_t"])
    cut = PACK_OFF[1] + PACK_ROWS[1] // 2
    att, *ga = _attn_fwd(q, kv, sinks, _gather_exchange(shard[PACK_OFF[1]:cut]) if dist else None)
    xc, h, rec, *gb = _rnn_fwd(xr, gr, small["rnn_conv_w"], row(small["rnn_conv_b"]), wa, row(small["gate_a_b"]),
                               wx, row(small["gate_x_b"]), row(small["lru_lambda"]),
                               _gather_exchange(shard[cut:PACK_OFF[3]]) if dist else None)
    if dist:
        gw = {**gw, **_split_pack(jnp.concatenate([ga[0], gb[0]], axis=1), 1, 3)}
    g1, b1 = row(small["ln1_g"]), row(small["ln1_b"])
    fcw = small["ffn_conv_w"].reshape(3, NC, FF_CHUNK).transpose(1, 0, 2)
    fcb = small["ffn_conv_b"].reshape(NC, 1, FF_CHUNK)
    z1, h1b = _out_proj(att, rec, x, gw["w_out"], g1, b1)
    gate, ge, vd, act, *gc = _ffn_up(h1b, gw["w_up_t"], fcw, fcb,
                                     _gather_exchange(shard[PACK_OFF[3]:PACK_OFF[6]]) if dist else None)
    if dist:
        gw = {**gw, **_split_pack(gc[0], 3, 6)}
    dz2, dz2b, dpre, dpp, vec2 = _ffn_down(act, z1, p, tgt, gw["w_down"], gw["w_g"], gw["w_p_t"], g1, b1,
                                           row(small["ln2_g"]), row(small["ln2_b"]), row(small["ple_gate_b"]))
    dup, dfc = _ffn_bwd(dz2b, gate, ge, vd, gw["w_down"], fcw)
    dz1, vec1 = _ffn_dh1(dup, dz2, dpre, z1, gw["w_up_t"], gw["w_g"], g1, b1)
    per_chip = 2 * D_FF // 4 // FF_CHUNK
    big = {"w_ffn_up": _weight_grad_cols(
        h1b, dup.reshape(2 * NC, -1, FF_CHUNK), "dw_up", 2 * NC,
        lambda bt: pl.BlockSpec((None, bt, FF_CHUNK), lambda m, k: (m, k, 0)), (4, D, 2 * D_FF // 4),
        pl.BlockSpec((None, D, FF_CHUNK), lambda m, k: (2 * (m % 2) + (m // 2) // per_chip, 0, (m // 2) % per_chip)))[0]}
    g_dn, *got_up = _weight_grad(act, dz2b, 512, "dw_down", _swap_exchange([big["w_ffn_up"]])) if dist else (
        _weight_grad(act, dz2b, 512, "dw_down"),)
    big["w_ffn_down"] = g_dn.reshape(4, D_FF // 4, D)
    big["ple_gate_w"] = _weight_grad(h1b, dpre, 512, "dw_gate").reshape(4, D // 4, D)
    big["ple_proj"] = _weight_grad(p, dpp, PLE, "dw_proj").reshape(PLE, 4, D // 4).transpose(1, 0, 2)
    big["w_out"] = _dw_out(att, rec, dz1).reshape(4, D // 4, D)
    reduced = None
    if dist:
        g_ffn = [big[k] for k in EARLY_WEIGHTS]
        ex = _swap_exchange(g_ffn[1:])
    datt, drec, *got = _out_proj_bwd(dz1, gw["w_out"], ex if dist else None)
    if dist:
        sums = _add_half(g_ffn, got_up + got, core, "add_half_ffn")
        ex, ex2 = _scatter_exchange(sums[:1]), _scatter_exchange(sums[1:])
    dxr, dgr, dwa, dwx, dvec, *got = _rnn_bwd(drec, gr, h, xc, xr, small["rnn_conv_w"], wa, row(small["gate_a_b"]),
                                              wx, row(small["gate_x_b"]), row(small["lru_lambda"]), ex if dist else None)
    dq, dkv, dsinks, *got2 = _attn_bwd(q, kv, datt, sinks, ex2 if dist else None)
    if dist:
        mine = _add4(got + got2, "add_chips_ffn")
        big = {}
    sg = {
        "attn_sinks": dsinks[:, 0],
        "rnn_conv_w": dvec[4:8],
        "rnn_conv_b": dvec[3],
        "gate_a_w": _diag_blocks(dwa),
        "gate_a_b": dvec[0],
        "gate_x_w": _diag_blocks(dwx),
        "gate_x_b": dvec[1],
        "lru_lambda": dvec[2],
        "ln1_g": vec1[0],
        "ln1_b": vec1[1],
        "ffn_conv_w": dfc[:, 0:3].transpose(1, 0, 2).reshape(3, D_FF),
        "ffn_conv_b": dfc[:, 3].reshape(D_FF),
        "ple_gate_b": vec2[3],
        "ln2_g": vec2[1],
        "ln2_b": vec2[2],
    }
    loss = vec2[0, 0:1]
    grad_x, du = _in_proj_bwd(dq, dkv, dxr, dgr, dz1, gw["w_in_t"])
    ex = None
    if dist:
        ex = _join_exchanges(_send_exchange(mine), _all_devices_exchange(_pack_vecs([sg[k] for k in SMALL] + [loss])[0]))
    big["w_in"], *got = _weight_grad_cols(
        xb, du, "dw_in", 4, lambda bt: pl.BlockSpec((None, bt, D_IN // 4), lambda j, k: (j, k, 0)), (4, D, D_IN // 4),
        pl.BlockSpec((None, D, D_IN // 4), lambda j, k: (j, 0, 0)), ex)
    if dist:
        reduced = (mine, got[:len(mine)])
    return grad_x, big, sg, loss, reduced, got[-1:]


BIG = ("w_in", "w_ffn_up", "w_out", "w_ffn_down", "ple_gate_w", "ple_proj")
BIG_KEYS = ("w_in_t", "w_up_t", "w_out", "w_down", "w_g", "w_p_t")
BIG_T = (True, True, False, False, False, True)
EARLY_WEIGHTS = ("w_ffn_up", "w_ffn_down", "ple_gate_w", "ple_proj", "w_out")
LATE_WEIGHTS = ("w_in",)
SMALL = ("attn_sinks", "rnn_conv_w", "rnn_conv_b", "gate_a_w", "gate_a_b", "gate_x_w", "gate_x_b", "lru_lambda",
         "ln1_g", "ln1_b", "ffn_conv_w", "ffn_conv_b", "ple_gate_b", "ln2_g", "ln2_b")
SHARDED_SMALL = ("rnn_conv_w", "ffn_conv_w")
WEIGHTS = ("w_in", "attn_sinks", "rnn_conv_w", "rnn_conv_b", "gate_a_w", "gate_a_b", "gate_x_w", "gate_x_b",
           "lru_lambda", "w_out", "ln1_g", "ln1_b", "w_ffn_up", "ffn_conv_w", "ffn_conv_b", "w_ffn_down",
           "ple_gate_w", "ple_gate_b", "ple_proj", "ln2_g", "ln2_b")


def _pack_big(d, first=0, last=6):
    parts = []
    for name, t in zip(BIG[first:last], BIG_T[first:last]):
        a = d[name]
        a = a.T if t else a
        parts.append(a.reshape(-1, 1024))
    return jnp.concatenate(parts, axis=0)


def _pack_vecs(items):
    parts, offs, n = [], [], 0
    for a in items:
        f = a.reshape(-1).astype(F32)
        pad = (-f.shape[0]) % 128
        parts.append(jnp.pad(f, (0, pad)))
        offs.append(n)
        n += (f.shape[0] + pad) // 128
    padr = (-n) % 8
    if padr:
        parts.append(jnp.zeros((padr * 128,), F32))
    return jnp.concatenate(parts).reshape(-1, 128), offs


def _unpack_vecs(a, offs, shapes):
    flat = a.reshape(-1)
    out = []
    for o, s in zip(offs, shapes):
        n = 1
        for d in s:
            n *= d
        out.append(flat[o * 128:o * 128 + n].reshape(s))
    return out


def kernel(x, p, w_in, attn_sinks, rnn_conv_w, rnn_conv_b, gate_a_w, gate_a_b, gate_x_w, gate_x_b, lru_lambda, w_out, ln1_g, ln1_b, w_ffn_up, ffn_conv_w, ffn_conv_b, w_ffn_down, ple_gate_w, ple_gate_b, ple_proj, ln2_g, ln2_b, loss_target, m_w_in, m_attn_sinks, m_rnn_conv_w, m_rnn_conv_b, m_gate_a_w, m_gate_a_b, m_gate_x_w, m_gate_x_b, m_lru_lambda, m_w_out, m_ln1_g, m_ln1_b, m_w_ffn_up, m_ffn_conv_w, m_ffn_conv_b, m_w_ffn_down, m_ple_gate_w, m_ple_gate_b, m_ple_proj, m_ln2_g, m_ln2_b, v_w_in, v_attn_sinks, v_rnn_conv_w, v_rnn_conv_b, v_gate_a_w, v_gate_a_b, v_gate_x_w, v_gate_x_b, v_lru_lambda, v_w_out, v_ln1_g, v_ln1_b, v_w_ffn_up, v_ffn_conv_w, v_ffn_conv_b, v_w_ffn_down, v_ple_gate_w, v_ple_gate_b, v_ple_proj, v_ln2_g, v_ln2_b):
    w = dict(w_in=w_in, attn_sinks=attn_sinks, rnn_conv_w=rnn_conv_w, rnn_conv_b=rnn_conv_b, gate_a_w=gate_a_w,
             gate_a_b=gate_a_b, gate_x_w=gate_x_w, gate_x_b=gate_x_b, lru_lambda=lru_lambda, w_out=w_out, ln1_g=ln1_g,
             ln1_b=ln1_b, w_ffn_up=w_ffn_up, ffn_conv_w=ffn_conv_w, ffn_conv_b=ffn_conv_b, w_ffn_down=w_ffn_down,
             ple_gate_w=ple_gate_w, ple_gate_b=ple_gate_b, ple_proj=ple_proj, ln2_g=ln2_g, ln2_b=ln2_b)
    m = dict(w_in=m_w_in, attn_sinks=m_attn_sinks, rnn_conv_w=m_rnn_conv_w, rnn_conv_b=m_rnn_conv_b, gate_a_w=m_gate_a_w,
             gate_a_b=m_gate_a_b, gate_x_w=m_gate_x_w, gate_x_b=m_gate_x_b, lru_lambda=m_lru_lambda, w_out=m_w_out,
             ln1_g=m_ln1_g, ln1_b=m_ln1_b, w_ffn_up=m_w_ffn_up, ffn_conv_w=m_ffn_conv_w, ffn_conv_b=m_ffn_conv_b,
             w_ffn_down=m_w_ffn_down, ple_gate_w=m_ple_gate_w, ple_gate_b=m_ple_gate_b, ple_proj=m_ple_proj,
             ln2_g=m_ln2_g, ln2_b=m_ln2_b)
    v = dict(w_in=v_w_in, attn_sinks=v_attn_sinks, rnn_conv_w=v_rnn_conv_w, rnn_conv_b=v_rnn_conv_b, gate_a_w=v_gate_a_w,
             gate_a_b=v_gate_a_b, gate_x_w=v_gate_x_w, gate_x_b=v_gate_x_b, lru_lambda=v_lru_lambda, w_out=v_w_out,
             ln1_g=v_ln1_g, ln1_b=v_ln1_b, w_ffn_up=v_w_ffn_up, ffn_conv_w=v_ffn_conv_w, ffn_conv_b=v_ffn_conv_b,
             w_ffn_down=v_w_ffn_down, ple_gate_w=v_ple_gate_w, ple_gate_b=v_ple_gate_b, ple_proj=v_ple_proj,
             ln2_g=v_ln2_g, ln2_b=v_ln2_b)
    w, m, v = ({k: a[0] for k, a in d.items()} for d in (w, m, v))
    chip = 2 * lax.axis_index("x") + lax.axis_index("y")
    core = lax.axis_index("c")

    wpack = _pack_big(w)
    cpack, _ = _pack_vecs([w["rnn_conv_w"], w["ffn_conv_w"]])
    shard = wpack.astype(MXU_DTYPE)
    g_in, gcp = _gather_first(shard[PACK_OFF[0]:PACK_OFF[1]], cpack)
    gw = _split_pack(g_in, 0, 1)
    small = {k: w[k] for k in SMALL}
    small["rnn_conv_w"] = gcp[:, 0:4].reshape(4, 4, 128).transpose(1, 0, 2).reshape(4, 512)
    small["ffn_conv_w"] = gcp[:, 4:22].reshape(4, 3, 768).transpose(1, 0, 2).reshape(3, 3072)

    core1 = core.reshape(1).astype(jnp.int32)
    grad_x, big, sg, loss, ffn_halves, small_all = _layer_grads(x[0], p[0, 0], loss_target[0], gw, small, shard, core1)

    shapes = [sg[k].shape for k in SMALL] + [(1,)]
    _, offs = _pack_vecs([jnp.zeros(s, F32) for s in shapes])
    red = dict(zip(SMALL + ("loss",), _unpack_vecs(_sum_devices(small_all[0]), offs, shapes)))
    red["rnn_conv_w"] = lax.dynamic_slice_in_dim(red["rnn_conv_w"], chip * 128, 128, axis=1)
    red["ffn_conv_w"] = lax.dynamic_slice_in_dim(red["ffn_conv_w"], chip * 768, 768, axis=1)

    late_mine, late_other = ([a] for a in _reduce_in_vmem(big["w_in"]))

    def adamw(names, mine, other, name):
        out, _ = _adamw_halves([w[k] for k in names], mine, other, [m[k] for k in names], [v[k] for k in names],
                               core1, name)
        return dict(zip(names, out))

    big_out = {**adamw(LATE_WEIGHTS, late_mine, late_other, "adamw_late"), **adamw(EARLY_WEIGHTS, *ffn_halves, "adamw_early")}
    wsm, offs2 = _pack_vecs([w[k] for k in SMALL])
    gsm, _ = _pack_vecs([red[k] for k in SMALL])
    msm, _ = _pack_vecs([m[k] for k in SMALL])
    vsm, _ = _pack_vecs([v[k] for k in SMALL])
    dsm, nmsm, nvsm = _adamw(wsm, gsm, msm, vsm, "adamw_small")
    shapes2 = [w[k].shape for k in SMALL]

    def named(n, smallp):
        d = {k: out[n][None] for k, out in big_out.items()}
        d.update({k: a[None] for k, a in zip(SMALL, _unpack_vecs(smallp, offs2, shapes2))})
        return [d[k] for k in WEIGHTS]

    return (red["loss"].reshape(()), grad_x[None], *named(0, gsm), *named(1, dsm), *named(2, nmsm), *named(3, nvsm))
```

```python
import functools

import jax
import jax.numpy as jnp
from jax import lax
from jax.experimental import pallas as pl
from jax.experimental.pallas import tpu as pltpu

F32 = jnp.float32
BF16 = jnp.bfloat16
MXU_DTYPE = jnp.bfloat16

D = 1024
D_ATT = 512
D_KV = 128
D_RNN = 512
D_IN = 1792
D_FF = 3072
FF_CHUNK = 512
PLE = 256
HEADS = 8
HEAD_DIM = 64
BLK = 128
ATTN_BLOCKS = 8
DW_TOKENS = 4096
RNN_BLOCKS = 8
LN_EPS = 1e-5
LRU_C = 8.0
ALPHA = float(2.0 ** 0.25)
SCALE = HEAD_DIM ** -0.5
NEG = -1e30

ADAM_LR = 0.001
ADAM_B1 = 0.9
ADAM_B2 = 0.999
ADAM_EPS = 1e-08
ADAM_WD = 0.01
ADAM_STEP = 10

VMEM_LIMIT_BYTES = 56 * 1024 * 1024
MESH = pl.DeviceIdType.MESH

PACK_ROWS = (448, 1536, 256, 768, 256, 64)
PACK_OFF = tuple(sum(PACK_ROWS[:i]) for i in range(len(PACK_ROWS) + 1))
PACK_TOTAL = PACK_OFF[-1]


def _params(**kw):
    return pltpu.CompilerParams(vmem_limit_bytes=VMEM_LIMIT_BYTES, **kw)


def _mm(a, b):
    return jnp.dot(a.astype(MXU_DTYPE), b.astype(MXU_DTYPE), preferred_element_type=F32)


def _mm_nt(a, b):
    return lax.dot_general(a.astype(MXU_DTYPE), b.astype(MXU_DTYPE), (((1,), (1,)), ((), ())),
                           preferred_element_type=F32)


def _mm_tn(a, b):
    return lax.dot_general(a.astype(MXU_DTYPE), b.astype(MXU_DTYPE), (((0,), (0,)), ((), ())),
                           preferred_element_type=F32)


def _sigmoid(x):
    return 0.5 + 0.5 * jnp.tanh(0.5 * x)


def _gelu(x):
    c = 0.7978845608028654
    k = 0.044715
    x2 = x * x
    t = jnp.tanh(x * (c + (c * k) * x2))
    h = 0.5 * (1.0 + t)
    return x * h, h * (1.0 + (x * (1.0 - t)) * (c + (3.0 * c * k) * x2))


def _shift_rows(x, s, edge8):
    R = x.shape[0]
    row8 = lax.broadcasted_iota(jnp.int32, (8, x.shape[1]), 0)
    if s > 0:
        rolled = pltpu.roll(x, s, 0)
        first = jnp.where(row8 < s, pltpu.roll(edge8, s, 0), rolled[0:8])
        return jnp.concatenate([first, rolled[8:]], axis=0)
    k = -s
    rolled = pltpu.roll(x, R - k, 0)
    last = jnp.where(row8 >= 8 - k, pltpu.roll(edge8, 8 - k, 0), rolled[R - 8:])
    return jnp.concatenate([rolled[:R - 8], last], axis=0)


def _softplus(x):
    return jnp.maximum(x, 0.0) + jnp.log(1.0 + jnp.exp(-jnp.abs(x)))


def _ln(z, g, b):
    mu = jnp.mean(z, axis=-1, keepdims=True)
    zc = z - mu
    var = jnp.mean(zc * zc, axis=-1, keepdims=True)
    rstd = lax.rsqrt(var + LN_EPS)
    xhat = zc * rstd
    return xhat * g + b, xhat, rstd


def _ln_bwd(dy, xhat, rstd, g):
    dxh = dy * g
    m1 = jnp.mean(dxh, axis=-1, keepdims=True)
    m2 = jnp.mean(dxh * xhat, axis=-1, keepdims=True)
    return rstd * (dxh - m1 - xhat * m2)


def _colsum(x):
    return jnp.sum(x, axis=0, keepdims=True)


def _full(shape):
    nd = len(shape)
    return pl.BlockSpec(shape, lambda *_: (0,) * nd)


def _rows(tm, cols, fn=None):
    if fn is None:
        return pl.BlockSpec((tm, cols), lambda i: (i, 0))
    return pl.BlockSpec((tm, cols), lambda i: (fn(i), 0))


def _heads(tm):
    return pl.BlockSpec((HEADS, tm, HEAD_DIM), lambda i: (0, i, 0))


def _in_proj(x, w_in_t):
    T = x.shape[0]
    tm = min(1024, T)

    def body(x_ref, w_ref, q_ref, kv_ref, xr_ref, gr_ref, xb_ref):
        xb = x_ref[...].astype(MXU_DTYPE)
        xb_ref[...] = xb.astype(BF16)
        q = _mm_nt(xb, w_ref[0:512, :])
        for h in range(HEADS):
            q_ref[h] = q[:, h * 64:(h + 1) * 64].astype(BF16)
        kv_ref[...] = _mm_nt(xb, w_ref[512:768, :]).astype(BF16)
        xr_ref[...] = _mm_nt(xb, w_ref[768:1280, :])
        gr_ref[...] = _mm_nt(xb, w_ref[1280:1792, :])

    return pl.pallas_call(
        body, name="in_proj", grid=(T // tm,),
        in_specs=[_rows(tm, D), _full((D_IN, D))],
        out_specs=[_heads(tm), _rows(tm, 256), _rows(tm, 512), _rows(tm, 512), _rows(tm, D)],
        out_shape=[jax.ShapeDtypeStruct((HEADS, T, 64), BF16), jax.ShapeDtypeStruct((T, 256), BF16),
                   jax.ShapeDtypeStruct((T, 512), F32), jax.ShapeDtypeStruct((T, 512), F32),
                   jax.ShapeDtypeStruct((T, D), BF16)],
        compiler_params=_params(),
    )(x, w_in_t)


def _attn_band(kv_ref, i):
    cur = pl.multiple_of(i * BLK, BLK)
    prev = pl.multiple_of(jnp.maximum(i - 1, 0) * BLK, BLK)
    band = jnp.concatenate([kv_ref[pl.ds(prev, BLK), :], kv_ref[pl.ds(cur, BLK), :]], axis=0)
    key = lax.broadcasted_iota(jnp.int32, (2 * BLK, 4 * BLK), 0)
    qry = lax.broadcasted_iota(jnp.int32, (2 * BLK, 4 * BLK), 1) & (BLK - 1)
    in_prev = jnp.logical_and(jnp.logical_and(key < BLK, key > qry), i > 0)
    mask = jnp.logical_or(in_prev, jnp.logical_and(key >= BLK, key - BLK <= qry))
    return band, mask, cur, prev


def _attn_scores(band, mask, qs, s_ref, g):
    st = jnp.where(mask, _mm_nt(band[:, g * 64:(g + 1) * 64], qs) * SCALE, NEG)
    lane = lax.broadcasted_iota(jnp.int32, (1, 4 * BLK), 1)
    sv = jnp.where(lane < BLK, s_ref[0, 4 * g],
                   jnp.where(lane < 2 * BLK, s_ref[0, 4 * g + 1], jnp.where(lane < 3 * BLK, s_ref[0, 4 * g + 2], s_ref[0, 4 * g + 3])))
    m = jnp.maximum(jnp.max(st, axis=0, keepdims=True), sv)
    p = jnp.exp(st - m)
    ps = jnp.exp(sv - m)
    return p, ps, jnp.sum(p, axis=0, keepdims=True) + ps


def _pos():
    return lax.axis_index("x"), lax.axis_index("y"), lax.axis_index("c")


def _other_chips(x, y):
    return [(1 - x, y), (x, 1 - y), (1 - x, 1 - y)]


def _gather_steps(w_ref, gw_ref, send_sems, recv_sems, local_sem):
    x, y, c = _pos()
    me = 2 * x + y
    chips = _other_chips(x, y)
    half = w_ref.shape[0] // 2
    mine = pl.ds(pl.multiple_of(c * half, 16), half)
    theirs = pl.ds(pl.multiple_of((1 - c) * half, 16), half)
    loc = pltpu.make_async_copy(w_ref, gw_ref.at[me], local_sem)

    def copy(k, src, dst, to):
        return pltpu.make_async_remote_copy(src_ref=src, dst_ref=dst, send_sem=send_sems.at[k], recv_sem=recv_sems.at[k],
                                            device_id=to, device_id_type=MESH)

    def out(k):
        px, py = chips[k]
        return copy(k, w_ref.at[mine], gw_ref.at[me, mine], (px, py, c))

    def fwd(k, rows):
        px, py = chips[k]
        return copy(3 + k, gw_ref.at[2 * px + py, rows], gw_ref.at[2 * px + py, rows], (x, y, 1 - c))

    def start():
        loc.start()
        for k in range(3):
            out(k).start()

    def forward():
        for k in range(3):
            px, py = chips[k]
            copy(k, w_ref.at[mine], gw_ref.at[2 * px + py, mine], (px, py, c)).wait_recv()
            fwd(k, mine).start()

    def finish():
        for k in range(3):
            fwd(k, theirs).wait_recv()
        for k in range(3):
            out(k).wait_send()
            fwd(k, mine).wait_send()
        loc.wait()

    return start, forward, finish


GATHER_SCRATCH = [pltpu.SemaphoreType.DMA((6,)), pltpu.SemaphoreType.DMA((6,)), pltpu.SemaphoreType.DMA]


class _Exchange:
    def __init__(self, args, out_shape, scratch, make):
        self.args, self.out_shape, self.scratch, self.make = list(args), list(out_shape), list(scratch), make


def _join_exchanges(a, b):
    na, nao, nas = len(a.args), len(a.out_shape), len(a.scratch)

    def make(ins, outs, sems):
        steps_a = a.make(ins[:na], outs[:nao], sems[:nas])
        steps_b = b.make(ins[na:], outs[nao:], sems[nas:])

        def both(f, g):
            def run():
                f()
                g()
            return run

        return tuple(both(f, g) for f, g in zip(steps_a, steps_b))

    return _Exchange(a.args + b.args, a.out_shape + b.out_shape, a.scratch + b.scratch, make)


def _gather_exchange(wsrc):
    return _Exchange([wsrc], [jax.ShapeDtypeStruct((4,) + wsrc.shape, wsrc.dtype)], GATHER_SCRATCH,
                     lambda ins, outs, sems: _gather_steps(ins[0], outs[0], *sems))


def _launch(body, name, grid, in_specs, out_specs, out_shape, scratch, args, exchange=None, prefetch=0):
    def call(fn, fn_name, ins, outs, shapes, scr, operands, effects):
        spec = pltpu.PrefetchScalarGridSpec(num_scalar_prefetch=prefetch, grid=grid, in_specs=ins, out_specs=outs,
                                            scratch_shapes=scr)
        return pl.pallas_call(fn, name=fn_name, grid_spec=spec, out_shape=shapes,
                              compiler_params=_params(has_side_effects=effects))(*operands)

    if exchange is None:
        return call(body, name, list(in_specs), list(out_specs), list(out_shape), list(scratch), args, False)
    n_in, n_out, ei, eo, ns = len(in_specs), len(out_specs), len(exchange.args), len(exchange.out_shape), len(exchange.scratch)
    nsteps = 1
    for g in grid:
        nsteps *= g

    def wrapped(*refs):
        scalars, refs = refs[:prefetch], refs[prefetch:]
        ins, xin = refs[:n_in], refs[n_in:n_in + ei]
        outs, xout = refs[n_in + ei:n_in + ei + n_out], refs[n_in + ei + n_out:n_in + ei + n_out + eo]
        rest = refs[n_in + ei + n_out + eo:]
        own, sems = rest[:len(rest) - ns], rest[len(rest) - ns:]
        start, forward, finish = exchange.make(xin, xout, sems)
        i = pl.program_id(0)
        for d in range(1, len(grid)):
            i = i * grid[d] + pl.program_id(d)
        pl.when(i == 0)(start)
        body(*scalars, *ins, *outs, *own)
        pl.when(i == max(nsteps - 3, 0))(forward)
        pl.when(i == nsteps - 1)(finish)

    anyspec = pl.BlockSpec(memory_space=pl.ANY)
    return call(wrapped, name + "_x", list(in_specs) + [anyspec] * ei, list(out_specs) + [anyspec] * eo,
                list(out_shape) + exchange.out_shape, list(scratch) + exchange.scratch, (*args, *exchange.args), True)


def _attn_fwd(q, kv, sinks, exchange=None):
    T = kv.shape[0]
    nblk = min(ATTN_BLOCKS, T // BLK)

    def body(q_ref, kv_ref, s_ref, o_ref):
        for b in range(nblk):
            rows = slice(b * BLK, (b + 1) * BLK)
            band, mask, _, _ = _attn_band(kv_ref, nblk * pl.program_id(0) + b)
            for g in range(2):
                qs = q_ref[4 * g:4 * g + 4, rows, :].reshape(4 * BLK, HEAD_DIM)
                p, _, den = _attn_scores(band, mask, qs, s_ref, g)
                ot = _mm_tn(band[:, 128:256], p) * (1.0 / den)
                for hh in range(4):
                    o = ot[:, hh * BLK:(hh + 1) * BLK].T
                    o_ref[rows, (4 * g + hh) * 64:(4 * g + hh + 1) * 64] = o[:, g * 64:(g + 1) * 64].astype(BF16)

    tq = nblk * BLK
    return _launch(body, "attn_fwd", (T // tq,), [_heads(tq), _full((T, 256)), pl.BlockSpec(memory_space=pltpu.SMEM)],
                   [_rows(tq, 512)], [jax.ShapeDtypeStruct((T, 512), BF16)], [], (q, kv, sinks), exchange)


def _attn_bwd(q, kv, do, sinks, exchange=None):
    T = kv.shape[0]
    nblk = min(ATTN_BLOCKS, T // BLK)

    def body(q_ref, kv_ref, do_ref, s_ref, dq_ref, dkv_ref, ds_ref):
        @pl.when(pl.program_id(0) == 0)
        def _():
            ds_ref[...] = jnp.zeros_like(ds_ref)

        for b in range(nblk):
            rows = slice(b * BLK, (b + 1) * BLK)
            band, mask, cur, prev = _attn_band(kv_ref, nblk * pl.program_id(0) + b)
            for g in range(2):
                qs = q_ref[4 * g:4 * g + 4, rows, :].reshape(4 * BLK, HEAD_DIM)
                dos = do_ref[4 * g:4 * g + 4, rows, :].reshape(4 * BLK, HEAD_DIM)
                p, ps, den = _attn_scores(band, mask, qs, s_ref, g)
                inv = 1.0 / den
                p = p * inv
                dpt = _mm_nt(band[:, 128 + g * 64:192 + g * 64], dos)
                delta = jnp.sum(p * dpt, axis=0, keepdims=True)
                dst = p * (dpt - delta)
                dsv = -(ps * inv) * delta
                for hh in range(4):
                    dsink = jnp.sum(dsv[:, hh * BLK:(hh + 1) * BLK], axis=1, keepdims=True)
                    ds_ref[4 * g + hh:4 * g + hh + 1, :] += jnp.broadcast_to(dsink, (1, 128))
                dqt = _mm_tn(band[:, 0:128], dst) * SCALE
                for hh in range(4):
                    dqh = dqt[:, hh * BLK:(hh + 1) * BLK].T
                    dq_ref[rows, (4 * g + hh) * 64:(4 * g + hh + 1) * 64] = dqh[:, g * 64:(g + 1) * 64].astype(BF16)
                dk = _mm(dst, qs) * SCALE
                dv = _mm(p, dos)
                dkv_ref[pl.ds(cur, BLK), g * 64:(g + 1) * 64] = dk[BLK:2 * BLK]
                dkv_ref[pl.ds(cur, BLK), 128 + g * 64:192 + g * 64] = dv[BLK:2 * BLK]
                dkv_ref[pl.ds(prev, BLK), g * 64:(g + 1) * 64] += dk[0:BLK]
                dkv_ref[pl.ds(prev, BLK), 128 + g * 64:192 + g * 64] += dv[0:BLK]

    tq = nblk * BLK
    return _launch(body, "attn_bwd", (T // tq,),
                   [_heads(tq), _full((T, 256)), _heads(tq), pl.BlockSpec(memory_space=pltpu.SMEM)],
                   [_rows(tq, 512), _full((T, 256)), _full((8, 128))],
                   [jax.ShapeDtypeStruct((T, 512), BF16), jax.ShapeDtypeStruct((T, 256), F32),
                    jax.ShapeDtypeStruct((8, 128), F32)], [], (q, kv, do, sinks), exchange)


def _rows8(tm, cols):
    return lax.broadcasted_iota(jnp.int32, (tm, cols), 0) & 7


def _lru_gates(xc, wa, ba, wx, bx, lam):
    r = _sigmoid(_mm(xc, wa) + ba)
    ii = _sigmoid(_mm(xc, wx) + bx)
    sp = _softplus(-lam)
    la = -LRU_C * r * sp
    a = jnp.exp(la)
    m = jnp.sqrt(-jnp.tanh(la) * (a * a + 1.0))
    return r, ii, sp, a, m


def _rnn_fwd(xr, gr, cw, cb, wa, ba, wx, bx, lam, exchange=None):
    T = xr.shape[0]
    tm = 512
    C = D_RNN

    def body(xr_ref, gr_ref, cw_ref, cb_ref, wa_ref, ba_ref, wx_ref, bx_ref, lam_ref,
             xc_ref, h_ref, rec_ref, ext, a_s, b_s, carry):
        i = pl.program_id(0)

        @pl.when(i == 0)
        def _():
            ext[...] = jnp.zeros((8, C), F32)
            carry[...] = jnp.zeros((8, C), F32)

        xr = xr_ref[...]
        edge = ext[...]
        xc = cb_ref[...] + cw_ref[3:4, :] * xr
        for k in range(3):
            xc = xc + cw_ref[k:k + 1, :] * _shift_rows(xr, 3 - k, edge)
        ext[...] = xr[tm - 8:tm, :]
        xc_ref[...] = xc
        _, ii, _, a, m = _lru_gates(xc, wa_ref[...], ba_ref[...], wx_ref[...], bx_ref[...], lam_ref[...])
        b = m * ii * xc
        r8 = _rows8(tm, C)
        for d in (1, 2, 4):
            ok = r8 >= d
            a_sh = jnp.where(ok, pltpu.roll(a, d, 0), 1.0)
            b_sh = jnp.where(ok, pltpu.roll(b, d, 0), 0.0)
            b = a * b_sh + b
            a = a * a_sh
        a_s[...] = a
        b_s[...] = b

        def step(g, hin):
            s = pl.multiple_of(g * 8, 8)
            hg = a_s[pl.ds(s, 8), :] * hin + b_s[pl.ds(s, 8), :]
            h_ref[pl.ds(s, 8), :] = hg
            return jnp.broadcast_to(hg[7:8, :], (8, C))

        carry[...] = lax.fori_loop(0, tm // 8, step, carry[...], unroll=4)
        ge, _ = _gelu(gr_ref[...])
        rec_ref[...] = (h_ref[...] * ge).astype(BF16)

    vec = _full((1, C))
    in_specs = [_rows(tm, C), _rows(tm, C), _full((4, C)), vec, _full((C, C)), vec, _full((C, C)), vec, vec]
    out_specs = [_rows(tm, C), _rows(tm, C), _rows(tm, C)]
    out_shape = [jax.ShapeDtypeStruct((T, C), F32), jax.ShapeDtypeStruct((T, C), F32), jax.ShapeDtypeStruct((T, C), BF16)]
    scratch = [pltpu.VMEM((8, C), F32), pltpu.VMEM((tm, C), F32), pltpu.VMEM((tm, C), F32), pltpu.VMEM((8, C), F32)]
    return _launch(body, "rnn_fwd", (T // tm,), in_specs, out_specs, out_shape, scratch,
                   (xr, gr, cw, cb, wa, ba, wx, bx, lam), exchange)


def _rnn_bwd(drec, gr, h, xc, xr, cw, wa, ba, wx, bx, lam, exchange=None):
    T = xr.shape[0]
    tm = 512
    C = D_RNN
    nt = T // tm
    t8 = tm // 8

    def body(drec_ref, gr_ref, h_ref, hp_ref, xc_ref, xr_ref, cw_ref, wa_ref, ba_ref, wx_ref, bx_ref,
             lam_ref, dxr_ref, dgr_ref, dwa_ref, dwx_ref, dvec_ref, c_s, g_s, gout, ext, anext, gcarry):
        i = pl.program_id(0)
        j = nt - 1 - i

        @pl.when(i == 0)
        def _():
            dwa_ref[...] = jnp.zeros_like(dwa_ref)
            dwx_ref[...] = jnp.zeros_like(dwx_ref)
            dvec_ref[...] = jnp.zeros_like(dvec_ref)
            anext[...] = jnp.zeros((8, C), F32)
            gcarry[...] = jnp.zeros((8, C), F32)
            ext[...] = jnp.zeros((8, C), F32)

        xc = xc_ref[...]
        lam = lam_ref[...]
        r, ii, sp, a, m = _lru_gates(xc, wa_ref[...], ba_ref[...], wx_ref[...], bx_ref[...], lam)
        ge, dge = _gelu(gr_ref[...])
        drec = drec_ref[...]
        hh = h_ref[...]
        dgr_ref[...] = (drec * hh * dge).astype(BF16)
        dh = drec * ge
        rowi = lax.broadcasted_iota(jnp.int32, (tm, C), 0)
        c = jnp.where(rowi == tm - 1, jnp.broadcast_to(anext[0:1, :], (tm, C)), pltpu.roll(a, tm - 1, 0))
        anext[...] = a[0:8, :]
        r8 = rowi & 7
        gg = dh
        for d in (1, 2, 4):
            ok = r8 < 8 - d
            c_sh = jnp.where(ok, pltpu.roll(c, tm - d, 0), 1.0)
            g_sh = jnp.where(ok, pltpu.roll(gg, tm - d, 0), 0.0)
            gg = c * g_sh + gg
            c = c * c_sh
        c_s[...] = c
        g_s[...] = gg

        def step(k, gin):
            s = pl.multiple_of((t8 - 1 - k) * 8, 8)
            og = c_s[pl.ds(s, 8), :] * gin + g_s[pl.ds(s, 8), :]
            gout[pl.ds(s, 8), :] = og
            return jnp.broadcast_to(og[0:1, :], (8, C))

        gcarry[...] = lax.fori_loop(0, t8, step, gcarry[...], unroll=4)
        G = gout[...]
        hprev_row = jnp.where(j > 0, hp_ref[7:8, :], 0.0)
        hprev = jnp.where(rowi == 0, jnp.broadcast_to(hprev_row, (tm, C)), pltpu.roll(hh, 1, 0))
        da = G * hprev
        dm = G * ii * xc
        di = G * m * xc
        dxc = G * m * ii
        dla = da * a - dm * a * a / m
        dr = dla * (-LRU_C * sp)
        dsp = _colsum(dla * (-LRU_C * r))
        dlam = dsp * (-_sigmoid(-lam))
        dpr = dr * r * (1.0 - r)
        dpi = di * ii * (1.0 - ii)
        dxc = dxc + _mm_nt(dpr, wa_ref[...]) + _mm_nt(dpi, wx_ref[...])
        dwa_ref[...] += _mm_tn(xc, dpr)
        dwx_ref[...] += _mm_tn(xc, dpi)
        dvec_ref[0:1, :] += _colsum(dpr)
        dvec_ref[1:2, :] += _colsum(dpi)
        dvec_ref[2:3, :] += dlam
        dvec_ref[3:4, :] += _colsum(dxc)
        edge = ext[...]
        xr = xr_ref[...]
        dxr = cw_ref[3:4, :] * dxc
        dvec_ref[7:8, :] += _colsum(dxc * xr)
        for k in range(3):
            up = _shift_rows(dxc, k - 3, edge)
            dxr = dxr + cw_ref[k:k + 1, :] * up
            dvec_ref[4 + k:5 + k, :] += _colsum(up * xr)
        ext[...] = dxc[0:8, :]
        dxr_ref[...] = dxr.astype(BF16)

    rev = lambda i: nt - 1 - i
    prev8 = lambda i: jnp.maximum((nt - 1 - i) * t8 - 1, 0)
    vec = _full((1, C))
    return _launch(
        body, "rnn_bwd", (nt,),
        [_rows(tm, C, rev), _rows(tm, C, rev), _rows(tm, C, rev), _rows(8, C, prev8), _rows(tm, C, rev),
         _rows(tm, C, rev), _full((4, C)), _full((C, C)), vec, _full((C, C)), vec, vec],
        [_rows(tm, C, rev), _rows(tm, C, rev), _full((C, C)), _full((C, C)), _full((8, C))],
        [jax.ShapeDtypeStruct((T, C), BF16), jax.ShapeDtypeStruct((T, C), BF16),
         jax.ShapeDtypeStruct((C, C), F32), jax.ShapeDtypeStruct((C, C), F32), jax.ShapeDtypeStruct((8, C), F32)],
        [pltpu.VMEM((tm, C), F32), pltpu.VMEM((tm, C), F32), pltpu.VMEM((tm, C), F32),
         pltpu.VMEM((8, C), F32), pltpu.VMEM((8, C), F32), pltpu.VMEM((8, C), F32)],
        (drec, gr, h, h, xc, xr, cw, wa, ba, wx, bx, lam), exchange)


def _out_proj(att, rec, x, w_out, g1, b1):
    T = x.shape[0]
    tm = min(1024, T)

    def body(att_ref, rec_ref, x_ref, w_ref, g1_ref, b1_ref, z_ref, h_ref):
        mix = _mm(att_ref[...], w_ref[0:512, :]) + _mm(rec_ref[...], w_ref[512:1024, :])
        z1 = ALPHA * x_ref[...] + mix
        z_ref[...] = z1
        h1, _, _ = _ln(z1, g1_ref[...], b1_ref[...])
        h_ref[...] = h1.astype(MXU_DTYPE).astype(BF16)

    return pl.pallas_call(
        body, name="out_proj", grid=(T // tm,),
        in_specs=[_rows(tm, 512), _rows(tm, 512), _rows(tm, D), _full((D, D)), _full((1, D)), _full((1, D))],
        out_specs=[_rows(tm, D), _rows(tm, D)],
        out_shape=[jax.ShapeDtypeStruct((T, D), F32), jax.ShapeDtypeStruct((T, D), BF16)],
        compiler_params=_params(),
    )(att, rec, x, w_out, g1, b1)


NC = D_FF // FF_CHUNK


def _ffn_up(h1b, w_up_t, fcw, fcb, exchange=None):
    T = h1b.shape[0]
    tm = min(1024, T)
    CW = FF_CHUNK

    def body(h_ref, wg_ref, wv_ref, fcw_ref, fcb_ref, gate_ref, ge_ref, vd_ref, act_ref, before):
        i = pl.program_id(1)

        @pl.when(i == 0)
        def _():
            before[...] = jnp.zeros((8, CW), F32)

        hb = h_ref[...]
        gate = _mm_nt(hb, wg_ref[...])
        val = _mm_nt(hb, wv_ref[...])
        gate_ref[...] = gate.astype(BF16)
        edge = before[...]
        gc = (fcb_ref[...] + fcw_ref[0:1, :] * _shift_rows(gate, 2, edge) + fcw_ref[1:2, :] * _shift_rows(gate, 1, edge)
              + fcw_ref[2:3, :] * gate)
        before[...] = gate[tm - 8:tm, :]
        ge, dge = _gelu(gc)
        ge_ref[...] = ge.astype(BF16)
        vd_ref[...] = (val * dge).astype(BF16)
        act_ref[...] = (ge * val).astype(BF16)

    chunk = pl.BlockSpec((None, tm, CW), lambda c, i: (c, i, 0))
    return _launch(
        body, "ffn_up", (NC, T // tm),
        [pl.BlockSpec((tm, D), lambda c, i: (i, 0)), pl.BlockSpec((CW, D), lambda c, i: (c, 0)),
         pl.BlockSpec((CW, D), lambda c, i: (NC + c, 0)), pl.BlockSpec((None, 3, CW), lambda c, i: (c, 0, 0)),
         pl.BlockSpec((None, 1, CW), lambda c, i: (c, 0, 0))],
        [chunk] * 4, [jax.ShapeDtypeStruct((NC, T, CW), BF16)] * 4, [pltpu.VMEM((8, CW), F32)],
        (h1b, w_up_t, w_up_t, fcw, fcb), exchange)


def _ffn_down(act, z1, p, tgt, w_down, w_g, w_p_t, g1, b1, g2, b2, bg):
    T = z1.shape[0]
    tm = 512

    def body(act_ref, z_ref, p_ref, t_ref, wdn_hbm, wg_hbm, wp_hbm, g1_ref, b1_ref, g2_ref, b2_ref, bg_ref,
             dz2_ref, dz2b_ref, dpre_ref, dpp_ref, vec_ref, wdn, wg, wp):
        @pl.when(pl.program_id(0) == 0)
        def _():
            pltpu.sync_copy(wdn_hbm, wdn)
            pltpu.sync_copy(wg_hbm, wg)
            pltpu.sync_copy(wp_hbm, wp)
            vec_ref[...] = jnp.zeros_like(vec_ref)

        g2v = g2_ref[...]
        for r in (slice(0, tm // 2), slice(tm // 2, tm)):
            h1, _, _ = _ln(z_ref[r, :], g1_ref[...], b1_ref[...])
            h1b = h1.astype(MXU_DTYPE)
            ffn = _mm(act_ref[0, r, :], wdn[0:FF_CHUNK, :])
            for c in range(1, NC):
                ffn = ffn + _mm(act_ref[c, r, :], wdn[c * FF_CHUNK:(c + 1) * FF_CHUNK, :])
            sg = _sigmoid(_mm(h1b, wg[...]) + bg_ref[...])
            pp = _mm_nt(p_ref[r, :], wp[...])
            z2 = ALPHA * h1 + ffn + sg * pp
            y, xh2, rstd2 = _ln(z2, g2v, b2_ref[...])
            diff = y - t_ref[r, :]
            dy = diff * (1.0 / D)
            dz2 = _ln_bwd(dy, xh2, rstd2, g2v)
            dpre = dz2 * pp * sg * (1.0 - sg)
            dz2_ref[r, :] = dz2
            dz2b_ref[r, :] = dz2.astype(BF16)
            dpre_ref[r, :] = dpre.astype(BF16)
            dpp_ref[r, :] = (dz2 * sg).astype(BF16)
            loss = 0.5 * jnp.sum(jnp.sum(diff * diff, axis=1, keepdims=True), axis=0, keepdims=True) * (1.0 / D)
            vec_ref[0:1, :] += jnp.broadcast_to(loss, (1, D))
            vec_ref[1:2, :] += _colsum(dy * xh2)
            vec_ref[2:3, :] += _colsum(dy)
            vec_ref[3:4, :] += _colsum(dpre)

    anyspec = pl.BlockSpec(memory_space=pl.ANY)
    vec = _full((1, D))
    return pl.pallas_call(
        body, name="ffn_down", grid=(T // tm,),
        in_specs=[pl.BlockSpec((NC, tm, FF_CHUNK), lambda i: (0, i, 0)), _rows(tm, D), _rows(tm, PLE), _rows(tm, D),
                  anyspec, anyspec, anyspec] + [vec] * 5,
        out_specs=[_rows(tm, D)] * 4 + [_full((8, D))],
        out_shape=[jax.ShapeDtypeStruct((T, D), F32)] + [jax.ShapeDtypeStruct((T, D), BF16)] * 3
                  + [jax.ShapeDtypeStruct((8, D), F32)],
        scratch_shapes=[pltpu.VMEM((D_FF, D), MXU_DTYPE), pltpu.VMEM((D, D), MXU_DTYPE), pltpu.VMEM((D, PLE), MXU_DTYPE)],
        compiler_params=_params(),
    )(act, z1, p, tgt, w_down, w_g, w_p_t, g1, b1, g2, b2, bg)


def _ffn_bwd(dz2b, gate, ge, vd, w_down, fcw):
    T = dz2b.shape[0]
    tm = min(1024, T)
    CW = FF_CHUNK
    nt = T // tm

    def body(dz_ref, wdn_ref, gate_ref, ge_ref, vd_ref, fcw_ref, dup_ref, dfc_ref, after):
        i = pl.program_id(1)

        @pl.when(i == 0)
        def _():
            after[...] = jnp.zeros((8, CW), F32)
            dfc_ref[...] = jnp.zeros_like(dfc_ref)

        gate = gate_ref[...].astype(F32)
        dact = _mm_nt(dz_ref[...], wdn_ref[...])
        dgc = dact * vd_ref[...].astype(F32)
        edge = after[...]
        dgc1 = _shift_rows(dgc, -1, edge)
        dgc2 = _shift_rows(dgc, -2, edge)
        after[...] = dgc[0:8, :]
        dup_ref[0] = (fcw_ref[2:3, :] * dgc + fcw_ref[1:2, :] * dgc1 + fcw_ref[0:1, :] * dgc2).astype(BF16)
        dup_ref[1] = (dact * ge_ref[...].astype(F32)).astype(BF16)
        dfc_ref[0:1, :] += _colsum(dgc2 * gate)
        dfc_ref[1:2, :] += _colsum(dgc1 * gate)
        dfc_ref[2:3, :] += _colsum(dgc * gate)
        dfc_ref[3:4, :] += _colsum(dgc)

    rev = lambda c, i: (c, nt - 1 - i, 0)
    chunk = pl.BlockSpec((None, tm, CW), rev)
    return pl.pallas_call(
        body, name="ffn_bwd", grid=(NC, nt),
        in_specs=[pl.BlockSpec((tm, D), lambda c, i: (nt - 1 - i, 0)), pl.BlockSpec((CW, D), lambda c, i: (c, 0)),
                  chunk, chunk, chunk, pl.BlockSpec((None, 3, CW), lambda c, i: (c, 0, 0))],
        out_specs=[pl.BlockSpec((None, 2, tm, CW), lambda c, i: (c, 0, nt - 1 - i, 0)),
                   pl.BlockSpec((None, 8, CW), lambda c, i: (c, 0, 0))],
        out_shape=[jax.ShapeDtypeStruct((NC, 2, T, CW), BF16), jax.ShapeDtypeStruct((NC, 8, CW), F32)],
        scratch_shapes=[pltpu.VMEM((8, CW), F32)],
        compiler_params=_params(),
    )(dz2b, w_down, gate, ge, vd, fcw)


def _ffn_dh1(dup, dz2, dpre, z1, w_up_t, w_g, g1, b1):
    T = z1.shape[0]
    tm = 512

    def body(dup_ref, dz2_ref, dpre_ref, z_ref, wup_hbm, wg_hbm, g1_ref, b1_ref, dz1_ref, vec_ref, wup, wg):
        @pl.when(pl.program_id(0) == 0)
        def _():
            pltpu.sync_copy(wup_hbm, wup)
            pltpu.sync_copy(wg_hbm, wg)
            vec_ref[...] = jnp.zeros_like(vec_ref)

        g1v = g1_ref[...]
        _, xh1, rstd1 = _ln(z_ref[...], g1v, b1_ref[...])
        dh1 = ALPHA * dz2_ref[...] + _mm_nt(dpre_ref[...], wg[...])
        for c in range(NC):
            for s in range(2):
                r0 = s * D_FF + c * FF_CHUNK
                dh1 = dh1 + _mm(dup_ref[c, s], wup[r0:r0 + FF_CHUNK, :])
        dz1_ref[...] = _ln_bwd(dh1, xh1, rstd1, g1v)
        vec_ref[0:1, :] += _colsum(dh1 * xh1)
        vec_ref[1:2, :] += _colsum(dh1)

    anyspec = pl.BlockSpec(memory_space=pl.ANY)
    vec = _full((1, D))
    return pl.pallas_call(
        body, name="ffn_dh1", grid=(T // tm,),
        in_specs=[pl.BlockSpec((NC, 2, tm, FF_CHUNK), lambda i: (0, 0, i, 0)), _rows(tm, D), _rows(tm, D), _rows(tm, D),
                  anyspec, anyspec, vec, vec],
        out_specs=[_rows(tm, D), _full((8, D))],
        out_shape=[jax.ShapeDtypeStruct((T, D), F32), jax.ShapeDtypeStruct((8, D), F32)],
        scratch_shapes=[pltpu.VMEM((2 * D_FF, D), MXU_DTYPE), pltpu.VMEM((D, D), MXU_DTYPE)],
        compiler_params=_params(),
    )(dup, dz2, dpre, z1, w_up_t, w_g, g1, b1)


def _out_proj_bwd(dz1, w_out, exchange=None):
    T = dz1.shape[0]
    tm = min(1024, T)

    def body(dz_ref, w_ref, datt_ref, drec_ref):
        dzb = dz_ref[...].astype(MXU_DTYPE)
        datt = _mm_nt(dzb, w_ref[0:512, :])
        for h in range(HEADS):
            datt_ref[h] = datt[:, h * 64:(h + 1) * 64].astype(BF16)
        drec_ref[...] = _mm_nt(dzb, w_ref[512:1024, :])

    return _launch(body, "out_proj_bwd", (T // tm,), [_rows(tm, D), _full((D, D))], [_heads(tm), _rows(tm, 512)],
                   [jax.ShapeDtypeStruct((HEADS, T, 64), BF16), jax.ShapeDtypeStruct((T, 512), F32)], [],
                   (dz1, w_out), exchange)


def _in_proj_bwd(dq, dkv, dxr, dgr, dz1, w_in_t, exchange=None):
    T = dz1.shape[0]
    tm = 512
    W = D_IN // 4

    def body(dq_ref, dkv_ref, dxr_ref, dgr_ref, dz_ref, w_ref, dx_ref, du_ref):
        dkv = dkv_ref[...]
        dx_ref[...] = (ALPHA * dz_ref[...] + _mm(dq_ref[...], w_ref[0:512, :]) + _mm(dkv, w_ref[512:768, :])
                       + _mm(dxr_ref[...], w_ref[768:1280, :]) + _mm(dgr_ref[...], w_ref[1280:1792, :]))
        dq, dxr, dgr = dq_ref[...].astype(F32), dxr_ref[...].astype(F32), dgr_ref[...].astype(F32)
        du_ref[0] = dq[:, 0:W].astype(BF16)
        du_ref[1, :, 0:64] = dq[:, W:512].astype(BF16)
        du_ref[1, :, 64:320] = dkv.astype(BF16)
        du_ref[1, :, 320:W] = dxr[:, 0:128].astype(BF16)
        du_ref[2, :, 0:384] = dxr[:, 128:512].astype(BF16)
        du_ref[2, :, 384:W] = dgr[:, 0:64].astype(BF16)
        du_ref[3] = dgr[:, 64:512].astype(BF16)

    return _launch(body, "in_proj_bwd", (T // tm,),
                   [_rows(tm, 512), _rows(tm, 256), _rows(tm, 512), _rows(tm, 512), _rows(tm, D), _full((D_IN, D))],
                   [_rows(tm, D), pl.BlockSpec((4, tm, W), lambda i: (0, i, 0))],
                   [jax.ShapeDtypeStruct((T, D), F32), jax.ShapeDtypeStruct((4, T, W), BF16)], [],
                   (dq, dkv, dxr, dgr, dz1, w_in_t), exchange)


def _accumulate_tn(a_ref, b_ref, o_ref):
    @pl.when(pl.program_id(1) == 0)
    def _():
        o_ref[...] = jnp.zeros_like(o_ref)

    o_ref[...] += _mm_tn(a_ref[...], b_ref[...])


def _weight_grad_cols(a, b, name, n_blocks, b_spec, out_shape, out_spec, exchange=None):
    T, M = a.shape
    bt = min(DW_TOKENS, T)
    return _launch(functools.partial(_accumulate_tn), name, (n_blocks, T // bt),
                   [pl.BlockSpec((bt, M), lambda m, k: (k, 0)), b_spec(bt)], [out_spec],
                   [jax.ShapeDtypeStruct(out_shape, F32)], [], (a, b), exchange)


def _dw_out(att, rec, dz1):
    T = dz1.shape[0]
    bt = min(DW_TOKENS // 2, T)

    def body(att_ref, rec_ref, dz_ref, o_ref):
        @pl.when(pl.program_id(0) == 0)
        def _():
            o_ref[...] = jnp.zeros_like(o_ref)

        dz = dz_ref[...].astype(MXU_DTYPE)
        o_ref[0:512, :] += _mm_tn(att_ref[...], dz)
        o_ref[512:1024, :] += _mm_tn(rec_ref[...], dz)

    return pl.pallas_call(
        body, name="dw_out", grid=(T // bt,), in_specs=[_rows(bt, 512), _rows(bt, 512), _rows(bt, D)],
        out_specs=_full((D, D)), out_shape=jax.ShapeDtypeStruct((D, D), F32), compiler_params=_params())(att, rec, dz1)


def _weight_grad(a, b, bm, name, exchange=None):
    bt = min(DW_TOKENS // 2 if b.dtype == F32 else DW_TOKENS, b.shape[0])
    if a.ndim == 3:
        assert a.shape[2] == bm
        T, M = a.shape[1], a.shape[0] * bm
        a_spec = pl.BlockSpec((None, bt, bm), lambda m, k: (m, k, 0))
    else:
        T, M = a.shape
        a_spec = pl.BlockSpec((bt, bm), lambda m, k: (k, m))
    N = b.shape[1]
    nk = T // bt

    out = _launch(functools.partial(_accumulate_tn), name, (M // bm, nk),
                  [a_spec, pl.BlockSpec((bt, N), lambda m, k: (k, 0))], [pl.BlockSpec((bm, N), lambda m, k: (m, 0))],
                  [jax.ShapeDtypeStruct((M, N), F32)], [], (a, b), exchange)
    return out[0] if exchange is None else out


def _adamw(w, g, m, v, name):
    R, C = w.shape
    tr = R // 8 if R % 64 == 0 else R
    c1 = 1.0 / (1.0 - ADAM_B1 ** ADAM_STEP)
    c2 = 1.0 / (1.0 - ADAM_B2 ** ADAM_STEP)

    def body(w_ref, g_ref, m_ref, v_ref, d_ref, nm_ref, nv_ref):
        g = g_ref[...]
        nm = ADAM_B1 * m_ref[...] + (1.0 - ADAM_B1) * g
        nv = ADAM_B2 * v_ref[...] + (1.0 - ADAM_B2) * g * g
        nm_ref[...] = nm
        nv_ref[...] = nv
        d_ref[...] = -ADAM_LR * ((nm * c1) / (jnp.sqrt(nv * c2) + ADAM_EPS) + ADAM_WD * w_ref[...])

    spec = pl.BlockSpec((tr, C), lambda i: (i, 0))
    return pl.pallas_call(
        body, name=name, grid=(R // tr,),
        in_specs=[spec] * 4, out_specs=[spec] * 3,
        out_shape=[jax.ShapeDtypeStruct((R, C), F32)] * 3,
        compiler_params=_params(),
    )(w, g, m, v)


def _adamw_halves(ws, mines, sibs, ms, vs, c, name, exchange=None):
    n, nb = len(ws), 4
    c1 = 1.0 / (1.0 - ADAM_B1 ** ADAM_STEP)
    c2 = 1.0 / (1.0 - ADAM_B2 ** ADAM_STEP)

    def body(c_ref, *refs):
        own = (pl.program_id(0) // nb) == c_ref[0]
        for i in range(n):
            w_ref, a_ref, b_ref, m_ref, v_ref = refs[5 * i:5 * i + 5]
            g_ref, d_ref, nm_ref, nv_ref = refs[5 * n + 4 * i:5 * n + 4 * i + 4]
            g = jnp.where(own, a_ref[...], b_ref[...])
            nm = ADAM_B1 * m_ref[...] + (1.0 - ADAM_B1) * g
            nv = ADAM_B2 * v_ref[...] + (1.0 - ADAM_B2) * g * g
            g_ref[...] = g
            nm_ref[...] = nm
            nv_ref[...] = nv
            d_ref[...] = -ADAM_LR * ((nm * c1) / (jnp.sqrt(nv * c2) + ADAM_EPS) + ADAM_WD * w_ref[...])

    in_specs, out_specs, out_shape, args = [], [], [], []
    for w, a, b, m, v in zip(ws, mines, sibs, ms, vs):
        R, C = w.shape
        tr = R // (2 * nb)
        assert tr % 8 == 0 and a.shape == (R // 2, C)
        full = pl.BlockSpec((tr, C), lambda i, c_ref: (i, 0))
        mine_spec = pl.BlockSpec((tr, C), lambda i, c_ref: (jnp.where(i // nb == c_ref[0], i % nb, nb - 1), 0))
        sib_spec = pl.BlockSpec((tr, C), lambda i, c_ref: (jnp.where(i // nb == c_ref[0], nb - 1, i % nb), 0))
        in_specs += [full, mine_spec, sib_spec, full, full]
        out_specs += [full] * 4
        out_shape += [jax.ShapeDtypeStruct((R, C), F32)] * 4
        args += [w, a, b, m, v]
    out = _launch(body, name, (2 * nb,), in_specs, out_specs, out_shape, [], (c, *args), exchange, prefetch=1)
    return [tuple(out[4 * i:4 * i + 4]) for i in range(n)], list(out[4 * n:])


def _add4(fs, name):
    n = len(fs)

    def body(*refs):
        for a_ref, o_ref in zip(refs[:n], refs[n:]):
            o_ref[...] = ((a_ref[0].astype(F32) + a_ref[1].astype(F32)) + a_ref[2].astype(F32)) + a_ref[3].astype(F32)

    for f in fs:
        assert (f.shape[1] // 2) % 16 == 0
    return pl.pallas_call(
        body, name=name, grid=(2,),
        in_specs=[pl.BlockSpec((4, f.shape[1] // 2, f.shape[2]), lambda i: (0, i, 0)) for f in fs],
        out_specs=[pl.BlockSpec((f.shape[1] // 2, f.shape[2]), lambda i: (i, 0)) for f in fs],
        out_shape=[jax.ShapeDtypeStruct(f.shape[1:], F32) for f in fs], compiler_params=_params())(*fs)


def _gather_first(wsrc, cpack):
    def body(w_ref, c_ref, gw_ref, gc_ref, send_sems, recv_sems, local_sem, csend, crecv, clocal):
        x, y, c = _pos()
        me = 2 * x + y
        chips = _other_chips(x, y)
        start, forward, finish = _gather_steps(w_ref, gw_ref, send_sems, recv_sems, local_sem)
        start()
        loc = pltpu.make_async_copy(c_ref, gc_ref.at[me], clocal)
        loc.start()

        def conv_copy(k, slot):
            px, py = chips[k]
            return pltpu.make_async_remote_copy(src_ref=c_ref, dst_ref=gc_ref.at[slot], send_sem=csend.at[k],
                                                recv_sem=crecv.at[k], device_id=(px, py, c), device_id_type=MESH)

        for k in range(3):
            conv_copy(k, me).start()
        forward()
        finish()
        for k, (px, py) in enumerate(chips):
            conv_copy(k, 2 * px + py).wait_recv()
        for k in range(3):
            conv_copy(k, me).wait_send()
        loc.wait()

    anyspec = pl.BlockSpec(memory_space=pl.ANY)
    return pl.pallas_call(
        body, name="gather_first",
        in_specs=[anyspec, anyspec], out_specs=[anyspec, anyspec],
        out_shape=[jax.ShapeDtypeStruct((4,) + wsrc.shape, wsrc.dtype), jax.ShapeDtypeStruct((4,) + cpack.shape, cpack.dtype)],
        scratch_shapes=GATHER_SCRATCH + [pltpu.SemaphoreType.DMA((3,)), pltpu.SemaphoreType.DMA((3,)), pltpu.SemaphoreType.DMA],
        compiler_params=_params(has_side_effects=True),
    )(wsrc, cpack)


def _all_devices_exchange(s):
    def make(ins, outs, sems):
        s_ref, o_ref = ins[0], outs[0]
        send_sems, recv_sems, local_sem = sems
        x, y, c = _pos()
        me = 4 * x + 2 * y + c
        loc = pltpu.make_async_copy(s_ref, o_ref.at[me], local_sem)

        def copy(k, slot):
            peer = (x ^ (k >> 2), y ^ ((k >> 1) & 1), c ^ (k & 1))
            return pltpu.make_async_remote_copy(src_ref=s_ref, dst_ref=o_ref.at[slot], send_sem=send_sems.at[k - 1],
                                                recv_sem=recv_sems.at[k - 1], device_id=peer, device_id_type=MESH)

        def start():
            loc.start()
            for k in range(1, 8):
                copy(k, me).start()

        def finish():
            for k in range(1, 8):
                copy(k, 4 * (x ^ (k >> 2)) + 2 * (y ^ ((k >> 1) & 1)) + (c ^ (k & 1))).wait_recv()
            for k in range(1, 8):
                copy(k, me).wait_send()
            loc.wait()

        return start, lambda: None, finish

    return _Exchange([s], [jax.ShapeDtypeStruct((8,) + s.shape, s.dtype)],
                     [pltpu.SemaphoreType.DMA((7,)), pltpu.SemaphoreType.DMA((7,)), pltpu.SemaphoreType.DMA], make)


def _sum_devices(a):
    def body(a_ref, o_ref):
        acc = a_ref[0]
        for d in range(1, 8):
            acc = acc + a_ref[d]
        o_ref[...] = acc

    vm = pl.BlockSpec(memory_space=pltpu.VMEM)
    return pl.pallas_call(body, name="sum_devices", in_specs=[vm], out_specs=vm,
                          out_shape=jax.ShapeDtypeStruct(a.shape[1:], F32), compiler_params=_params())(a)


def _swap_exchange(gs):
    n = len(gs)

    def make(ins, outs, sems):
        x, y, c = _pos()
        cps = []
        for i in range(n):
            half = gs[i].shape[1] // 2
            rows = pl.ds(pl.multiple_of((1 - c) * half, 8), half)
            cps.append(pltpu.make_async_remote_copy(src_ref=ins[i].at[:, rows, :], dst_ref=outs[i], send_sem=sems[0].at[i],
                                                    recv_sem=sems[1].at[i], device_id=(x, y, 1 - c), device_id_type=MESH))

        def start():
            for cp in cps:
                cp.start()

        def finish():
            for cp in cps:
                cp.wait()

        return start, lambda: None, finish

    return _Exchange(gs, [jax.ShapeDtypeStruct((4, g.shape[1] // 2, g.shape[2]), g.dtype) for g in gs],
                     [pltpu.SemaphoreType.DMA((n,)), pltpu.SemaphoreType.DMA((n,))], make)


def _scatter_exchange(ss):
    n = len(ss)

    def make(ins, outs, sems):
        send_sems, recv_sems, local_sems = sems
        x, y, c = _pos()
        me = 2 * x + y
        chips = _other_chips(x, y)
        locs = [pltpu.make_async_copy(ins[i].at[me], outs[i].at[me], local_sems.at[i]) for i in range(n)]

        def copy(i, k, src_slot, dst_slot):
            px, py = chips[k]
            return pltpu.make_async_remote_copy(src_ref=ins[i].at[src_slot], dst_ref=outs[i].at[dst_slot],
                                                send_sem=send_sems.at[3 * i + k], recv_sem=recv_sems.at[3 * i + k],
                                                device_id=(px, py, c), device_id_type=MESH)

        def start():
            for i in range(n):
                locs[i].start()
                for k, (px, py) in enumerate(chips):
                    copy(i, k, 2 * px + py, me).start()

        def finish():
            for i in range(n):
                for k, (px, py) in enumerate(chips):
                    copy(i, k, me, 2 * px + py).wait_recv()
            for i in range(n):
                for k, (px, py) in enumerate(chips):
                    copy(i, k, 2 * px + py, me).wait_send()
                locs[i].wait()

        return start, lambda: None, finish

    return _Exchange(ss, [jax.ShapeDtypeStruct(s.shape, s.dtype) for s in ss],
                     [pltpu.SemaphoreType.DMA((3 * n,)), pltpu.SemaphoreType.DMA((3 * n,)), pltpu.SemaphoreType.DMA((n,))], make)


def _send_exchange(rs):
    n = len(rs)

    def make(ins, outs, sems):
        x, y, c = _pos()
        cps = [pltpu.make_async_remote_copy(src_ref=ins[i], dst_ref=outs[i], send_sem=sems[0].at[i], recv_sem=sems[1].at[i],
                                            device_id=(x, y, 1 - c), device_id_type=MESH) for i in range(n)]

        def start():
            for cp in cps:
                cp.start()

        def finish():
            for cp in cps:
                cp.wait()

        return start, lambda: None, finish

    return _Exchange(rs, [jax.ShapeDtypeStruct(r.shape, r.dtype) for r in rs],
                     [pltpu.SemaphoreType.DMA((n,)), pltpu.SemaphoreType.DMA((n,))], make)


def _reduce_in_vmem(g):
    _, R, C = g.shape
    H = R // 2

    def body(g_ref, mine_ref, other_ref, sib, part, got, swap_sems, send_sems, recv_sems, last_sems):
        x, y, c = _pos()
        me = 2 * x + y
        chips = _other_chips(x, y)
        sibling = (x, y, 1 - c)
        mine = pl.ds(pl.multiple_of(c * H, 8), H)
        theirs = pl.ds(pl.multiple_of((1 - c) * H, 8), H)
        swap = pltpu.make_async_remote_copy(src_ref=g_ref.at[:, theirs, :], dst_ref=sib, send_sem=swap_sems.at[0],
                                            recv_sem=swap_sems.at[1], device_id=sibling, device_id_type=MESH)
        swap.start()
        swap.wait()
        part[...] = (g_ref[:, mine, :] + sib[...]).astype(BF16)

        def copy(k, src_slot, dst_slot):
            px, py = chips[k]
            return pltpu.make_async_remote_copy(src_ref=part.at[src_slot], dst_ref=got.at[dst_slot], send_sem=send_sems.at[k],
                                                recv_sem=recv_sems.at[k], device_id=(px, py, c), device_id_type=MESH)

        for k, (px, py) in enumerate(chips):
            copy(k, 2 * px + py, me).start()
        got[me] = part[me]
        for k, (px, py) in enumerate(chips):
            copy(k, me, 2 * px + py).wait_recv()
        for k, (px, py) in enumerate(chips):
            copy(k, 2 * px + py, me).wait_send()
        mine_ref[...] = ((got[0].astype(F32) + got[1].astype(F32)) + got[2].astype(F32)) + got[3].astype(F32)
        last = pltpu.make_async_remote_copy(src_ref=mine_ref, dst_ref=other_ref, send_sem=last_sems.at[0],
                                            recv_sem=last_sems.at[1], device_id=sibling, device_id_type=MESH)
        last.start()
        last.wait()

    vm = pl.BlockSpec(memory_space=pltpu.VMEM)
    half = jax.ShapeDtypeStruct((H, C), F32)
    return pl.pallas_call(
        body, name="reduce_late", in_specs=[vm], out_specs=[vm, vm], out_shape=[half, half],
        scratch_shapes=[pltpu.VMEM((4, H, C), F32), pltpu.VMEM((4, H, C), BF16), pltpu.VMEM((4, H, C), BF16),
                        pltpu.SemaphoreType.DMA((2,)), pltpu.SemaphoreType.DMA((3,)), pltpu.SemaphoreType.DMA((3,)),
                        pltpu.SemaphoreType.DMA((2,))],
        compiler_params=_params(has_side_effects=True))(g)


def _add_half(gs, rs, c, name):
    n = len(gs)

    def body(c_ref, *refs):
        for g_ref, r_ref, o_ref in zip(refs[:n], refs[n:2 * n], refs[2 * n:]):
            o_ref[...] = (g_ref[...] + r_ref[...]).astype(BF16)

    g_specs, r_specs, out_shape = [], [], []
    for g, r in zip(gs, rs):
        _, H, C = r.shape
        tr = H // 2
        assert tr % 16 == 0 and g.shape == (4, 2 * H, C)
        g_specs.append(pl.BlockSpec((1, tr, C), lambda j, i, c_ref: (j, c_ref[0] * 2 + i, 0)))
        r_specs.append(pl.BlockSpec((1, tr, C), lambda j, i, c_ref: (j, i, 0)))
        out_shape.append(jax.ShapeDtypeStruct((4, H, C), BF16))
    grid_spec = pltpu.PrefetchScalarGridSpec(num_scalar_prefetch=1, grid=(4, 2), in_specs=g_specs + r_specs, out_specs=r_specs)
    return pl.pallas_call(body, name=name, grid_spec=grid_spec, out_shape=out_shape, compiler_params=_params())(c, *gs, *rs)


def _block_diag(w):
    eye = jnp.eye(RNN_BLOCKS, dtype=w.dtype)
    return (eye[:, None, :, None] * w[:, :, None, :]).reshape(D_RNN, D_RNN)


def _diag_blocks(wd):
    d = wd.reshape(RNN_BLOCKS, 64, RNN_BLOCKS, 64)
    return jnp.stack([d[h, :, h, :] for h in range(RNN_BLOCKS)])


def _split_pack(a, first, last):
    out, base = {}, PACK_OFF[first]
    for i in range(first, last):
        s = a[:, PACK_OFF[i] - base:PACK_OFF[i + 1] - base]
        out[BIG_KEYS[i]] = s.reshape(4 * 256, 256) if BIG_KEYS[i] == "w_p_t" else s.reshape(-1, 1024)
    return out


def _layer_grads(x, p, tgt, gw, small, shard=None, core=None):
    row = lambda v: v.reshape(1, -1)
    wa = _block_diag(small["gate_a_w"]).astype(MXU_DTYPE)
    wx = _block_diag(small["gate_x_w"]).astype(MXU_DTYPE)
    sinks = small["attn_sinks"].reshape(1, HEADS)

    dist = shard is not None
    q, kv, xr, gr, xb = _in_proj(x, gw["w_in_t"])
    cut = PACK_OFF[1] + PACK_ROWS[1] // 2
    att, *ga = _attn_fwd(q, kv, sinks, _gather_exchange(shard[PACK_OFF[1]:cut]) if dist else None)
    xc, h, rec, *gb = _rnn_fwd(xr, gr, small["rnn_conv_w"], row(small["rnn_conv_b"]), wa, row(small["gate_a_b"]),
                               wx, row(small["gate_x_b"]), row(small["lru_lambda"]),
                               _gather_exchange(shard[cut:PACK_OFF[3]]) if dist else None)
    if dist:
        gw = {**gw, **_split_pack(jnp.concatenate([ga[0], gb[0]], axis=1), 1, 3)}
    g1, b1 = row(small["ln1_g"]), row(small["ln1_b"])
    fcw = small["ffn_conv_w"].reshape(3, NC, FF_CHUNK).transpose(1, 0, 2)
    fcb = small["ffn_conv_b"].reshape(NC, 1, FF_CHUNK)
    z1, h1b = _out_proj(att, rec, x, gw["w_out"], g1, b1)
    gate, ge, vd, act, *gc = _ffn_up(h1b, gw["w_up_t"], fcw, fcb,
                                     _gather_exchange(shard[PACK_OFF[3]:PACK_OFF[6]]) if dist else None)
    if dist:
        gw = {**gw, **_split_pack(gc[0], 3, 6)}
    dz2, dz2b, dpre, dpp, vec2 = _ffn_down(act, z1, p, tgt, gw["w_down"], gw["w_g"], gw["w_p_t"], g1, b1,
                                           row(small["ln2_g"]), row(small["ln2_b"]), row(small["ple_gate_b"]))
    dup, dfc = _ffn_bwd(dz2b, gate, ge, vd, gw["w_down"], fcw)
    dz1, vec1 = _ffn_dh1(dup, dz2, dpre, z1, gw["w_up_t"], gw["w_g"], g1, b1)
    per_chip = 2 * D_FF // 4 // FF_CHUNK
    big = {"w_ffn_up": _weight_grad_cols(
        h1b, dup.reshape(2 * NC, -1, FF_CHUNK), "dw_up", 2 * NC,
        lambda bt: pl.BlockSpec((None, bt, FF_CHUNK), lambda m, k: (m, k, 0)), (4, D, 2 * D_FF // 4),
        pl.BlockSpec((None, D, FF_CHUNK), lambda m, k: (2 * (m % 2) + (m // 2) // per_chip, 0, (m // 2) % per_chip)))[0]}
    g_dn, *got_up = _weight_grad(act, dz2b, 512, "dw_down", _swap_exchange([big["w_ffn_up"]])) if dist else (
        _weight_grad(act, dz2b, 512, "dw_down"),)
    big["w_ffn_down"] = g_dn.reshape(4, D_FF // 4, D)
    big["ple_gate_w"] = _weight_grad(h1b, dpre, 512, "dw_gate").reshape(4, D // 4, D)
    big["ple_proj"] = _weight_grad(p, dpp, PLE, "dw_proj").reshape(PLE, 4, D // 4).transpose(1, 0, 2)
    big["w_out"] = _dw_out(att, rec, dz1).reshape(4, D // 4, D)
    reduced = None
    if dist:
        g_ffn = [big[k] for k in EARLY_WEIGHTS]
        ex = _swap_exchange(g_ffn[1:])
    datt, drec, *got = _out_proj_bwd(dz1, gw["w_out"], ex if dist else None)
    if dist:
        sums = _add_half(g_ffn, got_up + got, core, "add_half_ffn")
        ex, ex2 = _scatter_exchange(sums[:1]), _scatter_exchange(sums[1:])
    dxr, dgr, dwa, dwx, dvec, *got = _rnn_bwd(drec, gr, h, xc, xr, small["rnn_conv_w"], wa, row(small["gate_a_b"]),
                                              wx, row(small["gate_x_b"]), row(small["lru_lambda"]), ex if dist else None)
    dq, dkv, dsinks, *got2 = _attn_bwd(q, kv, datt, sinks, ex2 if dist else None)
    if dist:
        mine = _add4(got + got2, "add_chips_ffn")
        big = {}
    sg = {
        "attn_sinks": dsinks[:, 0],
        "rnn_conv_w": dvec[4:8],
        "rnn_conv_b": dvec[3],
        "gate_a_w": _diag_blocks(dwa),
        "gate_a_b": dvec[0],
        "gate_x_w": _diag_blocks(dwx),
        "gate_x_b": dvec[1],
        "lru_lambda": dvec[2],
        "ln1_g": vec1[0],
        "ln1_b": vec1[1],
        "ffn_conv_w": dfc[:, 0:3].transpose(1, 0, 2).reshape(3, D_FF),
        "ffn_conv_b": dfc[:, 3].reshape(D_FF),
        "ple_gate_b": vec2[3],
        "ln2_g": vec2[1],
        "ln2_b": vec2[2],
    }
    loss = vec2[0, 0:1]
    grad_x, du = _in_proj_bwd(dq, dkv, dxr, dgr, dz1, gw["w_in_t"])
    ex = None
    if dist:
        ex = _join_exchanges(_send_exchange(mine), _all_devices_exchange(_pack_vecs([sg[k] for k in SMALL] + [loss])[0]))
    big["w_in"], *got = _weight_grad_cols(
        xb, du, "dw_in", 4, lambda bt: pl.BlockSpec((None, bt, D_IN // 4), lambda j, k: (j, k, 0)), (4, D, D_IN // 4),
        pl.BlockSpec((None, D, D_IN // 4), lambda j, k: (j, 0, 0)), ex)
    if dist:
        reduced = (mine, got[:len(mine)])
    return grad_x, big, sg, loss, reduced, got[-1:]


BIG = ("w_in", "w_ffn_up", "w_out", "w_ffn_down", "ple_gate_w", "ple_proj")
BIG_KEYS = ("w_in_t", "w_up_t", "w_out", "w_down", "w_g", "w_p_t")
BIG_T = (True, True, False, False, False, True)
EARLY_WEIGHTS = ("w_ffn_up", "w_ffn_down", "ple_gate_w", "ple_proj", "w_out")
LATE_WEIGHTS = ("w_in",)
SMALL = ("attn_sinks", "rnn_conv_w", "rnn_conv_b", "gate_a_w", "gate_a_b", "gate_x_w", "gate_x_b", "lru_lambda",
         "ln1_g", "ln1_b", "ffn_conv_w", "ffn_conv_b", "ple_gate_b", "ln2_g", "ln2_b")
SHARDED_SMALL = ("rnn_conv_w", "ffn_conv_w")
WEIGHTS = ("w_in", "attn_sinks", "rnn_conv_w", "rnn_conv_b", "gate_a_w", "gate_a_b", "gate_x_w", "gate_x_b",
           "lru_lambda", "w_out", "ln1_g", "ln1_b", "w_ffn_up", "ffn_conv_w", "ffn_conv_b", "w_ffn_down",
           "ple_gate_w", "ple_gate_b", "ple_proj", "ln2_g", "ln2_b")


def _pack_big(d, first=0, last=6):
    parts = []
    for name, t in zip(BIG[first:last], BIG_T[first:last]):
        a = d[name]
        a = a.T if t else a
        parts.append(a.reshape(-1, 1024))
    return jnp.concatenate(parts, axis=0)


def _pack_vecs(items):
    parts, offs, n = [], [], 0
    for a in items:
        f = a.reshape(-1).astype(F32)
        pad = (-f.shape[0]) % 128
        parts.append(jnp.pad(f, (0, pad)))
        offs.append(n)
        n += (f.shape[0] + pad) // 128
    padr = (-n) % 8
    if padr:
        parts.append(jnp.zeros((padr * 128,), F32))
    return jnp.concatenate(parts).reshape(-1, 128), offs


def _unpack_vecs(a, offs, shapes):
    flat = a.reshape(-1)
    out = []
    for o, s in zip(offs, shapes):
        n = 1
        for d in s:
            n *= d
        out.append(flat[o * 128:o * 128 + n].reshape(s))
    return out


def kernel(x, p, w_in, attn_sinks, rnn_conv_w, rnn_conv_b, gate_a_w, gate_a_b, gate_x_w, gate_x_b, lru_lambda, w_out, ln1_g, ln1_b, w_ffn_up, ffn_conv_w, ffn_conv_b, w_ffn_down, ple_gate_w, ple_gate_b, ple_proj, ln2_g, ln2_b, loss_target, m_w_in, m_attn_sinks, m_rnn_conv_w, m_rnn_conv_b, m_gate_a_w, m_gate_a_b, m_gate_x_w, m_gate_x_b, m_lru_lambda, m_w_out, m_ln1_g, m_ln1_b, m_w_ffn_up, m_ffn_conv_w, m_ffn_conv_b, m_w_ffn_down, m_ple_gate_w, m_ple_gate_b, m_ple_proj, m_ln2_g, m_ln2_b, v_w_in, v_attn_sinks, v_rnn_conv_w, v_rnn_conv_b, v_gate_a_w, v_gate_a_b, v_gate_x_w, v_gate_x_b, v_lru_lambda, v_w_out, v_ln1_g, v_ln1_b, v_w_ffn_up, v_ffn_conv_w, v_ffn_conv_b, v_w_ffn_down, v_ple_gate_w, v_ple_gate_b, v_ple_proj, v_ln2_g, v_ln2_b):
    w = dict(w_in=w_in, attn_sinks=attn_sinks, rnn_conv_w=rnn_conv_w, rnn_conv_b=rnn_conv_b, gate_a_w=gate_a_w,
             gate_a_b=gate_a_b, gate_x_w=gate_x_w, gate_x_b=gate_x_b, lru_lambda=lru_lambda, w_out=w_out, ln1_g=ln1_g,
             ln1_b=ln1_b, w_ffn_up=w_ffn_up, ffn_conv_w=ffn_conv_w, ffn_conv_b=ffn_conv_b, w_ffn_down=w_ffn_down,
             ple_gate_w=ple_gate_w, ple_gate_b=ple_gate_b, ple_proj=ple_proj, ln2_g=ln2_g, ln2_b=ln2_b)
    m = dict(w_in=m_w_in, attn_sinks=m_attn_sinks, rnn_conv_w=m_rnn_conv_w, rnn_conv_b=m_rnn_conv_b, gate_a_w=m_gate_a_w,
             gate_a_b=m_gate_a_b, gate_x_w=m_gate_x_w, gate_x_b=m_gate_x_b, lru_lambda=m_lru_lambda, w_out=m_w_out,
             ln1_g=m_ln1_g, ln1_b=m_ln1_b, w_ffn_up=m_w_ffn_up, ffn_conv_w=m_ffn_conv_w, ffn_conv_b=m_ffn_conv_b,
             w_ffn_down=m_w_ffn_down, ple_gate_w=m_ple_gate_w, ple_gate_b=m_ple_gate_b, ple_proj=m_ple_proj,
             ln2_g=m_ln2_g, ln2_b=m_ln2_b)
    v = dict(w_in=v_w_in, attn_sinks=v_attn_sinks, rnn_conv_w=v_rnn_conv_w, rnn_conv_b=v_rnn_conv_b, gate_a_w=v_gate_a_w,
             gate_a_b=v_gate_a_b, gate_x_w=v_gate_x_w, gate_x_b=v_gate_x_b, lru_lambda=v_lru_lambda, w_out=v_w_out,
             ln1_g=v_ln1_g, ln1_b=v_ln1_b, w_ffn_up=v_w_ffn_up, ffn_conv_w=v_ffn_conv_w, ffn_conv_b=v_ffn_conv_b,
             w_ffn_down=v_w_ffn_down, ple_gate_w=v_ple_gate_w, ple_gate_b=v_ple_gate_b, ple_proj=v_ple_proj,
             ln2_g=v_ln2_g, ln2_b=v_ln2_b)
    w, m, v = ({k: a[0] for k, a in d.items()} for d in (w, m, v))
    chip = 2 * lax.axis_index("x") + lax.axis_index("y")
    core = lax.axis_index("c")

    wpack = _pack_big(w)
    cpack, _ = _pack_vecs([w["rnn_conv_w"], w["ffn_conv_w"]])
    shard = wpack.astype(MXU_DTYPE)
    g_in, gcp = _gather_first(shard[PACK_OFF[0]:PACK_OFF[1]], cpack)
    gw = _split_pack(g_in, 0, 1)
    small = {k: w[k] for k in SMALL}
    small["rnn_conv_w"] = gcp[:, 0:4].reshape(4, 4, 128).transpose(1, 0, 2).reshape(4, 512)
    small["ffn_conv_w"] = gcp[:, 4:22].reshape(4, 3, 768).transpose(1, 0, 2).reshape(3, 3072)

    core1 = core.reshape(1).astype(jnp.int32)
    grad_x, big, sg, loss, ffn_halves, small_all = _layer_grads(x[0], p[0, 0], loss_target[0], gw, small, shard, core1)

    shapes = [sg[k].shape for k in SMALL] + [(1,)]
    _, offs = _pack_vecs([jnp.zeros(s, F32) for s in shapes])
    red = dict(zip(SMALL + ("loss",), _unpack_vecs(_sum_devices(small_all[0]), offs, shapes)))
    red["rnn_conv_w"] = lax.dynamic_slice_in_dim(red["rnn_conv_w"], chip * 128, 128, axis=1)
    red["ffn_conv_w"] = lax.dynamic_slice_in_dim(red["ffn_conv_w"], chip * 768, 768, axis=1)

    late_mine, late_other = ([a] for a in _reduce_in_vmem(big["w_in"]))

    def adamw(names, mine, other, name):
        out, _ = _adamw_halves([w[k] for k in names], mine, other, [m[k] for k in names], [v[k] for k in names],
                               core1, name)
        return dict(zip(names, out))

    big_out = {**adamw(LATE_WEIGHTS, late_mine, late_other, "adamw_late"), **adamw(EARLY_WEIGHTS, *ffn_halves, "adamw_early")}
    wsm, offs2 = _pack_vecs([w[k] for k in SMALL])
    gsm, _ = _pack_vecs([red[k] for k in SMALL])
    msm, _ = _pack_vecs([m[k] for k in SMALL])
    vsm, _ = _pack_vecs([v[k] for k in SMALL])
    dsm, nmsm, nvsm = _adamw(wsm, gsm, msm, vsm, "adamw_small")
    shapes2 = [w[k].shape for k in SMALL]

    def named(n, smallp):
        d = {k: out[n][None] for k, out in big_out.items()}
        d.update({k: a[None] for k, a in zip(SMALL, _unpack_vecs(smallp, offs2, shapes2))})
        return [d[k] for k in WEIGHTS]

    return (red["loss"].reshape(()), grad_x[None], *named(0, gsm), *named(1, dsm), *named(2, nmsm), *named(3, nvsm))
```

```python
import functools

import jax
import jax.numpy as jnp
from jax import lax
from jax.experimental import pallas as pl
from jax.experimental.pallas import tpu as pltpu

F32 = jnp.float32
BF16 = jnp.bfloat16
MXU_DTYPE = jnp.bfloat16

D = 1024
D_ATT = 512
D_KV = 128
D_RNN = 512
D_IN = 1792
D_FF = 3072
FF_CHUNK = 512
PLE = 256
HEADS = 8
HEAD_DIM = 64
BLK = 128
ATTN_BLOCKS = 8
DW_TOKENS = 4096
RNN_BLOCKS = 8
LN_EPS = 1e-5
LRU_C = 8.0
ALPHA = float(2.0 ** 0.25)
SCALE = HEAD_DIM ** -0.5
NEG = -1e30

ADAM_LR = 0.001
ADAM_B1 = 0.9
ADAM_B2 = 0.999
ADAM_EPS = 1e-08
ADAM_WD = 0.01
ADAM_STEP = 10

VMEM_LIMIT_BYTES = 56 * 1024 * 1024
MESH = pl.DeviceIdType.MESH

PACK_ROWS = (448, 1536, 256, 768, 256, 64)
PACK_OFF = tuple(sum(PACK_ROWS[:i]) for i in range(len(PACK_ROWS) + 1))
PACK_TOTAL = PACK_OFF[-1]


def _params(**kw):
    return pltpu.CompilerParams(vmem_limit_bytes=VMEM_LIMIT_BYTES, **kw)


def _mm(a, b):
    return jnp.dot(a.astype(MXU_DTYPE), b.astype(MXU_DTYPE), preferred_element_type=F32)


def _mm_nt(a, b):
    return lax.dot_general(a.astype(MXU_DTYPE), b.astype(MXU_DTYPE), (((1,), (1,)), ((), ())),
                           preferred_element_type=F32)


def _mm_tn(a, b):
    return lax.dot_general(a.astype(MXU_DTYPE), b.astype(MXU_DTYPE), (((0,), (0,)), ((), ())),
                           preferred_element_type=F32)


def _sigmoid(x):
    return 0.5 + 0.5 * jnp.tanh(0.5 * x)


def _gelu(x):
    c = 0.7978845608028654
    k = 0.044715
    x2 = x * x
    t = jnp.tanh(x * (c + (c * k) * x2))
    h = 0.5 * (1.0 + t)
    return x * h, h * (1.0 + (x * (1.0 - t)) * (c + (3.0 * c * k) * x2))


def _shift_rows(x, s, edge8):
    R = x.shape[0]
    row8 = lax.broadcasted_iota(jnp.int32, (8, x.shape[1]), 0)
    if s > 0:
        rolled = pltpu.roll(x, s, 0)
        first = jnp.where(row8 < s, pltpu.roll(edge8, s, 0), rolled[0:8])
        return jnp.concatenate([first, rolled[8:]], axis=0)
    k = -s
    rolled = pltpu.roll(x, R - k, 0)
    last = jnp.where(row8 >= 8 - k, pltpu.roll(edge8, 8 - k, 0), rolled[R - 8:])
    return jnp.concatenate([rolled[:R - 8], last], axis=0)


def _softplus(x):
    return jnp.maximum(x, 0.0) + jnp.log(1.0 + jnp.exp(-jnp.abs(x)))


def _ln(z, g, b):
    mu = jnp.mean(z, axis=-1, keepdims=True)
    zc = z - mu
    var = jnp.mean(zc * zc, axis=-1, keepdims=True)
    rstd = lax.rsqrt(var + LN_EPS)
    xhat = zc * rstd
    return xhat * g + b, xhat, rstd


def _ln_bwd(dy, xhat, rstd, g):
    dxh = dy * g
    m1 = jnp.mean(dxh, axis=-1, keepdims=True)
    m2 = jnp.mean(dxh * xhat, axis=-1, keepdims=True)
    return rstd * (dxh - m1 - xhat * m2)


def _colsum(x):
    return jnp.sum(x, axis=0, keepdims=True)


def _full(shape):
    nd = len(shape)
    return pl.BlockSpec(shape, lambda *_: (0,) * nd)


def _rows(tm, cols, fn=None):
    if fn is None:
        return pl.BlockSpec((tm, cols), lambda i: (i, 0))
    return pl.BlockSpec((tm, cols), lambda i: (fn(i), 0))


def _heads(tm):
    return pl.BlockSpec((HEADS, tm, HEAD_DIM), lambda i: (0, i, 0))


def _in_proj(x, w_in_t):
    T = x.shape[0]
    tm = min(1024, T)

    def body(x_ref, w_ref, q_ref, kv_ref, xr_ref, gr_ref, xb_ref):
        xb = x_ref[...].astype(MXU_DTYPE)
        xb_ref[...] = xb.astype(BF16)
        q = _mm_nt(xb, w_ref[0:512, :])
        for h in range(HEADS):
            q_ref[h] = q[:, h * 64:(h + 1) * 64].astype(BF16)
        kv_ref[...] = _mm_nt(xb, w_ref[512:768, :]).astype(BF16)
        xr_ref[...] = _mm_nt(xb, w_ref[768:1280, :])
        gr_ref[...] = _mm_nt(xb, w_ref[1280:1792, :])

    return pl.pallas_call(
        body, name="in_proj", grid=(T // tm,),
        in_specs=[_rows(tm, D), _full((D_IN, D))],
        out_specs=[_heads(tm), _rows(tm, 256), _rows(tm, 512), _rows(tm, 512), _rows(tm, D)],
        out_shape=[jax.ShapeDtypeStruct((HEADS, T, 64), BF16), jax.ShapeDtypeStruct((T, 256), BF16),
                   jax.ShapeDtypeStruct((T, 512), F32), jax.ShapeDtypeStruct((T, 512), F32),
                   jax.ShapeDtypeStruct((T, D), BF16)],
        compiler_params=_params(),
    )(x, w_in_t)


def _attn_band(kv_ref, i):
    cur = pl.multiple_of(i * BLK, BLK)
    prev = pl.multiple_of(jnp.maximum(i - 1, 0) * BLK, BLK)
    band = jnp.concatenate([kv_ref[pl.ds(prev, BLK), :], kv_ref[pl.ds(cur, BLK), :]], axis=0)
    key = lax.broadcasted_iota(jnp.int32, (2 * BLK, 4 * BLK), 0)
    qry = lax.broadcasted_iota(jnp.int32, (2 * BLK, 4 * BLK), 1) & (BLK - 1)
    in_prev = jnp.logical_and(jnp.logical_and(key < BLK, key > qry), i > 0)
    mask = jnp.logical_or(in_prev, jnp.logical_and(key >= BLK, key - BLK <= qry))
    return band, mask, cur, prev


def _attn_scores(band, mask, qs, s_ref, g):
    st = jnp.where(mask, _mm_nt(band[:, g * 64:(g + 1) * 64], qs) * SCALE, NEG)
    lane = lax.broadcasted_iota(jnp.int32, (1, 4 * BLK), 1)
    sv = jnp.where(lane < BLK, s_ref[0, 4 * g],
                   jnp.where(lane < 2 * BLK, s_ref[0, 4 * g + 1], jnp.where(lane < 3 * BLK, s_ref[0, 4 * g + 2], s_ref[0, 4 * g + 3])))
    m = jnp.maximum(jnp.max(st, axis=0, keepdims=True), sv)
    p = jnp.exp(st - m)
    ps = jnp.exp(sv - m)
    return p, ps, jnp.sum(p, axis=0, keepdims=True) + ps


def _pos():
    return lax.axis_index("x"), lax.axis_index("y"), lax.axis_index("c")


def _other_chips(x, y):
    return [(1 - x, y), (x, 1 - y), (1 - x, 1 - y)]


def _gather_steps(w_ref, gw_ref, send_sems, recv_sems, local_sem):
    x, y, c = _pos()
    me = 2 * x + y
    chips = _other_chips(x, y)
    half = w_ref.shape[0] // 2
    mine = pl.ds(pl.multiple_of(c * half, 16), half)
    theirs = pl.ds(pl.multiple_of((1 - c) * half, 16), half)
    loc = pltpu.make_async_copy(w_ref, gw_ref.at[me], local_sem)

    def copy(k, src, dst, to):
        return pltpu.make_async_remote_copy(src_ref=src, dst_ref=dst, send_sem=send_sems.at[k], recv_sem=recv_sems.at[k],
                                            device_id=to, device_id_type=MESH)

    def out(k):
        px, py = chips[k]
        return copy(k, w_ref.at[mine], gw_ref.at[me, mine], (px, py, c))

    def fwd(k, rows):
        px, py = chips[k]
        return copy(3 + k, gw_ref.at[2 * px + py, rows], gw_ref.at[2 * px + py, rows], (x, y, 1 - c))

    def start():
        loc.start()
        for k in range(3):
            out(k).start()

    def forward():
        for k in range(3):
            px, py = chips[k]
            copy(k, w_ref.at[mine], gw_ref.at[2 * px + py, mine], (px, py, c)).wait_recv()
            fwd(k, mine).start()

    def finish():
        for k in range(3):
            fwd(k, theirs).wait_recv()
        for k in range(3):
            out(k).wait_send()
            fwd(k, mine).wait_send()
        loc.wait()

    return start, forward, finish


GATHER_SCRATCH = [pltpu.SemaphoreType.DMA((6,)), pltpu.SemaphoreType.DMA((6,)), pltpu.SemaphoreType.DMA]


class _Exchange:
    def __init__(self, args, out_shape, scratch, make):
        self.args, self.out_shape, self.scratch, self.make = list(args), list(out_shape), list(scratch), make


def _join_exchanges(a, b):
    na, nao, nas = len(a.args), len(a.out_shape), len(a.scratch)

    def make(ins, outs, sems):
        steps_a = a.make(ins[:na], outs[:nao], sems[:nas])
        steps_b = b.make(ins[na:], outs[nao:], sems[nas:])

        def both(f, g):
            def run():
                f()
                g()
            return run

        return tuple(both(f, g) for f, g in zip(steps_a, steps_b))

    return _Exchange(a.args + b.args, a.out_shape + b.out_shape, a.scratch + b.scratch, make)


def _gather_exchange(wsrc):
    return _Exchange([wsrc], [jax.ShapeDtypeStruct((4,) + wsrc.shape, wsrc.dtype)], GATHER_SCRATCH,
                     lambda ins, outs, sems: _gather_steps(ins[0], outs[0], *sems))


def _launch(body, name, grid, in_specs, out_specs, out_shape, scratch, args, exchange=None, prefetch=0):
    def call(fn, fn_name, ins, outs, shapes, scr, operands, effects):
        spec = pltpu.PrefetchScalarGridSpec(num_scalar_prefetch=prefetch, grid=grid, in_specs=ins, out_specs=outs,
                                            scratch_shapes=scr)
        return pl.pallas_call(fn, name=fn_name, grid_spec=spec, out_shape=shapes,
                              compiler_params=_params(has_side_effects=effects))(*operands)

    if exchange is None:
        return call(body, name, list(in_specs), list(out_specs), list(out_shape), list(scratch), args, False)
    n_in, n_out, ei, eo, ns = len(in_specs), len(out_specs), len(exchange.args), len(exchange.out_shape), len(exchange.scratch)
    nsteps = 1
    for g in grid:
        nsteps *= g

    def wrapped(*refs):
        scalars, refs = refs[:prefetch], refs[prefetch:]
        ins, xin = refs[:n_in], refs[n_in:n_in + ei]
        outs, xout = refs[n_in + ei:n_in + ei + n_out], refs[n_in + ei + n_out:n_in + ei + n_out + eo]
        rest = refs[n_in + ei + n_out + eo:]
        own, sems = rest[:len(rest) - ns], rest[len(rest) - ns:]
        start, forward, finish = exchange.make(xin, xout, sems)
        i = pl.program_id(0)
        for d in range(1, len(grid)):
            i = i * grid[d] + pl.program_id(d)
        pl.when(i == 0)(start)
        body(*scalars, *ins, *outs, *own)
        pl.when(i == max(nsteps - 3, 0))(forward)
        pl.when(i == nsteps - 1)(finish)

    anyspec = pl.BlockSpec(memory_space=pl.ANY)
    return call(wrapped, name + "_x", list(in_specs) + [anyspec] * ei, list(out_specs) + [anyspec] * eo,
                list(out_shape) + exchange.out_shape, list(scratch) + exchange.scratch, (*args, *exchange.args), True)


def _attn_fwd(q, kv, sinks, exchange=None):
    T = kv.shape[0]
    nblk = min(ATTN_BLOCKS, T // BLK)

    def body(q_ref, kv_ref, s_ref, o_ref):
        for b in range(nblk):
            rows = slice(b * BLK, (b + 1) * BLK)
            band, mask, _, _ = _attn_band(kv_ref, nblk * pl.program_id(0) + b)
            for g in range(2):
                qs = q_ref[4 * g:4 * g + 4, rows, :].reshape(4 * BLK, HEAD_DIM)
                p, _, den = _attn_scores(band, mask, qs, s_ref, g)
                ot = _mm_tn(band[:, 128:256], p) * (1.0 / den)
                for hh in range(4):
                    o = ot[:, hh * BLK:(hh + 1) * BLK].T
                    o_ref[rows, (4 * g + hh) * 64:(4 * g + hh + 1) * 64] = o[:, g * 64:(g + 1) * 64].astype(BF16)

    tq = nblk * BLK
    return _launch(body, "attn_fwd", (T // tq,), [_heads(tq), _full((T, 256)), pl.BlockSpec(memory_space=pltpu.SMEM)],
                   [_rows(tq, 512)], [jax.ShapeDtypeStruct((T, 512), BF16)], [], (q, kv, sinks), exchange)


def _attn_bwd(q, kv, do, sinks, exchange=None):
    T = kv.shape[0]
    nblk = min(ATTN_BLOCKS, T // BLK)

    def body(q_ref, kv_ref, do_ref, s_ref, dq_ref, dkv_ref, ds_ref):
        @pl.when(pl.program_id(0) == 0)
        def _():
            ds_ref[...] = jnp.zeros_like(ds_ref)

        for b in range(nblk):
            rows = slice(b * BLK, (b + 1) * BLK)
            band, mask, cur, prev = _attn_band(kv_ref, nblk * pl.program_id(0) + b)
            for g in range(2):
                qs = q_ref[4 * g:4 * g + 4, rows, :].reshape(4 * BLK, HEAD_DIM)
                dos = do_ref[4 * g:4 * g + 4, rows, :].reshape(4 * BLK, HEAD_DIM)
                p, ps, den = _attn_scores(band, mask, qs, s_ref, g)
                inv = 1.0 / den
                p = p * inv
                dpt = _mm_nt(band[:, 128 + g * 64:192 + g * 64], dos)
                delta = jnp.sum(p * dpt, axis=0, keepdims=True)
                dst = p * (dpt - delta)
                dsv = -(ps * inv) * delta
                for hh in range(4):
                    dsink = jnp.sum(dsv[:, hh * BLK:(hh + 1) * BLK], axis=1, keepdims=True)
                    ds_ref[4 * g + hh:4 * g + hh + 1, :] += jnp.broadcast_to(dsink, (1, 128))
                dqt = _mm_tn(band[:, 0:128], dst) * SCALE
                for hh in range(4):
                    dqh = dqt[:, hh * BLK:(hh + 1) * BLK].T
                    dq_ref[rows, (4 * g + hh) * 64:(4 * g + hh + 1) * 64] = dqh[:, g * 64:(g + 1) * 64].astype(BF16)
                dk = _mm(dst, qs) * SCALE
                dv = _mm(p, dos)
                dkv_ref[pl.ds(cur, BLK), g * 64:(g + 1) * 64] = dk[BLK:2 * BLK]
                dkv_ref[pl.ds(cur, BLK), 128 + g * 64:192 + g * 64] = dv[BLK:2 * BLK]
                dkv_ref[pl.ds(prev, BLK), g * 64:(g + 1) * 64] += dk[0:BLK]
                dkv_ref[pl.ds(prev, BLK), 128 + g * 64:192 + g * 64] += dv[0:BLK]

    tq = nblk * BLK
    return _launch(body, "attn_bwd", (T // tq,),
                   [_heads(tq), _full((T, 256)), _heads(tq), pl.BlockSpec(memory_space=pltpu.SMEM)],
                   [_rows(tq, 512), _full((T, 256)), _full((8, 128))],
                   [jax.ShapeDtypeStruct((T, 512), BF16), jax.ShapeDtypeStruct((T, 256), F32),
                    jax.ShapeDtypeStruct((8, 128), F32)], [], (q, kv, do, sinks), exchange)


def _rows8(tm, cols):
    return lax.broadcasted_iota(jnp.int32, (tm, cols), 0) & 7


def _lru_gates(xc, wa, ba, wx, bx, lam):
    r = _sigmoid(_mm(xc, wa) + ba)
    ii = _sigmoid(_mm(xc, wx) + bx)
    sp = _softplus(-lam)
    la = -LRU_C * r * sp
    a = jnp.exp(la)
    m = jnp.sqrt(-jnp.tanh(la) * (a * a + 1.0))
    return r, ii, sp, a, m


def _rnn_fwd(xr, gr, cw, cb, wa, ba, wx, bx, lam, exchange=None):
    T = xr.shape[0]
    tm = 512
    C = D_RNN

    def body(xr_ref, gr_ref, cw_ref, cb_ref, wa_ref, ba_ref, wx_ref, bx_ref, lam_ref,
             xc_ref, h_ref, rec_ref, ext, a_s, b_s, carry):
        i = pl.program_id(0)

        @pl.when(i == 0)
        def _():
            ext[...] = jnp.zeros((8, C), F32)
            carry[...] = jnp.zeros((8, C), F32)

        xr = xr_ref[...]
        edge = ext[...]
        xc = cb_ref[...] + cw_ref[3:4, :] * xr
        for k in range(3):
            xc = xc + cw_ref[k:k + 1, :] * _shift_rows(xr, 3 - k, edge)
        ext[...] = xr[tm - 8:tm, :]
        xc_ref[...] = xc
        _, ii, _, a, m = _lru_gates(xc, wa_ref[...], ba_ref[...], wx_ref[...], bx_ref[...], lam_ref[...])
        b = m * ii * xc
        r8 = _rows8(tm, C)
        for d in (1, 2, 4):
            ok = r8 >= d
            a_sh = jnp.where(ok, pltpu.roll(a, d, 0), 1.0)
            b_sh = jnp.where(ok, pltpu.roll(b, d, 0), 0.0)
            b = a * b_sh + b
            a = a * a_sh
        a_s[...] = a
        b_s[...] = b

        def step(g, hin):
            s = pl.multiple_of(g * 8, 8)
            hg = a_s[pl.ds(s, 8), :] * hin + b_s[pl.ds(s, 8), :]
            h_ref[pl.ds(s, 8), :] = hg
            return jnp.broadcast_to(hg[7:8, :], (8, C))

        carry[...] = lax.fori_loop(0, tm // 8, step, carry[...], unroll=4)
        ge, _ = _gelu(gr_ref[...])
        rec_ref[...] = (h_ref[...] * ge).astype(BF16)

    vec = _full((1, C))
    in_specs = [_rows(tm, C), _rows(tm, C), _full((4, C)), vec, _full((C, C)), vec, _full((C, C)), vec, vec]
    out_specs = [_rows(tm, C), _rows(tm, C), _rows(tm, C)]
    out_shape = [jax.ShapeDtypeStruct((T, C), F32), jax.ShapeDtypeStruct((T, C), F32), jax.ShapeDtypeStruct((T, C), BF16)]
    scratch = [pltpu.VMEM((8, C), F32), pltpu.VMEM((tm, C), F32), pltpu.VMEM((tm, C), F32), pltpu.VMEM((8, C), F32)]
    return _launch(body, "rnn_fwd", (T // tm,), in_specs, out_specs, out_shape, scratch,
                   (xr, gr, cw, cb, wa, ba, wx, bx, lam), exchange)


def _rnn_bwd(drec, gr, h, xc, xr, cw, wa, ba, wx, bx, lam, exchange=None):
    T = xr.shape[0]
    tm = 512
    C = D_RNN
    nt = T // tm
    t8 = tm // 8

    def body(drec_ref, gr_ref, h_ref, hp_ref, xc_ref, xr_ref, cw_ref, wa_ref, ba_ref, wx_ref, bx_ref,
             lam_ref, dxr_ref, dgr_ref, dwa_ref, dwx_ref, dvec_ref, c_s, g_s, gout, ext, anext, gcarry):
        i = pl.program_id(0)
        j = nt - 1 - i

        @pl.when(i == 0)
        def _():
            dwa_ref[...] = jnp.zeros_like(dwa_ref)
            dwx_ref[...] = jnp.zeros_like(dwx_ref)
            dvec_ref[...] = jnp.zeros_like(dvec_ref)
            anext[...] = jnp.zeros((8, C), F32)
            gcarry[...] = jnp.zeros((8, C), F32)
            ext[...] = jnp.zeros((8, C), F32)

        xc = xc_ref[...]
        lam = lam_ref[...]
        r, ii, sp, a, m = _lru_gates(xc, wa_ref[...], ba_ref[...], wx_ref[...], bx_ref[...], lam)
        ge, dge = _gelu(gr_ref[...])
        drec = drec_ref[...]
        hh = h_ref[...]
        dgr_ref[...] = (drec * hh * dge).astype(BF16)
        dh = drec * ge
        rowi = lax.broadcasted_iota(jnp.int32, (tm, C), 0)
        c = jnp.where(rowi == tm - 1, jnp.broadcast_to(anext[0:1, :], (tm, C)), pltpu.roll(a, tm - 1, 0))
        anext[...] = a[0:8, :]
        r8 = rowi & 7
        gg = dh
        for d in (1, 2, 4):
            ok = r8 < 8 - d
            c_sh = jnp.where(ok, pltpu.roll(c, tm - d, 0), 1.0)
            g_sh = jnp.where(ok, pltpu.roll(gg, tm - d, 0), 0.0)
            gg = c * g_sh + gg
            c = c * c_sh
        c_s[...] = c
        g_s[...] = gg

        def step(k, gin):
            s = pl.multiple_of((t8 - 1 - k) * 8, 8)
            og = c_s[pl.ds(s, 8), :] * gin + g_s[pl.ds(s, 8), :]
            gout[pl.ds(s, 8), :] = og
            return jnp.broadcast_to(og[0:1, :], (8, C))

        gcarry[...] = lax.fori_loop(0, t8, step, gcarry[...], unroll=4)
        G = gout[...]
        hprev_row = jnp.where(j > 0, hp_ref[7:8, :], 0.0)
        hprev = jnp.where(rowi == 0, jnp.broadcast_to(hprev_row, (tm, C)), pltpu.roll(hh, 1, 0))
        da = G * hprev
        dm = G * ii * xc
        di = G * m * xc
        dxc = G * m * ii
        dla = da * a - dm * a * a / m
        dr = dla * (-LRU_C * sp)
        dsp = _colsum(dla * (-LRU_C * r))
        dlam = dsp * (-_sigmoid(-lam))
        dpr = dr * r * (1.0 - r)
        dpi = di * ii * (1.0 - ii)
        dxc = dxc + _mm_nt(dpr, wa_ref[...]) + _mm_nt(dpi, wx_ref[...])
        dwa_ref[...] += _mm_tn(xc, dpr)
        dwx_ref[...] += _mm_tn(xc, dpi)
        dvec_ref[0:1, :] += _colsum(dpr)
        dvec_ref[1:2, :] += _colsum(dpi)
        dvec_ref[2:3, :] += dlam
        dvec_ref[3:4, :] += _colsum(dxc)
        edge = ext[...]
        xr = xr_ref[...]
        dxr = cw_ref[3:4, :] * dxc
        dvec_ref[7:8, :] += _colsum(dxc * xr)
        for k in range(3):
            up = _shift_rows(dxc, k - 3, edge)
            dxr = dxr + cw_ref[k:k + 1, :] * up
            dvec_ref[4 + k:5 + k, :] += _colsum(up * xr)
        ext[...] = dxc[0:8, :]
        dxr_ref[...] = dxr.astype(BF16)

    rev = lambda i: nt - 1 - i
    prev8 = lambda i: jnp.maximum((nt - 1 - i) * t8 - 1, 0)
    vec = _full((1, C))
    return _launch(
        body, "rnn_bwd", (nt,),
        [_rows(tm, C, rev), _rows(tm, C, rev), _rows(tm, C, rev), _rows(8, C, prev8), _rows(tm, C, rev),
         _rows(tm, C, rev), _full((4, C)), _full((C, C)), vec, _full((C, C)), vec, vec],
        [_rows(tm, C, rev), _rows(tm, C, rev), _full((C, C)), _full((C, C)), _full((8, C))],
        [jax.ShapeDtypeStruct((T, C), BF16), jax.ShapeDtypeStruct((T, C), BF16),
         jax.ShapeDtypeStruct((C, C), F32), jax.ShapeDtypeStruct((C, C), F32), jax.ShapeDtypeStruct((8, C), F32)],
        [pltpu.VMEM((tm, C), F32), pltpu.VMEM((tm, C), F32), pltpu.VMEM((tm, C), F32),
         pltpu.VMEM((8, C), F32), pltpu.VMEM((8, C), F32), pltpu.VMEM((8, C), F32)],
        (drec, gr, h, h, xc, xr, cw, wa, ba, wx, bx, lam), exchange)


def _out_proj(att, rec, x, w_out, g1, b1):
    T = x.shape[0]
    tm = min(1024, T)

    def body(att_ref, rec_ref, x_ref, w_ref, g1_ref, b1_ref, z_ref, h_ref):
        mix = _mm(att_ref[...], w_ref[0:512, :]) + _mm(rec_ref[...], w_ref[512:1024, :])
        z1 = ALPHA * x_ref[...] + mix
        z_ref[...] = z1
        h1, _, _ = _ln(z1, g1_ref[...], b1_ref[...])
        h_ref[...] = h1.astype(MXU_DTYPE).astype(BF16)

    return pl.pallas_call(
        body, name="out_proj", grid=(T // tm,),
        in_specs=[_rows(tm, 512), _rows(tm, 512), _rows(tm, D), _full((D, D)), _full((1, D)), _full((1, D))],
        out_specs=[_rows(tm, D), _rows(tm, D)],
        out_shape=[jax.ShapeDtypeStruct((T, D), F32), jax.ShapeDtypeStruct((T, D), BF16)],
        compiler_params=_params(),
    )(att, rec, x, w_out, g1, b1)


NC = D_FF // FF_CHUNK
UP_CHUNKS = 2 * D_FF // 4 // FF_CHUNK


def _ffn_up(h1b, w_up_a, w_up_b, fcw, fcb, exchange=None):
    T = h1b.shape[0]
    tm = min(1024, T)
    CW = FF_CHUNK

    def body(h_ref, ga_ref, gb_ref, va_ref, vb_ref, fcw_ref, fcb_ref, gate_ref, ge_ref, vd_ref, act_ref, before):
        i = pl.program_id(1)
        third = pl.program_id(0) % UP_CHUNKS == UP_CHUNKS - 1

        @pl.when(i == 0)
        def _():
            before[...] = jnp.zeros((8, CW), F32)

        hb = h_ref[...]
        gate = _mm_nt(hb, jnp.where(third, gb_ref[...], ga_ref[...]))
        val = _mm_nt(hb, jnp.where(third, vb_ref[...], va_ref[...]))
        gate_ref[...] = gate.astype(BF16)
        edge = before[...]
        gc = (fcb_ref[...] + fcw_ref[0:1, :] * _shift_rows(gate, 2, edge) + fcw_ref[1:2, :] * _shift_rows(gate, 1, edge)
              + fcw_ref[2:3, :] * gate)
        before[...] = gate[tm - 8:tm, :]
        ge, dge = _gelu(gc)
        ge_ref[...] = ge.astype(BF16)
        vd_ref[...] = (val * dge).astype(BF16)
        act_ref[...] = (ge * val).astype(BF16)

    chunk = pl.BlockSpec((None, tm, CW), lambda c, i: (c, i, 0))
    return _launch(
        body, "ffn_up", (NC, T // tm),
        [pl.BlockSpec((tm, D), lambda c, i: (i, 0)),
         pl.BlockSpec((None, CW, D), lambda c, i: (c // UP_CHUNKS, jnp.minimum(c % UP_CHUNKS, UP_CHUNKS - 2), 0)),
         pl.BlockSpec((None, CW, D), lambda c, i: (c // UP_CHUNKS, 0, 0)),
         pl.BlockSpec((None, CW, D), lambda c, i: (2 + c // UP_CHUNKS, jnp.minimum(c % UP_CHUNKS, UP_CHUNKS - 2), 0)),
         pl.BlockSpec((None, CW, D), lambda c, i: (2 + c // UP_CHUNKS, 0, 0)),
         pl.BlockSpec((None, 3, CW), lambda c, i: (c, 0, 0)), pl.BlockSpec((None, 1, CW), lambda c, i: (c, 0, 0))],
        [chunk] * 4, [jax.ShapeDtypeStruct((NC, T, CW), BF16)] * 4, [pltpu.VMEM((8, CW), F32)],
        (h1b, w_up_a, w_up_b, w_up_a, w_up_b, fcw, fcb), exchange)


def _ffn_down(act, z1, p, tgt, w_down, w_g, w_p_t, g1, b1, g2, b2, bg):
    T = z1.shape[0]
    tm = 512

    def body(act_ref, z_ref, p_ref, t_ref, wdn_hbm, wg_hbm, wp_hbm, g1_ref, b1_ref, g2_ref, b2_ref, bg_ref,
             dz2_ref, dz2b_ref, dpre_ref, dpp_ref, vec_ref, wdn, wg, wp):
        @pl.when(pl.program_id(0) == 0)
        def _():
            pltpu.sync_copy(wdn_hbm, wdn)
            pltpu.sync_copy(wg_hbm, wg)
            pltpu.sync_copy(wp_hbm, wp)
            vec_ref[...] = jnp.zeros_like(vec_ref)

        g2v = g2_ref[...]
        for r in (slice(0, tm // 2), slice(tm // 2, tm)):
            h1, _, _ = _ln(z_ref[r, :], g1_ref[...], b1_ref[...])
            h1b = h1.astype(MXU_DTYPE)
            ffn = _mm(act_ref[0, r, :], wdn[0:FF_CHUNK, :])
            for c in range(1, NC):
                ffn = ffn + _mm(act_ref[c, r, :], wdn[c * FF_CHUNK:(c + 1) * FF_CHUNK, :])
            sg = _sigmoid(_mm(h1b, wg[...]) + bg_ref[...])
            pp = _mm_nt(p_ref[r, :], wp[...])
            z2 = ALPHA * h1 + ffn + sg * pp
            y, xh2, rstd2 = _ln(z2, g2v, b2_ref[...])
            diff = y - t_ref[r, :]
            dy = diff * (1.0 / D)
            dz2 = _ln_bwd(dy, xh2, rstd2, g2v)
            dpre = dz2 * pp * sg * (1.0 - sg)
            dz2_ref[r, :] = dz2
            dz2b_ref[r, :] = dz2.astype(BF16)
            dpre_ref[r, :] = dpre.astype(BF16)
            dpp_ref[r, :] = (dz2 * sg).astype(BF16)
            loss = 0.5 * jnp.sum(jnp.sum(diff * diff, axis=1, keepdims=True), axis=0, keepdims=True) * (1.0 / D)
            vec_ref[0:1, :] += jnp.broadcast_to(loss, (1, D))
            vec_ref[1:2, :] += _colsum(dy * xh2)
            vec_ref[2:3, :] += _colsum(dy)
            vec_ref[3:4, :] += _colsum(dpre)

    anyspec = pl.BlockSpec(memory_space=pl.ANY)
    vec = _full((1, D))
    return pl.pallas_call(
        body, name="ffn_down", grid=(T // tm,),
        in_specs=[pl.BlockSpec((NC, tm, FF_CHUNK), lambda i: (0, i, 0)), _rows(tm, D), _rows(tm, PLE), _rows(tm, D),
                  anyspec, anyspec, anyspec] + [vec] * 5,
        out_specs=[_rows(tm, D)] * 4 + [_full((8, D))],
        out_shape=[jax.ShapeDtypeStruct((T, D), F32)] + [jax.ShapeDtypeStruct((T, D), BF16)] * 3
                  + [jax.ShapeDtypeStruct((8, D), F32)],
        scratch_shapes=[pltpu.VMEM((D_FF, D), MXU_DTYPE), pltpu.VMEM((D, D), MXU_DTYPE), pltpu.VMEM((D, PLE), MXU_DTYPE)],
        compiler_params=_params(),
    )(act, z1, p, tgt, w_down, w_g, w_p_t, g1, b1, g2, b2, bg)


def _ffn_bwd(dz2b, gate, ge, vd, w_down, fcw):
    T = dz2b.shape[0]
    tm = min(1024, T)
    CW = FF_CHUNK
    nt = T // tm

    def body(dz_ref, wdn_ref, gate_ref, ge_ref, vd_ref, fcw_ref, dup_ref, dfc_ref, after):
        i = pl.program_id(1)

        @pl.when(i == 0)
        def _():
            after[...] = jnp.zeros((8, CW), F32)
            dfc_ref[...] = jnp.zeros_like(dfc_ref)

        gate = gate_ref[...].astype(F32)
        dact = _mm_nt(dz_ref[...], wdn_ref[...])
        dgc = dact * vd_ref[...].astype(F32)
        edge = after[...]
        dgc1 = _shift_rows(dgc, -1, edge)
        dgc2 = _shift_rows(dgc, -2, edge)
        after[...] = dgc[0:8, :]
        dup_ref[0] = (fcw_ref[2:3, :] * dgc + fcw_ref[1:2, :] * dgc1 + fcw_ref[0:1, :] * dgc2).astype(BF16)
        dup_ref[1] = (dact * ge_ref[...].astype(F32)).astype(BF16)
        dfc_ref[0:1, :] += _colsum(dgc2 * gate)
        dfc_ref[1:2, :] += _colsum(dgc1 * gate)
        dfc_ref[2:3, :] += _colsum(dgc * gate)
        dfc_ref[3:4, :] += _colsum(dgc)

    rev = lambda c, i: (c, nt - 1 - i, 0)
    chunk = pl.BlockSpec((None, tm, CW), rev)
    return pl.pallas_call(
        body, name="ffn_bwd", grid=(NC, nt),
        in_specs=[pl.BlockSpec((tm, D), lambda c, i: (nt - 1 - i, 0)), pl.BlockSpec((CW, D), lambda c, i: (c, 0)),
                  chunk, chunk, chunk, pl.BlockSpec((None, 3, CW), lambda c, i: (c, 0, 0))],
        out_specs=[pl.BlockSpec((None, 2, tm, CW), lambda c, i: (c, 0, nt - 1 - i, 0)),
                   pl.BlockSpec((None, 8, CW), lambda c, i: (c, 0, 0))],
        out_shape=[jax.ShapeDtypeStruct((NC, 2, T, CW), BF16), jax.ShapeDtypeStruct((NC, 8, CW), F32)],
        scratch_shapes=[pltpu.VMEM((8, CW), F32)],
        compiler_params=_params(),
    )(dz2b, w_down, gate, ge, vd, fcw)


def _ffn_dh1(dup, dz2, dpre, z1, w_up_a, w_up_b, w_g, g1, b1):
    T = z1.shape[0]
    tm = 512
    chip_rows = UP_CHUNKS * FF_CHUNK
    rows_a = chip_rows - FF_CHUNK

    def body(dup_ref, dz2_ref, dpre_ref, z_ref, wa_hbm, wb_hbm, wg_hbm, g1_ref, b1_ref, dz1_ref, vec_ref, wup, wg):
        @pl.when(pl.program_id(0) == 0)
        def _():
            for j in range(4):
                pltpu.sync_copy(wa_hbm.at[j], wup.at[pl.ds(j * chip_rows, rows_a)])
                pltpu.sync_copy(wb_hbm.at[j, pl.ds(0, FF_CHUNK)], wup.at[pl.ds(j * chip_rows + rows_a, FF_CHUNK)])
            pltpu.sync_copy(wg_hbm, wg)
            vec_ref[...] = jnp.zeros_like(vec_ref)

        g1v = g1_ref[...]
        _, xh1, rstd1 = _ln(z_ref[...], g1v, b1_ref[...])
        dh1 = ALPHA * dz2_ref[...] + _mm_nt(dpre_ref[...], wg[...])
        for c in range(NC):
            for s in range(2):
                r0 = s * D_FF + c * FF_CHUNK
                dh1 = dh1 + _mm(dup_ref[c, s], wup[r0:r0 + FF_CHUNK, :])
        dz1_ref[...] = _ln_bwd(dh1, xh1, rstd1, g1v)
        vec_ref[0:1, :] += _colsum(dh1 * xh1)
        vec_ref[1:2, :] += _colsum(dh1)

    anyspec = pl.BlockSpec(memory_space=pl.ANY)
    vec = _full((1, D))
    return pl.pallas_call(
        body, name="ffn_dh1", grid=(T // tm,),
        in_specs=[pl.BlockSpec((NC, 2, tm, FF_CHUNK), lambda i: (0, 0, i, 0)), _rows(tm, D), _rows(tm, D), _rows(tm, D),
                  anyspec, anyspec, anyspec, vec, vec],
        out_specs=[_rows(tm, D), _full((8, D))],
        out_shape=[jax.ShapeDtypeStruct((T, D), F32), jax.ShapeDtypeStruct((8, D), F32)],
        scratch_shapes=[pltpu.VMEM((2 * D_FF, D), MXU_DTYPE), pltpu.VMEM((D, D), MXU_DTYPE)],
        compiler_params=_params(),
    )(dup, dz2, dpre, z1, w_up_a, w_up_b, w_g, g1, b1)


def _out_proj_bwd(dz1, w_out, exchange=None):
    T = dz1.shape[0]
    tm = min(1024, T)

    def body(dz_ref, w_ref, datt_ref, drec_ref):
        dzb = dz_ref[...].astype(MXU_DTYPE)
        datt = _mm_nt(dzb, w_ref[0:512, :])
        for h in range(HEADS):
            datt_ref[h] = datt[:, h * 64:(h + 1) * 64].astype(BF16)
        drec_ref[...] = _mm_nt(dzb, w_ref[512:1024, :])

    return _launch(body, "out_proj_bwd", (T // tm,), [_rows(tm, D), _full((D, D))], [_heads(tm), _rows(tm, 512)],
                   [jax.ShapeDtypeStruct((HEADS, T, 64), BF16), jax.ShapeDtypeStruct((T, 512), F32)], [],
                   (dz1, w_out), exchange)


def _in_proj_bwd(dq, dkv, dxr, dgr, dz1, w_in_t, exchange=None):
    T = dz1.shape[0]
    tm = 512
    W = D_IN // 4

    def body(dq_ref, dkv_ref, dxr_ref, dgr_ref, dz_ref, w_ref, dx_ref, du_ref):
        dkv = dkv_ref[...]
        dx_ref[...] = (ALPHA * dz_ref[...] + _mm(dq_ref[...], w_ref[0:512, :]) + _mm(dkv, w_ref[512:768, :])
                       + _mm(dxr_ref[...], w_ref[768:1280, :]) + _mm(dgr_ref[...], w_ref[1280:1792, :]))
        dq, dxr, dgr = dq_ref[...].astype(F32), dxr_ref[...].astype(F32), dgr_ref[...].astype(F32)
        du_ref[0] = dq[:, 0:W].astype(BF16)
        du_ref[1, :, 0:64] = dq[:, W:512].astype(BF16)
        du_ref[1, :, 64:320] = dkv.astype(BF16)
        du_ref[1, :, 320:W] = dxr[:, 0:128].astype(BF16)
        du_ref[2, :, 0:384] = dxr[:, 128:512].astype(BF16)
        du_ref[2, :, 384:W] = dgr[:, 0:64].astype(BF16)
        du_ref[3] = dgr[:, 64:512].astype(BF16)

    return _launch(body, "in_proj_bwd", (T // tm,),
                   [_rows(tm, 512), _rows(tm, 256), _rows(tm, 512), _rows(tm, 512), _rows(tm, D), _full((D_IN, D))],
                   [_rows(tm, D), pl.BlockSpec((4, tm, W), lambda i: (0, i, 0))],
                   [jax.ShapeDtypeStruct((T, D), F32), jax.ShapeDtypeStruct((4, T, W), BF16)], [],
                   (dq, dkv, dxr, dgr, dz1, w_in_t), exchange)


def _accumulate_tn(a_ref, b_ref, o_ref):
    @pl.when(pl.program_id(1) == 0)
    def _():
        o_ref[...] = jnp.zeros_like(o_ref)

    o_ref[...] += _mm_tn(a_ref[...], b_ref[...])


def _weight_grad_cols(a, b, name, n_blocks, b_spec, out_shape, out_spec, exchange=None):
    T, M = a.shape
    bt = min(DW_TOKENS, T)
    return _launch(functools.partial(_accumulate_tn), name, (n_blocks, T // bt),
                   [pl.BlockSpec((bt, M), lambda m, k: (k, 0)), b_spec(bt)], [out_spec],
                   [jax.ShapeDtypeStruct(out_shape, F32)], [], (a, b), exchange)


def _dw_out(att, rec, dz1):
    T = dz1.shape[0]
    bt = min(DW_TOKENS // 2, T)

    def body(att_ref, rec_ref, dz_ref, o_ref):
        @pl.when(pl.program_id(0) == 0)
        def _():
            o_ref[...] = jnp.zeros_like(o_ref)

        dz = dz_ref[...].astype(MXU_DTYPE)
        o_ref[0:512, :] += _mm_tn(att_ref[...], dz)
        o_ref[512:1024, :] += _mm_tn(rec_ref[...], dz)

    return pl.pallas_call(
        body, name="dw_out", grid=(T // bt,), in_specs=[_rows(bt, 512), _rows(bt, 512), _rows(bt, D)],
        out_specs=_full((D, D)), out_shape=jax.ShapeDtypeStruct((D, D), F32), compiler_params=_params())(att, rec, dz1)


def _weight_grad(a, b, bm, name, exchange=None):
    bt = min(DW_TOKENS // 2 if b.dtype == F32 else DW_TOKENS, b.shape[0])
    if a.ndim == 3:
        assert a.shape[2] == bm
        T, M = a.shape[1], a.shape[0] * bm
        a_spec = pl.BlockSpec((None, bt, bm), lambda m, k: (m, k, 0))
    else:
        T, M = a.shape
        a_spec = pl.BlockSpec((bt, bm), lambda m, k: (k, m))
    N = b.shape[1]
    nk = T // bt

    out = _launch(functools.partial(_accumulate_tn), name, (M // bm, nk),
                  [a_spec, pl.BlockSpec((bt, N), lambda m, k: (k, 0))], [pl.BlockSpec((bm, N), lambda m, k: (m, 0))],
                  [jax.ShapeDtypeStruct((M, N), F32)], [], (a, b), exchange)
    return out[0] if exchange is None else out


def _adamw(w, g, m, v, name):
    R, C = w.shape
    tr = R // 8 if R % 64 == 0 else R
    c1 = 1.0 / (1.0 - ADAM_B1 ** ADAM_STEP)
    c2 = 1.0 / (1.0 - ADAM_B2 ** ADAM_STEP)

    def body(w_ref, g_ref, m_ref, v_ref, d_ref, nm_ref, nv_ref):
        g = g_ref[...]
        nm = ADAM_B1 * m_ref[...] + (1.0 - ADAM_B1) * g
        nv = ADAM_B2 * v_ref[...] + (1.0 - ADAM_B2) * g * g
        nm_ref[...] = nm
        nv_ref[...] = nv
        d_ref[...] = -ADAM_LR * ((nm * c1) / (jnp.sqrt(nv * c2) + ADAM_EPS) + ADAM_WD * w_ref[...])

    spec = pl.BlockSpec((tr, C), lambda i: (i, 0))
    return pl.pallas_call(
        body, name=name, grid=(R // tr,),
        in_specs=[spec] * 4, out_specs=[spec] * 3,
        out_shape=[jax.ShapeDtypeStruct((R, C), F32)] * 3,
        compiler_params=_params(),
    )(w, g, m, v)


def _adamw_halves(ws, mines, sibs, ms, vs, c, name, exchange=None):
    n, nb = len(ws), 4
    c1 = 1.0 / (1.0 - ADAM_B1 ** ADAM_STEP)
    c2 = 1.0 / (1.0 - ADAM_B2 ** ADAM_STEP)

    def body(c_ref, *refs):
        own = (pl.program_id(0) // nb) == c_ref[0]
        for i in range(n):
            w_ref, a_ref, b_ref, m_ref, v_ref = refs[5 * i:5 * i + 5]
            g_ref, d_ref, nm_ref, nv_ref = refs[5 * n + 4 * i:5 * n + 4 * i + 4]
            g = jnp.where(own, a_ref[...], b_ref[...])
            nm = ADAM_B1 * m_ref[...] + (1.0 - ADAM_B1) * g
            nv = ADAM_B2 * v_ref[...] + (1.0 - ADAM_B2) * g * g
            g_ref[...] = g
            nm_ref[...] = nm
            nv_ref[...] = nv
            d_ref[...] = -ADAM_LR * ((nm * c1) / (jnp.sqrt(nv * c2) + ADAM_EPS) + ADAM_WD * w_ref[...])

    in_specs, out_specs, out_shape, args = [], [], [], []
    for w, a, b, m, v in zip(ws, mines, sibs, ms, vs):
        R, C = w.shape
        tr = R // (2 * nb)
        assert tr % 8 == 0 and a.shape == (R // 2, C)
        full = pl.BlockSpec((tr, C), lambda i, c_ref: (i, 0))
        mine_spec = pl.BlockSpec((tr, C), lambda i, c_ref: (jnp.where(i // nb == c_ref[0], i % nb, nb - 1), 0))
        sib_spec = pl.BlockSpec((tr, C), lambda i, c_ref: (jnp.where(i // nb == c_ref[0], nb - 1, i % nb), 0))
        in_specs += [full, mine_spec, sib_spec, full, full]
        out_specs += [full] * 4
        out_shape += [jax.ShapeDtypeStruct((R, C), F32)] * 4
        args += [w, a, b, m, v]
    out = _launch(body, name, (2 * nb,), in_specs, out_specs, out_shape, [], (c, *args), exchange, prefetch=1)
    return [tuple(out[4 * i:4 * i + 4]) for i in range(n)], list(out[4 * n:])


def _add4(fs, name):
    n = len(fs)

    def body(*refs):
        for a_ref, o_ref in zip(refs[:n], refs[n:]):
            o_ref[...] = ((a_ref[0].astype(F32) + a_ref[1].astype(F32)) + a_ref[2].astype(F32)) + a_ref[3].astype(F32)

    for f in fs:
        assert (f.shape[1] // 2) % 16 == 0
    return pl.pallas_call(
        body, name=name, grid=(2,),
        in_specs=[pl.BlockSpec((4, f.shape[1] // 2, f.shape[2]), lambda i: (0, i, 0)) for f in fs],
        out_specs=[pl.BlockSpec((f.shape[1] // 2, f.shape[2]), lambda i: (i, 0)) for f in fs],
        out_shape=[jax.ShapeDtypeStruct(f.shape[1:], F32) for f in fs], compiler_params=_params())(*fs)


def _gather_first(wsrc, cpack):
    def body(w_ref, c_ref, gw_ref, gc_ref, send_sems, recv_sems, local_sem, csend, crecv, clocal):
        x, y, c = _pos()
        me = 2 * x + y
        chips = _other_chips(x, y)
        start, forward, finish = _gather_steps(w_ref, gw_ref, send_sems, recv_sems, local_sem)
        start()
        loc = pltpu.make_async_copy(c_ref, gc_ref.at[me], clocal)
        loc.start()

        def conv_copy(k, slot):
            px, py = chips[k]
            return pltpu.make_async_remote_copy(src_ref=c_ref, dst_ref=gc_ref.at[slot], send_sem=csend.at[k],
                                                recv_sem=crecv.at[k], device_id=(px, py, c), device_id_type=MESH)

        for k in range(3):
            conv_copy(k, me).start()
        forward()
        finish()
        for k, (px, py) in enumerate(chips):
            conv_copy(k, 2 * px + py).wait_recv()
        for k in range(3):
            conv_copy(k, me).wait_send()
        loc.wait()

    anyspec = pl.BlockSpec(memory_space=pl.ANY)
    return pl.pallas_call(
        body, name="gather_first",
        in_specs=[anyspec, anyspec], out_specs=[anyspec, anyspec],
        out_shape=[jax.ShapeDtypeStruct((4,) + wsrc.shape, wsrc.dtype), jax.ShapeDtypeStruct((4,) + cpack.shape, cpack.dtype)],
        scratch_shapes=GATHER_SCRATCH + [pltpu.SemaphoreType.DMA((3,)), pltpu.SemaphoreType.DMA((3,)), pltpu.SemaphoreType.DMA],
        compiler_params=_params(has_side_effects=True),
    )(wsrc, cpack)


def _all_devices_exchange(s):
    def make(ins, outs, sems):
        s_ref, o_ref = ins[0], outs[0]
        send_sems, recv_sems, local_sem = sems
        x, y, c = _pos()
        me = 4 * x + 2 * y + c
        loc = pltpu.make_async_copy(s_ref, o_ref.at[me], local_sem)

        def copy(k, slot):
            peer = (x ^ (k >> 2), y ^ ((k >> 1) & 1), c ^ (k & 1))
            return pltpu.make_async_remote_copy(src_ref=s_ref, dst_ref=o_ref.at[slot], send_sem=send_sems.at[k - 1],
                                                recv_sem=recv_sems.at[k - 1], device_id=peer, device_id_type=MESH)

        def start():
            loc.start()
            for k in range(1, 8):
                copy(k, me).start()

        def finish():
            for k in range(1, 8):
                copy(k, 4 * (x ^ (k >> 2)) + 2 * (y ^ ((k >> 1) & 1)) + (c ^ (k & 1))).wait_recv()
            for k in range(1, 8):
                copy(k, me).wait_send()
            loc.wait()

        return start, lambda: None, finish

    return _Exchange([s], [jax.ShapeDtypeStruct((8,) + s.shape, s.dtype)],
                     [pltpu.SemaphoreType.DMA((7,)), pltpu.SemaphoreType.DMA((7,)), pltpu.SemaphoreType.DMA], make)


def _sum_devices(a):
    def body(a_ref, o_ref):
        acc = a_ref[0]
        for d in range(1, 8):
            acc = acc + a_ref[d]
        o_ref[...] = acc

    vm = pl.BlockSpec(memory_space=pltpu.VMEM)
    return pl.pallas_call(body, name="sum_devices", in_specs=[vm], out_specs=vm,
                          out_shape=jax.ShapeDtypeStruct(a.shape[1:], F32), compiler_params=_params())(a)


def _swap_exchange(gs):
    n = len(gs)

    def make(ins, outs, sems):
        x, y, c = _pos()
        cps = []
        for i in range(n):
            half = gs[i].shape[1] // 2
            rows = pl.ds(pl.multiple_of((1 - c) * half, 8), half)
            cps.append(pltpu.make_async_remote_copy(src_ref=ins[i].at[:, rows, :], dst_ref=outs[i], send_sem=sems[0].at[i],
                                                    recv_sem=sems[1].at[i], device_id=(x, y, 1 - c), device_id_type=MESH))

        def start():
            for cp in cps:
                cp.start()

        def finish():
            for cp in cps:
                cp.wait()

        return start, lambda: None, finish

    return _Exchange(gs, [jax.ShapeDtypeStruct((4, g.shape[1] // 2, g.shape[2]), g.dtype) for g in gs],
                     [pltpu.SemaphoreType.DMA((n,)), pltpu.SemaphoreType.DMA((n,))], make)


def _scatter_exchange(ss):
    n = len(ss)

    def make(ins, outs, sems):
        send_sems, recv_sems, local_sems = sems
        x, y, c = _pos()
        me = 2 * x + y
        chips = _other_chips(x, y)
        locs = [pltpu.make_async_copy(ins[i].at[me], outs[i].at[me], local_sems.at[i]) for i in range(n)]

        def copy(i, k, src_slot, dst_slot):
            px, py = chips[k]
            return pltpu.make_async_remote_copy(src_ref=ins[i].at[src_slot], dst_ref=outs[i].at[dst_slot],
                                                send_sem=send_sems.at[3 * i + k], recv_sem=recv_sems.at[3 * i + k],
                                                device_id=(px, py, c), device_id_type=MESH)

        def start():
            for i in range(n):
                locs[i].start()
                for k, (px, py) in enumerate(chips):
                    copy(i, k, 2 * px + py, me).start()

        def finish():
            for i in range(n):
                for k, (px, py) in enumerate(chips):
                    copy(i, k, me, 2 * px + py).wait_recv()
            for i in range(n):
                for k, (px, py) in enumerate(chips):
                    copy(i, k, 2 * px + py, me).wait_send()
                locs[i].wait()

        return start, lambda: None, finish

    return _Exchange(ss, [jax.ShapeDtypeStruct(s.shape, s.dtype) for s in ss],
                     [pltpu.SemaphoreType.DMA((3 * n,)), pltpu.SemaphoreType.DMA((3 * n,)), pltpu.SemaphoreType.DMA((n,))], make)


def _send_exchange(rs):
    n = len(rs)

    def make(ins, outs, sems):
        x, y, c = _pos()
        cps = [pltpu.make_async_remote_copy(src_ref=ins[i], dst_ref=outs[i], send_sem=sems[0].at[i], recv_sem=sems[1].at[i],
                                            device_id=(x, y, 1 - c), device_id_type=MESH) for i in range(n)]

        def start():
            for cp in cps:
                cp.start()

        def finish():
            for cp in cps:
                cp.wait()

        return start, lambda: None, finish

    return _Exchange(rs, [jax.ShapeDtypeStruct(r.shape, r.dtype) for r in rs],
                     [pltpu.SemaphoreType.DMA((n,)), pltpu.SemaphoreType.DMA((n,))], make)


def _reduce_in_vmem(g):
    _, R, C = g.shape
    H = R // 2

    def body(g_ref, mine_ref, other_ref, sib, part, got, swap_sems, send_sems, recv_sems, last_sems):
        x, y, c = _pos()
        me = 2 * x + y
        chips = _other_chips(x, y)
        sibling = (x, y, 1 - c)
        mine = pl.ds(pl.multiple_of(c * H, 8), H)
        theirs = pl.ds(pl.multiple_of((1 - c) * H, 8), H)
        swap = pltpu.make_async_remote_copy(src_ref=g_ref.at[:, theirs, :], dst_ref=sib, send_sem=swap_sems.at[0],
                                            recv_sem=swap_sems.at[1], device_id=sibling, device_id_type=MESH)
        swap.start()
        swap.wait()
        part[...] = (g_ref[:, mine, :] + sib[...]).astype(BF16)

        def copy(k, src_slot, dst_slot):
            px, py = chips[k]
            return pltpu.make_async_remote_copy(src_ref=part.at[src_slot], dst_ref=got.at[dst_slot], send_sem=send_sems.at[k],
                                                recv_sem=recv_sems.at[k], device_id=(px, py, c), device_id_type=MESH)

        for k, (px, py) in enumerate(chips):
            copy(k, 2 * px + py, me).start()
        got[me] = part[me]
        for k, (px, py) in enumerate(chips):
            copy(k, me, 2 * px + py).wait_recv()
        for k, (px, py) in enumerate(chips):
            copy(k, 2 * px + py, me).wait_send()
        mine_ref[...] = ((got[0].astype(F32) + got[1].astype(F32)) + got[2].astype(F32)) + got[3].astype(F32)
        last = pltpu.make_async_remote_copy(src_ref=mine_ref, dst_ref=other_ref, send_sem=last_sems.at[0],
                                            recv_sem=last_sems.at[1], device_id=sibling, device_id_type=MESH)
        last.start()
        last.wait()

    vm = pl.BlockSpec(memory_space=pltpu.VMEM)
    half = jax.ShapeDtypeStruct((H, C), F32)
    return pl.pallas_call(
        body, name="reduce_late", in_specs=[vm], out_specs=[vm, vm], out_shape=[half, half],
        scratch_shapes=[pltpu.VMEM((4, H, C), F32), pltpu.VMEM((4, H, C), BF16), pltpu.VMEM((4, H, C), BF16),
                        pltpu.SemaphoreType.DMA((2,)), pltpu.SemaphoreType.DMA((3,)), pltpu.SemaphoreType.DMA((3,)),
                        pltpu.SemaphoreType.DMA((2,))],
        compiler_params=_params(has_side_effects=True))(g)


def _add_half(gs, rs, c, name):
    n = len(gs)

    def body(c_ref, *refs):
        for g_ref, r_ref, o_ref in zip(refs[:n], refs[n:2 * n], refs[2 * n:]):
            o_ref[...] = (g_ref[...] + r_ref[...]).astype(BF16)

    g_specs, r_specs, out_shape = [], [], []
    for g, r in zip(gs, rs):
        _, H, C = r.shape
        tr = H // 2
        assert tr % 16 == 0 and g.shape == (4, 2 * H, C)
        g_specs.append(pl.BlockSpec((1, tr, C), lambda j, i, c_ref: (j, c_ref[0] * 2 + i, 0)))
        r_specs.append(pl.BlockSpec((1, tr, C), lambda j, i, c_ref: (j, i, 0)))
        out_shape.append(jax.ShapeDtypeStruct((4, H, C), BF16))
    grid_spec = pltpu.PrefetchScalarGridSpec(num_scalar_prefetch=1, grid=(4, 2), in_specs=g_specs + r_specs, out_specs=r_specs)
    return pl.pallas_call(body, name=name, grid_spec=grid_spec, out_shape=out_shape, compiler_params=_params())(c, *gs, *rs)


def _block_diag(w):
    eye = jnp.eye(RNN_BLOCKS, dtype=w.dtype)
    return (eye[:, None, :, None] * w[:, :, None, :]).reshape(D_RNN, D_RNN)


def _diag_blocks(wd):
    d = wd.reshape(RNN_BLOCKS, 64, RNN_BLOCKS, 64)
    return jnp.stack([d[h, :, h, :] for h in range(RNN_BLOCKS)])


def _split_pack(a, first, last):
    out, base = {}, PACK_OFF[first]
    for i in range(first, last):
        s = a[:, PACK_OFF[i] - base:PACK_OFF[i + 1] - base]
        out[BIG_KEYS[i]] = s.reshape(4 * 256, 256) if BIG_KEYS[i] == "w_p_t" else s.reshape(-1, 1024)
    return out


def _layer_grads(x, p, tgt, gw, small, shard=None, core=None):
    row = lambda v: v.reshape(1, -1)
    wa = _block_diag(small["gate_a_w"]).astype(MXU_DTYPE)
    wx = _block_diag(small["gate_x_w"]).astype(MXU_DTYPE)
    sinks = small["attn_sinks"].reshape(1, HEADS)

    dist = shard is not None
    q, kv, xr, gr, xb = _in_proj(x, gw["w_in_t"])
    cut = PACK_OFF[1] + (UP_CHUNKS - 1) * FF_CHUNK
    att, *ga = _attn_fwd(q, kv, sinks, _gather_exchange(shard[PACK_OFF[1]:cut]) if dist else None)
    xc, h, rec, *gb = _rnn_fwd(xr, gr, small["rnn_conv_w"], row(small["rnn_conv_b"]), wa, row(small["gate_a_b"]),
                               wx, row(small["gate_x_b"]), row(small["lru_lambda"]),
                               _gather_exchange(shard[cut:PACK_OFF[3]]) if dist else None)
    if dist:
        w_up_a, w_up_b = ga[0], gb[0]
        gw = {**gw, "w_out": gb[0][:, FF_CHUNK:].reshape(-1, 1024)}
    else:
        per_chip_rows = gw["w_up_t"].reshape(4, UP_CHUNKS * FF_CHUNK, 1024)
        w_up_a, w_up_b = per_chip_rows[:, :cut - PACK_OFF[1]], per_chip_rows[:, cut - PACK_OFF[1]:]
    g1, b1 = row(small["ln1_g"]), row(small["ln1_b"])
    fcw = small["ffn_conv_w"].reshape(3, NC, FF_CHUNK).transpose(1, 0, 2)
    fcb = small["ffn_conv_b"].reshape(NC, 1, FF_CHUNK)
    z1, h1b = _out_proj(att, rec, x, gw["w_out"], g1, b1)
    gate, ge, vd, act, *gc = _ffn_up(h1b, w_up_a, w_up_b, fcw, fcb,
                                     _gather_exchange(shard[PACK_OFF[3]:PACK_OFF[6]]) if dist else None)
    if dist:
        gw = {**gw, **_split_pack(gc[0], 3, 6)}
    dz2, dz2b, dpre, dpp, vec2 = _ffn_down(act, z1, p, tgt, gw["w_down"], gw["w_g"], gw["w_p_t"], g1, b1,
                                           row(small["ln2_g"]), row(small["ln2_b"]), row(small["ple_gate_b"]))
    dup, dfc = _ffn_bwd(dz2b, gate, ge, vd, gw["w_down"], fcw)
    dz1, vec1 = _ffn_dh1(dup, dz2, dpre, z1, w_up_a, w_up_b, gw["w_g"], g1, b1)
    per_chip = 2 * D_FF // 4 // FF_CHUNK
    big = {"w_ffn_up": _weight_grad_cols(
        h1b, dup.reshape(2 * NC, -1, FF_CHUNK), "dw_up", 2 * NC,
        lambda bt: pl.BlockSpec((None, bt, FF_CHUNK), lambda m, k: (m, k, 0)), (4, D, 2 * D_FF // 4),
        pl.BlockSpec((None, D, FF_CHUNK), lambda m, k: (2 * (m % 2) + (m // 2) // per_chip, 0, (m // 2) % per_chip)))[0]}
    g_dn, *got_up = _weight_grad(act, dz2b, 512, "dw_down", _swap_exchange([big["w_ffn_up"]])) if dist else (
        _weight_grad(act, dz2b, 512, "dw_down"),)
    big["w_ffn_down"] = g_dn.reshape(4, D_FF // 4, D)
    big["ple_gate_w"] = _weight_grad(h1b, dpre, 512, "dw_gate").reshape(4, D // 4, D)
    big["ple_proj"] = _weight_grad(p, dpp, PLE, "dw_proj").reshape(PLE, 4, D // 4).transpose(1, 0, 2)
    big["w_out"] = _dw_out(att, rec, dz1).reshape(4, D // 4, D)
    reduced = None
    if dist:
        g_ffn = [big[k] for k in EARLY_WEIGHTS]
        ex = _swap_exchange(g_ffn[1:])
    datt, drec, *got = _out_proj_bwd(dz1, gw["w_out"], ex if dist else None)
    if dist:
        sums = _add_half(g_ffn, got_up + got, core, "add_half_ffn")
        ex, ex2 = _scatter_exchange(sums[:1]), _scatter_exchange(sums[1:])
    dxr, dgr, dwa, dwx, dvec, *got = _rnn_bwd(drec, gr, h, xc, xr, small["rnn_conv_w"], wa, row(small["gate_a_b"]),
                                              wx, row(small["gate_x_b"]), row(small["lru_lambda"]), ex if dist else None)
    dq, dkv, dsinks, *got2 = _attn_bwd(q, kv, datt, sinks, ex2 if dist else None)
    if dist:
        mine = _add4(got + got2, "add_chips_ffn")
        big = {}
    sg = {
        "attn_sinks": dsinks[:, 0],
        "rnn_conv_w": dvec[4:8],
        "rnn_conv_b": dvec[3],
        "gate_a_w": _diag_blocks(dwa),
        "gate_a_b": dvec[0],
        "gate_x_w": _diag_blocks(dwx),
        "gate_x_b": dvec[1],
        "lru_lambda": dvec[2],
        "ln1_g": vec1[0],
        "ln1_b": vec1[1],
        "ffn_conv_w": dfc[:, 0:3].transpose(1, 0, 2).reshape(3, D_FF),
        "ffn_conv_b": dfc[:, 3].reshape(D_FF),
        "ple_gate_b": vec2[3],
        "ln2_g": vec2[1],
        "ln2_b": vec2[2],
    }
    loss = vec2[0, 0:1]
    grad_x, du = _in_proj_bwd(dq, dkv, dxr, dgr, dz1, gw["w_in_t"])
    ex = None
    if dist:
        ex = _join_exchanges(_send_exchange(mine), _all_devices_exchange(_pack_vecs([sg[k] for k in SMALL] + [loss])[0]))
    big["w_in"], *got = _weight_grad_cols(
        xb, du, "dw_in", 4, lambda bt: pl.BlockSpec((None, bt, D_IN // 4), lambda j, k: (j, k, 0)), (4, D, D_IN // 4),
        pl.BlockSpec((None, D, D_IN // 4), lambda j, k: (j, 0, 0)), ex)
    if dist:
        reduced = (mine, got[:len(mine)])
    return grad_x, big, sg, loss, reduced, got[-1:]


BIG = ("w_in", "w_ffn_up", "w_out", "w_ffn_down", "ple_gate_w", "ple_proj")
BIG_KEYS = ("w_in_t", "w_up_t", "w_out", "w_down", "w_g", "w_p_t")
BIG_T = (True, True, False, False, False, True)
EARLY_WEIGHTS = ("w_ffn_up", "w_ffn_down", "ple_gate_w", "ple_proj", "w_out")
LATE_WEIGHTS = ("w_in",)
SMALL = ("attn_sinks", "rnn_conv_w", "rnn_conv_b", "gate_a_w", "gate_a_b", "gate_x_w", "gate_x_b", "lru_lambda",
         "ln1_g", "ln1_b", "ffn_conv_w", "ffn_conv_b", "ple_gate_b", "ln2_g", "ln2_b")
WEIGHTS = ("w_in", "attn_sinks", "rnn_conv_w", "rnn_conv_b", "gate_a_w", "gate_a_b", "gate_x_w", "gate_x_b",
           "lru_lambda", "w_out", "ln1_g", "ln1_b", "w_ffn_up", "ffn_conv_w", "ffn_conv_b", "w_ffn_down",
           "ple_gate_w", "ple_gate_b", "ple_proj", "ln2_g", "ln2_b")


def _pack_big(d, first=0, last=6):
    parts = []
    for name, t in zip(BIG[first:last], BIG_T[first:last]):
        a = d[name]
        a = a.T if t else a
        parts.append(a.reshape(-1, 1024))
    return jnp.concatenate(parts, axis=0)


def _pack_vecs(items):
    parts, offs, n = [], [], 0
    for a in items:
        f = a.reshape(-1).astype(F32)
        pad = (-f.shape[0]) % 128
        parts.append(jnp.pad(f, (0, pad)))
        offs.append(n)
        n += (f.shape[0] + pad) // 128
    padr = (-n) % 8
    if padr:
        parts.append(jnp.zeros((padr * 128,), F32))
    return jnp.concatenate(parts).reshape(-1, 128), offs


def _unpack_vecs(a, offs, shapes):
    flat = a.reshape(-1)
    out = []
    for o, s in zip(offs, shapes):
        n = 1
        for d in s:
            n *= d
        out.append(flat[o * 128:o * 128 + n].reshape(s))
    return out


def kernel(x, p, w_in, attn_sinks, rnn_conv_w, rnn_conv_b, gate_a_w, gate_a_b, gate_x_w, gate_x_b, lru_lambda, w_out, ln1_g, ln1_b, w_ffn_up, ffn_conv_w, ffn_conv_b, w_ffn_down, ple_gate_w, ple_gate_b, ple_proj, ln2_g, ln2_b, loss_target, m_w_in, m_attn_sinks, m_rnn_conv_w, m_rnn_conv_b, m_gate_a_w, m_gate_a_b, m_gate_x_w, m_gate_x_b, m_lru_lambda, m_w_out, m_ln1_g, m_ln1_b, m_w_ffn_up, m_ffn_conv_w, m_ffn_conv_b, m_w_ffn_down, m_ple_gate_w, m_ple_gate_b, m_ple_proj, m_ln2_g, m_ln2_b, v_w_in, v_attn_sinks, v_rnn_conv_w, v_rnn_conv_b, v_gate_a_w, v_gate_a_b, v_gate_x_w, v_gate_x_b, v_lru_lambda, v_w_out, v_ln1_g, v_ln1_b, v_w_ffn_up, v_ffn_conv_w, v_ffn_conv_b, v_w_ffn_down, v_ple_gate_w, v_ple_gate_b, v_ple_proj, v_ln2_g, v_ln2_b):
    w = dict(w_in=w_in, attn_sinks=attn_sinks, rnn_conv_w=rnn_conv_w, rnn_conv_b=rnn_conv_b, gate_a_w=gate_a_w,
             gate_a_b=gate_a_b, gate_x_w=gate_x_w, gate_x_b=gate_x_b, lru_lambda=lru_lambda, w_out=w_out, ln1_g=ln1_g,
             ln1_b=ln1_b, w_ffn_up=w_ffn_up, ffn_conv_w=ffn_conv_w, ffn_conv_b=ffn_conv_b, w_ffn_down=w_ffn_down,
             ple_gate_w=ple_gate_w, ple_gate_b=ple_gate_b, ple_proj=ple_proj, ln2_g=ln2_g, ln2_b=ln2_b)
    m = dict(w_in=m_w_in, attn_sinks=m_attn_sinks, rnn_conv_w=m_rnn_conv_w, rnn_conv_b=m_rnn_conv_b, gate_a_w=m_gate_a_w,
             gate_a_b=m_gate_a_b, gate_x_w=m_gate_x_w, gate_x_b=m_gate_x_b, lru_lambda=m_lru_lambda, w_out=m_w_out,
             ln1_g=m_ln1_g, ln1_b=m_ln1_b, w_ffn_up=m_w_ffn_up, ffn_conv_w=m_ffn_conv_w, ffn_conv_b=m_ffn_conv_b,
             w_ffn_down=m_w_ffn_down, ple_gate_w=m_ple_gate_w, ple_gate_b=m_ple_gate_b, ple_proj=m_ple_proj,
             ln2_g=m_ln2_g, ln2_b=m_ln2_b)
    v = dict(w_in=v_w_in, attn_sinks=v_attn_sinks, rnn_conv_w=v_rnn_conv_w, rnn_conv_b=v_rnn_conv_b, gate_a_w=v_gate_a_w,
             gate_a_b=v_gate_a_b, gate_x_w=v_gate_x_w, gate_x_b=v_gate_x_b, lru_lambda=v_lru_lambda, w_out=v_w_out,
             ln1_g=v_ln1_g, ln1_b=v_ln1_b, w_ffn_up=v_w_ffn_up, ffn_conv_w=v_ffn_conv_w, ffn_conv_b=v_ffn_conv_b,
             w_ffn_down=v_w_ffn_down, ple_gate_w=v_ple_gate_w, ple_gate_b=v_ple_gate_b, ple_proj=v_ple_proj,
             ln2_g=v_ln2_g, ln2_b=v_ln2_b)
    w, m, v = ({k: a[0] for k, a in d.items()} for d in (w, m, v))
    chip = 2 * lax.axis_index("x") + lax.axis_index("y")
    core = lax.axis_index("c")

    wpack = _pack_big(w)
    cpack, _ = _pack_vecs([w["rnn_conv_w"], w["ffn_conv_w"]])
    shard = wpack.astype(MXU_DTYPE)
    g_in, gcp = _gather_first(shard[PACK_OFF[0]:PACK_OFF[1]], cpack)
    gw = _split_pack(g_in, 0, 1)
    small = {k: w[k] for k in SMALL}
    small["rnn_conv_w"] = gcp[:, 0:4].reshape(4, 4, 128).transpose(1, 0, 2).reshape(4, 512)
    small["ffn_conv_w"] = gcp[:, 4:22].reshape(4, 3, 768).transpose(1, 0, 2).reshape(3, 3072)

    core1 = core.reshape(1).astype(jnp.int32)
    grad_x, big, sg, loss, ffn_halves, small_all = _layer_grads(x[0], p[0, 0], loss_target[0], gw, small, shard, core1)

    shapes = [sg[k].shape for k in SMALL] + [(1,)]
    _, offs = _pack_vecs([jnp.zeros(s, F32) for s in shapes])
    red = dict(zip(SMALL + ("loss",), _unpack_vecs(_sum_devices(small_all[0]), offs, shapes)))
    red["rnn_conv_w"] = lax.dynamic_slice_in_dim(red["rnn_conv_w"], chip * 128, 128, axis=1)
    red["ffn_conv_w"] = lax.dynamic_slice_in_dim(red["ffn_conv_w"], chip * 768, 768, axis=1)

    late_mine, late_other = ([a] for a in _reduce_in_vmem(big["w_in"]))

    def adamw(names, mine, other, name):
        out, _ = _adamw_halves([w[k] for k in names], mine, other, [m[k] for k in names], [v[k] for k in names],
                               core1, name)
        return dict(zip(names, out))

    big_out = {**adamw(LATE_WEIGHTS, late_mine, late_other, "adamw_late"), **adamw(EARLY_WEIGHTS, *ffn_halves, "adamw_early")}
    wsm, offs2 = _pack_vecs([w[k] for k in SMALL])
    gsm, _ = _pack_vecs([red[k] for k in SMALL])
    msm, _ = _pack_vecs([m[k] for k in SMALL])
    vsm, _ = _pack_vecs([v[k] for k in SMALL])
    dsm, nmsm, nvsm = _adamw(wsm, gsm, msm, vsm, "adamw_small")
    shapes2 = [w[k].shape for k in SMALL]

    def named(n, smallp):
        d = {k: out[n][None] for k, out in big_out.items()}
        d.update({k: a[None] for k, a in zip(SMALL, _unpack_vecs(smallp, offs2, shapes2))})
        return [d[k] for k in WEIGHTS]

    return (red["loss"].reshape(()), grad_x[None], *named(0, gsm), *named(1, dsm), *named(2, nmsm), *named(3, nvsm))
```

```python
import functools

import jax
import jax.numpy as jnp
from jax import lax
from jax.experimental import pallas as pl
from jax.experimental.pallas import tpu as pltpu

F32 = jnp.float32
BF16 = jnp.bfloat16
MXU_DTYPE = jnp.bfloat16

D = 1024
D_ATT = 512
D_KV = 128
D_RNN = 512
D_IN = 1792
D_FF = 3072
FF_CHUNK = 768
PLE = 256
HEADS = 8
HEAD_DIM = 64
BLK = 128
ATTN_BLOCKS = 8
DW_TOKENS = 4096
RNN_BLOCKS = 8
LN_EPS = 1e-5
LRU_C = 8.0
ALPHA = float(2.0 ** 0.25)
SCALE = HEAD_DIM ** -0.5
NEG = -1e30

ADAM_LR = 0.001
ADAM_B1 = 0.9
ADAM_B2 = 0.999
ADAM_EPS = 1e-08
ADAM_WD = 0.01
ADAM_STEP = 10

VMEM_LIMIT_BYTES = 56 * 1024 * 1024
MESH = pl.DeviceIdType.MESH

PACK_ROWS = (448, 1536, 256, 768, 256, 64)
PACK_OFF = tuple(sum(PACK_ROWS[:i]) for i in range(len(PACK_ROWS) + 1))
PACK_TOTAL = PACK_OFF[-1]


def _params(**kw):
    return pltpu.CompilerParams(vmem_limit_bytes=VMEM_LIMIT_BYTES, **kw)


def _mm(a, b):
    return jnp.dot(a.astype(MXU_DTYPE), b.astype(MXU_DTYPE), preferred_element_type=F32)


def _mm_nt(a, b):
    return lax.dot_general(a.astype(MXU_DTYPE), b.astype(MXU_DTYPE), (((1,), (1,)), ((), ())),
                           preferred_element_type=F32)


def _mm_tn(a, b):
    return lax.dot_general(a.astype(MXU_DTYPE), b.astype(MXU_DTYPE), (((0,), (0,)), ((), ())),
                           preferred_element_type=F32)


def _sigmoid(x):
    return 0.5 + 0.5 * jnp.tanh(0.5 * x)


def _gelu(x):
    c = 0.7978845608028654
    k = 0.044715
    x2 = x * x
    t = jnp.tanh(x * (c + (c * k) * x2))
    h = 0.5 * (1.0 + t)
    return x * h, h * (1.0 + (x * (1.0 - t)) * (c + (3.0 * c * k) * x2))


def _shift_rows(x, s, edge8):
    R = x.shape[0]
    row8 = lax.broadcasted_iota(jnp.int32, (8, x.shape[1]), 0)
    if s > 0:
        rolled = pltpu.roll(x, s, 0)
        first = jnp.where(row8 < s, pltpu.roll(edge8, s, 0), rolled[0:8])
        return jnp.concatenate([first, rolled[8:]], axis=0)
    k = -s
    rolled = pltpu.roll(x, R - k, 0)
    last = jnp.where(row8 >= 8 - k, pltpu.roll(edge8, 8 - k, 0), rolled[R - 8:])
    return jnp.concatenate([rolled[:R - 8], last], axis=0)


def _softplus(x):
    return jnp.maximum(x, 0.0) + jnp.log(1.0 + jnp.exp(-jnp.abs(x)))


def _ln(z, g, b):
    mu = jnp.mean(z, axis=-1, keepdims=True)
    zc = z - mu
    var = jnp.mean(zc * zc, axis=-1, keepdims=True)
    rstd = lax.rsqrt(var + LN_EPS)
    xhat = zc * rstd
    return xhat * g + b, xhat, rstd


def _ln_bwd(dy, xhat, rstd, g):
    dxh = dy * g
    m1 = jnp.mean(dxh, axis=-1, keepdims=True)
    m2 = jnp.mean(dxh * xhat, axis=-1, keepdims=True)
    return rstd * (dxh - m1 - xhat * m2)


def _colsum(x):
    return jnp.sum(x, axis=0, keepdims=True)


def _full(shape):
    nd = len(shape)
    return pl.BlockSpec(shape, lambda *_: (0,) * nd)


def _rows(tm, cols, fn=None):
    if fn is None:
        return pl.BlockSpec((tm, cols), lambda i: (i, 0))
    return pl.BlockSpec((tm, cols), lambda i: (fn(i), 0))


def _heads(tm):
    return pl.BlockSpec((HEADS, tm, HEAD_DIM), lambda i: (0, i, 0))


def _in_proj(x, w_in_t):
    T = x.shape[0]
    tm = min(1024, T)

    def body(x_ref, w_ref, q_ref, kv_ref, xr_ref, gr_ref, xb_ref):
        xb = x_ref[...].astype(MXU_DTYPE)
        xb_ref[...] = xb.astype(BF16)
        q = _mm_nt(xb, w_ref[0:512, :])
        for h in range(HEADS):
            q_ref[h] = q[:, h * 64:(h + 1) * 64].astype(BF16)
        kv_ref[...] = _mm_nt(xb, w_ref[512:768, :]).astype(BF16)
        xr_ref[...] = _mm_nt(xb, w_ref[768:1280, :])
        gr_ref[...] = _mm_nt(xb, w_ref[1280:1792, :])

    return pl.pallas_call(
        body, name="in_proj", grid=(T // tm,),
        in_specs=[_rows(tm, D), _full((D_IN, D))],
        out_specs=[_heads(tm), _rows(tm, 256), _rows(tm, 512), _rows(tm, 512), _rows(tm, D)],
        out_shape=[jax.ShapeDtypeStruct((HEADS, T, 64), BF16), jax.ShapeDtypeStruct((T, 256), BF16),
                   jax.ShapeDtypeStruct((T, 512), F32), jax.ShapeDtypeStruct((T, 512), F32),
                   jax.ShapeDtypeStruct((T, D), BF16)],
        compiler_params=_params(),
    )(x, w_in_t)


def _attn_band(kv_ref, i):
    cur = pl.multiple_of(i * BLK, BLK)
    prev = pl.multiple_of(jnp.maximum(i - 1, 0) * BLK, BLK)
    band = jnp.concatenate([kv_ref[pl.ds(prev, BLK), :], kv_ref[pl.ds(cur, BLK), :]], axis=0)
    key = lax.broadcasted_iota(jnp.int32, (2 * BLK, 4 * BLK), 0)
    qry = lax.broadcasted_iota(jnp.int32, (2 * BLK, 4 * BLK), 1) & (BLK - 1)
    in_prev = jnp.logical_and(jnp.logical_and(key < BLK, key > qry), i > 0)
    mask = jnp.logical_or(in_prev, jnp.logical_and(key >= BLK, key - BLK <= qry))
    return band, mask, cur, prev


def _attn_scores(band, mask, qs, s_ref, g):
    st = jnp.where(mask, _mm_nt(band[:, g * 64:(g + 1) * 64], qs) * SCALE, NEG)
    lane = lax.broadcasted_iota(jnp.int32, (1, 4 * BLK), 1)
    sv = jnp.where(lane < BLK, s_ref[0, 4 * g],
                   jnp.where(lane < 2 * BLK, s_ref[0, 4 * g + 1], jnp.where(lane < 3 * BLK, s_ref[0, 4 * g + 2], s_ref[0, 4 * g + 3])))
    m = jnp.maximum(jnp.max(st, axis=0, keepdims=True), sv)
    p = jnp.exp(st - m)
    ps = jnp.exp(sv - m)
    return p, ps, jnp.sum(p, axis=0, keepdims=True) + ps


def _pos():
    return lax.axis_index("x"), lax.axis_index("y"), lax.axis_index("c")


def _other_chips(x, y):
    return [(1 - x, y), (x, 1 - y), (1 - x, 1 - y)]


def _gather_steps(w_ref, gw_ref, send_sems, recv_sems, local_sem):
    x, y, c = _pos()
    me = 2 * x + y
    chips = _other_chips(x, y)
    half = w_ref.shape[0] // 2
    mine = pl.ds(pl.multiple_of(c * half, 16), half)
    theirs = pl.ds(pl.multiple_of((1 - c) * half, 16), half)
    loc = pltpu.make_async_copy(w_ref, gw_ref.at[me], local_sem)

    def copy(k, src, dst, to):
        return pltpu.make_async_remote_copy(src_ref=src, dst_ref=dst, send_sem=send_sems.at[k], recv_sem=recv_sems.at[k],
                                            device_id=to, device_id_type=MESH)

    def out(k):
        px, py = chips[k]
        return copy(k, w_ref.at[mine], gw_ref.at[me, mine], (px, py, c))

    def fwd(k, rows):
        px, py = chips[k]
        return copy(3 + k, gw_ref.at[2 * px + py, rows], gw_ref.at[2 * px + py, rows], (x, y, 1 - c))

    def start():
        loc.start()
        for k in range(3):
            out(k).start()

    def forward():
        for k in range(3):
            px, py = chips[k]
            copy(k, w_ref.at[mine], gw_ref.at[2 * px + py, mine], (px, py, c)).wait_recv()
            fwd(k, mine).start()

    def finish():
        for k in range(3):
            fwd(k, theirs).wait_recv()
        for k in range(3):
            out(k).wait_send()
            fwd(k, mine).wait_send()
        loc.wait()

    return start, forward, finish


GATHER_SCRATCH = [pltpu.SemaphoreType.DMA((6,)), pltpu.SemaphoreType.DMA((6,)), pltpu.SemaphoreType.DMA]


class _Exchange:
    def __init__(self, args, out_shape, scratch, make):
        self.args, self.out_shape, self.scratch, self.make = list(args), list(out_shape), list(scratch), make


def _join_exchanges(a, b):
    na, nao, nas = len(a.args), len(a.out_shape), len(a.scratch)

    def make(ins, outs, sems):
        steps_a = a.make(ins[:na], outs[:nao], sems[:nas])
        steps_b = b.make(ins[na:], outs[nao:], sems[nas:])

        def both(f, g):
            def run():
                f()
                g()
            return run

        return tuple(both(f, g) for f, g in zip(steps_a, steps_b))

    return _Exchange(a.args + b.args, a.out_shape + b.out_shape, a.scratch + b.scratch, make)


def _gather_exchange(wsrc):
    return _Exchange([wsrc], [jax.ShapeDtypeStruct((4,) + wsrc.shape, wsrc.dtype)], GATHER_SCRATCH,
                     lambda ins, outs, sems: _gather_steps(ins[0], outs[0], *sems))


def _launch(body, name, grid, in_specs, out_specs, out_shape, scratch, args, exchange=None, prefetch=0):
    def call(fn, fn_name, ins, outs, shapes, scr, operands, effects):
        spec = pltpu.PrefetchScalarGridSpec(num_scalar_prefetch=prefetch, grid=grid, in_specs=ins, out_specs=outs,
                                            scratch_shapes=scr)
        return pl.pallas_call(fn, name=fn_name, grid_spec=spec, out_shape=shapes,
                              compiler_params=_params(has_side_effects=effects))(*operands)

    if exchange is None:
        return call(body, name, list(in_specs), list(out_specs), list(out_shape), list(scratch), args, False)
    n_in, n_out, ei, eo, ns = len(in_specs), len(out_specs), len(exchange.args), len(exchange.out_shape), len(exchange.scratch)
    nsteps = 1
    for g in grid:
        nsteps *= g

    def wrapped(*refs):
        scalars, refs = refs[:prefetch], refs[prefetch:]
        ins, xin = refs[:n_in], refs[n_in:n_in + ei]
        outs, xout = refs[n_in + ei:n_in + ei + n_out], refs[n_in + ei + n_out:n_in + ei + n_out + eo]
        rest = refs[n_in + ei + n_out + eo:]
        own, sems = rest[:len(rest) - ns], rest[len(rest) - ns:]
        start, forward, finish = exchange.make(xin, xout, sems)
        i = pl.program_id(0)
        for d in range(1, len(grid)):
            i = i * grid[d] + pl.program_id(d)
        pl.when(i == 0)(start)
        body(*scalars, *ins, *outs, *own)
        pl.when(i == max(nsteps - 3, 0))(forward)
        pl.when(i == nsteps - 1)(finish)

    anyspec = pl.BlockSpec(memory_space=pl.ANY)
    return call(wrapped, name + "_x", list(in_specs) + [anyspec] * ei, list(out_specs) + [anyspec] * eo,
                list(out_shape) + exchange.out_shape, list(scratch) + exchange.scratch, (*args, *exchange.args), True)


def _attn_fwd(q, kv, sinks, exchange=None):
    T = kv.shape[0]
    nblk = min(ATTN_BLOCKS, T // BLK)

    def body(q_ref, kv_ref, s_ref, o_ref):
        for b in range(nblk):
            rows = slice(b * BLK, (b + 1) * BLK)
            band, mask, _, _ = _attn_band(kv_ref, nblk * pl.program_id(0) + b)
            for g in range(2):
                qs = q_ref[4 * g:4 * g + 4, rows, :].reshape(4 * BLK, HEAD_DIM)
                p, _, den = _attn_scores(band, mask, qs, s_ref, g)
                ot = _mm_tn(band[:, 128:256], p) * (1.0 / den)
                for hh in range(4):
                    o = ot[:, hh * BLK:(hh + 1) * BLK].T
                    o_ref[rows, (4 * g + hh) * 64:(4 * g + hh + 1) * 64] = o[:, g * 64:(g + 1) * 64].astype(BF16)

    tq = nblk * BLK
    return _launch(body, "attn_fwd", (T // tq,), [_heads(tq), _full((T, 256)), pl.BlockSpec(memory_space=pltpu.SMEM)],
                   [_rows(tq, 512)], [jax.ShapeDtypeStruct((T, 512), BF16)], [], (q, kv, sinks), exchange)


def _attn_bwd(q, kv, do, sinks, exchange=None):
    T = kv.shape[0]
    nblk = min(ATTN_BLOCKS, T // BLK)

    def body(q_ref, kv_ref, do_ref, s_ref, dq_ref, dkv_ref, ds_ref):
        @pl.when(pl.program_id(0) == 0)
        def _():
            ds_ref[...] = jnp.zeros_like(ds_ref)

        for b in range(nblk):
            rows = slice(b * BLK, (b + 1) * BLK)
            band, mask, cur, prev = _attn_band(kv_ref, nblk * pl.program_id(0) + b)
            for g in range(2):
                qs = q_ref[4 * g:4 * g + 4, rows, :].reshape(4 * BLK, HEAD_DIM)
                dos = do_ref[4 * g:4 * g + 4, rows, :].reshape(4 * BLK, HEAD_DIM)
                p, ps, den = _attn_scores(band, mask, qs, s_ref, g)
                inv = 1.0 / den
                p = p * inv
                dpt = _mm_nt(band[:, 128 + g * 64:192 + g * 64], dos)
                delta = jnp.sum(p * dpt, axis=0, keepdims=True)
                dst = p * (dpt - delta)
                dsv = -(ps * inv) * delta
                for hh in range(4):
                    dsink = jnp.sum(dsv[:, hh * BLK:(hh + 1) * BLK], axis=1, keepdims=True)
                    ds_ref[4 * g + hh:4 * g + hh + 1, :] += jnp.broadcast_to(dsink, (1, 128))
                dqt = _mm_tn(band[:, 0:128], dst) * SCALE
                for hh in range(4):
                    dqh = dqt[:, hh * BLK:(hh + 1) * BLK].T
                    dq_ref[rows, (4 * g + hh) * 64:(4 * g + hh + 1) * 64] = dqh[:, g * 64:(g + 1) * 64].astype(BF16)
                dk = _mm(dst, qs) * SCALE
                dv = _mm(p, dos)
                dkv_ref[pl.ds(cur, BLK), g * 64:(g + 1) * 64] = dk[BLK:2 * BLK]
                dkv_ref[pl.ds(cur, BLK), 128 + g * 64:192 + g * 64] = dv[BLK:2 * BLK]
                dkv_ref[pl.ds(prev, BLK), g * 64:(g + 1) * 64] += dk[0:BLK]
                dkv_ref[pl.ds(prev, BLK), 128 + g * 64:192 + g * 64] += dv[0:BLK]

    tq = nblk * BLK
    return _launch(body, "attn_bwd", (T // tq,),
                   [_heads(tq), _full((T, 256)), _heads(tq), pl.BlockSpec(memory_space=pltpu.SMEM)],
                   [_rows(tq, 512), _full((T, 256)), _full((8, 128))],
                   [jax.ShapeDtypeStruct((T, 512), BF16), jax.ShapeDtypeStruct((T, 256), F32),
                    jax.ShapeDtypeStruct((8, 128), F32)], [], (q, kv, do, sinks), exchange)


def _rows8(tm, cols):
    return lax.broadcasted_iota(jnp.int32, (tm, cols), 0) & 7


def _lru_gates(xc, wa, ba, wx, bx, lam):
    r = _sigmoid(_mm(xc, wa) + ba)
    ii = _sigmoid(_mm(xc, wx) + bx)
    sp = _softplus(-lam)
    la = -LRU_C * r * sp
    a = jnp.exp(la)
    m = jnp.sqrt(-jnp.tanh(la) * (a * a + 1.0))
    return r, ii, sp, a, m


def _rnn_fwd(xr, gr, cw, cb, wa, ba, wx, bx, lam, exchange=None):
    T = xr.shape[0]
    tm = 512
    C = D_RNN

    def body(xr_ref, gr_ref, cw_ref, cb_ref, wa_ref, ba_ref, wx_ref, bx_ref, lam_ref,
             xc_ref, h_ref, rec_ref, ext, a_s, b_s, carry):
        i = pl.program_id(0)

        @pl.when(i == 0)
        def _():
            ext[...] = jnp.zeros((8, C), F32)
            carry[...] = jnp.zeros((8, C), F32)

        xr = xr_ref[...]
        edge = ext[...]
        xc = cb_ref[...] + cw_ref[3:4, :] * xr
        for k in range(3):
            xc = xc + cw_ref[k:k + 1, :] * _shift_rows(xr, 3 - k, edge)
        ext[...] = xr[tm - 8:tm, :]
        xc_ref[...] = xc
        _, ii, _, a, m = _lru_gates(xc, wa_ref[...], ba_ref[...], wx_ref[...], bx_ref[...], lam_ref[...])
        b = m * ii * xc
        r8 = _rows8(tm, C)
        for d in (1, 2, 4):
            ok = r8 >= d
            a_sh = jnp.where(ok, pltpu.roll(a, d, 0), 1.0)
            b_sh = jnp.where(ok, pltpu.roll(b, d, 0), 0.0)
            b = a * b_sh + b
            a = a * a_sh
        a_s[...] = a
        b_s[...] = b

        def step(g, hin):
            s = pl.multiple_of(g * 8, 8)
            hg = a_s[pl.ds(s, 8), :] * hin + b_s[pl.ds(s, 8), :]
            h_ref[pl.ds(s, 8), :] = hg
            return jnp.broadcast_to(hg[7:8, :], (8, C))

        carry[...] = lax.fori_loop(0, tm // 8, step, carry[...], unroll=4)
        ge, _ = _gelu(gr_ref[...])
        rec_ref[...] = (h_ref[...] * ge).astype(BF16)

    vec = _full((1, C))
    in_specs = [_rows(tm, C), _rows(tm, C), _full((4, C)), vec, _full((C, C)), vec, _full((C, C)), vec, vec]
    out_specs = [_rows(tm, C), _rows(tm, C), _rows(tm, C)]
    out_shape = [jax.ShapeDtypeStruct((T, C), F32), jax.ShapeDtypeStruct((T, C), F32), jax.ShapeDtypeStruct((T, C), BF16)]
    scratch = [pltpu.VMEM((8, C), F32), pltpu.VMEM((tm, C), F32), pltpu.VMEM((tm, C), F32), pltpu.VMEM((8, C), F32)]
    return _launch(body, "rnn_fwd", (T // tm,), in_specs, out_specs, out_shape, scratch,
                   (xr, gr, cw, cb, wa, ba, wx, bx, lam), exchange)


def _rnn_bwd(drec, gr, h, xc, xr, cw, wa, ba, wx, bx, lam, exchange=None):
    T = xr.shape[0]
    tm = 512
    C = D_RNN
    nt = T // tm
    t8 = tm // 8

    def body(drec_ref, gr_ref, h_ref, hp_ref, xc_ref, xr_ref, cw_ref, wa_ref, ba_ref, wx_ref, bx_ref,
             lam_ref, dxr_ref, dgr_ref, dwa_ref, dwx_ref, dvec_ref, c_s, g_s, gout, ext, anext, gcarry):
        i = pl.program_id(0)
        j = nt - 1 - i

        @pl.when(i == 0)
        def _():
            dwa_ref[...] = jnp.zeros_like(dwa_ref)
            dwx_ref[...] = jnp.zeros_like(dwx_ref)
            dvec_ref[...] = jnp.zeros_like(dvec_ref)
            anext[...] = jnp.zeros((8, C), F32)
            gcarry[...] = jnp.zeros((8, C), F32)
            ext[...] = jnp.zeros((8, C), F32)

        xc = xc_ref[...]
        lam = lam_ref[...]
        r, ii, sp, a, m = _lru_gates(xc, wa_ref[...], ba_ref[...], wx_ref[...], bx_ref[...], lam)
        ge, dge = _gelu(gr_ref[...])
        drec = drec_ref[...]
        hh = h_ref[...]
        dgr_ref[...] = (drec * hh * dge).astype(BF16)
        dh = drec * ge
        rowi = lax.broadcasted_iota(jnp.int32, (tm, C), 0)
        c = jnp.where(rowi == tm - 1, jnp.broadcast_to(anext[0:1, :], (tm, C)), pltpu.roll(a, tm - 1, 0))
        anext[...] = a[0:8, :]
        r8 = rowi & 7
        gg = dh
        for d in (1, 2, 4):
            ok = r8 < 8 - d
            c_sh = jnp.where(ok, pltpu.roll(c, tm - d, 0), 1.0)
            g_sh = jnp.where(ok, pltpu.roll(gg, tm - d, 0), 0.0)
            gg = c * g_sh + gg
            c = c * c_sh
        c_s[...] = c
        g_s[...] = gg

        def step(k, gin):
            s = pl.multiple_of((t8 - 1 - k) * 8, 8)
            og = c_s[pl.ds(s, 8), :] * gin + g_s[pl.ds(s, 8), :]
            gout[pl.ds(s, 8), :] = og
            return jnp.broadcast_to(og[0:1, :], (8, C))

        gcarry[...] = lax.fori_loop(0, t8, step, gcarry[...], unroll=4)
        G = gout[...]
        hprev_row = jnp.where(j > 0, hp_ref[7:8, :], 0.0)
        hprev = jnp.where(rowi == 0, jnp.broadcast_to(hprev_row, (tm, C)), pltpu.roll(hh, 1, 0))
        da = G * hprev
        dm = G * ii * xc
        di = G * m * xc
        dxc = G * m * ii
        dla = da * a - dm * a * a / m
        dr = dla * (-LRU_C * sp)
        dsp = _colsum(dla * (-LRU_C * r))
        dlam = dsp * (-_sigmoid(-lam))
        dpr = dr * r * (1.0 - r)
        dpi = di * ii * (1.0 - ii)
        dxc = dxc + _mm_nt(dpr, wa_ref[...]) + _mm_nt(dpi, wx_ref[...])
        dwa_ref[...] += _mm_tn(xc, dpr)
        dwx_ref[...] += _mm_tn(xc, dpi)
        dvec_ref[0:1, :] += _colsum(dpr)
        dvec_ref[1:2, :] += _colsum(dpi)
        dvec_ref[2:3, :] += dlam
        dvec_ref[3:4, :] += _colsum(dxc)
        edge = ext[...]
        xr = xr_ref[...]
        dxr = cw_ref[3:4, :] * dxc
        dvec_ref[7:8, :] += _colsum(dxc * xr)
        for k in range(3):
            up = _shift_rows(dxc, k - 3, edge)
            dxr = dxr + cw_ref[k:k + 1, :] * up
            dvec_ref[4 + k:5 + k, :] += _colsum(up * xr)
        ext[...] = dxc[0:8, :]
        dxr_ref[...] = dxr.astype(BF16)

    rev = lambda i: nt - 1 - i
    prev8 = lambda i: jnp.maximum((nt - 1 - i) * t8 - 1, 0)
    vec = _full((1, C))
    return _launch(
        body, "rnn_bwd", (nt,),
        [_rows(tm, C, rev), _rows(tm, C, rev), _rows(tm, C, rev), _rows(8, C, prev8), _rows(tm, C, rev),
         _rows(tm, C, rev), _full((4, C)), _full((C, C)), vec, _full((C, C)), vec, vec],
        [_rows(tm, C, rev), _rows(tm, C, rev), _full((C, C)), _full((C, C)), _full((8, C))],
        [jax.ShapeDtypeStruct((T, C), BF16), jax.ShapeDtypeStruct((T, C), BF16),
         jax.ShapeDtypeStruct((C, C), F32), jax.ShapeDtypeStruct((C, C), F32), jax.ShapeDtypeStruct((8, C), F32)],
        [pltpu.VMEM((tm, C), F32), pltpu.VMEM((tm, C), F32), pltpu.VMEM((tm, C), F32),
         pltpu.VMEM((8, C), F32), pltpu.VMEM((8, C), F32), pltpu.VMEM((8, C), F32)],
        (drec, gr, h, h, xc, xr, cw, wa, ba, wx, bx, lam), exchange)


def _out_proj(att, rec, x, w_out, g1, b1):
    T = x.shape[0]
    tm = min(1024, T)

    def body(att_ref, rec_ref, x_ref, w_ref, g1_ref, b1_ref, z_ref, h_ref):
        mix = _mm(att_ref[...], w_ref[0:512, :]) + _mm(rec_ref[...], w_ref[512:1024, :])
        z1 = ALPHA * x_ref[...] + mix
        z_ref[...] = z1
        h1, _, _ = _ln(z1, g1_ref[...], b1_ref[...])
        h_ref[...] = h1.astype(MXU_DTYPE).astype(BF16)

    return pl.pallas_call(
        body, name="out_proj", grid=(T // tm,),
        in_specs=[_rows(tm, 512), _rows(tm, 512), _rows(tm, D), _full((D, D)), _full((1, D)), _full((1, D))],
        out_specs=[_rows(tm, D), _rows(tm, D)],
        out_shape=[jax.ShapeDtypeStruct((T, D), F32), jax.ShapeDtypeStruct((T, D), BF16)],
        compiler_params=_params(),
    )(att, rec, x, w_out, g1, b1)


NC = D_FF // FF_CHUNK


def _ffn_up(h1b, w_up_t, fcw, fcb, exchange=None):
    T = h1b.shape[0]
    tm = min(1024, T)
    CW = FF_CHUNK

    def body(h_ref, wg_ref, wv_ref, fcw_ref, fcb_ref, gate_ref, ge_ref, vd_ref, act_ref, before):
        i = pl.program_id(1)

        @pl.when(i == 0)
        def _():
            before[...] = jnp.zeros((8, CW), F32)

        hb = h_ref[...]
        gate = _mm_nt(hb, wg_ref[...])
        val = _mm_nt(hb, wv_ref[...])
        gate_ref[...] = gate.astype(BF16)
        edge = before[...]
        gc = (fcb_ref[...] + fcw_ref[0:1, :] * _shift_rows(gate, 2, edge) + fcw_ref[1:2, :] * _shift_rows(gate, 1, edge)
              + fcw_ref[2:3, :] * gate)
        before[...] = gate[tm - 8:tm, :]
        ge, dge = _gelu(gc)
        ge_ref[...] = ge.astype(BF16)
        vd_ref[...] = (val * dge).astype(BF16)
        act_ref[...] = (ge * val).astype(BF16)

    chunk = pl.BlockSpec((None, tm, CW), lambda c, i: (c, i, 0))
    return _launch(
        body, "ffn_up", (NC, T // tm),
        [pl.BlockSpec((tm, D), lambda c, i: (i, 0)), pl.BlockSpec((CW, D), lambda c, i: (c, 0)),
         pl.BlockSpec((CW, D), lambda c, i: (NC + c, 0)), pl.BlockSpec((None, 3, CW), lambda c, i: (c, 0, 0)),
         pl.BlockSpec((None, 1, CW), lambda c, i: (c, 0, 0))],
        [chunk] * 4, [jax.ShapeDtypeStruct((NC, T, CW), BF16)] * 4, [pltpu.VMEM((8, CW), F32)],
        (h1b, w_up_t, w_up_t, fcw, fcb), exchange)


def _ffn_down(act, z1, p, tgt, w_down, w_g, w_p_t, g1, b1, g2, b2, bg):
    T = z1.shape[0]
    tm = 512

    def body(act_ref, z_ref, p_ref, t_ref, wdn_hbm, wg_hbm, wp_hbm, g1_ref, b1_ref, g2_ref, b2_ref, bg_ref,
             dz2_ref, dz2b_ref, dpre_ref, dpp_ref, vec_ref, wdn, wg, wp):
        @pl.when(pl.program_id(0) == 0)
        def _():
            pltpu.sync_copy(wdn_hbm, wdn)
            pltpu.sync_copy(wg_hbm, wg)
            pltpu.sync_copy(wp_hbm, wp)
            vec_ref[...] = jnp.zeros_like(vec_ref)

        g2v = g2_ref[...]
        for r in (slice(0, tm // 2), slice(tm // 2, tm)):
            h1, _, _ = _ln(z_ref[r, :], g1_ref[...], b1_ref[...])
            h1b = h1.astype(MXU_DTYPE)
            ffn = _mm(act_ref[0, r, :], wdn[0:FF_CHUNK, :])
            for c in range(1, NC):
                ffn = ffn + _mm(act_ref[c, r, :], wdn[c * FF_CHUNK:(c + 1) * FF_CHUNK, :])
            sg = _sigmoid(_mm(h1b, wg[...]) + bg_ref[...])
            pp = _mm_nt(p_ref[r, :], wp[...])
            z2 = ALPHA * h1 + ffn + sg * pp
            y, xh2, rstd2 = _ln(z2, g2v, b2_ref[...])
            diff = y - t_ref[r, :]
            dy = diff * (1.0 / D)
            dz2 = _ln_bwd(dy, xh2, rstd2, g2v)
            dpre = dz2 * pp * sg * (1.0 - sg)
            dz2_ref[r, :] = dz2
            dz2b_ref[r, :] = dz2.astype(BF16)
            dpre_ref[r, :] = dpre.astype(BF16)
            dpp_ref[r, :] = (dz2 * sg).astype(BF16)
            loss = 0.5 * jnp.sum(jnp.sum(diff * diff, axis=1, keepdims=True), axis=0, keepdims=True) * (1.0 / D)
            vec_ref[0:1, :] += jnp.broadcast_to(loss, (1, D))
            vec_ref[1:2, :] += _colsum(dy * xh2)
            vec_ref[2:3, :] += _colsum(dy)
            vec_ref[3:4, :] += _colsum(dpre)

    anyspec = pl.BlockSpec(memory_space=pl.ANY)
    vec = _full((1, D))
    return pl.pallas_call(
        body, name="ffn_down", grid=(T // tm,),
        in_specs=[pl.BlockSpec((NC, tm, FF_CHUNK), lambda i: (0, i, 0)), _rows(tm, D), _rows(tm, PLE), _rows(tm, D),
                  anyspec, anyspec, anyspec] + [vec] * 5,
        out_specs=[_rows(tm, D)] * 4 + [_full((8, D))],
        out_shape=[jax.ShapeDtypeStruct((T, D), F32)] + [jax.ShapeDtypeStruct((T, D), BF16)] * 3
                  + [jax.ShapeDtypeStruct((8, D), F32)],
        scratch_shapes=[pltpu.VMEM((D_FF, D), MXU_DTYPE), pltpu.VMEM((D, D), MXU_DTYPE), pltpu.VMEM((D, PLE), MXU_DTYPE)],
        compiler_params=_params(),
    )(act, z1, p, tgt, w_down, w_g, w_p_t, g1, b1, g2, b2, bg)


def _ffn_bwd(dz2b, gate, ge, vd, w_down, fcw):
    T = dz2b.shape[0]
    tm = min(1024, T)
    CW = FF_CHUNK
    nt = T // tm

    def body(dz_ref, wdn_ref, gate_ref, ge_ref, vd_ref, fcw_ref, dup_ref, dfc_ref, after):
        i = pl.program_id(1)

        @pl.when(i == 0)
        def _():
            after[...] = jnp.zeros((8, CW), F32)
            dfc_ref[...] = jnp.zeros_like(dfc_ref)

        gate = gate_ref[...].astype(F32)
        dact = _mm_nt(dz_ref[...], wdn_ref[...])
        dgc = dact * vd_ref[...].astype(F32)
        edge = after[...]
        dgc1 = _shift_rows(dgc, -1, edge)
        dgc2 = _shift_rows(dgc, -2, edge)
        after[...] = dgc[0:8, :]
        dup_ref[0] = (fcw_ref[2:3, :] * dgc + fcw_ref[1:2, :] * dgc1 + fcw_ref[0:1, :] * dgc2).astype(BF16)
        dup_ref[1] = (dact * ge_ref[...].astype(F32)).astype(BF16)
        dfc_ref[0:1, :] += _colsum(dgc2 * gate)
        dfc_ref[1:2, :] += _colsum(dgc1 * gate)
        dfc_ref[2:3, :] += _colsum(dgc * gate)
        dfc_ref[3:4, :] += _colsum(dgc)

    rev = lambda c, i: (c, nt - 1 - i, 0)
    chunk = pl.BlockSpec((None, tm, CW), rev)
    return pl.pallas_call(
        body, name="ffn_bwd", grid=(NC, nt),
        in_specs=[pl.BlockSpec((tm, D), lambda c, i: (nt - 1 - i, 0)), pl.BlockSpec((CW, D), lambda c, i: (c, 0)),
                  chunk, chunk, chunk, pl.BlockSpec((None, 3, CW), lambda c, i: (c, 0, 0))],
        out_specs=[pl.BlockSpec((None, 2, tm, CW), lambda c, i: (c, 0, nt - 1 - i, 0)),
                   pl.BlockSpec((None, 8, CW), lambda c, i: (c, 0, 0))],
        out_shape=[jax.ShapeDtypeStruct((NC, 2, T, CW), BF16), jax.ShapeDtypeStruct((NC, 8, CW), F32)],
        scratch_shapes=[pltpu.VMEM((8, CW), F32)],
        compiler_params=_params(),
    )(dz2b, w_down, gate, ge, vd, fcw)


def _ffn_dh1(dup, dz2, dpre, z1, w_up_t, w_g, g1, b1):
    T = z1.shape[0]
    tm = 512

    def body(dup_ref, dz2_ref, dpre_ref, z_ref, wup_hbm, wg_hbm, g1_ref, b1_ref, dz1_ref, vec_ref, wup, wg):
        @pl.when(pl.program_id(0) == 0)
        def _():
            pltpu.sync_copy(wup_hbm, wup)
            pltpu.sync_copy(wg_hbm, wg)
            vec_ref[...] = jnp.zeros_like(vec_ref)

        g1v = g1_ref[...]
        _, xh1, rstd1 = _ln(z_ref[...], g1v, b1_ref[...])
        dh1 = ALPHA * dz2_ref[...] + _mm_nt(dpre_ref[...], wg[...])
        for c in range(NC):
            for s in range(2):
                r0 = s * D_FF + c * FF_CHUNK
                dh1 = dh1 + _mm(dup_ref[c, s], wup[r0:r0 + FF_CHUNK, :])
        dz1_ref[...] = _ln_bwd(dh1, xh1, rstd1, g1v)
        vec_ref[0:1, :] += _colsum(dh1 * xh1)
        vec_ref[1:2, :] += _colsum(dh1)

    anyspec = pl.BlockSpec(memory_space=pl.ANY)
    vec = _full((1, D))
    return pl.pallas_call(
        body, name="ffn_dh1", grid=(T // tm,),
        in_specs=[pl.BlockSpec((NC, 2, tm, FF_CHUNK), lambda i: (0, 0, i, 0)), _rows(tm, D), _rows(tm, D), _rows(tm, D),
                  anyspec, anyspec, vec, vec],
        out_specs=[_rows(tm, D), _full((8, D))],
        out_shape=[jax.ShapeDtypeStruct((T, D), F32), jax.ShapeDtypeStruct((8, D), F32)],
        scratch_shapes=[pltpu.VMEM((2 * D_FF, D), MXU_DTYPE), pltpu.VMEM((D, D), MXU_DTYPE)],
        compiler_params=_params(),
    )(dup, dz2, dpre, z1, w_up_t, w_g, g1, b1)


def _out_proj_bwd(dz1, w_out, exchange=None):
    T = dz1.shape[0]
    tm = min(1024, T)

    def body(dz_ref, w_ref, datt_ref, drec_ref):
        dzb = dz_ref[...].astype(MXU_DTYPE)
        datt = _mm_nt(dzb, w_ref[0:512, :])
        for h in range(HEADS):
            datt_ref[h] = datt[:, h * 64:(h + 1) * 64].astype(BF16)
        drec_ref[...] = _mm_nt(dzb, w_ref[512:1024, :])

    return _launch(body, "out_proj_bwd", (T // tm,), [_rows(tm, D), _full((D, D))], [_heads(tm), _rows(tm, 512)],
                   [jax.ShapeDtypeStruct((HEADS, T, 64), BF16), jax.ShapeDtypeStruct((T, 512), F32)], [],
                   (dz1, w_out), exchange)


def _in_proj_bwd(dq, dkv, dxr, dgr, dz1, w_in_t, exchange=None):
    T = dz1.shape[0]
    tm = 512
    W = D_IN // 4

    def body(dq_ref, dkv_ref, dxr_ref, dgr_ref, dz_ref, w_ref, dx_ref, du_ref):
        dkv = dkv_ref[...]
        dx_ref[...] = (ALPHA * dz_ref[...] + _mm(dq_ref[...], w_ref[0:512, :]) + _mm(dkv, w_ref[512:768, :])
                       + _mm(dxr_ref[...], w_ref[768:1280, :]) + _mm(dgr_ref[...], w_ref[1280:1792, :]))
        dq, dxr, dgr = dq_ref[...].astype(F32), dxr_ref[...].astype(F32), dgr_ref[...].astype(F32)
        du_ref[0] = dq[:, 0:W].astype(BF16)
        du_ref[1, :, 0:64] = dq[:, W:512].astype(BF16)
        du_ref[1, :, 64:320] = dkv.astype(BF16)
        du_ref[1, :, 320:W] = dxr[:, 0:128].astype(BF16)
        du_ref[2, :, 0:384] = dxr[:, 128:512].astype(BF16)
        du_ref[2, :, 384:W] = dgr[:, 0:64].astype(BF16)
        du_ref[3] = dgr[:, 64:512].astype(BF16)

    return _launch(body, "in_proj_bwd", (T // tm,),
                   [_rows(tm, 512), _rows(tm, 256), _rows(tm, 512), _rows(tm, 512), _rows(tm, D), _full((D_IN, D))],
                   [_rows(tm, D), pl.BlockSpec((4, tm, W), lambda i: (0, i, 0))],
                   [jax.ShapeDtypeStruct((T, D), F32), jax.ShapeDtypeStruct((4, T, W), BF16)], [],
                   (dq, dkv, dxr, dgr, dz1, w_in_t), exchange)


def _accumulate_tn(a_ref, b_ref, o_ref):
    @pl.when(pl.program_id(1) == 0)
    def _():
        o_ref[...] = jnp.zeros_like(o_ref)

    o_ref[...] += _mm_tn(a_ref[...], b_ref[...])


def _weight_grad_cols(a, b, name, n_blocks, b_spec, out_shape, out_spec, exchange=None):
    T, M = a.shape
    bt = min(DW_TOKENS, T)
    return _launch(functools.partial(_accumulate_tn), name, (n_blocks, T // bt),
                   [pl.BlockSpec((bt, M), lambda m, k: (k, 0)), b_spec(bt)], [out_spec],
                   [jax.ShapeDtypeStruct(out_shape, F32)], [], (a, b), exchange)


def _dw_out(att, rec, dz1):
    T = dz1.shape[0]
    bt = min(DW_TOKENS // 2, T)

    def body(att_ref, rec_ref, dz_ref, o_ref):
        @pl.when(pl.program_id(0) == 0)
        def _():
            o_ref[...] = jnp.zeros_like(o_ref)

        dz = dz_ref[...].astype(MXU_DTYPE)
        o_ref[0:512, :] += _mm_tn(att_ref[...], dz)
        o_ref[512:1024, :] += _mm_tn(rec_ref[...], dz)

    return pl.pallas_call(
        body, name="dw_out", grid=(T // bt,), in_specs=[_rows(bt, 512), _rows(bt, 512), _rows(bt, D)],
        out_specs=_full((D, D)), out_shape=jax.ShapeDtypeStruct((D, D), F32), compiler_params=_params())(att, rec, dz1)


def _weight_grad(a, b, bm, name, exchange=None):
    bt = min(DW_TOKENS // 2 if b.dtype == F32 else DW_TOKENS, b.shape[0])
    if a.ndim == 3:
        assert a.shape[2] == bm
        T, M = a.shape[1], a.shape[0] * bm
        a_spec = pl.BlockSpec((None, bt, bm), lambda m, k: (m, k, 0))
    else:
        T, M = a.shape
        a_spec = pl.BlockSpec((bt, bm), lambda m, k: (k, m))
    N = b.shape[1]
    nk = T // bt

    out = _launch(functools.partial(_accumulate_tn), name, (M // bm, nk),
                  [a_spec, pl.BlockSpec((bt, N), lambda m, k: (k, 0))], [pl.BlockSpec((bm, N), lambda m, k: (m, 0))],
                  [jax.ShapeDtypeStruct((M, N), F32)], [], (a, b), exchange)
    return out[0] if exchange is None else out


def _adamw(w, g, m, v, name):
    R, C = w.shape
    tr = R // 8 if R % 64 == 0 else R
    c1 = 1.0 / (1.0 - ADAM_B1 ** ADAM_STEP)
    c2 = 1.0 / (1.0 - ADAM_B2 ** ADAM_STEP)

    def body(w_ref, g_ref, m_ref, v_ref, d_ref, nm_ref, nv_ref):
        g = g_ref[...]
        nm = ADAM_B1 * m_ref[...] + (1.0 - ADAM_B1) * g
        nv = ADAM_B2 * v_ref[...] + (1.0 - ADAM_B2) * g * g
        nm_ref[...] = nm
        nv_ref[...] = nv
        d_ref[...] = -ADAM_LR * ((nm * c1) / (jnp.sqrt(nv * c2) + ADAM_EPS) + ADAM_WD * w_ref[...])

    spec = pl.BlockSpec((tr, C), lambda i: (i, 0))
    return pl.pallas_call(
        body, name=name, grid=(R // tr,),
        in_specs=[spec] * 4, out_specs=[spec] * 3,
        out_shape=[jax.ShapeDtypeStruct((R, C), F32)] * 3,
        compiler_params=_params(),
    )(w, g, m, v)


def _adamw_halves(ws, mines, sibs, ms, vs, c, name, exchange=None):
    n, nb = len(ws), 4
    c1 = 1.0 / (1.0 - ADAM_B1 ** ADAM_STEP)
    c2 = 1.0 / (1.0 - ADAM_B2 ** ADAM_STEP)

    def body(c_ref, *refs):
        own = (pl.program_id(0) // nb) == c_ref[0]
        for i in range(n):
            w_ref, a_ref, b_ref, m_ref, v_ref = refs[5 * i:5 * i + 5]
            g_ref, d_ref, nm_ref, nv_ref = refs[5 * n + 4 * i:5 * n + 4 * i + 4]
            g = jnp.where(own, a_ref[...], b_ref[...])
            nm = ADAM_B1 * m_ref[...] + (1.0 - ADAM_B1) * g
            nv = ADAM_B2 * v_ref[...] + (1.0 - ADAM_B2) * g * g
            g_ref[...] = g
            nm_ref[...] = nm
            nv_ref[...] = nv
            d_ref[...] = -ADAM_LR * ((nm * c1) / (jnp.sqrt(nv * c2) + ADAM_EPS) + ADAM_WD * w_ref[...])

    in_specs, out_specs, out_shape, args = [], [], [], []
    for w, a, b, m, v in zip(ws, mines, sibs, ms, vs):
        R, C = w.shape
        tr = R // (2 * nb)
        assert tr % 8 == 0 and a.shape == (R // 2, C)
        full = pl.BlockSpec((tr, C), lambda i, c_ref: (i, 0))
        mine_spec = pl.BlockSpec((tr, C), lambda i, c_ref: (jnp.where(i // nb == c_ref[0], i % nb, nb - 1), 0))
        sib_spec = pl.BlockSpec((tr, C), lambda i, c_ref: (jnp.where(i // nb == c_ref[0], nb - 1, i % nb), 0))
        in_specs += [full, mine_spec, sib_spec, full, full]
        out_specs += [full] * 4
        out_shape += [jax.ShapeDtypeStruct((R, C), F32)] * 4
        args += [w, a, b, m, v]
    out = _launch(body, name, (2 * nb,), in_specs, out_specs, out_shape, [], (c, *args), exchange, prefetch=1)
    return [tuple(out[4 * i:4 * i + 4]) for i in range(n)], list(out[4 * n:])


def _add4(fs, name):
    n = len(fs)

    def body(*refs):
        for a_ref, o_ref in zip(refs[:n], refs[n:]):
            o_ref[...] = ((a_ref[0].astype(F32) + a_ref[1].astype(F32)) + a_ref[2].astype(F32)) + a_ref[3].astype(F32)

    for f in fs:
        assert (f.shape[1] // 2) % 16 == 0
    return pl.pallas_call(
        body, name=name, grid=(2,),
        in_specs=[pl.BlockSpec((4, f.shape[1] // 2, f.shape[2]), lambda i: (0, i, 0)) for f in fs],
        out_specs=[pl.BlockSpec((f.shape[1] // 2, f.shape[2]), lambda i: (i, 0)) for f in fs],
        out_shape=[jax.ShapeDtypeStruct(f.shape[1:], F32) for f in fs], compiler_params=_params())(*fs)


def _gather_first(wsrc, cpack):
    def body(w_ref, c_ref, gw_ref, gc_ref, send_sems, recv_sems, local_sem, csend, crecv, clocal):
        x, y, c = _pos()
        me = 2 * x + y
        chips = _other_chips(x, y)
        start, forward, finish = _gather_steps(w_ref, gw_ref, send_sems, recv_sems, local_sem)
        start()
        loc = pltpu.make_async_copy(c_ref, gc_ref.at[me], clocal)
        loc.start()

        def conv_copy(k, slot):
            px, py = chips[k]
            return pltpu.make_async_remote_copy(src_ref=c_ref, dst_ref=gc_ref.at[slot], send_sem=csend.at[k],
                                                recv_sem=crecv.at[k], device_id=(px, py, c), device_id_type=MESH)

        for k in range(3):
            conv_copy(k, me).start()
        forward()
        finish()
        for k, (px, py) in enumerate(chips):
            conv_copy(k, 2 * px + py).wait_recv()
        for k in range(3):
            conv_copy(k, me).wait_send()
        loc.wait()

    anyspec = pl.BlockSpec(memory_space=pl.ANY)
    return pl.pallas_call(
        body, name="gather_first",
        in_specs=[anyspec, anyspec], out_specs=[anyspec, anyspec],
        out_shape=[jax.ShapeDtypeStruct((4,) + wsrc.shape, wsrc.dtype), jax.ShapeDtypeStruct((4,) + cpack.shape, cpack.dtype)],
        scratch_shapes=GATHER_SCRATCH + [pltpu.SemaphoreType.DMA((3,)), pltpu.SemaphoreType.DMA((3,)), pltpu.SemaphoreType.DMA],
        compiler_params=_params(has_side_effects=True),
    )(wsrc, cpack)


def _all_devices_exchange(s):
    def make(ins, outs, sems):
        s_ref, o_ref = ins[0], outs[0]
        send_sems, recv_sems, local_sem = sems
        x, y, c = _pos()
        me = 4 * x + 2 * y + c
        loc = pltpu.make_async_copy(s_ref, o_ref.at[me], local_sem)

        def copy(k, slot):
            peer = (x ^ (k >> 2), y ^ ((k >> 1) & 1), c ^ (k & 1))
            return pltpu.make_async_remote_copy(src_ref=s_ref, dst_ref=o_ref.at[slot], send_sem=send_sems.at[k - 1],
                                                recv_sem=recv_sems.at[k - 1], device_id=peer, device_id_type=MESH)

        def start():
            loc.start()
            for k in range(1, 8):
                copy(k, me).start()

        def finish():
            for k in range(1, 8):
                copy(k, 4 * (x ^ (k >> 2)) + 2 * (y ^ ((k >> 1) & 1)) + (c ^ (k & 1))).wait_recv()
            for k in range(1, 8):
                copy(k, me).wait_send()
            loc.wait()

        return start, lambda: None, finish

    return _Exchange([s], [jax.ShapeDtypeStruct((8,) + s.shape, s.dtype)],
                     [pltpu.SemaphoreType.DMA((7,)), pltpu.SemaphoreType.DMA((7,)), pltpu.SemaphoreType.DMA], make)


def _sum_devices(a):
    def body(a_ref, o_ref):
        acc = a_ref[0]
        for d in range(1, 8):
            acc = acc + a_ref[d]
        o_ref[...] = acc

    vm = pl.BlockSpec(memory_space=pltpu.VMEM)
    return pl.pallas_call(body, name="sum_devices", in_specs=[vm], out_specs=vm,
                          out_shape=jax.ShapeDtypeStruct(a.shape[1:], F32), compiler_params=_params())(a)


def _swap_exchange(gs):
    n = len(gs)

    def make(ins, outs, sems):
        x, y, c = _pos()
        cps = []
        for i in range(n):
            half = gs[i].shape[1] // 2
            rows = pl.ds(pl.multiple_of((1 - c) * half, 8), half)
            cps.append(pltpu.make_async_remote_copy(src_ref=ins[i].at[:, rows, :], dst_ref=outs[i], send_sem=sems[0].at[i],
                                                    recv_sem=sems[1].at[i], device_id=(x, y, 1 - c), device_id_type=MESH))

        def start():
            for cp in cps:
                cp.start()

        def finish():
            for cp in cps:
                cp.wait()

        return start, lambda: None, finish

    return _Exchange(gs, [jax.ShapeDtypeStruct((4, g.shape[1] // 2, g.shape[2]), g.dtype) for g in gs],
                     [pltpu.SemaphoreType.DMA((n,)), pltpu.SemaphoreType.DMA((n,))], make)


def _scatter_exchange(ss):
    n = len(ss)

    def make(ins, outs, sems):
        send_sems, recv_sems, local_sems = sems
        x, y, c = _pos()
        me = 2 * x + y
        chips = _other_chips(x, y)
        locs = [pltpu.make_async_copy(ins[i].at[me], outs[i].at[me], local_sems.at[i]) for i in range(n)]

        def copy(i, k, src_slot, dst_slot):
            px, py = chips[k]
            return pltpu.make_async_remote_copy(src_ref=ins[i].at[src_slot], dst_ref=outs[i].at[dst_slot],
                                                send_sem=send_sems.at[3 * i + k], recv_sem=recv_sems.at[3 * i + k],
                                                device_id=(px, py, c), device_id_type=MESH)

        def start():
            for i in range(n):
                locs[i].start()
                for k, (px, py) in enumerate(chips):
                    copy(i, k, 2 * px + py, me).start()

        def finish():
            for i in range(n):
                for k, (px, py) in enumerate(chips):
                    copy(i, k, me, 2 * px + py).wait_recv()
            for i in range(n):
                for k, (px, py) in enumerate(chips):
                    copy(i, k, 2 * px + py, me).wait_send()
                locs[i].wait()

        return start, lambda: None, finish

    return _Exchange(ss, [jax.ShapeDtypeStruct(s.shape, s.dtype) for s in ss],
                     [pltpu.SemaphoreType.DMA((3 * n,)), pltpu.SemaphoreType.DMA((3 * n,)), pltpu.SemaphoreType.DMA((n,))], make)


def _send_exchange(rs):
    n = len(rs)

    def make(ins, outs, sems):
        x, y, c = _pos()
        cps = [pltpu.make_async_remote_copy(src_ref=ins[i], dst_ref=outs[i], send_sem=sems[0].at[i], recv_sem=sems[1].at[i],
                                            device_id=(x, y, 1 - c), device_id_type=MESH) for i in range(n)]

        def start():
            for cp in cps:
                cp.start()

        def finish():
            for cp in cps:
                cp.wait()

        return start, lambda: None, finish

    return _Exchange(rs, [jax.ShapeDtypeStruct(r.shape, r.dtype) for r in rs],
                     [pltpu.SemaphoreType.DMA((n,)), pltpu.SemaphoreType.DMA((n,))], make)


def _reduce_in_vmem(g):
    _, R, C = g.shape
    H = R // 2

    def body(g_ref, mine_ref, other_ref, sib, part, got, swap_sems, send_sems, recv_sems, last_sems):
        x, y, c = _pos()
        me = 2 * x + y
        chips = _other_chips(x, y)
        sibling = (x, y, 1 - c)
        mine = pl.ds(pl.multiple_of(c * H, 8), H)
        theirs = pl.ds(pl.multiple_of((1 - c) * H, 8), H)
        swap = pltpu.make_async_remote_copy(src_ref=g_ref.at[:, theirs, :], dst_ref=sib, send_sem=swap_sems.at[0],
                                            recv_sem=swap_sems.at[1], device_id=sibling, device_id_type=MESH)
        swap.start()
        swap.wait()
        part[...] = (g_ref[:, mine, :] + sib[...]).astype(BF16)

        def copy(k, src_slot, dst_slot):
            px, py = chips[k]
            return pltpu.make_async_remote_copy(src_ref=part.at[src_slot], dst_ref=got.at[dst_slot], send_sem=send_sems.at[k],
                                                recv_sem=recv_sems.at[k], device_id=(px, py, c), device_id_type=MESH)

        for k, (px, py) in enumerate(chips):
            copy(k, 2 * px + py, me).start()
        got[me] = part[me]
        for k, (px, py) in enumerate(chips):
            copy(k, me, 2 * px + py).wait_recv()
        for k, (px, py) in enumerate(chips):
            copy(k, 2 * px + py, me).wait_send()
        mine_ref[...] = ((got[0].astype(F32) + got[1].astype(F32)) + got[2].astype(F32)) + got[3].astype(F32)
        last = pltpu.make_async_remote_copy(src_ref=mine_ref, dst_ref=other_ref, send_sem=last_sems.at[0],
                                            recv_sem=last_sems.at[1], device_id=sibling, device_id_type=MESH)
        last.start()
        last.wait()

    vm = pl.BlockSpec(memory_space=pltpu.VMEM)
    half = jax.ShapeDtypeStruct((H, C), F32)
    return pl.pallas_call(
        body, name="reduce_late", in_specs=[vm], out_specs=[vm, vm], out_shape=[half, half],
        scratch_shapes=[pltpu.VMEM((4, H, C), F32), pltpu.VMEM((4, H, C), BF16), pltpu.VMEM((4, H, C), BF16),
                        pltpu.SemaphoreType.DMA((2,)), pltpu.SemaphoreType.DMA((3,)), pltpu.SemaphoreType.DMA((3,)),
                        pltpu.SemaphoreType.DMA((2,))],
        compiler_params=_params(has_side_effects=True))(g)


def _add_half(gs, rs, c, name):
    n = len(gs)

    def body(c_ref, *refs):
        for g_ref, r_ref, o_ref in zip(refs[:n], refs[n:2 * n], refs[2 * n:]):
            o_ref[...] = (g_ref[...] + r_ref[...]).astype(BF16)

    g_specs, r_specs, out_shape = [], [], []
    for g, r in zip(gs, rs):
        _, H, C = r.shape
        tr = H // 2
        assert tr % 16 == 0 and g.shape == (4, 2 * H, C)
        g_specs.append(pl.BlockSpec((1, tr, C), lambda j, i, c_ref: (j, c_ref[0] * 2 + i, 0)))
        r_specs.append(pl.BlockSpec((1, tr, C), lambda j, i, c_ref: (j, i, 0)))
        out_shape.append(jax.ShapeDtypeStruct((4, H, C), BF16))
    grid_spec = pltpu.PrefetchScalarGridSpec(num_scalar_prefetch=1, grid=(4, 2), in_specs=g_specs + r_specs, out_specs=r_specs)
    return pl.pallas_call(body, name=name, grid_spec=grid_spec, out_shape=out_shape, compiler_params=_params())(c, *gs, *rs)


def _block_diag(w):
    eye = jnp.eye(RNN_BLOCKS, dtype=w.dtype)
    return (eye[:, None, :, None] * w[:, :, None, :]).reshape(D_RNN, D_RNN)


def _diag_blocks(wd):
    d = wd.reshape(RNN_BLOCKS, 64, RNN_BLOCKS, 64)
    return jnp.stack([d[h, :, h, :] for h in range(RNN_BLOCKS)])


def _split_pack(a, first, last):
    out, base = {}, PACK_OFF[first]
    for i in range(first, last):
        s = a[:, PACK_OFF[i] - base:PACK_OFF[i + 1] - base]
        out[BIG_KEYS[i]] = s.reshape(4 * 256, 256) if BIG_KEYS[i] == "w_p_t" else s.reshape(-1, 1024)
    return out


def _layer_grads(x, p, tgt, gw, small, shard=None, core=None):
    row = lambda v: v.reshape(1, -1)
    wa = _block_diag(small["gate_a_w"]).astype(MXU_DTYPE)
    wx = _block_diag(small["gate_x_w"]).astype(MXU_DTYPE)
    sinks = small["attn_sinks"].reshape(1, HEADS)

    dist = shard is not None
    q, kv, xr, gr, xb = _in_proj(x, gw["w_in_t"])
    cut = PACK_OFF[1] + PACK_ROWS[1] // 2
    att, *ga = _attn_fwd(q, kv, sinks, _gather_exchange(shard[PACK_OFF[1]:cut]) if dist else None)
    xc, h, rec, *gb = _rnn_fwd(xr, gr, small["rnn_conv_w"], row(small["rnn_conv_b"]), wa, row(small["gate_a_b"]),
                               wx, row(small["gate_x_b"]), row(small["lru_lambda"]),
                               _gather_exchange(shard[cut:PACK_OFF[3]]) if dist else None)
    if dist:
        gw = {**gw, **_split_pack(jnp.concatenate([ga[0], gb[0]], axis=1), 1, 3)}
    g1, b1 = row(small["ln1_g"]), row(small["ln1_b"])
    fcw = small["ffn_conv_w"].reshape(3, NC, FF_CHUNK).transpose(1, 0, 2)
    fcb = small["ffn_conv_b"].reshape(NC, 1, FF_CHUNK)
    z1, h1b = _out_proj(att, rec, x, gw["w_out"], g1, b1)
    gate, ge, vd, act, *gc = _ffn_up(h1b, gw["w_up_t"], fcw, fcb,
                                     _gather_exchange(shard[PACK_OFF[3]:PACK_OFF[6]]) if dist else None)
    if dist:
        gw = {**gw, **_split_pack(gc[0], 3, 6)}
    dz2, dz2b, dpre, dpp, vec2 = _ffn_down(act, z1, p, tgt, gw["w_down"], gw["w_g"], gw["w_p_t"], g1, b1,
                                           row(small["ln2_g"]), row(small["ln2_b"]), row(small["ple_gate_b"]))
    dup, dfc = _ffn_bwd(dz2b, gate, ge, vd, gw["w_down"], fcw)
    dz1, vec1 = _ffn_dh1(dup, dz2, dpre, z1, gw["w_up_t"], gw["w_g"], g1, b1)
    per_chip = 2 * D_FF // 4 // FF_CHUNK
    big = {"w_ffn_up": _weight_grad_cols(
        h1b, dup.reshape(2 * NC, -1, FF_CHUNK), "dw_up", 2 * NC,
        lambda bt: pl.BlockSpec((None, bt, FF_CHUNK), lambda m, k: (m, k, 0)), (4, D, 2 * D_FF // 4),
        pl.BlockSpec((None, D, FF_CHUNK), lambda m, k: (2 * (m % 2) + (m // 2) // per_chip, 0, (m // 2) % per_chip)))[0]}
    g_dn, *got_up = _weight_grad(act, dz2b, FF_CHUNK, "dw_down", _swap_exchange([big["w_ffn_up"]])) if dist else (
        _weight_grad(act, dz2b, FF_CHUNK, "dw_down"),)
    big["w_ffn_down"] = g_dn.reshape(4, D_FF // 4, D)
    big["ple_gate_w"] = _weight_grad(h1b, dpre, 512, "dw_gate").reshape(4, D // 4, D)
    big["ple_proj"] = _weight_grad(p, dpp, PLE, "dw_proj").reshape(PLE, 4, D // 4).transpose(1, 0, 2)
    big["w_out"] = _dw_out(att, rec, dz1).reshape(4, D // 4, D)
    reduced = None
    if dist:
        g_ffn = [big[k] for k in EARLY_WEIGHTS]
        ex = _swap_exchange(g_ffn[1:])
    datt, drec, *got = _out_proj_bwd(dz1, gw["w_out"], ex if dist else None)
    if dist:
        sums = _add_half(g_ffn, got_up + got, core, "add_half_ffn")
        ex, ex2 = _scatter_exchange(sums[:1]), _scatter_exchange(sums[1:])
    dxr, dgr, dwa, dwx, dvec, *got = _rnn_bwd(drec, gr, h, xc, xr, small["rnn_conv_w"], wa, row(small["gate_a_b"]),
                                              wx, row(small["gate_x_b"]), row(small["lru_lambda"]), ex if dist else None)
    dq, dkv, dsinks, *got2 = _attn_bwd(q, kv, datt, sinks, ex2 if dist else None)
    if dist:
        mine = _add4(got + got2, "add_chips_ffn")
        big = {}
    sg = {
        "attn_sinks": dsinks[:, 0],
        "rnn_conv_w": dvec[4:8],
        "rnn_conv_b": dvec[3],
        "gate_a_w": _diag_blocks(dwa),
        "gate_a_b": dvec[0],
        "gate_x_w": _diag_blocks(dwx),
        "gate_x_b": dvec[1],
        "lru_lambda": dvec[2],
        "ln1_g": vec1[0],
        "ln1_b": vec1[1],
        "ffn_conv_w": dfc[:, 0:3].transpose(1, 0, 2).reshape(3, D_FF),
        "ffn_conv_b": dfc[:, 3].reshape(D_FF),
        "ple_gate_b": vec2[3],
        "ln2_g": vec2[1],
        "ln2_b": vec2[2],
    }
    loss = vec2[0, 0:1]
    grad_x, du = _in_proj_bwd(dq, dkv, dxr, dgr, dz1, gw["w_in_t"])
    ex = None
    if dist:
        ex = _join_exchanges(_send_exchange(mine), _all_devices_exchange(_pack_vecs([sg[k] for k in SMALL] + [loss])[0]))
    big["w_in"], *got = _weight_grad_cols(
        xb, du, "dw_in", 4, lambda bt: pl.BlockSpec((None, bt, D_IN // 4), lambda j, k: (j, k, 0)), (4, D, D_IN // 4),
        pl.BlockSpec((None, D, D_IN // 4), lambda j, k: (j, 0, 0)), ex)
    if dist:
        reduced = (mine, got[:len(mine)])
    return grad_x, big, sg, loss, reduced, got[-1:]


BIG = ("w_in", "w_ffn_up", "w_out", "w_ffn_down", "ple_gate_w", "ple_proj")
BIG_KEYS = ("w_in_t", "w_up_t", "w_out", "w_down", "w_g", "w_p_t")
BIG_T = (True, True, False, False, False, True)
EARLY_WEIGHTS = ("w_ffn_up", "w_ffn_down", "ple_gate_w", "ple_proj", "w_out")
LATE_WEIGHTS = ("w_in",)
SMALL = ("attn_sinks", "rnn_conv_w", "rnn_conv_b", "gate_a_w", "gate_a_b", "gate_x_w", "gate_x_b", "lru_lambda",
         "ln1_g", "ln1_b", "ffn_conv_w", "ffn_conv_b", "ple_gate_b", "ln2_g", "ln2_b")
SHARDED_SMALL = ("rnn_conv_w", "ffn_conv_w")
WEIGHTS = ("w_in", "attn_sinks", "rnn_conv_w", "rnn_conv_b", "gate_a_w", "gate_a_b", "gate_x_w", "gate_x_b",
           "lru_lambda", "w_out", "ln1_g", "ln1_b", "w_ffn_up", "ffn_conv_w", "ffn_conv_b", "w_ffn_down",
           "ple_gate_w", "ple_gate_b", "ple_proj", "ln2_g", "ln2_b")


def _pack_big(d, first=0, last=6):
    parts = []
    for name, t in zip(BIG[first:last], BIG_T[first:last]):
        a = d[name]
        a = a.T if t else a
        parts.append(a.reshape(-1, 1024))
    return jnp.concatenate(parts, axis=0)


def _pack_vecs(items):
    parts, offs, n = [], [], 0
    for a in items:
        f = a.reshape(-1).astype(F32)
        pad = (-f.shape[0]) % 128
        parts.append(jnp.pad(f, (0, pad)))
        offs.append(n)
        n += (f.shape[0] + pad) // 128
    padr = (-n) % 8
    if padr:
        parts.append(jnp.zeros((padr * 128,), F32))
    return jnp.concatenate(parts).reshape(-1, 128), offs


def _unpack_vecs(a, offs, shapes):
    flat = a.reshape(-1)
    out = []
    for o, s in zip(offs, shapes):
        n = 1
        for d in s:
            n *= d
        out.append(flat[o * 128:o * 128 + n].reshape(s))
    return out


def kernel(x, p, w_in, attn_sinks, rnn_conv_w, rnn_conv_b, gate_a_w, gate_a_b, gate_x_w, gate_x_b, lru_lambda, w_out, ln1_g, ln1_b, w_ffn_up, ffn_conv_w, ffn_conv_b, w_ffn_down, ple_gate_w, ple_gate_b, ple_proj, ln2_g, ln2_b, loss_target, m_w_in, m_attn_sinks, m_rnn_conv_w, m_rnn_conv_b, m_gate_a_w, m_gate_a_b, m_gate_x_w, m_gate_x_b, m_lru_lambda, m_w_out, m_ln1_g, m_ln1_b, m_w_ffn_up, m_ffn_conv_w, m_ffn_conv_b, m_w_ffn_down, m_ple_gate_w, m_ple_gate_b, m_ple_proj, m_ln2_g, m_ln2_b, v_w_in, v_attn_sinks, v_rnn_conv_w, v_rnn_conv_b, v_gate_a_w, v_gate_a_b, v_gate_x_w, v_gate_x_b, v_lru_lambda, v_w_out, v_ln1_g, v_ln1_b, v_w_ffn_up, v_ffn_conv_w, v_ffn_conv_b, v_w_ffn_down, v_ple_gate_w, v_ple_gate_b, v_ple_proj, v_ln2_g, v_ln2_b):
    w = dict(w_in=w_in, attn_sinks=attn_sinks, rnn_conv_w=rnn_conv_w, rnn_conv_b=rnn_conv_b, gate_a_w=gate_a_w,
             gate_a_b=gate_a_b, gate_x_w=gate_x_w, gate_x_b=gate_x_b, lru_lambda=lru_lambda, w_out=w_out, ln1_g=ln1_g,
             ln1_b=ln1_b, w_ffn_up=w_ffn_up, ffn_conv_w=ffn_conv_w, ffn_conv_b=ffn_conv_b, w_ffn_down=w_ffn_down,
             ple_gate_w=ple_gate_w, ple_gate_b=ple_gate_b, ple_proj=ple_proj, ln2_g=ln2_g, ln2_b=ln2_b)
    m = dict(w_in=m_w_in, attn_sinks=m_attn_sinks, rnn_conv_w=m_rnn_conv_w, rnn_conv_b=m_rnn_conv_b, gate_a_w=m_gate_a_w,
             gate_a_b=m_gate_a_b, gate_x_w=m_gate_x_w, gate_x_b=m_gate_x_b, lru_lambda=m_lru_lambda, w_out=m_w_out,
             ln1_g=m_ln1_g, ln1_b=m_ln1_b, w_ffn_up=m_w_ffn_up, ffn_conv_w=m_ffn_conv_w, ffn_conv_b=m_ffn_conv_b,
             w_ffn_down=m_w_ffn_down, ple_gate_w=m_ple_gate_w, ple_gate_b=m_ple_gate_b, ple_proj=m_ple_proj,
             ln2_g=m_ln2_g, ln2_b=m_ln2_b)
    v = dict(w_in=v_w_in, attn_sinks=v_attn_sinks, rnn_conv_w=v_rnn_conv_w, rnn_conv_b=v_rnn_conv_b, gate_a_w=v_gate_a_w,
             gate_a_b=v_gate_a_b, gate_x_w=v_gate_x_w, gate_x_b=v_gate_x_b, lru_lambda=v_lru_lambda, w_out=v_w_out,
             ln1_g=v_ln1_g, ln1_b=v_ln1_b, w_ffn_up=v_w_ffn_up, ffn_conv_w=v_ffn_conv_w, ffn_conv_b=v_ffn_conv_b,
             w_ffn_down=v_w_ffn_down, ple_gate_w=v_ple_gate_w, ple_gate_b=v_ple_gate_b, ple_proj=v_ple_proj,
             ln2_g=v_ln2_g, ln2_b=v_ln2_b)
    w, m, v = ({k: a[0] for k, a in d.items()} for d in (w, m, v))
    chip = 2 * lax.axis_index("x") + lax.axis_index("y")
    core = lax.axis_index("c")

    wpack = _pack_big(w)
    cpack, _ = _pack_vecs([w["rnn_conv_w"], w["ffn_conv_w"]])
    shard = wpack.astype(MXU_DTYPE)
    g_in, gcp = _gather_first(shard[PACK_OFF[0]:PACK_OFF[1]], cpack)
    gw = _split_pack(g_in, 0, 1)
    small = {k: w[k] for k in SMALL}
    small["rnn_conv_w"] = gcp[:, 0:4].reshape(4, 4, 128).transpose(1, 0, 2).reshape(4, 512)
    small["ffn_conv_w"] = gcp[:, 4:22].reshape(4, 3, 768).transpose(1, 0, 2).reshape(3, 3072)

    core1 = core.reshape(1).astype(jnp.int32)
    grad_x, big, sg, loss, ffn_halves, small_all = _layer_grads(x[0], p[0, 0], loss_target[0], gw, small, shard, core1)

    shapes = [sg[k].shape for k in SMALL] + [(1,)]
    _, offs = _pack_vecs([jnp.zeros(s, F32) for s in shapes])
    red = dict(zip(SMALL + ("loss",), _unpack_vecs(_sum_devices(small_all[0]), offs, shapes)))
    red["rnn_conv_w"] = lax.dynamic_slice_in_dim(red["rnn_conv_w"], chip * 128, 128, axis=1)
    red["ffn_conv_w"] = lax.dynamic_slice_in_dim(red["ffn_conv_w"], chip * 768, 768, axis=1)

    late_mine, late_other = ([a] for a in _reduce_in_vmem(big["w_in"]))

    def adamw(names, mine, other, name):
        out, _ = _adamw_halves([w[k] for k in names], mine, other, [m[k] for k in names], [v[k] for k in names],
                               core1, name)
        return dict(zip(names, out))

    big_out = {**adamw(LATE_WEIGHTS, late_mine, late_other, "adamw_late"), **adamw(EARLY_WEIGHTS, *ffn_halves, "adamw_early")}
    wsm, offs2 = _pack_vecs([w[k] for k in SMALL])
    gsm, _ = _pack_vecs([red[k] for k in SMALL])
    msm, _ = _pack_vecs([m[k] for k in SMALL])
    vsm, _ = _pack_vecs([v[k] for k in SMALL])
    dsm, nmsm, nvsm = _adamw(wsm, gsm, msm, vsm, "adamw_small")
    shapes2 = [w[k].shape for k in SMALL]

    def named(n, smallp):
        d = {k: out[n][None] for k, out in big_out.items()}
        d.update({k: a[None] for k, a in zip(SMALL, _unpack_vecs(smallp, offs2, shapes2))})
        return [d[k] for k in WEIGHTS]

    return (red["loss"].reshape(()), grad_x[None], *named(0, gsm), *named(1, dsm), *named(2, nmsm), *named(3, nvsm))
```

```python
import functools

import jax
import jax.numpy as jnp
from jax import lax
from jax.experimental import pallas as pl
from jax.experimental.pallas import tpu as pltpu

F32 = jnp.float32
BF16 = jnp.bfloat16
MXU_DTYPE = jnp.bfloat16

D = 1024
D_ATT = 512
D_KV = 128
D_RNN = 512
D_IN = 1792
D_FF = 3072
FF_CHUNK = 768
PLE = 256
HEADS = 8
HEAD_DIM = 64
BLK = 128
ATTN_BLOCKS = 8
DW_TOKENS = 4096
RNN_BLOCKS = 8
LN_EPS = 1e-5
LRU_C = 8.0
ALPHA = float(2.0 ** 0.25)
SCALE = HEAD_DIM ** -0.5
NEG = -1e30

ADAM_LR = 0.001
ADAM_B1 = 0.9
ADAM_B2 = 0.999
ADAM_EPS = 1e-08
ADAM_WD = 0.01
ADAM_STEP = 10

VMEM_LIMIT_BYTES = 56 * 1024 * 1024
MESH = pl.DeviceIdType.MESH

PACK_ROWS = (448, 1536, 256, 768, 256, 64)
PACK_OFF = tuple(sum(PACK_ROWS[:i]) for i in range(len(PACK_ROWS) + 1))
PACK_TOTAL = PACK_OFF[-1]


def _params(**kw):
    return pltpu.CompilerParams(vmem_limit_bytes=VMEM_LIMIT_BYTES, **kw)


def _mm(a, b):
    return jnp.dot(a.astype(MXU_DTYPE), b.astype(MXU_DTYPE), preferred_element_type=F32)


def _mm_nt(a, b):
    return lax.dot_general(a.astype(MXU_DTYPE), b.astype(MXU_DTYPE), (((1,), (1,)), ((), ())),
                           preferred_element_type=F32)


def _mm_tn(a, b):
    return lax.dot_general(a.astype(MXU_DTYPE), b.astype(MXU_DTYPE), (((0,), (0,)), ((), ())),
                           preferred_element_type=F32)


def _sigmoid(x):
    return 0.5 + 0.5 * jnp.tanh(0.5 * x)


def _gelu(x):
    c = 0.7978845608028654
    k = 0.044715
    x2 = x * x
    t = jnp.tanh(x * (c + (c * k) * x2))
    h = 0.5 * (1.0 + t)
    return x * h, h * (1.0 + (x * (1.0 - t)) * (c + (3.0 * c * k) * x2))


def _shift_rows(x, s, edge8):
    R = x.shape[0]
    row8 = lax.broadcasted_iota(jnp.int32, (8, x.shape[1]), 0)
    if s > 0:
        rolled = pltpu.roll(x, s, 0)
        first = jnp.where(row8 < s, pltpu.roll(edge8, s, 0), rolled[0:8])
        return jnp.concatenate([first, rolled[8:]], axis=0)
    k = -s
    rolled = pltpu.roll(x, R - k, 0)
    last = jnp.where(row8 >= 8 - k, pltpu.roll(edge8, 8 - k, 0), rolled[R - 8:])
    return jnp.concatenate([rolled[:R - 8], last], axis=0)


def _softplus(x):
    return jnp.maximum(x, 0.0) + jnp.log(1.0 + jnp.exp(-jnp.abs(x)))


def _ln(z, g, b):
    mu = jnp.mean(z, axis=-1, keepdims=True)
    zc = z - mu
    var = jnp.mean(zc * zc, axis=-1, keepdims=True)
    rstd = lax.rsqrt(var + LN_EPS)
    xhat = zc * rstd
    return xhat * g + b, xhat, rstd


def _ln_bwd(dy, xhat, rstd, g):
    dxh = dy * g
    m1 = jnp.mean(dxh, axis=-1, keepdims=True)
    m2 = jnp.mean(dxh * xhat, axis=-1, keepdims=True)
    return rstd * (dxh - m1 - xhat * m2)


def _colsum(x):
    return jnp.sum(x, axis=0, keepdims=True)


def _full(shape):
    nd = len(shape)
    return pl.BlockSpec(shape, lambda *_: (0,) * nd)


def _rows(tm, cols, fn=None):
    if fn is None:
        return pl.BlockSpec((tm, cols), lambda i: (i, 0))
    return pl.BlockSpec((tm, cols), lambda i: (fn(i), 0))


def _heads(tm):
    return pl.BlockSpec((HEADS, tm, HEAD_DIM), lambda i: (0, i, 0))


def _in_proj(x, w_in_t):
    T = x.shape[0]
    tm = min(1024, T)

    def body(x_ref, w_ref, q_ref, kv_ref, xr_ref, gr_ref, xb_ref):
        xb = x_ref[...].astype(MXU_DTYPE)
        xb_ref[...] = xb.astype(BF16)
        q = _mm_nt(xb, w_ref[0:512, :])
        for h in range(HEADS):
            q_ref[h] = q[:, h * 64:(h + 1) * 64].astype(BF16)
        kv_ref[...] = _mm_nt(xb, w_ref[512:768, :]).astype(BF16)
        xr_ref[...] = _mm_nt(xb, w_ref[768:1280, :])
        gr_ref[...] = _mm_nt(xb, w_ref[1280:1792, :])

    return pl.pallas_call(
        body, name="in_proj", grid=(T // tm,),
        in_specs=[_rows(tm, D), _full((D_IN, D))],
        out_specs=[_heads(tm), _rows(tm, 256), _rows(tm, 512), _rows(tm, 512), _rows(tm, D)],
        out_shape=[jax.ShapeDtypeStruct((HEADS, T, 64), BF16), jax.ShapeDtypeStruct((T, 256), BF16),
                   jax.ShapeDtypeStruct((T, 512), F32), jax.ShapeDtypeStruct((T, 512), F32),
                   jax.ShapeDtypeStruct((T, D), BF16)],
        compiler_params=_params(),
    )(x, w_in_t)


def _attn_band(kv_ref, i):
    cur = pl.multiple_of(i * BLK, BLK)
    prev = pl.multiple_of(jnp.maximum(i - 1, 0) * BLK, BLK)
    band = jnp.concatenate([kv_ref[pl.ds(prev, BLK), :], kv_ref[pl.ds(cur, BLK), :]], axis=0)
    key = lax.broadcasted_iota(jnp.int32, (2 * BLK, 4 * BLK), 0)
    qry = lax.broadcasted_iota(jnp.int32, (2 * BLK, 4 * BLK), 1) & (BLK - 1)
    in_prev = jnp.logical_and(jnp.logical_and(key < BLK, key > qry), i > 0)
    mask = jnp.logical_or(in_prev, jnp.logical_and(key >= BLK, key - BLK <= qry))
    return band, mask, cur, prev


def _attn_scores(band, mask, qs, s_ref, g):
    st = jnp.where(mask, _mm_nt(band[:, g * 64:(g + 1) * 64], qs) * SCALE, NEG)
    lane = lax.broadcasted_iota(jnp.int32, (1, 4 * BLK), 1)
    sv = jnp.where(lane < BLK, s_ref[0, 4 * g],
                   jnp.where(lane < 2 * BLK, s_ref[0, 4 * g + 1], jnp.where(lane < 3 * BLK, s_ref[0, 4 * g + 2], s_ref[0, 4 * g + 3])))
    m = jnp.maximum(jnp.max(st, axis=0, keepdims=True), sv)
    p = jnp.exp(st - m)
    ps = jnp.exp(sv - m)
    return p, ps, jnp.sum(p, axis=0, keepdims=True) + ps


def _pos():
    return lax.axis_index("x"), lax.axis_index("y"), lax.axis_index("c")


def _other_chips(x, y):
    return [(1 - x, y), (x, 1 - y), (1 - x, 1 - y)]


def _gather_steps(w_ref, gw_ref, send_sems, recv_sems, local_sem):
    x, y, c = _pos()
    me = 2 * x + y
    chips = _other_chips(x, y)
    half = w_ref.shape[0] // 2
    mine = pl.ds(pl.multiple_of(c * half, 16), half)
    theirs = pl.ds(pl.multiple_of((1 - c) * half, 16), half)
    loc = pltpu.make_async_copy(w_ref, gw_ref.at[me], local_sem)

    def copy(k, src, dst, to):
        return pltpu.make_async_remote_copy(src_ref=src, dst_ref=dst, send_sem=send_sems.at[k], recv_sem=recv_sems.at[k],
                                            device_id=to, device_id_type=MESH)

    def out(k):
        px, py = chips[k]
        return copy(k, w_ref.at[mine], gw_ref.at[me, mine], (px, py, c))

    def fwd(k, rows):
        px, py = chips[k]
        return copy(3 + k, gw_ref.at[2 * px + py, rows], gw_ref.at[2 * px + py, rows], (x, y, 1 - c))

    def start():
        loc.start()
        for k in range(3):
            out(k).start()

    def forward():
        for k in range(3):
            px, py = chips[k]
            copy(k, w_ref.at[mine], gw_ref.at[2 * px + py, mine], (px, py, c)).wait_recv()
            fwd(k, mine).start()

    def finish():
        for k in range(3):
            fwd(k, theirs).wait_recv()
        for k in range(3):
            out(k).wait_send()
            fwd(k, mine).wait_send()
        loc.wait()

    return start, forward, finish


GATHER_SCRATCH = [pltpu.SemaphoreType.DMA((6,)), pltpu.SemaphoreType.DMA((6,)), pltpu.SemaphoreType.DMA]


class _Exchange:
    def __init__(self, args, out_shape, scratch, make):
        self.args, self.out_shape, self.scratch, self.make = list(args), list(out_shape), list(scratch), make


def _join_exchanges(a, b):
    na, nao, nas = len(a.args), len(a.out_shape), len(a.scratch)

    def make(ins, outs, sems):
        steps_a = a.make(ins[:na], outs[:nao], sems[:nas])
        steps_b = b.make(ins[na:], outs[nao:], sems[nas:])

        def both(f, g):
            def run():
                f()
                g()
            return run

        return tuple(both(f, g) for f, g in zip(steps_a, steps_b))

    return _Exchange(a.args + b.args, a.out_shape + b.out_shape, a.scratch + b.scratch, make)


def _gather_exchange(wsrc):
    return _Exchange([wsrc], [jax.ShapeDtypeStruct((4,) + wsrc.shape, wsrc.dtype)], GATHER_SCRATCH,
                     lambda ins, outs, sems: _gather_steps(ins[0], outs[0], *sems))


def _launch(body, name, grid, in_specs, out_specs, out_shape, scratch, args, exchange=None, prefetch=0):
    def call(fn, fn_name, ins, outs, shapes, scr, operands, effects):
        spec = pltpu.PrefetchScalarGridSpec(num_scalar_prefetch=prefetch, grid=grid, in_specs=ins, out_specs=outs,
                                            scratch_shapes=scr)
        return pl.pallas_call(fn, name=fn_name, grid_spec=spec, out_shape=shapes,
                              compiler_params=_params(has_side_effects=effects))(*operands)

    if exchange is None:
        return call(body, name, list(in_specs), list(out_specs), list(out_shape), list(scratch), args, False)
    n_in, n_out, ei, eo, ns = len(in_specs), len(out_specs), len(exchange.args), len(exchange.out_shape), len(exchange.scratch)
    nsteps = 1
    for g in grid:
        nsteps *= g

    def wrapped(*refs):
        scalars, refs = refs[:prefetch], refs[prefetch:]
        ins, xin = refs[:n_in], refs[n_in:n_in + ei]
        outs, xout = refs[n_in + ei:n_in + ei + n_out], refs[n_in + ei + n_out:n_in + ei + n_out + eo]
        rest = refs[n_in + ei + n_out + eo:]
        own, sems = rest[:len(rest) - ns], rest[len(rest) - ns:]
        start, forward, finish = exchange.make(xin, xout, sems)
        i = pl.program_id(0)
        for d in range(1, len(grid)):
            i = i * grid[d] + pl.program_id(d)
        pl.when(i == 0)(start)
        body(*scalars, *ins, *outs, *own)
        pl.when(i == max(nsteps - 3, 0))(forward)
        pl.when(i == nsteps - 1)(finish)

    anyspec = pl.BlockSpec(memory_space=pl.ANY)
    return call(wrapped, name + "_x", list(in_specs) + [anyspec] * ei, list(out_specs) + [anyspec] * eo,
                list(out_shape) + exchange.out_shape, list(scratch) + exchange.scratch, (*args, *exchange.args), True)


def _attn_fwd(q, kv, sinks, exchange=None):
    T = kv.shape[0]
    nblk = min(ATTN_BLOCKS, T // BLK)

    def body(q_ref, kv_ref, s_ref, o_ref):
        for b in range(nblk):
            rows = slice(b * BLK, (b + 1) * BLK)
            band, mask, _, _ = _attn_band(kv_ref, nblk * pl.program_id(0) + b)
            for g in range(2):
                qs = q_ref[4 * g:4 * g + 4, rows, :].reshape(4 * BLK, HEAD_DIM)
                p, _, den = _attn_scores(band, mask, qs, s_ref, g)
                ot = _mm_tn(band[:, 128:256], p) * (1.0 / den)
                for hh in range(4):
                    o = ot[:, hh * BLK:(hh + 1) * BLK].T
                    o_ref[rows, (4 * g + hh) * 64:(4 * g + hh + 1) * 64] = o[:, g * 64:(g + 1) * 64].astype(BF16)

    tq = nblk * BLK
    return _launch(body, "attn_fwd", (T // tq,), [_heads(tq), _full((T, 256)), pl.BlockSpec(memory_space=pltpu.SMEM)],
                   [_rows(tq, 512)], [jax.ShapeDtypeStruct((T, 512), BF16)], [], (q, kv, sinks), exchange)


def _attn_bwd(q, kv, do, sinks, exchange=None):
    T = kv.shape[0]
    nblk = min(ATTN_BLOCKS, T // BLK)

    def body(q_ref, kv_ref, do_ref, s_ref, dq_ref, dkv_ref, ds_ref):
        @pl.when(pl.program_id(0) == 0)
        def _():
            ds_ref[...] = jnp.zeros_like(ds_ref)

        for b in range(nblk):
            rows = slice(b * BLK, (b + 1) * BLK)
            band, mask, cur, prev = _attn_band(kv_ref, nblk * pl.program_id(0) + b)
            for g in range(2):
                qs = q_ref[4 * g:4 * g + 4, rows, :].reshape(4 * BLK, HEAD_DIM)
                dos = do_ref[4 * g:4 * g + 4, rows, :].reshape(4 * BLK, HEAD_DIM)
                p, ps, den = _attn_scores(band, mask, qs, s_ref, g)
                inv = 1.0 / den
                p = p * inv
                dpt = _mm_nt(band[:, 128 + g * 64:192 + g * 64], dos)
                delta = jnp.sum(p * dpt, axis=0, keepdims=True)
                dst = p * (dpt - delta)
                dsv = -(ps * inv) * delta
                for hh in range(4):
                    dsink = jnp.sum(dsv[:, hh * BLK:(hh + 1) * BLK], axis=1, keepdims=True)
                    ds_ref[4 * g + hh:4 * g + hh + 1, :] += jnp.broadcast_to(dsink, (1, 128))
                dqt = _mm_tn(band[:, 0:128], dst) * SCALE
                for hh in range(4):
                    dqh = dqt[:, hh * BLK:(hh + 1) * BLK].T
                    dq_ref[rows, (4 * g + hh) * 64:(4 * g + hh + 1) * 64] = dqh[:, g * 64:(g + 1) * 64].astype(BF16)
                dk = _mm(dst, qs) * SCALE
                dv = _mm(p, dos)
                dkv_ref[pl.ds(cur, BLK), g * 64:(g + 1) * 64] = dk[BLK:2 * BLK]
                dkv_ref[pl.ds(cur, BLK), 128 + g * 64:192 + g * 64] = dv[BLK:2 * BLK]
                dkv_ref[pl.ds(prev, BLK), g * 64:(g + 1) * 64] += dk[0:BLK]
                dkv_ref[pl.ds(prev, BLK), 128 + g * 64:192 + g * 64] += dv[0:BLK]

    tq = nblk * BLK
    return _launch(body, "attn_bwd", (T // tq,),
                   [_heads(tq), _full((T, 256)), _heads(tq), pl.BlockSpec(memory_space=pltpu.SMEM)],
                   [_rows(tq, 512), _full((T, 256)), _full((8, 128))],
                   [jax.ShapeDtypeStruct((T, 512), BF16), jax.ShapeDtypeStruct((T, 256), F32),
                    jax.ShapeDtypeStruct((8, 128), F32)], [], (q, kv, do, sinks), exchange)


def _rows8(tm, cols):
    return lax.broadcasted_iota(jnp.int32, (tm, cols), 0) & 7


def _lru_gates(xc, wa, ba, wx, bx, lam):
    r = _sigmoid(_mm(xc, wa) + ba)
    ii = _sigmoid(_mm(xc, wx) + bx)
    sp = _softplus(-lam)
    la = -LRU_C * r * sp
    a = jnp.exp(la)
    m = jnp.sqrt(-jnp.tanh(la) * (a * a + 1.0))
    return r, ii, sp, a, m


def _rnn_fwd(xr, gr, cw, cb, wa, ba, wx, bx, lam, exchange=None):
    T = xr.shape[0]
    tm = 512
    C = D_RNN

    def body(xr_ref, gr_ref, cw_ref, cb_ref, wa_ref, ba_ref, wx_ref, bx_ref, lam_ref,
             xc_ref, h_ref, rec_ref, ext, a_s, b_s, carry):
        i = pl.program_id(0)

        @pl.when(i == 0)
        def _():
            ext[...] = jnp.zeros((8, C), F32)
            carry[...] = jnp.zeros((8, C), F32)

        xr = xr_ref[...]
        edge = ext[...]
        xc = cb_ref[...] + cw_ref[3:4, :] * xr
        for k in range(3):
            xc = xc + cw_ref[k:k + 1, :] * _shift_rows(xr, 3 - k, edge)
        ext[...] = xr[tm - 8:tm, :]
        xc_ref[...] = xc
        _, ii, _, a, m = _lru_gates(xc, wa_ref[...], ba_ref[...], wx_ref[...], bx_ref[...], lam_ref[...])
        b = m * ii * xc
        r8 = _rows8(tm, C)
        for d in (1, 2, 4):
            ok = r8 >= d
            a_sh = jnp.where(ok, pltpu.roll(a, d, 0), 1.0)
            b_sh = jnp.where(ok, pltpu.roll(b, d, 0), 0.0)
            b = a * b_sh + b
            a = a * a_sh
        a_s[...] = a
        b_s[...] = b

        def step(g, hin):
            s = pl.multiple_of(g * 8, 8)
            hg = a_s[pl.ds(s, 8), :] * hin + b_s[pl.ds(s, 8), :]
            h_ref[pl.ds(s, 8), :] = hg
            return jnp.broadcast_to(hg[7:8, :], (8, C))

        carry[...] = lax.fori_loop(0, tm // 8, step, carry[...], unroll=4)
        ge, _ = _gelu(gr_ref[...])
        rec_ref[...] = (h_ref[...] * ge).astype(BF16)

    vec = _full((1, C))
    in_specs = [_rows(tm, C), _rows(tm, C), _full((4, C)), vec, _full((C, C)), vec, _full((C, C)), vec, vec]
    out_specs = [_rows(tm, C), _rows(tm, C), _rows(tm, C)]
    out_shape = [jax.ShapeDtypeStruct((T, C), F32), jax.ShapeDtypeStruct((T, C), F32), jax.ShapeDtypeStruct((T, C), BF16)]
    scratch = [pltpu.VMEM((8, C), F32), pltpu.VMEM((tm, C), F32), pltpu.VMEM((tm, C), F32), pltpu.VMEM((8, C), F32)]
    return _launch(body, "rnn_fwd", (T // tm,), in_specs, out_specs, out_shape, scratch,
                   (xr, gr, cw, cb, wa, ba, wx, bx, lam), exchange)


def _rnn_bwd(drec, gr, h, xc, xr, cw, wa, ba, wx, bx, lam, exchange=None):
    T = xr.shape[0]
    tm = 512
    C = D_RNN
    nt = T // tm
    t8 = tm // 8

    def body(drec_ref, gr_ref, h_ref, hp_ref, xc_ref, xr_ref, cw_ref, wa_ref, ba_ref, wx_ref, bx_ref,
             lam_ref, dxr_ref, dgr_ref, dwa_ref, dwx_ref, dvec_ref, c_s, g_s, gout, ext, anext, gcarry):
        i = pl.program_id(0)
        j = nt - 1 - i

        @pl.when(i == 0)
        def _():
            dwa_ref[...] = jnp.zeros_like(dwa_ref)
            dwx_ref[...] = jnp.zeros_like(dwx_ref)
            dvec_ref[...] = jnp.zeros_like(dvec_ref)
            anext[...] = jnp.zeros((8, C), F32)
            gcarry[...] = jnp.zeros((8, C), F32)
            ext[...] = jnp.zeros((8, C), F32)

        xc = xc_ref[...]
        lam = lam_ref[...]
        r, ii, sp, a, m = _lru_gates(xc, wa_ref[...], ba_ref[...], wx_ref[...], bx_ref[...], lam)
        ge, dge = _gelu(gr_ref[...])
        drec = drec_ref[...]
        hh = h_ref[...]
        dgr_ref[...] = (drec * hh * dge).astype(BF16)
        dh = drec * ge
        rowi = lax.broadcasted_iota(jnp.int32, (tm, C), 0)
        c = jnp.where(rowi == tm - 1, jnp.broadcast_to(anext[0:1, :], (tm, C)), pltpu.roll(a, tm - 1, 0))
        anext[...] = a[0:8, :]
        r8 = rowi & 7
        gg = dh
        for d in (1, 2, 4):
            ok = r8 < 8 - d
            c_sh = jnp.where(ok, pltpu.roll(c, tm - d, 0), 1.0)
            g_sh = jnp.where(ok, pltpu.roll(gg, tm - d, 0), 0.0)
            gg = c * g_sh + gg
            c = c * c_sh
        c_s[...] = c
        g_s[...] = gg

        def step(k, gin):
            s = pl.multiple_of((t8 - 1 - k) * 8, 8)
            og = c_s[pl.ds(s, 8), :] * gin + g_s[pl.ds(s, 8), :]
            gout[pl.ds(s, 8), :] = og
            return jnp.broadcast_to(og[0:1, :], (8, C))

        gcarry[...] = lax.fori_loop(0, t8, step, gcarry[...], unroll=4)
        G = gout[...]
        hprev_row = jnp.where(j > 0, hp_ref[7:8, :], 0.0)
        hprev = jnp.where(rowi == 0, jnp.broadcast_to(hprev_row, (tm, C)), pltpu.roll(hh, 1, 0))
        da = G * hprev
        dm = G * ii * xc
        di = G * m * xc
        dxc = G * m * ii
        dla = da * a - dm * a * a / m
        dr = dla * (-LRU_C * sp)
        dsp = _colsum(dla * (-LRU_C * r))
        dlam = dsp * (-_sigmoid(-lam))
        dpr = dr * r * (1.0 - r)
        dpi = di * ii * (1.0 - ii)
        dxc = dxc + _mm_nt(dpr, wa_ref[...]) + _mm_nt(dpi, wx_ref[...])
        dwa_ref[...] += _mm_tn(xc, dpr)
        dwx_ref[...] += _mm_tn(xc, dpi)
        dvec_ref[0:1, :] += _colsum(dpr)
        dvec_ref[1:2, :] += _colsum(dpi)
        dvec_ref[2:3, :] += dlam
        dvec_ref[3:4, :] += _colsum(dxc)
        edge = ext[...]
        xr = xr_ref[...]
        dxr = cw_ref[3:4, :] * dxc
        dvec_ref[7:8, :] += _colsum(dxc * xr)
        for k in range(3):
            up = _shift_rows(dxc, k - 3, edge)
            dxr = dxr + cw_ref[k:k + 1, :] * up
            dvec_ref[4 + k:5 + k, :] += _colsum(up * xr)
        ext[...] = dxc[0:8, :]
        dxr_ref[...] = dxr.astype(BF16)

    rev = lambda i: nt - 1 - i
    prev8 = lambda i: jnp.maximum((nt - 1 - i) * t8 - 1, 0)
    vec = _full((1, C))
    return _launch(
        body, "rnn_bwd", (nt,),
        [_rows(tm, C, rev), _rows(tm, C, rev), _rows(tm, C, rev), _rows(8, C, prev8), _rows(tm, C, rev),
         _rows(tm, C, rev), _full((4, C)), _full((C, C)), vec, _full((C, C)), vec, vec],
        [_rows(tm, C, rev), _rows(tm, C, rev), _full((C, C)), _full((C, C)), _full((8, C))],
        [jax.ShapeDtypeStruct((T, C), BF16), jax.ShapeDtypeStruct((T, C), BF16),
         jax.ShapeDtypeStruct((C, C), F32), jax.ShapeDtypeStruct((C, C), F32), jax.ShapeDtypeStruct((8, C), F32)],
        [pltpu.VMEM((tm, C), F32), pltpu.VMEM((tm, C), F32), pltpu.VMEM((tm, C), F32),
         pltpu.VMEM((8, C), F32), pltpu.VMEM((8, C), F32), pltpu.VMEM((8, C), F32)],
        (drec, gr, h, h, xc, xr, cw, wa, ba, wx, bx, lam), exchange)


def _out_proj(att, rec, x, w_out, g1, b1):
    T = x.shape[0]
    tm = min(1024, T)

    def body(att_ref, rec_ref, x_ref, w_ref, g1_ref, b1_ref, z_ref, h_ref):
        mix = _mm(att_ref[...], w_ref[0:512, :]) + _mm(rec_ref[...], w_ref[512:1024, :])
        z1 = ALPHA * x_ref[...] + mix
        z_ref[...] = z1
        h1, _, _ = _ln(z1, g1_ref[...], b1_ref[...])
        h_ref[...] = h1.astype(MXU_DTYPE).astype(BF16)

    return pl.pallas_call(
        body, name="out_proj", grid=(T // tm,),
        in_specs=[_rows(tm, 512), _rows(tm, 512), _rows(tm, D), _full((D, D)), _full((1, D)), _full((1, D))],
        out_specs=[_rows(tm, D), _rows(tm, D)],
        out_shape=[jax.ShapeDtypeStruct((T, D), F32), jax.ShapeDtypeStruct((T, D), BF16)],
        compiler_params=_params(),
    )(att, rec, x, w_out, g1, b1)


NC = D_FF // FF_CHUNK


def _ffn_up(h1b, w_up_t, fcw, fcb, exchange=None):
    T = h1b.shape[0]
    tm = min(1024, T)
    CW = FF_CHUNK

    def body(h_ref, wg_ref, wv_ref, fcw_ref, fcb_ref, gate_ref, ge_ref, vd_ref, act_ref, before):
        i = pl.program_id(1)

        @pl.when(i == 0)
        def _():
            before[...] = jnp.zeros((8, CW), F32)

        hb = h_ref[...]
        gate = _mm_nt(hb, wg_ref[...])
        val = _mm_nt(hb, wv_ref[...])
        gate_ref[...] = gate.astype(BF16)
        edge = before[...]
        gc = (fcb_ref[...] + fcw_ref[0:1, :] * _shift_rows(gate, 2, edge) + fcw_ref[1:2, :] * _shift_rows(gate, 1, edge)
              + fcw_ref[2:3, :] * gate)
        before[...] = gate[tm - 8:tm, :]
        ge, dge = _gelu(gc)
        ge_ref[...] = ge.astype(BF16)
        vd_ref[...] = (val * dge).astype(BF16)
        act_ref[...] = (ge * val).astype(BF16)

    chunk = pl.BlockSpec((None, tm, CW), lambda c, i: (c, i, 0))
    return _launch(
        body, "ffn_up", (NC, T // tm),
        [pl.BlockSpec((tm, D), lambda c, i: (i, 0)), pl.BlockSpec((CW, D), lambda c, i: (c, 0)),
         pl.BlockSpec((CW, D), lambda c, i: (NC + c, 0)), pl.BlockSpec((None, 3, CW), lambda c, i: (c, 0, 0)),
         pl.BlockSpec((None, 1, CW), lambda c, i: (c, 0, 0))],
        [chunk] * 3 + [pl.BlockSpec((tm, CW), lambda c, i: (i, c))],
        [jax.ShapeDtypeStruct((NC, T, CW), BF16)] * 3 + [jax.ShapeDtypeStruct((T, D_FF), BF16)], [pltpu.VMEM((8, CW), F32)],
        (h1b, w_up_t, w_up_t, fcw, fcb), exchange)


def _ffn_down(act, z1, p, tgt, w_down, w_g, w_p_t, g1, b1, g2, b2, bg):
    T = z1.shape[0]
    tm = 512

    def body(act_ref, z_ref, p_ref, t_ref, wdn_hbm, wg_hbm, wp_hbm, g1_ref, b1_ref, g2_ref, b2_ref, bg_ref,
             dz2_ref, dz2b_ref, dpre_ref, dpp_ref, vec_ref, wdn, wg, wp):
        @pl.when(pl.program_id(0) == 0)
        def _():
            pltpu.sync_copy(wdn_hbm, wdn)
            pltpu.sync_copy(wg_hbm, wg)
            pltpu.sync_copy(wp_hbm, wp)
            vec_ref[...] = jnp.zeros_like(vec_ref)

        g2v = g2_ref[...]
        for r in (slice(0, tm // 2), slice(tm // 2, tm)):
            h1, _, _ = _ln(z_ref[r, :], g1_ref[...], b1_ref[...])
            h1b = h1.astype(MXU_DTYPE)
            ffn = _mm(act_ref[r, :], wdn[...])
            sg = _sigmoid(_mm(h1b, wg[...]) + bg_ref[...])
            pp = _mm_nt(p_ref[r, :], wp[...])
            z2 = ALPHA * h1 + ffn + sg * pp
            y, xh2, rstd2 = _ln(z2, g2v, b2_ref[...])
            diff = y - t_ref[r, :]
            dy = diff * (1.0 / D)
            dz2 = _ln_bwd(dy, xh2, rstd2, g2v)
            dpre = dz2 * pp * sg * (1.0 - sg)
            dz2_ref[r, :] = dz2
            dz2b_ref[r, :] = dz2.astype(BF16)
            dpre_ref[r, :] = dpre.astype(BF16)
            dpp_ref[r, :] = (dz2 * sg).astype(BF16)
            loss = 0.5 * jnp.sum(jnp.sum(diff * diff, axis=1, keepdims=True), axis=0, keepdims=True) * (1.0 / D)
            vec_ref[0:1, :] += jnp.broadcast_to(loss, (1, D))
            vec_ref[1:2, :] += _colsum(dy * xh2)
            vec_ref[2:3, :] += _colsum(dy)
            vec_ref[3:4, :] += _colsum(dpre)

    anyspec = pl.BlockSpec(memory_space=pl.ANY)
    vec = _full((1, D))
    return pl.pallas_call(
        body, name="ffn_down", grid=(T // tm,),
        in_specs=[_rows(tm, D_FF), _rows(tm, D), _rows(tm, PLE), _rows(tm, D),
                  anyspec, anyspec, anyspec] + [vec] * 5,
        out_specs=[_rows(tm, D)] * 4 + [_full((8, D))],
        out_shape=[jax.ShapeDtypeStruct((T, D), F32)] + [jax.ShapeDtypeStruct((T, D), BF16)] * 3
                  + [jax.ShapeDtypeStruct((8, D), F32)],
        scratch_shapes=[pltpu.VMEM((D_FF, D), MXU_DTYPE), pltpu.VMEM((D, D), MXU_DTYPE), pltpu.VMEM((D, PLE), MXU_DTYPE)],
        compiler_params=_params(),
    )(act, z1, p, tgt, w_down, w_g, w_p_t, g1, b1, g2, b2, bg)


def _ffn_bwd(dz2b, gate, ge, vd, w_down, fcw):
    T = dz2b.shape[0]
    tm = min(1024, T)
    CW = FF_CHUNK
    nt = T // tm

    def body(dz_ref, wdn_ref, gate_ref, ge_ref, vd_ref, fcw_ref, dup_ref, dfc_ref, after):
        i = pl.program_id(1)

        @pl.when(i == 0)
        def _():
            after[...] = jnp.zeros((8, CW), F32)
            dfc_ref[...] = jnp.zeros_like(dfc_ref)

        gate = gate_ref[...].astype(F32)
        dact = _mm_nt(dz_ref[...], wdn_ref[...])
        dgc = dact * vd_ref[...].astype(F32)
        edge = after[...]
        dgc1 = _shift_rows(dgc, -1, edge)
        dgc2 = _shift_rows(dgc, -2, edge)
        after[...] = dgc[0:8, :]
        dup_ref[:, 0:CW] = (fcw_ref[2:3, :] * dgc + fcw_ref[1:2, :] * dgc1 + fcw_ref[0:1, :] * dgc2).astype(BF16)
        dup_ref[:, CW:2 * CW] = (dact * ge_ref[...].astype(F32)).astype(BF16)
        dfc_ref[0:1, :] += _colsum(dgc2 * gate)
        dfc_ref[1:2, :] += _colsum(dgc1 * gate)
        dfc_ref[2:3, :] += _colsum(dgc * gate)
        dfc_ref[3:4, :] += _colsum(dgc)

    rev = lambda c, i: (c, nt - 1 - i, 0)
    chunk = pl.BlockSpec((None, tm, CW), rev)
    return pl.pallas_call(
        body, name="ffn_bwd", grid=(NC, nt),
        in_specs=[pl.BlockSpec((tm, D), lambda c, i: (nt - 1 - i, 0)), pl.BlockSpec((CW, D), lambda c, i: (c, 0)),
                  chunk, chunk, chunk, pl.BlockSpec((None, 3, CW), lambda c, i: (c, 0, 0))],
        out_specs=[pl.BlockSpec((tm, 2 * CW), lambda c, i: (nt - 1 - i, c)),
                   pl.BlockSpec((None, 8, CW), lambda c, i: (c, 0, 0))],
        out_shape=[jax.ShapeDtypeStruct((T, 2 * D_FF), BF16), jax.ShapeDtypeStruct((NC, 8, CW), F32)],
        scratch_shapes=[pltpu.VMEM((8, CW), F32)],
        compiler_params=_params(),
    )(dz2b, w_down, gate, ge, vd, fcw)


def _ffn_dh1(dup, dz2, dpre, z1, w_up_t, w_g, g1, b1):
    T = z1.shape[0]
    tm = 512

    def body(dup_ref, dz2_ref, dpre_ref, z_ref, wup_hbm, wg_hbm, g1_ref, b1_ref, dz1_ref, vec_ref, wup, wg):
        @pl.when(pl.program_id(0) == 0)
        def _():
            for c in range(NC):
                for s in range(2):
                    pltpu.sync_copy(wup_hbm.at[pl.ds(s * D_FF + c * FF_CHUNK, FF_CHUNK)],
                                    wup.at[pl.ds((2 * c + s) * FF_CHUNK, FF_CHUNK)])
            pltpu.sync_copy(wg_hbm, wg)
            vec_ref[...] = jnp.zeros_like(vec_ref)

        g1v = g1_ref[...]
        _, xh1, rstd1 = _ln(z_ref[...], g1v, b1_ref[...])
        dh1 = ALPHA * dz2_ref[...] + _mm_nt(dpre_ref[...], wg[...]) + _mm(dup_ref[...], wup[...])
        dz1_ref[...] = _ln_bwd(dh1, xh1, rstd1, g1v)
        vec_ref[0:1, :] += _colsum(dh1 * xh1)
        vec_ref[1:2, :] += _colsum(dh1)

    anyspec = pl.BlockSpec(memory_space=pl.ANY)
    vec = _full((1, D))
    return pl.pallas_call(
        body, name="ffn_dh1", grid=(T // tm,),
        in_specs=[_rows(tm, 2 * D_FF), _rows(tm, D), _rows(tm, D), _rows(tm, D),
                  anyspec, anyspec, vec, vec],
        out_specs=[_rows(tm, D), _full((8, D))],
        out_shape=[jax.ShapeDtypeStruct((T, D), F32), jax.ShapeDtypeStruct((8, D), F32)],
        scratch_shapes=[pltpu.VMEM((2 * D_FF, D), MXU_DTYPE), pltpu.VMEM((D, D), MXU_DTYPE)],
        compiler_params=_params(),
    )(dup, dz2, dpre, z1, w_up_t, w_g, g1, b1)


def _out_proj_bwd(dz1, w_out, exchange=None):
    T = dz1.shape[0]
    tm = min(1024, T)

    def body(dz_ref, w_ref, datt_ref, drec_ref):
        dzb = dz_ref[...].astype(MXU_DTYPE)
        datt = _mm_nt(dzb, w_ref[0:512, :])
        for h in range(HEADS):
            datt_ref[h] = datt[:, h * 64:(h + 1) * 64].astype(BF16)
        drec_ref[...] = _mm_nt(dzb, w_ref[512:1024, :])

    return _launch(body, "out_proj_bwd", (T // tm,), [_rows(tm, D), _full((D, D))], [_heads(tm), _rows(tm, 512)],
                   [jax.ShapeDtypeStruct((HEADS, T, 64), BF16), jax.ShapeDtypeStruct((T, 512), F32)], [],
                   (dz1, w_out), exchange)


def _in_proj_bwd(dq, dkv, dxr, dgr, dz1, w_in_t, exchange=None):
    T = dz1.shape[0]
    tm = 512
    W = D_IN // 4

    def body(dq_ref, dkv_ref, dxr_ref, dgr_ref, dz_ref, w_ref, dx_ref, du_ref):
        dkv = dkv_ref[...]
        dx_ref[...] = (ALPHA * dz_ref[...] + _mm(dq_ref[...], w_ref[0:512, :]) + _mm(dkv, w_ref[512:768, :])
                       + _mm(dxr_ref[...], w_ref[768:1280, :]) + _mm(dgr_ref[...], w_ref[1280:1792, :]))
        dq, dxr, dgr = dq_ref[...].astype(F32), dxr_ref[...].astype(F32), dgr_ref[...].astype(F32)
        du_ref[0] = dq[:, 0:W].astype(BF16)
        du_ref[1, :, 0:64] = dq[:, W:512].astype(BF16)
        du_ref[1, :, 64:320] = dkv.astype(BF16)
        du_ref[1, :, 320:W] = dxr[:, 0:128].astype(BF16)
        du_ref[2, :, 0:384] = dxr[:, 128:512].astype(BF16)
        du_ref[2, :, 384:W] = dgr[:, 0:64].astype(BF16)
        du_ref[3] = dgr[:, 64:512].astype(BF16)

    return _launch(body, "in_proj_bwd", (T // tm,),
                   [_rows(tm, 512), _rows(tm, 256), _rows(tm, 512), _rows(tm, 512), _rows(tm, D), _full((D_IN, D))],
                   [_rows(tm, D), pl.BlockSpec((4, tm, W), lambda i: (0, i, 0))],
                   [jax.ShapeDtypeStruct((T, D), F32), jax.ShapeDtypeStruct((4, T, W), BF16)], [],
                   (dq, dkv, dxr, dgr, dz1, w_in_t), exchange)


def _accumulate_tn(a_ref, b_ref, o_ref):
    @pl.when(pl.program_id(1) == 0)
    def _():
        o_ref[...] = jnp.zeros_like(o_ref)

    o_ref[...] += _mm_tn(a_ref[...], b_ref[...])


def _weight_grad_cols(a, b, name, n_blocks, b_spec, out_shape, out_spec, exchange=None):
    T, M = a.shape
    bt = min(DW_TOKENS, T)
    return _launch(functools.partial(_accumulate_tn), name, (n_blocks, T // bt),
                   [pl.BlockSpec((bt, M), lambda m, k: (k, 0)), b_spec(bt)], [out_spec],
                   [jax.ShapeDtypeStruct(out_shape, F32)], [], (a, b), exchange)


def _dw_out(att, rec, dz1):
    T = dz1.shape[0]
    bt = min(DW_TOKENS // 2, T)

    def body(att_ref, rec_ref, dz_ref, o_ref):
        @pl.when(pl.program_id(0) == 0)
        def _():
            o_ref[...] = jnp.zeros_like(o_ref)

        dz = dz_ref[...].astype(MXU_DTYPE)
        o_ref[0:512, :] += _mm_tn(att_ref[...], dz)
        o_ref[512:1024, :] += _mm_tn(rec_ref[...], dz)

    return pl.pallas_call(
        body, name="dw_out", grid=(T // bt,), in_specs=[_rows(bt, 512), _rows(bt, 512), _rows(bt, D)],
        out_specs=_full((D, D)), out_shape=jax.ShapeDtypeStruct((D, D), F32), compiler_params=_params())(att, rec, dz1)


def _weight_grad(a, b, bm, name, exchange=None):
    bt = min(DW_TOKENS // 2 if b.dtype == F32 else DW_TOKENS, b.shape[0])
    if a.ndim == 3:
        assert a.shape[2] == bm
        T, M = a.shape[1], a.shape[0] * bm
        a_spec = pl.BlockSpec((None, bt, bm), lambda m, k: (m, k, 0))
    else:
        T, M = a.shape
        a_spec = pl.BlockSpec((bt, bm), lambda m, k: (k, m))
    N = b.shape[1]
    nk = T // bt

    out = _launch(functools.partial(_accumulate_tn), name, (M // bm, nk),
                  [a_spec, pl.BlockSpec((bt, N), lambda m, k: (k, 0))], [pl.BlockSpec((bm, N), lambda m, k: (m, 0))],
                  [jax.ShapeDtypeStruct((M, N), F32)], [], (a, b), exchange)
    return out[0] if exchange is None else out


def _adamw(w, g, m, v, name):
    R, C = w.shape
    tr = R // 8 if R % 64 == 0 else R
    c1 = 1.0 / (1.0 - ADAM_B1 ** ADAM_STEP)
    c2 = 1.0 / (1.0 - ADAM_B2 ** ADAM_STEP)

    def body(w_ref, g_ref, m_ref, v_ref, d_ref, nm_ref, nv_ref):
        g = g_ref[...]
        nm = ADAM_B1 * m_ref[...] + (1.0 - ADAM_B1) * g
        nv = ADAM_B2 * v_ref[...] + (1.0 - ADAM_B2) * g * g
        nm_ref[...] = nm
        nv_ref[...] = nv
        d_ref[...] = -ADAM_LR * ((nm * c1) / (jnp.sqrt(nv * c2) + ADAM_EPS) + ADAM_WD * w_ref[...])

    spec = pl.BlockSpec((tr, C), lambda i: (i, 0))
    return pl.pallas_call(
        body, name=name, grid=(R // tr,),
        in_specs=[spec] * 4, out_specs=[spec] * 3,
        out_shape=[jax.ShapeDtypeStruct((R, C), F32)] * 3,
        compiler_params=_params(),
    )(w, g, m, v)


def _adamw_halves(ws, mines, sibs, ms, vs, c, name, exchange=None):
    n, nb = len(ws), 4
    c1 = 1.0 / (1.0 - ADAM_B1 ** ADAM_STEP)
    c2 = 1.0 / (1.0 - ADAM_B2 ** ADAM_STEP)

    def body(c_ref, *refs):
        own = (pl.program_id(0) // nb) == c_ref[0]
        for i in range(n):
            w_ref, a_ref, b_ref, m_ref, v_ref = refs[5 * i:5 * i + 5]
            g_ref, d_ref, nm_ref, nv_ref = refs[5 * n + 4 * i:5 * n + 4 * i + 4]
            g = jnp.where(own, a_ref[...], b_ref[...])
            nm = ADAM_B1 * m_ref[...] + (1.0 - ADAM_B1) * g
            nv = ADAM_B2 * v_ref[...] + (1.0 - ADAM_B2) * g * g
            g_ref[...] = g
            nm_ref[...] = nm
            nv_ref[...] = nv
            d_ref[...] = -ADAM_LR * ((nm * c1) / (jnp.sqrt(nv * c2) + ADAM_EPS) + ADAM_WD * w_ref[...])

    in_specs, out_specs, out_shape, args = [], [], [], []
    for w, a, b, m, v in zip(ws, mines, sibs, ms, vs):
        R, C = w.shape
        tr = R // (2 * nb)
        assert tr % 8 == 0 and a.shape == (R // 2, C)
        full = pl.BlockSpec((tr, C), lambda i, c_ref: (i, 0))
        mine_spec = pl.BlockSpec((tr, C), lambda i, c_ref: (jnp.where(i // nb == c_ref[0], i % nb, nb - 1), 0))
        sib_spec = pl.BlockSpec((tr, C), lambda i, c_ref: (jnp.where(i // nb == c_ref[0], nb - 1, i % nb), 0))
        in_specs += [full, mine_spec, sib_spec, full, full]
        out_specs += [full] * 4
        out_shape += [jax.ShapeDtypeStruct((R, C), F32)] * 4
        args += [w, a, b, m, v]
    out = _launch(body, name, (2 * nb,), in_specs, out_specs, out_shape, [], (c, *args), exchange, prefetch=1)
    return [tuple(out[4 * i:4 * i + 4]) for i in range(n)], list(out[4 * n:])


def _add4(fs, name):
    n = len(fs)

    def body(*refs):
        for a_ref, o_ref in zip(refs[:n], refs[n:]):
            o_ref[...] = ((a_ref[0].astype(F32) + a_ref[1].astype(F32)) + a_ref[2].astype(F32)) + a_ref[3].astype(F32)

    for f in fs:
        assert (f.shape[1] // 2) % 16 == 0
    return pl.pallas_call(
        body, name=name, grid=(2,),
        in_specs=[pl.BlockSpec((4, f.shape[1] // 2, f.shape[2]), lambda i: (0, i, 0)) for f in fs],
        out_specs=[pl.BlockSpec((f.shape[1] // 2, f.shape[2]), lambda i: (i, 0)) for f in fs],
        out_shape=[jax.ShapeDtypeStruct(f.shape[1:], F32) for f in fs], compiler_params=_params())(*fs)


def _gather_first(wsrc, cpack):
    def body(w_ref, c_ref, gw_ref, gc_ref, send_sems, recv_sems, local_sem, csend, crecv, clocal):
        x, y, c = _pos()
        me = 2 * x + y
        chips = _other_chips(x, y)
        start, forward, finish = _gather_steps(w_ref, gw_ref, send_sems, recv_sems, local_sem)
        start()
        loc = pltpu.make_async_copy(c_ref, gc_ref.at[me], clocal)
        loc.start()

        def conv_copy(k, slot):
            px, py = chips[k]
            return pltpu.make_async_remote_copy(src_ref=c_ref, dst_ref=gc_ref.at[slot], send_sem=csend.at[k],
                                                recv_sem=crecv.at[k], device_id=(px, py, c), device_id_type=MESH)

        for k in range(3):
            conv_copy(k, me).start()
        forward()
        finish()
        for k, (px, py) in enumerate(chips):
            conv_copy(k, 2 * px + py).wait_recv()
        for k in range(3):
            conv_copy(k, me).wait_send()
        loc.wait()

    anyspec = pl.BlockSpec(memory_space=pl.ANY)
    return pl.pallas_call(
        body, name="gather_first",
        in_specs=[anyspec, anyspec], out_specs=[anyspec, anyspec],
        out_shape=[jax.ShapeDtypeStruct((4,) + wsrc.shape, wsrc.dtype), jax.ShapeDtypeStruct((4,) + cpack.shape, cpack.dtype)],
        scratch_shapes=GATHER_SCRATCH + [pltpu.SemaphoreType.DMA((3,)), pltpu.SemaphoreType.DMA((3,)), pltpu.SemaphoreType.DMA],
        compiler_params=_params(has_side_effects=True),
    )(wsrc, cpack)


def _all_devices_exchange(s):
    def make(ins, outs, sems):
        s_ref, o_ref = ins[0], outs[0]
        send_sems, recv_sems, local_sem = sems
        x, y, c = _pos()
        me = 4 * x + 2 * y + c
        loc = pltpu.make_async_copy(s_ref, o_ref.at[me], local_sem)

        def copy(k, slot):
            peer = (x ^ (k >> 2), y ^ ((k >> 1) & 1), c ^ (k & 1))
            return pltpu.make_async_remote_copy(src_ref=s_ref, dst_ref=o_ref.at[slot], send_sem=send_sems.at[k - 1],
                                                recv_sem=recv_sems.at[k - 1], device_id=peer, device_id_type=MESH)

        def start():
            loc.start()
            for k in range(1, 8):
                copy(k, me).start()

        def finish():
            for k in range(1, 8):
                copy(k, 4 * (x ^ (k >> 2)) + 2 * (y ^ ((k >> 1) & 1)) + (c ^ (k & 1))).wait_recv()
            for k in range(1, 8):
                copy(k, me).wait_send()
            loc.wait()

        return start, lambda: None, finish

    return _Exchange([s], [jax.ShapeDtypeStruct((8,) + s.shape, s.dtype)],
                     [pltpu.SemaphoreType.DMA((7,)), pltpu.SemaphoreType.DMA((7,)), pltpu.SemaphoreType.DMA], make)


def _sum_devices(a):
    def body(a_ref, o_ref):
        acc = a_ref[0]
        for d in range(1, 8):
            acc = acc + a_ref[d]
        o_ref[...] = acc

    vm = pl.BlockSpec(memory_space=pltpu.VMEM)
    return pl.pallas_call(body, name="sum_devices", in_specs=[vm], out_specs=vm,
                          out_shape=jax.ShapeDtypeStruct(a.shape[1:], F32), compiler_params=_params())(a)


def _swap_exchange(gs):
    n = len(gs)

    def make(ins, outs, sems):
        x, y, c = _pos()
        cps = []
        for i in range(n):
            half = gs[i].shape[1] // 2
            rows = pl.ds(pl.multiple_of((1 - c) * half, 8), half)
            cps.append(pltpu.make_async_remote_copy(src_ref=ins[i].at[:, rows, :], dst_ref=outs[i], send_sem=sems[0].at[i],
                                                    recv_sem=sems[1].at[i], device_id=(x, y, 1 - c), device_id_type=MESH))

        def start():
            for cp in cps:
                cp.start()

        def finish():
            for cp in cps:
                cp.wait()

        return start, lambda: None, finish

    return _Exchange(gs, [jax.ShapeDtypeStruct((4, g.shape[1] // 2, g.shape[2]), g.dtype) for g in gs],
                     [pltpu.SemaphoreType.DMA((n,)), pltpu.SemaphoreType.DMA((n,))], make)


def _scatter_exchange(ss):
    n = len(ss)

    def make(ins, outs, sems):
        send_sems, recv_sems, local_sems = sems
        x, y, c = _pos()
        me = 2 * x + y
        chips = _other_chips(x, y)
        locs = [pltpu.make_async_copy(ins[i].at[me], outs[i].at[me], local_sems.at[i]) for i in range(n)]

        def copy(i, k, src_slot, dst_slot):
            px, py = chips[k]
            return pltpu.make_async_remote_copy(src_ref=ins[i].at[src_slot], dst_ref=outs[i].at[dst_slot],
                                                send_sem=send_sems.at[3 * i + k], recv_sem=recv_sems.at[3 * i + k],
                                                device_id=(px, py, c), device_id_type=MESH)

        def start():
            for i in range(n):
                locs[i].start()
                for k, (px, py) in enumerate(chips):
                    copy(i, k, 2 * px + py, me).start()

        def finish():
            for i in range(n):
                for k, (px, py) in enumerate(chips):
                    copy(i, k, me, 2 * px + py).wait_recv()
            for i in range(n):
                for k, (px, py) in enumerate(chips):
                    copy(i, k, 2 * px + py, me).wait_send()
                locs[i].wait()

        return start, lambda: None, finish

    return _Exchange(ss, [jax.ShapeDtypeStruct(s.shape, s.dtype) for s in ss],
                     [pltpu.SemaphoreType.DMA((3 * n,)), pltpu.SemaphoreType.DMA((3 * n,)), pltpu.SemaphoreType.DMA((n,))], make)


def _send_exchange(rs):
    n = len(rs)

    def make(ins, outs, sems):
        x, y, c = _pos()
        cps = [pltpu.make_async_remote_copy(src_ref=ins[i], dst_ref=outs[i], send_sem=sems[0].at[i], recv_sem=sems[1].at[i],
                                            device_id=(x, y, 1 - c), device_id_type=MESH) for i in range(n)]

        def start():
            for cp in cps:
                cp.start()

        def finish():
            for cp in cps:
                cp.wait()

        return start, lambda: None, finish

    return _Exchange(rs, [jax.ShapeDtypeStruct(r.shape, r.dtype) for r in rs],
                     [pltpu.SemaphoreType.DMA((n,)), pltpu.SemaphoreType.DMA((n,))], make)


def _reduce_in_vmem(g):
    _, R, C = g.shape
    H = R // 2

    def body(g_ref, mine_ref, other_ref, sib, part, got, swap_sems, send_sems, recv_sems, last_sems):
        x, y, c = _pos()
        me = 2 * x + y
        chips = _other_chips(x, y)
        sibling = (x, y, 1 - c)
        mine = pl.ds(pl.multiple_of(c * H, 8), H)
        theirs = pl.ds(pl.multiple_of((1 - c) * H, 8), H)
        swap = pltpu.make_async_remote_copy(src_ref=g_ref.at[:, theirs, :], dst_ref=sib, send_sem=swap_sems.at[0],
                                            recv_sem=swap_sems.at[1], device_id=sibling, device_id_type=MESH)
        swap.start()
        swap.wait()
        part[...] = (g_ref[:, mine, :] + sib[...]).astype(BF16)

        def copy(k, src_slot, dst_slot):
            px, py = chips[k]
            return pltpu.make_async_remote_copy(src_ref=part.at[src_slot], dst_ref=got.at[dst_slot], send_sem=send_sems.at[k],
                                                recv_sem=recv_sems.at[k], device_id=(px, py, c), device_id_type=MESH)

        for k, (px, py) in enumerate(chips):
            copy(k, 2 * px + py, me).start()
        got[me] = part[me]
        for k, (px, py) in enumerate(chips):
            copy(k, me, 2 * px + py).wait_recv()
        for k, (px, py) in enumerate(chips):
            copy(k, 2 * px + py, me).wait_send()
        mine_ref[...] = ((got[0].astype(F32) + got[1].astype(F32)) + got[2].astype(F32)) + got[3].astype(F32)
        last = pltpu.make_async_remote_copy(src_ref=mine_ref, dst_ref=other_ref, send_sem=last_sems.at[0],
                                            recv_sem=last_sems.at[1], device_id=sibling, device_id_type=MESH)
        last.start()
        last.wait()

    vm = pl.BlockSpec(memory_space=pltpu.VMEM)
    half = jax.ShapeDtypeStruct((H, C), F32)
    return pl.pallas_call(
        body, name="reduce_late", in_specs=[vm], out_specs=[vm, vm], out_shape=[half, half],
        scratch_shapes=[pltpu.VMEM((4, H, C), F32), pltpu.VMEM((4, H, C), BF16), pltpu.VMEM((4, H, C), BF16),
                        pltpu.SemaphoreType.DMA((2,)), pltpu.SemaphoreType.DMA((3,)), pltpu.SemaphoreType.DMA((3,)),
                        pltpu.SemaphoreType.DMA((2,))],
        compiler_params=_params(has_side_effects=True))(g)


def _add_half(gs, rs, c, name):
    n = len(gs)

    def body(c_ref, *refs):
        for g_ref, r_ref, o_ref in zip(refs[:n], refs[n:2 * n], refs[2 * n:]):
            o_ref[...] = (g_ref[...] + r_ref[...]).astype(BF16)

    g_specs, r_specs, out_shape = [], [], []
    for g, r in zip(gs, rs):
        _, H, C = r.shape
        tr = H // 2
        assert tr % 16 == 0 and g.shape == (4, 2 * H, C)
        g_specs.append(pl.BlockSpec((1, tr, C), lambda j, i, c_ref: (j, c_ref[0] * 2 + i, 0)))
        r_specs.append(pl.BlockSpec((1, tr, C), lambda j, i, c_ref: (j, i, 0)))
        out_shape.append(jax.ShapeDtypeStruct((4, H, C), BF16))
    grid_spec = pltpu.PrefetchScalarGridSpec(num_scalar_prefetch=1, grid=(4, 2), in_specs=g_specs + r_specs, out_specs=r_specs)
    return pl.pallas_call(body, name=name, grid_spec=grid_spec, out_shape=out_shape, compiler_params=_params())(c, *gs, *rs)


def _block_diag(w):
    eye = jnp.eye(RNN_BLOCKS, dtype=w.dtype)
    return (eye[:, None, :, None] * w[:, :, None, :]).reshape(D_RNN, D_RNN)


def _diag_blocks(wd):
    d = wd.reshape(RNN_BLOCKS, 64, RNN_BLOCKS, 64)
    return jnp.stack([d[h, :, h, :] for h in range(RNN_BLOCKS)])


def _split_pack(a, first, last):
    out, base = {}, PACK_OFF[first]
    for i in range(first, last):
        s = a[:, PACK_OFF[i] - base:PACK_OFF[i + 1] - base]
        out[BIG_KEYS[i]] = s.reshape(4 * 256, 256) if BIG_KEYS[i] == "w_p_t" else s.reshape(-1, 1024)
    return out


def _layer_grads(x, p, tgt, gw, small, shard=None, core=None):
    row = lambda v: v.reshape(1, -1)
    wa = _block_diag(small["gate_a_w"]).astype(MXU_DTYPE)
    wx = _block_diag(small["gate_x_w"]).astype(MXU_DTYPE)
    sinks = small["attn_sinks"].reshape(1, HEADS)

    dist = shard is not None
    q, kv, xr, gr, xb = _in_proj(x, gw["w_in_t"])
    cut = PACK_OFF[1] + PACK_ROWS[1] // 2
    att, *ga = _attn_fwd(q, kv, sinks, _gather_exchange(shard[PACK_OFF[1]:cut]) if dist else None)
    xc, h, rec, *gb = _rnn_fwd(xr, gr, small["rnn_conv_w"], row(small["rnn_conv_b"]), wa, row(small["gate_a_b"]),
                               wx, row(small["gate_x_b"]), row(small["lru_lambda"]),
                               _gather_exchange(shard[cut:PACK_OFF[3]]) if dist else None)
    if dist:
        gw = {**gw, **_split_pack(jnp.concatenate([ga[0], gb[0]], axis=1), 1, 3)}
    g1, b1 = row(small["ln1_g"]), row(small["ln1_b"])
    fcw = small["ffn_conv_w"].reshape(3, NC, FF_CHUNK).transpose(1, 0, 2)
    fcb = small["ffn_conv_b"].reshape(NC, 1, FF_CHUNK)
    z1, h1b = _out_proj(att, rec, x, gw["w_out"], g1, b1)
    gate, ge, vd, act, *gc = _ffn_up(h1b, gw["w_up_t"], fcw, fcb,
                                     _gather_exchange(shard[PACK_OFF[3]:PACK_OFF[6]]) if dist else None)
    if dist:
        gw = {**gw, **_split_pack(gc[0], 3, 6)}
    dz2, dz2b, dpre, dpp, vec2 = _ffn_down(act, z1, p, tgt, gw["w_down"], gw["w_g"], gw["w_p_t"], g1, b1,
                                           row(small["ln2_g"]), row(small["ln2_b"]), row(small["ple_gate_b"]))
    dup, dfc = _ffn_bwd(dz2b, gate, ge, vd, gw["w_down"], fcw)
    dz1, vec1 = _ffn_dh1(dup, dz2, dpre, z1, gw["w_up_t"], gw["w_g"], g1, b1)
    per_chip = 2 * D_FF // 4 // FF_CHUNK
    big = {"w_ffn_up": _weight_grad_cols(
        h1b, dup, "dw_up", 2 * NC, lambda bt: pl.BlockSpec((bt, FF_CHUNK), lambda m, k: (k, m)), (4, D, 2 * D_FF // 4),
        pl.BlockSpec((None, D, FF_CHUNK), lambda m, k: (2 * (m % 2) + (m // 2) // per_chip, 0, (m // 2) % per_chip)))[0]}
    g_dn, *got_up = _weight_grad(act, dz2b, FF_CHUNK, "dw_down", _swap_exchange([big["w_ffn_up"]])) if dist else (
        _weight_grad(act, dz2b, FF_CHUNK, "dw_down"),)
    big["w_ffn_down"] = g_dn.reshape(4, D_FF // 4, D)
    big["ple_gate_w"] = _weight_grad(h1b, dpre, 512, "dw_gate").reshape(4, D // 4, D)
    big["ple_proj"] = _weight_grad(p, dpp, PLE, "dw_proj").reshape(PLE, 4, D // 4).transpose(1, 0, 2)
    big["w_out"] = _dw_out(att, rec, dz1).reshape(4, D // 4, D)
    reduced = None
    if dist:
        g_ffn = [big[k] for k in EARLY_WEIGHTS]
        ex = _swap_exchange(g_ffn[1:])
    datt, drec, *got = _out_proj_bwd(dz1, gw["w_out"], ex if dist else None)
    if dist:
        sums = _add_half(g_ffn, got_up + got, core, "add_half_ffn")
        ex, ex2 = _scatter_exchange(sums[:1]), _scatter_exchange(sums[1:])
    dxr, dgr, dwa, dwx, dvec, *got = _rnn_bwd(drec, gr, h, xc, xr, small["rnn_conv_w"], wa, row(small["gate_a_b"]),
                                              wx, row(small["gate_x_b"]), row(small["lru_lambda"]), ex if dist else None)
    dq, dkv, dsinks, *got2 = _attn_bwd(q, kv, datt, sinks, ex2 if dist else None)
    if dist:
        mine = _add4(got + got2, "add_chips_ffn")
        big = {}
    sg = {
        "attn_sinks": dsinks[:, 0],
        "rnn_conv_w": dvec[4:8],
        "rnn_conv_b": dvec[3],
        "gate_a_w": _diag_blocks(dwa),
        "gate_a_b": dvec[0],
        "gate_x_w": _diag_blocks(dwx),
        "gate_x_b": dvec[1],
        "lru_lambda": dvec[2],
        "ln1_g": vec1[0],
        "ln1_b": vec1[1],
        "ffn_conv_w": dfc[:, 0:3].transpose(1, 0, 2).reshape(3, D_FF),
        "ffn_conv_b": dfc[:, 3].reshape(D_FF),
        "ple_gate_b": vec2[3],
        "ln2_g": vec2[1],
        "ln2_b": vec2[2],
    }
    loss = vec2[0, 0:1]
    grad_x, du = _in_proj_bwd(dq, dkv, dxr, dgr, dz1, gw["w_in_t"])
    ex = None
    if dist:
        ex = _join_exchanges(_send_exchange(mine), _all_devices_exchange(_pack_vecs([sg[k] for k in SMALL] + [loss])[0]))
    big["w_in"], *got = _weight_grad_cols(
        xb, du, "dw_in", 4, lambda bt: pl.BlockSpec((None, bt, D_IN // 4), lambda j, k: (j, k, 0)), (4, D, D_IN // 4),
        pl.BlockSpec((None, D, D_IN // 4), lambda j, k: (j, 0, 0)), ex)
    if dist:
        reduced = (mine, got[:len(mine)])
    return grad_x, big, sg, loss, reduced, got[-1:]


BIG = ("w_in", "w_ffn_up", "w_out", "w_ffn_down", "ple_gate_w", "ple_proj")
BIG_KEYS = ("w_in_t", "w_up_t", "w_out", "w_down", "w_g", "w_p_t")
BIG_T = (True, True, False, False, False, True)
EARLY_WEIGHTS = ("w_ffn_up", "w_ffn_down", "ple_gate_w", "ple_proj", "w_out")
LATE_WEIGHTS = ("w_in",)
SMALL = ("attn_sinks", "rnn_conv_w", "rnn_conv_b", "gate_a_w", "gate_a_b", "gate_x_w", "gate_x_b", "lru_lambda",
         "ln1_g", "ln1_b", "ffn_conv_w", "ffn_conv_b", "ple_gate_b", "ln2_g", "ln2_b")
SHARDED_SMALL = ("rnn_conv_w", "ffn_conv_w")
WEIGHTS = ("w_in", "attn_sinks", "rnn_conv_w", "rnn_conv_b", "gate_a_w", "gate_a_b", "gate_x_w", "gate_x_b",
           "lru_lambda", "w_out", "ln1_g", "ln1_b", "w_ffn_up", "ffn_conv_w", "ffn_conv_b", "w_ffn_down",
           "ple_gate_w", "ple_gate_b", "ple_proj", "ln2_g", "ln2_b")


def _pack_big(d, first=0, last=6):
    parts = []
    for name, t in zip(BIG[first:last], BIG_T[first:last]):
        a = d[name]
        a = a.T if t else a
        parts.append(a.reshape(-1, 1024))
    return jnp.concatenate(parts, axis=0)


def _pack_vecs(items):
    parts, offs, n = [], [], 0
    for a in items:
        f = a.reshape(-1).astype(F32)
        pad = (-f.shape[0]) % 128
        parts.append(jnp.pad(f, (0, pad)))
        offs.append(n)
        n += (f.shape[0] + pad) // 128
    padr = (-n) % 8
    if padr:
        parts.append(jnp.zeros((padr * 128,), F32))
    return jnp.concatenate(parts).reshape(-1, 128), offs


def _unpack_vecs(a, offs, shapes):
    flat = a.reshape(-1)
    out = []
    for o, s in zip(offs, shapes):
        n = 1
        for d in s:
            n *= d
        out.append(flat[o * 128:o * 128 + n].reshape(s))
    return out


def kernel(x, p, w_in, attn_sinks, rnn_conv_w, rnn_conv_b, gate_a_w, gate_a_b, gate_x_w, gate_x_b, lru_lambda, w_out, ln1_g, ln1_b, w_ffn_up, ffn_conv_w, ffn_conv_b, w_ffn_down, ple_gate_w, ple_gate_b, ple_proj, ln2_g, ln2_b, loss_target, m_w_in, m_attn_sinks, m_rnn_conv_w, m_rnn_conv_b, m_gate_a_w, m_gate_a_b, m_gate_x_w, m_gate_x_b, m_lru_lambda, m_w_out, m_ln1_g, m_ln1_b, m_w_ffn_up, m_ffn_conv_w, m_ffn_conv_b, m_w_ffn_down, m_ple_gate_w, m_ple_gate_b, m_ple_proj, m_ln2_g, m_ln2_b, v_w_in, v_attn_sinks, v_rnn_conv_w, v_rnn_conv_b, v_gate_a_w, v_gate_a_b, v_gate_x_w, v_gate_x_b, v_lru_lambda, v_w_out, v_ln1_g, v_ln1_b, v_w_ffn_up, v_ffn_conv_w, v_ffn_conv_b, v_w_ffn_down, v_ple_gate_w, v_ple_gate_b, v_ple_proj, v_ln2_g, v_ln2_b):
    w = dict(w_in=w_in, attn_sinks=attn_sinks, rnn_conv_w=rnn_conv_w, rnn_conv_b=rnn_conv_b, gate_a_w=gate_a_w,
             gate_a_b=gate_a_b, gate_x_w=gate_x_w, gate_x_b=gate_x_b, lru_lambda=lru_lambda, w_out=w_out, ln1_g=ln1_g,
             ln1_b=ln1_b, w_ffn_up=w_ffn_up, ffn_conv_w=ffn_conv_w, ffn_conv_b=ffn_conv_b, w_ffn_down=w_ffn_down,
             ple_gate_w=ple_gate_w, ple_gate_b=ple_gate_b, ple_proj=ple_proj, ln2_g=ln2_g, ln2_b=ln2_b)
    m = dict(w_in=m_w_in, attn_sinks=m_attn_sinks, rnn_conv_w=m_rnn_conv_w, rnn_conv_b=m_rnn_conv_b, gate_a_w=m_gate_a_w,
             gate_a_b=m_gate_a_b, gate_x_w=m_gate_x_w, gate_x_b=m_gate_x_b, lru_lambda=m_lru_lambda, w_out=m_w_out,
             ln1_g=m_ln1_g, ln1_b=m_ln1_b, w_ffn_up=m_w_ffn_up, ffn_conv_w=m_ffn_conv_w, ffn_conv_b=m_ffn_conv_b,
             w_ffn_down=m_w_ffn_down, ple_gate_w=m_ple_gate_w, ple_gate_b=m_ple_gate_b, ple_proj=m_ple_proj,
             ln2_g=m_ln2_g, ln2_b=m_ln2_b)
    v = dict(w_in=v_w_in, attn_sinks=v_attn_sinks, rnn_conv_w=v_rnn_conv_w, rnn_conv_b=v_rnn_conv_b, gate_a_w=v_gate_a_w,
             gate_a_b=v_gate_a_b, gate_x_w=v_gate_x_w, gate_x_b=v_gate_x_b, lru_lambda=v_lru_lambda, w_out=v_w_out,
             ln1_g=v_ln1_g, ln1_b=v_ln1_b, w_ffn_up=v_w_ffn_up, ffn_conv_w=v_ffn_conv_w, ffn_conv_b=v_ffn_conv_b,
             w_ffn_down=v_w_ffn_down, ple_gate_w=v_ple_gate_w, ple_gate_b=v_ple_gate_b, ple_proj=v_ple_proj,
             ln2_g=v_ln2_g, ln2_b=v_ln2_b)
    w, m, v = ({k: a[0] for k, a in d.items()} for d in (w, m, v))
    chip = 2 * lax.axis_index("x") + lax.axis_index("y")
    core = lax.axis_index("c")

    wpack = _pack_big(w)
    cpack, _ = _pack_vecs([w["rnn_conv_w"], w["ffn_conv_w"]])
    shard = wpack.astype(MXU_DTYPE)
    g_in, gcp = _gather_first(shard[PACK_OFF[0]:PACK_OFF[1]], cpack)
    gw = _split_pack(g_in, 0, 1)
    small = {k: w[k] for k in SMALL}
    small["rnn_conv_w"] = gcp[:, 0:4].reshape(4, 4, 128).transpose(1, 0, 2).reshape(4, 512)
    small["ffn_conv_w"] = gcp[:, 4:22].reshape(4, 3, 768).transpose(1, 0, 2).reshape(3, 3072)

    core1 = core.reshape(1).astype(jnp.int32)
    grad_x, big, sg, loss, ffn_halves, small_all = _layer_grads(x[0], p[0, 0], loss_target[0], gw, small, shard, core1)

    shapes = [sg[k].shape for k in SMALL] + [(1,)]
    _, offs = _pack_vecs([jnp.zeros(s, F32) for s in shapes])
    red = dict(zip(SMALL + ("loss",), _unpack_vecs(_sum_devices(small_all[0]), offs, shapes)))
    red["rnn_conv_w"] = lax.dynamic_slice_in_dim(red["rnn_conv_w"], chip * 128, 128, axis=1)
    red["ffn_conv_w"] = lax.dynamic_slice_in_dim(red["ffn_conv_w"], chip * 768, 768, axis=1)

    late_mine, late_other = ([a] for a in _reduce_in_vmem(big["w_in"]))

    def adamw(names, mine, other, name):
        out, _ = _adamw_halves([w[k] for k in names], mine, other, [m[k] for k in names], [v[k] for k in names],
                               core1, name)
        return dict(zip(names, out))

    big_out = {**adamw(LATE_WEIGHTS, late_mine, late_other, "adamw_late"), **adamw(EARLY_WEIGHTS, *ffn_halves, "adamw_early")}
    wsm, offs2 = _pack_vecs([w[k] for k in SMALL])
    gsm, _ = _pack_vecs([red[k] for k in SMALL])
    msm, _ = _pack_vecs([m[k] for k in SMALL])
    vsm, _ = _pack_vecs([v[k] for k in SMALL])
    dsm, nmsm, nvsm = _adamw(wsm, gsm, msm, vsm, "adamw_small")
    shapes2 = [w[k].shape for k in SMALL]

    def named(n, smallp):
        d = {k: out[n][None] for k, out in big_out.items()}
        d.update({k: a[None] for k, a in zip(SMALL, _unpack_vecs(smallp, offs2, shapes2))})
        return [d[k] for k in WEIGHTS]

    return (red["loss"].reshape(()), grad_x[None], *named(0, gsm), *named(1, dsm), *named(2, nmsm), *named(3, nvsm))
```

```python
import functools

import jax
import jax.numpy as jnp
from jax import lax
from jax.experimental import pallas as pl
from jax.experimental.pallas import tpu as pltpu

F32 = jnp.float32
BF16 = jnp.bfloat16
MXU_DTYPE = jnp.bfloat16

D = 1024
D_ATT = 512
D_KV = 128
D_RNN = 512
D_IN = 1792
D_FF = 3072
FF_CHUNK = 768
PLE = 256
HEADS = 8
HEAD_DIM = 64
BLK = 128
ATTN_BLOCKS = 8
DW_TOKENS = 4096
RNN_BLOCKS = 8
LN_EPS = 1e-5
LRU_C = 8.0
ALPHA = float(2.0 ** 0.25)
SCALE = HEAD_DIM ** -0.5
NEG = -1e30

ADAM_LR = 0.001
ADAM_B1 = 0.9
ADAM_B2 = 0.999
ADAM_EPS = 1e-08
ADAM_WD = 0.01
ADAM_STEP = 10

VMEM_LIMIT_BYTES = 56 * 1024 * 1024
MESH = pl.DeviceIdType.MESH

PACK_ROWS = (448, 1536, 256, 768, 256, 64)
PACK_OFF = tuple(sum(PACK_ROWS[:i]) for i in range(len(PACK_ROWS) + 1))
PACK_TOTAL = PACK_OFF[-1]


def _params(**kw):
    return pltpu.CompilerParams(vmem_limit_bytes=VMEM_LIMIT_BYTES, **kw)


def _mm(a, b):
    return jnp.dot(a.astype(MXU_DTYPE), b.astype(MXU_DTYPE), preferred_element_type=F32)


def _mm_nt(a, b):
    return lax.dot_general(a.astype(MXU_DTYPE), b.astype(MXU_DTYPE), (((1,), (1,)), ((), ())),
                           preferred_element_type=F32)


def _mm_tn(a, b):
    return lax.dot_general(a.astype(MXU_DTYPE), b.astype(MXU_DTYPE), (((0,), (0,)), ((), ())),
                           preferred_element_type=F32)


def _sigmoid(x):
    return 0.5 + 0.5 * jnp.tanh(0.5 * x)


def _gelu(x):
    c = 0.7978845608028654
    k = 0.044715
    x2 = x * x
    t = jnp.tanh(x * (c + (c * k) * x2))
    h = 0.5 * (1.0 + t)
    return x * h, h * (1.0 + (x * (1.0 - t)) * (c + (3.0 * c * k) * x2))


def _shift_rows(x, s, edge8):
    R = x.shape[0]
    row8 = lax.broadcasted_iota(jnp.int32, (8, x.shape[1]), 0)
    if s > 0:
        rolled = pltpu.roll(x, s, 0)
        first = jnp.where(row8 < s, pltpu.roll(edge8, s, 0), rolled[0:8])
        return jnp.concatenate([first, rolled[8:]], axis=0)
    k = -s
    rolled = pltpu.roll(x, R - k, 0)
    last = jnp.where(row8 >= 8 - k, pltpu.roll(edge8, 8 - k, 0), rolled[R - 8:])
    return jnp.concatenate([rolled[:R - 8], last], axis=0)


def _softplus(x):
    return jnp.maximum(x, 0.0) + jnp.log(1.0 + jnp.exp(-jnp.abs(x)))


def _ln(z, g, b):
    mu = jnp.mean(z, axis=-1, keepdims=True)
    zc = z - mu
    var = jnp.mean(zc * zc, axis=-1, keepdims=True)
    rstd = lax.rsqrt(var + LN_EPS)
    xhat = zc * rstd
    return xhat * g + b, xhat, rstd


def _ln_bwd(dy, xhat, rstd, g):
    dxh = dy * g
    m1 = jnp.mean(dxh, axis=-1, keepdims=True)
    m2 = jnp.mean(dxh * xhat, axis=-1, keepdims=True)
    return rstd * (dxh - m1 - xhat * m2)


def _colsum(x):
    return jnp.sum(x, axis=0, keepdims=True)


def _full(shape):
    nd = len(shape)
    return pl.BlockSpec(shape, lambda *_: (0,) * nd)


def _rows(tm, cols, fn=None):
    if fn is None:
        return pl.BlockSpec((tm, cols), lambda i: (i, 0))
    return pl.BlockSpec((tm, cols), lambda i: (fn(i), 0))


def _heads(tm):
    return pl.BlockSpec((HEADS, tm, HEAD_DIM), lambda i: (0, i, 0))


def _in_proj(x, w_in_t):
    T = x.shape[0]
    tm = min(1024, T)

    def body(x_ref, w_ref, q_ref, kv_ref, xr_ref, gr_ref, xb_ref):
        xb = x_ref[...].astype(MXU_DTYPE)
        xb_ref[...] = xb.astype(BF16)
        q = _mm_nt(xb, w_ref[0:512, :])
        for h in range(HEADS):
            q_ref[h] = q[:, h * 64:(h + 1) * 64].astype(BF16)
        kv_ref[...] = _mm_nt(xb, w_ref[512:768, :]).astype(BF16)
        xr_ref[...] = _mm_nt(xb, w_ref[768:1280, :])
        gr_ref[...] = _mm_nt(xb, w_ref[1280:1792, :])

    return pl.pallas_call(
        body, name="in_proj", grid=(T // tm,),
        in_specs=[_rows(tm, D), _full((D_IN, D))],
        out_specs=[_heads(tm), _rows(tm, 256), _rows(tm, 512), _rows(tm, 512), _rows(tm, D)],
        out_shape=[jax.ShapeDtypeStruct((HEADS, T, 64), BF16), jax.ShapeDtypeStruct((T, 256), BF16),
                   jax.ShapeDtypeStruct((T, 512), F32), jax.ShapeDtypeStruct((T, 512), F32),
                   jax.ShapeDtypeStruct((T, D), BF16)],
        compiler_params=_params(),
    )(x, w_in_t)


def _attn_band(kv_ref, i):
    cur = pl.multiple_of(i * BLK, BLK)
    prev = pl.multiple_of(jnp.maximum(i - 1, 0) * BLK, BLK)
    band = jnp.concatenate([kv_ref[pl.ds(prev, BLK), :], kv_ref[pl.ds(cur, BLK), :]], axis=0)
    key = lax.broadcasted_iota(jnp.int32, (2 * BLK, 4 * BLK), 0)
    qry = lax.broadcasted_iota(jnp.int32, (2 * BLK, 4 * BLK), 1) & (BLK - 1)
    in_prev = jnp.logical_and(jnp.logical_and(key < BLK, key > qry), i > 0)
    mask = jnp.logical_or(in_prev, jnp.logical_and(key >= BLK, key - BLK <= qry))
    return band, mask, cur, prev


def _attn_scores(band, mask, qs, s_ref, g):
    st = jnp.where(mask, _mm_nt(band[:, g * 64:(g + 1) * 64], qs) * SCALE, NEG)
    lane = lax.broadcasted_iota(jnp.int32, (1, 4 * BLK), 1)
    sv = jnp.where(lane < BLK, s_ref[0, 4 * g],
                   jnp.where(lane < 2 * BLK, s_ref[0, 4 * g + 1], jnp.where(lane < 3 * BLK, s_ref[0, 4 * g + 2], s_ref[0, 4 * g + 3])))
    m = jnp.maximum(jnp.max(st, axis=0, keepdims=True), sv)
    p = jnp.exp(st - m)
    ps = jnp.exp(sv - m)
    return p, ps, jnp.sum(p, axis=0, keepdims=True) + ps


def _pos():
    return lax.axis_index("x"), lax.axis_index("y"), lax.axis_index("c")


def _other_chips(x, y):
    return [(1 - x, y), (x, 1 - y), (1 - x, 1 - y)]


def _gather_steps(w_ref, gw_ref, send_sems, recv_sems, local_sem):
    x, y, c = _pos()
    me = 2 * x + y
    chips = _other_chips(x, y)
    half = w_ref.shape[0] // 2
    mine = pl.ds(pl.multiple_of(c * half, 16), half)
    theirs = pl.ds(pl.multiple_of((1 - c) * half, 16), half)
    loc = pltpu.make_async_copy(w_ref, gw_ref.at[me], local_sem)

    def copy(k, src, dst, to):
        return pltpu.make_async_remote_copy(src_ref=src, dst_ref=dst, send_sem=send_sems.at[k], recv_sem=recv_sems.at[k],
                                            device_id=to, device_id_type=MESH)

    def out(k):
        px, py = chips[k]
        return copy(k, w_ref.at[mine], gw_ref.at[me, mine], (px, py, c))

    def fwd(k, rows):
        px, py = chips[k]
        return copy(3 + k, gw_ref.at[2 * px + py, rows], gw_ref.at[2 * px + py, rows], (x, y, 1 - c))

    def start():
        loc.start()
        for k in range(3):
            out(k).start()

    def forward():
        for k in range(3):
            px, py = chips[k]
            copy(k, w_ref.at[mine], gw_ref.at[2 * px + py, mine], (px, py, c)).wait_recv()
            fwd(k, mine).start()

    def finish():
        for k in range(3):
            fwd(k, theirs).wait_recv()
        for k in range(3):
            out(k).wait_send()
            fwd(k, mine).wait_send()
        loc.wait()

    return start, forward, finish


GATHER_SCRATCH = [pltpu.SemaphoreType.DMA((6,)), pltpu.SemaphoreType.DMA((6,)), pltpu.SemaphoreType.DMA]


class _Exchange:
    def __init__(self, args, out_shape, scratch, make):
        self.args, self.out_shape, self.scratch, self.make = list(args), list(out_shape), list(scratch), make


def _join_exchanges(a, b):
    na, nao, nas = len(a.args), len(a.out_shape), len(a.scratch)

    def make(ins, outs, sems):
        steps_a = a.make(ins[:na], outs[:nao], sems[:nas])
        steps_b = b.make(ins[na:], outs[nao:], sems[nas:])

        def both(f, g):
            def run():
                f()
                g()
            return run

        return tuple(both(f, g) for f, g in zip(steps_a, steps_b))

    return _Exchange(a.args + b.args, a.out_shape + b.out_shape, a.scratch + b.scratch, make)


def _gather_exchange(wsrc):
    return _Exchange([wsrc], [jax.ShapeDtypeStruct((4,) + wsrc.shape, wsrc.dtype)], GATHER_SCRATCH,
                     lambda ins, outs, sems: _gather_steps(ins[0], outs[0], *sems))


def _launch(body, name, grid, in_specs, out_specs, out_shape, scratch, args, exchange=None, prefetch=0):
    def call(fn, fn_name, ins, outs, shapes, scr, operands, effects):
        spec = pltpu.PrefetchScalarGridSpec(num_scalar_prefetch=prefetch, grid=grid, in_specs=ins, out_specs=outs,
                                            scratch_shapes=scr)
        return pl.pallas_call(fn, name=fn_name, grid_spec=spec, out_shape=shapes,
                              compiler_params=_params(has_side_effects=effects))(*operands)

    if exchange is None:
        return call(body, name, list(in_specs), list(out_specs), list(out_shape), list(scratch), args, False)
    n_in, n_out, ei, eo, ns = len(in_specs), len(out_specs), len(exchange.args), len(exchange.out_shape), len(exchange.scratch)
    nsteps = 1
    for g in grid:
        nsteps *= g

    def wrapped(*refs):
        scalars, refs = refs[:prefetch], refs[prefetch:]
        ins, xin = refs[:n_in], refs[n_in:n_in + ei]
        outs, xout = refs[n_in + ei:n_in + ei + n_out], refs[n_in + ei + n_out:n_in + ei + n_out + eo]
        rest = refs[n_in + ei + n_out + eo:]
        own, sems = rest[:len(rest) - ns], rest[len(rest) - ns:]
        start, forward, finish = exchange.make(xin, xout, sems)
        i = pl.program_id(0)
        for d in range(1, len(grid)):
            i = i * grid[d] + pl.program_id(d)
        pl.when(i == 0)(start)
        body(*scalars, *ins, *outs, *own)
        pl.when(i == max(nsteps - 3, 0))(forward)
        pl.when(i == nsteps - 1)(finish)

    anyspec = pl.BlockSpec(memory_space=pl.ANY)
    return call(wrapped, name + "_x", list(in_specs) + [anyspec] * ei, list(out_specs) + [anyspec] * eo,
                list(out_shape) + exchange.out_shape, list(scratch) + exchange.scratch, (*args, *exchange.args), True)


def _attn_fwd(q, kv, sinks, exchange=None):
    T = kv.shape[0]
    nblk = min(ATTN_BLOCKS, T // BLK)

    def body(q_ref, kv_ref, s_ref, o_ref):
        for b in range(nblk):
            rows = slice(b * BLK, (b + 1) * BLK)
            band, mask, _, _ = _attn_band(kv_ref, nblk * pl.program_id(0) + b)
            for g in range(2):
                qs = q_ref[4 * g:4 * g + 4, rows, :].reshape(4 * BLK, HEAD_DIM)
                p, _, den = _attn_scores(band, mask, qs, s_ref, g)
                ot = _mm_tn(band[:, 128:256], p) * (1.0 / den)
                for hh in range(4):
                    o = ot[:, hh * BLK:(hh + 1) * BLK].T
                    o_ref[rows, (4 * g + hh) * 64:(4 * g + hh + 1) * 64] = o[:, g * 64:(g + 1) * 64].astype(BF16)

    tq = nblk * BLK
    return _launch(body, "attn_fwd", (T // tq,), [_heads(tq), _full((T, 256)), pl.BlockSpec(memory_space=pltpu.SMEM)],
                   [_rows(tq, 512)], [jax.ShapeDtypeStruct((T, 512), BF16)], [], (q, kv, sinks), exchange)


def _attn_bwd(q, kv, do, sinks, exchange=None):
    T = kv.shape[0]
    nblk = min(ATTN_BLOCKS, T // BLK)

    def body(q_ref, kv_ref, do_ref, s_ref, dq_ref, dkv_ref, ds_ref):
        @pl.when(pl.program_id(0) == 0)
        def _():
            ds_ref[...] = jnp.zeros_like(ds_ref)

        for b in range(nblk):
            rows = slice(b * BLK, (b + 1) * BLK)
            band, mask, cur, prev = _attn_band(kv_ref, nblk * pl.program_id(0) + b)
            for g in range(2):
                qs = q_ref[4 * g:4 * g + 4, rows, :].reshape(4 * BLK, HEAD_DIM)
                dos = do_ref[4 * g:4 * g + 4, rows, :].reshape(4 * BLK, HEAD_DIM)
                p, ps, den = _attn_scores(band, mask, qs, s_ref, g)
                inv = 1.0 / den
                p = p * inv
                dpt = _mm_nt(band[:, 128 + g * 64:192 + g * 64], dos)
                delta = jnp.sum(p * dpt, axis=0, keepdims=True)
                dst = p * (dpt - delta)
                dsv = -(ps * inv) * delta
                for hh in range(4):
                    dsink = jnp.sum(dsv[:, hh * BLK:(hh + 1) * BLK], axis=1, keepdims=True)
                    ds_ref[4 * g + hh:4 * g + hh + 1, :] += jnp.broadcast_to(dsink, (1, 128))
                dqt = _mm_tn(band[:, 0:128], dst) * SCALE
                for hh in range(4):
                    dqh = dqt[:, hh * BLK:(hh + 1) * BLK].T
                    dq_ref[rows, (4 * g + hh) * 64:(4 * g + hh + 1) * 64] = dqh[:, g * 64:(g + 1) * 64].astype(BF16)
                dk = _mm(dst, qs) * SCALE
                dv = _mm(p, dos)
                dkv_ref[pl.ds(cur, BLK), g * 64:(g + 1) * 64] = dk[BLK:2 * BLK]
                dkv_ref[pl.ds(cur, BLK), 128 + g * 64:192 + g * 64] = dv[BLK:2 * BLK]
                dkv_ref[pl.ds(prev, BLK), g * 64:(g + 1) * 64] += dk[0:BLK]
                dkv_ref[pl.ds(prev, BLK), 128 + g * 64:192 + g * 64] += dv[0:BLK]

    tq = nblk * BLK
    return _launch(body, "attn_bwd", (T // tq,),
                   [_heads(tq), _full((T, 256)), _heads(tq), pl.BlockSpec(memory_space=pltpu.SMEM)],
                   [_rows(tq, 512), _full((T, 256)), _full((8, 128))],
                   [jax.ShapeDtypeStruct((T, 512), BF16), jax.ShapeDtypeStruct((T, 256), F32),
                    jax.ShapeDtypeStruct((8, 128), F32)], [], (q, kv, do, sinks), exchange)


def _rows8(tm, cols):
    return lax.broadcasted_iota(jnp.int32, (tm, cols), 0) & 7


def _lru_gates(xc, wa, ba, wx, bx, lam):
    r = _sigmoid(_mm(xc, wa) + ba)
    ii = _sigmoid(_mm(xc, wx) + bx)
    sp = _softplus(-lam)
    la = -LRU_C * r * sp
    a = jnp.exp(la)
    m = jnp.sqrt(-jnp.tanh(la) * (a * a + 1.0))
    return r, ii, sp, a, m


def _rnn_fwd(xr, gr, cw, cb, wa, ba, wx, bx, lam, exchange=None):
    T = xr.shape[0]
    tm = 512
    C = D_RNN

    def body(xr_ref, gr_ref, cw_ref, cb_ref, wa_ref, ba_ref, wx_ref, bx_ref, lam_ref,
             xc_ref, h_ref, rec_ref, ext, a_s, b_s, carry):
        i = pl.program_id(0)

        @pl.when(i == 0)
        def _():
            ext[...] = jnp.zeros((8, C), F32)
            carry[...] = jnp.zeros((8, C), F32)

        xr = xr_ref[...]
        edge = ext[...]
        xc = cb_ref[...] + cw_ref[3:4, :] * xr
        for k in range(3):
            xc = xc + cw_ref[k:k + 1, :] * _shift_rows(xr, 3 - k, edge)
        ext[...] = xr[tm - 8:tm, :]
        xc_ref[...] = xc
        _, ii, _, a, m = _lru_gates(xc, wa_ref[...], ba_ref[...], wx_ref[...], bx_ref[...], lam_ref[...])
        b = m * ii * xc
        r8 = _rows8(tm, C)
        for d in (1, 2, 4):
            ok = r8 >= d
            a_sh = jnp.where(ok, pltpu.roll(a, d, 0), 1.0)
            b_sh = jnp.where(ok, pltpu.roll(b, d, 0), 0.0)
            b = a * b_sh + b
            a = a * a_sh
        a_s[...] = a
        b_s[...] = b

        def step(g, hin):
            s = pl.multiple_of(g * 8, 8)
            hg = a_s[pl.ds(s, 8), :] * hin + b_s[pl.ds(s, 8), :]
            h_ref[pl.ds(s, 8), :] = hg
            return jnp.broadcast_to(hg[7:8, :], (8, C))

        carry[...] = lax.fori_loop(0, tm // 8, step, carry[...], unroll=4)
        ge, _ = _gelu(gr_ref[...])
        rec_ref[...] = (h_ref[...] * ge).astype(BF16)

    vec = _full((1, C))
    in_specs = [_rows(tm, C), _rows(tm, C), _full((4, C)), vec, _full((C, C)), vec, _full((C, C)), vec, vec]
    out_specs = [_rows(tm, C), _rows(tm, C), _rows(tm, C)]
    out_shape = [jax.ShapeDtypeStruct((T, C), F32), jax.ShapeDtypeStruct((T, C), F32), jax.ShapeDtypeStruct((T, C), BF16)]
    scratch = [pltpu.VMEM((8, C), F32), pltpu.VMEM((tm, C), F32), pltpu.VMEM((tm, C), F32), pltpu.VMEM((8, C), F32)]
    return _launch(body, "rnn_fwd", (T // tm,), in_specs, out_specs, out_shape, scratch,
                   (xr, gr, cw, cb, wa, ba, wx, bx, lam), exchange)


def _rnn_bwd(drec, gr, h, xc, xr, cw, wa, ba, wx, bx, lam, exchange=None):
    T = xr.shape[0]
    tm = 512
    C = D_RNN
    nt = T // tm
    t8 = tm // 8

    def body(drec_ref, gr_ref, h_ref, hp_ref, xc_ref, xr_ref, cw_ref, wa_ref, ba_ref, wx_ref, bx_ref,
             lam_ref, dxr_ref, dgr_ref, dwa_ref, dwx_ref, dvec_ref, c_s, g_s, gout, ext, anext, gcarry):
        i = pl.program_id(0)
        j = nt - 1 - i

        @pl.when(i == 0)
        def _():
            dwa_ref[...] = jnp.zeros_like(dwa_ref)
            dwx_ref[...] = jnp.zeros_like(dwx_ref)
            dvec_ref[...] = jnp.zeros_like(dvec_ref)
            anext[...] = jnp.zeros((8, C), F32)
            gcarry[...] = jnp.zeros((8, C), F32)
            ext[...] = jnp.zeros((8, C), F32)

        xc = xc_ref[...]
        lam = lam_ref[...]
        r, ii, sp, a, m = _lru_gates(xc, wa_ref[...], ba_ref[...], wx_ref[...], bx_ref[...], lam)
        ge, dge = _gelu(gr_ref[...])
        drec = drec_ref[...]
        hh = h_ref[...]
        dgr_ref[...] = (drec * hh * dge).astype(BF16)
        dh = drec * ge
        rowi = lax.broadcasted_iota(jnp.int32, (tm, C), 0)
        c = jnp.where(rowi == tm - 1, jnp.broadcast_to(anext[0:1, :], (tm, C)), pltpu.roll(a, tm - 1, 0))
        anext[...] = a[0:8, :]
        r8 = rowi & 7
        gg = dh
        for d in (1, 2, 4):
            ok = r8 < 8 - d
            c_sh = jnp.where(ok, pltpu.roll(c, tm - d, 0), 1.0)
            g_sh = jnp.where(ok, pltpu.roll(gg, tm - d, 0), 0.0)
            gg = c * g_sh + gg
            c = c * c_sh
        c_s[...] = c
        g_s[...] = gg

        def step(k, gin):
            s = pl.multiple_of((t8 - 1 - k) * 8, 8)
            og = c_s[pl.ds(s, 8), :] * gin + g_s[pl.ds(s, 8), :]
            gout[pl.ds(s, 8), :] = og
            return jnp.broadcast_to(og[0:1, :], (8, C))

        gcarry[...] = lax.fori_loop(0, t8, step, gcarry[...], unroll=4)
        G = gout[...]
        hprev_row = jnp.where(j > 0, hp_ref[7:8, :], 0.0)
        hprev = jnp.where(rowi == 0, jnp.broadcast_to(hprev_row, (tm, C)), pltpu.roll(hh, 1, 0))
        da = G * hprev
        dm = G * ii * xc
        di = G * m * xc
        dxc = G * m * ii
        dla = da * a - dm * a * a / m
        dr = dla * (-LRU_C * sp)
        dsp = _colsum(dla * (-LRU_C * r))
        dlam = dsp * (-_sigmoid(-lam))
        dpr = dr * r * (1.0 - r)
        dpi = di * ii * (1.0 - ii)
        dxc = dxc + _mm_nt(dpr, wa_ref[...]) + _mm_nt(dpi, wx_ref[...])
        dwa_ref[...] += _mm_tn(xc, dpr)
        dwx_ref[...] += _mm_tn(xc, dpi)
        dvec_ref[0:1, :] += _colsum(dpr)
        dvec_ref[1:2, :] += _colsum(dpi)
        dvec_ref[2:3, :] += dlam
        dvec_ref[3:4, :] += _colsum(dxc)
        edge = ext[...]
        xr = xr_ref[...]
        dxr = cw_ref[3:4, :] * dxc
        dvec_ref[7:8, :] += _colsum(dxc * xr)
        for k in range(3):
            up = _shift_rows(dxc, k - 3, edge)
            dxr = dxr + cw_ref[k:k + 1, :] * up
            dvec_ref[4 + k:5 + k, :] += _colsum(up * xr)
        ext[...] = dxc[0:8, :]
        dxr_ref[...] = dxr.astype(BF16)

    rev = lambda i: nt - 1 - i
    prev8 = lambda i: jnp.maximum((nt - 1 - i) * t8 - 1, 0)
    vec = _full((1, C))
    return _launch(
        body, "rnn_bwd", (nt,),
        [_rows(tm, C, rev), _rows(tm, C, rev), _rows(tm, C, rev), _rows(8, C, prev8), _rows(tm, C, rev),
         _rows(tm, C, rev), _full((4, C)), _full((C, C)), vec, _full((C, C)), vec, vec],
        [_rows(tm, C, rev), _rows(tm, C, rev), _full((C, C)), _full((C, C)), _full((8, C))],
        [jax.ShapeDtypeStruct((T, C), BF16), jax.ShapeDtypeStruct((T, C), BF16),
         jax.ShapeDtypeStruct((C, C), F32), jax.ShapeDtypeStruct((C, C), F32), jax.ShapeDtypeStruct((8, C), F32)],
        [pltpu.VMEM((tm, C), F32), pltpu.VMEM((tm, C), F32), pltpu.VMEM((tm, C), F32),
         pltpu.VMEM((8, C), F32), pltpu.VMEM((8, C), F32), pltpu.VMEM((8, C), F32)],
        (drec, gr, h, h, xc, xr, cw, wa, ba, wx, bx, lam), exchange)


def _out_proj(att, rec, x, w_out, g1, b1):
    T = x.shape[0]
    tm = min(1024, T)

    def body(att_ref, rec_ref, x_ref, w_ref, g1_ref, b1_ref, z_ref, h_ref):
        mix = _mm(att_ref[...], w_ref[0:512, :]) + _mm(rec_ref[...], w_ref[512:1024, :])
        z1 = ALPHA * x_ref[...] + mix
        z_ref[...] = z1
        h1, _, _ = _ln(z1, g1_ref[...], b1_ref[...])
        h_ref[...] = h1.astype(MXU_DTYPE).astype(BF16)

    return pl.pallas_call(
        body, name="out_proj", grid=(T // tm,),
        in_specs=[_rows(tm, 512), _rows(tm, 512), _rows(tm, D), _full((D, D)), _full((1, D)), _full((1, D))],
        out_specs=[_rows(tm, D), _rows(tm, D)],
        out_shape=[jax.ShapeDtypeStruct((T, D), F32), jax.ShapeDtypeStruct((T, D), BF16)],
        compiler_params=_params(),
    )(att, rec, x, w_out, g1, b1)


NC = D_FF // FF_CHUNK


def _ffn_up(h1b, w_up_t, fcw, fcb, exchange=None):
    T = h1b.shape[0]
    tm = min(1024, T)
    CW = FF_CHUNK

    def body(h_ref, wg_ref, wv_ref, fcw_ref, fcb_ref, gate_ref, ge_ref, vd_ref, act_ref, before):
        i = pl.program_id(1)

        @pl.when(i == 0)
        def _():
            before[...] = jnp.zeros((8, CW), F32)

        hb = h_ref[...]
        gate = _mm_nt(hb, wg_ref[...])
        val = _mm_nt(hb, wv_ref[...])
        gate_ref[...] = gate.astype(BF16)
        edge = before[...]
        gc = (fcb_ref[...] + fcw_ref[0:1, :] * _shift_rows(gate, 2, edge) + fcw_ref[1:2, :] * _shift_rows(gate, 1, edge)
              + fcw_ref[2:3, :] * gate)
        before[...] = gate[tm - 8:tm, :]
        ge, dge = _gelu(gc)
        ge_ref[...] = ge.astype(BF16)
        vd_ref[...] = (val * dge).astype(BF16)
        act_ref[...] = (ge * val).astype(BF16)

    chunk = pl.BlockSpec((None, tm, CW), lambda c, i: (c, i, 0))
    return _launch(
        body, "ffn_up", (NC, T // tm),
        [pl.BlockSpec((tm, D), lambda c, i: (i, 0)), pl.BlockSpec((CW, D), lambda c, i: (c, 0)),
         pl.BlockSpec((CW, D), lambda c, i: (NC + c, 0)), pl.BlockSpec((None, 3, CW), lambda c, i: (c, 0, 0)),
         pl.BlockSpec((None, 1, CW), lambda c, i: (c, 0, 0))],
        [chunk] * 3 + [pl.BlockSpec((tm, CW), lambda c, i: (i, c))],
        [jax.ShapeDtypeStruct((NC, T, CW), BF16)] * 3 + [jax.ShapeDtypeStruct((T, D_FF), BF16)], [pltpu.VMEM((8, CW), F32)],
        (h1b, w_up_t, w_up_t, fcw, fcb), exchange)


def _ffn_down(act, z1, p, tgt, w_down, w_g, w_p_t, g1, b1, g2, b2, bg):
    T = z1.shape[0]
    tm = 512

    def body(act_ref, z_ref, p_ref, t_ref, wdn_hbm, wg_hbm, wp_hbm, g1_ref, b1_ref, g2_ref, b2_ref, bg_ref,
             dz2_ref, dz2b_ref, dpre_ref, dpp_ref, vec_ref, wdn, wg, wp):
        @pl.when(pl.program_id(0) == 0)
        def _():
            pltpu.sync_copy(wdn_hbm, wdn)
            pltpu.sync_copy(wg_hbm, wg)
            pltpu.sync_copy(wp_hbm, wp)
            vec_ref[...] = jnp.zeros_like(vec_ref)

        g2v = g2_ref[...]
        for r in (slice(0, tm // 2), slice(tm // 2, tm)):
            h1, _, _ = _ln(z_ref[r, :], g1_ref[...], b1_ref[...])
            h1b = h1.astype(MXU_DTYPE)
            ffn = _mm(act_ref[r, :], wdn[...])
            sg = _sigmoid(_mm(h1b, wg[...]) + bg_ref[...])
            pp = _mm_nt(p_ref[r, :], wp[...])
            z2 = ALPHA * h1 + ffn + sg * pp
            y, xh2, rstd2 = _ln(z2, g2v, b2_ref[...])
            diff = y - t_ref[r, :]
            dy = diff * (1.0 / D)
            dz2 = _ln_bwd(dy, xh2, rstd2, g2v)
            dpre = dz2 * pp * sg * (1.0 - sg)
            dz2_ref[r, :] = dz2
            dz2b_ref[r, :] = dz2.astype(BF16)
            dpre_ref[r, :] = dpre.astype(BF16)
            dpp_ref[r, :] = (dz2 * sg).astype(BF16)
            loss = 0.5 * jnp.sum(jnp.sum(diff * diff, axis=1, keepdims=True), axis=0, keepdims=True) * (1.0 / D)
            vec_ref[0:1, :] += jnp.broadcast_to(loss, (1, D))
            vec_ref[1:2, :] += _colsum(dy * xh2)
            vec_ref[2:3, :] += _colsum(dy)
            vec_ref[3:4, :] += _colsum(dpre)

    anyspec = pl.BlockSpec(memory_space=pl.ANY)
    vec = _full((1, D))
    return pl.pallas_call(
        body, name="ffn_down", grid=(T // tm,),
        in_specs=[_rows(tm, D_FF), _rows(tm, D), _rows(tm, PLE), _rows(tm, D),
                  anyspec, anyspec, anyspec] + [vec] * 5,
        out_specs=[_rows(tm, D)] * 4 + [_full((8, D))],
        out_shape=[jax.ShapeDtypeStruct((T, D), F32)] + [jax.ShapeDtypeStruct((T, D), BF16)] * 3
                  + [jax.ShapeDtypeStruct((8, D), F32)],
        scratch_shapes=[pltpu.VMEM((D_FF, D), MXU_DTYPE), pltpu.VMEM((D, D), MXU_DTYPE), pltpu.VMEM((D, PLE), MXU_DTYPE)],
        compiler_params=_params(),
    )(act, z1, p, tgt, w_down, w_g, w_p_t, g1, b1, g2, b2, bg)


def _ffn_bwd(dz2b, gate, ge, vd, w_down, fcw):
    T = dz2b.shape[0]
    tm = min(1024, T)
    CW = FF_CHUNK
    nt = T // tm

    def body(dz_ref, wdn_ref, gate_ref, ge_ref, vd_ref, fcw_ref, dup_ref, dfc_ref, after):
        i = pl.program_id(1)

        @pl.when(i == 0)
        def _():
            after[...] = jnp.zeros((8, CW), F32)
            dfc_ref[...] = jnp.zeros_like(dfc_ref)

        gate = gate_ref[...].astype(F32)
        dact = _mm_nt(dz_ref[...], wdn_ref[...])
        dgc = dact * vd_ref[...].astype(F32)
        edge = after[...]
        dgc1 = _shift_rows(dgc, -1, edge)
        dgc2 = _shift_rows(dgc, -2, edge)
        after[...] = dgc[0:8, :]
        dup_ref[:, 0:CW] = (fcw_ref[2:3, :] * dgc + fcw_ref[1:2, :] * dgc1 + fcw_ref[0:1, :] * dgc2).astype(BF16)
        dup_ref[:, CW:2 * CW] = (dact * ge_ref[...].astype(F32)).astype(BF16)
        dfc_ref[0:1, :] += _colsum(dgc2 * gate)
        dfc_ref[1:2, :] += _colsum(dgc1 * gate)
        dfc_ref[2:3, :] += _colsum(dgc * gate)
        dfc_ref[3:4, :] += _colsum(dgc)

    rev = lambda c, i: (c, nt - 1 - i, 0)
    chunk = pl.BlockSpec((None, tm, CW), rev)
    return pl.pallas_call(
        body, name="ffn_bwd", grid=(NC, nt),
        in_specs=[pl.BlockSpec((tm, D), lambda c, i: (nt - 1 - i, 0)), pl.BlockSpec((CW, D), lambda c, i: (c, 0)),
                  chunk, chunk, chunk, pl.BlockSpec((None, 3, CW), lambda c, i: (c, 0, 0))],
        out_specs=[pl.BlockSpec((tm, 2 * CW), lambda c, i: (nt - 1 - i, c)),
                   pl.BlockSpec((None, 8, CW), lambda c, i: (c, 0, 0))],
        out_shape=[jax.ShapeDtypeStruct((T, 2 * D_FF), BF16), jax.ShapeDtypeStruct((NC, 8, CW), F32)],
        scratch_shapes=[pltpu.VMEM((8, CW), F32)],
        compiler_params=_params(),
    )(dz2b, w_down, gate, ge, vd, fcw)


def _ffn_dh1(dup, dz2, dpre, z1, w_up_t, w_g, g1, b1):
    T = z1.shape[0]
    tm = 512

    def body(dup_ref, dz2_ref, dpre_ref, z_ref, wup_hbm, wg_hbm, g1_ref, b1_ref, dz1_ref, vec_ref, wup, wg, sems):
        @pl.when(pl.program_id(0) == 0)
        def _():
            copies = [pltpu.make_async_copy(wup_hbm.at[pl.ds(s * D_FF + c * FF_CHUNK, FF_CHUNK)],
                                            wup.at[pl.ds((2 * c + s) * FF_CHUNK, FF_CHUNK)], sems.at[2 * c + s])
                      for c in range(NC) for s in range(2)]
            copies.append(pltpu.make_async_copy(wg_hbm, wg, sems.at[2 * NC]))
            for cp in copies:
                cp.start()
            vec_ref[...] = jnp.zeros_like(vec_ref)
            for cp in copies:
                cp.wait()

        g1v = g1_ref[...]
        _, xh1, rstd1 = _ln(z_ref[...], g1v, b1_ref[...])
        dh1 = ALPHA * dz2_ref[...] + _mm_nt(dpre_ref[...], wg[...]) + _mm(dup_ref[...], wup[...])
        dz1_ref[...] = _ln_bwd(dh1, xh1, rstd1, g1v)
        vec_ref[0:1, :] += _colsum(dh1 * xh1)
        vec_ref[1:2, :] += _colsum(dh1)

    anyspec = pl.BlockSpec(memory_space=pl.ANY)
    vec = _full((1, D))
    return pl.pallas_call(
        body, name="ffn_dh1", grid=(T // tm,),
        in_specs=[_rows(tm, 2 * D_FF), _rows(tm, D), _rows(tm, D), _rows(tm, D),
                  anyspec, anyspec, vec, vec],
        out_specs=[_rows(tm, D), _full((8, D))],
        out_shape=[jax.ShapeDtypeStruct((T, D), F32), jax.ShapeDtypeStruct((8, D), F32)],
        scratch_shapes=[pltpu.VMEM((2 * D_FF, D), MXU_DTYPE), pltpu.VMEM((D, D), MXU_DTYPE),
                        pltpu.SemaphoreType.DMA((2 * NC + 1,))],
        compiler_params=_params(),
    )(dup, dz2, dpre, z1, w_up_t, w_g, g1, b1)


def _out_proj_bwd(dz1, w_out, exchange=None):
    T = dz1.shape[0]
    tm = min(1024, T)

    def body(dz_ref, w_ref, datt_ref, drec_ref):
        dzb = dz_ref[...].astype(MXU_DTYPE)
        datt = _mm_nt(dzb, w_ref[0:512, :])
        for h in range(HEADS):
            datt_ref[h] = datt[:, h * 64:(h + 1) * 64].astype(BF16)
        drec_ref[...] = _mm_nt(dzb, w_ref[512:1024, :])

    return _launch(body, "out_proj_bwd", (T // tm,), [_rows(tm, D), _full((D, D))], [_heads(tm), _rows(tm, 512)],
                   [jax.ShapeDtypeStruct((HEADS, T, 64), BF16), jax.ShapeDtypeStruct((T, 512), F32)], [],
                   (dz1, w_out), exchange)


def _in_proj_bwd(dq, dkv, dxr, dgr, dz1, w_in_t, exchange=None):
    T = dz1.shape[0]
    tm = 512
    W = D_IN // 4

    def body(dq_ref, dkv_ref, dxr_ref, dgr_ref, dz_ref, w_ref, dx_ref, du_ref):
        dkv = dkv_ref[...]
        dx_ref[...] = (ALPHA * dz_ref[...] + _mm(dq_ref[...], w_ref[0:512, :]) + _mm(dkv, w_ref[512:768, :])
                       + _mm(dxr_ref[...], w_ref[768:1280, :]) + _mm(dgr_ref[...], w_ref[1280:1792, :]))
        dq, dxr, dgr = dq_ref[...].astype(F32), dxr_ref[...].astype(F32), dgr_ref[...].astype(F32)
        du_ref[0] = dq[:, 0:W].astype(BF16)
        du_ref[1, :, 0:64] = dq[:, W:512].astype(BF16)
        du_ref[1, :, 64:320] = dkv.astype(BF16)
        du_ref[1, :, 320:W] = dxr[:, 0:128].astype(BF16)
        du_ref[2, :, 0:384] = dxr[:, 128:512].astype(BF16)
        du_ref[2, :, 384:W] = dgr[:, 0:64].astype(BF16)
        du_ref[3] = dgr[:, 64:512].astype(BF16)

    return _launch(body, "in_proj_bwd", (T // tm,),
                   [_rows(tm, 512), _rows(tm, 256), _rows(tm, 512), _rows(tm, 512), _rows(tm, D), _full((D_IN, D))],
                   [_rows(tm, D), pl.BlockSpec((4, tm, W), lambda i: (0, i, 0))],
                   [jax.ShapeDtypeStruct((T, D), F32), jax.ShapeDtypeStruct((4, T, W), BF16)], [],
                   (dq, dkv, dxr, dgr, dz1, w_in_t), exchange)


def _accumulate_tn(a_ref, b_ref, o_ref):
    @pl.when(pl.program_id(1) == 0)
    def _():
        o_ref[...] = jnp.zeros_like(o_ref)

    o_ref[...] += _mm_tn(a_ref[...], b_ref[...])


def _weight_grad_cols(a, b, name, n_blocks, b_spec, out_shape, out_spec, exchange=None):
    T, M = a.shape
    bt = min(DW_TOKENS, T)
    return _launch(functools.partial(_accumulate_tn), name, (n_blocks, T // bt),
                   [pl.BlockSpec((bt, M), lambda m, k: (k, 0)), b_spec(bt)], [out_spec],
                   [jax.ShapeDtypeStruct(out_shape, F32)], [], (a, b), exchange)


def _dw_out(att, rec, dz1):
    T = dz1.shape[0]
    bt = min(DW_TOKENS // 2, T)

    def body(att_ref, rec_ref, dz_ref, o_ref):
        @pl.when(pl.program_id(0) == 0)
        def _():
            o_ref[...] = jnp.zeros_like(o_ref)

        dz = dz_ref[...].astype(MXU_DTYPE)
        o_ref[0:512, :] += _mm_tn(att_ref[...], dz)
        o_ref[512:1024, :] += _mm_tn(rec_ref[...], dz)

    return pl.pallas_call(
        body, name="dw_out", grid=(T // bt,), in_specs=[_rows(bt, 512), _rows(bt, 512), _rows(bt, D)],
        out_specs=_full((D, D)), out_shape=jax.ShapeDtypeStruct((D, D), F32), compiler_params=_params())(att, rec, dz1)


def _weight_grad(a, b, bm, name, exchange=None):
    bt = min(DW_TOKENS // 2 if b.dtype == F32 else DW_TOKENS, b.shape[0])
    if a.ndim == 3:
        assert a.shape[2] == bm
        T, M = a.shape[1], a.shape[0] * bm
        a_spec = pl.BlockSpec((None, bt, bm), lambda m, k: (m, k, 0))
    else:
        T, M = a.shape
        a_spec = pl.BlockSpec((bt, bm), lambda m, k: (k, m))
    N = b.shape[1]
    nk = T // bt

    out = _launch(functools.partial(_accumulate_tn), name, (M // bm, nk),
                  [a_spec, pl.BlockSpec((bt, N), lambda m, k: (k, 0))], [pl.BlockSpec((bm, N), lambda m, k: (m, 0))],
                  [jax.ShapeDtypeStruct((M, N), F32)], [], (a, b), exchange)
    return out[0] if exchange is None else out


def _adamw(w, g, m, v, name):
    R, C = w.shape
    tr = R // 8 if R % 64 == 0 else R
    c1 = 1.0 / (1.0 - ADAM_B1 ** ADAM_STEP)
    c2 = 1.0 / (1.0 - ADAM_B2 ** ADAM_STEP)

    def body(w_ref, g_ref, m_ref, v_ref, d_ref, nm_ref, nv_ref):
        g = g_ref[...]
        nm = ADAM_B1 * m_ref[...] + (1.0 - ADAM_B1) * g
        nv = ADAM_B2 * v_ref[...] + (1.0 - ADAM_B2) * g * g
        nm_ref[...] = nm
        nv_ref[...] = nv
        d_ref[...] = -ADAM_LR * ((nm * c1) / (jnp.sqrt(nv * c2) + ADAM_EPS) + ADAM_WD * w_ref[...])

    spec = pl.BlockSpec((tr, C), lambda i: (i, 0))
    return pl.pallas_call(
        body, name=name, grid=(R // tr,),
        in_specs=[spec] * 4, out_specs=[spec] * 3,
        out_shape=[jax.ShapeDtypeStruct((R, C), F32)] * 3,
        compiler_params=_params(),
    )(w, g, m, v)


def _adamw_halves(ws, mines, sibs, ms, vs, c, name, exchange=None):
    n, nb = len(ws), 4
    c1 = 1.0 / (1.0 - ADAM_B1 ** ADAM_STEP)
    c2 = 1.0 / (1.0 - ADAM_B2 ** ADAM_STEP)

    def body(c_ref, *refs):
        own = (pl.program_id(0) // nb) == c_ref[0]
        for i in range(n):
            w_ref, a_ref, b_ref, m_ref, v_ref = refs[5 * i:5 * i + 5]
            g_ref, d_ref, nm_ref, nv_ref = refs[5 * n + 4 * i:5 * n + 4 * i + 4]
            g = jnp.where(own, a_ref[...], b_ref[...])
            nm = ADAM_B1 * m_ref[...] + (1.0 - ADAM_B1) * g
            nv = ADAM_B2 * v_ref[...] + (1.0 - ADAM_B2) * g * g
            g_ref[...] = g
            nm_ref[...] = nm
            nv_ref[...] = nv
            d_ref[...] = -ADAM_LR * ((nm * c1) / (jnp.sqrt(nv * c2) + ADAM_EPS) + ADAM_WD * w_ref[...])

    in_specs, out_specs, out_shape, args = [], [], [], []
    for w, a, b, m, v in zip(ws, mines, sibs, ms, vs):
        R, C = w.shape
        tr = R // (2 * nb)
        assert tr % 8 == 0 and a.shape == (R // 2, C)
        full = pl.BlockSpec((tr, C), lambda i, c_ref: (i, 0))
        mine_spec = pl.BlockSpec((tr, C), lambda i, c_ref: (jnp.where(i // nb == c_ref[0], i % nb, nb - 1), 0))
        sib_spec = pl.BlockSpec((tr, C), lambda i, c_ref: (jnp.where(i // nb == c_ref[0], nb - 1, i % nb), 0))
        in_specs += [full, mine_spec, sib_spec, full, full]
        out_specs += [full] * 4
        out_shape += [jax.ShapeDtypeStruct((R, C), F32)] * 4
        args += [w, a, b, m, v]
    out = _launch(body, name, (2 * nb,), in_specs, out_specs, out_shape, [], (c, *args), exchange, prefetch=1)
    return [tuple(out[4 * i:4 * i + 4]) for i in range(n)], list(out[4 * n:])


def _add4(fs, name):
    n = len(fs)

    def body(*refs):
        for a_ref, o_ref in zip(refs[:n], refs[n:]):
            o_ref[...] = ((a_ref[0].astype(F32) + a_ref[1].astype(F32)) + a_ref[2].astype(F32)) + a_ref[3].astype(F32)

    for f in fs:
        assert (f.shape[1] // 2) % 16 == 0
    return pl.pallas_call(
        body, name=name, grid=(2,),
        in_specs=[pl.BlockSpec((4, f.shape[1] // 2, f.shape[2]), lambda i: (0, i, 0)) for f in fs],
        out_specs=[pl.BlockSpec((f.shape[1] // 2, f.shape[2]), lambda i: (i, 0)) for f in fs],
        out_shape=[jax.ShapeDtypeStruct(f.shape[1:], F32) for f in fs], compiler_params=_params())(*fs)


def _gather_first(wsrc, cpack):
    def body(w_ref, c_ref, gw_ref, gc_ref, send_sems, recv_sems, local_sem, csend, crecv, clocal):
        x, y, c = _pos()
        me = 2 * x + y
        chips = _other_chips(x, y)
        start, forward, finish = _gather_steps(w_ref, gw_ref, send_sems, recv_sems, local_sem)
        start()
        loc = pltpu.make_async_copy(c_ref, gc_ref.at[me], clocal)
        loc.start()

        def conv_copy(k, slot):
            px, py = chips[k]
            return pltpu.make_async_remote_copy(src_ref=c_ref, dst_ref=gc_ref.at[slot], send_sem=csend.at[k],
                                                recv_sem=crecv.at[k], device_id=(px, py, c), device_id_type=MESH)

        for k in range(3):
            conv_copy(k, me).start()
        forward()
        finish()
        for k, (px, py) in enumerate(chips):
            conv_copy(k, 2 * px + py).wait_recv()
        for k in range(3):
            conv_copy(k, me).wait_send()
        loc.wait()

    anyspec = pl.BlockSpec(memory_space=pl.ANY)
    return pl.pallas_call(
        body, name="gather_first",
        in_specs=[anyspec, anyspec], out_specs=[anyspec, anyspec],
        out_shape=[jax.ShapeDtypeStruct((4,) + wsrc.shape, wsrc.dtype), jax.ShapeDtypeStruct((4,) + cpack.shape, cpack.dtype)],
        scratch_shapes=GATHER_SCRATCH + [pltpu.SemaphoreType.DMA((3,)), pltpu.SemaphoreType.DMA((3,)), pltpu.SemaphoreType.DMA],
        compiler_params=_params(has_side_effects=True),
    )(wsrc, cpack)


def _all_devices_exchange(s):
    def make(ins, outs, sems):
        s_ref, o_ref = ins[0], outs[0]
        send_sems, recv_sems, local_sem = sems
        x, y, c = _pos()
        me = 4 * x + 2 * y + c
        loc = pltpu.make_async_copy(s_ref, o_ref.at[me], local_sem)

        def copy(k, slot):
            peer = (x ^ (k >> 2), y ^ ((k >> 1) & 1), c ^ (k & 1))
            return pltpu.make_async_remote_copy(src_ref=s_ref, dst_ref=o_ref.at[slot], send_sem=send_sems.at[k - 1],
                                                recv_sem=recv_sems.at[k - 1], device_id=peer, device_id_type=MESH)

        def start():
            loc.start()
            for k in range(1, 8):
                copy(k, me).start()

        def finish():
            for k in range(1, 8):
                copy(k, 4 * (x ^ (k >> 2)) + 2 * (y ^ ((k >> 1) & 1)) + (c ^ (k & 1))).wait_recv()
            for k in range(1, 8):
                copy(k, me).wait_send()
            loc.wait()

        return start, lambda: None, finish

    return _Exchange([s], [jax.ShapeDtypeStruct((8,) + s.shape, s.dtype)],
                     [pltpu.SemaphoreType.DMA((7,)), pltpu.SemaphoreType.DMA((7,)), pltpu.SemaphoreType.DMA], make)


def _sum_devices(a):
    def body(a_ref, o_ref):
        acc = a_ref[0]
        for d in range(1, 8):
            acc = acc + a_ref[d]
        o_ref[...] = acc

    vm = pl.BlockSpec(memory_space=pltpu.VMEM)
    return pl.pallas_call(body, name="sum_devices", in_specs=[vm], out_specs=vm,
                          out_shape=jax.ShapeDtypeStruct(a.shape[1:], F32), compiler_params=_params())(a)


def _swap_exchange(gs):
    n = len(gs)

    def make(ins, outs, sems):
        x, y, c = _pos()
        cps = []
        for i in range(n):
            half = gs[i].shape[1] // 2
            rows = pl.ds(pl.multiple_of((1 - c) * half, 8), half)
            cps.append(pltpu.make_async_remote_copy(src_ref=ins[i].at[:, rows, :], dst_ref=outs[i], send_sem=sems[0].at[i],
                                                    recv_sem=sems[1].at[i], device_id=(x, y, 1 - c), device_id_type=MESH))

        def start():
            for cp in cps:
                cp.start()

        def finish():
            for cp in cps:
                cp.wait()

        return start, lambda: None, finish

    return _Exchange(gs, [jax.ShapeDtypeStruct((4, g.shape[1] // 2, g.shape[2]), g.dtype) for g in gs],
                     [pltpu.SemaphoreType.DMA((n,)), pltpu.SemaphoreType.DMA((n,))], make)


def _scatter_exchange(ss):
    n = len(ss)

    def make(ins, outs, sems):
        send_sems, recv_sems, local_sems = sems
        x, y, c = _pos()
        me = 2 * x + y
        chips = _other_chips(x, y)
        locs = [pltpu.make_async_copy(ins[i].at[me], outs[i].at[me], local_sems.at[i]) for i in range(n)]

        def copy(i, k, src_slot, dst_slot):
            px, py = chips[k]
            return pltpu.make_async_remote_copy(src_ref=ins[i].at[src_slot], dst_ref=outs[i].at[dst_slot],
                                                send_sem=send_sems.at[3 * i + k], recv_sem=recv_sems.at[3 * i + k],
                                                device_id=(px, py, c), device_id_type=MESH)

        def start():
            for i in range(n):
                locs[i].start()
                for k, (px, py) in enumerate(chips):
                    copy(i, k, 2 * px + py, me).start()

        def finish():
            for i in range(n):
                for k, (px, py) in enumerate(chips):
                    copy(i, k, me, 2 * px + py).wait_recv()
            for i in range(n):
                for k, (px, py) in enumerate(chips):
                    copy(i, k, 2 * px + py, me).wait_send()
                locs[i].wait()

        return start, lambda: None, finish

    return _Exchange(ss, [jax.ShapeDtypeStruct(s.shape, s.dtype) for s in ss],
                     [pltpu.SemaphoreType.DMA((3 * n,)), pltpu.SemaphoreType.DMA((3 * n,)), pltpu.SemaphoreType.DMA((n,))], make)


def _send_exchange(rs):
    n = len(rs)

    def make(ins, outs, sems):
        x, y, c = _pos()
        cps = [pltpu.make_async_remote_copy(src_ref=ins[i], dst_ref=outs[i], send_sem=sems[0].at[i], recv_sem=sems[1].at[i],
                                            device_id=(x, y, 1 - c), device_id_type=MESH) for i in range(n)]

        def start():
            for cp in cps:
                cp.start()

        def finish():
            for cp in cps:
                cp.wait()

        return start, lambda: None, finish

    return _Exchange(rs, [jax.ShapeDtypeStruct(r.shape, r.dtype) for r in rs],
                     [pltpu.SemaphoreType.DMA((n,)), pltpu.SemaphoreType.DMA((n,))], make)


def _reduce_in_vmem(g):
    _, R, C = g.shape
    H = R // 2

    def body(g_ref, mine_ref, other_ref, sib, part, got, swap_sems, send_sems, recv_sems, last_sems):
        x, y, c = _pos()
        me = 2 * x + y
        chips = _other_chips(x, y)
        sibling = (x, y, 1 - c)
        mine = pl.ds(pl.multiple_of(c * H, 8), H)
        theirs = pl.ds(pl.multiple_of((1 - c) * H, 8), H)
        swap = pltpu.make_async_remote_copy(src_ref=g_ref.at[:, theirs, :], dst_ref=sib, send_sem=swap_sems.at[0],
                                            recv_sem=swap_sems.at[1], device_id=sibling, device_id_type=MESH)
        swap.start()
        swap.wait()
        part[...] = (g_ref[:, mine, :] + sib[...]).astype(BF16)

        def copy(k, src_slot, dst_slot):
            px, py = chips[k]
            return pltpu.make_async_remote_copy(src_ref=part.at[src_slot], dst_ref=got.at[dst_slot], send_sem=send_sems.at[k],
                                                recv_sem=recv_sems.at[k], device_id=(px, py, c), device_id_type=MESH)

        for k, (px, py) in enumerate(chips):
            copy(k, 2 * px + py, me).start()
        got[me] = part[me]
        for k, (px, py) in enumerate(chips):
            copy(k, me, 2 * px + py).wait_recv()
        for k, (px, py) in enumerate(chips):
            copy(k, 2 * px + py, me).wait_send()
        mine_ref[...] = ((got[0].astype(F32) + got[1].astype(F32)) + got[2].astype(F32)) + got[3].astype(F32)
        last = pltpu.make_async_remote_copy(src_ref=mine_ref, dst_ref=other_ref, send_sem=last_sems.at[0],
                                            recv_sem=last_sems.at[1], device_id=sibling, device_id_type=MESH)
        last.start()
        last.wait()

    vm = pl.BlockSpec(memory_space=pltpu.VMEM)
    half = jax.ShapeDtypeStruct((H, C), F32)
    return pl.pallas_call(
        body, name="reduce_late", in_specs=[vm], out_specs=[vm, vm], out_shape=[half, half],
        scratch_shapes=[pltpu.VMEM((4, H, C), F32), pltpu.VMEM((4, H, C), BF16), pltpu.VMEM((4, H, C), BF16),
                        pltpu.SemaphoreType.DMA((2,)), pltpu.SemaphoreType.DMA((3,)), pltpu.SemaphoreType.DMA((3,)),
                        pltpu.SemaphoreType.DMA((2,))],
        compiler_params=_params(has_side_effects=True))(g)


def _add_half(gs, rs, c, name):
    n = len(gs)

    def body(c_ref, *refs):
        for g_ref, r_ref, o_ref in zip(refs[:n], refs[n:2 * n], refs[2 * n:]):
            o_ref[...] = (g_ref[...] + r_ref[...]).astype(BF16)

    g_specs, r_specs, out_shape = [], [], []
    for g, r in zip(gs, rs):
        _, H, C = r.shape
        tr = H // 2
        assert tr % 16 == 0 and g.shape == (4, 2 * H, C)
        g_specs.append(pl.BlockSpec((1, tr, C), lambda j, i, c_ref: (j, c_ref[0] * 2 + i, 0)))
        r_specs.append(pl.BlockSpec((1, tr, C), lambda j, i, c_ref: (j, i, 0)))
        out_shape.append(jax.ShapeDtypeStruct((4, H, C), BF16))
    grid_spec = pltpu.PrefetchScalarGridSpec(num_scalar_prefetch=1, grid=(4, 2), in_specs=g_specs + r_specs, out_specs=r_specs)
    return pl.pallas_call(body, name=name, grid_spec=grid_spec, out_shape=out_shape, compiler_params=_params())(c, *gs, *rs)


def _block_diag(w):
    eye = jnp.eye(RNN_BLOCKS, dtype=w.dtype)
    return (eye[:, None, :, None] * w[:, :, None, :]).reshape(D_RNN, D_RNN)


def _diag_blocks(wd):
    d = wd.reshape(RNN_BLOCKS, 64, RNN_BLOCKS, 64)
    return jnp.stack([d[h, :, h, :] for h in range(RNN_BLOCKS)])


def _split_pack(a, first, last):
    out, base = {}, PACK_OFF[first]
    for i in range(first, last):
        s = a[:, PACK_OFF[i] - base:PACK_OFF[i + 1] - base]
        out[BIG_KEYS[i]] = s.reshape(4 * 256, 256) if BIG_KEYS[i] == "w_p_t" else s.reshape(-1, 1024)
    return out


def _layer_grads(x, p, tgt, gw, small, shard=None, core=None):
    row = lambda v: v.reshape(1, -1)
    wa = _block_diag(small["gate_a_w"]).astype(MXU_DTYPE)
    wx = _block_diag(small["gate_x_w"]).astype(MXU_DTYPE)
    sinks = small["attn_sinks"].reshape(1, HEADS)

    dist = shard is not None
    q, kv, xr, gr, xb = _in_proj(x, gw["w_in_t"])
    cut = PACK_OFF[1] + PACK_ROWS[1] // 2
    att, *ga = _attn_fwd(q, kv, sinks, _gather_exchange(shard[PACK_OFF[1]:cut]) if dist else None)
    xc, h, rec, *gb = _rnn_fwd(xr, gr, small["rnn_conv_w"], row(small["rnn_conv_b"]), wa, row(small["gate_a_b"]),
                               wx, row(small["gate_x_b"]), row(small["lru_lambda"]),
                               _gather_exchange(shard[cut:PACK_OFF[3]]) if dist else None)
    if dist:
        gw = {**gw, **_split_pack(jnp.concatenate([ga[0], gb[0]], axis=1), 1, 3)}
    g1, b1 = row(small["ln1_g"]), row(small["ln1_b"])
    fcw = small["ffn_conv_w"].reshape(3, NC, FF_CHUNK).transpose(1, 0, 2)
    fcb = small["ffn_conv_b"].reshape(NC, 1, FF_CHUNK)
    z1, h1b = _out_proj(att, rec, x, gw["w_out"], g1, b1)
    gate, ge, vd, act, *gc = _ffn_up(h1b, gw["w_up_t"], fcw, fcb,
                                     _gather_exchange(shard[PACK_OFF[3]:PACK_OFF[6]]) if dist else None)
    if dist:
        gw = {**gw, **_split_pack(gc[0], 3, 6)}
    dz2, dz2b, dpre, dpp, vec2 = _ffn_down(act, z1, p, tgt, gw["w_down"], gw["w_g"], gw["w_p_t"], g1, b1,
                                           row(small["ln2_g"]), row(small["ln2_b"]), row(small["ple_gate_b"]))
    dup, dfc = _ffn_bwd(dz2b, gate, ge, vd, gw["w_down"], fcw)
    dz1, vec1 = _ffn_dh1(dup, dz2, dpre, z1, gw["w_up_t"], gw["w_g"], g1, b1)
    per_chip = 2 * D_FF // 4 // FF_CHUNK
    big = {"w_ffn_up": _weight_grad_cols(
        h1b, dup, "dw_up", 2 * NC, lambda bt: pl.BlockSpec((bt, FF_CHUNK), lambda m, k: (k, m)), (4, D, 2 * D_FF // 4),
        pl.BlockSpec((None, D, FF_CHUNK), lambda m, k: (2 * (m % 2) + (m // 2) // per_chip, 0, (m // 2) % per_chip)))[0]}
    g_dn, *got_up = _weight_grad(act, dz2b, FF_CHUNK, "dw_down", _swap_exchange([big["w_ffn_up"]])) if dist else (
        _weight_grad(act, dz2b, FF_CHUNK, "dw_down"),)
    big["w_ffn_down"] = g_dn.reshape(4, D_FF // 4, D)
    big["ple_gate_w"] = _weight_grad(h1b, dpre, 512, "dw_gate").reshape(4, D // 4, D)
    big["ple_proj"] = _weight_grad(p, dpp, PLE, "dw_proj").reshape(PLE, 4, D // 4).transpose(1, 0, 2)
    big["w_out"] = _dw_out(att, rec, dz1).reshape(4, D // 4, D)
    reduced = None
    if dist:
        g_ffn = [big[k] for k in EARLY_WEIGHTS]
        ex = _swap_exchange(g_ffn[1:])
    datt, drec, *got = _out_proj_bwd(dz1, gw["w_out"], ex if dist else None)
    if dist:
        sums = _add_half(g_ffn, got_up + got, core, "add_half_ffn")
        ex, ex2 = _scatter_exchange(sums[:1]), _scatter_exchange(sums[1:])
    dxr, dgr, dwa, dwx, dvec, *got = _rnn_bwd(drec, gr, h, xc, xr, small["rnn_conv_w"], wa, row(small["gate_a_b"]),
                                              wx, row(small["gate_x_b"]), row(small["lru_lambda"]), ex if dist else None)
    dq, dkv, dsinks, *got2 = _attn_bwd(q, kv, datt, sinks, ex2 if dist else None)
    if dist:
        mine = _add4(got + got2, "add_chips_ffn")
        big = {}
    sg = {
        "attn_sinks": dsinks[:, 0],
        "rnn_conv_w": dvec[4:8],
        "rnn_conv_b": dvec[3],
        "gate_a_w": _diag_blocks(dwa),
        "gate_a_b": dvec[0],
        "gate_x_w": _diag_blocks(dwx),
        "gate_x_b": dvec[1],
        "lru_lambda": dvec[2],
        "ln1_g": vec1[0],
        "ln1_b": vec1[1],
        "ffn_conv_w": dfc[:, 0:3].transpose(1, 0, 2).reshape(3, D_FF),
        "ffn_conv_b": dfc[:, 3].reshape(D_FF),
        "ple_gate_b": vec2[3],
        "ln2_g": vec2[1],
        "ln2_b": vec2[2],
    }
    loss = vec2[0, 0:1]
    grad_x, du = _in_proj_bwd(dq, dkv, dxr, dgr, dz1, gw["w_in_t"])
    ex = None
    if dist:
        ex = _join_exchanges(_send_exchange(mine), _all_devices_exchange(_pack_vecs([sg[k] for k in SMALL] + [loss])[0]))
    big["w_in"], *got = _weight_grad_cols(
        xb, du, "dw_in", 4, lambda bt: pl.BlockSpec((None, bt, D_IN // 4), lambda j, k: (j, k, 0)), (4, D, D_IN // 4),
        pl.BlockSpec((None, D, D_IN // 4), lambda j, k: (j, 0, 0)), ex)
    if dist:
        reduced = (mine, got[:len(mine)])
    return grad_x, big, sg, loss, reduced, got[-1:]


BIG = ("w_in", "w_ffn_up", "w_out", "w_ffn_down", "ple_gate_w", "ple_proj")
BIG_KEYS = ("w_in_t", "w_up_t", "w_out", "w_down", "w_g", "w_p_t")
BIG_T = (True, True, False, False, False, True)
EARLY_WEIGHTS = ("w_ffn_up", "w_ffn_down", "ple_gate_w", "ple_proj", "w_out")
LATE_WEIGHTS = ("w_in",)
SMALL = ("attn_sinks", "rnn_conv_w", "rnn_conv_b", "gate_a_w", "gate_a_b", "gate_x_w", "gate_x_b", "lru_lambda",
         "ln1_g", "ln1_b", "ffn_conv_w", "ffn_conv_b", "ple_gate_b", "ln2_g", "ln2_b")
SHARDED_SMALL = ("rnn_conv_w", "ffn_conv_w")
WEIGHTS = ("w_in", "attn_sinks", "rnn_conv_w", "rnn_conv_b", "gate_a_w", "gate_a_b", "gate_x_w", "gate_x_b",
           "lru_lambda", "w_out", "ln1_g", "ln1_b", "w_ffn_up", "ffn_conv_w", "ffn_conv_b", "w_ffn_down",
           "ple_gate_w", "ple_gate_b", "ple_proj", "ln2_g", "ln2_b")


def _pack_big(d, first=0, last=6):
    parts = []
    for name, t in zip(BIG[first:last], BIG_T[first:last]):
        a = d[name]
        a = a.T if t else a
        parts.append(a.reshape(-1, 1024))
    return jnp.concatenate(parts, axis=0)


def _pack_vecs(items):
    parts, offs, n = [], [], 0
    for a in items:
        f = a.reshape(-1).astype(F32)
        pad = (-f.shape[0]) % 128
        parts.append(jnp.pad(f, (0, pad)))
        offs.append(n)
        n += (f.shape[0] + pad) // 128
    padr = (-n) % 8
    if padr:
        parts.append(jnp.zeros((padr * 128,), F32))
    return jnp.concatenate(parts).reshape(-1, 128), offs


def _unpack_vecs(a, offs, shapes):
    flat = a.reshape(-1)
    out = []
    for o, s in zip(offs, shapes):
        n = 1
        for d in s:
            n *= d
        out.append(flat[o * 128:o * 128 + n].reshape(s))
    return out


def kernel(x, p, w_in, attn_sinks, rnn_conv_w, rnn_conv_b, gate_a_w, gate_a_b, gate_x_w, gate_x_b, lru_lambda, w_out, ln1_g, ln1_b, w_ffn_up, ffn_conv_w, ffn_conv_b, w_ffn_down, ple_gate_w, ple_gate_b, ple_proj, ln2_g, ln2_b, loss_target, m_w_in, m_attn_sinks, m_rnn_conv_w, m_rnn_conv_b, m_gate_a_w, m_gate_a_b, m_gate_x_w, m_gate_x_b, m_lru_lambda, m_w_out, m_ln1_g, m_ln1_b, m_w_ffn_up, m_ffn_conv_w, m_ffn_conv_b, m_w_ffn_down, m_ple_gate_w, m_ple_gate_b, m_ple_proj, m_ln2_g, m_ln2_b, v_w_in, v_attn_sinks, v_rnn_conv_w, v_rnn_conv_b, v_gate_a_w, v_gate_a_b, v_gate_x_w, v_gate_x_b, v_lru_lambda, v_w_out, v_ln1_g, v_ln1_b, v_w_ffn_up, v_ffn_conv_w, v_ffn_conv_b, v_w_ffn_down, v_ple_gate_w, v_ple_gate_b, v_ple_proj, v_ln2_g, v_ln2_b):
    w = dict(w_in=w_in, attn_sinks=attn_sinks, rnn_conv_w=rnn_conv_w, rnn_conv_b=rnn_conv_b, gate_a_w=gate_a_w,
             gate_a_b=gate_a_b, gate_x_w=gate_x_w, gate_x_b=gate_x_b, lru_lambda=lru_lambda, w_out=w_out, ln1_g=ln1_g,
             ln1_b=ln1_b, w_ffn_up=w_ffn_up, ffn_conv_w=ffn_conv_w, ffn_conv_b=ffn_conv_b, w_ffn_down=w_ffn_down,
             ple_gate_w=ple_gate_w, ple_gate_b=ple_gate_b, ple_proj=ple_proj, ln2_g=ln2_g, ln2_b=ln2_b)
    m = dict(w_in=m_w_in, attn_sinks=m_attn_sinks, rnn_conv_w=m_rnn_conv_w, rnn_conv_b=m_rnn_conv_b, gate_a_w=m_gate_a_w,
             gate_a_b=m_gate_a_b, gate_x_w=m_gate_x_w, gate_x_b=m_gate_x_b, lru_lambda=m_lru_lambda, w_out=m_w_out,
             ln1_g=m_ln1_g, ln1_b=m_ln1_b, w_ffn_up=m_w_ffn_up, ffn_conv_w=m_ffn_conv_w, ffn_conv_b=m_ffn_conv_b,
             w_ffn_down=m_w_ffn_down, ple_gate_w=m_ple_gate_w, ple_gate_b=m_ple_gate_b, ple_proj=m_ple_proj,
             ln2_g=m_ln2_g, ln2_b=m_ln2_b)
    v = dict(w_in=v_w_in, attn_sinks=v_attn_sinks, rnn_conv_w=v_rnn_conv_w, rnn_conv_b=v_rnn_conv_b, gate_a_w=v_gate_a_w,
             gate_a_b=v_gate_a_b, gate_x_w=v_gate_x_w, gate_x_b=v_gate_x_b, lru_lambda=v_lru_lambda, w_out=v_w_out,
             ln1_g=v_ln1_g, ln1_b=v_ln1_b, w_ffn_up=v_w_ffn_up, ffn_conv_w=v_ffn_conv_w, ffn_conv_b=v_ffn_conv_b,
             w_ffn_down=v_w_ffn_down, ple_gate_w=v_ple_gate_w, ple_gate_b=v_ple_gate_b, ple_proj=v_ple_proj,
             ln2_g=v_ln2_g, ln2_b=v_ln2_b)
    w, m, v = ({k: a[0] for k, a in d.items()} for d in (w, m, v))
    chip = 2 * lax.axis_index("x") + lax.axis_index("y")
    core = lax.axis_index("c")

    wpack = _pack_big(w)
    cpack, _ = _pack_vecs([w["rnn_conv_w"], w["ffn_conv_w"]])
    shard = wpack.astype(MXU_DTYPE)
    g_in, gcp = _gather_first(shard[PACK_OFF[0]:PACK_OFF[1]], cpack)
    gw = _split_pack(g_in, 0, 1)
    small = {k: w[k] for k in SMALL}
    small["rnn_conv_w"] = gcp[:, 0:4].reshape(4, 4, 128).transpose(1, 0, 2).reshape(4, 512)
    small["ffn_conv_w"] = gcp[:, 4:22].reshape(4, 3, 768).transpose(1, 0, 2).reshape(3, 3072)

    core1 = core.reshape(1).astype(jnp.int32)
    grad_x, big, sg, loss, ffn_halves, small_all = _layer_grads(x[0], p[0, 0], loss_target[0], gw, small, shard, core1)

    shapes = [sg[k].shape for k in SMALL] + [(1,)]
    _, offs = _pack_vecs([jnp.zeros(s, F32) for s in shapes])
    red = dict(zip(SMALL + ("loss",), _unpack_vecs(_sum_devices(small_all[0]), offs, shapes)))
    red["rnn_conv_w"] = lax.dynamic_slice_in_dim(red["rnn_conv_w"], chip * 128, 128, axis=1)
    red["ffn_conv_w"] = lax.dynamic_slice_in_dim(red["ffn_conv_w"], chip * 768, 768, axis=1)

    late_mine, late_other = ([a] for a in _reduce_in_vmem(big["w_in"]))

    def adamw(names, mine, other, name):
        out, _ = _adamw_halves([w[k] for k in names], mine, other, [m[k] for k in names], [v[k] for k in names],
                               core1, name)
        return dict(zip(names, out))

    big_out = {**adamw(LATE_WEIGHTS, late_mine, late_other, "adamw_late"), **adamw(EARLY_WEIGHTS, *ffn_halves, "adamw_early")}
    wsm, offs2 = _pack_vecs([w[k] for k in SMALL])
    gsm, _ = _pack_vecs([red[k] for k in SMALL])
    msm, _ = _pack_vecs([m[k] for k in SMALL])
    vsm, _ = _pack_vecs([v[k] for k in SMALL])
    dsm, nmsm, nvsm = _adamw(wsm, gsm, msm, vsm, "adamw_small")
    shapes2 = [w[k].shape for k in SMALL]

    def named(n, smallp):
        d = {k: out[n][None] for k, out in big_out.items()}
        d.update({k: a[None] for k, a in zip(SMALL, _unpack_vecs(smallp, offs2, shapes2))})
        return [d[k] for k in WEIGHTS]

    return (red["loss"].reshape(()), grad_x[None], *named(0, gsm), *named(1, dsm), *named(2, nmsm), *named(3, nvsm))
```

```python
import functools

import jax
import jax.numpy as jnp
from jax import lax
from jax.experimental import pallas as pl
from jax.experimental.pallas import tpu as pltpu

F32 = jnp.float32
BF16 = jnp.bfloat16
MXU_DTYPE = jnp.bfloat16

D = 1024
D_ATT = 512
D_KV = 128
D_RNN = 512
D_IN = 1792
D_FF = 3072
FF_CHUNK = 768
PLE = 256
HEADS = 8
HEAD_DIM = 64
BLK = 128
ATTN_BLOCKS = 8
DW_TOKENS = 4096
RNN_BLOCKS = 8
LN_EPS = 1e-5
LRU_C = 8.0
ALPHA = float(2.0 ** 0.25)
SCALE = HEAD_DIM ** -0.5
NEG = -1e30

ADAM_LR = 0.001
ADAM_B1 = 0.9
ADAM_B2 = 0.999
ADAM_EPS = 1e-08
ADAM_WD = 0.01
ADAM_STEP = 10

VMEM_LIMIT_BYTES = 56 * 1024 * 1024
MESH = pl.DeviceIdType.MESH

PACK_ROWS = (448, 1536, 256, 768, 256, 64)
PACK_OFF = tuple(sum(PACK_ROWS[:i]) for i in range(len(PACK_ROWS) + 1))
PACK_TOTAL = PACK_OFF[-1]


def _params(**kw):
    return pltpu.CompilerParams(vmem_limit_bytes=VMEM_LIMIT_BYTES, **kw)


def _mm(a, b):
    return jnp.dot(a.astype(MXU_DTYPE), b.astype(MXU_DTYPE), preferred_element_type=F32)


def _mm_nt(a, b):
    return lax.dot_general(a.astype(MXU_DTYPE), b.astype(MXU_DTYPE), (((1,), (1,)), ((), ())),
                           preferred_element_type=F32)


def _mm_tn(a, b):
    return lax.dot_general(a.astype(MXU_DTYPE), b.astype(MXU_DTYPE), (((0,), (0,)), ((), ())),
                           preferred_element_type=F32)


def _sigmoid(x):
    return 0.5 + 0.5 * jnp.tanh(0.5 * x)


def _gelu(x):
    c = 0.7978845608028654
    k = 0.044715
    x2 = x * x
    t = jnp.tanh(x * (c + (c * k) * x2))
    h = 0.5 * (1.0 + t)
    return x * h, h * (1.0 + (x * (1.0 - t)) * (c + (3.0 * c * k) * x2))


def _shift_rows(x, s, edge8):
    R = x.shape[0]
    row8 = lax.broadcasted_iota(jnp.int32, (8, x.shape[1]), 0)
    if s > 0:
        rolled = pltpu.roll(x, s, 0)
        first = jnp.where(row8 < s, pltpu.roll(edge8, s, 0), rolled[0:8])
        return jnp.concatenate([first, rolled[8:]], axis=0)
    k = -s
    rolled = pltpu.roll(x, R - k, 0)
    last = jnp.where(row8 >= 8 - k, pltpu.roll(edge8, 8 - k, 0), rolled[R - 8:])
    return jnp.concatenate([rolled[:R - 8], last], axis=0)


def _softplus(x):
    return jnp.maximum(x, 0.0) + jnp.log(1.0 + jnp.exp(-jnp.abs(x)))


def _ln(z, g, b):
    mu = jnp.mean(z, axis=-1, keepdims=True)
    zc = z - mu
    var = jnp.mean(zc * zc, axis=-1, keepdims=True)
    rstd = lax.rsqrt(var + LN_EPS)
    xhat = zc * rstd
    return xhat * g + b, xhat, rstd


def _ln_bwd(dy, xhat, rstd, g):
    dxh = dy * g
    m1 = jnp.mean(dxh, axis=-1, keepdims=True)
    m2 = jnp.mean(dxh * xhat, axis=-1, keepdims=True)
    return rstd * (dxh - m1 - xhat * m2)


def _colsum(x):
    return jnp.sum(x, axis=0, keepdims=True)


def _full(shape):
    nd = len(shape)
    return pl.BlockSpec(shape, lambda *_: (0,) * nd)


def _rows(tm, cols, fn=None):
    if fn is None:
        return pl.BlockSpec((tm, cols), lambda i: (i, 0))
    return pl.BlockSpec((tm, cols), lambda i: (fn(i), 0))


def _heads(tm):
    return pl.BlockSpec((HEADS, tm, HEAD_DIM), lambda i: (0, i, 0))


def _in_proj(x, w_in_t):
    T = x.shape[0]
    tm = min(1024, T)

    def body(x_ref, w_ref, q_ref, kv_ref, xr_ref, gr_ref, xb_ref):
        xb = x_ref[...].astype(MXU_DTYPE)
        xb_ref[...] = xb.astype(BF16)
        q = _mm_nt(xb, w_ref[0:512, :])
        for h in range(HEADS):
            q_ref[h] = q[:, h * 64:(h + 1) * 64].astype(BF16)
        kv_ref[...] = _mm_nt(xb, w_ref[512:768, :]).astype(BF16)
        xr_ref[...] = _mm_nt(xb, w_ref[768:1280, :])
        gr_ref[...] = _mm_nt(xb, w_ref[1280:1792, :])

    return pl.pallas_call(
        body, name="in_proj", grid=(T // tm,),
        in_specs=[_rows(tm, D), _full((D_IN, D))],
        out_specs=[_heads(tm), _rows(tm, 256), _rows(tm, 512), _rows(tm, 512), _rows(tm, D)],
        out_shape=[jax.ShapeDtypeStruct((HEADS, T, 64), BF16), jax.ShapeDtypeStruct((T, 256), BF16),
                   jax.ShapeDtypeStruct((T, 512), F32), jax.ShapeDtypeStruct((T, 512), F32),
                   jax.ShapeDtypeStruct((T, D), BF16)],
        compiler_params=_params(),
    )(x, w_in_t)


def _attn_band(kv_ref, i):
    cur = pl.multiple_of(i * BLK, BLK)
    prev = pl.multiple_of(jnp.maximum(i - 1, 0) * BLK, BLK)
    band = jnp.concatenate([kv_ref[pl.ds(prev, BLK), :], kv_ref[pl.ds(cur, BLK), :]], axis=0)
    key = lax.broadcasted_iota(jnp.int32, (2 * BLK, 4 * BLK), 0)
    qry = lax.broadcasted_iota(jnp.int32, (2 * BLK, 4 * BLK), 1) & (BLK - 1)
    in_prev = jnp.logical_and(jnp.logical_and(key < BLK, key > qry), i > 0)
    mask = jnp.logical_or(in_prev, jnp.logical_and(key >= BLK, key - BLK <= qry))
    return band, mask, cur, prev


def _attn_scores(band, mask, qs, s_ref, g):
    st = jnp.where(mask, _mm_nt(band[:, g * 64:(g + 1) * 64], qs) * SCALE, NEG)
    lane = lax.broadcasted_iota(jnp.int32, (1, 4 * BLK), 1)
    sv = jnp.where(lane < BLK, s_ref[0, 4 * g],
                   jnp.where(lane < 2 * BLK, s_ref[0, 4 * g + 1], jnp.where(lane < 3 * BLK, s_ref[0, 4 * g + 2], s_ref[0, 4 * g + 3])))
    m = jnp.maximum(jnp.max(st, axis=0, keepdims=True), sv)
    p = jnp.exp(st - m)
    ps = jnp.exp(sv - m)
    return p, ps, jnp.sum(p, axis=0, keepdims=True) + ps


def _pos():
    return lax.axis_index("x"), lax.axis_index("y"), lax.axis_index("c")


def _other_chips(x, y):
    return [(1 - x, y), (x, 1 - y), (1 - x, 1 - y)]


def _gather_steps(w_ref, gw_ref, send_sems, recv_sems, local_sem):
    x, y, c = _pos()
    me = 2 * x + y
    chips = _other_chips(x, y)
    half = w_ref.shape[0] // 2
    mine = pl.ds(pl.multiple_of(c * half, 16), half)
    theirs = pl.ds(pl.multiple_of((1 - c) * half, 16), half)
    loc = pltpu.make_async_copy(w_ref, gw_ref.at[me], local_sem)

    def copy(k, src, dst, to):
        return pltpu.make_async_remote_copy(src_ref=src, dst_ref=dst, send_sem=send_sems.at[k], recv_sem=recv_sems.at[k],
                                            device_id=to, device_id_type=MESH)

    def out(k):
        px, py = chips[k]
        return copy(k, w_ref.at[mine], gw_ref.at[me, mine], (px, py, c))

    def fwd(k, rows):
        px, py = chips[k]
        return copy(3 + k, gw_ref.at[2 * px + py, rows], gw_ref.at[2 * px + py, rows], (x, y, 1 - c))

    def start():
        loc.start()
        for k in range(3):
            out(k).start()

    def forward():
        for k in range(3):
            px, py = chips[k]
            copy(k, w_ref.at[mine], gw_ref.at[2 * px + py, mine], (px, py, c)).wait_recv()
            fwd(k, mine).start()

    def finish():
        for k in range(3):
            fwd(k, theirs).wait_recv()
        for k in range(3):
            out(k).wait_send()
            fwd(k, mine).wait_send()
        loc.wait()

    return start, forward, finish


GATHER_SCRATCH = [pltpu.SemaphoreType.DMA((6,)), pltpu.SemaphoreType.DMA((6,)), pltpu.SemaphoreType.DMA]


class _Exchange:
    def __init__(self, args, out_shape, scratch, make):
        self.args, self.out_shape, self.scratch, self.make = list(args), list(out_shape), list(scratch), make


def _join_exchanges(a, b):
    na, nao, nas = len(a.args), len(a.out_shape), len(a.scratch)

    def make(ins, outs, sems):
        steps_a = a.make(ins[:na], outs[:nao], sems[:nas])
        steps_b = b.make(ins[na:], outs[nao:], sems[nas:])

        def both(f, g):
            def run():
                f()
                g()
            return run

        return tuple(both(f, g) for f, g in zip(steps_a, steps_b))

    return _Exchange(a.args + b.args, a.out_shape + b.out_shape, a.scratch + b.scratch, make)


def _gather_exchange(wsrc):
    return _Exchange([wsrc], [jax.ShapeDtypeStruct((4,) + wsrc.shape, wsrc.dtype)], GATHER_SCRATCH,
                     lambda ins, outs, sems: _gather_steps(ins[0], outs[0], *sems))


def _launch(body, name, grid, in_specs, out_specs, out_shape, scratch, args, exchange=None, prefetch=0):
    def call(fn, fn_name, ins, outs, shapes, scr, operands, effects):
        spec = pltpu.PrefetchScalarGridSpec(num_scalar_prefetch=prefetch, grid=grid, in_specs=ins, out_specs=outs,
                                            scratch_shapes=scr)
        return pl.pallas_call(fn, name=fn_name, grid_spec=spec, out_shape=shapes,
                              compiler_params=_params(has_side_effects=effects))(*operands)

    if exchange is None:
        return call(body, name, list(in_specs), list(out_specs), list(out_shape), list(scratch), args, False)
    n_in, n_out, ei, eo, ns = len(in_specs), len(out_specs), len(exchange.args), len(exchange.out_shape), len(exchange.scratch)
    nsteps = 1
    for g in grid:
        nsteps *= g

    def wrapped(*refs):
        scalars, refs = refs[:prefetch], refs[prefetch:]
        ins, xin = refs[:n_in], refs[n_in:n_in + ei]
        outs, xout = refs[n_in + ei:n_in + ei + n_out], refs[n_in + ei + n_out:n_in + ei + n_out + eo]
        rest = refs[n_in + ei + n_out + eo:]
        own, sems = rest[:len(rest) - ns], rest[len(rest) - ns:]
        start, forward, finish = exchange.make(xin, xout, sems)
        i = pl.program_id(0)
        for d in range(1, len(grid)):
            i = i * grid[d] + pl.program_id(d)
        pl.when(i == 0)(start)
        body(*scalars, *ins, *outs, *own)
        pl.when(i == max(nsteps - 3, 0))(forward)
        pl.when(i == nsteps - 1)(finish)

    anyspec = pl.BlockSpec(memory_space=pl.ANY)
    return call(wrapped, name + "_x", list(in_specs) + [anyspec] * ei, list(out_specs) + [anyspec] * eo,
                list(out_shape) + exchange.out_shape, list(scratch) + exchange.scratch, (*args, *exchange.args), True)


def _attn_fwd(q, kv, sinks, exchange=None):
    T = kv.shape[0]
    nblk = min(ATTN_BLOCKS, T // BLK)

    def body(q_ref, kv_ref, s_ref, o_ref):
        for b in range(nblk):
            rows = slice(b * BLK, (b + 1) * BLK)
            band, mask, _, _ = _attn_band(kv_ref, nblk * pl.program_id(0) + b)
            for g in range(2):
                qs = q_ref[4 * g:4 * g + 4, rows, :].reshape(4 * BLK, HEAD_DIM)
                p, _, den = _attn_scores(band, mask, qs, s_ref, g)
                ot = _mm_tn(band[:, 128:256], p) * (1.0 / den)
                for hh in range(4):
                    o = ot[:, hh * BLK:(hh + 1) * BLK].T
                    o_ref[rows, (4 * g + hh) * 64:(4 * g + hh + 1) * 64] = o[:, g * 64:(g + 1) * 64].astype(BF16)

    tq = nblk * BLK
    return _launch(body, "attn_fwd", (T // tq,), [_heads(tq), _full((T, 256)), pl.BlockSpec(memory_space=pltpu.SMEM)],
                   [_rows(tq, 512)], [jax.ShapeDtypeStruct((T, 512), BF16)], [], (q, kv, sinks), exchange)


def _attn_bwd(q, kv, do, sinks, exchange=None):
    T = kv.shape[0]
    nblk = min(ATTN_BLOCKS, T // BLK)

    def body(q_ref, kv_ref, do_ref, s_ref, dq_ref, dkv_ref, ds_ref):
        @pl.when(pl.program_id(0) == 0)
        def _():
            ds_ref[...] = jnp.zeros_like(ds_ref)

        for b in range(nblk):
            rows = slice(b * BLK, (b + 1) * BLK)
            band, mask, cur, prev = _attn_band(kv_ref, nblk * pl.program_id(0) + b)
            for g in range(2):
                qs = q_ref[4 * g:4 * g + 4, rows, :].reshape(4 * BLK, HEAD_DIM)
                dos = do_ref[4 * g:4 * g + 4, rows, :].reshape(4 * BLK, HEAD_DIM)
                p, ps, den = _attn_scores(band, mask, qs, s_ref, g)
                inv = 1.0 / den
                p = p * inv
                dpt = _mm_nt(band[:, 128 + g * 64:192 + g * 64], dos)
                delta = jnp.sum(p * dpt, axis=0, keepdims=True)
                dst = p * (dpt - delta)
                dsv = -(ps * inv) * delta
                for hh in range(4):
                    dsink = jnp.sum(dsv[:, hh * BLK:(hh + 1) * BLK], axis=1, keepdims=True)
                    ds_ref[4 * g + hh:4 * g + hh + 1, :] += jnp.broadcast_to(dsink, (1, 128))
                dqt = _mm_tn(band[:, 0:128], dst) * SCALE
                for hh in range(4):
                    dqh = dqt[:, hh * BLK:(hh + 1) * BLK].T
                    dq_ref[rows, (4 * g + hh) * 64:(4 * g + hh + 1) * 64] = dqh[:, g * 64:(g + 1) * 64].astype(BF16)
                dk = _mm(dst, qs) * SCALE
                dv = _mm(p, dos)
                dkv_ref[pl.ds(cur, BLK), g * 64:(g + 1) * 64] = dk[BLK:2 * BLK]
                dkv_ref[pl.ds(cur, BLK), 128 + g * 64:192 + g * 64] = dv[BLK:2 * BLK]
                dkv_ref[pl.ds(prev, BLK), g * 64:(g + 1) * 64] += dk[0:BLK]
                dkv_ref[pl.ds(prev, BLK), 128 + g * 64:192 + g * 64] += dv[0:BLK]

    tq = nblk * BLK
    return _launch(body, "attn_bwd", (T // tq,),
                   [_heads(tq), _full((T, 256)), _heads(tq), pl.BlockSpec(memory_space=pltpu.SMEM)],
                   [_rows(tq, 512), _full((T, 256)), _full((8, 128))],
                   [jax.ShapeDtypeStruct((T, 512), BF16), jax.ShapeDtypeStruct((T, 256), F32),
                    jax.ShapeDtypeStruct((8, 128), F32)], [], (q, kv, do, sinks), exchange)


def _rows8(tm, cols):
    return lax.broadcasted_iota(jnp.int32, (tm, cols), 0) & 7


def _lru_gates(xc, wa, ba, wx, bx, lam):
    r = _sigmoid(_mm(xc, wa) + ba)
    ii = _sigmoid(_mm(xc, wx) + bx)
    sp = _softplus(-lam)
    la = -LRU_C * r * sp
    a = jnp.exp(la)
    m = jnp.sqrt(-jnp.tanh(la) * (a * a + 1.0))
    return r, ii, sp, a, m


def _rnn_fwd(xr, gr, cw, cb, wa, ba, wx, bx, lam, exchange=None):
    T = xr.shape[0]
    tm = 512
    C = D_RNN

    def body(xr_ref, gr_ref, cw_ref, cb_ref, wa_ref, ba_ref, wx_ref, bx_ref, lam_ref,
             xc_ref, h_ref, rec_ref, ext, a_s, b_s, carry):
        i = pl.program_id(0)

        @pl.when(i == 0)
        def _():
            ext[...] = jnp.zeros((8, C), F32)
            carry[...] = jnp.zeros((8, C), F32)

        xr = xr_ref[...]
        edge = ext[...]
        xc = cb_ref[...] + cw_ref[3:4, :] * xr
        for k in range(3):
            xc = xc + cw_ref[k:k + 1, :] * _shift_rows(xr, 3 - k, edge)
        ext[...] = xr[tm - 8:tm, :]
        xc_ref[...] = xc
        _, ii, _, a, m = _lru_gates(xc, wa_ref[...], ba_ref[...], wx_ref[...], bx_ref[...], lam_ref[...])
        b = m * ii * xc
        r8 = _rows8(tm, C)
        for d in (1, 2, 4):
            ok = r8 >= d
            a_sh = jnp.where(ok, pltpu.roll(a, d, 0), 1.0)
            b_sh = jnp.where(ok, pltpu.roll(b, d, 0), 0.0)
            b = a * b_sh + b
            a = a * a_sh
        a_s[...] = a
        b_s[...] = b

        def step(g, hin):
            s = pl.multiple_of(g * 8, 8)
            hg = a_s[pl.ds(s, 8), :] * hin + b_s[pl.ds(s, 8), :]
            h_ref[pl.ds(s, 8), :] = hg
            return jnp.broadcast_to(hg[7:8, :], (8, C))

        carry[...] = lax.fori_loop(0, tm // 8, step, carry[...], unroll=4)
        ge, _ = _gelu(gr_ref[...])
        rec_ref[...] = (h_ref[...] * ge).astype(BF16)

    vec = _full((1, C))
    in_specs = [_rows(tm, C), _rows(tm, C), _full((4, C)), vec, _full((C, C)), vec, _full((C, C)), vec, vec]
    out_specs = [_rows(tm, C), _rows(tm, C), _rows(tm, C)]
    out_shape = [jax.ShapeDtypeStruct((T, C), F32), jax.ShapeDtypeStruct((T, C), F32), jax.ShapeDtypeStruct((T, C), BF16)]
    scratch = [pltpu.VMEM((8, C), F32), pltpu.VMEM((tm, C), F32), pltpu.VMEM((tm, C), F32), pltpu.VMEM((8, C), F32)]
    return _launch(body, "rnn_fwd", (T // tm,), in_specs, out_specs, out_shape, scratch,
                   (xr, gr, cw, cb, wa, ba, wx, bx, lam), exchange)


def _rnn_bwd(drec, gr, h, xc, xr, cw, wa, ba, wx, bx, lam, exchange=None):
    T = xr.shape[0]
    tm = 512
    C = D_RNN
    nt = T // tm
    t8 = tm // 8

    def body(drec_ref, gr_ref, h_ref, hp_ref, xc_ref, xr_ref, cw_ref, wa_ref, ba_ref, wx_ref, bx_ref,
             lam_ref, dxr_ref, dgr_ref, dwa_ref, dwx_ref, dvec_ref, c_s, g_s, gout, ext, anext, gcarry):
        i = pl.program_id(0)
        j = nt - 1 - i

        @pl.when(i == 0)
        def _():
            dwa_ref[...] = jnp.zeros_like(dwa_ref)
            dwx_ref[...] = jnp.zeros_like(dwx_ref)
            dvec_ref[...] = jnp.zeros_like(dvec_ref)
            anext[...] = jnp.zeros((8, C), F32)
            gcarry[...] = jnp.zeros((8, C), F32)
            ext[...] = jnp.zeros((8, C), F32)

        xc = xc_ref[...]
        lam = lam_ref[...]
        r, ii, sp, a, m = _lru_gates(xc, wa_ref[...], ba_ref[...], wx_ref[...], bx_ref[...], lam)
        ge, dge = _gelu(gr_ref[...])
        drec = drec_ref[...]
        hh = h_ref[...]
        dgr_ref[...] = (drec * hh * dge).astype(BF16)
        dh = drec * ge
        rowi = lax.broadcasted_iota(jnp.int32, (tm, C), 0)
        c = jnp.where(rowi == tm - 1, jnp.broadcast_to(anext[0:1, :], (tm, C)), pltpu.roll(a, tm - 1, 0))
        anext[...] = a[0:8, :]
        r8 = rowi & 7
        gg = dh
        for d in (1, 2, 4):
            ok = r8 < 8 - d
            c_sh = jnp.where(ok, pltpu.roll(c, tm - d, 0), 1.0)
            g_sh = jnp.where(ok, pltpu.roll(gg, tm - d, 0), 0.0)
            gg = c * g_sh + gg
            c = c * c_sh
        c_s[...] = c
        g_s[...] = gg

        def step(k, gin):
            s = pl.multiple_of((t8 - 1 - k) * 8, 8)
            og = c_s[pl.ds(s, 8), :] * gin + g_s[pl.ds(s, 8), :]
            gout[pl.ds(s, 8), :] = og
            return jnp.broadcast_to(og[0:1, :], (8, C))

        gcarry[...] = lax.fori_loop(0, t8, step, gcarry[...], unroll=4)
        G = gout[...]
        hprev_row = jnp.where(j > 0, hp_ref[7:8, :], 0.0)
        hprev = jnp.where(rowi == 0, jnp.broadcast_to(hprev_row, (tm, C)), pltpu.roll(hh, 1, 0))
        da = G * hprev
        dm = G * ii * xc
        di = G * m * xc
        dxc = G * m * ii
        dla = da * a - dm * a * a / m
        dr = dla * (-LRU_C * sp)
        dsp = _colsum(dla * (-LRU_C * r))
        dlam = dsp * (-_sigmoid(-lam))
        dpr = dr * r * (1.0 - r)
        dpi = di * ii * (1.0 - ii)
        dxc = dxc + _mm_nt(dpr, wa_ref[...]) + _mm_nt(dpi, wx_ref[...])
        dwa_ref[...] += _mm_tn(xc, dpr)
        dwx_ref[...] += _mm_tn(xc, dpi)
        dvec_ref[0:1, :] += _colsum(dpr)
        dvec_ref[1:2, :] += _colsum(dpi)
        dvec_ref[2:3, :] += dlam
        dvec_ref[3:4, :] += _colsum(dxc)
        edge = ext[...]
        xr = xr_ref[...]
        dxr = cw_ref[3:4, :] * dxc
        dvec_ref[7:8, :] += _colsum(dxc * xr)
        for k in range(3):
            up = _shift_rows(dxc, k - 3, edge)
            dxr = dxr + cw_ref[k:k + 1, :] * up
            dvec_ref[4 + k:5 + k, :] += _colsum(up * xr)
        ext[...] = dxc[0:8, :]
        dxr_ref[...] = dxr.astype(BF16)

    rev = lambda i: nt - 1 - i
    prev8 = lambda i: jnp.maximum((nt - 1 - i) * t8 - 1, 0)
    vec = _full((1, C))
    return _launch(
        body, "rnn_bwd", (nt,),
        [_rows(tm, C, rev), _rows(tm, C, rev), _rows(tm, C, rev), _rows(8, C, prev8), _rows(tm, C, rev),
         _rows(tm, C, rev), _full((4, C)), _full((C, C)), vec, _full((C, C)), vec, vec],
        [_rows(tm, C, rev), _rows(tm, C, rev), _full((C, C)), _full((C, C)), _full((8, C))],
        [jax.ShapeDtypeStruct((T, C), BF16), jax.ShapeDtypeStruct((T, C), BF16),
         jax.ShapeDtypeStruct((C, C), F32), jax.ShapeDtypeStruct((C, C), F32), jax.ShapeDtypeStruct((8, C), F32)],
        [pltpu.VMEM((tm, C), F32), pltpu.VMEM((tm, C), F32), pltpu.VMEM((tm, C), F32),
         pltpu.VMEM((8, C), F32), pltpu.VMEM((8, C), F32), pltpu.VMEM((8, C), F32)],
        (drec, gr, h, h, xc, xr, cw, wa, ba, wx, bx, lam), exchange)


def _out_proj(att, rec, x, w_out, g1, b1):
    T = x.shape[0]
    tm = min(1024, T)

    def body(att_ref, rec_ref, x_ref, w_ref, g1_ref, b1_ref, z_ref, h_ref):
        mix = _mm(att_ref[...], w_ref[0:512, :]) + _mm(rec_ref[...], w_ref[512:1024, :])
        z1 = ALPHA * x_ref[...] + mix
        z_ref[...] = z1
        h1, _, _ = _ln(z1, g1_ref[...], b1_ref[...])
        h_ref[...] = h1.astype(MXU_DTYPE).astype(BF16)

    return pl.pallas_call(
        body, name="out_proj", grid=(T // tm,),
        in_specs=[_rows(tm, 512), _rows(tm, 512), _rows(tm, D), _full((D, D)), _full((1, D)), _full((1, D))],
        out_specs=[_rows(tm, D), _rows(tm, D)],
        out_shape=[jax.ShapeDtypeStruct((T, D), F32), jax.ShapeDtypeStruct((T, D), BF16)],
        compiler_params=_params(),
    )(att, rec, x, w_out, g1, b1)


NC = D_FF // FF_CHUNK


def _ffn_up(h1b, w_up_t, fcw, fcb, exchange=None):
    T = h1b.shape[0]
    tm = min(1024, T)
    CW = FF_CHUNK

    def body(h_ref, wg_ref, wv_ref, fcw_ref, fcb_ref, gate_ref, ge_ref, vd_ref, act_ref, before):
        i = pl.program_id(1)

        @pl.when(i == 0)
        def _():
            before[...] = jnp.zeros((8, CW), F32)

        hb = h_ref[...]
        gate = _mm_nt(hb, wg_ref[...])
        val = _mm_nt(hb, wv_ref[...])
        gate_ref[...] = gate.astype(BF16)
        edge = before[...]
        gc = (fcb_ref[...] + fcw_ref[0:1, :] * _shift_rows(gate, 2, edge) + fcw_ref[1:2, :] * _shift_rows(gate, 1, edge)
              + fcw_ref[2:3, :] * gate)
        before[...] = gate[tm - 8:tm, :]
        ge, dge = _gelu(gc)
        ge_ref[...] = ge.astype(BF16)
        vd_ref[...] = (val * dge).astype(BF16)
        act_ref[...] = (ge * val).astype(BF16)

    chunk = pl.BlockSpec((None, tm, CW), lambda c, i: (c, i, 0))
    return _launch(
        body, "ffn_up", (NC, T // tm),
        [pl.BlockSpec((tm, D), lambda c, i: (i, 0)), pl.BlockSpec((CW, D), lambda c, i: (c, 0)),
         pl.BlockSpec((CW, D), lambda c, i: (NC + c, 0)), pl.BlockSpec((None, 3, CW), lambda c, i: (c, 0, 0)),
         pl.BlockSpec((None, 1, CW), lambda c, i: (c, 0, 0))],
        [chunk] * 3 + [pl.BlockSpec((tm, CW), lambda c, i: (i, c))],
        [jax.ShapeDtypeStruct((NC, T, CW), BF16)] * 3 + [jax.ShapeDtypeStruct((T, D_FF), BF16)], [pltpu.VMEM((8, CW), F32)],
        (h1b, w_up_t, w_up_t, fcw, fcb), exchange)


def _ffn_down(act, z1, p, tgt, w_down, w_g, w_p_t, g1, b1, g2, b2, bg):
    T = z1.shape[0]
    tm = 512

    def body(act_ref, z_ref, p_ref, t_ref, wdn_hbm, wg_hbm, wp_hbm, g1_ref, b1_ref, g2_ref, b2_ref, bg_ref,
             dz2_ref, dz2b_ref, dpre_ref, dpp_ref, vec_ref, wdn, wg, wp, sems):
        @pl.when(pl.program_id(0) == 0)
        def _():
            copies = [pltpu.make_async_copy(src, dst, sems.at[n])
                      for n, (src, dst) in enumerate(((wdn_hbm, wdn), (wg_hbm, wg), (wp_hbm, wp)))]
            for cp in copies:
                cp.start()
            vec_ref[...] = jnp.zeros_like(vec_ref)
            for cp in copies:
                cp.wait()

        g2v = g2_ref[...]
        for r in (slice(0, tm // 2), slice(tm // 2, tm)):
            h1, _, _ = _ln(z_ref[r, :], g1_ref[...], b1_ref[...])
            h1b = h1.astype(MXU_DTYPE)
            ffn = _mm(act_ref[r, :], wdn[...])
            sg = _sigmoid(_mm(h1b, wg[...]) + bg_ref[...])
            pp = _mm_nt(p_ref[r, :], wp[...])
            z2 = ALPHA * h1 + ffn + sg * pp
            y, xh2, rstd2 = _ln(z2, g2v, b2_ref[...])
            diff = y - t_ref[r, :]
            dy = diff * (1.0 / D)
            dz2 = _ln_bwd(dy, xh2, rstd2, g2v)
            dpre = dz2 * pp * sg * (1.0 - sg)
            dz2_ref[r, :] = dz2
            dz2b_ref[r, :] = dz2.astype(BF16)
            dpre_ref[r, :] = dpre.astype(BF16)
            dpp_ref[r, :] = (dz2 * sg).astype(BF16)
            loss = 0.5 * jnp.sum(jnp.sum(diff * diff, axis=1, keepdims=True), axis=0, keepdims=True) * (1.0 / D)
            vec_ref[0:1, :] += jnp.broadcast_to(loss, (1, D))
            vec_ref[1:2, :] += _colsum(dy * xh2)
            vec_ref[2:3, :] += _colsum(dy)
            vec_ref[3:4, :] += _colsum(dpre)

    anyspec = pl.BlockSpec(memory_space=pl.ANY)
    vec = _full((1, D))
    return pl.pallas_call(
        body, name="ffn_down", grid=(T // tm,),
        in_specs=[_rows(tm, D_FF), _rows(tm, D), _rows(tm, PLE), _rows(tm, D),
                  anyspec, anyspec, anyspec] + [vec] * 5,
        out_specs=[_rows(tm, D)] * 4 + [_full((8, D))],
        out_shape=[jax.ShapeDtypeStruct((T, D), F32)] + [jax.ShapeDtypeStruct((T, D), BF16)] * 3
                  + [jax.ShapeDtypeStruct((8, D), F32)],
        scratch_shapes=[pltpu.VMEM((D_FF, D), MXU_DTYPE), pltpu.VMEM((D, D), MXU_DTYPE), pltpu.VMEM((D, PLE), MXU_DTYPE),
                        pltpu.SemaphoreType.DMA((3,))],
        compiler_params=_params(),
    )(act, z1, p, tgt, w_down, w_g, w_p_t, g1, b1, g2, b2, bg)


def _ffn_bwd(dz2b, gate, ge, vd, w_down, fcw):
    T = dz2b.shape[0]
    tm = min(1024, T)
    CW = FF_CHUNK
    nt = T // tm

    def body(dz_ref, wdn_ref, gate_ref, ge_ref, vd_ref, fcw_ref, dup_ref, dfc_ref, after):
        i = pl.program_id(1)

        @pl.when(i == 0)
        def _():
            after[...] = jnp.zeros((8, CW), F32)
            dfc_ref[...] = jnp.zeros_like(dfc_ref)

        gate = gate_ref[...].astype(F32)
        dact = _mm_nt(dz_ref[...], wdn_ref[...])
        dgc = dact * vd_ref[...].astype(F32)
        edge = after[...]
        dgc1 = _shift_rows(dgc, -1, edge)
        dgc2 = _shift_rows(dgc, -2, edge)
        after[...] = dgc[0:8, :]
        dup_ref[:, 0:CW] = (fcw_ref[2:3, :] * dgc + fcw_ref[1:2, :] * dgc1 + fcw_ref[0:1, :] * dgc2).astype(BF16)
        dup_ref[:, CW:2 * CW] = (dact * ge_ref[...].astype(F32)).astype(BF16)
        dfc_ref[0:1, :] += _colsum(dgc2 * gate)
        dfc_ref[1:2, :] += _colsum(dgc1 * gate)
        dfc_ref[2:3, :] += _colsum(dgc * gate)
        dfc_ref[3:4, :] += _colsum(dgc)

    rev = lambda c, i: (c, nt - 1 - i, 0)
    chunk = pl.BlockSpec((None, tm, CW), rev)
    return pl.pallas_call(
        body, name="ffn_bwd", grid=(NC, nt),
        in_specs=[pl.BlockSpec((tm, D), lambda c, i: (nt - 1 - i, 0)), pl.BlockSpec((CW, D), lambda c, i: (c, 0)),
                  chunk, chunk, chunk, pl.BlockSpec((None, 3, CW), lambda c, i: (c, 0, 0))],
        out_specs=[pl.BlockSpec((tm, 2 * CW), lambda c, i: (nt - 1 - i, c)),
                   pl.BlockSpec((None, 8, CW), lambda c, i: (c, 0, 0))],
        out_shape=[jax.ShapeDtypeStruct((T, 2 * D_FF), BF16), jax.ShapeDtypeStruct((NC, 8, CW), F32)],
        scratch_shapes=[pltpu.VMEM((8, CW), F32)],
        compiler_params=_params(),
    )(dz2b, w_down, gate, ge, vd, fcw)


def _ffn_dh1(dup, dz2, dpre, z1, w_up_t, w_g, g1, b1):
    T = z1.shape[0]
    tm = 512

    def body(dup_ref, dz2_ref, dpre_ref, z_ref, wup_hbm, wg_hbm, g1_ref, b1_ref, dz1_ref, vec_ref, wup, wg, sems):
        @pl.when(pl.program_id(0) == 0)
        def _():
            copies = [pltpu.make_async_copy(wup_hbm.at[pl.ds(s * D_FF + c * FF_CHUNK, FF_CHUNK)],
                                            wup.at[pl.ds((2 * c + s) * FF_CHUNK, FF_CHUNK)], sems.at[2 * c + s])
                      for c in range(NC) for s in range(2)]
            copies.append(pltpu.make_async_copy(wg_hbm, wg, sems.at[2 * NC]))
            for cp in copies:
                cp.start()
            vec_ref[...] = jnp.zeros_like(vec_ref)
            for cp in copies:
                cp.wait()

        g1v = g1_ref[...]
        _, xh1, rstd1 = _ln(z_ref[...], g1v, b1_ref[...])
        dh1 = ALPHA * dz2_ref[...] + _mm_nt(dpre_ref[...], wg[...]) + _mm(dup_ref[...], wup[...])
        dz1_ref[...] = _ln_bwd(dh1, xh1, rstd1, g1v)
        vec_ref[0:1, :] += _colsum(dh1 * xh1)
        vec_ref[1:2, :] += _colsum(dh1)

    anyspec = pl.BlockSpec(memory_space=pl.ANY)
    vec = _full((1, D))
    return pl.pallas_call(
        body, name="ffn_dh1", grid=(T // tm,),
        in_specs=[_rows(tm, 2 * D_FF), _rows(tm, D), _rows(tm, D), _rows(tm, D),
                  anyspec, anyspec, vec, vec],
        out_specs=[_rows(tm, D), _full((8, D))],
        out_shape=[jax.ShapeDtypeStruct((T, D), F32), jax.ShapeDtypeStruct((8, D), F32)],
        scratch_shapes=[pltpu.VMEM((2 * D_FF, D), MXU_DTYPE), pltpu.VMEM((D, D), MXU_DTYPE),
                        pltpu.SemaphoreType.DMA((2 * NC + 1,))],
        compiler_params=_params(),
    )(dup, dz2, dpre, z1, w_up_t, w_g, g1, b1)


def _out_proj_bwd(dz1, w_out, exchange=None):
    T = dz1.shape[0]
    tm = min(1024, T)

    def body(dz_ref, w_ref, datt_ref, drec_ref):
        dzb = dz_ref[...].astype(MXU_DTYPE)
        datt = _mm_nt(dzb, w_ref[0:512, :])
        for h in range(HEADS):
            datt_ref[h] = datt[:, h * 64:(h + 1) * 64].astype(BF16)
        drec_ref[...] = _mm_nt(dzb, w_ref[512:1024, :])

    return _launch(body, "out_proj_bwd", (T // tm,), [_rows(tm, D), _full((D, D))], [_heads(tm), _rows(tm, 512)],
                   [jax.ShapeDtypeStruct((HEADS, T, 64), BF16), jax.ShapeDtypeStruct((T, 512), F32)], [],
                   (dz1, w_out), exchange)


def _in_proj_bwd(dq, dkv, dxr, dgr, dz1, w_in_t, exchange=None):
    T = dz1.shape[0]
    tm = 512
    W = D_IN // 4

    def body(dq_ref, dkv_ref, dxr_ref, dgr_ref, dz_ref, w_ref, dx_ref, du_ref):
        dkv = dkv_ref[...]
        dx_ref[...] = (ALPHA * dz_ref[...] + _mm(dq_ref[...], w_ref[0:512, :]) + _mm(dkv, w_ref[512:768, :])
                       + _mm(dxr_ref[...], w_ref[768:1280, :]) + _mm(dgr_ref[...], w_ref[1280:1792, :]))
        dq, dxr, dgr = dq_ref[...].astype(F32), dxr_ref[...].astype(F32), dgr_ref[...].astype(F32)
        du_ref[0] = dq[:, 0:W].astype(BF16)
        du_ref[1, :, 0:64] = dq[:, W:512].astype(BF16)
        du_ref[1, :, 64:320] = dkv.astype(BF16)
        du_ref[1, :, 320:W] = dxr[:, 0:128].astype(BF16)
        du_ref[2, :, 0:384] = dxr[:, 128:512].astype(BF16)
        du_ref[2, :, 384:W] = dgr[:, 0:64].astype(BF16)
        du_ref[3] = dgr[:, 64:512].astype(BF16)

    return _launch(body, "in_proj_bwd", (T // tm,),
                   [_rows(tm, 512), _rows(tm, 256), _rows(tm, 512), _rows(tm, 512), _rows(tm, D), _full((D_IN, D))],
                   [_rows(tm, D), pl.BlockSpec((4, tm, W), lambda i: (0, i, 0))],
                   [jax.ShapeDtypeStruct((T, D), F32), jax.ShapeDtypeStruct((4, T, W), BF16)], [],
                   (dq, dkv, dxr, dgr, dz1, w_in_t), exchange)


def _accumulate_tn(a_ref, b_ref, o_ref):
    @pl.when(pl.program_id(1) == 0)
    def _():
        o_ref[...] = jnp.zeros_like(o_ref)

    o_ref[...] += _mm_tn(a_ref[...], b_ref[...])


def _weight_grad_cols(a, b, name, n_blocks, b_spec, out_shape, out_spec, exchange=None):
    T, M = a.shape
    bt = min(DW_TOKENS, T)
    return _launch(functools.partial(_accumulate_tn), name, (n_blocks, T // bt),
                   [pl.BlockSpec((bt, M), lambda m, k: (k, 0)), b_spec(bt)], [out_spec],
                   [jax.ShapeDtypeStruct(out_shape, F32)], [], (a, b), exchange)


def _dw_out(att, rec, dz1):
    T = dz1.shape[0]
    bt = min(DW_TOKENS // 2, T)

    def body(att_ref, rec_ref, dz_ref, o_ref):
        @pl.when(pl.program_id(0) == 0)
        def _():
            o_ref[...] = jnp.zeros_like(o_ref)

        dz = dz_ref[...].astype(MXU_DTYPE)
        o_ref[0:512, :] += _mm_tn(att_ref[...], dz)
        o_ref[512:1024, :] += _mm_tn(rec_ref[...], dz)

    return pl.pallas_call(
        body, name="dw_out", grid=(T // bt,), in_specs=[_rows(bt, 512), _rows(bt, 512), _rows(bt, D)],
        out_specs=_full((D, D)), out_shape=jax.ShapeDtypeStruct((D, D), F32), compiler_params=_params())(att, rec, dz1)


def _weight_grad(a, b, bm, name, exchange=None):
    bt = min(DW_TOKENS // 2 if b.dtype == F32 else DW_TOKENS, b.shape[0])
    if a.ndim == 3:
        assert a.shape[2] == bm
        T, M = a.shape[1], a.shape[0] * bm
        a_spec = pl.BlockSpec((None, bt, bm), lambda m, k: (m, k, 0))
    else:
        T, M = a.shape
        a_spec = pl.BlockSpec((bt, bm), lambda m, k: (k, m))
    N = b.shape[1]
    nk = T // bt

    out = _launch(functools.partial(_accumulate_tn), name, (M // bm, nk),
                  [a_spec, pl.BlockSpec((bt, N), lambda m, k: (k, 0))], [pl.BlockSpec((bm, N), lambda m, k: (m, 0))],
                  [jax.ShapeDtypeStruct((M, N), F32)], [], (a, b), exchange)
    return out[0] if exchange is None else out


def _adamw(w, g, m, v, name):
    R, C = w.shape
    tr = R // 8 if R % 64 == 0 else R
    c1 = 1.0 / (1.0 - ADAM_B1 ** ADAM_STEP)
    c2 = 1.0 / (1.0 - ADAM_B2 ** ADAM_STEP)

    def body(w_ref, g_ref, m_ref, v_ref, d_ref, nm_ref, nv_ref):
        g = g_ref[...]
        nm = ADAM_B1 * m_ref[...] + (1.0 - ADAM_B1) * g
        nv = ADAM_B2 * v_ref[...] + (1.0 - ADAM_B2) * g * g
        nm_ref[...] = nm
        nv_ref[...] = nv
        d_ref[...] = -ADAM_LR * ((nm * c1) / (jnp.sqrt(nv * c2) + ADAM_EPS) + ADAM_WD * w_ref[...])

    spec = pl.BlockSpec((tr, C), lambda i: (i, 0))
    return pl.pallas_call(
        body, name=name, grid=(R // tr,),
        in_specs=[spec] * 4, out_specs=[spec] * 3,
        out_shape=[jax.ShapeDtypeStruct((R, C), F32)] * 3,
        compiler_params=_params(),
    )(w, g, m, v)


def _adamw_halves(ws, mines, sibs, ms, vs, c, name, exchange=None):
    n, nb = len(ws), 4
    c1 = 1.0 / (1.0 - ADAM_B1 ** ADAM_STEP)
    c2 = 1.0 / (1.0 - ADAM_B2 ** ADAM_STEP)

    def body(c_ref, *refs):
        own = (pl.program_id(0) // nb) == c_ref[0]
        for i in range(n):
            w_ref, a_ref, b_ref, m_ref, v_ref = refs[5 * i:5 * i + 5]
            g_ref, d_ref, nm_ref, nv_ref = refs[5 * n + 4 * i:5 * n + 4 * i + 4]
            g = jnp.where(own, a_ref[...], b_ref[...])
            nm = ADAM_B1 * m_ref[...] + (1.0 - ADAM_B1) * g
            nv = ADAM_B2 * v_ref[...] + (1.0 - ADAM_B2) * g * g
            g_ref[...] = g
            nm_ref[...] = nm
            nv_ref[...] = nv
            d_ref[...] = -ADAM_LR * ((nm * c1) / (jnp.sqrt(nv * c2) + ADAM_EPS) + ADAM_WD * w_ref[...])

    in_specs, out_specs, out_shape, args = [], [], [], []
    for w, a, b, m, v in zip(ws, mines, sibs, ms, vs):
        R, C = w.shape
        tr = R // (2 * nb)
        assert tr % 8 == 0 and a.shape == (R // 2, C)
        full = pl.BlockSpec((tr, C), lambda i, c_ref: (i, 0))
        mine_spec = pl.BlockSpec((tr, C), lambda i, c_ref: (jnp.where(i // nb == c_ref[0], i % nb, nb - 1), 0))
        sib_spec = pl.BlockSpec((tr, C), lambda i, c_ref: (jnp.where(i // nb == c_ref[0], nb - 1, i % nb), 0))
        in_specs += [full, mine_spec, sib_spec, full, full]
        out_specs += [full] * 4
        out_shape += [jax.ShapeDtypeStruct((R, C), F32)] * 4
        args += [w, a, b, m, v]
    out = _launch(body, name, (2 * nb,), in_specs, out_specs, out_shape, [], (c, *args), exchange, prefetch=1)
    return [tuple(out[4 * i:4 * i + 4]) for i in range(n)], list(out[4 * n:])


def _add4(fs, name):
    n = len(fs)

    def body(*refs):
        for a_ref, o_ref in zip(refs[:n], refs[n:]):
            o_ref[...] = ((a_ref[0].astype(F32) + a_ref[1].astype(F32)) + a_ref[2].astype(F32)) + a_ref[3].astype(F32)

    for f in fs:
        assert (f.shape[1] // 2) % 16 == 0
    return pl.pallas_call(
        body, name=name, grid=(2,),
        in_specs=[pl.BlockSpec((4, f.shape[1] // 2, f.shape[2]), lambda i: (0, i, 0)) for f in fs],
        out_specs=[pl.BlockSpec((f.shape[1] // 2, f.shape[2]), lambda i: (i, 0)) for f in fs],
        out_shape=[jax.ShapeDtypeStruct(f.shape[1:], F32) for f in fs], compiler_params=_params())(*fs)


def _gather_first(wsrc, cpack):
    def body(w_ref, c_ref, gw_ref, gc_ref, send_sems, recv_sems, local_sem, csend, crecv, clocal):
        x, y, c = _pos()
        me = 2 * x + y
        chips = _other_chips(x, y)
        start, forward, finish = _gather_steps(w_ref, gw_ref, send_sems, recv_sems, local_sem)
        start()
        loc = pltpu.make_async_copy(c_ref, gc_ref.at[me], clocal)
        loc.start()

        def conv_copy(k, slot):
            px, py = chips[k]
            return pltpu.make_async_remote_copy(src_ref=c_ref, dst_ref=gc_ref.at[slot], send_sem=csend.at[k],
                                                recv_sem=crecv.at[k], device_id=(px, py, c), device_id_type=MESH)

        for k in range(3):
            conv_copy(k, me).start()
        forward()
        finish()
        for k, (px, py) in enumerate(chips):
            conv_copy(k, 2 * px + py).wait_recv()
        for k in range(3):
            conv_copy(k, me).wait_send()
        loc.wait()

    anyspec = pl.BlockSpec(memory_space=pl.ANY)
    return pl.pallas_call(
        body, name="gather_first",
        in_specs=[anyspec, anyspec], out_specs=[anyspec, anyspec],
        out_shape=[jax.ShapeDtypeStruct((4,) + wsrc.shape, wsrc.dtype), jax.ShapeDtypeStruct((4,) + cpack.shape, cpack.dtype)],
        scratch_shapes=GATHER_SCRATCH + [pltpu.SemaphoreType.DMA((3,)), pltpu.SemaphoreType.DMA((3,)), pltpu.SemaphoreType.DMA],
        compiler_params=_params(has_side_effects=True),
    )(wsrc, cpack)


def _all_devices_exchange(s):
    def make(ins, outs, sems):
        s_ref, o_ref = ins[0], outs[0]
        send_sems, recv_sems, local_sem = sems
        x, y, c = _pos()
        me = 4 * x + 2 * y + c
        loc = pltpu.make_async_copy(s_ref, o_ref.at[me], local_sem)

        def copy(k, slot):
            peer = (x ^ (k >> 2), y ^ ((k >> 1) & 1), c ^ (k & 1))
            return pltpu.make_async_remote_copy(src_ref=s_ref, dst_ref=o_ref.at[slot], send_sem=send_sems.at[k - 1],
                                                recv_sem=recv_sems.at[k - 1], device_id=peer, device_id_type=MESH)

        def start():
            loc.start()
            for k in range(1, 8):
                copy(k, me).start()

        def finish():
            for k in range(1, 8):
                copy(k, 4 * (x ^ (k >> 2)) + 2 * (y ^ ((k >> 1) & 1)) + (c ^ (k & 1))).wait_recv()
            for k in range(1, 8):
                copy(k, me).wait_send()
            loc.wait()

        return start, lambda: None, finish

    return _Exchange([s], [jax.ShapeDtypeStruct((8,) + s.shape, s.dtype)],
                     [pltpu.SemaphoreType.DMA((7,)), pltpu.SemaphoreType.DMA((7,)), pltpu.SemaphoreType.DMA], make)


def _sum_devices(a):
    def body(a_ref, o_ref):
        acc = a_ref[0]
        for d in range(1, 8):
            acc = acc + a_ref[d]
        o_ref[...] = acc

    vm = pl.BlockSpec(memory_space=pltpu.VMEM)
    return pl.pallas_call(body, name="sum_devices", in_specs=[vm], out_specs=vm,
                          out_shape=jax.ShapeDtypeStruct(a.shape[1:], F32), compiler_params=_params())(a)


def _swap_exchange(gs):
    n = len(gs)

    def make(ins, outs, sems):
        x, y, c = _pos()
        cps = []
        for i in range(n):
            half = gs[i].shape[1] // 2
            rows = pl.ds(pl.multiple_of((1 - c) * half, 8), half)
            cps.append(pltpu.make_async_remote_copy(src_ref=ins[i].at[:, rows, :], dst_ref=outs[i], send_sem=sems[0].at[i],
                                                    recv_sem=sems[1].at[i], device_id=(x, y, 1 - c), device_id_type=MESH))

        def start():
            for cp in cps:
                cp.start()

        def finish():
            for cp in cps:
                cp.wait()

        return start, lambda: None, finish

    return _Exchange(gs, [jax.ShapeDtypeStruct((4, g.shape[1] // 2, g.shape[2]), g.dtype) for g in gs],
                     [pltpu.SemaphoreType.DMA((n,)), pltpu.SemaphoreType.DMA((n,))], make)


def _scatter_exchange(ss):
    n = len(ss)

    def make(ins, outs, sems):
        send_sems, recv_sems, local_sems = sems
        x, y, c = _pos()
        me = 2 * x + y
        chips = _other_chips(x, y)
        locs = [pltpu.make_async_copy(ins[i].at[me], outs[i].at[me], local_sems.at[i]) for i in range(n)]

        def copy(i, k, src_slot, dst_slot):
            px, py = chips[k]
            return pltpu.make_async_remote_copy(src_ref=ins[i].at[src_slot], dst_ref=outs[i].at[dst_slot],
                                                send_sem=send_sems.at[3 * i + k], recv_sem=recv_sems.at[3 * i + k],
                                                device_id=(px, py, c), device_id_type=MESH)

        def start():
            for i in range(n):
                locs[i].start()
                for k, (px, py) in enumerate(chips):
                    copy(i, k, 2 * px + py, me).start()

        def finish():
            for i in range(n):
                for k, (px, py) in enumerate(chips):
                    copy(i, k, me, 2 * px + py).wait_recv()
            for i in range(n):
                for k, (px, py) in enumerate(chips):
                    copy(i, k, 2 * px + py, me).wait_send()
                locs[i].wait()

        return start, lambda: None, finish

    return _Exchange(ss, [jax.ShapeDtypeStruct(s.shape, s.dtype) for s in ss],
                     [pltpu.SemaphoreType.DMA((3 * n,)), pltpu.SemaphoreType.DMA((3 * n,)), pltpu.SemaphoreType.DMA((n,))], make)


def _send_exchange(rs):
    n = len(rs)

    def make(ins, outs, sems):
        x, y, c = _pos()
        cps = [pltpu.make_async_remote_copy(src_ref=ins[i], dst_ref=outs[i], send_sem=sems[0].at[i], recv_sem=sems[1].at[i],
                                            device_id=(x, y, 1 - c), device_id_type=MESH) for i in range(n)]

        def start():
            for cp in cps:
                cp.start()

        def finish():
            for cp in cps:
                cp.wait()

        return start, lambda: None, finish

    return _Exchange(rs, [jax.ShapeDtypeStruct(r.shape, r.dtype) for r in rs],
                     [pltpu.SemaphoreType.DMA((n,)), pltpu.SemaphoreType.DMA((n,))], make)


def _reduce_in_vmem(g):
    _, R, C = g.shape
    H = R // 2

    def body(g_ref, mine_ref, other_ref, sib, part, got, swap_sems, send_sems, recv_sems, last_sems):
        x, y, c = _pos()
        me = 2 * x + y
        chips = _other_chips(x, y)
        sibling = (x, y, 1 - c)
        mine = pl.ds(pl.multiple_of(c * H, 8), H)
        theirs = pl.ds(pl.multiple_of((1 - c) * H, 8), H)
        swap = pltpu.make_async_remote_copy(src_ref=g_ref.at[:, theirs, :], dst_ref=sib, send_sem=swap_sems.at[0],
                                            recv_sem=swap_sems.at[1], device_id=sibling, device_id_type=MESH)
        swap.start()
        swap.wait()
        part[...] = (g_ref[:, mine, :] + sib[...]).astype(BF16)

        def copy(k, src_slot, dst_slot):
            px, py = chips[k]
            return pltpu.make_async_remote_copy(src_ref=part.at[src_slot], dst_ref=got.at[dst_slot], send_sem=send_sems.at[k],
                                                recv_sem=recv_sems.at[k], device_id=(px, py, c), device_id_type=MESH)

        for k, (px, py) in enumerate(chips):
            copy(k, 2 * px + py, me).start()
        got[me] = part[me]
        for k, (px, py) in enumerate(chips):
            copy(k, me, 2 * px + py).wait_recv()
        for k, (px, py) in enumerate(chips):
            copy(k, 2 * px + py, me).wait_send()
        mine_ref[...] = ((got[0].astype(F32) + got[1].astype(F32)) + got[2].astype(F32)) + got[3].astype(F32)
        last = pltpu.make_async_remote_copy(src_ref=mine_ref, dst_ref=other_ref, send_sem=last_sems.at[0],
                                            recv_sem=last_sems.at[1], device_id=sibling, device_id_type=MESH)
        last.start()
        last.wait()

    vm = pl.BlockSpec(memory_space=pltpu.VMEM)
    half = jax.ShapeDtypeStruct((H, C), F32)
    return pl.pallas_call(
        body, name="reduce_late", in_specs=[vm], out_specs=[vm, vm], out_shape=[half, half],
        scratch_shapes=[pltpu.VMEM((4, H, C), F32), pltpu.VMEM((4, H, C), BF16), pltpu.VMEM((4, H, C), BF16),
                        pltpu.SemaphoreType.DMA((2,)), pltpu.SemaphoreType.DMA((3,)), pltpu.SemaphoreType.DMA((3,)),
                        pltpu.SemaphoreType.DMA((2,))],
        compiler_params=_params(has_side_effects=True))(g)


def _add_half(gs, rs, c, name):
    n = len(gs)

    def body(c_ref, *refs):
        for g_ref, r_ref, o_ref in zip(refs[:n], refs[n:2 * n], refs[2 * n:]):
            o_ref[...] = (g_ref[...] + r_ref[...]).astype(BF16)

    g_specs, r_specs, out_shape = [], [], []
    for g, r in zip(gs, rs):
        _, H, C = r.shape
        tr = H // 2
        assert tr % 16 == 0 and g.shape == (4, 2 * H, C)
        g_specs.append(pl.BlockSpec((1, tr, C), lambda j, i, c_ref: (j, c_ref[0] * 2 + i, 0)))
        r_specs.append(pl.BlockSpec((1, tr, C), lambda j, i, c_ref: (j, i, 0)))
        out_shape.append(jax.ShapeDtypeStruct((4, H, C), BF16))
    grid_spec = pltpu.PrefetchScalarGridSpec(num_scalar_prefetch=1, grid=(4, 2), in_specs=g_specs + r_specs, out_specs=r_specs)
    return pl.pallas_call(body, name=name, grid_spec=grid_spec, out_shape=out_shape, compiler_params=_params())(c, *gs, *rs)


def _block_diag(w):
    eye = jnp.eye(RNN_BLOCKS, dtype=w.dtype)
    return (eye[:, None, :, None] * w[:, :, None, :]).reshape(D_RNN, D_RNN)


def _diag_blocks(wd):
    d = wd.reshape(RNN_BLOCKS, 64, RNN_BLOCKS, 64)
    return jnp.stack([d[h, :, h, :] for h in range(RNN_BLOCKS)])


def _split_pack(a, first, last):
    out, base = {}, PACK_OFF[first]
    for i in range(first, last):
        s = a[:, PACK_OFF[i] - base:PACK_OFF[i + 1] - base]
        out[BIG_KEYS[i]] = s.reshape(4 * 256, 256) if BIG_KEYS[i] == "w_p_t" else s.reshape(-1, 1024)
    return out


def _layer_grads(x, p, tgt, gw, small, shard=None, core=None):
    row = lambda v: v.reshape(1, -1)
    wa = _block_diag(small["gate_a_w"]).astype(MXU_DTYPE)
    wx = _block_diag(small["gate_x_w"]).astype(MXU_DTYPE)
    sinks = small["attn_sinks"].reshape(1, HEADS)

    dist = shard is not None
    q, kv, xr, gr, xb = _in_proj(x, gw["w_in_t"])
    cut = PACK_OFF[1] + PACK_ROWS[1] // 2
    att, *ga = _attn_fwd(q, kv, sinks, _gather_exchange(shard[PACK_OFF[1]:cut]) if dist else None)
    xc, h, rec, *gb = _rnn_fwd(xr, gr, small["rnn_conv_w"], row(small["rnn_conv_b"]), wa, row(small["gate_a_b"]),
                               wx, row(small["gate_x_b"]), row(small["lru_lambda"]),
                               _gather_exchange(shard[cut:PACK_OFF[3]]) if dist else None)
    if dist:
        gw = {**gw, **_split_pack(jnp.concatenate([ga[0], gb[0]], axis=1), 1, 3)}
    g1, b1 = row(small["ln1_g"]), row(small["ln1_b"])
    fcw = small["ffn_conv_w"].reshape(3, NC, FF_CHUNK).transpose(1, 0, 2)
    fcb = small["ffn_conv_b"].reshape(NC, 1, FF_CHUNK)
    z1, h1b = _out_proj(att, rec, x, gw["w_out"], g1, b1)
    gate, ge, vd, act, *gc = _ffn_up(h1b, gw["w_up_t"], fcw, fcb,
                                     _gather_exchange(shard[PACK_OFF[3]:PACK_OFF[6]]) if dist else None)
    if dist:
        gw = {**gw, **_split_pack(gc[0], 3, 6)}
    dz2, dz2b, dpre, dpp, vec2 = _ffn_down(act, z1, p, tgt, gw["w_down"], gw["w_g"], gw["w_p_t"], g1, b1,
                                           row(small["ln2_g"]), row(small["ln2_b"]), row(small["ple_gate_b"]))
    dup, dfc = _ffn_bwd(dz2b, gate, ge, vd, gw["w_down"], fcw)
    dz1, vec1 = _ffn_dh1(dup, dz2, dpre, z1, gw["w_up_t"], gw["w_g"], g1, b1)
    per_chip = 2 * D_FF // 4 // FF_CHUNK
    big = {"w_ffn_up": _weight_grad_cols(
        h1b, dup, "dw_up", 2 * NC, lambda bt: pl.BlockSpec((bt, FF_CHUNK), lambda m, k: (k, m)), (4, D, 2 * D_FF // 4),
        pl.BlockSpec((None, D, FF_CHUNK), lambda m, k: (2 * (m % 2) + (m // 2) // per_chip, 0, (m // 2) % per_chip)))[0]}
    g_dn, *got_up = _weight_grad(act, dz2b, FF_CHUNK, "dw_down", _swap_exchange([big["w_ffn_up"]])) if dist else (
        _weight_grad(act, dz2b, FF_CHUNK, "dw_down"),)
    big["w_ffn_down"] = g_dn.reshape(4, D_FF // 4, D)
    big["ple_gate_w"] = _weight_grad(h1b, dpre, 512, "dw_gate").reshape(4, D // 4, D)
    big["ple_proj"] = _weight_grad(p, dpp, PLE, "dw_proj").reshape(PLE, 4, D // 4).transpose(1, 0, 2)
    big["w_out"] = _dw_out(att, rec, dz1).reshape(4, D // 4, D)
    reduced = None
    if dist:
        g_ffn = [big[k] for k in EARLY_WEIGHTS]
        ex = _swap_exchange(g_ffn[1:])
    datt, drec, *got = _out_proj_bwd(dz1, gw["w_out"], ex if dist else None)
    if dist:
        sums = _add_half(g_ffn, got_up + got, core, "add_half_ffn")
        ex, ex2 = _scatter_exchange(sums[:1]), _scatter_exchange(sums[1:])
    dxr, dgr, dwa, dwx, dvec, *got = _rnn_bwd(drec, gr, h, xc, xr, small["rnn_conv_w"], wa, row(small["gate_a_b"]),
                                              wx, row(small["gate_x_b"]), row(small["lru_lambda"]), ex if dist else None)
    dq, dkv, dsinks, *got2 = _attn_bwd(q, kv, datt, sinks, ex2 if dist else None)
    if dist:
        mine = _add4(got + got2, "add_chips_ffn")
        big = {}
    sg = {
        "attn_sinks": dsinks[:, 0],
        "rnn_conv_w": dvec[4:8],
        "rnn_conv_b": dvec[3],
        "gate_a_w": _diag_blocks(dwa),
        "gate_a_b": dvec[0],
        "gate_x_w": _diag_blocks(dwx),
        "gate_x_b": dvec[1],
        "lru_lambda": dvec[2],
        "ln1_g": vec1[0],
        "ln1_b": vec1[1],
        "ffn_conv_w": dfc[:, 0:3].transpose(1, 0, 2).reshape(3, D_FF),
        "ffn_conv_b": dfc[:, 3].reshape(D_FF),
        "ple_gate_b": vec2[3],
        "ln2_g": vec2[1],
        "ln2_b": vec2[2],
    }
    loss = vec2[0, 0:1]
    grad_x, du = _in_proj_bwd(dq, dkv, dxr, dgr, dz1, gw["w_in_t"])
    ex = None
    if dist:
        ex = _join_exchanges(_send_exchange(mine), _all_devices_exchange(_pack_vecs([sg[k] for k in SMALL] + [loss])[0]))
    big["w_in"], *got = _weight_grad_cols(
        xb, du, "dw_in", 4, lambda bt: pl.BlockSpec((None, bt, D_IN // 4), lambda j, k: (j, k, 0)), (4, D, D_IN // 4),
        pl.BlockSpec((None, D, D_IN // 4), lambda j, k: (j, 0, 0)), ex)
    if dist:
        reduced = (mine, got[:len(mine)])
    return grad_x, big, sg, loss, reduced, got[-1:]


BIG = ("w_in", "w_ffn_up", "w_out", "w_ffn_down", "ple_gate_w", "ple_proj")
BIG_KEYS = ("w_in_t", "w_up_t", "w_out", "w_down", "w_g", "w_p_t")
BIG_T = (True, True, False, False, False, True)
EARLY_WEIGHTS = ("w_ffn_up", "w_ffn_down", "ple_gate_w", "ple_proj", "w_out")
LATE_WEIGHTS = ("w_in",)
SMALL = ("attn_sinks", "rnn_conv_w", "rnn_conv_b", "gate_a_w", "gate_a_b", "gate_x_w", "gate_x_b", "lru_lambda",
         "ln1_g", "ln1_b", "ffn_conv_w", "ffn_conv_b", "ple_gate_b", "ln2_g", "ln2_b")
WEIGHTS = ("w_in", "attn_sinks", "rnn_conv_w", "rnn_conv_b", "gate_a_w", "gate_a_b", "gate_x_w", "gate_x_b",
           "lru_lambda", "w_out", "ln1_g", "ln1_b", "w_ffn_up", "ffn_conv_w", "ffn_conv_b", "w_ffn_down",
           "ple_gate_w", "ple_gate_b", "ple_proj", "ln2_g", "ln2_b")


def _pack_big(d, first=0, last=6):
    parts = []
    for name, t in zip(BIG[first:last], BIG_T[first:last]):
        a = d[name]
        a = a.T if t else a
        parts.append(a.reshape(-1, 1024))
    return jnp.concatenate(parts, axis=0)


def _pack_vecs(items):
    parts, offs, n = [], [], 0
    for a in items:
        f = a.reshape(-1).astype(F32)
        pad = (-f.shape[0]) % 128
        parts.append(jnp.pad(f, (0, pad)))
        offs.append(n)
        n += (f.shape[0] + pad) // 128
    padr = (-n) % 8
    if padr:
        parts.append(jnp.zeros((padr * 128,), F32))
    return jnp.concatenate(parts).reshape(-1, 128), offs


def _unpack_vecs(a, offs, shapes):
    flat = a.reshape(-1)
    out = []
    for o, s in zip(offs, shapes):
        n = 1
        for d in s:
            n *= d
        out.append(flat[o * 128:o * 128 + n].reshape(s))
    return out


def kernel(x, p, w_in, attn_sinks, rnn_conv_w, rnn_conv_b, gate_a_w, gate_a_b, gate_x_w, gate_x_b, lru_lambda, w_out, ln1_g, ln1_b, w_ffn_up, ffn_conv_w, ffn_conv_b, w_ffn_down, ple_gate_w, ple_gate_b, ple_proj, ln2_g, ln2_b, loss_target, m_w_in, m_attn_sinks, m_rnn_conv_w, m_rnn_conv_b, m_gate_a_w, m_gate_a_b, m_gate_x_w, m_gate_x_b, m_lru_lambda, m_w_out, m_ln1_g, m_ln1_b, m_w_ffn_up, m_ffn_conv_w, m_ffn_conv_b, m_w_ffn_down, m_ple_gate_w, m_ple_gate_b, m_ple_proj, m_ln2_g, m_ln2_b, v_w_in, v_attn_sinks, v_rnn_conv_w, v_rnn_conv_b, v_gate_a_w, v_gate_a_b, v_gate_x_w, v_gate_x_b, v_lru_lambda, v_w_out, v_ln1_g, v_ln1_b, v_w_ffn_up, v_ffn_conv_w, v_ffn_conv_b, v_w_ffn_down, v_ple_gate_w, v_ple_gate_b, v_ple_proj, v_ln2_g, v_ln2_b):
    w = dict(w_in=w_in, attn_sinks=attn_sinks, rnn_conv_w=rnn_conv_w, rnn_conv_b=rnn_conv_b, gate_a_w=gate_a_w,
             gate_a_b=gate_a_b, gate_x_w=gate_x_w, gate_x_b=gate_x_b, lru_lambda=lru_lambda, w_out=w_out, ln1_g=ln1_g,
             ln1_b=ln1_b, w_ffn_up=w_ffn_up, ffn_conv_w=ffn_conv_w, ffn_conv_b=ffn_conv_b, w_ffn_down=w_ffn_down,
             ple_gate_w=ple_gate_w, ple_gate_b=ple_gate_b, ple_proj=ple_proj, ln2_g=ln2_g, ln2_b=ln2_b)
    m = dict(w_in=m_w_in, attn_sinks=m_attn_sinks, rnn_conv_w=m_rnn_conv_w, rnn_conv_b=m_rnn_conv_b, gate_a_w=m_gate_a_w,
             gate_a_b=m_gate_a_b, gate_x_w=m_gate_x_w, gate_x_b=m_gate_x_b, lru_lambda=m_lru_lambda, w_out=m_w_out,
             ln1_g=m_ln1_g, ln1_b=m_ln1_b, w_ffn_up=m_w_ffn_up, ffn_conv_w=m_ffn_conv_w, ffn_conv_b=m_ffn_conv_b,
             w_ffn_down=m_w_ffn_down, ple_gate_w=m_ple_gate_w, ple_gate_b=m_ple_gate_b, ple_proj=m_ple_proj,
             ln2_g=m_ln2_g, ln2_b=m_ln2_b)
    v = dict(w_in=v_w_in, attn_sinks=v_attn_sinks, rnn_conv_w=v_rnn_conv_w, rnn_conv_b=v_rnn_conv_b, gate_a_w=v_gate_a_w,
             gate_a_b=v_gate_a_b, gate_x_w=v_gate_x_w, gate_x_b=v_gate_x_b, lru_lambda=v_lru_lambda, w_out=v_w_out,
             ln1_g=v_ln1_g, ln1_b=v_ln1_b, w_ffn_up=v_w_ffn_up, ffn_conv_w=v_ffn_conv_w, ffn_conv_b=v_ffn_conv_b,
             w_ffn_down=v_w_ffn_down, ple_gate_w=v_ple_gate_w, ple_gate_b=v_ple_gate_b, ple_proj=v_ple_proj,
             ln2_g=v_ln2_g, ln2_b=v_ln2_b)
    w, m, v = ({k: a[0] for k, a in d.items()} for d in (w, m, v))
    chip = 2 * lax.axis_index("x") + lax.axis_index("y")
    core = lax.axis_index("c")

    wpack = _pack_big(w)
    cpack, _ = _pack_vecs([w["rnn_conv_w"], w["ffn_conv_w"]])
    shard = wpack.astype(MXU_DTYPE)
    g_in, gcp = _gather_first(shard[PACK_OFF[0]:PACK_OFF[1]], cpack)
    gw = _split_pack(g_in, 0, 1)
    small = {k: w[k] for k in SMALL}
    small["rnn_conv_w"] = gcp[:, 0:4].reshape(4, 4, 128).transpose(1, 0, 2).reshape(4, 512)
    small["ffn_conv_w"] = gcp[:, 4:22].reshape(4, 3, 768).transpose(1, 0, 2).reshape(3, 3072)

    core1 = core.reshape(1).astype(jnp.int32)
    grad_x, big, sg, loss, ffn_halves, small_all = _layer_grads(x[0], p[0, 0], loss_target[0], gw, small, shard, core1)

    shapes = [sg[k].shape for k in SMALL] + [(1,)]
    _, offs = _pack_vecs([jnp.zeros(s, F32) for s in shapes])
    red = dict(zip(SMALL + ("loss",), _unpack_vecs(_sum_devices(small_all[0]), offs, shapes)))
    red["rnn_conv_w"] = lax.dynamic_slice_in_dim(red["rnn_conv_w"], chip * 128, 128, axis=1)
    red["ffn_conv_w"] = lax.dynamic_slice_in_dim(red["ffn_conv_w"], chip * 768, 768, axis=1)

    late_mine, late_other = ([a] for a in _reduce_in_vmem(big["w_in"]))

    def adamw(names, mine, other, name):
        out, _ = _adamw_halves([w[k] for k in names], mine, other, [m[k] for k in names], [v[k] for k in names],
                               core1, name)
        return dict(zip(names, out))

    big_out = {**adamw(LATE_WEIGHTS, late_mine, late_other, "adamw_late"), **adamw(EARLY_WEIGHTS, *ffn_halves, "adamw_early")}
    wsm, offs2 = _pack_vecs([w[k] for k in SMALL])
    gsm, _ = _pack_vecs([red[k] for k in SMALL])
    msm, _ = _pack_vecs([m[k] for k in SMALL])
    vsm, _ = _pack_vecs([v[k] for k in SMALL])
    dsm, nmsm, nvsm = _adamw(wsm, gsm, msm, vsm, "adamw_small")
    shapes2 = [w[k].shape for k in SMALL]

    def named(n, smallp):
        d = {k: out[n][None] for k, out in big_out.items()}
        d.update({k: a[None] for k, a in zip(SMALL, _unpack_vecs(smallp, offs2, shapes2))})
        return [d[k] for k in WEIGHTS]

    return (red["loss"].reshape(()), grad_x[None], *named(0, gsm), *named(1, dsm), *named(2, nmsm), *named(3, nvsm))
```

```python
import functools

import jax
import jax.numpy as jnp
from jax import lax
from jax.experimental import pallas as pl
from jax.experimental.pallas import tpu as pltpu

F32 = jnp.float32
BF16 = jnp.bfloat16
MXU_DTYPE = jnp.bfloat16

D = 1024
D_ATT = 512
D_KV = 128
D_RNN = 512
D_IN = 1792
D_FF = 3072
FF_CHUNK = 768
PLE = 256
HEADS = 8
HEAD_DIM = 64
BLK = 128
ATTN_BLOCKS = 8
DW_TOKENS = 4096
RNN_BLOCKS = 8
LN_EPS = 1e-5
LRU_C = 8.0
ALPHA = float(2.0 ** 0.25)
SCALE = HEAD_DIM ** -0.5
NEG = -1e30

ADAM_LR = 0.001
ADAM_B1 = 0.9
ADAM_B2 = 0.999
ADAM_EPS = 1e-08
ADAM_WD = 0.01
ADAM_STEP = 10

VMEM_LIMIT_BYTES = 56 * 1024 * 1024
MESH = pl.DeviceIdType.MESH

PACK_ROWS = (448, 1536, 256, 768, 256, 64)
PACK_OFF = tuple(sum(PACK_ROWS[:i]) for i in range(len(PACK_ROWS) + 1))
PACK_TOTAL = PACK_OFF[-1]


def _params(**kw):
    return pltpu.CompilerParams(vmem_limit_bytes=VMEM_LIMIT_BYTES, **kw)


def _mm(a, b):
    return jnp.dot(a.astype(MXU_DTYPE), b.astype(MXU_DTYPE), preferred_element_type=F32)


def _mm_nt(a, b):
    return lax.dot_general(a.astype(MXU_DTYPE), b.astype(MXU_DTYPE), (((1,), (1,)), ((), ())),
                           preferred_element_type=F32)


def _mm_tn(a, b):
    return lax.dot_general(a.astype(MXU_DTYPE), b.astype(MXU_DTYPE), (((0,), (0,)), ((), ())),
                           preferred_element_type=F32)


def _sigmoid(x):
    return 0.5 + 0.5 * jnp.tanh(0.5 * x)


def _gelu(x):
    c = 0.7978845608028654
    k = 0.044715
    x2 = x * x
    t = jnp.tanh(x * (c + (c * k) * x2))
    h = 0.5 * (1.0 + t)
    return x * h, h * (1.0 + (x * (1.0 - t)) * (c + (3.0 * c * k) * x2))


def _shift_rows(x, s, edge8):
    R = x.shape[0]
    row8 = lax.broadcasted_iota(jnp.int32, (8, x.shape[1]), 0)
    if s > 0:
        rolled = pltpu.roll(x, s, 0)
        first = jnp.where(row8 < s, pltpu.roll(edge8, s, 0), rolled[0:8])
        return jnp.concatenate([first, rolled[8:]], axis=0)
    k = -s
    rolled = pltpu.roll(x, R - k, 0)
    last = jnp.where(row8 >= 8 - k, pltpu.roll(edge8, 8 - k, 0), rolled[R - 8:])
    return jnp.concatenate([rolled[:R - 8], last], axis=0)


def _softplus(x):
    return jnp.maximum(x, 0.0) + jnp.log(1.0 + jnp.exp(-jnp.abs(x)))


def _ln(z, g, b):
    mu = jnp.mean(z, axis=-1, keepdims=True)
    zc = z - mu
    var = jnp.mean(zc * zc, axis=-1, keepdims=True)
    rstd = lax.rsqrt(var + LN_EPS)
    xhat = zc * rstd
    return xhat * g + b, xhat, rstd


def _ln_bwd(dy, xhat, rstd, g):
    dxh = dy * g
    m1 = jnp.mean(dxh, axis=-1, keepdims=True)
    m2 = jnp.mean(dxh * xhat, axis=-1, keepdims=True)
    return rstd * (dxh - m1 - xhat * m2)


def _colsum(x):
    return jnp.sum(x, axis=0, keepdims=True)


def _full(shape):
    nd = len(shape)
    return pl.BlockSpec(shape, lambda *_: (0,) * nd)


def _rows(tm, cols, fn=None):
    if fn is None:
        return pl.BlockSpec((tm, cols), lambda i: (i, 0))
    return pl.BlockSpec((tm, cols), lambda i: (fn(i), 0))


def _heads(tm):
    return pl.BlockSpec((HEADS, tm, HEAD_DIM), lambda i: (0, i, 0))


def _in_proj(x, w_in_t):
    T = x.shape[0]
    tm = min(1024, T)

    def body(x_ref, w_ref, q_ref, kv_ref, xr_ref, gr_ref, xb_ref):
        xb = x_ref[...].astype(MXU_DTYPE)
        xb_ref[...] = xb.astype(BF16)
        q = _mm_nt(xb, w_ref[0:512, :]) * SCALE
        for h in range(HEADS):
            q_ref[h] = q[:, h * 64:(h + 1) * 64].astype(BF16)
        kv_ref[...] = _mm_nt(xb, w_ref[512:768, :]).astype(BF16)
        xr_ref[...] = _mm_nt(xb, w_ref[768:1280, :])
        gr_ref[...] = _mm_nt(xb, w_ref[1280:1792, :])

    return pl.pallas_call(
        body, name="in_proj", grid=(T // tm,),
        in_specs=[_rows(tm, D), _full((D_IN, D))],
        out_specs=[_heads(tm), _rows(tm, 256), _rows(tm, 512), _rows(tm, 512), _rows(tm, D)],
        out_shape=[jax.ShapeDtypeStruct((HEADS, T, 64), BF16), jax.ShapeDtypeStruct((T, 256), BF16),
                   jax.ShapeDtypeStruct((T, 512), F32), jax.ShapeDtypeStruct((T, 512), F32),
                   jax.ShapeDtypeStruct((T, D), BF16)],
        compiler_params=_params(),
    )(x, w_in_t)


def _attn_band(kv_ref, i):
    cur = pl.multiple_of(i * BLK, BLK)
    prev = pl.multiple_of(jnp.maximum(i - 1, 0) * BLK, BLK)
    band = jnp.concatenate([kv_ref[pl.ds(prev, BLK), :], kv_ref[pl.ds(cur, BLK), :]], axis=0)
    key = lax.broadcasted_iota(jnp.int32, (2 * BLK, 4 * BLK), 0)
    qry = lax.broadcasted_iota(jnp.int32, (2 * BLK, 4 * BLK), 1) & (BLK - 1)
    in_prev = jnp.logical_and(jnp.logical_and(key < BLK, key > qry), i > 0)
    mask = jnp.logical_or(in_prev, jnp.logical_and(key >= BLK, key - BLK <= qry))
    return band, mask, cur, prev


def _attn_scores(band, mask, qs, s_ref, g):
    st = jnp.where(mask, _mm_nt(band[:, g * 64:(g + 1) * 64], qs), NEG)
    lane = lax.broadcasted_iota(jnp.int32, (1, 4 * BLK), 1)
    sv = jnp.where(lane < BLK, s_ref[0, 4 * g],
                   jnp.where(lane < 2 * BLK, s_ref[0, 4 * g + 1], jnp.where(lane < 3 * BLK, s_ref[0, 4 * g + 2], s_ref[0, 4 * g + 3])))
    m = jnp.maximum(jnp.max(st, axis=0, keepdims=True), sv)
    p = jnp.exp(st - m)
    ps = jnp.exp(sv - m)
    return p, ps, jnp.sum(p, axis=0, keepdims=True) + ps


def _pos():
    return lax.axis_index("x"), lax.axis_index("y"), lax.axis_index("c")


def _other_chips(x, y):
    return [(1 - x, y), (x, 1 - y), (1 - x, 1 - y)]


def _gather_steps(w_ref, gw_ref, send_sems, recv_sems, local_sem):
    x, y, c = _pos()
    me = 2 * x + y
    chips = _other_chips(x, y)
    half = w_ref.shape[0] // 2
    mine = pl.ds(pl.multiple_of(c * half, 16), half)
    theirs = pl.ds(pl.multiple_of((1 - c) * half, 16), half)
    loc = pltpu.make_async_copy(w_ref, gw_ref.at[me], local_sem)

    def copy(k, src, dst, to):
        return pltpu.make_async_remote_copy(src_ref=src, dst_ref=dst, send_sem=send_sems.at[k], recv_sem=recv_sems.at[k],
                                            device_id=to, device_id_type=MESH)

    def out(k):
        px, py = chips[k]
        return copy(k, w_ref.at[mine], gw_ref.at[me, mine], (px, py, c))

    def fwd(k, rows):
        px, py = chips[k]
        return copy(3 + k, gw_ref.at[2 * px + py, rows], gw_ref.at[2 * px + py, rows], (x, y, 1 - c))

    def start():
        loc.start()
        for k in range(3):
            out(k).start()

    def forward():
        for k in range(3):
            px, py = chips[k]
            copy(k, w_ref.at[mine], gw_ref.at[2 * px + py, mine], (px, py, c)).wait_recv()
            fwd(k, mine).start()

    def finish():
        for k in range(3):
            fwd(k, theirs).wait_recv()
        for k in range(3):
            out(k).wait_send()
            fwd(k, mine).wait_send()
        loc.wait()

    return start, forward, finish


GATHER_SCRATCH = [pltpu.SemaphoreType.DMA((6,)), pltpu.SemaphoreType.DMA((6,)), pltpu.SemaphoreType.DMA]


class _Exchange:
    def __init__(self, args, out_shape, scratch, make):
        self.args, self.out_shape, self.scratch, self.make = list(args), list(out_shape), list(scratch), make


def _join_exchanges(a, b):
    na, nao, nas = len(a.args), len(a.out_shape), len(a.scratch)

    def make(ins, outs, sems):
        steps_a = a.make(ins[:na], outs[:nao], sems[:nas])
        steps_b = b.make(ins[na:], outs[nao:], sems[nas:])

        def both(f, g):
            def run():
                f()
                g()
            return run

        return tuple(both(f, g) for f, g in zip(steps_a, steps_b))

    return _Exchange(a.args + b.args, a.out_shape + b.out_shape, a.scratch + b.scratch, make)


def _gather_exchange(wsrc):
    return _Exchange([wsrc], [jax.ShapeDtypeStruct((4,) + wsrc.shape, wsrc.dtype)], GATHER_SCRATCH,
                     lambda ins, outs, sems: _gather_steps(ins[0], outs[0], *sems))


def _launch(body, name, grid, in_specs, out_specs, out_shape, scratch, args, exchange=None, prefetch=0):
    def call(fn, fn_name, ins, outs, shapes, scr, operands, effects):
        spec = pltpu.PrefetchScalarGridSpec(num_scalar_prefetch=prefetch, grid=grid, in_specs=ins, out_specs=outs,
                                            scratch_shapes=scr)
        return pl.pallas_call(fn, name=fn_name, grid_spec=spec, out_shape=shapes,
                              compiler_params=_params(has_side_effects=effects))(*operands)

    if exchange is None:
        return call(body, name, list(in_specs), list(out_specs), list(out_shape), list(scratch), args, False)
    n_in, n_out, ei, eo, ns = len(in_specs), len(out_specs), len(exchange.args), len(exchange.out_shape), len(exchange.scratch)
    nsteps = 1
    for g in grid:
        nsteps *= g

    def wrapped(*refs):
        scalars, refs = refs[:prefetch], refs[prefetch:]
        ins, xin = refs[:n_in], refs[n_in:n_in + ei]
        outs, xout = refs[n_in + ei:n_in + ei + n_out], refs[n_in + ei + n_out:n_in + ei + n_out + eo]
        rest = refs[n_in + ei + n_out + eo:]
        own, sems = rest[:len(rest) - ns], rest[len(rest) - ns:]
        start, forward, finish = exchange.make(xin, xout, sems)
        i = pl.program_id(0)
        for d in range(1, len(grid)):
            i = i * grid[d] + pl.program_id(d)
        pl.when(i == 0)(start)
        body(*scalars, *ins, *outs, *own)
        pl.when(i == max(nsteps - 3, 0))(forward)
        pl.when(i == nsteps - 1)(finish)

    anyspec = pl.BlockSpec(memory_space=pl.ANY)
    return call(wrapped, name + "_x", list(in_specs) + [anyspec] * ei, list(out_specs) + [anyspec] * eo,
                list(out_shape) + exchange.out_shape, list(scratch) + exchange.scratch, (*args, *exchange.args), True)


def _attn_fwd(q, kv, sinks, exchange=None):
    T = kv.shape[0]
    nblk = min(ATTN_BLOCKS, T // BLK)

    def body(q_ref, kv_ref, s_ref, o_ref):
        for b in range(nblk):
            rows = slice(b * BLK, (b + 1) * BLK)
            band, mask, _, _ = _attn_band(kv_ref, nblk * pl.program_id(0) + b)
            for g in range(2):
                qs = q_ref[4 * g:4 * g + 4, rows, :].reshape(4 * BLK, HEAD_DIM)
                p, _, den = _attn_scores(band, mask, qs, s_ref, g)
                ot = _mm_tn(band[:, 128:256], p) * (1.0 / den)
                for hh in range(4):
                    o = ot[:, hh * BLK:(hh + 1) * BLK].T
                    o_ref[rows, (4 * g + hh) * 64:(4 * g + hh + 1) * 64] = o[:, g * 64:(g + 1) * 64].astype(BF16)

    tq = nblk * BLK
    return _launch(body, "attn_fwd", (T // tq,), [_heads(tq), _full((T, 256)), pl.BlockSpec(memory_space=pltpu.SMEM)],
                   [_rows(tq, 512)], [jax.ShapeDtypeStruct((T, 512), BF16)], [], (q, kv, sinks), exchange)


def _attn_bwd(q, kv, do, sinks, exchange=None):
    T = kv.shape[0]
    nblk = min(ATTN_BLOCKS, T // BLK)

    def body(q_ref, kv_ref, do_ref, s_ref, dq_ref, dkv_ref, ds_ref):
        @pl.when(pl.program_id(0) == 0)
        def _():
            ds_ref[...] = jnp.zeros_like(ds_ref)

        for b in range(nblk):
            rows = slice(b * BLK, (b + 1) * BLK)
            band, mask, cur, prev = _attn_band(kv_ref, nblk * pl.program_id(0) + b)
            for g in range(2):
                qs = q_ref[4 * g:4 * g + 4, rows, :].reshape(4 * BLK, HEAD_DIM)
                dos = do_ref[4 * g:4 * g + 4, rows, :].reshape(4 * BLK, HEAD_DIM)
                p, ps, den = _attn_scores(band, mask, qs, s_ref, g)
                inv = 1.0 / den
                p = p * inv
                dpt = _mm_nt(band[:, 128 + g * 64:192 + g * 64], dos)
                delta = jnp.sum(p * dpt, axis=0, keepdims=True)
                dst = p * (dpt - delta)
                dsv = -(ps * inv) * delta
                for hh in range(4):
                    dsink = jnp.sum(dsv[:, hh * BLK:(hh + 1) * BLK], axis=1, keepdims=True)
                    ds_ref[4 * g + hh:4 * g + hh + 1, :] += jnp.broadcast_to(dsink, (1, 128))
                dqt = _mm_tn(band[:, 0:128], dst) * SCALE
                for hh in range(4):
                    dqh = dqt[:, hh * BLK:(hh + 1) * BLK].T
                    dq_ref[rows, (4 * g + hh) * 64:(4 * g + hh + 1) * 64] = dqh[:, g * 64:(g + 1) * 64].astype(BF16)
                dk = _mm(dst, qs)
                dv = _mm(p, dos)
                dkv_ref[pl.ds(cur, BLK), g * 64:(g + 1) * 64] = dk[BLK:2 * BLK]
                dkv_ref[pl.ds(cur, BLK), 128 + g * 64:192 + g * 64] = dv[BLK:2 * BLK]
                dkv_ref[pl.ds(prev, BLK), g * 64:(g + 1) * 64] += dk[0:BLK]
                dkv_ref[pl.ds(prev, BLK), 128 + g * 64:192 + g * 64] += dv[0:BLK]

    tq = nblk * BLK
    return _launch(body, "attn_bwd", (T // tq,),
                   [_heads(tq), _full((T, 256)), _heads(tq), pl.BlockSpec(memory_space=pltpu.SMEM)],
                   [_rows(tq, 512), _full((T, 256)), _full((8, 128))],
                   [jax.ShapeDtypeStruct((T, 512), BF16), jax.ShapeDtypeStruct((T, 256), F32),
                    jax.ShapeDtypeStruct((8, 128), F32)], [], (q, kv, do, sinks), exchange)


def _rows8(tm, cols):
    return lax.broadcasted_iota(jnp.int32, (tm, cols), 0) & 7


def _lru_gates(xc, wa, ba, wx, bx, lam):
    r = _sigmoid(_mm(xc, wa) + ba)
    ii = _sigmoid(_mm(xc, wx) + bx)
    sp = _softplus(-lam)
    la = -LRU_C * r * sp
    a = jnp.exp(la)
    m = jnp.sqrt(-jnp.tanh(la) * (a * a + 1.0))
    return r, ii, sp, a, m


def _rnn_fwd(xr, gr, cw, cb, wa, ba, wx, bx, lam, exchange=None):
    T = xr.shape[0]
    tm = 512
    C = D_RNN

    def body(xr_ref, gr_ref, cw_ref, cb_ref, wa_ref, ba_ref, wx_ref, bx_ref, lam_ref,
             xc_ref, h_ref, rec_ref, ext, a_s, b_s, carry):
        i = pl.program_id(0)

        @pl.when(i == 0)
        def _():
            ext[...] = jnp.zeros((8, C), F32)
            carry[...] = jnp.zeros((8, C), F32)

        xr = xr_ref[...]
        edge = ext[...]
        xc = cb_ref[...] + cw_ref[3:4, :] * xr
        for k in range(3):
            xc = xc + cw_ref[k:k + 1, :] * _shift_rows(xr, 3 - k, edge)
        ext[...] = xr[tm - 8:tm, :]
        xc_ref[...] = xc
        _, ii, _, a, m = _lru_gates(xc, wa_ref[...], ba_ref[...], wx_ref[...], bx_ref[...], lam_ref[...])
        b = m * ii * xc
        r8 = _rows8(tm, C)
        for d in (1, 2, 4):
            ok = r8 >= d
            a_sh = jnp.where(ok, pltpu.roll(a, d, 0), 1.0)
            b_sh = jnp.where(ok, pltpu.roll(b, d, 0), 0.0)
            b = a * b_sh + b
            a = a * a_sh
        a_s[...] = a
        b_s[...] = b

        def step(g, hin):
            s = pl.multiple_of(g * 8, 8)
            hg = a_s[pl.ds(s, 8), :] * hin + b_s[pl.ds(s, 8), :]
            h_ref[pl.ds(s, 8), :] = hg
            return jnp.broadcast_to(hg[7:8, :], (8, C))

        carry[...] = lax.fori_loop(0, tm // 8, step, carry[...], unroll=4)
        ge, _ = _gelu(gr_ref[...])
        rec_ref[...] = (h_ref[...] * ge).astype(BF16)

    vec = _full((1, C))
    in_specs = [_rows(tm, C), _rows(tm, C), _full((4, C)), vec, _full((C, C)), vec, _full((C, C)), vec, vec]
    out_specs = [_rows(tm, C), _rows(tm, C), _rows(tm, C)]
    out_shape = [jax.ShapeDtypeStruct((T, C), F32), jax.ShapeDtypeStruct((T, C), F32), jax.ShapeDtypeStruct((T, C), BF16)]
    scratch = [pltpu.VMEM((8, C), F32), pltpu.VMEM((tm, C), F32), pltpu.VMEM((tm, C), F32), pltpu.VMEM((8, C), F32)]
    return _launch(body, "rnn_fwd", (T // tm,), in_specs, out_specs, out_shape, scratch,
                   (xr, gr, cw, cb, wa, ba, wx, bx, lam), exchange)


def _rnn_bwd(drec, gr, h, xc, xr, cw, wa, ba, wx, bx, lam, exchange=None):
    T = xr.shape[0]
    tm = 512
    C = D_RNN
    nt = T // tm
    t8 = tm // 8

    def body(drec_ref, gr_ref, h_ref, hp_ref, xc_ref, xr_ref, cw_ref, wa_ref, ba_ref, wx_ref, bx_ref,
             lam_ref, dxr_ref, dgr_ref, dwa_ref, dwx_ref, dvec_ref, c_s, g_s, gout, ext, anext, gcarry):
        i = pl.program_id(0)
        j = nt - 1 - i

        @pl.when(i == 0)
        def _():
            dwa_ref[...] = jnp.zeros_like(dwa_ref)
            dwx_ref[...] = jnp.zeros_like(dwx_ref)
            dvec_ref[...] = jnp.zeros_like(dvec_ref)
            anext[...] = jnp.zeros((8, C), F32)
            gcarry[...] = jnp.zeros((8, C), F32)
            ext[...] = jnp.zeros((8, C), F32)

        xc = xc_ref[...]
        lam = lam_ref[...]
        r, ii, sp, a, m = _lru_gates(xc, wa_ref[...], ba_ref[...], wx_ref[...], bx_ref[...], lam)
        ge, dge = _gelu(gr_ref[...])
        drec = drec_ref[...]
        hh = h_ref[...]
        dgr_ref[...] = (drec * hh * dge).astype(BF16)
        dh = drec * ge
        rowi = lax.broadcasted_iota(jnp.int32, (tm, C), 0)
        c = jnp.where(rowi == tm - 1, jnp.broadcast_to(anext[0:1, :], (tm, C)), pltpu.roll(a, tm - 1, 0))
        anext[...] = a[0:8, :]
        r8 = rowi & 7
        gg = dh
        for d in (1, 2, 4):
            ok = r8 < 8 - d
            c_sh = jnp.where(ok, pltpu.roll(c, tm - d, 0), 1.0)
            g_sh = jnp.where(ok, pltpu.roll(gg, tm - d, 0), 0.0)
            gg = c * g_sh + gg
            c = c * c_sh
        c_s[...] = c
        g_s[...] = gg

        def step(k, gin):
            s = pl.multiple_of((t8 - 1 - k) * 8, 8)
            og = c_s[pl.ds(s, 8), :] * gin + g_s[pl.ds(s, 8), :]
            gout[pl.ds(s, 8), :] = og
            return jnp.broadcast_to(og[0:1, :], (8, C))

        gcarry[...] = lax.fori_loop(0, t8, step, gcarry[...], unroll=4)
        G = gout[...]
        hprev_row = jnp.where(j > 0, hp_ref[7:8, :], 0.0)
        hprev = jnp.where(rowi == 0, jnp.broadcast_to(hprev_row, (tm, C)), pltpu.roll(hh, 1, 0))
        da = G * hprev
        dm = G * ii * xc
        di = G * m * xc
        dxc = G * m * ii
        dla = da * a - dm * a * a / m
        dr = dla * (-LRU_C * sp)
        dsp = _colsum(dla * (-LRU_C * r))
        dlam = dsp * (-_sigmoid(-lam))
        dpr = dr * r * (1.0 - r)
        dpi = di * ii * (1.0 - ii)
        dxc = dxc + _mm_nt(dpr, wa_ref[...]) + _mm_nt(dpi, wx_ref[...])
        dwa_ref[...] += _mm_tn(xc, dpr)
        dwx_ref[...] += _mm_tn(xc, dpi)
        dvec_ref[0:1, :] += _colsum(dpr)
        dvec_ref[1:2, :] += _colsum(dpi)
        dvec_ref[2:3, :] += dlam
        dvec_ref[3:4, :] += _colsum(dxc)
        edge = ext[...]
        xr = xr_ref[...]
        dxr = cw_ref[3:4, :] * dxc
        dvec_ref[7:8, :] += _colsum(dxc * xr)
        for k in range(3):
            up = _shift_rows(dxc, k - 3, edge)
            dxr = dxr + cw_ref[k:k + 1, :] * up
            dvec_ref[4 + k:5 + k, :] += _colsum(up * xr)
        ext[...] = dxc[0:8, :]
        dxr_ref[...] = dxr.astype(BF16)

    rev = lambda i: nt - 1 - i
    prev8 = lambda i: jnp.maximum((nt - 1 - i) * t8 - 1, 0)
    vec = _full((1, C))
    return _launch(
        body, "rnn_bwd", (nt,),
        [_rows(tm, C, rev), _rows(tm, C, rev), _rows(tm, C, rev), _rows(8, C, prev8), _rows(tm, C, rev),
         _rows(tm, C, rev), _full((4, C)), _full((C, C)), vec, _full((C, C)), vec, vec],
        [_rows(tm, C, rev), _rows(tm, C, rev), _full((C, C)), _full((C, C)), _full((8, C))],
        [jax.ShapeDtypeStruct((T, C), BF16), jax.ShapeDtypeStruct((T, C), BF16),
         jax.ShapeDtypeStruct((C, C), F32), jax.ShapeDtypeStruct((C, C), F32), jax.ShapeDtypeStruct((8, C), F32)],
        [pltpu.VMEM((tm, C), F32), pltpu.VMEM((tm, C), F32), pltpu.VMEM((tm, C), F32),
         pltpu.VMEM((8, C), F32), pltpu.VMEM((8, C), F32), pltpu.VMEM((8, C), F32)],
        (drec, gr, h, h, xc, xr, cw, wa, ba, wx, bx, lam), exchange)


def _out_proj(att, rec, x, w_out, g1, b1):
    T = x.shape[0]
    tm = min(1024, T)

    def body(att_ref, rec_ref, x_ref, w_ref, g1_ref, b1_ref, z_ref, h_ref):
        mix = _mm(att_ref[...], w_ref[0:512, :]) + _mm(rec_ref[...], w_ref[512:1024, :])
        z1 = ALPHA * x_ref[...] + mix
        z_ref[...] = z1
        h1, _, _ = _ln(z1, g1_ref[...], b1_ref[...])
        h_ref[...] = h1.astype(MXU_DTYPE).astype(BF16)

    return pl.pallas_call(
        body, name="out_proj", grid=(T // tm,),
        in_specs=[_rows(tm, 512), _rows(tm, 512), _rows(tm, D), _full((D, D)), _full((1, D)), _full((1, D))],
        out_specs=[_rows(tm, D), _rows(tm, D)],
        out_shape=[jax.ShapeDtypeStruct((T, D), F32), jax.ShapeDtypeStruct((T, D), BF16)],
        compiler_params=_params(),
    )(att, rec, x, w_out, g1, b1)


NC = D_FF // FF_CHUNK


def _ffn_up(h1b, w_up_t, fcw, fcb, exchange=None):
    T = h1b.shape[0]
    tm = min(1024, T)
    CW = FF_CHUNK

    def body(h_ref, wg_ref, wv_ref, fcw_ref, fcb_ref, gate_ref, ge_ref, vd_ref, act_ref, before):
        i = pl.program_id(1)

        @pl.when(i == 0)
        def _():
            before[...] = jnp.zeros((8, CW), F32)

        hb = h_ref[...]
        gate = _mm_nt(hb, wg_ref[...])
        val = _mm_nt(hb, wv_ref[...])
        gate_ref[...] = gate.astype(BF16)
        edge = before[...]
        gc = (fcb_ref[...] + fcw_ref[0:1, :] * _shift_rows(gate, 2, edge) + fcw_ref[1:2, :] * _shift_rows(gate, 1, edge)
              + fcw_ref[2:3, :] * gate)
        before[...] = gate[tm - 8:tm, :]
        ge, dge = _gelu(gc)
        ge_ref[...] = ge.astype(BF16)
        vd_ref[...] = (val * dge).astype(BF16)
        act_ref[...] = (ge * val).astype(BF16)

    chunk = pl.BlockSpec((None, tm, CW), lambda c, i: (c, i, 0))
    return _launch(
        body, "ffn_up", (NC, T // tm),
        [pl.BlockSpec((tm, D), lambda c, i: (i, 0)), pl.BlockSpec((CW, D), lambda c, i: (c, 0)),
         pl.BlockSpec((CW, D), lambda c, i: (NC + c, 0)), pl.BlockSpec((None, 3, CW), lambda c, i: (c, 0, 0)),
         pl.BlockSpec((None, 1, CW), lambda c, i: (c, 0, 0))],
        [chunk] * 3 + [pl.BlockSpec((tm, CW), lambda c, i: (i, c))],
        [jax.ShapeDtypeStruct((NC, T, CW), BF16)] * 3 + [jax.ShapeDtypeStruct((T, D_FF), BF16)], [pltpu.VMEM((8, CW), F32)],
        (h1b, w_up_t, w_up_t, fcw, fcb), exchange)


def _ffn_down(act, z1, p, tgt, w_down, w_g, w_p_t, g1, b1, g2, b2, bg):
    T = z1.shape[0]
    tm = 512

    def body(act_ref, z_ref, p_ref, t_ref, wdn_hbm, wg_hbm, wp_hbm, g1_ref, b1_ref, g2_ref, b2_ref, bg_ref,
             dz2_ref, dz2b_ref, dpre_ref, dpp_ref, vec_ref, wdn, wg, wp, sems):
        @pl.when(pl.program_id(0) == 0)
        def _():
            copies = [pltpu.make_async_copy(src, dst, sems.at[n])
                      for n, (src, dst) in enumerate(((wdn_hbm, wdn), (wg_hbm, wg), (wp_hbm, wp)))]
            for cp in copies:
                cp.start()
            vec_ref[...] = jnp.zeros_like(vec_ref)
            for cp in copies:
                cp.wait()

        g2v = g2_ref[...]
        for r in (slice(0, tm // 2), slice(tm // 2, tm)):
            h1, _, _ = _ln(z_ref[r, :], g1_ref[...], b1_ref[...])
            h1b = h1.astype(MXU_DTYPE)
            ffn = _mm(act_ref[r, :], wdn[...])
            sg = _sigmoid(_mm(h1b, wg[...]) + bg_ref[...])
            pp = _mm_nt(p_ref[r, :], wp[...])
            z2 = ALPHA * h1 + ffn + sg * pp
            y, xh2, rstd2 = _ln(z2, g2v, b2_ref[...])
            diff = y - t_ref[r, :]
            dy = diff * (1.0 / D)
            dz2 = _ln_bwd(dy, xh2, rstd2, g2v)
            dpre = dz2 * pp * sg * (1.0 - sg)
            dz2_ref[r, :] = dz2
            dz2b_ref[r, :] = dz2.astype(BF16)
            dpre_ref[r, :] = dpre.astype(BF16)
            dpp_ref[r, :] = (dz2 * sg).astype(BF16)
            loss = 0.5 * jnp.sum(jnp.sum(diff * diff, axis=1, keepdims=True), axis=0, keepdims=True) * (1.0 / D)
            vec_ref[0:1, :] += jnp.broadcast_to(loss, (1, D))
            vec_ref[1:2, :] += _colsum(dy * xh2)
            vec_ref[2:3, :] += _colsum(dy)
            vec_ref[3:4, :] += _colsum(dpre)

    anyspec = pl.BlockSpec(memory_space=pl.ANY)
    vec = _full((1, D))
    return pl.pallas_call(
        body, name="ffn_down", grid=(T // tm,),
        in_specs=[_rows(tm, D_FF), _rows(tm, D), _rows(tm, PLE), _rows(tm, D),
                  anyspec, anyspec, anyspec] + [vec] * 5,
        out_specs=[_rows(tm, D)] * 4 + [_full((8, D))],
        out_shape=[jax.ShapeDtypeStruct((T, D), F32)] + [jax.ShapeDtypeStruct((T, D), BF16)] * 3
                  + [jax.ShapeDtypeStruct((8, D), F32)],
        scratch_shapes=[pltpu.VMEM((D_FF, D), MXU_DTYPE), pltpu.VMEM((D, D), MXU_DTYPE), pltpu.VMEM((D, PLE), MXU_DTYPE),
                        pltpu.SemaphoreType.DMA((3,))],
        compiler_params=_params(),
    )(act, z1, p, tgt, w_down, w_g, w_p_t, g1, b1, g2, b2, bg)


def _ffn_bwd(dz2b, gate, ge, vd, w_down, fcw):
    T = dz2b.shape[0]
    tm = min(1024, T)
    CW = FF_CHUNK
    nt = T // tm

    def body(dz_ref, wdn_ref, gate_ref, ge_ref, vd_ref, fcw_ref, dup_ref, dfc_ref, after):
        i = pl.program_id(1)

        @pl.when(i == 0)
        def _():
            after[...] = jnp.zeros((8, CW), F32)
            dfc_ref[...] = jnp.zeros_like(dfc_ref)

        gate = gate_ref[...].astype(F32)
        dact = _mm_nt(dz_ref[...], wdn_ref[...])
        dgc = dact * vd_ref[...].astype(F32)
        edge = after[...]
        dgc1 = _shift_rows(dgc, -1, edge)
        dgc2 = _shift_rows(dgc, -2, edge)
        after[...] = dgc[0:8, :]
        dup_ref[:, 0:CW] = (fcw_ref[2:3, :] * dgc + fcw_ref[1:2, :] * dgc1 + fcw_ref[0:1, :] * dgc2).astype(BF16)
        dup_ref[:, CW:2 * CW] = (dact * ge_ref[...].astype(F32)).astype(BF16)
        dfc_ref[0:1, :] += _colsum(dgc2 * gate)
        dfc_ref[1:2, :] += _colsum(dgc1 * gate)
        dfc_ref[2:3, :] += _colsum(dgc * gate)
        dfc_ref[3:4, :] += _colsum(dgc)

    rev = lambda c, i: (c, nt - 1 - i, 0)
    chunk = pl.BlockSpec((None, tm, CW), rev)
    return pl.pallas_call(
        body, name="ffn_bwd", grid=(NC, nt),
        in_specs=[pl.BlockSpec((tm, D), lambda c, i: (nt - 1 - i, 0)), pl.BlockSpec((CW, D), lambda c, i: (c, 0)),
                  chunk, chunk, chunk, pl.BlockSpec((None, 3, CW), lambda c, i: (c, 0, 0))],
        out_specs=[pl.BlockSpec((tm, 2 * CW), lambda c, i: (nt - 1 - i, c)),
                   pl.BlockSpec((None, 8, CW), lambda c, i: (c, 0, 0))],
        out_shape=[jax.ShapeDtypeStruct((T, 2 * D_FF), BF16), jax.ShapeDtypeStruct((NC, 8, CW), F32)],
        scratch_shapes=[pltpu.VMEM((8, CW), F32)],
        compiler_params=_params(),
    )(dz2b, w_down, gate, ge, vd, fcw)


def _ffn_dh1(dup, dz2, dpre, z1, w_up_t, w_g, g1, b1):
    T = z1.shape[0]
    tm = 512

    def body(dup_ref, dz2_ref, dpre_ref, z_ref, wup_hbm, wg_hbm, g1_ref, b1_ref, dz1_ref, vec_ref, wup, wg, sems):
        @pl.when(pl.program_id(0) == 0)
        def _():
            copies = [pltpu.make_async_copy(wup_hbm.at[pl.ds(s * D_FF + c * FF_CHUNK, FF_CHUNK)],
                                            wup.at[pl.ds((2 * c + s) * FF_CHUNK, FF_CHUNK)], sems.at[2 * c + s])
                      for c in range(NC) for s in range(2)]
            copies.append(pltpu.make_async_copy(wg_hbm, wg, sems.at[2 * NC]))
            for cp in copies:
                cp.start()
            vec_ref[...] = jnp.zeros_like(vec_ref)
            for cp in copies:
                cp.wait()

        g1v = g1_ref[...]
        _, xh1, rstd1 = _ln(z_ref[...], g1v, b1_ref[...])
        dh1 = ALPHA * dz2_ref[...] + _mm_nt(dpre_ref[...], wg[...]) + _mm(dup_ref[...], wup[...])
        dz1_ref[...] = _ln_bwd(dh1, xh1, rstd1, g1v)
        vec_ref[0:1, :] += _colsum(dh1 * xh1)
        vec_ref[1:2, :] += _colsum(dh1)

    anyspec = pl.BlockSpec(memory_space=pl.ANY)
    vec = _full((1, D))
    return pl.pallas_call(
        body, name="ffn_dh1", grid=(T // tm,),
        in_specs=[_rows(tm, 2 * D_FF), _rows(tm, D), _rows(tm, D), _rows(tm, D),
                  anyspec, anyspec, vec, vec],
        out_specs=[_rows(tm, D), _full((8, D))],
        out_shape=[jax.ShapeDtypeStruct((T, D), F32), jax.ShapeDtypeStruct((8, D), F32)],
        scratch_shapes=[pltpu.VMEM((2 * D_FF, D), MXU_DTYPE), pltpu.VMEM((D, D), MXU_DTYPE),
                        pltpu.SemaphoreType.DMA((2 * NC + 1,))],
        compiler_params=_params(),
    )(dup, dz2, dpre, z1, w_up_t, w_g, g1, b1)


def _out_proj_bwd(dz1, w_out, exchange=None):
    T = dz1.shape[0]
    tm = min(1024, T)

    def body(dz_ref, w_ref, datt_ref, drec_ref):
        dzb = dz_ref[...].astype(MXU_DTYPE)
        datt = _mm_nt(dzb, w_ref[0:512, :])
        for h in range(HEADS):
            datt_ref[h] = datt[:, h * 64:(h + 1) * 64].astype(BF16)
        drec_ref[...] = _mm_nt(dzb, w_ref[512:1024, :])

    return _launch(body, "out_proj_bwd", (T // tm,), [_rows(tm, D), _full((D, D))], [_heads(tm), _rows(tm, 512)],
                   [jax.ShapeDtypeStruct((HEADS, T, 64), BF16), jax.ShapeDtypeStruct((T, 512), F32)], [],
                   (dz1, w_out), exchange)


def _in_proj_bwd(dq, dkv, dxr, dgr, dz1, w_in_t, exchange=None):
    T = dz1.shape[0]
    tm = 512
    W = D_IN // 4

    def body(dq_ref, dkv_ref, dxr_ref, dgr_ref, dz_ref, w_ref, dx_ref, du_ref):
        dkv = dkv_ref[...]
        dx_ref[...] = (ALPHA * dz_ref[...] + _mm(dq_ref[...], w_ref[0:512, :]) + _mm(dkv, w_ref[512:768, :])
                       + _mm(dxr_ref[...], w_ref[768:1280, :]) + _mm(dgr_ref[...], w_ref[1280:1792, :]))
        dq, dxr, dgr = dq_ref[...].astype(F32), dxr_ref[...].astype(F32), dgr_ref[...].astype(F32)
        du_ref[0] = dq[:, 0:W].astype(BF16)
        du_ref[1, :, 0:64] = dq[:, W:512].astype(BF16)
        du_ref[1, :, 64:320] = dkv.astype(BF16)
        du_ref[1, :, 320:W] = dxr[:, 0:128].astype(BF16)
        du_ref[2, :, 0:384] = dxr[:, 128:512].astype(BF16)
        du_ref[2, :, 384:W] = dgr[:, 0:64].astype(BF16)
        du_ref[3] = dgr[:, 64:512].astype(BF16)

    return _launch(body, "in_proj_bwd", (T // tm,),
                   [_rows(tm, 512), _rows(tm, 256), _rows(tm, 512), _rows(tm, 512), _rows(tm, D), _full((D_IN, D))],
                   [_rows(tm, D), pl.BlockSpec((4, tm, W), lambda i: (0, i, 0))],
                   [jax.ShapeDtypeStruct((T, D), F32), jax.ShapeDtypeStruct((4, T, W), BF16)], [],
                   (dq, dkv, dxr, dgr, dz1, w_in_t), exchange)


def _accumulate_tn(a_ref, b_ref, o_ref):
    @pl.when(pl.program_id(1) == 0)
    def _():
        o_ref[...] = jnp.zeros_like(o_ref)

    o_ref[...] += _mm_tn(a_ref[...], b_ref[...])


def _weight_grad_cols(a, b, name, n_blocks, b_spec, out_shape, out_spec, exchange=None):
    T, M = a.shape
    bt = min(DW_TOKENS, T)
    return _launch(functools.partial(_accumulate_tn), name, (n_blocks, T // bt),
                   [pl.BlockSpec((bt, M), lambda m, k: (k, 0)), b_spec(bt)], [out_spec],
                   [jax.ShapeDtypeStruct(out_shape, F32)], [], (a, b), exchange)


def _dw_out(att, rec, dz1):
    T = dz1.shape[0]
    bt = min(DW_TOKENS // 2, T)

    def body(att_ref, rec_ref, dz_ref, o_ref):
        @pl.when(pl.program_id(0) == 0)
        def _():
            o_ref[...] = jnp.zeros_like(o_ref)

        dz = dz_ref[...].astype(MXU_DTYPE)
        o_ref[0:512, :] += _mm_tn(att_ref[...], dz)
        o_ref[512:1024, :] += _mm_tn(rec_ref[...], dz)

    return pl.pallas_call(
        body, name="dw_out", grid=(T // bt,), in_specs=[_rows(bt, 512), _rows(bt, 512), _rows(bt, D)],
        out_specs=_full((D, D)), out_shape=jax.ShapeDtypeStruct((D, D), F32), compiler_params=_params())(att, rec, dz1)


def _weight_grad(a, b, bm, name, exchange=None):
    bt = min(DW_TOKENS // 2 if b.dtype == F32 else DW_TOKENS, b.shape[0])
    if a.ndim == 3:
        assert a.shape[2] == bm
        T, M = a.shape[1], a.shape[0] * bm
        a_spec = pl.BlockSpec((None, bt, bm), lambda m, k: (m, k, 0))
    else:
        T, M = a.shape
        a_spec = pl.BlockSpec((bt, bm), lambda m, k: (k, m))
    N = b.shape[1]
    nk = T // bt

    out = _launch(functools.partial(_accumulate_tn), name, (M // bm, nk),
                  [a_spec, pl.BlockSpec((bt, N), lambda m, k: (k, 0))], [pl.BlockSpec((bm, N), lambda m, k: (m, 0))],
                  [jax.ShapeDtypeStruct((M, N), F32)], [], (a, b), exchange)
    return out[0] if exchange is None else out


def _adamw(w, g, m, v, name):
    R, C = w.shape
    tr = R // 8 if R % 64 == 0 else R
    c1 = 1.0 / (1.0 - ADAM_B1 ** ADAM_STEP)
    c2 = 1.0 / (1.0 - ADAM_B2 ** ADAM_STEP)

    def body(w_ref, g_ref, m_ref, v_ref, d_ref, nm_ref, nv_ref):
        g = g_ref[...]
        nm = ADAM_B1 * m_ref[...] + (1.0 - ADAM_B1) * g
        nv = ADAM_B2 * v_ref[...] + (1.0 - ADAM_B2) * g * g
        nm_ref[...] = nm
        nv_ref[...] = nv
        d_ref[...] = -ADAM_LR * ((nm * c1) / (jnp.sqrt(nv * c2) + ADAM_EPS) + ADAM_WD * w_ref[...])

    spec = pl.BlockSpec((tr, C), lambda i: (i, 0))
    return pl.pallas_call(
        body, name=name, grid=(R // tr,),
        in_specs=[spec] * 4, out_specs=[spec] * 3,
        out_shape=[jax.ShapeDtypeStruct((R, C), F32)] * 3,
        compiler_params=_params(),
    )(w, g, m, v)


def _adamw_halves(ws, mines, sibs, ms, vs, c, name, exchange=None):
    n, nb = len(ws), 4
    c1 = 1.0 / (1.0 - ADAM_B1 ** ADAM_STEP)
    c2 = 1.0 / (1.0 - ADAM_B2 ** ADAM_STEP)

    def body(c_ref, *refs):
        own = (pl.program_id(0) // nb) == c_ref[0]
        for i in range(n):
            w_ref, a_ref, b_ref, m_ref, v_ref = refs[5 * i:5 * i + 5]
            g_ref, d_ref, nm_ref, nv_ref = refs[5 * n + 4 * i:5 * n + 4 * i + 4]
            g = jnp.where(own, a_ref[...], b_ref[...])
            nm = ADAM_B1 * m_ref[...] + (1.0 - ADAM_B1) * g
            nv = ADAM_B2 * v_ref[...] + (1.0 - ADAM_B2) * g * g
            g_ref[...] = g
            nm_ref[...] = nm
            nv_ref[...] = nv
            d_ref[...] = -ADAM_LR * ((nm * c1) / (jnp.sqrt(nv * c2) + ADAM_EPS) + ADAM_WD * w_ref[...])

    in_specs, out_specs, out_shape, args = [], [], [], []
    for w, a, b, m, v in zip(ws, mines, sibs, ms, vs):
        R, C = w.shape
        tr = R // (2 * nb)
        assert tr % 8 == 0 and a.shape == (R // 2, C)
        full = pl.BlockSpec((tr, C), lambda i, c_ref: (i, 0))
        mine_spec = pl.BlockSpec((tr, C), lambda i, c_ref: (jnp.where(i // nb == c_ref[0], i % nb, nb - 1), 0))
        sib_spec = pl.BlockSpec((tr, C), lambda i, c_ref: (jnp.where(i // nb == c_ref[0], nb - 1, i % nb), 0))
        in_specs += [full, mine_spec, sib_spec, full, full]
        out_specs += [full] * 4
        out_shape += [jax.ShapeDtypeStruct((R, C), F32)] * 4
        args += [w, a, b, m, v]
    out = _launch(body, name, (2 * nb,), in_specs, out_specs, out_shape, [], (c, *args), exchange, prefetch=1)
    return [tuple(out[4 * i:4 * i + 4]) for i in range(n)], list(out[4 * n:])


def _add4(fs, name):
    n = len(fs)

    def body(*refs):
        for a_ref, o_ref in zip(refs[:n], refs[n:]):
            o_ref[...] = ((a_ref[0].astype(F32) + a_ref[1].astype(F32)) + a_ref[2].astype(F32)) + a_ref[3].astype(F32)

    for f in fs:
        assert (f.shape[1] // 2) % 16 == 0
    return pl.pallas_call(
        body, name=name, grid=(2,),
        in_specs=[pl.BlockSpec((4, f.shape[1] // 2, f.shape[2]), lambda i: (0, i, 0)) for f in fs],
        out_specs=[pl.BlockSpec((f.shape[1] // 2, f.shape[2]), lambda i: (i, 0)) for f in fs],
        out_shape=[jax.ShapeDtypeStruct(f.shape[1:], F32) for f in fs], compiler_params=_params())(*fs)


def _gather_first(wsrc, cpack):
    def body(w_ref, c_ref, gw_ref, gc_ref, send_sems, recv_sems, local_sem, csend, crecv, clocal):
        x, y, c = _pos()
        me = 2 * x + y
        chips = _other_chips(x, y)
        start, forward, finish = _gather_steps(w_ref, gw_ref, send_sems, recv_sems, local_sem)
        start()
        loc = pltpu.make_async_copy(c_ref, gc_ref.at[me], clocal)
        loc.start()

        def conv_copy(k, slot):
            px, py = chips[k]
            return pltpu.make_async_remote_copy(src_ref=c_ref, dst_ref=gc_ref.at[slot], send_sem=csend.at[k],
                                                recv_sem=crecv.at[k], device_id=(px, py, c), device_id_type=MESH)

        for k in range(3):
            conv_copy(k, me).start()
        forward()
        finish()
        for k, (px, py) in enumerate(chips):
            conv_copy(k, 2 * px + py).wait_recv()
        for k in range(3):
            conv_copy(k, me).wait_send()
        loc.wait()

    anyspec = pl.BlockSpec(memory_space=pl.ANY)
    return pl.pallas_call(
        body, name="gather_first",
        in_specs=[anyspec, anyspec], out_specs=[anyspec, anyspec],
        out_shape=[jax.ShapeDtypeStruct((4,) + wsrc.shape, wsrc.dtype), jax.ShapeDtypeStruct((4,) + cpack.shape, cpack.dtype)],
        scratch_shapes=GATHER_SCRATCH + [pltpu.SemaphoreType.DMA((3,)), pltpu.SemaphoreType.DMA((3,)), pltpu.SemaphoreType.DMA],
        compiler_params=_params(has_side_effects=True),
    )(wsrc, cpack)


def _all_devices_exchange(s):
    def make(ins, outs, sems):
        s_ref, o_ref = ins[0], outs[0]
        send_sems, recv_sems, local_sem = sems
        x, y, c = _pos()
        me = 4 * x + 2 * y + c
        loc = pltpu.make_async_copy(s_ref, o_ref.at[me], local_sem)

        def copy(k, slot):
            peer = (x ^ (k >> 2), y ^ ((k >> 1) & 1), c ^ (k & 1))
            return pltpu.make_async_remote_copy(src_ref=s_ref, dst_ref=o_ref.at[slot], send_sem=send_sems.at[k - 1],
                                                recv_sem=recv_sems.at[k - 1], device_id=peer, device_id_type=MESH)

        def start():
            loc.start()
            for k in range(1, 8):
                copy(k, me).start()

        def finish():
            for k in range(1, 8):
                copy(k, 4 * (x ^ (k >> 2)) + 2 * (y ^ ((k >> 1) & 1)) + (c ^ (k & 1))).wait_recv()
            for k in range(1, 8):
                copy(k, me).wait_send()
            loc.wait()

        return start, lambda: None, finish

    return _Exchange([s], [jax.ShapeDtypeStruct((8,) + s.shape, s.dtype)],
                     [pltpu.SemaphoreType.DMA((7,)), pltpu.SemaphoreType.DMA((7,)), pltpu.SemaphoreType.DMA], make)


def _sum_devices(a):
    def body(a_ref, o_ref):
        acc = a_ref[0]
        for d in range(1, 8):
            acc = acc + a_ref[d]
        o_ref[...] = acc

    vm = pl.BlockSpec(memory_space=pltpu.VMEM)
    return pl.pallas_call(body, name="sum_devices", in_specs=[vm], out_specs=vm,
                          out_shape=jax.ShapeDtypeStruct(a.shape[1:], F32), compiler_params=_params())(a)


def _swap_exchange(gs):
    n = len(gs)

    def make(ins, outs, sems):
        x, y, c = _pos()
        cps = []
        for i in range(n):
            half = gs[i].shape[1] // 2
            rows = pl.ds(pl.multiple_of((1 - c) * half, 8), half)
            cps.append(pltpu.make_async_remote_copy(src_ref=ins[i].at[:, rows, :], dst_ref=outs[i], send_sem=sems[0].at[i],
                                                    recv_sem=sems[1].at[i], device_id=(x, y, 1 - c), device_id_type=MESH))

        def start():
            for cp in cps:
                cp.start()

        def finish():
            for cp in cps:
                cp.wait()

        return start, lambda: None, finish

    return _Exchange(gs, [jax.ShapeDtypeStruct((4, g.shape[1] // 2, g.shape[2]), g.dtype) for g in gs],
                     [pltpu.SemaphoreType.DMA((n,)), pltpu.SemaphoreType.DMA((n,))], make)


def _scatter_exchange(ss):
    n = len(ss)

    def make(ins, outs, sems):
        send_sems, recv_sems, local_sems = sems
        x, y, c = _pos()
        me = 2 * x + y
        chips = _other_chips(x, y)
        locs = [pltpu.make_async_copy(ins[i].at[me], outs[i].at[me], local_sems.at[i]) for i in range(n)]

        def copy(i, k, src_slot, dst_slot):
            px, py = chips[k]
            return pltpu.make_async_remote_copy(src_ref=ins[i].at[src_slot], dst_ref=outs[i].at[dst_slot],
                                                send_sem=send_sems.at[3 * i + k], recv_sem=recv_sems.at[3 * i + k],
                                                device_id=(px, py, c), device_id_type=MESH)

        def start():
            for i in range(n):
                locs[i].start()
                for k, (px, py) in enumerate(chips):
                    copy(i, k, 2 * px + py, me).start()

        def finish():
            for i in range(n):
                for k, (px, py) in enumerate(chips):
                    copy(i, k, me, 2 * px + py).wait_recv()
            for i in range(n):
                for k, (px, py) in enumerate(chips):
                    copy(i, k, 2 * px + py, me).wait_send()
                locs[i].wait()

        return start, lambda: None, finish

    return _Exchange(ss, [jax.ShapeDtypeStruct(s.shape, s.dtype) for s in ss],
                     [pltpu.SemaphoreType.DMA((3 * n,)), pltpu.SemaphoreType.DMA((3 * n,)), pltpu.SemaphoreType.DMA((n,))], make)


def _send_exchange(rs):
    n = len(rs)

    def make(ins, outs, sems):
        x, y, c = _pos()
        cps = [pltpu.make_async_remote_copy(src_ref=ins[i], dst_ref=outs[i], send_sem=sems[0].at[i], recv_sem=sems[1].at[i],
                                            device_id=(x, y, 1 - c), device_id_type=MESH) for i in range(n)]

        def start():
            for cp in cps:
                cp.start()

        def finish():
            for cp in cps:
                cp.wait()

        return start, lambda: None, finish

    return _Exchange(rs, [jax.ShapeDtypeStruct(r.shape, r.dtype) for r in rs],
                     [pltpu.SemaphoreType.DMA((n,)), pltpu.SemaphoreType.DMA((n,))], make)


def _reduce_in_vmem(g):
    _, R, C = g.shape
    H = R // 2

    def body(g_ref, mine_ref, other_ref, sib, part, got, swap_sems, send_sems, recv_sems, last_sems):
        x, y, c = _pos()
        me = 2 * x + y
        chips = _other_chips(x, y)
        sibling = (x, y, 1 - c)
        mine = pl.ds(pl.multiple_of(c * H, 8), H)
        theirs = pl.ds(pl.multiple_of((1 - c) * H, 8), H)
        swap = pltpu.make_async_remote_copy(src_ref=g_ref.at[:, theirs, :], dst_ref=sib, send_sem=swap_sems.at[0],
                                            recv_sem=swap_sems.at[1], device_id=sibling, device_id_type=MESH)
        swap.start()
        swap.wait()
        part[...] = (g_ref[:, mine, :] + sib[...]).astype(BF16)

        def copy(k, src_slot, dst_slot):
            px, py = chips[k]
            return pltpu.make_async_remote_copy(src_ref=part.at[src_slot], dst_ref=got.at[dst_slot], send_sem=send_sems.at[k],
                                                recv_sem=recv_sems.at[k], device_id=(px, py, c), device_id_type=MESH)

        for k, (px, py) in enumerate(chips):
            copy(k, 2 * px + py, me).start()
        got[me] = part[me]
        for k, (px, py) in enumerate(chips):
            copy(k, me, 2 * px + py).wait_recv()
        for k, (px, py) in enumerate(chips):
            copy(k, 2 * px + py, me).wait_send()
        mine_ref[...] = ((got[0].astype(F32) + got[1].astype(F32)) + got[2].astype(F32)) + got[3].astype(F32)
        last = pltpu.make_async_remote_copy(src_ref=mine_ref, dst_ref=other_ref, send_sem=last_sems.at[0],
                                            recv_sem=last_sems.at[1], device_id=sibling, device_id_type=MESH)
        last.start()
        last.wait()

    vm = pl.BlockSpec(memory_space=pltpu.VMEM)
    half = jax.ShapeDtypeStruct((H, C), F32)
    return pl.pallas_call(
        body, name="reduce_late", in_specs=[vm], out_specs=[vm, vm], out_shape=[half, half],
        scratch_shapes=[pltpu.VMEM((4, H, C), F32), pltpu.VMEM((4, H, C), BF16), pltpu.VMEM((4, H, C), BF16),
                        pltpu.SemaphoreType.DMA((2,)), pltpu.SemaphoreType.DMA((3,)), pltpu.SemaphoreType.DMA((3,)),
                        pltpu.SemaphoreType.DMA((2,))],
        compiler_params=_params(has_side_effects=True))(g)


def _add_half(gs, rs, c, name):
    n = len(gs)

    def body(c_ref, *refs):
        for g_ref, r_ref, o_ref in zip(refs[:n], refs[n:2 * n], refs[2 * n:]):
            o_ref[...] = (g_ref[...] + r_ref[...]).astype(BF16)

    g_specs, r_specs, out_shape = [], [], []
    for g, r in zip(gs, rs):
        _, H, C = r.shape
        tr = H // 2
        assert tr % 16 == 0 and g.shape == (4, 2 * H, C)
        g_specs.append(pl.BlockSpec((1, tr, C), lambda j, i, c_ref: (j, c_ref[0] * 2 + i, 0)))
        r_specs.append(pl.BlockSpec((1, tr, C), lambda j, i, c_ref: (j, i, 0)))
        out_shape.append(jax.ShapeDtypeStruct((4, H, C), BF16))
    grid_spec = pltpu.PrefetchScalarGridSpec(num_scalar_prefetch=1, grid=(4, 2), in_specs=g_specs + r_specs, out_specs=r_specs)
    return pl.pallas_call(body, name=name, grid_spec=grid_spec, out_shape=out_shape, compiler_params=_params())(c, *gs, *rs)


def _block_diag(w):
    eye = jnp.eye(RNN_BLOCKS, dtype=w.dtype)
    return (eye[:, None, :, None] * w[:, :, None, :]).reshape(D_RNN, D_RNN)


def _diag_blocks(wd):
    d = wd.reshape(RNN_BLOCKS, 64, RNN_BLOCKS, 64)
    return jnp.stack([d[h, :, h, :] for h in range(RNN_BLOCKS)])


def _split_pack(a, first, last):
    out, base = {}, PACK_OFF[first]
    for i in range(first, last):
        s = a[:, PACK_OFF[i] - base:PACK_OFF[i + 1] - base]
        out[BIG_KEYS[i]] = s.reshape(4 * 256, 256) if BIG_KEYS[i] == "w_p_t" else s.reshape(-1, 1024)
    return out


def _layer_grads(x, p, tgt, gw, small, shard=None, core=None):
    row = lambda v: v.reshape(1, -1)
    wa = _block_diag(small["gate_a_w"]).astype(MXU_DTYPE)
    wx = _block_diag(small["gate_x_w"]).astype(MXU_DTYPE)
    sinks = small["attn_sinks"].reshape(1, HEADS)

    dist = shard is not None
    q, kv, xr, gr, xb = _in_proj(x, gw["w_in_t"])
    cut = PACK_OFF[1] + PACK_ROWS[1] // 2
    att, *ga = _attn_fwd(q, kv, sinks, _gather_exchange(shard[PACK_OFF[1]:cut]) if dist else None)
    xc, h, rec, *gb = _rnn_fwd(xr, gr, small["rnn_conv_w"], row(small["rnn_conv_b"]), wa, row(small["gate_a_b"]),
                               wx, row(small["gate_x_b"]), row(small["lru_lambda"]),
                               _gather_exchange(shard[cut:PACK_OFF[3]]) if dist else None)
    if dist:
        gw = {**gw, **_split_pack(jnp.concatenate([ga[0], gb[0]], axis=1), 1, 3)}
    g1, b1 = row(small["ln1_g"]), row(small["ln1_b"])
    fcw = small["ffn_conv_w"].reshape(3, NC, FF_CHUNK).transpose(1, 0, 2)
    fcb = small["ffn_conv_b"].reshape(NC, 1, FF_CHUNK)
    z1, h1b = _out_proj(att, rec, x, gw["w_out"], g1, b1)
    gate, ge, vd, act, *gc = _ffn_up(h1b, gw["w_up_t"], fcw, fcb,
                                     _gather_exchange(shard[PACK_OFF[3]:PACK_OFF[6]]) if dist else None)
    if dist:
        gw = {**gw, **_split_pack(gc[0], 3, 6)}
    dz2, dz2b, dpre, dpp, vec2 = _ffn_down(act, z1, p, tgt, gw["w_down"], gw["w_g"], gw["w_p_t"], g1, b1,
                                           row(small["ln2_g"]), row(small["ln2_b"]), row(small["ple_gate_b"]))
    dup, dfc = _ffn_bwd(dz2b, gate, ge, vd, gw["w_down"], fcw)
    dz1, vec1 = _ffn_dh1(dup, dz2, dpre, z1, gw["w_up_t"], gw["w_g"], g1, b1)
    per_chip = 2 * D_FF // 4 // FF_CHUNK
    big = {"w_ffn_up": _weight_grad_cols(
        h1b, dup, "dw_up", 2 * NC, lambda bt: pl.BlockSpec((bt, FF_CHUNK), lambda m, k: (k, m)), (4, D, 2 * D_FF // 4),
        pl.BlockSpec((None, D, FF_CHUNK), lambda m, k: (2 * (m % 2) + (m // 2) // per_chip, 0, (m // 2) % per_chip)))[0]}
    g_dn, *got_up = _weight_grad(act, dz2b, FF_CHUNK, "dw_down", _swap_exchange([big["w_ffn_up"]])) if dist else (
        _weight_grad(act, dz2b, FF_CHUNK, "dw_down"),)
    big["w_ffn_down"] = g_dn.reshape(4, D_FF // 4, D)
    big["ple_gate_w"] = _weight_grad(h1b, dpre, 512, "dw_gate").reshape(4, D // 4, D)
    big["ple_proj"] = _weight_grad(p, dpp, PLE, "dw_proj").reshape(PLE, 4, D // 4).transpose(1, 0, 2)
    big["w_out"] = _dw_out(att, rec, dz1).reshape(4, D // 4, D)
    reduced = None
    if dist:
        g_ffn = [big[k] for k in EARLY_WEIGHTS]
        ex = _swap_exchange(g_ffn[1:])
    datt, drec, *got = _out_proj_bwd(dz1, gw["w_out"], ex if dist else None)
    if dist:
        sums = _add_half(g_ffn, got_up + got, core, "add_half_ffn")
        ex, ex2 = _scatter_exchange(sums[:1]), _scatter_exchange(sums[1:])
    dxr, dgr, dwa, dwx, dvec, *got = _rnn_bwd(drec, gr, h, xc, xr, small["rnn_conv_w"], wa, row(small["gate_a_b"]),
                                              wx, row(small["gate_x_b"]), row(small["lru_lambda"]), ex if dist else None)
    dq, dkv, dsinks, *got2 = _attn_bwd(q, kv, datt, sinks, ex2 if dist else None)
    if dist:
        mine = _add4(got + got2, "add_chips_ffn")
        big = {}
    sg = {
        "attn_sinks": dsinks[:, 0],
        "rnn_conv_w": dvec[4:8],
        "rnn_conv_b": dvec[3],
        "gate_a_w": _diag_blocks(dwa),
        "gate_a_b": dvec[0],
        "gate_x_w": _diag_blocks(dwx),
        "gate_x_b": dvec[1],
        "lru_lambda": dvec[2],
        "ln1_g": vec1[0],
        "ln1_b": vec1[1],
        "ffn_conv_w": dfc[:, 0:3].transpose(1, 0, 2).reshape(3, D_FF),
        "ffn_conv_b": dfc[:, 3].reshape(D_FF),
        "ple_gate_b": vec2[3],
        "ln2_g": vec2[1],
        "ln2_b": vec2[2],
    }
    loss = vec2[0, 0:1]
    grad_x, du = _in_proj_bwd(dq, dkv, dxr, dgr, dz1, gw["w_in_t"])
    ex = None
    if dist:
        ex = _join_exchanges(_send_exchange(mine), _all_devices_exchange(_pack_vecs([sg[k] for k in SMALL] + [loss])[0]))
    big["w_in"], *got = _weight_grad_cols(
        xb, du, "dw_in", 4, lambda bt: pl.BlockSpec((None, bt, D_IN // 4), lambda j, k: (j, k, 0)), (4, D, D_IN // 4),
        pl.BlockSpec((None, D, D_IN // 4), lambda j, k: (j, 0, 0)), ex)
    if dist:
        reduced = (mine, got[:len(mine)])
    return grad_x, big, sg, loss, reduced, got[-1:]


BIG = ("w_in", "w_ffn_up", "w_out", "w_ffn_down", "ple_gate_w", "ple_proj")
BIG_KEYS = ("w_in_t", "w_up_t", "w_out", "w_down", "w_g", "w_p_t")
BIG_T = (True, True, False, False, False, True)
EARLY_WEIGHTS = ("w_ffn_up", "w_ffn_down", "ple_gate_w", "ple_proj", "w_out")
LATE_WEIGHTS = ("w_in",)
SMALL = ("attn_sinks", "rnn_conv_w", "rnn_conv_b", "gate_a_w", "gate_a_b", "gate_x_w", "gate_x_b", "lru_lambda",
         "ln1_g", "ln1_b", "ffn_conv_w", "ffn_conv_b", "ple_gate_b", "ln2_g", "ln2_b")
WEIGHTS = ("w_in", "attn_sinks", "rnn_conv_w", "rnn_conv_b", "gate_a_w", "gate_a_b", "gate_x_w", "gate_x_b",
           "lru_lambda", "w_out", "ln1_g", "ln1_b", "w_ffn_up", "ffn_conv_w", "ffn_conv_b", "w_ffn_down",
           "ple_gate_w", "ple_gate_b", "ple_proj", "ln2_g", "ln2_b")


def _pack_big(d, first=0, last=6):
    parts = []
    for name, t in zip(BIG[first:last], BIG_T[first:last]):
        a = d[name]
        a = a.T if t else a
        parts.append(a.reshape(-1, 1024))
    return jnp.concatenate(parts, axis=0)


def _pack_vecs(items):
    parts, offs, n = [], [], 0
    for a in items:
        f = a.reshape(-1).astype(F32)
        pad = (-f.shape[0]) % 128
        parts.append(jnp.pad(f, (0, pad)))
        offs.append(n)
        n += (f.shape[0] + pad) // 128
    padr = (-n) % 8
    if padr:
        parts.append(jnp.zeros((padr * 128,), F32))
    return jnp.concatenate(parts).reshape(-1, 128), offs


def _unpack_vecs(a, offs, shapes):
    flat = a.reshape(-1)
    out = []
    for o, s in zip(offs, shapes):
        n = 1
        for d in s:
            n *= d
        out.append(flat[o * 128:o * 128 + n].reshape(s))
    return out


def kernel(x, p, w_in, attn_sinks, rnn_conv_w, rnn_conv_b, gate_a_w, gate_a_b, gate_x_w, gate_x_b, lru_lambda, w_out, ln1_g, ln1_b, w_ffn_up, ffn_conv_w, ffn_conv_b, w_ffn_down, ple_gate_w, ple_gate_b, ple_proj, ln2_g, ln2_b, loss_target, m_w_in, m_attn_sinks, m_rnn_conv_w, m_rnn_conv_b, m_gate_a_w, m_gate_a_b, m_gate_x_w, m_gate_x_b, m_lru_lambda, m_w_out, m_ln1_g, m_ln1_b, m_w_ffn_up, m_ffn_conv_w, m_ffn_conv_b, m_w_ffn_down, m_ple_gate_w, m_ple_gate_b, m_ple_proj, m_ln2_g, m_ln2_b, v_w_in, v_attn_sinks, v_rnn_conv_w, v_rnn_conv_b, v_gate_a_w, v_gate_a_b, v_gate_x_w, v_gate_x_b, v_lru_lambda, v_w_out, v_ln1_g, v_ln1_b, v_w_ffn_up, v_ffn_conv_w, v_ffn_conv_b, v_w_ffn_down, v_ple_gate_w, v_ple_gate_b, v_ple_proj, v_ln2_g, v_ln2_b):
    w = dict(w_in=w_in, attn_sinks=attn_sinks, rnn_conv_w=rnn_conv_w, rnn_conv_b=rnn_conv_b, gate_a_w=gate_a_w,
             gate_a_b=gate_a_b, gate_x_w=gate_x_w, gate_x_b=gate_x_b, lru_lambda=lru_lambda, w_out=w_out, ln1_g=ln1_g,
             ln1_b=ln1_b, w_ffn_up=w_ffn_up, ffn_conv_w=ffn_conv_w, ffn_conv_b=ffn_conv_b, w_ffn_down=w_ffn_down,
             ple_gate_w=ple_gate_w, ple_gate_b=ple_gate_b, ple_proj=ple_proj, ln2_g=ln2_g, ln2_b=ln2_b)
    m = dict(w_in=m_w_in, attn_sinks=m_attn_sinks, rnn_conv_w=m_rnn_conv_w, rnn_conv_b=m_rnn_conv_b, gate_a_w=m_gate_a_w,
             gate_a_b=m_gate_a_b, gate_x_w=m_gate_x_w, gate_x_b=m_gate_x_b, lru_lambda=m_lru_lambda, w_out=m_w_out,
             ln1_g=m_ln1_g, ln1_b=m_ln1_b, w_ffn_up=m_w_ffn_up, ffn_conv_w=m_ffn_conv_w, ffn_conv_b=m_ffn_conv_b,
             w_ffn_down=m_w_ffn_down, ple_gate_w=m_ple_gate_w, ple_gate_b=m_ple_gate_b, ple_proj=m_ple_proj,
             ln2_g=m_ln2_g, ln2_b=m_ln2_b)
    v = dict(w_in=v_w_in, attn_sinks=v_attn_sinks, rnn_conv_w=v_rnn_conv_w, rnn_conv_b=v_rnn_conv_b, gate_a_w=v_gate_a_w,
             gate_a_b=v_gate_a_b, gate_x_w=v_gate_x_w, gate_x_b=v_gate_x_b, lru_lambda=v_lru_lambda, w_out=v_w_out,
             ln1_g=v_ln1_g, ln1_b=v_ln1_b, w_ffn_up=v_w_ffn_up, ffn_conv_w=v_ffn_conv_w, ffn_conv_b=v_ffn_conv_b,
             w_ffn_down=v_w_ffn_down, ple_gate_w=v_ple_gate_w, ple_gate_b=v_ple_gate_b, ple_proj=v_ple_proj,
             ln2_g=v_ln2_g, ln2_b=v_ln2_b)
    w, m, v = ({k: a[0] for k, a in d.items()} for d in (w, m, v))
    chip = 2 * lax.axis_index("x") + lax.axis_index("y")
    core = lax.axis_index("c")

    wpack = _pack_big(w)
    cpack, _ = _pack_vecs([w["rnn_conv_w"], w["ffn_conv_w"]])
    shard = wpack.astype(MXU_DTYPE)
    g_in, gcp = _gather_first(shard[PACK_OFF[0]:PACK_OFF[1]], cpack)
    gw = _split_pack(g_in, 0, 1)
    small = {k: w[k] for k in SMALL}
    small["rnn_conv_w"] = gcp[:, 0:4].reshape(4, 4, 128).transpose(1, 0, 2).reshape(4, 512)
    small["ffn_conv_w"] = gcp[:, 4:22].reshape(4, 3, 768).transpose(1, 0, 2).reshape(3, 3072)

    core1 = core.reshape(1).astype(jnp.int32)
    grad_x, big, sg, loss, ffn_halves, small_all = _layer_grads(x[0], p[0, 0], loss_target[0], gw, small, shard, core1)

    shapes = [sg[k].shape for k in SMALL] + [(1,)]
    _, offs = _pack_vecs([jnp.zeros(s, F32) for s in shapes])
    red = dict(zip(SMALL + ("loss",), _unpack_vecs(_sum_devices(small_all[0]), offs, shapes)))
    red["rnn_conv_w"] = lax.dynamic_slice_in_dim(red["rnn_conv_w"], chip * 128, 128, axis=1)
    red["ffn_conv_w"] = lax.dynamic_slice_in_dim(red["ffn_conv_w"], chip * 768, 768, axis=1)

    late_mine, late_other = ([a] for a in _reduce_in_vmem(big["w_in"]))

    def adamw(names, mine, other, name):
        out, _ = _adamw_halves([w[k] for k in names], mine, other, [m[k] for k in names], [v[k] for k in names],
                               core1, name)
        return dict(zip(names, out))

    big_out = {**adamw(LATE_WEIGHTS, late_mine, late_other, "adamw_late"), **adamw(EARLY_WEIGHTS, *ffn_halves, "adamw_early")}
    wsm, offs2 = _pack_vecs([w[k] for k in SMALL])
    gsm, _ = _pack_vecs([red[k] for k in SMALL])
    msm, _ = _pack_vecs([m[k] for k in SMALL])
    vsm, _ = _pack_vecs([v[k] for k in SMALL])
    dsm, nmsm, nvsm = _adamw(wsm, gsm, msm, vsm, "adamw_small")
    shapes2 = [w[k].shape for k in SMALL]

    def named(n, smallp):
        d = {k: out[n][None] for k, out in big_out.items()}
        d.update({k: a[None] for k, a in zip(SMALL, _unpack_vecs(smallp, offs2, shapes2))})
        return [d[k] for k in WEIGHTS]

    return (red["loss"].reshape(()), grad_x[None], *named(0, gsm), *named(1, dsm), *named(2, nmsm), *named(3, nvsm))
```

```python
import functools

import jax
import jax.numpy as jnp
from jax import lax
from jax.experimental import pallas as pl
from jax.experimental.pallas import tpu as pltpu

F32 = jnp.float32
BF16 = jnp.bfloat16
MXU_DTYPE = jnp.bfloat16

D = 1024
D_ATT = 512
D_KV = 128
D_RNN = 512
D_IN = 1792
D_FF = 3072
FF_CHUNK = 768
PLE = 256
HEADS = 8
HEAD_DIM = 64
BLK = 128
ATTN_BLOCKS = 8
DW_TOKENS = 4096
RNN_BLOCKS = 8
LN_EPS = 1e-5
LRU_C = 8.0
ALPHA = float(2.0 ** 0.25)
SCALE = HEAD_DIM ** -0.5
NEG = -1e30

ADAM_LR = 0.001
ADAM_B1 = 0.9
ADAM_B2 = 0.999
ADAM_EPS = 1e-08
ADAM_WD = 0.01
ADAM_STEP = 10

VMEM_LIMIT_BYTES = 56 * 1024 * 1024
MESH = pl.DeviceIdType.MESH

PACK_ROWS = (448, 1536, 256, 768, 256, 64)
PACK_OFF = tuple(sum(PACK_ROWS[:i]) for i in range(len(PACK_ROWS) + 1))
PACK_TOTAL = PACK_OFF[-1]


def _params(**kw):
    return pltpu.CompilerParams(vmem_limit_bytes=VMEM_LIMIT_BYTES, **kw)


def _mm(a, b):
    return jnp.dot(a.astype(MXU_DTYPE), b.astype(MXU_DTYPE), preferred_element_type=F32)


def _mm_nt(a, b):
    return lax.dot_general(a.astype(MXU_DTYPE), b.astype(MXU_DTYPE), (((1,), (1,)), ((), ())),
                           preferred_element_type=F32)


def _mm_tn(a, b):
    return lax.dot_general(a.astype(MXU_DTYPE), b.astype(MXU_DTYPE), (((0,), (0,)), ((), ())),
                           preferred_element_type=F32)


def _sigmoid(x):
    return 0.5 + 0.5 * jnp.tanh(0.5 * x)


def _gelu(x):
    c = 0.7978845608028654
    k = 0.044715
    x2 = x * x
    t = jnp.tanh(x * (c + (c * k) * x2))
    h = 0.5 * (1.0 + t)
    return x * h, h * (1.0 + (x * (1.0 - t)) * (c + (3.0 * c * k) * x2))


def _shift_rows(x, s, edge8):
    R = x.shape[0]
    row8 = lax.broadcasted_iota(jnp.int32, (8, x.shape[1]), 0)
    if s > 0:
        rolled = pltpu.roll(x, s, 0)
        first = jnp.where(row8 < s, pltpu.roll(edge8, s, 0), rolled[0:8])
        return jnp.concatenate([first, rolled[8:]], axis=0)
    k = -s
    rolled = pltpu.roll(x, R - k, 0)
    last = jnp.where(row8 >= 8 - k, pltpu.roll(edge8, 8 - k, 0), rolled[R - 8:])
    return jnp.concatenate([rolled[:R - 8], last], axis=0)


def _softplus(x):
    return jnp.maximum(x, 0.0) + jnp.log(1.0 + jnp.exp(-jnp.abs(x)))


def _ln(z, g, b):
    mu = jnp.mean(z, axis=-1, keepdims=True)
    zc = z - mu
    var = jnp.mean(zc * zc, axis=-1, keepdims=True)
    rstd = lax.rsqrt(var + LN_EPS)
    xhat = zc * rstd
    return xhat * g + b, xhat, rstd


def _ln_bwd(dy, xhat, rstd, g):
    dxh = dy * g
    m1 = jnp.mean(dxh, axis=-1, keepdims=True)
    m2 = jnp.mean(dxh * xhat, axis=-1, keepdims=True)
    return rstd * (dxh - m1 - xhat * m2)


def _colsum(x):
    return jnp.sum(x, axis=0, keepdims=True)


def _full(shape):
    nd = len(shape)
    return pl.BlockSpec(shape, lambda *_: (0,) * nd)


def _rows(tm, cols, fn=None):
    if fn is None:
        return pl.BlockSpec((tm, cols), lambda i: (i, 0))
    return pl.BlockSpec((tm, cols), lambda i: (fn(i), 0))


def _heads(tm):
    return pl.BlockSpec((HEADS, tm, HEAD_DIM), lambda i: (0, i, 0))


def _in_proj(x, w_in_t, exchange=None):
    T = x.shape[0]
    tm = min(1024, T)

    def body(x_ref, w_ref, q_ref, kv_ref, xr_ref, gr_ref, xb_ref):
        xb = x_ref[...].astype(MXU_DTYPE)
        xb_ref[...] = xb.astype(BF16)
        q = _mm_nt(xb, w_ref[0:512, :]) * SCALE
        for h in range(HEADS):
            q_ref[h] = q[:, h * 64:(h + 1) * 64].astype(BF16)
        kv_ref[...] = _mm_nt(xb, w_ref[512:768, :]).astype(BF16)
        xr_ref[...] = _mm_nt(xb, w_ref[768:1280, :])
        gr_ref[...] = _mm_nt(xb, w_ref[1280:1792, :])

    return _launch(body, "in_proj", (T // tm,), [_rows(tm, D), _full((D_IN, D))],
                   [_heads(tm), _rows(tm, 256), _rows(tm, 512), _rows(tm, 512), _rows(tm, D)],
                   [jax.ShapeDtypeStruct((HEADS, T, 64), BF16), jax.ShapeDtypeStruct((T, 256), BF16),
                    jax.ShapeDtypeStruct((T, 512), F32), jax.ShapeDtypeStruct((T, 512), F32),
                    jax.ShapeDtypeStruct((T, D), BF16)], [], (x, w_in_t), exchange)


def _attn_band(kv_ref, i):
    cur = pl.multiple_of(i * BLK, BLK)
    prev = pl.multiple_of(jnp.maximum(i - 1, 0) * BLK, BLK)
    band = jnp.concatenate([kv_ref[pl.ds(prev, BLK), :], kv_ref[pl.ds(cur, BLK), :]], axis=0)
    key = lax.broadcasted_iota(jnp.int32, (2 * BLK, 4 * BLK), 0)
    qry = lax.broadcasted_iota(jnp.int32, (2 * BLK, 4 * BLK), 1) & (BLK - 1)
    in_prev = jnp.logical_and(jnp.logical_and(key < BLK, key > qry), i > 0)
    mask = jnp.logical_or(in_prev, jnp.logical_and(key >= BLK, key - BLK <= qry))
    return band, mask, cur, prev


def _attn_scores(band, mask, qs, s_ref, g):
    st = jnp.where(mask, _mm_nt(band[:, g * 64:(g + 1) * 64], qs), NEG)
    lane = lax.broadcasted_iota(jnp.int32, (1, 4 * BLK), 1)
    sv = jnp.where(lane < BLK, s_ref[0, 4 * g],
                   jnp.where(lane < 2 * BLK, s_ref[0, 4 * g + 1], jnp.where(lane < 3 * BLK, s_ref[0, 4 * g + 2], s_ref[0, 4 * g + 3])))
    m = jnp.maximum(jnp.max(st, axis=0, keepdims=True), sv)
    p = jnp.exp(st - m)
    ps = jnp.exp(sv - m)
    return p, ps, jnp.sum(p, axis=0, keepdims=True) + ps


def _pos():
    return lax.axis_index("x"), lax.axis_index("y"), lax.axis_index("c")


def _other_chips(x, y):
    return [(1 - x, y), (x, 1 - y), (1 - x, 1 - y)]


def _gather_steps(w_ref, gw_ref, send_sems, recv_sems, local_sem):
    x, y, c = _pos()
    me = 2 * x + y
    chips = _other_chips(x, y)
    half = w_ref.shape[0] // 2
    mine = pl.ds(pl.multiple_of(c * half, 16), half)
    theirs = pl.ds(pl.multiple_of((1 - c) * half, 16), half)
    loc = pltpu.make_async_copy(w_ref, gw_ref.at[me], local_sem)

    def copy(k, src, dst, to):
        return pltpu.make_async_remote_copy(src_ref=src, dst_ref=dst, send_sem=send_sems.at[k], recv_sem=recv_sems.at[k],
                                            device_id=to, device_id_type=MESH)

    def out(k):
        px, py = chips[k]
        return copy(k, w_ref.at[mine], gw_ref.at[me, mine], (px, py, c))

    def fwd(k, rows):
        px, py = chips[k]
        return copy(3 + k, gw_ref.at[2 * px + py, rows], gw_ref.at[2 * px + py, rows], (x, y, 1 - c))

    def start():
        loc.start()
        for k in range(3):
            out(k).start()

    def forward():
        for k in range(3):
            px, py = chips[k]
            copy(k, w_ref.at[mine], gw_ref.at[2 * px + py, mine], (px, py, c)).wait_recv()
            fwd(k, mine).start()

    def finish():
        for k in range(3):
            fwd(k, theirs).wait_recv()
        for k in range(3):
            out(k).wait_send()
            fwd(k, mine).wait_send()
        loc.wait()

    return start, forward, finish


GATHER_SCRATCH = [pltpu.SemaphoreType.DMA((6,)), pltpu.SemaphoreType.DMA((6,)), pltpu.SemaphoreType.DMA]


class _Exchange:
    def __init__(self, args, out_shape, scratch, make):
        self.args, self.out_shape, self.scratch, self.make = list(args), list(out_shape), list(scratch), make


def _join_exchanges(a, b):
    na, nao, nas = len(a.args), len(a.out_shape), len(a.scratch)

    def make(ins, outs, sems):
        steps_a = a.make(ins[:na], outs[:nao], sems[:nas])
        steps_b = b.make(ins[na:], outs[nao:], sems[nas:])

        def both(f, g):
            def run():
                f()
                g()
            return run

        return tuple(both(f, g) for f, g in zip(steps_a, steps_b))

    return _Exchange(a.args + b.args, a.out_shape + b.out_shape, a.scratch + b.scratch, make)


def _gather_exchange(wsrc):
    return _Exchange([wsrc], [jax.ShapeDtypeStruct((4,) + wsrc.shape, wsrc.dtype)], GATHER_SCRATCH,
                     lambda ins, outs, sems: _gather_steps(ins[0], outs[0], *sems))


def _launch(body, name, grid, in_specs, out_specs, out_shape, scratch, args, exchange=None, prefetch=0):
    def call(fn, fn_name, ins, outs, shapes, scr, operands, effects):
        spec = pltpu.PrefetchScalarGridSpec(num_scalar_prefetch=prefetch, grid=grid, in_specs=ins, out_specs=outs,
                                            scratch_shapes=scr)
        return pl.pallas_call(fn, name=fn_name, grid_spec=spec, out_shape=shapes,
                              compiler_params=_params(has_side_effects=effects))(*operands)

    if exchange is None:
        return call(body, name, list(in_specs), list(out_specs), list(out_shape), list(scratch), args, False)
    n_in, n_out, ei, eo, ns = len(in_specs), len(out_specs), len(exchange.args), len(exchange.out_shape), len(exchange.scratch)
    nsteps = 1
    for g in grid:
        nsteps *= g

    def wrapped(*refs):
        scalars, refs = refs[:prefetch], refs[prefetch:]
        ins, xin = refs[:n_in], refs[n_in:n_in + ei]
        outs, xout = refs[n_in + ei:n_in + ei + n_out], refs[n_in + ei + n_out:n_in + ei + n_out + eo]
        rest = refs[n_in + ei + n_out + eo:]
        own, sems = rest[:len(rest) - ns], rest[len(rest) - ns:]
        start, forward, finish = exchange.make(xin, xout, sems)
        i = pl.program_id(0)
        for d in range(1, len(grid)):
            i = i * grid[d] + pl.program_id(d)
        pl.when(i == 0)(start)
        body(*scalars, *ins, *outs, *own)
        pl.when(i == max(nsteps - 3, 0))(forward)
        pl.when(i == nsteps - 1)(finish)

    anyspec = pl.BlockSpec(memory_space=pl.ANY)
    return call(wrapped, name + "_x", list(in_specs) + [anyspec] * ei, list(out_specs) + [anyspec] * eo,
                list(out_shape) + exchange.out_shape, list(scratch) + exchange.scratch, (*args, *exchange.args), True)


def _attn_fwd(q, kv, sinks, exchange=None):
    T = kv.shape[0]
    nblk = min(ATTN_BLOCKS, T // BLK)

    def body(q_ref, kv_ref, s_ref, o_ref):
        for b in range(nblk):
            rows = slice(b * BLK, (b + 1) * BLK)
            band, mask, _, _ = _attn_band(kv_ref, nblk * pl.program_id(0) + b)
            for g in range(2):
                qs = q_ref[4 * g:4 * g + 4, rows, :].reshape(4 * BLK, HEAD_DIM)
                p, _, den = _attn_scores(band, mask, qs, s_ref, g)
                ot = _mm_tn(band[:, 128:256], p) * (1.0 / den)
                for hh in range(4):
                    o = ot[:, hh * BLK:(hh + 1) * BLK].T
                    o_ref[rows, (4 * g + hh) * 64:(4 * g + hh + 1) * 64] = o[:, g * 64:(g + 1) * 64].astype(BF16)

    tq = nblk * BLK
    return _launch(body, "attn_fwd", (T // tq,), [_heads(tq), _full((T, 256)), pl.BlockSpec(memory_space=pltpu.SMEM)],
                   [_rows(tq, 512)], [jax.ShapeDtypeStruct((T, 512), BF16)], [], (q, kv, sinks), exchange)


def _attn_bwd(q, kv, do, sinks, exchange=None):
    T = kv.shape[0]
    nblk = min(ATTN_BLOCKS, T // BLK)

    def body(q_ref, kv_ref, do_ref, s_ref, dq_ref, dkv_ref, ds_ref):
        @pl.when(pl.program_id(0) == 0)
        def _():
            ds_ref[...] = jnp.zeros_like(ds_ref)

        for b in range(nblk):
            rows = slice(b * BLK, (b + 1) * BLK)
            band, mask, cur, prev = _attn_band(kv_ref, nblk * pl.program_id(0) + b)
            for g in range(2):
                qs = q_ref[4 * g:4 * g + 4, rows, :].reshape(4 * BLK, HEAD_DIM)
                dos = do_ref[4 * g:4 * g + 4, rows, :].reshape(4 * BLK, HEAD_DIM)
                p, ps, den = _attn_scores(band, mask, qs, s_ref, g)
                inv = 1.0 / den
                p = p * inv
                dpt = _mm_nt(band[:, 128 + g * 64:192 + g * 64], dos)
                delta = jnp.sum(p * dpt, axis=0, keepdims=True)
                dst = p * (dpt - delta)
                dsv = -(ps * inv) * delta
                for hh in range(4):
                    dsink = jnp.sum(dsv[:, hh * BLK:(hh + 1) * BLK], axis=1, keepdims=True)
                    ds_ref[4 * g + hh:4 * g + hh + 1, :] += jnp.broadcast_to(dsink, (1, 128))
                dqt = _mm_tn(band[:, 0:128], dst) * SCALE
                for hh in range(4):
                    dqh = dqt[:, hh * BLK:(hh + 1) * BLK].T
                    dq_ref[rows, (4 * g + hh) * 64:(4 * g + hh + 1) * 64] = dqh[:, g * 64:(g + 1) * 64].astype(BF16)
                dk = _mm(dst, qs)
                dv = _mm(p, dos)
                dkv_ref[pl.ds(cur, BLK), g * 64:(g + 1) * 64] = dk[BLK:2 * BLK]
                dkv_ref[pl.ds(cur, BLK), 128 + g * 64:192 + g * 64] = dv[BLK:2 * BLK]
                dkv_ref[pl.ds(prev, BLK), g * 64:(g + 1) * 64] += dk[0:BLK]
                dkv_ref[pl.ds(prev, BLK), 128 + g * 64:192 + g * 64] += dv[0:BLK]

    tq = nblk * BLK
    return _launch(body, "attn_bwd", (T // tq,),
                   [_heads(tq), _full((T, 256)), _heads(tq), pl.BlockSpec(memory_space=pltpu.SMEM)],
                   [_rows(tq, 512), _full((T, 256)), _full((8, 128))],
                   [jax.ShapeDtypeStruct((T, 512), BF16), jax.ShapeDtypeStruct((T, 256), F32),
                    jax.ShapeDtypeStruct((8, 128), F32)], [], (q, kv, do, sinks), exchange)


def _rows8(tm, cols):
    return lax.broadcasted_iota(jnp.int32, (tm, cols), 0) & 7


def _lru_gates(xc, wa, ba, wx, bx, lam):
    r = _sigmoid(_mm(xc, wa) + ba)
    ii = _sigmoid(_mm(xc, wx) + bx)
    sp = _softplus(-lam)
    la = -LRU_C * r * sp
    a = jnp.exp(la)
    m = jnp.sqrt(-jnp.tanh(la) * (a * a + 1.0))
    return r, ii, sp, a, m


def _rnn_fwd(xr, gr, cw, cb, wa, ba, wx, bx, lam, exchange=None):
    T = xr.shape[0]
    tm = 512
    C = D_RNN

    def body(xr_ref, gr_ref, cw_ref, cb_ref, wa_ref, ba_ref, wx_ref, bx_ref, lam_ref,
             xc_ref, h_ref, rec_ref, ext, a_s, b_s, carry):
        i = pl.program_id(0)

        @pl.when(i == 0)
        def _():
            ext[...] = jnp.zeros((8, C), F32)
            carry[...] = jnp.zeros((8, C), F32)

        xr = xr_ref[...]
        edge = ext[...]
        xc = cb_ref[...] + cw_ref[3:4, :] * xr
        for k in range(3):
            xc = xc + cw_ref[k:k + 1, :] * _shift_rows(xr, 3 - k, edge)
        ext[...] = xr[tm - 8:tm, :]
        xc_ref[...] = xc
        _, ii, _, a, m = _lru_gates(xc, wa_ref[...], ba_ref[...], wx_ref[...], bx_ref[...], lam_ref[...])
        b = m * ii * xc
        r8 = _rows8(tm, C)
        for d in (1, 2, 4):
            ok = r8 >= d
            a_sh = jnp.where(ok, pltpu.roll(a, d, 0), 1.0)
            b_sh = jnp.where(ok, pltpu.roll(b, d, 0), 0.0)
            b = a * b_sh + b
            a = a * a_sh
        a_s[...] = a
        b_s[...] = b

        def step(g, hin):
            s = pl.multiple_of(g * 8, 8)
            hg = a_s[pl.ds(s, 8), :] * hin + b_s[pl.ds(s, 8), :]
            h_ref[pl.ds(s, 8), :] = hg
            return jnp.broadcast_to(hg[7:8, :], (8, C))

        carry[...] = lax.fori_loop(0, tm // 8, step, carry[...], unroll=4)
        ge, _ = _gelu(gr_ref[...])
        rec_ref[...] = (h_ref[...] * ge).astype(BF16)

    vec = _full((1, C))
    in_specs = [_rows(tm, C), _rows(tm, C), _full((4, C)), vec, _full((C, C)), vec, _full((C, C)), vec, vec]
    out_specs = [_rows(tm, C), _rows(tm, C), _rows(tm, C)]
    out_shape = [jax.ShapeDtypeStruct((T, C), F32), jax.ShapeDtypeStruct((T, C), F32), jax.ShapeDtypeStruct((T, C), BF16)]
    scratch = [pltpu.VMEM((8, C), F32), pltpu.VMEM((tm, C), F32), pltpu.VMEM((tm, C), F32), pltpu.VMEM((8, C), F32)]
    return _launch(body, "rnn_fwd", (T // tm,), in_specs, out_specs, out_shape, scratch,
                   (xr, gr, cw, cb, wa, ba, wx, bx, lam), exchange)


def _rnn_bwd(drec, gr, h, xc, xr, cw, wa, ba, wx, bx, lam, exchange=None):
    T = xr.shape[0]
    tm = 512
    C = D_RNN
    nt = T // tm
    t8 = tm // 8

    def body(drec_ref, gr_ref, h_ref, hp_ref, xc_ref, xr_ref, cw_ref, wa_ref, ba_ref, wx_ref, bx_ref,
             lam_ref, dxr_ref, dgr_ref, dwa_ref, dwx_ref, dvec_ref, c_s, g_s, gout, ext, anext, gcarry):
        i = pl.program_id(0)
        j = nt - 1 - i

        @pl.when(i == 0)
        def _():
            dwa_ref[...] = jnp.zeros_like(dwa_ref)
            dwx_ref[...] = jnp.zeros_like(dwx_ref)
            dvec_ref[...] = jnp.zeros_like(dvec_ref)
            anext[...] = jnp.zeros((8, C), F32)
            gcarry[...] = jnp.zeros((8, C), F32)
            ext[...] = jnp.zeros((8, C), F32)

        xc = xc_ref[...]
        lam = lam_ref[...]
        r, ii, sp, a, m = _lru_gates(xc, wa_ref[...], ba_ref[...], wx_ref[...], bx_ref[...], lam)
        ge, dge = _gelu(gr_ref[...])
        drec = drec_ref[...]
        hh = h_ref[...]
        dgr_ref[...] = (drec * hh * dge).astype(BF16)
        dh = drec * ge
        rowi = lax.broadcasted_iota(jnp.int32, (tm, C), 0)
        c = jnp.where(rowi == tm - 1, jnp.broadcast_to(anext[0:1, :], (tm, C)), pltpu.roll(a, tm - 1, 0))
        anext[...] = a[0:8, :]
        r8 = rowi & 7
        gg = dh
        for d in (1, 2, 4):
            ok = r8 < 8 - d
            c_sh = jnp.where(ok, pltpu.roll(c, tm - d, 0), 1.0)
            g_sh = jnp.where(ok, pltpu.roll(gg, tm - d, 0), 0.0)
            gg = c * g_sh + gg
            c = c * c_sh
        c_s[...] = c
        g_s[...] = gg

        def step(k, gin):
            s = pl.multiple_of((t8 - 1 - k) * 8, 8)
            og = c_s[pl.ds(s, 8), :] * gin + g_s[pl.ds(s, 8), :]
            gout[pl.ds(s, 8), :] = og
            return jnp.broadcast_to(og[0:1, :], (8, C))

        gcarry[...] = lax.fori_loop(0, t8, step, gcarry[...], unroll=4)
        G = gout[...]
        hprev_row = jnp.where(j > 0, hp_ref[7:8, :], 0.0)
        hprev = jnp.where(rowi == 0, jnp.broadcast_to(hprev_row, (tm, C)), pltpu.roll(hh, 1, 0))
        da = G * hprev
        dm = G * ii * xc
        di = G * m * xc
        dxc = G * m * ii
        dla = da * a - dm * a * a / m
        dr = dla * (-LRU_C * sp)
        dsp = _colsum(dla * (-LRU_C * r))
        dlam = dsp * (-_sigmoid(-lam))
        dpr = dr * r * (1.0 - r)
        dpi = di * ii * (1.0 - ii)
        dxc = dxc + _mm_nt(dpr, wa_ref[...]) + _mm_nt(dpi, wx_ref[...])
        dwa_ref[...] += _mm_tn(xc, dpr)
        dwx_ref[...] += _mm_tn(xc, dpi)
        dvec_ref[0:1, :] += _colsum(dpr)
        dvec_ref[1:2, :] += _colsum(dpi)
        dvec_ref[2:3, :] += dlam
        dvec_ref[3:4, :] += _colsum(dxc)
        edge = ext[...]
        xr = xr_ref[...]
        dxr = cw_ref[3:4, :] * dxc
        dvec_ref[7:8, :] += _colsum(dxc * xr)
        for k in range(3):
            up = _shift_rows(dxc, k - 3, edge)
            dxr = dxr + cw_ref[k:k + 1, :] * up
            dvec_ref[4 + k:5 + k, :] += _colsum(up * xr)
        ext[...] = dxc[0:8, :]
        dxr_ref[...] = dxr.astype(BF16)

    rev = lambda i: nt - 1 - i
    prev8 = lambda i: jnp.maximum((nt - 1 - i) * t8 - 1, 0)
    vec = _full((1, C))
    return _launch(
        body, "rnn_bwd", (nt,),
        [_rows(tm, C, rev), _rows(tm, C, rev), _rows(tm, C, rev), _rows(8, C, prev8), _rows(tm, C, rev),
         _rows(tm, C, rev), _full((4, C)), _full((C, C)), vec, _full((C, C)), vec, vec],
        [_rows(tm, C, rev), _rows(tm, C, rev), _full((C, C)), _full((C, C)), _full((8, C))],
        [jax.ShapeDtypeStruct((T, C), BF16), jax.ShapeDtypeStruct((T, C), BF16),
         jax.ShapeDtypeStruct((C, C), F32), jax.ShapeDtypeStruct((C, C), F32), jax.ShapeDtypeStruct((8, C), F32)],
        [pltpu.VMEM((tm, C), F32), pltpu.VMEM((tm, C), F32), pltpu.VMEM((tm, C), F32),
         pltpu.VMEM((8, C), F32), pltpu.VMEM((8, C), F32), pltpu.VMEM((8, C), F32)],
        (drec, gr, h, h, xc, xr, cw, wa, ba, wx, bx, lam), exchange)


def _out_proj(att, rec, x, w_out, g1, b1):
    T = x.shape[0]
    tm = min(1024, T)

    def body(att_ref, rec_ref, x_ref, w_ref, g1_ref, b1_ref, z_ref, h_ref):
        mix = _mm(att_ref[...], w_ref[0:512, :]) + _mm(rec_ref[...], w_ref[512:1024, :])
        z1 = ALPHA * x_ref[...] + mix
        z_ref[...] = z1
        h1, _, _ = _ln(z1, g1_ref[...], b1_ref[...])
        h_ref[...] = h1.astype(MXU_DTYPE).astype(BF16)

    return pl.pallas_call(
        body, name="out_proj", grid=(T // tm,),
        in_specs=[_rows(tm, 512), _rows(tm, 512), _rows(tm, D), _full((D, D)), _full((1, D)), _full((1, D))],
        out_specs=[_rows(tm, D), _rows(tm, D)],
        out_shape=[jax.ShapeDtypeStruct((T, D), F32), jax.ShapeDtypeStruct((T, D), BF16)],
        compiler_params=_params(),
    )(att, rec, x, w_out, g1, b1)


NC = D_FF // FF_CHUNK


def _ffn_up(h1b, w_up_t, fcw, fcb, exchange=None):
    T = h1b.shape[0]
    tm = min(1024, T)
    CW = FF_CHUNK

    def body(h_ref, wg_ref, wv_ref, fcw_ref, fcb_ref, gate_ref, ge_ref, vd_ref, act_ref, before):
        i = pl.program_id(1)

        @pl.when(i == 0)
        def _():
            before[...] = jnp.zeros((8, CW), F32)

        hb = h_ref[...]
        gate = _mm_nt(hb, wg_ref[...])
        val = _mm_nt(hb, wv_ref[...])
        gate_ref[...] = gate.astype(BF16)
        edge = before[...]
        gc = (fcb_ref[...] + fcw_ref[0:1, :] * _shift_rows(gate, 2, edge) + fcw_ref[1:2, :] * _shift_rows(gate, 1, edge)
              + fcw_ref[2:3, :] * gate)
        before[...] = gate[tm - 8:tm, :]
        ge, dge = _gelu(gc)
        ge_ref[...] = ge.astype(BF16)
        vd_ref[...] = (val * dge).astype(BF16)
        act_ref[...] = (ge * val).astype(BF16)

    chunk = pl.BlockSpec((None, tm, CW), lambda c, i: (c, i, 0))
    return _launch(
        body, "ffn_up", (NC, T // tm),
        [pl.BlockSpec((tm, D), lambda c, i: (i, 0)), pl.BlockSpec((CW, D), lambda c, i: (c, 0)),
         pl.BlockSpec((CW, D), lambda c, i: (NC + c, 0)), pl.BlockSpec((None, 3, CW), lambda c, i: (c, 0, 0)),
         pl.BlockSpec((None, 1, CW), lambda c, i: (c, 0, 0))],
        [chunk] * 3 + [pl.BlockSpec((tm, CW), lambda c, i: (i, c))],
        [jax.ShapeDtypeStruct((NC, T, CW), BF16)] * 3 + [jax.ShapeDtypeStruct((T, D_FF), BF16)], [pltpu.VMEM((8, CW), F32)],
        (h1b, w_up_t, w_up_t, fcw, fcb), exchange)


def _ffn_down(act, z1, p, tgt, w_down, w_g, w_p_t, g1, b1, g2, b2, bg):
    T = z1.shape[0]
    tm = 512

    def body(act_ref, z_ref, p_ref, t_ref, wdn_hbm, wg_hbm, wp_hbm, g1_ref, b1_ref, g2_ref, b2_ref, bg_ref,
             dz2_ref, dz2b_ref, dpre_ref, dpp_ref, vec_ref, wdn, wg, wp, sems):
        @pl.when(pl.program_id(0) == 0)
        def _():
            copies = [pltpu.make_async_copy(src, dst, sems.at[n])
                      for n, (src, dst) in enumerate(((wdn_hbm, wdn), (wg_hbm, wg), (wp_hbm, wp)))]
            for cp in copies:
                cp.start()
            vec_ref[...] = jnp.zeros_like(vec_ref)
            for cp in copies:
                cp.wait()

        g2v = g2_ref[...]
        for r in (slice(0, tm // 2), slice(tm // 2, tm)):
            h1, _, _ = _ln(z_ref[r, :], g1_ref[...], b1_ref[...])
            h1b = h1.astype(MXU_DTYPE)
            ffn = _mm(act_ref[r, :], wdn[...])
            sg = _sigmoid(_mm(h1b, wg[...]) + bg_ref[...])
            pp = _mm_nt(p_ref[r, :], wp[...])
            z2 = ALPHA * h1 + ffn + sg * pp
            y, xh2, rstd2 = _ln(z2, g2v, b2_ref[...])
            diff = y - t_ref[r, :]
            dy = diff * (1.0 / D)
            dz2 = _ln_bwd(dy, xh2, rstd2, g2v)
            dpre = dz2 * pp * sg * (1.0 - sg)
            dz2_ref[r, :] = dz2
            dz2b_ref[r, :] = dz2.astype(BF16)
            dpre_ref[r, :] = dpre.astype(BF16)
            dpp_ref[r, :] = (dz2 * sg).astype(BF16)
            loss = 0.5 * jnp.sum(jnp.sum(diff * diff, axis=1, keepdims=True), axis=0, keepdims=True) * (1.0 / D)
            vec_ref[0:1, :] += jnp.broadcast_to(loss, (1, D))
            vec_ref[1:2, :] += _colsum(dy * xh2)
            vec_ref[2:3, :] += _colsum(dy)
            vec_ref[3:4, :] += _colsum(dpre)

    anyspec = pl.BlockSpec(memory_space=pl.ANY)
    vec = _full((1, D))
    return pl.pallas_call(
        body, name="ffn_down", grid=(T // tm,),
        in_specs=[_rows(tm, D_FF), _rows(tm, D), _rows(tm, PLE), _rows(tm, D),
                  anyspec, anyspec, anyspec] + [vec] * 5,
        out_specs=[_rows(tm, D)] * 4 + [_full((8, D))],
        out_shape=[jax.ShapeDtypeStruct((T, D), F32)] + [jax.ShapeDtypeStruct((T, D), BF16)] * 3
                  + [jax.ShapeDtypeStruct((8, D), F32)],
        scratch_shapes=[pltpu.VMEM((D_FF, D), MXU_DTYPE), pltpu.VMEM((D, D), MXU_DTYPE), pltpu.VMEM((D, PLE), MXU_DTYPE),
                        pltpu.SemaphoreType.DMA((3,))],
        compiler_params=_params(),
    )(act, z1, p, tgt, w_down, w_g, w_p_t, g1, b1, g2, b2, bg)


def _ffn_bwd(dz2b, gate, ge, vd, w_down, fcw):
    T = dz2b.shape[0]
    tm = min(1024, T)
    CW = FF_CHUNK
    nt = T // tm

    def body(dz_ref, wdn_ref, gate_ref, ge_ref, vd_ref, fcw_ref, dup_ref, dfc_ref, after):
        i = pl.program_id(1)

        @pl.when(i == 0)
        def _():
            after[...] = jnp.zeros((8, CW), F32)
            dfc_ref[...] = jnp.zeros_like(dfc_ref)

        gate = gate_ref[...].astype(F32)
        dact = _mm_nt(dz_ref[...], wdn_ref[...])
        dgc = dact * vd_ref[...].astype(F32)
        edge = after[...]
        dgc1 = _shift_rows(dgc, -1, edge)
        dgc2 = _shift_rows(dgc, -2, edge)
        after[...] = dgc[0:8, :]
        dup_ref[:, 0:CW] = (fcw_ref[2:3, :] * dgc + fcw_ref[1:2, :] * dgc1 + fcw_ref[0:1, :] * dgc2).astype(BF16)
        dup_ref[:, CW:2 * CW] = (dact * ge_ref[...].astype(F32)).astype(BF16)
        dfc_ref[0:1, :] += _colsum(dgc2 * gate)
        dfc_ref[1:2, :] += _colsum(dgc1 * gate)
        dfc_ref[2:3, :] += _colsum(dgc * gate)
        dfc_ref[3:4, :] += _colsum(dgc)

    rev = lambda c, i: (c, nt - 1 - i, 0)
    chunk = pl.BlockSpec((None, tm, CW), rev)
    return pl.pallas_call(
        body, name="ffn_bwd", grid=(NC, nt),
        in_specs=[pl.BlockSpec((tm, D), lambda c, i: (nt - 1 - i, 0)), pl.BlockSpec((CW, D), lambda c, i: (c, 0)),
                  chunk, chunk, chunk, pl.BlockSpec((None, 3, CW), lambda c, i: (c, 0, 0))],
        out_specs=[pl.BlockSpec((tm, 2 * CW), lambda c, i: (nt - 1 - i, c)),
                   pl.BlockSpec((None, 8, CW), lambda c, i: (c, 0, 0))],
        out_shape=[jax.ShapeDtypeStruct((T, 2 * D_FF), BF16), jax.ShapeDtypeStruct((NC, 8, CW), F32)],
        scratch_shapes=[pltpu.VMEM((8, CW), F32)],
        compiler_params=_params(),
    )(dz2b, w_down, gate, ge, vd, fcw)


def _ffn_dh1(dup, dz2, dpre, z1, w_up_t, w_g, g1, b1):
    T = z1.shape[0]
    tm = 512

    def body(dup_ref, dz2_ref, dpre_ref, z_ref, wup_hbm, wg_hbm, g1_ref, b1_ref, dz1_ref, vec_ref, wup, wg, sems):
        @pl.when(pl.program_id(0) == 0)
        def _():
            copies = [pltpu.make_async_copy(wup_hbm.at[pl.ds(s * D_FF + c * FF_CHUNK, FF_CHUNK)],
                                            wup.at[pl.ds((2 * c + s) * FF_CHUNK, FF_CHUNK)], sems.at[2 * c + s])
                      for c in range(NC) for s in range(2)]
            copies.append(pltpu.make_async_copy(wg_hbm, wg, sems.at[2 * NC]))
            for cp in copies:
                cp.start()
            vec_ref[...] = jnp.zeros_like(vec_ref)
            for cp in copies:
                cp.wait()

        g1v = g1_ref[...]
        _, xh1, rstd1 = _ln(z_ref[...], g1v, b1_ref[...])
        dh1 = ALPHA * dz2_ref[...] + _mm_nt(dpre_ref[...], wg[...]) + _mm(dup_ref[...], wup[...])
        dz1_ref[...] = _ln_bwd(dh1, xh1, rstd1, g1v)
        vec_ref[0:1, :] += _colsum(dh1 * xh1)
        vec_ref[1:2, :] += _colsum(dh1)

    anyspec = pl.BlockSpec(memory_space=pl.ANY)
    vec = _full((1, D))
    return pl.pallas_call(
        body, name="ffn_dh1", grid=(T // tm,),
        in_specs=[_rows(tm, 2 * D_FF), _rows(tm, D), _rows(tm, D), _rows(tm, D),
                  anyspec, anyspec, vec, vec],
        out_specs=[_rows(tm, D), _full((8, D))],
        out_shape=[jax.ShapeDtypeStruct((T, D), F32), jax.ShapeDtypeStruct((8, D), F32)],
        scratch_shapes=[pltpu.VMEM((2 * D_FF, D), MXU_DTYPE), pltpu.VMEM((D, D), MXU_DTYPE),
                        pltpu.SemaphoreType.DMA((2 * NC + 1,))],
        compiler_params=_params(),
    )(dup, dz2, dpre, z1, w_up_t, w_g, g1, b1)


def _out_proj_bwd(dz1, w_out, exchange=None):
    T = dz1.shape[0]
    tm = min(1024, T)

    def body(dz_ref, w_ref, datt_ref, drec_ref):
        dzb = dz_ref[...].astype(MXU_DTYPE)
        datt = _mm_nt(dzb, w_ref[0:512, :])
        for h in range(HEADS):
            datt_ref[h] = datt[:, h * 64:(h + 1) * 64].astype(BF16)
        drec_ref[...] = _mm_nt(dzb, w_ref[512:1024, :])

    return _launch(body, "out_proj_bwd", (T // tm,), [_rows(tm, D), _full((D, D))], [_heads(tm), _rows(tm, 512)],
                   [jax.ShapeDtypeStruct((HEADS, T, 64), BF16), jax.ShapeDtypeStruct((T, 512), F32)], [],
                   (dz1, w_out), exchange)


def _in_proj_bwd(dq, dkv, dxr, dgr, dz1, w_in_t, exchange=None):
    T = dz1.shape[0]
    tm = 512
    W = D_IN // 4

    def body(dq_ref, dkv_ref, dxr_ref, dgr_ref, dz_ref, w_ref, dx_ref, du_ref):
        dkv = dkv_ref[...]
        dx_ref[...] = (ALPHA * dz_ref[...] + _mm(dq_ref[...], w_ref[0:512, :]) + _mm(dkv, w_ref[512:768, :])
                       + _mm(dxr_ref[...], w_ref[768:1280, :]) + _mm(dgr_ref[...], w_ref[1280:1792, :]))
        dq, dxr, dgr = dq_ref[...].astype(F32), dxr_ref[...].astype(F32), dgr_ref[...].astype(F32)
        du_ref[0] = dq[:, 0:W].astype(BF16)
        du_ref[1, :, 0:64] = dq[:, W:512].astype(BF16)
        du_ref[1, :, 64:320] = dkv.astype(BF16)
        du_ref[1, :, 320:W] = dxr[:, 0:128].astype(BF16)
        du_ref[2, :, 0:384] = dxr[:, 128:512].astype(BF16)
        du_ref[2, :, 384:W] = dgr[:, 0:64].astype(BF16)
        du_ref[3] = dgr[:, 64:512].astype(BF16)

    return _launch(body, "in_proj_bwd", (T // tm,),
                   [_rows(tm, 512), _rows(tm, 256), _rows(tm, 512), _rows(tm, 512), _rows(tm, D), _full((D_IN, D))],
                   [_rows(tm, D), pl.BlockSpec((4, tm, W), lambda i: (0, i, 0))],
                   [jax.ShapeDtypeStruct((T, D), F32), jax.ShapeDtypeStruct((4, T, W), BF16)], [],
                   (dq, dkv, dxr, dgr, dz1, w_in_t), exchange)


def _accumulate_tn(a_ref, b_ref, o_ref):
    @pl.when(pl.program_id(1) == 0)
    def _():
        o_ref[...] = jnp.zeros_like(o_ref)

    o_ref[...] += _mm_tn(a_ref[...], b_ref[...])


def _weight_grad_cols(a, b, name, n_blocks, b_spec, out_shape, out_spec, exchange=None):
    T, M = a.shape
    bt = min(DW_TOKENS, T)
    return _launch(functools.partial(_accumulate_tn), name, (n_blocks, T // bt),
                   [pl.BlockSpec((bt, M), lambda m, k: (k, 0)), b_spec(bt)], [out_spec],
                   [jax.ShapeDtypeStruct(out_shape, F32)], [], (a, b), exchange)


def _dw_out(att, rec, dz1):
    T = dz1.shape[0]
    bt = min(DW_TOKENS // 2, T)

    def body(att_ref, rec_ref, dz_ref, o_ref):
        @pl.when(pl.program_id(0) == 0)
        def _():
            o_ref[...] = jnp.zeros_like(o_ref)

        dz = dz_ref[...].astype(MXU_DTYPE)
        o_ref[0:512, :] += _mm_tn(att_ref[...], dz)
        o_ref[512:1024, :] += _mm_tn(rec_ref[...], dz)

    return pl.pallas_call(
        body, name="dw_out", grid=(T // bt,), in_specs=[_rows(bt, 512), _rows(bt, 512), _rows(bt, D)],
        out_specs=_full((D, D)), out_shape=jax.ShapeDtypeStruct((D, D), F32), compiler_params=_params())(att, rec, dz1)


def _weight_grad(a, b, bm, name, exchange=None):
    bt = min(DW_TOKENS // 2 if b.dtype == F32 else DW_TOKENS, b.shape[0])
    if a.ndim == 3:
        assert a.shape[2] == bm
        T, M = a.shape[1], a.shape[0] * bm
        a_spec = pl.BlockSpec((None, bt, bm), lambda m, k: (m, k, 0))
    else:
        T, M = a.shape
        a_spec = pl.BlockSpec((bt, bm), lambda m, k: (k, m))
    N = b.shape[1]
    nk = T // bt

    out = _launch(functools.partial(_accumulate_tn), name, (M // bm, nk),
                  [a_spec, pl.BlockSpec((bt, N), lambda m, k: (k, 0))], [pl.BlockSpec((bm, N), lambda m, k: (m, 0))],
                  [jax.ShapeDtypeStruct((M, N), F32)], [], (a, b), exchange)
    return out[0] if exchange is None else out


def _adamw(w, g, m, v, name):
    R, C = w.shape
    tr = R // 8 if R % 64 == 0 else R
    c1 = 1.0 / (1.0 - ADAM_B1 ** ADAM_STEP)
    c2 = 1.0 / (1.0 - ADAM_B2 ** ADAM_STEP)

    def body(w_ref, g_ref, m_ref, v_ref, d_ref, nm_ref, nv_ref):
        g = g_ref[...]
        nm = ADAM_B1 * m_ref[...] + (1.0 - ADAM_B1) * g
        nv = ADAM_B2 * v_ref[...] + (1.0 - ADAM_B2) * g * g
        nm_ref[...] = nm
        nv_ref[...] = nv
        d_ref[...] = -ADAM_LR * ((nm * c1) / (jnp.sqrt(nv * c2) + ADAM_EPS) + ADAM_WD * w_ref[...])

    spec = pl.BlockSpec((tr, C), lambda i: (i, 0))
    return pl.pallas_call(
        body, name=name, grid=(R // tr,),
        in_specs=[spec] * 4, out_specs=[spec] * 3,
        out_shape=[jax.ShapeDtypeStruct((R, C), F32)] * 3,
        compiler_params=_params(),
    )(w, g, m, v)


def _adamw_halves(ws, mines, sibs, ms, vs, c, name, exchange=None):
    n, nb = len(ws), 4
    c1 = 1.0 / (1.0 - ADAM_B1 ** ADAM_STEP)
    c2 = 1.0 / (1.0 - ADAM_B2 ** ADAM_STEP)

    def body(c_ref, *refs):
        own = (pl.program_id(0) // nb) == c_ref[0]
        for i in range(n):
            w_ref, a_ref, b_ref, m_ref, v_ref = refs[5 * i:5 * i + 5]
            g_ref, d_ref, nm_ref, nv_ref = refs[5 * n + 4 * i:5 * n + 4 * i + 4]
            g = jnp.where(own, a_ref[...], b_ref[...])
            nm = ADAM_B1 * m_ref[...] + (1.0 - ADAM_B1) * g
            nv = ADAM_B2 * v_ref[...] + (1.0 - ADAM_B2) * g * g
            g_ref[...] = g
            nm_ref[...] = nm
            nv_ref[...] = nv
            d_ref[...] = -ADAM_LR * ((nm * c1) / (jnp.sqrt(nv * c2) + ADAM_EPS) + ADAM_WD * w_ref[...])

    in_specs, out_specs, out_shape, args = [], [], [], []
    for w, a, b, m, v in zip(ws, mines, sibs, ms, vs):
        R, C = w.shape
        tr = R // (2 * nb)
        assert tr % 8 == 0 and a.shape == (R // 2, C)
        full = pl.BlockSpec((tr, C), lambda i, c_ref: (i, 0))
        mine_spec = pl.BlockSpec((tr, C), lambda i, c_ref: (jnp.where(i // nb == c_ref[0], i % nb, nb - 1), 0))
        sib_spec = pl.BlockSpec((tr, C), lambda i, c_ref: (jnp.where(i // nb == c_ref[0], nb - 1, i % nb), 0))
        in_specs += [full, mine_spec, sib_spec, full, full]
        out_specs += [full] * 4
        out_shape += [jax.ShapeDtypeStruct((R, C), F32)] * 4
        args += [w, a, b, m, v]
    out = _launch(body, name, (2 * nb,), in_specs, out_specs, out_shape, [], (c, *args), exchange, prefetch=1)
    return [tuple(out[4 * i:4 * i + 4]) for i in range(n)], list(out[4 * n:])


def _add4(fs, name):
    n = len(fs)

    def body(*refs):
        for a_ref, o_ref in zip(refs[:n], refs[n:]):
            o_ref[...] = ((a_ref[0].astype(F32) + a_ref[1].astype(F32)) + a_ref[2].astype(F32)) + a_ref[3].astype(F32)

    for f in fs:
        assert (f.shape[1] // 2) % 16 == 0
    return pl.pallas_call(
        body, name=name, grid=(2,),
        in_specs=[pl.BlockSpec((4, f.shape[1] // 2, f.shape[2]), lambda i: (0, i, 0)) for f in fs],
        out_specs=[pl.BlockSpec((f.shape[1] // 2, f.shape[2]), lambda i: (i, 0)) for f in fs],
        out_shape=[jax.ShapeDtypeStruct(f.shape[1:], F32) for f in fs], compiler_params=_params())(*fs)


def _gather_first(wsrc, cpack):
    def body(w_ref, c_ref, gw_ref, gc_ref, send_sems, recv_sems, local_sem, csend, crecv, clocal):
        x, y, c = _pos()
        me = 2 * x + y
        chips = _other_chips(x, y)
        start, forward, finish = _gather_steps(w_ref, gw_ref, send_sems, recv_sems, local_sem)
        start()
        loc = pltpu.make_async_copy(c_ref, gc_ref.at[me], clocal)
        loc.start()

        def conv_copy(k, slot):
            px, py = chips[k]
            return pltpu.make_async_remote_copy(src_ref=c_ref, dst_ref=gc_ref.at[slot], send_sem=csend.at[k],
                                                recv_sem=crecv.at[k], device_id=(px, py, c), device_id_type=MESH)

        for k in range(3):
            conv_copy(k, me).start()
        forward()
        finish()
        for k, (px, py) in enumerate(chips):
            conv_copy(k, 2 * px + py).wait_recv()
        for k in range(3):
            conv_copy(k, me).wait_send()
        loc.wait()

    anyspec = pl.BlockSpec(memory_space=pl.ANY)
    return pl.pallas_call(
        body, name="gather_first",
        in_specs=[anyspec, anyspec], out_specs=[anyspec, anyspec],
        out_shape=[jax.ShapeDtypeStruct((4,) + wsrc.shape, wsrc.dtype), jax.ShapeDtypeStruct((4,) + cpack.shape, cpack.dtype)],
        scratch_shapes=GATHER_SCRATCH + [pltpu.SemaphoreType.DMA((3,)), pltpu.SemaphoreType.DMA((3,)), pltpu.SemaphoreType.DMA],
        compiler_params=_params(has_side_effects=True),
    )(wsrc, cpack)


def _all_devices_exchange(s):
    def make(ins, outs, sems):
        s_ref, o_ref = ins[0], outs[0]
        send_sems, recv_sems, local_sem = sems
        x, y, c = _pos()
        me = 4 * x + 2 * y + c
        loc = pltpu.make_async_copy(s_ref, o_ref.at[me], local_sem)

        def copy(k, slot):
            peer = (x ^ (k >> 2), y ^ ((k >> 1) & 1), c ^ (k & 1))
            return pltpu.make_async_remote_copy(src_ref=s_ref, dst_ref=o_ref.at[slot], send_sem=send_sems.at[k - 1],
                                                recv_sem=recv_sems.at[k - 1], device_id=peer, device_id_type=MESH)

        def start():
            loc.start()
            for k in range(1, 8):
                copy(k, me).start()

        def finish():
            for k in range(1, 8):
                copy(k, 4 * (x ^ (k >> 2)) + 2 * (y ^ ((k >> 1) & 1)) + (c ^ (k & 1))).wait_recv()
            for k in range(1, 8):
                copy(k, me).wait_send()
            loc.wait()

        return start, lambda: None, finish

    return _Exchange([s], [jax.ShapeDtypeStruct((8,) + s.shape, s.dtype)],
                     [pltpu.SemaphoreType.DMA((7,)), pltpu.SemaphoreType.DMA((7,)), pltpu.SemaphoreType.DMA], make)


def _sum_devices(a):
    def body(a_ref, o_ref):
        acc = a_ref[0]
        for d in range(1, 8):
            acc = acc + a_ref[d]
        o_ref[...] = acc

    vm = pl.BlockSpec(memory_space=pltpu.VMEM)
    return pl.pallas_call(body, name="sum_devices", in_specs=[vm], out_specs=vm,
                          out_shape=jax.ShapeDtypeStruct(a.shape[1:], F32), compiler_params=_params())(a)


def _swap_exchange(gs):
    n = len(gs)

    def make(ins, outs, sems):
        x, y, c = _pos()
        cps = []
        for i in range(n):
            half = gs[i].shape[1] // 2
            rows = pl.ds(pl.multiple_of((1 - c) * half, 8), half)
            cps.append(pltpu.make_async_remote_copy(src_ref=ins[i].at[:, rows, :], dst_ref=outs[i], send_sem=sems[0].at[i],
                                                    recv_sem=sems[1].at[i], device_id=(x, y, 1 - c), device_id_type=MESH))

        def start():
            for cp in cps:
                cp.start()

        def finish():
            for cp in cps:
                cp.wait()

        return start, lambda: None, finish

    return _Exchange(gs, [jax.ShapeDtypeStruct((4, g.shape[1] // 2, g.shape[2]), g.dtype) for g in gs],
                     [pltpu.SemaphoreType.DMA((n,)), pltpu.SemaphoreType.DMA((n,))], make)


def _scatter_exchange(ss):
    n = len(ss)

    def make(ins, outs, sems):
        send_sems, recv_sems, local_sems = sems
        x, y, c = _pos()
        me = 2 * x + y
        chips = _other_chips(x, y)
        locs = [pltpu.make_async_copy(ins[i].at[me], outs[i].at[me], local_sems.at[i]) for i in range(n)]

        def copy(i, k, src_slot, dst_slot):
            px, py = chips[k]
            return pltpu.make_async_remote_copy(src_ref=ins[i].at[src_slot], dst_ref=outs[i].at[dst_slot],
                                                send_sem=send_sems.at[3 * i + k], recv_sem=recv_sems.at[3 * i + k],
                                                device_id=(px, py, c), device_id_type=MESH)

        def start():
            for i in range(n):
                locs[i].start()
                for k, (px, py) in enumerate(chips):
                    copy(i, k, 2 * px + py, me).start()

        def finish():
            for i in range(n):
                for k, (px, py) in enumerate(chips):
                    copy(i, k, me, 2 * px + py).wait_recv()
            for i in range(n):
                for k, (px, py) in enumerate(chips):
                    copy(i, k, 2 * px + py, me).wait_send()
                locs[i].wait()

        return start, lambda: None, finish

    return _Exchange(ss, [jax.ShapeDtypeStruct(s.shape, s.dtype) for s in ss],
                     [pltpu.SemaphoreType.DMA((3 * n,)), pltpu.SemaphoreType.DMA((3 * n,)), pltpu.SemaphoreType.DMA((n,))], make)


def _send_exchange(rs):
    n = len(rs)

    def make(ins, outs, sems):
        x, y, c = _pos()
        cps = [pltpu.make_async_remote_copy(src_ref=ins[i], dst_ref=outs[i], send_sem=sems[0].at[i], recv_sem=sems[1].at[i],
                                            device_id=(x, y, 1 - c), device_id_type=MESH) for i in range(n)]

        def start():
            for cp in cps:
                cp.start()

        def finish():
            for cp in cps:
                cp.wait()

        return start, lambda: None, finish

    return _Exchange(rs, [jax.ShapeDtypeStruct(r.shape, r.dtype) for r in rs],
                     [pltpu.SemaphoreType.DMA((n,)), pltpu.SemaphoreType.DMA((n,))], make)


def _reduce_in_vmem(g):
    _, R, C = g.shape
    H = R // 2

    def body(g_ref, mine_ref, other_ref, sib, part, got, swap_sems, send_sems, recv_sems, last_sems):
        x, y, c = _pos()
        me = 2 * x + y
        chips = _other_chips(x, y)
        sibling = (x, y, 1 - c)
        mine = pl.ds(pl.multiple_of(c * H, 8), H)
        theirs = pl.ds(pl.multiple_of((1 - c) * H, 8), H)
        swap = pltpu.make_async_remote_copy(src_ref=g_ref.at[:, theirs, :], dst_ref=sib, send_sem=swap_sems.at[0],
                                            recv_sem=swap_sems.at[1], device_id=sibling, device_id_type=MESH)
        swap.start()
        swap.wait()
        part[...] = (g_ref[:, mine, :] + sib[...]).astype(BF16)

        def copy(k, src_slot, dst_slot):
            px, py = chips[k]
            return pltpu.make_async_remote_copy(src_ref=part.at[src_slot], dst_ref=got.at[dst_slot], send_sem=send_sems.at[k],
                                                recv_sem=recv_sems.at[k], device_id=(px, py, c), device_id_type=MESH)

        for k, (px, py) in enumerate(chips):
            copy(k, 2 * px + py, me).start()
        got[me] = part[me]
        for k, (px, py) in enumerate(chips):
            copy(k, me, 2 * px + py).wait_recv()
        for k, (px, py) in enumerate(chips):
            copy(k, 2 * px + py, me).wait_send()
        mine_ref[...] = ((got[0].astype(F32) + got[1].astype(F32)) + got[2].astype(F32)) + got[3].astype(F32)
        last = pltpu.make_async_remote_copy(src_ref=mine_ref, dst_ref=other_ref, send_sem=last_sems.at[0],
                                            recv_sem=last_sems.at[1], device_id=sibling, device_id_type=MESH)
        last.start()
        last.wait()

    vm = pl.BlockSpec(memory_space=pltpu.VMEM)
    half = jax.ShapeDtypeStruct((H, C), F32)
    return pl.pallas_call(
        body, name="reduce_late", in_specs=[vm], out_specs=[vm, vm], out_shape=[half, half],
        scratch_shapes=[pltpu.VMEM((4, H, C), F32), pltpu.VMEM((4, H, C), BF16), pltpu.VMEM((4, H, C), BF16),
                        pltpu.SemaphoreType.DMA((2,)), pltpu.SemaphoreType.DMA((3,)), pltpu.SemaphoreType.DMA((3,)),
                        pltpu.SemaphoreType.DMA((2,))],
        compiler_params=_params(has_side_effects=True))(g)


def _add_half(gs, rs, c, name):
    n = len(gs)

    def body(c_ref, *refs):
        for g_ref, r_ref, o_ref in zip(refs[:n], refs[n:2 * n], refs[2 * n:]):
            o_ref[...] = (g_ref[...] + r_ref[...]).astype(BF16)

    g_specs, r_specs, out_shape = [], [], []
    for g, r in zip(gs, rs):
        _, H, C = r.shape
        tr = H // 2
        assert tr % 16 == 0 and g.shape == (4, 2 * H, C)
        g_specs.append(pl.BlockSpec((1, tr, C), lambda j, i, c_ref: (j, c_ref[0] * 2 + i, 0)))
        r_specs.append(pl.BlockSpec((1, tr, C), lambda j, i, c_ref: (j, i, 0)))
        out_shape.append(jax.ShapeDtypeStruct((4, H, C), BF16))
    grid_spec = pltpu.PrefetchScalarGridSpec(num_scalar_prefetch=1, grid=(4, 2), in_specs=g_specs + r_specs, out_specs=r_specs)
    return pl.pallas_call(body, name=name, grid_spec=grid_spec, out_shape=out_shape, compiler_params=_params())(c, *gs, *rs)


def _block_diag(w):
    eye = jnp.eye(RNN_BLOCKS, dtype=w.dtype)
    return (eye[:, None, :, None] * w[:, :, None, :]).reshape(D_RNN, D_RNN)


def _diag_blocks(wd):
    d = wd.reshape(RNN_BLOCKS, 64, RNN_BLOCKS, 64)
    return jnp.stack([d[h, :, h, :] for h in range(RNN_BLOCKS)])


def _split_pack(a, first, last):
    out, base = {}, PACK_OFF[first]
    for i in range(first, last):
        s = a[:, PACK_OFF[i] - base:PACK_OFF[i + 1] - base]
        out[BIG_KEYS[i]] = s.reshape(4 * 256, 256) if BIG_KEYS[i] == "w_p_t" else s.reshape(-1, 1024)
    return out


def _layer_grads(x, p, tgt, gw, small, shard=None, core=None):
    row = lambda v: v.reshape(1, -1)
    wa = _block_diag(small["gate_a_w"]).astype(MXU_DTYPE)
    wx = _block_diag(small["gate_x_w"]).astype(MXU_DTYPE)
    sinks = small["attn_sinks"].reshape(1, HEADS)

    dist = shard is not None
    cut = PACK_OFF[1] + PACK_ROWS[1] // 2
    q, kv, xr, gr, xb, *ga = _in_proj(x, gw["w_in_t"], _gather_exchange(shard[PACK_OFF[1]:cut]) if dist else None)
    att, *gb = _attn_fwd(q, kv, sinks, _gather_exchange(shard[cut:PACK_OFF[3]]) if dist else None)
    xc, h, rec = _rnn_fwd(xr, gr, small["rnn_conv_w"], row(small["rnn_conv_b"]), wa, row(small["gate_a_b"]),
                          wx, row(small["gate_x_b"]), row(small["lru_lambda"]))
    if dist:
        gw = {**gw, **_split_pack(jnp.concatenate([ga[0], gb[0]], axis=1), 1, 3)}
    g1, b1 = row(small["ln1_g"]), row(small["ln1_b"])
    fcw = small["ffn_conv_w"].reshape(3, NC, FF_CHUNK).transpose(1, 0, 2)
    fcb = small["ffn_conv_b"].reshape(NC, 1, FF_CHUNK)
    z1, h1b = _out_proj(att, rec, x, gw["w_out"], g1, b1)
    gate, ge, vd, act, *gc = _ffn_up(h1b, gw["w_up_t"], fcw, fcb,
                                     _gather_exchange(shard[PACK_OFF[3]:PACK_OFF[6]]) if dist else None)
    if dist:
        gw = {**gw, **_split_pack(gc[0], 3, 6)}
    dz2, dz2b, dpre, dpp, vec2 = _ffn_down(act, z1, p, tgt, gw["w_down"], gw["w_g"], gw["w_p_t"], g1, b1,
                                           row(small["ln2_g"]), row(small["ln2_b"]), row(small["ple_gate_b"]))
    dup, dfc = _ffn_bwd(dz2b, gate, ge, vd, gw["w_down"], fcw)
    dz1, vec1 = _ffn_dh1(dup, dz2, dpre, z1, gw["w_up_t"], gw["w_g"], g1, b1)
    per_chip = 2 * D_FF // 4 // FF_CHUNK
    big = {"w_ffn_up": _weight_grad_cols(
        h1b, dup, "dw_up", 2 * NC, lambda bt: pl.BlockSpec((bt, FF_CHUNK), lambda m, k: (k, m)), (4, D, 2 * D_FF // 4),
        pl.BlockSpec((None, D, FF_CHUNK), lambda m, k: (2 * (m % 2) + (m // 2) // per_chip, 0, (m // 2) % per_chip)))[0]}
    g_dn, *got_up = _weight_grad(act, dz2b, FF_CHUNK, "dw_down", _swap_exchange([big["w_ffn_up"]])) if dist else (
        _weight_grad(act, dz2b, FF_CHUNK, "dw_down"),)
    big["w_ffn_down"] = g_dn.reshape(4, D_FF // 4, D)
    big["ple_gate_w"] = _weight_grad(h1b, dpre, 512, "dw_gate").reshape(4, D // 4, D)
    big["ple_proj"] = _weight_grad(p, dpp, PLE, "dw_proj").reshape(PLE, 4, D // 4).transpose(1, 0, 2)
    big["w_out"] = _dw_out(att, rec, dz1).reshape(4, D // 4, D)
    reduced = None
    if dist:
        g_ffn = [big[k] for k in EARLY_WEIGHTS]
        ex = _swap_exchange(g_ffn[1:])
    datt, drec, *got = _out_proj_bwd(dz1, gw["w_out"], ex if dist else None)
    if dist:
        sums = _add_half(g_ffn, got_up + got, core, "add_half_ffn")
        ex, ex2 = _scatter_exchange(sums[:1]), _scatter_exchange(sums[1:])
    dxr, dgr, dwa, dwx, dvec, *got = _rnn_bwd(drec, gr, h, xc, xr, small["rnn_conv_w"], wa, row(small["gate_a_b"]),
                                              wx, row(small["gate_x_b"]), row(small["lru_lambda"]), ex if dist else None)
    dq, dkv, dsinks, *got2 = _attn_bwd(q, kv, datt, sinks, ex2 if dist else None)
    if dist:
        mine = _add4(got + got2, "add_chips_ffn")
        big = {}
    sg = {
        "attn_sinks": dsinks[:, 0],
        "rnn_conv_w": dvec[4:8],
        "rnn_conv_b": dvec[3],
        "gate_a_w": _diag_blocks(dwa),
        "gate_a_b": dvec[0],
        "gate_x_w": _diag_blocks(dwx),
        "gate_x_b": dvec[1],
        "lru_lambda": dvec[2],
        "ln1_g": vec1[0],
        "ln1_b": vec1[1],
        "ffn_conv_w": dfc[:, 0:3].transpose(1, 0, 2).reshape(3, D_FF),
        "ffn_conv_b": dfc[:, 3].reshape(D_FF),
        "ple_gate_b": vec2[3],
        "ln2_g": vec2[1],
        "ln2_b": vec2[2],
    }
    loss = vec2[0, 0:1]
    grad_x, du = _in_proj_bwd(dq, dkv, dxr, dgr, dz1, gw["w_in_t"])
    ex = None
    if dist:
        ex = _join_exchanges(_send_exchange(mine), _all_devices_exchange(_pack_vecs([sg[k] for k in SMALL] + [loss])[0]))
    big["w_in"], *got = _weight_grad_cols(
        xb, du, "dw_in", 4, lambda bt: pl.BlockSpec((None, bt, D_IN // 4), lambda j, k: (j, k, 0)), (4, D, D_IN // 4),
        pl.BlockSpec((None, D, D_IN // 4), lambda j, k: (j, 0, 0)), ex)
    if dist:
        reduced = (mine, got[:len(mine)])
    return grad_x, big, sg, loss, reduced, got[-1:]


BIG = ("w_in", "w_ffn_up", "w_out", "w_ffn_down", "ple_gate_w", "ple_proj")
BIG_KEYS = ("w_in_t", "w_up_t", "w_out", "w_down", "w_g", "w_p_t")
BIG_T = (True, True, False, False, False, True)
EARLY_WEIGHTS = ("w_ffn_up", "w_ffn_down", "ple_gate_w", "ple_proj", "w_out")
LATE_WEIGHTS = ("w_in",)
SMALL = ("attn_sinks", "rnn_conv_w", "rnn_conv_b", "gate_a_w", "gate_a_b", "gate_x_w", "gate_x_b", "lru_lambda",
         "ln1_g", "ln1_b", "ffn_conv_w", "ffn_conv_b", "ple_gate_b", "ln2_g", "ln2_b")
WEIGHTS = ("w_in", "attn_sinks", "rnn_conv_w", "rnn_conv_b", "gate_a_w", "gate_a_b", "gate_x_w", "gate_x_b",
           "lru_lambda", "w_out", "ln1_g", "ln1_b", "w_ffn_up", "ffn_conv_w", "ffn_conv_b", "w_ffn_down",
           "ple_gate_w", "ple_gate_b", "ple_proj", "ln2_g", "ln2_b")


def _pack_big(d, first=0, last=6):
    parts = []
    for name, t in zip(BIG[first:last], BIG_T[first:last]):
        a = d[name]
        a = a.T if t else a
        parts.append(a.reshape(-1, 1024))
    return jnp.concatenate(parts, axis=0)


def _pack_vecs(items):
    parts, offs, n = [], [], 0
    for a in items:
        f = a.reshape(-1).astype(F32)
        pad = (-f.shape[0]) % 128
        parts.append(jnp.pad(f, (0, pad)))
        offs.append(n)
        n += (f.shape[0] + pad) // 128
    padr = (-n) % 8
    if padr:
        parts.append(jnp.zeros((padr * 128,), F32))
    return jnp.concatenate(parts).reshape(-1, 128), offs


def _unpack_vecs(a, offs, shapes):
    flat = a.reshape(-1)
    out = []
    for o, s in zip(offs, shapes):
        n = 1
        for d in s:
            n *= d
        out.append(flat[o * 128:o * 128 + n].reshape(s))
    return out


def kernel(x, p, w_in, attn_sinks, rnn_conv_w, rnn_conv_b, gate_a_w, gate_a_b, gate_x_w, gate_x_b, lru_lambda, w_out, ln1_g, ln1_b, w_ffn_up, ffn_conv_w, ffn_conv_b, w_ffn_down, ple_gate_w, ple_gate_b, ple_proj, ln2_g, ln2_b, loss_target, m_w_in, m_attn_sinks, m_rnn_conv_w, m_rnn_conv_b, m_gate_a_w, m_gate_a_b, m_gate_x_w, m_gate_x_b, m_lru_lambda, m_w_out, m_ln1_g, m_ln1_b, m_w_ffn_up, m_ffn_conv_w, m_ffn_conv_b, m_w_ffn_down, m_ple_gate_w, m_ple_gate_b, m_ple_proj, m_ln2_g, m_ln2_b, v_w_in, v_attn_sinks, v_rnn_conv_w, v_rnn_conv_b, v_gate_a_w, v_gate_a_b, v_gate_x_w, v_gate_x_b, v_lru_lambda, v_w_out, v_ln1_g, v_ln1_b, v_w_ffn_up, v_ffn_conv_w, v_ffn_conv_b, v_w_ffn_down, v_ple_gate_w, v_ple_gate_b, v_ple_proj, v_ln2_g, v_ln2_b):
    w = dict(w_in=w_in, attn_sinks=attn_sinks, rnn_conv_w=rnn_conv_w, rnn_conv_b=rnn_conv_b, gate_a_w=gate_a_w,
             gate_a_b=gate_a_b, gate_x_w=gate_x_w, gate_x_b=gate_x_b, lru_lambda=lru_lambda, w_out=w_out, ln1_g=ln1_g,
             ln1_b=ln1_b, w_ffn_up=w_ffn_up, ffn_conv_w=ffn_conv_w, ffn_conv_b=ffn_conv_b, w_ffn_down=w_ffn_down,
             ple_gate_w=ple_gate_w, ple_gate_b=ple_gate_b, ple_proj=ple_proj, ln2_g=ln2_g, ln2_b=ln2_b)
    m = dict(w_in=m_w_in, attn_sinks=m_attn_sinks, rnn_conv_w=m_rnn_conv_w, rnn_conv_b=m_rnn_conv_b, gate_a_w=m_gate_a_w,
             gate_a_b=m_gate_a_b, gate_x_w=m_gate_x_w, gate_x_b=m_gate_x_b, lru_lambda=m_lru_lambda, w_out=m_w_out,
             ln1_g=m_ln1_g, ln1_b=m_ln1_b, w_ffn_up=m_w_ffn_up, ffn_conv_w=m_ffn_conv_w, ffn_conv_b=m_ffn_conv_b,
             w_ffn_down=m_w_ffn_down, ple_gate_w=m_ple_gate_w, ple_gate_b=m_ple_gate_b, ple_proj=m_ple_proj,
             ln2_g=m_ln2_g, ln2_b=m_ln2_b)
    v = dict(w_in=v_w_in, attn_sinks=v_attn_sinks, rnn_conv_w=v_rnn_conv_w, rnn_conv_b=v_rnn_conv_b, gate_a_w=v_gate_a_w,
             gate_a_b=v_gate_a_b, gate_x_w=v_gate_x_w, gate_x_b=v_gate_x_b, lru_lambda=v_lru_lambda, w_out=v_w_out,
             ln1_g=v_ln1_g, ln1_b=v_ln1_b, w_ffn_up=v_w_ffn_up, ffn_conv_w=v_ffn_conv_w, ffn_conv_b=v_ffn_conv_b,
             w_ffn_down=v_w_ffn_down, ple_gate_w=v_ple_gate_w, ple_gate_b=v_ple_gate_b, ple_proj=v_ple_proj,
             ln2_g=v_ln2_g, ln2_b=v_ln2_b)
    w, m, v = ({k: a[0] for k, a in d.items()} for d in (w, m, v))
    chip = 2 * lax.axis_index("x") + lax.axis_index("y")
    core = lax.axis_index("c")

    wpack = _pack_big(w)
    cpack, _ = _pack_vecs([w["rnn_conv_w"], w["ffn_conv_w"]])
    shard = wpack.astype(MXU_DTYPE)
    g_in, gcp = _gather_first(shard[PACK_OFF[0]:PACK_OFF[1]], cpack)
    gw = _split_pack(g_in, 0, 1)
    small = {k: w[k] for k in SMALL}
    small["rnn_conv_w"] = gcp[:, 0:4].reshape(4, 4, 128).transpose(1, 0, 2).reshape(4, 512)
    small["ffn_conv_w"] = gcp[:, 4:22].reshape(4, 3, 768).transpose(1, 0, 2).reshape(3, 3072)

    core1 = core.reshape(1).astype(jnp.int32)
    grad_x, big, sg, loss, ffn_halves, small_all = _layer_grads(x[0], p[0, 0], loss_target[0], gw, small, shard, core1)

    shapes = [sg[k].shape for k in SMALL] + [(1,)]
    _, offs = _pack_vecs([jnp.zeros(s, F32) for s in shapes])
    red = dict(zip(SMALL + ("loss",), _unpack_vecs(_sum_devices(small_all[0]), offs, shapes)))
    red["rnn_conv_w"] = lax.dynamic_slice_in_dim(red["rnn_conv_w"], chip * 128, 128, axis=1)
    red["ffn_conv_w"] = lax.dynamic_slice_in_dim(red["ffn_conv_w"], chip * 768, 768, axis=1)

    late_mine, late_other = ([a] for a in _reduce_in_vmem(big["w_in"]))

    def adamw(names, mine, other, name):
        out, _ = _adamw_halves([w[k] for k in names], mine, other, [m[k] for k in names], [v[k] for k in names],
                               core1, name)
        return dict(zip(names, out))

    big_out = {**adamw(LATE_WEIGHTS, late_mine, late_other, "adamw_late"), **adamw(EARLY_WEIGHTS, *ffn_halves, "adamw_early")}
    wsm, offs2 = _pack_vecs([w[k] for k in SMALL])
    gsm, _ = _pack_vecs([red[k] for k in SMALL])
    msm, _ = _pack_vecs([m[k] for k in SMALL])
    vsm, _ = _pack_vecs([v[k] for k in SMALL])
    dsm, nmsm, nvsm = _adamw(wsm, gsm, msm, vsm, "adamw_small")
    shapes2 = [w[k].shape for k in SMALL]

    def named(n, smallp):
        d = {k: out[n][None] for k, out in big_out.items()}
        d.update({k: a[None] for k, a in zip(SMALL, _unpack_vecs(smallp, offs2, shapes2))})
        return [d[k] for k in WEIGHTS]

    return (red["loss"].reshape(()), grad_x[None], *named(0, gsm), *named(1, dsm), *named(2, nmsm), *named(3, nvsm))
```
